```python
import math
import jax, jax.numpy as jnp
from jax import lax
import numpy as np

D_MODEL = 1024
BATCH = 8
SEQ = 4096
DEPTH = 2

N_MEM = 256
EPS = 1e-6

RET_WIDTH = D_MODEL // 2
RET_HEADS = 4
RET_HEAD_DIM = RET_WIDTH // RET_HEADS
RET_CHUNK = 128
ROPE_BASE = 10000.0
S5_WIDTH = D_MODEL - RET_WIDTH
S5_GROUP = 16
S5_GROUPS = S5_WIDTH // S5_GROUP
S5_STATE = 64
EVEN_IN = 4 * RET_WIDTH + S5_WIDTH

GDN_HEADS = 8
GDN_HEAD_DIM = D_MODEL // GDN_HEADS
GDN_WIDTH = GDN_HEADS * GDN_HEAD_DIM
GDN_CONV = 4
GDN_CHUNK = 64
ODD_IN = 4 * GDN_WIDTH + 2 * GDN_HEADS

XA_HEADS = 4
XA_HEAD_DIM = D_MODEL // XA_HEADS

FFN_DIM = ((8 * D_MODEL) // 3 + 255) // 256 * 256
FFN_CONV = 3

kernel_name = "hybrid_retention_s5_gdn_convffn"

F32 = jnp.float32


def rmsnorm(x, g):
    xf = x.astype(F32)
    y = xf * lax.rsqrt(jnp.mean(xf * xf, axis=-1, keepdims=True) + EPS)
    return (y * g.astype(F32)).astype(x.dtype)


def causal_dwconv(x, w):
    k_w, ch = w.shape
    return lax.conv_general_dilated(x, w[:, None, :].astype(x.dtype), window_strides=(1,),
                                    padding=[(k_w - 1, 0)],
                                    dimension_numbers=('NWC', 'WIO', 'NWC'),
                                    feature_group_count=ch)


def rotary(x, positions):
    half = x.shape[-1] // 2
    inv = jnp.exp(-math.log(ROPE_BASE) * jnp.arange(half, dtype=F32) / half)
    ang = positions.astype(F32)[:, None] * inv[None, :]
    cos = jnp.cos(ang)[None, :, None, :]
    sin = jnp.sin(ang)[None, :, None, :]
    x1, x2 = x[..., :half], x[..., half:]
    return jnp.concatenate([x1 * cos - x2 * sin, x1 * sin + x2 * cos], axis=-1)


def retention_chunkwise(q, k, v):
    b, h, s, dh = q.shape
    c = RET_CHUNK
    n = s // c
    log_gamma = jnp.log1p(-jnp.exp2(-5.0 - jnp.arange(h, dtype=F32)))
    idx = jnp.arange(c, dtype=F32)
    diff = idx[:, None] - idx[None, :]
    causal = diff >= 0
    intra = jnp.where(causal, jnp.exp(log_gamma[:, None, None] * jnp.where(causal, diff, 0.0)), 0.0)
    q = q.reshape(b, h, n, c, dh)
    k = k.reshape(b, h, n, c, dh)
    v = v.reshape(b, h, n, c, dh)
    scores = jnp.einsum('bhnid,bhnjd->bhnij', q, k) * intra[None, :, None]
    inner = jnp.einsum('bhnij,bhnjd->bhnid', scores, v)
    k_dec = k * jnp.exp(log_gamma[:, None] * (c - 1 - idx))[None, :, None, :, None]
    kv = jnp.einsum('bhnjd,bhnje->nbhde', k_dec, v)
    chunk_decay = jnp.exp(log_gamma * c)[None, :, None, None]

    def step(state, kv_n):
        return state * chunk_decay + kv_n, state

    _, prev = lax.scan(step, jnp.zeros((b, h, dh, dh), F32), kv)
    q_dec = q * jnp.exp(log_gamma[:, None] * (idx + 1))[None, :, None, :, None]
    cross = jnp.einsum('bhnid,nbhde->bhnie', q_dec, prev)
    return (inner + cross).reshape(b, h, s, dh)


def complex_affine_combine(e1, e2):
    a1r, a1i, b1r, b1i = e1
    a2r, a2i, b2r, b2i = e2
    return (a2r * a1r - a2i * a1i,
            a2r * a1i + a2i * a1r,
            a2r * b1r - a2i * b1i + b2r,
            a2r * b1i + a2i * b1r + b2i)


def s5_ssm(u, lam_re, lam_im, b_re, b_im, c_re, c_im, d, log_dt):
    bsz, s, _ = u.shape
    uf = u.astype(F32).reshape(bsz, s, S5_GROUPS, S5_GROUP)
    lr, li = lam_re.astype(F32), lam_im.astype(F32)
    dt = jnp.exp(log_dt.astype(F32))[:, None]
    mag = jnp.exp(lr * dt)
    a_re = mag * jnp.cos(li * dt)
    a_im = mag * jnp.sin(li * dt)
    den = lr * lr + li * li
    z_re = ((a_re - 1.0) * lr + a_im * li) / den
    z_im = (a_im * lr - (a_re - 1.0) * li) / den
    br, bi = b_re.astype(F32), b_im.astype(F32)
    bb_re = z_re[:, None, :] * br - z_im[:, None, :] * bi
    bb_im = z_re[:, None, :] * bi + z_im[:, None, :] * br
    bu_re = jnp.einsum('bsgh,ghp->bsgp', uf, bb_re)
    bu_im = jnp.einsum('bsgh,ghp->bsgp', uf, bb_im)
    elems = (jnp.broadcast_to(a_re, bu_re.shape), jnp.broadcast_to(a_im, bu_re.shape), bu_re, bu_im)
    _, _, st_re, st_im = lax.associative_scan(complex_affine_combine, elems, axis=1)
    y = (jnp.einsum('bsgp,gph->bsgh', st_re, c_re.astype(F32))
         - jnp.einsum('bsgp,gph->bsgh', st_im, c_im.astype(F32))
         + d.astype(F32) * uf)
    return y.reshape(bsz, s, S5_WIDTH)


def even_mixer(h, w_in, ret_norm, lam_re, lam_im, b_re, b_im, c_re, c_im, s5_d, s5_log_dt,
               w_glu, b_glu, w_out):
    bsz, s, _ = h.shape
    proj = h @ w_in
    q, k, v, gate, u = jnp.split(proj, [RET_WIDTH, 2 * RET_WIDTH, 3 * RET_WIDTH, 4 * RET_WIDTH], axis=-1)
    pos = jnp.arange(s)

    def heads(t):
        return t.astype(F32).reshape(bsz, s, RET_HEADS, RET_HEAD_DIM)

    qh = rotary(heads(q), pos)
    kh = rotary(heads(k), pos) * (RET_HEAD_DIM ** -0.5)
    o = retention_chunkwise(qh.transpose(0, 2, 1, 3), kh.transpose(0, 2, 1, 3),
                            heads(v).transpose(0, 2, 1, 3)).transpose(0, 2, 1, 3)
    o = o * lax.rsqrt(jnp.mean(o * o, axis=-1, keepdims=True) + EPS)
    o = o.reshape(bsz, s, RET_WIDTH) * ret_norm.astype(F32) * jax.nn.silu(gate.astype(F32))
    y = s5_ssm(u, lam_re, lam_im, b_re, b_im, c_re, c_im, s5_d, s5_log_dt)
    y = jax.nn.gelu(y)
    y = y * jax.nn.sigmoid(y @ w_glu.astype(F32) + b_glu.astype(F32))
    merged = jnp.concatenate([o, y], axis=-1).astype(h.dtype)
    return merged @ w_out


def gated_delta_chunkwise(q, k, v, g, beta):
    b, h, s, dk = q.shape
    dv = v.shape[-1]
    c = GDN_CHUNK
    n = s // c
    q = q.reshape(b, h, n, c, dk)
    k = k.reshape(b, h, n, c, dk)
    v = v.reshape(b, h, n, c, dv)
    gc = jnp.cumsum(g.reshape(b, h, n, c), axis=-1)
    beta = beta.reshape(b, h, n, c)
    kb = k * beta[..., None]
    vb = v * beta[..., None]
    incl = jnp.tril(jnp.ones((c, c), bool))
    strict = jnp.tril(jnp.ones((c, c), bool), -1)
    gdiff = gc[..., :, None] - gc[..., None, :]
    decay = jnp.where(incl, jnp.exp(jnp.where(incl, gdiff, 0.0)), 0.0)
    a_mat = jnp.where(strict, jnp.einsum('bhnid,bhnjd->bhnij', kb, k) * decay, 0.0)
    eye = jnp.eye(c, dtype=F32)
    t_mat = lax.linalg.triangular_solve(a_mat + eye, jnp.broadcast_to(eye, a_mat.shape),
                                        left_side=True, lower=True)
    w = jnp.einsum('bhnij,bhnjd->bhnid', t_mat, kb * jnp.exp(gc)[..., None])
    u = jnp.einsum('bhnij,bhnjd->bhnid', t_mat, vb)
    qk = jnp.where(incl, jnp.einsum('bhnid,bhnjd->bhnij', q, k) * decay, 0.0)
    q_dec = q * jnp.exp(gc)[..., None]
    k_dec = k * jnp.exp(gc[..., -1:] - gc)[..., None]
    g_last = jnp.exp(gc[..., -1])
    xs = tuple(jnp.moveaxis(t, 2, 0) for t in (q_dec, k_dec, u, w, qk, g_last))

    def step(state, inp):
        qd, kd, un, wn, qkn, gl = inp
        v_new = un - jnp.einsum('bhcd,bhde->bhce', wn, state)
        o = jnp.einsum('bhcd,bhde->bhce', qd, state) + jnp.einsum('bhij,bhje->bhie', qkn, v_new)
        state = state * gl[..., None, None] + jnp.einsum('bhcd,bhce->bhde', kd, v_new)
        return state, o

    _, o = lax.scan(step, jnp.zeros((b, h, dk, dv), F32), xs)
    return jnp.moveaxis(o, 0, 2).reshape(b, h, s, dv)


def odd_mixer(h, w_in, conv_w, a_log, dt_bias, o_norm, w_out):
    bsz, s, _ = h.shape
    proj = h @ w_in
    qkv, z, b_in, a_in = jnp.split(proj, [3 * GDN_WIDTH, 4 * GDN_WIDTH, 4 * GDN_WIDTH + GDN_HEADS], axis=-1)
    qkv = jax.nn.silu(causal_dwconv(qkv, conv_w)).astype(F32)
    q, k, v = jnp.split(qkv, 3, axis=-1)

    def heads(t):
        return t.reshape(bsz, s, GDN_HEADS, GDN_HEAD_DIM).transpose(0, 2, 1, 3)

    def l2n(t):
        return t * lax.rsqrt(jnp.sum(t * t, axis=-1, keepdims=True) + EPS)

    q = l2n(heads(q)) * (GDN_HEAD_DIM ** -0.5)
    k = l2n(heads(k))
    v = heads(v)
    beta = jax.nn.sigmoid(b_in.astype(F32)).transpose(0, 2, 1)
    g = -(jnp.exp(a_log.astype(F32)) * jax.nn.softplus(a_in.astype(F32) + dt_bias.astype(F32)))
    g = g.transpose(0, 2, 1)
    o = gated_delta_chunkwise(q, k, v, g, beta).transpose(0, 2, 1, 3)
    o = o * lax.rsqrt(jnp.mean(o * o, axis=-1, keepdims=True) + EPS) * o_norm.astype(F32)
    o = o * jax.nn.silu(z.astype(F32).reshape(bsz, s, GDN_HEADS, GDN_HEAD_DIM))
    return o.reshape(bsz, s, GDN_WIDTH).astype(h.dtype) @ w_out


def memory_cross_attention(h, mem_n, wq, wkv, wo):
    bsz, s, _ = h.shape
    m = mem_n.shape[1]
    q = (h @ wq).reshape(bsz, s, XA_HEADS, XA_HEAD_DIM)
    k, v = jnp.split(mem_n @ wkv, 2, axis=-1)
    k = k.reshape(bsz, m, XA_HEADS, XA_HEAD_DIM)
    v = v.reshape(bsz, m, XA_HEADS, XA_HEAD_DIM)
    scores = jnp.einsum('bshd,bmhd->bhsm', q, k).astype(F32) * (XA_HEAD_DIM ** -0.5)
    p = jax.nn.softmax(scores, axis=-1).astype(h.dtype)
    o = jnp.einsum('bhsm,bmhd->bshd', p, v).reshape(bsz, s, D_MODEL)
    return o @ wo


def conv_ffn(h, w_up, conv_w, w_down):
    hu = causal_dwconv(h @ w_up, conv_w)
    up, gate = jnp.split(hu, 2, axis=-1)
    return (jax.nn.silu(gate) * up) @ w_down


def _fwd_setup_inputs(seed: int = 0) -> dict:
    key = jax.random.key(seed)
    it = iter(jax.random.split(key, 64))

    def nrm(shape, scale):
        return jax.random.normal(next(it), shape, F32) * scale

    def dense(fan_in, fan_out):
        return nrm((fan_in, fan_out), fan_in ** -0.5)

    def gain(n):
        return 1.0 + nrm((n,), 0.02)

    def log_uniform(shape, lo, hi):
        return jax.random.uniform(next(it), shape, F32, math.log(lo), math.log(hi))

    def common(p):
        return {
            p + "xa_norm": gain(D_MODEL),
            p + "mem_norm": gain(D_MODEL),
            p + "xa_wq": dense(D_MODEL, D_MODEL),
            p + "xa_wkv": dense(D_MODEL, 2 * D_MODEL),
            p + "xa_wo": dense(D_MODEL, D_MODEL),
            p + "ffn_norm": gain(D_MODEL),
            p + "ffn_w_up": dense(D_MODEL, 2 * FFN_DIM),
            p + "ffn_conv": nrm((FFN_CONV, 2 * FFN_DIM), FFN_CONV ** -0.5),
            p + "ffn_w_down": dense(FFN_DIM, D_MODEL),
        }

    out = {
        "x": nrm((BATCH, SEQ, D_MODEL), 1.0),
        "mem": nrm((BATCH, N_MEM, D_MODEL), 1.0),
        "l0_mix_norm": gain(D_MODEL),
        "l0_w_in": dense(D_MODEL, EVEN_IN),
        "l0_ret_norm": gain(RET_WIDTH),
        "l0_s5_lambda_re": -0.5 + nrm((S5_GROUPS, S5_STATE), 0.01),
        "l0_s5_lambda_im": math.pi * jnp.broadcast_to(jnp.arange(S5_STATE, dtype=F32), (S5_GROUPS, S5_STATE))
                           + nrm((S5_GROUPS, S5_STATE), 0.01),
        "l0_s5_b_re": nrm((S5_GROUPS, S5_GROUP, S5_STATE), (2 * S5_GROUP) ** -0.5),
        "l0_s5_b_im": nrm((S5_GROUPS, S5_GROUP, S5_STATE), (2 * S5_GROUP) ** -0.5),
        "l0_s5_c_re": nrm((S5_GROUPS, S5_STATE, S5_GROUP), (2 * S5_STATE) ** -0.5),
        "l0_s5_c_im": nrm((S5_GROUPS, S5_STATE, S5_GROUP), (2 * S5_STATE) ** -0.5),
        "l0_s5_d": nrm((S5_GROUPS, S5_GROUP), 1.0),
        "l0_s5_log_dt": log_uniform((S5_GROUPS,), 1e-3, 1e-1),
        "l0_s5_w_glu": dense(S5_WIDTH, S5_WIDTH),
        "l0_s5_b_glu": nrm((S5_WIDTH,), 0.01),
        "l0_w_out": dense(D_MODEL, D_MODEL),
    }
    out.update(common("l0_"))
    dt = jnp.exp(log_uniform((GDN_HEADS,), 1e-3, 1e-1))
    out.update({
        "l1_mix_norm": gain(D_MODEL),
        "l1_w_in": dense(D_MODEL, ODD_IN),
        "l1_conv": nrm((GDN_CONV, 3 * GDN_WIDTH), GDN_CONV ** -0.5),
        "l1_a_log": jnp.log(jax.random.uniform(next(it), (GDN_HEADS,), F32, 1.0, 16.0)),
        "l1_dt_bias": dt + jnp.log(-jnp.expm1(-dt)),
        "l1_o_norm": gain(GDN_HEAD_DIM),
        "l1_w_out": dense(GDN_WIDTH, D_MODEL),
    })
    out.update(common("l1_"))
    out["final_norm"] = gain(D_MODEL)
    return out


def _fwd_reference(x, mem,
              l0_mix_norm, l0_w_in, l0_ret_norm, l0_s5_lambda_re, l0_s5_lambda_im, l0_s5_b_re, l0_s5_b_im,
              l0_s5_c_re, l0_s5_c_im, l0_s5_d, l0_s5_log_dt, l0_s5_w_glu, l0_s5_b_glu, l0_w_out,
              l0_xa_norm, l0_mem_norm, l0_xa_wq, l0_xa_wkv, l0_xa_wo,
              l0_ffn_norm, l0_ffn_w_up, l0_ffn_conv, l0_ffn_w_down,
              l1_mix_norm, l1_w_in, l1_conv, l1_a_log, l1_dt_bias, l1_o_norm, l1_w_out,
              l1_xa_norm, l1_mem_norm, l1_xa_wq, l1_xa_wkv, l1_xa_wo,
              l1_ffn_norm, l1_ffn_w_up, l1_ffn_conv, l1_ffn_w_down,
              final_norm):
    mixers = (
        lambda h: even_mixer(h, l0_w_in, l0_ret_norm, l0_s5_lambda_re, l0_s5_lambda_im, l0_s5_b_re,
                             l0_s5_b_im, l0_s5_c_re, l0_s5_c_im, l0_s5_d, l0_s5_log_dt,
                             l0_s5_w_glu, l0_s5_b_glu, l0_w_out),
        lambda h: odd_mixer(h, l1_w_in, l1_conv, l1_a_log, l1_dt_bias, l1_o_norm, l1_w_out),
    )
    commons = (
        (l0_mix_norm, l0_xa_norm, l0_mem_norm, l0_xa_wq, l0_xa_wkv, l0_xa_wo,
         l0_ffn_norm, l0_ffn_w_up, l0_ffn_conv, l0_ffn_w_down),
        (l1_mix_norm, l1_xa_norm, l1_mem_norm, l1_xa_wq, l1_xa_wkv, l1_xa_wo,
         l1_ffn_norm, l1_ffn_w_up, l1_ffn_conv, l1_ffn_w_down),
    )
    for i in range(DEPTH):
        (mix_norm, xa_norm, mem_norm, xa_wq, xa_wkv, xa_wo,
         ffn_norm, ffn_w_up, ffn_conv, ffn_w_down) = commons[i]
        x = x + mixers[i](rmsnorm(x, mix_norm))
        x = x + memory_cross_attention(rmsnorm(x, xa_norm), rmsnorm(mem, mem_norm), xa_wq, xa_wkv, xa_wo)
        x = x + conv_ffn(rmsnorm(x, ffn_norm), ffn_w_up, ffn_conv, ffn_w_down)
    return rmsnorm(x, final_norm)


import jax as _jax
import jax.numpy as _jnp

TWIN_FORMAT = 'train_step'
FWD_PARAMS = ['x', 'mem', 'l0_mix_norm', 'l0_w_in', 'l0_ret_norm', 'l0_s5_lambda_re', 'l0_s5_lambda_im', 'l0_s5_b_re', 'l0_s5_b_im', 'l0_s5_c_re', 'l0_s5_c_im', 'l0_s5_d', 'l0_s5_log_dt', 'l0_s5_w_glu', 'l0_s5_b_glu', 'l0_w_out', 'l0_xa_norm', 'l0_mem_norm', 'l0_xa_wq', 'l0_xa_wkv', 'l0_xa_wo', 'l0_ffn_norm', 'l0_ffn_w_up', 'l0_ffn_conv', 'l0_ffn_w_down', 'l1_mix_norm', 'l1_w_in', 'l1_conv', 'l1_a_log', 'l1_dt_bias', 'l1_o_norm', 'l1_w_out', 'l1_xa_norm', 'l1_mem_norm', 'l1_xa_wq', 'l1_xa_wkv', 'l1_xa_wo', 'l1_ffn_norm', 'l1_ffn_w_up', 'l1_ffn_conv', 'l1_ffn_w_down', 'final_norm']
TWIN_WEIGHTS = ['l0_mix_norm', 'l0_w_in', 'l0_ret_norm', 'l0_s5_lambda_re', 'l0_s5_lambda_im', 'l0_s5_b_re', 'l0_s5_b_im', 'l0_s5_c_re', 'l0_s5_c_im', 'l0_s5_d', 'l0_s5_log_dt', 'l0_s5_w_glu', 'l0_s5_b_glu', 'l0_w_out', 'l0_xa_norm', 'l0_mem_norm', 'l0_xa_wq', 'l0_xa_wkv', 'l0_xa_wo', 'l0_ffn_norm', 'l0_ffn_w_up', 'l0_ffn_conv', 'l0_ffn_w_down', 'l1_mix_norm', 'l1_w_in', 'l1_conv', 'l1_a_log', 'l1_dt_bias', 'l1_o_norm', 'l1_w_out', 'l1_xa_norm', 'l1_mem_norm', 'l1_xa_wq', 'l1_xa_wkv', 'l1_xa_wo', 'l1_ffn_norm', 'l1_ffn_w_up', 'l1_ffn_conv', 'l1_ffn_w_down', 'final_norm']
TWIN_DIFF_INPUT = 'x'
TWIN_INPUTS = ['x', 'mem', 'l0_mix_norm', 'l0_w_in', 'l0_ret_norm', 'l0_s5_lambda_re', 'l0_s5_lambda_im', 'l0_s5_b_re', 'l0_s5_b_im', 'l0_s5_c_re', 'l0_s5_c_im', 'l0_s5_d', 'l0_s5_log_dt', 'l0_s5_w_glu', 'l0_s5_b_glu', 'l0_w_out', 'l0_xa_norm', 'l0_mem_norm', 'l0_xa_wq', 'l0_xa_wkv', 'l0_xa_wo', 'l0_ffn_norm', 'l0_ffn_w_up', 'l0_ffn_conv', 'l0_ffn_w_down', 'l1_mix_norm', 'l1_w_in', 'l1_conv', 'l1_a_log', 'l1_dt_bias', 'l1_o_norm', 'l1_w_out', 'l1_xa_norm', 'l1_mem_norm', 'l1_xa_wq', 'l1_xa_wkv', 'l1_xa_wo', 'l1_ffn_norm', 'l1_ffn_w_up', 'l1_ffn_conv', 'l1_ffn_w_down', 'final_norm', 'loss_target', 'm_l0_mix_norm', 'm_l0_w_in', 'm_l0_ret_norm', 'm_l0_s5_lambda_re', 'm_l0_s5_lambda_im', 'm_l0_s5_b_re', 'm_l0_s5_b_im', 'm_l0_s5_c_re', 'm_l0_s5_c_im', 'm_l0_s5_d', 'm_l0_s5_log_dt', 'm_l0_s5_w_glu', 'm_l0_s5_b_glu', 'm_l0_w_out', 'm_l0_xa_norm', 'm_l0_mem_norm', 'm_l0_xa_wq', 'm_l0_xa_wkv', 'm_l0_xa_wo', 'm_l0_ffn_norm', 'm_l0_ffn_w_up', 'm_l0_ffn_conv', 'm_l0_ffn_w_down', 'm_l1_mix_norm', 'm_l1_w_in', 'm_l1_conv', 'm_l1_a_log', 'm_l1_dt_bias', 'm_l1_o_norm', 'm_l1_w_out', 'm_l1_xa_norm', 'm_l1_mem_norm', 'm_l1_xa_wq', 'm_l1_xa_wkv', 'm_l1_xa_wo', 'm_l1_ffn_norm', 'm_l1_ffn_w_up', 'm_l1_ffn_conv', 'm_l1_ffn_w_down', 'm_final_norm', 'v_l0_mix_norm', 'v_l0_w_in', 'v_l0_ret_norm', 'v_l0_s5_lambda_re', 'v_l0_s5_lambda_im', 'v_l0_s5_b_re', 'v_l0_s5_b_im', 'v_l0_s5_c_re', 'v_l0_s5_c_im', 'v_l0_s5_d', 'v_l0_s5_log_dt', 'v_l0_s5_w_glu', 'v_l0_s5_b_glu', 'v_l0_w_out', 'v_l0_xa_norm', 'v_l0_mem_norm', 'v_l0_xa_wq', 'v_l0_xa_wkv', 'v_l0_xa_wo', 'v_l0_ffn_norm', 'v_l0_ffn_w_up', 'v_l0_ffn_conv', 'v_l0_ffn_w_down', 'v_l1_mix_norm', 'v_l1_w_in', 'v_l1_conv', 'v_l1_a_log', 'v_l1_dt_bias', 'v_l1_o_norm', 'v_l1_w_out', 'v_l1_xa_norm', 'v_l1_mem_norm', 'v_l1_xa_wq', 'v_l1_xa_wkv', 'v_l1_xa_wo', 'v_l1_ffn_norm', 'v_l1_ffn_w_up', 'v_l1_ffn_conv', 'v_l1_ffn_w_down', 'v_final_norm']
TWIN_OUTPUTS = ['loss', 'grad_x', 'grad_l0_mix_norm', 'grad_l0_w_in', 'grad_l0_ret_norm', 'grad_l0_s5_lambda_re', 'grad_l0_s5_lambda_im', 'grad_l0_s5_b_re', 'grad_l0_s5_b_im', 'grad_l0_s5_c_re', 'grad_l0_s5_c_im', 'grad_l0_s5_d', 'grad_l0_s5_log_dt', 'grad_l0_s5_w_glu', 'grad_l0_s5_b_glu', 'grad_l0_w_out', 'grad_l0_xa_norm', 'grad_l0_mem_norm', 'grad_l0_xa_wq', 'grad_l0_xa_wkv', 'grad_l0_xa_wo', 'grad_l0_ffn_norm', 'grad_l0_ffn_w_up', 'grad_l0_ffn_conv', 'grad_l0_ffn_w_down', 'grad_l1_mix_norm', 'grad_l1_w_in', 'grad_l1_conv', 'grad_l1_a_log', 'grad_l1_dt_bias', 'grad_l1_o_norm', 'grad_l1_w_out', 'grad_l1_xa_norm', 'grad_l1_mem_norm', 'grad_l1_xa_wq', 'grad_l1_xa_wkv', 'grad_l1_xa_wo', 'grad_l1_ffn_norm', 'grad_l1_ffn_w_up', 'grad_l1_ffn_conv', 'grad_l1_ffn_w_down', 'grad_final_norm', 'delta_l0_mix_norm', 'delta_l0_w_in', 'delta_l0_ret_norm', 'delta_l0_s5_lambda_re', 'delta_l0_s5_lambda_im', 'delta_l0_s5_b_re', 'delta_l0_s5_b_im', 'delta_l0_s5_c_re', 'delta_l0_s5_c_im', 'delta_l0_s5_d', 'delta_l0_s5_log_dt', 'delta_l0_s5_w_glu', 'delta_l0_s5_b_glu', 'delta_l0_w_out', 'delta_l0_xa_norm', 'delta_l0_mem_norm', 'delta_l0_xa_wq', 'delta_l0_xa_wkv', 'delta_l0_xa_wo', 'delta_l0_ffn_norm', 'delta_l0_ffn_w_up', 'delta_l0_ffn_conv', 'delta_l0_ffn_w_down', 'delta_l1_mix_norm', 'delta_l1_w_in', 'delta_l1_conv', 'delta_l1_a_log', 'delta_l1_dt_bias', 'delta_l1_o_norm', 'delta_l1_w_out', 'delta_l1_xa_norm', 'delta_l1_mem_norm', 'delta_l1_xa_wq', 'delta_l1_xa_wkv', 'delta_l1_xa_wo', 'delta_l1_ffn_norm', 'delta_l1_ffn_w_up', 'delta_l1_ffn_conv', 'delta_l1_ffn_w_down', 'delta_final_norm', 'new_m_l0_mix_norm', 'new_m_l0_w_in', 'new_m_l0_ret_norm', 'new_m_l0_s5_lambda_re', 'new_m_l0_s5_lambda_im', 'new_m_l0_s5_b_re', 'new_m_l0_s5_b_im', 'new_m_l0_s5_c_re', 'new_m_l0_s5_c_im', 'new_m_l0_s5_d', 'new_m_l0_s5_log_dt', 'new_m_l0_s5_w_glu', 'new_m_l0_s5_b_glu', 'new_m_l0_w_out', 'new_m_l0_xa_norm', 'new_m_l0_mem_norm', 'new_m_l0_xa_wq', 'new_m_l0_xa_wkv', 'new_m_l0_xa_wo', 'new_m_l0_ffn_norm', 'new_m_l0_ffn_w_up', 'new_m_l0_ffn_conv', 'new_m_l0_ffn_w_down', 'new_m_l1_mix_norm', 'new_m_l1_w_in', 'new_m_l1_conv', 'new_m_l1_a_log', 'new_m_l1_dt_bias', 'new_m_l1_o_norm', 'new_m_l1_w_out', 'new_m_l1_xa_norm', 'new_m_l1_mem_norm', 'new_m_l1_xa_wq', 'new_m_l1_xa_wkv', 'new_m_l1_xa_wo', 'new_m_l1_ffn_norm', 'new_m_l1_ffn_w_up', 'new_m_l1_ffn_conv', 'new_m_l1_ffn_w_down', 'new_m_final_norm', 'new_v_l0_mix_norm', 'new_v_l0_w_in', 'new_v_l0_ret_norm', 'new_v_l0_s5_lambda_re', 'new_v_l0_s5_lambda_im', 'new_v_l0_s5_b_re', 'new_v_l0_s5_b_im', 'new_v_l0_s5_c_re', 'new_v_l0_s5_c_im', 'new_v_l0_s5_d', 'new_v_l0_s5_log_dt', 'new_v_l0_s5_w_glu', 'new_v_l0_s5_b_glu', 'new_v_l0_w_out', 'new_v_l0_xa_norm', 'new_v_l0_mem_norm', 'new_v_l0_xa_wq', 'new_v_l0_xa_wkv', 'new_v_l0_xa_wo', 'new_v_l0_ffn_norm', 'new_v_l0_ffn_w_up', 'new_v_l0_ffn_conv', 'new_v_l0_ffn_w_down', 'new_v_l1_mix_norm', 'new_v_l1_w_in', 'new_v_l1_conv', 'new_v_l1_a_log', 'new_v_l1_dt_bias', 'new_v_l1_o_norm', 'new_v_l1_w_out', 'new_v_l1_xa_norm', 'new_v_l1_mem_norm', 'new_v_l1_xa_wq', 'new_v_l1_xa_wkv', 'new_v_l1_xa_wo', 'new_v_l1_ffn_norm', 'new_v_l1_ffn_w_up', 'new_v_l1_ffn_conv', 'new_v_l1_ffn_w_down', 'new_v_final_norm']
TWIN_LEAF_KINDS = {'loss': 'loss', 'grad_x': 'grad_x', 'grad_l0_mix_norm': 'grad_w', 'grad_l0_w_in': 'grad_w', 'grad_l0_ret_norm': 'grad_w', 'grad_l0_s5_lambda_re': 'grad_w', 'grad_l0_s5_lambda_im': 'grad_w', 'grad_l0_s5_b_re': 'grad_w', 'grad_l0_s5_b_im': 'grad_w', 'grad_l0_s5_c_re': 'grad_w', 'grad_l0_s5_c_im': 'grad_w', 'grad_l0_s5_d': 'grad_w', 'grad_l0_s5_log_dt': 'grad_w', 'grad_l0_s5_w_glu': 'grad_w', 'grad_l0_s5_b_glu': 'grad_w', 'grad_l0_w_out': 'grad_w', 'grad_l0_xa_norm': 'grad_w', 'grad_l0_mem_norm': 'grad_w', 'grad_l0_xa_wq': 'grad_w', 'grad_l0_xa_wkv': 'grad_w', 'grad_l0_xa_wo': 'grad_w', 'grad_l0_ffn_norm': 'grad_w', 'grad_l0_ffn_w_up': 'grad_w', 'grad_l0_ffn_conv': 'grad_w', 'grad_l0_ffn_w_down': 'grad_w', 'grad_l1_mix_norm': 'grad_w', 'grad_l1_w_in': 'grad_w', 'grad_l1_conv': 'grad_w', 'grad_l1_a_log': 'grad_w', 'grad_l1_dt_bias': 'grad_w', 'grad_l1_o_norm': 'grad_w', 'grad_l1_w_out': 'grad_w', 'grad_l1_xa_norm': 'grad_w', 'grad_l1_mem_norm': 'grad_w', 'grad_l1_xa_wq': 'grad_w', 'grad_l1_xa_wkv': 'grad_w', 'grad_l1_xa_wo': 'grad_w', 'grad_l1_ffn_norm': 'grad_w', 'grad_l1_ffn_w_up': 'grad_w', 'grad_l1_ffn_conv': 'grad_w', 'grad_l1_ffn_w_down': 'grad_w', 'grad_final_norm': 'grad_w', 'delta_l0_mix_norm': 'delta_w', 'delta_l0_w_in': 'delta_w', 'delta_l0_ret_norm': 'delta_w', 'delta_l0_s5_lambda_re': 'delta_w', 'delta_l0_s5_lambda_im': 'delta_w', 'delta_l0_s5_b_re': 'delta_w', 'delta_l0_s5_b_im': 'delta_w', 'delta_l0_s5_c_re': 'delta_w', 'delta_l0_s5_c_im': 'delta_w', 'delta_l0_s5_d': 'delta_w', 'delta_l0_s5_log_dt': 'delta_w', 'delta_l0_s5_w_glu': 'delta_w', 'delta_l0_s5_b_glu': 'delta_w', 'delta_l0_w_out': 'delta_w', 'delta_l0_xa_norm': 'delta_w', 'delta_l0_mem_norm': 'delta_w', 'delta_l0_xa_wq': 'delta_w', 'delta_l0_xa_wkv': 'delta_w', 'delta_l0_xa_wo': 'delta_w', 'delta_l0_ffn_norm': 'delta_w', 'delta_l0_ffn_w_up': 'delta_w', 'delta_l0_ffn_conv': 'delta_w', 'delta_l0_ffn_w_down': 'delta_w', 'delta_l1_mix_norm': 'delta_w', 'delta_l1_w_in': 'delta_w', 'delta_l1_conv': 'delta_w', 'delta_l1_a_log': 'delta_w', 'delta_l1_dt_bias': 'delta_w', 'delta_l1_o_norm': 'delta_w', 'delta_l1_w_out': 'delta_w', 'delta_l1_xa_norm': 'delta_w', 'delta_l1_mem_norm': 'delta_w', 'delta_l1_xa_wq': 'delta_w', 'delta_l1_xa_wkv': 'delta_w', 'delta_l1_xa_wo': 'delta_w', 'delta_l1_ffn_norm': 'delta_w', 'delta_l1_ffn_w_up': 'delta_w', 'delta_l1_ffn_conv': 'delta_w', 'delta_l1_ffn_w_down': 'delta_w', 'delta_final_norm': 'delta_w', 'new_m_l0_mix_norm': 'new_m', 'new_m_l0_w_in': 'new_m', 'new_m_l0_ret_norm': 'new_m', 'new_m_l0_s5_lambda_re': 'new_m', 'new_m_l0_s5_lambda_im': 'new_m', 'new_m_l0_s5_b_re': 'new_m', 'new_m_l0_s5_b_im': 'new_m', 'new_m_l0_s5_c_re': 'new_m', 'new_m_l0_s5_c_im': 'new_m', 'new_m_l0_s5_d': 'new_m', 'new_m_l0_s5_log_dt': 'new_m', 'new_m_l0_s5_w_glu': 'new_m', 'new_m_l0_s5_b_glu': 'new_m', 'new_m_l0_w_out': 'new_m', 'new_m_l0_xa_norm': 'new_m', 'new_m_l0_mem_norm': 'new_m', 'new_m_l0_xa_wq': 'new_m', 'new_m_l0_xa_wkv': 'new_m', 'new_m_l0_xa_wo': 'new_m', 'new_m_l0_ffn_norm': 'new_m', 'new_m_l0_ffn_w_up': 'new_m', 'new_m_l0_ffn_conv': 'new_m', 'new_m_l0_ffn_w_down': 'new_m', 'new_m_l1_mix_norm': 'new_m', 'new_m_l1_w_in': 'new_m', 'new_m_l1_conv': 'new_m', 'new_m_l1_a_log': 'new_m', 'new_m_l1_dt_bias': 'new_m', 'new_m_l1_o_norm': 'new_m', 'new_m_l1_w_out': 'new_m', 'new_m_l1_xa_norm': 'new_m', 'new_m_l1_mem_norm': 'new_m', 'new_m_l1_xa_wq': 'new_m', 'new_m_l1_xa_wkv': 'new_m', 'new_m_l1_xa_wo': 'new_m', 'new_m_l1_ffn_norm': 'new_m', 'new_m_l1_ffn_w_up': 'new_m', 'new_m_l1_ffn_conv': 'new_m', 'new_m_l1_ffn_w_down': 'new_m', 'new_m_final_norm': 'new_m', 'new_v_l0_mix_norm': 'new_v', 'new_v_l0_w_in': 'new_v', 'new_v_l0_ret_norm': 'new_v', 'new_v_l0_s5_lambda_re': 'new_v', 'new_v_l0_s5_lambda_im': 'new_v', 'new_v_l0_s5_b_re': 'new_v', 'new_v_l0_s5_b_im': 'new_v', 'new_v_l0_s5_c_re': 'new_v', 'new_v_l0_s5_c_im': 'new_v', 'new_v_l0_s5_d': 'new_v', 'new_v_l0_s5_log_dt': 'new_v', 'new_v_l0_s5_w_glu': 'new_v', 'new_v_l0_s5_b_glu': 'new_v', 'new_v_l0_w_out': 'new_v', 'new_v_l0_xa_norm': 'new_v', 'new_v_l0_mem_norm': 'new_v', 'new_v_l0_xa_wq': 'new_v', 'new_v_l0_xa_wkv': 'new_v', 'new_v_l0_xa_wo': 'new_v', 'new_v_l0_ffn_norm': 'new_v', 'new_v_l0_ffn_w_up': 'new_v', 'new_v_l0_ffn_conv': 'new_v', 'new_v_l0_ffn_w_down': 'new_v', 'new_v_l1_mix_norm': 'new_v', 'new_v_l1_w_in': 'new_v', 'new_v_l1_conv': 'new_v', 'new_v_l1_a_log': 'new_v', 'new_v_l1_dt_bias': 'new_v', 'new_v_l1_o_norm': 'new_v', 'new_v_l1_w_out': 'new_v', 'new_v_l1_xa_norm': 'new_v', 'new_v_l1_mem_norm': 'new_v', 'new_v_l1_xa_wq': 'new_v', 'new_v_l1_xa_wkv': 'new_v', 'new_v_l1_xa_wo': 'new_v', 'new_v_l1_ffn_norm': 'new_v', 'new_v_l1_ffn_w_up': 'new_v', 'new_v_l1_ffn_conv': 'new_v', 'new_v_l1_ffn_w_down': 'new_v', 'new_v_final_norm': 'new_v'}


def _forward(args):
    return _fwd_reference(*[args[k] for k in FWD_PARAMS])


def _output_shape():
    out = _jax.eval_shape(lambda: _forward(_fwd_setup_inputs(0)))
    return out.shape, out.dtype

N_MICROBATCH = 1
ADAM_LR = 0.001
ADAM_B1 = 0.9
ADAM_B2 = 0.999
ADAM_EPS = 1e-08
ADAM_WD = 0.01
ADAM_STEP = 10
PER_EXAMPLE_BATCH_AXIS = {'x': 0, 'mem': 0, 'loss_target': 0}
SHARED_INPUTS = []
_WEIGHT_DTYPES = {'l0_mix_norm': _jnp.float32, 'l0_w_in': _jnp.float32, 'l0_ret_norm': _jnp.float32, 'l0_s5_lambda_re': _jnp.float32, 'l0_s5_lambda_im': _jnp.float32, 'l0_s5_b_re': _jnp.float32, 'l0_s5_b_im': _jnp.float32, 'l0_s5_c_re': _jnp.float32, 'l0_s5_c_im': _jnp.float32, 'l0_s5_d': _jnp.float32, 'l0_s5_log_dt': _jnp.float32, 'l0_s5_w_glu': _jnp.float32, 'l0_s5_b_glu': _jnp.float32, 'l0_w_out': _jnp.float32, 'l0_xa_norm': _jnp.float32, 'l0_mem_norm': _jnp.float32, 'l0_xa_wq': _jnp.float32, 'l0_xa_wkv': _jnp.float32, 'l0_xa_wo': _jnp.float32, 'l0_ffn_norm': _jnp.float32, 'l0_ffn_w_up': _jnp.float32, 'l0_ffn_conv': _jnp.float32, 'l0_ffn_w_down': _jnp.float32, 'l1_mix_norm': _jnp.float32, 'l1_w_in': _jnp.float32, 'l1_conv': _jnp.float32, 'l1_a_log': _jnp.float32, 'l1_dt_bias': _jnp.float32, 'l1_o_norm': _jnp.float32, 'l1_w_out': _jnp.float32, 'l1_xa_norm': _jnp.float32, 'l1_mem_norm': _jnp.float32, 'l1_xa_wq': _jnp.float32, 'l1_xa_wkv': _jnp.float32, 'l1_xa_wo': _jnp.float32, 'l1_ffn_norm': _jnp.float32, 'l1_ffn_w_up': _jnp.float32, 'l1_ffn_conv': _jnp.float32, 'l1_ffn_w_down': _jnp.float32, 'final_norm': _jnp.float32}
MOMENT_SCALE = {'l0_mix_norm': 2.073210e-01, 'l0_w_in': 1.268005e-01, 'l0_ret_norm': 1.324558e-01, 'l0_s5_lambda_re': 4.614154e-03, 'l0_s5_lambda_im': 4.207371e-03, 'l0_s5_b_re': 2.783260e-03, 'l0_s5_b_im': 2.940600e-03, 'l0_s5_c_re': 5.717788e-03, 'l0_s5_c_im': 5.669187e-03, 'l0_s5_d': 1.064832e-01, 'l0_s5_log_dt': 3.491859e+00, 'l0_s5_w_glu': 2.523259e-02, 'l0_s5_b_glu': 4.238004e-02, 'l0_w_out': 1.104300e-01, 'l0_xa_norm': 2.331302e-02, 'l0_mem_norm': 3.525259e-02, 'l0_xa_wq': 2.351315e-02, 'l0_xa_wkv': 2.393267e-02, 'l0_xa_wo': 2.449457e-02, 'l0_ffn_norm': 1.605032e-01, 'l0_ffn_w_up': 6.674500e-02, 'l0_ffn_conv': 6.703723e-02, 'l0_ffn_w_down': 1.087624e-01, 'l1_mix_norm': 1.359600e-01, 'l1_w_in': 6.881611e-02, 'l1_conv': 6.372752e-02, 'l1_a_log': 6.692861e-01, 'l1_dt_bias': 6.610981e-01, 'l1_o_norm': 2.371155e-01, 'l1_w_out': 7.967398e-02, 'l1_xa_norm': 1.467067e-02, 'l1_mem_norm': 2.094123e-02, 'l1_xa_wq': 1.421090e-02, 'l1_xa_wkv': 1.438137e-02, 'l1_xa_wo': 1.457238e-02, 'l1_ffn_norm': 1.082958e-01, 'l1_ffn_w_up': 4.427697e-02, 'l1_ffn_conv': 4.332316e-02, 'l1_ffn_w_down': 7.275469e-02, 'final_norm': 3.199237e+01}


def _to_microbatches(a, axis):
    t = _jnp.moveaxis(a, axis, 0)
    t = t.reshape((N_MICROBATCH, t.shape[0] // N_MICROBATCH) + t.shape[1:])
    return _jnp.moveaxis(t, 1, axis + 1)


def setup_inputs(seed: int = 0) -> dict:
    inp = _fwd_setup_inputs(seed)
    key = _jax.random.fold_in(_jax.random.key(seed), 7919)
    shape, _ = _output_shape()
    out = dict(inp)
    out["loss_target"] = _jax.random.normal(_jax.random.fold_in(key, 0), shape, _jnp.float32)
    for i, name in enumerate(TWIN_WEIGHTS):
        w = inp[name].astype(_jnp.float32)
        if MOMENT_SCALE is None:
            s = _jnp.sqrt(_jnp.mean(_jnp.square(w)) + 1e-30)
        else:
            s = MOMENT_SCALE[name]
        km, kv = _jax.random.split(_jax.random.fold_in(key, i + 1))
        out[name] = w
        out["m_" + name] = s * _jax.random.normal(km, w.shape, _jnp.float32)
        out["v_" + name] = (s * s) * _jax.random.uniform(kv, w.shape, _jnp.float32, 0.5, 1.5)
    if N_MICROBATCH > 1:
        for name, axis in PER_EXAMPLE_BATCH_AXIS.items():
            out[name] = _to_microbatches(out[name], axis)
    return {'x': out['x'], 'mem': out['mem'], 'l0_mix_norm': out['l0_mix_norm'], 'l0_w_in': out['l0_w_in'], 'l0_ret_norm': out['l0_ret_norm'], 'l0_s5_lambda_re': out['l0_s5_lambda_re'], 'l0_s5_lambda_im': out['l0_s5_lambda_im'], 'l0_s5_b_re': out['l0_s5_b_re'], 'l0_s5_b_im': out['l0_s5_b_im'], 'l0_s5_c_re': out['l0_s5_c_re'], 'l0_s5_c_im': out['l0_s5_c_im'], 'l0_s5_d': out['l0_s5_d'], 'l0_s5_log_dt': out['l0_s5_log_dt'], 'l0_s5_w_glu': out['l0_s5_w_glu'], 'l0_s5_b_glu': out['l0_s5_b_glu'], 'l0_w_out': out['l0_w_out'], 'l0_xa_norm': out['l0_xa_norm'], 'l0_mem_norm': out['l0_mem_norm'], 'l0_xa_wq': out['l0_xa_wq'], 'l0_xa_wkv': out['l0_xa_wkv'], 'l0_xa_wo': out['l0_xa_wo'], 'l0_ffn_norm': out['l0_ffn_norm'], 'l0_ffn_w_up': out['l0_ffn_w_up'], 'l0_ffn_conv': out['l0_ffn_conv'], 'l0_ffn_w_down': out['l0_ffn_w_down'], 'l1_mix_norm': out['l1_mix_norm'], 'l1_w_in': out['l1_w_in'], 'l1_conv': out['l1_conv'], 'l1_a_log': out['l1_a_log'], 'l1_dt_bias': out['l1_dt_bias'], 'l1_o_norm': out['l1_o_norm'], 'l1_w_out': out['l1_w_out'], 'l1_xa_norm': out['l1_xa_norm'], 'l1_mem_norm': out['l1_mem_norm'], 'l1_xa_wq': out['l1_xa_wq'], 'l1_xa_wkv': out['l1_xa_wkv'], 'l1_xa_wo': out['l1_xa_wo'], 'l1_ffn_norm': out['l1_ffn_norm'], 'l1_ffn_w_up': out['l1_ffn_w_up'], 'l1_ffn_conv': out['l1_ffn_conv'], 'l1_ffn_w_down': out['l1_ffn_w_down'], 'final_norm': out['final_norm'], 'loss_target': out['loss_target'], 'm_l0_mix_norm': out['m_l0_mix_norm'], 'm_l0_w_in': out['m_l0_w_in'], 'm_l0_ret_norm': out['m_l0_ret_norm'], 'm_l0_s5_lambda_re': out['m_l0_s5_lambda_re'], 'm_l0_s5_lambda_im': out['m_l0_s5_lambda_im'], 'm_l0_s5_b_re': out['m_l0_s5_b_re'], 'm_l0_s5_b_im': out['m_l0_s5_b_im'], 'm_l0_s5_c_re': out['m_l0_s5_c_re'], 'm_l0_s5_c_im': out['m_l0_s5_c_im'], 'm_l0_s5_d': out['m_l0_s5_d'], 'm_l0_s5_log_dt': out['m_l0_s5_log_dt'], 'm_l0_s5_w_glu': out['m_l0_s5_w_glu'], 'm_l0_s5_b_glu': out['m_l0_s5_b_glu'], 'm_l0_w_out': out['m_l0_w_out'], 'm_l0_xa_norm': out['m_l0_xa_norm'], 'm_l0_mem_norm': out['m_l0_mem_norm'], 'm_l0_xa_wq': out['m_l0_xa_wq'], 'm_l0_xa_wkv': out['m_l0_xa_wkv'], 'm_l0_xa_wo': out['m_l0_xa_wo'], 'm_l0_ffn_norm': out['m_l0_ffn_norm'], 'm_l0_ffn_w_up': out['m_l0_ffn_w_up'], 'm_l0_ffn_conv': out['m_l0_ffn_conv'], 'm_l0_ffn_w_down': out['m_l0_ffn_w_down'], 'm_l1_mix_norm': out['m_l1_mix_norm'], 'm_l1_w_in': out['m_l1_w_in'], 'm_l1_conv': out['m_l1_conv'], 'm_l1_a_log': out['m_l1_a_log'], 'm_l1_dt_bias': out['m_l1_dt_bias'], 'm_l1_o_norm': out['m_l1_o_norm'], 'm_l1_w_out': out['m_l1_w_out'], 'm_l1_xa_norm': out['m_l1_xa_norm'], 'm_l1_mem_norm': out['m_l1_mem_norm'], 'm_l1_xa_wq': out['m_l1_xa_wq'], 'm_l1_xa_wkv': out['m_l1_xa_wkv'], 'm_l1_xa_wo': out['m_l1_xa_wo'], 'm_l1_ffn_norm': out['m_l1_ffn_norm'], 'm_l1_ffn_w_up': out['m_l1_ffn_w_up'], 'm_l1_ffn_conv': out['m_l1_ffn_conv'], 'm_l1_ffn_w_down': out['m_l1_ffn_w_down'], 'm_final_norm': out['m_final_norm'], 'v_l0_mix_norm': out['v_l0_mix_norm'], 'v_l0_w_in': out['v_l0_w_in'], 'v_l0_ret_norm': out['v_l0_ret_norm'], 'v_l0_s5_lambda_re': out['v_l0_s5_lambda_re'], 'v_l0_s5_lambda_im': out['v_l0_s5_lambda_im'], 'v_l0_s5_b_re': out['v_l0_s5_b_re'], 'v_l0_s5_b_im': out['v_l0_s5_b_im'], 'v_l0_s5_c_re': out['v_l0_s5_c_re'], 'v_l0_s5_c_im': out['v_l0_s5_c_im'], 'v_l0_s5_d': out['v_l0_s5_d'], 'v_l0_s5_log_dt': out['v_l0_s5_log_dt'], 'v_l0_s5_w_glu': out['v_l0_s5_w_glu'], 'v_l0_s5_b_glu': out['v_l0_s5_b_glu'], 'v_l0_w_out': out['v_l0_w_out'], 'v_l0_xa_norm': out['v_l0_xa_norm'], 'v_l0_mem_norm': out['v_l0_mem_norm'], 'v_l0_xa_wq': out['v_l0_xa_wq'], 'v_l0_xa_wkv': out['v_l0_xa_wkv'], 'v_l0_xa_wo': out['v_l0_xa_wo'], 'v_l0_ffn_norm': out['v_l0_ffn_norm'], 'v_l0_ffn_w_up': out['v_l0_ffn_w_up'], 'v_l0_ffn_conv': out['v_l0_ffn_conv'], 'v_l0_ffn_w_down': out['v_l0_ffn_w_down'], 'v_l1_mix_norm': out['v_l1_mix_norm'], 'v_l1_w_in': out['v_l1_w_in'], 'v_l1_conv': out['v_l1_conv'], 'v_l1_a_log': out['v_l1_a_log'], 'v_l1_dt_bias': out['v_l1_dt_bias'], 'v_l1_o_norm': out['v_l1_o_norm'], 'v_l1_w_out': out['v_l1_w_out'], 'v_l1_xa_norm': out['v_l1_xa_norm'], 'v_l1_mem_norm': out['v_l1_mem_norm'], 'v_l1_xa_wq': out['v_l1_xa_wq'], 'v_l1_xa_wkv': out['v_l1_xa_wkv'], 'v_l1_xa_wo': out['v_l1_xa_wo'], 'v_l1_ffn_norm': out['v_l1_ffn_norm'], 'v_l1_ffn_w_up': out['v_l1_ffn_w_up'], 'v_l1_ffn_conv': out['v_l1_ffn_conv'], 'v_l1_ffn_w_down': out['v_l1_ffn_w_down'], 'v_final_norm': out['v_final_norm']}


def _loss(weights, diff, rest, loss_target):
    with _jax.named_scope("forward"):
        args = {**rest, TWIN_DIFF_INPUT: diff, **{k: w.astype(_WEIGHT_DTYPES[k]) for k, w in weights.items()}}
        y = _forward(args)
    with _jax.named_scope("loss_head"):
        err = _jnp.square(y.astype(_jnp.float32) - loss_target)
        return 0.5 * _jnp.sum(_jnp.mean(err, axis=-1)) if err.ndim else 0.5 * err


def _adamw(w, g, m, v):
    m = ADAM_B1 * m + (1.0 - ADAM_B1) * g
    v = ADAM_B2 * v + (1.0 - ADAM_B2) * _jnp.square(g)
    m_hat = m / (1.0 - ADAM_B1 ** ADAM_STEP)
    v_hat = v / (1.0 - ADAM_B2 ** ADAM_STEP)
    delta = -ADAM_LR * (m_hat / (_jnp.sqrt(v_hat) + ADAM_EPS) + ADAM_WD * w)
    return delta, m, v


def reference(x, mem, l0_mix_norm, l0_w_in, l0_ret_norm, l0_s5_lambda_re, l0_s5_lambda_im, l0_s5_b_re, l0_s5_b_im, l0_s5_c_re, l0_s5_c_im, l0_s5_d, l0_s5_log_dt, l0_s5_w_glu, l0_s5_b_glu, l0_w_out, l0_xa_norm, l0_mem_norm, l0_xa_wq, l0_xa_wkv, l0_xa_wo, l0_ffn_norm, l0_ffn_w_up, l0_ffn_conv, l0_ffn_w_down, l1_mix_norm, l1_w_in, l1_conv, l1_a_log, l1_dt_bias, l1_o_norm, l1_w_out, l1_xa_norm, l1_mem_norm, l1_xa_wq, l1_xa_wkv, l1_xa_wo, l1_ffn_norm, l1_ffn_w_up, l1_ffn_conv, l1_ffn_w_down, final_norm, loss_target, m_l0_mix_norm, m_l0_w_in, m_l0_ret_norm, m_l0_s5_lambda_re, m_l0_s5_lambda_im, m_l0_s5_b_re, m_l0_s5_b_im, m_l0_s5_c_re, m_l0_s5_c_im, m_l0_s5_d, m_l0_s5_log_dt, m_l0_s5_w_glu, m_l0_s5_b_glu, m_l0_w_out, m_l0_xa_norm, m_l0_mem_norm, m_l0_xa_wq, m_l0_xa_wkv, m_l0_xa_wo, m_l0_ffn_norm, m_l0_ffn_w_up, m_l0_ffn_conv, m_l0_ffn_w_down, m_l1_mix_norm, m_l1_w_in, m_l1_conv, m_l1_a_log, m_l1_dt_bias, m_l1_o_norm, m_l1_w_out, m_l1_xa_norm, m_l1_mem_norm, m_l1_xa_wq, m_l1_xa_wkv, m_l1_xa_wo, m_l1_ffn_norm, m_l1_ffn_w_up, m_l1_ffn_conv, m_l1_ffn_w_down, m_final_norm, v_l0_mix_norm, v_l0_w_in, v_l0_ret_norm, v_l0_s5_lambda_re, v_l0_s5_lambda_im, v_l0_s5_b_re, v_l0_s5_b_im, v_l0_s5_c_re, v_l0_s5_c_im, v_l0_s5_d, v_l0_s5_log_dt, v_l0_s5_w_glu, v_l0_s5_b_glu, v_l0_w_out, v_l0_xa_norm, v_l0_mem_norm, v_l0_xa_wq, v_l0_xa_wkv, v_l0_xa_wo, v_l0_ffn_norm, v_l0_ffn_w_up, v_l0_ffn_conv, v_l0_ffn_w_down, v_l1_mix_norm, v_l1_w_in, v_l1_conv, v_l1_a_log, v_l1_dt_bias, v_l1_o_norm, v_l1_w_out, v_l1_xa_norm, v_l1_mem_norm, v_l1_xa_wq, v_l1_xa_wkv, v_l1_xa_wo, v_l1_ffn_norm, v_l1_ffn_w_up, v_l1_ffn_conv, v_l1_ffn_w_down, v_final_norm):
    given = dict(x=x, mem=mem, l0_mix_norm=l0_mix_norm, l0_w_in=l0_w_in, l0_ret_norm=l0_ret_norm, l0_s5_lambda_re=l0_s5_lambda_re, l0_s5_lambda_im=l0_s5_lambda_im, l0_s5_b_re=l0_s5_b_re, l0_s5_b_im=l0_s5_b_im, l0_s5_c_re=l0_s5_c_re, l0_s5_c_im=l0_s5_c_im, l0_s5_d=l0_s5_d, l0_s5_log_dt=l0_s5_log_dt, l0_s5_w_glu=l0_s5_w_glu, l0_s5_b_glu=l0_s5_b_glu, l0_w_out=l0_w_out, l0_xa_norm=l0_xa_norm, l0_mem_norm=l0_mem_norm, l0_xa_wq=l0_xa_wq, l0_xa_wkv=l0_xa_wkv, l0_xa_wo=l0_xa_wo, l0_ffn_norm=l0_ffn_norm, l0_ffn_w_up=l0_ffn_w_up, l0_ffn_conv=l0_ffn_conv, l0_ffn_w_down=l0_ffn_w_down, l1_mix_norm=l1_mix_norm, l1_w_in=l1_w_in, l1_conv=l1_conv, l1_a_log=l1_a_log, l1_dt_bias=l1_dt_bias, l1_o_norm=l1_o_norm, l1_w_out=l1_w_out, l1_xa_norm=l1_xa_norm, l1_mem_norm=l1_mem_norm, l1_xa_wq=l1_xa_wq, l1_xa_wkv=l1_xa_wkv, l1_xa_wo=l1_xa_wo, l1_ffn_norm=l1_ffn_norm, l1_ffn_w_up=l1_ffn_w_up, l1_ffn_conv=l1_ffn_conv, l1_ffn_w_down=l1_ffn_w_down, final_norm=final_norm, loss_target=loss_target, m_l0_mix_norm=m_l0_mix_norm, m_l0_w_in=m_l0_w_in, m_l0_ret_norm=m_l0_ret_norm, m_l0_s5_lambda_re=m_l0_s5_lambda_re, m_l0_s5_lambda_im=m_l0_s5_lambda_im, m_l0_s5_b_re=m_l0_s5_b_re, m_l0_s5_b_im=m_l0_s5_b_im, m_l0_s5_c_re=m_l0_s5_c_re, m_l0_s5_c_im=m_l0_s5_c_im, m_l0_s5_d=m_l0_s5_d, m_l0_s5_log_dt=m_l0_s5_log_dt, m_l0_s5_w_glu=m_l0_s5_w_glu, m_l0_s5_b_glu=m_l0_s5_b_glu, m_l0_w_out=m_l0_w_out, m_l0_xa_norm=m_l0_xa_norm, m_l0_mem_norm=m_l0_mem_norm, m_l0_xa_wq=m_l0_xa_wq, m_l0_xa_wkv=m_l0_xa_wkv, m_l0_xa_wo=m_l0_xa_wo, m_l0_ffn_norm=m_l0_ffn_norm, m_l0_ffn_w_up=m_l0_ffn_w_up, m_l0_ffn_conv=m_l0_ffn_conv, m_l0_ffn_w_down=m_l0_ffn_w_down, m_l1_mix_norm=m_l1_mix_norm, m_l1_w_in=m_l1_w_in, m_l1_conv=m_l1_conv, m_l1_a_log=m_l1_a_log, m_l1_dt_bias=m_l1_dt_bias, m_l1_o_norm=m_l1_o_norm, m_l1_w_out=m_l1_w_out, m_l1_xa_norm=m_l1_xa_norm, m_l1_mem_norm=m_l1_mem_norm, m_l1_xa_wq=m_l1_xa_wq, m_l1_xa_wkv=m_l1_xa_wkv, m_l1_xa_wo=m_l1_xa_wo, m_l1_ffn_norm=m_l1_ffn_norm, m_l1_ffn_w_up=m_l1_ffn_w_up, m_l1_ffn_conv=m_l1_ffn_conv, m_l1_ffn_w_down=m_l1_ffn_w_down, m_final_norm=m_final_norm, v_l0_mix_norm=v_l0_mix_norm, v_l0_w_in=v_l0_w_in, v_l0_ret_norm=v_l0_ret_norm, v_l0_s5_lambda_re=v_l0_s5_lambda_re, v_l0_s5_lambda_im=v_l0_s5_lambda_im, v_l0_s5_b_re=v_l0_s5_b_re, v_l0_s5_b_im=v_l0_s5_b_im, v_l0_s5_c_re=v_l0_s5_c_re, v_l0_s5_c_im=v_l0_s5_c_im, v_l0_s5_d=v_l0_s5_d, v_l0_s5_log_dt=v_l0_s5_log_dt, v_l0_s5_w_glu=v_l0_s5_w_glu, v_l0_s5_b_glu=v_l0_s5_b_glu, v_l0_w_out=v_l0_w_out, v_l0_xa_norm=v_l0_xa_norm, v_l0_mem_norm=v_l0_mem_norm, v_l0_xa_wq=v_l0_xa_wq, v_l0_xa_wkv=v_l0_xa_wkv, v_l0_xa_wo=v_l0_xa_wo, v_l0_ffn_norm=v_l0_ffn_norm, v_l0_ffn_w_up=v_l0_ffn_w_up, v_l0_ffn_conv=v_l0_ffn_conv, v_l0_ffn_w_down=v_l0_ffn_w_down, v_l1_mix_norm=v_l1_mix_norm, v_l1_w_in=v_l1_w_in, v_l1_conv=v_l1_conv, v_l1_a_log=v_l1_a_log, v_l1_dt_bias=v_l1_dt_bias, v_l1_o_norm=v_l1_o_norm, v_l1_w_out=v_l1_w_out, v_l1_xa_norm=v_l1_xa_norm, v_l1_mem_norm=v_l1_mem_norm, v_l1_xa_wq=v_l1_xa_wq, v_l1_xa_wkv=v_l1_xa_wkv, v_l1_xa_wo=v_l1_xa_wo, v_l1_ffn_norm=v_l1_ffn_norm, v_l1_ffn_w_up=v_l1_ffn_w_up, v_l1_ffn_conv=v_l1_ffn_conv, v_l1_ffn_w_down=v_l1_ffn_w_down, v_final_norm=v_final_norm)
    weights = {n: given[n] for n in TWIN_WEIGHTS}
    shared = {n: given[n] for n in SHARED_INPUTS}
    per_example = {n: given[n] for n in ['x', 'mem']}
    grad_fn = _jax.value_and_grad(_loss, argnums=(0, 1))

    def one_microbatch(ex, loss_target):
        ex = dict(ex)
        diff = ex.pop(TWIN_DIFF_INPUT)
        return grad_fn(weights, diff, {**shared, **ex}, loss_target)

    if N_MICROBATCH == 1:
        loss, (grad_w, grad_x) = one_microbatch(per_example, given["loss_target"])
    else:
        def body(carry, xs):
            loss_sum, grad_sum = carry
            l_k, (gw_k, gx_k) = one_microbatch(xs[0], xs[1])
            with _jax.named_scope("update"):
                return (loss_sum + l_k, _jax.tree.map(_jnp.add, grad_sum, gw_k)), gx_k

        init = (_jnp.zeros((), _jnp.float32), _jax.tree.map(_jnp.zeros_like, weights))
        (loss, grad_w), grad_x = _jax.lax.scan(body, init, (per_example, given["loss_target"]))
    with _jax.named_scope("update"):
        delta_w, new_m, new_v = {}, {}, {}
        for n in TWIN_WEIGHTS:
            delta_w[n], new_m[n], new_v[n] = _adamw(weights[n], grad_w[n], given["m_" + n], given["v_" + n])
    return (loss, grad_x, *[grad_w[n] for n in TWIN_WEIGHTS], *[delta_w[n] for n in TWIN_WEIGHTS],
            *[new_m[n] for n in TWIN_WEIGHTS], *[new_v[n] for n in TWIN_WEIGHTS])
```

```python
import functools
import math

import numpy as np
import jax
import jax.numpy as jnp
from jax import lax
from jax.experimental import pallas as pl
from jax.experimental.pallas import tpu as pltpu

F32 = jnp.float32
BF16 = jnp.bfloat16
EPS = 1e-6
N_DEV = 8
LANES = 128
VMEM_LIMIT = 48 * 1024 * 1024
HI = lax.Precision.HIGHEST

RET_HEADS, RET_DH, RET_CHUNK = 4, 128, 128
S5_GROUPS, S5_GROUP, S5_STATE = 32, 16, 64
GDN_HEADS, GDN_DH, GDN_CHUNK, GDN_CONV = 8, 128, 64, 4
XA_HEADS, XA_DH = 4, 256
FFN_CONV = 3
SCAN_ROWS = 256

ADAM_LR, ADAM_B1, ADAM_B2, ADAM_EPS, ADAM_WD, ADAM_STEP = 0.001, 0.9, 0.999, 1e-08, 0.01, 10


def _cp(*sem):
    return pltpu.CompilerParams(dimension_semantics=sem if sem else None, vmem_limit_bytes=VMEM_LIMIT)


def _tile(n, cap):
    if n <= cap:
        return n
    best = None
    for t in range(LANES, cap + 1, LANES):
        if n % t == 0:
            best = t
    assert best is not None, n
    return best


def _dot(a, b, ca=1, cb=0, precision=None):
    return lax.dot_general(a, b, (((ca,), (cb,)), ((), ())), precision=precision, preferred_element_type=F32)


def _mxu(a, b, ca=1, cb=0):
    return _dot(a.astype(BF16), b.astype(BF16), ca, cb)


def _sigmoid(x):
    return 1.0 / (1.0 + jnp.exp(-x))


def _shift_down(x, k):
    row = lax.broadcasted_iota(jnp.int32, x.shape, 0)
    return jnp.where(row >= k, pltpu.roll(x, k, 0), 0.0)


def _shift_up(x, k):
    n = x.shape[0]
    row = lax.broadcasted_iota(jnp.int32, x.shape, 0)
    return jnp.where(row < n - k, pltpu.roll(x, n - k, 0), 0.0)


def _mesh_pos():
    return lax.axis_index("x"), lax.axis_index("y"), lax.axis_index("c")


def _slot(px, py, pc):
    return 4 * px + 2 * py + pc


def _all_gather(arrs, name):
    n = len(arrs)
    hbm = pl.BlockSpec(memory_space=pl.ANY)

    def body(*refs):
        ins, outs = refs[:n], refs[n:2 * n]
        send_sems, recv_sems, local_sems = refs[2 * n:]
        x, y, c = _mesh_pos()
        me, sibling = (x, y, c), (x, y, 1 - c)
        chips = [(1 - x, y), (x, 1 - y), (1 - x, 1 - y)]

        def copy(a, k, block, to, src=None):
            dst = outs[a].at[_slot(*block)]
            return pltpu.make_async_remote_copy(
                src_ref=dst if src is None else src, dst_ref=dst,
                send_sem=send_sems.at[a, k], recv_sem=recv_sems.at[a, k],
                device_id=to, device_id_type=pl.DeviceIdType.MESH)

        mine = [pltpu.make_async_copy(ins[a], outs[a].at[_slot(*me)], local_sems.at[a]) for a in range(n)]
        for cp in mine:
            cp.start()
        first = []
        for a in range(n):
            first.append(copy(a, 0, me, sibling, src=ins[a]))
            first += [copy(a, 1 + j, me, (*chip, c), src=ins[a]) for j, chip in enumerate(chips)]
        for cp in first:
            cp.start()
        passed = []
        for j, chip in enumerate(chips):
            for a in range(n):
                copy(a, 1 + j, (*chip, c), me).wait_recv()
                fw = copy(a, 4 + j, (*chip, c), sibling)
                fw.start()
                passed.append(fw)
        for a in range(n):
            copy(a, 0, sibling, me).wait_recv()
            for j, chip in enumerate(chips):
                copy(a, 4 + j, (*chip, 1 - c), me).wait_recv()
        for cp in first + passed:
            cp.wait_send()
        for cp in mine:
            cp.wait()

    return pl.pallas_call(
        body, name=name,
        out_shape=[jax.ShapeDtypeStruct((N_DEV,) + a.shape, a.dtype) for a in arrs],
        in_specs=[hbm] * n, out_specs=[hbm] * n,
        scratch_shapes=[pltpu.SemaphoreType.DMA((n, 7)), pltpu.SemaphoreType.DMA((n, 7)),
                        pltpu.SemaphoreType.DMA((n,))],
    )(*arrs)


def _exchange(scatter, gather, name):
    ns, ng = len(scatter), len(gather)
    n = ns + ng
    hbm = pl.BlockSpec(memory_space=pl.ANY)

    def body(*refs):
        ins, outs = refs[:n], refs[n:2 * n]
        send_sems, recv_sems, local_sems = refs[2 * n:]
        x, y, c = _mesh_pos()
        me = _slot(x, y, c)
        flips = [(fx, fy, fc) for fx in (0, 1) for fy in (0, 1) for fc in (0, 1)][1:]

        def peer(f):
            return (1 - x if f[0] else x, 1 - y if f[1] else y, 1 - c if f[2] else c)

        def copy(a, k, to):
            src = ins[a].at[_slot(*to)] if a < ns else ins[a]
            return pltpu.make_async_remote_copy(
                src_ref=src, dst_ref=outs[a].at[me],
                send_sem=send_sems.at[a, k], recv_sem=recv_sems.at[a, k],
                device_id=to, device_id_type=pl.DeviceIdType.MESH)

        def arrival(a, k, frm):
            dst = outs[a].at[_slot(*frm)]
            return pltpu.make_async_remote_copy(
                src_ref=dst, dst_ref=dst, send_sem=send_sems.at[a, k], recv_sem=recv_sems.at[a, k],
                device_id=frm, device_id_type=pl.DeviceIdType.MESH)

        mine = []
        for a in range(n):
            src = ins[a].at[me] if a < ns else ins[a]
            mine.append(pltpu.make_async_copy(src, outs[a].at[me], local_sems.at[a]))
        for cp in mine:
            cp.start()
        sends = [copy(a, k, peer(f)) for k, f in enumerate(flips) for a in range(n)]
        for cp in sends:
            cp.start()
        for k, f in enumerate(flips):
            for a in range(n):
                arrival(a, k, peer(f)).wait_recv()
        for cp in sends:
            cp.wait_send()
        for cp in mine:
            cp.wait()

    out_shape = [jax.ShapeDtypeStruct(a.shape, a.dtype) for a in scatter]
    out_shape += [jax.ShapeDtypeStruct((N_DEV,) + a.shape, a.dtype) for a in gather]
    return pl.pallas_call(
        body, name=name, out_shape=out_shape, in_specs=[hbm] * n, out_specs=[hbm] * n,
        scratch_shapes=[pltpu.SemaphoreType.DMA((n, 7)), pltpu.SemaphoreType.DMA((n, 7)),
                        pltpu.SemaphoreType.DMA((n,))],
    )(*scatter, *gather)


def _mm(a, b, *, ta=False, tb=False, out_dtype=F32, res=None, name="mm"):
    m, k = (a.shape[1], a.shape[0]) if ta else a.shape
    n = b.shape[0] if tb else b.shape[1]
    assert k == (b.shape[1] if tb else b.shape[0]), (a.shape, b.shape, ta, tb)
    tm, tn, tk = _tile(m, 1024), _tile(n, 512), _tile(k, 1024)
    nk = k // tk
    has_res = res is not None

    def body(*refs):
        if has_res:
            a_ref, b_ref, r_ref, o_ref, acc = refs
        else:
            a_ref, b_ref, o_ref, acc = refs
        kk = pl.program_id(2)

        @pl.when(kk == 0)
        def _():
            acc[...] = jnp.zeros_like(acc)

        acc[...] += _mxu(a_ref[...], b_ref[...], 0 if ta else 1, 1 if tb else 0)

        @pl.when(kk == nk - 1)
        def _():
            r = acc[...]
            if has_res:
                r = r + r_ref[...].astype(F32)
            o_ref[...] = r.astype(out_dtype)

    a_spec = pl.BlockSpec((tk, tm), lambda i, j, kk: (kk, i)) if ta else pl.BlockSpec((tm, tk), lambda i, j, kk: (i, kk))
    b_spec = pl.BlockSpec((tn, tk), lambda i, j, kk: (j, kk)) if tb else pl.BlockSpec((tk, tn), lambda i, j, kk: (kk, j))
    o_spec = pl.BlockSpec((tm, tn), lambda i, j, kk: (i, j))
    in_specs = [a_spec, b_spec] + ([o_spec] if has_res else [])
    args = (a, b) + ((res,) if has_res else ())
    return pl.pallas_call(
        body, name=name, grid=(m // tm, n // tn, nk), in_specs=in_specs, out_specs=o_spec,
        out_shape=jax.ShapeDtypeStruct((m, n), out_dtype),
        scratch_shapes=[pltpu.VMEM((tm, tn), F32)],
        compiler_params=_cp("parallel", "parallel", "arbitrary"),
    )(*args)


def _norm_fwd(x, g, name):
    s, d = x.shape
    tr = min(512, s)

    def body(x_ref, g_ref, o_ref):
        xv = x_ref[...]
        r = lax.rsqrt(jnp.mean(xv * xv, axis=-1, keepdims=True) + EPS)
        o_ref[...] = (xv * r * g_ref[...]).astype(BF16)

    row = pl.BlockSpec((tr, d), lambda i: (i, 0))
    return pl.pallas_call(
        body, name=name, grid=(s // tr,), in_specs=[row, pl.BlockSpec((1, d), lambda i: (0, 0))],
        out_specs=row, out_shape=jax.ShapeDtypeStruct((s, d), BF16), compiler_params=_cp("parallel"),
    )(x, g.reshape(1, d))


def _norm_bwd(x, g, dh, dres, name):
    s, d = x.shape
    tr = min(512, s)

    def body(x_ref, g_ref, dh_ref, dres_ref, dx_ref, dg_ref):
        @pl.when(pl.program_id(0) == 0)
        def _():
            dg_ref[...] = jnp.zeros_like(dg_ref)

        xv = x_ref[...]
        r = lax.rsqrt(jnp.mean(xv * xv, axis=-1, keepdims=True) + EPS)
        xn = xv * r
        dhv = dh_ref[...].astype(F32)
        dg_ref[...] += jnp.sum(dhv * xn, axis=0, keepdims=True)
        dhg = dhv * g_ref[...]
        dx_ref[...] = dres_ref[...] + r * (dhg - xn * jnp.mean(dhg * xn, axis=-1, keepdims=True))

    row = pl.BlockSpec((tr, d), lambda i: (i, 0))
    vec = pl.BlockSpec((1, d), lambda i: (0, 0))
    return pl.pallas_call(
        body, name=name, grid=(s // tr,), in_specs=[row, vec, row, row], out_specs=[row, vec],
        out_shape=[jax.ShapeDtypeStruct((s, d), F32), jax.ShapeDtypeStruct((1, d), F32)],
        compiler_params=_cp("arbitrary"),
    )(x, g.reshape(1, d), dh, dres)


def _loss_head(x, g, tgt, name):
    s, d = x.shape
    tr = min(512, s)

    def body(x_ref, g_ref, t_ref, l_ref, dx_ref, dg_ref):
        @pl.when(pl.program_id(0) == 0)
        def _():
            dg_ref[...] = jnp.zeros_like(dg_ref)
            l_ref[...] = jnp.zeros_like(l_ref)

        xv = x_ref[...]
        r = lax.rsqrt(jnp.mean(xv * xv, axis=-1, keepdims=True) + EPS)
        xn = xv * r
        err = xn * g_ref[...] - t_ref[...]
        part = 0.5 * jnp.sum(jnp.mean(err * err, axis=-1, keepdims=True), axis=0, keepdims=True)
        l_ref[...] += jnp.broadcast_to(part, l_ref.shape)
        dy = err * (1.0 / d)
        dg_ref[...] += jnp.sum(dy * xn, axis=0, keepdims=True)
        dyg = dy * g_ref[...]
        dx_ref[...] = r * (dyg - xn * jnp.mean(dyg * xn, axis=-1, keepdims=True))

    row = pl.BlockSpec((tr, d), lambda i: (i, 0))
    vec = pl.BlockSpec((1, d), lambda i: (0, 0))
    return pl.pallas_call(
        body, name=name, grid=(s // tr,), in_specs=[row, vec, row],
        out_specs=[pl.BlockSpec((1, LANES), lambda i: (0, 0)), row, vec],
        out_shape=[jax.ShapeDtypeStruct((1, LANES), F32), jax.ShapeDtypeStruct((s, d), F32),
                   jax.ShapeDtypeStruct((1, d), F32)],
        compiler_params=_cp("arbitrary"),
    )(x, g.reshape(1, d), tgt)


def _adamw(parts, w, m, v, name):
    r, c = w.shape
    cap = max(8, 256 * 1024 // c)
    tr = max(t for t in range(8, min(r, cap) + 1, 8) if r % t == 0)
    bc1 = 1.0 - ADAM_B1 ** ADAM_STEP
    bc2 = 1.0 - ADAM_B2 ** ADAM_STEP

    def body(p_ref, w_ref, m_ref, v_ref, g_ref, d_ref, nm_ref, nv_ref):
        g = p_ref[0].astype(F32)
        for i in range(1, N_DEV):
            g = g + p_ref[i].astype(F32)
        mm = ADAM_B1 * m_ref[...] + (1.0 - ADAM_B1) * g
        vv = ADAM_B2 * v_ref[...] + (1.0 - ADAM_B2) * (g * g)
        g_ref[...] = g
        nm_ref[...] = mm
        nv_ref[...] = vv
        d_ref[...] = -ADAM_LR * ((mm / bc1) / (jnp.sqrt(vv / bc2) + ADAM_EPS) + ADAM_WD * w_ref[...])

    blk = pl.BlockSpec((tr, c), lambda i: (i, 0))
    return pl.pallas_call(
        body, name=name, grid=(r // tr,),
        in_specs=[pl.BlockSpec((N_DEV, tr, c), lambda i: (0, i, 0)), blk, blk, blk],
        out_specs=[blk] * 4, out_shape=[jax.ShapeDtypeStruct((r, c), F32)] * 4,
        compiler_params=_cp("parallel"),
    )(parts, w, m, v)


def _conv_fwd(x, w_ref, kw):
    acc = w_ref[kw - 1:kw, :] * x
    for j in range(kw - 1):
        acc = acc + w_ref[j:j + 1, :] * _shift_down(x, kw - 1 - j)
    return acc


def _conv_bwd(x, dy, w_ref, dw_ref, kw):
    dx = w_ref[kw - 1:kw, :] * dy
    dw_ref[kw - 1:kw, :] = jnp.sum(dy * x, axis=0, keepdims=True)
    for j in range(kw - 1):
        dx = dx + w_ref[j:j + 1, :] * _shift_up(dy, kw - 1 - j)
        dw_ref[j:j + 1, :] = jnp.sum(dy * _shift_down(x, kw - 1 - j), axis=0, keepdims=True)
    return dx


def _ffn_act_fwd(pre, cw, name):
    s, f2 = pre.shape
    nt = f2 // 2 // LANES

    def body(pu_ref, pg_ref, wu_ref, wg_ref, o_ref):
        up = _conv_fwd(pu_ref[...], wu_ref, FFN_CONV)
        gate = _conv_fwd(pg_ref[...], wg_ref, FFN_CONV)
        o_ref[...] = (gate * _sigmoid(gate) * up).astype(BF16)

    def col(rows, off):
        return pl.BlockSpec((rows, LANES), lambda j: (0, j + off))

    return pl.pallas_call(
        body, name=name, grid=(nt,),
        in_specs=[col(s, 0), col(s, nt), col(FFN_CONV, 0), col(FFN_CONV, nt)], out_specs=col(s, 0),
        out_shape=jax.ShapeDtypeStruct((s, f2 // 2), BF16), compiler_params=_cp("parallel"),
    )(pre, pre, cw, cw)


def _ffn_act_bwd(pre, cw, dact, name):
    s, f2 = pre.shape
    f = f2 // 2
    nt = f // LANES

    def body(pu_ref, pg_ref, wu_ref, wg_ref, da_ref, dpu_ref, dpg_ref, dwu_ref, dwg_ref):
        pu, pg = pu_ref[...], pg_ref[...]
        up = _conv_fwd(pu, wu_ref, FFN_CONV)
        gate = _conv_fwd(pg, wg_ref, FFN_CONV)
        sg = _sigmoid(gate)
        da = da_ref[...]
        dup = da * gate * sg
        dgate = da * up * (sg * (1.0 + gate * (1.0 - sg)))
        dpu_ref[...] = _conv_bwd(pu, dup, wu_ref, dwu_ref, FFN_CONV).astype(BF16)
        dpg_ref[...] = _conv_bwd(pg, dgate, wg_ref, dwg_ref, FFN_CONV).astype(BF16)

    def col(rows, off):
        return pl.BlockSpec((rows, LANES), lambda j: (0, j + off))

    return pl.pallas_call(
        body, name=name, grid=(nt,),
        in_specs=[col(s, 0), col(s, nt), col(FFN_CONV, 0), col(FFN_CONV, nt), col(s, 0)],
        out_specs=[col(s, 0), col(s, 0), col(FFN_CONV, 0), col(FFN_CONV, 0)],
        out_shape=[jax.ShapeDtypeStruct((s, f), BF16), jax.ShapeDtypeStruct((s, f), BF16),
                   jax.ShapeDtypeStruct((FFN_CONV, f), F32), jax.ShapeDtypeStruct((FFN_CONV, f), F32)],
        compiler_params=_cp("parallel"),
    )(pre, pre, cw, cw, dact)


def _xa_probs(qh, kh):
    sc = _mxu(qh, kh, 1, 1) * (XA_DH ** -0.5)
    e = jnp.exp(sc - jnp.max(sc, axis=-1, keepdims=True))
    return e / jnp.sum(e, axis=-1, keepdims=True)


def _xattn_fwd(q, kv, name):
    s, d = q.shape
    m = kv.shape[0]
    tr = min(512, s)

    def body(q_ref, kv_ref, o_ref):
        for h in range(XA_HEADS):
            lo, hi = h * XA_DH, (h + 1) * XA_DH
            p = _xa_probs(q_ref[:, lo:hi], kv_ref[:, lo:hi])
            o_ref[:, lo:hi] = _mxu(p, kv_ref[:, d + lo:d + hi]).astype(BF16)

    row = pl.BlockSpec((tr, d), lambda i: (i, 0))
    return pl.pallas_call(
        body, name=name, grid=(s // tr,), in_specs=[row, pl.BlockSpec((m, 2 * d), lambda i: (0, 0))],
        out_specs=row, out_shape=jax.ShapeDtypeStruct((s, d), BF16), compiler_params=_cp("parallel"),
    )(q, kv)


def _xattn_bwd(q, kv, do, name):
    s, d = q.shape
    m = kv.shape[0]
    tr = min(512, s)

    def body(q_ref, kv_ref, do_ref, dq_ref, dkv_ref):
        @pl.when(pl.program_id(0) == 0)
        def _():
            dkv_ref[...] = jnp.zeros_like(dkv_ref)

        for h in range(XA_HEADS):
            lo, hi = h * XA_DH, (h + 1) * XA_DH
            qh, kh, vh = q_ref[:, lo:hi], kv_ref[:, lo:hi], kv_ref[:, d + lo:d + hi]
            doh = do_ref[:, lo:hi]
            p = _xa_probs(qh, kh)
            dp = _mxu(doh, vh, 1, 1)
            ds = p * (dp - jnp.sum(p * dp, axis=-1, keepdims=True)) * (XA_DH ** -0.5)
            dq_ref[:, lo:hi] = _mxu(ds, kh).astype(BF16)
            dkv_ref[:, lo:hi] += _mxu(ds, qh, 0, 0)
            dkv_ref[:, d + lo:d + hi] += _mxu(p, doh, 0, 0)

    row = pl.BlockSpec((tr, d), lambda i: (i, 0))
    full = pl.BlockSpec((m, 2 * d), lambda i: (0, 0))
    return pl.pallas_call(
        body, name=name, grid=(s // tr,), in_specs=[row, full, row], out_specs=[row, full],
        out_shape=[jax.ShapeDtypeStruct((s, d), BF16), jax.ShapeDtypeStruct((m, 2 * d), F32)],
        compiler_params=_cp("arbitrary"),
    )(q, kv, do)


def _ret_tables():
    c = RET_CHUNK
    lg = np.log1p(-np.exp2(-5.0 - np.arange(RET_HEADS, dtype=np.float32))).astype(np.float32)
    idx = np.arange(c, dtype=np.float32)
    diff = idx[:, None] - idx[None, :]
    intra = np.where(diff >= 0, np.exp(lg[:, None, None] * np.where(diff >= 0, diff, 0.0)), 0.0)
    rk = np.broadcast_to(np.exp(lg[:, None] * (c - 1 - idx))[:, :, None], (RET_HEADS, c, LANES))
    rq = np.broadcast_to(np.exp(lg[:, None] * (idx + 1))[:, :, None], (RET_HEADS, c, LANES))
    return jnp.asarray(np.stack([intra, rk, rq], axis=1).astype(np.float32))


def _rope_tables(s):
    half = RET_DH // 2
    inv = jnp.exp(-math.log(10000.0) * jnp.arange(half, dtype=F32) / half)
    ang = jnp.arange(s, dtype=F32)[:, None] * inv[None, :]
    cos, sin = jnp.cos(ang), jnp.sin(ang)
    return jnp.concatenate([cos, cos], axis=1), jnp.concatenate([-sin, sin], axis=1)


def _ret_specs(n_of):
    c = RET_CHUNK

    def head(off):
        return pl.BlockSpec((c, RET_DH), lambda h, n: (n_of(n), h + off * RET_HEADS))

    pos = pl.BlockSpec((c, RET_DH), lambda h, n: (n_of(n), 0))
    gain = pl.BlockSpec((1, RET_DH), lambda h, n: (0, h))
    tab = pl.BlockSpec((None, 3, c, LANES), lambda h, n: (h, 0, 0, 0))
    st = pl.BlockSpec((None, None, RET_DH, RET_DH), lambda h, n: (h, n_of(n), 0, 0))
    return head, pos, gain, tab, st


def _ret_chunk(q_ref, k_ref, v_ref, cos_ref, sin_ref, tab_ref, prev):
    cos, sin = cos_ref[...], sin_ref[...]
    q = q_ref[...] * cos + pltpu.roll(q_ref[...], RET_DH // 2, 1) * sin
    k = (k_ref[...] * cos + pltpu.roll(k_ref[...], RET_DH // 2, 1) * sin) * (RET_DH ** -0.5)
    v = v_ref[...]
    scores = _mxu(q, k, 1, 1) * tab_ref[0]
    qdec = q * tab_ref[2]
    kdec = k * tab_ref[1]
    o = _mxu(scores, v) + _mxu(qdec, prev)
    return q, k, v, scores, qdec, kdec, o


def _ret_fwd(proj, cos, sin, gain, name):
    s = proj.shape[0]
    c = RET_CHUNK
    nc = s // c
    head, pos, gvec, tab, st = _ret_specs(lambda n: n)

    def body(q_ref, k_ref, v_ref, g_ref, cos_ref, sin_ref, rn_ref, tab_ref, o_ref, st_ref, state):
        @pl.when(pl.program_id(1) == 0)
        def _():
            state[...] = jnp.zeros_like(state)

        prev = state[...]
        st_ref[...] = prev
        _, _, v, _, _, kdec, o = _ret_chunk(q_ref, k_ref, v_ref, cos_ref, sin_ref, tab_ref, prev)
        state[...] = prev * tab_ref[2, c - 1:c, :] + _mxu(kdec, v, 0, 0)
        r = lax.rsqrt(jnp.mean(o * o, axis=-1, keepdims=True) + EPS)
        gate = g_ref[...]
        o_ref[...] = (o * r * rn_ref[...] * (gate * _sigmoid(gate))).astype(BF16)

    return pl.pallas_call(
        body, name=name, grid=(RET_HEADS, nc),
        in_specs=[head(0), head(1), head(2), head(3), pos, pos, gvec, tab],
        out_specs=[head(0), st],
        out_shape=[jax.ShapeDtypeStruct((s, RET_HEADS * RET_DH), BF16),
                   jax.ShapeDtypeStruct((RET_HEADS, nc, RET_DH, RET_DH), F32)],
        scratch_shapes=[pltpu.VMEM((RET_DH, RET_DH), F32)],
        compiler_params=_cp("parallel", "arbitrary"),
    )(proj, proj, proj, proj, cos, sin, gain.reshape(1, -1), _ret_tables())


def _ret_bwd(proj, cos, sin, gain, states, dmerged, name):
    s = proj.shape[0]
    c = RET_CHUNK
    nc = s // c
    head, pos, gvec, tab, st = _ret_specs(lambda n: nc - 1 - n)

    def body(q_ref, k_ref, v_ref, g_ref, cos_ref, sin_ref, rn_ref, tab_ref, st_ref, do_ref,
             dq_ref, dk_ref, dv_ref, dg_ref, drn_ref, carry):
        @pl.when(pl.program_id(1) == 0)
        def _():
            carry[...] = jnp.zeros_like(carry)
            drn_ref[...] = jnp.zeros_like(drn_ref)

        prev = st_ref[...]
        q, k, v, scores, qdec, kdec, o = _ret_chunk(q_ref, k_ref, v_ref, cos_ref, sin_ref, tab_ref, prev)
        r = lax.rsqrt(jnp.mean(o * o, axis=-1, keepdims=True) + EPS)
        on = o * r
        gate = g_ref[...]
        sg = _sigmoid(gate)
        sil = gate * sg
        dout = do_ref[...]
        rn = rn_ref[...]
        dg_ref[...] = (dout * on * rn * (sg * (1.0 + gate * (1.0 - sg)))).astype(BF16)
        drn_ref[...] += jnp.sum(dout * on * sil, axis=0, keepdims=True)
        don = dout * rn * sil
        do = r * (don - on * jnp.mean(don * on, axis=-1, keepdims=True))
        dc = carry[...]
        dsc = _mxu(do, v, 1, 1) * tab_ref[0]
        dq = _mxu(dsc, k) + _mxu(do, prev, 1, 1) * tab_ref[2]
        dk = _mxu(dsc, q, 0, 0) + _mxu(v, dc, 1, 1) * tab_ref[1]
        dv = _mxu(scores, do, 0, 0) + _mxu(kdec, dc)
        carry[...] = _mxu(qdec, do, 0, 0) + dc * tab_ref[2, c - 1:c, :]
        cos, sin = cos_ref[...], sin_ref[...]
        dk = dk * (RET_DH ** -0.5)
        dq_ref[...] = (dq * cos + pltpu.roll(dq * sin, RET_DH // 2, 1)).astype(BF16)
        dk_ref[...] = (dk * cos + pltpu.roll(dk * sin, RET_DH // 2, 1)).astype(BF16)
        dv_ref[...] = dv.astype(BF16)

    width = RET_HEADS * RET_DH
    return pl.pallas_call(
        body, name=name, grid=(RET_HEADS, nc),
        in_specs=[head(0), head(1), head(2), head(3), pos, pos, gvec, tab, st, head(0)],
        out_specs=[head(0)] * 4 + [gvec],
        out_shape=[jax.ShapeDtypeStruct((s, width), BF16)] * 4 + [jax.ShapeDtypeStruct((1, width), F32)],
        scratch_shapes=[pltpu.VMEM((RET_DH, RET_DH), F32)],
        compiler_params=_cp("parallel", "arbitrary"),
    )(proj, proj, proj, proj, cos, sin, gain.reshape(1, -1), _ret_tables(), states, dmerged)


S5_TILE = 512


def _cmul_add(xr, xi, ar, ai, yr, yi):
    return xr + ar * yr - ai * yi, xi + ar * yi + ai * yr


def _s5_scan_fwd(bu, apow, name):
    s, w2 = bu.shape
    r = SCAN_ROWS
    t = S5_TILE
    steps = r.bit_length() - 1

    def body(b_ref, p_ref, o_ref, cr, ci):
        @pl.when(pl.program_id(1) == 0)
        def _():
            cr[...] = jnp.zeros_like(cr)
            ci[...] = jnp.zeros_like(ci)

        xr, xi = b_ref[:, :t], b_ref[:, t:]
        for k in range(steps):
            sh = 1 << k
            xr, xi = _cmul_add(xr, xi, p_ref[sh - 1:sh, :t], p_ref[sh - 1:sh, t:],
                               _shift_down(xr, sh), _shift_down(xi, sh))
        xr, xi = _cmul_add(xr, xi, p_ref[:, :t], p_ref[:, t:], cr[...], ci[...])
        o_ref[:, :t] = xr
        o_ref[:, t:] = xi
        cr[...] = xr[r - 1:r, :]
        ci[...] = xi[r - 1:r, :]

    blk = pl.BlockSpec((r, 2 * t), lambda j, i: (i, j))
    return pl.pallas_call(
        body, name=name, grid=(w2 // (2 * t), s // r),
        in_specs=[blk, pl.BlockSpec((r, 2 * t), lambda j, i: (0, j))], out_specs=blk,
        out_shape=jax.ShapeDtypeStruct((s, w2), F32),
        scratch_shapes=[pltpu.VMEM((1, t), F32), pltpu.VMEM((1, t), F32)],
        compiler_params=_cp("parallel", "arbitrary"),
    )(bu, apow)


def _s5_scan_bwd(dst, apow_rev, st, name):
    s, w2 = dst.shape
    r = SCAN_ROWS
    t = S5_TILE
    nb = s // r
    steps = r.bit_length() - 1

    def body(d_ref, p_ref, s_ref, sp_ref, g_ref, da_ref, cr, ci):
        i = pl.program_id(1)

        @pl.when(i == 0)
        def _():
            cr[...] = jnp.zeros_like(cr)
            ci[...] = jnp.zeros_like(ci)
            da_ref[...] = jnp.zeros_like(da_ref)

        xr, xi = d_ref[:, :t], d_ref[:, t:]
        for k in range(steps):
            sh = 1 << k
            xr, xi = _cmul_add(xr, xi, p_ref[r - sh:r - sh + 1, :t], p_ref[r - sh:r - sh + 1, t:],
                               _shift_up(xr, sh), _shift_up(xi, sh))
        xr, xi = _cmul_add(xr, xi, p_ref[:, :t], p_ref[:, t:], cr[...], ci[...])
        g_ref[:, :t] = xr.astype(BF16)
        g_ref[:, t:] = xi.astype(BF16)
        cr[...] = xr[0:1, :]
        ci[...] = xi[0:1, :]
        first = i == nb - 1
        row = lax.broadcasted_iota(jnp.int32, (r, t), 0)
        last_r = jnp.where(first, 0.0, sp_ref[7:8, :t])
        last_i = jnp.where(first, 0.0, sp_ref[7:8, t:])
        pr = jnp.where(row == 0, last_r, pltpu.roll(s_ref[:, :t], 1, 0))
        pi = jnp.where(row == 0, last_i, pltpu.roll(s_ref[:, t:], 1, 0))
        da_ref[:, :t] += jnp.sum(xr * pr + xi * pi, axis=0, keepdims=True)
        da_ref[:, t:] += jnp.sum(xi * pr - xr * pi, axis=0, keepdims=True)

    blk = pl.BlockSpec((r, 2 * t), lambda j, i: (nb - 1 - i, j))
    halo = pl.BlockSpec((8, 2 * t), lambda j, i: (jnp.maximum((nb - 1 - i) * (r // 8) - 1, 0), j))
    vec = pl.BlockSpec((1, 2 * t), lambda j, i: (0, j))
    return pl.pallas_call(
        body, name=name, grid=(w2 // (2 * t), nb),
        in_specs=[blk, pl.BlockSpec((r, 2 * t), lambda j, i: (0, j)), blk, halo], out_specs=[blk, vec],
        out_shape=[jax.ShapeDtypeStruct((s, w2), BF16), jax.ShapeDtypeStruct((1, w2), F32)],
        scratch_shapes=[pltpu.VMEM((1, t), F32), pltpu.VMEM((1, t), F32)],
        compiler_params=_cp("parallel", "arbitrary"),
    )(dst, apow_rev, st, st)


_GELU_C = math.sqrt(2.0 / math.pi)
_GELU_A = 0.044715


def _gelu(y):
    return 0.5 * y * (1.0 + jnp.tanh(_GELU_C * (y + _GELU_A * y * y * y)))


def _gelu_grad(y):
    th = jnp.tanh(_GELU_C * (y + _GELU_A * y * y * y))
    return 0.5 * (1.0 + th) + 0.5 * y * (1.0 - th * th) * _GELU_C * (1.0 + 3.0 * _GELU_A * y * y)


def _row_call(body, name, s, ins, outs, acc=False):
    tr = min(512, s)

    def spec(width, cb, rows):
        if rows == 1:
            return pl.BlockSpec((1, width), lambda i: (0, cb))
        return pl.BlockSpec((tr, width), lambda i: (i, cb))

    in_specs = [spec(w, cb, a.shape[0]) for a, w, cb in ins]
    out_specs = [spec(w, cb, sd.shape[0]) for sd, w, cb in outs]
    return pl.pallas_call(
        body, name=name, grid=(s // tr,), in_specs=in_specs, out_specs=out_specs,
        out_shape=[sd for sd, _, _ in outs],
        compiler_params=_cp("arbitrary" if acc else "parallel"),
    )(*[a for a, _, _ in ins])


def _sds(shape, dtype):
    return jax.ShapeDtypeStruct(shape, dtype)


def _s5_gelu_fwd(yraw, proj, dvec, name):
    s, w = yraw.shape

    def body(y_ref, u_ref, d_ref, yo_ref, g_ref):
        y = y_ref[...] + d_ref[...] * u_ref[...]
        yo_ref[...] = y
        g_ref[...] = _gelu(y).astype(BF16)

    return _row_call(body, name, s, [(yraw, w, 0), (proj, w, 4), (dvec, w, 0)],
                     [(_sds((s, w), F32), w, 0), (_sds((s, w), BF16), w, 0)])


def _s5_glu_fwd(y, z, b, name):
    s, w = y.shape

    def body(y_ref, z_ref, b_ref, o_ref):
        o_ref[...] = (_gelu(y_ref[...]) * _sigmoid(z_ref[...] + b_ref[...])).astype(BF16)

    return _row_call(body, name, s, [(y, w, 0), (z, w, 0), (b, w, 0)], [(_sds((s, w), BF16), w, 0)])[0]


def _s5_glu_bwd(dmerged, y, z, b, name):
    s, w = y.shape

    def body(do_ref, y_ref, z_ref, b_ref, dz_ref, dg_ref, db_ref):
        @pl.when(pl.program_id(0) == 0)
        def _():
            db_ref[...] = jnp.zeros_like(db_ref)

        g = _gelu(y_ref[...])
        sg = _sigmoid(z_ref[...] + b_ref[...])
        dout = do_ref[...]
        dz = dout * g * sg * (1.0 - sg)
        dz_ref[...] = dz.astype(BF16)
        dg_ref[...] = dout * sg
        db_ref[...] += jnp.sum(dz, axis=0, keepdims=True)

    return _row_call(body, name, s, [(dmerged, w, 1), (y, w, 0), (z, w, 0), (b, w, 0)],
                     [(_sds((s, w), BF16), w, 0), (_sds((s, w), F32), w, 0), (_sds((1, w), F32), w, 0)], acc=True)


def _s5_gelu_bwd(dg1, dg2, y, proj, dvec, name):
    s, w = y.shape

    def body(a_ref, b_ref, y_ref, u_ref, d_ref, dy_ref, du_ref, dd_ref):
        @pl.when(pl.program_id(0) == 0)
        def _():
            dd_ref[...] = jnp.zeros_like(dd_ref)

        dy = (a_ref[...] + b_ref[...]) * _gelu_grad(y_ref[...])
        dy_ref[...] = dy.astype(BF16)
        du_ref[...] = dy * d_ref[...]
        dd_ref[...] += jnp.sum(dy * u_ref[...], axis=0, keepdims=True)

    return _row_call(body, name, s, [(dg1, w, 0), (dg2, w, 0), (y, w, 0), (proj, w, 4), (dvec, w, 0)],
                     [(_sds((s, w), BF16), w, 0), (_sds((s, w), F32), w, 0), (_sds((1, w), F32), w, 0)], acc=True)


def _gdn_conv_fwd(projx, cw, name):
    s = projx.shape[0]
    nh = GDN_HEADS

    def body(x_ref, w_ref, o_ref):
        j = pl.program_id(0)
        cv = _conv_fwd(x_ref[...], w_ref, GDN_CONV)
        y = cv * _sigmoid(cv)
        nrm = y * lax.rsqrt(jnp.sum(y * y, axis=-1, keepdims=True) + EPS)
        o_ref[...] = jnp.where(j < nh, nrm * (GDN_DH ** -0.5), jnp.where(j < 2 * nh, nrm, y))

    return pl.pallas_call(
        body, name=name, grid=(3 * nh,),
        in_specs=[pl.BlockSpec((s, GDN_DH), lambda j: (0, j)), pl.BlockSpec((GDN_CONV, GDN_DH), lambda j: (0, j))],
        out_specs=pl.BlockSpec((s, GDN_DH), lambda j: (0, j)),
        out_shape=jax.ShapeDtypeStruct((s, 3 * nh * GDN_DH), F32), compiler_params=_cp("parallel"),
    )(projx, cw)


def _gdn_conv_bwd(projx, cw, dqkv, name):
    s = projx.shape[0]
    nh = GDN_HEADS

    def body(x_ref, w_ref, d_ref, dx_ref, dw_ref):
        j = pl.program_id(0)
        x = x_ref[...]
        cv = _conv_fwd(x, w_ref, GDN_CONV)
        sg = _sigmoid(cv)
        y = cv * sg
        rinv = lax.rsqrt(jnp.sum(y * y, axis=-1, keepdims=True) + EPS)
        nrm = y * rinv
        dn = d_ref[...]
        dns = jnp.where(j < nh, dn * (GDN_DH ** -0.5), dn)
        dyn = rinv * (dns - nrm * jnp.sum(dns * nrm, axis=-1, keepdims=True))
        dy = jnp.where(j < 2 * nh, dyn, dn)
        dc = dy * (sg * (1.0 + cv * (1.0 - sg)))
        dx_ref[...] = _conv_bwd(x, dc, w_ref, dw_ref, GDN_CONV).astype(BF16)

    col = pl.BlockSpec((s, GDN_DH), lambda j: (0, j))
    wcol = pl.BlockSpec((GDN_CONV, GDN_DH), lambda j: (0, j))
    return pl.pallas_call(
        body, name=name, grid=(3 * nh,), in_specs=[col, wcol, col], out_specs=[col, wcol],
        out_shape=[jax.ShapeDtypeStruct((s, 3 * nh * GDN_DH), BF16), jax.ShapeDtypeStruct((GDN_CONV, 3 * nh * GDN_DH), F32)],
        compiler_params=_cp("parallel"),
    )(projx, cw, dqkv)


def _softplus(x):
    return jnp.maximum(x, 0.0) + jnp.log1p(jnp.exp(-jnp.abs(x)))


def _gdn_gates_fwd(projx, alog, dtb, name):
    s = projx.shape[0]
    w = GDN_HEADS * GDN_DH

    def body(b_ref, a_ref, al_ref, dt_ref, bo_ref, go_ref):
        bo_ref[...] = _sigmoid(b_ref[...])
        go_ref[...] = -jnp.exp(al_ref[...]) * _softplus(a_ref[...] + dt_ref[...])

    return _row_call(body, name, s, [(projx, w, 4), (projx, w, 5), (alog, w, 0), (dtb, w, 0)],
                     [(_sds((s, w), F32), w, 0), (_sds((s, w), F32), w, 0)])


def _gdn_gates_bwd(projx, alog, dtb, dbeta, dg, name):
    s = projx.shape[0]
    w = GDN_HEADS * GDN_DH

    def body(b_ref, a_ref, al_ref, dt_ref, dbe_ref, dg_ref, db_ref, da_ref, dal_ref, ddt_ref):
        @pl.when(pl.program_id(0) == 0)
        def _():
            dal_ref[...] = jnp.zeros_like(dal_ref)
            ddt_ref[...] = jnp.zeros_like(ddt_ref)

        for h in range(GDN_HEADS):
            lo, hi = h * GDN_DH, (h + 1) * GDN_DH
            beta = _sigmoid(b_ref[:, lo:hi])
            pb = jnp.sum(dbe_ref[:, lo:hi], axis=-1, keepdims=True) * (1.0 / GDN_DH)
            db_ref[:, lo:hi] = (pb * beta * (1.0 - beta)).astype(BF16)
            xa = a_ref[:, lo:hi] + dt_ref[:, lo:hi]
            ea = -jnp.exp(al_ref[:, lo:hi])
            pg = jnp.sum(dg_ref[:, lo:hi], axis=-1, keepdims=True) * (1.0 / GDN_DH)
            da = pg * ea * _sigmoid(xa)
            da_ref[:, lo:hi] = da.astype(BF16)
            dal_ref[:, lo:hi] += jnp.sum(pg * ea * _softplus(xa), axis=0, keepdims=True)
            ddt_ref[:, lo:hi] += jnp.sum(da, axis=0, keepdims=True)

    return _row_call(body, name, s,
                     [(projx, w, 4), (projx, w, 5), (alog, w, 0), (dtb, w, 0), (dbeta, w, 0), (dg, w, 0)],
                     [(_sds((s, w), BF16), w, 0), (_sds((s, w), BF16), w, 0),
                      (_sds((1, w), F32), w, 0), (_sds((1, w), F32), w, 0)], acc=True)


def _gdn_tri():
    c = GDN_CHUNK
    i = lax.broadcasted_iota(jnp.int32, (c, c), 0)
    j = lax.broadcasted_iota(jnp.int32, (c, c), 1)
    return ((i >= j).astype(F32), (i <= j).astype(F32), i >= j, i > j, (i == j).astype(F32))


def _gdn_chunk(q, k, v, bb, gb, tri):
    low, up, incl, strict, eye = tri
    c = GDN_CHUNK
    gc = _dot(low, gb, precision=HI)
    gcr = _dot(gb, up, 0, 0, precision=HI)
    gdiff = gc[:, :c] - gcr[:c, :]
    decay = jnp.where(incl, jnp.exp(jnp.where(incl, gdiff, 0.0)), 0.0)
    kb, vb = k * bb, v * bb
    kbk = _mxu(kb, k, 1, 1)
    x = -jnp.where(strict, kbk * decay, 0.0)
    t = eye + x
    p = x
    for _ in range(c.bit_length() - 2):
        p = _dot(p, p, precision=HI)
        t = t + _dot(t, p, precision=HI)
    eg = jnp.exp(gc)
    kbg = kb * eg
    gcl = gc[c - 1:c, :]
    ek = jnp.exp(gcl - gc)
    qkraw = _mxu(q, k, 1, 1)
    return dict(gc=gc, decay=decay, kb=kb, vb=vb, kbk=kbk, t=t, eg=eg, kbg=kbg, ek=ek, gl=jnp.exp(gcl),
                w=_mxu(t, kbg), u=_mxu(t, vb), qkraw=qkraw, qk=jnp.where(incl, qkraw * decay, 0.0),
                qd=q * eg, kd=k * ek)


def _gdn_specs(n_of):
    c, w = GDN_CHUNK, GDN_HEADS * GDN_DH

    def blk(cb):
        return pl.BlockSpec((c, w), lambda n: (n_of(n), cb))

    st = pl.BlockSpec((None, GDN_HEADS, GDN_DH, GDN_DH), lambda n: (n_of(n), 0, 0, 0))
    vec = pl.BlockSpec((1, GDN_DH), lambda n: (0, 0))
    return blk, st, vec


def _gdn_fwd(qkv, beta, g, projx, onorm, name):
    s = qkv.shape[0]
    nc = s // GDN_CHUNK
    w = GDN_HEADS * GDN_DH
    blk, st, vec = _gdn_specs(lambda n: n)

    def body(q_ref, k_ref, v_ref, b_ref, g_ref, z_ref, on_ref, o_ref, st_ref, state):
        @pl.when(pl.program_id(0) == 0)
        def _():
            state[...] = jnp.zeros_like(state)

        tri = _gdn_tri()
        for h in range(GDN_HEADS):
            lo, hi = h * GDN_DH, (h + 1) * GDN_DH
            ch = _gdn_chunk(q_ref[:, lo:hi], k_ref[:, lo:hi], v_ref[:, lo:hi], b_ref[:, lo:hi], g_ref[:, lo:hi], tri)
            sp = state[h]
            st_ref[h] = sp
            vn = ch["u"] - _mxu(ch["w"], sp)
            o = _mxu(ch["qd"], sp) + _mxu(ch["qk"], vn)
            state[h] = sp * ch["gl"] + _mxu(ch["kd"], vn, 0, 0)
            r = lax.rsqrt(jnp.mean(o * o, axis=-1, keepdims=True) + EPS)
            z = z_ref[:, lo:hi]
            o_ref[:, lo:hi] = (o * r * on_ref[...] * (z * _sigmoid(z))).astype(BF16)

    return pl.pallas_call(
        body, name=name, grid=(nc,),
        in_specs=[blk(0), blk(1), blk(2), blk(0), blk(0), blk(3), vec], out_specs=[blk(0), st],
        out_shape=[jax.ShapeDtypeStruct((s, w), BF16), jax.ShapeDtypeStruct((nc, GDN_HEADS, GDN_DH, GDN_DH), F32)],
        scratch_shapes=[pltpu.VMEM((GDN_HEADS, GDN_DH, GDN_DH), F32)],
        compiler_params=_cp("arbitrary"),
    )(qkv, qkv, qkv, beta, g, projx, onorm.reshape(1, -1))


def _gdn_bwd(qkv, beta, g, projx, onorm, states, dout, name):
    s = qkv.shape[0]
    c = GDN_CHUNK
    nc = s // c
    w = GDN_HEADS * GDN_DH
    blk, st, vec = _gdn_specs(lambda n: nc - 1 - n)

    def body(q_ref, k_ref, v_ref, b_ref, g_ref, z_ref, on_ref, st_ref, do_ref,
             dq_ref, dk_ref, dv_ref, db_ref, dg_ref, dz_ref, don_ref, carry):
        @pl.when(pl.program_id(0) == 0)
        def _():
            carry[...] = jnp.zeros_like(carry)
            don_ref[...] = jnp.zeros_like(don_ref)

        tri = _gdn_tri()
        low, up, incl, strict, eye = tri
        row = lax.broadcasted_iota(jnp.int32, (c, GDN_DH), 0)
        zpad = jnp.zeros((c, GDN_DH - c), F32)
        for h in range(GDN_HEADS):
            lo, hi = h * GDN_DH, (h + 1) * GDN_DH
            q, k, v, bb = q_ref[:, lo:hi], k_ref[:, lo:hi], v_ref[:, lo:hi], b_ref[:, lo:hi]
            ch = _gdn_chunk(q, k, v, bb, g_ref[:, lo:hi], tri)
            sp = st_ref[h]
            vn = ch["u"] - _mxu(ch["w"], sp)
            o = _mxu(ch["qd"], sp) + _mxu(ch["qk"], vn)
            r = lax.rsqrt(jnp.mean(o * o, axis=-1, keepdims=True) + EPS)
            orn = o * r
            z = z_ref[:, lo:hi]
            sg = _sigmoid(z)
            dout_h = do_ref[:, lo:hi]
            onw = on_ref[...]
            dz_ref[:, lo:hi] = (dout_h * orn * onw * (sg * (1.0 + z * (1.0 - sg)))).astype(BF16)
            don = dout_h * (z * sg)
            don_ref[...] += jnp.sum(don * orn, axis=0, keepdims=True)
            dor = don * onw
            do = r * (dor - orn * jnp.mean(dor * orn, axis=-1, keepdims=True))
            dsn = carry[h]
            dqd = _mxu(do, sp, 1, 1)
            dqk = jnp.where(incl, _mxu(do, vn, 1, 1), 0.0)
            dvn = _mxu(ch["qk"], do, 0, 0) + _mxu(ch["kd"], dsn)
            dkd = _mxu(vn, dsn, 1, 1)
            dgl = jnp.sum(dsn * sp, axis=0, keepdims=True)
            dw = -_mxu(dvn, sp, 1, 1)
            carry[h] = _mxu(ch["qd"], do, 0, 0) + dsn * ch["gl"] - _mxu(ch["w"], dvn, 0, 0)
            t = ch["t"]
            dvb = _mxu(t, dvn, 0, 0)
            dkbg = _mxu(t, dw, 0, 0)
            dt = _mxu(dvn, ch["vb"], 1, 1) + _mxu(dw, ch["kbg"], 1, 1)
            da = -_dot(_dot(t, dt, 0, 0, precision=HI), t, 1, 1, precision=HI)
            da = jnp.where(strict, da, 0.0)
            decay = ch["decay"]
            dkbk = da * decay
            dqkr = dqk * decay
            mdec = (da * ch["kbk"] + dqk * ch["qkraw"]) * decay
            dkb = _mxu(dkbk, k) + dkbg * ch["eg"]
            dk = _mxu(dkbk, ch["kb"], 0, 0) + _mxu(dqkr, q, 0, 0) + dkd * ch["ek"] + dkb * bb
            dq = _mxu(dqkr, k) + dqd * ch["eg"]
            tk = dkd * ch["kd"]
            dgcl = jnp.sum(tk, axis=0, keepdims=True) + dgl * ch["gl"]
            dgc = (jnp.concatenate([mdec, zpad], axis=1) - jnp.concatenate([mdec.T, zpad], axis=1)
                   + dqd * ch["qd"] - tk + dkbg * ch["kbg"] + jnp.where(row == c - 1, dgcl, 0.0))
            dq_ref[:, lo:hi] = dq
            dk_ref[:, lo:hi] = dk
            dv_ref[:, lo:hi] = dvb * bb
            db_ref[:, lo:hi] = dkb * k + dvb * v
            dg_ref[:, lo:hi] = _dot(up, dgc, precision=HI)

    return pl.pallas_call(
        body, name=name, grid=(nc,),
        in_specs=[blk(0), blk(1), blk(2), blk(0), blk(0), blk(3), vec, st, blk(0)],
        out_specs=[blk(0)] * 6 + [vec],
        out_shape=[jax.ShapeDtypeStruct((s, w), F32)] * 5 + [jax.ShapeDtypeStruct((s, w), BF16),
                                                              jax.ShapeDtypeStruct((1, GDN_DH), F32)],
        scratch_shapes=[pltpu.VMEM((GDN_HEADS, GDN_DH, GDN_DH), F32)],
        compiler_params=_cp("arbitrary"),
    )(qkv, qkv, qkv, beta, g, projx, onorm.reshape(1, -1), states, dout)


_WEIGHTS = (
    "l0_mix_norm", "l0_w_in", "l0_ret_norm", "l0_s5_lambda_re", "l0_s5_lambda_im", "l0_s5_b_re", "l0_s5_b_im",
    "l0_s5_c_re", "l0_s5_c_im", "l0_s5_d", "l0_s5_log_dt", "l0_s5_w_glu", "l0_s5_b_glu", "l0_w_out",
    "l0_xa_norm", "l0_mem_norm", "l0_xa_wq", "l0_xa_wkv", "l0_xa_wo", "l0_ffn_norm", "l0_ffn_w_up",
    "l0_ffn_conv", "l0_ffn_w_down", "l1_mix_norm", "l1_w_in", "l1_conv", "l1_a_log", "l1_dt_bias", "l1_o_norm",
    "l1_w_out", "l1_xa_norm", "l1_mem_norm", "l1_xa_wq", "l1_xa_wkv", "l1_xa_wo", "l1_ffn_norm", "l1_ffn_w_up",
    "l1_ffn_conv", "l1_ffn_w_down", "final_norm")
_INPUTS = ("x", "mem") + _WEIGHTS + ("loss_target",) + tuple("m_" + n for n in _WEIGHTS) + tuple("v_" + n for n in _WEIGHTS)

_COL = ("l0_w_in", "l0_xa_wkv", "l0_ffn_w_up", "l1_w_in", "l1_xa_wkv", "l1_ffn_w_up")
_ROW = ("l0_w_out", "l0_xa_wq", "l0_xa_wo", "l0_ffn_w_down", "l1_w_out", "l1_xa_wq", "l1_xa_wo", "l1_ffn_w_down",
        "l0_s5_w_glu")
_CONV = ("l0_ffn_conv", "l1_conv", "l1_ffn_conv")
_REP = tuple(n for n in _WEIGHTS if n not in _COL + _ROW + _CONV)
_CONV_ROWS = 8


def _round_up(n, m):
    return (n + m - 1) // m * m


def _pack_col(ts):
    a = jnp.concatenate(ts, axis=-1)
    return jnp.pad(a, [(0, 0)] * (a.ndim - 1) + [(0, _round_up(a.shape[-1], LANES) - a.shape[-1])])


def _pack_row(ts):
    ts = list(ts[:-1]) + [ts[-1].reshape(ts[-1].shape[:-2] + (ts[-1].shape[-2] // 2, 2 * ts[-1].shape[-1]))]
    return jnp.concatenate(ts, axis=-2)


def _pack_conv(ts):
    ts = [jnp.pad(t, [(0, 0)] * (t.ndim - 2) + [(0, _CONV_ROWS - t.shape[-2]), (0, 0)]) for t in ts]
    return _pack_col(ts)


def _pack_rep(ts):
    a = jnp.concatenate([t.reshape(-1) for t in ts])
    return jnp.pad(a, (0, _round_up(a.shape[0], 8 * LANES) - a.shape[0])).reshape(-1, LANES)


def _unpack(packed, shapes, axis):
    out, off = [], 0
    for n in shapes:
        out.append(lax.slice_in_dim(packed, off, off + n, axis=axis))
        off += n
    return out


def _s5_interleave(re, im):
    lead = re.shape[:-1]
    nt = re.shape[-1] // S5_TILE
    both = jnp.stack([re.reshape(lead + (nt, S5_TILE)), im.reshape(lead + (nt, S5_TILE))], axis=-2)
    return both.reshape(lead + (2 * re.shape[-1],))


def _s5_split(x):
    lead = x.shape[:-1]
    y = x.reshape(lead + (x.shape[-1] // (2 * S5_TILE), 2, S5_TILE))
    return y[..., 0, :].reshape(lead + (-1,)), y[..., 1, :].reshape(lead + (-1,))


def _s5_discretise(lr, li, log_dt, b_re, b_im):
    dt = jnp.exp(log_dt)[:, None]
    mag = jnp.exp(lr * dt)
    a_re = mag * jnp.cos(li * dt)
    a_im = mag * jnp.sin(li * dt)
    den = lr * lr + li * li
    z_re = ((a_re - 1.0) * lr + a_im * li) / den
    z_im = (a_im * lr - (a_re - 1.0) * li) / den
    bb_re = z_re[:, None, :] * b_re - z_im[:, None, :] * b_im
    bb_im = z_re[:, None, :] * b_im + z_im[:, None, :] * b_re
    return a_re, a_im, bb_re, bb_im


def _pow_table(a_re, a_im, rows):
    tr, ti = a_re[None], a_im[None]
    while tr.shape[0] < rows:
        lr, li = tr[-1:], ti[-1:]
        tr, ti = (jnp.concatenate([tr, tr * lr - ti * li]), jnp.concatenate([ti, tr * li + ti * lr]))
    return tr, ti


def _block_diag(b):
    g, r, c = b.shape
    return jnp.einsum("grc,gk->grkc", b, jnp.eye(g, dtype=b.dtype)).reshape(g * r, g * c)


def _block_diag_of(d, g):
    r, c = d.shape[0] // g, d.shape[1] // g
    return jnp.einsum("grkc,gk->grc", d.reshape(g, r, g, c), jnp.eye(g, dtype=d.dtype))


def kernel(*args):
    p = dict(zip(_INPUTS, args, strict=True))
    x0, mem0, tgt = p["x"][0], p["mem"][0], p["loss_target"][0]
    s, d = x0.shape
    grads = {}

    col_w = [p[n].shape[1] for n in _COL]
    row_h = [p[n].shape[0] for n in _ROW[:-1]] + [p["l0_s5_w_glu"].shape[0] // 2]
    conv_w = [p[n].shape[1] for n in _CONV]
    ga, gb, gc = _all_gather([_pack_col([p[n] for n in _COL]).astype(BF16),
                              _pack_row([p[n] for n in _ROW]).astype(BF16),
                              _pack_conv([p[n] for n in _CONV])], "gather_weights")
    w = {}
    for n, t in zip(_COL, _unpack(ga, col_w, 2)):
        w[n] = t.transpose(1, 0, 2).reshape(t.shape[1], -1)
    for n, t in zip(_ROW, _unpack(gb, row_h, 1)):
        w[n] = t.reshape(-1, t.shape[2])
    w["l0_s5_w_glu"] = w["l0_s5_w_glu"].reshape(N_DEV * p["l0_s5_w_glu"].shape[0], -1)
    for n, t in zip(_CONV, _unpack(gc, conv_w, 2)):
        w[n] = t[:, :p[n].shape[0], :].transpose(1, 0, 2).reshape(p[n].shape[0], -1)

    def xattn(pre, x_in):
        hx = _norm_fwd(x_in, p[pre + "xa_norm"], pre + "xa_norm_fwd")
        q = _mm(hx, w[pre + "xa_wq"], out_dtype=BF16, name=pre + "xa_q")
        memn = _norm_fwd(mem0, p[pre + "mem_norm"], pre + "mem_norm_fwd")
        kv = _mm(memn, w[pre + "xa_wkv"], out_dtype=BF16, name=pre + "xa_kv")
        ao = _xattn_fwd(q, kv, pre + "xattn_fwd")
        x_out = _mm(ao, w[pre + "xa_wo"], res=x_in, name=pre + "xa_o")
        return x_out, (x_in, hx, q, memn, kv, ao)

    def xattn_bwd(pre, saved, dxo):
        x_in, hx, q, memn, kv, ao = saved
        dao = _mm(dxo, w[pre + "xa_wo"], tb=True, name=pre + "xa_o_dx")
        grads[pre + "xa_wo"] = _mm(ao, dxo, ta=True, name=pre + "xa_o_dw")
        dq, dkv = _xattn_bwd(q, kv, dao, pre + "xattn_bwd")
        grads[pre + "xa_wq"] = _mm(hx, dq, ta=True, name=pre + "xa_q_dw")
        dhx = _mm(dq, w[pre + "xa_wq"], tb=True, name=pre + "xa_q_dx")
        grads[pre + "xa_wkv"] = _mm(memn, dkv, ta=True, name=pre + "xa_kv_dw")
        dmemn = _mm(dkv, w[pre + "xa_wkv"], tb=True, name=pre + "xa_kv_dx")
        dx_in, grads[pre + "xa_norm"] = _norm_bwd(x_in, p[pre + "xa_norm"], dhx, dxo, pre + "xa_norm_bwd")
        _, grads[pre + "mem_norm"] = _norm_bwd(mem0, p[pre + "mem_norm"], dmemn, jnp.zeros_like(mem0), pre + "mem_norm_bwd")
        return dx_in

    def ffn(pre, x_in):
        hf = _norm_fwd(x_in, p[pre + "ffn_norm"], pre + "ffn_norm_fwd")
        up = _mm(hf, w[pre + "ffn_w_up"], name=pre + "ffn_up")
        act = _ffn_act_fwd(up, w[pre + "ffn_conv"], pre + "ffn_act_fwd")
        x_out = _mm(act, w[pre + "ffn_w_down"], res=x_in, name=pre + "ffn_down")
        return x_out, (x_in, hf, up, act)

    def ffn_bwd(pre, saved, dxo):
        x_in, hf, up, act = saved
        dact = _mm(dxo, w[pre + "ffn_w_down"], tb=True, name=pre + "ffn_down_dx")
        grads[pre + "ffn_w_down"] = _mm(act, dxo, ta=True, name=pre + "ffn_down_dw")
        dpu, dpg, dcu, dcg = _ffn_act_bwd(up, w[pre + "ffn_conv"], dact, pre + "ffn_act_bwd")
        dup = jnp.concatenate([dpu, dpg], axis=1)
        grads[pre + "ffn_conv"] = jnp.concatenate([dcu, dcg], axis=1)
        dhf = _mm(dup, w[pre + "ffn_w_up"], tb=True, name=pre + "ffn_up_dx")
        grads[pre + "ffn_w_up"] = _mm(hf, dup, ta=True, name=pre + "ffn_up_dw")
        dx_in, grads[pre + "ffn_norm"] = _norm_bwd(x_in, p[pre + "ffn_norm"], dhf, dxo, pre + "ffn_norm_bwd")
        return dx_in

    cos, sin = _rope_tables(s)
    (a_re, a_im, bb_re, bb_im), disc_vjp = jax.vjp(
        _s5_discretise, p["l0_s5_lambda_re"], p["l0_s5_lambda_im"], p["l0_s5_log_dt"], p["l0_s5_b_re"], p["l0_s5_b_im"])
    pw_re, pw_im = _pow_table(a_re.reshape(-1), a_im.reshape(-1), SCAN_ROWS)
    apow = _s5_interleave(pw_re, pw_im)
    apow_rev = _s5_interleave(pw_re[::-1], -pw_im[::-1])
    bbig = _s5_interleave(_block_diag(bb_re), _block_diag(bb_im)).astype(BF16)
    cbig = _s5_interleave(_block_diag(p["l0_s5_c_re"]).T, -_block_diag(p["l0_s5_c_im"]).T).T.astype(BF16)
    s5_d = p["l0_s5_d"].reshape(1, -1)
    b_glu = p["l0_s5_b_glu"].reshape(1, -1)

    h0 = _norm_fwd(x0, p["l0_mix_norm"], "l0_mix_norm_fwd")
    proj = _mm(h0, w["l0_w_in"], name="l0_in")
    o_ret, ret_states = _ret_fwd(proj, cos, sin, p["l0_ret_norm"], "l0_ret_fwd")
    u = proj[:, 4 * RET_HEADS * RET_DH:]
    bu = _mm(u, bbig, name="l0_s5_bu")
    st = _s5_scan_fwd(bu, apow, "l0_s5_scan_fwd")
    yraw = _mm(st, cbig, name="l0_s5_c")
    y, gy = _s5_gelu_fwd(yraw, proj, s5_d, "l0_s5_gelu_fwd")
    z = _mm(gy, w["l0_s5_w_glu"], name="l0_s5_glu_mm")
    y2 = _s5_glu_fwd(y, z, b_glu, "l0_s5_glu_fwd")
    merged = jnp.concatenate([o_ret, y2], axis=1)
    x1 = _mm(merged, w["l0_w_out"], res=x0, name="l0_out")
    x2, xa0 = xattn("l0_", x1)
    x3, ff0 = ffn("l0_", x2)

    nqkv = 4 * GDN_HEADS * GDN_DH
    w1 = w["l1_w_in"]
    wx = jnp.concatenate([w1[:, :nqkv], jnp.repeat(w1[:, nqkv:nqkv + GDN_HEADS], GDN_DH, axis=1),
                          jnp.repeat(w1[:, nqkv + GDN_HEADS:], GDN_DH, axis=1)], axis=1)
    alog_x = jnp.repeat(p["l1_a_log"], GDN_DH).reshape(1, -1)
    dtb_x = jnp.repeat(p["l1_dt_bias"], GDN_DH).reshape(1, -1)
    h1 = _norm_fwd(x3, p["l1_mix_norm"], "l1_mix_norm_fwd")
    projx = _mm(h1, wx, name="l1_in")
    qkv = _gdn_conv_fwd(projx, w["l1_conv"], "l1_conv_fwd")
    beta, glog = _gdn_gates_fwd(projx, alog_x, dtb_x, "l1_gates_fwd")
    o_gdn, gdn_states = _gdn_fwd(qkv, beta, glog, projx, p["l1_o_norm"], "l1_gdn_fwd")
    x4 = _mm(o_gdn, w["l1_w_out"], res=x3, name="l1_out")
    x5, xa1 = xattn("l1_", x4)
    x6, ff1 = ffn("l1_", x5)

    loss_part, dx6, grads["final_norm"] = _loss_head(x6, p["final_norm"], tgt, "loss_head")
    loss = lax.psum(loss_part[0, 0], ("x", "y", "c"))
    dx5 = ffn_bwd("l1_", ff1, dx6)
    dx4 = xattn_bwd("l1_", xa1, dx5)

    do_gdn = _mm(dx4, w["l1_w_out"], tb=True, name="l1_out_dx")
    grads["l1_w_out"] = _mm(o_gdn, dx4, ta=True, name="l1_out_dw")
    dq, dk, dv, dbeta, dglog, dz, grads["l1_o_norm"] = _gdn_bwd(
        qkv, beta, glog, projx, p["l1_o_norm"], gdn_states, do_gdn, "l1_gdn_bwd")
    dpre, grads["l1_conv"] = _gdn_conv_bwd(projx, w["l1_conv"], jnp.concatenate([dq, dk, dv], axis=1), "l1_conv_bwd")
    db, da, dalog_x, ddtb_x = _gdn_gates_bwd(projx, alog_x, dtb_x, dbeta, dglog, "l1_gates_bwd")
    dprojx = jnp.concatenate([dpre, dz, db, da], axis=1)
    dh1 = _mm(dprojx, wx, tb=True, name="l1_in_dx")
    dwx = _mm(h1, dprojx, ta=True, name="l1_in_dw")
    grads["l1_w_in"] = jnp.concatenate(
        [dwx[:, :nqkv], dwx[:, nqkv:nqkv + GDN_HEADS * GDN_DH].reshape(d, GDN_HEADS, GDN_DH).sum(-1),
         dwx[:, nqkv + GDN_HEADS * GDN_DH:].reshape(d, GDN_HEADS, GDN_DH).sum(-1)], axis=1)
    grads["l1_a_log"] = dalog_x.reshape(GDN_HEADS, GDN_DH).sum(-1)
    grads["l1_dt_bias"] = ddtb_x.reshape(GDN_HEADS, GDN_DH).sum(-1)
    dx3, grads["l1_mix_norm"] = _norm_bwd(x3, p["l1_mix_norm"], dh1, dx4, "l1_mix_norm_bwd")

    dx2 = ffn_bwd("l0_", ff0, dx3)
    dx1 = xattn_bwd("l0_", xa0, dx2)

    dmerged = _mm(dx1, w["l0_w_out"], tb=True, name="l0_out_dx")
    grads["l0_w_out"] = _mm(merged, dx1, ta=True, name="l0_out_dw")
    drq, drk, drv, drg, grads["l0_ret_norm"] = _ret_bwd(proj, cos, sin, p["l0_ret_norm"], ret_states, dmerged, "l0_ret_bwd")
    dzg, dg1, grads["l0_s5_b_glu"] = _s5_glu_bwd(dmerged, y, z, b_glu, "l0_s5_glu_bwd")
    grads["l0_s5_w_glu"] = _mm(gy, dzg, ta=True, name="l0_s5_glu_dw")
    dg2 = _mm(dzg, w["l0_s5_w_glu"], tb=True, name="l0_s5_glu_dx")
    dyraw, du_dir, grads["l0_s5_d"] = _s5_gelu_bwd(dg1, dg2, y, proj, s5_d, "l0_s5_gelu_bwd")
    dst = _mm(dyraw, cbig, tb=True, name="l0_s5_c_dx")
    dcbig = _mm(st, dyraw, ta=True, name="l0_s5_c_dw")
    gsc, da_s5 = _s5_scan_bwd(dst, apow_rev, st, "l0_s5_scan_bwd")
    du = _mm(gsc, bbig, tb=True, res=du_dir, out_dtype=BF16, name="l0_s5_bu_dx")
    dbbig = _mm(u, gsc, ta=True, name="l0_s5_bu_dw")
    dproj = jnp.concatenate([drq, drk, drv, drg, du], axis=1)
    dh0 = _mm(dproj, w["l0_w_in"], tb=True, name="l0_in_dx")
    grads["l0_w_in"] = _mm(h0, dproj, ta=True, name="l0_in_dw")
    dx0, grads["l0_mix_norm"] = _norm_bwd(x0, p["l0_mix_norm"], dh0, dx1, "l0_mix_norm_bwd")

    dbb_re, dbb_im = (_block_diag_of(t, S5_GROUPS) for t in _s5_split(dbbig))
    dct_re, dct_im = _s5_split(dcbig.T)
    grads["l0_s5_c_re"] = _block_diag_of(dct_re.T, S5_GROUPS)
    grads["l0_s5_c_im"] = -_block_diag_of(dct_im.T, S5_GROUPS)
    da_re, da_im = (t.reshape(S5_GROUPS, S5_STATE) for t in _s5_split(da_s5[0]))
    (grads["l0_s5_lambda_re"], grads["l0_s5_lambda_im"], grads["l0_s5_log_dt"], grads["l0_s5_b_re"],
     grads["l0_s5_b_im"]) = disc_vjp((da_re, da_im, dbb_re, dbb_im))

    def slots_col(g):
        return g.reshape(g.shape[0], N_DEV, -1).transpose(1, 0, 2)

    def slots_row(g):
        return g.reshape(N_DEV, -1, g.shape[1])

    pa = _pack_col([slots_col(grads[n]) for n in _COL]).astype(BF16)
    pb = _pack_row([slots_row(grads[n]) for n in _ROW]).astype(BF16)
    pc = _pack_conv([slots_col(grads[n]) for n in _CONV])
    pr = _pack_rep([grads[n].reshape(p[n].shape) for n in _REP])
    ra, rb, rc, rr = _exchange([pa, pb, pc], [pr], "exchange_grads")

    outs = {}
    groups = ((_COL, _pack_col, ra, col_w, 1, "adamw_col"), (_ROW, _pack_row, rb, row_h, 0, "adamw_row"),
              (_CONV, _pack_conv, rc, conv_w, 1, "adamw_conv"))
    for names, pack, parts, sizes, axis, nm in groups:
        res = _adamw(parts, *(pack([p[pre + n] for n in names]) for pre in ("", "m_", "v_")), nm)
        for kind, packed in zip(("grad_", "delta_", "new_m_", "new_v_"), res):
            for n, t in zip(names, _unpack(packed, sizes, axis)):
                outs[kind + n] = t[:p[n].shape[0]].reshape(p[n].shape)
    res = _adamw(rr, *(_pack_rep([p[pre + n] for n in _REP]) for pre in ("", "m_", "v_")), "adamw_rep")
    rep_n = [math.prod(p[n].shape) for n in _REP]
    for kind, packed in zip(("grad_", "delta_", "new_m_", "new_v_"), res):
        for n, t in zip(_REP, _unpack(packed.reshape(-1), rep_n, 0)):
            outs[kind + n] = t.reshape(p[n].shape)

    return (loss, dx0[None]) + tuple(outs[kind + n] for kind in ("grad_", "delta_", "new_m_", "new_v_") for n in _WEIGHTS)
```

```python
import functools
import math

import numpy as np
import jax
import jax.numpy as jnp
from jax import lax
from jax.experimental import pallas as pl
from jax.experimental.pallas import tpu as pltpu

F32 = jnp.float32
BF16 = jnp.bfloat16
EPS = 1e-6
N_DEV = 8
LANES = 128
VMEM_LIMIT = 48 * 1024 * 1024
HI = lax.Precision.HIGHEST

RET_HEADS, RET_DH, RET_CHUNK = 4, 128, 128
S5_GROUPS, S5_GROUP, S5_STATE = 32, 16, 64
GDN_HEADS, GDN_DH, GDN_CHUNK, GDN_CONV = 8, 128, 64, 4
XA_HEADS, XA_DH = 4, 256
FFN_CONV = 3
SCAN_ROWS = 256

ADAM_LR, ADAM_B1, ADAM_B2, ADAM_EPS, ADAM_WD, ADAM_STEP = 0.001, 0.9, 0.999, 1e-08, 0.01, 10


def _cp(*sem):
    return pltpu.CompilerParams(dimension_semantics=sem if sem else None, vmem_limit_bytes=VMEM_LIMIT)


def _tile(n, cap):
    if n <= cap:
        return n
    best = None
    for t in range(LANES, cap + 1, LANES):
        if n % t == 0:
            best = t
    assert best is not None, n
    return best


def _dot(a, b, ca=1, cb=0, precision=None):
    return lax.dot_general(a, b, (((ca,), (cb,)), ((), ())), precision=precision, preferred_element_type=F32)


def _mxu(a, b, ca=1, cb=0):
    return _dot(a.astype(BF16), b.astype(BF16), ca, cb)


def _sigmoid(x):
    return 1.0 / (1.0 + jnp.exp(-x))


def _shift_down(x, k):
    row = lax.broadcasted_iota(jnp.int32, x.shape, 0)
    return jnp.where(row >= k, pltpu.roll(x, k, 0), 0.0)


def _shift_up(x, k):
    n = x.shape[0]
    row = lax.broadcasted_iota(jnp.int32, x.shape, 0)
    return jnp.where(row < n - k, pltpu.roll(x, n - k, 0), 0.0)


def _mesh_pos():
    return lax.axis_index("x"), lax.axis_index("y"), lax.axis_index("c")


def _slot(px, py, pc):
    return 4 * px + 2 * py + pc


def _all_gather(arrs, name):
    n = len(arrs)
    hbm = pl.BlockSpec(memory_space=pl.ANY)

    def body(*refs):
        ins, outs = refs[:n], refs[n:2 * n]
        send_sems, recv_sems, local_sems = refs[2 * n:]
        x, y, c = _mesh_pos()
        me, sibling = (x, y, c), (x, y, 1 - c)
        chips = [(1 - x, y), (x, 1 - y), (1 - x, 1 - y)]

        def copy(a, k, block, to, src=None):
            dst = outs[a].at[_slot(*block)]
            return pltpu.make_async_remote_copy(
                src_ref=dst if src is None else src, dst_ref=dst,
                send_sem=send_sems.at[a, k], recv_sem=recv_sems.at[a, k],
                device_id=to, device_id_type=pl.DeviceIdType.MESH)

        mine = [pltpu.make_async_copy(ins[a], outs[a].at[_slot(*me)], local_sems.at[a]) for a in range(n)]
        for cp in mine:
            cp.start()
        first = []
        for a in range(n):
            first.append(copy(a, 0, me, sibling, src=ins[a]))
            first += [copy(a, 1 + j, me, (*chip, c), src=ins[a]) for j, chip in enumerate(chips)]
        for cp in first:
            cp.start()
        passed = []
        for j, chip in enumerate(chips):
            for a in range(n):
                copy(a, 1 + j, (*chip, c), me).wait_recv()
                fw = copy(a, 4 + j, (*chip, c), sibling)
                fw.start()
                passed.append(fw)
        for a in range(n):
            copy(a, 0, sibling, me).wait_recv()
            for j, chip in enumerate(chips):
                copy(a, 4 + j, (*chip, 1 - c), me).wait_recv()
        for cp in first + passed:
            cp.wait_send()
        for cp in mine:
            cp.wait()

    return pl.pallas_call(
        body, name=name,
        out_shape=[jax.ShapeDtypeStruct((N_DEV,) + a.shape, a.dtype) for a in arrs],
        in_specs=[hbm] * n, out_specs=[hbm] * n,
        scratch_shapes=[pltpu.SemaphoreType.DMA((n, 7)), pltpu.SemaphoreType.DMA((n, 7)),
                        pltpu.SemaphoreType.DMA((n,))],
    )(*arrs)


def _exchange(scatter, gather, name):
    ns, ng = len(scatter), len(gather)
    n = ns + ng
    hbm = pl.BlockSpec(memory_space=pl.ANY)

    def body(*refs):
        ins, outs = refs[:n], refs[n:2 * n]
        send_sems, recv_sems, local_sems = refs[2 * n:]
        x, y, c = _mesh_pos()
        me = _slot(x, y, c)
        flips = [(fx, fy, fc) for fx in (0, 1) for fy in (0, 1) for fc in (0, 1)][1:]

        def peer(f):
            return (1 - x if f[0] else x, 1 - y if f[1] else y, 1 - c if f[2] else c)

        def copy(a, k, to):
            src = ins[a].at[_slot(*to)] if a < ns else ins[a]
            return pltpu.make_async_remote_copy(
                src_ref=src, dst_ref=outs[a].at[me],
                send_sem=send_sems.at[a, k], recv_sem=recv_sems.at[a, k],
                device_id=to, device_id_type=pl.DeviceIdType.MESH)

        def arrival(a, k, frm):
            dst = outs[a].at[_slot(*frm)]
            return pltpu.make_async_remote_copy(
                src_ref=dst, dst_ref=dst, send_sem=send_sems.at[a, k], recv_sem=recv_sems.at[a, k],
                device_id=frm, device_id_type=pl.DeviceIdType.MESH)

        mine = []
        for a in range(n):
            src = ins[a].at[me] if a < ns else ins[a]
            mine.append(pltpu.make_async_copy(src, outs[a].at[me], local_sems.at[a]))
        for cp in mine:
            cp.start()
        sends = [copy(a, k, peer(f)) for k, f in enumerate(flips) for a in range(n)]
        for cp in sends:
            cp.start()
        for k, f in enumerate(flips):
            for a in range(n):
                arrival(a, k, peer(f)).wait_recv()
        for cp in sends:
            cp.wait_send()
        for cp in mine:
            cp.wait()

    out_shape = [jax.ShapeDtypeStruct(a.shape, a.dtype) for a in scatter]
    out_shape += [jax.ShapeDtypeStruct((N_DEV,) + a.shape, a.dtype) for a in gather]
    return pl.pallas_call(
        body, name=name, out_shape=out_shape, in_specs=[hbm] * n, out_specs=[hbm] * n,
        scratch_shapes=[pltpu.SemaphoreType.DMA((n, 7)), pltpu.SemaphoreType.DMA((n, 7)),
                        pltpu.SemaphoreType.DMA((n,))],
    )(*scatter, *gather)


def _mm(a, b, *, ta=False, tb=False, out_dtype=F32, res=None, name="mm"):
    m, k = (a.shape[1], a.shape[0]) if ta else a.shape
    n = b.shape[0] if tb else b.shape[1]
    assert k == (b.shape[1] if tb else b.shape[0]), (a.shape, b.shape, ta, tb)
    tm, tn, tk = _tile(m, 1408), _tile(n, 512), _tile(k, 1408)
    nk = k // tk
    has_res = res is not None

    def body(*refs):
        a_ref, b_ref = refs[:2]
        r_ref = refs[2] if has_res else None
        o_ref = refs[3 if has_res else 2]
        part = _mxu(a_ref[...], b_ref[...], 0 if ta else 1, 1 if tb else 0)

        def finish(r):
            if has_res:
                r = r + r_ref[...].astype(F32)
            o_ref[...] = r.astype(out_dtype)

        if nk == 1:
            finish(part)
            return
        acc = refs[-1]
        kk = pl.program_id(2)

        @pl.when(kk == 0)
        def _():
            acc[...] = part

        @pl.when(kk > 0)
        def _():
            acc[...] += part

        @pl.when(kk == nk - 1)
        def _():
            finish(acc[...])

    a_spec = pl.BlockSpec((tk, tm), lambda i, j, kk: (kk, i)) if ta else pl.BlockSpec((tm, tk), lambda i, j, kk: (i, kk))
    b_spec = pl.BlockSpec((tn, tk), lambda i, j, kk: (j, kk)) if tb else pl.BlockSpec((tk, tn), lambda i, j, kk: (kk, j))
    o_spec = pl.BlockSpec((tm, tn), lambda i, j, kk: (i, j))
    in_specs = [a_spec, b_spec] + ([o_spec] if has_res else [])
    args = (a, b) + ((res,) if has_res else ())
    return pl.pallas_call(
        body, name=name, grid=(m // tm, n // tn, nk), in_specs=in_specs, out_specs=o_spec,
        out_shape=jax.ShapeDtypeStruct((m, n), out_dtype),
        scratch_shapes=[pltpu.VMEM((tm, tn), F32)] if nk > 1 else [],
        compiler_params=_cp("parallel", "parallel", "arbitrary"),
    )(*args)


def _norm_fwd(x, g, name):
    s, d = x.shape
    tr = min(512, s)

    def body(x_ref, g_ref, o_ref):
        xv = x_ref[...]
        r = lax.rsqrt(jnp.mean(xv * xv, axis=-1, keepdims=True) + EPS)
        o_ref[...] = (xv * r * g_ref[...]).astype(BF16)

    row = pl.BlockSpec((tr, d), lambda i: (i, 0))
    return pl.pallas_call(
        body, name=name, grid=(s // tr,), in_specs=[row, pl.BlockSpec((1, d), lambda i: (0, 0))],
        out_specs=row, out_shape=jax.ShapeDtypeStruct((s, d), BF16), compiler_params=_cp("parallel"),
    )(x, g.reshape(1, d))


def _norm_bwd(x, g, dh, dres, name):
    s, d = x.shape
    tr = min(512, s)

    def body(x_ref, g_ref, dh_ref, dres_ref, dx_ref, dg_ref):
        @pl.when(pl.program_id(0) == 0)
        def _():
            dg_ref[...] = jnp.zeros_like(dg_ref)

        xv = x_ref[...]
        r = lax.rsqrt(jnp.mean(xv * xv, axis=-1, keepdims=True) + EPS)
        xn = xv * r
        dhv = dh_ref[...].astype(F32)
        dg_ref[...] += jnp.sum(dhv * xn, axis=0, keepdims=True)
        dhg = dhv * g_ref[...]
        dx_ref[...] = dres_ref[...] + r * (dhg - xn * jnp.mean(dhg * xn, axis=-1, keepdims=True))

    row = pl.BlockSpec((tr, d), lambda i: (i, 0))
    vec = pl.BlockSpec((1, d), lambda i: (0, 0))
    return pl.pallas_call(
        body, name=name, grid=(s // tr,), in_specs=[row, vec, row, row], out_specs=[row, vec],
        out_shape=[jax.ShapeDtypeStruct((s, d), F32), jax.ShapeDtypeStruct((1, d), F32)],
        compiler_params=_cp("arbitrary"),
    )(x, g.reshape(1, d), dh, dres)


def _loss_head(x, g, tgt, name):
    s, d = x.shape
    tr = min(512, s)

    def body(x_ref, g_ref, t_ref, l_ref, dx_ref, dg_ref):
        @pl.when(pl.program_id(0) == 0)
        def _():
            dg_ref[...] = jnp.zeros_like(dg_ref)
            l_ref[...] = jnp.zeros_like(l_ref)

        xv = x_ref[...]
        r = lax.rsqrt(jnp.mean(xv * xv, axis=-1, keepdims=True) + EPS)
        xn = xv * r
        err = xn * g_ref[...] - t_ref[...]
        part = 0.5 * jnp.sum(jnp.mean(err * err, axis=-1, keepdims=True), axis=0, keepdims=True)
        l_ref[...] += jnp.broadcast_to(part, l_ref.shape)
        dy = err * (1.0 / d)
        dg_ref[...] += jnp.sum(dy * xn, axis=0, keepdims=True)
        dyg = dy * g_ref[...]
        dx_ref[...] = r * (dyg - xn * jnp.mean(dyg * xn, axis=-1, keepdims=True))

    row = pl.BlockSpec((tr, d), lambda i: (i, 0))
    vec = pl.BlockSpec((1, d), lambda i: (0, 0))
    return pl.pallas_call(
        body, name=name, grid=(s // tr,), in_specs=[row, vec, row],
        out_specs=[pl.BlockSpec((1, LANES), lambda i: (0, 0)), row, vec],
        out_shape=[jax.ShapeDtypeStruct((1, LANES), F32), jax.ShapeDtypeStruct((s, d), F32),
                   jax.ShapeDtypeStruct((1, d), F32)],
        compiler_params=_cp("arbitrary"),
    )(x, g.reshape(1, d), tgt)


def _adamw(parts, w, m, v, name):
    r, c = w.shape
    cap = max(8, 256 * 1024 // c)
    tr = max(t for t in range(8, min(r, cap) + 1, 8) if r % t == 0)
    bc1 = 1.0 - ADAM_B1 ** ADAM_STEP
    bc2 = 1.0 - ADAM_B2 ** ADAM_STEP

    def body(p_ref, w_ref, m_ref, v_ref, g_ref, d_ref, nm_ref, nv_ref):
        g = p_ref[0].astype(F32)
        for i in range(1, N_DEV):
            g = g + p_ref[i].astype(F32)
        mm = ADAM_B1 * m_ref[...] + (1.0 - ADAM_B1) * g
        vv = ADAM_B2 * v_ref[...] + (1.0 - ADAM_B2) * (g * g)
        g_ref[...] = g
        nm_ref[...] = mm
        nv_ref[...] = vv
        d_ref[...] = -ADAM_LR * ((mm / bc1) / (jnp.sqrt(vv / bc2) + ADAM_EPS) + ADAM_WD * w_ref[...])

    blk = pl.BlockSpec((tr, c), lambda i: (i, 0))
    return pl.pallas_call(
        body, name=name, grid=(r // tr,),
        in_specs=[pl.BlockSpec((N_DEV, tr, c), lambda i: (0, i, 0)), blk, blk, blk],
        out_specs=[blk] * 4, out_shape=[jax.ShapeDtypeStruct((r, c), F32)] * 4,
        compiler_params=_cp("parallel"),
    )(parts, w, m, v)


def _conv_fwd(x, w_ref, kw):
    acc = w_ref[kw - 1:kw, :] * x
    for j in range(kw - 1):
        acc = acc + w_ref[j:j + 1, :] * _shift_down(x, kw - 1 - j)
    return acc


def _conv_bwd(x, dy, w_ref, dw_ref, kw):
    dx = w_ref[kw - 1:kw, :] * dy
    dw_ref[kw - 1:kw, :] = jnp.sum(dy * x, axis=0, keepdims=True)
    for j in range(kw - 1):
        dx = dx + w_ref[j:j + 1, :] * _shift_up(dy, kw - 1 - j)
        dw_ref[j:j + 1, :] = jnp.sum(dy * _shift_down(x, kw - 1 - j), axis=0, keepdims=True)
    return dx


def _ffn_act_fwd(pre, cw, name):
    s, f2 = pre.shape
    nt = f2 // 2 // LANES

    def body(pu_ref, pg_ref, wu_ref, wg_ref, o_ref):
        up = _conv_fwd(pu_ref[...], wu_ref, FFN_CONV)
        gate = _conv_fwd(pg_ref[...], wg_ref, FFN_CONV)
        o_ref[...] = (gate * _sigmoid(gate) * up).astype(BF16)

    def col(rows, off):
        return pl.BlockSpec((rows, LANES), lambda j: (0, j + off))

    return pl.pallas_call(
        body, name=name, grid=(nt,),
        in_specs=[col(s, 0), col(s, nt), col(FFN_CONV, 0), col(FFN_CONV, nt)], out_specs=col(s, 0),
        out_shape=jax.ShapeDtypeStruct((s, f2 // 2), BF16), compiler_params=_cp("parallel"),
    )(pre, pre, cw, cw)


def _ffn_act_bwd(pre, cw, dact, name):
    s, f2 = pre.shape
    f = f2 // 2
    nt = f // LANES

    def body(pu_ref, pg_ref, wu_ref, wg_ref, da_ref, dpu_ref, dpg_ref, dwu_ref, dwg_ref):
        pu, pg = pu_ref[...], pg_ref[...]
        up = _conv_fwd(pu, wu_ref, FFN_CONV)
        gate = _conv_fwd(pg, wg_ref, FFN_CONV)
        sg = _sigmoid(gate)
        da = da_ref[...]
        dup = da * gate * sg
        dgate = da * up * (sg * (1.0 + gate * (1.0 - sg)))
        dpu_ref[...] = _conv_bwd(pu, dup, wu_ref, dwu_ref, FFN_CONV).astype(BF16)
        dpg_ref[...] = _conv_bwd(pg, dgate, wg_ref, dwg_ref, FFN_CONV).astype(BF16)

    def col(rows, off):
        return pl.BlockSpec((rows, LANES), lambda j: (0, j + off))

    return pl.pallas_call(
        body, name=name, grid=(nt,),
        in_specs=[col(s, 0), col(s, nt), col(FFN_CONV, 0), col(FFN_CONV, nt), col(s, 0)],
        out_specs=[col(s, 0), col(s, 0), col(FFN_CONV, 0), col(FFN_CONV, 0)],
        out_shape=[jax.ShapeDtypeStruct((s, f), BF16), jax.ShapeDtypeStruct((s, f), BF16),
                   jax.ShapeDtypeStruct((FFN_CONV, f), F32), jax.ShapeDtypeStruct((FFN_CONV, f), F32)],
        compiler_params=_cp("parallel"),
    )(pre, pre, cw, cw, dact)


def _xa_probs(qh, kh):
    sc = _mxu(qh, kh, 1, 1) * (XA_DH ** -0.5)
    e = jnp.exp(sc - jnp.max(sc, axis=-1, keepdims=True))
    return e / jnp.sum(e, axis=-1, keepdims=True)


def _xattn_fwd(q, kv, name):
    s, d = q.shape
    m = kv.shape[0]
    tr = min(512, s)

    def body(q_ref, kv_ref, o_ref):
        for h in range(XA_HEADS):
            lo, hi = h * XA_DH, (h + 1) * XA_DH
            p = _xa_probs(q_ref[:, lo:hi], kv_ref[:, lo:hi])
            o_ref[:, lo:hi] = _mxu(p, kv_ref[:, d + lo:d + hi]).astype(BF16)

    row = pl.BlockSpec((tr, d), lambda i: (i, 0))
    return pl.pallas_call(
        body, name=name, grid=(s // tr,), in_specs=[row, pl.BlockSpec((m, 2 * d), lambda i: (0, 0))],
        out_specs=row, out_shape=jax.ShapeDtypeStruct((s, d), BF16), compiler_params=_cp("parallel"),
    )(q, kv)


def _xattn_bwd(q, kv, do, name):
    s, d = q.shape
    m = kv.shape[0]
    tr = min(512, s)

    def body(q_ref, kv_ref, do_ref, dq_ref, dkv_ref):
        @pl.when(pl.program_id(0) == 0)
        def _():
            dkv_ref[...] = jnp.zeros_like(dkv_ref)

        for h in range(XA_HEADS):
            lo, hi = h * XA_DH, (h + 1) * XA_DH
            qh, kh, vh = q_ref[:, lo:hi], kv_ref[:, lo:hi], kv_ref[:, d + lo:d + hi]
            doh = do_ref[:, lo:hi]
            p = _xa_probs(qh, kh)
            dp = _mxu(doh, vh, 1, 1)
            ds = p * (dp - jnp.sum(p * dp, axis=-1, keepdims=True)) * (XA_DH ** -0.5)
            dq_ref[:, lo:hi] = _mxu(ds, kh).astype(BF16)
            dkv_ref[:, lo:hi] += _mxu(ds, qh, 0, 0)
            dkv_ref[:, d + lo:d + hi] += _mxu(p, doh, 0, 0)

    row = pl.BlockSpec((tr, d), lambda i: (i, 0))
    full = pl.BlockSpec((m, 2 * d), lambda i: (0, 0))
    return pl.pallas_call(
        body, name=name, grid=(s // tr,), in_specs=[row, full, row], out_specs=[row, full],
        out_shape=[jax.ShapeDtypeStruct((s, d), BF16), jax.ShapeDtypeStruct((m, 2 * d), F32)],
        compiler_params=_cp("arbitrary"),
    )(q, kv, do)


def _ret_tables():
    c = RET_CHUNK
    lg = np.log1p(-np.exp2(-5.0 - np.arange(RET_HEADS, dtype=np.float32))).astype(np.float32)
    idx = np.arange(c, dtype=np.float32)
    diff = idx[:, None] - idx[None, :]
    intra = np.where(diff >= 0, np.exp(lg[:, None, None] * np.where(diff >= 0, diff, 0.0)), 0.0)
    rk = np.broadcast_to(np.exp(lg[:, None] * (c - 1 - idx))[:, :, None], (RET_HEADS, c, LANES))
    rq = np.broadcast_to(np.exp(lg[:, None] * (idx + 1))[:, :, None], (RET_HEADS, c, LANES))
    return jnp.asarray(np.stack([intra, rk, rq], axis=1).astype(np.float32))


def _rope_tables(s):
    half = RET_DH // 2
    inv = jnp.exp(-math.log(10000.0) * jnp.arange(half, dtype=F32) / half)
    ang = jnp.arange(s, dtype=F32)[:, None] * inv[None, :]
    cos, sin = jnp.cos(ang), jnp.sin(ang)
    return jnp.concatenate([cos, cos], axis=1), jnp.concatenate([-sin, sin], axis=1)


def _ret_specs(n_of):
    c = RET_CHUNK

    def head(off):
        return pl.BlockSpec((c, RET_DH), lambda h, n: (n_of(n), h + off * RET_HEADS))

    pos = pl.BlockSpec((c, RET_DH), lambda h, n: (n_of(n), 0))
    gain = pl.BlockSpec((1, RET_DH), lambda h, n: (0, h))
    tab = pl.BlockSpec((None, 3, c, LANES), lambda h, n: (h, 0, 0, 0))
    st = pl.BlockSpec((None, None, RET_DH, RET_DH), lambda h, n: (h, n_of(n), 0, 0))
    return head, pos, gain, tab, st


def _ret_chunk(q_ref, k_ref, v_ref, cos_ref, sin_ref, tab_ref, prev):
    cos, sin = cos_ref[...], sin_ref[...]
    q = q_ref[...] * cos + pltpu.roll(q_ref[...], RET_DH // 2, 1) * sin
    k = (k_ref[...] * cos + pltpu.roll(k_ref[...], RET_DH // 2, 1) * sin) * (RET_DH ** -0.5)
    v = v_ref[...]
    scores = _mxu(q, k, 1, 1) * tab_ref[0]
    qdec = q * tab_ref[2]
    kdec = k * tab_ref[1]
    o = _mxu(scores, v) + _mxu(qdec, prev)
    return q, k, v, scores, qdec, kdec, o


def _ret_fwd(proj, cos, sin, gain, name):
    s = proj.shape[0]
    c = RET_CHUNK
    nc = s // c
    head, pos, gvec, tab, st = _ret_specs(lambda n: n)

    def body(q_ref, k_ref, v_ref, g_ref, cos_ref, sin_ref, rn_ref, tab_ref, o_ref, st_ref, state):
        @pl.when(pl.program_id(1) == 0)
        def _():
            state[...] = jnp.zeros_like(state)

        prev = state[...]
        st_ref[...] = prev
        _, _, v, _, _, kdec, o = _ret_chunk(q_ref, k_ref, v_ref, cos_ref, sin_ref, tab_ref, prev)
        state[...] = prev * tab_ref[2, c - 1:c, :] + _mxu(kdec, v, 0, 0)
        r = lax.rsqrt(jnp.mean(o * o, axis=-1, keepdims=True) + EPS)
        gate = g_ref[...]
        o_ref[...] = (o * r * rn_ref[...] * (gate * _sigmoid(gate))).astype(BF16)

    return pl.pallas_call(
        body, name=name, grid=(RET_HEADS, nc),
        in_specs=[head(0), head(1), head(2), head(3), pos, pos, gvec, tab],
        out_specs=[head(0), st],
        out_shape=[jax.ShapeDtypeStruct((s, RET_HEADS * RET_DH), BF16),
                   jax.ShapeDtypeStruct((RET_HEADS, nc, RET_DH, RET_DH), F32)],
        scratch_shapes=[pltpu.VMEM((RET_DH, RET_DH), F32)],
        compiler_params=_cp("parallel", "arbitrary"),
    )(proj, proj, proj, proj, cos, sin, gain.reshape(1, -1), _ret_tables())


def _ret_bwd(proj, cos, sin, gain, states, dmerged, name):
    s = proj.shape[0]
    c = RET_CHUNK
    nc = s // c
    head, pos, gvec, tab, st = _ret_specs(lambda n: nc - 1 - n)

    def body(q_ref, k_ref, v_ref, g_ref, cos_ref, sin_ref, rn_ref, tab_ref, st_ref, do_ref,
             dq_ref, dk_ref, dv_ref, dg_ref, drn_ref, carry):
        @pl.when(pl.program_id(1) == 0)
        def _():
            carry[...] = jnp.zeros_like(carry)
            drn_ref[...] = jnp.zeros_like(drn_ref)

        prev = st_ref[...]
        q, k, v, scores, qdec, kdec, o = _ret_chunk(q_ref, k_ref, v_ref, cos_ref, sin_ref, tab_ref, prev)
        r = lax.rsqrt(jnp.mean(o * o, axis=-1, keepdims=True) + EPS)
        on = o * r
        gate = g_ref[...]
        sg = _sigmoid(gate)
        sil = gate * sg
        dout = do_ref[...]
        rn = rn_ref[...]
        dg_ref[...] = (dout * on * rn * (sg * (1.0 + gate * (1.0 - sg)))).astype(BF16)
        drn_ref[...] += jnp.sum(dout * on * sil, axis=0, keepdims=True)
        don = dout * rn * sil
        do = r * (don - on * jnp.mean(don * on, axis=-1, keepdims=True))
        dc = carry[...]
        dsc = _mxu(do, v, 1, 1) * tab_ref[0]
        dq = _mxu(dsc, k) + _mxu(do, prev, 1, 1) * tab_ref[2]
        dk = _mxu(dsc, q, 0, 0) + _mxu(v, dc, 1, 1) * tab_ref[1]
        dv = _mxu(scores, do, 0, 0) + _mxu(kdec, dc)
        carry[...] = _mxu(qdec, do, 0, 0) + dc * tab_ref[2, c - 1:c, :]
        cos, sin = cos_ref[...], sin_ref[...]
        dk = dk * (RET_DH ** -0.5)
        dq_ref[...] = (dq * cos + pltpu.roll(dq * sin, RET_DH // 2, 1)).astype(BF16)
        dk_ref[...] = (dk * cos + pltpu.roll(dk * sin, RET_DH // 2, 1)).astype(BF16)
        dv_ref[...] = dv.astype(BF16)

    width = RET_HEADS * RET_DH
    return pl.pallas_call(
        body, name=name, grid=(RET_HEADS, nc),
        in_specs=[head(0), head(1), head(2), head(3), pos, pos, gvec, tab, st, head(0)],
        out_specs=[head(0)] * 4 + [gvec],
        out_shape=[jax.ShapeDtypeStruct((s, width), BF16)] * 4 + [jax.ShapeDtypeStruct((1, width), F32)],
        scratch_shapes=[pltpu.VMEM((RET_DH, RET_DH), F32)],
        compiler_params=_cp("parallel", "arbitrary"),
    )(proj, proj, proj, proj, cos, sin, gain.reshape(1, -1), _ret_tables(), states, dmerged)


S5_TILE = 512


def _cmul_add(xr, xi, ar, ai, yr, yi):
    return xr + ar * yr - ai * yi, xi + ar * yi + ai * yr


def _s5_scan_fwd(bu, apow, name):
    s, w2 = bu.shape
    r = SCAN_ROWS
    t = S5_TILE
    steps = r.bit_length() - 1

    def body(b_ref, p_ref, o_ref, cr, ci):
        @pl.when(pl.program_id(1) == 0)
        def _():
            cr[...] = jnp.zeros_like(cr)
            ci[...] = jnp.zeros_like(ci)

        xr, xi = b_ref[:, :t], b_ref[:, t:]
        for k in range(steps):
            sh = 1 << k
            xr, xi = _cmul_add(xr, xi, p_ref[sh - 1:sh, :t], p_ref[sh - 1:sh, t:],
                               _shift_down(xr, sh), _shift_down(xi, sh))
        xr, xi = _cmul_add(xr, xi, p_ref[:, :t], p_ref[:, t:], cr[...], ci[...])
        o_ref[:, :t] = xr
        o_ref[:, t:] = xi
        cr[...] = xr[r - 1:r, :]
        ci[...] = xi[r - 1:r, :]

    blk = pl.BlockSpec((r, 2 * t), lambda j, i: (i, j))
    return pl.pallas_call(
        body, name=name, grid=(w2 // (2 * t), s // r),
        in_specs=[blk, pl.BlockSpec((r, 2 * t), lambda j, i: (0, j))], out_specs=blk,
        out_shape=jax.ShapeDtypeStruct((s, w2), F32),
        scratch_shapes=[pltpu.VMEM((1, t), F32), pltpu.VMEM((1, t), F32)],
        compiler_params=_cp("parallel", "arbitrary"),
    )(bu, apow)


def _s5_scan_bwd(dst, apow_rev, st, name):
    s, w2 = dst.shape
    r = SCAN_ROWS
    t = S5_TILE
    nb = s // r
    steps = r.bit_length() - 1

    def body(d_ref, p_ref, s_ref, sp_ref, g_ref, da_ref, cr, ci):
        i = pl.program_id(1)

        @pl.when(i == 0)
        def _():
            cr[...] = jnp.zeros_like(cr)
            ci[...] = jnp.zeros_like(ci)
            da_ref[...] = jnp.zeros_like(da_ref)

        xr, xi = d_ref[:, :t], d_ref[:, t:]
        for k in range(steps):
            sh = 1 << k
            xr, xi = _cmul_add(xr, xi, p_ref[r - sh:r - sh + 1, :t], p_ref[r - sh:r - sh + 1, t:],
                               _shift_up(xr, sh), _shift_up(xi, sh))
        xr, xi = _cmul_add(xr, xi, p_ref[:, :t], p_ref[:, t:], cr[...], ci[...])
        g_ref[:, :t] = xr.astype(BF16)
        g_ref[:, t:] = xi.astype(BF16)
        cr[...] = xr[0:1, :]
        ci[...] = xi[0:1, :]
        first = i == nb - 1
        row = lax.broadcasted_iota(jnp.int32, (r, t), 0)
        last_r = jnp.where(first, 0.0, sp_ref[7:8, :t])
        last_i = jnp.where(first, 0.0, sp_ref[7:8, t:])
        pr = jnp.where(row == 0, last_r, pltpu.roll(s_ref[:, :t], 1, 0))
        pi = jnp.where(row == 0, last_i, pltpu.roll(s_ref[:, t:], 1, 0))
        da_ref[:, :t] += jnp.sum(xr * pr + xi * pi, axis=0, keepdims=True)
        da_ref[:, t:] += jnp.sum(xi * pr - xr * pi, axis=0, keepdims=True)

    blk = pl.BlockSpec((r, 2 * t), lambda j, i: (nb - 1 - i, j))
    halo = pl.BlockSpec((8, 2 * t), lambda j, i: (jnp.maximum((nb - 1 - i) * (r // 8) - 1, 0), j))
    vec = pl.BlockSpec((1, 2 * t), lambda j, i: (0, j))
    return pl.pallas_call(
        body, name=name, grid=(w2 // (2 * t), nb),
        in_specs=[blk, pl.BlockSpec((r, 2 * t), lambda j, i: (0, j)), blk, halo], out_specs=[blk, vec],
        out_shape=[jax.ShapeDtypeStruct((s, w2), BF16), jax.ShapeDtypeStruct((1, w2), F32)],
        scratch_shapes=[pltpu.VMEM((1, t), F32), pltpu.VMEM((1, t), F32)],
        compiler_params=_cp("parallel", "arbitrary"),
    )(dst, apow_rev, st, st)


_GELU_C = math.sqrt(2.0 / math.pi)
_GELU_A = 0.044715


def _gelu(y):
    return 0.5 * y * (1.0 + jnp.tanh(_GELU_C * (y + _GELU_A * y * y * y)))


def _gelu_grad(y):
    th = jnp.tanh(_GELU_C * (y + _GELU_A * y * y * y))
    return 0.5 * (1.0 + th) + 0.5 * y * (1.0 - th * th) * _GELU_C * (1.0 + 3.0 * _GELU_A * y * y)


def _row_call(body, name, s, ins, outs, acc=False):
    tr = min(512, s)

    def spec(width, cb, rows):
        if rows == 1:
            return pl.BlockSpec((1, width), lambda i: (0, cb))
        return pl.BlockSpec((tr, width), lambda i: (i, cb))

    in_specs = [spec(w, cb, a.shape[0]) for a, w, cb in ins]
    out_specs = [spec(w, cb, sd.shape[0]) for sd, w, cb in outs]
    return pl.pallas_call(
        body, name=name, grid=(s // tr,), in_specs=in_specs, out_specs=out_specs,
        out_shape=[sd for sd, _, _ in outs],
        compiler_params=_cp("arbitrary" if acc else "parallel"),
    )(*[a for a, _, _ in ins])


def _sds(shape, dtype):
    return jax.ShapeDtypeStruct(shape, dtype)


def _s5_gelu_fwd(yraw, proj, dvec, name):
    s, w = yraw.shape

    def body(y_ref, u_ref, d_ref, yo_ref, g_ref):
        y = y_ref[...] + d_ref[...] * u_ref[...]
        yo_ref[...] = y
        g_ref[...] = _gelu(y).astype(BF16)

    return _row_call(body, name, s, [(yraw, w, 0), (proj, w, 4), (dvec, w, 0)],
                     [(_sds((s, w), F32), w, 0), (_sds((s, w), BF16), w, 0)])


def _s5_glu_fwd(y, z, b, name):
    s, w = y.shape

    def body(y_ref, z_ref, b_ref, o_ref):
        o_ref[...] = (_gelu(y_ref[...]) * _sigmoid(z_ref[...] + b_ref[...])).astype(BF16)

    return _row_call(body, name, s, [(y, w, 0), (z, w, 0), (b, w, 0)], [(_sds((s, w), BF16), w, 0)])[0]


def _s5_glu_bwd(dmerged, y, z, b, name):
    s, w = y.shape

    def body(do_ref, y_ref, z_ref, b_ref, dz_ref, dg_ref, db_ref):
        @pl.when(pl.program_id(0) == 0)
        def _():
            db_ref[...] = jnp.zeros_like(db_ref)

        g = _gelu(y_ref[...])
        sg = _sigmoid(z_ref[...] + b_ref[...])
        dout = do_ref[...]
        dz = dout * g * sg * (1.0 - sg)
        dz_ref[...] = dz.astype(BF16)
        dg_ref[...] = dout * sg
        db_ref[...] += jnp.sum(dz, axis=0, keepdims=True)

    return _row_call(body, name, s, [(dmerged, w, 1), (y, w, 0), (z, w, 0), (b, w, 0)],
                     [(_sds((s, w), BF16), w, 0), (_sds((s, w), F32), w, 0), (_sds((1, w), F32), w, 0)], acc=True)


def _s5_gelu_bwd(dg1, dg2, y, proj, dvec, name):
    s, w = y.shape

    def body(a_ref, b_ref, y_ref, u_ref, d_ref, dy_ref, du_ref, dd_ref):
        @pl.when(pl.program_id(0) == 0)
        def _():
            dd_ref[...] = jnp.zeros_like(dd_ref)

        dy = (a_ref[...] + b_ref[...]) * _gelu_grad(y_ref[...])
        dy_ref[...] = dy.astype(BF16)
        du_ref[...] = dy * d_ref[...]
        dd_ref[...] += jnp.sum(dy * u_ref[...], axis=0, keepdims=True)

    return _row_call(body, name, s, [(dg1, w, 0), (dg2, w, 0), (y, w, 0), (proj, w, 4), (dvec, w, 0)],
                     [(_sds((s, w), BF16), w, 0), (_sds((s, w), F32), w, 0), (_sds((1, w), F32), w, 0)], acc=True)


def _gdn_conv_fwd(projx, cw, name):
    s = projx.shape[0]
    nh = GDN_HEADS

    def body(x_ref, w_ref, o_ref):
        j = pl.program_id(0)
        cv = _conv_fwd(x_ref[...], w_ref, GDN_CONV)
        y = cv * _sigmoid(cv)
        nrm = y * lax.rsqrt(jnp.sum(y * y, axis=-1, keepdims=True) + EPS)
        o_ref[...] = jnp.where(j < nh, nrm * (GDN_DH ** -0.5), jnp.where(j < 2 * nh, nrm, y))

    return pl.pallas_call(
        body, name=name, grid=(3 * nh,),
        in_specs=[pl.BlockSpec((s, GDN_DH), lambda j: (0, j)), pl.BlockSpec((GDN_CONV, GDN_DH), lambda j: (0, j))],
        out_specs=pl.BlockSpec((s, GDN_DH), lambda j: (0, j)),
        out_shape=jax.ShapeDtypeStruct((s, 3 * nh * GDN_DH), F32), compiler_params=_cp("parallel"),
    )(projx, cw)


def _gdn_conv_bwd(projx, cw, dqkv, name):
    s = projx.shape[0]
    nh = GDN_HEADS

    def body(x_ref, w_ref, d_ref, dx_ref, dw_ref):
        j = pl.program_id(0)
        x = x_ref[...]
        cv = _conv_fwd(x, w_ref, GDN_CONV)
        sg = _sigmoid(cv)
        y = cv * sg
        rinv = lax.rsqrt(jnp.sum(y * y, axis=-1, keepdims=True) + EPS)
        nrm = y * rinv
        dn = d_ref[...]
        dns = jnp.where(j < nh, dn * (GDN_DH ** -0.5), dn)
        dyn = rinv * (dns - nrm * jnp.sum(dns * nrm, axis=-1, keepdims=True))
        dy = jnp.where(j < 2 * nh, dyn, dn)
        dc = dy * (sg * (1.0 + cv * (1.0 - sg)))
        dx_ref[...] = _conv_bwd(x, dc, w_ref, dw_ref, GDN_CONV).astype(BF16)

    col = pl.BlockSpec((s, GDN_DH), lambda j: (0, j))
    wcol = pl.BlockSpec((GDN_CONV, GDN_DH), lambda j: (0, j))
    return pl.pallas_call(
        body, name=name, grid=(3 * nh,), in_specs=[col, wcol, col], out_specs=[col, wcol],
        out_shape=[jax.ShapeDtypeStruct((s, 3 * nh * GDN_DH), BF16), jax.ShapeDtypeStruct((GDN_CONV, 3 * nh * GDN_DH), F32)],
        compiler_params=_cp("parallel"),
    )(projx, cw, dqkv)


def _softplus(x):
    return jnp.maximum(x, 0.0) + jnp.log1p(jnp.exp(-jnp.abs(x)))


def _gdn_gates_fwd(projx, alog, dtb, name):
    s = projx.shape[0]
    w = GDN_HEADS * GDN_DH

    def body(b_ref, a_ref, al_ref, dt_ref, bo_ref, go_ref):
        bo_ref[...] = _sigmoid(b_ref[...])
        go_ref[...] = -jnp.exp(al_ref[...]) * _softplus(a_ref[...] + dt_ref[...])

    return _row_call(body, name, s, [(projx, w, 4), (projx, w, 5), (alog, w, 0), (dtb, w, 0)],
                     [(_sds((s, w), F32), w, 0), (_sds((s, w), F32), w, 0)])


def _gdn_gates_bwd(projx, alog, dtb, dbeta, dg, name):
    s = projx.shape[0]
    w = GDN_HEADS * GDN_DH

    def body(b_ref, a_ref, al_ref, dt_ref, dbe_ref, dg_ref, db_ref, da_ref, dal_ref, ddt_ref):
        @pl.when(pl.program_id(0) == 0)
        def _():
            dal_ref[...] = jnp.zeros_like(dal_ref)
            ddt_ref[...] = jnp.zeros_like(ddt_ref)

        for h in range(GDN_HEADS):
            lo, hi = h * GDN_DH, (h + 1) * GDN_DH
            beta = _sigmoid(b_ref[:, lo:hi])
            pb = jnp.sum(dbe_ref[:, lo:hi], axis=-1, keepdims=True) * (1.0 / GDN_DH)
            db_ref[:, lo:hi] = (pb * beta * (1.0 - beta)).astype(BF16)
            xa = a_ref[:, lo:hi] + dt_ref[:, lo:hi]
            ea = -jnp.exp(al_ref[:, lo:hi])
            pg = jnp.sum(dg_ref[:, lo:hi], axis=-1, keepdims=True) * (1.0 / GDN_DH)
            da = pg * ea * _sigmoid(xa)
            da_ref[:, lo:hi] = da.astype(BF16)
            dal_ref[:, lo:hi] += jnp.sum(pg * ea * _softplus(xa), axis=0, keepdims=True)
            ddt_ref[:, lo:hi] += jnp.sum(da, axis=0, keepdims=True)

    return _row_call(body, name, s,
                     [(projx, w, 4), (projx, w, 5), (alog, w, 0), (dtb, w, 0), (dbeta, w, 0), (dg, w, 0)],
                     [(_sds((s, w), BF16), w, 0), (_sds((s, w), BF16), w, 0),
                      (_sds((1, w), F32), w, 0), (_sds((1, w), F32), w, 0)], acc=True)


def _gdn_tri():
    c = GDN_CHUNK
    i = lax.broadcasted_iota(jnp.int32, (c, c), 0)
    j = lax.broadcasted_iota(jnp.int32, (c, c), 1)
    return ((i >= j).astype(F32), (i <= j).astype(F32), i >= j, i > j, (i == j).astype(F32))


def _bdot(a, b, ca=2, cb=1, precision=None):
    return lax.dot_general(a, b, (((ca,), (cb,)), ((0,), (0,))), precision=precision, preferred_element_type=F32)


def _bmxu(a, b, ca=2, cb=1):
    return _bdot(a.astype(BF16), b.astype(BF16), ca, cb)


def _heads(x):
    return jnp.stack([x[:, h * GDN_DH:(h + 1) * GDN_DH] for h in range(GDN_HEADS)], axis=0)


def _unheads(x):
    return jnp.concatenate([x[h] for h in range(GDN_HEADS)], axis=1)


def _gdn_chunk(q, k, v, bb, g2d, tri):
    low, up, incl, strict, eye = tri
    c = GDN_CHUNK
    gc = _heads(_dot(low, g2d, precision=HI))
    gci = gc[:, :, :c]
    gdiff = gci - jnp.swapaxes(gci, 1, 2)
    decay = jnp.where(incl, jnp.exp(jnp.where(incl, gdiff, 0.0)), 0.0)
    kb, vb = k * bb, v * bb
    kbk = _bmxu(kb, k, 2, 2)
    x = -jnp.where(strict, kbk * decay, 0.0)
    t = eye + x
    p = x
    for _ in range(c.bit_length() - 2):
        p = _bdot(p, p, precision=HI)
        t = t + _bdot(t, p, precision=HI)
    eg = jnp.exp(gc)
    kbg = kb * eg
    gcl = gc[:, c - 1:c, :]
    ek = jnp.exp(gcl - gc)
    qkraw = _bmxu(q, k, 2, 2)
    return dict(decay=decay, kb=kb, vb=vb, kbk=kbk, t=t, eg=eg, kbg=kbg, ek=ek, gl=jnp.exp(gcl),
                w=_bmxu(t, kbg), u=_bmxu(t, vb), qkraw=qkraw, qk=jnp.where(incl, qkraw * decay, 0.0),
                qd=q * eg, kd=k * ek)


def _gdn_specs(n_of):
    c, w = GDN_CHUNK, GDN_HEADS * GDN_DH

    def blk(cb, width=w):
        return pl.BlockSpec((c, width), lambda n: (n_of(n), cb))

    st = pl.BlockSpec((None, GDN_HEADS, GDN_DH, GDN_DH), lambda n: (n_of(n), 0, 0, 0))
    vec = pl.BlockSpec((1, GDN_DH), lambda n: (0, 0))
    return blk, st, vec


def _gdn_load(qkv_ref, b_ref, g_ref, tri):
    w = GDN_HEADS * GDN_DH
    q, k, v = _heads(qkv_ref[:, :w]), _heads(qkv_ref[:, w:2 * w]), _heads(qkv_ref[:, 2 * w:])
    bb = _heads(b_ref[...])
    return q, k, v, bb, _gdn_chunk(q, k, v, bb, g_ref[...], tri)


def _gdn_fwd(qkv, beta, g, projx, onorm, name):
    s = qkv.shape[0]
    nc = s // GDN_CHUNK
    w = GDN_HEADS * GDN_DH
    blk, st, vec = _gdn_specs(lambda n: n)

    def body(qkv_ref, b_ref, g_ref, z_ref, on_ref, o_ref, st_ref, state):
        @pl.when(pl.program_id(0) == 0)
        def _():
            state[...] = jnp.zeros_like(state)

        _, _, _, _, ch = _gdn_load(qkv_ref, b_ref, g_ref, _gdn_tri())
        sp = state[...]
        st_ref[...] = sp
        vn = ch["u"] - _bmxu(ch["w"], sp)
        o = _bmxu(ch["qd"], sp) + _bmxu(ch["qk"], vn)
        state[...] = sp * ch["gl"] + _bmxu(ch["kd"], vn, 1, 1)
        r = lax.rsqrt(jnp.mean(o * o, axis=-1, keepdims=True) + EPS)
        z = _heads(z_ref[...])
        o_ref[...] = _unheads(o * r * on_ref[...] * (z * _sigmoid(z))).astype(BF16)

    return pl.pallas_call(
        body, name=name, grid=(nc,),
        in_specs=[blk(0, 3 * w), blk(0), blk(0), blk(3), vec], out_specs=[blk(0), st],
        out_shape=[jax.ShapeDtypeStruct((s, w), BF16), jax.ShapeDtypeStruct((nc, GDN_HEADS, GDN_DH, GDN_DH), F32)],
        scratch_shapes=[pltpu.VMEM((GDN_HEADS, GDN_DH, GDN_DH), F32)],
        compiler_params=_cp("arbitrary"),
    )(qkv, beta, g, projx, onorm.reshape(1, -1))


def _gdn_bwd(qkv, beta, g, projx, onorm, states, dout, name):
    s = qkv.shape[0]
    c = GDN_CHUNK
    nc = s // c
    w = GDN_HEADS * GDN_DH
    blk, st, vec = _gdn_specs(lambda n: nc - 1 - n)

    def body(qkv_ref, b_ref, g_ref, z_ref, on_ref, st_ref, do_ref,
             dqkv_ref, db_ref, dg_ref, dz_ref, don_ref, carry):
        @pl.when(pl.program_id(0) == 0)
        def _():
            carry[...] = jnp.zeros_like(carry)
            don_ref[...] = jnp.zeros_like(don_ref)

        tri = _gdn_tri()
        low, up, incl, strict, eye = tri
        q, k, v, bb, ch = _gdn_load(qkv_ref, b_ref, g_ref, tri)
        sp = st_ref[...]
        vn = ch["u"] - _bmxu(ch["w"], sp)
        o = _bmxu(ch["qd"], sp) + _bmxu(ch["qk"], vn)
        r = lax.rsqrt(jnp.mean(o * o, axis=-1, keepdims=True) + EPS)
        orn = o * r
        z = _heads(z_ref[...])
        sg = _sigmoid(z)
        dout = _heads(do_ref[...])
        onw = on_ref[...]
        dz_ref[...] = _unheads(dout * orn * onw * (sg * (1.0 + z * (1.0 - sg)))).astype(BF16)
        don = dout * (z * sg)
        don_ref[...] += jnp.sum(jnp.sum(don * orn, axis=0), axis=0, keepdims=True)
        dor = don * onw
        do = r * (dor - orn * jnp.mean(dor * orn, axis=-1, keepdims=True))
        dsn = carry[...]
        dqd = _bmxu(do, sp, 2, 2)
        dqk = jnp.where(incl, _bmxu(do, vn, 2, 2), 0.0)
        dvn = _bmxu(ch["qk"], do, 1, 1) + _bmxu(ch["kd"], dsn)
        dkd = _bmxu(vn, dsn, 2, 2)
        dgl = jnp.sum(dsn * sp, axis=1, keepdims=True)
        dw = -_bmxu(dvn, sp, 2, 2)
        carry[...] = _bmxu(ch["qd"], do, 1, 1) + dsn * ch["gl"] - _bmxu(ch["w"], dvn, 1, 1)
        t = ch["t"]
        dvb = _bmxu(t, dvn, 1, 1)
        dkbg = _bmxu(t, dw, 1, 1)
        dt = _bmxu(dvn, ch["vb"], 2, 2) + _bmxu(dw, ch["kbg"], 2, 2)
        da = -_bdot(_bdot(t, dt, 1, 1, precision=HI), t, 2, 2, precision=HI)
        da = jnp.where(strict, da, 0.0)
        decay = ch["decay"]
        dkbk = da * decay
        dqkr = dqk * decay
        mdec = (da * ch["kbk"] + dqk * ch["qkraw"]) * decay
        dkb = _bmxu(dkbk, k) + dkbg * ch["eg"]
        dk = _bmxu(dkbk, ch["kb"], 1, 1) + _bmxu(dqkr, q, 1, 1) + dkd * ch["ek"] + dkb * bb
        dq = _bmxu(dqkr, k) + dqd * ch["eg"]
        tk = dkd * ch["kd"]
        dgcl = jnp.sum(tk, axis=1, keepdims=True) + dgl * ch["gl"]
        row = lax.broadcasted_iota(jnp.int32, (GDN_HEADS, c, GDN_DH), 1)
        zpad = jnp.zeros((GDN_HEADS, c, GDN_DH - c), F32)
        dgc = (jnp.concatenate([mdec, zpad], axis=2) - jnp.concatenate([jnp.swapaxes(mdec, 1, 2), zpad], axis=2)
               + dqd * ch["qd"] - tk + dkbg * ch["kbg"] + jnp.where(row == c - 1, dgcl, 0.0))
        dqkv_ref[:, :w] = _unheads(dq)
        dqkv_ref[:, w:2 * w] = _unheads(dk)
        dqkv_ref[:, 2 * w:] = _unheads(dvb * bb)
        db_ref[...] = _unheads(dkb * k + dvb * v)
        dg_ref[...] = _dot(up, _unheads(dgc), precision=HI)

    return pl.pallas_call(
        body, name=name, grid=(nc,),
        in_specs=[blk(0, 3 * w), blk(0), blk(0), blk(3), vec, st, blk(0)],
        out_specs=[blk(0, 3 * w), blk(0), blk(0), blk(0), vec],
        out_shape=[jax.ShapeDtypeStruct((s, 3 * w), F32), jax.ShapeDtypeStruct((s, w), F32),
                   jax.ShapeDtypeStruct((s, w), F32), jax.ShapeDtypeStruct((s, w), BF16),
                   jax.ShapeDtypeStruct((1, GDN_DH), F32)],
        scratch_shapes=[pltpu.VMEM((GDN_HEADS, GDN_DH, GDN_DH), F32)],
        compiler_params=_cp("arbitrary"),
    )(qkv, beta, g, projx, onorm.reshape(1, -1), states, dout)


_WEIGHTS = (
    "l0_mix_norm", "l0_w_in", "l0_ret_norm", "l0_s5_lambda_re", "l0_s5_lambda_im", "l0_s5_b_re", "l0_s5_b_im",
    "l0_s5_c_re", "l0_s5_c_im", "l0_s5_d", "l0_s5_log_dt", "l0_s5_w_glu", "l0_s5_b_glu", "l0_w_out",
    "l0_xa_norm", "l0_mem_norm", "l0_xa_wq", "l0_xa_wkv", "l0_xa_wo", "l0_ffn_norm", "l0_ffn_w_up",
    "l0_ffn_conv", "l0_ffn_w_down", "l1_mix_norm", "l1_w_in", "l1_conv", "l1_a_log", "l1_dt_bias", "l1_o_norm",
    "l1_w_out", "l1_xa_norm", "l1_mem_norm", "l1_xa_wq", "l1_xa_wkv", "l1_xa_wo", "l1_ffn_norm", "l1_ffn_w_up",
    "l1_ffn_conv", "l1_ffn_w_down", "final_norm")
_INPUTS = ("x", "mem") + _WEIGHTS + ("loss_target",) + tuple("m_" + n for n in _WEIGHTS) + tuple("v_" + n for n in _WEIGHTS)

_COL = ("l0_w_in", "l0_xa_wkv", "l0_ffn_w_up", "l1_w_in", "l1_xa_wkv", "l1_ffn_w_up")
_ROW = ("l0_w_out", "l0_xa_wq", "l0_xa_wo", "l0_ffn_w_down", "l1_w_out", "l1_xa_wq", "l1_xa_wo", "l1_ffn_w_down",
        "l0_s5_w_glu")
_CONV = ("l0_ffn_conv", "l1_conv", "l1_ffn_conv")
_REP = tuple(n for n in _WEIGHTS if n not in _COL + _ROW + _CONV)
_CONV_ROWS = 8


def _round_up(n, m):
    return (n + m - 1) // m * m


def _pack_col(ts):
    a = jnp.concatenate(ts, axis=-1)
    return jnp.pad(a, [(0, 0)] * (a.ndim - 1) + [(0, _round_up(a.shape[-1], LANES) - a.shape[-1])])


def _pack_row(ts):
    ts = list(ts[:-1]) + [ts[-1].reshape(ts[-1].shape[:-2] + (ts[-1].shape[-2] // 2, 2 * ts[-1].shape[-1]))]
    return jnp.concatenate(ts, axis=-2)


def _pack_conv(ts):
    ts = [jnp.pad(t, [(0, 0)] * (t.ndim - 2) + [(0, _CONV_ROWS - t.shape[-2]), (0, 0)]) for t in ts]
    return _pack_col(ts)


def _pack_rep(ts):
    a = jnp.concatenate([t.reshape(-1) for t in ts])
    return jnp.pad(a, (0, _round_up(a.shape[0], 8 * LANES) - a.shape[0])).reshape(-1, LANES)


def _unpack(packed, shapes, axis):
    out, off = [], 0
    for n in shapes:
        out.append(lax.slice_in_dim(packed, off, off + n, axis=axis))
        off += n
    return out


def _s5_interleave(re, im):
    lead = re.shape[:-1]
    nt = re.shape[-1] // S5_TILE
    both = jnp.stack([re.reshape(lead + (nt, S5_TILE)), im.reshape(lead + (nt, S5_TILE))], axis=-2)
    return both.reshape(lead + (2 * re.shape[-1],))


def _s5_split(x):
    lead = x.shape[:-1]
    y = x.reshape(lead + (x.shape[-1] // (2 * S5_TILE), 2, S5_TILE))
    return y[..., 0, :].reshape(lead + (-1,)), y[..., 1, :].reshape(lead + (-1,))


def _s5_discretise(lr, li, log_dt, b_re, b_im):
    dt = jnp.exp(log_dt)[:, None]
    mag = jnp.exp(lr * dt)
    a_re = mag * jnp.cos(li * dt)
    a_im = mag * jnp.sin(li * dt)
    den = lr * lr + li * li
    z_re = ((a_re - 1.0) * lr + a_im * li) / den
    z_im = (a_im * lr - (a_re - 1.0) * li) / den
    bb_re = z_re[:, None, :] * b_re - z_im[:, None, :] * b_im
    bb_im = z_re[:, None, :] * b_im + z_im[:, None, :] * b_re
    return a_re, a_im, bb_re, bb_im


def _pow_table(a_re, a_im, rows):
    tr, ti = a_re[None], a_im[None]
    while tr.shape[0] < rows:
        lr, li = tr[-1:], ti[-1:]
        tr, ti = (jnp.concatenate([tr, tr * lr - ti * li]), jnp.concatenate([ti, tr * li + ti * lr]))
    return tr, ti


def _block_diag(b):
    g, r, c = b.shape
    return jnp.einsum("grc,gk->grkc", b, jnp.eye(g, dtype=b.dtype)).reshape(g * r, g * c)


def _block_diag_of(d, g):
    r, c = d.shape[0] // g, d.shape[1] // g
    return jnp.einsum("grkc,gk->grc", d.reshape(g, r, g, c), jnp.eye(g, dtype=d.dtype))


def kernel(*args):
    p = dict(zip(_INPUTS, args, strict=True))
    x0, mem0, tgt = p["x"][0], p["mem"][0], p["loss_target"][0]
    s, d = x0.shape
    grads = {}

    col_w = [p[n].shape[1] for n in _COL]
    row_h = [p[n].shape[0] for n in _ROW[:-1]] + [p["l0_s5_w_glu"].shape[0] // 2]
    conv_w = [p[n].shape[1] for n in _CONV]
    ga, gb, gc = _all_gather([_pack_col([p[n] for n in _COL]).astype(BF16),
                              _pack_row([p[n] for n in _ROW]).astype(BF16),
                              _pack_conv([p[n] for n in _CONV])], "gather_weights")
    w = {}
    for n, t in zip(_COL, _unpack(ga, col_w, 2)):
        w[n] = t.transpose(1, 0, 2).reshape(t.shape[1], -1)
    for n, t in zip(_ROW, _unpack(gb, row_h, 1)):
        w[n] = t.reshape(-1, t.shape[2])
    w["l0_s5_w_glu"] = w["l0_s5_w_glu"].reshape(N_DEV * p["l0_s5_w_glu"].shape[0], -1)
    for n, t in zip(_CONV, _unpack(gc, conv_w, 2)):
        w[n] = t[:, :p[n].shape[0], :].transpose(1, 0, 2).reshape(p[n].shape[0], -1)

    def xattn(pre, x_in):
        hx = _norm_fwd(x_in, p[pre + "xa_norm"], pre + "xa_norm_fwd")
        q = _mm(hx, w[pre + "xa_wq"], out_dtype=BF16, name=pre + "xa_q")
        memn = _norm_fwd(mem0, p[pre + "mem_norm"], pre + "mem_norm_fwd")
        kv = _mm(memn, w[pre + "xa_wkv"], out_dtype=BF16, name=pre + "xa_kv")
        ao = _xattn_fwd(q, kv, pre + "xattn_fwd")
        x_out = _mm(ao, w[pre + "xa_wo"], res=x_in, name=pre + "xa_o")
        return x_out, (x_in, hx, q, memn, kv, ao)

    def xattn_bwd(pre, saved, dxo):
        x_in, hx, q, memn, kv, ao = saved
        dao = _mm(dxo, w[pre + "xa_wo"], tb=True, name=pre + "xa_o_dx")
        grads[pre + "xa_wo"] = _mm(ao, dxo, ta=True, name=pre + "xa_o_dw")
        dq, dkv = _xattn_bwd(q, kv, dao, pre + "xattn_bwd")
        grads[pre + "xa_wq"] = _mm(hx, dq, ta=True, name=pre + "xa_q_dw")
        dhx = _mm(dq, w[pre + "xa_wq"], tb=True, name=pre + "xa_q_dx")
        grads[pre + "xa_wkv"] = _mm(memn, dkv, ta=True, name=pre + "xa_kv_dw")
        dmemn = _mm(dkv, w[pre + "xa_wkv"], tb=True, name=pre + "xa_kv_dx")
        dx_in, grads[pre + "xa_norm"] = _norm_bwd(x_in, p[pre + "xa_norm"], dhx, dxo, pre + "xa_norm_bwd")
        _, grads[pre + "mem_norm"] = _norm_bwd(mem0, p[pre + "mem_norm"], dmemn, jnp.zeros_like(mem0), pre + "mem_norm_bwd")
        return dx_in

    def ffn(pre, x_in):
        hf = _norm_fwd(x_in, p[pre + "ffn_norm"], pre + "ffn_norm_fwd")
        up = _mm(hf, w[pre + "ffn_w_up"], name=pre + "ffn_up")
        act = _ffn_act_fwd(up, w[pre + "ffn_conv"], pre + "ffn_act_fwd")
        x_out = _mm(act, w[pre + "ffn_w_down"], res=x_in, name=pre + "ffn_down")
        return x_out, (x_in, hf, up, act)

    def ffn_bwd(pre, saved, dxo):
        x_in, hf, up, act = saved
        dact = _mm(dxo, w[pre + "ffn_w_down"], tb=True, name=pre + "ffn_down_dx")
        grads[pre + "ffn_w_down"] = _mm(act, dxo, ta=True, name=pre + "ffn_down_dw")
        dpu, dpg, dcu, dcg = _ffn_act_bwd(up, w[pre + "ffn_conv"], dact, pre + "ffn_act_bwd")
        dup = jnp.concatenate([dpu, dpg], axis=1)
        grads[pre + "ffn_conv"] = jnp.concatenate([dcu, dcg], axis=1)
        dhf = _mm(dup, w[pre + "ffn_w_up"], tb=True, name=pre + "ffn_up_dx")
        grads[pre + "ffn_w_up"] = _mm(hf, dup, ta=True, name=pre + "ffn_up_dw")
        dx_in, grads[pre + "ffn_norm"] = _norm_bwd(x_in, p[pre + "ffn_norm"], dhf, dxo, pre + "ffn_norm_bwd")
        return dx_in

    cos, sin = _rope_tables(s)
    (a_re, a_im, bb_re, bb_im), disc_vjp = jax.vjp(
        _s5_discretise, p["l0_s5_lambda_re"], p["l0_s5_lambda_im"], p["l0_s5_log_dt"], p["l0_s5_b_re"], p["l0_s5_b_im"])
    pw_re, pw_im = _pow_table(a_re.reshape(-1), a_im.reshape(-1), SCAN_ROWS)
    apow = _s5_interleave(pw_re, pw_im)
    apow_rev = _s5_interleave(pw_re[::-1], -pw_im[::-1])
    bbig = _s5_interleave(_block_diag(bb_re), _block_diag(bb_im)).astype(BF16)
    cbig = _s5_interleave(_block_diag(p["l0_s5_c_re"]).T, -_block_diag(p["l0_s5_c_im"]).T).T.astype(BF16)
    s5_d = p["l0_s5_d"].reshape(1, -1)
    b_glu = p["l0_s5_b_glu"].reshape(1, -1)

    h0 = _norm_fwd(x0, p["l0_mix_norm"], "l0_mix_norm_fwd")
    proj = _mm(h0, w["l0_w_in"], name="l0_in")
    o_ret, ret_states = _ret_fwd(proj, cos, sin, p["l0_ret_norm"], "l0_ret_fwd")
    u = proj[:, 4 * RET_HEADS * RET_DH:]
    bu = _mm(u, bbig, name="l0_s5_bu")
    st = _s5_scan_fwd(bu, apow, "l0_s5_scan_fwd")
    yraw = _mm(st, cbig, name="l0_s5_c")
    y, gy = _s5_gelu_fwd(yraw, proj, s5_d, "l0_s5_gelu_fwd")
    z = _mm(gy, w["l0_s5_w_glu"], name="l0_s5_glu_mm")
    y2 = _s5_glu_fwd(y, z, b_glu, "l0_s5_glu_fwd")
    merged = jnp.concatenate([o_ret, y2], axis=1)
    x1 = _mm(merged, w["l0_w_out"], res=x0, name="l0_out")
    x2, xa0 = xattn("l0_", x1)
    x3, ff0 = ffn("l0_", x2)

    nqkv = 4 * GDN_HEADS * GDN_DH
    w1 = w["l1_w_in"]
    wx = jnp.concatenate([w1[:, :nqkv], jnp.repeat(w1[:, nqkv:nqkv + GDN_HEADS], GDN_DH, axis=1),
                          jnp.repeat(w1[:, nqkv + GDN_HEADS:], GDN_DH, axis=1)], axis=1)
    alog_x = jnp.repeat(p["l1_a_log"], GDN_DH).reshape(1, -1)
    dtb_x = jnp.repeat(p["l1_dt_bias"], GDN_DH).reshape(1, -1)
    h1 = _norm_fwd(x3, p["l1_mix_norm"], "l1_mix_norm_fwd")
    projx = _mm(h1, wx, name="l1_in")
    qkv = _gdn_conv_fwd(projx, w["l1_conv"], "l1_conv_fwd")
    beta, glog = _gdn_gates_fwd(projx, alog_x, dtb_x, "l1_gates_fwd")
    o_gdn, gdn_states = _gdn_fwd(qkv, beta, glog, projx, p["l1_o_norm"], "l1_gdn_fwd")
    x4 = _mm(o_gdn, w["l1_w_out"], res=x3, name="l1_out")
    x5, xa1 = xattn("l1_", x4)
    x6, ff1 = ffn("l1_", x5)

    loss_part, dx6, grads["final_norm"] = _loss_head(x6, p["final_norm"], tgt, "loss_head")
    loss = lax.psum(loss_part[0, 0], ("x", "y", "c"))
    dx5 = ffn_bwd("l1_", ff1, dx6)
    dx4 = xattn_bwd("l1_", xa1, dx5)

    do_gdn = _mm(dx4, w["l1_w_out"], tb=True, name="l1_out_dx")
    grads["l1_w_out"] = _mm(o_gdn, dx4, ta=True, name="l1_out_dw")
    dqkv, dbeta, dglog, dz, grads["l1_o_norm"] = _gdn_bwd(
        qkv, beta, glog, projx, p["l1_o_norm"], gdn_states, do_gdn, "l1_gdn_bwd")
    dpre, grads["l1_conv"] = _gdn_conv_bwd(projx, w["l1_conv"], dqkv, "l1_conv_bwd")
    db, da, dalog_x, ddtb_x = _gdn_gates_bwd(projx, alog_x, dtb_x, dbeta, dglog, "l1_gates_bwd")
    dprojx = jnp.concatenate([dpre, dz, db, da], axis=1)
    dh1 = _mm(dprojx, wx, tb=True, name="l1_in_dx")
    dwx = _mm(h1, dprojx, ta=True, name="l1_in_dw")
    grads["l1_w_in"] = jnp.concatenate(
        [dwx[:, :nqkv], dwx[:, nqkv:nqkv + GDN_HEADS * GDN_DH].reshape(d, GDN_HEADS, GDN_DH).sum(-1),
         dwx[:, nqkv + GDN_HEADS * GDN_DH:].reshape(d, GDN_HEADS, GDN_DH).sum(-1)], axis=1)
    grads["l1_a_log"] = dalog_x.reshape(GDN_HEADS, GDN_DH).sum(-1)
    grads["l1_dt_bias"] = ddtb_x.reshape(GDN_HEADS, GDN_DH).sum(-1)
    dx3, grads["l1_mix_norm"] = _norm_bwd(x3, p["l1_mix_norm"], dh1, dx4, "l1_mix_norm_bwd")

    dx2 = ffn_bwd("l0_", ff0, dx3)
    dx1 = xattn_bwd("l0_", xa0, dx2)

    dmerged = _mm(dx1, w["l0_w_out"], tb=True, name="l0_out_dx")
    grads["l0_w_out"] = _mm(merged, dx1, ta=True, name="l0_out_dw")
    drq, drk, drv, drg, grads["l0_ret_norm"] = _ret_bwd(proj, cos, sin, p["l0_ret_norm"], ret_states, dmerged, "l0_ret_bwd")
    dzg, dg1, grads["l0_s5_b_glu"] = _s5_glu_bwd(dmerged, y, z, b_glu, "l0_s5_glu_bwd")
    grads["l0_s5_w_glu"] = _mm(gy, dzg, ta=True, name="l0_s5_glu_dw")
    dg2 = _mm(dzg, w["l0_s5_w_glu"], tb=True, name="l0_s5_glu_dx")
    dyraw, du_dir, grads["l0_s5_d"] = _s5_gelu_bwd(dg1, dg2, y, proj, s5_d, "l0_s5_gelu_bwd")
    dst = _mm(dyraw, cbig, tb=True, name="l0_s5_c_dx")
    dcbig = _mm(st, dyraw, ta=True, name="l0_s5_c_dw")
    gsc, da_s5 = _s5_scan_bwd(dst, apow_rev, st, "l0_s5_scan_bwd")
    du = _mm(gsc, bbig, tb=True, res=du_dir, out_dtype=BF16, name="l0_s5_bu_dx")
    dbbig = _mm(u, gsc, ta=True, name="l0_s5_bu_dw")
    dproj = jnp.concatenate([drq, drk, drv, drg, du], axis=1)
    dh0 = _mm(dproj, w["l0_w_in"], tb=True, name="l0_in_dx")
    grads["l0_w_in"] = _mm(h0, dproj, ta=True, name="l0_in_dw")
    dx0, grads["l0_mix_norm"] = _norm_bwd(x0, p["l0_mix_norm"], dh0, dx1, "l0_mix_norm_bwd")

    dbb_re, dbb_im = (_block_diag_of(t, S5_GROUPS) for t in _s5_split(dbbig))
    dct_re, dct_im = _s5_split(dcbig.T)
    grads["l0_s5_c_re"] = _block_diag_of(dct_re.T, S5_GROUPS)
    grads["l0_s5_c_im"] = -_block_diag_of(dct_im.T, S5_GROUPS)
    da_re, da_im = (t.reshape(S5_GROUPS, S5_STATE) for t in _s5_split(da_s5[0]))
    (grads["l0_s5_lambda_re"], grads["l0_s5_lambda_im"], grads["l0_s5_log_dt"], grads["l0_s5_b_re"],
     grads["l0_s5_b_im"]) = disc_vjp((da_re, da_im, dbb_re, dbb_im))

    def slots_col(g):
        return g.reshape(g.shape[0], N_DEV, -1).transpose(1, 0, 2)

    def slots_row(g):
        return g.reshape(N_DEV, -1, g.shape[1])

    pa = _pack_col([slots_col(grads[n]) for n in _COL]).astype(BF16)
    pb = _pack_row([slots_row(grads[n]) for n in _ROW]).astype(BF16)
    pc = _pack_conv([slots_col(grads[n]) for n in _CONV])
    pr = _pack_rep([grads[n].reshape(p[n].shape) for n in _REP])
    ra, rb, rc, rr = _exchange([pa, pb, pc], [pr], "exchange_grads")

    outs = {}
    groups = ((_COL, _pack_col, ra, col_w, 1, "adamw_col"), (_ROW, _pack_row, rb, row_h, 0, "adamw_row"),
              (_CONV, _pack_conv, rc, conv_w, 1, "adamw_conv"))
    for names, pack, parts, sizes, axis, nm in groups:
        res = _adamw(parts, *(pack([p[pre + n] for n in names]) for pre in ("", "m_", "v_")), nm)
        for kind, packed in zip(("grad_", "delta_", "new_m_", "new_v_"), res):
            for n, t in zip(names, _unpack(packed, sizes, axis)):
                outs[kind + n] = t[:p[n].shape[0]].reshape(p[n].shape)
    res = _adamw(rr, *(_pack_rep([p[pre + n] for n in _REP]) for pre in ("", "m_", "v_")), "adamw_rep")
    rep_n = [math.prod(p[n].shape) for n in _REP]
    for kind, packed in zip(("grad_", "delta_", "new_m_", "new_v_"), res):
        for n, t in zip(_REP, _unpack(packed.reshape(-1), rep_n, 0)):
            outs[kind + n] = t.reshape(p[n].shape)

    return (loss, dx0[None]) + tuple(outs[kind + n] for kind in ("grad_", "delta_", "new_m_", "new_v_") for n in _WEIGHTS)
```

```python
import functools
import math

import numpy as np
import jax
import jax.numpy as jnp
from jax import lax
from jax.experimental import pallas as pl
from jax.experimental.pallas import tpu as pltpu

F32 = jnp.float32
BF16 = jnp.bfloat16
EPS = 1e-6
N_DEV = 8
LANES = 128
VMEM_LIMIT = 48 * 1024 * 1024
HI = lax.Precision.HIGHEST

RET_HEADS, RET_DH, RET_CHUNK = 4, 128, 128
S5_GROUPS, S5_GROUP, S5_STATE = 32, 16, 64
GDN_HEADS, GDN_DH, GDN_CHUNK, GDN_CONV = 8, 128, 64, 4
XA_HEADS, XA_DH = 4, 256
FFN_CONV = 3
SCAN_ROWS = 256

ADAM_LR, ADAM_B1, ADAM_B2, ADAM_EPS, ADAM_WD, ADAM_STEP = 0.001, 0.9, 0.999, 1e-08, 0.01, 10


def _cp(*sem):
    return pltpu.CompilerParams(dimension_semantics=sem if sem else None, vmem_limit_bytes=VMEM_LIMIT)


def _tile(n, cap):
    if n <= cap:
        return n
    best = None
    for t in range(LANES, cap + 1, LANES):
        if n % t == 0:
            best = t
    assert best is not None, n
    return best


def _dot(a, b, ca=1, cb=0, precision=None):
    return lax.dot_general(a, b, (((ca,), (cb,)), ((), ())), precision=precision, preferred_element_type=F32)


def _mxu(a, b, ca=1, cb=0):
    return _dot(a.astype(BF16), b.astype(BF16), ca, cb)


def _sigmoid(x):
    return 1.0 / (1.0 + jnp.exp(-x))


def _shift_down(x, k):
    row = lax.broadcasted_iota(jnp.int32, x.shape, 0)
    return jnp.where(row >= k, pltpu.roll(x, k, 0), 0.0)


def _shift_up(x, k):
    n = x.shape[0]
    row = lax.broadcasted_iota(jnp.int32, x.shape, 0)
    return jnp.where(row < n - k, pltpu.roll(x, n - k, 0), 0.0)


def _mesh_pos():
    return lax.axis_index("x"), lax.axis_index("y"), lax.axis_index("c")


def _slot(px, py, pc):
    return 4 * px + 2 * py + pc


def _all_peers(x, y, c):
    flips = [(fx, fy, fc) for fx in (0, 1) for fy in (0, 1) for fc in (0, 1)][1:]
    return [(1 - x if fx else x, 1 - y if fy else y, 1 - c if fc else c) for fx, fy, fc in flips]


_HBM = pl.BlockSpec(memory_space=pltpu.HBM)
_SEM = pl.BlockSpec(memory_space=pltpu.SEMAPHORE)
N_PEERS = N_DEV - 1


def _push_copies(srcs, lands, send_sems, recv_sems, ns, start):
    x, y, c = _mesh_pos()
    me = _slot(x, y, c)
    out = []
    for k, to in enumerate(_all_peers(x, y, c)):
        for a in range(len(srcs)):
            src = srcs[a].at[_slot(*to)] if a < ns else srcs[a]
            dst = lands[a].at[me if start else _slot(*to)]
            out.append(pltpu.make_async_remote_copy(
                src_ref=src, dst_ref=dst, send_sem=send_sems.at[a * N_PEERS + k], recv_sem=recv_sems.at[a * N_PEERS + k],
                device_id=to, device_id_type=pl.DeviceIdType.MESH))
    return out


def _push_start(scatter, gather, name):
    ns = len(scatter)
    arrs = list(scatter) + list(gather)
    n = len(arrs)
    land_shapes = [a.shape for a in scatter] + [(N_DEV,) + a.shape for a in gather]

    def body(*refs):
        srcs, lands = refs[:n], refs[n:2 * n]
        send_sems, recv_sems = refs[2 * n], refs[2 * n + 1]
        for cp in _push_copies(srcs, lands, send_sems, recv_sems, ns, True):
            cp.start()
        refs[-1][...] = jnp.zeros((8, LANES), F32)

    hbm_in = [pltpu.with_memory_space_constraint(a, pltpu.HBM) for a in arrs]
    hbm_in += [pltpu.with_memory_space_constraint(lax.empty(s, a.dtype), pltpu.HBM) for s, a in zip(land_shapes, arrs)]
    res = pl.pallas_call(
        body, name=name,
        out_shape=(pltpu.SemaphoreType.DMA((n * N_PEERS,)), pltpu.SemaphoreType.DMA((n * N_PEERS,)))
        + tuple(pltpu.HBM(a.shape, a.dtype) for a in arrs)
        + tuple(pltpu.HBM(s, a.dtype) for s, a in zip(land_shapes, arrs))
        + (jax.ShapeDtypeStruct((8, LANES), F32),),
        in_specs=[_HBM] * (2 * n),
        out_specs=(_SEM, _SEM) + (_HBM,) * (2 * n) + (pl.BlockSpec(memory_space=pltpu.VMEM),),
        input_output_aliases={i: 2 + i for i in range(2 * n)},
        compiler_params=pltpu.CompilerParams(has_side_effects=pltpu.SideEffectType.DATAFLOW_SIDE_EFFECTING),
    )(*hbm_in)
    return (res[0], res[1], res[2:2 + n], res[2 + n:2 + 2 * n], ns), res[-1]


def _push_wait(handle, after, name):
    send_sems, recv_sems, srcs, lands, ns = handle
    n = len(srcs)

    def body(*refs):
        for cp in _push_copies(refs[:n], refs[n:2 * n], refs[2 * n], refs[2 * n + 1], ns, False):
            cp.wait_send()
            cp.wait_recv()

    res = pl.pallas_call(
        body, name=name,
        out_shape=tuple(pltpu.HBM(a.shape, a.dtype) for a in srcs) + tuple(pltpu.HBM(a.shape, a.dtype) for a in lands),
        in_specs=[_HBM] * (2 * n) + [_SEM, _SEM, pl.BlockSpec(memory_space=pl.ANY)],
        out_specs=(_HBM,) * (2 * n),
        input_output_aliases={i: i for i in range(2 * n)},
        compiler_params=pltpu.CompilerParams(has_side_effects=pltpu.SideEffectType.DATAFLOW_SIDE_EFFECTING),
    )(*srcs, *lands, send_sems, recv_sems, after)
    return res[n:]


def _mm(a, b, *, ta=False, tb=False, out_dtype=F32, res=None, name="mm"):
    m, k = (a.shape[1], a.shape[0]) if ta else a.shape
    n = b.shape[0] if tb else b.shape[1]
    assert k == (b.shape[1] if tb else b.shape[0]), (a.shape, b.shape, ta, tb)
    tm, tn, tk = _tile(m, 1408), _tile(n, 512), _tile(k, 1408)
    nk = k // tk
    has_res = res is not None

    def body(*refs):
        a_ref, b_ref = refs[:2]
        r_ref = refs[2] if has_res else None
        o_ref = refs[3 if has_res else 2]
        part = _mxu(a_ref[...], b_ref[...], 0 if ta else 1, 1 if tb else 0)

        def finish(r):
            if has_res:
                r = r + r_ref[...].astype(F32)
            o_ref[...] = r.astype(out_dtype)

        if nk == 1:
            finish(part)
            return
        acc = refs[-1]
        kk = pl.program_id(2)

        @pl.when(kk == 0)
        def _():
            acc[...] = part

        @pl.when(kk > 0)
        def _():
            acc[...] += part

        @pl.when(kk == nk - 1)
        def _():
            finish(acc[...])

    a_spec = pl.BlockSpec((tk, tm), lambda i, j, kk: (kk, i)) if ta else pl.BlockSpec((tm, tk), lambda i, j, kk: (i, kk))
    b_spec = pl.BlockSpec((tn, tk), lambda i, j, kk: (j, kk)) if tb else pl.BlockSpec((tk, tn), lambda i, j, kk: (kk, j))
    o_spec = pl.BlockSpec((tm, tn), lambda i, j, kk: (i, j))
    in_specs = [a_spec, b_spec] + ([o_spec] if has_res else [])
    args = (a, b) + ((res,) if has_res else ())
    return pl.pallas_call(
        body, name=name, grid=(m // tm, n // tn, nk), in_specs=in_specs, out_specs=o_spec,
        out_shape=jax.ShapeDtypeStruct((m, n), out_dtype),
        scratch_shapes=[pltpu.VMEM((tm, tn), F32)] if nk > 1 else [],
        compiler_params=_cp("parallel", "parallel", "arbitrary"),
    )(*args)


def _norm_fwd(x, g, name):
    s, d = x.shape
    tr = min(512, s)

    def body(x_ref, g_ref, o_ref):
        xv = x_ref[...]
        r = lax.rsqrt(jnp.mean(xv * xv, axis=-1, keepdims=True) + EPS)
        o_ref[...] = (xv * r * g_ref[...]).astype(BF16)

    row = pl.BlockSpec((tr, d), lambda i: (i, 0))
    return pl.pallas_call(
        body, name=name, grid=(s // tr,), in_specs=[row, pl.BlockSpec((1, d), lambda i: (0, 0))],
        out_specs=row, out_shape=jax.ShapeDtypeStruct((s, d), BF16), compiler_params=_cp("parallel"),
    )(x, g.reshape(1, d))


def _norm_bwd(x, g, dh, dres, name):
    s, d = x.shape
    tr = min(512, s)

    def body(x_ref, g_ref, dh_ref, dres_ref, dx_ref, dg_ref):
        @pl.when(pl.program_id(0) == 0)
        def _():
            dg_ref[...] = jnp.zeros_like(dg_ref)

        xv = x_ref[...]
        r = lax.rsqrt(jnp.mean(xv * xv, axis=-1, keepdims=True) + EPS)
        xn = xv * r
        dhv = dh_ref[...].astype(F32)
        dg_ref[...] += jnp.sum(dhv * xn, axis=0, keepdims=True)
        dhg = dhv * g_ref[...]
        dx_ref[...] = dres_ref[...] + r * (dhg - xn * jnp.mean(dhg * xn, axis=-1, keepdims=True))

    row = pl.BlockSpec((tr, d), lambda i: (i, 0))
    vec = pl.BlockSpec((1, d), lambda i: (0, 0))
    return pl.pallas_call(
        body, name=name, grid=(s // tr,), in_specs=[row, vec, row, row], out_specs=[row, vec],
        out_shape=[jax.ShapeDtypeStruct((s, d), F32), jax.ShapeDtypeStruct((1, d), F32)],
        compiler_params=_cp("arbitrary"),
    )(x, g.reshape(1, d), dh, dres)


def _loss_head(x, g, tgt, name):
    s, d = x.shape
    tr = min(512, s)

    def body(x_ref, g_ref, t_ref, l_ref, dx_ref, dg_ref):
        @pl.when(pl.program_id(0) == 0)
        def _():
            dg_ref[...] = jnp.zeros_like(dg_ref)
            l_ref[...] = jnp.zeros_like(l_ref)

        xv = x_ref[...]
        r = lax.rsqrt(jnp.mean(xv * xv, axis=-1, keepdims=True) + EPS)
        xn = xv * r
        err = xn * g_ref[...] - t_ref[...]
        part = 0.5 * jnp.sum(jnp.mean(err * err, axis=-1, keepdims=True), axis=0, keepdims=True)
        l_ref[...] += jnp.broadcast_to(part, l_ref.shape)
        dy = err * (1.0 / d)
        dg_ref[...] += jnp.sum(dy * xn, axis=0, keepdims=True)
        dyg = dy * g_ref[...]
        dx_ref[...] = r * (dyg - xn * jnp.mean(dyg * xn, axis=-1, keepdims=True))

    row = pl.BlockSpec((tr, d), lambda i: (i, 0))
    vec = pl.BlockSpec((1, d), lambda i: (0, 0))
    return pl.pallas_call(
        body, name=name, grid=(s // tr,), in_specs=[row, vec, row],
        out_specs=[pl.BlockSpec((1, LANES), lambda i: (0, 0)), row, vec],
        out_shape=[jax.ShapeDtypeStruct((1, LANES), F32), jax.ShapeDtypeStruct((s, d), F32),
                   jax.ShapeDtypeStruct((1, d), F32)],
        compiler_params=_cp("arbitrary"),
    )(x, g.reshape(1, d), tgt)


def _adamw(landed, own, w, m, v, name):
    r, c = w.shape
    cap = max(8, 256 * 1024 // c)
    tr = max(t for t in range(8, min(r, cap) + 1, 8) if r % t == 0) if r % 8 == 0 else r
    bc1 = 1.0 - ADAM_B1 ** ADAM_STEP
    bc2 = 1.0 - ADAM_B2 ** ADAM_STEP

    def body(p_ref, o_ref, w_ref, m_ref, v_ref, g_ref, d_ref, nm_ref, nv_ref):
        me = _slot(*_mesh_pos())
        mine = o_ref[...].astype(F32)
        g = jnp.where(me == 0, mine, p_ref[0].astype(F32))
        for i in range(1, N_DEV):
            g = g + jnp.where(me == i, mine, p_ref[i].astype(F32))
        mm = ADAM_B1 * m_ref[...] + (1.0 - ADAM_B1) * g
        vv = ADAM_B2 * v_ref[...] + (1.0 - ADAM_B2) * (g * g)
        g_ref[...] = g
        nm_ref[...] = mm
        nv_ref[...] = vv
        d_ref[...] = -ADAM_LR * ((mm / bc1) / (jnp.sqrt(vv / bc2) + ADAM_EPS) + ADAM_WD * w_ref[...])

    blk = pl.BlockSpec((tr, c), lambda i: (i, 0))
    return pl.pallas_call(
        body, name=name, grid=(r // tr,),
        in_specs=[pl.BlockSpec((N_DEV, tr, c), lambda i: (0, i, 0)), blk, blk, blk, blk],
        out_specs=[blk] * 4, out_shape=[jax.ShapeDtypeStruct((r, c), F32)] * 4,
        compiler_params=_cp("parallel"),
    )(landed, own, w, m, v)


def _conv_fwd(x, w_ref, kw):
    acc = w_ref[kw - 1:kw, :] * x
    for j in range(kw - 1):
        acc = acc + w_ref[j:j + 1, :] * _shift_down(x, kw - 1 - j)
    return acc


def _conv_bwd(x, dy, w_ref, dw_ref, kw):
    dx = w_ref[kw - 1:kw, :] * dy
    dw_ref[kw - 1:kw, :] = jnp.sum(dy * x, axis=0, keepdims=True)
    for j in range(kw - 1):
        dx = dx + w_ref[j:j + 1, :] * _shift_up(dy, kw - 1 - j)
        dw_ref[j:j + 1, :] = jnp.sum(dy * _shift_down(x, kw - 1 - j), axis=0, keepdims=True)
    return dx


def _ffn_act_fwd(pre, cw, name):
    s, f2 = pre.shape
    nt = f2 // 2 // LANES

    def body(pu_ref, pg_ref, wu_ref, wg_ref, o_ref):
        up = _conv_fwd(pu_ref[...], wu_ref, FFN_CONV)
        gate = _conv_fwd(pg_ref[...], wg_ref, FFN_CONV)
        o_ref[...] = (gate * _sigmoid(gate) * up).astype(BF16)

    def col(rows, off):
        return pl.BlockSpec((rows, LANES), lambda j: (0, j + off))

    return pl.pallas_call(
        body, name=name, grid=(nt,),
        in_specs=[col(s, 0), col(s, nt), col(FFN_CONV, 0), col(FFN_CONV, nt)], out_specs=col(s, 0),
        out_shape=jax.ShapeDtypeStruct((s, f2 // 2), BF16), compiler_params=_cp("parallel"),
    )(pre, pre, cw, cw)


def _ffn_act_bwd(pre, cw, dact, name):
    s, f2 = pre.shape
    f = f2 // 2
    nt = f // LANES

    def body(pu_ref, pg_ref, wu_ref, wg_ref, da_ref, dpu_ref, dpg_ref, dwu_ref, dwg_ref):
        pu, pg = pu_ref[...], pg_ref[...]
        up = _conv_fwd(pu, wu_ref, FFN_CONV)
        gate = _conv_fwd(pg, wg_ref, FFN_CONV)
        sg = _sigmoid(gate)
        da = da_ref[...]
        dup = da * gate * sg
        dgate = da * up * (sg * (1.0 + gate * (1.0 - sg)))
        dpu_ref[...] = _conv_bwd(pu, dup, wu_ref, dwu_ref, FFN_CONV).astype(BF16)
        dpg_ref[...] = _conv_bwd(pg, dgate, wg_ref, dwg_ref, FFN_CONV).astype(BF16)

    def col(rows, off):
        return pl.BlockSpec((rows, LANES), lambda j: (0, j + off))

    return pl.pallas_call(
        body, name=name, grid=(nt,),
        in_specs=[col(s, 0), col(s, nt), col(FFN_CONV, 0), col(FFN_CONV, nt), col(s, 0)],
        out_specs=[col(s, 0), col(s, 0), col(FFN_CONV, 0), col(FFN_CONV, 0)],
        out_shape=[jax.ShapeDtypeStruct((s, f), BF16), jax.ShapeDtypeStruct((s, f), BF16),
                   jax.ShapeDtypeStruct((FFN_CONV, f), F32), jax.ShapeDtypeStruct((FFN_CONV, f), F32)],
        compiler_params=_cp("parallel"),
    )(pre, pre, cw, cw, dact)


def _xa_probs(qh, kh):
    sc = _mxu(qh, kh, 1, 1) * (XA_DH ** -0.5)
    e = jnp.exp(sc - jnp.max(sc, axis=-1, keepdims=True))
    return e / jnp.sum(e, axis=-1, keepdims=True)


def _xattn_fwd(q, kv, name):
    s, d = q.shape
    m = kv.shape[0]
    tr = min(512, s)

    def body(q_ref, kv_ref, o_ref):
        for h in range(XA_HEADS):
            lo, hi = h * XA_DH, (h + 1) * XA_DH
            p = _xa_probs(q_ref[:, lo:hi], kv_ref[:, lo:hi])
            o_ref[:, lo:hi] = _mxu(p, kv_ref[:, d + lo:d + hi]).astype(BF16)

    row = pl.BlockSpec((tr, d), lambda i: (i, 0))
    return pl.pallas_call(
        body, name=name, grid=(s // tr,), in_specs=[row, pl.BlockSpec((m, 2 * d), lambda i: (0, 0))],
        out_specs=row, out_shape=jax.ShapeDtypeStruct((s, d), BF16), compiler_params=_cp("parallel"),
    )(q, kv)


def _xattn_bwd(q, kv, do, name):
    s, d = q.shape
    m = kv.shape[0]
    tr = min(512, s)

    def body(q_ref, kv_ref, do_ref, dq_ref, dkv_ref):
        @pl.when(pl.program_id(0) == 0)
        def _():
            dkv_ref[...] = jnp.zeros_like(dkv_ref)

        for h in range(XA_HEADS):
            lo, hi = h * XA_DH, (h + 1) * XA_DH
            qh, kh, vh = q_ref[:, lo:hi], kv_ref[:, lo:hi], kv_ref[:, d + lo:d + hi]
            doh = do_ref[:, lo:hi]
            p = _xa_probs(qh, kh)
            dp = _mxu(doh, vh, 1, 1)
            ds = p * (dp - jnp.sum(p * dp, axis=-1, keepdims=True)) * (XA_DH ** -0.5)
            dq_ref[:, lo:hi] = _mxu(ds, kh).astype(BF16)
            dkv_ref[:, lo:hi] += _mxu(ds, qh, 0, 0)
            dkv_ref[:, d + lo:d + hi] += _mxu(p, doh, 0, 0)

    row = pl.BlockSpec((tr, d), lambda i: (i, 0))
    full = pl.BlockSpec((m, 2 * d), lambda i: (0, 0))
    return pl.pallas_call(
        body, name=name, grid=(s // tr,), in_specs=[row, full, row], out_specs=[row, full],
        out_shape=[jax.ShapeDtypeStruct((s, d), BF16), jax.ShapeDtypeStruct((m, 2 * d), F32)],
        compiler_params=_cp("arbitrary"),
    )(q, kv, do)


def _ret_tables():
    c = RET_CHUNK
    lg = np.log1p(-np.exp2(-5.0 - np.arange(RET_HEADS, dtype=np.float32))).astype(np.float32)
    idx = np.arange(c, dtype=np.float32)
    diff = idx[:, None] - idx[None, :]
    intra = np.where(diff >= 0, np.exp(lg[:, None, None] * np.where(diff >= 0, diff, 0.0)), 0.0)
    rk = np.broadcast_to(np.exp(lg[:, None] * (c - 1 - idx))[:, :, None], (RET_HEADS, c, LANES))
    rq = np.broadcast_to(np.exp(lg[:, None] * (idx + 1))[:, :, None], (RET_HEADS, c, LANES))
    return jnp.asarray(np.stack([intra, rk, rq], axis=1).astype(np.float32))


def _rope_tables(s):
    half = RET_DH // 2
    inv = jnp.exp(-math.log(10000.0) * jnp.arange(half, dtype=F32) / half)
    ang = jnp.arange(s, dtype=F32)[:, None] * inv[None, :]
    cos, sin = jnp.cos(ang), jnp.sin(ang)
    return jnp.concatenate([cos, cos], axis=1), jnp.concatenate([-sin, sin], axis=1)


def _ret_specs(n_of):
    c = RET_CHUNK

    def head(off):
        return pl.BlockSpec((c, RET_DH), lambda h, n: (n_of(n), h + off * RET_HEADS))

    pos = pl.BlockSpec((c, RET_DH), lambda h, n: (n_of(n), 0))
    gain = pl.BlockSpec((1, RET_DH), lambda h, n: (0, h))
    tab = pl.BlockSpec((None, 3, c, LANES), lambda h, n: (h, 0, 0, 0))
    st = pl.BlockSpec((None, None, RET_DH, RET_DH), lambda h, n: (h, n_of(n), 0, 0))
    return head, pos, gain, tab, st


def _ret_chunk(q_ref, k_ref, v_ref, cos_ref, sin_ref, tab_ref, prev):
    cos, sin = cos_ref[...], sin_ref[...]
    q = q_ref[...] * cos + pltpu.roll(q_ref[...], RET_DH // 2, 1) * sin
    k = (k_ref[...] * cos + pltpu.roll(k_ref[...], RET_DH // 2, 1) * sin) * (RET_DH ** -0.5)
    v = v_ref[...]
    scores = _mxu(q, k, 1, 1) * tab_ref[0]
    qdec = q * tab_ref[2]
    kdec = k * tab_ref[1]
    o = _mxu(scores, v) + _mxu(qdec, prev)
    return q, k, v, scores, qdec, kdec, o


def _ret_fwd(proj, cos, sin, gain, name):
    s = proj.shape[0]
    c = RET_CHUNK
    nc = s // c
    head, pos, gvec, tab, st = _ret_specs(lambda n: n)

    def body(q_ref, k_ref, v_ref, g_ref, cos_ref, sin_ref, rn_ref, tab_ref, o_ref, st_ref, state):
        @pl.when(pl.program_id(1) == 0)
        def _():
            state[...] = jnp.zeros_like(state)

        prev = state[...]
        st_ref[...] = prev
        _, _, v, _, _, kdec, o = _ret_chunk(q_ref, k_ref, v_ref, cos_ref, sin_ref, tab_ref, prev)
        state[...] = prev * tab_ref[2, c - 1:c, :] + _mxu(kdec, v, 0, 0)
        r = lax.rsqrt(jnp.mean(o * o, axis=-1, keepdims=True) + EPS)
        gate = g_ref[...]
        o_ref[...] = (o * r * rn_ref[...] * (gate * _sigmoid(gate))).astype(BF16)

    return pl.pallas_call(
        body, name=name, grid=(RET_HEADS, nc),
        in_specs=[head(0), head(1), head(2), head(3), pos, pos, gvec, tab],
        out_specs=[head(0), st],
        out_shape=[jax.ShapeDtypeStruct((s, RET_HEADS * RET_DH), BF16),
                   jax.ShapeDtypeStruct((RET_HEADS, nc, RET_DH, RET_DH), F32)],
        scratch_shapes=[pltpu.VMEM((RET_DH, RET_DH), F32)],
        compiler_params=_cp("parallel", "arbitrary"),
    )(proj, proj, proj, proj, cos, sin, gain.reshape(1, -1), _ret_tables())


def _ret_bwd(proj, cos, sin, gain, states, dmerged, name):
    s = proj.shape[0]
    c = RET_CHUNK
    nc = s // c
    head, pos, gvec, tab, st = _ret_specs(lambda n: nc - 1 - n)

    def body(q_ref, k_ref, v_ref, g_ref, cos_ref, sin_ref, rn_ref, tab_ref, st_ref, do_ref,
             dq_ref, dk_ref, dv_ref, dg_ref, drn_ref, carry):
        @pl.when(pl.program_id(1) == 0)
        def _():
            carry[...] = jnp.zeros_like(carry)
            drn_ref[...] = jnp.zeros_like(drn_ref)

        prev = st_ref[...]
        q, k, v, scores, qdec, kdec, o = _ret_chunk(q_ref, k_ref, v_ref, cos_ref, sin_ref, tab_ref, prev)
        r = lax.rsqrt(jnp.mean(o * o, axis=-1, keepdims=True) + EPS)
        on = o * r
        gate = g_ref[...]
        sg = _sigmoid(gate)
        sil = gate * sg
        dout = do_ref[...]
        rn = rn_ref[...]
        dg_ref[...] = (dout * on * rn * (sg * (1.0 + gate * (1.0 - sg)))).astype(BF16)
        drn_ref[...] += jnp.sum(dout * on * sil, axis=0, keepdims=True)
        don = dout * rn * sil
        do = r * (don - on * jnp.mean(don * on, axis=-1, keepdims=True))
        dc = carry[...]
        dsc = _mxu(do, v, 1, 1) * tab_ref[0]
        dq = _mxu(dsc, k) + _mxu(do, prev, 1, 1) * tab_ref[2]
        dk = _mxu(dsc, q, 0, 0) + _mxu(v, dc, 1, 1) * tab_ref[1]
        dv = _mxu(scores, do, 0, 0) + _mxu(kdec, dc)
        carry[...] = _mxu(qdec, do, 0, 0) + dc * tab_ref[2, c - 1:c, :]
        cos, sin = cos_ref[...], sin_ref[...]
        dk = dk * (RET_DH ** -0.5)
        dq_ref[...] = (dq * cos + pltpu.roll(dq * sin, RET_DH // 2, 1)).astype(BF16)
        dk_ref[...] = (dk * cos + pltpu.roll(dk * sin, RET_DH // 2, 1)).astype(BF16)
        dv_ref[...] = dv.astype(BF16)

    width = RET_HEADS * RET_DH
    return pl.pallas_call(
        body, name=name, grid=(RET_HEADS, nc),
        in_specs=[head(0), head(1), head(2), head(3), pos, pos, gvec, tab, st, head(0)],
        out_specs=[head(0)] * 4 + [gvec],
        out_shape=[jax.ShapeDtypeStruct((s, width), BF16)] * 4 + [jax.ShapeDtypeStruct((1, width), F32)],
        scratch_shapes=[pltpu.VMEM((RET_DH, RET_DH), F32)],
        compiler_params=_cp("parallel", "arbitrary"),
    )(proj, proj, proj, proj, cos, sin, gain.reshape(1, -1), _ret_tables(), states, dmerged)


S5_TILE = 512


def _cmul_add(xr, xi, ar, ai, yr, yi):
    return xr + ar * yr - ai * yi, xi + ar * yi + ai * yr


def _s5_scan_fwd(bu, apow, name):
    s, w2 = bu.shape
    r = SCAN_ROWS
    t = S5_TILE
    steps = r.bit_length() - 1

    def body(b_ref, p_ref, o_ref, cr, ci):
        @pl.when(pl.program_id(1) == 0)
        def _():
            cr[...] = jnp.zeros_like(cr)
            ci[...] = jnp.zeros_like(ci)

        xr, xi = b_ref[:, :t], b_ref[:, t:]
        for k in range(steps):
            sh = 1 << k
            xr, xi = _cmul_add(xr, xi, p_ref[sh - 1:sh, :t], p_ref[sh - 1:sh, t:],
                               _shift_down(xr, sh), _shift_down(xi, sh))
        xr, xi = _cmul_add(xr, xi, p_ref[:, :t], p_ref[:, t:], cr[...], ci[...])
        o_ref[:, :t] = xr
        o_ref[:, t:] = xi
        cr[...] = xr[r - 1:r, :]
        ci[...] = xi[r - 1:r, :]

    blk = pl.BlockSpec((r, 2 * t), lambda j, i: (i, j))
    return pl.pallas_call(
        body, name=name, grid=(w2 // (2 * t), s // r),
        in_specs=[blk, pl.BlockSpec((r, 2 * t), lambda j, i: (0, j))], out_specs=blk,
        out_shape=jax.ShapeDtypeStruct((s, w2), F32),
        scratch_shapes=[pltpu.VMEM((1, t), F32), pltpu.VMEM((1, t), F32)],
        compiler_params=_cp("parallel", "arbitrary"),
    )(bu, apow)


def _s5_scan_bwd(dst, apow_rev, st, name):
    s, w2 = dst.shape
    r = SCAN_ROWS
    t = S5_TILE
    nb = s // r
    steps = r.bit_length() - 1

    def body(d_ref, p_ref, s_ref, sp_ref, g_ref, da_ref, cr, ci):
        i = pl.program_id(1)

        @pl.when(i == 0)
        def _():
            cr[...] = jnp.zeros_like(cr)
            ci[...] = jnp.zeros_like(ci)
            da_ref[...] = jnp.zeros_like(da_ref)

        xr, xi = d_ref[:, :t], d_ref[:, t:]
        for k in range(steps):
            sh = 1 << k
            xr, xi = _cmul_add(xr, xi, p_ref[r - sh:r - sh + 1, :t], p_ref[r - sh:r - sh + 1, t:],
                               _shift_up(xr, sh), _shift_up(xi, sh))
        xr, xi = _cmul_add(xr, xi, p_ref[:, :t], p_ref[:, t:], cr[...], ci[...])
        g_ref[:, :t] = xr.astype(BF16)
        g_ref[:, t:] = xi.astype(BF16)
        cr[...] = xr[0:1, :]
        ci[...] = xi[0:1, :]
        first = i == nb - 1
        row = lax.broadcasted_iota(jnp.int32, (r, t), 0)
        last_r = jnp.where(first, 0.0, sp_ref[7:8, :t])
        last_i = jnp.where(first, 0.0, sp_ref[7:8, t:])
        pr = jnp.where(row == 0, last_r, pltpu.roll(s_ref[:, :t], 1, 0))
        pi = jnp.where(row == 0, last_i, pltpu.roll(s_ref[:, t:], 1, 0))
        da_ref[:, :t] += jnp.sum(xr * pr + xi * pi, axis=0, keepdims=True)
        da_ref[:, t:] += jnp.sum(xi * pr - xr * pi, axis=0, keepdims=True)

    blk = pl.BlockSpec((r, 2 * t), lambda j, i: (nb - 1 - i, j))
    halo = pl.BlockSpec((8, 2 * t), lambda j, i: (jnp.maximum((nb - 1 - i) * (r // 8) - 1, 0), j))
    vec = pl.BlockSpec((1, 2 * t), lambda j, i: (0, j))
    return pl.pallas_call(
        body, name=name, grid=(w2 // (2 * t), nb),
        in_specs=[blk, pl.BlockSpec((r, 2 * t), lambda j, i: (0, j)), blk, halo], out_specs=[blk, vec],
        out_shape=[jax.ShapeDtypeStruct((s, w2), BF16), jax.ShapeDtypeStruct((1, w2), F32)],
        scratch_shapes=[pltpu.VMEM((1, t), F32), pltpu.VMEM((1, t), F32)],
        compiler_params=_cp("parallel", "arbitrary"),
    )(dst, apow_rev, st, st)


_GELU_C = math.sqrt(2.0 / math.pi)
_GELU_A = 0.044715


def _gelu(y):
    return 0.5 * y * (1.0 + jnp.tanh(_GELU_C * (y + _GELU_A * y * y * y)))


def _gelu_grad(y):
    th = jnp.tanh(_GELU_C * (y + _GELU_A * y * y * y))
    return 0.5 * (1.0 + th) + 0.5 * y * (1.0 - th * th) * _GELU_C * (1.0 + 3.0 * _GELU_A * y * y)


def _row_call(body, name, s, ins, outs, acc=False):
    tr = min(512, s)

    def spec(width, cb, rows):
        if rows == 1:
            return pl.BlockSpec((1, width), lambda i: (0, cb))
        return pl.BlockSpec((tr, width), lambda i: (i, cb))

    in_specs = [spec(w, cb, a.shape[0]) for a, w, cb in ins]
    out_specs = [spec(w, cb, sd.shape[0]) for sd, w, cb in outs]
    return pl.pallas_call(
        body, name=name, grid=(s // tr,), in_specs=in_specs, out_specs=out_specs,
        out_shape=[sd for sd, _, _ in outs],
        compiler_params=_cp("arbitrary" if acc else "parallel"),
    )(*[a for a, _, _ in ins])


def _sds(shape, dtype):
    return jax.ShapeDtypeStruct(shape, dtype)


def _s5_gelu_fwd(yraw, proj, dvec, name):
    s, w = yraw.shape

    def body(y_ref, u_ref, d_ref, yo_ref, g_ref):
        y = y_ref[...] + d_ref[...] * u_ref[...]
        yo_ref[...] = y
        g_ref[...] = _gelu(y).astype(BF16)

    return _row_call(body, name, s, [(yraw, w, 0), (proj, w, 4), (dvec, w, 0)],
                     [(_sds((s, w), F32), w, 0), (_sds((s, w), BF16), w, 0)])


def _s5_glu_fwd(y, z, b, name):
    s, w = y.shape

    def body(y_ref, z_ref, b_ref, o_ref):
        o_ref[...] = (_gelu(y_ref[...]) * _sigmoid(z_ref[...] + b_ref[...])).astype(BF16)

    return _row_call(body, name, s, [(y, w, 0), (z, w, 0), (b, w, 0)], [(_sds((s, w), BF16), w, 0)])[0]


def _s5_glu_bwd(dmerged, y, z, b, name):
    s, w = y.shape

    def body(do_ref, y_ref, z_ref, b_ref, dz_ref, dg_ref, db_ref):
        @pl.when(pl.program_id(0) == 0)
        def _():
            db_ref[...] = jnp.zeros_like(db_ref)

        g = _gelu(y_ref[...])
        sg = _sigmoid(z_ref[...] + b_ref[...])
        dout = do_ref[...]
        dz = dout * g * sg * (1.0 - sg)
        dz_ref[...] = dz.astype(BF16)
        dg_ref[...] = dout * sg
        db_ref[...] += jnp.sum(dz, axis=0, keepdims=True)

    return _row_call(body, name, s, [(dmerged, w, 1), (y, w, 0), (z, w, 0), (b, w, 0)],
                     [(_sds((s, w), BF16), w, 0), (_sds((s, w), F32), w, 0), (_sds((1, w), F32), w, 0)], acc=True)


def _s5_gelu_bwd(dg1, dg2, y, proj, dvec, name):
    s, w = y.shape

    def body(a_ref, b_ref, y_ref, u_ref, d_ref, dy_ref, du_ref, dd_ref):
        @pl.when(pl.program_id(0) == 0)
        def _():
            dd_ref[...] = jnp.zeros_like(dd_ref)

        dy = (a_ref[...] + b_ref[...]) * _gelu_grad(y_ref[...])
        dy_ref[...] = dy.astype(BF16)
        du_ref[...] = dy * d_ref[...]
        dd_ref[...] += jnp.sum(dy * u_ref[...], axis=0, keepdims=True)

    return _row_call(body, name, s, [(dg1, w, 0), (dg2, w, 0), (y, w, 0), (proj, w, 4), (dvec, w, 0)],
                     [(_sds((s, w), BF16), w, 0), (_sds((s, w), F32), w, 0), (_sds((1, w), F32), w, 0)], acc=True)


def _gdn_conv_fwd(projx, cw, name):
    s = projx.shape[0]
    nh = GDN_HEADS

    def body(x_ref, w_ref, o_ref):
        j = pl.program_id(0)
        cv = _conv_fwd(x_ref[...], w_ref, GDN_CONV)
        y = cv * _sigmoid(cv)
        nrm = y * lax.rsqrt(jnp.sum(y * y, axis=-1, keepdims=True) + EPS)
        o_ref[...] = jnp.where(j < nh, nrm * (GDN_DH ** -0.5), jnp.where(j < 2 * nh, nrm, y))

    return pl.pallas_call(
        body, name=name, grid=(3 * nh,),
        in_specs=[pl.BlockSpec((s, GDN_DH), lambda j: (0, j)), pl.BlockSpec((GDN_CONV, GDN_DH), lambda j: (0, j))],
        out_specs=pl.BlockSpec((s, GDN_DH), lambda j: (0, j)),
        out_shape=jax.ShapeDtypeStruct((s, 3 * nh * GDN_DH), F32), compiler_params=_cp("parallel"),
    )(projx, cw)


def _gdn_conv_bwd(projx, cw, dqkv, name):
    s = projx.shape[0]
    nh = GDN_HEADS

    def body(x_ref, w_ref, d_ref, dx_ref, dw_ref):
        j = pl.program_id(0)
        x = x_ref[...]
        cv = _conv_fwd(x, w_ref, GDN_CONV)
        sg = _sigmoid(cv)
        y = cv * sg
        rinv = lax.rsqrt(jnp.sum(y * y, axis=-1, keepdims=True) + EPS)
        nrm = y * rinv
        dn = d_ref[...]
        dns = jnp.where(j < nh, dn * (GDN_DH ** -0.5), dn)
        dyn = rinv * (dns - nrm * jnp.sum(dns * nrm, axis=-1, keepdims=True))
        dy = jnp.where(j < 2 * nh, dyn, dn)
        dc = dy * (sg * (1.0 + cv * (1.0 - sg)))
        dx_ref[...] = _conv_bwd(x, dc, w_ref, dw_ref, GDN_CONV).astype(BF16)

    col = pl.BlockSpec((s, GDN_DH), lambda j: (0, j))
    wcol = pl.BlockSpec((GDN_CONV, GDN_DH), lambda j: (0, j))
    return pl.pallas_call(
        body, name=name, grid=(3 * nh,), in_specs=[col, wcol, col], out_specs=[col, wcol],
        out_shape=[jax.ShapeDtypeStruct((s, 3 * nh * GDN_DH), BF16), jax.ShapeDtypeStruct((GDN_CONV, 3 * nh * GDN_DH), F32)],
        compiler_params=_cp("parallel"),
    )(projx, cw, dqkv)


def _softplus(x):
    return jnp.maximum(x, 0.0) + jnp.log1p(jnp.exp(-jnp.abs(x)))


def _gdn_gates_fwd(projx, alog, dtb, name):
    s = projx.shape[0]
    w = GDN_HEADS * GDN_DH

    def body(b_ref, a_ref, al_ref, dt_ref, bo_ref, go_ref):
        bo_ref[...] = _sigmoid(b_ref[...])
        go_ref[...] = -jnp.exp(al_ref[...]) * _softplus(a_ref[...] + dt_ref[...])

    return _row_call(body, name, s, [(projx, w, 4), (projx, w, 5), (alog, w, 0), (dtb, w, 0)],
                     [(_sds((s, w), F32), w, 0), (_sds((s, w), F32), w, 0)])


def _gdn_gates_bwd(projx, alog, dtb, dbeta, dg, name):
    s = projx.shape[0]
    w = GDN_HEADS * GDN_DH

    def body(b_ref, a_ref, al_ref, dt_ref, dbe_ref, dg_ref, db_ref, da_ref, dal_ref, ddt_ref):
        @pl.when(pl.program_id(0) == 0)
        def _():
            dal_ref[...] = jnp.zeros_like(dal_ref)
            ddt_ref[...] = jnp.zeros_like(ddt_ref)

        for h in range(GDN_HEADS):
            lo, hi = h * GDN_DH, (h + 1) * GDN_DH
            beta = _sigmoid(b_ref[:, lo:hi])
            pb = jnp.sum(dbe_ref[:, lo:hi], axis=-1, keepdims=True) * (1.0 / GDN_DH)
            db_ref[:, lo:hi] = (pb * beta * (1.0 - beta)).astype(BF16)
            xa = a_ref[:, lo:hi] + dt_ref[:, lo:hi]
            ea = -jnp.exp(al_ref[:, lo:hi])
            pg = jnp.sum(dg_ref[:, lo:hi], axis=-1, keepdims=True) * (1.0 / GDN_DH)
            da = pg * ea * _sigmoid(xa)
            da_ref[:, lo:hi] = da.astype(BF16)
            dal_ref[:, lo:hi] += jnp.sum(pg * ea * _softplus(xa), axis=0, keepdims=True)
            ddt_ref[:, lo:hi] += jnp.sum(da, axis=0, keepdims=True)

    return _row_call(body, name, s,
                     [(projx, w, 4), (projx, w, 5), (alog, w, 0), (dtb, w, 0), (dbeta, w, 0), (dg, w, 0)],
                     [(_sds((s, w), BF16), w, 0), (_sds((s, w), BF16), w, 0),
                      (_sds((1, w), F32), w, 0), (_sds((1, w), F32), w, 0)], acc=True)


def _gdn_tri():
    c = GDN_CHUNK
    i = lax.broadcasted_iota(jnp.int32, (c, c), 0)
    j = lax.broadcasted_iota(jnp.int32, (c, c), 1)
    return ((i >= j).astype(F32), (i <= j).astype(F32), i >= j, i > j, (i == j).astype(F32))


def _bdot(a, b, ca=2, cb=1, precision=None):
    return lax.dot_general(a, b, (((ca,), (cb,)), ((0,), (0,))), precision=precision, preferred_element_type=F32)


def _bmxu(a, b, ca=2, cb=1):
    return _bdot(a.astype(BF16), b.astype(BF16), ca, cb)


def _heads(x):
    return jnp.stack([x[:, h * GDN_DH:(h + 1) * GDN_DH] for h in range(GDN_HEADS)], axis=0)


def _unheads(x):
    return jnp.concatenate([x[h] for h in range(GDN_HEADS)], axis=1)


def _gdn_chunk(q, k, v, bb, g2d, tri):
    low, up, incl, strict, eye = tri
    c = GDN_CHUNK
    gc = _heads(_dot(low, g2d, precision=HI))
    gci = gc[:, :, :c]
    gdiff = gci - jnp.swapaxes(gci, 1, 2)
    decay = jnp.where(incl, jnp.exp(jnp.where(incl, gdiff, 0.0)), 0.0)
    kb, vb = k * bb, v * bb
    kbk = _bmxu(kb, k, 2, 2)
    x = -jnp.where(strict, kbk * decay, 0.0)
    t = eye + x
    p = x
    for _ in range(c.bit_length() - 2):
        p = _bdot(p, p, precision=HI)
        t = t + _bdot(t, p, precision=HI)
    eg = jnp.exp(gc)
    kbg = kb * eg
    gcl = gc[:, c - 1:c, :]
    ek = jnp.exp(gcl - gc)
    qkraw = _bmxu(q, k, 2, 2)
    return dict(decay=decay, kb=kb, vb=vb, kbk=kbk, t=t, eg=eg, kbg=kbg, ek=ek, gl=jnp.exp(gcl),
                w=_bmxu(t, kbg), u=_bmxu(t, vb), qkraw=qkraw, qk=jnp.where(incl, qkraw * decay, 0.0),
                qd=q * eg, kd=k * ek)


def _gdn_specs(n_of):
    c, w = GDN_CHUNK, GDN_HEADS * GDN_DH

    def blk(cb, width=w):
        return pl.BlockSpec((c, width), lambda n: (n_of(n), cb))

    st = pl.BlockSpec((None, GDN_HEADS, GDN_DH, GDN_DH), lambda n: (n_of(n), 0, 0, 0))
    vec = pl.BlockSpec((1, GDN_DH), lambda n: (0, 0))
    return blk, st, vec


def _gdn_load(qkv_ref, b_ref, g_ref, tri):
    w = GDN_HEADS * GDN_DH
    q, k, v = _heads(qkv_ref[:, :w]), _heads(qkv_ref[:, w:2 * w]), _heads(qkv_ref[:, 2 * w:])
    bb = _heads(b_ref[...])
    return q, k, v, bb, _gdn_chunk(q, k, v, bb, g_ref[...], tri)


def _gdn_fwd(qkv, beta, g, projx, onorm, name):
    s = qkv.shape[0]
    nc = s // GDN_CHUNK
    w = GDN_HEADS * GDN_DH
    blk, st, vec = _gdn_specs(lambda n: n)

    def body(qkv_ref, b_ref, g_ref, z_ref, on_ref, o_ref, st_ref, state):
        @pl.when(pl.program_id(0) == 0)
        def _():
            state[...] = jnp.zeros_like(state)

        _, _, _, _, ch = _gdn_load(qkv_ref, b_ref, g_ref, _gdn_tri())
        sp = state[...]
        st_ref[...] = sp
        vn = ch["u"] - _bmxu(ch["w"], sp)
        o = _bmxu(ch["qd"], sp) + _bmxu(ch["qk"], vn)
        state[...] = sp * ch["gl"] + _bmxu(ch["kd"], vn, 1, 1)
        r = lax.rsqrt(jnp.mean(o * o, axis=-1, keepdims=True) + EPS)
        z = _heads(z_ref[...])
        o_ref[...] = _unheads(o * r * on_ref[...] * (z * _sigmoid(z))).astype(BF16)

    return pl.pallas_call(
        body, name=name, grid=(nc,),
        in_specs=[blk(0, 3 * w), blk(0), blk(0), blk(3), vec], out_specs=[blk(0), st],
        out_shape=[jax.ShapeDtypeStruct((s, w), BF16), jax.ShapeDtypeStruct((nc, GDN_HEADS, GDN_DH, GDN_DH), F32)],
        scratch_shapes=[pltpu.VMEM((GDN_HEADS, GDN_DH, GDN_DH), F32)],
        compiler_params=_cp("arbitrary"),
    )(qkv, beta, g, projx, onorm.reshape(1, -1))


def _gdn_bwd(qkv, beta, g, projx, onorm, states, dout, name):
    s = qkv.shape[0]
    c = GDN_CHUNK
    nc = s // c
    w = GDN_HEADS * GDN_DH
    blk, st, vec = _gdn_specs(lambda n: nc - 1 - n)

    def body(qkv_ref, b_ref, g_ref, z_ref, on_ref, st_ref, do_ref,
             dqkv_ref, db_ref, dg_ref, dz_ref, don_ref, carry):
        @pl.when(pl.program_id(0) == 0)
        def _():
            carry[...] = jnp.zeros_like(carry)
            don_ref[...] = jnp.zeros_like(don_ref)

        tri = _gdn_tri()
        low, up, incl, strict, eye = tri
        q, k, v, bb, ch = _gdn_load(qkv_ref, b_ref, g_ref, tri)
        sp = st_ref[...]
        vn = ch["u"] - _bmxu(ch["w"], sp)
        o = _bmxu(ch["qd"], sp) + _bmxu(ch["qk"], vn)
        r = lax.rsqrt(jnp.mean(o * o, axis=-1, keepdims=True) + EPS)
        orn = o * r
        z = _heads(z_ref[...])
        sg = _sigmoid(z)
        dout = _heads(do_ref[...])
        onw = on_ref[...]
        dz_ref[...] = _unheads(dout * orn * onw * (sg * (1.0 + z * (1.0 - sg)))).astype(BF16)
        don = dout * (z * sg)
        don_ref[...] += jnp.sum(jnp.sum(don * orn, axis=0), axis=0, keepdims=True)
        dor = don * onw
        do = r * (dor - orn * jnp.mean(dor * orn, axis=-1, keepdims=True))
        dsn = carry[...]
        dqd = _bmxu(do, sp, 2, 2)
        dqk = jnp.where(incl, _bmxu(do, vn, 2, 2), 0.0)
        dvn = _bmxu(ch["qk"], do, 1, 1) + _bmxu(ch["kd"], dsn)
        dkd = _bmxu(vn, dsn, 2, 2)
        dgl = jnp.sum(dsn * sp, axis=1, keepdims=True)
        dw = -_bmxu(dvn, sp, 2, 2)
        carry[...] = _bmxu(ch["qd"], do, 1, 1) + dsn * ch["gl"] - _bmxu(ch["w"], dvn, 1, 1)
        t = ch["t"]
        dvb = _bmxu(t, dvn, 1, 1)
        dkbg = _bmxu(t, dw, 1, 1)
        dt = _bmxu(dvn, ch["vb"], 2, 2) + _bmxu(dw, ch["kbg"], 2, 2)
        da = -_bdot(_bdot(t, dt, 1, 1, precision=HI), t, 2, 2, precision=HI)
        da = jnp.where(strict, da, 0.0)
        decay = ch["decay"]
        dkbk = da * decay
        dqkr = dqk * decay
        mdec = (da * ch["kbk"] + dqk * ch["qkraw"]) * decay
        dkb = _bmxu(dkbk, k) + dkbg * ch["eg"]
        dk = _bmxu(dkbk, ch["kb"], 1, 1) + _bmxu(dqkr, q, 1, 1) + dkd * ch["ek"] + dkb * bb
        dq = _bmxu(dqkr, k) + dqd * ch["eg"]
        tk = dkd * ch["kd"]
        dgcl = jnp.sum(tk, axis=1, keepdims=True) + dgl * ch["gl"]
        row = lax.broadcasted_iota(jnp.int32, (GDN_HEADS, c, GDN_DH), 1)
        zpad = jnp.zeros((GDN_HEADS, c, GDN_DH - c), F32)
        dgc = (jnp.concatenate([mdec, zpad], axis=2) - jnp.concatenate([jnp.swapaxes(mdec, 1, 2), zpad], axis=2)
               + dqd * ch["qd"] - tk + dkbg * ch["kbg"] + jnp.where(row == c - 1, dgcl, 0.0))
        dqkv_ref[:, :w] = _unheads(dq)
        dqkv_ref[:, w:2 * w] = _unheads(dk)
        dqkv_ref[:, 2 * w:] = _unheads(dvb * bb)
        db_ref[...] = _unheads(dkb * k + dvb * v)
        dg_ref[...] = _dot(up, _unheads(dgc), precision=HI)

    return pl.pallas_call(
        body, name=name, grid=(nc,),
        in_specs=[blk(0, 3 * w), blk(0), blk(0), blk(3), vec, st, blk(0)],
        out_specs=[blk(0, 3 * w), blk(0), blk(0), blk(0), vec],
        out_shape=[jax.ShapeDtypeStruct((s, 3 * w), F32), jax.ShapeDtypeStruct((s, w), F32),
                   jax.ShapeDtypeStruct((s, w), F32), jax.ShapeDtypeStruct((s, w), BF16),
                   jax.ShapeDtypeStruct((1, GDN_DH), F32)],
        scratch_shapes=[pltpu.VMEM((GDN_HEADS, GDN_DH, GDN_DH), F32)],
        compiler_params=_cp("arbitrary"),
    )(qkv, beta, g, projx, onorm.reshape(1, -1), states, dout)


_WEIGHTS = (
    "l0_mix_norm", "l0_w_in", "l0_ret_norm", "l0_s5_lambda_re", "l0_s5_lambda_im", "l0_s5_b_re", "l0_s5_b_im",
    "l0_s5_c_re", "l0_s5_c_im", "l0_s5_d", "l0_s5_log_dt", "l0_s5_w_glu", "l0_s5_b_glu", "l0_w_out",
    "l0_xa_norm", "l0_mem_norm", "l0_xa_wq", "l0_xa_wkv", "l0_xa_wo", "l0_ffn_norm", "l0_ffn_w_up",
    "l0_ffn_conv", "l0_ffn_w_down", "l1_mix_norm", "l1_w_in", "l1_conv", "l1_a_log", "l1_dt_bias", "l1_o_norm",
    "l1_w_out", "l1_xa_norm", "l1_mem_norm", "l1_xa_wq", "l1_xa_wkv", "l1_xa_wo", "l1_ffn_norm", "l1_ffn_w_up",
    "l1_ffn_conv", "l1_ffn_w_down", "final_norm")
_INPUTS = ("x", "mem") + _WEIGHTS + ("loss_target",) + tuple("m_" + n for n in _WEIGHTS) + tuple("v_" + n for n in _WEIGHTS)

_COL = ("l0_w_in", "l0_xa_wkv", "l0_ffn_w_up", "l0_ffn_conv", "l1_w_in", "l1_conv", "l1_xa_wkv", "l1_ffn_w_up",
        "l1_ffn_conv")
_ROW = ("l0_s5_w_glu", "l0_w_out", "l0_xa_wq", "l0_xa_wo", "l0_ffn_w_down", "l1_w_out", "l1_xa_wq", "l1_xa_wo",
        "l1_ffn_w_down")
_F32_WIRE = ("l0_ffn_conv", "l1_conv", "l1_ffn_conv")
_REP = tuple(n for n in _WEIGHTS if n not in _COL + _ROW)
_GATHER_GROUPS = (("l0_w_in", "l0_s5_w_glu", "l0_w_out"),
                  ("l0_xa_wq", "l0_xa_wkv", "l0_xa_wo", "l0_ffn_w_up", "l0_ffn_conv", "l0_ffn_w_down"),
                  ("l1_w_in", "l1_conv", "l1_w_out", "l1_xa_wq", "l1_xa_wkv", "l1_xa_wo", "l1_ffn_w_up", "l1_ffn_conv",
                   "l1_ffn_w_down"))


def _round_up(n, m):
    return (n + m - 1) // m * m


def _pack_rep(ts):
    a = jnp.concatenate([t.reshape(-1) for t in ts])
    return jnp.pad(a, (0, _round_up(a.shape[0], 8 * LANES) - a.shape[0])).reshape(-1, LANES)


def _s5_interleave(re, im):
    lead = re.shape[:-1]
    nt = re.shape[-1] // S5_TILE
    both = jnp.stack([re.reshape(lead + (nt, S5_TILE)), im.reshape(lead + (nt, S5_TILE))], axis=-2)
    return both.reshape(lead + (2 * re.shape[-1],))


def _s5_split(x):
    lead = x.shape[:-1]
    y = x.reshape(lead + (x.shape[-1] // (2 * S5_TILE), 2, S5_TILE))
    return y[..., 0, :].reshape(lead + (-1,)), y[..., 1, :].reshape(lead + (-1,))


def _s5_discretise(lr, li, log_dt, b_re, b_im):
    dt = jnp.exp(log_dt)[:, None]
    mag = jnp.exp(lr * dt)
    a_re = mag * jnp.cos(li * dt)
    a_im = mag * jnp.sin(li * dt)
    den = lr * lr + li * li
    z_re = ((a_re - 1.0) * lr + a_im * li) / den
    z_im = (a_im * lr - (a_re - 1.0) * li) / den
    bb_re = z_re[:, None, :] * b_re - z_im[:, None, :] * b_im
    bb_im = z_re[:, None, :] * b_im + z_im[:, None, :] * b_re
    return a_re, a_im, bb_re, bb_im


def _pow_tables(a_re, a_im, rows):
    ur, ui = a_re[None], a_im[None]
    dr, di = ur, ui
    while ur.shape[0] < rows:
        lr, li = ur[-1:], ui[-1:]
        ur, ui = (jnp.concatenate([ur, ur * lr - ui * li]), jnp.concatenate([ui, ur * li + ui * lr]))
        dr, di = (jnp.concatenate([dr * lr - di * li, dr]), jnp.concatenate([dr * li + di * lr, di]))
    return (ur, ui), (dr, di)


def _block_diag(b):
    g, r, c = b.shape
    return jnp.einsum("grc,gk->grkc", b, jnp.eye(g, dtype=b.dtype)).reshape(g * r, g * c)


def _block_diag_of(d, g):
    r, c = d.shape[0] // g, d.shape[1] // g
    return jnp.einsum("grkc,gk->grc", d.reshape(g, r, g, c), jnp.eye(g, dtype=d.dtype))


def kernel(*args):
    p = dict(zip(_INPUTS, args, strict=True))
    x0, mem0, tgt = p["x"][0], p["mem"][0], p["loss_target"][0]
    s, d = x0.shape
    me = _slot(*_mesh_pos())
    grads = {}
    wire = {n: (F32 if n in _F32_WIRE else BF16) for n in _COL + _ROW}

    shards = {n: p[n].astype(wire[n]) for n in _COL + _ROW}
    gather, pin = [], jnp.zeros((), F32)
    for i, names in enumerate(_GATHER_GROUPS):
        handle, token = _push_start([], [shards[n] for n in names], f"gather{i}_start")
        gather.append(handle)
        pin = pin + token[0, 0]
    w = {}

    def gathered(i, after):
        for n, land in zip(_GATHER_GROUPS[i], _push_wait(gather[i], after, f"gather{i}_wait")):
            full = lax.dynamic_update_index_in_dim(land, shards[n], me, 0)
            if n in _COL:
                full = full.transpose(1, 0, 2)
            w[n] = full.reshape(-1, full.shape[-1]) if n in _ROW else full.reshape(full.shape[0], -1)

    pending = []

    def exchange(names, gain, tag):
        slots = []
        for n in names:
            g = grads[n]
            if n in _COL:
                g = g.reshape(g.shape[0], N_DEV, -1).transpose(1, 0, 2)
            else:
                g = g.reshape((N_DEV, -1) + g.shape[1:])
            slots.append(g.astype(wire[n]))
        handle, token = _push_start(slots, [], tag + "_start")
        pending.append((names, slots, handle, tag))
        return gain + token[0, 0]

    def xattn(pre, x_in):
        hx = _norm_fwd(x_in, p[pre + "xa_norm"], pre + "xa_norm_fwd")
        q = _mm(hx, w[pre + "xa_wq"], out_dtype=BF16, name=pre + "xa_q")
        memn = _norm_fwd(mem0, p[pre + "mem_norm"], pre + "mem_norm_fwd")
        kv = _mm(memn, w[pre + "xa_wkv"], out_dtype=BF16, name=pre + "xa_kv")
        ao = _xattn_fwd(q, kv, pre + "xattn_fwd")
        x_out = _mm(ao, w[pre + "xa_wo"], res=x_in, name=pre + "xa_o")
        return x_out, (x_in, hx, q, memn, kv, ao)

    def xattn_bwd(pre, saved, dxo):
        x_in, hx, q, memn, kv, ao = saved
        dao = _mm(dxo, w[pre + "xa_wo"], tb=True, name=pre + "xa_o_dx")
        grads[pre + "xa_wo"] = _mm(ao, dxo, ta=True, out_dtype=BF16, name=pre + "xa_o_dw")
        dq, dkv = _xattn_bwd(q, kv, dao, pre + "xattn_bwd")
        grads[pre + "xa_wq"] = _mm(hx, dq, ta=True, out_dtype=BF16, name=pre + "xa_q_dw")
        dhx = _mm(dq, w[pre + "xa_wq"], tb=True, name=pre + "xa_q_dx")
        grads[pre + "xa_wkv"] = _mm(memn, dkv, ta=True, out_dtype=BF16, name=pre + "xa_kv_dw")
        dmemn = _mm(dkv, w[pre + "xa_wkv"], tb=True, name=pre + "xa_kv_dx")
        gain = exchange((pre + "xa_wo", pre + "xa_wq", pre + "xa_wkv"), p[pre + "xa_norm"], pre + "xa_grads")
        dx_in, grads[pre + "xa_norm"] = _norm_bwd(x_in, gain, dhx, dxo, pre + "xa_norm_bwd")
        _, grads[pre + "mem_norm"] = _norm_bwd(mem0, p[pre + "mem_norm"], dmemn, jnp.zeros_like(mem0), pre + "mem_norm_bwd")
        return dx_in

    def ffn(pre, x_in):
        hf = _norm_fwd(x_in, p[pre + "ffn_norm"], pre + "ffn_norm_fwd")
        up = _mm(hf, w[pre + "ffn_w_up"], name=pre + "ffn_up")
        act = _ffn_act_fwd(up, w[pre + "ffn_conv"], pre + "ffn_act_fwd")
        x_out = _mm(act, w[pre + "ffn_w_down"], res=x_in, name=pre + "ffn_down")
        return x_out, (x_in, hf, up, act)

    def ffn_bwd(pre, saved, dxo):
        x_in, hf, up, act = saved
        dact = _mm(dxo, w[pre + "ffn_w_down"], tb=True, name=pre + "ffn_down_dx")
        grads[pre + "ffn_w_down"] = _mm(act, dxo, ta=True, out_dtype=BF16, name=pre + "ffn_down_dw")
        dpu, dpg, dcu, dcg = _ffn_act_bwd(up, w[pre + "ffn_conv"], dact, pre + "ffn_act_bwd")
        dup = jnp.concatenate([dpu, dpg], axis=1)
        grads[pre + "ffn_conv"] = jnp.concatenate([dcu, dcg], axis=1)
        dhf = _mm(dup, w[pre + "ffn_w_up"], tb=True, name=pre + "ffn_up_dx")
        grads[pre + "ffn_w_up"] = _mm(hf, dup, ta=True, out_dtype=BF16, name=pre + "ffn_up_dw")
        gain = exchange((pre + "ffn_w_down", pre + "ffn_w_up", pre + "ffn_conv"), p[pre + "ffn_norm"], pre + "ffn_grads")
        dx_in, grads[pre + "ffn_norm"] = _norm_bwd(x_in, gain, dhf, dxo, pre + "ffn_norm_bwd")
        return dx_in

    cos, sin = _rope_tables(s)
    (a_re, a_im, bb_re, bb_im), disc_vjp = jax.vjp(
        _s5_discretise, p["l0_s5_lambda_re"], p["l0_s5_lambda_im"], p["l0_s5_log_dt"], p["l0_s5_b_re"], p["l0_s5_b_im"])
    (pu_re, pu_im), (pd_re, pd_im) = _pow_tables(a_re.reshape(-1), a_im.reshape(-1), SCAN_ROWS)
    apow = _s5_interleave(pu_re, pu_im)
    apow_rev = _s5_interleave(pd_re, -pd_im)
    bbig = _s5_interleave(_block_diag(bb_re), _block_diag(bb_im)).astype(BF16)
    cbig = _s5_interleave(_block_diag(p["l0_s5_c_re"]).T, -_block_diag(p["l0_s5_c_im"]).T).T.astype(BF16)
    s5_d = p["l0_s5_d"].reshape(1, -1)
    b_glu = p["l0_s5_b_glu"].reshape(1, -1)

    h0 = _norm_fwd(x0, p["l0_mix_norm"] + pin, "l0_mix_norm_fwd")
    gathered(0, h0)
    proj = _mm(h0, w["l0_w_in"], name="l0_in")
    o_ret, ret_states = _ret_fwd(proj, cos, sin, p["l0_ret_norm"], "l0_ret_fwd")
    u = proj[:, 4 * RET_HEADS * RET_DH:]
    bu = _mm(u, bbig, name="l0_s5_bu")
    st = _s5_scan_fwd(bu, apow, "l0_s5_scan_fwd")
    yraw = _mm(st, cbig, name="l0_s5_c")
    y, gy = _s5_gelu_fwd(yraw, proj, s5_d, "l0_s5_gelu_fwd")
    z = _mm(gy, w["l0_s5_w_glu"], name="l0_s5_glu_mm")
    y2 = _s5_glu_fwd(y, z, b_glu, "l0_s5_glu_fwd")
    merged = jnp.concatenate([o_ret, y2], axis=1)
    x1 = _mm(merged, w["l0_w_out"], res=x0, name="l0_out")
    gathered(1, x1)
    x2, xa0 = xattn("l0_", x1)
    x3, ff0 = ffn("l0_", x2)

    gathered(2, x3)
    nqkv = 4 * GDN_HEADS * GDN_DH
    w1 = w["l1_w_in"]
    wx = jnp.concatenate([w1[:, :nqkv], jnp.repeat(w1[:, nqkv:nqkv + GDN_HEADS], GDN_DH, axis=1),
                          jnp.repeat(w1[:, nqkv + GDN_HEADS:], GDN_DH, axis=1)], axis=1)
    alog_x = jnp.repeat(p["l1_a_log"], GDN_DH).reshape(1, -1)
    dtb_x = jnp.repeat(p["l1_dt_bias"], GDN_DH).reshape(1, -1)
    h1 = _norm_fwd(x3, p["l1_mix_norm"], "l1_mix_norm_fwd")
    projx = _mm(h1, wx, name="l1_in")
    qkv = _gdn_conv_fwd(projx, w["l1_conv"], "l1_conv_fwd")
    beta, glog = _gdn_gates_fwd(projx, alog_x, dtb_x, "l1_gates_fwd")
    o_gdn, gdn_states = _gdn_fwd(qkv, beta, glog, projx, p["l1_o_norm"], "l1_gdn_fwd")
    x4 = _mm(o_gdn, w["l1_w_out"], res=x3, name="l1_out")
    x5, xa1 = xattn("l1_", x4)
    x6, ff1 = ffn("l1_", x5)

    loss_part, dx6, grads["final_norm"] = _loss_head(x6, p["final_norm"], tgt, "loss_head")
    loss = lax.psum(loss_part[0, 0], ("x", "y", "c"))
    dx5 = ffn_bwd("l1_", ff1, dx6)
    dx4 = xattn_bwd("l1_", xa1, dx5)

    do_gdn = _mm(dx4, w["l1_w_out"], tb=True, name="l1_out_dx")
    grads["l1_w_out"] = _mm(o_gdn, dx4, ta=True, out_dtype=BF16, name="l1_out_dw")
    dqkv, dbeta, dglog, dz, grads["l1_o_norm"] = _gdn_bwd(
        qkv, beta, glog, projx, p["l1_o_norm"], gdn_states, do_gdn, "l1_gdn_bwd")
    dpre, grads["l1_conv"] = _gdn_conv_bwd(projx, w["l1_conv"], dqkv, "l1_conv_bwd")
    db, da, dalog_x, ddtb_x = _gdn_gates_bwd(projx, alog_x, dtb_x, dbeta, dglog, "l1_gates_bwd")
    dprojx = jnp.concatenate([dpre, dz, db, da], axis=1)
    dh1 = _mm(dprojx, wx, tb=True, name="l1_in_dx")
    dwx = _mm(h1, dprojx, ta=True, name="l1_in_dw")
    grads["l1_w_in"] = jnp.concatenate(
        [dwx[:, :nqkv], dwx[:, nqkv:nqkv + GDN_HEADS * GDN_DH].reshape(d, GDN_HEADS, GDN_DH).sum(-1),
         dwx[:, nqkv + GDN_HEADS * GDN_DH:].reshape(d, GDN_HEADS, GDN_DH).sum(-1)], axis=1)
    grads["l1_a_log"] = dalog_x.reshape(GDN_HEADS, GDN_DH).sum(-1)
    grads["l1_dt_bias"] = ddtb_x.reshape(GDN_HEADS, GDN_DH).sum(-1)
    gain = exchange(("l1_w_out", "l1_w_in", "l1_conv"), p["l1_mix_norm"], "l1_mix_grads")
    dx3, grads["l1_mix_norm"] = _norm_bwd(x3, gain, dh1, dx4, "l1_mix_norm_bwd")

    dx2 = ffn_bwd("l0_", ff0, dx3)
    dx1 = xattn_bwd("l0_", xa0, dx2)

    dmerged = _mm(dx1, w["l0_w_out"], tb=True, name="l0_out_dx")
    grads["l0_w_out"] = _mm(merged, dx1, ta=True, out_dtype=BF16, name="l0_out_dw")
    drq, drk, drv, drg, grads["l0_ret_norm"] = _ret_bwd(proj, cos, sin, p["l0_ret_norm"], ret_states, dmerged, "l0_ret_bwd")
    dzg, dg1, grads["l0_s5_b_glu"] = _s5_glu_bwd(dmerged, y, z, b_glu, "l0_s5_glu_bwd")
    grads["l0_s5_w_glu"] = _mm(gy, dzg, ta=True, out_dtype=BF16, name="l0_s5_glu_dw")
    dg2 = _mm(dzg, w["l0_s5_w_glu"], tb=True, name="l0_s5_glu_dx")
    dyraw, du_dir, grads["l0_s5_d"] = _s5_gelu_bwd(dg1, dg2, y, proj, s5_d, "l0_s5_gelu_bwd")
    dst = _mm(dyraw, cbig, tb=True, name="l0_s5_c_dx")
    dcbig = _mm(st, dyraw, ta=True, name="l0_s5_c_dw")
    gsc, da_s5 = _s5_scan_bwd(dst, apow_rev, st, "l0_s5_scan_bwd")
    du = _mm(gsc, bbig, tb=True, res=du_dir, out_dtype=BF16, name="l0_s5_bu_dx")
    dbbig = _mm(u, gsc, ta=True, name="l0_s5_bu_dw")
    dproj = jnp.concatenate([drq, drk, drv, drg, du], axis=1)
    dh0 = _mm(dproj, w["l0_w_in"], tb=True, name="l0_in_dx")
    grads["l0_w_in"] = _mm(h0, dproj, ta=True, out_dtype=BF16, name="l0_in_dw")
    gain = exchange(("l0_w_out", "l0_s5_w_glu", "l0_w_in"), p["l0_mix_norm"], "l0_mix_grads")
    dx0, grads["l0_mix_norm"] = _norm_bwd(x0, gain, dh0, dx1, "l0_mix_norm_bwd")

    dbb_re, dbb_im = (_block_diag_of(t, S5_GROUPS) for t in _s5_split(dbbig))
    dct_re, dct_im = _s5_split(dcbig.T)
    grads["l0_s5_c_re"] = _block_diag_of(dct_re.T, S5_GROUPS)
    grads["l0_s5_c_im"] = -_block_diag_of(dct_im.T, S5_GROUPS)
    da_re, da_im = (t.reshape(S5_GROUPS, S5_STATE) for t in _s5_split(da_s5[0]))
    (grads["l0_s5_lambda_re"], grads["l0_s5_lambda_im"], grads["l0_s5_log_dt"], grads["l0_s5_b_re"],
     grads["l0_s5_b_im"]) = disc_vjp((da_re, da_im, dbb_re, dbb_im))

    rep_own = _pack_rep([grads[n].reshape(p[n].shape) for n in _REP])
    rep_handle, _ = _push_start([], [rep_own], "rep_grads_start")
    rep_land, = _push_wait(rep_handle, dx0, "rep_grads_wait")

    outs = {}
    kinds = ("grad_", "delta_", "new_m_", "new_v_")
    for names, slots, handle, tag in pending:
        for n, own_slots, land in zip(names, slots, _push_wait(handle, rep_land, tag + "_wait")):
            shape = p[n].shape
            own = lax.dynamic_index_in_dim(own_slots, me, 0, keepdims=False)
            res = _adamw(land, own, *(p[pre + n].reshape(own.shape) for pre in ("", "m_", "v_")), "adamw_" + n)
            for kind, t in zip(kinds, res):
                outs[kind + n] = t.reshape(shape)
    res = _adamw(rep_land, rep_own, *(_pack_rep([p[pre + n] for n in _REP]) for pre in ("", "m_", "v_")), "adamw_rep")
    off = 0
    flat = [t.reshape(-1) for t in res]
    for n in _REP:
        size = math.prod(p[n].shape)
        for kind, t in zip(kinds, flat):
            outs[kind + n] = t[off:off + size].reshape(p[n].shape)
        off += size

    return (loss, dx0[None]) + tuple(outs[kind + n] for kind in kinds for n in _WEIGHTS)
```

```python
import functools
import math

import numpy as np
import jax
import jax.numpy as jnp
from jax import lax
from jax.experimental import pallas as pl
from jax.experimental.pallas import tpu as pltpu

F32 = jnp.float32
BF16 = jnp.bfloat16
EPS = 1e-6
N_DEV = 8
LANES = 128
VMEM_LIMIT = 48 * 1024 * 1024
HI = lax.Precision.HIGHEST

RET_HEADS, RET_DH, RET_CHUNK = 4, 128, 128
S5_GROUPS, S5_GROUP, S5_STATE = 32, 16, 64
GDN_HEADS, GDN_DH, GDN_CHUNK, GDN_CONV = 8, 128, 64, 4
XA_HEADS, XA_DH = 4, 256
FFN_CONV = 3
SCAN_ROWS = 256

ADAM_LR, ADAM_B1, ADAM_B2, ADAM_EPS, ADAM_WD, ADAM_STEP = 0.001, 0.9, 0.999, 1e-08, 0.01, 10


def _cp(*sem):
    return pltpu.CompilerParams(dimension_semantics=sem if sem else None, vmem_limit_bytes=VMEM_LIMIT)


def _tile(n, cap):
    if n <= cap:
        return n
    best = None
    for t in range(LANES, cap + 1, LANES):
        if n % t == 0:
            best = t
    assert best is not None, n
    return best


def _dot(a, b, ca=1, cb=0, precision=None):
    return lax.dot_general(a, b, (((ca,), (cb,)), ((), ())), precision=precision, preferred_element_type=F32)


def _mxu(a, b, ca=1, cb=0):
    return _dot(a.astype(BF16), b.astype(BF16), ca, cb)


def _sigmoid(x):
    return 1.0 / (1.0 + jnp.exp(-x))


def _shift_down(x, k):
    row = lax.broadcasted_iota(jnp.int32, x.shape, 0)
    return jnp.where(row >= k, pltpu.roll(x, k, 0), 0.0)


def _shift_up(x, k):
    n = x.shape[0]
    row = lax.broadcasted_iota(jnp.int32, x.shape, 0)
    return jnp.where(row < n - k, pltpu.roll(x, n - k, 0), 0.0)


def _mesh_pos():
    return lax.axis_index("x"), lax.axis_index("y"), lax.axis_index("c")


def _slot(px, py, pc):
    return 4 * px + 2 * py + pc


def _all_peers(x, y, c):
    flips = [(fx, fy, fc) for fx in (0, 1) for fy in (0, 1) for fc in (0, 1)][1:]
    return [(1 - x if fx else x, 1 - y if fy else y, 1 - c if fc else c) for fx, fy, fc in flips]


_HBM = pl.BlockSpec(memory_space=pltpu.HBM)
_SEM = pl.BlockSpec(memory_space=pltpu.SEMAPHORE)
N_PEERS = N_DEV - 1


def _push_copies(srcs, lands, send_sems, recv_sems, ns, start):
    x, y, c = _mesh_pos()
    me = _slot(x, y, c)
    out = []
    for k, to in enumerate(_all_peers(x, y, c)):
        for a in range(len(srcs)):
            src = srcs[a].at[_slot(*to)] if a < ns else srcs[a]
            dst = lands[a].at[me if start else _slot(*to)]
            out.append(pltpu.make_async_remote_copy(
                src_ref=src, dst_ref=dst, send_sem=send_sems.at[a * N_PEERS + k], recv_sem=recv_sems.at[a * N_PEERS + k],
                device_id=to, device_id_type=pl.DeviceIdType.MESH))
    return out


def _place_own(arrs, name):
    n = len(arrs)
    anyspace = pl.BlockSpec(memory_space=pl.ANY)

    def body(*refs):
        me = _slot(*_mesh_pos())
        copies = [pltpu.make_async_copy(refs[a], refs[n + a].at[me], refs[2 * n].at[a]) for a in range(n)]
        for cp in copies:
            cp.start()
        for cp in copies:
            cp.wait()

    return pl.pallas_call(
        body, name=name, out_shape=[jax.ShapeDtypeStruct((N_DEV,) + a.shape, a.dtype) for a in arrs],
        in_specs=[anyspace] * n, out_specs=[anyspace] * n, scratch_shapes=[pltpu.SemaphoreType.DMA((n,))],
    )(*arrs)


def _push_start(scatter, gather, name, gather_lands=None):
    ns = len(scatter)
    arrs = list(scatter) + list(gather)
    n = len(arrs)
    land_shapes = [a.shape for a in scatter] + [(N_DEV,) + a.shape for a in gather]
    lands = [lax.empty(s, a.dtype) for s, a in zip(land_shapes[:ns], scatter)]
    lands += list(gather_lands) if gather_lands is not None else [lax.empty(s, a.dtype) for s, a in zip(land_shapes[ns:], gather)]

    def body(*refs):
        srcs, lands = refs[:n], refs[n:2 * n]
        send_sems, recv_sems = refs[2 * n], refs[2 * n + 1]
        for cp in _push_copies(srcs, lands, send_sems, recv_sems, ns, True):
            cp.start()
        refs[-1][...] = jnp.zeros((8, LANES), F32)

    hbm_in = [pltpu.with_memory_space_constraint(a, pltpu.HBM) for a in arrs]
    hbm_in += [pltpu.with_memory_space_constraint(z, pltpu.HBM) for z in lands]
    res = pl.pallas_call(
        body, name=name,
        out_shape=(pltpu.SemaphoreType.DMA((n * N_PEERS,)), pltpu.SemaphoreType.DMA((n * N_PEERS,)))
        + tuple(pltpu.HBM(a.shape, a.dtype) for a in arrs)
        + tuple(pltpu.HBM(s, a.dtype) for s, a in zip(land_shapes, arrs))
        + (jax.ShapeDtypeStruct((8, LANES), F32),),
        in_specs=[_HBM] * (2 * n),
        out_specs=(_SEM, _SEM) + (_HBM,) * (2 * n) + (pl.BlockSpec(memory_space=pltpu.VMEM),),
        input_output_aliases={i: 2 + i for i in range(2 * n)},
        compiler_params=pltpu.CompilerParams(has_side_effects=pltpu.SideEffectType.DATAFLOW_SIDE_EFFECTING),
    )(*hbm_in)
    return (res[0], res[1], res[2:2 + n], res[2 + n:2 + 2 * n], ns), res[-1]


def _push_wait(handle, after, name):
    send_sems, recv_sems, srcs, lands, ns = handle
    n = len(srcs)

    def body(*refs):
        for cp in _push_copies(refs[:n], refs[n:2 * n], refs[2 * n], refs[2 * n + 1], ns, False):
            cp.wait_send()
            cp.wait_recv()

    res = pl.pallas_call(
        body, name=name,
        out_shape=tuple(pltpu.HBM(a.shape, a.dtype) for a in srcs) + tuple(pltpu.HBM(a.shape, a.dtype) for a in lands),
        in_specs=[_HBM] * (2 * n) + [_SEM, _SEM, pl.BlockSpec(memory_space=pl.ANY)],
        out_specs=(_HBM,) * (2 * n),
        input_output_aliases={i: i for i in range(2 * n)},
        compiler_params=pltpu.CompilerParams(has_side_effects=pltpu.SideEffectType.DATAFLOW_SIDE_EFFECTING),
    )(*srcs, *lands, send_sems, recv_sems, after)
    return res[n:]


def _mm(a, b, *, ta=False, tb=False, out_dtype=F32, res=None, name="mm"):
    m, k = (a.shape[1], a.shape[0]) if ta else a.shape
    n = b.shape[0] if tb else b.shape[1]
    assert k == (b.shape[1] if tb else b.shape[0]), (a.shape, b.shape, ta, tb)
    tm, tn, tk = _tile(m, 1408), _tile(n, 1536), _tile(k, 1408)
    nk = k // tk
    has_res = res is not None

    def body(*refs):
        a_ref, b_ref = refs[:2]
        r_ref = refs[2] if has_res else None
        o_ref = refs[3 if has_res else 2]
        part = _mxu(a_ref[...], b_ref[...], 0 if ta else 1, 1 if tb else 0)

        def finish(r):
            if has_res:
                r = r + r_ref[...].astype(F32)
            o_ref[...] = r.astype(out_dtype)

        if nk == 1:
            finish(part)
            return
        acc = refs[-1]
        kk = pl.program_id(2)

        @pl.when(kk == 0)
        def _():
            acc[...] = part

        @pl.when(kk > 0)
        def _():
            acc[...] += part

        @pl.when(kk == nk - 1)
        def _():
            finish(acc[...])

    a_spec = pl.BlockSpec((tk, tm), lambda i, j, kk: (kk, i)) if ta else pl.BlockSpec((tm, tk), lambda i, j, kk: (i, kk))
    b_spec = pl.BlockSpec((tn, tk), lambda i, j, kk: (j, kk)) if tb else pl.BlockSpec((tk, tn), lambda i, j, kk: (kk, j))
    o_spec = pl.BlockSpec((tm, tn), lambda i, j, kk: (i, j))
    in_specs = [a_spec, b_spec] + ([o_spec] if has_res else [])
    args = (a, b) + ((res,) if has_res else ())
    return pl.pallas_call(
        body, name=name, grid=(m // tm, n // tn, nk), in_specs=in_specs, out_specs=o_spec,
        out_shape=jax.ShapeDtypeStruct((m, n), out_dtype),
        scratch_shapes=[pltpu.VMEM((tm, tn), F32)] if nk > 1 else [],
        compiler_params=_cp("parallel", "parallel", "arbitrary"),
    )(*args)


def _norm_fwd(x, g, name):
    s, d = x.shape
    tr = min(512, s)

    def body(x_ref, g_ref, o_ref):
        xv = x_ref[...]
        r = lax.rsqrt(jnp.mean(xv * xv, axis=-1, keepdims=True) + EPS)
        o_ref[...] = (xv * r * g_ref[...]).astype(BF16)

    row = pl.BlockSpec((tr, d), lambda i: (i, 0))
    return pl.pallas_call(
        body, name=name, grid=(s // tr,), in_specs=[row, pl.BlockSpec((1, d), lambda i: (0, 0))],
        out_specs=row, out_shape=jax.ShapeDtypeStruct((s, d), BF16), compiler_params=_cp("parallel"),
    )(x, g.reshape(1, d))


def _norm_bwd(x, g, dh, dres, name):
    s, d = x.shape
    tr = min(512, s)

    def body(x_ref, g_ref, dh_ref, dres_ref, dx_ref, dg_ref):
        @pl.when(pl.program_id(0) == 0)
        def _():
            dg_ref[...] = jnp.zeros_like(dg_ref)

        xv = x_ref[...]
        r = lax.rsqrt(jnp.mean(xv * xv, axis=-1, keepdims=True) + EPS)
        xn = xv * r
        dhv = dh_ref[...].astype(F32)
        dg_ref[...] += jnp.sum(dhv * xn, axis=0, keepdims=True)
        dhg = dhv * g_ref[...]
        dx_ref[...] = dres_ref[...] + r * (dhg - xn * jnp.mean(dhg * xn, axis=-1, keepdims=True))

    row = pl.BlockSpec((tr, d), lambda i: (i, 0))
    vec = pl.BlockSpec((1, d), lambda i: (0, 0))
    return pl.pallas_call(
        body, name=name, grid=(s // tr,), in_specs=[row, vec, row, row], out_specs=[row, vec],
        out_shape=[jax.ShapeDtypeStruct((s, d), F32), jax.ShapeDtypeStruct((1, d), F32)],
        compiler_params=_cp("arbitrary"),
    )(x, g.reshape(1, d), dh, dres)


def _loss_head(x, g, tgt, name):
    s, d = x.shape
    tr = min(512, s)

    def body(x_ref, g_ref, t_ref, l_ref, dx_ref, dg_ref):
        @pl.when(pl.program_id(0) == 0)
        def _():
            dg_ref[...] = jnp.zeros_like(dg_ref)
            l_ref[...] = jnp.zeros_like(l_ref)

        xv = x_ref[...]
        r = lax.rsqrt(jnp.mean(xv * xv, axis=-1, keepdims=True) + EPS)
        xn = xv * r
        err = xn * g_ref[...] - t_ref[...]
        part = 0.5 * jnp.sum(jnp.mean(err * err, axis=-1, keepdims=True), axis=0, keepdims=True)
        l_ref[...] += jnp.broadcast_to(part, l_ref.shape)
        dy = err * (1.0 / d)
        dg_ref[...] += jnp.sum(dy * xn, axis=0, keepdims=True)
        dyg = dy * g_ref[...]
        dx_ref[...] = r * (dyg - xn * jnp.mean(dyg * xn, axis=-1, keepdims=True))

    row = pl.BlockSpec((tr, d), lambda i: (i, 0))
    vec = pl.BlockSpec((1, d), lambda i: (0, 0))
    return pl.pallas_call(
        body, name=name, grid=(s // tr,), in_specs=[row, vec, row],
        out_specs=[pl.BlockSpec((1, LANES), lambda i: (0, 0)), row, vec],
        out_shape=[jax.ShapeDtypeStruct((1, LANES), F32), jax.ShapeDtypeStruct((s, d), F32),
                   jax.ShapeDtypeStruct((1, d), F32)],
        compiler_params=_cp("arbitrary"),
    )(x, g.reshape(1, d), tgt)


def _sum_slots(landed_slot, own):
    me = _slot(*_mesh_pos())
    mine = own.astype(F32)
    g = jnp.where(me == 0, mine, landed_slot(0).astype(F32))
    for i in range(1, N_DEV):
        g = g + jnp.where(me == i, mine, landed_slot(i).astype(F32))
    return g


def _adam_update(g, w, m, v):
    mm = ADAM_B1 * m + (1.0 - ADAM_B1) * g
    vv = ADAM_B2 * v + (1.0 - ADAM_B2) * (g * g)
    m_hat = mm / (1.0 - ADAM_B1 ** ADAM_STEP)
    v_hat = vv / (1.0 - ADAM_B2 ** ADAM_STEP)
    return g, -ADAM_LR * (m_hat / (jnp.sqrt(v_hat) + ADAM_EPS) + ADAM_WD * w), mm, vv


def _adamw_rows(landed, own, ws, ms, vs, name):
    k = len(ws)
    sizes = [w.shape[1] for w in ws]

    def body(*refs):
        p_ref, o_ref = refs[:2]
        w_refs, m_refs, v_refs = refs[2:2 + k], refs[2 + k:2 + 2 * k], refs[2 + 2 * k:2 + 3 * k]
        outs = refs[2 + 3 * k:]
        for i, n in enumerate(sizes):
            g = _sum_slots(lambda s: p_ref[s, i:i + 1, :n], o_ref[i:i + 1, :n])
            res = _adam_update(g, w_refs[i][...], m_refs[i][...], v_refs[i][...])
            for j in range(4):
                outs[j * k + i][...] = res[j]

    return pl.pallas_call(
        body, name=name, out_shape=[jax.ShapeDtypeStruct((1, n), F32) for _ in range(4) for n in sizes],
    )(landed, own, *ws, *ms, *vs)


def _adamw(landed, own, w, m, v, name):
    r, c = w.shape
    cap = max(8, 256 * 1024 // c)
    tr = max(t for t in range(8, min(r, cap) + 1, 8) if r % t == 0) if r % 8 == 0 else r

    def body(p_ref, o_ref, w_ref, m_ref, v_ref, g_ref, d_ref, nm_ref, nv_ref):
        g = _sum_slots(lambda i: p_ref[i], o_ref[...])
        g_ref[...], d_ref[...], nm_ref[...], nv_ref[...] = _adam_update(g, w_ref[...], m_ref[...], v_ref[...])

    blk = pl.BlockSpec((tr, c), lambda i: (i, 0))
    return pl.pallas_call(
        body, name=name, grid=(r // tr,),
        in_specs=[pl.BlockSpec((N_DEV, tr, c), lambda i: (0, i, 0)), blk, blk, blk, blk],
        out_specs=[blk] * 4, out_shape=[jax.ShapeDtypeStruct((r, c), F32)] * 4,
        compiler_params=_cp("parallel"),
    )(landed, own, w, m, v)


def _conv_fwd(x, w_ref, kw):
    acc = w_ref[kw - 1:kw, :] * x
    for j in range(kw - 1):
        acc = acc + w_ref[j:j + 1, :] * _shift_down(x, kw - 1 - j)
    return acc


def _conv_bwd(x, dy, w_ref, dw_ref, kw):
    dx = w_ref[kw - 1:kw, :] * dy
    dw_ref[kw - 1:kw, :] = jnp.sum(dy * x, axis=0, keepdims=True)
    for j in range(kw - 1):
        dx = dx + w_ref[j:j + 1, :] * _shift_up(dy, kw - 1 - j)
        dw_ref[j:j + 1, :] = jnp.sum(dy * _shift_down(x, kw - 1 - j), axis=0, keepdims=True)
    return dx


def _ffn_act_fwd(pre, cw, name):
    s, f2 = pre.shape
    nt = f2 // 2 // LANES

    def body(pu_ref, pg_ref, wu_ref, wg_ref, o_ref):
        up = _conv_fwd(pu_ref[...], wu_ref, FFN_CONV)
        gate = _conv_fwd(pg_ref[...], wg_ref, FFN_CONV)
        o_ref[...] = (gate * _sigmoid(gate) * up).astype(BF16)

    def col(rows, off):
        return pl.BlockSpec((rows, LANES), lambda j: (0, j + off))

    return pl.pallas_call(
        body, name=name, grid=(nt,),
        in_specs=[col(s, 0), col(s, nt), col(FFN_CONV, 0), col(FFN_CONV, nt)], out_specs=col(s, 0),
        out_shape=jax.ShapeDtypeStruct((s, f2 // 2), BF16), compiler_params=_cp("parallel"),
    )(pre, pre, cw, cw)


def _ffn_act_bwd(pre, cw, dact, name):
    s, f2 = pre.shape
    f = f2 // 2
    nt = f // LANES

    def body(pu_ref, pg_ref, wu_ref, wg_ref, da_ref, dpu_ref, dpg_ref, dwu_ref, dwg_ref):
        pu, pg = pu_ref[...], pg_ref[...]
        up = _conv_fwd(pu, wu_ref, FFN_CONV)
        gate = _conv_fwd(pg, wg_ref, FFN_CONV)
        sg = _sigmoid(gate)
        da = da_ref[...]
        dup = da * gate * sg
        dgate = da * up * (sg * (1.0 + gate * (1.0 - sg)))
        dpu_ref[...] = _conv_bwd(pu, dup, wu_ref, dwu_ref, FFN_CONV).astype(BF16)
        dpg_ref[...] = _conv_bwd(pg, dgate, wg_ref, dwg_ref, FFN_CONV).astype(BF16)

    def col(rows, off):
        return pl.BlockSpec((rows, LANES), lambda j: (0, j + off))

    return pl.pallas_call(
        body, name=name, grid=(nt,),
        in_specs=[col(s, 0), col(s, nt), col(FFN_CONV, 0), col(FFN_CONV, nt), col(s, 0)],
        out_specs=[col(s, 0), col(s, 0), col(FFN_CONV, 0), col(FFN_CONV, 0)],
        out_shape=[jax.ShapeDtypeStruct((s, f), BF16), jax.ShapeDtypeStruct((s, f), BF16),
                   jax.ShapeDtypeStruct((FFN_CONV, f), F32), jax.ShapeDtypeStruct((FFN_CONV, f), F32)],
        compiler_params=_cp("parallel"),
    )(pre, pre, cw, cw, dact)


def _xa_probs(qh, kh):
    sc = _mxu(qh, kh, 1, 1) * (XA_DH ** -0.5)
    e = jnp.exp(sc - jnp.max(sc, axis=-1, keepdims=True))
    return e / jnp.sum(e, axis=-1, keepdims=True)


def _xattn_fwd(q, kv, name):
    s, d = q.shape
    m = kv.shape[0]
    tr = min(512, s)

    def body(q_ref, kv_ref, o_ref):
        for h in range(XA_HEADS):
            lo, hi = h * XA_DH, (h + 1) * XA_DH
            p = _xa_probs(q_ref[:, lo:hi], kv_ref[:, lo:hi])
            o_ref[:, lo:hi] = _mxu(p, kv_ref[:, d + lo:d + hi]).astype(BF16)

    row = pl.BlockSpec((tr, d), lambda i: (i, 0))
    return pl.pallas_call(
        body, name=name, grid=(s // tr,), in_specs=[row, pl.BlockSpec((m, 2 * d), lambda i: (0, 0))],
        out_specs=row, out_shape=jax.ShapeDtypeStruct((s, d), BF16), compiler_params=_cp("parallel"),
    )(q, kv)


def _xattn_bwd(q, kv, do, name):
    s, d = q.shape
    m = kv.shape[0]
    tr = min(512, s)

    def body(q_ref, kv_ref, do_ref, dq_ref, dkv_ref):
        @pl.when(pl.program_id(0) == 0)
        def _():
            dkv_ref[...] = jnp.zeros_like(dkv_ref)

        for h in range(XA_HEADS):
            lo, hi = h * XA_DH, (h + 1) * XA_DH
            qh, kh, vh = q_ref[:, lo:hi], kv_ref[:, lo:hi], kv_ref[:, d + lo:d + hi]
            doh = do_ref[:, lo:hi]
            p = _xa_probs(qh, kh)
            dp = _mxu(doh, vh, 1, 1)
            ds = p * (dp - jnp.sum(p * dp, axis=-1, keepdims=True)) * (XA_DH ** -0.5)
            dq_ref[:, lo:hi] = _mxu(ds, kh).astype(BF16)
            dkv_ref[:, lo:hi] += _mxu(ds, qh, 0, 0)
            dkv_ref[:, d + lo:d + hi] += _mxu(p, doh, 0, 0)

    row = pl.BlockSpec((tr, d), lambda i: (i, 0))
    full = pl.BlockSpec((m, 2 * d), lambda i: (0, 0))
    return pl.pallas_call(
        body, name=name, grid=(s // tr,), in_specs=[row, full, row], out_specs=[row, full],
        out_shape=[jax.ShapeDtypeStruct((s, d), BF16), jax.ShapeDtypeStruct((m, 2 * d), F32)],
        compiler_params=_cp("arbitrary"),
    )(q, kv, do)


def _ret_tables():
    c = RET_CHUNK
    lg = np.log1p(-np.exp2(-5.0 - np.arange(RET_HEADS, dtype=np.float32))).astype(np.float32)
    idx = np.arange(c, dtype=np.float32)
    diff = idx[:, None] - idx[None, :]
    intra = np.where(diff >= 0, np.exp(lg[:, None, None] * np.where(diff >= 0, diff, 0.0)), 0.0)
    rk = np.broadcast_to(np.exp(lg[:, None] * (c - 1 - idx))[:, :, None], (RET_HEADS, c, LANES))
    rq = np.broadcast_to(np.exp(lg[:, None] * (idx + 1))[:, :, None], (RET_HEADS, c, LANES))
    return jnp.asarray(np.stack([intra, rk, rq], axis=1).astype(np.float32))


def _rope_tables(s):
    half = RET_DH // 2
    inv = jnp.exp(-math.log(10000.0) * jnp.arange(half, dtype=F32) / half)
    ang = jnp.arange(s, dtype=F32)[:, None] * inv[None, :]
    cos, sin = jnp.cos(ang), jnp.sin(ang)
    return jnp.concatenate([cos, cos], axis=1), jnp.concatenate([-sin, sin], axis=1)


def _ret_specs(n_of):
    c = RET_CHUNK

    def head(off):
        return pl.BlockSpec((c, RET_DH), lambda h, n: (n_of(n), h + off * RET_HEADS))

    pos = pl.BlockSpec((c, RET_DH), lambda h, n: (n_of(n), 0))
    gain = pl.BlockSpec((1, RET_DH), lambda h, n: (0, h))
    tab = pl.BlockSpec((None, 3, c, LANES), lambda h, n: (h, 0, 0, 0))
    st = pl.BlockSpec((None, None, RET_DH, RET_DH), lambda h, n: (h, n_of(n), 0, 0))
    return head, pos, gain, tab, st


def _ret_chunk(q_ref, k_ref, v_ref, cos_ref, sin_ref, tab_ref, prev):
    cos, sin = cos_ref[...], sin_ref[...]
    q = q_ref[...] * cos + pltpu.roll(q_ref[...], RET_DH // 2, 1) * sin
    k = (k_ref[...] * cos + pltpu.roll(k_ref[...], RET_DH // 2, 1) * sin) * (RET_DH ** -0.5)
    v = v_ref[...]
    scores = _mxu(q, k, 1, 1) * tab_ref[0]
    qdec = q * tab_ref[2]
    kdec = k * tab_ref[1]
    o = _mxu(scores, v) + _mxu(qdec, prev)
    return q, k, v, scores, qdec, kdec, o


def _ret_fwd(proj, cos, sin, gain, name):
    s = proj.shape[0]
    c = RET_CHUNK
    nc = s // c
    head, pos, gvec, tab, st = _ret_specs(lambda n: n)

    def body(q_ref, k_ref, v_ref, g_ref, cos_ref, sin_ref, rn_ref, tab_ref, o_ref, st_ref, state):
        @pl.when(pl.program_id(1) == 0)
        def _():
            state[...] = jnp.zeros_like(state)

        prev = state[...]
        st_ref[...] = prev
        _, _, v, _, _, kdec, o = _ret_chunk(q_ref, k_ref, v_ref, cos_ref, sin_ref, tab_ref, prev)
        state[...] = prev * tab_ref[2, c - 1:c, :] + _mxu(kdec, v, 0, 0)
        r = lax.rsqrt(jnp.mean(o * o, axis=-1, keepdims=True) + EPS)
        gate = g_ref[...]
        o_ref[...] = (o * r * rn_ref[...] * (gate * _sigmoid(gate))).astype(BF16)

    return pl.pallas_call(
        body, name=name, grid=(RET_HEADS, nc),
        in_specs=[head(0), head(1), head(2), head(3), pos, pos, gvec, tab],
        out_specs=[head(0), st],
        out_shape=[jax.ShapeDtypeStruct((s, RET_HEADS * RET_DH), BF16),
                   jax.ShapeDtypeStruct((RET_HEADS, nc, RET_DH, RET_DH), F32)],
        scratch_shapes=[pltpu.VMEM((RET_DH, RET_DH), F32)],
        compiler_params=_cp("parallel", "arbitrary"),
    )(proj, proj, proj, proj, cos, sin, gain.reshape(1, -1), _ret_tables())


def _ret_bwd(proj, cos, sin, gain, states, dmerged, name):
    s = proj.shape[0]
    c = RET_CHUNK
    nc = s // c
    head, pos, gvec, tab, st = _ret_specs(lambda n: nc - 1 - n)

    def body(q_ref, k_ref, v_ref, g_ref, cos_ref, sin_ref, rn_ref, tab_ref, st_ref, do_ref,
             dq_ref, dk_ref, dv_ref, dg_ref, drn_ref, carry):
        @pl.when(pl.program_id(1) == 0)
        def _():
            carry[...] = jnp.zeros_like(carry)
            drn_ref[...] = jnp.zeros_like(drn_ref)

        prev = st_ref[...]
        q, k, v, scores, qdec, kdec, o = _ret_chunk(q_ref, k_ref, v_ref, cos_ref, sin_ref, tab_ref, prev)
        r = lax.rsqrt(jnp.mean(o * o, axis=-1, keepdims=True) + EPS)
        on = o * r
        gate = g_ref[...]
        sg = _sigmoid(gate)
        sil = gate * sg
        dout = do_ref[...]
        rn = rn_ref[...]
        dg_ref[...] = (dout * on * rn * (sg * (1.0 + gate * (1.0 - sg)))).astype(BF16)
        drn_ref[...] += jnp.sum(dout * on * sil, axis=0, keepdims=True)
        don = dout * rn * sil
        do = r * (don - on * jnp.mean(don * on, axis=-1, keepdims=True))
        dc = carry[...]
        dsc = _mxu(do, v, 1, 1) * tab_ref[0]
        dq = _mxu(dsc, k) + _mxu(do, prev, 1, 1) * tab_ref[2]
        dk = _mxu(dsc, q, 0, 0) + _mxu(v, dc, 1, 1) * tab_ref[1]
        dv = _mxu(scores, do, 0, 0) + _mxu(kdec, dc)
        carry[...] = _mxu(qdec, do, 0, 0) + dc * tab_ref[2, c - 1:c, :]
        cos, sin = cos_ref[...], sin_ref[...]
        dk = dk * (RET_DH ** -0.5)
        dq_ref[...] = (dq * cos + pltpu.roll(dq * sin, RET_DH // 2, 1)).astype(BF16)
        dk_ref[...] = (dk * cos + pltpu.roll(dk * sin, RET_DH // 2, 1)).astype(BF16)
        dv_ref[...] = dv.astype(BF16)

    width = RET_HEADS * RET_DH
    return pl.pallas_call(
        body, name=name, grid=(RET_HEADS, nc),
        in_specs=[head(0), head(1), head(2), head(3), pos, pos, gvec, tab, st, head(0)],
        out_specs=[head(0)] * 4 + [gvec],
        out_shape=[jax.ShapeDtypeStruct((s, width), BF16)] * 4 + [jax.ShapeDtypeStruct((1, width), F32)],
        scratch_shapes=[pltpu.VMEM((RET_DH, RET_DH), F32)],
        compiler_params=_cp("parallel", "arbitrary"),
    )(proj, proj, proj, proj, cos, sin, gain.reshape(1, -1), _ret_tables(), states, dmerged)


S5_TILE = 512


def _cmul_add(xr, xi, ar, ai, yr, yi):
    return xr + ar * yr - ai * yi, xi + ar * yi + ai * yr


def _s5_scan_fwd(bu, apow, name):
    s, w2 = bu.shape
    r = SCAN_ROWS
    t = S5_TILE
    steps = r.bit_length() - 1

    def body(b_ref, p_ref, o_ref, cr, ci):
        @pl.when(pl.program_id(1) == 0)
        def _():
            cr[...] = jnp.zeros_like(cr)
            ci[...] = jnp.zeros_like(ci)

        xr, xi = b_ref[:, :t], b_ref[:, t:]
        for k in range(steps):
            sh = 1 << k
            xr, xi = _cmul_add(xr, xi, p_ref[sh - 1:sh, :t], p_ref[sh - 1:sh, t:],
                               _shift_down(xr, sh), _shift_down(xi, sh))
        xr, xi = _cmul_add(xr, xi, p_ref[:, :t], p_ref[:, t:], cr[...], ci[...])
        o_ref[:, :t] = xr
        o_ref[:, t:] = xi
        cr[...] = xr[r - 1:r, :]
        ci[...] = xi[r - 1:r, :]

    blk = pl.BlockSpec((r, 2 * t), lambda j, i: (i, j))
    return pl.pallas_call(
        body, name=name, grid=(w2 // (2 * t), s // r),
        in_specs=[blk, pl.BlockSpec((r, 2 * t), lambda j, i: (0, j))], out_specs=blk,
        out_shape=jax.ShapeDtypeStruct((s, w2), F32),
        scratch_shapes=[pltpu.VMEM((1, t), F32), pltpu.VMEM((1, t), F32)],
        compiler_params=_cp("parallel", "arbitrary"),
    )(bu, apow)


def _s5_scan_bwd(dst, apow_rev, st, name):
    s, w2 = dst.shape
    r = SCAN_ROWS
    t = S5_TILE
    nb = s // r
    steps = r.bit_length() - 1

    def body(d_ref, p_ref, s_ref, sp_ref, g_ref, da_ref, cr, ci):
        i = pl.program_id(1)

        @pl.when(i == 0)
        def _():
            cr[...] = jnp.zeros_like(cr)
            ci[...] = jnp.zeros_like(ci)
            da_ref[...] = jnp.zeros_like(da_ref)

        xr, xi = d_ref[:, :t], d_ref[:, t:]
        for k in range(steps):
            sh = 1 << k
            xr, xi = _cmul_add(xr, xi, p_ref[r - sh:r - sh + 1, :t], p_ref[r - sh:r - sh + 1, t:],
                               _shift_up(xr, sh), _shift_up(xi, sh))
        xr, xi = _cmul_add(xr, xi, p_ref[:, :t], p_ref[:, t:], cr[...], ci[...])
        g_ref[:, :t] = xr.astype(BF16)
        g_ref[:, t:] = xi.astype(BF16)
        cr[...] = xr[0:1, :]
        ci[...] = xi[0:1, :]
        first = i == nb - 1
        row = lax.broadcasted_iota(jnp.int32, (r, t), 0)
        last_r = jnp.where(first, 0.0, sp_ref[7:8, :t])
        last_i = jnp.where(first, 0.0, sp_ref[7:8, t:])
        pr = jnp.where(row == 0, last_r, pltpu.roll(s_ref[:, :t], 1, 0))
        pi = jnp.where(row == 0, last_i, pltpu.roll(s_ref[:, t:], 1, 0))
        da_ref[:, :t] += jnp.sum(xr * pr + xi * pi, axis=0, keepdims=True)
        da_ref[:, t:] += jnp.sum(xi * pr - xr * pi, axis=0, keepdims=True)

    blk = pl.BlockSpec((r, 2 * t), lambda j, i: (nb - 1 - i, j))
    halo = pl.BlockSpec((8, 2 * t), lambda j, i: (jnp.maximum((nb - 1 - i) * (r // 8) - 1, 0), j))
    vec = pl.BlockSpec((1, 2 * t), lambda j, i: (0, j))
    return pl.pallas_call(
        body, name=name, grid=(w2 // (2 * t), nb),
        in_specs=[blk, pl.BlockSpec((r, 2 * t), lambda j, i: (0, j)), blk, halo], out_specs=[blk, vec],
        out_shape=[jax.ShapeDtypeStruct((s, w2), BF16), jax.ShapeDtypeStruct((1, w2), F32)],
        scratch_shapes=[pltpu.VMEM((1, t), F32), pltpu.VMEM((1, t), F32)],
        compiler_params=_cp("parallel", "arbitrary"),
    )(dst, apow_rev, st, st)


_GELU_C = math.sqrt(2.0 / math.pi)
_GELU_A = 0.044715


def _gelu(y):
    return 0.5 * y * (1.0 + jnp.tanh(_GELU_C * (y + _GELU_A * y * y * y)))


def _gelu_grad(y):
    th = jnp.tanh(_GELU_C * (y + _GELU_A * y * y * y))
    return 0.5 * (1.0 + th) + 0.5 * y * (1.0 - th * th) * _GELU_C * (1.0 + 3.0 * _GELU_A * y * y)


def _row_call(body, name, s, ins, outs, acc=False):
    tr = min(512, s)

    def spec(width, cb, rows):
        if rows == 1:
            return pl.BlockSpec((1, width), lambda i: (0, cb))
        return pl.BlockSpec((tr, width), lambda i: (i, cb))

    in_specs = [spec(w, cb, a.shape[0]) for a, w, cb in ins]
    out_specs = [spec(w, cb, sd.shape[0]) for sd, w, cb in outs]
    return pl.pallas_call(
        body, name=name, grid=(s // tr,), in_specs=in_specs, out_specs=out_specs,
        out_shape=[sd for sd, _, _ in outs],
        compiler_params=_cp("arbitrary" if acc else "parallel"),
    )(*[a for a, _, _ in ins])


def _sds(shape, dtype):
    return jax.ShapeDtypeStruct(shape, dtype)


def _s5_gelu_fwd(yraw, proj, dvec, name):
    s, w = yraw.shape

    def body(y_ref, u_ref, d_ref, yo_ref, g_ref):
        y = y_ref[...] + d_ref[...] * u_ref[...]
        yo_ref[...] = y
        g_ref[...] = _gelu(y).astype(BF16)

    return _row_call(body, name, s, [(yraw, w, 0), (proj, w, 4), (dvec, w, 0)],
                     [(_sds((s, w), F32), w, 0), (_sds((s, w), BF16), w, 0)])


def _s5_glu_fwd(y, z, b, name):
    s, w = y.shape

    def body(y_ref, z_ref, b_ref, o_ref):
        o_ref[...] = (_gelu(y_ref[...]) * _sigmoid(z_ref[...] + b_ref[...])).astype(BF16)

    return _row_call(body, name, s, [(y, w, 0), (z, w, 0), (b, w, 0)], [(_sds((s, w), BF16), w, 0)])[0]


def _s5_glu_bwd(dmerged, y, z, b, name):
    s, w = y.shape

    def body(do_ref, y_ref, z_ref, b_ref, dz_ref, dg_ref, db_ref):
        @pl.when(pl.program_id(0) == 0)
        def _():
            db_ref[...] = jnp.zeros_like(db_ref)

        g = _gelu(y_ref[...])
        sg = _sigmoid(z_ref[...] + b_ref[...])
        dout = do_ref[...]
        dz = dout * g * sg * (1.0 - sg)
        dz_ref[...] = dz.astype(BF16)
        dg_ref[...] = dout * sg
        db_ref[...] += jnp.sum(dz, axis=0, keepdims=True)

    return _row_call(body, name, s, [(dmerged, w, 1), (y, w, 0), (z, w, 0), (b, w, 0)],
                     [(_sds((s, w), BF16), w, 0), (_sds((s, w), F32), w, 0), (_sds((1, w), F32), w, 0)], acc=True)


def _s5_gelu_bwd(dg1, dg2, y, proj, dvec, name):
    s, w = y.shape

    def body(a_ref, b_ref, y_ref, u_ref, d_ref, dy_ref, du_ref, dd_ref):
        @pl.when(pl.program_id(0) == 0)
        def _():
            dd_ref[...] = jnp.zeros_like(dd_ref)

        dy = (a_ref[...] + b_ref[...]) * _gelu_grad(y_ref[...])
        dy_ref[...] = dy.astype(BF16)
        du_ref[...] = dy * d_ref[...]
        dd_ref[...] += jnp.sum(dy * u_ref[...], axis=0, keepdims=True)

    return _row_call(body, name, s, [(dg1, w, 0), (dg2, w, 0), (y, w, 0), (proj, w, 4), (dvec, w, 0)],
                     [(_sds((s, w), BF16), w, 0), (_sds((s, w), F32), w, 0), (_sds((1, w), F32), w, 0)], acc=True)


def _gdn_conv_fwd(projx, cw, name):
    s = projx.shape[0]
    nh = GDN_HEADS

    def body(x_ref, w_ref, o_ref):
        j = pl.program_id(0)
        cv = _conv_fwd(x_ref[...], w_ref, GDN_CONV)
        y = cv * _sigmoid(cv)
        nrm = y * lax.rsqrt(jnp.sum(y * y, axis=-1, keepdims=True) + EPS)
        o_ref[...] = jnp.where(j < nh, nrm * (GDN_DH ** -0.5), jnp.where(j < 2 * nh, nrm, y))

    return pl.pallas_call(
        body, name=name, grid=(3 * nh,),
        in_specs=[pl.BlockSpec((s, GDN_DH), lambda j: (0, j)), pl.BlockSpec((GDN_CONV, GDN_DH), lambda j: (0, j))],
        out_specs=pl.BlockSpec((s, GDN_DH), lambda j: (0, j)),
        out_shape=jax.ShapeDtypeStruct((s, 3 * nh * GDN_DH), F32), compiler_params=_cp("parallel"),
    )(projx, cw)


def _gdn_conv_bwd(projx, cw, dqkv, name):
    s = projx.shape[0]
    nh = GDN_HEADS

    def body(x_ref, w_ref, d_ref, dx_ref, dw_ref):
        j = pl.program_id(0)
        x = x_ref[...]
        cv = _conv_fwd(x, w_ref, GDN_CONV)
        sg = _sigmoid(cv)
        y = cv * sg
        rinv = lax.rsqrt(jnp.sum(y * y, axis=-1, keepdims=True) + EPS)
        nrm = y * rinv
        dn = d_ref[...]
        dns = jnp.where(j < nh, dn * (GDN_DH ** -0.5), dn)
        dyn = rinv * (dns - nrm * jnp.sum(dns * nrm, axis=-1, keepdims=True))
        dy = jnp.where(j < 2 * nh, dyn, dn)
        dc = dy * (sg * (1.0 + cv * (1.0 - sg)))
        dx_ref[...] = _conv_bwd(x, dc, w_ref, dw_ref, GDN_CONV).astype(BF16)

    col = pl.BlockSpec((s, GDN_DH), lambda j: (0, j))
    wcol = pl.BlockSpec((GDN_CONV, GDN_DH), lambda j: (0, j))
    return pl.pallas_call(
        body, name=name, grid=(3 * nh,), in_specs=[col, wcol, col], out_specs=[col, wcol],
        out_shape=[jax.ShapeDtypeStruct((s, 3 * nh * GDN_DH), BF16), jax.ShapeDtypeStruct((GDN_CONV, 3 * nh * GDN_DH), F32)],
        compiler_params=_cp("parallel"),
    )(projx, cw, dqkv)


def _softplus(x):
    return jnp.maximum(x, 0.0) + jnp.log1p(jnp.exp(-jnp.abs(x)))


def _gdn_gates_fwd(projx, alog, dtb, name):
    s = projx.shape[0]
    w = GDN_HEADS * GDN_DH

    def body(b_ref, a_ref, al_ref, dt_ref, bo_ref, go_ref):
        bo_ref[...] = _sigmoid(b_ref[...])
        go_ref[...] = -jnp.exp(al_ref[...]) * _softplus(a_ref[...] + dt_ref[...])

    return _row_call(body, name, s, [(projx, w, 4), (projx, w, 5), (alog, w, 0), (dtb, w, 0)],
                     [(_sds((s, w), F32), w, 0), (_sds((s, w), F32), w, 0)])


def _gdn_gates_bwd(projx, alog, dtb, dbeta, dg, name):
    s = projx.shape[0]
    w = GDN_HEADS * GDN_DH

    def body(b_ref, a_ref, al_ref, dt_ref, dbe_ref, dg_ref, db_ref, da_ref, dal_ref, ddt_ref):
        @pl.when(pl.program_id(0) == 0)
        def _():
            dal_ref[...] = jnp.zeros_like(dal_ref)
            ddt_ref[...] = jnp.zeros_like(ddt_ref)

        for h in range(GDN_HEADS):
            lo, hi = h * GDN_DH, (h + 1) * GDN_DH
            beta = _sigmoid(b_ref[:, lo:hi])
            pb = jnp.sum(dbe_ref[:, lo:hi], axis=-1, keepdims=True) * (1.0 / GDN_DH)
            db_ref[:, lo:hi] = (pb * beta * (1.0 - beta)).astype(BF16)
            xa = a_ref[:, lo:hi] + dt_ref[:, lo:hi]
            ea = -jnp.exp(al_ref[:, lo:hi])
            pg = jnp.sum(dg_ref[:, lo:hi], axis=-1, keepdims=True) * (1.0 / GDN_DH)
            da = pg * ea * _sigmoid(xa)
            da_ref[:, lo:hi] = da.astype(BF16)
            dal_ref[:, lo:hi] += jnp.sum(pg * ea * _softplus(xa), axis=0, keepdims=True)
            ddt_ref[:, lo:hi] += jnp.sum(da, axis=0, keepdims=True)

    return _row_call(body, name, s,
                     [(projx, w, 4), (projx, w, 5), (alog, w, 0), (dtb, w, 0), (dbeta, w, 0), (dg, w, 0)],
                     [(_sds((s, w), BF16), w, 0), (_sds((s, w), BF16), w, 0),
                      (_sds((1, w), F32), w, 0), (_sds((1, w), F32), w, 0)], acc=True)


def _gdn_tri():
    c = GDN_CHUNK
    i = lax.broadcasted_iota(jnp.int32, (c, c), 0)
    j = lax.broadcasted_iota(jnp.int32, (c, c), 1)
    return ((i >= j).astype(F32), (i <= j).astype(F32), i >= j, i > j, (i == j).astype(F32))


def _bdot(a, b, ca=2, cb=1, precision=None):
    return lax.dot_general(a, b, (((ca,), (cb,)), ((0,), (0,))), precision=precision, preferred_element_type=F32)


def _bmxu(a, b, ca=2, cb=1):
    return _bdot(a.astype(BF16), b.astype(BF16), ca, cb)


def _heads(x):
    return jnp.stack([x[:, h * GDN_DH:(h + 1) * GDN_DH] for h in range(GDN_HEADS)], axis=0)


def _unheads(x):
    return jnp.concatenate([x[h] for h in range(GDN_HEADS)], axis=1)


def _gdn_chunk(q, k, v, bb, g2d, tri):
    low, up, incl, strict, eye = tri
    c = GDN_CHUNK
    gc = _heads(_dot(low, g2d, precision=HI))
    gci = gc[:, :, :c]
    gdiff = gci - jnp.swapaxes(gci, 1, 2)
    decay = jnp.where(incl, jnp.exp(jnp.where(incl, gdiff, 0.0)), 0.0)
    kb, vb = k * bb, v * bb
    kbk = _bmxu(kb, k, 2, 2)
    x = -jnp.where(strict, kbk * decay, 0.0)
    t = eye + x
    p = x
    for _ in range(c.bit_length() - 2):
        p = _bdot(p, p, precision=HI)
        t = t + _bdot(t, p, precision=HI)
    eg = jnp.exp(gc)
    kbg = kb * eg
    gcl = gc[:, c - 1:c, :]
    ek = jnp.exp(gcl - gc)
    qkraw = _bmxu(q, k, 2, 2)
    return dict(decay=decay, kb=kb, vb=vb, kbk=kbk, t=t, eg=eg, kbg=kbg, ek=ek, gl=jnp.exp(gcl),
                w=_bmxu(t, kbg), u=_bmxu(t, vb), qkraw=qkraw, qk=jnp.where(incl, qkraw * decay, 0.0),
                qd=q * eg, kd=k * ek)


def _gdn_specs(n_of):
    c, w = GDN_CHUNK, GDN_HEADS * GDN_DH

    def blk(cb, width=w):
        return pl.BlockSpec((c, width), lambda n: (n_of(n), cb))

    st = pl.BlockSpec((None, GDN_HEADS, GDN_DH, GDN_DH), lambda n: (n_of(n), 0, 0, 0))
    vec = pl.BlockSpec((1, GDN_DH), lambda n: (0, 0))
    return blk, st, vec


def _gdn_load(qkv_ref, b_ref, g_ref, tri):
    w = GDN_HEADS * GDN_DH
    q, k, v = _heads(qkv_ref[:, :w]), _heads(qkv_ref[:, w:2 * w]), _heads(qkv_ref[:, 2 * w:])
    bb = _heads(b_ref[...])
    return q, k, v, bb, _gdn_chunk(q, k, v, bb, g_ref[...], tri)


def _gdn_fwd(qkv, beta, g, projx, onorm, name):
    s = qkv.shape[0]
    nc = s // GDN_CHUNK
    w = GDN_HEADS * GDN_DH
    blk, st, vec = _gdn_specs(lambda n: n)

    def body(qkv_ref, b_ref, g_ref, z_ref, on_ref, o_ref, st_ref, state):
        @pl.when(pl.program_id(0) == 0)
        def _():
            state[...] = jnp.zeros_like(state)

        _, _, _, _, ch = _gdn_load(qkv_ref, b_ref, g_ref, _gdn_tri())
        sp = state[...]
        st_ref[...] = sp
        vn = ch["u"] - _bmxu(ch["w"], sp)
        o = _bmxu(ch["qd"], sp) + _bmxu(ch["qk"], vn)
        state[...] = sp * ch["gl"] + _bmxu(ch["kd"], vn, 1, 1)
        r = lax.rsqrt(jnp.mean(o * o, axis=-1, keepdims=True) + EPS)
        z = _heads(z_ref[...])
        o_ref[...] = _unheads(o * r * on_ref[...] * (z * _sigmoid(z))).astype(BF16)

    return pl.pallas_call(
        body, name=name, grid=(nc,),
        in_specs=[blk(0, 3 * w), blk(0), blk(0), blk(3), vec], out_specs=[blk(0), st],
        out_shape=[jax.ShapeDtypeStruct((s, w), BF16), jax.ShapeDtypeStruct((nc, GDN_HEADS, GDN_DH, GDN_DH), F32)],
        scratch_shapes=[pltpu.VMEM((GDN_HEADS, GDN_DH, GDN_DH), F32)],
        compiler_params=_cp("arbitrary"),
    )(qkv, beta, g, projx, onorm.reshape(1, -1))


def _gdn_bwd(qkv, beta, g, projx, onorm, states, dout, name):
    s = qkv.shape[0]
    c = GDN_CHUNK
    nc = s // c
    w = GDN_HEADS * GDN_DH
    blk, st, vec = _gdn_specs(lambda n: nc - 1 - n)

    def body(qkv_ref, b_ref, g_ref, z_ref, on_ref, st_ref, do_ref,
             dqkv_ref, db_ref, dg_ref, dz_ref, don_ref, carry):
        @pl.when(pl.program_id(0) == 0)
        def _():
            carry[...] = jnp.zeros_like(carry)
            don_ref[...] = jnp.zeros_like(don_ref)

        tri = _gdn_tri()
        low, up, incl, strict, eye = tri
        q, k, v, bb, ch = _gdn_load(qkv_ref, b_ref, g_ref, tri)
        sp = st_ref[...]
        vn = ch["u"] - _bmxu(ch["w"], sp)
        o = _bmxu(ch["qd"], sp) + _bmxu(ch["qk"], vn)
        r = lax.rsqrt(jnp.mean(o * o, axis=-1, keepdims=True) + EPS)
        orn = o * r
        z = _heads(z_ref[...])
        sg = _sigmoid(z)
        dout = _heads(do_ref[...])
        onw = on_ref[...]
        dz_ref[...] = _unheads(dout * orn * onw * (sg * (1.0 + z * (1.0 - sg)))).astype(BF16)
        don = dout * (z * sg)
        don_ref[...] += jnp.sum(jnp.sum(don * orn, axis=0), axis=0, keepdims=True)
        dor = don * onw
        do = r * (dor - orn * jnp.mean(dor * orn, axis=-1, keepdims=True))
        dsn = carry[...]
        dqd = _bmxu(do, sp, 2, 2)
        dqk = jnp.where(incl, _bmxu(do, vn, 2, 2), 0.0)
        dvn = _bmxu(ch["qk"], do, 1, 1) + _bmxu(ch["kd"], dsn)
        dkd = _bmxu(vn, dsn, 2, 2)
        dgl = jnp.sum(dsn * sp, axis=1, keepdims=True)
        dw = -_bmxu(dvn, sp, 2, 2)
        carry[...] = _bmxu(ch["qd"], do, 1, 1) + dsn * ch["gl"] - _bmxu(ch["w"], dvn, 1, 1)
        t = ch["t"]
        dvb = _bmxu(t, dvn, 1, 1)
        dkbg = _bmxu(t, dw, 1, 1)
        dt = _bmxu(dvn, ch["vb"], 2, 2) + _bmxu(dw, ch["kbg"], 2, 2)
        da = -_bdot(_bdot(t, dt, 1, 1, precision=HI), t, 2, 2, precision=HI)
        da = jnp.where(strict, da, 0.0)
        decay = ch["decay"]
        dkbk = da * decay
        dqkr = dqk * decay
        mdec = (da * ch["kbk"] + dqk * ch["qkraw"]) * decay
        dkb = _bmxu(dkbk, k) + dkbg * ch["eg"]
        dk = _bmxu(dkbk, ch["kb"], 1, 1) + _bmxu(dqkr, q, 1, 1) + dkd * ch["ek"] + dkb * bb
        dq = _bmxu(dqkr, k) + dqd * ch["eg"]
        tk = dkd * ch["kd"]
        dgcl = jnp.sum(tk, axis=1, keepdims=True) + dgl * ch["gl"]
        row = lax.broadcasted_iota(jnp.int32, (GDN_HEADS, c, GDN_DH), 1)
        zpad = jnp.zeros((GDN_HEADS, c, GDN_DH - c), F32)
        dgc = (jnp.concatenate([mdec, zpad], axis=2) - jnp.concatenate([jnp.swapaxes(mdec, 1, 2), zpad], axis=2)
               + dqd * ch["qd"] - tk + dkbg * ch["kbg"] + jnp.where(row == c - 1, dgcl, 0.0))
        dqkv_ref[:, :w] = _unheads(dq)
        dqkv_ref[:, w:2 * w] = _unheads(dk)
        dqkv_ref[:, 2 * w:] = _unheads(dvb * bb)
        db_ref[...] = _unheads(dkb * k + dvb * v)
        dg_ref[...] = _dot(up, _unheads(dgc), precision=HI)

    return pl.pallas_call(
        body, name=name, grid=(nc,),
        in_specs=[blk(0, 3 * w), blk(0), blk(0), blk(3), vec, st, blk(0)],
        out_specs=[blk(0, 3 * w), blk(0), blk(0), blk(0), vec],
        out_shape=[jax.ShapeDtypeStruct((s, 3 * w), F32), jax.ShapeDtypeStruct((s, w), F32),
                   jax.ShapeDtypeStruct((s, w), F32), jax.ShapeDtypeStruct((s, w), BF16),
                   jax.ShapeDtypeStruct((1, GDN_DH), F32)],
        scratch_shapes=[pltpu.VMEM((GDN_HEADS, GDN_DH, GDN_DH), F32)],
        compiler_params=_cp("arbitrary"),
    )(qkv, beta, g, projx, onorm.reshape(1, -1), states, dout)


_WEIGHTS = (
    "l0_mix_norm", "l0_w_in", "l0_ret_norm", "l0_s5_lambda_re", "l0_s5_lambda_im", "l0_s5_b_re", "l0_s5_b_im",
    "l0_s5_c_re", "l0_s5_c_im", "l0_s5_d", "l0_s5_log_dt", "l0_s5_w_glu", "l0_s5_b_glu", "l0_w_out",
    "l0_xa_norm", "l0_mem_norm", "l0_xa_wq", "l0_xa_wkv", "l0_xa_wo", "l0_ffn_norm", "l0_ffn_w_up",
    "l0_ffn_conv", "l0_ffn_w_down", "l1_mix_norm", "l1_w_in", "l1_conv", "l1_a_log", "l1_dt_bias", "l1_o_norm",
    "l1_w_out", "l1_xa_norm", "l1_mem_norm", "l1_xa_wq", "l1_xa_wkv", "l1_xa_wo", "l1_ffn_norm", "l1_ffn_w_up",
    "l1_ffn_conv", "l1_ffn_w_down", "final_norm")
_INPUTS = ("x", "mem") + _WEIGHTS + ("loss_target",) + tuple("m_" + n for n in _WEIGHTS) + tuple("v_" + n for n in _WEIGHTS)

_COL = ("l0_w_in", "l0_xa_wkv", "l0_ffn_w_up", "l0_ffn_conv", "l1_w_in", "l1_conv", "l1_xa_wkv", "l1_ffn_w_up",
        "l1_ffn_conv")
_ROW = ("l0_s5_w_glu", "l0_w_out", "l0_xa_wq", "l0_xa_wo", "l0_ffn_w_down", "l1_w_out", "l1_xa_wq", "l1_xa_wo",
        "l1_ffn_w_down")
_F32_WIRE = ("l0_ffn_conv", "l1_conv", "l1_ffn_conv")
_REP = tuple(n for n in _WEIGHTS if n not in _COL + _ROW)
_GATHER_GROUPS = (("l0_w_in", "l0_s5_w_glu", "l0_w_out"),
                  ("l0_xa_wq", "l0_xa_wkv", "l0_xa_wo", "l0_ffn_w_up", "l0_ffn_conv", "l0_ffn_w_down"),
                  ("l1_w_in", "l1_conv", "l1_w_out", "l1_xa_wq", "l1_xa_wkv", "l1_xa_wo", "l1_ffn_w_up", "l1_ffn_conv",
                   "l1_ffn_w_down"))


def _round_up(n, m):
    return (n + m - 1) // m * m


_REP_BIG = ("l0_s5_lambda_re", "l0_s5_lambda_im", "l0_s5_b_re", "l0_s5_b_im", "l0_s5_c_re", "l0_s5_c_im", "l0_s5_d")
_REP_SMALL = tuple(n for n in _REP if n not in _REP_BIG)
PACK_WIDTH = 1024


def _pack_rows(ts):
    rows = [jnp.pad(t, ((0, 0), (0, PACK_WIDTH - t.shape[1]))) for t in ts]
    rows.append(jnp.zeros((_round_up(len(ts), 8) - len(ts), PACK_WIDTH), F32))
    return jnp.concatenate(rows, axis=0)


def _s5_interleave(re, im):
    lead = re.shape[:-1]
    nt = re.shape[-1] // S5_TILE
    both = jnp.stack([re.reshape(lead + (nt, S5_TILE)), im.reshape(lead + (nt, S5_TILE))], axis=-2)
    return both.reshape(lead + (2 * re.shape[-1],))


def _s5_split(x):
    lead = x.shape[:-1]
    y = x.reshape(lead + (x.shape[-1] // (2 * S5_TILE), 2, S5_TILE))
    return y[..., 0, :].reshape(lead + (-1,)), y[..., 1, :].reshape(lead + (-1,))


def _s5_discretise(lr, li, log_dt, b_re, b_im):
    dt = jnp.exp(log_dt)[:, None]
    mag = jnp.exp(lr * dt)
    a_re = mag * jnp.cos(li * dt)
    a_im = mag * jnp.sin(li * dt)
    den = lr * lr + li * li
    z_re = ((a_re - 1.0) * lr + a_im * li) / den
    z_im = (a_im * lr - (a_re - 1.0) * li) / den
    bb_re = z_re[:, None, :] * b_re - z_im[:, None, :] * b_im
    bb_im = z_re[:, None, :] * b_im + z_im[:, None, :] * b_re
    return a_re, a_im, bb_re, bb_im


def _pow_tables(a_re, a_im, rows):
    ur, ui = a_re[None], a_im[None]
    dr, di = ur, ui
    while ur.shape[0] < rows:
        lr, li = ur[-1:], ui[-1:]
        ur, ui = (jnp.concatenate([ur, ur * lr - ui * li]), jnp.concatenate([ui, ur * li + ui * lr]))
        dr, di = (jnp.concatenate([dr * lr - di * li, dr]), jnp.concatenate([dr * li + di * lr, di]))
    return (ur, ui), (dr, di)


def _block_diag(b):
    g, r, c = b.shape
    return jnp.einsum("grc,gk->grkc", b, jnp.eye(g, dtype=b.dtype)).reshape(g * r, g * c)


def _block_diag_of(d, g):
    r, c = d.shape[0] // g, d.shape[1] // g
    return jnp.einsum("grkc,gk->grc", d.reshape(g, r, g, c), jnp.eye(g, dtype=d.dtype))


def kernel(*args):
    p = dict(zip(_INPUTS, args, strict=True))
    x0, mem0, tgt = p["x"][0], p["mem"][0], p["loss_target"][0]
    s, d = x0.shape
    me = _slot(*_mesh_pos())
    grads = {}
    wire = {n: (F32 if n in _F32_WIRE else BF16) for n in _COL + _ROW}

    shards = {n: p[n].astype(wire[n]) for n in _COL + _ROW}
    gather, pin = [], jnp.zeros((), F32)
    for i, names in enumerate(_GATHER_GROUPS):
        group = [shards[n] for n in names]
        handle, token = _push_start([], group, f"gather{i}_start", _place_own(group, f"gather{i}_place"))
        gather.append(handle)
        pin = pin + token[0, 0]
    w = {}

    def gathered(i, after):
        for n, full in zip(_GATHER_GROUPS[i], _push_wait(gather[i], after, f"gather{i}_wait")):
            if n in _COL:
                full = full.transpose(1, 0, 2)
            w[n] = full.reshape(-1, full.shape[-1]) if n in _ROW else full.reshape(full.shape[0], -1)

    pending = []

    def exchange(names, gain, tag):
        slots = []
        for n in names:
            g = grads[n]
            if n in _COL:
                g = g.reshape(g.shape[0], N_DEV, -1).transpose(1, 0, 2)
            else:
                g = g.reshape((N_DEV, -1) + g.shape[1:])
            slots.append(g.astype(wire[n]))
        handle, token = _push_start(slots, [], tag + "_start")
        pending.append((names, slots, handle, tag))
        return gain + token[0, 0]

    def xattn(pre, x_in):
        hx = _norm_fwd(x_in, p[pre + "xa_norm"], pre + "xa_norm_fwd")
        q = _mm(hx, w[pre + "xa_wq"], out_dtype=BF16, name=pre + "xa_q")
        memn = _norm_fwd(mem0, p[pre + "mem_norm"], pre + "mem_norm_fwd")
        kv = _mm(memn, w[pre + "xa_wkv"], out_dtype=BF16, name=pre + "xa_kv")
        ao = _xattn_fwd(q, kv, pre + "xattn_fwd")
        x_out = _mm(ao, w[pre + "xa_wo"], res=x_in, name=pre + "xa_o")
        return x_out, (x_in, hx, q, memn, kv, ao)

    def xattn_bwd(pre, saved, dxo):
        x_in, hx, q, memn, kv, ao = saved
        dao = _mm(dxo, w[pre + "xa_wo"], tb=True, name=pre + "xa_o_dx")
        grads[pre + "xa_wo"] = _mm(ao, dxo, ta=True, out_dtype=BF16, name=pre + "xa_o_dw")
        dq, dkv = _xattn_bwd(q, kv, dao, pre + "xattn_bwd")
        grads[pre + "xa_wq"] = _mm(hx, dq, ta=True, out_dtype=BF16, name=pre + "xa_q_dw")
        dhx = _mm(dq, w[pre + "xa_wq"], tb=True, name=pre + "xa_q_dx")
        grads[pre + "xa_wkv"] = _mm(memn, dkv, ta=True, out_dtype=BF16, name=pre + "xa_kv_dw")
        dmemn = _mm(dkv, w[pre + "xa_wkv"], tb=True, name=pre + "xa_kv_dx")
        gain = exchange((pre + "xa_wo", pre + "xa_wq", pre + "xa_wkv"), p[pre + "xa_norm"], pre + "xa_grads")
        dx_in, grads[pre + "xa_norm"] = _norm_bwd(x_in, gain, dhx, dxo, pre + "xa_norm_bwd")
        _, grads[pre + "mem_norm"] = _norm_bwd(mem0, p[pre + "mem_norm"], dmemn, jnp.zeros_like(mem0), pre + "mem_norm_bwd")
        return dx_in

    def ffn(pre, x_in):
        hf = _norm_fwd(x_in, p[pre + "ffn_norm"], pre + "ffn_norm_fwd")
        up = _mm(hf, w[pre + "ffn_w_up"], name=pre + "ffn_up")
        act = _ffn_act_fwd(up, w[pre + "ffn_conv"], pre + "ffn_act_fwd")
        x_out = _mm(act, w[pre + "ffn_w_down"], res=x_in, name=pre + "ffn_down")
        return x_out, (x_in, hf, up, act)

    def ffn_bwd(pre, saved, dxo):
        x_in, hf, up, act = saved
        dact = _mm(dxo, w[pre + "ffn_w_down"], tb=True, name=pre + "ffn_down_dx")
        grads[pre + "ffn_w_down"] = _mm(act, dxo, ta=True, out_dtype=BF16, name=pre + "ffn_down_dw")
        dpu, dpg, dcu, dcg = _ffn_act_bwd(up, w[pre + "ffn_conv"], dact, pre + "ffn_act_bwd")
        dup = jnp.concatenate([dpu, dpg], axis=1)
        grads[pre + "ffn_conv"] = jnp.concatenate([dcu, dcg], axis=1)
        dhf = _mm(dup, w[pre + "ffn_w_up"], tb=True, name=pre + "ffn_up_dx")
        grads[pre + "ffn_w_up"] = _mm(hf, dup, ta=True, out_dtype=BF16, name=pre + "ffn_up_dw")
        gain = exchange((pre + "ffn_w_down", pre + "ffn_w_up", pre + "ffn_conv"), p[pre + "ffn_norm"], pre + "ffn_grads")
        dx_in, grads[pre + "ffn_norm"] = _norm_bwd(x_in, gain, dhf, dxo, pre + "ffn_norm_bwd")
        return dx_in

    cos, sin = _rope_tables(s)
    (a_re, a_im, bb_re, bb_im), disc_vjp = jax.vjp(
        _s5_discretise, p["l0_s5_lambda_re"], p["l0_s5_lambda_im"], p["l0_s5_log_dt"], p["l0_s5_b_re"], p["l0_s5_b_im"])
    (pu_re, pu_im), (pd_re, pd_im) = _pow_tables(a_re.reshape(-1), a_im.reshape(-1), SCAN_ROWS)
    apow = _s5_interleave(pu_re, pu_im)
    apow_rev = _s5_interleave(pd_re, -pd_im)
    bbig = _s5_interleave(_block_diag(bb_re), _block_diag(bb_im)).astype(BF16)
    cbig = _s5_interleave(_block_diag(p["l0_s5_c_re"]).T, -_block_diag(p["l0_s5_c_im"]).T).T.astype(BF16)
    s5_d = p["l0_s5_d"].reshape(1, -1)
    b_glu = p["l0_s5_b_glu"].reshape(1, -1)

    h0 = _norm_fwd(x0, p["l0_mix_norm"] + pin, "l0_mix_norm_fwd")
    gathered(0, h0)
    proj = _mm(h0, w["l0_w_in"], name="l0_in")
    o_ret, ret_states = _ret_fwd(proj, cos, sin, p["l0_ret_norm"], "l0_ret_fwd")
    u = proj[:, 4 * RET_HEADS * RET_DH:]
    bu = _mm(u, bbig, name="l0_s5_bu")
    st = _s5_scan_fwd(bu, apow, "l0_s5_scan_fwd")
    yraw = _mm(st, cbig, name="l0_s5_c")
    y, gy = _s5_gelu_fwd(yraw, proj, s5_d, "l0_s5_gelu_fwd")
    z = _mm(gy, w["l0_s5_w_glu"], name="l0_s5_glu_mm")
    y2 = _s5_glu_fwd(y, z, b_glu, "l0_s5_glu_fwd")
    merged = jnp.concatenate([o_ret, y2], axis=1)
    x1 = _mm(merged, w["l0_w_out"], res=x0, name="l0_out")
    gathered(1, x1)
    x2, xa0 = xattn("l0_", x1)
    x3, ff0 = ffn("l0_", x2)

    gathered(2, x3)
    nqkv = 4 * GDN_HEADS * GDN_DH
    w1 = w["l1_w_in"]
    wx = jnp.concatenate([w1[:, :nqkv], jnp.repeat(w1[:, nqkv:nqkv + GDN_HEADS], GDN_DH, axis=1),
                          jnp.repeat(w1[:, nqkv + GDN_HEADS:], GDN_DH, axis=1)], axis=1)
    alog_x = jnp.repeat(p["l1_a_log"], GDN_DH).reshape(1, -1)
    dtb_x = jnp.repeat(p["l1_dt_bias"], GDN_DH).reshape(1, -1)
    h1 = _norm_fwd(x3, p["l1_mix_norm"], "l1_mix_norm_fwd")
    projx = _mm(h1, wx, name="l1_in")
    qkv = _gdn_conv_fwd(projx, w["l1_conv"], "l1_conv_fwd")
    beta, glog = _gdn_gates_fwd(projx, alog_x, dtb_x, "l1_gates_fwd")
    o_gdn, gdn_states = _gdn_fwd(qkv, beta, glog, projx, p["l1_o_norm"], "l1_gdn_fwd")
    x4 = _mm(o_gdn, w["l1_w_out"], res=x3, name="l1_out")
    x5, xa1 = xattn("l1_", x4)
    x6, ff1 = ffn("l1_", x5)

    loss_part, dx6, grads["final_norm"] = _loss_head(x6, p["final_norm"], tgt, "loss_head")
    loss = lax.psum(loss_part[0, 0], ("x", "y", "c"))
    dx5 = ffn_bwd("l1_", ff1, dx6)
    dx4 = xattn_bwd("l1_", xa1, dx5)

    do_gdn = _mm(dx4, w["l1_w_out"], tb=True, name="l1_out_dx")
    grads["l1_w_out"] = _mm(o_gdn, dx4, ta=True, out_dtype=BF16, name="l1_out_dw")
    dqkv, dbeta, dglog, dz, grads["l1_o_norm"] = _gdn_bwd(
        qkv, beta, glog, projx, p["l1_o_norm"], gdn_states, do_gdn, "l1_gdn_bwd")
    dpre, grads["l1_conv"] = _gdn_conv_bwd(projx, w["l1_conv"], dqkv, "l1_conv_bwd")
    db, da, dalog_x, ddtb_x = _gdn_gates_bwd(projx, alog_x, dtb_x, dbeta, dglog, "l1_gates_bwd")
    dprojx = jnp.concatenate([dpre, dz, db, da], axis=1)
    dh1 = _mm(dprojx, wx, tb=True, name="l1_in_dx")
    dwx = _mm(h1, dprojx, ta=True, name="l1_in_dw")
    grads["l1_w_in"] = jnp.concatenate(
        [dwx[:, :nqkv], dwx[:, nqkv:nqkv + GDN_HEADS * GDN_DH].reshape(d, GDN_HEADS, GDN_DH).sum(-1),
         dwx[:, nqkv + GDN_HEADS * GDN_DH:].reshape(d, GDN_HEADS, GDN_DH).sum(-1)], axis=1)
    grads["l1_a_log"] = dalog_x.reshape(GDN_HEADS, GDN_DH).sum(-1)
    grads["l1_dt_bias"] = ddtb_x.reshape(GDN_HEADS, GDN_DH).sum(-1)
    gain = exchange(("l1_w_out", "l1_w_in", "l1_conv"), p["l1_mix_norm"], "l1_mix_grads")
    dx3, grads["l1_mix_norm"] = _norm_bwd(x3, gain, dh1, dx4, "l1_mix_norm_bwd")

    dx2 = ffn_bwd("l0_", ff0, dx3)
    dx1 = xattn_bwd("l0_", xa0, dx2)

    dmerged = _mm(dx1, w["l0_w_out"], tb=True, name="l0_out_dx")
    grads["l0_w_out"] = _mm(merged, dx1, ta=True, out_dtype=BF16, name="l0_out_dw")
    drq, drk, drv, drg, grads["l0_ret_norm"] = _ret_bwd(proj, cos, sin, p["l0_ret_norm"], ret_states, dmerged, "l0_ret_bwd")
    dzg, dg1, grads["l0_s5_b_glu"] = _s5_glu_bwd(dmerged, y, z, b_glu, "l0_s5_glu_bwd")
    grads["l0_s5_w_glu"] = _mm(gy, dzg, ta=True, out_dtype=BF16, name="l0_s5_glu_dw")
    dg2 = _mm(dzg, w["l0_s5_w_glu"], tb=True, name="l0_s5_glu_dx")
    dyraw, du_dir, grads["l0_s5_d"] = _s5_gelu_bwd(dg1, dg2, y, proj, s5_d, "l0_s5_gelu_bwd")
    dst = _mm(dyraw, cbig, tb=True, name="l0_s5_c_dx")
    dcbig = _mm(st, dyraw, ta=True, name="l0_s5_c_dw")
    gsc, da_s5 = _s5_scan_bwd(dst, apow_rev, st, "l0_s5_scan_bwd")
    du = _mm(gsc, bbig, tb=True, res=du_dir, out_dtype=BF16, name="l0_s5_bu_dx")
    dbbig = _mm(u, gsc, ta=True, name="l0_s5_bu_dw")
    dproj = jnp.concatenate([drq, drk, drv, drg, du], axis=1)
    dh0 = _mm(dproj, w["l0_w_in"], tb=True, name="l0_in_dx")
    grads["l0_w_in"] = _mm(h0, dproj, ta=True, out_dtype=BF16, name="l0_in_dw")
    gain = exchange(("l0_w_out", "l0_s5_w_glu", "l0_w_in"), p["l0_mix_norm"], "l0_mix_grads")
    dx0, grads["l0_mix_norm"] = _norm_bwd(x0, gain, dh0, dx1, "l0_mix_norm_bwd")

    dbb_re, dbb_im = (_block_diag_of(t, S5_GROUPS) for t in _s5_split(dbbig))
    dct_re, dct_im = _s5_split(dcbig.T)
    grads["l0_s5_c_re"] = _block_diag_of(dct_re.T, S5_GROUPS)
    grads["l0_s5_c_im"] = -_block_diag_of(dct_im.T, S5_GROUPS)
    da_re, da_im = (t.reshape(S5_GROUPS, S5_STATE) for t in _s5_split(da_s5[0]))
    (grads["l0_s5_lambda_re"], grads["l0_s5_lambda_im"], grads["l0_s5_log_dt"], grads["l0_s5_b_re"],
     grads["l0_s5_b_im"]) = disc_vjp((da_re, da_im, dbb_re, dbb_im))

    def as_2d(t):
        return t.reshape(-1, t.shape[-1])

    def as_row(t):
        return t.reshape(1, -1)

    small_own = _pack_rows([as_row(grads[n]) for n in _REP_SMALL])
    big_own = [as_2d(grads[n].reshape(p[n].shape)) for n in _REP_BIG]
    rep_handle, _ = _push_start([], [small_own] + big_own, "rep_grads_start")
    rep_lands = _push_wait(rep_handle, dx0, "rep_grads_wait")
    rep_land = rep_lands[0]

    outs = {}
    kinds = ("grad_", "delta_", "new_m_", "new_v_")
    for names, slots, handle, tag in pending:
        for n, own_slots, land in zip(names, slots, _push_wait(handle, rep_land, tag + "_wait")):
            shape = p[n].shape
            own = lax.dynamic_index_in_dim(own_slots, me, 0, keepdims=False)
            res = _adamw(land, own, *(p[pre + n].reshape(own.shape) for pre in ("", "m_", "v_")), "adamw_" + n)
            for kind, t in zip(kinds, res):
                outs[kind + n] = t.reshape(shape)
    for n, own, land in zip(_REP_BIG, big_own, rep_lands[1:]):
        res = _adamw(land.reshape((N_DEV,) + own.shape), own, *(as_2d(p[pre + n]) for pre in ("", "m_", "v_")), "adamw_" + n)
        for kind, t in zip(kinds, res):
            outs[kind + n] = t.reshape(p[n].shape)
    res = _adamw_rows(rep_land, small_own, *([as_row(p[pre + n]) for n in _REP_SMALL] for pre in ("", "m_", "v_")), "adamw_small")
    for j, kind in enumerate(kinds):
        for i, n in enumerate(_REP_SMALL):
            outs[kind + n] = res[j * len(_REP_SMALL) + i].reshape(p[n].shape)

    return (loss, dx0[None]) + tuple(outs[kind + n] for kind in kinds for n in _WEIGHTS)
```

```python
import functools
import math

import numpy as np
import jax
import jax.numpy as jnp
from jax import lax
from jax.experimental import pallas as pl
from jax.experimental.pallas import tpu as pltpu

F32 = jnp.float32
BF16 = jnp.bfloat16
EPS = 1e-6
N_DEV = 8
LANES = 128
VMEM_LIMIT = 48 * 1024 * 1024
HI = lax.Precision.HIGHEST

RET_HEADS, RET_DH, RET_CHUNK = 4, 128, 128
S5_GROUPS, S5_GROUP, S5_STATE = 32, 16, 64
GDN_HEADS, GDN_DH, GDN_CHUNK, GDN_CONV = 8, 128, 64, 4
XA_HEADS, XA_DH = 4, 256
FFN_CONV = 3
SCAN_ROWS = 256

ADAM_LR, ADAM_B1, ADAM_B2, ADAM_EPS, ADAM_WD, ADAM_STEP = 0.001, 0.9, 0.999, 1e-08, 0.01, 10


def _cp(*sem):
    return pltpu.CompilerParams(dimension_semantics=sem if sem else None, vmem_limit_bytes=VMEM_LIMIT)


def _tile(n, cap):
    if n <= cap:
        return n
    best = None
    for t in range(LANES, cap + 1, LANES):
        if n % t == 0:
            best = t
    assert best is not None, n
    return best


def _dot(a, b, ca=1, cb=0, precision=None):
    return lax.dot_general(a, b, (((ca,), (cb,)), ((), ())), precision=precision, preferred_element_type=F32)


def _mxu(a, b, ca=1, cb=0):
    return _dot(a.astype(BF16), b.astype(BF16), ca, cb)


def _sigmoid(x):
    return 1.0 / (1.0 + jnp.exp(-x))


def _shift_down(x, k):
    row = lax.broadcasted_iota(jnp.int32, x.shape, 0)
    return jnp.where(row >= k, pltpu.roll(x, k, 0), 0.0)


def _shift_up(x, k):
    n = x.shape[0]
    row = lax.broadcasted_iota(jnp.int32, x.shape, 0)
    return jnp.where(row < n - k, pltpu.roll(x, n - k, 0), 0.0)


def _mesh_pos():
    return lax.axis_index("x"), lax.axis_index("y"), lax.axis_index("c")


def _slot(px, py, pc):
    return 4 * px + 2 * py + pc


def _all_peers(x, y, c):
    flips = [(fx, fy, fc) for fx in (0, 1) for fy in (0, 1) for fc in (0, 1)][1:]
    return [(1 - x if fx else x, 1 - y if fy else y, 1 - c if fc else c) for fx, fy, fc in flips]


_HBM = pl.BlockSpec(memory_space=pltpu.HBM)
_SEM = pl.BlockSpec(memory_space=pltpu.SEMAPHORE)
N_PEERS = N_DEV - 1


def _push_copies(srcs, lands, send_sems, recv_sems, ns, start):
    x, y, c = _mesh_pos()
    me = _slot(x, y, c)
    out = []
    for k, to in enumerate(_all_peers(x, y, c)):
        for a in range(len(srcs)):
            src = srcs[a].at[_slot(*to)] if a < ns else srcs[a]
            dst = lands[a].at[me if start else _slot(*to)]
            out.append(pltpu.make_async_remote_copy(
                src_ref=src, dst_ref=dst, send_sem=send_sems.at[a * N_PEERS + k], recv_sem=recv_sems.at[a * N_PEERS + k],
                device_id=to, device_id_type=pl.DeviceIdType.MESH))
    return out


def _place_own(arrs, name):
    n = len(arrs)
    anyspace = pl.BlockSpec(memory_space=pl.ANY)

    def body(*refs):
        me = _slot(*_mesh_pos())
        copies = [pltpu.make_async_copy(refs[a], refs[n + a].at[me], refs[2 * n].at[a]) for a in range(n)]
        for cp in copies:
            cp.start()
        for cp in copies:
            cp.wait()

    return pl.pallas_call(
        body, name=name, out_shape=[jax.ShapeDtypeStruct((N_DEV,) + a.shape, a.dtype) for a in arrs],
        in_specs=[anyspace] * n, out_specs=[anyspace] * n, scratch_shapes=[pltpu.SemaphoreType.DMA((n,))],
    )(*arrs)


def _push_start(scatter, gather, name, gather_lands=None):
    ns = len(scatter)
    arrs = list(scatter) + list(gather)
    n = len(arrs)
    land_shapes = [a.shape for a in scatter] + [(N_DEV,) + a.shape for a in gather]
    lands = [lax.empty(s, a.dtype) for s, a in zip(land_shapes[:ns], scatter)]
    lands += list(gather_lands) if gather_lands is not None else [lax.empty(s, a.dtype) for s, a in zip(land_shapes[ns:], gather)]

    def body(*refs):
        srcs, lands = refs[:n], refs[n:2 * n]
        send_sems, recv_sems = refs[2 * n], refs[2 * n + 1]
        for cp in _push_copies(srcs, lands, send_sems, recv_sems, ns, True):
            cp.start()
        refs[-1][...] = jnp.zeros((8, LANES), F32)

    hbm_in = [pltpu.with_memory_space_constraint(a, pltpu.HBM) for a in arrs]
    hbm_in += [pltpu.with_memory_space_constraint(z, pltpu.HBM) for z in lands]
    res = pl.pallas_call(
        body, name=name,
        out_shape=(pltpu.SemaphoreType.DMA((n * N_PEERS,)), pltpu.SemaphoreType.DMA((n * N_PEERS,)))
        + tuple(pltpu.HBM(a.shape, a.dtype) for a in arrs)
        + tuple(pltpu.HBM(s, a.dtype) for s, a in zip(land_shapes, arrs))
        + (jax.ShapeDtypeStruct((8, LANES), F32),),
        in_specs=[_HBM] * (2 * n),
        out_specs=(_SEM, _SEM) + (_HBM,) * (2 * n) + (pl.BlockSpec(memory_space=pltpu.VMEM),),
        input_output_aliases={i: 2 + i for i in range(2 * n)},
        compiler_params=pltpu.CompilerParams(has_side_effects=pltpu.SideEffectType.DATAFLOW_SIDE_EFFECTING),
    )(*hbm_in)
    return (res[0], res[1], res[2:2 + n], res[2 + n:2 + 2 * n], ns), res[-1]


def _push_wait(handle, after, name):
    send_sems, recv_sems, srcs, lands, ns = handle
    n = len(srcs)

    def body(*refs):
        for cp in _push_copies(refs[:n], refs[n:2 * n], refs[2 * n], refs[2 * n + 1], ns, False):
            cp.wait_send()
            cp.wait_recv()

    res = pl.pallas_call(
        body, name=name,
        out_shape=tuple(pltpu.HBM(a.shape, a.dtype) for a in srcs) + tuple(pltpu.HBM(a.shape, a.dtype) for a in lands),
        in_specs=[_HBM] * (2 * n) + [_SEM, _SEM, pl.BlockSpec(memory_space=pl.ANY)],
        out_specs=(_HBM,) * (2 * n),
        input_output_aliases={i: i for i in range(2 * n)},
        compiler_params=pltpu.CompilerParams(has_side_effects=pltpu.SideEffectType.DATAFLOW_SIDE_EFFECTING),
    )(*srcs, *lands, send_sems, recv_sems, after)
    return res[n:]


def _mm(a, b, *, ta=False, tb=False, out_dtype=F32, res=None, name="mm"):
    m, k = (a.shape[1], a.shape[0]) if ta else a.shape
    n = b.shape[0] if tb else b.shape[1]
    assert k == (b.shape[1] if tb else b.shape[0]), (a.shape, b.shape, ta, tb)
    tm, tn, tk = _tile(m, 1408), _tile(n, 1536), _tile(k, 1408)
    nk = k // tk
    has_res = res is not None

    def body(*refs):
        a_ref, b_ref = refs[:2]
        r_ref = refs[2] if has_res else None
        o_ref = refs[3 if has_res else 2]
        part = _mxu(a_ref[...], b_ref[...], 0 if ta else 1, 1 if tb else 0)

        def finish(r):
            if has_res:
                r = r + r_ref[...].astype(F32)
            o_ref[...] = r.astype(out_dtype)

        if nk == 1:
            finish(part)
            return
        acc = refs[-1]
        kk = pl.program_id(2)

        @pl.when(kk == 0)
        def _():
            acc[...] = part

        @pl.when(kk > 0)
        def _():
            acc[...] += part

        @pl.when(kk == nk - 1)
        def _():
            finish(acc[...])

    a_spec = pl.BlockSpec((tk, tm), lambda i, j, kk: (kk, i)) if ta else pl.BlockSpec((tm, tk), lambda i, j, kk: (i, kk))
    b_spec = pl.BlockSpec((tn, tk), lambda i, j, kk: (j, kk)) if tb else pl.BlockSpec((tk, tn), lambda i, j, kk: (kk, j))
    o_spec = pl.BlockSpec((tm, tn), lambda i, j, kk: (i, j))
    in_specs = [a_spec, b_spec] + ([o_spec] if has_res else [])
    args = (a, b) + ((res,) if has_res else ())
    return pl.pallas_call(
        body, name=name, grid=(m // tm, n // tn, nk), in_specs=in_specs, out_specs=o_spec,
        out_shape=jax.ShapeDtypeStruct((m, n), out_dtype),
        scratch_shapes=[pltpu.VMEM((tm, tn), F32)] if nk > 1 else [],
        compiler_params=_cp("parallel", "parallel", "arbitrary"),
    )(*args)


def _norm_fwd(x, g, name):
    s, d = x.shape
    tr = min(512, s)

    def body(x_ref, g_ref, o_ref):
        xv = x_ref[...]
        r = lax.rsqrt(jnp.mean(xv * xv, axis=-1, keepdims=True) + EPS)
        o_ref[...] = (xv * r * g_ref[...]).astype(BF16)

    row = pl.BlockSpec((tr, d), lambda i: (i, 0))
    return pl.pallas_call(
        body, name=name, grid=(s // tr,), in_specs=[row, pl.BlockSpec((1, d), lambda i: (0, 0))],
        out_specs=row, out_shape=jax.ShapeDtypeStruct((s, d), BF16), compiler_params=_cp("parallel"),
    )(x, g.reshape(1, d))


def _norm_bwd(x, g, dh, dres, name):
    s, d = x.shape
    tr = min(512, s)

    def body(x_ref, g_ref, dh_ref, dres_ref, dx_ref, dg_ref):
        @pl.when(pl.program_id(0) == 0)
        def _():
            dg_ref[...] = jnp.zeros_like(dg_ref)

        xv = x_ref[...]
        r = lax.rsqrt(jnp.mean(xv * xv, axis=-1, keepdims=True) + EPS)
        xn = xv * r
        dhv = dh_ref[...].astype(F32)
        dg_ref[...] += jnp.sum(dhv * xn, axis=0, keepdims=True)
        dhg = dhv * g_ref[...]
        dx_ref[...] = dres_ref[...] + r * (dhg - xn * jnp.mean(dhg * xn, axis=-1, keepdims=True))

    row = pl.BlockSpec((tr, d), lambda i: (i, 0))
    vec = pl.BlockSpec((1, d), lambda i: (0, 0))
    return pl.pallas_call(
        body, name=name, grid=(s // tr,), in_specs=[row, vec, row, row], out_specs=[row, vec],
        out_shape=[jax.ShapeDtypeStruct((s, d), F32), jax.ShapeDtypeStruct((1, d), F32)],
        compiler_params=_cp("arbitrary"),
    )(x, g.reshape(1, d), dh, dres)


def _loss_head(x, g, tgt, name):
    s, d = x.shape
    tr = min(512, s)

    def body(x_ref, g_ref, t_ref, l_ref, dx_ref, dg_ref):
        @pl.when(pl.program_id(0) == 0)
        def _():
            dg_ref[...] = jnp.zeros_like(dg_ref)
            l_ref[...] = jnp.zeros_like(l_ref)

        xv = x_ref[...]
        r = lax.rsqrt(jnp.mean(xv * xv, axis=-1, keepdims=True) + EPS)
        xn = xv * r
        err = xn * g_ref[...] - t_ref[...]
        part = 0.5 * jnp.sum(jnp.mean(err * err, axis=-1, keepdims=True), axis=0, keepdims=True)
        l_ref[...] += jnp.broadcast_to(part, l_ref.shape)
        dy = err * (1.0 / d)
        dg_ref[...] += jnp.sum(dy * xn, axis=0, keepdims=True)
        dyg = dy * g_ref[...]
        dx_ref[...] = r * (dyg - xn * jnp.mean(dyg * xn, axis=-1, keepdims=True))

    row = pl.BlockSpec((tr, d), lambda i: (i, 0))
    vec = pl.BlockSpec((1, d), lambda i: (0, 0))
    return pl.pallas_call(
        body, name=name, grid=(s // tr,), in_specs=[row, vec, row],
        out_specs=[pl.BlockSpec((1, LANES), lambda i: (0, 0)), row, vec],
        out_shape=[jax.ShapeDtypeStruct((1, LANES), F32), jax.ShapeDtypeStruct((s, d), F32),
                   jax.ShapeDtypeStruct((1, d), F32)],
        compiler_params=_cp("arbitrary"),
    )(x, g.reshape(1, d), tgt)


def _sum_slots(landed_slot, own):
    me = _slot(*_mesh_pos())
    mine = own.astype(F32)
    g = jnp.where(me == 0, mine, landed_slot(0).astype(F32))
    for i in range(1, N_DEV):
        g = g + jnp.where(me == i, mine, landed_slot(i).astype(F32))
    return g


def _adam_update(g, w, m, v):
    mm = ADAM_B1 * m + (1.0 - ADAM_B1) * g
    vv = ADAM_B2 * v + (1.0 - ADAM_B2) * (g * g)
    m_hat = mm / (1.0 - ADAM_B1 ** ADAM_STEP)
    v_hat = vv / (1.0 - ADAM_B2 ** ADAM_STEP)
    return g, -ADAM_LR * (m_hat / (jnp.sqrt(v_hat) + ADAM_EPS) + ADAM_WD * w), mm, vv


def _adamw_rows(landed, own, ws, ms, vs, name):
    k = len(ws)
    sizes = [w.shape[1] for w in ws]

    def body(*refs):
        p_ref, o_ref = refs[:2]
        w_refs, m_refs, v_refs = refs[2:2 + k], refs[2 + k:2 + 2 * k], refs[2 + 2 * k:2 + 3 * k]
        outs = refs[2 + 3 * k:]
        for i, n in enumerate(sizes):
            g = _sum_slots(lambda s: p_ref[s, i:i + 1, :n], o_ref[i:i + 1, :n])
            res = _adam_update(g, w_refs[i][...], m_refs[i][...], v_refs[i][...])
            for j in range(4):
                outs[j * k + i][...] = res[j]

    return pl.pallas_call(
        body, name=name, out_shape=[jax.ShapeDtypeStruct((1, n), F32) for _ in range(4) for n in sizes],
    )(landed, own, *ws, *ms, *vs)


def _adamw(landed, own, w, m, v, name):
    r, c = w.shape
    cap = max(8, 256 * 1024 // c)
    tr = max(t for t in range(8, min(r, cap) + 1, 8) if r % t == 0) if r % 8 == 0 else r

    def body(p_ref, o_ref, w_ref, m_ref, v_ref, g_ref, d_ref, nm_ref, nv_ref):
        g = _sum_slots(lambda i: p_ref[i], o_ref[...])
        g_ref[...], d_ref[...], nm_ref[...], nv_ref[...] = _adam_update(g, w_ref[...], m_ref[...], v_ref[...])

    blk = pl.BlockSpec((tr, c), lambda i: (i, 0))
    return pl.pallas_call(
        body, name=name, grid=(r // tr,),
        in_specs=[pl.BlockSpec((N_DEV, tr, c), lambda i: (0, i, 0)), blk, blk, blk, blk],
        out_specs=[blk] * 4, out_shape=[jax.ShapeDtypeStruct((r, c), F32)] * 4,
        compiler_params=_cp("parallel"),
    )(landed, own, w, m, v)


def _conv_fwd(x, w_ref, kw):
    acc = w_ref[kw - 1:kw, :] * x
    for j in range(kw - 1):
        acc = acc + w_ref[j:j + 1, :] * _shift_down(x, kw - 1 - j)
    return acc


def _conv_bwd(x, dy, w_ref, dw_ref, kw):
    dx = w_ref[kw - 1:kw, :] * dy
    dw_ref[kw - 1:kw, :] = jnp.sum(dy * x, axis=0, keepdims=True)
    for j in range(kw - 1):
        dx = dx + w_ref[j:j + 1, :] * _shift_up(dy, kw - 1 - j)
        dw_ref[j:j + 1, :] = jnp.sum(dy * _shift_down(x, kw - 1 - j), axis=0, keepdims=True)
    return dx


def _ffn_act_fwd(pre, cw, name):
    s, f2 = pre.shape
    nt = f2 // 2 // LANES

    def body(pu_ref, pg_ref, wu_ref, wg_ref, o_ref):
        up = _conv_fwd(pu_ref[...].astype(F32), wu_ref, FFN_CONV)
        gate = _conv_fwd(pg_ref[...].astype(F32), wg_ref, FFN_CONV)
        o_ref[...] = (gate * _sigmoid(gate) * up).astype(BF16)

    def col(rows, off):
        return pl.BlockSpec((rows, LANES), lambda j: (0, j + off))

    return pl.pallas_call(
        body, name=name, grid=(nt,),
        in_specs=[col(s, 0), col(s, nt), col(FFN_CONV, 0), col(FFN_CONV, nt)], out_specs=col(s, 0),
        out_shape=jax.ShapeDtypeStruct((s, f2 // 2), BF16), compiler_params=_cp("parallel"),
    )(pre, pre, cw, cw)


def _ffn_act_bwd(pre, cw, dact, name):
    s, f2 = pre.shape
    f = f2 // 2
    nt = f // LANES

    def body(pu_ref, pg_ref, wu_ref, wg_ref, da_ref, dpu_ref, dpg_ref, dwu_ref, dwg_ref):
        pu, pg = pu_ref[...].astype(F32), pg_ref[...].astype(F32)
        up = _conv_fwd(pu, wu_ref, FFN_CONV)
        gate = _conv_fwd(pg, wg_ref, FFN_CONV)
        sg = _sigmoid(gate)
        da = da_ref[...].astype(F32)
        dup = da * gate * sg
        dgate = da * up * (sg * (1.0 + gate * (1.0 - sg)))
        dpu_ref[...] = _conv_bwd(pu, dup, wu_ref, dwu_ref, FFN_CONV).astype(BF16)
        dpg_ref[...] = _conv_bwd(pg, dgate, wg_ref, dwg_ref, FFN_CONV).astype(BF16)

    def col(rows, off):
        return pl.BlockSpec((rows, LANES), lambda j: (0, j + off))

    return pl.pallas_call(
        body, name=name, grid=(nt,),
        in_specs=[col(s, 0), col(s, nt), col(FFN_CONV, 0), col(FFN_CONV, nt), col(s, 0)],
        out_specs=[col(s, 0), col(s, 0), col(FFN_CONV, 0), col(FFN_CONV, 0)],
        out_shape=[jax.ShapeDtypeStruct((s, f), BF16), jax.ShapeDtypeStruct((s, f), BF16),
                   jax.ShapeDtypeStruct((FFN_CONV, f), F32), jax.ShapeDtypeStruct((FFN_CONV, f), F32)],
        compiler_params=_cp("parallel"),
    )(pre, pre, cw, cw, dact)


def _xa_probs(qh, kh):
    sc = _mxu(qh, kh, 1, 1) * (XA_DH ** -0.5)
    e = jnp.exp(sc - jnp.max(sc, axis=-1, keepdims=True))
    return e / jnp.sum(e, axis=-1, keepdims=True)


def _xattn_fwd(q, kv, name):
    s, d = q.shape
    m = kv.shape[0]
    tr = min(512, s)

    def body(q_ref, kv_ref, o_ref):
        for h in range(XA_HEADS):
            lo, hi = h * XA_DH, (h + 1) * XA_DH
            p = _xa_probs(q_ref[:, lo:hi], kv_ref[:, lo:hi])
            o_ref[:, lo:hi] = _mxu(p, kv_ref[:, d + lo:d + hi]).astype(BF16)

    row = pl.BlockSpec((tr, d), lambda i: (i, 0))
    return pl.pallas_call(
        body, name=name, grid=(s // tr,), in_specs=[row, pl.BlockSpec((m, 2 * d), lambda i: (0, 0))],
        out_specs=row, out_shape=jax.ShapeDtypeStruct((s, d), BF16), compiler_params=_cp("parallel"),
    )(q, kv)


def _xattn_bwd(q, kv, do, name):
    s, d = q.shape
    m = kv.shape[0]
    tr = min(512, s)

    def body(q_ref, kv_ref, do_ref, dq_ref, dkv_ref):
        @pl.when(pl.program_id(0) == 0)
        def _():
            dkv_ref[...] = jnp.zeros_like(dkv_ref)

        for h in range(XA_HEADS):
            lo, hi = h * XA_DH, (h + 1) * XA_DH
            qh, kh, vh = q_ref[:, lo:hi], kv_ref[:, lo:hi], kv_ref[:, d + lo:d + hi]
            doh = do_ref[:, lo:hi]
            p = _xa_probs(qh, kh)
            dp = _mxu(doh, vh, 1, 1)
            ds = p * (dp - jnp.sum(p * dp, axis=-1, keepdims=True)) * (XA_DH ** -0.5)
            dq_ref[:, lo:hi] = _mxu(ds, kh).astype(BF16)
            dkv_ref[:, lo:hi] += _mxu(ds, qh, 0, 0)
            dkv_ref[:, d + lo:d + hi] += _mxu(p, doh, 0, 0)

    row = pl.BlockSpec((tr, d), lambda i: (i, 0))
    full = pl.BlockSpec((m, 2 * d), lambda i: (0, 0))
    return pl.pallas_call(
        body, name=name, grid=(s // tr,), in_specs=[row, full, row], out_specs=[row, full],
        out_shape=[jax.ShapeDtypeStruct((s, d), BF16), jax.ShapeDtypeStruct((m, 2 * d), F32)],
        compiler_params=_cp("arbitrary"),
    )(q, kv, do)


def _ret_tables():
    c = RET_CHUNK
    lg = np.log1p(-np.exp2(-5.0 - np.arange(RET_HEADS, dtype=np.float32))).astype(np.float32)
    idx = np.arange(c, dtype=np.float32)
    diff = idx[:, None] - idx[None, :]
    intra = np.where(diff >= 0, np.exp(lg[:, None, None] * np.where(diff >= 0, diff, 0.0)), 0.0)
    rk = np.broadcast_to(np.exp(lg[:, None] * (c - 1 - idx))[:, :, None], (RET_HEADS, c, LANES))
    rq = np.broadcast_to(np.exp(lg[:, None] * (idx + 1))[:, :, None], (RET_HEADS, c, LANES))
    return jnp.asarray(np.stack([intra, rk, rq], axis=1).astype(np.float32))


def _rope_tables(s):
    half = RET_DH // 2
    inv = jnp.exp(-math.log(10000.0) * jnp.arange(half, dtype=F32) / half)
    ang = jnp.arange(s, dtype=F32)[:, None] * inv[None, :]
    cos, sin = jnp.cos(ang), jnp.sin(ang)
    return jnp.concatenate([cos, cos], axis=1), jnp.concatenate([-sin, sin], axis=1)


def _ret_specs(n_of):
    c, w = RET_CHUNK, RET_HEADS * RET_DH

    def part(off):
        return pl.BlockSpec((c, w), lambda n: (n_of(n), off))

    pos = pl.BlockSpec((c, RET_DH), lambda n: (n_of(n), 0))
    gain = pl.BlockSpec((1, w), lambda n: (0, 0))
    tab = pl.BlockSpec((RET_HEADS, 3, c, LANES), lambda n: (0, 0, 0, 0))
    st = pl.BlockSpec((RET_HEADS, None, RET_DH, RET_DH), lambda n: (0, n_of(n), 0, 0))
    return part, pos, gain, tab, st


def _rheads(x):
    return jnp.stack([x[:, h * RET_DH:(h + 1) * RET_DH] for h in range(RET_HEADS)], axis=0)


def _runheads(x):
    return jnp.concatenate([x[h] for h in range(RET_HEADS)], axis=1)


def _rope(x, cos, sin):
    return x * cos + pltpu.roll(x, RET_DH // 2, 2) * sin


def _ret_chunk(q_ref, k_ref, v_ref, cos_ref, sin_ref, tab_ref, prev):
    cos, sin = cos_ref[...], sin_ref[...]
    q = _rope(_rheads(q_ref[...]), cos, sin)
    k = _rope(_rheads(k_ref[...]), cos, sin) * (RET_DH ** -0.5)
    v = _rheads(v_ref[...])
    scores = _bmxu(q, k, 2, 2) * tab_ref[:, 0]
    qdec = q * tab_ref[:, 2]
    kdec = k * tab_ref[:, 1]
    o = _bmxu(scores, v) + _bmxu(qdec, prev)
    return q, k, v, scores, qdec, kdec, o


def _ret_fwd(proj, cos, sin, gain, name):
    s = proj.shape[0]
    c = RET_CHUNK
    nc = s // c
    part, pos, gvec, tab, st = _ret_specs(lambda n: n)

    def body(q_ref, k_ref, v_ref, g_ref, cos_ref, sin_ref, rn_ref, tab_ref, o_ref, st_ref, state):
        @pl.when(pl.program_id(0) == 0)
        def _():
            state[...] = jnp.zeros_like(state)

        prev = state[...]
        st_ref[...] = prev
        _, _, v, _, _, kdec, o = _ret_chunk(q_ref, k_ref, v_ref, cos_ref, sin_ref, tab_ref, prev)
        state[...] = prev * tab_ref[:, 2, c - 1:c, :] + _bmxu(kdec, v, 1, 1)
        r = lax.rsqrt(jnp.mean(o * o, axis=-1, keepdims=True) + EPS)
        gate = g_ref[...]
        o_ref[...] = (_runheads(o * r) * rn_ref[...] * (gate * _sigmoid(gate))).astype(BF16)

    return pl.pallas_call(
        body, name=name, grid=(nc,),
        in_specs=[part(0), part(1), part(2), part(3), pos, pos, gvec, tab],
        out_specs=[part(0), st],
        out_shape=[jax.ShapeDtypeStruct((s, RET_HEADS * RET_DH), BF16),
                   jax.ShapeDtypeStruct((RET_HEADS, nc, RET_DH, RET_DH), F32)],
        scratch_shapes=[pltpu.VMEM((RET_HEADS, RET_DH, RET_DH), F32)],
        compiler_params=_cp("arbitrary"),
    )(proj, proj, proj, proj, cos, sin, gain.reshape(1, -1), _ret_tables())


def _ret_bwd(proj, cos, sin, gain, states, dmerged, name):
    s = proj.shape[0]
    c = RET_CHUNK
    nc = s // c
    part, pos, gvec, tab, st = _ret_specs(lambda n: nc - 1 - n)

    def body(q_ref, k_ref, v_ref, g_ref, cos_ref, sin_ref, rn_ref, tab_ref, st_ref, do_ref,
             dq_ref, dk_ref, dv_ref, dg_ref, drn_ref, carry):
        @pl.when(pl.program_id(0) == 0)
        def _():
            carry[...] = jnp.zeros_like(carry)
            drn_ref[...] = jnp.zeros_like(drn_ref)

        prev = st_ref[...]
        q, k, v, scores, qdec, kdec, o = _ret_chunk(q_ref, k_ref, v_ref, cos_ref, sin_ref, tab_ref, prev)
        r = lax.rsqrt(jnp.mean(o * o, axis=-1, keepdims=True) + EPS)
        on = o * r
        on2 = _runheads(on)
        gate = g_ref[...]
        sg = _sigmoid(gate)
        sil = gate * sg
        dout = do_ref[...]
        rn = rn_ref[...]
        dg_ref[...] = (dout * on2 * rn * (sg * (1.0 + gate * (1.0 - sg)))).astype(BF16)
        drn_ref[...] += jnp.sum(dout * on2 * sil, axis=0, keepdims=True)
        don = _rheads(dout * rn * sil)
        do = r * (don - on * jnp.mean(don * on, axis=-1, keepdims=True))
        dc = carry[...]
        dsc = _bmxu(do, v, 2, 2) * tab_ref[:, 0]
        dq = _bmxu(dsc, k) + _bmxu(do, prev, 2, 2) * tab_ref[:, 2]
        dk = _bmxu(dsc, q, 1, 1) + _bmxu(v, dc, 2, 2) * tab_ref[:, 1]
        dv = _bmxu(scores, do, 1, 1) + _bmxu(kdec, dc)
        carry[...] = _bmxu(qdec, do, 1, 1) + dc * tab_ref[:, 2, c - 1:c, :]
        cos, sin = cos_ref[...], sin_ref[...]
        dk = dk * (RET_DH ** -0.5)
        dq_ref[...] = _runheads(dq * cos + pltpu.roll(dq * sin, RET_DH // 2, 2)).astype(BF16)
        dk_ref[...] = _runheads(dk * cos + pltpu.roll(dk * sin, RET_DH // 2, 2)).astype(BF16)
        dv_ref[...] = _runheads(dv).astype(BF16)

    width = RET_HEADS * RET_DH
    return pl.pallas_call(
        body, name=name, grid=(nc,),
        in_specs=[part(0), part(1), part(2), part(3), pos, pos, gvec, tab, st, part(0)],
        out_specs=[part(0)] * 4 + [gvec],
        out_shape=[jax.ShapeDtypeStruct((s, width), BF16)] * 4 + [jax.ShapeDtypeStruct((1, width), F32)],
        scratch_shapes=[pltpu.VMEM((RET_HEADS, RET_DH, RET_DH), F32)],
        compiler_params=_cp("arbitrary"),
    )(proj, proj, proj, proj, cos, sin, gain.reshape(1, -1), _ret_tables(), states, dmerged)


S5_TILE = 512


def _cmul_add(xr, xi, ar, ai, yr, yi):
    return xr + ar * yr - ai * yi, xi + ar * yi + ai * yr


def _s5_scan_fwd(bu, apow, name):
    s, w2 = bu.shape
    r = SCAN_ROWS
    t = S5_TILE
    steps = r.bit_length() - 1

    def body(b_ref, p_ref, o_ref, cr, ci):
        @pl.when(pl.program_id(1) == 0)
        def _():
            cr[...] = jnp.zeros_like(cr)
            ci[...] = jnp.zeros_like(ci)

        xr, xi = b_ref[:, :t], b_ref[:, t:]
        for k in range(steps):
            sh = 1 << k
            xr, xi = _cmul_add(xr, xi, p_ref[sh - 1:sh, :t], p_ref[sh - 1:sh, t:],
                               _shift_down(xr, sh), _shift_down(xi, sh))
        xr, xi = _cmul_add(xr, xi, p_ref[:, :t], p_ref[:, t:], cr[...], ci[...])
        o_ref[:, :t] = xr
        o_ref[:, t:] = xi
        cr[...] = xr[r - 1:r, :]
        ci[...] = xi[r - 1:r, :]

    blk = pl.BlockSpec((r, 2 * t), lambda j, i: (i, j))
    return pl.pallas_call(
        body, name=name, grid=(w2 // (2 * t), s // r),
        in_specs=[blk, pl.BlockSpec((r, 2 * t), lambda j, i: (0, j))], out_specs=blk,
        out_shape=jax.ShapeDtypeStruct((s, w2), F32),
        scratch_shapes=[pltpu.VMEM((1, t), F32), pltpu.VMEM((1, t), F32)],
        compiler_params=_cp("parallel", "arbitrary"),
    )(bu, apow)


def _s5_scan_bwd(dst, apow_rev, st, name):
    s, w2 = dst.shape
    r = SCAN_ROWS
    t = S5_TILE
    nb = s // r
    steps = r.bit_length() - 1

    def body(d_ref, p_ref, s_ref, sp_ref, g_ref, da_ref, cr, ci):
        i = pl.program_id(1)

        @pl.when(i == 0)
        def _():
            cr[...] = jnp.zeros_like(cr)
            ci[...] = jnp.zeros_like(ci)
            da_ref[...] = jnp.zeros_like(da_ref)

        xr, xi = d_ref[:, :t], d_ref[:, t:]
        for k in range(steps):
            sh = 1 << k
            xr, xi = _cmul_add(xr, xi, p_ref[r - sh:r - sh + 1, :t], p_ref[r - sh:r - sh + 1, t:],
                               _shift_up(xr, sh), _shift_up(xi, sh))
        xr, xi = _cmul_add(xr, xi, p_ref[:, :t], p_ref[:, t:], cr[...], ci[...])
        g_ref[:, :t] = xr.astype(BF16)
        g_ref[:, t:] = xi.astype(BF16)
        cr[...] = xr[0:1, :]
        ci[...] = xi[0:1, :]
        first = i == nb - 1
        row = lax.broadcasted_iota(jnp.int32, (r, t), 0)
        last_r = jnp.where(first, 0.0, sp_ref[7:8, :t])
        last_i = jnp.where(first, 0.0, sp_ref[7:8, t:])
        pr = jnp.where(row == 0, last_r, pltpu.roll(s_ref[:, :t], 1, 0))
        pi = jnp.where(row == 0, last_i, pltpu.roll(s_ref[:, t:], 1, 0))
        da_ref[:, :t] += jnp.sum(xr * pr + xi * pi, axis=0, keepdims=True)
        da_ref[:, t:] += jnp.sum(xi * pr - xr * pi, axis=0, keepdims=True)

    blk = pl.BlockSpec((r, 2 * t), lambda j, i: (nb - 1 - i, j))
    halo = pl.BlockSpec((8, 2 * t), lambda j, i: (jnp.maximum((nb - 1 - i) * (r // 8) - 1, 0), j))
    vec = pl.BlockSpec((1, 2 * t), lambda j, i: (0, j))
    return pl.pallas_call(
        body, name=name, grid=(w2 // (2 * t), nb),
        in_specs=[blk, pl.BlockSpec((r, 2 * t), lambda j, i: (0, j)), blk, halo], out_specs=[blk, vec],
        out_shape=[jax.ShapeDtypeStruct((s, w2), BF16), jax.ShapeDtypeStruct((1, w2), F32)],
        scratch_shapes=[pltpu.VMEM((1, t), F32), pltpu.VMEM((1, t), F32)],
        compiler_params=_cp("parallel", "arbitrary"),
    )(dst, apow_rev, st, st)


_GELU_C = math.sqrt(2.0 / math.pi)
_GELU_A = 0.044715


def _gelu(y):
    return 0.5 * y * (1.0 + jnp.tanh(_GELU_C * (y + _GELU_A * y * y * y)))


def _gelu_grad(y):
    th = jnp.tanh(_GELU_C * (y + _GELU_A * y * y * y))
    return 0.5 * (1.0 + th) + 0.5 * y * (1.0 - th * th) * _GELU_C * (1.0 + 3.0 * _GELU_A * y * y)


def _row_call(body, name, s, ins, outs, acc=False):
    tr = min(512, s)

    def spec(width, cb, rows):
        if rows == 1:
            return pl.BlockSpec((1, width), lambda i: (0, cb))
        return pl.BlockSpec((tr, width), lambda i: (i, cb))

    in_specs = [spec(w, cb, a.shape[0]) for a, w, cb in ins]
    out_specs = [spec(w, cb, sd.shape[0]) for sd, w, cb in outs]
    return pl.pallas_call(
        body, name=name, grid=(s // tr,), in_specs=in_specs, out_specs=out_specs,
        out_shape=[sd for sd, _, _ in outs],
        compiler_params=_cp("arbitrary" if acc else "parallel"),
    )(*[a for a, _, _ in ins])


def _sds(shape, dtype):
    return jax.ShapeDtypeStruct(shape, dtype)


def _s5_gelu_fwd(yraw, proj, dvec, name):
    s, w = yraw.shape

    def body(y_ref, u_ref, d_ref, yo_ref, g_ref):
        y = y_ref[...] + d_ref[...] * u_ref[...]
        yo_ref[...] = y
        g_ref[...] = _gelu(y).astype(BF16)

    return _row_call(body, name, s, [(yraw, w, 0), (proj, w, 4), (dvec, w, 0)],
                     [(_sds((s, w), F32), w, 0), (_sds((s, w), BF16), w, 0)])


def _s5_glu_fwd(y, z, b, name):
    s, w = y.shape

    def body(y_ref, z_ref, b_ref, o_ref):
        o_ref[...] = (_gelu(y_ref[...]) * _sigmoid(z_ref[...] + b_ref[...])).astype(BF16)

    return _row_call(body, name, s, [(y, w, 0), (z, w, 0), (b, w, 0)], [(_sds((s, w), BF16), w, 0)])[0]


def _s5_glu_bwd(dmerged, y, z, b, name):
    s, w = y.shape

    def body(do_ref, y_ref, z_ref, b_ref, dz_ref, dg_ref, db_ref):
        @pl.when(pl.program_id(0) == 0)
        def _():
            db_ref[...] = jnp.zeros_like(db_ref)

        g = _gelu(y_ref[...])
        sg = _sigmoid(z_ref[...] + b_ref[...])
        dout = do_ref[...]
        dz = dout * g * sg * (1.0 - sg)
        dz_ref[...] = dz.astype(BF16)
        dg_ref[...] = dout * sg
        db_ref[...] += jnp.sum(dz, axis=0, keepdims=True)

    return _row_call(body, name, s, [(dmerged, w, 1), (y, w, 0), (z, w, 0), (b, w, 0)],
                     [(_sds((s, w), BF16), w, 0), (_sds((s, w), F32), w, 0), (_sds((1, w), F32), w, 0)], acc=True)


def _s5_gelu_bwd(dg1, dg2, y, proj, dvec, name):
    s, w = y.shape

    def body(a_ref, b_ref, y_ref, u_ref, d_ref, dy_ref, du_ref, dd_ref):
        @pl.when(pl.program_id(0) == 0)
        def _():
            dd_ref[...] = jnp.zeros_like(dd_ref)

        dy = (a_ref[...] + b_ref[...]) * _gelu_grad(y_ref[...])
        dy_ref[...] = dy.astype(BF16)
        du_ref[...] = dy * d_ref[...]
        dd_ref[...] += jnp.sum(dy * u_ref[...], axis=0, keepdims=True)

    return _row_call(body, name, s, [(dg1, w, 0), (dg2, w, 0), (y, w, 0), (proj, w, 4), (dvec, w, 0)],
                     [(_sds((s, w), BF16), w, 0), (_sds((s, w), F32), w, 0), (_sds((1, w), F32), w, 0)], acc=True)


def _gdn_conv_fwd(projx, cw, name):
    s = projx.shape[0]
    nh = GDN_HEADS

    def body(x_ref, w_ref, o_ref):
        j = pl.program_id(0)
        cv = _conv_fwd(x_ref[...], w_ref, GDN_CONV)
        y = cv * _sigmoid(cv)
        nrm = y * lax.rsqrt(jnp.sum(y * y, axis=-1, keepdims=True) + EPS)
        o_ref[...] = jnp.where(j < nh, nrm * (GDN_DH ** -0.5), jnp.where(j < 2 * nh, nrm, y))

    return pl.pallas_call(
        body, name=name, grid=(3 * nh,),
        in_specs=[pl.BlockSpec((s, GDN_DH), lambda j: (0, j)), pl.BlockSpec((GDN_CONV, GDN_DH), lambda j: (0, j))],
        out_specs=pl.BlockSpec((s, GDN_DH), lambda j: (0, j)),
        out_shape=jax.ShapeDtypeStruct((s, 3 * nh * GDN_DH), F32), compiler_params=_cp("parallel"),
    )(projx, cw)


def _gdn_conv_bwd(projx, cw, dqkv, name):
    s = projx.shape[0]
    nh = GDN_HEADS

    def body(x_ref, w_ref, d_ref, dx_ref, dw_ref):
        j = pl.program_id(0)
        x = x_ref[...]
        cv = _conv_fwd(x, w_ref, GDN_CONV)
        sg = _sigmoid(cv)
        y = cv * sg
        rinv = lax.rsqrt(jnp.sum(y * y, axis=-1, keepdims=True) + EPS)
        nrm = y * rinv
        dn = d_ref[...]
        dns = jnp.where(j < nh, dn * (GDN_DH ** -0.5), dn)
        dyn = rinv * (dns - nrm * jnp.sum(dns * nrm, axis=-1, keepdims=True))
        dy = jnp.where(j < 2 * nh, dyn, dn)
        dc = dy * (sg * (1.0 + cv * (1.0 - sg)))
        dx_ref[...] = _conv_bwd(x, dc, w_ref, dw_ref, GDN_CONV).astype(BF16)

    col = pl.BlockSpec((s, GDN_DH), lambda j: (0, j))
    wcol = pl.BlockSpec((GDN_CONV, GDN_DH), lambda j: (0, j))
    return pl.pallas_call(
        body, name=name, grid=(3 * nh,), in_specs=[col, wcol, col], out_specs=[col, wcol],
        out_shape=[jax.ShapeDtypeStruct((s, 3 * nh * GDN_DH), BF16), jax.ShapeDtypeStruct((GDN_CONV, 3 * nh * GDN_DH), F32)],
        compiler_params=_cp("parallel"),
    )(projx, cw, dqkv)


def _softplus(x):
    return jnp.maximum(x, 0.0) + jnp.log1p(jnp.exp(-jnp.abs(x)))


def _gdn_gates_fwd(projx, alog, dtb, name):
    s = projx.shape[0]
    w = GDN_HEADS * GDN_DH

    def body(b_ref, a_ref, al_ref, dt_ref, bo_ref, go_ref):
        bo_ref[...] = _sigmoid(b_ref[...])
        go_ref[...] = -jnp.exp(al_ref[...]) * _softplus(a_ref[...] + dt_ref[...])

    return _row_call(body, name, s, [(projx, w, 4), (projx, w, 5), (alog, w, 0), (dtb, w, 0)],
                     [(_sds((s, w), F32), w, 0), (_sds((s, w), F32), w, 0)])


def _gdn_gates_bwd(projx, alog, dtb, dbeta, dg, name):
    s = projx.shape[0]
    w = GDN_HEADS * GDN_DH

    def body(b_ref, a_ref, al_ref, dt_ref, dbe_ref, dg_ref, db_ref, da_ref, dal_ref, ddt_ref):
        @pl.when(pl.program_id(0) == 0)
        def _():
            dal_ref[...] = jnp.zeros_like(dal_ref)
            ddt_ref[...] = jnp.zeros_like(ddt_ref)

        for h in range(GDN_HEADS):
            lo, hi = h * GDN_DH, (h + 1) * GDN_DH
            beta = _sigmoid(b_ref[:, lo:hi])
            pb = jnp.sum(dbe_ref[:, lo:hi], axis=-1, keepdims=True) * (1.0 / GDN_DH)
            db_ref[:, lo:hi] = (pb * beta * (1.0 - beta)).astype(BF16)
            xa = a_ref[:, lo:hi] + dt_ref[:, lo:hi]
            ea = -jnp.exp(al_ref[:, lo:hi])
            pg = jnp.sum(dg_ref[:, lo:hi], axis=-1, keepdims=True) * (1.0 / GDN_DH)
            da = pg * ea * _sigmoid(xa)
            da_ref[:, lo:hi] = da.astype(BF16)
            dal_ref[:, lo:hi] += jnp.sum(pg * ea * _softplus(xa), axis=0, keepdims=True)
            ddt_ref[:, lo:hi] += jnp.sum(da, axis=0, keepdims=True)

    return _row_call(body, name, s,
                     [(projx, w, 4), (projx, w, 5), (alog, w, 0), (dtb, w, 0), (dbeta, w, 0), (dg, w, 0)],
                     [(_sds((s, w), BF16), w, 0), (_sds((s, w), BF16), w, 0),
                      (_sds((1, w), F32), w, 0), (_sds((1, w), F32), w, 0)], acc=True)


def _gdn_tri():
    c = GDN_CHUNK
    i = lax.broadcasted_iota(jnp.int32, (c, c), 0)
    j = lax.broadcasted_iota(jnp.int32, (c, c), 1)
    return ((i >= j).astype(F32), (i <= j).astype(F32), i >= j, i > j, (i == j).astype(F32))


def _bdot(a, b, ca=2, cb=1, precision=None):
    return lax.dot_general(a, b, (((ca,), (cb,)), ((0,), (0,))), precision=precision, preferred_element_type=F32)


def _bmxu(a, b, ca=2, cb=1):
    return _bdot(a.astype(BF16), b.astype(BF16), ca, cb)


def _heads(x):
    return jnp.stack([x[:, h * GDN_DH:(h + 1) * GDN_DH] for h in range(GDN_HEADS)], axis=0)


def _unheads(x):
    return jnp.concatenate([x[h] for h in range(GDN_HEADS)], axis=1)


def _gdn_chunk(q, k, v, bb, g2d, tri):
    low, up, incl, strict, eye = tri
    c = GDN_CHUNK
    gc = _heads(_dot(low, g2d, precision=HI))
    gci = gc[:, :, :c]
    gdiff = gci - jnp.swapaxes(gci, 1, 2)
    decay = jnp.where(incl, jnp.exp(jnp.where(incl, gdiff, 0.0)), 0.0)
    kb, vb = k * bb, v * bb
    kbk = _bmxu(kb, k, 2, 2)
    x = -jnp.where(strict, kbk * decay, 0.0)
    t = eye + x
    p = x
    for _ in range(c.bit_length() - 2):
        p = _bdot(p, p, precision=HI)
        t = t + _bdot(t, p, precision=HI)
    eg = jnp.exp(gc)
    kbg = kb * eg
    gcl = gc[:, c - 1:c, :]
    ek = jnp.exp(gcl - gc)
    qkraw = _bmxu(q, k, 2, 2)
    return dict(decay=decay, kb=kb, vb=vb, kbk=kbk, t=t, eg=eg, kbg=kbg, ek=ek, gl=jnp.exp(gcl),
                w=_bmxu(t, kbg), u=_bmxu(t, vb), qkraw=qkraw, qk=jnp.where(incl, qkraw * decay, 0.0),
                qd=q * eg, kd=k * ek)


def _gdn_specs(n_of):
    c, w = GDN_CHUNK, GDN_HEADS * GDN_DH

    def blk(cb, width=w):
        return pl.BlockSpec((c, width), lambda n: (n_of(n), cb))

    st = pl.BlockSpec((None, GDN_HEADS, GDN_DH, GDN_DH), lambda n: (n_of(n), 0, 0, 0))
    vec = pl.BlockSpec((1, GDN_DH), lambda n: (0, 0))
    return blk, st, vec


def _gdn_load(qkv_ref, b_ref, g_ref, tri):
    w = GDN_HEADS * GDN_DH
    q, k, v = _heads(qkv_ref[:, :w]), _heads(qkv_ref[:, w:2 * w]), _heads(qkv_ref[:, 2 * w:])
    bb = _heads(b_ref[...])
    return q, k, v, bb, _gdn_chunk(q, k, v, bb, g_ref[...], tri)


def _gdn_fwd(qkv, beta, g, projx, onorm, name):
    s = qkv.shape[0]
    nc = s // GDN_CHUNK
    w = GDN_HEADS * GDN_DH
    blk, st, vec = _gdn_specs(lambda n: n)

    def body(qkv_ref, b_ref, g_ref, z_ref, on_ref, o_ref, st_ref, state):
        @pl.when(pl.program_id(0) == 0)
        def _():
            state[...] = jnp.zeros_like(state)

        _, _, _, _, ch = _gdn_load(qkv_ref, b_ref, g_ref, _gdn_tri())
        sp = state[...]
        st_ref[...] = sp
        vn = ch["u"] - _bmxu(ch["w"], sp)
        o = _bmxu(ch["qd"], sp) + _bmxu(ch["qk"], vn)
        state[...] = sp * ch["gl"] + _bmxu(ch["kd"], vn, 1, 1)
        r = lax.rsqrt(jnp.mean(o * o, axis=-1, keepdims=True) + EPS)
        z = _heads(z_ref[...])
        o_ref[...] = _unheads(o * r * on_ref[...] * (z * _sigmoid(z))).astype(BF16)

    return pl.pallas_call(
        body, name=name, grid=(nc,),
        in_specs=[blk(0, 3 * w), blk(0), blk(0), blk(3), vec], out_specs=[blk(0), st],
        out_shape=[jax.ShapeDtypeStruct((s, w), BF16), jax.ShapeDtypeStruct((nc, GDN_HEADS, GDN_DH, GDN_DH), F32)],
        scratch_shapes=[pltpu.VMEM((GDN_HEADS, GDN_DH, GDN_DH), F32)],
        compiler_params=_cp("arbitrary"),
    )(qkv, beta, g, projx, onorm.reshape(1, -1))


def _gdn_bwd(qkv, beta, g, projx, onorm, states, dout, name):
    s = qkv.shape[0]
    c = GDN_CHUNK
    nc = s // c
    w = GDN_HEADS * GDN_DH
    blk, st, vec = _gdn_specs(lambda n: nc - 1 - n)

    def body(qkv_ref, b_ref, g_ref, z_ref, on_ref, st_ref, do_ref,
             dqkv_ref, db_ref, dg_ref, dz_ref, don_ref, carry):
        @pl.when(pl.program_id(0) == 0)
        def _():
            carry[...] = jnp.zeros_like(carry)
            don_ref[...] = jnp.zeros_like(don_ref)

        tri = _gdn_tri()
        low, up, incl, strict, eye = tri
        q, k, v, bb, ch = _gdn_load(qkv_ref, b_ref, g_ref, tri)
        sp = st_ref[...]
        vn = ch["u"] - _bmxu(ch["w"], sp)
        o = _bmxu(ch["qd"], sp) + _bmxu(ch["qk"], vn)
        r = lax.rsqrt(jnp.mean(o * o, axis=-1, keepdims=True) + EPS)
        orn = o * r
        z = _heads(z_ref[...])
        sg = _sigmoid(z)
        dout = _heads(do_ref[...])
        onw = on_ref[...]
        dz_ref[...] = _unheads(dout * orn * onw * (sg * (1.0 + z * (1.0 - sg)))).astype(BF16)
        don = dout * (z * sg)
        don_ref[...] += jnp.sum(jnp.sum(don * orn, axis=0), axis=0, keepdims=True)
        dor = don * onw
        do = r * (dor - orn * jnp.mean(dor * orn, axis=-1, keepdims=True))
        dsn = carry[...]
        dqd = _bmxu(do, sp, 2, 2)
        dqk = jnp.where(incl, _bmxu(do, vn, 2, 2), 0.0)
        dvn = _bmxu(ch["qk"], do, 1, 1) + _bmxu(ch["kd"], dsn)
        dkd = _bmxu(vn, dsn, 2, 2)
        dgl = jnp.sum(dsn * sp, axis=1, keepdims=True)
        dw = -_bmxu(dvn, sp, 2, 2)
        carry[...] = _bmxu(ch["qd"], do, 1, 1) + dsn * ch["gl"] - _bmxu(ch["w"], dvn, 1, 1)
        t = ch["t"]
        dvb = _bmxu(t, dvn, 1, 1)
        dkbg = _bmxu(t, dw, 1, 1)
        dt = _bmxu(dvn, ch["vb"], 2, 2) + _bmxu(dw, ch["kbg"], 2, 2)
        da = -_bdot(_bdot(t, dt, 1, 1, precision=HI), t, 2, 2, precision=HI)
        da = jnp.where(strict, da, 0.0)
        decay = ch["decay"]
        dkbk = da * decay
        dqkr = dqk * decay
        mdec = (da * ch["kbk"] + dqk * ch["qkraw"]) * decay
        dkb = _bmxu(dkbk, k) + dkbg * ch["eg"]
        dk = _bmxu(dkbk, ch["kb"], 1, 1) + _bmxu(dqkr, q, 1, 1) + dkd * ch["ek"] + dkb * bb
        dq = _bmxu(dqkr, k) + dqd * ch["eg"]
        tk = dkd * ch["kd"]
        dgcl = jnp.sum(tk, axis=1, keepdims=True) + dgl * ch["gl"]
        row = lax.broadcasted_iota(jnp.int32, (GDN_HEADS, c, GDN_DH), 1)
        zpad = jnp.zeros((GDN_HEADS, c, GDN_DH - c), F32)
        dgc = (jnp.concatenate([mdec, zpad], axis=2) - jnp.concatenate([jnp.swapaxes(mdec, 1, 2), zpad], axis=2)
               + dqd * ch["qd"] - tk + dkbg * ch["kbg"] + jnp.where(row == c - 1, dgcl, 0.0))
        dqkv_ref[:, :w] = _unheads(dq)
        dqkv_ref[:, w:2 * w] = _unheads(dk)
        dqkv_ref[:, 2 * w:] = _unheads(dvb * bb)
        db_ref[...] = _unheads(dkb * k + dvb * v)
        dg_ref[...] = _dot(up, _unheads(dgc), precision=HI)

    return pl.pallas_call(
        body, name=name, grid=(nc,),
        in_specs=[blk(0, 3 * w), blk(0), blk(0), blk(3), vec, st, blk(0)],
        out_specs=[blk(0, 3 * w), blk(0), blk(0), blk(0), vec],
        out_shape=[jax.ShapeDtypeStruct((s, 3 * w), F32), jax.ShapeDtypeStruct((s, w), F32),
                   jax.ShapeDtypeStruct((s, w), F32), jax.ShapeDtypeStruct((s, w), BF16),
                   jax.ShapeDtypeStruct((1, GDN_DH), F32)],
        scratch_shapes=[pltpu.VMEM((GDN_HEADS, GDN_DH, GDN_DH), F32)],
        compiler_params=_cp("arbitrary"),
    )(qkv, beta, g, projx, onorm.reshape(1, -1), states, dout)


_WEIGHTS = (
    "l0_mix_norm", "l0_w_in", "l0_ret_norm", "l0_s5_lambda_re", "l0_s5_lambda_im", "l0_s5_b_re", "l0_s5_b_im",
    "l0_s5_c_re", "l0_s5_c_im", "l0_s5_d", "l0_s5_log_dt", "l0_s5_w_glu", "l0_s5_b_glu", "l0_w_out",
    "l0_xa_norm", "l0_mem_norm", "l0_xa_wq", "l0_xa_wkv", "l0_xa_wo", "l0_ffn_norm", "l0_ffn_w_up",
    "l0_ffn_conv", "l0_ffn_w_down", "l1_mix_norm", "l1_w_in", "l1_conv", "l1_a_log", "l1_dt_bias", "l1_o_norm",
    "l1_w_out", "l1_xa_norm", "l1_mem_norm", "l1_xa_wq", "l1_xa_wkv", "l1_xa_wo", "l1_ffn_norm", "l1_ffn_w_up",
    "l1_ffn_conv", "l1_ffn_w_down", "final_norm")
_INPUTS = ("x", "mem") + _WEIGHTS + ("loss_target",) + tuple("m_" + n for n in _WEIGHTS) + tuple("v_" + n for n in _WEIGHTS)

_COL = ("l0_w_in", "l0_xa_wkv", "l0_ffn_w_up", "l0_ffn_conv", "l1_w_in", "l1_conv", "l1_xa_wkv", "l1_ffn_w_up",
        "l1_ffn_conv")
_ROW = ("l0_s5_w_glu", "l0_w_out", "l0_xa_wq", "l0_xa_wo", "l0_ffn_w_down", "l1_w_out", "l1_xa_wq", "l1_xa_wo",
        "l1_ffn_w_down")
_F32_WIRE = ("l0_ffn_conv", "l1_conv", "l1_ffn_conv")
_REP = tuple(n for n in _WEIGHTS if n not in _COL + _ROW)
_GATHER_GROUPS = (("l0_w_in", "l0_s5_w_glu", "l0_w_out"),
                  ("l0_xa_wq", "l0_xa_wkv", "l0_xa_wo", "l0_ffn_w_up", "l0_ffn_conv", "l0_ffn_w_down"),
                  ("l1_w_in", "l1_conv", "l1_w_out", "l1_xa_wq", "l1_xa_wkv", "l1_xa_wo", "l1_ffn_w_up", "l1_ffn_conv",
                   "l1_ffn_w_down"))


def _round_up(n, m):
    return (n + m - 1) // m * m


_REP_BIG = ("l0_s5_lambda_re", "l0_s5_lambda_im", "l0_s5_b_re", "l0_s5_b_im", "l0_s5_c_re", "l0_s5_c_im", "l0_s5_d")
_REP_SMALL = tuple(n for n in _REP if n not in _REP_BIG)
PACK_WIDTH = 1024


def _pack_rows(ts):
    rows = [jnp.pad(t, ((0, 0), (0, PACK_WIDTH - t.shape[1]))) for t in ts]
    rows.append(jnp.zeros((_round_up(len(ts), 8) - len(ts), PACK_WIDTH), F32))
    return jnp.concatenate(rows, axis=0)


def _s5_interleave(re, im):
    lead = re.shape[:-1]
    nt = re.shape[-1] // S5_TILE
    both = jnp.stack([re.reshape(lead + (nt, S5_TILE)), im.reshape(lead + (nt, S5_TILE))], axis=-2)
    return both.reshape(lead + (2 * re.shape[-1],))


def _s5_split(x):
    lead = x.shape[:-1]
    y = x.reshape(lead + (x.shape[-1] // (2 * S5_TILE), 2, S5_TILE))
    return y[..., 0, :].reshape(lead + (-1,)), y[..., 1, :].reshape(lead + (-1,))


def _s5_discretise(lr, li, log_dt, b_re, b_im):
    dt = jnp.exp(log_dt)[:, None]
    mag = jnp.exp(lr * dt)
    a_re = mag * jnp.cos(li * dt)
    a_im = mag * jnp.sin(li * dt)
    den = lr * lr + li * li
    z_re = ((a_re - 1.0) * lr + a_im * li) / den
    z_im = (a_im * lr - (a_re - 1.0) * li) / den
    bb_re = z_re[:, None, :] * b_re - z_im[:, None, :] * b_im
    bb_im = z_re[:, None, :] * b_im + z_im[:, None, :] * b_re
    return a_re, a_im, bb_re, bb_im


def _pow_tables(a_re, a_im, rows):
    ur, ui = a_re[None], a_im[None]
    dr, di = ur, ui
    while ur.shape[0] < rows:
        lr, li = ur[-1:], ui[-1:]
        ur, ui = (jnp.concatenate([ur, ur * lr - ui * li]), jnp.concatenate([ui, ur * li + ui * lr]))
        dr, di = (jnp.concatenate([dr * lr - di * li, dr]), jnp.concatenate([dr * li + di * lr, di]))
    return (ur, ui), (dr, di)


def _block_diag(b):
    g, r, c = b.shape
    return jnp.einsum("grc,gk->grkc", b, jnp.eye(g, dtype=b.dtype)).reshape(g * r, g * c)


def _block_diag_of(d, g):
    r, c = d.shape[0] // g, d.shape[1] // g
    return jnp.einsum("grkc,gk->grc", d.reshape(g, r, g, c), jnp.eye(g, dtype=d.dtype))


def kernel(*args):
    p = dict(zip(_INPUTS, args, strict=True))
    x0, mem0, tgt = p["x"][0], p["mem"][0], p["loss_target"][0]
    s, d = x0.shape
    me = _slot(*_mesh_pos())
    grads = {}
    wire = {n: (F32 if n in _F32_WIRE else BF16) for n in _COL + _ROW}

    shards = {n: p[n].astype(wire[n]) for n in _COL + _ROW}
    gather, pin = [], jnp.zeros((), F32)
    groups = [[shards[n] for n in names] for names in _GATHER_GROUPS]
    zones = _place_own([a for group in groups for a in group], "gather_place")
    for i, group in enumerate(groups):
        handle, token = _push_start([], group, f"gather{i}_start", zones[:len(group)])
        zones = zones[len(group):]
        gather.append(handle)
        pin = pin + token[0, 0]
    w = {}

    def gathered(i, after):
        for n, full in zip(_GATHER_GROUPS[i], _push_wait(gather[i], after, f"gather{i}_wait")):
            if n in _COL:
                full = full.transpose(1, 0, 2)
            w[n] = full.reshape(-1, full.shape[-1]) if n in _ROW else full.reshape(full.shape[0], -1)

    pending = []

    def exchange(names, gain, tag):
        slots = []
        for n in names:
            g = grads[n]
            if n in _COL:
                g = g.reshape(g.shape[0], N_DEV, -1).transpose(1, 0, 2)
            else:
                g = g.reshape((N_DEV, -1) + g.shape[1:])
            slots.append(g.astype(wire[n]))
        handle, token = _push_start(slots, [], tag + "_start")
        pending.append((names, slots, handle, tag))
        return gain + token[0, 0]

    def xattn(pre, x_in):
        hx = _norm_fwd(x_in, p[pre + "xa_norm"], pre + "xa_norm_fwd")
        q = _mm(hx, w[pre + "xa_wq"], out_dtype=BF16, name=pre + "xa_q")
        memn = _norm_fwd(mem0, p[pre + "mem_norm"], pre + "mem_norm_fwd")
        kv = _mm(memn, w[pre + "xa_wkv"], out_dtype=BF16, name=pre + "xa_kv")
        ao = _xattn_fwd(q, kv, pre + "xattn_fwd")
        x_out = _mm(ao, w[pre + "xa_wo"], res=x_in, name=pre + "xa_o")
        return x_out, (x_in, hx, q, memn, kv, ao)

    def xattn_bwd(pre, saved, dxo):
        x_in, hx, q, memn, kv, ao = saved
        dao = _mm(dxo, w[pre + "xa_wo"], tb=True, name=pre + "xa_o_dx")
        grads[pre + "xa_wo"] = _mm(ao, dxo, ta=True, out_dtype=BF16, name=pre + "xa_o_dw")
        dq, dkv = _xattn_bwd(q, kv, dao, pre + "xattn_bwd")
        grads[pre + "xa_wq"] = _mm(hx, dq, ta=True, out_dtype=BF16, name=pre + "xa_q_dw")
        dhx = _mm(dq, w[pre + "xa_wq"], tb=True, name=pre + "xa_q_dx")
        grads[pre + "xa_wkv"] = _mm(memn, dkv, ta=True, out_dtype=BF16, name=pre + "xa_kv_dw")
        dmemn = _mm(dkv, w[pre + "xa_wkv"], tb=True, name=pre + "xa_kv_dx")
        gain = exchange((pre + "xa_wo", pre + "xa_wq", pre + "xa_wkv"), p[pre + "xa_norm"], pre + "xa_grads")
        dx_in, grads[pre + "xa_norm"] = _norm_bwd(x_in, gain, dhx, dxo, pre + "xa_norm_bwd")
        _, grads[pre + "mem_norm"] = _norm_bwd(mem0, p[pre + "mem_norm"], dmemn, jnp.zeros_like(mem0), pre + "mem_norm_bwd")
        return dx_in

    def ffn(pre, x_in):
        hf = _norm_fwd(x_in, p[pre + "ffn_norm"], pre + "ffn_norm_fwd")
        up = _mm(hf, w[pre + "ffn_w_up"], out_dtype=BF16, name=pre + "ffn_up")
        act = _ffn_act_fwd(up, w[pre + "ffn_conv"], pre + "ffn_act_fwd")
        x_out = _mm(act, w[pre + "ffn_w_down"], res=x_in, name=pre + "ffn_down")
        return x_out, (x_in, hf, up, act)

    def ffn_bwd(pre, saved, dxo):
        x_in, hf, up, act = saved
        dact = _mm(dxo, w[pre + "ffn_w_down"], tb=True, out_dtype=BF16, name=pre + "ffn_down_dx")
        grads[pre + "ffn_w_down"] = _mm(act, dxo, ta=True, out_dtype=BF16, name=pre + "ffn_down_dw")
        dpu, dpg, dcu, dcg = _ffn_act_bwd(up, w[pre + "ffn_conv"], dact, pre + "ffn_act_bwd")
        dup = jnp.concatenate([dpu, dpg], axis=1)
        grads[pre + "ffn_conv"] = jnp.concatenate([dcu, dcg], axis=1)
        dhf = _mm(dup, w[pre + "ffn_w_up"], tb=True, name=pre + "ffn_up_dx")
        grads[pre + "ffn_w_up"] = _mm(hf, dup, ta=True, out_dtype=BF16, name=pre + "ffn_up_dw")
        gain = exchange((pre + "ffn_w_down", pre + "ffn_w_up", pre + "ffn_conv"), p[pre + "ffn_norm"], pre + "ffn_grads")
        dx_in, grads[pre + "ffn_norm"] = _norm_bwd(x_in, gain, dhf, dxo, pre + "ffn_norm_bwd")
        return dx_in

    cos, sin = _rope_tables(s)
    (a_re, a_im, bb_re, bb_im), disc_vjp = jax.vjp(
        _s5_discretise, p["l0_s5_lambda_re"], p["l0_s5_lambda_im"], p["l0_s5_log_dt"], p["l0_s5_b_re"], p["l0_s5_b_im"])
    (pu_re, pu_im), (pd_re, pd_im) = _pow_tables(a_re.reshape(-1), a_im.reshape(-1), SCAN_ROWS)
    apow = _s5_interleave(pu_re, pu_im)
    apow_rev = _s5_interleave(pd_re, -pd_im)
    bbig = _s5_interleave(_block_diag(bb_re), _block_diag(bb_im)).astype(BF16)
    cbig = _s5_interleave(_block_diag(p["l0_s5_c_re"]).T, -_block_diag(p["l0_s5_c_im"]).T).T.astype(BF16)
    s5_d = p["l0_s5_d"].reshape(1, -1)
    b_glu = p["l0_s5_b_glu"].reshape(1, -1)

    h0 = _norm_fwd(x0, p["l0_mix_norm"] + pin, "l0_mix_norm_fwd")
    gathered(0, h0)
    proj = _mm(h0, w["l0_w_in"], name="l0_in")
    o_ret, ret_states = _ret_fwd(proj, cos, sin, p["l0_ret_norm"], "l0_ret_fwd")
    u = proj[:, 4 * RET_HEADS * RET_DH:]
    bu = _mm(u, bbig, name="l0_s5_bu")
    st = _s5_scan_fwd(bu, apow, "l0_s5_scan_fwd")
    yraw = _mm(st, cbig, name="l0_s5_c")
    y, gy = _s5_gelu_fwd(yraw, proj, s5_d, "l0_s5_gelu_fwd")
    z = _mm(gy, w["l0_s5_w_glu"], name="l0_s5_glu_mm")
    y2 = _s5_glu_fwd(y, z, b_glu, "l0_s5_glu_fwd")
    merged = jnp.concatenate([o_ret, y2], axis=1)
    x1 = _mm(merged, w["l0_w_out"], res=x0, name="l0_out")
    gathered(1, x1)
    x2, xa0 = xattn("l0_", x1)
    x3, ff0 = ffn("l0_", x2)

    gathered(2, x3)
    nqkv = 4 * GDN_HEADS * GDN_DH
    w1 = w["l1_w_in"]
    wx = jnp.concatenate([w1[:, :nqkv], jnp.repeat(w1[:, nqkv:nqkv + GDN_HEADS], GDN_DH, axis=1),
                          jnp.repeat(w1[:, nqkv + GDN_HEADS:], GDN_DH, axis=1)], axis=1)
    alog_x = jnp.repeat(p["l1_a_log"], GDN_DH).reshape(1, -1)
    dtb_x = jnp.repeat(p["l1_dt_bias"], GDN_DH).reshape(1, -1)
    h1 = _norm_fwd(x3, p["l1_mix_norm"], "l1_mix_norm_fwd")
    projx = _mm(h1, wx, name="l1_in")
    qkv = _gdn_conv_fwd(projx, w["l1_conv"], "l1_conv_fwd")
    beta, glog = _gdn_gates_fwd(projx, alog_x, dtb_x, "l1_gates_fwd")
    o_gdn, gdn_states = _gdn_fwd(qkv, beta, glog, projx, p["l1_o_norm"], "l1_gdn_fwd")
    x4 = _mm(o_gdn, w["l1_w_out"], res=x3, name="l1_out")
    x5, xa1 = xattn("l1_", x4)
    x6, ff1 = ffn("l1_", x5)

    loss_part, dx6, grads["final_norm"] = _loss_head(x6, p["final_norm"], tgt, "loss_head")
    loss = lax.psum(loss_part[0, 0], ("x", "y", "c"))
    dx5 = ffn_bwd("l1_", ff1, dx6)
    dx4 = xattn_bwd("l1_", xa1, dx5)

    do_gdn = _mm(dx4, w["l1_w_out"], tb=True, name="l1_out_dx")
    grads["l1_w_out"] = _mm(o_gdn, dx4, ta=True, out_dtype=BF16, name="l1_out_dw")
    dqkv, dbeta, dglog, dz, grads["l1_o_norm"] = _gdn_bwd(
        qkv, beta, glog, projx, p["l1_o_norm"], gdn_states, do_gdn, "l1_gdn_bwd")
    dpre, grads["l1_conv"] = _gdn_conv_bwd(projx, w["l1_conv"], dqkv, "l1_conv_bwd")
    db, da, dalog_x, ddtb_x = _gdn_gates_bwd(projx, alog_x, dtb_x, dbeta, dglog, "l1_gates_bwd")
    dprojx = jnp.concatenate([dpre, dz, db, da], axis=1)
    dh1 = _mm(dprojx, wx, tb=True, name="l1_in_dx")
    dwx = _mm(h1, dprojx, ta=True, name="l1_in_dw")
    grads["l1_w_in"] = jnp.concatenate(
        [dwx[:, :nqkv], dwx[:, nqkv:nqkv + GDN_HEADS * GDN_DH].reshape(d, GDN_HEADS, GDN_DH).sum(-1),
         dwx[:, nqkv + GDN_HEADS * GDN_DH:].reshape(d, GDN_HEADS, GDN_DH).sum(-1)], axis=1)
    grads["l1_a_log"] = dalog_x.reshape(GDN_HEADS, GDN_DH).sum(-1)
    grads["l1_dt_bias"] = ddtb_x.reshape(GDN_HEADS, GDN_DH).sum(-1)
    gain = exchange(("l1_w_out", "l1_w_in", "l1_conv"), p["l1_mix_norm"], "l1_mix_grads")
    dx3, grads["l1_mix_norm"] = _norm_bwd(x3, gain, dh1, dx4, "l1_mix_norm_bwd")

    dx2 = ffn_bwd("l0_", ff0, dx3)
    dx1 = xattn_bwd("l0_", xa0, dx2)

    dmerged = _mm(dx1, w["l0_w_out"], tb=True, name="l0_out_dx")
    grads["l0_w_out"] = _mm(merged, dx1, ta=True, out_dtype=BF16, name="l0_out_dw")
    drq, drk, drv, drg, grads["l0_ret_norm"] = _ret_bwd(proj, cos, sin, p["l0_ret_norm"], ret_states, dmerged, "l0_ret_bwd")
    dzg, dg1, grads["l0_s5_b_glu"] = _s5_glu_bwd(dmerged, y, z, b_glu, "l0_s5_glu_bwd")
    grads["l0_s5_w_glu"] = _mm(gy, dzg, ta=True, out_dtype=BF16, name="l0_s5_glu_dw")
    dg2 = _mm(dzg, w["l0_s5_w_glu"], tb=True, name="l0_s5_glu_dx")
    dyraw, du_dir, grads["l0_s5_d"] = _s5_gelu_bwd(dg1, dg2, y, proj, s5_d, "l0_s5_gelu_bwd")
    dst = _mm(dyraw, cbig, tb=True, name="l0_s5_c_dx")
    dcbig = _mm(st, dyraw, ta=True, name="l0_s5_c_dw")
    gsc, da_s5 = _s5_scan_bwd(dst, apow_rev, st, "l0_s5_scan_bwd")
    du = _mm(gsc, bbig, tb=True, res=du_dir, out_dtype=BF16, name="l0_s5_bu_dx")
    dbbig = _mm(u, gsc, ta=True, name="l0_s5_bu_dw")
    dproj = jnp.concatenate([drq, drk, drv, drg, du], axis=1)
    dh0 = _mm(dproj, w["l0_w_in"], tb=True, name="l0_in_dx")
    grads["l0_w_in"] = _mm(h0, dproj, ta=True, out_dtype=BF16, name="l0_in_dw")
    gain = exchange(("l0_w_out", "l0_s5_w_glu", "l0_w_in"), p["l0_mix_norm"], "l0_mix_grads")
    dx0, grads["l0_mix_norm"] = _norm_bwd(x0, gain, dh0, dx1, "l0_mix_norm_bwd")

    dbb_re, dbb_im = (_block_diag_of(t, S5_GROUPS) for t in _s5_split(dbbig))
    dct_re, dct_im = _s5_split(dcbig.T)
    grads["l0_s5_c_re"] = _block_diag_of(dct_re.T, S5_GROUPS)
    grads["l0_s5_c_im"] = -_block_diag_of(dct_im.T, S5_GROUPS)
    da_re, da_im = (t.reshape(S5_GROUPS, S5_STATE) for t in _s5_split(da_s5[0]))
    (grads["l0_s5_lambda_re"], grads["l0_s5_lambda_im"], grads["l0_s5_log_dt"], grads["l0_s5_b_re"],
     grads["l0_s5_b_im"]) = disc_vjp((da_re, da_im, dbb_re, dbb_im))

    def as_2d(t):
        return t.reshape(-1, t.shape[-1])

    def as_row(t):
        return t.reshape(1, -1)

    small_own = _pack_rows([as_row(grads[n]) for n in _REP_SMALL])
    big_own = [as_2d(grads[n].reshape(p[n].shape)) for n in _REP_BIG]
    rep_handle, _ = _push_start([], [small_own] + [t.reshape(-1, LANES) for t in big_own], "rep_grads_start")
    rep_lands = _push_wait(rep_handle, dx0, "rep_grads_wait")
    rep_land = rep_lands[0]

    outs = {}
    kinds = ("grad_", "delta_", "new_m_", "new_v_")
    for names, slots, handle, tag in pending:
        for n, own_slots, land in zip(names, slots, _push_wait(handle, rep_land, tag + "_wait")):
            shape = p[n].shape
            own = lax.dynamic_index_in_dim(own_slots, me, 0, keepdims=False)
            res = _adamw(land, own, *(p[pre + n].reshape(own.shape) for pre in ("", "m_", "v_")), "adamw_" + n)
            for kind, t in zip(kinds, res):
                outs[kind + n] = t.reshape(shape)
    for n, own, land in zip(_REP_BIG, big_own, rep_lands[1:]):
        res = _adamw(land.reshape((N_DEV,) + own.shape), own, *(as_2d(p[pre + n]) for pre in ("", "m_", "v_")), "adamw_" + n)
        for kind, t in zip(kinds, res):
            outs[kind + n] = t.reshape(p[n].shape)
    res = _adamw_rows(rep_land, small_own, *([as_row(p[pre + n]) for n in _REP_SMALL] for pre in ("", "m_", "v_")), "adamw_small")
    for j, kind in enumerate(kinds):
        for i, n in enumerate(_REP_SMALL):
            outs[kind + n] = res[j * len(_REP_SMALL) + i].reshape(p[n].shape)

    return (loss, dx0[None]) + tuple(outs[kind + n] for kind in kinds for n in _WEIGHTS)
```

```python
import functools
import math

import numpy as np
import jax
import jax.numpy as jnp
from jax import lax
from jax.experimental import pallas as pl
from jax.experimental.pallas import tpu as pltpu

F32 = jnp.float32
BF16 = jnp.bfloat16
EPS = 1e-6
N_DEV = 8
LANES = 128
VMEM_LIMIT = 48 * 1024 * 1024
HI = lax.Precision.HIGHEST

RET_HEADS, RET_DH, RET_CHUNK = 4, 128, 128
S5_GROUPS, S5_GROUP, S5_STATE = 32, 16, 64
GDN_HEADS, GDN_DH, GDN_CHUNK, GDN_CONV = 8, 128, 64, 4
XA_HEADS, XA_DH = 4, 256
FFN_CONV = 3
SCAN_ROWS = 256

ADAM_LR, ADAM_B1, ADAM_B2, ADAM_EPS, ADAM_WD, ADAM_STEP = 0.001, 0.9, 0.999, 1e-08, 0.01, 10


def _cp(*sem):
    return pltpu.CompilerParams(dimension_semantics=sem if sem else None, vmem_limit_bytes=VMEM_LIMIT)


def _tile(n, cap):
    if n <= cap:
        return n
    best = None
    for t in range(LANES, cap + 1, LANES):
        if n % t == 0:
            best = t
    assert best is not None, n
    return best


def _dot(a, b, ca=1, cb=0, precision=None):
    return lax.dot_general(a, b, (((ca,), (cb,)), ((), ())), precision=precision, preferred_element_type=F32)


def _mxu(a, b, ca=1, cb=0):
    return _dot(a.astype(BF16), b.astype(BF16), ca, cb)


def _sigmoid(x):
    return 1.0 / (1.0 + jnp.exp(-x))


def _shift_down(x, k):
    row = lax.broadcasted_iota(jnp.int32, x.shape, 0)
    return jnp.where(row >= k, pltpu.roll(x, k, 0), 0.0)


def _shift_up(x, k):
    n = x.shape[0]
    row = lax.broadcasted_iota(jnp.int32, x.shape, 0)
    return jnp.where(row < n - k, pltpu.roll(x, n - k, 0), 0.0)


def _mesh_pos():
    return lax.axis_index("x"), lax.axis_index("y"), lax.axis_index("c")


def _slot(px, py, pc):
    return 4 * px + 2 * py + pc


def _all_peers(x, y, c):
    flips = [(fx, fy, fc) for fx in (0, 1) for fy in (0, 1) for fc in (0, 1)][1:]
    return [(1 - x if fx else x, 1 - y if fy else y, 1 - c if fc else c) for fx, fy, fc in flips]


_HBM = pl.BlockSpec(memory_space=pltpu.HBM)
_SEM = pl.BlockSpec(memory_space=pltpu.SEMAPHORE)
N_PEERS = N_DEV - 1


def _push_copies(srcs, lands, send_sems, recv_sems, start):
    x, y, c = _mesh_pos()
    me = _slot(x, y, c)
    out = []
    for k, to in enumerate(_all_peers(x, y, c)):
        for a in range(len(lands)):
            src = srcs[a].at[_slot(*to)] if a < len(srcs) else lands[a].at[me]
            dst = lands[a].at[me if start else _slot(*to)]
            out.append(pltpu.make_async_remote_copy(
                src_ref=src, dst_ref=dst, send_sem=send_sems.at[a * N_PEERS + k], recv_sem=recv_sems.at[a * N_PEERS + k],
                device_id=to, device_id_type=pl.DeviceIdType.MESH))
    return out


def _into_slot(x, dtype, me, name):
    r, c = x.shape
    cap = max(16, 512 * 1024 // c)
    tr = max(t for t in range(16, min(r, cap) + 1, 16) if r % t == 0) if r % 16 == 0 else r

    def body(me_ref, x_ref, o_ref):
        o_ref[...] = x_ref[...].astype(dtype)

    return pl.pallas_call(
        body, name=name, out_shape=jax.ShapeDtypeStruct((N_DEV, r, c), dtype),
        grid_spec=pltpu.PrefetchScalarGridSpec(
            num_scalar_prefetch=1, grid=(r // tr,),
            in_specs=[pl.BlockSpec((tr, c), lambda i, me_ref: (i, 0))],
            out_specs=pl.BlockSpec((None, tr, c), lambda i, me_ref: (me_ref[0], i, 0))),
        compiler_params=_cp("parallel"),
    )(me.reshape(1).astype(jnp.int32), x)


def _push_start(scatter, gather_lands, name):
    ns, n = len(scatter), len(scatter) + len(gather_lands)
    lands = [lax.empty(a.shape, a.dtype) for a in scatter] + list(gather_lands)

    def body(*refs):
        srcs, zones = refs[:ns], refs[ns:ns + n]
        for cp in _push_copies(srcs, zones, refs[ns + n], refs[ns + n + 1], True):
            cp.start()
        refs[-1][...] = jnp.zeros((8, LANES), F32)

    hbm_in = [pltpu.with_memory_space_constraint(a, pltpu.HBM) for a in list(scatter) + lands]
    res = pl.pallas_call(
        body, name=name,
        out_shape=(pltpu.SemaphoreType.DMA((n * N_PEERS,)), pltpu.SemaphoreType.DMA((n * N_PEERS,)))
        + tuple(pltpu.HBM(a.shape, a.dtype) for a in list(scatter) + lands)
        + (jax.ShapeDtypeStruct((8, LANES), F32),),
        in_specs=[_HBM] * (ns + n),
        out_specs=(_SEM, _SEM) + (_HBM,) * (ns + n) + (pl.BlockSpec(memory_space=pltpu.VMEM),),
        input_output_aliases={i: 2 + i for i in range(ns + n)},
        compiler_params=pltpu.CompilerParams(has_side_effects=pltpu.SideEffectType.DATAFLOW_SIDE_EFFECTING),
    )(*hbm_in)
    return (res[0], res[1], res[2:2 + ns], res[2 + ns:2 + ns + n]), res[-1]


def _push_wait(handle, after, name):
    send_sems, recv_sems, srcs, lands = handle
    ns, n = len(srcs), len(lands)

    def body(*refs):
        for cp in _push_copies(refs[:ns], refs[ns:ns + n], refs[ns + n], refs[ns + n + 1], False):
            cp.wait_send()
            cp.wait_recv()

    res = pl.pallas_call(
        body, name=name,
        out_shape=tuple(pltpu.HBM(a.shape, a.dtype) for a in list(srcs) + list(lands)),
        in_specs=[_HBM] * (ns + n) + [_SEM, _SEM, pl.BlockSpec(memory_space=pl.ANY)],
        out_specs=(_HBM,) * (ns + n),
        input_output_aliases={i: i for i in range(ns + n)},
        compiler_params=pltpu.CompilerParams(has_side_effects=pltpu.SideEffectType.DATAFLOW_SIDE_EFFECTING),
    )(*srcs, *lands, send_sems, recv_sems, after)
    return res[ns:]


def _mm(a, b, *, ta=False, tb=False, out_dtype=F32, res=None, name="mm"):
    m, k = (a.shape[1], a.shape[0]) if ta else a.shape
    n = b.shape[0] if tb else b.shape[1]
    assert k == (b.shape[1] if tb else b.shape[0]), (a.shape, b.shape, ta, tb)
    tm, tn, tk = _tile(m, 1408), _tile(n, 1536), _tile(k, 1408)
    nk = k // tk
    has_res = res is not None

    def body(*refs):
        a_ref, b_ref = refs[:2]
        r_ref = refs[2] if has_res else None
        o_ref = refs[3 if has_res else 2]
        part = _mxu(a_ref[...], b_ref[...], 0 if ta else 1, 1 if tb else 0)

        def finish(r):
            if has_res:
                r = r + r_ref[...].astype(F32)
            o_ref[...] = r.astype(out_dtype)

        if nk == 1:
            finish(part)
            return
        acc = refs[-1]
        kk = pl.program_id(2)

        @pl.when(kk == 0)
        def _():
            acc[...] = part

        @pl.when(kk > 0)
        def _():
            acc[...] += part

        @pl.when(kk == nk - 1)
        def _():
            finish(acc[...])

    a_spec = pl.BlockSpec((tk, tm), lambda i, j, kk: (kk, i)) if ta else pl.BlockSpec((tm, tk), lambda i, j, kk: (i, kk))
    b_spec = pl.BlockSpec((tn, tk), lambda i, j, kk: (j, kk)) if tb else pl.BlockSpec((tk, tn), lambda i, j, kk: (kk, j))
    o_spec = pl.BlockSpec((tm, tn), lambda i, j, kk: (i, j))
    in_specs = [a_spec, b_spec] + ([o_spec] if has_res else [])
    args = (a, b) + ((res,) if has_res else ())
    return pl.pallas_call(
        body, name=name, grid=(m // tm, n // tn, nk), in_specs=in_specs, out_specs=o_spec,
        out_shape=jax.ShapeDtypeStruct((m, n), out_dtype),
        scratch_shapes=[pltpu.VMEM((tm, tn), F32)] if nk > 1 else [],
        compiler_params=_cp("parallel", "parallel", "arbitrary"),
    )(*args)


def _norm_fwd(x, g, name):
    s, d = x.shape
    tr = min(512, s)

    def body(x_ref, g_ref, o_ref):
        xv = x_ref[...]
        r = lax.rsqrt(jnp.mean(xv * xv, axis=-1, keepdims=True) + EPS)
        o_ref[...] = (xv * r * g_ref[...]).astype(BF16)

    row = pl.BlockSpec((tr, d), lambda i: (i, 0))
    return pl.pallas_call(
        body, name=name, grid=(s // tr,), in_specs=[row, pl.BlockSpec((1, d), lambda i: (0, 0))],
        out_specs=row, out_shape=jax.ShapeDtypeStruct((s, d), BF16), compiler_params=_cp("parallel"),
    )(x, g.reshape(1, d))


def _norm_bwd(x, g, dh, dres, name):
    s, d = x.shape
    tr = min(512, s)

    def body(x_ref, g_ref, dh_ref, dres_ref, dx_ref, dg_ref):
        @pl.when(pl.program_id(0) == 0)
        def _():
            dg_ref[...] = jnp.zeros_like(dg_ref)

        xv = x_ref[...]
        r = lax.rsqrt(jnp.mean(xv * xv, axis=-1, keepdims=True) + EPS)
        xn = xv * r
        dhv = dh_ref[...].astype(F32)
        dg_ref[...] += jnp.sum(dhv * xn, axis=0, keepdims=True)
        dhg = dhv * g_ref[...]
        dx_ref[...] = dres_ref[...] + r * (dhg - xn * jnp.mean(dhg * xn, axis=-1, keepdims=True))

    row = pl.BlockSpec((tr, d), lambda i: (i, 0))
    vec = pl.BlockSpec((1, d), lambda i: (0, 0))
    return pl.pallas_call(
        body, name=name, grid=(s // tr,), in_specs=[row, vec, row, row], out_specs=[row, vec],
        out_shape=[jax.ShapeDtypeStruct((s, d), F32), jax.ShapeDtypeStruct((1, d), F32)],
        compiler_params=_cp("arbitrary"),
    )(x, g.reshape(1, d), dh, dres)


def _loss_head(x, g, tgt, name):
    s, d = x.shape
    tr = min(512, s)

    def body(x_ref, g_ref, t_ref, l_ref, dx_ref, dg_ref):
        @pl.when(pl.program_id(0) == 0)
        def _():
            dg_ref[...] = jnp.zeros_like(dg_ref)
            l_ref[...] = jnp.zeros_like(l_ref)

        xv = x_ref[...]
        r = lax.rsqrt(jnp.mean(xv * xv, axis=-1, keepdims=True) + EPS)
        xn = xv * r
        err = xn * g_ref[...] - t_ref[...]
        part = 0.5 * jnp.sum(jnp.mean(err * err, axis=-1, keepdims=True), axis=0, keepdims=True)
        l_ref[...] += jnp.broadcast_to(part, l_ref.shape)
        dy = err * (1.0 / d)
        dg_ref[...] += jnp.sum(dy * xn, axis=0, keepdims=True)
        dyg = dy * g_ref[...]
        dx_ref[...] = r * (dyg - xn * jnp.mean(dyg * xn, axis=-1, keepdims=True))

    row = pl.BlockSpec((tr, d), lambda i: (i, 0))
    vec = pl.BlockSpec((1, d), lambda i: (0, 0))
    return pl.pallas_call(
        body, name=name, grid=(s // tr,), in_specs=[row, vec, row],
        out_specs=[pl.BlockSpec((1, LANES), lambda i: (0, 0)), row, vec],
        out_shape=[jax.ShapeDtypeStruct((1, LANES), F32), jax.ShapeDtypeStruct((s, d), F32),
                   jax.ShapeDtypeStruct((1, d), F32)],
        compiler_params=_cp("arbitrary"),
    )(x, g.reshape(1, d), tgt)


def _sum_slots(landed_slot, own):
    me = _slot(*_mesh_pos())
    mine = own.astype(F32)
    g = jnp.where(me == 0, mine, landed_slot(0).astype(F32))
    for i in range(1, N_DEV):
        g = g + jnp.where(me == i, mine, landed_slot(i).astype(F32))
    return g


def _adam_update(g, w, m, v):
    mm = ADAM_B1 * m + (1.0 - ADAM_B1) * g
    vv = ADAM_B2 * v + (1.0 - ADAM_B2) * (g * g)
    m_hat = mm / (1.0 - ADAM_B1 ** ADAM_STEP)
    v_hat = vv / (1.0 - ADAM_B2 ** ADAM_STEP)
    return g, -ADAM_LR * (m_hat / (jnp.sqrt(v_hat) + ADAM_EPS) + ADAM_WD * w), mm, vv


def _adamw_rows(landed, own, ws, ms, vs, name):
    k = len(ws)
    sizes = [w.shape[1] for w in ws]

    def body(*refs):
        p_ref, o_ref = refs[:2]
        w_refs, m_refs, v_refs = refs[2:2 + k], refs[2 + k:2 + 2 * k], refs[2 + 2 * k:2 + 3 * k]
        outs = refs[2 + 3 * k:]
        for i, n in enumerate(sizes):
            g = _sum_slots(lambda s: p_ref[s, i:i + 1, :n], o_ref[i:i + 1, :n])
            res = _adam_update(g, w_refs[i][...], m_refs[i][...], v_refs[i][...])
            for j in range(4):
                outs[j * k + i][...] = res[j]

    return pl.pallas_call(
        body, name=name, out_shape=[jax.ShapeDtypeStruct((1, n), F32) for _ in range(4) for n in sizes],
    )(landed, own, *ws, *ms, *vs)


def _adamw(landed, own, w, m, v, name):
    r, c = w.shape
    cap = max(8, 256 * 1024 // c)
    tr = max(t for t in range(8, min(r, cap) + 1, 8) if r % t == 0) if r % 8 == 0 else r

    def body(p_ref, o_ref, w_ref, m_ref, v_ref, g_ref, d_ref, nm_ref, nv_ref):
        g = _sum_slots(lambda i: p_ref[i], o_ref[...])
        g_ref[...], d_ref[...], nm_ref[...], nv_ref[...] = _adam_update(g, w_ref[...], m_ref[...], v_ref[...])

    blk = pl.BlockSpec((tr, c), lambda i: (i, 0))
    return pl.pallas_call(
        body, name=name, grid=(r // tr,),
        in_specs=[pl.BlockSpec((N_DEV, tr, c), lambda i: (0, i, 0)), blk, blk, blk, blk],
        out_specs=[blk] * 4, out_shape=[jax.ShapeDtypeStruct((r, c), F32)] * 4,
        compiler_params=_cp("parallel"),
    )(landed, own, w, m, v)


def _conv_fwd(x, w_ref, kw):
    acc = w_ref[kw - 1:kw, :] * x
    for j in range(kw - 1):
        acc = acc + w_ref[j:j + 1, :] * _shift_down(x, kw - 1 - j)
    return acc


def _conv_bwd(x, dy, w_ref, dw_ref, kw):
    dx = w_ref[kw - 1:kw, :] * dy
    dw_ref[kw - 1:kw, :] = jnp.sum(dy * x, axis=0, keepdims=True)
    for j in range(kw - 1):
        dx = dx + w_ref[j:j + 1, :] * _shift_up(dy, kw - 1 - j)
        dw_ref[j:j + 1, :] = jnp.sum(dy * _shift_down(x, kw - 1 - j), axis=0, keepdims=True)
    return dx


def _ffn_act_fwd(pre, cw, name):
    s, f2 = pre.shape
    nt = f2 // 2 // LANES

    def body(pu_ref, pg_ref, wu_ref, wg_ref, o_ref):
        up = _conv_fwd(pu_ref[...].astype(F32), wu_ref, FFN_CONV)
        gate = _conv_fwd(pg_ref[...].astype(F32), wg_ref, FFN_CONV)
        o_ref[...] = (gate * _sigmoid(gate) * up).astype(BF16)

    def col(rows, off):
        return pl.BlockSpec((rows, LANES), lambda j: (0, j + off))

    return pl.pallas_call(
        body, name=name, grid=(nt,),
        in_specs=[col(s, 0), col(s, nt), col(FFN_CONV, 0), col(FFN_CONV, nt)], out_specs=col(s, 0),
        out_shape=jax.ShapeDtypeStruct((s, f2 // 2), BF16), compiler_params=_cp("parallel"),
    )(pre, pre, cw, cw)


def _ffn_act_bwd(pre, cw, dact, name):
    s, f2 = pre.shape
    f = f2 // 2
    nt = f // LANES

    def body(pu_ref, pg_ref, wu_ref, wg_ref, da_ref, dpu_ref, dpg_ref, dwu_ref, dwg_ref):
        pu, pg = pu_ref[...].astype(F32), pg_ref[...].astype(F32)
        up = _conv_fwd(pu, wu_ref, FFN_CONV)
        gate = _conv_fwd(pg, wg_ref, FFN_CONV)
        sg = _sigmoid(gate)
        da = da_ref[...].astype(F32)
        dup = da * gate * sg
        dgate = da * up * (sg * (1.0 + gate * (1.0 - sg)))
        dpu_ref[...] = _conv_bwd(pu, dup, wu_ref, dwu_ref, FFN_CONV).astype(BF16)
        dpg_ref[...] = _conv_bwd(pg, dgate, wg_ref, dwg_ref, FFN_CONV).astype(BF16)

    def col(rows, off):
        return pl.BlockSpec((rows, LANES), lambda j: (0, j + off))

    return pl.pallas_call(
        body, name=name, grid=(nt,),
        in_specs=[col(s, 0), col(s, nt), col(FFN_CONV, 0), col(FFN_CONV, nt), col(s, 0)],
        out_specs=[col(s, 0), col(s, 0), col(FFN_CONV, 0), col(FFN_CONV, 0)],
        out_shape=[jax.ShapeDtypeStruct((s, f), BF16), jax.ShapeDtypeStruct((s, f), BF16),
                   jax.ShapeDtypeStruct((FFN_CONV, f), F32), jax.ShapeDtypeStruct((FFN_CONV, f), F32)],
        compiler_params=_cp("parallel"),
    )(pre, pre, cw, cw, dact)


def _xa_probs(qh, kh):
    sc = _mxu(qh, kh, 1, 1) * (XA_DH ** -0.5)
    e = jnp.exp(sc - jnp.max(sc, axis=-1, keepdims=True))
    return e / jnp.sum(e, axis=-1, keepdims=True)


def _xattn_fwd(q, kv, name):
    s, d = q.shape
    m = kv.shape[0]
    tr = min(512, s)

    def body(q_ref, kv_ref, o_ref):
        for h in range(XA_HEADS):
            lo, hi = h * XA_DH, (h + 1) * XA_DH
            p = _xa_probs(q_ref[:, lo:hi], kv_ref[:, lo:hi])
            o_ref[:, lo:hi] = _mxu(p, kv_ref[:, d + lo:d + hi]).astype(BF16)

    row = pl.BlockSpec((tr, d), lambda i: (i, 0))
    return pl.pallas_call(
        body, name=name, grid=(s // tr,), in_specs=[row, pl.BlockSpec((m, 2 * d), lambda i: (0, 0))],
        out_specs=row, out_shape=jax.ShapeDtypeStruct((s, d), BF16), compiler_params=_cp("parallel"),
    )(q, kv)


def _xattn_bwd(q, kv, do, name):
    s, d = q.shape
    m = kv.shape[0]
    tr = min(512, s)

    def body(q_ref, kv_ref, do_ref, dq_ref, dkv_ref):
        @pl.when(pl.program_id(0) == 0)
        def _():
            dkv_ref[...] = jnp.zeros_like(dkv_ref)

        for h in range(XA_HEADS):
            lo, hi = h * XA_DH, (h + 1) * XA_DH
            qh, kh, vh = q_ref[:, lo:hi], kv_ref[:, lo:hi], kv_ref[:, d + lo:d + hi]
            doh = do_ref[:, lo:hi]
            p = _xa_probs(qh, kh)
            dp = _mxu(doh, vh, 1, 1)
            ds = p * (dp - jnp.sum(p * dp, axis=-1, keepdims=True)) * (XA_DH ** -0.5)
            dq_ref[:, lo:hi] = _mxu(ds, kh).astype(BF16)
            dkv_ref[:, lo:hi] += _mxu(ds, qh, 0, 0)
            dkv_ref[:, d + lo:d + hi] += _mxu(p, doh, 0, 0)

    row = pl.BlockSpec((tr, d), lambda i: (i, 0))
    full = pl.BlockSpec((m, 2 * d), lambda i: (0, 0))
    return pl.pallas_call(
        body, name=name, grid=(s // tr,), in_specs=[row, full, row], out_specs=[row, full],
        out_shape=[jax.ShapeDtypeStruct((s, d), BF16), jax.ShapeDtypeStruct((m, 2 * d), F32)],
        compiler_params=_cp("arbitrary"),
    )(q, kv, do)


def _ret_tables():
    c = RET_CHUNK
    lg = np.log1p(-np.exp2(-5.0 - np.arange(RET_HEADS, dtype=np.float32))).astype(np.float32)
    idx = np.arange(c, dtype=np.float32)
    diff = idx[:, None] - idx[None, :]
    intra = np.where(diff >= 0, np.exp(lg[:, None, None] * np.where(diff >= 0, diff, 0.0)), 0.0)
    rk = np.broadcast_to(np.exp(lg[:, None] * (c - 1 - idx))[:, :, None], (RET_HEADS, c, LANES))
    rq = np.broadcast_to(np.exp(lg[:, None] * (idx + 1))[:, :, None], (RET_HEADS, c, LANES))
    return jnp.asarray(np.stack([intra, rk, rq], axis=1).astype(np.float32))


def _rope_tables(s):
    half = RET_DH // 2
    inv = jnp.exp(-math.log(10000.0) * jnp.arange(half, dtype=F32) / half)
    ang = jnp.arange(s, dtype=F32)[:, None] * inv[None, :]
    cos, sin = jnp.cos(ang), jnp.sin(ang)
    return jnp.concatenate([cos, cos], axis=1), jnp.concatenate([-sin, sin], axis=1)


def _ret_specs(n_of):
    c, w = RET_CHUNK, RET_HEADS * RET_DH

    def part(off):
        return pl.BlockSpec((c, w), lambda n: (n_of(n), off))

    pos = pl.BlockSpec((c, RET_DH), lambda n: (n_of(n), 0))
    gain = pl.BlockSpec((1, w), lambda n: (0, 0))
    tab = pl.BlockSpec((RET_HEADS, 3, c, LANES), lambda n: (0, 0, 0, 0))
    st = pl.BlockSpec((RET_HEADS, None, RET_DH, RET_DH), lambda n: (0, n_of(n), 0, 0))
    return part, pos, gain, tab, st


def _rheads(x):
    return jnp.stack([x[:, h * RET_DH:(h + 1) * RET_DH] for h in range(RET_HEADS)], axis=0)


def _runheads(x):
    return jnp.concatenate([x[h] for h in range(RET_HEADS)], axis=1)


def _rope(x, cos, sin):
    return x * cos + pltpu.roll(x, RET_DH // 2, 2) * sin


def _ret_chunk(q_ref, k_ref, v_ref, cos_ref, sin_ref, tab_ref, prev):
    cos, sin = cos_ref[...], sin_ref[...]
    q = _rope(_rheads(q_ref[...]), cos, sin)
    k = _rope(_rheads(k_ref[...]), cos, sin) * (RET_DH ** -0.5)
    v = _rheads(v_ref[...])
    scores = _bmxu(q, k, 2, 2) * tab_ref[:, 0]
    qdec = q * tab_ref[:, 2]
    kdec = k * tab_ref[:, 1]
    o = _bmxu(scores, v) + _bmxu(qdec, prev)
    return q, k, v, scores, qdec, kdec, o


def _ret_fwd(proj, cos, sin, gain, name):
    s = proj.shape[0]
    c = RET_CHUNK
    nc = s // c
    part, pos, gvec, tab, st = _ret_specs(lambda n: n)

    def body(q_ref, k_ref, v_ref, g_ref, cos_ref, sin_ref, rn_ref, tab_ref, o_ref, st_ref, state):
        @pl.when(pl.program_id(0) == 0)
        def _():
            state[...] = jnp.zeros_like(state)

        prev = state[...]
        st_ref[...] = prev
        _, _, v, _, _, kdec, o = _ret_chunk(q_ref, k_ref, v_ref, cos_ref, sin_ref, tab_ref, prev)
        state[...] = prev * tab_ref[:, 2, c - 1:c, :] + _bmxu(kdec, v, 1, 1)
        r = lax.rsqrt(jnp.mean(o * o, axis=-1, keepdims=True) + EPS)
        gate = g_ref[...]
        o_ref[...] = (_runheads(o * r) * rn_ref[...] * (gate * _sigmoid(gate))).astype(BF16)

    return pl.pallas_call(
        body, name=name, grid=(nc,),
        in_specs=[part(0), part(1), part(2), part(3), pos, pos, gvec, tab],
        out_specs=[part(0), st],
        out_shape=[jax.ShapeDtypeStruct((s, RET_HEADS * RET_DH), BF16),
                   jax.ShapeDtypeStruct((RET_HEADS, nc, RET_DH, RET_DH), F32)],
        scratch_shapes=[pltpu.VMEM((RET_HEADS, RET_DH, RET_DH), F32)],
        compiler_params=_cp("arbitrary"),
    )(proj, proj, proj, proj, cos, sin, gain.reshape(1, -1), _ret_tables())


def _ret_bwd(proj, cos, sin, gain, states, dmerged, name):
    s = proj.shape[0]
    c = RET_CHUNK
    nc = s // c
    part, pos, gvec, tab, st = _ret_specs(lambda n: nc - 1 - n)

    def body(q_ref, k_ref, v_ref, g_ref, cos_ref, sin_ref, rn_ref, tab_ref, st_ref, do_ref,
             dq_ref, dk_ref, dv_ref, dg_ref, drn_ref, carry):
        @pl.when(pl.program_id(0) == 0)
        def _():
            carry[...] = jnp.zeros_like(carry)
            drn_ref[...] = jnp.zeros_like(drn_ref)

        prev = st_ref[...]
        q, k, v, scores, qdec, kdec, o = _ret_chunk(q_ref, k_ref, v_ref, cos_ref, sin_ref, tab_ref, prev)
        r = lax.rsqrt(jnp.mean(o * o, axis=-1, keepdims=True) + EPS)
        on = o * r
        on2 = _runheads(on)
        gate = g_ref[...]
        sg = _sigmoid(gate)
        sil = gate * sg
        dout = do_ref[...]
        rn = rn_ref[...]
        dg_ref[...] = (dout * on2 * rn * (sg * (1.0 + gate * (1.0 - sg)))).astype(BF16)
        drn_ref[...] += jnp.sum(dout * on2 * sil, axis=0, keepdims=True)
        don = _rheads(dout * rn * sil)
        do = r * (don - on * jnp.mean(don * on, axis=-1, keepdims=True))
        dc = carry[...]
        dsc = _bmxu(do, v, 2, 2) * tab_ref[:, 0]
        dq = _bmxu(dsc, k) + _bmxu(do, prev, 2, 2) * tab_ref[:, 2]
        dk = _bmxu(dsc, q, 1, 1) + _bmxu(v, dc, 2, 2) * tab_ref[:, 1]
        dv = _bmxu(scores, do, 1, 1) + _bmxu(kdec, dc)
        carry[...] = _bmxu(qdec, do, 1, 1) + dc * tab_ref[:, 2, c - 1:c, :]
        cos, sin = cos_ref[...], sin_ref[...]
        dk = dk * (RET_DH ** -0.5)
        dq_ref[...] = _runheads(dq * cos + pltpu.roll(dq * sin, RET_DH // 2, 2)).astype(BF16)
        dk_ref[...] = _runheads(dk * cos + pltpu.roll(dk * sin, RET_DH // 2, 2)).astype(BF16)
        dv_ref[...] = _runheads(dv).astype(BF16)

    width = RET_HEADS * RET_DH
    return pl.pallas_call(
        body, name=name, grid=(nc,),
        in_specs=[part(0), part(1), part(2), part(3), pos, pos, gvec, tab, st, part(0)],
        out_specs=[part(0)] * 4 + [gvec],
        out_shape=[jax.ShapeDtypeStruct((s, width), BF16)] * 4 + [jax.ShapeDtypeStruct((1, width), F32)],
        scratch_shapes=[pltpu.VMEM((RET_HEADS, RET_DH, RET_DH), F32)],
        compiler_params=_cp("arbitrary"),
    )(proj, proj, proj, proj, cos, sin, gain.reshape(1, -1), _ret_tables(), states, dmerged)


S5_TILE = 512


def _cmul_add(xr, xi, ar, ai, yr, yi):
    return xr + ar * yr - ai * yi, xi + ar * yi + ai * yr


def _s5_pow_tables(a_il, name):
    r = SCAN_ROWS
    t = S5_TILE
    w2 = a_il.shape[1]

    def body(a_ref, up_ref, dn_ref):
        for j in range(w2 // (2 * t)):
            re, im = pl.ds(2 * t * j, t), pl.ds(2 * t * j + t, t)
            up_ref[0:1, re] = a_ref[:, re]
            up_ref[0:1, im] = a_ref[:, im]
            dn_ref[r - 1:r, re] = a_ref[:, re]
            dn_ref[r - 1:r, im] = -a_ref[:, im]
            n = 1
            while n < r:
                lr, li = up_ref[n - 1:n, re], up_ref[n - 1:n, im]
                xr, xi = up_ref[0:n, re], up_ref[0:n, im]
                up_ref[n:2 * n, re] = xr * lr - xi * li
                up_ref[n:2 * n, im] = xr * li + xi * lr
                yr, yi = dn_ref[r - n:r, re], dn_ref[r - n:r, im]
                dn_ref[r - 2 * n:r - n, re] = yr * lr + yi * li
                dn_ref[r - 2 * n:r - n, im] = yi * lr - yr * li
                n *= 2

    return pl.pallas_call(
        body, name=name, out_shape=[jax.ShapeDtypeStruct((r, w2), F32)] * 2, compiler_params=_cp(),
    )(a_il)


def _s5_scan_fwd(bu, apow, name):
    s, w2 = bu.shape
    r = SCAN_ROWS
    t = S5_TILE
    steps = r.bit_length() - 1

    def body(b_ref, p_ref, o_ref, cr, ci):
        @pl.when(pl.program_id(1) == 0)
        def _():
            cr[...] = jnp.zeros_like(cr)
            ci[...] = jnp.zeros_like(ci)

        xr, xi = b_ref[:, :t], b_ref[:, t:]
        for k in range(steps):
            sh = 1 << k
            xr, xi = _cmul_add(xr, xi, p_ref[sh - 1:sh, :t], p_ref[sh - 1:sh, t:],
                               _shift_down(xr, sh), _shift_down(xi, sh))
        xr, xi = _cmul_add(xr, xi, p_ref[:, :t], p_ref[:, t:], cr[...], ci[...])
        o_ref[:, :t] = xr
        o_ref[:, t:] = xi
        cr[...] = xr[r - 1:r, :]
        ci[...] = xi[r - 1:r, :]

    blk = pl.BlockSpec((r, 2 * t), lambda j, i: (i, j))
    return pl.pallas_call(
        body, name=name, grid=(w2 // (2 * t), s // r),
        in_specs=[blk, pl.BlockSpec((r, 2 * t), lambda j, i: (0, j))], out_specs=blk,
        out_shape=jax.ShapeDtypeStruct((s, w2), F32),
        scratch_shapes=[pltpu.VMEM((1, t), F32), pltpu.VMEM((1, t), F32)],
        compiler_params=_cp("parallel", "arbitrary"),
    )(bu, apow)


def _s5_scan_bwd(dst, apow_rev, st, name):
    s, w2 = dst.shape
    r = SCAN_ROWS
    t = S5_TILE
    nb = s // r
    steps = r.bit_length() - 1

    def body(d_ref, p_ref, s_ref, sp_ref, g_ref, da_ref, cr, ci):
        i = pl.program_id(1)

        @pl.when(i == 0)
        def _():
            cr[...] = jnp.zeros_like(cr)
            ci[...] = jnp.zeros_like(ci)
            da_ref[...] = jnp.zeros_like(da_ref)

        xr, xi = d_ref[:, :t], d_ref[:, t:]
        for k in range(steps):
            sh = 1 << k
            xr, xi = _cmul_add(xr, xi, p_ref[r - sh:r - sh + 1, :t], p_ref[r - sh:r - sh + 1, t:],
                               _shift_up(xr, sh), _shift_up(xi, sh))
        xr, xi = _cmul_add(xr, xi, p_ref[:, :t], p_ref[:, t:], cr[...], ci[...])
        g_ref[:, :t] = xr.astype(BF16)
        g_ref[:, t:] = xi.astype(BF16)
        cr[...] = xr[0:1, :]
        ci[...] = xi[0:1, :]
        first = i == nb - 1
        row = lax.broadcasted_iota(jnp.int32, (r, t), 0)
        last_r = jnp.where(first, 0.0, sp_ref[7:8, :t])
        last_i = jnp.where(first, 0.0, sp_ref[7:8, t:])
        pr = jnp.where(row == 0, last_r, pltpu.roll(s_ref[:, :t], 1, 0))
        pi = jnp.where(row == 0, last_i, pltpu.roll(s_ref[:, t:], 1, 0))
        da_ref[:, :t] += jnp.sum(xr * pr + xi * pi, axis=0, keepdims=True)
        da_ref[:, t:] += jnp.sum(xi * pr - xr * pi, axis=0, keepdims=True)

    blk = pl.BlockSpec((r, 2 * t), lambda j, i: (nb - 1 - i, j))
    halo = pl.BlockSpec((8, 2 * t), lambda j, i: (jnp.maximum((nb - 1 - i) * (r // 8) - 1, 0), j))
    vec = pl.BlockSpec((1, 2 * t), lambda j, i: (0, j))
    return pl.pallas_call(
        body, name=name, grid=(w2 // (2 * t), nb),
        in_specs=[blk, pl.BlockSpec((r, 2 * t), lambda j, i: (0, j)), blk, halo], out_specs=[blk, vec],
        out_shape=[jax.ShapeDtypeStruct((s, w2), BF16), jax.ShapeDtypeStruct((1, w2), F32)],
        scratch_shapes=[pltpu.VMEM((1, t), F32), pltpu.VMEM((1, t), F32)],
        compiler_params=_cp("parallel", "arbitrary"),
    )(dst, apow_rev, st, st)


_GELU_C = math.sqrt(2.0 / math.pi)
_GELU_A = 0.044715


def _gelu(y):
    return 0.5 * y * (1.0 + jnp.tanh(_GELU_C * (y + _GELU_A * y * y * y)))


def _gelu_grad(y):
    th = jnp.tanh(_GELU_C * (y + _GELU_A * y * y * y))
    return 0.5 * (1.0 + th) + 0.5 * y * (1.0 - th * th) * _GELU_C * (1.0 + 3.0 * _GELU_A * y * y)


def _row_call(body, name, s, ins, outs, acc=False):
    tr = min(512, s)

    def spec(width, cb, rows):
        if rows == 1:
            return pl.BlockSpec((1, width), lambda i: (0, cb))
        return pl.BlockSpec((tr, width), lambda i: (i, cb))

    in_specs = [spec(w, cb, a.shape[0]) for a, w, cb in ins]
    out_specs = [spec(w, cb, sd.shape[0]) for sd, w, cb in outs]
    return pl.pallas_call(
        body, name=name, grid=(s // tr,), in_specs=in_specs, out_specs=out_specs,
        out_shape=[sd for sd, _, _ in outs],
        compiler_params=_cp("arbitrary" if acc else "parallel"),
    )(*[a for a, _, _ in ins])


def _sds(shape, dtype):
    return jax.ShapeDtypeStruct(shape, dtype)


def _s5_gelu_fwd(yraw, proj, dvec, name):
    s, w = yraw.shape

    def body(y_ref, u_ref, d_ref, yo_ref, g_ref):
        y = y_ref[...] + d_ref[...] * u_ref[...]
        yo_ref[...] = y
        g_ref[...] = _gelu(y).astype(BF16)

    return _row_call(body, name, s, [(yraw, w, 0), (proj, w, 4), (dvec, w, 0)],
                     [(_sds((s, w), F32), w, 0), (_sds((s, w), BF16), w, 0)])


def _s5_glu_fwd(y, z, b, name):
    s, w = y.shape

    def body(y_ref, z_ref, b_ref, o_ref):
        o_ref[...] = (_gelu(y_ref[...]) * _sigmoid(z_ref[...] + b_ref[...])).astype(BF16)

    return _row_call(body, name, s, [(y, w, 0), (z, w, 0), (b, w, 0)], [(_sds((s, w), BF16), w, 0)])[0]


def _s5_glu_bwd(dmerged, y, z, b, name):
    s, w = y.shape

    def body(do_ref, y_ref, z_ref, b_ref, dz_ref, dg_ref, db_ref):
        @pl.when(pl.program_id(0) == 0)
        def _():
            db_ref[...] = jnp.zeros_like(db_ref)

        g = _gelu(y_ref[...])
        sg = _sigmoid(z_ref[...] + b_ref[...])
        dout = do_ref[...]
        dz = dout * g * sg * (1.0 - sg)
        dz_ref[...] = dz.astype(BF16)
        dg_ref[...] = dout * sg
        db_ref[...] += jnp.sum(dz, axis=0, keepdims=True)

    return _row_call(body, name, s, [(dmerged, w, 1), (y, w, 0), (z, w, 0), (b, w, 0)],
                     [(_sds((s, w), BF16), w, 0), (_sds((s, w), F32), w, 0), (_sds((1, w), F32), w, 0)], acc=True)


def _s5_gelu_bwd(dg1, dg2, y, proj, dvec, name):
    s, w = y.shape

    def body(a_ref, b_ref, y_ref, u_ref, d_ref, dy_ref, du_ref, dd_ref):
        @pl.when(pl.program_id(0) == 0)
        def _():
            dd_ref[...] = jnp.zeros_like(dd_ref)

        dy = (a_ref[...] + b_ref[...]) * _gelu_grad(y_ref[...])
        dy_ref[...] = dy.astype(BF16)
        du_ref[...] = dy * d_ref[...]
        dd_ref[...] += jnp.sum(dy * u_ref[...], axis=0, keepdims=True)

    return _row_call(body, name, s, [(dg1, w, 0), (dg2, w, 0), (y, w, 0), (proj, w, 4), (dvec, w, 0)],
                     [(_sds((s, w), BF16), w, 0), (_sds((s, w), F32), w, 0), (_sds((1, w), F32), w, 0)], acc=True)


def _gdn_conv_fwd(projx, cw, name):
    s = projx.shape[0]
    nh = GDN_HEADS

    def body(x_ref, w_ref, o_ref):
        j = pl.program_id(0)
        cv = _conv_fwd(x_ref[...], w_ref, GDN_CONV)
        y = cv * _sigmoid(cv)
        nrm = y * lax.rsqrt(jnp.sum(y * y, axis=-1, keepdims=True) + EPS)
        o_ref[...] = jnp.where(j < nh, nrm * (GDN_DH ** -0.5), jnp.where(j < 2 * nh, nrm, y))

    return pl.pallas_call(
        body, name=name, grid=(3 * nh,),
        in_specs=[pl.BlockSpec((s, GDN_DH), lambda j: (0, j)), pl.BlockSpec((GDN_CONV, GDN_DH), lambda j: (0, j))],
        out_specs=pl.BlockSpec((s, GDN_DH), lambda j: (0, j)),
        out_shape=jax.ShapeDtypeStruct((s, 3 * nh * GDN_DH), F32), compiler_params=_cp("parallel"),
    )(projx, cw)


def _gdn_conv_bwd(projx, cw, dqkv, name):
    s = projx.shape[0]
    nh = GDN_HEADS

    def body(x_ref, w_ref, d_ref, dx_ref, dw_ref):
        j = pl.program_id(0)
        x = x_ref[...]
        cv = _conv_fwd(x, w_ref, GDN_CONV)
        sg = _sigmoid(cv)
        y = cv * sg
        rinv = lax.rsqrt(jnp.sum(y * y, axis=-1, keepdims=True) + EPS)
        nrm = y * rinv
        dn = d_ref[...]
        dns = jnp.where(j < nh, dn * (GDN_DH ** -0.5), dn)
        dyn = rinv * (dns - nrm * jnp.sum(dns * nrm, axis=-1, keepdims=True))
        dy = jnp.where(j < 2 * nh, dyn, dn)
        dc = dy * (sg * (1.0 + cv * (1.0 - sg)))
        dx_ref[...] = _conv_bwd(x, dc, w_ref, dw_ref, GDN_CONV).astype(BF16)

    col = pl.BlockSpec((s, GDN_DH), lambda j: (0, j))
    wcol = pl.BlockSpec((GDN_CONV, GDN_DH), lambda j: (0, j))
    return pl.pallas_call(
        body, name=name, grid=(3 * nh,), in_specs=[col, wcol, col], out_specs=[col, wcol],
        out_shape=[jax.ShapeDtypeStruct((s, 3 * nh * GDN_DH), BF16), jax.ShapeDtypeStruct((GDN_CONV, 3 * nh * GDN_DH), F32)],
        compiler_params=_cp("parallel"),
    )(projx, cw, dqkv)


def _softplus(x):
    return jnp.maximum(x, 0.0) + jnp.log1p(jnp.exp(-jnp.abs(x)))


def _gdn_gates_fwd(projx, alog, dtb, name):
    s = projx.shape[0]
    w = GDN_HEADS * GDN_DH

    def body(b_ref, a_ref, al_ref, dt_ref, bo_ref, go_ref):
        bo_ref[...] = _sigmoid(b_ref[...])
        go_ref[...] = -jnp.exp(al_ref[...]) * _softplus(a_ref[...] + dt_ref[...])

    return _row_call(body, name, s, [(projx, w, 4), (projx, w, 5), (alog, w, 0), (dtb, w, 0)],
                     [(_sds((s, w), F32), w, 0), (_sds((s, w), F32), w, 0)])


def _gdn_gates_bwd(projx, alog, dtb, dbeta, dg, name):
    s = projx.shape[0]
    w = GDN_HEADS * GDN_DH

    def body(b_ref, a_ref, al_ref, dt_ref, dbe_ref, dg_ref, db_ref, da_ref, dal_ref, ddt_ref):
        @pl.when(pl.program_id(0) == 0)
        def _():
            dal_ref[...] = jnp.zeros_like(dal_ref)
            ddt_ref[...] = jnp.zeros_like(ddt_ref)

        for h in range(GDN_HEADS):
            lo, hi = h * GDN_DH, (h + 1) * GDN_DH
            beta = _sigmoid(b_ref[:, lo:hi])
            pb = jnp.sum(dbe_ref[:, lo:hi], axis=-1, keepdims=True) * (1.0 / GDN_DH)
            db_ref[:, lo:hi] = (pb * beta * (1.0 - beta)).astype(BF16)
            xa = a_ref[:, lo:hi] + dt_ref[:, lo:hi]
            ea = -jnp.exp(al_ref[:, lo:hi])
            pg = jnp.sum(dg_ref[:, lo:hi], axis=-1, keepdims=True) * (1.0 / GDN_DH)
            da = pg * ea * _sigmoid(xa)
            da_ref[:, lo:hi] = da.astype(BF16)
            dal_ref[:, lo:hi] += jnp.sum(pg * ea * _softplus(xa), axis=0, keepdims=True)
            ddt_ref[:, lo:hi] += jnp.sum(da, axis=0, keepdims=True)

    return _row_call(body, name, s,
                     [(projx, w, 4), (projx, w, 5), (alog, w, 0), (dtb, w, 0), (dbeta, w, 0), (dg, w, 0)],
                     [(_sds((s, w), BF16), w, 0), (_sds((s, w), BF16), w, 0),
                      (_sds((1, w), F32), w, 0), (_sds((1, w), F32), w, 0)], acc=True)


def _gdn_tri():
    c = GDN_CHUNK
    i = lax.broadcasted_iota(jnp.int32, (c, c), 0)
    j = lax.broadcasted_iota(jnp.int32, (c, c), 1)
    return ((i >= j).astype(F32), (i <= j).astype(F32), i >= j, i > j, (i == j).astype(F32))


def _bdot(a, b, ca=2, cb=1, precision=None):
    return lax.dot_general(a, b, (((ca,), (cb,)), ((0,), (0,))), precision=precision, preferred_element_type=F32)


def _bmxu(a, b, ca=2, cb=1):
    return _bdot(a.astype(BF16), b.astype(BF16), ca, cb)


def _split(x):
    hi = x.astype(BF16)
    return hi, (x - hi.astype(F32)).astype(BF16)


def _bdot3(a, b, ca=2, cb=1):
    ah, al = _split(a)
    bh, bl = _split(b)
    return _bdot(ah, bh, ca, cb) + (_bdot(ah, bl, ca, cb) + _bdot(al, bh, ca, cb))


def _tri_dot(tri, x):
    t = tri.astype(BF16)
    hi = x.astype(BF16)
    r1 = x - hi.astype(F32)
    mid = r1.astype(BF16)
    lo = (r1 - mid.astype(F32)).astype(BF16)
    return _dot(t, hi) + (_dot(t, mid) + _dot(t, lo))


def _heads(x):
    return jnp.stack([x[:, h * GDN_DH:(h + 1) * GDN_DH] for h in range(GDN_HEADS)], axis=0)


def _unheads(x):
    return jnp.concatenate([x[h] for h in range(GDN_HEADS)], axis=1)


def _gdn_chunk(q, k, v, bb, g2d, tri):
    low, up, incl, strict, eye = tri
    c = GDN_CHUNK
    gc = _heads(_tri_dot(low, g2d))
    gci = gc[:, :, :c]
    gdiff = gci - jnp.swapaxes(gci, 1, 2)
    decay = jnp.where(incl, jnp.exp(jnp.where(incl, gdiff, 0.0)), 0.0)
    kb, vb = k * bb, v * bb
    kbk = _bmxu(kb, k, 2, 2)
    x = -jnp.where(strict, kbk * decay, 0.0)
    t = eye + x
    p = x
    for _ in range(c.bit_length() - 2):
        p = _bdot3(p, p)
        t = t + _bdot3(t, p)
    eg = jnp.exp(gc)
    kbg = kb * eg
    gcl = gc[:, c - 1:c, :]
    ek = jnp.exp(gcl - gc)
    qkraw = _bmxu(q, k, 2, 2)
    return dict(decay=decay, kb=kb, vb=vb, kbk=kbk, t=t, eg=eg, kbg=kbg, ek=ek, gl=jnp.exp(gcl),
                w=_bmxu(t, kbg), u=_bmxu(t, vb), qkraw=qkraw, qk=jnp.where(incl, qkraw * decay, 0.0),
                qd=q * eg, kd=k * ek)


def _gdn_specs(n_of):
    c, w = GDN_CHUNK, GDN_HEADS * GDN_DH

    def blk(cb, width=w):
        return pl.BlockSpec((c, width), lambda n: (n_of(n), cb))

    st = pl.BlockSpec((None, GDN_HEADS, GDN_DH, GDN_DH), lambda n: (n_of(n), 0, 0, 0))
    vec = pl.BlockSpec((1, GDN_DH), lambda n: (0, 0))
    return blk, st, vec


def _gdn_load(qkv_ref, b_ref, g_ref, tri):
    w = GDN_HEADS * GDN_DH
    q, k, v = _heads(qkv_ref[:, :w]), _heads(qkv_ref[:, w:2 * w]), _heads(qkv_ref[:, 2 * w:])
    bb = _heads(b_ref[...])
    return q, k, v, bb, _gdn_chunk(q, k, v, bb, g_ref[...], tri)


def _gdn_fwd(qkv, beta, g, projx, onorm, name):
    s = qkv.shape[0]
    nc = s // GDN_CHUNK
    w = GDN_HEADS * GDN_DH
    blk, st, vec = _gdn_specs(lambda n: n)

    def body(qkv_ref, b_ref, g_ref, z_ref, on_ref, o_ref, st_ref, state):
        @pl.when(pl.program_id(0) == 0)
        def _():
            state[...] = jnp.zeros_like(state)

        _, _, _, _, ch = _gdn_load(qkv_ref, b_ref, g_ref, _gdn_tri())
        sp = state[...]
        st_ref[...] = sp
        vn = ch["u"] - _bmxu(ch["w"], sp)
        o = _bmxu(ch["qd"], sp) + _bmxu(ch["qk"], vn)
        state[...] = sp * ch["gl"] + _bmxu(ch["kd"], vn, 1, 1)
        r = lax.rsqrt(jnp.mean(o * o, axis=-1, keepdims=True) + EPS)
        z = _heads(z_ref[...])
        o_ref[...] = _unheads(o * r * on_ref[...] * (z * _sigmoid(z))).astype(BF16)

    return pl.pallas_call(
        body, name=name, grid=(nc,),
        in_specs=[blk(0, 3 * w), blk(0), blk(0), blk(3), vec], out_specs=[blk(0), st],
        out_shape=[jax.ShapeDtypeStruct((s, w), BF16), jax.ShapeDtypeStruct((nc, GDN_HEADS, GDN_DH, GDN_DH), F32)],
        scratch_shapes=[pltpu.VMEM((GDN_HEADS, GDN_DH, GDN_DH), F32)],
        compiler_params=_cp("arbitrary"),
    )(qkv, beta, g, projx, onorm.reshape(1, -1))


def _gdn_bwd(qkv, beta, g, projx, onorm, states, dout, name):
    s = qkv.shape[0]
    c = GDN_CHUNK
    nc = s // c
    w = GDN_HEADS * GDN_DH
    blk, st, vec = _gdn_specs(lambda n: nc - 1 - n)

    def body(qkv_ref, b_ref, g_ref, z_ref, on_ref, st_ref, do_ref,
             dqkv_ref, db_ref, dg_ref, dz_ref, don_ref, carry):
        @pl.when(pl.program_id(0) == 0)
        def _():
            carry[...] = jnp.zeros_like(carry)
            don_ref[...] = jnp.zeros_like(don_ref)

        tri = _gdn_tri()
        low, up, incl, strict, eye = tri
        q, k, v, bb, ch = _gdn_load(qkv_ref, b_ref, g_ref, tri)
        sp = st_ref[...]
        vn = ch["u"] - _bmxu(ch["w"], sp)
        o = _bmxu(ch["qd"], sp) + _bmxu(ch["qk"], vn)
        r = lax.rsqrt(jnp.mean(o * o, axis=-1, keepdims=True) + EPS)
        orn = o * r
        z = _heads(z_ref[...])
        sg = _sigmoid(z)
        dout = _heads(do_ref[...])
        onw = on_ref[...]
        dz_ref[...] = _unheads(dout * orn * onw * (sg * (1.0 + z * (1.0 - sg)))).astype(BF16)
        don = dout * (z * sg)
        don_ref[...] += jnp.sum(jnp.sum(don * orn, axis=0), axis=0, keepdims=True)
        dor = don * onw
        do = r * (dor - orn * jnp.mean(dor * orn, axis=-1, keepdims=True))
        dsn = carry[...]
        dqd = _bmxu(do, sp, 2, 2)
        dqk = jnp.where(incl, _bmxu(do, vn, 2, 2), 0.0)
        dvn = _bmxu(ch["qk"], do, 1, 1) + _bmxu(ch["kd"], dsn)
        dkd = _bmxu(vn, dsn, 2, 2)
        dgl = jnp.sum(dsn * sp, axis=1, keepdims=True)
        dw = -_bmxu(dvn, sp, 2, 2)
        carry[...] = _bmxu(ch["qd"], do, 1, 1) + dsn * ch["gl"] - _bmxu(ch["w"], dvn, 1, 1)
        t = ch["t"]
        dvb = _bmxu(t, dvn, 1, 1)
        dkbg = _bmxu(t, dw, 1, 1)
        dt = _bmxu(dvn, ch["vb"], 2, 2) + _bmxu(dw, ch["kbg"], 2, 2)
        da = -_bdot3(_bdot3(t, dt, 1, 1), t, 2, 2)
        da = jnp.where(strict, da, 0.0)
        decay = ch["decay"]
        dkbk = da * decay
        dqkr = dqk * decay
        mdec = (da * ch["kbk"] + dqk * ch["qkraw"]) * decay
        dkb = _bmxu(dkbk, k) + dkbg * ch["eg"]
        dk = _bmxu(dkbk, ch["kb"], 1, 1) + _bmxu(dqkr, q, 1, 1) + dkd * ch["ek"] + dkb * bb
        dq = _bmxu(dqkr, k) + dqd * ch["eg"]
        tk = dkd * ch["kd"]
        dgcl = jnp.sum(tk, axis=1, keepdims=True) + dgl * ch["gl"]
        row = lax.broadcasted_iota(jnp.int32, (GDN_HEADS, c, GDN_DH), 1)
        zpad = jnp.zeros((GDN_HEADS, c, GDN_DH - c), F32)
        dgc = (jnp.concatenate([mdec, zpad], axis=2) - jnp.concatenate([jnp.swapaxes(mdec, 1, 2), zpad], axis=2)
               + dqd * ch["qd"] - tk + dkbg * ch["kbg"] + jnp.where(row == c - 1, dgcl, 0.0))
        dqkv_ref[:, :w] = _unheads(dq)
        dqkv_ref[:, w:2 * w] = _unheads(dk)
        dqkv_ref[:, 2 * w:] = _unheads(dvb * bb)
        db_ref[...] = _unheads(dkb * k + dvb * v)
        dg_ref[...] = _tri_dot(up, _unheads(dgc))

    return pl.pallas_call(
        body, name=name, grid=(nc,),
        in_specs=[blk(0, 3 * w), blk(0), blk(0), blk(3), vec, st, blk(0)],
        out_specs=[blk(0, 3 * w), blk(0), blk(0), blk(0), vec],
        out_shape=[jax.ShapeDtypeStruct((s, 3 * w), F32), jax.ShapeDtypeStruct((s, w), F32),
                   jax.ShapeDtypeStruct((s, w), F32), jax.ShapeDtypeStruct((s, w), BF16),
                   jax.ShapeDtypeStruct((1, GDN_DH), F32)],
        scratch_shapes=[pltpu.VMEM((GDN_HEADS, GDN_DH, GDN_DH), F32)],
        compiler_params=_cp("arbitrary"),
    )(qkv, beta, g, projx, onorm.reshape(1, -1), states, dout)


_WEIGHTS = (
    "l0_mix_norm", "l0_w_in", "l0_ret_norm", "l0_s5_lambda_re", "l0_s5_lambda_im", "l0_s5_b_re", "l0_s5_b_im",
    "l0_s5_c_re", "l0_s5_c_im", "l0_s5_d", "l0_s5_log_dt", "l0_s5_w_glu", "l0_s5_b_glu", "l0_w_out",
    "l0_xa_norm", "l0_mem_norm", "l0_xa_wq", "l0_xa_wkv", "l0_xa_wo", "l0_ffn_norm", "l0_ffn_w_up",
    "l0_ffn_conv", "l0_ffn_w_down", "l1_mix_norm", "l1_w_in", "l1_conv", "l1_a_log", "l1_dt_bias", "l1_o_norm",
    "l1_w_out", "l1_xa_norm", "l1_mem_norm", "l1_xa_wq", "l1_xa_wkv", "l1_xa_wo", "l1_ffn_norm", "l1_ffn_w_up",
    "l1_ffn_conv", "l1_ffn_w_down", "final_norm")
_INPUTS = ("x", "mem") + _WEIGHTS + ("loss_target",) + tuple("m_" + n for n in _WEIGHTS) + tuple("v_" + n for n in _WEIGHTS)

_COL = ("l0_w_in", "l0_xa_wkv", "l0_ffn_w_up", "l0_ffn_conv", "l1_w_in", "l1_conv", "l1_xa_wkv", "l1_ffn_w_up",
        "l1_ffn_conv")
_ROW = ("l0_s5_w_glu", "l0_w_out", "l0_xa_wq", "l0_xa_wo", "l0_ffn_w_down", "l1_w_out", "l1_xa_wq", "l1_xa_wo",
        "l1_ffn_w_down")
_F32_WIRE = ("l0_ffn_conv", "l1_conv", "l1_ffn_conv")
_REP = tuple(n for n in _WEIGHTS if n not in _COL + _ROW)
_GATHER_GROUPS = (("l0_w_in", "l0_s5_w_glu", "l0_w_out"),
                  ("l0_xa_wq", "l0_xa_wkv", "l0_xa_wo", "l0_ffn_w_up", "l0_ffn_conv", "l0_ffn_w_down"),
                  ("l1_w_in", "l1_conv", "l1_w_out", "l1_xa_wq", "l1_xa_wkv", "l1_xa_wo", "l1_ffn_w_up", "l1_ffn_conv",
                   "l1_ffn_w_down"))


def _round_up(n, m):
    return (n + m - 1) // m * m


_REP_BIG = ("l0_s5_lambda_re", "l0_s5_lambda_im", "l0_s5_b_re", "l0_s5_b_im", "l0_s5_c_re", "l0_s5_c_im", "l0_s5_d")
_REP_SMALL = tuple(n for n in _REP if n not in _REP_BIG)
PACK_WIDTH = 1024


def _pack_rows(ts):
    rows = [jnp.pad(t, ((0, 0), (0, PACK_WIDTH - t.shape[1]))) for t in ts]
    rows.append(jnp.zeros((_round_up(len(ts), 8) - len(ts), PACK_WIDTH), F32))
    return jnp.concatenate(rows, axis=0)


def _s5_interleave(re, im):
    lead = re.shape[:-1]
    nt = re.shape[-1] // S5_TILE
    both = jnp.stack([re.reshape(lead + (nt, S5_TILE)), im.reshape(lead + (nt, S5_TILE))], axis=-2)
    return both.reshape(lead + (2 * re.shape[-1],))


def _s5_split(x):
    lead = x.shape[:-1]
    y = x.reshape(lead + (x.shape[-1] // (2 * S5_TILE), 2, S5_TILE))
    return y[..., 0, :].reshape(lead + (-1,)), y[..., 1, :].reshape(lead + (-1,))


def _s5_discretise(lr, li, log_dt, b_re, b_im):
    dt = jnp.exp(log_dt)[:, None]
    mag = jnp.exp(lr * dt)
    a_re = mag * jnp.cos(li * dt)
    a_im = mag * jnp.sin(li * dt)
    den = lr * lr + li * li
    z_re = ((a_re - 1.0) * lr + a_im * li) / den
    z_im = (a_im * lr - (a_re - 1.0) * li) / den
    bb_re = z_re[:, None, :] * b_re - z_im[:, None, :] * b_im
    bb_im = z_re[:, None, :] * b_im + z_im[:, None, :] * b_re
    return a_re, a_im, bb_re, bb_im


def _block_diag(b):
    g, r, c = b.shape
    return jnp.einsum("grc,gk->grkc", b, jnp.eye(g, dtype=b.dtype)).reshape(g * r, g * c)


def _block_diag_of(d, g):
    r, c = d.shape[0] // g, d.shape[1] // g
    return jnp.einsum("grkc,gk->grc", d.reshape(g, r, g, c), jnp.eye(g, dtype=d.dtype))


def kernel(*args):
    p = dict(zip(_INPUTS, args, strict=True))
    x0, mem0, tgt = p["x"][0], p["mem"][0], p["loss_target"][0]
    s, d = x0.shape
    me = _slot(*_mesh_pos())
    grads = {}
    wire = {n: (F32 if n in _F32_WIRE else BF16) for n in _COL + _ROW}

    zones = {n: _into_slot(p[n], wire[n], me, "place_" + n) for names in _GATHER_GROUPS for n in names}
    gather, pin = [], jnp.zeros((), F32)
    for i, names in enumerate(_GATHER_GROUPS):
        handle, token = _push_start([], [zones[n] for n in names], f"gather{i}_start")
        gather.append(handle)
        pin = pin + token[0, 0]
    w = {}

    def gathered(i, after):
        for n, full in zip(_GATHER_GROUPS[i], _push_wait(gather[i], after, f"gather{i}_wait")):
            if n in _COL:
                full = full.transpose(1, 0, 2)
            w[n] = full.reshape(-1, full.shape[-1]) if n in _ROW else full.reshape(full.shape[0], -1)

    pending = []

    def exchange(names, gain, tag):
        slots = []
        for n in names:
            g = grads[n]
            if n in _COL:
                g = g.reshape(g.shape[0], N_DEV, -1).transpose(1, 0, 2)
            else:
                g = g.reshape((N_DEV, -1) + g.shape[1:])
            slots.append(g.astype(wire[n]))
        handle, token = _push_start(slots, [], tag + "_start")
        pending.append((names, slots, handle, tag))
        return gain + token[0, 0]

    def xattn(pre, x_in):
        hx = _norm_fwd(x_in, p[pre + "xa_norm"], pre + "xa_norm_fwd")
        q = _mm(hx, w[pre + "xa_wq"], out_dtype=BF16, name=pre + "xa_q")
        memn = _norm_fwd(mem0, p[pre + "mem_norm"], pre + "mem_norm_fwd")
        kv = _mm(memn, w[pre + "xa_wkv"], out_dtype=BF16, name=pre + "xa_kv")
        ao = _xattn_fwd(q, kv, pre + "xattn_fwd")
        x_out = _mm(ao, w[pre + "xa_wo"], res=x_in, name=pre + "xa_o")
        return x_out, (x_in, hx, q, memn, kv, ao)

    def xattn_bwd(pre, saved, dxo):
        x_in, hx, q, memn, kv, ao = saved
        dao = _mm(dxo, w[pre + "xa_wo"], tb=True, name=pre + "xa_o_dx")
        grads[pre + "xa_wo"] = _mm(ao, dxo, ta=True, out_dtype=BF16, name=pre + "xa_o_dw")
        dq, dkv = _xattn_bwd(q, kv, dao, pre + "xattn_bwd")
        grads[pre + "xa_wq"] = _mm(hx, dq, ta=True, out_dtype=BF16, name=pre + "xa_q_dw")
        dhx = _mm(dq, w[pre + "xa_wq"], tb=True, name=pre + "xa_q_dx")
        grads[pre + "xa_wkv"] = _mm(memn, dkv, ta=True, out_dtype=BF16, name=pre + "xa_kv_dw")
        dmemn = _mm(dkv, w[pre + "xa_wkv"], tb=True, name=pre + "xa_kv_dx")
        gain = exchange((pre + "xa_wo", pre + "xa_wq", pre + "xa_wkv"), p[pre + "xa_norm"], pre + "xa_grads")
        dx_in, grads[pre + "xa_norm"] = _norm_bwd(x_in, gain, dhx, dxo, pre + "xa_norm_bwd")
        _, grads[pre + "mem_norm"] = _norm_bwd(mem0, p[pre + "mem_norm"], dmemn, jnp.zeros_like(mem0), pre + "mem_norm_bwd")
        return dx_in

    def ffn(pre, x_in):
        hf = _norm_fwd(x_in, p[pre + "ffn_norm"], pre + "ffn_norm_fwd")
        up = _mm(hf, w[pre + "ffn_w_up"], out_dtype=BF16, name=pre + "ffn_up")
        act = _ffn_act_fwd(up, w[pre + "ffn_conv"], pre + "ffn_act_fwd")
        x_out = _mm(act, w[pre + "ffn_w_down"], res=x_in, name=pre + "ffn_down")
        return x_out, (x_in, hf, up, act)

    def ffn_bwd(pre, saved, dxo):
        x_in, hf, up, act = saved
        dact = _mm(dxo, w[pre + "ffn_w_down"], tb=True, out_dtype=BF16, name=pre + "ffn_down_dx")
        grads[pre + "ffn_w_down"] = _mm(act, dxo, ta=True, out_dtype=BF16, name=pre + "ffn_down_dw")
        dpu, dpg, dcu, dcg = _ffn_act_bwd(up, w[pre + "ffn_conv"], dact, pre + "ffn_act_bwd")
        dup = jnp.concatenate([dpu, dpg], axis=1)
        grads[pre + "ffn_conv"] = jnp.concatenate([dcu, dcg], axis=1)
        dhf = _mm(dup, w[pre + "ffn_w_up"], tb=True, name=pre + "ffn_up_dx")
        grads[pre + "ffn_w_up"] = _mm(hf, dup, ta=True, out_dtype=BF16, name=pre + "ffn_up_dw")
        gain = exchange((pre + "ffn_w_down", pre + "ffn_w_up", pre + "ffn_conv"), p[pre + "ffn_norm"], pre + "ffn_grads")
        dx_in, grads[pre + "ffn_norm"] = _norm_bwd(x_in, gain, dhf, dxo, pre + "ffn_norm_bwd")
        return dx_in

    cos, sin = _rope_tables(s)
    (a_re, a_im, bb_re, bb_im), disc_vjp = jax.vjp(
        _s5_discretise, p["l0_s5_lambda_re"], p["l0_s5_lambda_im"], p["l0_s5_log_dt"], p["l0_s5_b_re"], p["l0_s5_b_im"])
    apow, apow_rev = _s5_pow_tables(_s5_interleave(a_re.reshape(1, -1), a_im.reshape(1, -1)), "l0_s5_pow_tables")
    bbig = _s5_interleave(_block_diag(bb_re), _block_diag(bb_im)).astype(BF16)
    cbig = _s5_interleave(_block_diag(p["l0_s5_c_re"]).T, -_block_diag(p["l0_s5_c_im"]).T).T.astype(BF16)
    s5_d = p["l0_s5_d"].reshape(1, -1)
    b_glu = p["l0_s5_b_glu"].reshape(1, -1)

    h0 = _norm_fwd(x0, p["l0_mix_norm"] + pin, "l0_mix_norm_fwd")
    gathered(0, h0)
    proj = _mm(h0, w["l0_w_in"], name="l0_in")
    o_ret, ret_states = _ret_fwd(proj, cos, sin, p["l0_ret_norm"], "l0_ret_fwd")
    u = proj[:, 4 * RET_HEADS * RET_DH:]
    bu = _mm(u, bbig, name="l0_s5_bu")
    st = _s5_scan_fwd(bu, apow, "l0_s5_scan_fwd")
    yraw = _mm(st, cbig, name="l0_s5_c")
    y, gy = _s5_gelu_fwd(yraw, proj, s5_d, "l0_s5_gelu_fwd")
    z = _mm(gy, w["l0_s5_w_glu"], name="l0_s5_glu_mm")
    y2 = _s5_glu_fwd(y, z, b_glu, "l0_s5_glu_fwd")
    merged = jnp.concatenate([o_ret, y2], axis=1)
    x1 = _mm(merged, w["l0_w_out"], res=x0, name="l0_out")
    gathered(1, x1)
    x2, xa0 = xattn("l0_", x1)
    x3, ff0 = ffn("l0_", x2)

    gathered(2, x3)
    nqkv = 4 * GDN_HEADS * GDN_DH
    w1 = w["l1_w_in"]
    wx = jnp.concatenate([w1[:, :nqkv], jnp.repeat(w1[:, nqkv:nqkv + GDN_HEADS], GDN_DH, axis=1),
                          jnp.repeat(w1[:, nqkv + GDN_HEADS:], GDN_DH, axis=1)], axis=1)
    alog_x = jnp.repeat(p["l1_a_log"], GDN_DH).reshape(1, -1)
    dtb_x = jnp.repeat(p["l1_dt_bias"], GDN_DH).reshape(1, -1)
    h1 = _norm_fwd(x3, p["l1_mix_norm"], "l1_mix_norm_fwd")
    projx = _mm(h1, wx, name="l1_in")
    qkv = _gdn_conv_fwd(projx, w["l1_conv"], "l1_conv_fwd")
    beta, glog = _gdn_gates_fwd(projx, alog_x, dtb_x, "l1_gates_fwd")
    o_gdn, gdn_states = _gdn_fwd(qkv, beta, glog, projx, p["l1_o_norm"], "l1_gdn_fwd")
    x4 = _mm(o_gdn, w["l1_w_out"], res=x3, name="l1_out")
    x5, xa1 = xattn("l1_", x4)
    x6, ff1 = ffn("l1_", x5)

    loss_part, dx6, grads["final_norm"] = _loss_head(x6, p["final_norm"], tgt, "loss_head")
    loss = lax.psum(loss_part[0, 0], ("x", "y", "c"))
    dx5 = ffn_bwd("l1_", ff1, dx6)
    dx4 = xattn_bwd("l1_", xa1, dx5)

    do_gdn = _mm(dx4, w["l1_w_out"], tb=True, name="l1_out_dx")
    grads["l1_w_out"] = _mm(o_gdn, dx4, ta=True, out_dtype=BF16, name="l1_out_dw")
    dqkv, dbeta, dglog, dz, grads["l1_o_norm"] = _gdn_bwd(
        qkv, beta, glog, projx, p["l1_o_norm"], gdn_states, do_gdn, "l1_gdn_bwd")
    dpre, grads["l1_conv"] = _gdn_conv_bwd(projx, w["l1_conv"], dqkv, "l1_conv_bwd")
    db, da, dalog_x, ddtb_x = _gdn_gates_bwd(projx, alog_x, dtb_x, dbeta, dglog, "l1_gates_bwd")
    dprojx = jnp.concatenate([dpre, dz, db, da], axis=1)
    dh1 = _mm(dprojx, wx, tb=True, name="l1_in_dx")
    dwx = _mm(h1, dprojx, ta=True, name="l1_in_dw")
    grads["l1_w_in"] = jnp.concatenate(
        [dwx[:, :nqkv], dwx[:, nqkv:nqkv + GDN_HEADS * GDN_DH].reshape(d, GDN_HEADS, GDN_DH).sum(-1),
         dwx[:, nqkv + GDN_HEADS * GDN_DH:].reshape(d, GDN_HEADS, GDN_DH).sum(-1)], axis=1)
    grads["l1_a_log"] = dalog_x.reshape(GDN_HEADS, GDN_DH).sum(-1)
    grads["l1_dt_bias"] = ddtb_x.reshape(GDN_HEADS, GDN_DH).sum(-1)
    gain = exchange(("l1_w_out", "l1_w_in", "l1_conv"), p["l1_mix_norm"], "l1_mix_grads")
    dx3, grads["l1_mix_norm"] = _norm_bwd(x3, gain, dh1, dx4, "l1_mix_norm_bwd")

    dx2 = ffn_bwd("l0_", ff0, dx3)
    dx1 = xattn_bwd("l0_", xa0, dx2)

    dmerged = _mm(dx1, w["l0_w_out"], tb=True, name="l0_out_dx")
    grads["l0_w_out"] = _mm(merged, dx1, ta=True, out_dtype=BF16, name="l0_out_dw")
    drq, drk, drv, drg, grads["l0_ret_norm"] = _ret_bwd(proj, cos, sin, p["l0_ret_norm"], ret_states, dmerged, "l0_ret_bwd")
    dzg, dg1, grads["l0_s5_b_glu"] = _s5_glu_bwd(dmerged, y, z, b_glu, "l0_s5_glu_bwd")
    grads["l0_s5_w_glu"] = _mm(gy, dzg, ta=True, out_dtype=BF16, name="l0_s5_glu_dw")
    dg2 = _mm(dzg, w["l0_s5_w_glu"], tb=True, name="l0_s5_glu_dx")
    dyraw, du_dir, grads["l0_s5_d"] = _s5_gelu_bwd(dg1, dg2, y, proj, s5_d, "l0_s5_gelu_bwd")
    dst = _mm(dyraw, cbig, tb=True, name="l0_s5_c_dx")
    dcbig = _mm(st, dyraw, ta=True, name="l0_s5_c_dw")
    gsc, da_s5 = _s5_scan_bwd(dst, apow_rev, st, "l0_s5_scan_bwd")
    du = _mm(gsc, bbig, tb=True, res=du_dir, out_dtype=BF16, name="l0_s5_bu_dx")
    dbbig = _mm(u, gsc, ta=True, name="l0_s5_bu_dw")
    dproj = jnp.concatenate([drq, drk, drv, drg, du], axis=1)
    dh0 = _mm(dproj, w["l0_w_in"], tb=True, name="l0_in_dx")
    grads["l0_w_in"] = _mm(h0, dproj, ta=True, out_dtype=BF16, name="l0_in_dw")
    gain = exchange(("l0_w_out", "l0_s5_w_glu", "l0_w_in"), p["l0_mix_norm"], "l0_mix_grads")
    dx0, grads["l0_mix_norm"] = _norm_bwd(x0, gain, dh0, dx1, "l0_mix_norm_bwd")

    dbb_re, dbb_im = (_block_diag_of(t, S5_GROUPS) for t in _s5_split(dbbig))
    dct_re, dct_im = _s5_split(dcbig.T)
    grads["l0_s5_c_re"] = _block_diag_of(dct_re.T, S5_GROUPS)
    grads["l0_s5_c_im"] = -_block_diag_of(dct_im.T, S5_GROUPS)
    da_re, da_im = (t.reshape(S5_GROUPS, S5_STATE) for t in _s5_split(da_s5[0]))
    (grads["l0_s5_lambda_re"], grads["l0_s5_lambda_im"], grads["l0_s5_log_dt"], grads["l0_s5_b_re"],
     grads["l0_s5_b_im"]) = disc_vjp((da_re, da_im, dbb_re, dbb_im))

    def as_2d(t):
        return t.reshape(-1, t.shape[-1])

    def as_row(t):
        return t.reshape(1, -1)

    small_own = _pack_rows([as_row(grads[n]) for n in _REP_SMALL])
    big_own = [as_2d(grads[n].reshape(p[n].shape)) for n in _REP_BIG]
    rep_zones = [_into_slot(t, F32, me, f"place_rep{i}") for i, t in enumerate([small_own] + [t.reshape(-1, LANES) for t in big_own])]
    rep_handle, _ = _push_start([], rep_zones, "rep_grads_start")
    rep_lands = _push_wait(rep_handle, dx0, "rep_grads_wait")
    rep_land = rep_lands[0]

    outs = {}
    kinds = ("grad_", "delta_", "new_m_", "new_v_")
    for names, slots, handle, tag in pending:
        for n, own_slots, land in zip(names, slots, _push_wait(handle, rep_land, tag + "_wait")):
            shape = p[n].shape
            own = lax.dynamic_index_in_dim(own_slots, me, 0, keepdims=False)
            res = _adamw(land, own, *(p[pre + n].reshape(own.shape) for pre in ("", "m_", "v_")), "adamw_" + n)
            for kind, t in zip(kinds, res):
                outs[kind + n] = t.reshape(shape)
    for n, own, land in zip(_REP_BIG, big_own, rep_lands[1:]):
        res = _adamw(land.reshape((N_DEV,) + own.shape), own, *(as_2d(p[pre + n]) for pre in ("", "m_", "v_")), "adamw_" + n)
        for kind, t in zip(kinds, res):
            outs[kind + n] = t.reshape(p[n].shape)
    res = _adamw_rows(rep_land, small_own, *([as_row(p[pre + n]) for n in _REP_SMALL] for pre in ("", "m_", "v_")), "adamw_small")
    for j, kind in enumerate(kinds):
        for i, n in enumerate(_REP_SMALL):
            outs[kind + n] = res[j * len(_REP_SMALL) + i].reshape(p[n].shape)

    return (loss, dx0[None]) + tuple(outs[kind + n] for kind in kinds for n in _WEIGHTS)
```

```python
import functools
import math

import numpy as np
import jax
import jax.numpy as jnp
from jax import lax
from jax.experimental import pallas as pl
from jax.experimental.pallas import tpu as pltpu

F32 = jnp.float32
BF16 = jnp.bfloat16
EPS = 1e-6
N_DEV = 8
LANES = 128
VMEM_LIMIT = 48 * 1024 * 1024
HI = lax.Precision.HIGHEST

RET_HEADS, RET_DH, RET_CHUNK = 4, 128, 128
S5_GROUPS, S5_GROUP, S5_STATE = 32, 16, 64
GDN_HEADS, GDN_DH, GDN_CHUNK, GDN_CONV = 8, 128, 64, 4
XA_HEADS, XA_DH = 4, 256
FFN_CONV = 3
SCAN_ROWS = 256

ADAM_LR, ADAM_B1, ADAM_B2, ADAM_EPS, ADAM_WD, ADAM_STEP = 0.001, 0.9, 0.999, 1e-08, 0.01, 10


def _cp(*sem):
    return pltpu.CompilerParams(dimension_semantics=sem if sem else None, vmem_limit_bytes=VMEM_LIMIT)


def _tile(n, cap):
    if n <= cap:
        return n
    best = None
    for t in range(LANES, cap + 1, LANES):
        if n % t == 0:
            best = t
    assert best is not None, n
    return best


def _dot(a, b, ca=1, cb=0, precision=None):
    return lax.dot_general(a, b, (((ca,), (cb,)), ((), ())), precision=precision, preferred_element_type=F32)


def _mxu(a, b, ca=1, cb=0):
    return _dot(a.astype(BF16), b.astype(BF16), ca, cb)


def _sigmoid(x):
    return 1.0 / (1.0 + jnp.exp(-x))


def _shift_down(x, k):
    row = lax.broadcasted_iota(jnp.int32, x.shape, 0)
    return jnp.where(row >= k, pltpu.roll(x, k, 0), 0.0)


def _shift_up(x, k):
    n = x.shape[0]
    row = lax.broadcasted_iota(jnp.int32, x.shape, 0)
    return jnp.where(row < n - k, pltpu.roll(x, n - k, 0), 0.0)


def _mesh_pos():
    return lax.axis_index("x"), lax.axis_index("y"), lax.axis_index("c")


def _slot(px, py, pc):
    return 4 * px + 2 * py + pc


def _all_peers(x, y, c):
    flips = [(fx, fy, fc) for fx in (0, 1) for fy in (0, 1) for fc in (0, 1)][1:]
    return [(1 - x if fx else x, 1 - y if fy else y, 1 - c if fc else c) for fx, fy, fc in flips]


_HBM = pl.BlockSpec(memory_space=pltpu.HBM)
_SEM = pl.BlockSpec(memory_space=pltpu.SEMAPHORE)
N_PEERS = N_DEV - 1


def _push_copies(srcs, lands, send_sems, recv_sems, start):
    x, y, c = _mesh_pos()
    me = _slot(x, y, c)
    out = []
    for k, to in enumerate(_all_peers(x, y, c)):
        for a in range(len(lands)):
            src = srcs[a].at[_slot(*to)] if a < len(srcs) else lands[a].at[me]
            dst = lands[a].at[me if start else _slot(*to)]
            out.append(pltpu.make_async_remote_copy(
                src_ref=src, dst_ref=dst, send_sem=send_sems.at[a * N_PEERS + k], recv_sem=recv_sems.at[a * N_PEERS + k],
                device_id=to, device_id_type=pl.DeviceIdType.MESH))
    return out


def _into_slot(x, dtype, me, name):
    r, c = x.shape
    cap = max(16, 512 * 1024 // c)
    tr = max(t for t in range(16, min(r, cap) + 1, 16) if r % t == 0) if r % 16 == 0 else r

    def body(me_ref, x_ref, o_ref):
        o_ref[...] = x_ref[...].astype(dtype)

    return pl.pallas_call(
        body, name=name, out_shape=jax.ShapeDtypeStruct((N_DEV, r, c), dtype),
        grid_spec=pltpu.PrefetchScalarGridSpec(
            num_scalar_prefetch=1, grid=(r // tr,),
            in_specs=[pl.BlockSpec((tr, c), lambda i, me_ref: (i, 0))],
            out_specs=pl.BlockSpec((None, tr, c), lambda i, me_ref: (me_ref[0], i, 0))),
        compiler_params=_cp("parallel"),
    )(me.reshape(1).astype(jnp.int32), x)


def _push_start(scatter, gather_lands, name):
    ns, n = len(scatter), len(scatter) + len(gather_lands)
    lands = [lax.empty(a.shape, a.dtype) for a in scatter] + list(gather_lands)

    def body(*refs):
        srcs, zones = refs[:ns], refs[ns:ns + n]
        for cp in _push_copies(srcs, zones, refs[ns + n], refs[ns + n + 1], True):
            cp.start()
        refs[-1][...] = jnp.zeros((8, LANES), F32)

    hbm_in = [pltpu.with_memory_space_constraint(a, pltpu.HBM) for a in list(scatter) + lands]
    res = pl.pallas_call(
        body, name=name,
        out_shape=(pltpu.SemaphoreType.DMA((n * N_PEERS,)), pltpu.SemaphoreType.DMA((n * N_PEERS,)))
        + tuple(pltpu.HBM(a.shape, a.dtype) for a in list(scatter) + lands)
        + (jax.ShapeDtypeStruct((8, LANES), F32),),
        in_specs=[_HBM] * (ns + n),
        out_specs=(_SEM, _SEM) + (_HBM,) * (ns + n) + (pl.BlockSpec(memory_space=pltpu.VMEM),),
        input_output_aliases={i: 2 + i for i in range(ns + n)},
        compiler_params=pltpu.CompilerParams(has_side_effects=pltpu.SideEffectType.DATAFLOW_SIDE_EFFECTING),
    )(*hbm_in)
    return (res[0], res[1], res[2:2 + ns], res[2 + ns:2 + ns + n]), res[-1]


def _push_wait(handle, after, name):
    send_sems, recv_sems, srcs, lands = handle
    ns, n = len(srcs), len(lands)

    def body(*refs):
        for cp in _push_copies(refs[:ns], refs[ns:ns + n], refs[ns + n], refs[ns + n + 1], False):
            cp.wait_send()
            cp.wait_recv()

    res = pl.pallas_call(
        body, name=name,
        out_shape=tuple(pltpu.HBM(a.shape, a.dtype) for a in list(srcs) + list(lands)),
        in_specs=[_HBM] * (ns + n) + [_SEM, _SEM, pl.BlockSpec(memory_space=pl.ANY)],
        out_specs=(_HBM,) * (ns + n),
        input_output_aliases={i: i for i in range(ns + n)},
        compiler_params=pltpu.CompilerParams(has_side_effects=pltpu.SideEffectType.DATAFLOW_SIDE_EFFECTING),
    )(*srcs, *lands, send_sems, recv_sems, after)
    return res[ns:]


def _mm(a, b, *, ta=False, tb=False, out_dtype=F32, res=None, name="mm"):
    m, k = (a.shape[1], a.shape[0]) if ta else a.shape
    n = b.shape[0] if tb else b.shape[1]
    assert k == (b.shape[1] if tb else b.shape[0]), (a.shape, b.shape, ta, tb)
    tm, tn, tk = _tile(m, 1408), _tile(n, 1536), _tile(k, 1408)
    nk = k // tk
    has_res = res is not None

    def body(*refs):
        a_ref, b_ref = refs[:2]
        r_ref = refs[2] if has_res else None
        o_ref = refs[3 if has_res else 2]
        part = _mxu(a_ref[...], b_ref[...], 0 if ta else 1, 1 if tb else 0)

        def finish(r):
            if has_res:
                r = r + r_ref[...].astype(F32)
            o_ref[...] = r.astype(out_dtype)

        if nk == 1:
            finish(part)
            return
        acc = refs[-1]
        kk = pl.program_id(2)

        @pl.when(kk == 0)
        def _():
            acc[...] = part

        @pl.when(kk > 0)
        def _():
            acc[...] += part

        @pl.when(kk == nk - 1)
        def _():
            finish(acc[...])

    a_spec = pl.BlockSpec((tk, tm), lambda i, j, kk: (kk, i)) if ta else pl.BlockSpec((tm, tk), lambda i, j, kk: (i, kk))
    b_spec = pl.BlockSpec((tn, tk), lambda i, j, kk: (j, kk)) if tb else pl.BlockSpec((tk, tn), lambda i, j, kk: (kk, j))
    o_spec = pl.BlockSpec((tm, tn), lambda i, j, kk: (i, j))
    in_specs = [a_spec, b_spec] + ([o_spec] if has_res else [])
    args = (a, b) + ((res,) if has_res else ())
    return pl.pallas_call(
        body, name=name, grid=(m // tm, n // tn, nk), in_specs=in_specs, out_specs=o_spec,
        out_shape=jax.ShapeDtypeStruct((m, n), out_dtype),
        scratch_shapes=[pltpu.VMEM((tm, tn), F32)] if nk > 1 else [],
        compiler_params=_cp("parallel", "parallel", "arbitrary"),
    )(*args)


def _norm_fwd(x, g, name):
    s, d = x.shape
    tr = min(512, s)

    def body(x_ref, g_ref, o_ref):
        xv = x_ref[...]
        r = lax.rsqrt(jnp.mean(xv * xv, axis=-1, keepdims=True) + EPS)
        o_ref[...] = (xv * r * g_ref[...]).astype(BF16)

    row = pl.BlockSpec((tr, d), lambda i: (i, 0))
    return pl.pallas_call(
        body, name=name, grid=(s // tr,), in_specs=[row, pl.BlockSpec((1, d), lambda i: (0, 0))],
        out_specs=row, out_shape=jax.ShapeDtypeStruct((s, d), BF16), compiler_params=_cp("parallel"),
    )(x, g.reshape(1, d))


def _norm_bwd(x, g, dh, dres, name):
    s, d = x.shape
    tr = min(512, s)

    def body(x_ref, g_ref, dh_ref, dres_ref, dx_ref, dg_ref):
        @pl.when(pl.program_id(0) == 0)
        def _():
            dg_ref[...] = jnp.zeros_like(dg_ref)

        xv = x_ref[...]
        r = lax.rsqrt(jnp.mean(xv * xv, axis=-1, keepdims=True) + EPS)
        xn = xv * r
        dhv = dh_ref[...].astype(F32)
        dg_ref[...] += jnp.sum(dhv * xn, axis=0, keepdims=True)
        dhg = dhv * g_ref[...]
        dx_ref[...] = dres_ref[...] + r * (dhg - xn * jnp.mean(dhg * xn, axis=-1, keepdims=True))

    row = pl.BlockSpec((tr, d), lambda i: (i, 0))
    vec = pl.BlockSpec((1, d), lambda i: (0, 0))
    return pl.pallas_call(
        body, name=name, grid=(s // tr,), in_specs=[row, vec, row, row], out_specs=[row, vec],
        out_shape=[jax.ShapeDtypeStruct((s, d), F32), jax.ShapeDtypeStruct((1, d), F32)],
        compiler_params=_cp("arbitrary"),
    )(x, g.reshape(1, d), dh, dres)


def _loss_head(x, g, tgt, name):
    s, d = x.shape
    tr = min(512, s)

    def body(x_ref, g_ref, t_ref, l_ref, dx_ref, dg_ref):
        @pl.when(pl.program_id(0) == 0)
        def _():
            dg_ref[...] = jnp.zeros_like(dg_ref)
            l_ref[...] = jnp.zeros_like(l_ref)

        xv = x_ref[...]
        r = lax.rsqrt(jnp.mean(xv * xv, axis=-1, keepdims=True) + EPS)
        xn = xv * r
        err = xn * g_ref[...] - t_ref[...]
        part = 0.5 * jnp.sum(jnp.mean(err * err, axis=-1, keepdims=True), axis=0, keepdims=True)
        l_ref[...] += jnp.broadcast_to(part, l_ref.shape)
        dy = err * (1.0 / d)
        dg_ref[...] += jnp.sum(dy * xn, axis=0, keepdims=True)
        dyg = dy * g_ref[...]
        dx_ref[...] = r * (dyg - xn * jnp.mean(dyg * xn, axis=-1, keepdims=True))

    row = pl.BlockSpec((tr, d), lambda i: (i, 0))
    vec = pl.BlockSpec((1, d), lambda i: (0, 0))
    return pl.pallas_call(
        body, name=name, grid=(s // tr,), in_specs=[row, vec, row],
        out_specs=[pl.BlockSpec((1, LANES), lambda i: (0, 0)), row, vec],
        out_shape=[jax.ShapeDtypeStruct((1, LANES), F32), jax.ShapeDtypeStruct((s, d), F32),
                   jax.ShapeDtypeStruct((1, d), F32)],
        compiler_params=_cp("arbitrary"),
    )(x, g.reshape(1, d), tgt)


def _sum_slots(landed_slot, own):
    me = _slot(*_mesh_pos())
    mine = own.astype(F32)
    g = jnp.where(me == 0, mine, landed_slot(0).astype(F32))
    for i in range(1, N_DEV):
        g = g + jnp.where(me == i, mine, landed_slot(i).astype(F32))
    return g


def _adam_update(g, w, m, v):
    mm = ADAM_B1 * m + (1.0 - ADAM_B1) * g
    vv = ADAM_B2 * v + (1.0 - ADAM_B2) * (g * g)
    m_hat = mm / (1.0 - ADAM_B1 ** ADAM_STEP)
    v_hat = vv / (1.0 - ADAM_B2 ** ADAM_STEP)
    return g, -ADAM_LR * (m_hat / (jnp.sqrt(v_hat) + ADAM_EPS) + ADAM_WD * w), mm, vv


def _adamw_rows(landed, own, ws, ms, vs, name):
    k = len(ws)
    sizes = [w.shape[1] for w in ws]

    def body(*refs):
        p_ref, o_ref = refs[:2]
        w_refs, m_refs, v_refs = refs[2:2 + k], refs[2 + k:2 + 2 * k], refs[2 + 2 * k:2 + 3 * k]
        outs = refs[2 + 3 * k:]
        for i, n in enumerate(sizes):
            g = _sum_slots(lambda s: p_ref[s, i:i + 1, :n], o_ref[i:i + 1, :n])
            res = _adam_update(g, w_refs[i][...], m_refs[i][...], v_refs[i][...])
            for j in range(4):
                outs[j * k + i][...] = res[j]

    return pl.pallas_call(
        body, name=name, out_shape=[jax.ShapeDtypeStruct((1, n), F32) for _ in range(4) for n in sizes],
    )(landed, own, *ws, *ms, *vs)


def _adamw(landed, own, w, m, v, name):
    r, c = w.shape
    cap = max(8, 256 * 1024 // c)
    tr = max(t for t in range(8, min(r, cap) + 1, 8) if r % t == 0) if r % 8 == 0 else r

    def body(p_ref, o_ref, w_ref, m_ref, v_ref, g_ref, d_ref, nm_ref, nv_ref):
        g = _sum_slots(lambda i: p_ref[i], o_ref[...])
        g_ref[...], d_ref[...], nm_ref[...], nv_ref[...] = _adam_update(g, w_ref[...], m_ref[...], v_ref[...])

    blk = pl.BlockSpec((tr, c), lambda i: (i, 0))
    return pl.pallas_call(
        body, name=name, grid=(r // tr,),
        in_specs=[pl.BlockSpec((N_DEV, tr, c), lambda i: (0, i, 0)), blk, blk, blk, blk],
        out_specs=[blk] * 4, out_shape=[jax.ShapeDtypeStruct((r, c), F32)] * 4,
        compiler_params=_cp("parallel"),
    )(landed, own, w, m, v)


def _conv_fwd(x, w_ref, kw):
    acc = w_ref[kw - 1:kw, :] * x
    for j in range(kw - 1):
        acc = acc + w_ref[j:j + 1, :] * _shift_down(x, kw - 1 - j)
    return acc


def _conv_bwd(x, dy, w_ref, dw_ref, kw):
    dx = w_ref[kw - 1:kw, :] * dy
    dw_ref[kw - 1:kw, :] = jnp.sum(dy * x, axis=0, keepdims=True)
    for j in range(kw - 1):
        dx = dx + w_ref[j:j + 1, :] * _shift_up(dy, kw - 1 - j)
        dw_ref[j:j + 1, :] = jnp.sum(dy * _shift_down(x, kw - 1 - j), axis=0, keepdims=True)
    return dx


def _ffn_act_fwd(pre, cw, name):
    s, f2 = pre.shape
    nt = f2 // 2 // LANES

    def body(pu_ref, pg_ref, wu_ref, wg_ref, o_ref):
        up = _conv_fwd(pu_ref[...].astype(F32), wu_ref, FFN_CONV)
        gate = _conv_fwd(pg_ref[...].astype(F32), wg_ref, FFN_CONV)
        o_ref[...] = (gate * _sigmoid(gate) * up).astype(BF16)

    def col(rows, off):
        return pl.BlockSpec((rows, LANES), lambda j: (0, j + off))

    return pl.pallas_call(
        body, name=name, grid=(nt,),
        in_specs=[col(s, 0), col(s, nt), col(FFN_CONV, 0), col(FFN_CONV, nt)], out_specs=col(s, 0),
        out_shape=jax.ShapeDtypeStruct((s, f2 // 2), BF16), compiler_params=_cp("parallel"),
    )(pre, pre, cw, cw)


def _ffn_act_bwd(pre, cw, dact, name):
    s, f2 = pre.shape
    f = f2 // 2
    nt = f // LANES

    def body(pu_ref, pg_ref, wu_ref, wg_ref, da_ref, dpu_ref, dpg_ref, dwu_ref, dwg_ref):
        pu, pg = pu_ref[...].astype(F32), pg_ref[...].astype(F32)
        up = _conv_fwd(pu, wu_ref, FFN_CONV)
        gate = _conv_fwd(pg, wg_ref, FFN_CONV)
        sg = _sigmoid(gate)
        da = da_ref[...].astype(F32)
        dup = da * gate * sg
        dgate = da * up * (sg * (1.0 + gate * (1.0 - sg)))
        dpu_ref[...] = _conv_bwd(pu, dup, wu_ref, dwu_ref, FFN_CONV).astype(BF16)
        dpg_ref[...] = _conv_bwd(pg, dgate, wg_ref, dwg_ref, FFN_CONV).astype(BF16)

    def col(rows, off):
        return pl.BlockSpec((rows, LANES), lambda j: (0, j + off))

    return pl.pallas_call(
        body, name=name, grid=(nt,),
        in_specs=[col(s, 0), col(s, nt), col(FFN_CONV, 0), col(FFN_CONV, nt), col(s, 0)],
        out_specs=[col(s, 0), col(s, 0), col(FFN_CONV, 0), col(FFN_CONV, 0)],
        out_shape=[jax.ShapeDtypeStruct((s, f), BF16), jax.ShapeDtypeStruct((s, f), BF16),
                   jax.ShapeDtypeStruct((FFN_CONV, f), F32), jax.ShapeDtypeStruct((FFN_CONV, f), F32)],
        compiler_params=_cp("parallel"),
    )(pre, pre, cw, cw, dact)


def _xa_probs(qh, kh):
    sc = _mxu(qh, kh, 1, 1) * (XA_DH ** -0.5)
    e = jnp.exp(sc - jnp.max(sc, axis=-1, keepdims=True))
    return e / jnp.sum(e, axis=-1, keepdims=True)


def _xattn_fwd(q, kv, name):
    s, d = q.shape
    m = kv.shape[0]
    tr = min(512, s)

    def body(q_ref, kv_ref, o_ref):
        for h in range(XA_HEADS):
            lo, hi = h * XA_DH, (h + 1) * XA_DH
            p = _xa_probs(q_ref[:, lo:hi], kv_ref[:, lo:hi])
            o_ref[:, lo:hi] = _mxu(p, kv_ref[:, d + lo:d + hi]).astype(BF16)

    row = pl.BlockSpec((tr, d), lambda i: (i, 0))
    return pl.pallas_call(
        body, name=name, grid=(s // tr,), in_specs=[row, pl.BlockSpec((m, 2 * d), lambda i: (0, 0))],
        out_specs=row, out_shape=jax.ShapeDtypeStruct((s, d), BF16), compiler_params=_cp("parallel"),
    )(q, kv)


def _xattn_bwd(q, kv, do, name):
    s, d = q.shape
    m = kv.shape[0]
    tr = min(512, s)

    def body(q_ref, kv_ref, do_ref, dq_ref, dkv_ref):
        @pl.when(pl.program_id(0) == 0)
        def _():
            dkv_ref[...] = jnp.zeros_like(dkv_ref)

        for h in range(XA_HEADS):
            lo, hi = h * XA_DH, (h + 1) * XA_DH
            qh, kh, vh = q_ref[:, lo:hi], kv_ref[:, lo:hi], kv_ref[:, d + lo:d + hi]
            doh = do_ref[:, lo:hi]
            p = _xa_probs(qh, kh)
            dp = _mxu(doh, vh, 1, 1)
            ds = p * (dp - jnp.sum(p * dp, axis=-1, keepdims=True)) * (XA_DH ** -0.5)
            dq_ref[:, lo:hi] = _mxu(ds, kh).astype(BF16)
            dkv_ref[:, lo:hi] += _mxu(ds, qh, 0, 0)
            dkv_ref[:, d + lo:d + hi] += _mxu(p, doh, 0, 0)

    row = pl.BlockSpec((tr, d), lambda i: (i, 0))
    full = pl.BlockSpec((m, 2 * d), lambda i: (0, 0))
    return pl.pallas_call(
        body, name=name, grid=(s // tr,), in_specs=[row, full, row], out_specs=[row, full],
        out_shape=[jax.ShapeDtypeStruct((s, d), BF16), jax.ShapeDtypeStruct((m, 2 * d), F32)],
        compiler_params=_cp("arbitrary"),
    )(q, kv, do)


def _ret_tables():
    c = RET_CHUNK
    lg = np.log1p(-np.exp2(-5.0 - np.arange(RET_HEADS, dtype=np.float32))).astype(np.float32)
    idx = np.arange(c, dtype=np.float32)
    diff = idx[:, None] - idx[None, :]
    intra = np.where(diff >= 0, np.exp(lg[:, None, None] * np.where(diff >= 0, diff, 0.0)), 0.0)
    rk = np.broadcast_to(np.exp(lg[:, None] * (c - 1 - idx))[:, :, None], (RET_HEADS, c, LANES))
    rq = np.broadcast_to(np.exp(lg[:, None] * (idx + 1))[:, :, None], (RET_HEADS, c, LANES))
    return jnp.asarray(np.stack([intra, rk, rq], axis=1).astype(np.float32))


def _rope_tables(s):
    half = RET_DH // 2
    inv = jnp.exp(-math.log(10000.0) * jnp.arange(half, dtype=F32) / half)
    ang = jnp.arange(s, dtype=F32)[:, None] * inv[None, :]
    cos, sin = jnp.cos(ang), jnp.sin(ang)
    return jnp.concatenate([cos, cos], axis=1), jnp.concatenate([-sin, sin], axis=1)


def _ret_specs(n_of):
    c, w = RET_CHUNK, RET_HEADS * RET_DH

    def part(off):
        return pl.BlockSpec((c, w), lambda n: (n_of(n), off))

    pos = pl.BlockSpec((c, RET_DH), lambda n: (n_of(n), 0))
    gain = pl.BlockSpec((1, w), lambda n: (0, 0))
    tab = pl.BlockSpec((RET_HEADS, 3, c, LANES), lambda n: (0, 0, 0, 0))
    st = pl.BlockSpec((RET_HEADS, None, RET_DH, RET_DH), lambda n: (0, n_of(n), 0, 0))
    return part, pos, gain, tab, st


def _rheads(x):
    return jnp.stack([x[:, h * RET_DH:(h + 1) * RET_DH] for h in range(RET_HEADS)], axis=0)


def _runheads(x):
    return jnp.concatenate([x[h] for h in range(RET_HEADS)], axis=1)


def _rope(x, cos, sin):
    return x * cos + pltpu.roll(x, RET_DH // 2, 2) * sin


def _ret_chunk(q_ref, k_ref, v_ref, cos_ref, sin_ref, tab_ref, prev):
    cos, sin = cos_ref[...], sin_ref[...]
    q = _rope(_rheads(q_ref[...]), cos, sin)
    k = _rope(_rheads(k_ref[...]), cos, sin) * (RET_DH ** -0.5)
    v = _rheads(v_ref[...])
    scores = _bmxu(q, k, 2, 2) * tab_ref[:, 0]
    qdec = q * tab_ref[:, 2]
    kdec = k * tab_ref[:, 1]
    o = _bmxu(scores, v) + _bmxu(qdec, prev)
    return q, k, v, scores, qdec, kdec, o


def _ret_fwd(proj, cos, sin, gain, name):
    s = proj.shape[0]
    c = RET_CHUNK
    nc = s // c
    part, pos, gvec, tab, st = _ret_specs(lambda n: n)

    def body(q_ref, k_ref, v_ref, g_ref, cos_ref, sin_ref, rn_ref, tab_ref, o_ref, st_ref, state):
        @pl.when(pl.program_id(0) == 0)
        def _():
            state[...] = jnp.zeros_like(state)

        prev = state[...]
        st_ref[...] = prev
        _, _, v, _, _, kdec, o = _ret_chunk(q_ref, k_ref, v_ref, cos_ref, sin_ref, tab_ref, prev)
        state[...] = prev * tab_ref[:, 2, c - 1:c, :] + _bmxu(kdec, v, 1, 1)
        r = lax.rsqrt(jnp.mean(o * o, axis=-1, keepdims=True) + EPS)
        gate = g_ref[...]
        o_ref[...] = (_runheads(o * r) * rn_ref[...] * (gate * _sigmoid(gate))).astype(BF16)

    return pl.pallas_call(
        body, name=name, grid=(nc,),
        in_specs=[part(0), part(1), part(2), part(3), pos, pos, gvec, tab],
        out_specs=[part(0), st],
        out_shape=[jax.ShapeDtypeStruct((s, RET_HEADS * RET_DH), BF16),
                   jax.ShapeDtypeStruct((RET_HEADS, nc, RET_DH, RET_DH), F32)],
        scratch_shapes=[pltpu.VMEM((RET_HEADS, RET_DH, RET_DH), F32)],
        compiler_params=_cp("arbitrary"),
    )(proj, proj, proj, proj, cos, sin, gain.reshape(1, -1), _ret_tables())


def _ret_bwd(proj, cos, sin, gain, states, dmerged, name):
    s = proj.shape[0]
    c = RET_CHUNK
    nc = s // c
    part, pos, gvec, tab, st = _ret_specs(lambda n: nc - 1 - n)

    def body(q_ref, k_ref, v_ref, g_ref, cos_ref, sin_ref, rn_ref, tab_ref, st_ref, do_ref,
             dq_ref, dk_ref, dv_ref, dg_ref, drn_ref, carry):
        @pl.when(pl.program_id(0) == 0)
        def _():
            carry[...] = jnp.zeros_like(carry)
            drn_ref[...] = jnp.zeros_like(drn_ref)

        prev = st_ref[...]
        q, k, v, scores, qdec, kdec, o = _ret_chunk(q_ref, k_ref, v_ref, cos_ref, sin_ref, tab_ref, prev)
        r = lax.rsqrt(jnp.mean(o * o, axis=-1, keepdims=True) + EPS)
        on = o * r
        on2 = _runheads(on)
        gate = g_ref[...]
        sg = _sigmoid(gate)
        sil = gate * sg
        dout = do_ref[...]
        rn = rn_ref[...]
        dg_ref[...] = (dout * on2 * rn * (sg * (1.0 + gate * (1.0 - sg)))).astype(BF16)
        drn_ref[...] += jnp.sum(dout * on2 * sil, axis=0, keepdims=True)
        don = _rheads(dout * rn * sil)
        do = r * (don - on * jnp.mean(don * on, axis=-1, keepdims=True))
        dc = carry[...]
        dsc = _bmxu(do, v, 2, 2) * tab_ref[:, 0]
        dq = _bmxu(dsc, k) + _bmxu(do, prev, 2, 2) * tab_ref[:, 2]
        dk = _bmxu(dsc, q, 1, 1) + _bmxu(v, dc, 2, 2) * tab_ref[:, 1]
        dv = _bmxu(scores, do, 1, 1) + _bmxu(kdec, dc)
        carry[...] = _bmxu(qdec, do, 1, 1) + dc * tab_ref[:, 2, c - 1:c, :]
        cos, sin = cos_ref[...], sin_ref[...]
        dk = dk * (RET_DH ** -0.5)
        dq_ref[...] = _runheads(dq * cos + pltpu.roll(dq * sin, RET_DH // 2, 2)).astype(BF16)
        dk_ref[...] = _runheads(dk * cos + pltpu.roll(dk * sin, RET_DH // 2, 2)).astype(BF16)
        dv_ref[...] = _runheads(dv).astype(BF16)

    width = RET_HEADS * RET_DH
    return pl.pallas_call(
        body, name=name, grid=(nc,),
        in_specs=[part(0), part(1), part(2), part(3), pos, pos, gvec, tab, st, part(0)],
        out_specs=[part(0)] * 4 + [gvec],
        out_shape=[jax.ShapeDtypeStruct((s, width), BF16)] * 4 + [jax.ShapeDtypeStruct((1, width), F32)],
        scratch_shapes=[pltpu.VMEM((RET_HEADS, RET_DH, RET_DH), F32)],
        compiler_params=_cp("arbitrary"),
    )(proj, proj, proj, proj, cos, sin, gain.reshape(1, -1), _ret_tables(), states, dmerged)


S5_TILE = 512


def _cmul_add(xr, xi, ar, ai, yr, yi):
    return xr + ar * yr - ai * yi, xi + ar * yi + ai * yr


def _s5_pow_tables(a_il, name):
    r = SCAN_ROWS
    t = S5_TILE
    w2 = a_il.shape[1]

    def body(a_ref, up_ref, dn_ref):
        for j in range(w2 // (2 * t)):
            re, im = pl.ds(2 * t * j, t), pl.ds(2 * t * j + t, t)
            up_ref[0:1, re] = a_ref[:, re]
            up_ref[0:1, im] = a_ref[:, im]
            dn_ref[r - 1:r, re] = a_ref[:, re]
            dn_ref[r - 1:r, im] = -a_ref[:, im]
            n = 1
            while n < r:
                lr, li = up_ref[n - 1:n, re], up_ref[n - 1:n, im]
                xr, xi = up_ref[0:n, re], up_ref[0:n, im]
                up_ref[n:2 * n, re] = xr * lr - xi * li
                up_ref[n:2 * n, im] = xr * li + xi * lr
                yr, yi = dn_ref[r - n:r, re], dn_ref[r - n:r, im]
                dn_ref[r - 2 * n:r - n, re] = yr * lr + yi * li
                dn_ref[r - 2 * n:r - n, im] = yi * lr - yr * li
                n *= 2

    return pl.pallas_call(
        body, name=name, out_shape=[jax.ShapeDtypeStruct((r, w2), F32)] * 2, compiler_params=_cp(),
    )(a_il)


def _s5_scan_fwd(bu, apow, name):
    s, w2 = bu.shape
    r = SCAN_ROWS
    t = S5_TILE
    steps = r.bit_length() - 1

    def body(b_ref, p_ref, o_ref, cr, ci):
        @pl.when(pl.program_id(1) == 0)
        def _():
            cr[...] = jnp.zeros_like(cr)
            ci[...] = jnp.zeros_like(ci)

        xr, xi = b_ref[:, :t], b_ref[:, t:]
        for k in range(steps):
            sh = 1 << k
            xr, xi = _cmul_add(xr, xi, p_ref[sh - 1:sh, :t], p_ref[sh - 1:sh, t:],
                               _shift_down(xr, sh), _shift_down(xi, sh))
        xr, xi = _cmul_add(xr, xi, p_ref[:, :t], p_ref[:, t:], cr[...], ci[...])
        o_ref[:, :t] = xr
        o_ref[:, t:] = xi
        cr[...] = xr[r - 1:r, :]
        ci[...] = xi[r - 1:r, :]

    blk = pl.BlockSpec((r, 2 * t), lambda j, i: (i, j))
    return pl.pallas_call(
        body, name=name, grid=(w2 // (2 * t), s // r),
        in_specs=[blk, pl.BlockSpec((r, 2 * t), lambda j, i: (0, j))], out_specs=blk,
        out_shape=jax.ShapeDtypeStruct((s, w2), F32),
        scratch_shapes=[pltpu.VMEM((1, t), F32), pltpu.VMEM((1, t), F32)],
        compiler_params=_cp("parallel", "arbitrary"),
    )(bu, apow)


def _s5_scan_bwd(dst, apow_rev, st, name):
    s, w2 = dst.shape
    r = SCAN_ROWS
    t = S5_TILE
    nb = s // r
    steps = r.bit_length() - 1

    def body(d_ref, p_ref, s_ref, sp_ref, g_ref, da_ref, cr, ci):
        i = pl.program_id(1)

        @pl.when(i == 0)
        def _():
            cr[...] = jnp.zeros_like(cr)
            ci[...] = jnp.zeros_like(ci)
            da_ref[...] = jnp.zeros_like(da_ref)

        xr, xi = d_ref[:, :t], d_ref[:, t:]
        for k in range(steps):
            sh = 1 << k
            xr, xi = _cmul_add(xr, xi, p_ref[r - sh:r - sh + 1, :t], p_ref[r - sh:r - sh + 1, t:],
                               _shift_up(xr, sh), _shift_up(xi, sh))
        xr, xi = _cmul_add(xr, xi, p_ref[:, :t], p_ref[:, t:], cr[...], ci[...])
        g_ref[:, :t] = xr.astype(BF16)
        g_ref[:, t:] = xi.astype(BF16)
        cr[...] = xr[0:1, :]
        ci[...] = xi[0:1, :]
        first = i == nb - 1
        row = lax.broadcasted_iota(jnp.int32, (r, t), 0)
        last_r = jnp.where(first, 0.0, sp_ref[7:8, :t])
        last_i = jnp.where(first, 0.0, sp_ref[7:8, t:])
        pr = jnp.where(row == 0, last_r, pltpu.roll(s_ref[:, :t], 1, 0))
        pi = jnp.where(row == 0, last_i, pltpu.roll(s_ref[:, t:], 1, 0))
        da_ref[:, :t] += jnp.sum(xr * pr + xi * pi, axis=0, keepdims=True)
        da_ref[:, t:] += jnp.sum(xi * pr - xr * pi, axis=0, keepdims=True)

    blk = pl.BlockSpec((r, 2 * t), lambda j, i: (nb - 1 - i, j))
    halo = pl.BlockSpec((8, 2 * t), lambda j, i: (jnp.maximum((nb - 1 - i) * (r // 8) - 1, 0), j))
    vec = pl.BlockSpec((1, 2 * t), lambda j, i: (0, j))
    return pl.pallas_call(
        body, name=name, grid=(w2 // (2 * t), nb),
        in_specs=[blk, pl.BlockSpec((r, 2 * t), lambda j, i: (0, j)), blk, halo], out_specs=[blk, vec],
        out_shape=[jax.ShapeDtypeStruct((s, w2), BF16), jax.ShapeDtypeStruct((1, w2), F32)],
        scratch_shapes=[pltpu.VMEM((1, t), F32), pltpu.VMEM((1, t), F32)],
        compiler_params=_cp("parallel", "arbitrary"),
    )(dst, apow_rev, st, st)


_GELU_C = math.sqrt(2.0 / math.pi)
_GELU_A = 0.044715


def _gelu(y):
    return 0.5 * y * (1.0 + jnp.tanh(_GELU_C * (y + _GELU_A * y * y * y)))


def _gelu_grad(y):
    th = jnp.tanh(_GELU_C * (y + _GELU_A * y * y * y))
    return 0.5 * (1.0 + th) + 0.5 * y * (1.0 - th * th) * _GELU_C * (1.0 + 3.0 * _GELU_A * y * y)


def _rows_shift(x, k, axis, up):
    n = x.shape[axis]
    idx = lax.broadcasted_iota(jnp.int32, x.shape, axis)
    if up:
        return jnp.where(idx < n - k, pltpu.roll(x, n - k, axis), 0.0)
    return jnp.where(idx >= k, pltpu.roll(x, k, axis), 0.0)


def _scan_block(xr, xi, pr, pi, cr, ci, rev):
    r, w = xr.shape
    nt = r // 8
    x3r, x3i = xr.reshape(nt, 8, w), xi.reshape(nt, 8, w)
    p3r, p3i = pr.reshape(nt, 8, w), pi.reshape(nt, 8, w)

    def power(rows):
        t = r - rows if rev else rows - 1
        return pr[t:t + 1, :], pi[t:t + 1, :]

    for sh in (1, 2, 4):
        ar, ai = power(sh)
        x3r, x3i = _cmul_add(x3r, x3i, ar, ai, _rows_shift(x3r, sh, 1, rev), _rows_shift(x3i, sh, 1, rev))
    edge = 0 if rev else 7
    lr, li = x3r[:, edge, :], x3i[:, edge, :]
    sh = 1
    while sh < nt:
        ar, ai = power(8 * sh)
        lr, li = _cmul_add(lr, li, ar, ai, _rows_shift(lr, sh, 0, rev), _rows_shift(li, sh, 0, rev))
        sh *= 2
    tr_, ti_ = p3r[:, edge, :], p3i[:, edge, :]
    first = lax.broadcasted_iota(jnp.int32, (nt, w), 0) == (nt - 1 if rev else 0)
    wr = jnp.where(first, 1.0, _rows_shift(tr_, 1, 0, rev))
    wi = jnp.where(first, 0.0, _rows_shift(ti_, 1, 0, rev))
    er, ei = _cmul_add(_rows_shift(lr, 1, 0, rev), _rows_shift(li, 1, 0, rev), wr, wi, cr, ci)
    a8r, a8i = (p3r[nt - 1], p3i[nt - 1]) if rev else (p3r[0], p3i[0])
    x3r, x3i = _cmul_add(x3r, x3i, a8r[None], a8i[None], er[:, None, :], ei[:, None, :])
    outr, outi = x3r.reshape(r, w), x3i.reshape(r, w)
    last = 0 if rev else r - 1
    return outr, outi, outr[last:last + 1, :], outi[last:last + 1, :]


def _s5_tile_specs(n_of, r):
    t = S5_TILE
    ucol = 4 * RET_HEADS * RET_DH // LANES
    u = pl.BlockSpec((r, LANES), lambda j, i: (n_of(i), ucol + j))
    col = pl.BlockSpec((r, LANES), lambda j, i: (n_of(i), j))
    state = pl.BlockSpec((r, 2 * t), lambda j, i: (n_of(i), j))
    table = pl.BlockSpec((r, 2 * t), lambda j, i: (0, j))
    bbt = pl.BlockSpec((None, LANES, 2 * t), lambda j, i: (j, 0, 0))
    cct = pl.BlockSpec((None, 2 * t, LANES), lambda j, i: (j, 0, 0))
    vec = pl.BlockSpec((1, LANES), lambda j, i: (0, j))
    return u, col, state, table, bbt, cct, vec


def _s5_fwd(proj, bbt, cct, apow, dvec, name):
    s = proj.shape[0]
    r, t = SCAN_ROWS, S5_TILE
    w = S5_GROUPS * S5_GROUP
    u_s, col, state, table, bb_s, cc_s, vec = _s5_tile_specs(lambda i: i, r)

    def body(u_ref, bb_ref, cc_ref, p_ref, d_ref, st_ref, y_ref, g_ref, cr, ci):
        @pl.when(pl.program_id(1) == 0)
        def _():
            cr[...] = jnp.zeros_like(cr)
            ci[...] = jnp.zeros_like(ci)

        u = u_ref[...]
        bu = _mxu(u, bb_ref[...])
        xr, xi, cr[...], ci[...] = _scan_block(bu[:, :t], bu[:, t:], p_ref[:, :t], p_ref[:, t:], cr[...], ci[...], False)
        st_ref[:, :t] = xr
        st_ref[:, t:] = xi
        y = _mxu(xr, cc_ref[:t, :]) + _mxu(xi, cc_ref[t:, :]) + d_ref[...] * u
        y_ref[...] = y
        g_ref[...] = _gelu(y).astype(BF16)

    return pl.pallas_call(
        body, name=name, grid=(2 * S5_GROUPS * S5_STATE // (2 * t), s // r),
        in_specs=[u_s, bb_s, cc_s, table, vec], out_specs=[state, col, col],
        out_shape=[jax.ShapeDtypeStruct((s, 2 * S5_GROUPS * S5_STATE), F32), jax.ShapeDtypeStruct((s, w), F32),
                   jax.ShapeDtypeStruct((s, w), BF16)],
        scratch_shapes=[pltpu.VMEM((1, t), F32), pltpu.VMEM((1, t), F32)],
        compiler_params=_cp("parallel", "arbitrary"),
    )(proj, bbt, cct, apow, dvec)


def _s5_bwd(dg1, dg2, y, proj, st, bbt, cct, apow_rev, dvec, name):
    s = proj.shape[0]
    r, t = SCAN_ROWS, S5_TILE
    nb = s // r
    w = S5_GROUPS * S5_GROUP
    u_s, col, state, table, bb_s, cc_s, vec = _s5_tile_specs(lambda i: nb - 1 - i, r)
    halo = pl.BlockSpec((8, 2 * t), lambda j, i: (jnp.maximum((nb - 1 - i) * (r // 8) - 1, 0), j))
    acc = pl.BlockSpec((1, 2 * t), lambda j, i: (0, j))

    def body(a_ref, b_ref, y_ref, u_ref, s_ref, sp_ref, bb_ref, cc_ref, p_ref, d_ref,
             du_ref, da_ref, dbb_ref, dcc_ref, dd_ref, cr, ci):
        i = pl.program_id(1)

        @pl.when(i == 0)
        def _():
            cr[...] = jnp.zeros_like(cr)
            ci[...] = jnp.zeros_like(ci)
            da_ref[...] = jnp.zeros_like(da_ref)
            dbb_ref[...] = jnp.zeros_like(dbb_ref)
            dcc_ref[...] = jnp.zeros_like(dcc_ref)
            dd_ref[...] = jnp.zeros_like(dd_ref)

        u = u_ref[...]
        dy = (a_ref[...] + b_ref[...]) * _gelu_grad(y_ref[...])
        dd_ref[...] += jnp.sum(dy * u, axis=0, keepdims=True)
        sr, si = s_ref[:, :t], s_ref[:, t:]
        dcc_ref[:t, :] += _mxu(sr, dy, 0, 0)
        dcc_ref[t:, :] += _mxu(si, dy, 0, 0)
        xr, xi, cr[...], ci[...] = _scan_block(_mxu(dy, cc_ref[:t, :], 1, 1), _mxu(dy, cc_ref[t:, :], 1, 1),
                                               p_ref[:, :t], p_ref[:, t:], cr[...], ci[...], True)
        du_ref[...] = (dy * d_ref[...] + _mxu(xr, bb_ref[:, :t], 1, 1) + _mxu(xi, bb_ref[:, t:], 1, 1)).astype(BF16)
        dbb_ref[:, :t] += _mxu(u, xr, 0, 0)
        dbb_ref[:, t:] += _mxu(u, xi, 0, 0)
        first = i == nb - 1
        row = lax.broadcasted_iota(jnp.int32, (r, t), 0)
        pr = jnp.where(row == 0, jnp.where(first, 0.0, sp_ref[7:8, :t]), pltpu.roll(sr, 1, 0))
        pi = jnp.where(row == 0, jnp.where(first, 0.0, sp_ref[7:8, t:]), pltpu.roll(si, 1, 0))
        da_ref[:, :t] += jnp.sum(xr * pr + xi * pi, axis=0, keepdims=True)
        da_ref[:, t:] += jnp.sum(xi * pr - xr * pi, axis=0, keepdims=True)

    return pl.pallas_call(
        body, name=name, grid=(2 * S5_GROUPS * S5_STATE // (2 * t), nb),
        in_specs=[col, col, col, u_s, state, halo, bb_s, cc_s, table, vec],
        out_specs=[col, acc, bb_s, cc_s, vec],
        out_shape=[jax.ShapeDtypeStruct((s, w), BF16), jax.ShapeDtypeStruct((1, 2 * S5_GROUPS * S5_STATE), F32),
                   jax.ShapeDtypeStruct(bbt.shape, F32), jax.ShapeDtypeStruct(cct.shape, F32),
                   jax.ShapeDtypeStruct((1, w), F32)],
        scratch_shapes=[pltpu.VMEM((1, t), F32), pltpu.VMEM((1, t), F32)],
        compiler_params=_cp("parallel", "arbitrary"),
    )(dg1, dg2, y, proj, st, st, bbt, cct, apow_rev, dvec)


def _s5_tile_b(b_re, b_im):
    nt = S5_GROUPS * S5_STATE // S5_TILE
    gpt = S5_GROUPS // nt
    eye = jnp.eye(gpt, dtype=F32)

    def tile(b):
        t5 = jnp.einsum("jghp,gk->jghkp", b.reshape(nt, gpt, S5_GROUP, S5_STATE), eye)
        return t5.reshape(nt, gpt * S5_GROUP, S5_TILE)

    return jnp.concatenate([tile(b_re), tile(b_im)], axis=2)


def _s5_untile_b(d):
    nt = S5_GROUPS * S5_STATE // S5_TILE
    gpt = S5_GROUPS // nt
    eye = jnp.eye(gpt, dtype=F32)

    def untile(x):
        x5 = x.reshape(nt, gpt, S5_GROUP, gpt, S5_STATE)
        return jnp.einsum("jghkp,gk->jghp", x5, eye).reshape(S5_GROUPS, S5_GROUP, S5_STATE)

    return untile(d[:, :, :S5_TILE]), untile(d[:, :, S5_TILE:])


def _s5_tile_c(c_re, c_im):
    nt = S5_GROUPS * S5_STATE // S5_TILE
    gpt = S5_GROUPS // nt
    eye = jnp.eye(gpt, dtype=F32)

    def tile(c):
        t5 = jnp.einsum("jgph,gk->jkpgh", c.reshape(nt, gpt, S5_STATE, S5_GROUP), eye)
        return t5.reshape(nt, S5_TILE, gpt * S5_GROUP)

    return jnp.concatenate([tile(c_re), -tile(c_im)], axis=1)


def _s5_untile_c(d):
    nt = S5_GROUPS * S5_STATE // S5_TILE
    gpt = S5_GROUPS // nt
    eye = jnp.eye(gpt, dtype=F32)

    def untile(x):
        x5 = x.reshape(nt, gpt, S5_STATE, gpt, S5_GROUP)
        return jnp.einsum("jkpgh,gk->jgph", x5, eye).reshape(S5_GROUPS, S5_STATE, S5_GROUP)

    return untile(d[:, :S5_TILE, :]), -untile(d[:, S5_TILE:, :])


def _row_call(body, name, s, ins, outs, acc=False):
    tr = min(512, s)

    def spec(width, cb, rows):
        if rows == 1:
            return pl.BlockSpec((1, width), lambda i: (0, cb))
        return pl.BlockSpec((tr, width), lambda i: (i, cb))

    in_specs = [spec(w, cb, a.shape[0]) for a, w, cb in ins]
    out_specs = [spec(w, cb, sd.shape[0]) for sd, w, cb in outs]
    return pl.pallas_call(
        body, name=name, grid=(s // tr,), in_specs=in_specs, out_specs=out_specs,
        out_shape=[sd for sd, _, _ in outs],
        compiler_params=_cp("arbitrary" if acc else "parallel"),
    )(*[a for a, _, _ in ins])


def _sds(shape, dtype):
    return jax.ShapeDtypeStruct(shape, dtype)


def _s5_gelu_fwd(yraw, proj, dvec, name):
    s, w = yraw.shape

    def body(y_ref, u_ref, d_ref, yo_ref, g_ref):
        y = y_ref[...] + d_ref[...] * u_ref[...]
        yo_ref[...] = y
        g_ref[...] = _gelu(y).astype(BF16)

    return _row_call(body, name, s, [(yraw, w, 0), (proj, w, 4), (dvec, w, 0)],
                     [(_sds((s, w), F32), w, 0), (_sds((s, w), BF16), w, 0)])


def _s5_glu_fwd(y, z, b, name):
    s, w = y.shape

    def body(y_ref, z_ref, b_ref, o_ref):
        o_ref[...] = (_gelu(y_ref[...]) * _sigmoid(z_ref[...] + b_ref[...])).astype(BF16)

    return _row_call(body, name, s, [(y, w, 0), (z, w, 0), (b, w, 0)], [(_sds((s, w), BF16), w, 0)])[0]


def _s5_glu_bwd(dmerged, y, z, b, name):
    s, w = y.shape

    def body(do_ref, y_ref, z_ref, b_ref, dz_ref, dg_ref, db_ref):
        @pl.when(pl.program_id(0) == 0)
        def _():
            db_ref[...] = jnp.zeros_like(db_ref)

        g = _gelu(y_ref[...])
        sg = _sigmoid(z_ref[...] + b_ref[...])
        dout = do_ref[...]
        dz = dout * g * sg * (1.0 - sg)
        dz_ref[...] = dz.astype(BF16)
        dg_ref[...] = dout * sg
        db_ref[...] += jnp.sum(dz, axis=0, keepdims=True)

    return _row_call(body, name, s, [(dmerged, w, 1), (y, w, 0), (z, w, 0), (b, w, 0)],
                     [(_sds((s, w), BF16), w, 0), (_sds((s, w), F32), w, 0), (_sds((1, w), F32), w, 0)], acc=True)


def _s5_gelu_bwd(dg1, dg2, y, proj, dvec, name):
    s, w = y.shape

    def body(a_ref, b_ref, y_ref, u_ref, d_ref, dy_ref, du_ref, dd_ref):
        @pl.when(pl.program_id(0) == 0)
        def _():
            dd_ref[...] = jnp.zeros_like(dd_ref)

        dy = (a_ref[...] + b_ref[...]) * _gelu_grad(y_ref[...])
        dy_ref[...] = dy.astype(BF16)
        du_ref[...] = dy * d_ref[...]
        dd_ref[...] += jnp.sum(dy * u_ref[...], axis=0, keepdims=True)

    return _row_call(body, name, s, [(dg1, w, 0), (dg2, w, 0), (y, w, 0), (proj, w, 4), (dvec, w, 0)],
                     [(_sds((s, w), BF16), w, 0), (_sds((s, w), F32), w, 0), (_sds((1, w), F32), w, 0)], acc=True)


def _gdn_conv_fwd(projx, cw, name):
    s = projx.shape[0]
    nh = GDN_HEADS

    def body(x_ref, w_ref, o_ref):
        j = pl.program_id(0)
        cv = _conv_fwd(x_ref[...], w_ref, GDN_CONV)
        y = cv * _sigmoid(cv)
        nrm = y * lax.rsqrt(jnp.sum(y * y, axis=-1, keepdims=True) + EPS)
        o_ref[...] = jnp.where(j < nh, nrm * (GDN_DH ** -0.5), jnp.where(j < 2 * nh, nrm, y))

    return pl.pallas_call(
        body, name=name, grid=(3 * nh,),
        in_specs=[pl.BlockSpec((s, GDN_DH), lambda j: (0, j)), pl.BlockSpec((GDN_CONV, GDN_DH), lambda j: (0, j))],
        out_specs=pl.BlockSpec((s, GDN_DH), lambda j: (0, j)),
        out_shape=jax.ShapeDtypeStruct((s, 3 * nh * GDN_DH), F32), compiler_params=_cp("parallel"),
    )(projx, cw)


def _gdn_conv_bwd(projx, cw, dqkv, name):
    s = projx.shape[0]
    nh = GDN_HEADS

    def body(x_ref, w_ref, d_ref, dx_ref, dw_ref):
        j = pl.program_id(0)
        x = x_ref[...]
        cv = _conv_fwd(x, w_ref, GDN_CONV)
        sg = _sigmoid(cv)
        y = cv * sg
        rinv = lax.rsqrt(jnp.sum(y * y, axis=-1, keepdims=True) + EPS)
        nrm = y * rinv
        dn = d_ref[...]
        dns = jnp.where(j < nh, dn * (GDN_DH ** -0.5), dn)
        dyn = rinv * (dns - nrm * jnp.sum(dns * nrm, axis=-1, keepdims=True))
        dy = jnp.where(j < 2 * nh, dyn, dn)
        dc = dy * (sg * (1.0 + cv * (1.0 - sg)))
        dx_ref[...] = _conv_bwd(x, dc, w_ref, dw_ref, GDN_CONV).astype(BF16)

    col = pl.BlockSpec((s, GDN_DH), lambda j: (0, j))
    wcol = pl.BlockSpec((GDN_CONV, GDN_DH), lambda j: (0, j))
    return pl.pallas_call(
        body, name=name, grid=(3 * nh,), in_specs=[col, wcol, col], out_specs=[col, wcol],
        out_shape=[jax.ShapeDtypeStruct((s, 3 * nh * GDN_DH), BF16), jax.ShapeDtypeStruct((GDN_CONV, 3 * nh * GDN_DH), F32)],
        compiler_params=_cp("parallel"),
    )(projx, cw, dqkv)


def _softplus(x):
    return jnp.maximum(x, 0.0) + jnp.log1p(jnp.exp(-jnp.abs(x)))


def _gdn_gates_fwd(projx, alog, dtb, name):
    s = projx.shape[0]
    w = GDN_HEADS * GDN_DH

    def body(b_ref, a_ref, al_ref, dt_ref, bo_ref, go_ref):
        bo_ref[...] = _sigmoid(b_ref[...])
        go_ref[...] = -jnp.exp(al_ref[...]) * _softplus(a_ref[...] + dt_ref[...])

    return _row_call(body, name, s, [(projx, w, 4), (projx, w, 5), (alog, w, 0), (dtb, w, 0)],
                     [(_sds((s, w), F32), w, 0), (_sds((s, w), F32), w, 0)])


def _gdn_gates_bwd(projx, alog, dtb, dbeta, dg, name):
    s = projx.shape[0]
    w = GDN_HEADS * GDN_DH

    def body(b_ref, a_ref, al_ref, dt_ref, dbe_ref, dg_ref, db_ref, da_ref, dal_ref, ddt_ref):
        @pl.when(pl.program_id(0) == 0)
        def _():
            dal_ref[...] = jnp.zeros_like(dal_ref)
            ddt_ref[...] = jnp.zeros_like(ddt_ref)

        for h in range(GDN_HEADS):
            lo, hi = h * GDN_DH, (h + 1) * GDN_DH
            beta = _sigmoid(b_ref[:, lo:hi])
            pb = jnp.sum(dbe_ref[:, lo:hi], axis=-1, keepdims=True) * (1.0 / GDN_DH)
            db_ref[:, lo:hi] = (pb * beta * (1.0 - beta)).astype(BF16)
            xa = a_ref[:, lo:hi] + dt_ref[:, lo:hi]
            ea = -jnp.exp(al_ref[:, lo:hi])
            pg = jnp.sum(dg_ref[:, lo:hi], axis=-1, keepdims=True) * (1.0 / GDN_DH)
            da = pg * ea * _sigmoid(xa)
            da_ref[:, lo:hi] = da.astype(BF16)
            dal_ref[:, lo:hi] += jnp.sum(pg * ea * _softplus(xa), axis=0, keepdims=True)
            ddt_ref[:, lo:hi] += jnp.sum(da, axis=0, keepdims=True)

    return _row_call(body, name, s,
                     [(projx, w, 4), (projx, w, 5), (alog, w, 0), (dtb, w, 0), (dbeta, w, 0), (dg, w, 0)],
                     [(_sds((s, w), BF16), w, 0), (_sds((s, w), BF16), w, 0),
                      (_sds((1, w), F32), w, 0), (_sds((1, w), F32), w, 0)], acc=True)


def _gdn_tri():
    c = GDN_CHUNK
    i = lax.broadcasted_iota(jnp.int32, (c, c), 0)
    j = lax.broadcasted_iota(jnp.int32, (c, c), 1)
    return ((i >= j).astype(F32), (i <= j).astype(F32), i >= j, i > j, (i == j).astype(F32))


def _bdot(a, b, ca=2, cb=1, precision=None):
    return lax.dot_general(a, b, (((ca,), (cb,)), ((0,), (0,))), precision=precision, preferred_element_type=F32)


def _bmxu(a, b, ca=2, cb=1):
    return _bdot(a.astype(BF16), b.astype(BF16), ca, cb)


def _split(x):
    hi = x.astype(BF16)
    return hi, (x - hi.astype(F32)).astype(BF16)


def _bdot3(a, b, ca=2, cb=1):
    ah, al = _split(a)
    bh, bl = _split(b)
    return _bdot(ah, bh, ca, cb) + (_bdot(ah, bl, ca, cb) + _bdot(al, bh, ca, cb))


def _tri_dot(tri, x):
    t = tri.astype(BF16)
    hi = x.astype(BF16)
    r1 = x - hi.astype(F32)
    mid = r1.astype(BF16)
    lo = (r1 - mid.astype(F32)).astype(BF16)
    return _dot(t, hi) + (_dot(t, mid) + _dot(t, lo))


def _heads(x):
    return jnp.stack([x[:, h * GDN_DH:(h + 1) * GDN_DH] for h in range(GDN_HEADS)], axis=0)


def _unheads(x):
    return jnp.concatenate([x[h] for h in range(GDN_HEADS)], axis=1)


def _gdn_chunk(q, k, v, bb, g2d, tri):
    low, up, incl, strict, eye = tri
    c = GDN_CHUNK
    gc = _heads(_tri_dot(low, g2d))
    gci = gc[:, :, :c]
    gdiff = gci - jnp.swapaxes(gci, 1, 2)
    decay = jnp.where(incl, jnp.exp(jnp.where(incl, gdiff, 0.0)), 0.0)
    kb, vb = k * bb, v * bb
    kbk = _bmxu(kb, k, 2, 2)
    x = -jnp.where(strict, kbk * decay, 0.0)
    t = eye + x
    p = x
    for _ in range(c.bit_length() - 2):
        p = _bdot3(p, p)
        t = t + _bdot3(t, p)
    eg = jnp.exp(gc)
    kbg = kb * eg
    gcl = gc[:, c - 1:c, :]
    ek = jnp.exp(gcl - gc)
    qkraw = _bmxu(q, k, 2, 2)
    return dict(decay=decay, kb=kb, vb=vb, kbk=kbk, t=t, eg=eg, kbg=kbg, ek=ek, gl=jnp.exp(gcl),
                w=_bmxu(t, kbg), u=_bmxu(t, vb), qkraw=qkraw, qk=jnp.where(incl, qkraw * decay, 0.0),
                qd=q * eg, kd=k * ek)


def _gdn_specs(n_of):
    c, w = GDN_CHUNK, GDN_HEADS * GDN_DH

    def blk(cb, width=w):
        return pl.BlockSpec((c, width), lambda n: (n_of(n), cb))

    st = pl.BlockSpec((None, GDN_HEADS, GDN_DH, GDN_DH), lambda n: (n_of(n), 0, 0, 0))
    vec = pl.BlockSpec((1, GDN_DH), lambda n: (0, 0))
    return blk, st, vec


def _gdn_load(qkv_ref, b_ref, g_ref, tri):
    w = GDN_HEADS * GDN_DH
    q, k, v = _heads(qkv_ref[:, :w]), _heads(qkv_ref[:, w:2 * w]), _heads(qkv_ref[:, 2 * w:])
    bb = _heads(b_ref[...])
    return q, k, v, bb, _gdn_chunk(q, k, v, bb, g_ref[...], tri)


def _gdn_fwd(qkv, beta, g, projx, onorm, name):
    s = qkv.shape[0]
    nc = s // GDN_CHUNK
    w = GDN_HEADS * GDN_DH
    blk, st, vec = _gdn_specs(lambda n: n)

    def body(qkv_ref, b_ref, g_ref, z_ref, on_ref, o_ref, st_ref, state):
        @pl.when(pl.program_id(0) == 0)
        def _():
            state[...] = jnp.zeros_like(state)

        _, _, _, _, ch = _gdn_load(qkv_ref, b_ref, g_ref, _gdn_tri())
        sp = state[...]
        st_ref[...] = sp
        vn = ch["u"] - _bmxu(ch["w"], sp)
        o = _bmxu(ch["qd"], sp) + _bmxu(ch["qk"], vn)
        state[...] = sp * ch["gl"] + _bmxu(ch["kd"], vn, 1, 1)
        r = lax.rsqrt(jnp.mean(o * o, axis=-1, keepdims=True) + EPS)
        z = _heads(z_ref[...])
        o_ref[...] = _unheads(o * r * on_ref[...] * (z * _sigmoid(z))).astype(BF16)

    return pl.pallas_call(
        body, name=name, grid=(nc,),
        in_specs=[blk(0, 3 * w), blk(0), blk(0), blk(3), vec], out_specs=[blk(0), st],
        out_shape=[jax.ShapeDtypeStruct((s, w), BF16), jax.ShapeDtypeStruct((nc, GDN_HEADS, GDN_DH, GDN_DH), F32)],
        scratch_shapes=[pltpu.VMEM((GDN_HEADS, GDN_DH, GDN_DH), F32)],
        compiler_params=_cp("arbitrary"),
    )(qkv, beta, g, projx, onorm.reshape(1, -1))


def _gdn_bwd(qkv, beta, g, projx, onorm, states, dout, name):
    s = qkv.shape[0]
    c = GDN_CHUNK
    nc = s // c
    w = GDN_HEADS * GDN_DH
    blk, st, vec = _gdn_specs(lambda n: nc - 1 - n)

    def body(qkv_ref, b_ref, g_ref, z_ref, on_ref, st_ref, do_ref,
             dqkv_ref, db_ref, dg_ref, dz_ref, don_ref, carry):
        @pl.when(pl.program_id(0) == 0)
        def _():
            carry[...] = jnp.zeros_like(carry)
            don_ref[...] = jnp.zeros_like(don_ref)

        tri = _gdn_tri()
        low, up, incl, strict, eye = tri
        q, k, v, bb, ch = _gdn_load(qkv_ref, b_ref, g_ref, tri)
        sp = st_ref[...]
        vn = ch["u"] - _bmxu(ch["w"], sp)
        o = _bmxu(ch["qd"], sp) + _bmxu(ch["qk"], vn)
        r = lax.rsqrt(jnp.mean(o * o, axis=-1, keepdims=True) + EPS)
        orn = o * r
        z = _heads(z_ref[...])
        sg = _sigmoid(z)
        dout = _heads(do_ref[...])
        onw = on_ref[...]
        dz_ref[...] = _unheads(dout * orn * onw * (sg * (1.0 + z * (1.0 - sg)))).astype(BF16)
        don = dout * (z * sg)
        don_ref[...] += jnp.sum(jnp.sum(don * orn, axis=0), axis=0, keepdims=True)
        dor = don * onw
        do = r * (dor - orn * jnp.mean(dor * orn, axis=-1, keepdims=True))
        dsn = carry[...]
        dqd = _bmxu(do, sp, 2, 2)
        dqk = jnp.where(incl, _bmxu(do, vn, 2, 2), 0.0)
        dvn = _bmxu(ch["qk"], do, 1, 1) + _bmxu(ch["kd"], dsn)
        dkd = _bmxu(vn, dsn, 2, 2)
        dgl = jnp.sum(dsn * sp, axis=1, keepdims=True)
        dw = -_bmxu(dvn, sp, 2, 2)
        carry[...] = _bmxu(ch["qd"], do, 1, 1) + dsn * ch["gl"] - _bmxu(ch["w"], dvn, 1, 1)
        t = ch["t"]
        dvb = _bmxu(t, dvn, 1, 1)
        dkbg = _bmxu(t, dw, 1, 1)
        dt = _bmxu(dvn, ch["vb"], 2, 2) + _bmxu(dw, ch["kbg"], 2, 2)
        da = -_bdot3(_bdot3(t, dt, 1, 1), t, 2, 2)
        da = jnp.where(strict, da, 0.0)
        decay = ch["decay"]
        dkbk = da * decay
        dqkr = dqk * decay
        mdec = (da * ch["kbk"] + dqk * ch["qkraw"]) * decay
        dkb = _bmxu(dkbk, k) + dkbg * ch["eg"]
        dk = _bmxu(dkbk, ch["kb"], 1, 1) + _bmxu(dqkr, q, 1, 1) + dkd * ch["ek"] + dkb * bb
        dq = _bmxu(dqkr, k) + dqd * ch["eg"]
        tk = dkd * ch["kd"]
        dgcl = jnp.sum(tk, axis=1, keepdims=True) + dgl * ch["gl"]
        row = lax.broadcasted_iota(jnp.int32, (GDN_HEADS, c, GDN_DH), 1)
        zpad = jnp.zeros((GDN_HEADS, c, GDN_DH - c), F32)
        dgc = (jnp.concatenate([mdec, zpad], axis=2) - jnp.concatenate([jnp.swapaxes(mdec, 1, 2), zpad], axis=2)
               + dqd * ch["qd"] - tk + dkbg * ch["kbg"] + jnp.where(row == c - 1, dgcl, 0.0))
        dqkv_ref[:, :w] = _unheads(dq)
        dqkv_ref[:, w:2 * w] = _unheads(dk)
        dqkv_ref[:, 2 * w:] = _unheads(dvb * bb)
        db_ref[...] = _unheads(dkb * k + dvb * v)
        dg_ref[...] = _tri_dot(up, _unheads(dgc))

    return pl.pallas_call(
        body, name=name, grid=(nc,),
        in_specs=[blk(0, 3 * w), blk(0), blk(0), blk(3), vec, st, blk(0)],
        out_specs=[blk(0, 3 * w), blk(0), blk(0), blk(0), vec],
        out_shape=[jax.ShapeDtypeStruct((s, 3 * w), F32), jax.ShapeDtypeStruct((s, w), F32),
                   jax.ShapeDtypeStruct((s, w), F32), jax.ShapeDtypeStruct((s, w), BF16),
                   jax.ShapeDtypeStruct((1, GDN_DH), F32)],
        scratch_shapes=[pltpu.VMEM((GDN_HEADS, GDN_DH, GDN_DH), F32)],
        compiler_params=_cp("arbitrary"),
    )(qkv, beta, g, projx, onorm.reshape(1, -1), states, dout)


_WEIGHTS = (
    "l0_mix_norm", "l0_w_in", "l0_ret_norm", "l0_s5_lambda_re", "l0_s5_lambda_im", "l0_s5_b_re", "l0_s5_b_im",
    "l0_s5_c_re", "l0_s5_c_im", "l0_s5_d", "l0_s5_log_dt", "l0_s5_w_glu", "l0_s5_b_glu", "l0_w_out",
    "l0_xa_norm", "l0_mem_norm", "l0_xa_wq", "l0_xa_wkv", "l0_xa_wo", "l0_ffn_norm", "l0_ffn_w_up",
    "l0_ffn_conv", "l0_ffn_w_down", "l1_mix_norm", "l1_w_in", "l1_conv", "l1_a_log", "l1_dt_bias", "l1_o_norm",
    "l1_w_out", "l1_xa_norm", "l1_mem_norm", "l1_xa_wq", "l1_xa_wkv", "l1_xa_wo", "l1_ffn_norm", "l1_ffn_w_up",
    "l1_ffn_conv", "l1_ffn_w_down", "final_norm")
_INPUTS = ("x", "mem") + _WEIGHTS + ("loss_target",) + tuple("m_" + n for n in _WEIGHTS) + tuple("v_" + n for n in _WEIGHTS)

_COL = ("l0_w_in", "l0_xa_wkv", "l0_ffn_w_up", "l0_ffn_conv", "l1_w_in", "l1_conv", "l1_xa_wkv", "l1_ffn_w_up",
        "l1_ffn_conv")
_ROW = ("l0_s5_w_glu", "l0_w_out", "l0_xa_wq", "l0_xa_wo", "l0_ffn_w_down", "l1_w_out", "l1_xa_wq", "l1_xa_wo",
        "l1_ffn_w_down")
_F32_WIRE = ("l0_ffn_conv", "l1_conv", "l1_ffn_conv")
_REP = tuple(n for n in _WEIGHTS if n not in _COL + _ROW)
_GATHER_GROUPS = (("l0_w_in", "l0_s5_w_glu", "l0_w_out"),
                  ("l0_xa_wq", "l0_xa_wkv", "l0_xa_wo", "l0_ffn_w_up", "l0_ffn_conv", "l0_ffn_w_down"),
                  ("l1_w_in", "l1_conv", "l1_w_out", "l1_xa_wq", "l1_xa_wkv", "l1_xa_wo"),
                  ("l1_ffn_w_up", "l1_ffn_conv", "l1_ffn_w_down"))


def _round_up(n, m):
    return (n + m - 1) // m * m


_REP_BIG = ("l0_s5_lambda_re", "l0_s5_lambda_im", "l0_s5_b_re", "l0_s5_b_im", "l0_s5_c_re", "l0_s5_c_im", "l0_s5_d")
_REP_SMALL = tuple(n for n in _REP if n not in _REP_BIG)
PACK_WIDTH = 1024


def _pack_rows(ts):
    rows = [jnp.pad(t, ((0, 0), (0, PACK_WIDTH - t.shape[1]))) for t in ts]
    rows.append(jnp.zeros((_round_up(len(ts), 8) - len(ts), PACK_WIDTH), F32))
    return jnp.concatenate(rows, axis=0)


def _s5_interleave(re, im):
    lead = re.shape[:-1]
    nt = re.shape[-1] // S5_TILE
    both = jnp.stack([re.reshape(lead + (nt, S5_TILE)), im.reshape(lead + (nt, S5_TILE))], axis=-2)
    return both.reshape(lead + (2 * re.shape[-1],))


def _s5_split(x):
    lead = x.shape[:-1]
    y = x.reshape(lead + (x.shape[-1] // (2 * S5_TILE), 2, S5_TILE))
    return y[..., 0, :].reshape(lead + (-1,)), y[..., 1, :].reshape(lead + (-1,))


def _s5_discretise(lr, li, log_dt, b_re, b_im):
    dt = jnp.exp(log_dt)[:, None]
    mag = jnp.exp(lr * dt)
    a_re = mag * jnp.cos(li * dt)
    a_im = mag * jnp.sin(li * dt)
    den = lr * lr + li * li
    z_re = ((a_re - 1.0) * lr + a_im * li) / den
    z_im = (a_im * lr - (a_re - 1.0) * li) / den
    bb_re = z_re[:, None, :] * b_re - z_im[:, None, :] * b_im
    bb_im = z_re[:, None, :] * b_im + z_im[:, None, :] * b_re
    return a_re, a_im, bb_re, bb_im


def _block_diag(b):
    g, r, c = b.shape
    return jnp.einsum("grc,gk->grkc", b, jnp.eye(g, dtype=b.dtype)).reshape(g * r, g * c)


def _block_diag_of(d, g):
    r, c = d.shape[0] // g, d.shape[1] // g
    return jnp.einsum("grkc,gk->grc", d.reshape(g, r, g, c), jnp.eye(g, dtype=d.dtype))


def kernel(*args):
    p = dict(zip(_INPUTS, args, strict=True))
    x0, mem0, tgt = p["x"][0], p["mem"][0], p["loss_target"][0]
    s, d = x0.shape
    me = _slot(*_mesh_pos())
    grads = {}
    wire = {n: (F32 if n in _F32_WIRE else BF16) for n in _COL + _ROW}

    zones = {n: _into_slot(p[n], wire[n], me, "place_" + n) for names in _GATHER_GROUPS for n in names}
    gather, pin = [], jnp.zeros((), F32)
    for i, names in enumerate(_GATHER_GROUPS):
        handle, token = _push_start([], [zones[n] for n in names], f"gather{i}_start")
        gather.append(handle)
        pin = pin + token[0, 0]
    w = {}

    def gathered(i, after):
        for n, full in zip(_GATHER_GROUPS[i], _push_wait(gather[i], after, f"gather{i}_wait")):
            if n in _COL:
                full = full.transpose(1, 0, 2)
            w[n] = full.reshape(-1, full.shape[-1]) if n in _ROW else full.reshape(full.shape[0], -1)

    pending = []

    def exchange(names, gain, tag):
        slots = []
        for n in names:
            g = grads[n]
            if n in _COL:
                g = g.reshape(g.shape[0], N_DEV, -1).transpose(1, 0, 2)
            else:
                g = g.reshape((N_DEV, -1) + g.shape[1:])
            slots.append(g.astype(wire[n]))
        handle, token = _push_start(slots, [], tag + "_start")
        pending.append((names, slots, handle, tag))
        return gain + token[0, 0]

    def xattn(pre, x_in):
        hx = _norm_fwd(x_in, p[pre + "xa_norm"], pre + "xa_norm_fwd")
        q = _mm(hx, w[pre + "xa_wq"], out_dtype=BF16, name=pre + "xa_q")
        memn = _norm_fwd(mem0, p[pre + "mem_norm"], pre + "mem_norm_fwd")
        kv = _mm(memn, w[pre + "xa_wkv"], out_dtype=BF16, name=pre + "xa_kv")
        ao = _xattn_fwd(q, kv, pre + "xattn_fwd")
        x_out = _mm(ao, w[pre + "xa_wo"], res=x_in, name=pre + "xa_o")
        return x_out, (x_in, hx, q, memn, kv, ao)

    def xattn_bwd(pre, saved, dxo):
        x_in, hx, q, memn, kv, ao = saved
        dao = _mm(dxo, w[pre + "xa_wo"], tb=True, name=pre + "xa_o_dx")
        grads[pre + "xa_wo"] = _mm(ao, dxo, ta=True, out_dtype=BF16, name=pre + "xa_o_dw")
        dq, dkv = _xattn_bwd(q, kv, dao, pre + "xattn_bwd")
        grads[pre + "xa_wq"] = _mm(hx, dq, ta=True, out_dtype=BF16, name=pre + "xa_q_dw")
        dhx = _mm(dq, w[pre + "xa_wq"], tb=True, name=pre + "xa_q_dx")
        grads[pre + "xa_wkv"] = _mm(memn, dkv, ta=True, out_dtype=BF16, name=pre + "xa_kv_dw")
        dmemn = _mm(dkv, w[pre + "xa_wkv"], tb=True, name=pre + "xa_kv_dx")
        gain = exchange((pre + "xa_wo", pre + "xa_wq", pre + "xa_wkv"), p[pre + "xa_norm"], pre + "xa_grads")
        dx_in, grads[pre + "xa_norm"] = _norm_bwd(x_in, gain, dhx, dxo, pre + "xa_norm_bwd")
        _, grads[pre + "mem_norm"] = _norm_bwd(mem0, p[pre + "mem_norm"], dmemn, jnp.zeros_like(mem0), pre + "mem_norm_bwd")
        return dx_in

    def ffn(pre, x_in):
        hf = _norm_fwd(x_in, p[pre + "ffn_norm"], pre + "ffn_norm_fwd")
        up = _mm(hf, w[pre + "ffn_w_up"], out_dtype=BF16, name=pre + "ffn_up")
        act = _ffn_act_fwd(up, w[pre + "ffn_conv"], pre + "ffn_act_fwd")
        x_out = _mm(act, w[pre + "ffn_w_down"], res=x_in, name=pre + "ffn_down")
        return x_out, (x_in, hf, up, act)

    def ffn_bwd(pre, saved, dxo):
        x_in, hf, up, act = saved
        dact = _mm(dxo, w[pre + "ffn_w_down"], tb=True, out_dtype=BF16, name=pre + "ffn_down_dx")
        grads[pre + "ffn_w_down"] = _mm(act, dxo, ta=True, out_dtype=BF16, name=pre + "ffn_down_dw")
        dpu, dpg, dcu, dcg = _ffn_act_bwd(up, w[pre + "ffn_conv"], dact, pre + "ffn_act_bwd")
        dup = jnp.concatenate([dpu, dpg], axis=1)
        grads[pre + "ffn_conv"] = jnp.concatenate([dcu, dcg], axis=1)
        dhf = _mm(dup, w[pre + "ffn_w_up"], tb=True, name=pre + "ffn_up_dx")
        grads[pre + "ffn_w_up"] = _mm(hf, dup, ta=True, out_dtype=BF16, name=pre + "ffn_up_dw")
        gain = exchange((pre + "ffn_w_down", pre + "ffn_w_up", pre + "ffn_conv"), p[pre + "ffn_norm"], pre + "ffn_grads")
        dx_in, grads[pre + "ffn_norm"] = _norm_bwd(x_in, gain, dhf, dxo, pre + "ffn_norm_bwd")
        return dx_in

    cos, sin = _rope_tables(s)
    (a_re, a_im, bb_re, bb_im), disc_vjp = jax.vjp(
        _s5_discretise, p["l0_s5_lambda_re"], p["l0_s5_lambda_im"], p["l0_s5_log_dt"], p["l0_s5_b_re"], p["l0_s5_b_im"])
    apow, apow_rev = _s5_pow_tables(_s5_interleave(a_re.reshape(1, -1), a_im.reshape(1, -1)), "l0_s5_pow_tables")
    bbt = _s5_tile_b(bb_re, bb_im).astype(BF16)
    cct = _s5_tile_c(p["l0_s5_c_re"], p["l0_s5_c_im"]).astype(BF16)
    s5_d = p["l0_s5_d"].reshape(1, -1)
    b_glu = p["l0_s5_b_glu"].reshape(1, -1)

    h0 = _norm_fwd(x0, p["l0_mix_norm"] + pin, "l0_mix_norm_fwd")
    gathered(0, h0)
    proj = _mm(h0, w["l0_w_in"], name="l0_in")
    o_ret, ret_states = _ret_fwd(proj, cos, sin, p["l0_ret_norm"], "l0_ret_fwd")
    st, y, gy = _s5_fwd(proj, bbt, cct, apow, s5_d, "l0_s5_fwd")
    z = _mm(gy, w["l0_s5_w_glu"], name="l0_s5_glu_mm")
    y2 = _s5_glu_fwd(y, z, b_glu, "l0_s5_glu_fwd")
    merged = jnp.concatenate([o_ret, y2], axis=1)
    x1 = _mm(merged, w["l0_w_out"], res=x0, name="l0_out")
    gathered(1, x1)
    x2, xa0 = xattn("l0_", x1)
    x3, ff0 = ffn("l0_", x2)

    gathered(2, x3)
    nqkv = 4 * GDN_HEADS * GDN_DH
    w1 = w["l1_w_in"]
    wx = jnp.concatenate([w1[:, :nqkv], jnp.repeat(w1[:, nqkv:nqkv + GDN_HEADS], GDN_DH, axis=1),
                          jnp.repeat(w1[:, nqkv + GDN_HEADS:], GDN_DH, axis=1)], axis=1)
    alog_x = jnp.repeat(p["l1_a_log"], GDN_DH).reshape(1, -1)
    dtb_x = jnp.repeat(p["l1_dt_bias"], GDN_DH).reshape(1, -1)
    h1 = _norm_fwd(x3, p["l1_mix_norm"], "l1_mix_norm_fwd")
    projx = _mm(h1, wx, name="l1_in")
    qkv = _gdn_conv_fwd(projx, w["l1_conv"], "l1_conv_fwd")
    beta, glog = _gdn_gates_fwd(projx, alog_x, dtb_x, "l1_gates_fwd")
    o_gdn, gdn_states = _gdn_fwd(qkv, beta, glog, projx, p["l1_o_norm"], "l1_gdn_fwd")
    x4 = _mm(o_gdn, w["l1_w_out"], res=x3, name="l1_out")
    x5, xa1 = xattn("l1_", x4)
    gathered(3, x5)
    x6, ff1 = ffn("l1_", x5)

    loss_part, dx6, grads["final_norm"] = _loss_head(x6, p["final_norm"], tgt, "loss_head")
    loss = lax.psum(loss_part[0, 0], ("x", "y", "c"))
    dx5 = ffn_bwd("l1_", ff1, dx6)
    dx4 = xattn_bwd("l1_", xa1, dx5)

    do_gdn = _mm(dx4, w["l1_w_out"], tb=True, name="l1_out_dx")
    grads["l1_w_out"] = _mm(o_gdn, dx4, ta=True, out_dtype=BF16, name="l1_out_dw")
    dqkv, dbeta, dglog, dz, grads["l1_o_norm"] = _gdn_bwd(
        qkv, beta, glog, projx, p["l1_o_norm"], gdn_states, do_gdn, "l1_gdn_bwd")
    dpre, grads["l1_conv"] = _gdn_conv_bwd(projx, w["l1_conv"], dqkv, "l1_conv_bwd")
    db, da, dalog_x, ddtb_x = _gdn_gates_bwd(projx, alog_x, dtb_x, dbeta, dglog, "l1_gates_bwd")
    dprojx = jnp.concatenate([dpre, dz, db, da], axis=1)
    dh1 = _mm(dprojx, wx, tb=True, name="l1_in_dx")
    dwx = _mm(h1, dprojx, ta=True, name="l1_in_dw")
    grads["l1_w_in"] = jnp.concatenate(
        [dwx[:, :nqkv], dwx[:, nqkv:nqkv + GDN_HEADS * GDN_DH].reshape(d, GDN_HEADS, GDN_DH).sum(-1),
         dwx[:, nqkv + GDN_HEADS * GDN_DH:].reshape(d, GDN_HEADS, GDN_DH).sum(-1)], axis=1)
    grads["l1_a_log"] = dalog_x.reshape(GDN_HEADS, GDN_DH).sum(-1)
    grads["l1_dt_bias"] = ddtb_x.reshape(GDN_HEADS, GDN_DH).sum(-1)
    gain = exchange(("l1_w_out", "l1_w_in", "l1_conv"), p["l1_mix_norm"], "l1_mix_grads")
    dx3, grads["l1_mix_norm"] = _norm_bwd(x3, gain, dh1, dx4, "l1_mix_norm_bwd")

    dx2 = ffn_bwd("l0_", ff0, dx3)
    dx1 = xattn_bwd("l0_", xa0, dx2)

    dmerged = _mm(dx1, w["l0_w_out"], tb=True, name="l0_out_dx")
    grads["l0_w_out"] = _mm(merged, dx1, ta=True, out_dtype=BF16, name="l0_out_dw")
    drq, drk, drv, drg, grads["l0_ret_norm"] = _ret_bwd(proj, cos, sin, p["l0_ret_norm"], ret_states, dmerged, "l0_ret_bwd")
    dzg, dg1, grads["l0_s5_b_glu"] = _s5_glu_bwd(dmerged, y, z, b_glu, "l0_s5_glu_bwd")
    grads["l0_s5_w_glu"] = _mm(gy, dzg, ta=True, out_dtype=BF16, name="l0_s5_glu_dw")
    dg2 = _mm(dzg, w["l0_s5_w_glu"], tb=True, name="l0_s5_glu_dx")
    du, da_s5, dbbt, dcct, grads["l0_s5_d"] = _s5_bwd(dg1, dg2, y, proj, st, bbt, cct, apow_rev, s5_d, "l0_s5_bwd")
    dproj = jnp.concatenate([drq, drk, drv, drg, du], axis=1)
    dh0 = _mm(dproj, w["l0_w_in"], tb=True, name="l0_in_dx")
    grads["l0_w_in"] = _mm(h0, dproj, ta=True, out_dtype=BF16, name="l0_in_dw")
    gain = exchange(("l0_w_out", "l0_s5_w_glu", "l0_w_in"), p["l0_mix_norm"], "l0_mix_grads")
    dx0, grads["l0_mix_norm"] = _norm_bwd(x0, gain, dh0, dx1, "l0_mix_norm_bwd")

    dbb_re, dbb_im = _s5_untile_b(dbbt)
    grads["l0_s5_c_re"], grads["l0_s5_c_im"] = _s5_untile_c(dcct)
    da_re, da_im = (t.reshape(S5_GROUPS, S5_STATE) for t in _s5_split(da_s5[0]))
    (grads["l0_s5_lambda_re"], grads["l0_s5_lambda_im"], grads["l0_s5_log_dt"], grads["l0_s5_b_re"],
     grads["l0_s5_b_im"]) = disc_vjp((da_re, da_im, dbb_re, dbb_im))

    def as_2d(t):
        return t.reshape(-1, t.shape[-1])

    def as_row(t):
        return t.reshape(1, -1)

    small_own = _pack_rows([as_row(grads[n]) for n in _REP_SMALL])
    big_own = [as_2d(grads[n].reshape(p[n].shape)) for n in _REP_BIG]
    rep_zones = [_into_slot(t, F32, me, f"place_rep{i}") for i, t in enumerate([small_own] + [t.reshape(-1, LANES) for t in big_own])]
    rep_handle, _ = _push_start([], rep_zones, "rep_grads_start")
    rep_lands = _push_wait(rep_handle, dx0, "rep_grads_wait")
    rep_land = rep_lands[0]

    outs = {}
    kinds = ("grad_", "delta_", "new_m_", "new_v_")
    for names, slots, handle, tag in pending:
        for n, own_slots, land in zip(names, slots, _push_wait(handle, rep_land, tag + "_wait")):
            shape = p[n].shape
            own = lax.dynamic_index_in_dim(own_slots, me, 0, keepdims=False)
            res = _adamw(land, own, *(p[pre + n].reshape(own.shape) for pre in ("", "m_", "v_")), "adamw_" + n)
            for kind, t in zip(kinds, res):
                outs[kind + n] = t.reshape(shape)
    for n, own, land in zip(_REP_BIG, big_own, rep_lands[1:]):
        res = _adamw(land.reshape((N_DEV,) + own.shape), own, *(as_2d(p[pre + n]) for pre in ("", "m_", "v_")), "adamw_" + n)
        for kind, t in zip(kinds, res):
            outs[kind + n] = t.reshape(p[n].shape)
    res = _adamw_rows(rep_land, small_own, *([as_row(p[pre + n]) for n in _REP_SMALL] for pre in ("", "m_", "v_")), "adamw_small")
    for j, kind in enumerate(kinds):
        for i, n in enumerate(_REP_SMALL):
            outs[kind + n] = res[j * len(_REP_SMALL) + i].reshape(p[n].shape)

    return (loss, dx0[None]) + tuple(outs[kind + n] for kind in kinds for n in _WEIGHTS)
```

```python
import functools
import math

import numpy as np
import jax
import jax.numpy as jnp
from jax import lax
from jax.experimental import pallas as pl
from jax.experimental.pallas import tpu as pltpu

F32 = jnp.float32
BF16 = jnp.bfloat16
EPS = 1e-6
N_DEV = 8
LANES = 128
VMEM_LIMIT = 48 * 1024 * 1024
HI = lax.Precision.HIGHEST

RET_HEADS, RET_DH, RET_CHUNK = 4, 128, 128
S5_GROUPS, S5_GROUP, S5_STATE = 32, 16, 64
GDN_HEADS, GDN_DH, GDN_CHUNK, GDN_CONV = 8, 128, 64, 4
XA_HEADS, XA_DH = 4, 256
FFN_CONV = 3
SCAN_ROWS = 256

ADAM_LR, ADAM_B1, ADAM_B2, ADAM_EPS, ADAM_WD, ADAM_STEP = 0.001, 0.9, 0.999, 1e-08, 0.01, 10


def _cp(*sem):
    return pltpu.CompilerParams(dimension_semantics=sem if sem else None, vmem_limit_bytes=VMEM_LIMIT)


def _tile(n, cap):
    if n <= cap:
        return n
    best = None
    for t in range(LANES, cap + 1, LANES):
        if n % t == 0:
            best = t
    assert best is not None, n
    return best


def _dot(a, b, ca=1, cb=0, precision=None):
    return lax.dot_general(a, b, (((ca,), (cb,)), ((), ())), precision=precision, preferred_element_type=F32)


def _mxu(a, b, ca=1, cb=0):
    return _dot(a.astype(BF16), b.astype(BF16), ca, cb)


def _sigmoid(x):
    return 1.0 / (1.0 + jnp.exp(-x))


def _shift_down(x, k):
    row = lax.broadcasted_iota(jnp.int32, x.shape, 0)
    return jnp.where(row >= k, pltpu.roll(x, k, 0), 0.0)


def _shift_up(x, k):
    n = x.shape[0]
    row = lax.broadcasted_iota(jnp.int32, x.shape, 0)
    return jnp.where(row < n - k, pltpu.roll(x, n - k, 0), 0.0)


def _mesh_pos():
    return lax.axis_index("x"), lax.axis_index("y"), lax.axis_index("c")


def _slot(px, py, pc):
    return 4 * px + 2 * py + pc


def _all_peers(x, y, c):
    flips = [(fx, fy, fc) for fx in (0, 1) for fy in (0, 1) for fc in (0, 1)][1:]
    return [(1 - x if fx else x, 1 - y if fy else y, 1 - c if fc else c) for fx, fy, fc in flips]


_HBM = pl.BlockSpec(memory_space=pltpu.HBM)
_SEM = pl.BlockSpec(memory_space=pltpu.SEMAPHORE)
N_PEERS = N_DEV - 1


def _push_copies(srcs, lands, send_sems, recv_sems, start):
    x, y, c = _mesh_pos()
    me = _slot(x, y, c)
    out = []
    for k, to in enumerate(_all_peers(x, y, c)):
        for a in range(len(lands)):
            src = srcs[a].at[_slot(*to)] if a < len(srcs) else lands[a].at[me]
            dst = lands[a].at[me if start else _slot(*to)]
            out.append(pltpu.make_async_remote_copy(
                src_ref=src, dst_ref=dst, send_sem=send_sems.at[a * N_PEERS + k], recv_sem=recv_sems.at[a * N_PEERS + k],
                device_id=to, device_id_type=pl.DeviceIdType.MESH))
    return out


def _into_slot(x, dtype, me, name):
    r, c = x.shape
    cap = max(16, 512 * 1024 // c)
    tr = max(t for t in range(16, min(r, cap) + 1, 16) if r % t == 0) if r % 16 == 0 else r

    def body(me_ref, x_ref, o_ref):
        o_ref[...] = x_ref[...].astype(dtype)

    return pl.pallas_call(
        body, name=name, out_shape=jax.ShapeDtypeStruct((N_DEV, r, c), dtype),
        grid_spec=pltpu.PrefetchScalarGridSpec(
            num_scalar_prefetch=1, grid=(r // tr,),
            in_specs=[pl.BlockSpec((tr, c), lambda i, me_ref: (i, 0))],
            out_specs=pl.BlockSpec((None, tr, c), lambda i, me_ref: (me_ref[0], i, 0))),
        compiler_params=_cp("parallel"),
    )(me.reshape(1).astype(jnp.int32), x)


def _push_start(scatter, gather_lands, name):
    ns, n = len(scatter), len(scatter) + len(gather_lands)
    lands = [lax.empty(a.shape, a.dtype) for a in scatter] + list(gather_lands)

    def body(*refs):
        srcs, zones = refs[:ns], refs[ns:ns + n]
        for cp in _push_copies(srcs, zones, refs[ns + n], refs[ns + n + 1], True):
            cp.start()
        refs[-1][...] = jnp.zeros((8, LANES), F32)

    hbm_in = [pltpu.with_memory_space_constraint(a, pltpu.HBM) for a in list(scatter) + lands]
    res = pl.pallas_call(
        body, name=name,
        out_shape=(pltpu.SemaphoreType.DMA((n * N_PEERS,)), pltpu.SemaphoreType.DMA((n * N_PEERS,)))
        + tuple(pltpu.HBM(a.shape, a.dtype) for a in list(scatter) + lands)
        + (jax.ShapeDtypeStruct((8, LANES), F32),),
        in_specs=[_HBM] * (ns + n),
        out_specs=(_SEM, _SEM) + (_HBM,) * (ns + n) + (pl.BlockSpec(memory_space=pltpu.VMEM),),
        input_output_aliases={i: 2 + i for i in range(ns + n)},
        compiler_params=pltpu.CompilerParams(has_side_effects=pltpu.SideEffectType.DATAFLOW_SIDE_EFFECTING),
    )(*hbm_in)
    return (res[0], res[1], res[2:2 + ns], res[2 + ns:2 + ns + n]), res[-1]


def _push_wait(handle, after, name):
    send_sems, recv_sems, srcs, lands = handle
    ns, n = len(srcs), len(lands)

    def body(*refs):
        for cp in _push_copies(refs[:ns], refs[ns:ns + n], refs[ns + n], refs[ns + n + 1], False):
            cp.wait_send()
            cp.wait_recv()

    res = pl.pallas_call(
        body, name=name,
        out_shape=tuple(pltpu.HBM(a.shape, a.dtype) for a in list(srcs) + list(lands)),
        in_specs=[_HBM] * (ns + n) + [_SEM, _SEM, pl.BlockSpec(memory_space=pl.ANY)],
        out_specs=(_HBM,) * (ns + n),
        input_output_aliases={i: i for i in range(ns + n)},
        compiler_params=pltpu.CompilerParams(has_side_effects=pltpu.SideEffectType.DATAFLOW_SIDE_EFFECTING),
    )(*srcs, *lands, send_sems, recv_sems, after)
    return res[ns:]


def _mm(a, b, *, ta=False, tb=False, out_dtype=F32, res=None, pin=None, name="mm"):
    m, k = (a.shape[1], a.shape[0]) if ta else a.shape
    n = b.shape[0] if tb else b.shape[1]
    assert k == (b.shape[1] if tb else b.shape[0]), (a.shape, b.shape, ta, tb)
    tm, tn, tk = _tile(m, 1408), _tile(n, 1536), _tile(k, 1408)
    nk = k // tk
    has_res = res is not None
    n_in = 2 + has_res + (pin is not None)

    def body(*refs):
        a_ref, b_ref = refs[:2]
        r_ref = refs[2] if has_res else None
        o_ref = refs[n_in]
        part = _mxu(a_ref[...], b_ref[...], 0 if ta else 1, 1 if tb else 0)

        def finish(r):
            if has_res:
                r = r + r_ref[...].astype(F32)
            o_ref[...] = r.astype(out_dtype)

        if nk == 1:
            finish(part)
            return
        acc = refs[-1]
        kk = pl.program_id(2)

        @pl.when(kk == 0)
        def _():
            acc[...] = part

        @pl.when(kk > 0)
        def _():
            acc[...] += part

        @pl.when(kk == nk - 1)
        def _():
            finish(acc[...])

    a_spec = pl.BlockSpec((tk, tm), lambda i, j, kk: (kk, i)) if ta else pl.BlockSpec((tm, tk), lambda i, j, kk: (i, kk))
    b_spec = pl.BlockSpec((tn, tk), lambda i, j, kk: (j, kk)) if tb else pl.BlockSpec((tk, tn), lambda i, j, kk: (kk, j))
    o_spec = pl.BlockSpec((tm, tn), lambda i, j, kk: (i, j))
    in_specs = [a_spec, b_spec] + ([o_spec] if has_res else [])
    args = (a, b) + ((res,) if has_res else ())
    if pin is not None:
        in_specs.append(pl.BlockSpec(pin.shape, lambda i, j, kk: (0, 0)))
        args += (pin,)
    return pl.pallas_call(
        body, name=name, grid=(m // tm, n // tn, nk), in_specs=in_specs, out_specs=o_spec,
        out_shape=jax.ShapeDtypeStruct((m, n), out_dtype),
        scratch_shapes=[pltpu.VMEM((tm, tn), F32)] if nk > 1 else [],
        compiler_params=_cp("parallel", "parallel", "arbitrary"),
    )(*args)


def _norm_fwd(x, g, name):
    s, d = x.shape
    tr = min(512, s)

    def body(x_ref, g_ref, o_ref):
        xv = x_ref[...]
        r = lax.rsqrt(jnp.mean(xv * xv, axis=-1, keepdims=True) + EPS)
        o_ref[...] = (xv * r * g_ref[...]).astype(BF16)

    row = pl.BlockSpec((tr, d), lambda i: (i, 0))
    return pl.pallas_call(
        body, name=name, grid=(s // tr,), in_specs=[row, pl.BlockSpec((1, d), lambda i: (0, 0))],
        out_specs=row, out_shape=jax.ShapeDtypeStruct((s, d), BF16), compiler_params=_cp("parallel"),
    )(x, g.reshape(1, d))


def _norm_bwd(x, g, dh, dres, name):
    s, d = x.shape
    tr = min(512, s)

    def body(x_ref, g_ref, dh_ref, dres_ref, dx_ref, dg_ref):
        @pl.when(pl.program_id(0) == 0)
        def _():
            dg_ref[...] = jnp.zeros_like(dg_ref)

        xv = x_ref[...]
        r = lax.rsqrt(jnp.mean(xv * xv, axis=-1, keepdims=True) + EPS)
        xn = xv * r
        dhv = dh_ref[...].astype(F32)
        dg_ref[...] += jnp.sum(dhv * xn, axis=0, keepdims=True)
        dhg = dhv * g_ref[...]
        dx_ref[...] = dres_ref[...] + r * (dhg - xn * jnp.mean(dhg * xn, axis=-1, keepdims=True))

    row = pl.BlockSpec((tr, d), lambda i: (i, 0))
    vec = pl.BlockSpec((1, d), lambda i: (0, 0))
    return pl.pallas_call(
        body, name=name, grid=(s // tr,), in_specs=[row, vec, row, row], out_specs=[row, vec],
        out_shape=[jax.ShapeDtypeStruct((s, d), F32), jax.ShapeDtypeStruct((1, d), F32)],
        compiler_params=_cp("arbitrary"),
    )(x, g.reshape(1, d), dh, dres)


def _loss_head(x, g, tgt, name):
    s, d = x.shape
    tr = min(512, s)

    def body(x_ref, g_ref, t_ref, l_ref, dx_ref, dg_ref):
        @pl.when(pl.program_id(0) == 0)
        def _():
            dg_ref[...] = jnp.zeros_like(dg_ref)
            l_ref[...] = jnp.zeros_like(l_ref)

        xv = x_ref[...]
        r = lax.rsqrt(jnp.mean(xv * xv, axis=-1, keepdims=True) + EPS)
        xn = xv * r
        err = xn * g_ref[...] - t_ref[...]
        part = 0.5 * jnp.sum(jnp.mean(err * err, axis=-1, keepdims=True), axis=0, keepdims=True)
        l_ref[...] += jnp.broadcast_to(part, l_ref.shape)
        dy = err * (1.0 / d)
        dg_ref[...] += jnp.sum(dy * xn, axis=0, keepdims=True)
        dyg = dy * g_ref[...]
        dx_ref[...] = r * (dyg - xn * jnp.mean(dyg * xn, axis=-1, keepdims=True))

    row = pl.BlockSpec((tr, d), lambda i: (i, 0))
    vec = pl.BlockSpec((1, d), lambda i: (0, 0))
    return pl.pallas_call(
        body, name=name, grid=(s // tr,), in_specs=[row, vec, row],
        out_specs=[pl.BlockSpec((1, LANES), lambda i: (0, 0)), row, vec],
        out_shape=[jax.ShapeDtypeStruct((1, LANES), F32), jax.ShapeDtypeStruct((s, d), F32),
                   jax.ShapeDtypeStruct((1, d), F32)],
        compiler_params=_cp("arbitrary"),
    )(x, g.reshape(1, d), tgt)


def _sum_slots(landed_slot, own):
    me = _slot(*_mesh_pos())
    mine = own.astype(F32)
    g = jnp.where(me == 0, mine, landed_slot(0).astype(F32))
    for i in range(1, N_DEV):
        g = g + jnp.where(me == i, mine, landed_slot(i).astype(F32))
    return g


def _adam_update(g, w, m, v):
    mm = ADAM_B1 * m + (1.0 - ADAM_B1) * g
    vv = ADAM_B2 * v + (1.0 - ADAM_B2) * (g * g)
    m_hat = mm / (1.0 - ADAM_B1 ** ADAM_STEP)
    v_hat = vv / (1.0 - ADAM_B2 ** ADAM_STEP)
    return g, -ADAM_LR * (m_hat / (jnp.sqrt(v_hat) + ADAM_EPS) + ADAM_WD * w), mm, vv


def _adamw_rows(landed, own, ws, ms, vs, name):
    k = len(ws)
    sizes = [w.shape[1] for w in ws]

    def body(*refs):
        p_ref, o_ref = refs[:2]
        w_refs, m_refs, v_refs = refs[2:2 + k], refs[2 + k:2 + 2 * k], refs[2 + 2 * k:2 + 3 * k]
        outs = refs[2 + 3 * k:]
        for i, n in enumerate(sizes):
            g = _sum_slots(lambda s: p_ref[s, i:i + 1, :n], o_ref[i:i + 1, :n])
            res = _adam_update(g, w_refs[i][...], m_refs[i][...], v_refs[i][...])
            for j in range(4):
                outs[j * k + i][...] = res[j]

    return pl.pallas_call(
        body, name=name, out_shape=[jax.ShapeDtypeStruct((1, n), F32) for _ in range(4) for n in sizes],
    )(landed, own, *ws, *ms, *vs)


def _adamw(landed, own, w, m, v, name):
    r, c = w.shape
    cap = max(8, 256 * 1024 // c)
    tr = max(t for t in range(8, min(r, cap) + 1, 8) if r % t == 0) if r % 8 == 0 else r

    def body(p_ref, o_ref, w_ref, m_ref, v_ref, g_ref, d_ref, nm_ref, nv_ref):
        g = _sum_slots(lambda i: p_ref[i], o_ref[...])
        g_ref[...], d_ref[...], nm_ref[...], nv_ref[...] = _adam_update(g, w_ref[...], m_ref[...], v_ref[...])

    blk = pl.BlockSpec((tr, c), lambda i: (i, 0))
    return pl.pallas_call(
        body, name=name, grid=(r // tr,),
        in_specs=[pl.BlockSpec((N_DEV, tr, c), lambda i: (0, i, 0)), blk, blk, blk, blk],
        out_specs=[blk] * 4, out_shape=[jax.ShapeDtypeStruct((r, c), F32)] * 4,
        compiler_params=_cp("parallel"),
    )(landed, own, w, m, v)


def _conv_fwd(x, w_ref, kw):
    acc = w_ref[kw - 1:kw, :] * x
    for j in range(kw - 1):
        acc = acc + w_ref[j:j + 1, :] * _shift_down(x, kw - 1 - j)
    return acc


def _conv_bwd(x, dy, w_ref, dw_ref, kw):
    dx = w_ref[kw - 1:kw, :] * dy
    dw_ref[kw - 1:kw, :] = jnp.sum(dy * x, axis=0, keepdims=True)
    for j in range(kw - 1):
        dx = dx + w_ref[j:j + 1, :] * _shift_up(dy, kw - 1 - j)
        dw_ref[j:j + 1, :] = jnp.sum(dy * _shift_down(x, kw - 1 - j), axis=0, keepdims=True)
    return dx


def _ffn_act_fwd(pre, cw, name):
    s, f2 = pre.shape
    nt = f2 // 2 // LANES

    def body(pu_ref, pg_ref, wu_ref, wg_ref, o_ref):
        up = _conv_fwd(pu_ref[...].astype(F32), wu_ref, FFN_CONV)
        gate = _conv_fwd(pg_ref[...].astype(F32), wg_ref, FFN_CONV)
        o_ref[...] = (gate * _sigmoid(gate) * up).astype(BF16)

    def col(rows, off):
        return pl.BlockSpec((rows, LANES), lambda j: (0, j + off))

    return pl.pallas_call(
        body, name=name, grid=(nt,),
        in_specs=[col(s, 0), col(s, nt), col(FFN_CONV, 0), col(FFN_CONV, nt)], out_specs=col(s, 0),
        out_shape=jax.ShapeDtypeStruct((s, f2 // 2), BF16), compiler_params=_cp("parallel"),
    )(pre, pre, cw, cw)


def _ffn_act_bwd(pre, cw, dact, name):
    s, f2 = pre.shape
    f = f2 // 2
    nt = f // LANES

    def body(pu_ref, pg_ref, wu_ref, wg_ref, da_ref, dpu_ref, dpg_ref, dwu_ref, dwg_ref):
        pu, pg = pu_ref[...].astype(F32), pg_ref[...].astype(F32)
        up = _conv_fwd(pu, wu_ref, FFN_CONV)
        gate = _conv_fwd(pg, wg_ref, FFN_CONV)
        sg = _sigmoid(gate)
        da = da_ref[...].astype(F32)
        dup = da * gate * sg
        dgate = da * up * (sg * (1.0 + gate * (1.0 - sg)))
        dpu_ref[...] = _conv_bwd(pu, dup, wu_ref, dwu_ref, FFN_CONV).astype(BF16)
        dpg_ref[...] = _conv_bwd(pg, dgate, wg_ref, dwg_ref, FFN_CONV).astype(BF16)

    def col(rows, off):
        return pl.BlockSpec((rows, LANES), lambda j: (0, j + off))

    return pl.pallas_call(
        body, name=name, grid=(nt,),
        in_specs=[col(s, 0), col(s, nt), col(FFN_CONV, 0), col(FFN_CONV, nt), col(s, 0)],
        out_specs=[col(s, 0), col(s, 0), col(FFN_CONV, 0), col(FFN_CONV, 0)],
        out_shape=[jax.ShapeDtypeStruct((s, f), BF16), jax.ShapeDtypeStruct((s, f), BF16),
                   jax.ShapeDtypeStruct((FFN_CONV, f), F32), jax.ShapeDtypeStruct((FFN_CONV, f), F32)],
        compiler_params=_cp("parallel"),
    )(pre, pre, cw, cw, dact)


def _xa_probs(qh, kh):
    sc = _mxu(qh, kh, 1, 1) * (XA_DH ** -0.5)
    e = jnp.exp(sc - jnp.max(sc, axis=-1, keepdims=True))
    return e / jnp.sum(e, axis=-1, keepdims=True)


def _xattn_fwd(q, kv, name):
    s, d = q.shape
    m = kv.shape[0]
    tr = min(512, s)

    def body(q_ref, kv_ref, o_ref):
        for h in range(XA_HEADS):
            lo, hi = h * XA_DH, (h + 1) * XA_DH
            p = _xa_probs(q_ref[:, lo:hi], kv_ref[:, lo:hi])
            o_ref[:, lo:hi] = _mxu(p, kv_ref[:, d + lo:d + hi]).astype(BF16)

    row = pl.BlockSpec((tr, d), lambda i: (i, 0))
    return pl.pallas_call(
        body, name=name, grid=(s // tr,), in_specs=[row, pl.BlockSpec((m, 2 * d), lambda i: (0, 0))],
        out_specs=row, out_shape=jax.ShapeDtypeStruct((s, d), BF16), compiler_params=_cp("parallel"),
    )(q, kv)


def _xattn_bwd(q, kv, do, name):
    s, d = q.shape
    m = kv.shape[0]
    tr = min(512, s)

    def body(q_ref, kv_ref, do_ref, dq_ref, dkv_ref):
        @pl.when(pl.program_id(0) == 0)
        def _():
            dkv_ref[...] = jnp.zeros_like(dkv_ref)

        for h in range(XA_HEADS):
            lo, hi = h * XA_DH, (h + 1) * XA_DH
            qh, kh, vh = q_ref[:, lo:hi], kv_ref[:, lo:hi], kv_ref[:, d + lo:d + hi]
            doh = do_ref[:, lo:hi]
            p = _xa_probs(qh, kh)
            dp = _mxu(doh, vh, 1, 1)
            ds = p * (dp - jnp.sum(p * dp, axis=-1, keepdims=True)) * (XA_DH ** -0.5)
            dq_ref[:, lo:hi] = _mxu(ds, kh).astype(BF16)
            dkv_ref[:, lo:hi] += _mxu(ds, qh, 0, 0)
            dkv_ref[:, d + lo:d + hi] += _mxu(p, doh, 0, 0)

    row = pl.BlockSpec((tr, d), lambda i: (i, 0))
    full = pl.BlockSpec((m, 2 * d), lambda i: (0, 0))
    return pl.pallas_call(
        body, name=name, grid=(s // tr,), in_specs=[row, full, row], out_specs=[row, full],
        out_shape=[jax.ShapeDtypeStruct((s, d), BF16), jax.ShapeDtypeStruct((m, 2 * d), F32)],
        compiler_params=_cp("arbitrary"),
    )(q, kv, do)


def _ret_tables():
    c = RET_CHUNK
    lg = np.log1p(-np.exp2(-5.0 - np.arange(RET_HEADS, dtype=np.float32))).astype(np.float32)
    idx = np.arange(c, dtype=np.float32)
    diff = idx[:, None] - idx[None, :]
    intra = np.where(diff >= 0, np.exp(lg[:, None, None] * np.where(diff >= 0, diff, 0.0)), 0.0)
    rk = np.broadcast_to(np.exp(lg[:, None] * (c - 1 - idx))[:, :, None], (RET_HEADS, c, LANES))
    rq = np.broadcast_to(np.exp(lg[:, None] * (idx + 1))[:, :, None], (RET_HEADS, c, LANES))
    return jnp.asarray(np.stack([intra, rk, rq], axis=1).astype(np.float32))


def _rope_tables(s):
    half = RET_DH // 2
    inv = jnp.exp(-math.log(10000.0) * jnp.arange(half, dtype=F32) / half)
    ang = jnp.arange(s, dtype=F32)[:, None] * inv[None, :]
    cos, sin = jnp.cos(ang), jnp.sin(ang)
    return jnp.concatenate([cos, cos], axis=1), jnp.concatenate([-sin, sin], axis=1)


def _ret_specs(n_of):
    c, w = RET_CHUNK, RET_HEADS * RET_DH

    def part(off):
        return pl.BlockSpec((c, w), lambda n: (n_of(n), off))

    pos = pl.BlockSpec((c, RET_DH), lambda n: (n_of(n), 0))
    gain = pl.BlockSpec((1, w), lambda n: (0, 0))
    tab = pl.BlockSpec((RET_HEADS, 3, c, LANES), lambda n: (0, 0, 0, 0))
    st = pl.BlockSpec((RET_HEADS, None, RET_DH, RET_DH), lambda n: (0, n_of(n), 0, 0))
    return part, pos, gain, tab, st


def _rheads(x):
    return jnp.stack([x[:, h * RET_DH:(h + 1) * RET_DH] for h in range(RET_HEADS)], axis=0)


def _runheads(x):
    return jnp.concatenate([x[h] for h in range(RET_HEADS)], axis=1)


def _rope(x, cos, sin):
    return x * cos + pltpu.roll(x, RET_DH // 2, 2) * sin


def _ret_chunk(q_ref, k_ref, v_ref, cos_ref, sin_ref, tab_ref, prev):
    cos, sin = cos_ref[...], sin_ref[...]
    q = _rope(_rheads(q_ref[...]), cos, sin)
    k = _rope(_rheads(k_ref[...]), cos, sin) * (RET_DH ** -0.5)
    v = _rheads(v_ref[...])
    scores = _bmxu(q, k, 2, 2) * tab_ref[:, 0]
    qdec = q * tab_ref[:, 2]
    kdec = k * tab_ref[:, 1]
    o = _bmxu(scores, v) + _bmxu(qdec, prev)
    return q, k, v, scores, qdec, kdec, o


def _ret_fwd(proj, cos, sin, gain, name):
    s = proj.shape[0]
    c = RET_CHUNK
    nc = s // c
    part, pos, gvec, tab, st = _ret_specs(lambda n: n)

    def body(q_ref, k_ref, v_ref, g_ref, cos_ref, sin_ref, rn_ref, tab_ref, o_ref, st_ref, state):
        @pl.when(pl.program_id(0) == 0)
        def _():
            state[...] = jnp.zeros_like(state)

        prev = state[...]
        st_ref[...] = prev
        _, _, v, _, _, kdec, o = _ret_chunk(q_ref, k_ref, v_ref, cos_ref, sin_ref, tab_ref, prev)
        state[...] = prev * tab_ref[:, 2, c - 1:c, :] + _bmxu(kdec, v, 1, 1)
        r = lax.rsqrt(jnp.mean(o * o, axis=-1, keepdims=True) + EPS)
        gate = g_ref[...]
        o_ref[...] = (_runheads(o * r) * rn_ref[...] * (gate * _sigmoid(gate))).astype(BF16)

    return pl.pallas_call(
        body, name=name, grid=(nc,),
        in_specs=[part(0), part(1), part(2), part(3), pos, pos, gvec, tab],
        out_specs=[part(0), st],
        out_shape=[jax.ShapeDtypeStruct((s, RET_HEADS * RET_DH), BF16),
                   jax.ShapeDtypeStruct((RET_HEADS, nc, RET_DH, RET_DH), F32)],
        scratch_shapes=[pltpu.VMEM((RET_HEADS, RET_DH, RET_DH), F32)],
        compiler_params=_cp("arbitrary"),
    )(proj, proj, proj, proj, cos, sin, gain.reshape(1, -1), _ret_tables())


def _ret_bwd(proj, cos, sin, gain, states, dmerged, name):
    s = proj.shape[0]
    c = RET_CHUNK
    nc = s // c
    width = RET_HEADS * RET_DH
    part, pos, gvec, tab, st = _ret_specs(lambda n: nc - 1 - n)

    def body(q_ref, k_ref, v_ref, g_ref, cos_ref, sin_ref, rn_ref, tab_ref, st_ref, do_ref,
             dp_ref, drn_ref, carry):
        @pl.when(pl.program_id(0) == 0)
        def _():
            carry[...] = jnp.zeros_like(carry)
            drn_ref[...] = jnp.zeros_like(drn_ref)

        prev = st_ref[...]
        q, k, v, scores, qdec, kdec, o = _ret_chunk(q_ref, k_ref, v_ref, cos_ref, sin_ref, tab_ref, prev)
        r = lax.rsqrt(jnp.mean(o * o, axis=-1, keepdims=True) + EPS)
        on = o * r
        on2 = _runheads(on)
        gate = g_ref[...]
        sg = _sigmoid(gate)
        sil = gate * sg
        dout = do_ref[...]
        rn = rn_ref[...]
        dp_ref[:, 3 * width:] = (dout * on2 * rn * (sg * (1.0 + gate * (1.0 - sg)))).astype(BF16)
        drn_ref[...] += jnp.sum(dout * on2 * sil, axis=0, keepdims=True)
        don = _rheads(dout * rn * sil)
        do = r * (don - on * jnp.mean(don * on, axis=-1, keepdims=True))
        dc = carry[...]
        dsc = _bmxu(do, v, 2, 2) * tab_ref[:, 0]
        dq = _bmxu(dsc, k) + _bmxu(do, prev, 2, 2) * tab_ref[:, 2]
        dk = _bmxu(dsc, q, 1, 1) + _bmxu(v, dc, 2, 2) * tab_ref[:, 1]
        dv = _bmxu(scores, do, 1, 1) + _bmxu(kdec, dc)
        carry[...] = _bmxu(qdec, do, 1, 1) + dc * tab_ref[:, 2, c - 1:c, :]
        cos, sin = cos_ref[...], sin_ref[...]
        dk = dk * (RET_DH ** -0.5)
        dp_ref[:, :width] = _runheads(dq * cos + pltpu.roll(dq * sin, RET_DH // 2, 2)).astype(BF16)
        dp_ref[:, width:2 * width] = _runheads(dk * cos + pltpu.roll(dk * sin, RET_DH // 2, 2)).astype(BF16)
        dp_ref[:, 2 * width:3 * width] = _runheads(dv).astype(BF16)

    return pl.pallas_call(
        body, name=name, grid=(nc,),
        in_specs=[part(0), part(1), part(2), part(3), pos, pos, gvec, tab, st, part(0)],
        out_specs=[pl.BlockSpec((c, 4 * width), lambda n: (nc - 1 - n, 0)), gvec],
        out_shape=[jax.ShapeDtypeStruct(proj.shape, BF16), jax.ShapeDtypeStruct((1, width), F32)],
        scratch_shapes=[pltpu.VMEM((RET_HEADS, RET_DH, RET_DH), F32)],
        compiler_params=_cp("arbitrary"),
    )(proj, proj, proj, proj, cos, sin, gain.reshape(1, -1), _ret_tables(), states, dmerged)


S5_TILE = 512


def _cmul_add(xr, xi, ar, ai, yr, yi):
    return xr + ar * yr - ai * yi, xi + ar * yi + ai * yr


def _s5_pow_tables(a_il, name):
    r = SCAN_ROWS
    t = S5_TILE
    w2 = a_il.shape[1]

    def body(a_ref, up_ref, dn_ref):
        for j in range(w2 // (2 * t)):
            re, im = pl.ds(2 * t * j, t), pl.ds(2 * t * j + t, t)
            up_ref[0:1, re] = a_ref[:, re]
            up_ref[0:1, im] = a_ref[:, im]
            dn_ref[r - 1:r, re] = a_ref[:, re]
            dn_ref[r - 1:r, im] = -a_ref[:, im]
            n = 1
            while n < r:
                lr, li = up_ref[n - 1:n, re], up_ref[n - 1:n, im]
                xr, xi = up_ref[0:n, re], up_ref[0:n, im]
                up_ref[n:2 * n, re] = xr * lr - xi * li
                up_ref[n:2 * n, im] = xr * li + xi * lr
                yr, yi = dn_ref[r - n:r, re], dn_ref[r - n:r, im]
                dn_ref[r - 2 * n:r - n, re] = yr * lr + yi * li
                dn_ref[r - 2 * n:r - n, im] = yi * lr - yr * li
                n *= 2

    return pl.pallas_call(
        body, name=name, out_shape=[jax.ShapeDtypeStruct((r, w2), F32)] * 2, compiler_params=_cp(),
    )(a_il)


def _s5_scan_fwd(bu, apow, name):
    s, w2 = bu.shape
    r = SCAN_ROWS
    t = S5_TILE
    steps = r.bit_length() - 1

    def body(b_ref, p_ref, o_ref, cr, ci):
        @pl.when(pl.program_id(1) == 0)
        def _():
            cr[...] = jnp.zeros_like(cr)
            ci[...] = jnp.zeros_like(ci)

        xr, xi = b_ref[:, :t], b_ref[:, t:]
        for k in range(steps):
            sh = 1 << k
            xr, xi = _cmul_add(xr, xi, p_ref[sh - 1:sh, :t], p_ref[sh - 1:sh, t:],
                               _shift_down(xr, sh), _shift_down(xi, sh))
        xr, xi = _cmul_add(xr, xi, p_ref[:, :t], p_ref[:, t:], cr[...], ci[...])
        o_ref[:, :t] = xr
        o_ref[:, t:] = xi
        cr[...] = xr[r - 1:r, :]
        ci[...] = xi[r - 1:r, :]

    blk = pl.BlockSpec((r, 2 * t), lambda j, i: (i, j))
    return pl.pallas_call(
        body, name=name, grid=(w2 // (2 * t), s // r),
        in_specs=[blk, pl.BlockSpec((r, 2 * t), lambda j, i: (0, j))], out_specs=blk,
        out_shape=jax.ShapeDtypeStruct((s, w2), F32),
        scratch_shapes=[pltpu.VMEM((1, t), F32), pltpu.VMEM((1, t), F32)],
        compiler_params=_cp("parallel", "arbitrary"),
    )(bu, apow)


def _s5_scan_bwd(dst, apow_rev, st, name):
    s, w2 = dst.shape
    r = SCAN_ROWS
    t = S5_TILE
    nb = s // r
    steps = r.bit_length() - 1

    def body(d_ref, p_ref, s_ref, sp_ref, g_ref, da_ref, cr, ci):
        i = pl.program_id(1)

        @pl.when(i == 0)
        def _():
            cr[...] = jnp.zeros_like(cr)
            ci[...] = jnp.zeros_like(ci)
            da_ref[...] = jnp.zeros_like(da_ref)

        xr, xi = d_ref[:, :t], d_ref[:, t:]
        for k in range(steps):
            sh = 1 << k
            xr, xi = _cmul_add(xr, xi, p_ref[r - sh:r - sh + 1, :t], p_ref[r - sh:r - sh + 1, t:],
                               _shift_up(xr, sh), _shift_up(xi, sh))
        xr, xi = _cmul_add(xr, xi, p_ref[:, :t], p_ref[:, t:], cr[...], ci[...])
        g_ref[:, :t] = xr.astype(BF16)
        g_ref[:, t:] = xi.astype(BF16)
        cr[...] = xr[0:1, :]
        ci[...] = xi[0:1, :]
        first = i == nb - 1
        row = lax.broadcasted_iota(jnp.int32, (r, t), 0)
        last_r = jnp.where(first, 0.0, sp_ref[7:8, :t])
        last_i = jnp.where(first, 0.0, sp_ref[7:8, t:])
        pr = jnp.where(row == 0, last_r, pltpu.roll(s_ref[:, :t], 1, 0))
        pi = jnp.where(row == 0, last_i, pltpu.roll(s_ref[:, t:], 1, 0))
        da_ref[:, :t] += jnp.sum(xr * pr + xi * pi, axis=0, keepdims=True)
        da_ref[:, t:] += jnp.sum(xi * pr - xr * pi, axis=0, keepdims=True)

    blk = pl.BlockSpec((r, 2 * t), lambda j, i: (nb - 1 - i, j))
    halo = pl.BlockSpec((8, 2 * t), lambda j, i: (jnp.maximum((nb - 1 - i) * (r // 8) - 1, 0), j))
    vec = pl.BlockSpec((1, 2 * t), lambda j, i: (0, j))
    return pl.pallas_call(
        body, name=name, grid=(w2 // (2 * t), nb),
        in_specs=[blk, pl.BlockSpec((r, 2 * t), lambda j, i: (0, j)), blk, halo], out_specs=[blk, vec],
        out_shape=[jax.ShapeDtypeStruct((s, w2), BF16), jax.ShapeDtypeStruct((1, w2), F32)],
        scratch_shapes=[pltpu.VMEM((1, t), F32), pltpu.VMEM((1, t), F32)],
        compiler_params=_cp("parallel", "arbitrary"),
    )(dst, apow_rev, st, st)


_GELU_C = math.sqrt(2.0 / math.pi)
_GELU_A = 0.044715


def _gelu(y):
    return 0.5 * y * (1.0 + jnp.tanh(_GELU_C * (y + _GELU_A * y * y * y)))


def _gelu_grad(y):
    th = jnp.tanh(_GELU_C * (y + _GELU_A * y * y * y))
    return 0.5 * (1.0 + th) + 0.5 * y * (1.0 - th * th) * _GELU_C * (1.0 + 3.0 * _GELU_A * y * y)


def _rows_shift(x, k, axis, up):
    n = x.shape[axis]
    idx = lax.broadcasted_iota(jnp.int32, x.shape, axis)
    if up:
        return jnp.where(idx < n - k, pltpu.roll(x, n - k, axis), 0.0)
    return jnp.where(idx >= k, pltpu.roll(x, k, axis), 0.0)


def _scan_block(xr, xi, pr, pi, cr, ci, rev):
    r, w = xr.shape
    nt = r // 8
    x3r, x3i = xr.reshape(nt, 8, w), xi.reshape(nt, 8, w)
    p3r, p3i = pr.reshape(nt, 8, w), pi.reshape(nt, 8, w)

    def power(rows):
        t = r - rows if rev else rows - 1
        return pr[t:t + 1, :], pi[t:t + 1, :]

    for sh in (1, 2, 4):
        ar, ai = power(sh)
        x3r, x3i = _cmul_add(x3r, x3i, ar, ai, _rows_shift(x3r, sh, 1, rev), _rows_shift(x3i, sh, 1, rev))
    edge = 0 if rev else 7
    lr, li = x3r[:, edge, :], x3i[:, edge, :]
    sh = 1
    while sh < nt:
        ar, ai = power(8 * sh)
        lr, li = _cmul_add(lr, li, ar, ai, _rows_shift(lr, sh, 0, rev), _rows_shift(li, sh, 0, rev))
        sh *= 2
    tr_, ti_ = p3r[:, edge, :], p3i[:, edge, :]
    first = lax.broadcasted_iota(jnp.int32, (nt, w), 0) == (nt - 1 if rev else 0)
    wr = jnp.where(first, 1.0, _rows_shift(tr_, 1, 0, rev))
    wi = jnp.where(first, 0.0, _rows_shift(ti_, 1, 0, rev))
    er, ei = _cmul_add(_rows_shift(lr, 1, 0, rev), _rows_shift(li, 1, 0, rev), wr, wi, cr, ci)
    a8r, a8i = (p3r[nt - 1], p3i[nt - 1]) if rev else (p3r[0], p3i[0])
    x3r, x3i = _cmul_add(x3r, x3i, a8r[None], a8i[None], er[:, None, :], ei[:, None, :])
    outr, outi = x3r.reshape(r, w), x3i.reshape(r, w)
    last = 0 if rev else r - 1
    return outr, outi, outr[last:last + 1, :], outi[last:last + 1, :]


def _s5_tile_specs(n_of, r):
    t = S5_TILE
    ucol = 4 * RET_HEADS * RET_DH // LANES
    u = pl.BlockSpec((r, LANES), lambda j, i: (n_of(i), ucol + j))
    col = pl.BlockSpec((r, LANES), lambda j, i: (n_of(i), j))
    state = pl.BlockSpec((r, 2 * t), lambda j, i: (n_of(i), j))
    table = pl.BlockSpec((r, 2 * t), lambda j, i: (0, j))
    bbt = pl.BlockSpec((None, LANES, 2 * t), lambda j, i: (j, 0, 0))
    cct = pl.BlockSpec((None, 2 * t, LANES), lambda j, i: (j, 0, 0))
    vec = pl.BlockSpec((1, LANES), lambda j, i: (0, j))
    return u, col, state, table, bbt, cct, vec


def _s5_fwd(proj, bbt, cct, apow, dvec, name):
    s = proj.shape[0]
    r, t = SCAN_ROWS, S5_TILE
    w = S5_GROUPS * S5_GROUP
    u_s, col, state, table, bb_s, cc_s, vec = _s5_tile_specs(lambda i: i, r)

    def body(u_ref, bb_ref, cc_ref, p_ref, d_ref, st_ref, y_ref, g_ref, cr, ci):
        @pl.when(pl.program_id(1) == 0)
        def _():
            cr[...] = jnp.zeros_like(cr)
            ci[...] = jnp.zeros_like(ci)

        u = u_ref[...]
        bu = _mxu(u, bb_ref[...])
        xr, xi, cr[...], ci[...] = _scan_block(bu[:, :t], bu[:, t:], p_ref[:, :t], p_ref[:, t:], cr[...], ci[...], False)
        st_ref[:, :t] = xr
        st_ref[:, t:] = xi
        y = _mxu(xr, cc_ref[:t, :]) + _mxu(xi, cc_ref[t:, :]) + d_ref[...] * u
        y_ref[...] = y
        g_ref[...] = _gelu(y).astype(BF16)

    return pl.pallas_call(
        body, name=name, grid=(2 * S5_GROUPS * S5_STATE // (2 * t), s // r),
        in_specs=[u_s, bb_s, cc_s, table, vec], out_specs=[state, col, col],
        out_shape=[jax.ShapeDtypeStruct((s, 2 * S5_GROUPS * S5_STATE), F32), jax.ShapeDtypeStruct((s, w), F32),
                   jax.ShapeDtypeStruct((s, w), BF16)],
        scratch_shapes=[pltpu.VMEM((1, t), F32), pltpu.VMEM((1, t), F32)],
        compiler_params=_cp("parallel", "arbitrary"),
    )(proj, bbt, cct, apow, dvec)


def _s5_bwd(dg1, dg2, y, proj, st, bbt, cct, apow_rev, dvec, dproj, name):
    s = proj.shape[0]
    r, t = SCAN_ROWS, S5_TILE
    nb = s // r
    w = S5_GROUPS * S5_GROUP
    u_s, col, state, table, bb_s, cc_s, vec = _s5_tile_specs(lambda i: nb - 1 - i, r)
    halo = pl.BlockSpec((8, 2 * t), lambda j, i: (jnp.maximum((nb - 1 - i) * (r // 8) - 1, 0), j))
    acc = pl.BlockSpec((1, 2 * t), lambda j, i: (0, j))

    def body(a_ref, b_ref, y_ref, u_ref, s_ref, sp_ref, bb_ref, cc_ref, p_ref, d_ref, _,
             du_ref, da_ref, dbb_ref, dcc_ref, dd_ref, cr, ci):
        i = pl.program_id(1)

        @pl.when(i == 0)
        def _():
            cr[...] = jnp.zeros_like(cr)
            ci[...] = jnp.zeros_like(ci)
            da_ref[...] = jnp.zeros_like(da_ref)
            dbb_ref[...] = jnp.zeros_like(dbb_ref)
            dcc_ref[...] = jnp.zeros_like(dcc_ref)
            dd_ref[...] = jnp.zeros_like(dd_ref)

        u = u_ref[...]
        dy = (a_ref[...] + b_ref[...]) * _gelu_grad(y_ref[...])
        dd_ref[...] += jnp.sum(dy * u, axis=0, keepdims=True)
        sr, si = s_ref[:, :t], s_ref[:, t:]
        dcc_ref[:t, :] += _mxu(sr, dy, 0, 0)
        dcc_ref[t:, :] += _mxu(si, dy, 0, 0)
        xr, xi, cr[...], ci[...] = _scan_block(_mxu(dy, cc_ref[:t, :], 1, 1), _mxu(dy, cc_ref[t:, :], 1, 1),
                                               p_ref[:, :t], p_ref[:, t:], cr[...], ci[...], True)
        du_ref[...] = (dy * d_ref[...] + _mxu(xr, bb_ref[:, :t], 1, 1) + _mxu(xi, bb_ref[:, t:], 1, 1)).astype(BF16)
        dbb_ref[:, :t] += _mxu(u, xr, 0, 0)
        dbb_ref[:, t:] += _mxu(u, xi, 0, 0)
        first = i == nb - 1
        row = lax.broadcasted_iota(jnp.int32, (r, t), 0)
        pr = jnp.where(row == 0, jnp.where(first, 0.0, sp_ref[7:8, :t]), pltpu.roll(sr, 1, 0))
        pi = jnp.where(row == 0, jnp.where(first, 0.0, sp_ref[7:8, t:]), pltpu.roll(si, 1, 0))
        da_ref[:, :t] += jnp.sum(xr * pr + xi * pi, axis=0, keepdims=True)
        da_ref[:, t:] += jnp.sum(xi * pr - xr * pi, axis=0, keepdims=True)

    return pl.pallas_call(
        body, name=name, grid=(2 * S5_GROUPS * S5_STATE // (2 * t), nb),
        in_specs=[col, col, col, u_s, state, halo, bb_s, cc_s, table, vec, pl.BlockSpec(memory_space=pl.ANY)],
        out_specs=[u_s, acc, bb_s, cc_s, vec],
        out_shape=[jax.ShapeDtypeStruct(dproj.shape, dproj.dtype), jax.ShapeDtypeStruct((1, 2 * S5_GROUPS * S5_STATE), F32),
                   jax.ShapeDtypeStruct(bbt.shape, F32), jax.ShapeDtypeStruct(cct.shape, F32),
                   jax.ShapeDtypeStruct((1, w), F32)],
        scratch_shapes=[pltpu.VMEM((1, t), F32), pltpu.VMEM((1, t), F32)],
        input_output_aliases={10: 0}, compiler_params=_cp("parallel", "arbitrary"),
    )(dg1, dg2, y, proj, st, st, bbt, cct, apow_rev, dvec, dproj)


def _s5_tile_b(b_re, b_im):
    nt = S5_GROUPS * S5_STATE // S5_TILE
    gpt = S5_GROUPS // nt
    eye = jnp.eye(gpt, dtype=F32)

    def tile(b):
        t5 = jnp.einsum("jghp,gk->jghkp", b.reshape(nt, gpt, S5_GROUP, S5_STATE), eye)
        return t5.reshape(nt, gpt * S5_GROUP, S5_TILE)

    return jnp.concatenate([tile(b_re), tile(b_im)], axis=2)


def _s5_untile_b(d):
    nt = S5_GROUPS * S5_STATE // S5_TILE
    gpt = S5_GROUPS // nt
    eye = jnp.eye(gpt, dtype=F32)

    def untile(x):
        x5 = x.reshape(nt, gpt, S5_GROUP, gpt, S5_STATE)
        return jnp.einsum("jghkp,gk->jghp", x5, eye).reshape(S5_GROUPS, S5_GROUP, S5_STATE)

    return untile(d[:, :, :S5_TILE]), untile(d[:, :, S5_TILE:])


def _s5_tile_c(c_re, c_im):
    nt = S5_GROUPS * S5_STATE // S5_TILE
    gpt = S5_GROUPS // nt
    eye = jnp.eye(gpt, dtype=F32)

    def tile(c):
        t5 = jnp.einsum("jgph,gk->jkpgh", c.reshape(nt, gpt, S5_STATE, S5_GROUP), eye)
        return t5.reshape(nt, S5_TILE, gpt * S5_GROUP)

    return jnp.concatenate([tile(c_re), -tile(c_im)], axis=1)


def _s5_untile_c(d):
    nt = S5_GROUPS * S5_STATE // S5_TILE
    gpt = S5_GROUPS // nt
    eye = jnp.eye(gpt, dtype=F32)

    def untile(x):
        x5 = x.reshape(nt, gpt, S5_STATE, gpt, S5_GROUP)
        return jnp.einsum("jkpgh,gk->jgph", x5, eye).reshape(S5_GROUPS, S5_STATE, S5_GROUP)

    return untile(d[:, :S5_TILE, :]), -untile(d[:, S5_TILE:, :])


def _row_call(body, name, s, ins, outs, acc=False):
    tr = min(512, s)

    def spec(width, cb, rows):
        if rows == 1:
            return pl.BlockSpec((1, width), lambda i: (0, cb))
        return pl.BlockSpec((tr, width), lambda i: (i, cb))

    in_specs = [spec(w, cb, a.shape[0]) for a, w, cb in ins]
    out_specs = [spec(w, cb, sd.shape[0]) for sd, w, cb in outs]
    return pl.pallas_call(
        body, name=name, grid=(s // tr,), in_specs=in_specs, out_specs=out_specs,
        out_shape=[sd for sd, _, _ in outs],
        compiler_params=_cp("arbitrary" if acc else "parallel"),
    )(*[a for a, _, _ in ins])


def _sds(shape, dtype):
    return jax.ShapeDtypeStruct(shape, dtype)


def _s5_gelu_fwd(yraw, proj, dvec, name):
    s, w = yraw.shape

    def body(y_ref, u_ref, d_ref, yo_ref, g_ref):
        y = y_ref[...] + d_ref[...] * u_ref[...]
        yo_ref[...] = y
        g_ref[...] = _gelu(y).astype(BF16)

    return _row_call(body, name, s, [(yraw, w, 0), (proj, w, 4), (dvec, w, 0)],
                     [(_sds((s, w), F32), w, 0), (_sds((s, w), BF16), w, 0)])


def _s5_glu_fwd(y, z, b, name):
    s, w = y.shape

    def body(y_ref, z_ref, b_ref, o_ref):
        o_ref[...] = (_gelu(y_ref[...]) * _sigmoid(z_ref[...] + b_ref[...])).astype(BF16)

    return _row_call(body, name, s, [(y, w, 0), (z, w, 0), (b, w, 0)], [(_sds((s, w), BF16), w, 0)])[0]


def _s5_glu_bwd(dmerged, y, z, b, name):
    s, w = y.shape

    def body(do_ref, y_ref, z_ref, b_ref, dz_ref, dg_ref, db_ref):
        @pl.when(pl.program_id(0) == 0)
        def _():
            db_ref[...] = jnp.zeros_like(db_ref)

        g = _gelu(y_ref[...])
        sg = _sigmoid(z_ref[...] + b_ref[...])
        dout = do_ref[...]
        dz = dout * g * sg * (1.0 - sg)
        dz_ref[...] = dz.astype(BF16)
        dg_ref[...] = dout * sg
        db_ref[...] += jnp.sum(dz, axis=0, keepdims=True)

    return _row_call(body, name, s, [(dmerged, w, 1), (y, w, 0), (z, w, 0), (b, w, 0)],
                     [(_sds((s, w), BF16), w, 0), (_sds((s, w), F32), w, 0), (_sds((1, w), F32), w, 0)], acc=True)


def _s5_gelu_bwd(dg1, dg2, y, proj, dvec, name):
    s, w = y.shape

    def body(a_ref, b_ref, y_ref, u_ref, d_ref, dy_ref, du_ref, dd_ref):
        @pl.when(pl.program_id(0) == 0)
        def _():
            dd_ref[...] = jnp.zeros_like(dd_ref)

        dy = (a_ref[...] + b_ref[...]) * _gelu_grad(y_ref[...])
        dy_ref[...] = dy.astype(BF16)
        du_ref[...] = dy * d_ref[...]
        dd_ref[...] += jnp.sum(dy * u_ref[...], axis=0, keepdims=True)

    return _row_call(body, name, s, [(dg1, w, 0), (dg2, w, 0), (y, w, 0), (proj, w, 4), (dvec, w, 0)],
                     [(_sds((s, w), BF16), w, 0), (_sds((s, w), F32), w, 0), (_sds((1, w), F32), w, 0)], acc=True)


def _gdn_conv_fwd(projx, cw, name):
    s = projx.shape[0]
    nh = GDN_HEADS

    def body(x_ref, w_ref, o_ref):
        j = pl.program_id(0)
        cv = _conv_fwd(x_ref[...], w_ref, GDN_CONV)
        y = cv * _sigmoid(cv)
        nrm = y * lax.rsqrt(jnp.sum(y * y, axis=-1, keepdims=True) + EPS)
        o_ref[...] = jnp.where(j < nh, nrm * (GDN_DH ** -0.5), jnp.where(j < 2 * nh, nrm, y))

    return pl.pallas_call(
        body, name=name, grid=(3 * nh,),
        in_specs=[pl.BlockSpec((s, GDN_DH), lambda j: (0, j)), pl.BlockSpec((GDN_CONV, GDN_DH), lambda j: (0, j))],
        out_specs=pl.BlockSpec((s, GDN_DH), lambda j: (0, j)),
        out_shape=jax.ShapeDtypeStruct((s, 3 * nh * GDN_DH), F32), compiler_params=_cp("parallel"),
    )(projx, cw)


def _gdn_conv_bwd(projx, cw, dqkv, dprojx, name):
    s = projx.shape[0]
    nh = GDN_HEADS

    def body(x_ref, w_ref, d_ref, _, dx_ref, dw_ref):
        j = pl.program_id(0)
        x = x_ref[...]
        cv = _conv_fwd(x, w_ref, GDN_CONV)
        sg = _sigmoid(cv)
        y = cv * sg
        rinv = lax.rsqrt(jnp.sum(y * y, axis=-1, keepdims=True) + EPS)
        nrm = y * rinv
        dn = d_ref[...]
        dns = jnp.where(j < nh, dn * (GDN_DH ** -0.5), dn)
        dyn = rinv * (dns - nrm * jnp.sum(dns * nrm, axis=-1, keepdims=True))
        dy = jnp.where(j < 2 * nh, dyn, dn)
        dc = dy * (sg * (1.0 + cv * (1.0 - sg)))
        dx_ref[...] = _conv_bwd(x, dc, w_ref, dw_ref, GDN_CONV).astype(BF16)

    col = pl.BlockSpec((s, GDN_DH), lambda j: (0, j))
    wcol = pl.BlockSpec((GDN_CONV, GDN_DH), lambda j: (0, j))
    return pl.pallas_call(
        body, name=name, grid=(3 * nh,), in_specs=[col, wcol, col, pl.BlockSpec(memory_space=pl.ANY)],
        out_specs=[col, wcol],
        out_shape=[jax.ShapeDtypeStruct(dprojx.shape, dprojx.dtype), jax.ShapeDtypeStruct((GDN_CONV, 3 * nh * GDN_DH), F32)],
        input_output_aliases={3: 0}, compiler_params=_cp("parallel"),
    )(projx, cw, dqkv, dprojx)


def _softplus(x):
    return jnp.maximum(x, 0.0) + jnp.log1p(jnp.exp(-jnp.abs(x)))


def _gdn_gates_fwd(projx, alog, dtb, name):
    s = projx.shape[0]
    w = GDN_HEADS * GDN_DH

    def body(b_ref, a_ref, al_ref, dt_ref, bo_ref, go_ref):
        bo_ref[...] = _sigmoid(b_ref[...])
        go_ref[...] = -jnp.exp(al_ref[...]) * _softplus(a_ref[...] + dt_ref[...])

    return _row_call(body, name, s, [(projx, w, 4), (projx, w, 5), (alog, w, 0), (dtb, w, 0)],
                     [(_sds((s, w), F32), w, 0), (_sds((s, w), F32), w, 0)])


def _gdn_gates_bwd(projx, alog, dtb, dbeta, dg, dprojx, name):
    s = projx.shape[0]
    w = GDN_HEADS * GDN_DH
    tr = min(512, s)

    def body(b_ref, a_ref, al_ref, dt_ref, dbe_ref, dg_ref, _, o_ref, dal_ref, ddt_ref):
        @pl.when(pl.program_id(0) == 0)
        def _():
            dal_ref[...] = jnp.zeros_like(dal_ref)
            ddt_ref[...] = jnp.zeros_like(ddt_ref)

        for h in range(GDN_HEADS):
            lo, hi = h * GDN_DH, (h + 1) * GDN_DH
            beta = _sigmoid(b_ref[:, lo:hi])
            pb = jnp.sum(dbe_ref[:, lo:hi], axis=-1, keepdims=True) * (1.0 / GDN_DH)
            o_ref[:, lo:hi] = (pb * beta * (1.0 - beta)).astype(BF16)
            xa = a_ref[:, lo:hi] + dt_ref[:, lo:hi]
            ea = -jnp.exp(al_ref[:, lo:hi])
            pg = jnp.sum(dg_ref[:, lo:hi], axis=-1, keepdims=True) * (1.0 / GDN_DH)
            da = pg * ea * _sigmoid(xa)
            o_ref[:, w + lo:w + hi] = da.astype(BF16)
            dal_ref[:, lo:hi] += jnp.sum(pg * ea * _softplus(xa), axis=0, keepdims=True)
            ddt_ref[:, lo:hi] += jnp.sum(da, axis=0, keepdims=True)

    def row(cb):
        return pl.BlockSpec((tr, w), lambda i: (i, cb))

    vec = pl.BlockSpec((1, w), lambda i: (0, 0))
    return pl.pallas_call(
        body, name=name, grid=(s // tr,),
        in_specs=[row(4), row(5), vec, vec, row(0), row(0), pl.BlockSpec(memory_space=pl.ANY)],
        out_specs=[pl.BlockSpec((tr, 2 * w), lambda i: (i, 2)), vec, vec],
        out_shape=[jax.ShapeDtypeStruct(dprojx.shape, dprojx.dtype), jax.ShapeDtypeStruct((1, w), F32),
                   jax.ShapeDtypeStruct((1, w), F32)],
        input_output_aliases={6: 0}, compiler_params=_cp("arbitrary"),
    )(projx, projx, alog, dtb, dbeta, dg, dprojx)


def _gdn_tri():
    c = GDN_CHUNK
    i = lax.broadcasted_iota(jnp.int32, (c, c), 0)
    j = lax.broadcasted_iota(jnp.int32, (c, c), 1)
    return ((i >= j).astype(F32), (i <= j).astype(F32), i >= j, i > j, (i == j).astype(F32))


def _bdot(a, b, ca=2, cb=1, precision=None):
    return lax.dot_general(a, b, (((ca,), (cb,)), ((0,), (0,))), precision=precision, preferred_element_type=F32)


def _bmxu(a, b, ca=2, cb=1):
    return _bdot(a.astype(BF16), b.astype(BF16), ca, cb)


def _split(x):
    hi = x.astype(BF16)
    return hi, (x - hi.astype(F32)).astype(BF16)


def _bdot3(a, b, ca=2, cb=1):
    ah, al = _split(a)
    bh, bl = _split(b)
    return _bdot(ah, bh, ca, cb) + (_bdot(ah, bl, ca, cb) + _bdot(al, bh, ca, cb))


def _tri_dot(tri, x):
    t = tri.astype(BF16)
    hi = x.astype(BF16)
    r1 = x - hi.astype(F32)
    mid = r1.astype(BF16)
    lo = (r1 - mid.astype(F32)).astype(BF16)
    return _dot(t, hi) + (_dot(t, mid) + _dot(t, lo))


def _heads(x):
    return jnp.stack([x[:, h * GDN_DH:(h + 1) * GDN_DH] for h in range(GDN_HEADS)], axis=0)


def _unheads(x):
    return jnp.concatenate([x[h] for h in range(GDN_HEADS)], axis=1)


def _gdn_chunk(q, k, v, bb, g2d, tri):
    low, up, incl, strict, eye = tri
    c = GDN_CHUNK
    gc = _heads(_tri_dot(low, g2d))
    gci = gc[:, :, :c]
    gdiff = gci - jnp.swapaxes(gci, 1, 2)
    decay = jnp.where(incl, jnp.exp(jnp.where(incl, gdiff, 0.0)), 0.0)
    kb, vb = k * bb, v * bb
    kbk = _bmxu(kb, k, 2, 2)
    x = -jnp.where(strict, kbk * decay, 0.0)
    t = eye + x
    p = x
    for _ in range(c.bit_length() - 2):
        p = _bdot3(p, p)
        t = t + _bdot3(t, p)
    eg = jnp.exp(gc)
    kbg = kb * eg
    gcl = gc[:, c - 1:c, :]
    ek = jnp.exp(gcl - gc)
    qkraw = _bmxu(q, k, 2, 2)
    return dict(decay=decay, kb=kb, vb=vb, kbk=kbk, t=t, eg=eg, kbg=kbg, ek=ek, gl=jnp.exp(gcl),
                w=_bmxu(t, kbg), u=_bmxu(t, vb), qkraw=qkraw, qk=jnp.where(incl, qkraw * decay, 0.0),
                qd=q * eg, kd=k * ek)


def _gdn_specs(n_of):
    c, w = GDN_CHUNK, GDN_HEADS * GDN_DH

    def blk(cb, width=w):
        return pl.BlockSpec((c, width), lambda n: (n_of(n), cb))

    st = pl.BlockSpec((None, GDN_HEADS, GDN_DH, GDN_DH), lambda n: (n_of(n), 0, 0, 0))
    vec = pl.BlockSpec((1, GDN_DH), lambda n: (0, 0))
    return blk, st, vec


def _gdn_load(qkv_ref, b_ref, g_ref, tri):
    w = GDN_HEADS * GDN_DH
    q, k, v = _heads(qkv_ref[:, :w]), _heads(qkv_ref[:, w:2 * w]), _heads(qkv_ref[:, 2 * w:])
    bb = _heads(b_ref[...])
    return q, k, v, bb, _gdn_chunk(q, k, v, bb, g_ref[...], tri)


def _gdn_fwd(qkv, beta, g, projx, onorm, name):
    s = qkv.shape[0]
    nc = s // GDN_CHUNK
    w = GDN_HEADS * GDN_DH
    blk, st, vec = _gdn_specs(lambda n: n)

    def body(qkv_ref, b_ref, g_ref, z_ref, on_ref, o_ref, st_ref, state):
        @pl.when(pl.program_id(0) == 0)
        def _():
            state[...] = jnp.zeros_like(state)

        _, _, _, _, ch = _gdn_load(qkv_ref, b_ref, g_ref, _gdn_tri())
        sp = state[...]
        st_ref[...] = sp
        vn = ch["u"] - _bmxu(ch["w"], sp)
        o = _bmxu(ch["qd"], sp) + _bmxu(ch["qk"], vn)
        state[...] = sp * ch["gl"] + _bmxu(ch["kd"], vn, 1, 1)
        r = lax.rsqrt(jnp.mean(o * o, axis=-1, keepdims=True) + EPS)
        z = _heads(z_ref[...])
        o_ref[...] = _unheads(o * r * on_ref[...] * (z * _sigmoid(z))).astype(BF16)

    return pl.pallas_call(
        body, name=name, grid=(nc,),
        in_specs=[blk(0, 3 * w), blk(0), blk(0), blk(3), vec], out_specs=[blk(0), st],
        out_shape=[jax.ShapeDtypeStruct((s, w), BF16), jax.ShapeDtypeStruct((nc, GDN_HEADS, GDN_DH, GDN_DH), F32)],
        scratch_shapes=[pltpu.VMEM((GDN_HEADS, GDN_DH, GDN_DH), F32)],
        compiler_params=_cp("arbitrary"),
    )(qkv, beta, g, projx, onorm.reshape(1, -1))


def _gdn_bwd(qkv, beta, g, projx, onorm, states, dout, name):
    s = qkv.shape[0]
    c = GDN_CHUNK
    nc = s // c
    w = GDN_HEADS * GDN_DH
    blk, st, vec = _gdn_specs(lambda n: nc - 1 - n)

    def body(qkv_ref, b_ref, g_ref, z_ref, on_ref, st_ref, do_ref,
             dqkv_ref, db_ref, dg_ref, dz_ref, don_ref, carry):
        @pl.when(pl.program_id(0) == 0)
        def _():
            carry[...] = jnp.zeros_like(carry)
            don_ref[...] = jnp.zeros_like(don_ref)

        tri = _gdn_tri()
        low, up, incl, strict, eye = tri
        q, k, v, bb, ch = _gdn_load(qkv_ref, b_ref, g_ref, tri)
        sp = st_ref[...]
        vn = ch["u"] - _bmxu(ch["w"], sp)
        o = _bmxu(ch["qd"], sp) + _bmxu(ch["qk"], vn)
        r = lax.rsqrt(jnp.mean(o * o, axis=-1, keepdims=True) + EPS)
        orn = o * r
        z = _heads(z_ref[...])
        sg = _sigmoid(z)
        dout = _heads(do_ref[...])
        onw = on_ref[...]
        dz_ref[...] = _unheads(dout * orn * onw * (sg * (1.0 + z * (1.0 - sg)))).astype(BF16)
        don = dout * (z * sg)
        don_ref[...] += jnp.sum(jnp.sum(don * orn, axis=0), axis=0, keepdims=True)
        dor = don * onw
        do = r * (dor - orn * jnp.mean(dor * orn, axis=-1, keepdims=True))
        dsn = carry[...]
        dqd = _bmxu(do, sp, 2, 2)
        dqk = jnp.where(incl, _bmxu(do, vn, 2, 2), 0.0)
        dvn = _bmxu(ch["qk"], do, 1, 1) + _bmxu(ch["kd"], dsn)
        dkd = _bmxu(vn, dsn, 2, 2)
        dgl = jnp.sum(dsn * sp, axis=1, keepdims=True)
        dw = -_bmxu(dvn, sp, 2, 2)
        carry[...] = _bmxu(ch["qd"], do, 1, 1) + dsn * ch["gl"] - _bmxu(ch["w"], dvn, 1, 1)
        t = ch["t"]
        dvb = _bmxu(t, dvn, 1, 1)
        dkbg = _bmxu(t, dw, 1, 1)
        dt = _bmxu(dvn, ch["vb"], 2, 2) + _bmxu(dw, ch["kbg"], 2, 2)
        da = -_bdot3(_bdot3(t, dt, 1, 1), t, 2, 2)
        da = jnp.where(strict, da, 0.0)
        decay = ch["decay"]
        dkbk = da * decay
        dqkr = dqk * decay
        mdec = (da * ch["kbk"] + dqk * ch["qkraw"]) * decay
        dkb = _bmxu(dkbk, k) + dkbg * ch["eg"]
        dk = _bmxu(dkbk, ch["kb"], 1, 1) + _bmxu(dqkr, q, 1, 1) + dkd * ch["ek"] + dkb * bb
        dq = _bmxu(dqkr, k) + dqd * ch["eg"]
        tk = dkd * ch["kd"]
        dgcl = jnp.sum(tk, axis=1, keepdims=True) + dgl * ch["gl"]
        row = lax.broadcasted_iota(jnp.int32, (GDN_HEADS, c, GDN_DH), 1)
        zpad = jnp.zeros((GDN_HEADS, c, GDN_DH - c), F32)
        dgc = (jnp.concatenate([mdec, zpad], axis=2) - jnp.concatenate([jnp.swapaxes(mdec, 1, 2), zpad], axis=2)
               + dqd * ch["qd"] - tk + dkbg * ch["kbg"] + jnp.where(row == c - 1, dgcl, 0.0))
        dqkv_ref[:, :w] = _unheads(dq)
        dqkv_ref[:, w:2 * w] = _unheads(dk)
        dqkv_ref[:, 2 * w:] = _unheads(dvb * bb)
        db_ref[...] = _unheads(dkb * k + dvb * v)
        dg_ref[...] = _tri_dot(up, _unheads(dgc))

    return pl.pallas_call(
        body, name=name, grid=(nc,),
        in_specs=[blk(0, 3 * w), blk(0), blk(0), blk(3), vec, st, blk(0)],
        out_specs=[blk(0, 3 * w), blk(0), blk(0), blk(3), vec],
        out_shape=[jax.ShapeDtypeStruct((s, 3 * w), F32), jax.ShapeDtypeStruct((s, w), F32),
                   jax.ShapeDtypeStruct((s, w), F32), jax.ShapeDtypeStruct(projx.shape, BF16),
                   jax.ShapeDtypeStruct((1, GDN_DH), F32)],
        scratch_shapes=[pltpu.VMEM((GDN_HEADS, GDN_DH, GDN_DH), F32)],
        compiler_params=_cp("arbitrary"),
    )(qkv, beta, g, projx, onorm.reshape(1, -1), states, dout)


_WEIGHTS = (
    "l0_mix_norm", "l0_w_in", "l0_ret_norm", "l0_s5_lambda_re", "l0_s5_lambda_im", "l0_s5_b_re", "l0_s5_b_im",
    "l0_s5_c_re", "l0_s5_c_im", "l0_s5_d", "l0_s5_log_dt", "l0_s5_w_glu", "l0_s5_b_glu", "l0_w_out",
    "l0_xa_norm", "l0_mem_norm", "l0_xa_wq", "l0_xa_wkv", "l0_xa_wo", "l0_ffn_norm", "l0_ffn_w_up",
    "l0_ffn_conv", "l0_ffn_w_down", "l1_mix_norm", "l1_w_in", "l1_conv", "l1_a_log", "l1_dt_bias", "l1_o_norm",
    "l1_w_out", "l1_xa_norm", "l1_mem_norm", "l1_xa_wq", "l1_xa_wkv", "l1_xa_wo", "l1_ffn_norm", "l1_ffn_w_up",
    "l1_ffn_conv", "l1_ffn_w_down", "final_norm")
_INPUTS = ("x", "mem") + _WEIGHTS + ("loss_target",) + tuple("m_" + n for n in _WEIGHTS) + tuple("v_" + n for n in _WEIGHTS)

_COL = ("l0_w_in", "l0_xa_wkv", "l0_ffn_w_up", "l0_ffn_conv", "l1_w_in", "l1_conv", "l1_xa_wkv", "l1_ffn_w_up",
        "l1_ffn_conv")
_ROW = ("l0_s5_w_glu", "l0_w_out", "l0_xa_wq", "l0_xa_wo", "l0_ffn_w_down", "l1_w_out", "l1_xa_wq", "l1_xa_wo",
        "l1_ffn_w_down")
_F32_WIRE = ("l0_ffn_conv", "l1_conv", "l1_ffn_conv")
_REP = tuple(n for n in _WEIGHTS if n not in _COL + _ROW)
_GATHER_GROUPS = (("l0_w_in", "l0_s5_w_glu", "l0_w_out"),
                  ("l0_xa_wq", "l0_xa_wkv", "l0_xa_wo", "l0_ffn_w_up", "l0_ffn_conv", "l0_ffn_w_down"),
                  ("l1_w_in", "l1_conv", "l1_w_out", "l1_xa_wq", "l1_xa_wkv", "l1_xa_wo"),
                  ("l1_ffn_w_up", "l1_ffn_conv", "l1_ffn_w_down"))


def _round_up(n, m):
    return (n + m - 1) // m * m


_REP_BIG = ("l0_s5_lambda_re", "l0_s5_lambda_im", "l0_s5_b_re", "l0_s5_b_im", "l0_s5_c_re", "l0_s5_c_im", "l0_s5_d")
_REP_LAST = "l0_mix_norm"
_REP_SMALL = tuple(n for n in _REP if n not in _REP_BIG + (_REP_LAST,))
PACK_WIDTH = 1024


def _pack_rows(ts):
    rows = [jnp.pad(t, ((0, 0), (0, PACK_WIDTH - t.shape[1]))) for t in ts]
    rows.append(jnp.zeros((_round_up(len(ts), 8) - len(ts), PACK_WIDTH), F32))
    return jnp.concatenate(rows, axis=0)


def _s5_interleave(re, im):
    lead = re.shape[:-1]
    nt = re.shape[-1] // S5_TILE
    both = jnp.stack([re.reshape(lead + (nt, S5_TILE)), im.reshape(lead + (nt, S5_TILE))], axis=-2)
    return both.reshape(lead + (2 * re.shape[-1],))


def _s5_split(x):
    lead = x.shape[:-1]
    y = x.reshape(lead + (x.shape[-1] // (2 * S5_TILE), 2, S5_TILE))
    return y[..., 0, :].reshape(lead + (-1,)), y[..., 1, :].reshape(lead + (-1,))


def _s5_discretise(lr, li, log_dt, b_re, b_im):
    dt = jnp.exp(log_dt)[:, None]
    mag = jnp.exp(lr * dt)
    a_re = mag * jnp.cos(li * dt)
    a_im = mag * jnp.sin(li * dt)
    den = lr * lr + li * li
    z_re = ((a_re - 1.0) * lr + a_im * li) / den
    z_im = (a_im * lr - (a_re - 1.0) * li) / den
    bb_re = z_re[:, None, :] * b_re - z_im[:, None, :] * b_im
    bb_im = z_re[:, None, :] * b_im + z_im[:, None, :] * b_re
    return a_re, a_im, bb_re, bb_im


def _block_diag(b):
    g, r, c = b.shape
    return jnp.einsum("grc,gk->grkc", b, jnp.eye(g, dtype=b.dtype)).reshape(g * r, g * c)


def _block_diag_of(d, g):
    r, c = d.shape[0] // g, d.shape[1] // g
    return jnp.einsum("grkc,gk->grc", d.reshape(g, r, g, c), jnp.eye(g, dtype=d.dtype))


def kernel(*args):
    p = dict(zip(_INPUTS, args, strict=True))
    x0, mem0, tgt = p["x"][0], p["mem"][0], p["loss_target"][0]
    s, d = x0.shape
    me = _slot(*_mesh_pos())
    grads = {}
    wire = {n: (F32 if n in _F32_WIRE else BF16) for n in _COL + _ROW}

    zones = {n: _into_slot(p[n], wire[n], me, "place_" + n) for names in _GATHER_GROUPS for n in names}
    gather, pin = [], jnp.zeros((), F32)
    for i, names in enumerate(_GATHER_GROUPS):
        handle, token = _push_start([], [zones[n] for n in names], f"gather{i}_start")
        gather.append(handle)
        pin = pin + token[0, 0]
    w = {}

    def gathered(i, after):
        for n, full in zip(_GATHER_GROUPS[i], _push_wait(gather[i], after, f"gather{i}_wait")):
            if n in _COL:
                full = full.transpose(1, 0, 2)
            w[n] = full.reshape(-1, full.shape[-1]) if n in _ROW else full.reshape(full.shape[0], -1)

    pending = []

    def exchange(names, gain, tag):
        slots = []
        for n in names:
            g = grads[n]
            if n in _COL:
                g = g.reshape(g.shape[0], N_DEV, -1).transpose(1, 0, 2)
            else:
                g = g.reshape((N_DEV, -1) + g.shape[1:])
            slots.append(g.astype(wire[n]))
        handle, token = _push_start(slots, [], tag + "_start")
        pending.append((names, slots, handle, tag))
        return gain + token[0, 0]

    def xattn(pre, x_in):
        hx = _norm_fwd(x_in, p[pre + "xa_norm"], pre + "xa_norm_fwd")
        q = _mm(hx, w[pre + "xa_wq"], out_dtype=BF16, name=pre + "xa_q")
        memn = _norm_fwd(mem0, p[pre + "mem_norm"], pre + "mem_norm_fwd")
        kv = _mm(memn, w[pre + "xa_wkv"], out_dtype=BF16, name=pre + "xa_kv")
        ao = _xattn_fwd(q, kv, pre + "xattn_fwd")
        x_out = _mm(ao, w[pre + "xa_wo"], res=x_in, name=pre + "xa_o")
        return x_out, (x_in, hx, q, memn, kv, ao)

    def xattn_bwd(pre, saved, dxo):
        x_in, hx, q, memn, kv, ao = saved
        dao = _mm(dxo, w[pre + "xa_wo"], tb=True, name=pre + "xa_o_dx")
        grads[pre + "xa_wo"] = _mm(ao, dxo, ta=True, out_dtype=BF16, name=pre + "xa_o_dw")
        dq, dkv = _xattn_bwd(q, kv, dao, pre + "xattn_bwd")
        grads[pre + "xa_wq"] = _mm(hx, dq, ta=True, out_dtype=BF16, name=pre + "xa_q_dw")
        dhx = _mm(dq, w[pre + "xa_wq"], tb=True, name=pre + "xa_q_dx")
        grads[pre + "xa_wkv"] = _mm(memn, dkv, ta=True, out_dtype=BF16, name=pre + "xa_kv_dw")
        dmemn = _mm(dkv, w[pre + "xa_wkv"], tb=True, name=pre + "xa_kv_dx")
        gain = exchange((pre + "xa_wo", pre + "xa_wq", pre + "xa_wkv"), p[pre + "xa_norm"], pre + "xa_grads")
        dx_in, grads[pre + "xa_norm"] = _norm_bwd(x_in, gain, dhx, dxo, pre + "xa_norm_bwd")
        _, grads[pre + "mem_norm"] = _norm_bwd(mem0, p[pre + "mem_norm"], dmemn, jnp.zeros_like(mem0), pre + "mem_norm_bwd")
        return dx_in

    def ffn(pre, x_in):
        hf = _norm_fwd(x_in, p[pre + "ffn_norm"], pre + "ffn_norm_fwd")
        up = _mm(hf, w[pre + "ffn_w_up"], out_dtype=BF16, name=pre + "ffn_up")
        act = _ffn_act_fwd(up, w[pre + "ffn_conv"], pre + "ffn_act_fwd")
        x_out = _mm(act, w[pre + "ffn_w_down"], res=x_in, name=pre + "ffn_down")
        return x_out, (x_in, hf, up, act)

    def ffn_bwd(pre, saved, dxo):
        x_in, hf, up, act = saved
        dact = _mm(dxo, w[pre + "ffn_w_down"], tb=True, out_dtype=BF16, name=pre + "ffn_down_dx")
        grads[pre + "ffn_w_down"] = _mm(act, dxo, ta=True, out_dtype=BF16, name=pre + "ffn_down_dw")
        dpu, dpg, dcu, dcg = _ffn_act_bwd(up, w[pre + "ffn_conv"], dact, pre + "ffn_act_bwd")
        dup = jnp.concatenate([dpu, dpg], axis=1)
        grads[pre + "ffn_conv"] = jnp.concatenate([dcu, dcg], axis=1)
        dhf = _mm(dup, w[pre + "ffn_w_up"], tb=True, name=pre + "ffn_up_dx")
        grads[pre + "ffn_w_up"] = _mm(hf, dup, ta=True, out_dtype=BF16, name=pre + "ffn_up_dw")
        gain = exchange((pre + "ffn_w_down", pre + "ffn_w_up", pre + "ffn_conv"), p[pre + "ffn_norm"], pre + "ffn_grads")
        dx_in, grads[pre + "ffn_norm"] = _norm_bwd(x_in, gain, dhf, dxo, pre + "ffn_norm_bwd")
        return dx_in

    cos, sin = _rope_tables(s)
    (a_re, a_im, bb_re, bb_im), disc_vjp = jax.vjp(
        _s5_discretise, p["l0_s5_lambda_re"], p["l0_s5_lambda_im"], p["l0_s5_log_dt"], p["l0_s5_b_re"], p["l0_s5_b_im"])
    apow, apow_rev = _s5_pow_tables(_s5_interleave(a_re.reshape(1, -1), a_im.reshape(1, -1)), "l0_s5_pow_tables")
    bbt = _s5_tile_b(bb_re, bb_im).astype(BF16)
    cct = _s5_tile_c(p["l0_s5_c_re"], p["l0_s5_c_im"]).astype(BF16)
    s5_d = p["l0_s5_d"].reshape(1, -1)
    b_glu = p["l0_s5_b_glu"].reshape(1, -1)

    h0 = _norm_fwd(x0, p["l0_mix_norm"] + pin, "l0_mix_norm_fwd")
    gathered(0, h0)
    proj = _mm(h0, w["l0_w_in"], name="l0_in")
    o_ret, ret_states = _ret_fwd(proj, cos, sin, p["l0_ret_norm"], "l0_ret_fwd")
    st, y, gy = _s5_fwd(proj, bbt, cct, apow, s5_d, "l0_s5_fwd")
    z = _mm(gy, w["l0_s5_w_glu"], name="l0_s5_glu_mm")
    y2 = _s5_glu_fwd(y, z, b_glu, "l0_s5_glu_fwd")
    merged = jnp.concatenate([o_ret, y2], axis=1)
    x1 = _mm(merged, w["l0_w_out"], res=x0, name="l0_out")
    gathered(1, x1)
    x2, xa0 = xattn("l0_", x1)
    x3, ff0 = ffn("l0_", x2)

    gathered(2, x3)
    nqkv = 4 * GDN_HEADS * GDN_DH
    w1 = w["l1_w_in"]
    wx = jnp.concatenate([w1[:, :nqkv], jnp.repeat(w1[:, nqkv:nqkv + GDN_HEADS], GDN_DH, axis=1),
                          jnp.repeat(w1[:, nqkv + GDN_HEADS:], GDN_DH, axis=1)], axis=1)
    alog_x = jnp.repeat(p["l1_a_log"], GDN_DH).reshape(1, -1)
    dtb_x = jnp.repeat(p["l1_dt_bias"], GDN_DH).reshape(1, -1)
    h1 = _norm_fwd(x3, p["l1_mix_norm"], "l1_mix_norm_fwd")
    projx = _mm(h1, wx, name="l1_in")
    qkv = _gdn_conv_fwd(projx, w["l1_conv"], "l1_conv_fwd")
    beta, glog = _gdn_gates_fwd(projx, alog_x, dtb_x, "l1_gates_fwd")
    o_gdn, gdn_states = _gdn_fwd(qkv, beta, glog, projx, p["l1_o_norm"], "l1_gdn_fwd")
    x4 = _mm(o_gdn, w["l1_w_out"], res=x3, name="l1_out")
    x5, xa1 = xattn("l1_", x4)
    gathered(3, x5)
    x6, ff1 = ffn("l1_", x5)

    loss_part, dx6, grads["final_norm"] = _loss_head(x6, p["final_norm"], tgt, "loss_head")
    loss = lax.psum(loss_part[0, 0], ("x", "y", "c"))
    dx5 = ffn_bwd("l1_", ff1, dx6)
    dx4 = xattn_bwd("l1_", xa1, dx5)

    do_gdn = _mm(dx4, w["l1_w_out"], tb=True, name="l1_out_dx")
    grads["l1_w_out"] = _mm(o_gdn, dx4, ta=True, out_dtype=BF16, name="l1_out_dw")
    dqkv, dbeta, dglog, dprojx, grads["l1_o_norm"] = _gdn_bwd(
        qkv, beta, glog, projx, p["l1_o_norm"], gdn_states, do_gdn, "l1_gdn_bwd")
    dprojx, grads["l1_conv"] = _gdn_conv_bwd(projx, w["l1_conv"], dqkv, dprojx, "l1_conv_bwd")
    dprojx, dalog_x, ddtb_x = _gdn_gates_bwd(projx, alog_x, dtb_x, dbeta, dglog, dprojx, "l1_gates_bwd")
    dh1 = _mm(dprojx, wx, tb=True, name="l1_in_dx")
    dwx = _mm(h1, dprojx, ta=True, name="l1_in_dw")
    grads["l1_w_in"] = jnp.concatenate(
        [dwx[:, :nqkv], dwx[:, nqkv:nqkv + GDN_HEADS * GDN_DH].reshape(d, GDN_HEADS, GDN_DH).sum(-1),
         dwx[:, nqkv + GDN_HEADS * GDN_DH:].reshape(d, GDN_HEADS, GDN_DH).sum(-1)], axis=1)
    grads["l1_a_log"] = dalog_x.reshape(GDN_HEADS, GDN_DH).sum(-1)
    grads["l1_dt_bias"] = ddtb_x.reshape(GDN_HEADS, GDN_DH).sum(-1)
    gain = exchange(("l1_w_out", "l1_w_in", "l1_conv"), p["l1_mix_norm"], "l1_mix_grads")
    dx3, grads["l1_mix_norm"] = _norm_bwd(x3, gain, dh1, dx4, "l1_mix_norm_bwd")

    dx2 = ffn_bwd("l0_", ff0, dx3)
    dx1 = xattn_bwd("l0_", xa0, dx2)

    dmerged = _mm(dx1, w["l0_w_out"], tb=True, name="l0_out_dx")
    grads["l0_w_out"] = _mm(merged, dx1, ta=True, out_dtype=BF16, name="l0_out_dw")
    dproj, grads["l0_ret_norm"] = _ret_bwd(proj, cos, sin, p["l0_ret_norm"], ret_states, dmerged, "l0_ret_bwd")
    dzg, dg1, grads["l0_s5_b_glu"] = _s5_glu_bwd(dmerged, y, z, b_glu, "l0_s5_glu_bwd")
    grads["l0_s5_w_glu"] = _mm(gy, dzg, ta=True, out_dtype=BF16, name="l0_s5_glu_dw")
    dg2 = _mm(dzg, w["l0_s5_w_glu"], tb=True, name="l0_s5_glu_dx")
    dproj, da_s5, dbbt, dcct, grads["l0_s5_d"] = _s5_bwd(dg1, dg2, y, proj, st, bbt, cct, apow_rev, s5_d, dproj, "l0_s5_bwd")
    dbb_re, dbb_im = _s5_untile_b(dbbt)
    grads["l0_s5_c_re"], grads["l0_s5_c_im"] = _s5_untile_c(dcct)
    da_re, da_im = (t.reshape(S5_GROUPS, S5_STATE) for t in _s5_split(da_s5[0]))
    (grads["l0_s5_lambda_re"], grads["l0_s5_lambda_im"], grads["l0_s5_log_dt"], grads["l0_s5_b_re"],
     grads["l0_s5_b_im"]) = disc_vjp((da_re, da_im, dbb_re, dbb_im))

    def as_2d(t):
        return t.reshape(-1, t.shape[-1])

    def as_row(t):
        return t.reshape(1, -1)

    small_own = _pack_rows([as_row(grads[n]) for n in _REP_SMALL])
    big_own = [as_2d(grads[n].reshape(p[n].shape)) for n in _REP_BIG]
    rep_zones = [_into_slot(t, F32, me, f"place_rep{i}") for i, t in enumerate([small_own] + [t.reshape(-1, LANES) for t in big_own])]
    rep_handle, rep_token = _push_start([], rep_zones, "rep_grads_start")

    dh0 = _mm(dproj, w["l0_w_in"], tb=True, pin=rep_token, name="l0_in_dx")
    grads["l0_w_in"] = _mm(h0, dproj, ta=True, out_dtype=BF16, name="l0_in_dw")
    gain = exchange(("l0_w_out", "l0_s5_w_glu", "l0_w_in"), p["l0_mix_norm"], "l0_mix_grads")
    dx0, grads["l0_mix_norm"] = _norm_bwd(x0, gain, dh0, dx1, "l0_mix_norm_bwd")

    last_own = _pack_rows([as_row(grads[_REP_LAST])])
    last_handle, _ = _push_start([], [_into_slot(last_own, F32, me, "place_rep_last")], "rep_last_start")
    last_land, = _push_wait(last_handle, dx0, "rep_last_wait")
    rep_lands = _push_wait(rep_handle, last_land, "rep_grads_wait")
    rep_land = rep_lands[0]

    outs = {}
    kinds = ("grad_", "delta_", "new_m_", "new_v_")
    for names, slots, handle, tag in pending:
        for n, own_slots, land in zip(names, slots, _push_wait(handle, rep_land, tag + "_wait")):
            shape = p[n].shape
            own = lax.dynamic_index_in_dim(own_slots, me, 0, keepdims=False)
            res = _adamw(land, own, *(p[pre + n].reshape(own.shape) for pre in ("", "m_", "v_")), "adamw_" + n)
            for kind, t in zip(kinds, res):
                outs[kind + n] = t.reshape(shape)
    for n, own, land in zip(_REP_BIG, big_own, rep_lands[1:]):
        res = _adamw(land.reshape((N_DEV,) + own.shape), own, *(as_2d(p[pre + n]) for pre in ("", "m_", "v_")), "adamw_" + n)
        for kind, t in zip(kinds, res):
            outs[kind + n] = t.reshape(p[n].shape)
    for names, land, own, nm in ((_REP_SMALL, rep_land, small_own, "adamw_small"), ((_REP_LAST,), last_land, last_own, "adamw_last")):
        res = _adamw_rows(land, own, *([as_row(p[pre + n]) for n in names] for pre in ("", "m_", "v_")), nm)
        for j, kind in enumerate(kinds):
            for i, n in enumerate(names):
                outs[kind + n] = res[j * len(names) + i].reshape(p[n].shape)

    return (loss, dx0[None]) + tuple(outs[kind + n] for kind in kinds for n in _WEIGHTS)
```

```python
import functools
import math

import numpy as np
import jax
import jax.numpy as jnp
from jax import lax
from jax.experimental import pallas as pl
from jax.experimental.pallas import tpu as pltpu

F32 = jnp.float32
BF16 = jnp.bfloat16
EPS = 1e-6
N_DEV = 8
LANES = 128
VMEM_LIMIT = 48 * 1024 * 1024
HI = lax.Precision.HIGHEST

RET_HEADS, RET_DH, RET_CHUNK = 4, 128, 128
S5_GROUPS, S5_GROUP, S5_STATE = 32, 16, 64
GDN_HEADS, GDN_DH, GDN_CHUNK, GDN_CONV = 8, 128, 64, 4
XA_HEADS, XA_DH = 4, 256
FFN_CONV = 3
SCAN_ROWS = 256

ADAM_LR, ADAM_B1, ADAM_B2, ADAM_EPS, ADAM_WD, ADAM_STEP = 0.001, 0.9, 0.999, 1e-08, 0.01, 10


def _cp(*sem):
    return pltpu.CompilerParams(dimension_semantics=sem if sem else None, vmem_limit_bytes=VMEM_LIMIT)


def _tile(n, cap):
    if n <= cap:
        return n
    best = None
    for t in range(LANES, cap + 1, LANES):
        if n % t == 0:
            best = t
    assert best is not None, n
    return best


def _dot(a, b, ca=1, cb=0, precision=None):
    return lax.dot_general(a, b, (((ca,), (cb,)), ((), ())), precision=precision, preferred_element_type=F32)


def _mxu(a, b, ca=1, cb=0):
    return _dot(a.astype(BF16), b.astype(BF16), ca, cb)


def _sigmoid(x):
    return 1.0 / (1.0 + jnp.exp(-x))


def _shift_down(x, k):
    r = pltpu.roll(x, k, 0)
    row = lax.broadcasted_iota(jnp.int32, (8,) + x.shape[1:], 0)
    return jnp.concatenate([jnp.where(row >= k, r[:8], 0.0), r[8:]], axis=0)


def _shift_up(x, k):
    n = x.shape[0]
    r = pltpu.roll(x, n - k, 0)
    row = lax.broadcasted_iota(jnp.int32, (8,) + x.shape[1:], 0)
    return jnp.concatenate([r[:n - 8], jnp.where(row < 8 - k, r[n - 8:], 0.0)], axis=0)


def _mesh_pos():
    return lax.axis_index("x"), lax.axis_index("y"), lax.axis_index("c")


def _slot(px, py, pc):
    return 4 * px + 2 * py + pc


def _all_peers(x, y, c):
    flips = [(fx, fy, fc) for fx in (0, 1) for fy in (0, 1) for fc in (0, 1)][1:]
    return [(1 - x if fx else x, 1 - y if fy else y, 1 - c if fc else c) for fx, fy, fc in flips]


_HBM = pl.BlockSpec(memory_space=pltpu.HBM)
_SEM = pl.BlockSpec(memory_space=pltpu.SEMAPHORE)
N_PEERS = N_DEV - 1


def _push_copies(srcs, lands, send_sems, recv_sems, start):
    x, y, c = _mesh_pos()
    me = _slot(x, y, c)
    out = []
    for k, to in enumerate(_all_peers(x, y, c)):
        for a in range(len(lands)):
            src = srcs[a].at[_slot(*to)] if a < len(srcs) else lands[a].at[me]
            dst = lands[a].at[me if start else _slot(*to)]
            out.append(pltpu.make_async_remote_copy(
                src_ref=src, dst_ref=dst, send_sem=send_sems.at[a * N_PEERS + k], recv_sem=recv_sems.at[a * N_PEERS + k],
                device_id=to, device_id_type=pl.DeviceIdType.MESH))
    return out


def _into_slot(x, dtype, me, name):
    r, c = x.shape
    cap = max(16, 512 * 1024 // c)
    tr = max(t for t in range(16, min(r, cap) + 1, 16) if r % t == 0) if r % 16 == 0 else r

    def body(me_ref, x_ref, o_ref):
        o_ref[...] = x_ref[...].astype(dtype)

    return pl.pallas_call(
        body, name=name, out_shape=jax.ShapeDtypeStruct((N_DEV, r, c), dtype),
        grid_spec=pltpu.PrefetchScalarGridSpec(
            num_scalar_prefetch=1, grid=(r // tr,),
            in_specs=[pl.BlockSpec((tr, c), lambda i, me_ref: (i, 0))],
            out_specs=pl.BlockSpec((None, tr, c), lambda i, me_ref: (me_ref[0], i, 0))),
        compiler_params=_cp("parallel"),
    )(me.reshape(1).astype(jnp.int32), x)


def _push_start(scatter, gather_lands, name):
    ns, n = len(scatter), len(scatter) + len(gather_lands)
    lands = [lax.empty(a.shape, a.dtype) for a in scatter] + list(gather_lands)

    def body(*refs):
        srcs, zones = refs[:ns], refs[ns:ns + n]
        for cp in _push_copies(srcs, zones, refs[ns + n], refs[ns + n + 1], True):
            cp.start()
        refs[-1][...] = jnp.zeros((8, LANES), F32)

    hbm_in = [pltpu.with_memory_space_constraint(a, pltpu.HBM) for a in list(scatter) + lands]
    res = pl.pallas_call(
        body, name=name,
        out_shape=(pltpu.SemaphoreType.DMA((n * N_PEERS,)), pltpu.SemaphoreType.DMA((n * N_PEERS,)))
        + tuple(pltpu.HBM(a.shape, a.dtype) for a in list(scatter) + lands)
        + (jax.ShapeDtypeStruct((8, LANES), F32),),
        in_specs=[_HBM] * (ns + n),
        out_specs=(_SEM, _SEM) + (_HBM,) * (ns + n) + (pl.BlockSpec(memory_space=pltpu.VMEM),),
        input_output_aliases={i: 2 + i for i in range(ns + n)},
        compiler_params=pltpu.CompilerParams(has_side_effects=pltpu.SideEffectType.DATAFLOW_SIDE_EFFECTING),
    )(*hbm_in)
    return (res[0], res[1], res[2:2 + ns], res[2 + ns:2 + ns + n]), res[-1]


def _push_wait(handle, after, name):
    send_sems, recv_sems, srcs, lands = handle
    ns, n = len(srcs), len(lands)

    def body(*refs):
        for cp in _push_copies(refs[:ns], refs[ns:ns + n], refs[ns + n], refs[ns + n + 1], False):
            cp.wait_send()
            cp.wait_recv()

    res = pl.pallas_call(
        body, name=name,
        out_shape=tuple(pltpu.HBM(a.shape, a.dtype) for a in list(srcs) + list(lands)),
        in_specs=[_HBM] * (ns + n) + [_SEM, _SEM, pl.BlockSpec(memory_space=pl.ANY)],
        out_specs=(_HBM,) * (ns + n),
        input_output_aliases={i: i for i in range(ns + n)},
        compiler_params=pltpu.CompilerParams(has_side_effects=pltpu.SideEffectType.DATAFLOW_SIDE_EFFECTING),
    )(*srcs, *lands, send_sems, recv_sems, after)
    return res[ns:]


def _mm(a, b, *, ta=False, tb=False, out_dtype=F32, res=None, pin=None, name="mm"):
    m, k = (a.shape[1], a.shape[0]) if ta else a.shape
    n = b.shape[0] if tb else b.shape[1]
    assert k == (b.shape[1] if tb else b.shape[0]), (a.shape, b.shape, ta, tb)
    tm, tn, tk = _tile(m, 1408), _tile(n, 1536), _tile(k, 1408)
    nk = k // tk
    has_res = res is not None
    n_in = 2 + has_res + (pin is not None)

    def body(*refs):
        a_ref, b_ref = refs[:2]
        r_ref = refs[2] if has_res else None
        o_ref = refs[n_in]
        part = _mxu(a_ref[...], b_ref[...], 0 if ta else 1, 1 if tb else 0)

        def finish(r):
            if has_res:
                r = r + r_ref[...].astype(F32)
            o_ref[...] = r.astype(out_dtype)

        if nk == 1:
            finish(part)
            return
        acc = refs[-1]
        kk = pl.program_id(2)

        @pl.when(kk == 0)
        def _():
            acc[...] = part

        @pl.when(kk > 0)
        def _():
            acc[...] += part

        @pl.when(kk == nk - 1)
        def _():
            finish(acc[...])

    a_spec = pl.BlockSpec((tk, tm), lambda i, j, kk: (kk, i)) if ta else pl.BlockSpec((tm, tk), lambda i, j, kk: (i, kk))
    b_spec = pl.BlockSpec((tn, tk), lambda i, j, kk: (j, kk)) if tb else pl.BlockSpec((tk, tn), lambda i, j, kk: (kk, j))
    o_spec = pl.BlockSpec((tm, tn), lambda i, j, kk: (i, j))
    in_specs = [a_spec, b_spec] + ([o_spec] if has_res else [])
    args = (a, b) + ((res,) if has_res else ())
    if pin is not None:
        in_specs.append(pl.BlockSpec(pin.shape, lambda i, j, kk: (0, 0)))
        args += (pin,)
    return pl.pallas_call(
        body, name=name, grid=(m // tm, n // tn, nk), in_specs=in_specs, out_specs=o_spec,
        out_shape=jax.ShapeDtypeStruct((m, n), out_dtype),
        scratch_shapes=[pltpu.VMEM((tm, tn), F32)] if nk > 1 else [],
        compiler_params=_cp("parallel", "parallel", "arbitrary"),
    )(*args)


def _mm_norm_bwd(dy, w, x, g, dres, name, pin=None):
    s, k = dy.shape
    d = w.shape[0]
    tm, tk = min(512, s), _tile(k, 1408)
    nk = k // tk
    n_in = 5 + (pin is not None)

    def body(*refs):
        dy_ref, w_ref, x_ref, g_ref, dres_ref = refs[:5]
        dx_ref, dg_ref = refs[n_in], refs[n_in + 1]
        i, kk = pl.program_id(0), pl.program_id(1)

        @pl.when((i == 0) & (kk == 0))
        def _():
            dg_ref[...] = jnp.zeros_like(dg_ref)

        part = _mxu(dy_ref[...], w_ref[...], 1, 1)

        def finish(dh):
            xv = x_ref[...]
            r = lax.rsqrt(jnp.mean(xv * xv, axis=-1, keepdims=True) + EPS)
            xn = xv * r
            dg_ref[...] += jnp.sum(dh * xn, axis=0, keepdims=True)
            dhg = dh * g_ref[...]
            dx_ref[...] = dres_ref[...] + r * (dhg - xn * jnp.mean(dhg * xn, axis=-1, keepdims=True))

        if nk == 1:
            finish(part)
            return
        acc = refs[-1]

        @pl.when(kk == 0)
        def _():
            acc[...] = part

        @pl.when(kk > 0)
        def _():
            acc[...] += part

        @pl.when(kk == nk - 1)
        def _():
            finish(acc[...])

    row = pl.BlockSpec((tm, d), lambda i, kk: (i, 0))
    vec = pl.BlockSpec((1, d), lambda i, kk: (0, 0))
    in_specs = [pl.BlockSpec((tm, tk), lambda i, kk: (i, kk)), pl.BlockSpec((d, tk), lambda i, kk: (0, kk)), row, vec, row]
    args = (dy, w, x, g.reshape(1, d), dres)
    if pin is not None:
        in_specs.append(pl.BlockSpec(pin.shape, lambda i, kk: (0, 0)))
        args += (pin,)
    return pl.pallas_call(
        body, name=name, grid=(s // tm, nk), in_specs=in_specs, out_specs=[row, vec],
        out_shape=[jax.ShapeDtypeStruct((s, d), F32), jax.ShapeDtypeStruct((1, d), F32)],
        scratch_shapes=[pltpu.VMEM((tm, d), F32)] if nk > 1 else [],
        compiler_params=_cp("arbitrary", "arbitrary"),
    )(*args)


def _norm_fwd(x, g, name):
    s, d = x.shape
    tr = min(512, s)

    def body(x_ref, g_ref, o_ref):
        xv = x_ref[...]
        r = lax.rsqrt(jnp.mean(xv * xv, axis=-1, keepdims=True) + EPS)
        o_ref[...] = (xv * r * g_ref[...]).astype(BF16)

    row = pl.BlockSpec((tr, d), lambda i: (i, 0))
    return pl.pallas_call(
        body, name=name, grid=(s // tr,), in_specs=[row, pl.BlockSpec((1, d), lambda i: (0, 0))],
        out_specs=row, out_shape=jax.ShapeDtypeStruct((s, d), BF16), compiler_params=_cp("parallel"),
    )(x, g.reshape(1, d))


def _norm_bwd(x, g, dh, dres, name):
    s, d = x.shape
    tr = min(512, s)

    def body(x_ref, g_ref, dh_ref, dres_ref, dx_ref, dg_ref):
        @pl.when(pl.program_id(0) == 0)
        def _():
            dg_ref[...] = jnp.zeros_like(dg_ref)

        xv = x_ref[...]
        r = lax.rsqrt(jnp.mean(xv * xv, axis=-1, keepdims=True) + EPS)
        xn = xv * r
        dhv = dh_ref[...].astype(F32)
        dg_ref[...] += jnp.sum(dhv * xn, axis=0, keepdims=True)
        dhg = dhv * g_ref[...]
        dx_ref[...] = dres_ref[...] + r * (dhg - xn * jnp.mean(dhg * xn, axis=-1, keepdims=True))

    row = pl.BlockSpec((tr, d), lambda i: (i, 0))
    vec = pl.BlockSpec((1, d), lambda i: (0, 0))
    return pl.pallas_call(
        body, name=name, grid=(s // tr,), in_specs=[row, vec, row, row], out_specs=[row, vec],
        out_shape=[jax.ShapeDtypeStruct((s, d), F32), jax.ShapeDtypeStruct((1, d), F32)],
        compiler_params=_cp("arbitrary"),
    )(x, g.reshape(1, d), dh, dres)


def _loss_head(x, g, tgt, name):
    s, d = x.shape
    tr = min(512, s)

    def body(x_ref, g_ref, t_ref, l_ref, dx_ref, dg_ref):
        @pl.when(pl.program_id(0) == 0)
        def _():
            dg_ref[...] = jnp.zeros_like(dg_ref)
            l_ref[...] = jnp.zeros_like(l_ref)

        xv = x_ref[...]
        r = lax.rsqrt(jnp.mean(xv * xv, axis=-1, keepdims=True) + EPS)
        xn = xv * r
        err = xn * g_ref[...] - t_ref[...]
        part = 0.5 * jnp.sum(jnp.mean(err * err, axis=-1, keepdims=True), axis=0, keepdims=True)
        l_ref[...] += jnp.broadcast_to(part, l_ref.shape)
        dy = err * (1.0 / d)
        dg_ref[...] += jnp.sum(dy * xn, axis=0, keepdims=True)
        dyg = dy * g_ref[...]
        dx_ref[...] = r * (dyg - xn * jnp.mean(dyg * xn, axis=-1, keepdims=True))

    row = pl.BlockSpec((tr, d), lambda i: (i, 0))
    vec = pl.BlockSpec((1, d), lambda i: (0, 0))
    return pl.pallas_call(
        body, name=name, grid=(s // tr,), in_specs=[row, vec, row],
        out_specs=[pl.BlockSpec((1, LANES), lambda i: (0, 0)), row, vec],
        out_shape=[jax.ShapeDtypeStruct((1, LANES), F32), jax.ShapeDtypeStruct((s, d), F32),
                   jax.ShapeDtypeStruct((1, d), F32)],
        compiler_params=_cp("arbitrary"),
    )(x, g.reshape(1, d), tgt)


def _sum_slots(landed_slot, own):
    me = _slot(*_mesh_pos())
    mine = own.astype(F32)
    g = jnp.where(me == 0, mine, landed_slot(0).astype(F32))
    for i in range(1, N_DEV):
        g = g + jnp.where(me == i, mine, landed_slot(i).astype(F32))
    return g


def _adam_update(g, w, m, v):
    mm = ADAM_B1 * m + (1.0 - ADAM_B1) * g
    vv = ADAM_B2 * v + (1.0 - ADAM_B2) * (g * g)
    m_hat = mm / (1.0 - ADAM_B1 ** ADAM_STEP)
    v_hat = vv / (1.0 - ADAM_B2 ** ADAM_STEP)
    return g, -ADAM_LR * (m_hat / (jnp.sqrt(v_hat) + ADAM_EPS) + ADAM_WD * w), mm, vv


def _adamw_rows(landed, own, ws, ms, vs, name):
    k = len(ws)
    sizes = [w.shape[1] for w in ws]

    def body(*refs):
        p_ref, o_ref = refs[:2]
        w_refs, m_refs, v_refs = refs[2:2 + k], refs[2 + k:2 + 2 * k], refs[2 + 2 * k:2 + 3 * k]
        outs = refs[2 + 3 * k:]
        for i, n in enumerate(sizes):
            g = _sum_slots(lambda s: p_ref[s, i:i + 1, :n], o_ref[i:i + 1, :n])
            res = _adam_update(g, w_refs[i][...], m_refs[i][...], v_refs[i][...])
            for j in range(4):
                outs[j * k + i][...] = res[j]

    return pl.pallas_call(
        body, name=name, out_shape=[jax.ShapeDtypeStruct((1, n), F32) for _ in range(4) for n in sizes],
    )(landed, own, *ws, *ms, *vs)


def _adamw(landed, own, w, m, v, name):
    r, c = w.shape
    cap = max(8, 256 * 1024 // c)
    tr = max(t for t in range(8, min(r, cap) + 1, 8) if r % t == 0) if r % 8 == 0 else r
    gathered = own is None

    def body(*refs):
        p_ref = refs[0]
        w_ref, m_ref, v_ref, g_ref, d_ref, nm_ref, nv_ref = refs[1 if gathered else 2:]
        if gathered:
            g = p_ref[0].astype(F32)
            for i in range(1, N_DEV):
                g = g + p_ref[i].astype(F32)
        else:
            g = _sum_slots(lambda i: p_ref[i], refs[1][...])
        g_ref[...], d_ref[...], nm_ref[...], nv_ref[...] = _adam_update(g, w_ref[...], m_ref[...], v_ref[...])

    blk = pl.BlockSpec((tr, c), lambda i: (i, 0))
    n_blk = 3 if gathered else 4
    return pl.pallas_call(
        body, name=name, grid=(r // tr,),
        in_specs=[pl.BlockSpec((N_DEV, tr, c), lambda i: (0, i, 0))] + [blk] * n_blk,
        out_specs=[blk] * 4, out_shape=[jax.ShapeDtypeStruct((r, c), F32)] * 4,
        compiler_params=_cp("parallel"),
    )(*((landed,) if gathered else (landed, own)), w, m, v)


def _conv_taps(x, kw):
    return [_shift_down(x, kw - 1 - j) for j in range(kw - 1)] + [x]


def _conv_fwd(taps, w_ref):
    acc = w_ref[0:1, :] * taps[0]
    for j in range(1, len(taps)):
        acc = acc + w_ref[j:j + 1, :] * taps[j]
    return acc


def _conv_bwd(taps, dy, w_ref, dw_ref):
    kw = len(taps)
    dx = w_ref[kw - 1:kw, :] * dy
    for j in range(kw):
        dw_ref[j:j + 1, :] = jnp.sum(dy * taps[j], axis=0, keepdims=True)
        if j < kw - 1:
            dx = dx + w_ref[j:j + 1, :] * _shift_up(dy, kw - 1 - j)
    return dx


def _ffn_act_fwd(pre, cw, name):
    s, f2 = pre.shape
    nt = f2 // 2 // LANES

    def body(pu_ref, pg_ref, wu_ref, wg_ref, o_ref):
        up = _conv_fwd(_conv_taps(pu_ref[...].astype(F32), FFN_CONV), wu_ref)
        gate = _conv_fwd(_conv_taps(pg_ref[...].astype(F32), FFN_CONV), wg_ref)
        o_ref[...] = (gate * _sigmoid(gate) * up).astype(BF16)

    def col(rows, off):
        return pl.BlockSpec((rows, LANES), lambda j: (0, j + off))

    return pl.pallas_call(
        body, name=name, grid=(nt,),
        in_specs=[col(s, 0), col(s, nt), col(FFN_CONV, 0), col(FFN_CONV, nt)], out_specs=col(s, 0),
        out_shape=jax.ShapeDtypeStruct((s, f2 // 2), BF16), compiler_params=_cp("parallel"),
    )(pre, pre, cw, cw)


def _ffn_act_bwd(pre, cw, dact, name):
    s, f2 = pre.shape
    f = f2 // 2
    nt = f // LANES

    def body(pu_ref, pg_ref, wu_ref, wg_ref, da_ref, dpu_ref, dpg_ref, dwu_ref, dwg_ref):
        pu, pg = pu_ref[...].astype(F32), pg_ref[...].astype(F32)
        tu, tg = _conv_taps(pu, FFN_CONV), _conv_taps(pg, FFN_CONV)
        up = _conv_fwd(tu, wu_ref)
        gate = _conv_fwd(tg, wg_ref)
        sg = _sigmoid(gate)
        da = da_ref[...].astype(F32)
        dup = da * gate * sg
        dgate = da * up * (sg * (1.0 + gate * (1.0 - sg)))
        dpu_ref[...] = _conv_bwd(tu, dup, wu_ref, dwu_ref).astype(BF16)
        dpg_ref[...] = _conv_bwd(tg, dgate, wg_ref, dwg_ref).astype(BF16)

    def col(rows, off):
        return pl.BlockSpec((rows, LANES), lambda j: (0, j + off))

    return pl.pallas_call(
        body, name=name, grid=(nt,),
        in_specs=[col(s, 0), col(s, nt), col(FFN_CONV, 0), col(FFN_CONV, nt), col(s, 0)],
        out_specs=[col(s, 0), col(s, 0), col(FFN_CONV, 0), col(FFN_CONV, 0)],
        out_shape=[jax.ShapeDtypeStruct((s, f), BF16), jax.ShapeDtypeStruct((s, f), BF16),
                   jax.ShapeDtypeStruct((FFN_CONV, f), F32), jax.ShapeDtypeStruct((FFN_CONV, f), F32)],
        compiler_params=_cp("parallel"),
    )(pre, pre, cw, cw, dact)


def _xa_probs(qh, kh):
    sc = _mxu(qh, kh, 1, 1) * (XA_DH ** -0.5)
    e = jnp.exp(sc - jnp.max(sc, axis=-1, keepdims=True))
    return e / jnp.sum(e, axis=-1, keepdims=True)


def _xattn_fwd(q, kv, name):
    s, d = q.shape
    m = kv.shape[0]
    tr = min(512, s)

    def body(q_ref, kv_ref, o_ref):
        for h in range(XA_HEADS):
            lo, hi = h * XA_DH, (h + 1) * XA_DH
            p = _xa_probs(q_ref[:, lo:hi], kv_ref[:, lo:hi])
            o_ref[:, lo:hi] = _mxu(p, kv_ref[:, d + lo:d + hi]).astype(BF16)

    row = pl.BlockSpec((tr, d), lambda i: (i, 0))
    return pl.pallas_call(
        body, name=name, grid=(s // tr,), in_specs=[row, pl.BlockSpec((m, 2 * d), lambda i: (0, 0))],
        out_specs=row, out_shape=jax.ShapeDtypeStruct((s, d), BF16), compiler_params=_cp("parallel"),
    )(q, kv)


def _xattn_bwd(q, kv, do, name):
    s, d = q.shape
    m = kv.shape[0]
    tr = min(512, s)

    def body(q_ref, kv_ref, do_ref, dq_ref, dkv_ref):
        @pl.when(pl.program_id(0) == 0)
        def _():
            dkv_ref[...] = jnp.zeros_like(dkv_ref)

        for h in range(XA_HEADS):
            lo, hi = h * XA_DH, (h + 1) * XA_DH
            qh, kh, vh = q_ref[:, lo:hi], kv_ref[:, lo:hi], kv_ref[:, d + lo:d + hi]
            doh = do_ref[:, lo:hi]
            p = _xa_probs(qh, kh)
            dp = _mxu(doh, vh, 1, 1)
            ds = p * (dp - jnp.sum(p * dp, axis=-1, keepdims=True)) * (XA_DH ** -0.5)
            dq_ref[:, lo:hi] = _mxu(ds, kh).astype(BF16)
            dkv_ref[:, lo:hi] += _mxu(ds, qh, 0, 0)
            dkv_ref[:, d + lo:d + hi] += _mxu(p, doh, 0, 0)

    row = pl.BlockSpec((tr, d), lambda i: (i, 0))
    full = pl.BlockSpec((m, 2 * d), lambda i: (0, 0))
    return pl.pallas_call(
        body, name=name, grid=(s // tr,), in_specs=[row, full, row], out_specs=[row, full],
        out_shape=[jax.ShapeDtypeStruct((s, d), BF16), jax.ShapeDtypeStruct((m, 2 * d), F32)],
        compiler_params=_cp("arbitrary"),
    )(q, kv, do)


def _ret_tables():
    c = RET_CHUNK
    lg = np.log1p(-np.exp2(-5.0 - np.arange(RET_HEADS, dtype=np.float32))).astype(np.float32)
    idx = np.arange(c, dtype=np.float32)
    diff = idx[:, None] - idx[None, :]
    intra = np.where(diff >= 0, np.exp(lg[:, None, None] * np.where(diff >= 0, diff, 0.0)), 0.0)
    rk = np.broadcast_to(np.exp(lg[:, None] * (c - 1 - idx))[:, :, None], (RET_HEADS, c, LANES))
    rq = np.broadcast_to(np.exp(lg[:, None] * (idx + 1))[:, :, None], (RET_HEADS, c, LANES))
    return jnp.asarray(np.stack([intra, rk, rq], axis=1).astype(np.float32))


def _rope_tables(s):
    half = RET_DH // 2
    inv = jnp.exp(-math.log(10000.0) * jnp.arange(half, dtype=F32) / half)
    ang = jnp.arange(s, dtype=F32)[:, None] * inv[None, :]
    cos, sin = jnp.cos(ang), jnp.sin(ang)
    return jnp.concatenate([cos, cos], axis=1), jnp.concatenate([-sin, sin], axis=1)


def _ret_specs(n_of):
    c, w = RET_CHUNK, RET_HEADS * RET_DH

    def part(off):
        return pl.BlockSpec((c, w), lambda n: (n_of(n), off))

    pos = pl.BlockSpec((c, RET_DH), lambda n: (n_of(n), 0))
    gain = pl.BlockSpec((1, w), lambda n: (0, 0))
    tab = pl.BlockSpec((RET_HEADS, 3, c, LANES), lambda n: (0, 0, 0, 0))
    st = pl.BlockSpec((RET_HEADS, None, RET_DH, RET_DH), lambda n: (0, n_of(n), 0, 0))
    return part, pos, gain, tab, st


def _rheads(x):
    return jnp.stack([x[:, h * RET_DH:(h + 1) * RET_DH] for h in range(RET_HEADS)], axis=0)


def _runheads(x):
    return jnp.concatenate([x[h] for h in range(RET_HEADS)], axis=1)


def _rope(x, cos, sin):
    return x * cos + pltpu.roll(x, RET_DH // 2, 2) * sin


def _ret_chunk(q_ref, k_ref, v_ref, cos_ref, sin_ref, tab_ref, prev):
    cos, sin = cos_ref[...], sin_ref[...]
    q = _rope(_rheads(q_ref[...]), cos, sin)
    k = _rope(_rheads(k_ref[...]), cos, sin) * (RET_DH ** -0.5)
    v = _rheads(v_ref[...])
    scores = _bmxu(q, k, 2, 2) * tab_ref[:, 0]
    qdec = q * tab_ref[:, 2]
    kdec = k * tab_ref[:, 1]
    o = _bmxu(scores, v) + _bmxu(qdec, prev)
    return q, k, v, scores, qdec, kdec, o


def _ret_fwd(proj, cos, sin, gain, name):
    s = proj.shape[0]
    c = RET_CHUNK
    nc = s // c
    part, pos, gvec, tab, st = _ret_specs(lambda n: n)

    def body(q_ref, k_ref, v_ref, g_ref, cos_ref, sin_ref, rn_ref, tab_ref, o_ref, st_ref, state):
        @pl.when(pl.program_id(0) == 0)
        def _():
            state[...] = jnp.zeros_like(state)

        prev = state[...]
        st_ref[...] = prev
        _, _, v, _, _, kdec, o = _ret_chunk(q_ref, k_ref, v_ref, cos_ref, sin_ref, tab_ref, prev)
        state[...] = prev * tab_ref[:, 2, c - 1:c, :] + _bmxu(kdec, v, 1, 1)
        r = lax.rsqrt(jnp.mean(o * o, axis=-1, keepdims=True) + EPS)
        gate = g_ref[...]
        o_ref[...] = (_runheads(o * r) * rn_ref[...] * (gate * _sigmoid(gate))).astype(BF16)

    return pl.pallas_call(
        body, name=name, grid=(nc,),
        in_specs=[part(0), part(1), part(2), part(3), pos, pos, gvec, tab],
        out_specs=[part(0), st],
        out_shape=[jax.ShapeDtypeStruct((s, RET_HEADS * RET_DH), BF16),
                   jax.ShapeDtypeStruct((RET_HEADS, nc, RET_DH, RET_DH), F32)],
        scratch_shapes=[pltpu.VMEM((RET_HEADS, RET_DH, RET_DH), F32)],
        compiler_params=_cp("arbitrary"),
    )(proj, proj, proj, proj, cos, sin, gain.reshape(1, -1), _ret_tables())


def _ret_bwd(proj, cos, sin, gain, states, dmerged, name):
    s = proj.shape[0]
    c = RET_CHUNK
    nc = s // c
    width = RET_HEADS * RET_DH
    part, pos, gvec, tab, st = _ret_specs(lambda n: nc - 1 - n)

    def body(q_ref, k_ref, v_ref, g_ref, cos_ref, sin_ref, rn_ref, tab_ref, st_ref, do_ref,
             dp_ref, drn_ref, carry):
        @pl.when(pl.program_id(0) == 0)
        def _():
            carry[...] = jnp.zeros_like(carry)
            drn_ref[...] = jnp.zeros_like(drn_ref)

        prev = st_ref[...]
        q, k, v, scores, qdec, kdec, o = _ret_chunk(q_ref, k_ref, v_ref, cos_ref, sin_ref, tab_ref, prev)
        r = lax.rsqrt(jnp.mean(o * o, axis=-1, keepdims=True) + EPS)
        on = o * r
        on2 = _runheads(on)
        gate = g_ref[...]
        sg = _sigmoid(gate)
        sil = gate * sg
        dout = do_ref[...]
        rn = rn_ref[...]
        dp_ref[:, 3 * width:] = (dout * on2 * rn * (sg * (1.0 + gate * (1.0 - sg)))).astype(BF16)
        drn_ref[...] += jnp.sum(dout * on2 * sil, axis=0, keepdims=True)
        don = _rheads(dout * rn * sil)
        do = r * (don - on * jnp.mean(don * on, axis=-1, keepdims=True))
        dc = carry[...]
        dsc = _bmxu(do, v, 2, 2) * tab_ref[:, 0]
        dq = _bmxu(dsc, k) + _bmxu(do, prev, 2, 2) * tab_ref[:, 2]
        dk = _bmxu(dsc, q, 1, 1) + _bmxu(v, dc, 2, 2) * tab_ref[:, 1]
        dv = _bmxu(scores, do, 1, 1) + _bmxu(kdec, dc)
        carry[...] = _bmxu(qdec, do, 1, 1) + dc * tab_ref[:, 2, c - 1:c, :]
        cos, sin = cos_ref[...], sin_ref[...]
        dk = dk * (RET_DH ** -0.5)
        dp_ref[:, :width] = _runheads(dq * cos + pltpu.roll(dq * sin, RET_DH // 2, 2)).astype(BF16)
        dp_ref[:, width:2 * width] = _runheads(dk * cos + pltpu.roll(dk * sin, RET_DH // 2, 2)).astype(BF16)
        dp_ref[:, 2 * width:3 * width] = _runheads(dv).astype(BF16)

    return pl.pallas_call(
        body, name=name, grid=(nc,),
        in_specs=[part(0), part(1), part(2), part(3), pos, pos, gvec, tab, st, part(0)],
        out_specs=[pl.BlockSpec((c, 4 * width), lambda n: (nc - 1 - n, 0)), gvec],
        out_shape=[jax.ShapeDtypeStruct(proj.shape, BF16), jax.ShapeDtypeStruct((1, width), F32)],
        scratch_shapes=[pltpu.VMEM((RET_HEADS, RET_DH, RET_DH), F32)],
        compiler_params=_cp("arbitrary"),
    )(proj, proj, proj, proj, cos, sin, gain.reshape(1, -1), _ret_tables(), states, dmerged)


S5_TILE = 512


def _cmul_add(xr, xi, ar, ai, yr, yi):
    return xr + ar * yr - ai * yi, xi + ar * yi + ai * yr


def _s5_pow_tables(a_il, name):
    r = SCAN_ROWS
    t = S5_TILE
    w2 = a_il.shape[1]

    def body(a_ref, up_ref, dn_ref):
        for j in range(w2 // (2 * t)):
            re, im = pl.ds(2 * t * j, t), pl.ds(2 * t * j + t, t)
            up_ref[0:1, re] = a_ref[:, re]
            up_ref[0:1, im] = a_ref[:, im]
            dn_ref[r - 1:r, re] = a_ref[:, re]
            dn_ref[r - 1:r, im] = -a_ref[:, im]
            n = 1
            while n < r:
                lr, li = up_ref[n - 1:n, re], up_ref[n - 1:n, im]
                xr, xi = up_ref[0:n, re], up_ref[0:n, im]
                up_ref[n:2 * n, re] = xr * lr - xi * li
                up_ref[n:2 * n, im] = xr * li + xi * lr
                yr, yi = dn_ref[r - n:r, re], dn_ref[r - n:r, im]
                dn_ref[r - 2 * n:r - n, re] = yr * lr + yi * li
                dn_ref[r - 2 * n:r - n, im] = yi * lr - yr * li
                n *= 2

    return pl.pallas_call(
        body, name=name, out_shape=[jax.ShapeDtypeStruct((r, w2), F32)] * 2, compiler_params=_cp(),
    )(a_il)


def _s5_scan_fwd(bu, apow, name):
    s, w2 = bu.shape
    r = SCAN_ROWS
    t = S5_TILE
    steps = r.bit_length() - 1

    def body(b_ref, p_ref, o_ref, cr, ci):
        @pl.when(pl.program_id(1) == 0)
        def _():
            cr[...] = jnp.zeros_like(cr)
            ci[...] = jnp.zeros_like(ci)

        xr, xi = b_ref[:, :t], b_ref[:, t:]
        for k in range(steps):
            sh = 1 << k
            xr, xi = _cmul_add(xr, xi, p_ref[sh - 1:sh, :t], p_ref[sh - 1:sh, t:],
                               _shift_down(xr, sh), _shift_down(xi, sh))
        xr, xi = _cmul_add(xr, xi, p_ref[:, :t], p_ref[:, t:], cr[...], ci[...])
        o_ref[:, :t] = xr
        o_ref[:, t:] = xi
        cr[...] = xr[r - 1:r, :]
        ci[...] = xi[r - 1:r, :]

    blk = pl.BlockSpec((r, 2 * t), lambda j, i: (i, j))
    return pl.pallas_call(
        body, name=name, grid=(w2 // (2 * t), s // r),
        in_specs=[blk, pl.BlockSpec((r, 2 * t), lambda j, i: (0, j))], out_specs=blk,
        out_shape=jax.ShapeDtypeStruct((s, w2), F32),
        scratch_shapes=[pltpu.VMEM((1, t), F32), pltpu.VMEM((1, t), F32)],
        compiler_params=_cp("parallel", "arbitrary"),
    )(bu, apow)


def _s5_scan_bwd(dst, apow_rev, st, name):
    s, w2 = dst.shape
    r = SCAN_ROWS
    t = S5_TILE
    nb = s // r
    steps = r.bit_length() - 1

    def body(d_ref, p_ref, s_ref, sp_ref, g_ref, da_ref, cr, ci):
        i = pl.program_id(1)

        @pl.when(i == 0)
        def _():
            cr[...] = jnp.zeros_like(cr)
            ci[...] = jnp.zeros_like(ci)
            da_ref[...] = jnp.zeros_like(da_ref)

        xr, xi = d_ref[:, :t], d_ref[:, t:]
        for k in range(steps):
            sh = 1 << k
            xr, xi = _cmul_add(xr, xi, p_ref[r - sh:r - sh + 1, :t], p_ref[r - sh:r - sh + 1, t:],
                               _shift_up(xr, sh), _shift_up(xi, sh))
        xr, xi = _cmul_add(xr, xi, p_ref[:, :t], p_ref[:, t:], cr[...], ci[...])
        g_ref[:, :t] = xr.astype(BF16)
        g_ref[:, t:] = xi.astype(BF16)
        cr[...] = xr[0:1, :]
        ci[...] = xi[0:1, :]
        first = i == nb - 1
        row = lax.broadcasted_iota(jnp.int32, (r, t), 0)
        last_r = jnp.where(first, 0.0, sp_ref[7:8, :t])
        last_i = jnp.where(first, 0.0, sp_ref[7:8, t:])
        pr = jnp.where(row == 0, last_r, pltpu.roll(s_ref[:, :t], 1, 0))
        pi = jnp.where(row == 0, last_i, pltpu.roll(s_ref[:, t:], 1, 0))
        da_ref[:, :t] += jnp.sum(xr * pr + xi * pi, axis=0, keepdims=True)
        da_ref[:, t:] += jnp.sum(xi * pr - xr * pi, axis=0, keepdims=True)

    blk = pl.BlockSpec((r, 2 * t), lambda j, i: (nb - 1 - i, j))
    halo = pl.BlockSpec((8, 2 * t), lambda j, i: (jnp.maximum((nb - 1 - i) * (r // 8) - 1, 0), j))
    vec = pl.BlockSpec((1, 2 * t), lambda j, i: (0, j))
    return pl.pallas_call(
        body, name=name, grid=(w2 // (2 * t), nb),
        in_specs=[blk, pl.BlockSpec((r, 2 * t), lambda j, i: (0, j)), blk, halo], out_specs=[blk, vec],
        out_shape=[jax.ShapeDtypeStruct((s, w2), BF16), jax.ShapeDtypeStruct((1, w2), F32)],
        scratch_shapes=[pltpu.VMEM((1, t), F32), pltpu.VMEM((1, t), F32)],
        compiler_params=_cp("parallel", "arbitrary"),
    )(dst, apow_rev, st, st)


_GELU_C = math.sqrt(2.0 / math.pi)
_GELU_A = 0.044715


def _gelu(y):
    return 0.5 * y * (1.0 + jnp.tanh(_GELU_C * (y + _GELU_A * y * y * y)))


def _gelu_grad(y):
    th = jnp.tanh(_GELU_C * (y + _GELU_A * y * y * y))
    return 0.5 * (1.0 + th) + 0.5 * y * (1.0 - th * th) * _GELU_C * (1.0 + 3.0 * _GELU_A * y * y)


def _rows_shift(x, k, axis, up):
    n = x.shape[axis]
    idx = lax.broadcasted_iota(jnp.int32, x.shape, axis)
    if up:
        return jnp.where(idx < n - k, pltpu.roll(x, n - k, axis), 0.0)
    return jnp.where(idx >= k, pltpu.roll(x, k, axis), 0.0)


def _scan_block(xr, xi, pr, pi, cr, ci, rev):
    r, w = xr.shape
    nt = r // 8
    x3r, x3i = xr.reshape(nt, 8, w), xi.reshape(nt, 8, w)
    p3r, p3i = pr.reshape(nt, 8, w), pi.reshape(nt, 8, w)

    def power(rows):
        t = r - rows if rev else rows - 1
        return pr[t:t + 1, :], pi[t:t + 1, :]

    for sh in (1, 2, 4):
        ar, ai = power(sh)
        x3r, x3i = _cmul_add(x3r, x3i, ar, ai, _rows_shift(x3r, sh, 1, rev), _rows_shift(x3i, sh, 1, rev))
    edge = 0 if rev else 7
    lr, li = x3r[:, edge, :], x3i[:, edge, :]
    sh = 1
    while sh < nt:
        ar, ai = power(8 * sh)
        lr, li = _cmul_add(lr, li, ar, ai, _rows_shift(lr, sh, 0, rev), _rows_shift(li, sh, 0, rev))
        sh *= 2
    tr_, ti_ = p3r[:, edge, :], p3i[:, edge, :]
    first = lax.broadcasted_iota(jnp.int32, (nt, w), 0) == (nt - 1 if rev else 0)
    wr = jnp.where(first, 1.0, _rows_shift(tr_, 1, 0, rev))
    wi = jnp.where(first, 0.0, _rows_shift(ti_, 1, 0, rev))
    er, ei = _cmul_add(_rows_shift(lr, 1, 0, rev), _rows_shift(li, 1, 0, rev), wr, wi, cr, ci)
    a8r, a8i = (p3r[nt - 1], p3i[nt - 1]) if rev else (p3r[0], p3i[0])
    x3r, x3i = _cmul_add(x3r, x3i, a8r[None], a8i[None], er[:, None, :], ei[:, None, :])
    outr, outi = x3r.reshape(r, w), x3i.reshape(r, w)
    last = 0 if rev else r - 1
    return outr, outi, outr[last:last + 1, :], outi[last:last + 1, :]


def _s5_tile_specs(n_of, r):
    t = S5_TILE
    ucol = 4 * RET_HEADS * RET_DH // LANES
    u = pl.BlockSpec((r, LANES), lambda j, i: (n_of(i), ucol + j))
    col = pl.BlockSpec((r, LANES), lambda j, i: (n_of(i), j))
    state = pl.BlockSpec((r, 2 * t), lambda j, i: (n_of(i), j))
    table = pl.BlockSpec((r, 2 * t), lambda j, i: (0, j))
    bbt = pl.BlockSpec((None, LANES, 2 * t), lambda j, i: (j, 0, 0))
    cct = pl.BlockSpec((None, 2 * t, LANES), lambda j, i: (j, 0, 0))
    vec = pl.BlockSpec((1, LANES), lambda j, i: (0, j))
    return u, col, state, table, bbt, cct, vec


def _s5_fwd(proj, bbt, cct, apow, dvec, name):
    s = proj.shape[0]
    r, t = SCAN_ROWS, S5_TILE
    w = S5_GROUPS * S5_GROUP
    u_s, col, state, table, bb_s, cc_s, vec = _s5_tile_specs(lambda i: i, r)

    def body(u_ref, bb_ref, cc_ref, p_ref, d_ref, st_ref, y_ref, g_ref, cr, ci):
        @pl.when(pl.program_id(1) == 0)
        def _():
            cr[...] = jnp.zeros_like(cr)
            ci[...] = jnp.zeros_like(ci)

        u = u_ref[...]
        bu = _mxu(u, bb_ref[...])
        xr, xi, cr[...], ci[...] = _scan_block(bu[:, :t], bu[:, t:], p_ref[:, :t], p_ref[:, t:], cr[...], ci[...], False)
        st_ref[:, :t] = xr
        st_ref[:, t:] = xi
        y = _mxu(xr, cc_ref[:t, :]) + _mxu(xi, cc_ref[t:, :]) + d_ref[...] * u
        y_ref[...] = y
        g_ref[...] = _gelu(y).astype(BF16)

    return pl.pallas_call(
        body, name=name, grid=(2 * S5_GROUPS * S5_STATE // (2 * t), s // r),
        in_specs=[u_s, bb_s, cc_s, table, vec], out_specs=[state, col, col],
        out_shape=[jax.ShapeDtypeStruct((s, 2 * S5_GROUPS * S5_STATE), F32), jax.ShapeDtypeStruct((s, w), F32),
                   jax.ShapeDtypeStruct((s, w), BF16)],
        scratch_shapes=[pltpu.VMEM((1, t), F32), pltpu.VMEM((1, t), F32)],
        compiler_params=_cp("parallel", "arbitrary"),
    )(proj, bbt, cct, apow, dvec)


def _s5_bwd(dg1, dg2, y, proj, st, bbt, cct, apow_rev, dvec, dproj, name):
    s = proj.shape[0]
    r, t = SCAN_ROWS, S5_TILE
    nb = s // r
    w = S5_GROUPS * S5_GROUP
    u_s, col, state, table, bb_s, cc_s, vec = _s5_tile_specs(lambda i: nb - 1 - i, r)
    halo = pl.BlockSpec((8, 2 * t), lambda j, i: (jnp.maximum((nb - 1 - i) * (r // 8) - 1, 0), j))
    acc = pl.BlockSpec((1, 2 * t), lambda j, i: (0, j))

    def body(a_ref, b_ref, y_ref, u_ref, s_ref, sp_ref, bb_ref, cc_ref, p_ref, d_ref, _,
             du_ref, da_ref, dbb_ref, dcc_ref, dd_ref, cr, ci):
        i = pl.program_id(1)

        @pl.when(i == 0)
        def _():
            cr[...] = jnp.zeros_like(cr)
            ci[...] = jnp.zeros_like(ci)
            da_ref[...] = jnp.zeros_like(da_ref)
            dbb_ref[...] = jnp.zeros_like(dbb_ref)
            dcc_ref[...] = jnp.zeros_like(dcc_ref)
            dd_ref[...] = jnp.zeros_like(dd_ref)

        u = u_ref[...]
        dy = (a_ref[...] + b_ref[...]) * _gelu_grad(y_ref[...])
        dd_ref[...] += jnp.sum(dy * u, axis=0, keepdims=True)
        sr, si = s_ref[:, :t], s_ref[:, t:]
        dcc_ref[:t, :] += _mxu(sr, dy, 0, 0)
        dcc_ref[t:, :] += _mxu(si, dy, 0, 0)
        xr, xi, cr[...], ci[...] = _scan_block(_mxu(dy, cc_ref[:t, :], 1, 1), _mxu(dy, cc_ref[t:, :], 1, 1),
                                               p_ref[:, :t], p_ref[:, t:], cr[...], ci[...], True)
        du_ref[...] = (dy * d_ref[...] + _mxu(xr, bb_ref[:, :t], 1, 1) + _mxu(xi, bb_ref[:, t:], 1, 1)).astype(BF16)
        dbb_ref[:, :t] += _mxu(u, xr, 0, 0)
        dbb_ref[:, t:] += _mxu(u, xi, 0, 0)
        first = i == nb - 1
        row = lax.broadcasted_iota(jnp.int32, (r, t), 0)
        pr = jnp.where(row == 0, jnp.where(first, 0.0, sp_ref[7:8, :t]), pltpu.roll(sr, 1, 0))
        pi = jnp.where(row == 0, jnp.where(first, 0.0, sp_ref[7:8, t:]), pltpu.roll(si, 1, 0))
        da_ref[:, :t] += jnp.sum(xr * pr + xi * pi, axis=0, keepdims=True)
        da_ref[:, t:] += jnp.sum(xi * pr - xr * pi, axis=0, keepdims=True)

    return pl.pallas_call(
        body, name=name, grid=(2 * S5_GROUPS * S5_STATE // (2 * t), nb),
        in_specs=[col, col, col, u_s, state, halo, bb_s, cc_s, table, vec, pl.BlockSpec(memory_space=pl.ANY)],
        out_specs=[u_s, acc, bb_s, cc_s, vec],
        out_shape=[jax.ShapeDtypeStruct(dproj.shape, dproj.dtype), jax.ShapeDtypeStruct((1, 2 * S5_GROUPS * S5_STATE), F32),
                   jax.ShapeDtypeStruct(bbt.shape, F32), jax.ShapeDtypeStruct(cct.shape, F32),
                   jax.ShapeDtypeStruct((1, w), F32)],
        scratch_shapes=[pltpu.VMEM((1, t), F32), pltpu.VMEM((1, t), F32)],
        input_output_aliases={10: 0}, compiler_params=_cp("parallel", "arbitrary"),
    )(dg1, dg2, y, proj, st, st, bbt, cct, apow_rev, dvec, dproj)


def _s5_tile_b(b_re, b_im):
    nt = S5_GROUPS * S5_STATE // S5_TILE
    gpt = S5_GROUPS // nt
    eye = jnp.eye(gpt, dtype=F32)

    def tile(b):
        t5 = jnp.einsum("jghp,gk->jghkp", b.reshape(nt, gpt, S5_GROUP, S5_STATE), eye)
        return t5.reshape(nt, gpt * S5_GROUP, S5_TILE)

    return jnp.concatenate([tile(b_re), tile(b_im)], axis=2)


def _s5_untile_b(d):
    nt = S5_GROUPS * S5_STATE // S5_TILE
    gpt = S5_GROUPS // nt
    eye = jnp.eye(gpt, dtype=F32)

    def untile(x):
        x5 = x.reshape(nt, gpt, S5_GROUP, gpt, S5_STATE)
        return jnp.einsum("jghkp,gk->jghp", x5, eye).reshape(S5_GROUPS, S5_GROUP, S5_STATE)

    return untile(d[:, :, :S5_TILE]), untile(d[:, :, S5_TILE:])


def _s5_tile_c(c_re, c_im):
    nt = S5_GROUPS * S5_STATE // S5_TILE
    gpt = S5_GROUPS // nt
    eye = jnp.eye(gpt, dtype=F32)

    def tile(c):
        t5 = jnp.einsum("jgph,gk->jkpgh", c.reshape(nt, gpt, S5_STATE, S5_GROUP), eye)
        return t5.reshape(nt, S5_TILE, gpt * S5_GROUP)

    return jnp.concatenate([tile(c_re), -tile(c_im)], axis=1)


def _s5_untile_c(d):
    nt = S5_GROUPS * S5_STATE // S5_TILE
    gpt = S5_GROUPS // nt
    eye = jnp.eye(gpt, dtype=F32)

    def untile(x):
        x5 = x.reshape(nt, gpt, S5_STATE, gpt, S5_GROUP)
        return jnp.einsum("jkpgh,gk->jgph", x5, eye).reshape(S5_GROUPS, S5_STATE, S5_GROUP)

    return untile(d[:, :S5_TILE, :]), -untile(d[:, S5_TILE:, :])


def _row_call(body, name, s, ins, outs, acc=False):
    tr = min(512, s)

    def spec(width, cb, rows):
        if rows == 1:
            return pl.BlockSpec((1, width), lambda i: (0, cb))
        return pl.BlockSpec((tr, width), lambda i: (i, cb))

    in_specs = [spec(w, cb, a.shape[0]) for a, w, cb in ins]
    out_specs = [spec(w, cb, sd.shape[0]) for sd, w, cb in outs]
    return pl.pallas_call(
        body, name=name, grid=(s // tr,), in_specs=in_specs, out_specs=out_specs,
        out_shape=[sd for sd, _, _ in outs],
        compiler_params=_cp("arbitrary" if acc else "parallel"),
    )(*[a for a, _, _ in ins])


def _sds(shape, dtype):
    return jax.ShapeDtypeStruct(shape, dtype)


def _s5_gelu_fwd(yraw, proj, dvec, name):
    s, w = yraw.shape

    def body(y_ref, u_ref, d_ref, yo_ref, g_ref):
        y = y_ref[...] + d_ref[...] * u_ref[...]
        yo_ref[...] = y
        g_ref[...] = _gelu(y).astype(BF16)

    return _row_call(body, name, s, [(yraw, w, 0), (proj, w, 4), (dvec, w, 0)],
                     [(_sds((s, w), F32), w, 0), (_sds((s, w), BF16), w, 0)])


def _s5_glu_fwd(y, z, b, name):
    s, w = y.shape

    def body(y_ref, z_ref, b_ref, o_ref):
        o_ref[...] = (_gelu(y_ref[...]) * _sigmoid(z_ref[...] + b_ref[...])).astype(BF16)

    return _row_call(body, name, s, [(y, w, 0), (z, w, 0), (b, w, 0)], [(_sds((s, w), BF16), w, 0)])[0]


def _s5_glu_bwd(dmerged, y, z, b, name):
    s, w = y.shape

    def body(do_ref, y_ref, z_ref, b_ref, dz_ref, dg_ref, db_ref):
        @pl.when(pl.program_id(0) == 0)
        def _():
            db_ref[...] = jnp.zeros_like(db_ref)

        g = _gelu(y_ref[...])
        sg = _sigmoid(z_ref[...] + b_ref[...])
        dout = do_ref[...]
        dz = dout * g * sg * (1.0 - sg)
        dz_ref[...] = dz.astype(BF16)
        dg_ref[...] = dout * sg
        db_ref[...] += jnp.sum(dz, axis=0, keepdims=True)

    return _row_call(body, name, s, [(dmerged, w, 1), (y, w, 0), (z, w, 0), (b, w, 0)],
                     [(_sds((s, w), BF16), w, 0), (_sds((s, w), F32), w, 0), (_sds((1, w), F32), w, 0)], acc=True)


def _s5_gelu_bwd(dg1, dg2, y, proj, dvec, name):
    s, w = y.shape

    def body(a_ref, b_ref, y_ref, u_ref, d_ref, dy_ref, du_ref, dd_ref):
        @pl.when(pl.program_id(0) == 0)
        def _():
            dd_ref[...] = jnp.zeros_like(dd_ref)

        dy = (a_ref[...] + b_ref[...]) * _gelu_grad(y_ref[...])
        dy_ref[...] = dy.astype(BF16)
        du_ref[...] = dy * d_ref[...]
        dd_ref[...] += jnp.sum(dy * u_ref[...], axis=0, keepdims=True)

    return _row_call(body, name, s, [(dg1, w, 0), (dg2, w, 0), (y, w, 0), (proj, w, 4), (dvec, w, 0)],
                     [(_sds((s, w), BF16), w, 0), (_sds((s, w), F32), w, 0), (_sds((1, w), F32), w, 0)], acc=True)


def _gdn_conv_fwd(projx, cw, name):
    s = projx.shape[0]
    nh = GDN_HEADS

    def body(x_ref, w_ref, o_ref):
        j = pl.program_id(0)
        cv = _conv_fwd(_conv_taps(x_ref[...], GDN_CONV), w_ref)
        y = cv * _sigmoid(cv)
        nrm = y * lax.rsqrt(jnp.sum(y * y, axis=-1, keepdims=True) + EPS)
        o_ref[...] = jnp.where(j < nh, nrm * (GDN_DH ** -0.5), jnp.where(j < 2 * nh, nrm, y))

    return pl.pallas_call(
        body, name=name, grid=(3 * nh,),
        in_specs=[pl.BlockSpec((s, GDN_DH), lambda j: (0, j)), pl.BlockSpec((GDN_CONV, GDN_DH), lambda j: (0, j))],
        out_specs=pl.BlockSpec((s, GDN_DH), lambda j: (0, j)),
        out_shape=jax.ShapeDtypeStruct((s, 3 * nh * GDN_DH), F32), compiler_params=_cp("parallel"),
    )(projx, cw)


def _gdn_conv_bwd(projx, cw, dqkv, dprojx, name):
    s = projx.shape[0]
    nh = GDN_HEADS

    def body(x_ref, w_ref, d_ref, _, dx_ref, dw_ref):
        j = pl.program_id(0)
        x = x_ref[...]
        taps = _conv_taps(x, GDN_CONV)
        cv = _conv_fwd(taps, w_ref)
        sg = _sigmoid(cv)
        y = cv * sg
        rinv = lax.rsqrt(jnp.sum(y * y, axis=-1, keepdims=True) + EPS)
        nrm = y * rinv
        dn = d_ref[...]
        dns = jnp.where(j < nh, dn * (GDN_DH ** -0.5), dn)
        dyn = rinv * (dns - nrm * jnp.sum(dns * nrm, axis=-1, keepdims=True))
        dy = jnp.where(j < 2 * nh, dyn, dn)
        dc = dy * (sg * (1.0 + cv * (1.0 - sg)))
        dx_ref[...] = _conv_bwd(taps, dc, w_ref, dw_ref).astype(BF16)

    col = pl.BlockSpec((s, GDN_DH), lambda j: (0, j))
    wcol = pl.BlockSpec((GDN_CONV, GDN_DH), lambda j: (0, j))
    return pl.pallas_call(
        body, name=name, grid=(3 * nh,), in_specs=[col, wcol, col, pl.BlockSpec(memory_space=pl.ANY)],
        out_specs=[col, wcol],
        out_shape=[jax.ShapeDtypeStruct(dprojx.shape, dprojx.dtype), jax.ShapeDtypeStruct((GDN_CONV, 3 * nh * GDN_DH), F32)],
        input_output_aliases={3: 0}, compiler_params=_cp("parallel"),
    )(projx, cw, dqkv, dprojx)


def _softplus(x):
    return jnp.maximum(x, 0.0) + jnp.log1p(jnp.exp(-jnp.abs(x)))


def _gdn_gates_fwd(projx, alog, dtb, name):
    s = projx.shape[0]
    w = GDN_HEADS * GDN_DH

    def body(b_ref, a_ref, al_ref, dt_ref, bo_ref, go_ref):
        bo_ref[...] = _sigmoid(b_ref[...])
        go_ref[...] = -jnp.exp(al_ref[...]) * _softplus(a_ref[...] + dt_ref[...])

    return _row_call(body, name, s, [(projx, w, 4), (projx, w, 5), (alog, w, 0), (dtb, w, 0)],
                     [(_sds((s, w), F32), w, 0), (_sds((s, w), F32), w, 0)])


def _gdn_gates_bwd(projx, alog, dtb, dbeta, dg, dprojx, name):
    s = projx.shape[0]
    w = GDN_HEADS * GDN_DH
    tr = min(512, s)

    def body(b_ref, a_ref, al_ref, dt_ref, dbe_ref, dg_ref, _, o_ref, dal_ref, ddt_ref):
        @pl.when(pl.program_id(0) == 0)
        def _():
            dal_ref[...] = jnp.zeros_like(dal_ref)
            ddt_ref[...] = jnp.zeros_like(ddt_ref)

        for h in range(GDN_HEADS):
            lo, hi = h * GDN_DH, (h + 1) * GDN_DH
            beta = _sigmoid(b_ref[:, lo:hi])
            pb = jnp.sum(dbe_ref[:, lo:hi], axis=-1, keepdims=True) * (1.0 / GDN_DH)
            o_ref[:, lo:hi] = (pb * beta * (1.0 - beta)).astype(BF16)
            xa = a_ref[:, lo:hi] + dt_ref[:, lo:hi]
            ea = -jnp.exp(al_ref[:, lo:hi])
            pg = jnp.sum(dg_ref[:, lo:hi], axis=-1, keepdims=True) * (1.0 / GDN_DH)
            da = pg * ea * _sigmoid(xa)
            o_ref[:, w + lo:w + hi] = da.astype(BF16)
            dal_ref[:, lo:hi] += jnp.sum(pg * ea * _softplus(xa), axis=0, keepdims=True)
            ddt_ref[:, lo:hi] += jnp.sum(da, axis=0, keepdims=True)

    def row(cb):
        return pl.BlockSpec((tr, w), lambda i: (i, cb))

    vec = pl.BlockSpec((1, w), lambda i: (0, 0))
    return pl.pallas_call(
        body, name=name, grid=(s // tr,),
        in_specs=[row(4), row(5), vec, vec, row(0), row(0), pl.BlockSpec(memory_space=pl.ANY)],
        out_specs=[pl.BlockSpec((tr, 2 * w), lambda i: (i, 2)), vec, vec],
        out_shape=[jax.ShapeDtypeStruct(dprojx.shape, dprojx.dtype), jax.ShapeDtypeStruct((1, w), F32),
                   jax.ShapeDtypeStruct((1, w), F32)],
        input_output_aliases={6: 0}, compiler_params=_cp("arbitrary"),
    )(projx, projx, alog, dtb, dbeta, dg, dprojx)


def _gdn_tri():
    c = GDN_CHUNK
    i = lax.broadcasted_iota(jnp.int32, (c, c), 0)
    j = lax.broadcasted_iota(jnp.int32, (c, c), 1)
    return ((i >= j).astype(F32), (i <= j).astype(F32), i >= j, i > j, (i == j).astype(F32))


def _bdot(a, b, ca=2, cb=1, precision=None):
    return lax.dot_general(a, b, (((ca,), (cb,)), ((0,), (0,))), precision=precision, preferred_element_type=F32)


def _bmxu(a, b, ca=2, cb=1):
    return _bdot(a.astype(BF16), b.astype(BF16), ca, cb)


def _split(x):
    hi = x.astype(BF16)
    return hi, (x - hi.astype(F32)).astype(BF16)


def _bdot3(a, b, ca=2, cb=1):
    ah, al = _split(a)
    bh, bl = _split(b)
    return _bdot(ah, bh, ca, cb) + (_bdot(ah, bl, ca, cb) + _bdot(al, bh, ca, cb))


def _tri_dot(tri, x):
    t = tri.astype(BF16)
    hi = x.astype(BF16)
    r1 = x - hi.astype(F32)
    mid = r1.astype(BF16)
    lo = (r1 - mid.astype(F32)).astype(BF16)
    return _dot(t, hi) + (_dot(t, mid) + _dot(t, lo))


def _heads(x):
    return jnp.stack([x[:, h * GDN_DH:(h + 1) * GDN_DH] for h in range(GDN_HEADS)], axis=0)


def _unheads(x):
    return jnp.concatenate([x[h] for h in range(GDN_HEADS)], axis=1)


def _gdn_chunk(q, k, v, bb, g2d, tri):
    low, up, incl, strict, eye = tri
    c = GDN_CHUNK
    gc = _heads(_tri_dot(low, g2d))
    gci = gc[:, :, :c]
    gdiff = gci - jnp.swapaxes(gci, 1, 2)
    decay = jnp.where(incl, jnp.exp(jnp.where(incl, gdiff, 0.0)), 0.0)
    kb, vb = k * bb, v * bb
    kbk = _bmxu(kb, k, 2, 2)
    x = -jnp.where(strict, kbk * decay, 0.0)
    t = eye + x
    p = x
    for _ in range(c.bit_length() - 2):
        p = _bdot3(p, p)
        t = t + _bdot3(t, p)
    eg = jnp.exp(gc)
    kbg = kb * eg
    gcl = gc[:, c - 1:c, :]
    ek = jnp.exp(gcl - gc)
    qkraw = _bmxu(q, k, 2, 2)
    return dict(decay=decay, kb=kb, vb=vb, kbk=kbk, t=t, eg=eg, kbg=kbg, ek=ek, gl=jnp.exp(gcl),
                w=_bmxu(t, kbg), u=_bmxu(t, vb), qkraw=qkraw, qk=jnp.where(incl, qkraw * decay, 0.0),
                qd=q * eg, kd=k * ek)


def _gdn_specs(n_of):
    c, w = GDN_CHUNK, GDN_HEADS * GDN_DH

    def blk(cb, width=w):
        return pl.BlockSpec((c, width), lambda n: (n_of(n), cb))

    st = pl.BlockSpec((None, GDN_HEADS, GDN_DH, GDN_DH), lambda n: (n_of(n), 0, 0, 0))
    vec = pl.BlockSpec((1, GDN_DH), lambda n: (0, 0))
    return blk, st, vec


def _gdn_load(qkv_ref, b_ref, g_ref, tri):
    w = GDN_HEADS * GDN_DH
    q, k, v = _heads(qkv_ref[:, :w]), _heads(qkv_ref[:, w:2 * w]), _heads(qkv_ref[:, 2 * w:])
    bb = _heads(b_ref[...])
    return q, k, v, bb, _gdn_chunk(q, k, v, bb, g_ref[...], tri)


def _gdn_fwd(qkv, beta, g, projx, onorm, name):
    s = qkv.shape[0]
    nc = s // GDN_CHUNK
    w = GDN_HEADS * GDN_DH
    blk, st, vec = _gdn_specs(lambda n: n)

    def body(qkv_ref, b_ref, g_ref, z_ref, on_ref, o_ref, st_ref, state):
        @pl.when(pl.program_id(0) == 0)
        def _():
            state[...] = jnp.zeros_like(state)

        _, _, _, _, ch = _gdn_load(qkv_ref, b_ref, g_ref, _gdn_tri())
        sp = state[...]
        st_ref[...] = sp
        vn = ch["u"] - _bmxu(ch["w"], sp)
        o = _bmxu(ch["qd"], sp) + _bmxu(ch["qk"], vn)
        state[...] = sp * ch["gl"] + _bmxu(ch["kd"], vn, 1, 1)
        r = lax.rsqrt(jnp.mean(o * o, axis=-1, keepdims=True) + EPS)
        z = _heads(z_ref[...])
        o_ref[...] = _unheads(o * r * on_ref[...] * (z * _sigmoid(z))).astype(BF16)

    return pl.pallas_call(
        body, name=name, grid=(nc,),
        in_specs=[blk(0, 3 * w), blk(0), blk(0), blk(3), vec], out_specs=[blk(0), st],
        out_shape=[jax.ShapeDtypeStruct((s, w), BF16), jax.ShapeDtypeStruct((nc, GDN_HEADS, GDN_DH, GDN_DH), F32)],
        scratch_shapes=[pltpu.VMEM((GDN_HEADS, GDN_DH, GDN_DH), F32)],
        compiler_params=_cp("arbitrary"),
    )(qkv, beta, g, projx, onorm.reshape(1, -1))


def _gdn_bwd(qkv, beta, g, projx, onorm, states, dout, name):
    s = qkv.shape[0]
    c = GDN_CHUNK
    nc = s // c
    w = GDN_HEADS * GDN_DH
    blk, st, vec = _gdn_specs(lambda n: nc - 1 - n)

    def body(qkv_ref, b_ref, g_ref, z_ref, on_ref, st_ref, do_ref,
             dqkv_ref, db_ref, dg_ref, dz_ref, don_ref, carry):
        @pl.when(pl.program_id(0) == 0)
        def _():
            carry[...] = jnp.zeros_like(carry)
            don_ref[...] = jnp.zeros_like(don_ref)

        tri = _gdn_tri()
        low, up, incl, strict, eye = tri
        q, k, v, bb, ch = _gdn_load(qkv_ref, b_ref, g_ref, tri)
        sp = st_ref[...]
        vn = ch["u"] - _bmxu(ch["w"], sp)
        o = _bmxu(ch["qd"], sp) + _bmxu(ch["qk"], vn)
        r = lax.rsqrt(jnp.mean(o * o, axis=-1, keepdims=True) + EPS)
        orn = o * r
        z = _heads(z_ref[...])
        sg = _sigmoid(z)
        dout = _heads(do_ref[...])
        onw = on_ref[...]
        dz_ref[...] = _unheads(dout * orn * onw * (sg * (1.0 + z * (1.0 - sg)))).astype(BF16)
        don = dout * (z * sg)
        don_ref[...] += jnp.sum(jnp.sum(don * orn, axis=0), axis=0, keepdims=True)
        dor = don * onw
        do = r * (dor - orn * jnp.mean(dor * orn, axis=-1, keepdims=True))
        dsn = carry[...]
        dqd = _bmxu(do, sp, 2, 2)
        dqk = jnp.where(incl, _bmxu(do, vn, 2, 2), 0.0)
        dvn = _bmxu(ch["qk"], do, 1, 1) + _bmxu(ch["kd"], dsn)
        dkd = _bmxu(vn, dsn, 2, 2)
        dgl = jnp.sum(dsn * sp, axis=1, keepdims=True)
        dw = -_bmxu(dvn, sp, 2, 2)
        carry[...] = _bmxu(ch["qd"], do, 1, 1) + dsn * ch["gl"] - _bmxu(ch["w"], dvn, 1, 1)
        t = ch["t"]
        dvb = _bmxu(t, dvn, 1, 1)
        dkbg = _bmxu(t, dw, 1, 1)
        dt = _bmxu(dvn, ch["vb"], 2, 2) + _bmxu(dw, ch["kbg"], 2, 2)
        da = -_bdot3(_bdot3(t, dt, 1, 1), t, 2, 2)
        da = jnp.where(strict, da, 0.0)
        decay = ch["decay"]
        dkbk = da * decay
        dqkr = dqk * decay
        mdec = (da * ch["kbk"] + dqk * ch["qkraw"]) * decay
        dkb = _bmxu(dkbk, k) + dkbg * ch["eg"]
        dk = _bmxu(dkbk, ch["kb"], 1, 1) + _bmxu(dqkr, q, 1, 1) + dkd * ch["ek"] + dkb * bb
        dq = _bmxu(dqkr, k) + dqd * ch["eg"]
        tk = dkd * ch["kd"]
        dgcl = jnp.sum(tk, axis=1, keepdims=True) + dgl * ch["gl"]
        row = lax.broadcasted_iota(jnp.int32, (GDN_HEADS, c, GDN_DH), 1)
        zpad = jnp.zeros((GDN_HEADS, c, GDN_DH - c), F32)
        dgc = (jnp.concatenate([mdec, zpad], axis=2) - jnp.concatenate([jnp.swapaxes(mdec, 1, 2), zpad], axis=2)
               + dqd * ch["qd"] - tk + dkbg * ch["kbg"] + jnp.where(row == c - 1, dgcl, 0.0))
        dqkv_ref[:, :w] = _unheads(dq)
        dqkv_ref[:, w:2 * w] = _unheads(dk)
        dqkv_ref[:, 2 * w:] = _unheads(dvb * bb)
        db_ref[...] = _unheads(dkb * k + dvb * v)
        dg_ref[...] = _tri_dot(up, _unheads(dgc))

    return pl.pallas_call(
        body, name=name, grid=(nc,),
        in_specs=[blk(0, 3 * w), blk(0), blk(0), blk(3), vec, st, blk(0)],
        out_specs=[blk(0, 3 * w), blk(0), blk(0), blk(3), vec],
        out_shape=[jax.ShapeDtypeStruct((s, 3 * w), F32), jax.ShapeDtypeStruct((s, w), F32),
                   jax.ShapeDtypeStruct((s, w), F32), jax.ShapeDtypeStruct(projx.shape, BF16),
                   jax.ShapeDtypeStruct((1, GDN_DH), F32)],
        scratch_shapes=[pltpu.VMEM((GDN_HEADS, GDN_DH, GDN_DH), F32)],
        compiler_params=_cp("arbitrary"),
    )(qkv, beta, g, projx, onorm.reshape(1, -1), states, dout)


_WEIGHTS = (
    "l0_mix_norm", "l0_w_in", "l0_ret_norm", "l0_s5_lambda_re", "l0_s5_lambda_im", "l0_s5_b_re", "l0_s5_b_im",
    "l0_s5_c_re", "l0_s5_c_im", "l0_s5_d", "l0_s5_log_dt", "l0_s5_w_glu", "l0_s5_b_glu", "l0_w_out",
    "l0_xa_norm", "l0_mem_norm", "l0_xa_wq", "l0_xa_wkv", "l0_xa_wo", "l0_ffn_norm", "l0_ffn_w_up",
    "l0_ffn_conv", "l0_ffn_w_down", "l1_mix_norm", "l1_w_in", "l1_conv", "l1_a_log", "l1_dt_bias", "l1_o_norm",
    "l1_w_out", "l1_xa_norm", "l1_mem_norm", "l1_xa_wq", "l1_xa_wkv", "l1_xa_wo", "l1_ffn_norm", "l1_ffn_w_up",
    "l1_ffn_conv", "l1_ffn_w_down", "final_norm")
_INPUTS = ("x", "mem") + _WEIGHTS + ("loss_target",) + tuple("m_" + n for n in _WEIGHTS) + tuple("v_" + n for n in _WEIGHTS)

_COL = ("l0_w_in", "l0_xa_wkv", "l0_ffn_w_up", "l0_ffn_conv", "l1_w_in", "l1_conv", "l1_xa_wkv", "l1_ffn_w_up",
        "l1_ffn_conv")
_ROW = ("l0_s5_w_glu", "l0_w_out", "l0_xa_wq", "l0_xa_wo", "l0_ffn_w_down", "l1_w_out", "l1_xa_wq", "l1_xa_wo",
        "l1_ffn_w_down")
_F32_WIRE = ("l0_ffn_conv", "l1_conv", "l1_ffn_conv")
_REP = tuple(n for n in _WEIGHTS if n not in _COL + _ROW)
_GATHER_GROUPS = (("l0_w_in", "l0_s5_w_glu", "l0_w_out"),
                  ("l0_xa_wq", "l0_xa_wkv", "l0_xa_wo", "l0_ffn_w_up", "l0_ffn_conv", "l0_ffn_w_down"),
                  ("l1_w_in", "l1_conv", "l1_w_out", "l1_xa_wq", "l1_xa_wkv", "l1_xa_wo"),
                  ("l1_ffn_w_up", "l1_ffn_conv", "l1_ffn_w_down"))


def _round_up(n, m):
    return (n + m - 1) // m * m


_REP_BIG = ("l0_s5_lambda_re", "l0_s5_lambda_im", "l0_s5_b_re", "l0_s5_b_im", "l0_s5_c_re", "l0_s5_c_im", "l0_s5_d")
_REP_LAST = "l0_mix_norm"
_REP_SMALL = tuple(n for n in _REP if n not in _REP_BIG + (_REP_LAST,))
PACK_WIDTH = 1024


def _pack_rows(ts):
    rows = [jnp.pad(t, ((0, 0), (0, PACK_WIDTH - t.shape[1]))) for t in ts]
    rows.append(jnp.zeros((_round_up(len(ts), 8) - len(ts), PACK_WIDTH), F32))
    return jnp.concatenate(rows, axis=0)


def _s5_interleave(re, im):
    lead = re.shape[:-1]
    nt = re.shape[-1] // S5_TILE
    both = jnp.stack([re.reshape(lead + (nt, S5_TILE)), im.reshape(lead + (nt, S5_TILE))], axis=-2)
    return both.reshape(lead + (2 * re.shape[-1],))


def _s5_split(x):
    lead = x.shape[:-1]
    y = x.reshape(lead + (x.shape[-1] // (2 * S5_TILE), 2, S5_TILE))
    return y[..., 0, :].reshape(lead + (-1,)), y[..., 1, :].reshape(lead + (-1,))


def _s5_discretise(lr, li, log_dt, b_re, b_im):
    dt = jnp.exp(log_dt)[:, None]
    mag = jnp.exp(lr * dt)
    a_re = mag * jnp.cos(li * dt)
    a_im = mag * jnp.sin(li * dt)
    den = lr * lr + li * li
    z_re = ((a_re - 1.0) * lr + a_im * li) / den
    z_im = (a_im * lr - (a_re - 1.0) * li) / den
    bb_re = z_re[:, None, :] * b_re - z_im[:, None, :] * b_im
    bb_im = z_re[:, None, :] * b_im + z_im[:, None, :] * b_re
    return a_re, a_im, bb_re, bb_im


def _block_diag(b):
    g, r, c = b.shape
    return jnp.einsum("grc,gk->grkc", b, jnp.eye(g, dtype=b.dtype)).reshape(g * r, g * c)


def _block_diag_of(d, g):
    r, c = d.shape[0] // g, d.shape[1] // g
    return jnp.einsum("grkc,gk->grc", d.reshape(g, r, g, c), jnp.eye(g, dtype=d.dtype))


def kernel(*args):
    p = dict(zip(_INPUTS, args, strict=True))
    x0, mem0, tgt = p["x"][0], p["mem"][0], p["loss_target"][0]
    s, d = x0.shape
    me = _slot(*_mesh_pos())
    grads = {}
    wire = {n: (F32 if n in _F32_WIRE else BF16) for n in _COL + _ROW}

    zones = {n: _into_slot(p[n], wire[n], me, "place_" + n) for names in _GATHER_GROUPS for n in names}
    gather, pin = [], jnp.zeros((), F32)
    for i, names in enumerate(_GATHER_GROUPS):
        handle, token = _push_start([], [zones[n] for n in names], f"gather{i}_start")
        gather.append(handle)
        pin = pin + token[0, 0]
    w = {}

    def gathered(i, after):
        for n, full in zip(_GATHER_GROUPS[i], _push_wait(gather[i], after, f"gather{i}_wait")):
            if n in _COL:
                full = full.transpose(1, 0, 2)
            w[n] = full.reshape(-1, full.shape[-1]) if n in _ROW else full.reshape(full.shape[0], -1)

    pending = []

    def exchange(names, gain, tag):
        slots = []
        for n in names:
            g = grads[n]
            if n in _COL:
                g = g.reshape(g.shape[0], N_DEV, -1).transpose(1, 0, 2)
            else:
                g = g.reshape((N_DEV, -1) + g.shape[1:])
            slots.append(g.astype(wire[n]))
        handle, token = _push_start(slots, [], tag + "_start")
        pending.append((names, slots, handle, tag))
        return gain + token[0, 0]

    def xattn(pre, x_in):
        hx = _norm_fwd(x_in, p[pre + "xa_norm"], pre + "xa_norm_fwd")
        q = _mm(hx, w[pre + "xa_wq"], out_dtype=BF16, name=pre + "xa_q")
        memn = _norm_fwd(mem0, p[pre + "mem_norm"], pre + "mem_norm_fwd")
        kv = _mm(memn, w[pre + "xa_wkv"], out_dtype=BF16, name=pre + "xa_kv")
        ao = _xattn_fwd(q, kv, pre + "xattn_fwd")
        x_out = _mm(ao, w[pre + "xa_wo"], res=x_in, name=pre + "xa_o")
        return x_out, (x_in, hx, q, memn, kv, ao)

    def xattn_bwd(pre, saved, dxo):
        x_in, hx, q, memn, kv, ao = saved
        dao = _mm(dxo, w[pre + "xa_wo"], tb=True, name=pre + "xa_o_dx")
        grads[pre + "xa_wo"] = _mm(ao, dxo, ta=True, out_dtype=BF16, name=pre + "xa_o_dw")
        dq, dkv = _xattn_bwd(q, kv, dao, pre + "xattn_bwd")
        grads[pre + "xa_wq"] = _mm(hx, dq, ta=True, out_dtype=BF16, name=pre + "xa_q_dw")
        grads[pre + "xa_wkv"] = _mm(memn, dkv, ta=True, out_dtype=BF16, name=pre + "xa_kv_dw")
        dmemn = _mm(dkv, w[pre + "xa_wkv"], tb=True, name=pre + "xa_kv_dx")
        gain = exchange((pre + "xa_wo", pre + "xa_wq", pre + "xa_wkv"), p[pre + "xa_norm"], pre + "xa_grads")
        dx_in, grads[pre + "xa_norm"] = _mm_norm_bwd(dq, w[pre + "xa_wq"], x_in, gain, dxo, pre + "xa_q_dx")
        _, grads[pre + "mem_norm"] = _norm_bwd(mem0, p[pre + "mem_norm"], dmemn, jnp.zeros_like(mem0), pre + "mem_norm_bwd")
        return dx_in

    def ffn(pre, x_in):
        hf = _norm_fwd(x_in, p[pre + "ffn_norm"], pre + "ffn_norm_fwd")
        up = _mm(hf, w[pre + "ffn_w_up"], out_dtype=BF16, name=pre + "ffn_up")
        act = _ffn_act_fwd(up, w[pre + "ffn_conv"], pre + "ffn_act_fwd")
        x_out = _mm(act, w[pre + "ffn_w_down"], res=x_in, name=pre + "ffn_down")
        return x_out, (x_in, hf, up, act)

    def ffn_bwd(pre, saved, dxo):
        x_in, hf, up, act = saved
        dact = _mm(dxo, w[pre + "ffn_w_down"], tb=True, out_dtype=BF16, name=pre + "ffn_down_dx")
        grads[pre + "ffn_w_down"] = _mm(act, dxo, ta=True, out_dtype=BF16, name=pre + "ffn_down_dw")
        dpu, dpg, dcu, dcg = _ffn_act_bwd(up, w[pre + "ffn_conv"], dact, pre + "ffn_act_bwd")
        dup = jnp.concatenate([dpu, dpg], axis=1)
        grads[pre + "ffn_conv"] = jnp.concatenate([dcu, dcg], axis=1)
        grads[pre + "ffn_w_up"] = _mm(hf, dup, ta=True, out_dtype=BF16, name=pre + "ffn_up_dw")
        gain = exchange((pre + "ffn_w_down", pre + "ffn_w_up", pre + "ffn_conv"), p[pre + "ffn_norm"], pre + "ffn_grads")
        dx_in, grads[pre + "ffn_norm"] = _mm_norm_bwd(dup, w[pre + "ffn_w_up"], x_in, gain, dxo, pre + "ffn_up_dx")
        return dx_in

    cos, sin = _rope_tables(s)
    (a_re, a_im, bb_re, bb_im), disc_vjp = jax.vjp(
        _s5_discretise, p["l0_s5_lambda_re"], p["l0_s5_lambda_im"], p["l0_s5_log_dt"], p["l0_s5_b_re"], p["l0_s5_b_im"])
    apow, apow_rev = _s5_pow_tables(_s5_interleave(a_re.reshape(1, -1), a_im.reshape(1, -1)), "l0_s5_pow_tables")
    bbt = _s5_tile_b(bb_re, bb_im).astype(BF16)
    cct = _s5_tile_c(p["l0_s5_c_re"], p["l0_s5_c_im"]).astype(BF16)
    s5_d = p["l0_s5_d"].reshape(1, -1)
    b_glu = p["l0_s5_b_glu"].reshape(1, -1)

    h0 = _norm_fwd(x0, p["l0_mix_norm"] + pin, "l0_mix_norm_fwd")
    gathered(0, h0)
    proj = _mm(h0, w["l0_w_in"], name="l0_in")
    o_ret, ret_states = _ret_fwd(proj, cos, sin, p["l0_ret_norm"], "l0_ret_fwd")
    st, y, gy = _s5_fwd(proj, bbt, cct, apow, s5_d, "l0_s5_fwd")
    z = _mm(gy, w["l0_s5_w_glu"], name="l0_s5_glu_mm")
    y2 = _s5_glu_fwd(y, z, b_glu, "l0_s5_glu_fwd")
    merged = jnp.concatenate([o_ret, y2], axis=1)
    x1 = _mm(merged, w["l0_w_out"], res=x0, name="l0_out")
    gathered(1, x1)
    x2, xa0 = xattn("l0_", x1)
    x3, ff0 = ffn("l0_", x2)

    gathered(2, x3)
    nqkv = 4 * GDN_HEADS * GDN_DH
    w1 = w["l1_w_in"]
    wx = jnp.concatenate([w1[:, :nqkv], jnp.repeat(w1[:, nqkv:nqkv + GDN_HEADS], GDN_DH, axis=1),
                          jnp.repeat(w1[:, nqkv + GDN_HEADS:], GDN_DH, axis=1)], axis=1)
    alog_x = jnp.repeat(p["l1_a_log"], GDN_DH).reshape(1, -1)
    dtb_x = jnp.repeat(p["l1_dt_bias"], GDN_DH).reshape(1, -1)
    h1 = _norm_fwd(x3, p["l1_mix_norm"], "l1_mix_norm_fwd")
    projx = _mm(h1, wx, name="l1_in")
    qkv = _gdn_conv_fwd(projx, w["l1_conv"], "l1_conv_fwd")
    beta, glog = _gdn_gates_fwd(projx, alog_x, dtb_x, "l1_gates_fwd")
    o_gdn, gdn_states = _gdn_fwd(qkv, beta, glog, projx, p["l1_o_norm"], "l1_gdn_fwd")
    x4 = _mm(o_gdn, w["l1_w_out"], res=x3, name="l1_out")
    x5, xa1 = xattn("l1_", x4)
    gathered(3, x5)
    x6, ff1 = ffn("l1_", x5)

    loss_part, dx6, grads["final_norm"] = _loss_head(x6, p["final_norm"], tgt, "loss_head")
    loss = lax.psum(loss_part[0, 0], ("x", "y", "c"))
    dx5 = ffn_bwd("l1_", ff1, dx6)
    dx4 = xattn_bwd("l1_", xa1, dx5)

    do_gdn = _mm(dx4, w["l1_w_out"], tb=True, name="l1_out_dx")
    grads["l1_w_out"] = _mm(o_gdn, dx4, ta=True, out_dtype=BF16, name="l1_out_dw")
    dqkv, dbeta, dglog, dprojx, grads["l1_o_norm"] = _gdn_bwd(
        qkv, beta, glog, projx, p["l1_o_norm"], gdn_states, do_gdn, "l1_gdn_bwd")
    dprojx, grads["l1_conv"] = _gdn_conv_bwd(projx, w["l1_conv"], dqkv, dprojx, "l1_conv_bwd")
    dprojx, dalog_x, ddtb_x = _gdn_gates_bwd(projx, alog_x, dtb_x, dbeta, dglog, dprojx, "l1_gates_bwd")
    dwx = _mm(h1, dprojx, ta=True, name="l1_in_dw")
    grads["l1_w_in"] = jnp.concatenate(
        [dwx[:, :nqkv], dwx[:, nqkv:nqkv + GDN_HEADS * GDN_DH].reshape(d, GDN_HEADS, GDN_DH).sum(-1),
         dwx[:, nqkv + GDN_HEADS * GDN_DH:].reshape(d, GDN_HEADS, GDN_DH).sum(-1)], axis=1)
    grads["l1_a_log"] = dalog_x.reshape(GDN_HEADS, GDN_DH).sum(-1)
    grads["l1_dt_bias"] = ddtb_x.reshape(GDN_HEADS, GDN_DH).sum(-1)
    gain = exchange(("l1_w_out", "l1_w_in", "l1_conv"), p["l1_mix_norm"], "l1_mix_grads")
    dx3, grads["l1_mix_norm"] = _mm_norm_bwd(dprojx, wx, x3, gain, dx4, "l1_in_dx")

    dx2 = ffn_bwd("l0_", ff0, dx3)
    dx1 = xattn_bwd("l0_", xa0, dx2)

    dmerged = _mm(dx1, w["l0_w_out"], tb=True, name="l0_out_dx")
    grads["l0_w_out"] = _mm(merged, dx1, ta=True, out_dtype=BF16, name="l0_out_dw")
    dproj, grads["l0_ret_norm"] = _ret_bwd(proj, cos, sin, p["l0_ret_norm"], ret_states, dmerged, "l0_ret_bwd")
    dzg, dg1, grads["l0_s5_b_glu"] = _s5_glu_bwd(dmerged, y, z, b_glu, "l0_s5_glu_bwd")
    grads["l0_s5_w_glu"] = _mm(gy, dzg, ta=True, out_dtype=BF16, name="l0_s5_glu_dw")
    s5_d_after = exchange(("l0_w_out", "l0_s5_w_glu"), s5_d, "l0_out_grads")
    dg2 = _mm(dzg, w["l0_s5_w_glu"], tb=True, name="l0_s5_glu_dx")
    dproj, da_s5, dbbt, dcct, grads["l0_s5_d"] = _s5_bwd(dg1, dg2, y, proj, st, bbt, cct, apow_rev, s5_d_after, dproj, "l0_s5_bwd")
    dbb_re, dbb_im = _s5_untile_b(dbbt)
    grads["l0_s5_c_re"], grads["l0_s5_c_im"] = _s5_untile_c(dcct)
    da_re, da_im = (t.reshape(S5_GROUPS, S5_STATE) for t in _s5_split(da_s5[0]))
    (grads["l0_s5_lambda_re"], grads["l0_s5_lambda_im"], grads["l0_s5_log_dt"], grads["l0_s5_b_re"],
     grads["l0_s5_b_im"]) = disc_vjp((da_re, da_im, dbb_re, dbb_im))

    def as_2d(t):
        return t.reshape(-1, t.shape[-1])

    def as_row(t):
        return t.reshape(1, -1)

    small_own = _pack_rows([as_row(grads[n]) for n in _REP_SMALL])
    big_own = [as_2d(grads[n].reshape(p[n].shape)) for n in _REP_BIG]
    rep_zones = [_into_slot(small_own, F32, me, "place_rep0")]
    rep_zones += [_into_slot(t.reshape(-1, LANES), BF16, me, f"place_rep{i + 1}") for i, t in enumerate(big_own)]
    rep_handle, rep_token = _push_start([], rep_zones, "rep_grads_start")

    grads["l0_w_in"] = _mm(h0, dproj, ta=True, out_dtype=BF16, pin=rep_token, name="l0_in_dw")
    gain = exchange(("l0_w_in",), p["l0_mix_norm"], "l0_mix_grads")
    dx0, grads["l0_mix_norm"] = _mm_norm_bwd(dproj, w["l0_w_in"], x0, gain, dx1, "l0_in_dx")

    last_own = _pack_rows([as_row(grads[_REP_LAST])])
    last_handle, _ = _push_start([], [_into_slot(last_own, F32, me, "place_rep_last")], "rep_last_start")
    last_land, = _push_wait(last_handle, dx0, "rep_last_wait")
    rep_lands = _push_wait(rep_handle, last_land, "rep_grads_wait")
    rep_land = rep_lands[0]

    outs = {}
    kinds = ("grad_", "delta_", "new_m_", "new_v_")
    for names, slots, handle, tag in pending:
        for n, own_slots, land in zip(names, slots, _push_wait(handle, rep_land, tag + "_wait")):
            shape = p[n].shape
            own = lax.dynamic_index_in_dim(own_slots, me, 0, keepdims=False)
            res = _adamw(land, own, *(p[pre + n].reshape(own.shape) for pre in ("", "m_", "v_")), "adamw_" + n)
            for kind, t in zip(kinds, res):
                outs[kind + n] = t.reshape(shape)
    for n, own, land in zip(_REP_BIG, big_own, rep_lands[1:]):
        res = _adamw(land.reshape((N_DEV,) + own.shape), None, *(as_2d(p[pre + n]) for pre in ("", "m_", "v_")), "adamw_" + n)
        for kind, t in zip(kinds, res):
            outs[kind + n] = t.reshape(p[n].shape)
    for names, land, own, nm in ((_REP_SMALL, rep_land, small_own, "adamw_small"), ((_REP_LAST,), last_land, last_own, "adamw_last")):
        res = _adamw_rows(land, own, *([as_row(p[pre + n]) for n in names] for pre in ("", "m_", "v_")), nm)
        for j, kind in enumerate(kinds):
            for i, n in enumerate(names):
                outs[kind + n] = res[j * len(names) + i].reshape(p[n].shape)

    return (loss, dx0[None]) + tuple(outs[kind + n] for kind in kinds for n in _WEIGHTS)
```

```python
import functools
import math

import numpy as np
import jax
import jax.numpy as jnp
from jax import lax
from jax.experimental import pallas as pl
from jax.experimental.pallas import tpu as pltpu

F32 = jnp.float32
BF16 = jnp.bfloat16
EPS = 1e-6
N_DEV = 8
LANES = 128
VMEM_LIMIT = 48 * 1024 * 1024
HI = lax.Precision.HIGHEST

RET_HEADS, RET_DH, RET_CHUNK = 4, 128, 128
S5_GROUPS, S5_GROUP, S5_STATE = 32, 16, 64
GDN_HEADS, GDN_DH, GDN_CHUNK, GDN_CONV = 8, 128, 64, 4
XA_HEADS, XA_DH = 4, 256
FFN_CONV = 3
SCAN_ROWS = 256

ADAM_LR, ADAM_B1, ADAM_B2, ADAM_EPS, ADAM_WD, ADAM_STEP = 0.001, 0.9, 0.999, 1e-08, 0.01, 10


def _cp(*sem):
    return pltpu.CompilerParams(dimension_semantics=sem if sem else None, vmem_limit_bytes=VMEM_LIMIT)


def _tile(n, cap):
    if n <= cap:
        return n
    best = None
    for t in range(LANES, cap + 1, LANES):
        if n % t == 0:
            best = t
    assert best is not None, n
    return best


def _dot(a, b, ca=1, cb=0, precision=None):
    return lax.dot_general(a, b, (((ca,), (cb,)), ((), ())), precision=precision, preferred_element_type=F32)


def _mxu(a, b, ca=1, cb=0):
    return _dot(a.astype(BF16), b.astype(BF16), ca, cb)


def _sigmoid(x):
    return 1.0 / (1.0 + jnp.exp(-x))


def _shift_down(x, k):
    r = pltpu.roll(x, k, 0)
    row = lax.broadcasted_iota(jnp.int32, (8,) + x.shape[1:], 0)
    return jnp.concatenate([jnp.where(row >= k, r[:8], 0.0), r[8:]], axis=0)


def _shift_up(x, k):
    n = x.shape[0]
    r = pltpu.roll(x, n - k, 0)
    row = lax.broadcasted_iota(jnp.int32, (8,) + x.shape[1:], 0)
    return jnp.concatenate([r[:n - 8], jnp.where(row < 8 - k, r[n - 8:], 0.0)], axis=0)


def _mesh_pos():
    return lax.axis_index("x"), lax.axis_index("y"), lax.axis_index("c")


def _slot(px, py, pc):
    return 4 * px + 2 * py + pc


def _all_peers(x, y, c):
    flips = [(fx, fy, fc) for fx in (0, 1) for fy in (0, 1) for fc in (0, 1)][1:]
    return [(1 - x if fx else x, 1 - y if fy else y, 1 - c if fc else c) for fx, fy, fc in flips]


_HBM = pl.BlockSpec(memory_space=pltpu.HBM)
_SEM = pl.BlockSpec(memory_space=pltpu.SEMAPHORE)
N_PEERS = N_DEV - 1


def _push_copies(srcs, lands, send_sems, recv_sems, start):
    x, y, c = _mesh_pos()
    me = _slot(x, y, c)
    out = []
    for k, to in enumerate(_all_peers(x, y, c)):
        for a in range(len(lands)):
            src = srcs[a].at[_slot(*to)] if a < len(srcs) else lands[a].at[me]
            dst = lands[a].at[me if start else _slot(*to)]
            out.append(pltpu.make_async_remote_copy(
                src_ref=src, dst_ref=dst, send_sem=send_sems.at[a * N_PEERS + k], recv_sem=recv_sems.at[a * N_PEERS + k],
                device_id=to, device_id_type=pl.DeviceIdType.MESH))
    return out


def _into_slot(x, dtype, me, name):
    r, c = x.shape
    cap = max(16, 512 * 1024 // c)
    tr = max(t for t in range(16, min(r, cap) + 1, 16) if r % t == 0) if r % 16 == 0 else r

    def body(me_ref, x_ref, o_ref):
        o_ref[...] = x_ref[...].astype(dtype)

    return pl.pallas_call(
        body, name=name, out_shape=jax.ShapeDtypeStruct((N_DEV, r, c), dtype),
        grid_spec=pltpu.PrefetchScalarGridSpec(
            num_scalar_prefetch=1, grid=(r // tr,),
            in_specs=[pl.BlockSpec((tr, c), lambda i, me_ref: (i, 0))],
            out_specs=pl.BlockSpec((None, tr, c), lambda i, me_ref: (me_ref[0], i, 0))),
        compiler_params=_cp("parallel"),
    )(me.reshape(1).astype(jnp.int32), x)


def _push_start(scatter, gather_lands, name):
    ns, n = len(scatter), len(scatter) + len(gather_lands)
    lands = [lax.empty(a.shape, a.dtype) for a in scatter] + list(gather_lands)

    def body(*refs):
        srcs, zones = refs[:ns], refs[ns:ns + n]
        for cp in _push_copies(srcs, zones, refs[ns + n], refs[ns + n + 1], True):
            cp.start()
        refs[-1][...] = jnp.zeros((8, LANES), F32)

    hbm_in = [pltpu.with_memory_space_constraint(a, pltpu.HBM) for a in list(scatter) + lands]
    res = pl.pallas_call(
        body, name=name,
        out_shape=(pltpu.SemaphoreType.DMA((n * N_PEERS,)), pltpu.SemaphoreType.DMA((n * N_PEERS,)))
        + tuple(pltpu.HBM(a.shape, a.dtype) for a in list(scatter) + lands)
        + (jax.ShapeDtypeStruct((8, LANES), F32),),
        in_specs=[_HBM] * (ns + n),
        out_specs=(_SEM, _SEM) + (_HBM,) * (ns + n) + (pl.BlockSpec(memory_space=pltpu.VMEM),),
        input_output_aliases={i: 2 + i for i in range(ns + n)},
        compiler_params=pltpu.CompilerParams(has_side_effects=pltpu.SideEffectType.DATAFLOW_SIDE_EFFECTING),
    )(*hbm_in)
    return (res[0], res[1], res[2:2 + ns], res[2 + ns:2 + ns + n]), res[-1]


def _push_wait(handle, after, name):
    send_sems, recv_sems, srcs, lands = handle
    ns, n = len(srcs), len(lands)

    def body(*refs):
        for cp in _push_copies(refs[:ns], refs[ns:ns + n], refs[ns + n], refs[ns + n + 1], False):
            cp.wait_send()
            cp.wait_recv()

    res = pl.pallas_call(
        body, name=name,
        out_shape=tuple(pltpu.HBM(a.shape, a.dtype) for a in list(srcs) + list(lands)),
        in_specs=[_HBM] * (ns + n) + [_SEM, _SEM, pl.BlockSpec(memory_space=pl.ANY)],
        out_specs=(_HBM,) * (ns + n),
        input_output_aliases={i: i for i in range(ns + n)},
        compiler_params=pltpu.CompilerParams(has_side_effects=pltpu.SideEffectType.DATAFLOW_SIDE_EFFECTING),
    )(*srcs, *lands, send_sems, recv_sems, after)
    return res[ns:]


def _mm(a, b, *, ta=False, tb=False, out_dtype=F32, res=None, pin=None, name="mm"):
    m, k = (a.shape[1], a.shape[0]) if ta else a.shape
    n = b.shape[0] if tb else b.shape[1]
    assert k == (b.shape[1] if tb else b.shape[0]), (a.shape, b.shape, ta, tb)
    tm, tn, tk = _tile(m, 1408), _tile(n, 1536), _tile(k, 1408)
    nk = k // tk
    has_res = res is not None
    n_in = 2 + has_res + (pin is not None)

    def body(*refs):
        a_ref, b_ref = refs[:2]
        r_ref = refs[2] if has_res else None
        o_ref = refs[n_in]
        part = _mxu(a_ref[...], b_ref[...], 0 if ta else 1, 1 if tb else 0)

        def finish(r):
            if has_res:
                r = r + r_ref[...].astype(F32)
            o_ref[...] = r.astype(out_dtype)

        if nk == 1:
            finish(part)
            return
        acc = refs[-1]
        kk = pl.program_id(2)

        @pl.when(kk == 0)
        def _():
            acc[...] = part

        @pl.when(kk > 0)
        def _():
            acc[...] += part

        @pl.when(kk == nk - 1)
        def _():
            finish(acc[...])

    a_spec = pl.BlockSpec((tk, tm), lambda i, j, kk: (kk, i)) if ta else pl.BlockSpec((tm, tk), lambda i, j, kk: (i, kk))
    b_spec = pl.BlockSpec((tn, tk), lambda i, j, kk: (j, kk)) if tb else pl.BlockSpec((tk, tn), lambda i, j, kk: (kk, j))
    o_spec = pl.BlockSpec((tm, tn), lambda i, j, kk: (i, j))
    in_specs = [a_spec, b_spec] + ([o_spec] if has_res else [])
    args = (a, b) + ((res,) if has_res else ())
    if pin is not None:
        in_specs.append(pl.BlockSpec(pin.shape, lambda i, j, kk: (0, 0)))
        args += (pin,)
    return pl.pallas_call(
        body, name=name, grid=(m // tm, n // tn, nk), in_specs=in_specs, out_specs=o_spec,
        out_shape=jax.ShapeDtypeStruct((m, n), out_dtype),
        scratch_shapes=[pltpu.VMEM((tm, tn), F32)] if nk > 1 else [],
        compiler_params=_cp("parallel", "parallel", "arbitrary"),
    )(*args)


def _mm_norm_bwd(dy, w, x, g, dres, name, pin=None):
    s, k = dy.shape
    d = w.shape[0]
    tm, tk = min(1024, s), _tile(k, 1408)
    nk = k // tk
    n_in = 5 + (pin is not None)

    def body(*refs):
        dy_ref, w_ref, x_ref, g_ref, dres_ref = refs[:5]
        dx_ref, dg_ref = refs[n_in], refs[n_in + 1]
        i, kk = pl.program_id(0), pl.program_id(1)

        @pl.when((i == 0) & (kk == 0))
        def _():
            dg_ref[...] = jnp.zeros_like(dg_ref)

        part = _mxu(dy_ref[...], w_ref[...], 1, 1)

        def finish(dh):
            xv = x_ref[...]
            r = lax.rsqrt(jnp.mean(xv * xv, axis=-1, keepdims=True) + EPS)
            xn = xv * r
            dg_ref[...] += jnp.sum(dh * xn, axis=0, keepdims=True)
            dhg = dh * g_ref[...]
            dx_ref[...] = dres_ref[...] + r * (dhg - xn * jnp.mean(dhg * xn, axis=-1, keepdims=True))

        if nk == 1:
            finish(part)
            return
        acc = refs[-1]

        @pl.when(kk == 0)
        def _():
            acc[...] = part

        @pl.when(kk > 0)
        def _():
            acc[...] += part

        @pl.when(kk == nk - 1)
        def _():
            finish(acc[...])

    row = pl.BlockSpec((tm, d), lambda i, kk: (i, 0))
    vec = pl.BlockSpec((1, d), lambda i, kk: (0, 0))
    in_specs = [pl.BlockSpec((tm, tk), lambda i, kk: (i, kk)), pl.BlockSpec((d, tk), lambda i, kk: (0, kk)), row, vec, row]
    args = (dy, w, x, g.reshape(1, d), dres)
    if pin is not None:
        in_specs.append(pl.BlockSpec(pin.shape, lambda i, kk: (0, 0)))
        args += (pin,)
    return pl.pallas_call(
        body, name=name, grid=(s // tm, nk), in_specs=in_specs, out_specs=[row, vec],
        out_shape=[jax.ShapeDtypeStruct((s, d), F32), jax.ShapeDtypeStruct((1, d), F32)],
        scratch_shapes=[pltpu.VMEM((tm, d), F32)] if nk > 1 else [],
        compiler_params=_cp("arbitrary", "arbitrary"),
    )(*args)


def _norm_fwd(x, g, name):
    s, d = x.shape
    tr = min(512, s)

    def body(x_ref, g_ref, o_ref):
        xv = x_ref[...]
        r = lax.rsqrt(jnp.mean(xv * xv, axis=-1, keepdims=True) + EPS)
        o_ref[...] = (xv * r * g_ref[...]).astype(BF16)

    row = pl.BlockSpec((tr, d), lambda i: (i, 0))
    return pl.pallas_call(
        body, name=name, grid=(s // tr,), in_specs=[row, pl.BlockSpec((1, d), lambda i: (0, 0))],
        out_specs=row, out_shape=jax.ShapeDtypeStruct((s, d), BF16), compiler_params=_cp("parallel"),
    )(x, g.reshape(1, d))


def _norm_bwd(x, g, dh, dres, name):
    s, d = x.shape
    tr = min(512, s)

    def body(x_ref, g_ref, dh_ref, dres_ref, dx_ref, dg_ref):
        @pl.when(pl.program_id(0) == 0)
        def _():
            dg_ref[...] = jnp.zeros_like(dg_ref)

        xv = x_ref[...]
        r = lax.rsqrt(jnp.mean(xv * xv, axis=-1, keepdims=True) + EPS)
        xn = xv * r
        dhv = dh_ref[...].astype(F32)
        dg_ref[...] += jnp.sum(dhv * xn, axis=0, keepdims=True)
        dhg = dhv * g_ref[...]
        dx_ref[...] = dres_ref[...] + r * (dhg - xn * jnp.mean(dhg * xn, axis=-1, keepdims=True))

    row = pl.BlockSpec((tr, d), lambda i: (i, 0))
    vec = pl.BlockSpec((1, d), lambda i: (0, 0))
    return pl.pallas_call(
        body, name=name, grid=(s // tr,), in_specs=[row, vec, row, row], out_specs=[row, vec],
        out_shape=[jax.ShapeDtypeStruct((s, d), F32), jax.ShapeDtypeStruct((1, d), F32)],
        compiler_params=_cp("arbitrary"),
    )(x, g.reshape(1, d), dh, dres)


def _loss_head(x, g, tgt, name):
    s, d = x.shape
    tr = min(512, s)

    def body(x_ref, g_ref, t_ref, l_ref, dx_ref, dg_ref):
        @pl.when(pl.program_id(0) == 0)
        def _():
            dg_ref[...] = jnp.zeros_like(dg_ref)
            l_ref[...] = jnp.zeros_like(l_ref)

        xv = x_ref[...]
        r = lax.rsqrt(jnp.mean(xv * xv, axis=-1, keepdims=True) + EPS)
        xn = xv * r
        err = xn * g_ref[...] - t_ref[...]
        part = 0.5 * jnp.sum(jnp.mean(err * err, axis=-1, keepdims=True), axis=0, keepdims=True)
        l_ref[...] += jnp.broadcast_to(part, l_ref.shape)
        dy = err * (1.0 / d)
        dg_ref[...] += jnp.sum(dy * xn, axis=0, keepdims=True)
        dyg = dy * g_ref[...]
        dx_ref[...] = r * (dyg - xn * jnp.mean(dyg * xn, axis=-1, keepdims=True))

    row = pl.BlockSpec((tr, d), lambda i: (i, 0))
    vec = pl.BlockSpec((1, d), lambda i: (0, 0))
    return pl.pallas_call(
        body, name=name, grid=(s // tr,), in_specs=[row, vec, row],
        out_specs=[pl.BlockSpec((1, LANES), lambda i: (0, 0)), row, vec],
        out_shape=[jax.ShapeDtypeStruct((1, LANES), F32), jax.ShapeDtypeStruct((s, d), F32),
                   jax.ShapeDtypeStruct((1, d), F32)],
        compiler_params=_cp("arbitrary"),
    )(x, g.reshape(1, d), tgt)


def _sum_slots(landed_slot, own):
    me = _slot(*_mesh_pos())
    mine = own.astype(F32)
    g = jnp.where(me == 0, mine, landed_slot(0).astype(F32))
    for i in range(1, N_DEV):
        g = g + jnp.where(me == i, mine, landed_slot(i).astype(F32))
    return g


def _adam_update(g, w, m, v):
    mm = ADAM_B1 * m + (1.0 - ADAM_B1) * g
    vv = ADAM_B2 * v + (1.0 - ADAM_B2) * (g * g)
    m_hat = mm / (1.0 - ADAM_B1 ** ADAM_STEP)
    v_hat = vv / (1.0 - ADAM_B2 ** ADAM_STEP)
    return g, -ADAM_LR * (m_hat / (jnp.sqrt(v_hat) + ADAM_EPS) + ADAM_WD * w), mm, vv


def _adamw_rows(landed, own, ws, ms, vs, name):
    k = len(ws)
    sizes = [w.shape[1] for w in ws]

    def body(*refs):
        p_ref, o_ref = refs[:2]
        w_refs, m_refs, v_refs = refs[2:2 + k], refs[2 + k:2 + 2 * k], refs[2 + 2 * k:2 + 3 * k]
        outs = refs[2 + 3 * k:]
        for i, n in enumerate(sizes):
            g = _sum_slots(lambda s: p_ref[s, i:i + 1, :n], o_ref[i:i + 1, :n])
            res = _adam_update(g, w_refs[i][...], m_refs[i][...], v_refs[i][...])
            for j in range(4):
                outs[j * k + i][...] = res[j]

    return pl.pallas_call(
        body, name=name, out_shape=[jax.ShapeDtypeStruct((1, n), F32) for _ in range(4) for n in sizes],
    )(landed, own, *ws, *ms, *vs)


def _adamw(landed, own, w, m, v, name):
    r, c = w.shape
    cap = max(8, 256 * 1024 // c)
    tr = max(t for t in range(8, min(r, cap) + 1, 8) if r % t == 0) if r % 8 == 0 else r
    gathered = own is None

    def body(*refs):
        p_ref = refs[0]
        w_ref, m_ref, v_ref, g_ref, d_ref, nm_ref, nv_ref = refs[1 if gathered else 2:]
        if gathered:
            g = p_ref[0].astype(F32)
            for i in range(1, N_DEV):
                g = g + p_ref[i].astype(F32)
        else:
            g = _sum_slots(lambda i: p_ref[i], refs[1][...])
        g_ref[...], d_ref[...], nm_ref[...], nv_ref[...] = _adam_update(g, w_ref[...], m_ref[...], v_ref[...])

    blk = pl.BlockSpec((tr, c), lambda i: (i, 0))
    n_blk = 3 if gathered else 4
    return pl.pallas_call(
        body, name=name, grid=(r // tr,),
        in_specs=[pl.BlockSpec((N_DEV, tr, c), lambda i: (0, i, 0))] + [blk] * n_blk,
        out_specs=[blk] * 4, out_shape=[jax.ShapeDtypeStruct((r, c), F32)] * 4,
        compiler_params=_cp("parallel"),
    )(*((landed,) if gathered else (landed, own)), w, m, v)


def _conv_taps(x, kw):
    return [_shift_down(x, kw - 1 - j) for j in range(kw - 1)] + [x]


def _conv_fwd(taps, w_ref):
    acc = w_ref[0:1, :] * taps[0]
    for j in range(1, len(taps)):
        acc = acc + w_ref[j:j + 1, :] * taps[j]
    return acc


def _conv_bwd(taps, dy, w_ref, dw_ref):
    kw = len(taps)
    c = dy.shape[1]
    diag = lax.broadcasted_iota(jnp.int32, (c, c), 0) == lax.broadcasted_iota(jnp.int32, (c, c), 1)
    dyb = dy.astype(BF16)
    dx = w_ref[kw - 1:kw, :] * dy
    for j in range(kw):
        prod = _dot(taps[j].astype(BF16), dyb, 0, 0)
        dw_ref[j:j + 1, :] = jnp.sum(jnp.where(diag, prod, 0.0), axis=0, keepdims=True)
        if j < kw - 1:
            dx = dx + w_ref[j:j + 1, :] * _shift_up(dy, kw - 1 - j)
    return dx


def _ffn_act_fwd(pre, cw, name):
    s, f2 = pre.shape
    nt = f2 // 2 // LANES

    def body(pu_ref, pg_ref, wu_ref, wg_ref, o_ref):
        up = _conv_fwd(_conv_taps(pu_ref[...].astype(F32), FFN_CONV), wu_ref)
        gate = _conv_fwd(_conv_taps(pg_ref[...].astype(F32), FFN_CONV), wg_ref)
        o_ref[...] = (gate * _sigmoid(gate) * up).astype(BF16)

    def col(rows, off):
        return pl.BlockSpec((rows, LANES), lambda j: (0, j + off))

    return pl.pallas_call(
        body, name=name, grid=(nt,),
        in_specs=[col(s, 0), col(s, nt), col(FFN_CONV, 0), col(FFN_CONV, nt)], out_specs=col(s, 0),
        out_shape=jax.ShapeDtypeStruct((s, f2 // 2), BF16), compiler_params=_cp("parallel"),
    )(pre, pre, cw, cw)


def _ffn_act_bwd(pre, cw, dact, name):
    s, f2 = pre.shape
    f = f2 // 2
    nt = f // LANES

    def body(pu_ref, pg_ref, wu_ref, wg_ref, da_ref, dpu_ref, dpg_ref, dwu_ref, dwg_ref):
        pu, pg = pu_ref[...].astype(F32), pg_ref[...].astype(F32)
        tu, tg = _conv_taps(pu, FFN_CONV), _conv_taps(pg, FFN_CONV)
        up = _conv_fwd(tu, wu_ref)
        gate = _conv_fwd(tg, wg_ref)
        sg = _sigmoid(gate)
        da = da_ref[...].astype(F32)
        dup = da * gate * sg
        dgate = da * up * (sg * (1.0 + gate * (1.0 - sg)))
        dpu_ref[...] = _conv_bwd(tu, dup, wu_ref, dwu_ref).astype(BF16)
        dpg_ref[...] = _conv_bwd(tg, dgate, wg_ref, dwg_ref).astype(BF16)

    def col(rows, off):
        return pl.BlockSpec((rows, LANES), lambda j: (0, j + off))

    return pl.pallas_call(
        body, name=name, grid=(nt,),
        in_specs=[col(s, 0), col(s, nt), col(FFN_CONV, 0), col(FFN_CONV, nt), col(s, 0)],
        out_specs=[col(s, 0), col(s, 0), col(FFN_CONV, 0), col(FFN_CONV, 0)],
        out_shape=[jax.ShapeDtypeStruct((s, f), BF16), jax.ShapeDtypeStruct((s, f), BF16),
                   jax.ShapeDtypeStruct((FFN_CONV, f), F32), jax.ShapeDtypeStruct((FFN_CONV, f), F32)],
        compiler_params=_cp("parallel"),
    )(pre, pre, cw, cw, dact)


def _xa_probs(qh, kh):
    sc = _mxu(qh, kh, 1, 1) * (XA_DH ** -0.5)
    e = jnp.exp(sc - jnp.max(sc, axis=-1, keepdims=True))
    return e / jnp.sum(e, axis=-1, keepdims=True)


def _xattn_fwd(q, kv, name):
    s, d = q.shape
    m = kv.shape[0]
    tr = min(512, s)

    def body(q_ref, kv_ref, o_ref):
        for h in range(XA_HEADS):
            lo, hi = h * XA_DH, (h + 1) * XA_DH
            p = _xa_probs(q_ref[:, lo:hi], kv_ref[:, lo:hi])
            o_ref[:, lo:hi] = _mxu(p, kv_ref[:, d + lo:d + hi]).astype(BF16)

    row = pl.BlockSpec((tr, d), lambda i: (i, 0))
    return pl.pallas_call(
        body, name=name, grid=(s // tr,), in_specs=[row, pl.BlockSpec((m, 2 * d), lambda i: (0, 0))],
        out_specs=row, out_shape=jax.ShapeDtypeStruct((s, d), BF16), compiler_params=_cp("parallel"),
    )(q, kv)


def _xattn_bwd(q, kv, do, name):
    s, d = q.shape
    m = kv.shape[0]
    tr = min(512, s)

    def body(q_ref, kv_ref, do_ref, dq_ref, dkv_ref):
        @pl.when(pl.program_id(0) == 0)
        def _():
            dkv_ref[...] = jnp.zeros_like(dkv_ref)

        for h in range(XA_HEADS):
            lo, hi = h * XA_DH, (h + 1) * XA_DH
            qh, kh, vh = q_ref[:, lo:hi], kv_ref[:, lo:hi], kv_ref[:, d + lo:d + hi]
            doh = do_ref[:, lo:hi]
            p = _xa_probs(qh, kh)
            dp = _mxu(doh, vh, 1, 1)
            ds = p * (dp - jnp.sum(p * dp, axis=-1, keepdims=True)) * (XA_DH ** -0.5)
            dq_ref[:, lo:hi] = _mxu(ds, kh).astype(BF16)
            dkv_ref[:, lo:hi] += _mxu(ds, qh, 0, 0)
            dkv_ref[:, d + lo:d + hi] += _mxu(p, doh, 0, 0)

    row = pl.BlockSpec((tr, d), lambda i: (i, 0))
    full = pl.BlockSpec((m, 2 * d), lambda i: (0, 0))
    return pl.pallas_call(
        body, name=name, grid=(s // tr,), in_specs=[row, full, row], out_specs=[row, full],
        out_shape=[jax.ShapeDtypeStruct((s, d), BF16), jax.ShapeDtypeStruct((m, 2 * d), F32)],
        compiler_params=_cp("arbitrary"),
    )(q, kv, do)


def _ret_tables():
    c = RET_CHUNK
    lg = np.log1p(-np.exp2(-5.0 - np.arange(RET_HEADS, dtype=np.float32))).astype(np.float32)
    idx = np.arange(c, dtype=np.float32)
    diff = idx[:, None] - idx[None, :]
    intra = np.where(diff >= 0, np.exp(lg[:, None, None] * np.where(diff >= 0, diff, 0.0)), 0.0)
    rk = np.broadcast_to(np.exp(lg[:, None] * (c - 1 - idx))[:, :, None], (RET_HEADS, c, LANES))
    rq = np.broadcast_to(np.exp(lg[:, None] * (idx + 1))[:, :, None], (RET_HEADS, c, LANES))
    return jnp.asarray(np.stack([intra, rk, rq], axis=1).astype(np.float32))


def _rope_tables(s):
    half = RET_DH // 2
    inv = jnp.exp(-math.log(10000.0) * jnp.arange(half, dtype=F32) / half)
    ang = jnp.arange(s, dtype=F32)[:, None] * inv[None, :]
    cos, sin = jnp.cos(ang), jnp.sin(ang)
    return jnp.concatenate([cos, cos], axis=1), jnp.concatenate([-sin, sin], axis=1)


def _ret_specs(n_of):
    c, w = RET_CHUNK, RET_HEADS * RET_DH

    def part(off):
        return pl.BlockSpec((c, w), lambda n: (n_of(n), off))

    pos = pl.BlockSpec((c, RET_DH), lambda n: (n_of(n), 0))
    gain = pl.BlockSpec((1, w), lambda n: (0, 0))
    tab = pl.BlockSpec((RET_HEADS, 3, c, LANES), lambda n: (0, 0, 0, 0))
    st = pl.BlockSpec((RET_HEADS, None, RET_DH, RET_DH), lambda n: (0, n_of(n), 0, 0))
    return part, pos, gain, tab, st


def _rheads(x):
    return jnp.stack([x[:, h * RET_DH:(h + 1) * RET_DH] for h in range(RET_HEADS)], axis=0)


def _runheads(x):
    return jnp.concatenate([x[h] for h in range(RET_HEADS)], axis=1)


def _rope(x, cos, sin):
    return x * cos + pltpu.roll(x, RET_DH // 2, 2) * sin


def _ret_chunk(q_ref, k_ref, v_ref, cos_ref, sin_ref, tab_ref, prev):
    cos, sin = cos_ref[...], sin_ref[...]
    q = _rope(_rheads(q_ref[...]), cos, sin)
    k = _rope(_rheads(k_ref[...]), cos, sin) * (RET_DH ** -0.5)
    v = _rheads(v_ref[...])
    scores = _bmxu(q, k, 2, 2) * tab_ref[:, 0]
    qdec = q * tab_ref[:, 2]
    kdec = k * tab_ref[:, 1]
    o = _bmxu(scores, v) + _bmxu(qdec, prev)
    return q, k, v, scores, qdec, kdec, o


def _ret_fwd(proj, cos, sin, gain, name):
    s = proj.shape[0]
    c = RET_CHUNK
    nc = s // c
    part, pos, gvec, tab, st = _ret_specs(lambda n: n)

    def body(q_ref, k_ref, v_ref, g_ref, cos_ref, sin_ref, rn_ref, tab_ref, o_ref, st_ref, state):
        @pl.when(pl.program_id(0) == 0)
        def _():
            state[...] = jnp.zeros_like(state)

        prev = state[...]
        st_ref[...] = prev
        _, _, v, _, _, kdec, o = _ret_chunk(q_ref, k_ref, v_ref, cos_ref, sin_ref, tab_ref, prev)
        state[...] = prev * tab_ref[:, 2, c - 1:c, :] + _bmxu(kdec, v, 1, 1)
        r = lax.rsqrt(jnp.mean(o * o, axis=-1, keepdims=True) + EPS)
        gate = g_ref[...]
        o_ref[...] = (_runheads(o * r) * rn_ref[...] * (gate * _sigmoid(gate))).astype(BF16)

    return pl.pallas_call(
        body, name=name, grid=(nc,),
        in_specs=[part(0), part(1), part(2), part(3), pos, pos, gvec, tab],
        out_specs=[part(0), st],
        out_shape=[jax.ShapeDtypeStruct((s, RET_HEADS * RET_DH), BF16),
                   jax.ShapeDtypeStruct((RET_HEADS, nc, RET_DH, RET_DH), F32)],
        scratch_shapes=[pltpu.VMEM((RET_HEADS, RET_DH, RET_DH), F32)],
        compiler_params=_cp("arbitrary"),
    )(proj, proj, proj, proj, cos, sin, gain.reshape(1, -1), _ret_tables())


def _ret_bwd(proj, cos, sin, gain, states, dmerged, name):
    s = proj.shape[0]
    c = RET_CHUNK
    nc = s // c
    width = RET_HEADS * RET_DH
    part, pos, gvec, tab, st = _ret_specs(lambda n: nc - 1 - n)

    def body(q_ref, k_ref, v_ref, g_ref, cos_ref, sin_ref, rn_ref, tab_ref, st_ref, do_ref,
             dp_ref, drn_ref, carry):
        @pl.when(pl.program_id(0) == 0)
        def _():
            carry[...] = jnp.zeros_like(carry)
            drn_ref[...] = jnp.zeros_like(drn_ref)

        prev = st_ref[...]
        q, k, v, scores, qdec, kdec, o = _ret_chunk(q_ref, k_ref, v_ref, cos_ref, sin_ref, tab_ref, prev)
        r = lax.rsqrt(jnp.mean(o * o, axis=-1, keepdims=True) + EPS)
        on = o * r
        on2 = _runheads(on)
        gate = g_ref[...]
        sg = _sigmoid(gate)
        sil = gate * sg
        dout = do_ref[...]
        rn = rn_ref[...]
        dp_ref[:, 3 * width:] = (dout * on2 * rn * (sg * (1.0 + gate * (1.0 - sg)))).astype(BF16)
        drn_ref[...] += jnp.sum(dout * on2 * sil, axis=0, keepdims=True)
        don = _rheads(dout * rn * sil)
        do = r * (don - on * jnp.mean(don * on, axis=-1, keepdims=True))
        dc = carry[...]
        dsc = _bmxu(do, v, 2, 2) * tab_ref[:, 0]
        dq = _bmxu(dsc, k) + _bmxu(do, prev, 2, 2) * tab_ref[:, 2]
        dk = _bmxu(dsc, q, 1, 1) + _bmxu(v, dc, 2, 2) * tab_ref[:, 1]
        dv = _bmxu(scores, do, 1, 1) + _bmxu(kdec, dc)
        carry[...] = _bmxu(qdec, do, 1, 1) + dc * tab_ref[:, 2, c - 1:c, :]
        cos, sin = cos_ref[...], sin_ref[...]
        dk = dk * (RET_DH ** -0.5)
        dp_ref[:, :width] = _runheads(dq * cos + pltpu.roll(dq * sin, RET_DH // 2, 2)).astype(BF16)
        dp_ref[:, width:2 * width] = _runheads(dk * cos + pltpu.roll(dk * sin, RET_DH // 2, 2)).astype(BF16)
        dp_ref[:, 2 * width:3 * width] = _runheads(dv).astype(BF16)

    return pl.pallas_call(
        body, name=name, grid=(nc,),
        in_specs=[part(0), part(1), part(2), part(3), pos, pos, gvec, tab, st, part(0)],
        out_specs=[pl.BlockSpec((c, 4 * width), lambda n: (nc - 1 - n, 0)), gvec],
        out_shape=[jax.ShapeDtypeStruct(proj.shape, BF16), jax.ShapeDtypeStruct((1, width), F32)],
        scratch_shapes=[pltpu.VMEM((RET_HEADS, RET_DH, RET_DH), F32)],
        compiler_params=_cp("arbitrary"),
    )(proj, proj, proj, proj, cos, sin, gain.reshape(1, -1), _ret_tables(), states, dmerged)


S5_TILE = 512


def _cmul_add(xr, xi, ar, ai, yr, yi):
    return xr + ar * yr - ai * yi, xi + ar * yi + ai * yr


def _s5_pow_tables(a_il, name):
    r = SCAN_ROWS
    t = S5_TILE
    w2 = a_il.shape[1]

    def body(a_ref, up_ref, dn_ref):
        for j in range(w2 // (2 * t)):
            re, im = pl.ds(2 * t * j, t), pl.ds(2 * t * j + t, t)
            up_ref[0:1, re] = a_ref[:, re]
            up_ref[0:1, im] = a_ref[:, im]
            dn_ref[r - 1:r, re] = a_ref[:, re]
            dn_ref[r - 1:r, im] = -a_ref[:, im]
            n = 1
            while n < r:
                lr, li = up_ref[n - 1:n, re], up_ref[n - 1:n, im]
                xr, xi = up_ref[0:n, re], up_ref[0:n, im]
                up_ref[n:2 * n, re] = xr * lr - xi * li
                up_ref[n:2 * n, im] = xr * li + xi * lr
                yr, yi = dn_ref[r - n:r, re], dn_ref[r - n:r, im]
                dn_ref[r - 2 * n:r - n, re] = yr * lr + yi * li
                dn_ref[r - 2 * n:r - n, im] = yi * lr - yr * li
                n *= 2

    return pl.pallas_call(
        body, name=name, out_shape=[jax.ShapeDtypeStruct((r, w2), F32)] * 2, compiler_params=_cp(),
    )(a_il)


def _s5_scan_fwd(bu, apow, name):
    s, w2 = bu.shape
    r = SCAN_ROWS
    t = S5_TILE
    steps = r.bit_length() - 1

    def body(b_ref, p_ref, o_ref, cr, ci):
        @pl.when(pl.program_id(1) == 0)
        def _():
            cr[...] = jnp.zeros_like(cr)
            ci[...] = jnp.zeros_like(ci)

        xr, xi = b_ref[:, :t], b_ref[:, t:]
        for k in range(steps):
            sh = 1 << k
            xr, xi = _cmul_add(xr, xi, p_ref[sh - 1:sh, :t], p_ref[sh - 1:sh, t:],
                               _shift_down(xr, sh), _shift_down(xi, sh))
        xr, xi = _cmul_add(xr, xi, p_ref[:, :t], p_ref[:, t:], cr[...], ci[...])
        o_ref[:, :t] = xr
        o_ref[:, t:] = xi
        cr[...] = xr[r - 1:r, :]
        ci[...] = xi[r - 1:r, :]

    blk = pl.BlockSpec((r, 2 * t), lambda j, i: (i, j))
    return pl.pallas_call(
        body, name=name, grid=(w2 // (2 * t), s // r),
        in_specs=[blk, pl.BlockSpec((r, 2 * t), lambda j, i: (0, j))], out_specs=blk,
        out_shape=jax.ShapeDtypeStruct((s, w2), F32),
        scratch_shapes=[pltpu.VMEM((1, t), F32), pltpu.VMEM((1, t), F32)],
        compiler_params=_cp("parallel", "arbitrary"),
    )(bu, apow)


def _s5_scan_bwd(dst, apow_rev, st, name):
    s, w2 = dst.shape
    r = SCAN_ROWS
    t = S5_TILE
    nb = s // r
    steps = r.bit_length() - 1

    def body(d_ref, p_ref, s_ref, sp_ref, g_ref, da_ref, cr, ci):
        i = pl.program_id(1)

        @pl.when(i == 0)
        def _():
            cr[...] = jnp.zeros_like(cr)
            ci[...] = jnp.zeros_like(ci)
            da_ref[...] = jnp.zeros_like(da_ref)

        xr, xi = d_ref[:, :t], d_ref[:, t:]
        for k in range(steps):
            sh = 1 << k
            xr, xi = _cmul_add(xr, xi, p_ref[r - sh:r - sh + 1, :t], p_ref[r - sh:r - sh + 1, t:],
                               _shift_up(xr, sh), _shift_up(xi, sh))
        xr, xi = _cmul_add(xr, xi, p_ref[:, :t], p_ref[:, t:], cr[...], ci[...])
        g_ref[:, :t] = xr.astype(BF16)
        g_ref[:, t:] = xi.astype(BF16)
        cr[...] = xr[0:1, :]
        ci[...] = xi[0:1, :]
        first = i == nb - 1
        row = lax.broadcasted_iota(jnp.int32, (r, t), 0)
        last_r = jnp.where(first, 0.0, sp_ref[7:8, :t])
        last_i = jnp.where(first, 0.0, sp_ref[7:8, t:])
        pr = jnp.where(row == 0, last_r, pltpu.roll(s_ref[:, :t], 1, 0))
        pi = jnp.where(row == 0, last_i, pltpu.roll(s_ref[:, t:], 1, 0))
        da_ref[:, :t] += jnp.sum(xr * pr + xi * pi, axis=0, keepdims=True)
        da_ref[:, t:] += jnp.sum(xi * pr - xr * pi, axis=0, keepdims=True)

    blk = pl.BlockSpec((r, 2 * t), lambda j, i: (nb - 1 - i, j))
    halo = pl.BlockSpec((8, 2 * t), lambda j, i: (jnp.maximum((nb - 1 - i) * (r // 8) - 1, 0), j))
    vec = pl.BlockSpec((1, 2 * t), lambda j, i: (0, j))
    return pl.pallas_call(
        body, name=name, grid=(w2 // (2 * t), nb),
        in_specs=[blk, pl.BlockSpec((r, 2 * t), lambda j, i: (0, j)), blk, halo], out_specs=[blk, vec],
        out_shape=[jax.ShapeDtypeStruct((s, w2), BF16), jax.ShapeDtypeStruct((1, w2), F32)],
        scratch_shapes=[pltpu.VMEM((1, t), F32), pltpu.VMEM((1, t), F32)],
        compiler_params=_cp("parallel", "arbitrary"),
    )(dst, apow_rev, st, st)


_GELU_C = math.sqrt(2.0 / math.pi)
_GELU_A = 0.044715


def _gelu(y):
    return 0.5 * y * (1.0 + jnp.tanh(_GELU_C * (y + _GELU_A * y * y * y)))


def _gelu_grad(y):
    th = jnp.tanh(_GELU_C * (y + _GELU_A * y * y * y))
    return 0.5 * (1.0 + th) + 0.5 * y * (1.0 - th * th) * _GELU_C * (1.0 + 3.0 * _GELU_A * y * y)


def _rows_shift(x, k, axis, up):
    n = x.shape[axis]
    idx = lax.broadcasted_iota(jnp.int32, x.shape, axis)
    if up:
        return jnp.where(idx < n - k, pltpu.roll(x, n - k, axis), 0.0)
    return jnp.where(idx >= k, pltpu.roll(x, k, axis), 0.0)


def _scan_block(xr, xi, pr, pi, cr, ci, rev):
    r, w = xr.shape
    nt = r // 8
    x3r, x3i = xr.reshape(nt, 8, w), xi.reshape(nt, 8, w)
    p3r, p3i = pr.reshape(nt, 8, w), pi.reshape(nt, 8, w)

    def power(rows):
        t = r - rows if rev else rows - 1
        return pr[t:t + 1, :], pi[t:t + 1, :]

    tile_row = lax.broadcasted_iota(jnp.int32, (8, w), 0)
    for sh in (1, 2, 4):
        ar, ai = power(sh)
        keep = tile_row < 8 - sh if rev else tile_row >= sh
        mr, mi = jnp.where(keep, ar, 0.0)[None], jnp.where(keep, ai, 0.0)[None]
        turn = 8 - sh if rev else sh
        x3r, x3i = _cmul_add(x3r, x3i, mr, mi, pltpu.roll(x3r, turn, 1), pltpu.roll(x3i, turn, 1))
    edge = 0 if rev else 7
    lr, li = x3r[:, edge, :], x3i[:, edge, :]
    sh = 1
    while sh < nt:
        ar, ai = power(8 * sh)
        lr, li = _cmul_add(lr, li, ar, ai, _rows_shift(lr, sh, 0, rev), _rows_shift(li, sh, 0, rev))
        sh *= 2
    tr_, ti_ = p3r[:, edge, :], p3i[:, edge, :]
    first = lax.broadcasted_iota(jnp.int32, (nt, w), 0) == (nt - 1 if rev else 0)
    wr = jnp.where(first, 1.0, _rows_shift(tr_, 1, 0, rev))
    wi = jnp.where(first, 0.0, _rows_shift(ti_, 1, 0, rev))
    er, ei = _cmul_add(_rows_shift(lr, 1, 0, rev), _rows_shift(li, 1, 0, rev), wr, wi, cr, ci)
    a8r, a8i = (p3r[nt - 1], p3i[nt - 1]) if rev else (p3r[0], p3i[0])
    x3r, x3i = _cmul_add(x3r, x3i, a8r[None], a8i[None], er[:, None, :], ei[:, None, :])
    outr, outi = x3r.reshape(r, w), x3i.reshape(r, w)
    last = 0 if rev else r - 1
    return outr, outi, outr[last:last + 1, :], outi[last:last + 1, :]


def _s5_tile_specs(n_of, r):
    t = S5_TILE
    ucol = 4 * RET_HEADS * RET_DH // LANES
    u = pl.BlockSpec((r, LANES), lambda j, i: (n_of(i), ucol + j))
    col = pl.BlockSpec((r, LANES), lambda j, i: (n_of(i), j))
    state = pl.BlockSpec((r, 2 * t), lambda j, i: (n_of(i), j))
    table = pl.BlockSpec((r, 2 * t), lambda j, i: (0, j))
    bbt = pl.BlockSpec((None, LANES, 2 * t), lambda j, i: (j, 0, 0))
    cct = pl.BlockSpec((None, 2 * t, LANES), lambda j, i: (j, 0, 0))
    vec = pl.BlockSpec((1, LANES), lambda j, i: (0, j))
    return u, col, state, table, bbt, cct, vec


def _s5_fwd(proj, bbt, cct, apow, dvec, name):
    s = proj.shape[0]
    r, t = SCAN_ROWS, S5_TILE
    w = S5_GROUPS * S5_GROUP
    u_s, col, state, table, bb_s, cc_s, vec = _s5_tile_specs(lambda i: i, r)

    def body(u_ref, bb_ref, cc_ref, p_ref, d_ref, st_ref, y_ref, g_ref, cr, ci):
        @pl.when(pl.program_id(1) == 0)
        def _():
            cr[...] = jnp.zeros_like(cr)
            ci[...] = jnp.zeros_like(ci)

        u = u_ref[...]
        bu = _mxu(u, bb_ref[...])
        xr, xi, cr[...], ci[...] = _scan_block(bu[:, :t], bu[:, t:], p_ref[:, :t], p_ref[:, t:], cr[...], ci[...], False)
        st_ref[:, :t] = xr
        st_ref[:, t:] = xi
        y = _mxu(xr, cc_ref[:t, :]) + _mxu(xi, cc_ref[t:, :]) + d_ref[...] * u
        y_ref[...] = y
        g_ref[...] = _gelu(y).astype(BF16)

    return pl.pallas_call(
        body, name=name, grid=(2 * S5_GROUPS * S5_STATE // (2 * t), s // r),
        in_specs=[u_s, bb_s, cc_s, table, vec], out_specs=[state, col, col],
        out_shape=[jax.ShapeDtypeStruct((s, 2 * S5_GROUPS * S5_STATE), F32), jax.ShapeDtypeStruct((s, w), F32),
                   jax.ShapeDtypeStruct((s, w), BF16)],
        scratch_shapes=[pltpu.VMEM((1, t), F32), pltpu.VMEM((1, t), F32)],
        compiler_params=_cp("parallel", "arbitrary"),
    )(proj, bbt, cct, apow, dvec)


def _s5_bwd(dg1, dg2, y, proj, st, bbt, cct, apow_rev, dvec, dproj, name):
    s = proj.shape[0]
    r, t = SCAN_ROWS, S5_TILE
    nb = s // r
    w = S5_GROUPS * S5_GROUP
    u_s, col, state, table, bb_s, cc_s, vec = _s5_tile_specs(lambda i: nb - 1 - i, r)
    halo = pl.BlockSpec((8, 2 * t), lambda j, i: (jnp.maximum((nb - 1 - i) * (r // 8) - 1, 0), j))
    acc = pl.BlockSpec((1, 2 * t), lambda j, i: (0, j))

    def body(a_ref, b_ref, y_ref, u_ref, s_ref, sp_ref, bb_ref, cc_ref, p_ref, d_ref, _,
             du_ref, da_ref, dbb_ref, dcc_ref, dd_ref, cr, ci):
        i = pl.program_id(1)

        @pl.when(i == 0)
        def _():
            cr[...] = jnp.zeros_like(cr)
            ci[...] = jnp.zeros_like(ci)
            da_ref[...] = jnp.zeros_like(da_ref)
            dbb_ref[...] = jnp.zeros_like(dbb_ref)
            dcc_ref[...] = jnp.zeros_like(dcc_ref)
            dd_ref[...] = jnp.zeros_like(dd_ref)

        u = u_ref[...]
        dy = (a_ref[...] + b_ref[...]) * _gelu_grad(y_ref[...])
        dd_ref[...] += jnp.sum(dy * u, axis=0, keepdims=True)
        sr, si = s_ref[:, :t], s_ref[:, t:]
        dcc_ref[:t, :] += _mxu(sr, dy, 0, 0)
        dcc_ref[t:, :] += _mxu(si, dy, 0, 0)
        xr, xi, cr[...], ci[...] = _scan_block(_mxu(dy, cc_ref[:t, :], 1, 1), _mxu(dy, cc_ref[t:, :], 1, 1),
                                               p_ref[:, :t], p_ref[:, t:], cr[...], ci[...], True)
        du_ref[...] = (dy * d_ref[...] + _mxu(xr, bb_ref[:, :t], 1, 1) + _mxu(xi, bb_ref[:, t:], 1, 1)).astype(BF16)
        dbb_ref[:, :t] += _mxu(u, xr, 0, 0)
        dbb_ref[:, t:] += _mxu(u, xi, 0, 0)
        first = i == nb - 1
        row = lax.broadcasted_iota(jnp.int32, (r, t), 0)
        pr = jnp.where(row == 0, jnp.where(first, 0.0, sp_ref[7:8, :t]), pltpu.roll(sr, 1, 0))
        pi = jnp.where(row == 0, jnp.where(first, 0.0, sp_ref[7:8, t:]), pltpu.roll(si, 1, 0))
        da_ref[:, :t] += jnp.sum(xr * pr + xi * pi, axis=0, keepdims=True)
        da_ref[:, t:] += jnp.sum(xi * pr - xr * pi, axis=0, keepdims=True)

    return pl.pallas_call(
        body, name=name, grid=(2 * S5_GROUPS * S5_STATE // (2 * t), nb),
        in_specs=[col, col, col, u_s, state, halo, bb_s, cc_s, table, vec, pl.BlockSpec(memory_space=pl.ANY)],
        out_specs=[u_s, acc, bb_s, cc_s, vec],
        out_shape=[jax.ShapeDtypeStruct(dproj.shape, dproj.dtype), jax.ShapeDtypeStruct((1, 2 * S5_GROUPS * S5_STATE), F32),
                   jax.ShapeDtypeStruct(bbt.shape, F32), jax.ShapeDtypeStruct(cct.shape, F32),
                   jax.ShapeDtypeStruct((1, w), F32)],
        scratch_shapes=[pltpu.VMEM((1, t), F32), pltpu.VMEM((1, t), F32)],
        input_output_aliases={10: 0}, compiler_params=_cp("parallel", "arbitrary"),
    )(dg1, dg2, y, proj, st, st, bbt, cct, apow_rev, dvec, dproj)


def _s5_tile_b(b_re, b_im):
    nt = S5_GROUPS * S5_STATE // S5_TILE
    gpt = S5_GROUPS // nt
    eye = jnp.eye(gpt, dtype=F32)

    def tile(b):
        t5 = jnp.einsum("jghp,gk->jghkp", b.reshape(nt, gpt, S5_GROUP, S5_STATE), eye)
        return t5.reshape(nt, gpt * S5_GROUP, S5_TILE)

    return jnp.concatenate([tile(b_re), tile(b_im)], axis=2)


def _s5_untile_b(d):
    nt = S5_GROUPS * S5_STATE // S5_TILE
    gpt = S5_GROUPS // nt
    eye = jnp.eye(gpt, dtype=F32)

    def untile(x):
        x5 = x.reshape(nt, gpt, S5_GROUP, gpt, S5_STATE)
        return jnp.einsum("jghkp,gk->jghp", x5, eye).reshape(S5_GROUPS, S5_GROUP, S5_STATE)

    return untile(d[:, :, :S5_TILE]), untile(d[:, :, S5_TILE:])


def _s5_tile_c(c_re, c_im):
    nt = S5_GROUPS * S5_STATE // S5_TILE
    gpt = S5_GROUPS // nt
    eye = jnp.eye(gpt, dtype=F32)

    def tile(c):
        t5 = jnp.einsum("jgph,gk->jkpgh", c.reshape(nt, gpt, S5_STATE, S5_GROUP), eye)
        return t5.reshape(nt, S5_TILE, gpt * S5_GROUP)

    return jnp.concatenate([tile(c_re), -tile(c_im)], axis=1)


def _s5_untile_c(d):
    nt = S5_GROUPS * S5_STATE // S5_TILE
    gpt = S5_GROUPS // nt
    eye = jnp.eye(gpt, dtype=F32)

    def untile(x):
        x5 = x.reshape(nt, gpt, S5_STATE, gpt, S5_GROUP)
        return jnp.einsum("jkpgh,gk->jgph", x5, eye).reshape(S5_GROUPS, S5_STATE, S5_GROUP)

    return untile(d[:, :S5_TILE, :]), -untile(d[:, S5_TILE:, :])


def _row_call(body, name, s, ins, outs, acc=False):
    tr = min(512, s)

    def spec(width, cb, rows):
        if rows == 1:
            return pl.BlockSpec((1, width), lambda i: (0, cb))
        return pl.BlockSpec((tr, width), lambda i: (i, cb))

    in_specs = [spec(w, cb, a.shape[0]) for a, w, cb in ins]
    out_specs = [spec(w, cb, sd.shape[0]) for sd, w, cb in outs]
    return pl.pallas_call(
        body, name=name, grid=(s // tr,), in_specs=in_specs, out_specs=out_specs,
        out_shape=[sd for sd, _, _ in outs],
        compiler_params=_cp("arbitrary" if acc else "parallel"),
    )(*[a for a, _, _ in ins])


def _sds(shape, dtype):
    return jax.ShapeDtypeStruct(shape, dtype)


def _s5_gelu_fwd(yraw, proj, dvec, name):
    s, w = yraw.shape

    def body(y_ref, u_ref, d_ref, yo_ref, g_ref):
        y = y_ref[...] + d_ref[...] * u_ref[...]
        yo_ref[...] = y
        g_ref[...] = _gelu(y).astype(BF16)

    return _row_call(body, name, s, [(yraw, w, 0), (proj, w, 4), (dvec, w, 0)],
                     [(_sds((s, w), F32), w, 0), (_sds((s, w), BF16), w, 0)])


def _s5_glu_fwd(y, z, b, name):
    s, w = y.shape

    def body(y_ref, z_ref, b_ref, o_ref):
        o_ref[...] = (_gelu(y_ref[...]) * _sigmoid(z_ref[...] + b_ref[...])).astype(BF16)

    return _row_call(body, name, s, [(y, w, 0), (z, w, 0), (b, w, 0)], [(_sds((s, w), BF16), w, 0)])[0]


def _s5_glu_bwd(dmerged, y, z, b, name):
    s, w = y.shape

    def body(do_ref, y_ref, z_ref, b_ref, dz_ref, dg_ref, db_ref):
        @pl.when(pl.program_id(0) == 0)
        def _():
            db_ref[...] = jnp.zeros_like(db_ref)

        g = _gelu(y_ref[...])
        sg = _sigmoid(z_ref[...] + b_ref[...])
        dout = do_ref[...]
        dz = dout * g * sg * (1.0 - sg)
        dz_ref[...] = dz.astype(BF16)
        dg_ref[...] = dout * sg
        db_ref[...] += jnp.sum(dz, axis=0, keepdims=True)

    return _row_call(body, name, s, [(dmerged, w, 1), (y, w, 0), (z, w, 0), (b, w, 0)],
                     [(_sds((s, w), BF16), w, 0), (_sds((s, w), F32), w, 0), (_sds((1, w), F32), w, 0)], acc=True)


def _s5_gelu_bwd(dg1, dg2, y, proj, dvec, name):
    s, w = y.shape

    def body(a_ref, b_ref, y_ref, u_ref, d_ref, dy_ref, du_ref, dd_ref):
        @pl.when(pl.program_id(0) == 0)
        def _():
            dd_ref[...] = jnp.zeros_like(dd_ref)

        dy = (a_ref[...] + b_ref[...]) * _gelu_grad(y_ref[...])
        dy_ref[...] = dy.astype(BF16)
        du_ref[...] = dy * d_ref[...]
        dd_ref[...] += jnp.sum(dy * u_ref[...], axis=0, keepdims=True)

    return _row_call(body, name, s, [(dg1, w, 0), (dg2, w, 0), (y, w, 0), (proj, w, 4), (dvec, w, 0)],
                     [(_sds((s, w), BF16), w, 0), (_sds((s, w), F32), w, 0), (_sds((1, w), F32), w, 0)], acc=True)


def _gdn_conv_fwd(projx, cw, name):
    s = projx.shape[0]
    nh = GDN_HEADS

    def body(x_ref, w_ref, o_ref):
        j = pl.program_id(0)
        cv = _conv_fwd(_conv_taps(x_ref[...], GDN_CONV), w_ref)
        y = cv * _sigmoid(cv)
        nrm = y * lax.rsqrt(jnp.sum(y * y, axis=-1, keepdims=True) + EPS)
        o_ref[...] = jnp.where(j < nh, nrm * (GDN_DH ** -0.5), jnp.where(j < 2 * nh, nrm, y))

    return pl.pallas_call(
        body, name=name, grid=(3 * nh,),
        in_specs=[pl.BlockSpec((s, GDN_DH), lambda j: (0, j)), pl.BlockSpec((GDN_CONV, GDN_DH), lambda j: (0, j))],
        out_specs=pl.BlockSpec((s, GDN_DH), lambda j: (0, j)),
        out_shape=jax.ShapeDtypeStruct((s, 3 * nh * GDN_DH), F32), compiler_params=_cp("parallel"),
    )(projx, cw)


def _gdn_conv_bwd(projx, cw, dqkv, dprojx, name):
    s = projx.shape[0]
    nh = GDN_HEADS

    def body(x_ref, w_ref, d_ref, _, dx_ref, dw_ref):
        j = pl.program_id(0)
        x = x_ref[...]
        taps = _conv_taps(x, GDN_CONV)
        cv = _conv_fwd(taps, w_ref)
        sg = _sigmoid(cv)
        y = cv * sg
        rinv = lax.rsqrt(jnp.sum(y * y, axis=-1, keepdims=True) + EPS)
        nrm = y * rinv
        dn = d_ref[...]
        dns = jnp.where(j < nh, dn * (GDN_DH ** -0.5), dn)
        dyn = rinv * (dns - nrm * jnp.sum(dns * nrm, axis=-1, keepdims=True))
        dy = jnp.where(j < 2 * nh, dyn, dn)
        dc = dy * (sg * (1.0 + cv * (1.0 - sg)))
        dx_ref[...] = _conv_bwd(taps, dc, w_ref, dw_ref).astype(BF16)

    col = pl.BlockSpec((s, GDN_DH), lambda j: (0, j))
    wcol = pl.BlockSpec((GDN_CONV, GDN_DH), lambda j: (0, j))
    return pl.pallas_call(
        body, name=name, grid=(3 * nh,), in_specs=[col, wcol, col, pl.BlockSpec(memory_space=pl.ANY)],
        out_specs=[col, wcol],
        out_shape=[jax.ShapeDtypeStruct(dprojx.shape, dprojx.dtype), jax.ShapeDtypeStruct((GDN_CONV, 3 * nh * GDN_DH), F32)],
        input_output_aliases={3: 0}, compiler_params=_cp("parallel"),
    )(projx, cw, dqkv, dprojx)


def _softplus(x):
    return jnp.maximum(x, 0.0) + jnp.log1p(jnp.exp(-jnp.abs(x)))


def _gdn_gates_fwd(projx, alog, dtb, name):
    s = projx.shape[0]
    w = GDN_HEADS * GDN_DH

    def body(b_ref, a_ref, al_ref, dt_ref, bo_ref, go_ref):
        bo_ref[...] = _sigmoid(b_ref[...])
        go_ref[...] = -jnp.exp(al_ref[...]) * _softplus(a_ref[...] + dt_ref[...])

    return _row_call(body, name, s, [(projx, w, 4), (projx, w, 5), (alog, w, 0), (dtb, w, 0)],
                     [(_sds((s, w), F32), w, 0), (_sds((s, w), F32), w, 0)])


def _gdn_gates_bwd(projx, alog, dtb, dbeta, dg, dprojx, name):
    s = projx.shape[0]
    w = GDN_HEADS * GDN_DH
    tr = min(512, s)

    def body(b_ref, a_ref, al_ref, dt_ref, dbe_ref, dg_ref, _, o_ref, dal_ref, ddt_ref):
        @pl.when(pl.program_id(0) == 0)
        def _():
            dal_ref[...] = jnp.zeros_like(dal_ref)
            ddt_ref[...] = jnp.zeros_like(ddt_ref)

        for h in range(GDN_HEADS):
            lo, hi = h * GDN_DH, (h + 1) * GDN_DH
            beta = _sigmoid(b_ref[:, lo:hi])
            pb = jnp.sum(dbe_ref[:, lo:hi], axis=-1, keepdims=True) * (1.0 / GDN_DH)
            o_ref[:, lo:hi] = (pb * beta * (1.0 - beta)).astype(BF16)
            xa = a_ref[:, lo:hi] + dt_ref[:, lo:hi]
            ea = -jnp.exp(al_ref[:, lo:hi])
            pg = jnp.sum(dg_ref[:, lo:hi], axis=-1, keepdims=True) * (1.0 / GDN_DH)
            da = pg * ea * _sigmoid(xa)
            o_ref[:, w + lo:w + hi] = da.astype(BF16)
            dal_ref[:, lo:hi] += jnp.sum(pg * ea * _softplus(xa), axis=0, keepdims=True)
            ddt_ref[:, lo:hi] += jnp.sum(da, axis=0, keepdims=True)

    def row(cb):
        return pl.BlockSpec((tr, w), lambda i: (i, cb))

    vec = pl.BlockSpec((1, w), lambda i: (0, 0))
    return pl.pallas_call(
        body, name=name, grid=(s // tr,),
        in_specs=[row(4), row(5), vec, vec, row(0), row(0), pl.BlockSpec(memory_space=pl.ANY)],
        out_specs=[pl.BlockSpec((tr, 2 * w), lambda i: (i, 2)), vec, vec],
        out_shape=[jax.ShapeDtypeStruct(dprojx.shape, dprojx.dtype), jax.ShapeDtypeStruct((1, w), F32),
                   jax.ShapeDtypeStruct((1, w), F32)],
        input_output_aliases={6: 0}, compiler_params=_cp("arbitrary"),
    )(projx, projx, alog, dtb, dbeta, dg, dprojx)


def _gdn_tri():
    c = GDN_CHUNK
    i = lax.broadcasted_iota(jnp.int32, (c, c), 0)
    j = lax.broadcasted_iota(jnp.int32, (c, c), 1)
    return ((i >= j).astype(F32), (i <= j).astype(F32), i >= j, i > j, (i == j).astype(F32))


def _bdot(a, b, ca=2, cb=1, precision=None):
    return lax.dot_general(a, b, (((ca,), (cb,)), ((0,), (0,))), precision=precision, preferred_element_type=F32)


def _bmxu(a, b, ca=2, cb=1):
    return _bdot(a.astype(BF16), b.astype(BF16), ca, cb)


def _split(x):
    hi = x.astype(BF16)
    return hi, (x - hi.astype(F32)).astype(BF16)


def _bdot3(a, b, ca=2, cb=1):
    ah, al = _split(a)
    bh, bl = _split(b)
    return _bdot(ah, bh, ca, cb) + (_bdot(ah, bl, ca, cb) + _bdot(al, bh, ca, cb))


def _tri_dot(tri, x):
    t = tri.astype(BF16)
    hi = x.astype(BF16)
    r1 = x - hi.astype(F32)
    mid = r1.astype(BF16)
    lo = (r1 - mid.astype(F32)).astype(BF16)
    return _dot(t, hi) + (_dot(t, mid) + _dot(t, lo))


def _heads(x):
    return jnp.stack([x[:, h * GDN_DH:(h + 1) * GDN_DH] for h in range(GDN_HEADS)], axis=0)


def _unheads(x):
    return jnp.concatenate([x[h] for h in range(GDN_HEADS)], axis=1)


def _gdn_chunk(q, k, v, bb, g2d, tri):
    low, up, incl, strict, eye = tri
    c = GDN_CHUNK
    gc = _heads(_tri_dot(low, g2d))
    gci = gc[:, :, :c]
    gdiff = gci - jnp.swapaxes(gci, 1, 2)
    decay = jnp.where(incl, jnp.exp(jnp.where(incl, gdiff, 0.0)), 0.0)
    kb, vb = k * bb, v * bb
    kbk = _bmxu(kb, k, 2, 2)
    x = -jnp.where(strict, kbk * decay, 0.0)
    t = eye + x
    p = x
    for _ in range(c.bit_length() - 2):
        p = _bdot3(p, p)
        t = t + _bdot3(t, p)
    eg = jnp.exp(gc)
    kbg = kb * eg
    gcl = gc[:, c - 1:c, :]
    ek = jnp.exp(gcl - gc)
    qkraw = _bmxu(q, k, 2, 2)
    return dict(decay=decay, kb=kb, vb=vb, kbk=kbk, t=t, eg=eg, kbg=kbg, ek=ek, gl=jnp.exp(gcl),
                w=_bmxu(t, kbg), u=_bmxu(t, vb), qkraw=qkraw, qk=jnp.where(incl, qkraw * decay, 0.0),
                qd=q * eg, kd=k * ek)


def _gdn_specs(n_of):
    c, w = GDN_CHUNK, GDN_HEADS * GDN_DH

    def blk(cb, width=w):
        return pl.BlockSpec((c, width), lambda n: (n_of(n), cb))

    st = pl.BlockSpec((None, GDN_HEADS, GDN_DH, GDN_DH), lambda n: (n_of(n), 0, 0, 0))
    vec = pl.BlockSpec((1, GDN_DH), lambda n: (0, 0))
    return blk, st, vec


def _gdn_load(qkv_ref, b_ref, g_ref, tri):
    w = GDN_HEADS * GDN_DH
    q, k, v = _heads(qkv_ref[:, :w]), _heads(qkv_ref[:, w:2 * w]), _heads(qkv_ref[:, 2 * w:])
    bb = _heads(b_ref[...])
    return q, k, v, bb, _gdn_chunk(q, k, v, bb, g_ref[...], tri)


def _gdn_fwd(qkv, beta, g, projx, onorm, name):
    s = qkv.shape[0]
    nc = s // GDN_CHUNK
    w = GDN_HEADS * GDN_DH
    blk, st, vec = _gdn_specs(lambda n: n)

    def body(qkv_ref, b_ref, g_ref, z_ref, on_ref, o_ref, st_ref, state):
        @pl.when(pl.program_id(0) == 0)
        def _():
            state[...] = jnp.zeros_like(state)

        _, _, _, _, ch = _gdn_load(qkv_ref, b_ref, g_ref, _gdn_tri())
        sp = state[...]
        st_ref[...] = sp
        vn = ch["u"] - _bmxu(ch["w"], sp)
        o = _bmxu(ch["qd"], sp) + _bmxu(ch["qk"], vn)
        state[...] = sp * ch["gl"] + _bmxu(ch["kd"], vn, 1, 1)
        r = lax.rsqrt(jnp.mean(o * o, axis=-1, keepdims=True) + EPS)
        z = _heads(z_ref[...])
        o_ref[...] = _unheads(o * r * on_ref[...] * (z * _sigmoid(z))).astype(BF16)

    return pl.pallas_call(
        body, name=name, grid=(nc,),
        in_specs=[blk(0, 3 * w), blk(0), blk(0), blk(3), vec], out_specs=[blk(0), st],
        out_shape=[jax.ShapeDtypeStruct((s, w), BF16), jax.ShapeDtypeStruct((nc, GDN_HEADS, GDN_DH, GDN_DH), F32)],
        scratch_shapes=[pltpu.VMEM((GDN_HEADS, GDN_DH, GDN_DH), F32)],
        compiler_params=_cp("arbitrary"),
    )(qkv, beta, g, projx, onorm.reshape(1, -1))


def _gdn_bwd(qkv, beta, g, projx, onorm, states, dout, name):
    s = qkv.shape[0]
    c = GDN_CHUNK
    nc = s // c
    w = GDN_HEADS * GDN_DH
    blk, st, vec = _gdn_specs(lambda n: nc - 1 - n)

    def body(qkv_ref, b_ref, g_ref, z_ref, on_ref, st_ref, do_ref,
             dqkv_ref, db_ref, dg_ref, dz_ref, don_ref, carry):
        @pl.when(pl.program_id(0) == 0)
        def _():
            carry[...] = jnp.zeros_like(carry)
            don_ref[...] = jnp.zeros_like(don_ref)

        tri = _gdn_tri()
        low, up, incl, strict, eye = tri
        q, k, v, bb, ch = _gdn_load(qkv_ref, b_ref, g_ref, tri)
        sp = st_ref[...]
        vn = ch["u"] - _bmxu(ch["w"], sp)
        o = _bmxu(ch["qd"], sp) + _bmxu(ch["qk"], vn)
        r = lax.rsqrt(jnp.mean(o * o, axis=-1, keepdims=True) + EPS)
        orn = o * r
        z = _heads(z_ref[...])
        sg = _sigmoid(z)
        dout = _heads(do_ref[...])
        onw = on_ref[...]
        dz_ref[...] = _unheads(dout * orn * onw * (sg * (1.0 + z * (1.0 - sg)))).astype(BF16)
        don = dout * (z * sg)
        don_ref[...] += jnp.sum(jnp.sum(don * orn, axis=0), axis=0, keepdims=True)
        dor = don * onw
        do = r * (dor - orn * jnp.mean(dor * orn, axis=-1, keepdims=True))
        dsn = carry[...]
        dqd = _bmxu(do, sp, 2, 2)
        dqk = jnp.where(incl, _bmxu(do, vn, 2, 2), 0.0)
        dvn = _bmxu(ch["qk"], do, 1, 1) + _bmxu(ch["kd"], dsn)
        dkd = _bmxu(vn, dsn, 2, 2)
        dgl = jnp.sum(dsn * sp, axis=1, keepdims=True)
        dw = -_bmxu(dvn, sp, 2, 2)
        carry[...] = _bmxu(ch["qd"], do, 1, 1) + dsn * ch["gl"] - _bmxu(ch["w"], dvn, 1, 1)
        t = ch["t"]
        dvb = _bmxu(t, dvn, 1, 1)
        dkbg = _bmxu(t, dw, 1, 1)
        dt = _bmxu(dvn, ch["vb"], 2, 2) + _bmxu(dw, ch["kbg"], 2, 2)
        da = -_bdot3(_bdot3(t, dt, 1, 1), t, 2, 2)
        da = jnp.where(strict, da, 0.0)
        decay = ch["decay"]
        dkbk = da * decay
        dqkr = dqk * decay
        mdec = (da * ch["kbk"] + dqk * ch["qkraw"]) * decay
        dkb = _bmxu(dkbk, k) + dkbg * ch["eg"]
        dk = _bmxu(dkbk, ch["kb"], 1, 1) + _bmxu(dqkr, q, 1, 1) + dkd * ch["ek"] + dkb * bb
        dq = _bmxu(dqkr, k) + dqd * ch["eg"]
        tk = dkd * ch["kd"]
        dgcl = jnp.sum(tk, axis=1, keepdims=True) + dgl * ch["gl"]
        row = lax.broadcasted_iota(jnp.int32, (GDN_HEADS, c, GDN_DH), 1)
        zpad = jnp.zeros((GDN_HEADS, c, GDN_DH - c), F32)
        dgc = (jnp.concatenate([mdec, zpad], axis=2) - jnp.concatenate([jnp.swapaxes(mdec, 1, 2), zpad], axis=2)
               + dqd * ch["qd"] - tk + dkbg * ch["kbg"] + jnp.where(row == c - 1, dgcl, 0.0))
        dqkv_ref[:, :w] = _unheads(dq)
        dqkv_ref[:, w:2 * w] = _unheads(dk)
        dqkv_ref[:, 2 * w:] = _unheads(dvb * bb)
        db_ref[...] = _unheads(dkb * k + dvb * v)
        dg_ref[...] = _tri_dot(up, _unheads(dgc))

    return pl.pallas_call(
        body, name=name, grid=(nc,),
        in_specs=[blk(0, 3 * w), blk(0), blk(0), blk(3), vec, st, blk(0)],
        out_specs=[blk(0, 3 * w), blk(0), blk(0), blk(3), vec],
        out_shape=[jax.ShapeDtypeStruct((s, 3 * w), F32), jax.ShapeDtypeStruct((s, w), F32),
                   jax.ShapeDtypeStruct((s, w), F32), jax.ShapeDtypeStruct(projx.shape, BF16),
                   jax.ShapeDtypeStruct((1, GDN_DH), F32)],
        scratch_shapes=[pltpu.VMEM((GDN_HEADS, GDN_DH, GDN_DH), F32)],
        compiler_params=_cp("arbitrary"),
    )(qkv, beta, g, projx, onorm.reshape(1, -1), states, dout)


_WEIGHTS = (
    "l0_mix_norm", "l0_w_in", "l0_ret_norm", "l0_s5_lambda_re", "l0_s5_lambda_im", "l0_s5_b_re", "l0_s5_b_im",
    "l0_s5_c_re", "l0_s5_c_im", "l0_s5_d", "l0_s5_log_dt", "l0_s5_w_glu", "l0_s5_b_glu", "l0_w_out",
    "l0_xa_norm", "l0_mem_norm", "l0_xa_wq", "l0_xa_wkv", "l0_xa_wo", "l0_ffn_norm", "l0_ffn_w_up",
    "l0_ffn_conv", "l0_ffn_w_down", "l1_mix_norm", "l1_w_in", "l1_conv", "l1_a_log", "l1_dt_bias", "l1_o_norm",
    "l1_w_out", "l1_xa_norm", "l1_mem_norm", "l1_xa_wq", "l1_xa_wkv", "l1_xa_wo", "l1_ffn_norm", "l1_ffn_w_up",
    "l1_ffn_conv", "l1_ffn_w_down", "final_norm")
_INPUTS = ("x", "mem") + _WEIGHTS + ("loss_target",) + tuple("m_" + n for n in _WEIGHTS) + tuple("v_" + n for n in _WEIGHTS)

_COL = ("l0_w_in", "l0_xa_wkv", "l0_ffn_w_up", "l0_ffn_conv", "l1_w_in", "l1_conv", "l1_xa_wkv", "l1_ffn_w_up",
        "l1_ffn_conv")
_ROW = ("l0_s5_w_glu", "l0_w_out", "l0_xa_wq", "l0_xa_wo", "l0_ffn_w_down", "l1_w_out", "l1_xa_wq", "l1_xa_wo",
        "l1_ffn_w_down")
_F32_WIRE = ("l0_ffn_conv", "l1_conv", "l1_ffn_conv")
_REP = tuple(n for n in _WEIGHTS if n not in _COL + _ROW)
_GATHER_GROUPS = (("l0_w_in", "l0_s5_w_glu", "l0_w_out"),
                  ("l0_xa_wq", "l0_xa_wkv", "l0_xa_wo", "l0_ffn_w_up", "l0_ffn_conv", "l0_ffn_w_down"),
                  ("l1_w_in", "l1_conv", "l1_w_out", "l1_xa_wq", "l1_xa_wkv", "l1_xa_wo"),
                  ("l1_ffn_w_up", "l1_ffn_conv", "l1_ffn_w_down"))


def _round_up(n, m):
    return (n + m - 1) // m * m


_REP_BIG = ("l0_s5_lambda_re", "l0_s5_lambda_im", "l0_s5_b_re", "l0_s5_b_im", "l0_s5_c_re", "l0_s5_c_im", "l0_s5_d")
_REP_LAST = "l0_mix_norm"
_REP_SMALL = tuple(n for n in _REP if n not in _REP_BIG + (_REP_LAST,))
PACK_WIDTH = 1024


def _pack_rows(ts):
    rows = [jnp.pad(t, ((0, 0), (0, PACK_WIDTH - t.shape[1]))) for t in ts]
    rows.append(jnp.zeros((_round_up(len(ts), 8) - len(ts), PACK_WIDTH), F32))
    return jnp.concatenate(rows, axis=0)


def _s5_interleave(re, im):
    lead = re.shape[:-1]
    nt = re.shape[-1] // S5_TILE
    both = jnp.stack([re.reshape(lead + (nt, S5_TILE)), im.reshape(lead + (nt, S5_TILE))], axis=-2)
    return both.reshape(lead + (2 * re.shape[-1],))


def _s5_split(x):
    lead = x.shape[:-1]
    y = x.reshape(lead + (x.shape[-1] // (2 * S5_TILE), 2, S5_TILE))
    return y[..., 0, :].reshape(lead + (-1,)), y[..., 1, :].reshape(lead + (-1,))


def _s5_discretise(lr, li, log_dt, b_re, b_im):
    dt = jnp.exp(log_dt)[:, None]
    mag = jnp.exp(lr * dt)
    a_re = mag * jnp.cos(li * dt)
    a_im = mag * jnp.sin(li * dt)
    den = lr * lr + li * li
    z_re = ((a_re - 1.0) * lr + a_im * li) / den
    z_im = (a_im * lr - (a_re - 1.0) * li) / den
    bb_re = z_re[:, None, :] * b_re - z_im[:, None, :] * b_im
    bb_im = z_re[:, None, :] * b_im + z_im[:, None, :] * b_re
    return a_re, a_im, bb_re, bb_im


def _block_diag(b):
    g, r, c = b.shape
    return jnp.einsum("grc,gk->grkc", b, jnp.eye(g, dtype=b.dtype)).reshape(g * r, g * c)


def _block_diag_of(d, g):
    r, c = d.shape[0] // g, d.shape[1] // g
    return jnp.einsum("grkc,gk->grc", d.reshape(g, r, g, c), jnp.eye(g, dtype=d.dtype))


def kernel(*args):
    p = dict(zip(_INPUTS, args, strict=True))
    x0, mem0, tgt = p["x"][0], p["mem"][0], p["loss_target"][0]
    s, d = x0.shape
    me = _slot(*_mesh_pos())
    grads = {}
    wire = {n: (F32 if n in _F32_WIRE else BF16) for n in _COL + _ROW}

    zones = {n: _into_slot(p[n], wire[n], me, "place_" + n) for names in _GATHER_GROUPS for n in names}
    gather, pin = [], jnp.zeros((), F32)
    for i, names in enumerate(_GATHER_GROUPS):
        handle, token = _push_start([], [zones[n] for n in names], f"gather{i}_start")
        gather.append(handle)
        pin = pin + token[0, 0]
    w = {}

    def gathered(i, after):
        for n, full in zip(_GATHER_GROUPS[i], _push_wait(gather[i], after, f"gather{i}_wait")):
            if n in _COL:
                full = full.transpose(1, 0, 2)
            w[n] = full.reshape(-1, full.shape[-1]) if n in _ROW else full.reshape(full.shape[0], -1)

    pending = []

    def exchange(names, gain, tag):
        slots = []
        for n in names:
            g = grads[n]
            if n in _COL:
                g = g.reshape(g.shape[0], N_DEV, -1).transpose(1, 0, 2)
            else:
                g = g.reshape((N_DEV, -1) + g.shape[1:])
            slots.append(g.astype(wire[n]))
        handle, token = _push_start(slots, [], tag + "_start")
        pending.append((names, slots, handle, tag))
        return gain + token[0, 0]

    def xattn(pre, x_in):
        hx = _norm_fwd(x_in, p[pre + "xa_norm"], pre + "xa_norm_fwd")
        q = _mm(hx, w[pre + "xa_wq"], out_dtype=BF16, name=pre + "xa_q")
        memn = _norm_fwd(mem0, p[pre + "mem_norm"], pre + "mem_norm_fwd")
        kv = _mm(memn, w[pre + "xa_wkv"], out_dtype=BF16, name=pre + "xa_kv")
        ao = _xattn_fwd(q, kv, pre + "xattn_fwd")
        x_out = _mm(ao, w[pre + "xa_wo"], res=x_in, name=pre + "xa_o")
        return x_out, (x_in, hx, q, memn, kv, ao)

    def xattn_bwd(pre, saved, dxo):
        x_in, hx, q, memn, kv, ao = saved
        dao = _mm(dxo, w[pre + "xa_wo"], tb=True, name=pre + "xa_o_dx")
        grads[pre + "xa_wo"] = _mm(ao, dxo, ta=True, out_dtype=BF16, name=pre + "xa_o_dw")
        dq, dkv = _xattn_bwd(q, kv, dao, pre + "xattn_bwd")
        grads[pre + "xa_wq"] = _mm(hx, dq, ta=True, out_dtype=BF16, name=pre + "xa_q_dw")
        grads[pre + "xa_wkv"] = _mm(memn, dkv, ta=True, out_dtype=BF16, name=pre + "xa_kv_dw")
        dmemn = _mm(dkv, w[pre + "xa_wkv"], tb=True, name=pre + "xa_kv_dx")
        gain = exchange((pre + "xa_wo", pre + "xa_wq", pre + "xa_wkv"), p[pre + "xa_norm"], pre + "xa_grads")
        dx_in, grads[pre + "xa_norm"] = _mm_norm_bwd(dq, w[pre + "xa_wq"], x_in, gain, dxo, pre + "xa_q_dx")
        _, grads[pre + "mem_norm"] = _norm_bwd(mem0, p[pre + "mem_norm"], dmemn, jnp.zeros_like(mem0), pre + "mem_norm_bwd")
        return dx_in

    def ffn(pre, x_in):
        hf = _norm_fwd(x_in, p[pre + "ffn_norm"], pre + "ffn_norm_fwd")
        up = _mm(hf, w[pre + "ffn_w_up"], out_dtype=BF16, name=pre + "ffn_up")
        act = _ffn_act_fwd(up, w[pre + "ffn_conv"], pre + "ffn_act_fwd")
        x_out = _mm(act, w[pre + "ffn_w_down"], res=x_in, name=pre + "ffn_down")
        return x_out, (x_in, hf, up, act)

    def ffn_bwd(pre, saved, dxo):
        x_in, hf, up, act = saved
        dact = _mm(dxo, w[pre + "ffn_w_down"], tb=True, out_dtype=BF16, name=pre + "ffn_down_dx")
        grads[pre + "ffn_w_down"] = _mm(act, dxo, ta=True, out_dtype=BF16, name=pre + "ffn_down_dw")
        dpu, dpg, dcu, dcg = _ffn_act_bwd(up, w[pre + "ffn_conv"], dact, pre + "ffn_act_bwd")
        dup = jnp.concatenate([dpu, dpg], axis=1)
        grads[pre + "ffn_conv"] = jnp.concatenate([dcu, dcg], axis=1)
        grads[pre + "ffn_w_up"] = _mm(hf, dup, ta=True, out_dtype=BF16, name=pre + "ffn_up_dw")
        gain = exchange((pre + "ffn_w_down", pre + "ffn_w_up", pre + "ffn_conv"), p[pre + "ffn_norm"], pre + "ffn_grads")
        dx_in, grads[pre + "ffn_norm"] = _mm_norm_bwd(dup, w[pre + "ffn_w_up"], x_in, gain, dxo, pre + "ffn_up_dx")
        return dx_in

    cos, sin = _rope_tables(s)
    (a_re, a_im, bb_re, bb_im), disc_vjp = jax.vjp(
        _s5_discretise, p["l0_s5_lambda_re"], p["l0_s5_lambda_im"], p["l0_s5_log_dt"], p["l0_s5_b_re"], p["l0_s5_b_im"])
    apow, apow_rev = _s5_pow_tables(_s5_interleave(a_re.reshape(1, -1), a_im.reshape(1, -1)), "l0_s5_pow_tables")
    bbt = _s5_tile_b(bb_re, bb_im).astype(BF16)
    cct = _s5_tile_c(p["l0_s5_c_re"], p["l0_s5_c_im"]).astype(BF16)
    s5_d = p["l0_s5_d"].reshape(1, -1)
    b_glu = p["l0_s5_b_glu"].reshape(1, -1)

    h0 = _norm_fwd(x0, p["l0_mix_norm"] + pin, "l0_mix_norm_fwd")
    gathered(0, h0)
    proj = _mm(h0, w["l0_w_in"], name="l0_in")
    o_ret, ret_states = _ret_fwd(proj, cos, sin, p["l0_ret_norm"], "l0_ret_fwd")
    st, y, gy = _s5_fwd(proj, bbt, cct, apow, s5_d, "l0_s5_fwd")
    z = _mm(gy, w["l0_s5_w_glu"], name="l0_s5_glu_mm")
    y2 = _s5_glu_fwd(y, z, b_glu, "l0_s5_glu_fwd")
    merged = jnp.concatenate([o_ret, y2], axis=1)
    x1 = _mm(merged, w["l0_w_out"], res=x0, name="l0_out")
    gathered(1, x1)
    x2, xa0 = xattn("l0_", x1)
    x3, ff0 = ffn("l0_", x2)

    gathered(2, x3)
    nqkv = 4 * GDN_HEADS * GDN_DH
    w1 = w["l1_w_in"]
    wx = jnp.concatenate([w1[:, :nqkv], jnp.repeat(w1[:, nqkv:nqkv + GDN_HEADS], GDN_DH, axis=1),
                          jnp.repeat(w1[:, nqkv + GDN_HEADS:], GDN_DH, axis=1)], axis=1)
    alog_x = jnp.repeat(p["l1_a_log"], GDN_DH).reshape(1, -1)
    dtb_x = jnp.repeat(p["l1_dt_bias"], GDN_DH).reshape(1, -1)
    h1 = _norm_fwd(x3, p["l1_mix_norm"], "l1_mix_norm_fwd")
    projx = _mm(h1, wx, name="l1_in")
    qkv = _gdn_conv_fwd(projx, w["l1_conv"], "l1_conv_fwd")
    beta, glog = _gdn_gates_fwd(projx, alog_x, dtb_x, "l1_gates_fwd")
    o_gdn, gdn_states = _gdn_fwd(qkv, beta, glog, projx, p["l1_o_norm"], "l1_gdn_fwd")
    x4 = _mm(o_gdn, w["l1_w_out"], res=x3, name="l1_out")
    x5, xa1 = xattn("l1_", x4)
    gathered(3, x5)
    x6, ff1 = ffn("l1_", x5)

    loss_part, dx6, grads["final_norm"] = _loss_head(x6, p["final_norm"], tgt, "loss_head")
    loss = lax.psum(loss_part[0, 0], ("x", "y", "c"))
    dx5 = ffn_bwd("l1_", ff1, dx6)
    dx4 = xattn_bwd("l1_", xa1, dx5)

    do_gdn = _mm(dx4, w["l1_w_out"], tb=True, name="l1_out_dx")
    grads["l1_w_out"] = _mm(o_gdn, dx4, ta=True, out_dtype=BF16, name="l1_out_dw")
    dqkv, dbeta, dglog, dprojx, grads["l1_o_norm"] = _gdn_bwd(
        qkv, beta, glog, projx, p["l1_o_norm"], gdn_states, do_gdn, "l1_gdn_bwd")
    dprojx, grads["l1_conv"] = _gdn_conv_bwd(projx, w["l1_conv"], dqkv, dprojx, "l1_conv_bwd")
    dprojx, dalog_x, ddtb_x = _gdn_gates_bwd(projx, alog_x, dtb_x, dbeta, dglog, dprojx, "l1_gates_bwd")
    dwx = _mm(h1, dprojx, ta=True, name="l1_in_dw")
    grads["l1_w_in"] = jnp.concatenate(
        [dwx[:, :nqkv], dwx[:, nqkv:nqkv + GDN_HEADS * GDN_DH].reshape(d, GDN_HEADS, GDN_DH).sum(-1),
         dwx[:, nqkv + GDN_HEADS * GDN_DH:].reshape(d, GDN_HEADS, GDN_DH).sum(-1)], axis=1)
    grads["l1_a_log"] = dalog_x.reshape(GDN_HEADS, GDN_DH).sum(-1)
    grads["l1_dt_bias"] = ddtb_x.reshape(GDN_HEADS, GDN_DH).sum(-1)
    gain = exchange(("l1_w_out", "l1_w_in", "l1_conv"), p["l1_mix_norm"], "l1_mix_grads")
    dx3, grads["l1_mix_norm"] = _mm_norm_bwd(dprojx, wx, x3, gain, dx4, "l1_in_dx")

    dx2 = ffn_bwd("l0_", ff0, dx3)
    dx1 = xattn_bwd("l0_", xa0, dx2)

    dmerged = _mm(dx1, w["l0_w_out"], tb=True, name="l0_out_dx")
    grads["l0_w_out"] = _mm(merged, dx1, ta=True, out_dtype=BF16, name="l0_out_dw")
    dproj, grads["l0_ret_norm"] = _ret_bwd(proj, cos, sin, p["l0_ret_norm"], ret_states, dmerged, "l0_ret_bwd")
    dzg, dg1, grads["l0_s5_b_glu"] = _s5_glu_bwd(dmerged, y, z, b_glu, "l0_s5_glu_bwd")
    grads["l0_s5_w_glu"] = _mm(gy, dzg, ta=True, out_dtype=BF16, name="l0_s5_glu_dw")
    s5_d_after = exchange(("l0_w_out", "l0_s5_w_glu"), s5_d, "l0_out_grads")
    dg2 = _mm(dzg, w["l0_s5_w_glu"], tb=True, name="l0_s5_glu_dx")
    dproj, da_s5, dbbt, dcct, grads["l0_s5_d"] = _s5_bwd(dg1, dg2, y, proj, st, bbt, cct, apow_rev, s5_d_after, dproj, "l0_s5_bwd")
    dbb_re, dbb_im = _s5_untile_b(dbbt)
    grads["l0_s5_c_re"], grads["l0_s5_c_im"] = _s5_untile_c(dcct)
    da_re, da_im = (t.reshape(S5_GROUPS, S5_STATE) for t in _s5_split(da_s5[0]))
    (grads["l0_s5_lambda_re"], grads["l0_s5_lambda_im"], grads["l0_s5_log_dt"], grads["l0_s5_b_re"],
     grads["l0_s5_b_im"]) = disc_vjp((da_re, da_im, dbb_re, dbb_im))

    def as_2d(t):
        return t.reshape(-1, t.shape[-1])

    def as_row(t):
        return t.reshape(1, -1)

    small_own = _pack_rows([as_row(grads[n]) for n in _REP_SMALL])
    big_own = [as_2d(grads[n].reshape(p[n].shape)) for n in _REP_BIG]
    rep_zones = [_into_slot(small_own, F32, me, "place_rep0")]
    rep_zones += [_into_slot(t.reshape(-1, LANES), BF16, me, f"place_rep{i + 1}") for i, t in enumerate(big_own)]
    rep_handle, rep_token = _push_start([], rep_zones, "rep_grads_start")

    grads["l0_w_in"] = _mm(h0, dproj, ta=True, out_dtype=BF16, pin=rep_token, name="l0_in_dw")
    gain = exchange(("l0_w_in",), p["l0_mix_norm"], "l0_mix_grads")
    dx0, grads["l0_mix_norm"] = _mm_norm_bwd(dproj, w["l0_w_in"], x0, gain, dx1, "l0_in_dx")

    last_own = _pack_rows([as_row(grads[_REP_LAST])])
    last_handle, _ = _push_start([], [_into_slot(last_own, F32, me, "place_rep_last")], "rep_last_start")
    last_land, = _push_wait(last_handle, dx0, "rep_last_wait")
    rep_lands = _push_wait(rep_handle, last_land, "rep_grads_wait")
    rep_land = rep_lands[0]

    outs = {}
    kinds = ("grad_", "delta_", "new_m_", "new_v_")
    for names, slots, handle, tag in pending:
        for n, own_slots, land in zip(names, slots, _push_wait(handle, rep_land, tag + "_wait")):
            shape = p[n].shape
            own = lax.dynamic_index_in_dim(own_slots, me, 0, keepdims=False)
            res = _adamw(land, own, *(p[pre + n].reshape(own.shape) for pre in ("", "m_", "v_")), "adamw_" + n)
            for kind, t in zip(kinds, res):
                outs[kind + n] = t.reshape(shape)
    for n, own, land in zip(_REP_BIG, big_own, rep_lands[1:]):
        res = _adamw(land.reshape((N_DEV,) + own.shape), None, *(as_2d(p[pre + n]) for pre in ("", "m_", "v_")), "adamw_" + n)
        for kind, t in zip(kinds, res):
            outs[kind + n] = t.reshape(p[n].shape)
    for names, land, own, nm in ((_REP_SMALL, rep_land, small_own, "adamw_small"), ((_REP_LAST,), last_land, last_own, "adamw_last")):
        res = _adamw_rows(land, own, *([as_row(p[pre + n]) for n in names] for pre in ("", "m_", "v_")), nm)
        for j, kind in enumerate(kinds):
            for i, n in enumerate(names):
                outs[kind + n] = res[j * len(names) + i].reshape(p[n].shape)

    return (loss, dx0[None]) + tuple(outs[kind + n] for kind in kinds for n in _WEIGHTS)
```

```python
import functools
import math

import numpy as np
import jax
import jax.numpy as jnp
from jax import lax
from jax.experimental import pallas as pl
from jax.experimental.pallas import tpu as pltpu

F32 = jnp.float32
BF16 = jnp.bfloat16
EPS = 1e-6
N_DEV = 8
LANES = 128
VMEM_LIMIT = 48 * 1024 * 1024
HI = lax.Precision.HIGHEST

RET_HEADS, RET_DH, RET_CHUNK = 4, 128, 128
S5_GROUPS, S5_GROUP, S5_STATE = 32, 16, 64
GDN_HEADS, GDN_DH, GDN_CHUNK, GDN_CONV = 8, 128, 64, 4
XA_HEADS, XA_DH = 4, 256
FFN_CONV = 3
SCAN_ROWS = 256

ADAM_LR, ADAM_B1, ADAM_B2, ADAM_EPS, ADAM_WD, ADAM_STEP = 0.001, 0.9, 0.999, 1e-08, 0.01, 10


def _cp(*sem):
    return pltpu.CompilerParams(dimension_semantics=sem if sem else None, vmem_limit_bytes=VMEM_LIMIT)


def _tile(n, cap):
    if n <= cap:
        return n
    best = None
    for t in range(LANES, cap + 1, LANES):
        if n % t == 0:
            best = t
    assert best is not None, n
    return best


def _dot(a, b, ca=1, cb=0, precision=None):
    return lax.dot_general(a, b, (((ca,), (cb,)), ((), ())), precision=precision, preferred_element_type=F32)


def _mxu(a, b, ca=1, cb=0):
    return _dot(a.astype(BF16), b.astype(BF16), ca, cb)


def _sigmoid(x):
    return 0.5 * jnp.tanh(0.5 * x) + 0.5


def _shift_down(x, k):
    r = pltpu.roll(x, k, 0)
    row = lax.broadcasted_iota(jnp.int32, (8,) + x.shape[1:], 0)
    return jnp.concatenate([jnp.where(row >= k, r[:8], 0.0), r[8:]], axis=0)


def _shift_up(x, k):
    n = x.shape[0]
    r = pltpu.roll(x, n - k, 0)
    row = lax.broadcasted_iota(jnp.int32, (8,) + x.shape[1:], 0)
    return jnp.concatenate([r[:n - 8], jnp.where(row < 8 - k, r[n - 8:], 0.0)], axis=0)


def _mesh_pos():
    return lax.axis_index("x"), lax.axis_index("y"), lax.axis_index("c")


def _slot(px, py, pc):
    return 4 * px + 2 * py + pc


def _all_peers(x, y, c):
    flips = [(fx, fy, fc) for fx in (0, 1) for fy in (0, 1) for fc in (0, 1)][1:]
    return [(1 - x if fx else x, 1 - y if fy else y, 1 - c if fc else c) for fx, fy, fc in flips]


_HBM = pl.BlockSpec(memory_space=pltpu.HBM)
_SEM = pl.BlockSpec(memory_space=pltpu.SEMAPHORE)
N_PEERS = N_DEV - 1


def _push_copies(srcs, lands, send_sems, recv_sems, start):
    x, y, c = _mesh_pos()
    me = _slot(x, y, c)
    out = []
    for k, to in enumerate(_all_peers(x, y, c)):
        for a in range(len(lands)):
            src = srcs[a].at[_slot(*to)] if a < len(srcs) else lands[a].at[me]
            dst = lands[a].at[me if start else _slot(*to)]
            out.append(pltpu.make_async_remote_copy(
                src_ref=src, dst_ref=dst, send_sem=send_sems.at[a * N_PEERS + k], recv_sem=recv_sems.at[a * N_PEERS + k],
                device_id=to, device_id_type=pl.DeviceIdType.MESH))
    return out


def _into_slot(x, dtype, me, name):
    r, c = x.shape
    cap = max(16, 512 * 1024 // c)
    tr = max(t for t in range(16, min(r, cap) + 1, 16) if r % t == 0) if r % 16 == 0 else r

    def body(me_ref, x_ref, o_ref):
        o_ref[...] = x_ref[...].astype(dtype)

    return pl.pallas_call(
        body, name=name, out_shape=jax.ShapeDtypeStruct((N_DEV, r, c), dtype),
        grid_spec=pltpu.PrefetchScalarGridSpec(
            num_scalar_prefetch=1, grid=(r // tr,),
            in_specs=[pl.BlockSpec((tr, c), lambda i, me_ref: (i, 0))],
            out_specs=pl.BlockSpec((None, tr, c), lambda i, me_ref: (me_ref[0], i, 0))),
        compiler_params=_cp("parallel"),
    )(me.reshape(1).astype(jnp.int32), x)


def _push_start(scatter, gather_lands, name):
    ns, n = len(scatter), len(scatter) + len(gather_lands)
    lands = [lax.empty(a.shape, a.dtype) for a in scatter] + list(gather_lands)

    def body(*refs):
        srcs, zones = refs[:ns], refs[ns:ns + n]
        for cp in _push_copies(srcs, zones, refs[ns + n], refs[ns + n + 1], True):
            cp.start()
        refs[-1][...] = jnp.zeros((8, LANES), F32)

    hbm_in = [pltpu.with_memory_space_constraint(a, pltpu.HBM) for a in list(scatter) + lands]
    res = pl.pallas_call(
        body, name=name,
        out_shape=(pltpu.SemaphoreType.DMA((n * N_PEERS,)), pltpu.SemaphoreType.DMA((n * N_PEERS,)))
        + tuple(pltpu.HBM(a.shape, a.dtype) for a in list(scatter) + lands)
        + (jax.ShapeDtypeStruct((8, LANES), F32),),
        in_specs=[_HBM] * (ns + n),
        out_specs=(_SEM, _SEM) + (_HBM,) * (ns + n) + (pl.BlockSpec(memory_space=pltpu.VMEM),),
        input_output_aliases={i: 2 + i for i in range(ns + n)},
        compiler_params=pltpu.CompilerParams(has_side_effects=pltpu.SideEffectType.DATAFLOW_SIDE_EFFECTING),
    )(*hbm_in)
    return (res[0], res[1], res[2:2 + ns], res[2 + ns:2 + ns + n]), res[-1]


def _push_wait(handle, after, name):
    send_sems, recv_sems, srcs, lands = handle
    ns, n = len(srcs), len(lands)

    def body(*refs):
        for cp in _push_copies(refs[:ns], refs[ns:ns + n], refs[ns + n], refs[ns + n + 1], False):
            cp.wait_send()
            cp.wait_recv()

    res = pl.pallas_call(
        body, name=name,
        out_shape=tuple(pltpu.HBM(a.shape, a.dtype) for a in list(srcs) + list(lands)),
        in_specs=[_HBM] * (ns + n) + [_SEM, _SEM, pl.BlockSpec(memory_space=pl.ANY)],
        out_specs=(_HBM,) * (ns + n),
        input_output_aliases={i: i for i in range(ns + n)},
        compiler_params=pltpu.CompilerParams(has_side_effects=pltpu.SideEffectType.DATAFLOW_SIDE_EFFECTING),
    )(*srcs, *lands, send_sems, recv_sems, after)
    return res[ns:]


def _mm(a, b, *, ta=False, tb=False, out_dtype=F32, res=None, pin=None, name="mm"):
    m, k = (a.shape[1], a.shape[0]) if ta else a.shape
    n = b.shape[0] if tb else b.shape[1]
    assert k == (b.shape[1] if tb else b.shape[0]), (a.shape, b.shape, ta, tb)
    tm, tn, tk = _tile(m, 1408), _tile(n, 1536), _tile(k, 1408)
    nk = k // tk
    has_res = res is not None
    n_in = 2 + has_res + (pin is not None)

    def body(*refs):
        a_ref, b_ref = refs[:2]
        r_ref = refs[2] if has_res else None
        o_ref = refs[n_in]
        part = _mxu(a_ref[...], b_ref[...], 0 if ta else 1, 1 if tb else 0)

        def finish(r):
            if has_res:
                r = r + r_ref[...].astype(F32)
            o_ref[...] = r.astype(out_dtype)

        if nk == 1:
            finish(part)
            return
        acc = refs[-1]
        kk = pl.program_id(2)

        @pl.when(kk == 0)
        def _():
            acc[...] = part

        @pl.when(kk > 0)
        def _():
            acc[...] += part

        @pl.when(kk == nk - 1)
        def _():
            finish(acc[...])

    a_spec = pl.BlockSpec((tk, tm), lambda i, j, kk: (kk, i)) if ta else pl.BlockSpec((tm, tk), lambda i, j, kk: (i, kk))
    b_spec = pl.BlockSpec((tn, tk), lambda i, j, kk: (j, kk)) if tb else pl.BlockSpec((tk, tn), lambda i, j, kk: (kk, j))
    o_spec = pl.BlockSpec((tm, tn), lambda i, j, kk: (i, j))
    in_specs = [a_spec, b_spec] + ([o_spec] if has_res else [])
    args = (a, b) + ((res,) if has_res else ())
    if pin is not None:
        in_specs.append(pl.BlockSpec(pin.shape, lambda i, j, kk: (0, 0)))
        args += (pin,)
    return pl.pallas_call(
        body, name=name, grid=(m // tm, n // tn, nk), in_specs=in_specs, out_specs=o_spec,
        out_shape=jax.ShapeDtypeStruct((m, n), out_dtype),
        scratch_shapes=[pltpu.VMEM((tm, tn), F32)] if nk > 1 else [],
        compiler_params=_cp("parallel", "parallel", "arbitrary"),
    )(*args)


def _mm_norm_bwd(dy, w, x, g, dres, name, pin=None):
    dys = list(dy) if isinstance(dy, (list, tuple)) else [dy]
    nq = len(dys)
    s, kq = dys[0].shape
    d = w.shape[0]
    tm, tk = min(1024 if nq == 1 else 512, s), _tile(kq, 1408)
    per = kq // tk
    nk = nq * per
    n_in = nq + 4 + (pin is not None)

    def body(*refs):
        w_ref, x_ref, g_ref, dres_ref = refs[nq:nq + 4]
        dx_ref, dg_ref = refs[n_in], refs[n_in + 1]
        i, kk = pl.program_id(0), pl.program_id(1)

        @pl.when((i == 0) & (kk == 0))
        def _():
            dg_ref[...] = jnp.zeros_like(dg_ref)

        def finish(dh):
            xv = x_ref[...]
            r = lax.rsqrt(jnp.mean(xv * xv, axis=-1, keepdims=True) + EPS)
            xn = xv * r
            dg_ref[...] += jnp.sum(dh * xn, axis=0, keepdims=True)
            dhg = dh * g_ref[...]
            dx_ref[...] = dres_ref[...] + r * (dhg - xn * jnp.mean(dhg * xn, axis=-1, keepdims=True))

        if nk == 1:
            finish(_mxu(refs[0][...], w_ref[...], 1, 1))
            return
        acc = refs[-1]
        for q in range(nq):
            @pl.when((kk >= q * per) & (kk < (q + 1) * per))
            def _(q=q):
                part = _mxu(refs[q][...], w_ref[...], 1, 1)

                @pl.when(kk == 0)
                def _():
                    acc[...] = part

                @pl.when(kk > 0)
                def _():
                    acc[...] += part

        @pl.when(kk == nk - 1)
        def _():
            finish(acc[...])

    row = pl.BlockSpec((tm, d), lambda i, kk: (i, 0))
    vec = pl.BlockSpec((1, d), lambda i, kk: (0, 0))
    in_specs = [pl.BlockSpec((tm, tk), lambda i, kk, q=q: (i, jnp.clip(kk - q * per, 0, per - 1))) for q in range(nq)]
    in_specs += [pl.BlockSpec((d, tk), lambda i, kk: (0, kk)), row, vec, row]
    args = (*dys, w, x, g.reshape(1, d), dres)
    if pin is not None:
        in_specs.append(pl.BlockSpec(pin.shape, lambda i, kk: (0, 0)))
        args += (pin,)
    return pl.pallas_call(
        body, name=name, grid=(s // tm, nk), in_specs=in_specs, out_specs=[row, vec],
        out_shape=[jax.ShapeDtypeStruct((s, d), F32), jax.ShapeDtypeStruct((1, d), F32)],
        scratch_shapes=[pltpu.VMEM((tm, d), F32)] if nk > 1 else [],
        compiler_params=_cp("arbitrary", "arbitrary"),
    )(*args)


def _norm_fwd(x, g, name):
    s, d = x.shape
    tr = min(512, s)

    def body(x_ref, g_ref, o_ref):
        xv = x_ref[...]
        r = lax.rsqrt(jnp.mean(xv * xv, axis=-1, keepdims=True) + EPS)
        o_ref[...] = (xv * r * g_ref[...]).astype(BF16)

    row = pl.BlockSpec((tr, d), lambda i: (i, 0))
    return pl.pallas_call(
        body, name=name, grid=(s // tr,), in_specs=[row, pl.BlockSpec((1, d), lambda i: (0, 0))],
        out_specs=row, out_shape=jax.ShapeDtypeStruct((s, d), BF16), compiler_params=_cp("parallel"),
    )(x, g.reshape(1, d))


def _norm_bwd(x, g, dh, dres, name):
    s, d = x.shape
    tr = min(512, s)

    def body(x_ref, g_ref, dh_ref, dres_ref, dx_ref, dg_ref):
        @pl.when(pl.program_id(0) == 0)
        def _():
            dg_ref[...] = jnp.zeros_like(dg_ref)

        xv = x_ref[...]
        r = lax.rsqrt(jnp.mean(xv * xv, axis=-1, keepdims=True) + EPS)
        xn = xv * r
        dhv = dh_ref[...].astype(F32)
        dg_ref[...] += jnp.sum(dhv * xn, axis=0, keepdims=True)
        dhg = dhv * g_ref[...]
        dx_ref[...] = dres_ref[...] + r * (dhg - xn * jnp.mean(dhg * xn, axis=-1, keepdims=True))

    row = pl.BlockSpec((tr, d), lambda i: (i, 0))
    vec = pl.BlockSpec((1, d), lambda i: (0, 0))
    return pl.pallas_call(
        body, name=name, grid=(s // tr,), in_specs=[row, vec, row, row], out_specs=[row, vec],
        out_shape=[jax.ShapeDtypeStruct((s, d), F32), jax.ShapeDtypeStruct((1, d), F32)],
        compiler_params=_cp("arbitrary"),
    )(x, g.reshape(1, d), dh, dres)


def _loss_head(x, g, tgt, name):
    s, d = x.shape
    tr = min(512, s)

    def body(x_ref, g_ref, t_ref, l_ref, dx_ref, dg_ref):
        @pl.when(pl.program_id(0) == 0)
        def _():
            dg_ref[...] = jnp.zeros_like(dg_ref)
            l_ref[...] = jnp.zeros_like(l_ref)

        xv = x_ref[...]
        r = lax.rsqrt(jnp.mean(xv * xv, axis=-1, keepdims=True) + EPS)
        xn = xv * r
        err = xn * g_ref[...] - t_ref[...]
        part = 0.5 * jnp.sum(jnp.mean(err * err, axis=-1, keepdims=True), axis=0, keepdims=True)
        l_ref[...] += jnp.broadcast_to(part, l_ref.shape)
        dy = err * (1.0 / d)
        dg_ref[...] += jnp.sum(dy * xn, axis=0, keepdims=True)
        dyg = dy * g_ref[...]
        dx_ref[...] = r * (dyg - xn * jnp.mean(dyg * xn, axis=-1, keepdims=True))

    row = pl.BlockSpec((tr, d), lambda i: (i, 0))
    vec = pl.BlockSpec((1, d), lambda i: (0, 0))
    return pl.pallas_call(
        body, name=name, grid=(s // tr,), in_specs=[row, vec, row],
        out_specs=[pl.BlockSpec((1, LANES), lambda i: (0, 0)), row, vec],
        out_shape=[jax.ShapeDtypeStruct((1, LANES), F32), jax.ShapeDtypeStruct((s, d), F32),
                   jax.ShapeDtypeStruct((1, d), F32)],
        compiler_params=_cp("arbitrary"),
    )(x, g.reshape(1, d), tgt)


def _sum_slots(landed_slot, own):
    me = _slot(*_mesh_pos())
    mine = own.astype(F32)
    g = jnp.where(me == 0, mine, landed_slot(0).astype(F32))
    for i in range(1, N_DEV):
        g = g + jnp.where(me == i, mine, landed_slot(i).astype(F32))
    return g


def _adam_update(g, w, m, v):
    mm = ADAM_B1 * m + (1.0 - ADAM_B1) * g
    vv = ADAM_B2 * v + (1.0 - ADAM_B2) * (g * g)
    m_hat = mm / (1.0 - ADAM_B1 ** ADAM_STEP)
    v_hat = vv / (1.0 - ADAM_B2 ** ADAM_STEP)
    return g, -ADAM_LR * (m_hat / (jnp.sqrt(v_hat) + ADAM_EPS) + ADAM_WD * w), mm, vv


def _adamw_rows(landed, own, ws, ms, vs, name):
    k = len(ws)
    sizes = [w.shape[1] for w in ws]

    def body(*refs):
        p_ref, o_ref = refs[:2]
        w_refs, m_refs, v_refs = refs[2:2 + k], refs[2 + k:2 + 2 * k], refs[2 + 2 * k:2 + 3 * k]
        outs = refs[2 + 3 * k:]
        for i, n in enumerate(sizes):
            g = _sum_slots(lambda s: p_ref[s, i:i + 1, :n], o_ref[i:i + 1, :n])
            res = _adam_update(g, w_refs[i][...], m_refs[i][...], v_refs[i][...])
            for j in range(4):
                outs[j * k + i][...] = res[j]

    return pl.pallas_call(
        body, name=name, out_shape=[jax.ShapeDtypeStruct((1, n), F32) for _ in range(4) for n in sizes],
    )(landed, own, *ws, *ms, *vs)


def _adamw(landed, own, w, m, v, name):
    r, c = w.shape
    cap = max(8, 256 * 1024 // c)
    tr = max(t for t in range(8, min(r, cap) + 1, 8) if r % t == 0) if r % 8 == 0 else r
    gathered = own is None

    def body(*refs):
        p_ref = refs[0]
        w_ref, m_ref, v_ref, g_ref, d_ref, nm_ref, nv_ref = refs[1 if gathered else 2:]
        if gathered:
            g = p_ref[0].astype(F32)
            for i in range(1, N_DEV):
                g = g + p_ref[i].astype(F32)
        else:
            g = _sum_slots(lambda i: p_ref[i], refs[1][...])
        g_ref[...], d_ref[...], nm_ref[...], nv_ref[...] = _adam_update(g, w_ref[...], m_ref[...], v_ref[...])

    blk = pl.BlockSpec((tr, c), lambda i: (i, 0))
    n_blk = 3 if gathered else 4
    return pl.pallas_call(
        body, name=name, grid=(r // tr,),
        in_specs=[pl.BlockSpec((N_DEV, tr, c), lambda i: (0, i, 0))] + [blk] * n_blk,
        out_specs=[blk] * 4, out_shape=[jax.ShapeDtypeStruct((r, c), F32)] * 4,
        compiler_params=_cp("parallel"),
    )(*((landed,) if gathered else (landed, own)), w, m, v)


def _conv_taps(x, kw):
    return [_shift_down(x, kw - 1 - j) for j in range(kw - 1)] + [x]


def _conv_fwd(taps, w_ref):
    acc = w_ref[0:1, :] * taps[0]
    for j in range(1, len(taps)):
        acc = acc + w_ref[j:j + 1, :] * taps[j]
    return acc


def _conv_bwd(taps, dy, w_ref, dw_ref):
    kw = len(taps)
    dx = w_ref[kw - 1:kw, :] * dy
    for j in range(kw):
        dw_ref[j:j + 1, :] = jnp.sum(dy * taps[j], axis=0, keepdims=True)
        if j < kw - 1:
            dx = dx + w_ref[j:j + 1, :] * _shift_up(dy, kw - 1 - j)
    return dx


def _ffn_act_fwd(pre, cw, name):
    s, f2 = pre.shape
    nt = f2 // 2 // LANES

    def body(pu_ref, pg_ref, wu_ref, wg_ref, o_ref):
        up = _conv_fwd(_conv_taps(pu_ref[...].astype(F32), FFN_CONV), wu_ref)
        gate = _conv_fwd(_conv_taps(pg_ref[...].astype(F32), FFN_CONV), wg_ref)
        o_ref[...] = (gate * _sigmoid(gate) * up).astype(BF16)

    def col(rows, off):
        return pl.BlockSpec((rows, LANES), lambda j: (0, j + off))

    return pl.pallas_call(
        body, name=name, grid=(nt,),
        in_specs=[col(s, 0), col(s, nt), col(FFN_CONV, 0), col(FFN_CONV, nt)], out_specs=col(s, 0),
        out_shape=jax.ShapeDtypeStruct((s, f2 // 2), BF16), compiler_params=_cp("parallel"),
    )(pre, pre, cw, cw)


def _ffn_act_bwd(pre, cw, dact, name):
    s, f2 = pre.shape
    f = f2 // 2
    nt = f // LANES

    def body(pu_ref, pg_ref, wu_ref, wg_ref, da_ref, dpu_ref, dpg_ref, dwu_ref, dwg_ref):
        pu, pg = pu_ref[...].astype(F32), pg_ref[...].astype(F32)
        tu, tg = _conv_taps(pu, FFN_CONV), _conv_taps(pg, FFN_CONV)
        up = _conv_fwd(tu, wu_ref)
        gate = _conv_fwd(tg, wg_ref)
        sg = _sigmoid(gate)
        da = da_ref[...].astype(F32)
        dup = da * gate * sg
        dgate = da * up * (sg * (1.0 + gate * (1.0 - sg)))
        dpu_ref[...] = _conv_bwd(tu, dup, wu_ref, dwu_ref).astype(BF16)
        dpg_ref[...] = _conv_bwd(tg, dgate, wg_ref, dwg_ref).astype(BF16)

    def col(rows, off):
        return pl.BlockSpec((rows, LANES), lambda j: (0, j + off))

    return pl.pallas_call(
        body, name=name, grid=(nt,),
        in_specs=[col(s, 0), col(s, nt), col(FFN_CONV, 0), col(FFN_CONV, nt), col(s, 0)],
        out_specs=[col(s, 0), col(s, 0), col(FFN_CONV, 0), col(FFN_CONV, 0)],
        out_shape=[jax.ShapeDtypeStruct((s, f), BF16), jax.ShapeDtypeStruct((s, f), BF16),
                   jax.ShapeDtypeStruct((FFN_CONV, f), F32), jax.ShapeDtypeStruct((FFN_CONV, f), F32)],
        compiler_params=_cp("parallel"),
    )(pre, pre, cw, cw, dact)


def _xa_probs(qh, kh):
    sc = _mxu(qh, kh, 1, 1) * (XA_DH ** -0.5)
    e = jnp.exp(sc - jnp.max(sc, axis=-1, keepdims=True))
    return e / jnp.sum(e, axis=-1, keepdims=True)


def _xattn_fwd(q, kv, name):
    s, d = q.shape
    m = kv.shape[0]
    tr = min(512, s)

    def body(q_ref, kv_ref, o_ref):
        for h in range(XA_HEADS):
            lo, hi = h * XA_DH, (h + 1) * XA_DH
            p = _xa_probs(q_ref[:, lo:hi], kv_ref[:, lo:hi])
            o_ref[:, lo:hi] = _mxu(p, kv_ref[:, d + lo:d + hi]).astype(BF16)

    row = pl.BlockSpec((tr, d), lambda i: (i, 0))
    return pl.pallas_call(
        body, name=name, grid=(s // tr,), in_specs=[row, pl.BlockSpec((m, 2 * d), lambda i: (0, 0))],
        out_specs=row, out_shape=jax.ShapeDtypeStruct((s, d), BF16), compiler_params=_cp("parallel"),
    )(q, kv)


def _xattn_bwd(q, kv, do, name):
    s, d = q.shape
    m = kv.shape[0]
    tr = min(512, s)

    def body(q_ref, kv_ref, do_ref, dq_ref, dkv_ref):
        @pl.when(pl.program_id(0) == 0)
        def _():
            dkv_ref[...] = jnp.zeros_like(dkv_ref)

        for h in range(XA_HEADS):
            lo, hi = h * XA_DH, (h + 1) * XA_DH
            qh, kh, vh = q_ref[:, lo:hi], kv_ref[:, lo:hi], kv_ref[:, d + lo:d + hi]
            doh = do_ref[:, lo:hi]
            p = _xa_probs(qh, kh)
            dp = _mxu(doh, vh, 1, 1)
            ds = p * (dp - jnp.sum(p * dp, axis=-1, keepdims=True)) * (XA_DH ** -0.5)
            dq_ref[:, lo:hi] = _mxu(ds, kh).astype(BF16)
            dkv_ref[:, lo:hi] += _mxu(ds, qh, 0, 0)
            dkv_ref[:, d + lo:d + hi] += _mxu(p, doh, 0, 0)

    row = pl.BlockSpec((tr, d), lambda i: (i, 0))
    full = pl.BlockSpec((m, 2 * d), lambda i: (0, 0))
    return pl.pallas_call(
        body, name=name, grid=(s // tr,), in_specs=[row, full, row], out_specs=[row, full],
        out_shape=[jax.ShapeDtypeStruct((s, d), BF16), jax.ShapeDtypeStruct((m, 2 * d), F32)],
        compiler_params=_cp("arbitrary"),
    )(q, kv, do)


def _ret_tables():
    c = RET_CHUNK
    lg = np.log1p(-np.exp2(-5.0 - np.arange(RET_HEADS, dtype=np.float32))).astype(np.float32)
    idx = np.arange(c, dtype=np.float32)
    diff = idx[:, None] - idx[None, :]
    intra = np.where(diff >= 0, np.exp(lg[:, None, None] * np.where(diff >= 0, diff, 0.0)), 0.0)
    rk = np.broadcast_to(np.exp(lg[:, None] * (c - 1 - idx))[:, :, None], (RET_HEADS, c, LANES))
    rq = np.broadcast_to(np.exp(lg[:, None] * (idx + 1))[:, :, None], (RET_HEADS, c, LANES))
    return jnp.asarray(np.stack([intra, rk, rq], axis=1).astype(np.float32))


def _rope_tables(s):
    half = RET_DH // 2
    inv = jnp.exp(-math.log(10000.0) * jnp.arange(half, dtype=F32) / half)
    ang = jnp.arange(s, dtype=F32)[:, None] * inv[None, :]
    cos, sin = jnp.cos(ang), jnp.sin(ang)
    return jnp.concatenate([cos, cos], axis=1), jnp.concatenate([-sin, sin], axis=1)


def _ret_specs(n_of):
    c, w = RET_CHUNK, RET_HEADS * RET_DH

    def part(off):
        return pl.BlockSpec((c, w), lambda n: (n_of(n), off))

    pos = pl.BlockSpec((c, RET_DH), lambda n: (n_of(n), 0))
    gain = pl.BlockSpec((1, w), lambda n: (0, 0))
    tab = pl.BlockSpec((RET_HEADS, 3, c, LANES), lambda n: (0, 0, 0, 0))
    st = pl.BlockSpec((RET_HEADS, None, RET_DH, RET_DH), lambda n: (0, n_of(n), 0, 0))
    return part, pos, gain, tab, st


def _rheads(x):
    return jnp.stack([x[:, h * RET_DH:(h + 1) * RET_DH] for h in range(RET_HEADS)], axis=0)


def _runheads(x):
    return jnp.concatenate([x[h] for h in range(RET_HEADS)], axis=1)


def _rope(x, cos, sin):
    return x * cos + pltpu.roll(x, RET_DH // 2, 2) * sin


def _ret_chunk(q_ref, k_ref, v_ref, cos_ref, sin_ref, tab_ref, prev):
    cos, sin = cos_ref[...], sin_ref[...]
    q = _rope(_rheads(q_ref[...]), cos, sin)
    k = _rope(_rheads(k_ref[...]), cos, sin) * (RET_DH ** -0.5)
    v = _rheads(v_ref[...])
    scores = _bmxu(q, k, 2, 2) * tab_ref[:, 0]
    qdec = q * tab_ref[:, 2]
    kdec = k * tab_ref[:, 1]
    o = _bmxu(scores, v) + _bmxu(qdec, prev)
    return q, k, v, scores, qdec, kdec, o


def _ret_fwd(proj, cos, sin, gain, name):
    s = proj.shape[0]
    c = RET_CHUNK
    nc = s // c
    part, pos, gvec, tab, st = _ret_specs(lambda n: n)

    def body(q_ref, k_ref, v_ref, g_ref, cos_ref, sin_ref, rn_ref, tab_ref, o_ref, st_ref, state):
        @pl.when(pl.program_id(0) == 0)
        def _():
            state[...] = jnp.zeros_like(state)

        prev = state[...]
        st_ref[...] = prev
        _, _, v, _, _, kdec, o = _ret_chunk(q_ref, k_ref, v_ref, cos_ref, sin_ref, tab_ref, prev)
        state[...] = prev * tab_ref[:, 2, c - 1:c, :] + _bmxu(kdec, v, 1, 1)
        r = lax.rsqrt(jnp.mean(o * o, axis=-1, keepdims=True) + EPS)
        gate = g_ref[...]
        o_ref[...] = (_runheads(o * r) * rn_ref[...] * (gate * _sigmoid(gate))).astype(BF16)

    return pl.pallas_call(
        body, name=name, grid=(nc,),
        in_specs=[part(0), part(1), part(2), part(3), pos, pos, gvec, tab],
        out_specs=[part(0), st],
        out_shape=[jax.ShapeDtypeStruct((s, RET_HEADS * RET_DH), BF16),
                   jax.ShapeDtypeStruct((RET_HEADS, nc, RET_DH, RET_DH), F32)],
        scratch_shapes=[pltpu.VMEM((RET_HEADS, RET_DH, RET_DH), F32)],
        compiler_params=_cp("arbitrary"),
    )(proj, proj, proj, proj, cos, sin, gain.reshape(1, -1), _ret_tables())


def _ret_bwd(proj, cos, sin, gain, states, dmerged, name):
    s = proj.shape[0]
    c = RET_CHUNK
    nc = s // c
    width = RET_HEADS * RET_DH
    part, pos, gvec, tab, st = _ret_specs(lambda n: nc - 1 - n)

    def body(q_ref, k_ref, v_ref, g_ref, cos_ref, sin_ref, rn_ref, tab_ref, st_ref, do_ref,
             dp_ref, drn_ref, carry):
        @pl.when(pl.program_id(0) == 0)
        def _():
            carry[...] = jnp.zeros_like(carry)
            drn_ref[...] = jnp.zeros_like(drn_ref)

        prev = st_ref[...]
        q, k, v, scores, qdec, kdec, o = _ret_chunk(q_ref, k_ref, v_ref, cos_ref, sin_ref, tab_ref, prev)
        r = lax.rsqrt(jnp.mean(o * o, axis=-1, keepdims=True) + EPS)
        on = o * r
        on2 = _runheads(on)
        gate = g_ref[...]
        sg = _sigmoid(gate)
        sil = gate * sg
        dout = do_ref[...]
        rn = rn_ref[...]
        dp_ref[:, 3 * width:] = (dout * on2 * rn * (sg * (1.0 + gate * (1.0 - sg)))).astype(BF16)
        drn_ref[...] += jnp.sum(dout * on2 * sil, axis=0, keepdims=True)
        don = _rheads(dout * rn * sil)
        do = r * (don - on * jnp.mean(don * on, axis=-1, keepdims=True))
        dc = carry[...]
        dsc = _bmxu(do, v, 2, 2) * tab_ref[:, 0]
        dq = _bmxu(dsc, k) + _bmxu(do, prev, 2, 2) * tab_ref[:, 2]
        dk = _bmxu(dsc, q, 1, 1) + _bmxu(v, dc, 2, 2) * tab_ref[:, 1]
        dv = _bmxu(scores, do, 1, 1) + _bmxu(kdec, dc)
        carry[...] = _bmxu(qdec, do, 1, 1) + dc * tab_ref[:, 2, c - 1:c, :]
        cos, sin = cos_ref[...], sin_ref[...]
        dk = dk * (RET_DH ** -0.5)
        dp_ref[:, :width] = _runheads(dq * cos + pltpu.roll(dq * sin, RET_DH // 2, 2)).astype(BF16)
        dp_ref[:, width:2 * width] = _runheads(dk * cos + pltpu.roll(dk * sin, RET_DH // 2, 2)).astype(BF16)
        dp_ref[:, 2 * width:3 * width] = _runheads(dv).astype(BF16)

    return pl.pallas_call(
        body, name=name, grid=(nc,),
        in_specs=[part(0), part(1), part(2), part(3), pos, pos, gvec, tab, st, part(0)],
        out_specs=[pl.BlockSpec((c, 4 * width), lambda n: (nc - 1 - n, 0)), gvec],
        out_shape=[jax.ShapeDtypeStruct(proj.shape, BF16), jax.ShapeDtypeStruct((1, width), F32)],
        scratch_shapes=[pltpu.VMEM((RET_HEADS, RET_DH, RET_DH), F32)],
        compiler_params=_cp("arbitrary"),
    )(proj, proj, proj, proj, cos, sin, gain.reshape(1, -1), _ret_tables(), states, dmerged)


S5_TILE = 512


def _cmul_add(xr, xi, ar, ai, yr, yi):
    return xr + ar * yr - ai * yi, xi + ar * yi + ai * yr


def _s5_pow_tables(a_il, name):
    r = SCAN_ROWS
    t = S5_TILE
    w2 = a_il.shape[1]

    def body(a_ref, up_ref, dn_ref):
        for j in range(w2 // (2 * t)):
            re, im = pl.ds(2 * t * j, t), pl.ds(2 * t * j + t, t)
            up_ref[0:1, re] = a_ref[:, re]
            up_ref[0:1, im] = a_ref[:, im]
            dn_ref[r - 1:r, re] = a_ref[:, re]
            dn_ref[r - 1:r, im] = -a_ref[:, im]
            n = 1
            while n < r:
                lr, li = up_ref[n - 1:n, re], up_ref[n - 1:n, im]
                xr, xi = up_ref[0:n, re], up_ref[0:n, im]
                up_ref[n:2 * n, re] = xr * lr - xi * li
                up_ref[n:2 * n, im] = xr * li + xi * lr
                yr, yi = dn_ref[r - n:r, re], dn_ref[r - n:r, im]
                dn_ref[r - 2 * n:r - n, re] = yr * lr + yi * li
                dn_ref[r - 2 * n:r - n, im] = yi * lr - yr * li
                n *= 2

    return pl.pallas_call(
        body, name=name, out_shape=[jax.ShapeDtypeStruct((r, w2), F32)] * 2, compiler_params=_cp(),
    )(a_il)


def _s5_scan_fwd(bu, apow, name):
    s, w2 = bu.shape
    r = SCAN_ROWS
    t = S5_TILE
    steps = r.bit_length() - 1

    def body(b_ref, p_ref, o_ref, cr, ci):
        @pl.when(pl.program_id(1) == 0)
        def _():
            cr[...] = jnp.zeros_like(cr)
            ci[...] = jnp.zeros_like(ci)

        xr, xi = b_ref[:, :t], b_ref[:, t:]
        for k in range(steps):
            sh = 1 << k
            xr, xi = _cmul_add(xr, xi, p_ref[sh - 1:sh, :t], p_ref[sh - 1:sh, t:],
                               _shift_down(xr, sh), _shift_down(xi, sh))
        xr, xi = _cmul_add(xr, xi, p_ref[:, :t], p_ref[:, t:], cr[...], ci[...])
        o_ref[:, :t] = xr
        o_ref[:, t:] = xi
        cr[...] = xr[r - 1:r, :]
        ci[...] = xi[r - 1:r, :]

    blk = pl.BlockSpec((r, 2 * t), lambda j, i: (i, j))
    return pl.pallas_call(
        body, name=name, grid=(w2 // (2 * t), s // r),
        in_specs=[blk, pl.BlockSpec((r, 2 * t), lambda j, i: (0, j))], out_specs=blk,
        out_shape=jax.ShapeDtypeStruct((s, w2), F32),
        scratch_shapes=[pltpu.VMEM((1, t), F32), pltpu.VMEM((1, t), F32)],
        compiler_params=_cp("parallel", "arbitrary"),
    )(bu, apow)


def _s5_scan_bwd(dst, apow_rev, st, name):
    s, w2 = dst.shape
    r = SCAN_ROWS
    t = S5_TILE
    nb = s // r
    steps = r.bit_length() - 1

    def body(d_ref, p_ref, s_ref, sp_ref, g_ref, da_ref, cr, ci):
        i = pl.program_id(1)

        @pl.when(i == 0)
        def _():
            cr[...] = jnp.zeros_like(cr)
            ci[...] = jnp.zeros_like(ci)
            da_ref[...] = jnp.zeros_like(da_ref)

        xr, xi = d_ref[:, :t], d_ref[:, t:]
        for k in range(steps):
            sh = 1 << k
            xr, xi = _cmul_add(xr, xi, p_ref[r - sh:r - sh + 1, :t], p_ref[r - sh:r - sh + 1, t:],
                               _shift_up(xr, sh), _shift_up(xi, sh))
        xr, xi = _cmul_add(xr, xi, p_ref[:, :t], p_ref[:, t:], cr[...], ci[...])
        g_ref[:, :t] = xr.astype(BF16)
        g_ref[:, t:] = xi.astype(BF16)
        cr[...] = xr[0:1, :]
        ci[...] = xi[0:1, :]
        first = i == nb - 1
        row = lax.broadcasted_iota(jnp.int32, (r, t), 0)
        last_r = jnp.where(first, 0.0, sp_ref[7:8, :t])
        last_i = jnp.where(first, 0.0, sp_ref[7:8, t:])
        pr = jnp.where(row == 0, last_r, pltpu.roll(s_ref[:, :t], 1, 0))
        pi = jnp.where(row == 0, last_i, pltpu.roll(s_ref[:, t:], 1, 0))
        da_ref[:, :t] += jnp.sum(xr * pr + xi * pi, axis=0, keepdims=True)
        da_ref[:, t:] += jnp.sum(xi * pr - xr * pi, axis=0, keepdims=True)

    blk = pl.BlockSpec((r, 2 * t), lambda j, i: (nb - 1 - i, j))
    halo = pl.BlockSpec((8, 2 * t), lambda j, i: (jnp.maximum((nb - 1 - i) * (r // 8) - 1, 0), j))
    vec = pl.BlockSpec((1, 2 * t), lambda j, i: (0, j))
    return pl.pallas_call(
        body, name=name, grid=(w2 // (2 * t), nb),
        in_specs=[blk, pl.BlockSpec((r, 2 * t), lambda j, i: (0, j)), blk, halo], out_specs=[blk, vec],
        out_shape=[jax.ShapeDtypeStruct((s, w2), BF16), jax.ShapeDtypeStruct((1, w2), F32)],
        scratch_shapes=[pltpu.VMEM((1, t), F32), pltpu.VMEM((1, t), F32)],
        compiler_params=_cp("parallel", "arbitrary"),
    )(dst, apow_rev, st, st)


_GELU_C = math.sqrt(2.0 / math.pi)
_GELU_A = 0.044715


def _gelu(y):
    return 0.5 * y * (1.0 + jnp.tanh(_GELU_C * (y + _GELU_A * y * y * y)))


def _gelu_grad(y):
    th = jnp.tanh(_GELU_C * (y + _GELU_A * y * y * y))
    return 0.5 * (1.0 + th) + 0.5 * y * (1.0 - th * th) * _GELU_C * (1.0 + 3.0 * _GELU_A * y * y)


def _rows_shift(x, k, axis, up):
    n = x.shape[axis]
    idx = lax.broadcasted_iota(jnp.int32, x.shape, axis)
    if up:
        return jnp.where(idx < n - k, pltpu.roll(x, n - k, axis), 0.0)
    return jnp.where(idx >= k, pltpu.roll(x, k, axis), 0.0)


def _scan_block(xr, xi, pr, pi, cr, ci, rev):
    r, w = xr.shape
    nt = r // 8
    x3r, x3i = xr.reshape(nt, 8, w), xi.reshape(nt, 8, w)
    p3r, p3i = pr.reshape(nt, 8, w), pi.reshape(nt, 8, w)

    def power(rows):
        t = r - rows if rev else rows - 1
        return pr[t:t + 1, :], pi[t:t + 1, :]

    tile_row = lax.broadcasted_iota(jnp.int32, (8, w), 0)
    for sh in (1, 2, 4):
        ar, ai = power(sh)
        keep = tile_row < 8 - sh if rev else tile_row >= sh
        mr, mi = jnp.where(keep, ar, 0.0)[None], jnp.where(keep, ai, 0.0)[None]
        turn = 8 - sh if rev else sh
        x3r, x3i = _cmul_add(x3r, x3i, mr, mi, pltpu.roll(x3r, turn, 1), pltpu.roll(x3i, turn, 1))
    edge = 0 if rev else 7
    lr, li = x3r[:, edge, :], x3i[:, edge, :]
    sh = 1
    while sh < nt:
        ar, ai = power(8 * sh)
        lr, li = _cmul_add(lr, li, ar, ai, _rows_shift(lr, sh, 0, rev), _rows_shift(li, sh, 0, rev))
        sh *= 2
    tr_, ti_ = p3r[:, edge, :], p3i[:, edge, :]
    first = lax.broadcasted_iota(jnp.int32, (nt, w), 0) == (nt - 1 if rev else 0)
    wr = jnp.where(first, 1.0, _rows_shift(tr_, 1, 0, rev))
    wi = jnp.where(first, 0.0, _rows_shift(ti_, 1, 0, rev))
    er, ei = _cmul_add(_rows_shift(lr, 1, 0, rev), _rows_shift(li, 1, 0, rev), wr, wi, cr, ci)
    a8r, a8i = (p3r[nt - 1], p3i[nt - 1]) if rev else (p3r[0], p3i[0])
    x3r, x3i = _cmul_add(x3r, x3i, a8r[None], a8i[None], er[:, None, :], ei[:, None, :])
    outr, outi = x3r.reshape(r, w), x3i.reshape(r, w)
    last = 0 if rev else r - 1
    return outr, outi, outr[last:last + 1, :], outi[last:last + 1, :]


def _s5_tile_specs(n_of, r):
    t = S5_TILE
    ucol = 4 * RET_HEADS * RET_DH // LANES
    u = pl.BlockSpec((r, LANES), lambda j, i: (n_of(i), ucol + j))
    col = pl.BlockSpec((r, LANES), lambda j, i: (n_of(i), j))
    state = pl.BlockSpec((r, 2 * t), lambda j, i: (n_of(i), j))
    table = pl.BlockSpec((r, 2 * t), lambda j, i: (0, j))
    bbt = pl.BlockSpec((None, LANES, 2 * t), lambda j, i: (j, 0, 0))
    cct = pl.BlockSpec((None, 2 * t, LANES), lambda j, i: (j, 0, 0))
    vec = pl.BlockSpec((1, LANES), lambda j, i: (0, j))
    return u, col, state, table, bbt, cct, vec


def _s5_fwd(proj, bbt, cct, apow, dvec, name):
    s = proj.shape[0]
    r, t = SCAN_ROWS, S5_TILE
    w = S5_GROUPS * S5_GROUP
    u_s, col, state, table, bb_s, cc_s, vec = _s5_tile_specs(lambda i: i, r)

    def body(u_ref, bb_ref, cc_ref, p_ref, d_ref, st_ref, y_ref, g_ref, cr, ci):
        @pl.when(pl.program_id(1) == 0)
        def _():
            cr[...] = jnp.zeros_like(cr)
            ci[...] = jnp.zeros_like(ci)

        u = u_ref[...]
        bu = _mxu(u, bb_ref[...])
        xr, xi, cr[...], ci[...] = _scan_block(bu[:, :t], bu[:, t:], p_ref[:, :t], p_ref[:, t:], cr[...], ci[...], False)
        st_ref[:, :t] = xr
        st_ref[:, t:] = xi
        y = _mxu(xr, cc_ref[:t, :]) + _mxu(xi, cc_ref[t:, :]) + d_ref[...] * u
        y_ref[...] = y
        g_ref[...] = _gelu(y).astype(BF16)

    return pl.pallas_call(
        body, name=name, grid=(2 * S5_GROUPS * S5_STATE // (2 * t), s // r),
        in_specs=[u_s, bb_s, cc_s, table, vec], out_specs=[state, col, col],
        out_shape=[jax.ShapeDtypeStruct((s, 2 * S5_GROUPS * S5_STATE), F32), jax.ShapeDtypeStruct((s, w), F32),
                   jax.ShapeDtypeStruct((s, w), BF16)],
        scratch_shapes=[pltpu.VMEM((1, t), F32), pltpu.VMEM((1, t), F32)],
        compiler_params=_cp("parallel", "arbitrary"),
    )(proj, bbt, cct, apow, dvec)


def _s5_bwd(dg1, dg2, y, proj, st, bbt, cct, apow_rev, dvec, dproj, name):
    s = proj.shape[0]
    r, t = SCAN_ROWS, S5_TILE
    nb = s // r
    w = S5_GROUPS * S5_GROUP
    u_s, col, state, table, bb_s, cc_s, vec = _s5_tile_specs(lambda i: nb - 1 - i, r)
    halo = pl.BlockSpec((8, 2 * t), lambda j, i: (jnp.maximum((nb - 1 - i) * (r // 8) - 1, 0), j))
    acc = pl.BlockSpec((1, 2 * t), lambda j, i: (0, j))

    def body(a_ref, b_ref, y_ref, u_ref, s_ref, sp_ref, bb_ref, cc_ref, p_ref, d_ref, _,
             du_ref, da_ref, dbb_ref, dcc_ref, dd_ref, cr, ci):
        i = pl.program_id(1)

        @pl.when(i == 0)
        def _():
            cr[...] = jnp.zeros_like(cr)
            ci[...] = jnp.zeros_like(ci)
            da_ref[...] = jnp.zeros_like(da_ref)
            dbb_ref[...] = jnp.zeros_like(dbb_ref)
            dcc_ref[...] = jnp.zeros_like(dcc_ref)
            dd_ref[...] = jnp.zeros_like(dd_ref)

        u = u_ref[...]
        dy = (a_ref[...] + b_ref[...]) * _gelu_grad(y_ref[...])
        dd_ref[...] += jnp.sum(dy * u, axis=0, keepdims=True)
        sr, si = s_ref[:, :t], s_ref[:, t:]
        dcc_ref[:t, :] += _mxu(sr, dy, 0, 0)
        dcc_ref[t:, :] += _mxu(si, dy, 0, 0)
        xr, xi, cr[...], ci[...] = _scan_block(_mxu(dy, cc_ref[:t, :], 1, 1), _mxu(dy, cc_ref[t:, :], 1, 1),
                                               p_ref[:, :t], p_ref[:, t:], cr[...], ci[...], True)
        du_ref[...] = (dy * d_ref[...] + _mxu(xr, bb_ref[:, :t], 1, 1) + _mxu(xi, bb_ref[:, t:], 1, 1)).astype(BF16)
        dbb_ref[:, :t] += _mxu(u, xr, 0, 0)
        dbb_ref[:, t:] += _mxu(u, xi, 0, 0)
        first = i == nb - 1
        row = lax.broadcasted_iota(jnp.int32, (r, t), 0)
        pr = jnp.where(row == 0, jnp.where(first, 0.0, sp_ref[7:8, :t]), pltpu.roll(sr, 1, 0))
        pi = jnp.where(row == 0, jnp.where(first, 0.0, sp_ref[7:8, t:]), pltpu.roll(si, 1, 0))
        da_ref[:, :t] += jnp.sum(xr * pr + xi * pi, axis=0, keepdims=True)
        da_ref[:, t:] += jnp.sum(xi * pr - xr * pi, axis=0, keepdims=True)

    return pl.pallas_call(
        body, name=name, grid=(2 * S5_GROUPS * S5_STATE // (2 * t), nb),
        in_specs=[col, col, col, u_s, state, halo, bb_s, cc_s, table, vec, pl.BlockSpec(memory_space=pl.ANY)],
        out_specs=[u_s, acc, bb_s, cc_s, vec],
        out_shape=[jax.ShapeDtypeStruct(dproj.shape, dproj.dtype), jax.ShapeDtypeStruct((1, 2 * S5_GROUPS * S5_STATE), F32),
                   jax.ShapeDtypeStruct(bbt.shape, F32), jax.ShapeDtypeStruct(cct.shape, F32),
                   jax.ShapeDtypeStruct((1, w), F32)],
        scratch_shapes=[pltpu.VMEM((1, t), F32), pltpu.VMEM((1, t), F32)],
        input_output_aliases={10: 0}, compiler_params=_cp("parallel", "arbitrary"),
    )(dg1, dg2, y, proj, st, st, bbt, cct, apow_rev, dvec, dproj)


def _s5_tile_b(b_re, b_im):
    nt = S5_GROUPS * S5_STATE // S5_TILE
    gpt = S5_GROUPS // nt
    eye = jnp.eye(gpt, dtype=F32)

    def tile(b):
        t5 = jnp.einsum("jghp,gk->jghkp", b.reshape(nt, gpt, S5_GROUP, S5_STATE), eye)
        return t5.reshape(nt, gpt * S5_GROUP, S5_TILE)

    return jnp.concatenate([tile(b_re), tile(b_im)], axis=2)


def _s5_untile_b(d):
    nt = S5_GROUPS * S5_STATE // S5_TILE
    gpt = S5_GROUPS // nt
    eye = jnp.eye(gpt, dtype=F32)

    def untile(x):
        x5 = x.reshape(nt, gpt, S5_GROUP, gpt, S5_STATE)
        return jnp.einsum("jghkp,gk->jghp", x5, eye).reshape(S5_GROUPS, S5_GROUP, S5_STATE)

    return untile(d[:, :, :S5_TILE]), untile(d[:, :, S5_TILE:])


def _s5_tile_c(c_re, c_im):
    nt = S5_GROUPS * S5_STATE // S5_TILE
    gpt = S5_GROUPS // nt
    eye = jnp.eye(gpt, dtype=F32)

    def tile(c):
        t5 = jnp.einsum("jgph,gk->jkpgh", c.reshape(nt, gpt, S5_STATE, S5_GROUP), eye)
        return t5.reshape(nt, S5_TILE, gpt * S5_GROUP)

    return jnp.concatenate([tile(c_re), -tile(c_im)], axis=1)


def _s5_untile_c(d):
    nt = S5_GROUPS * S5_STATE // S5_TILE
    gpt = S5_GROUPS // nt
    eye = jnp.eye(gpt, dtype=F32)

    def untile(x):
        x5 = x.reshape(nt, gpt, S5_STATE, gpt, S5_GROUP)
        return jnp.einsum("jkpgh,gk->jgph", x5, eye).reshape(S5_GROUPS, S5_STATE, S5_GROUP)

    return untile(d[:, :S5_TILE, :]), -untile(d[:, S5_TILE:, :])


def _row_call(body, name, s, ins, outs, acc=False):
    tr = min(512, s)

    def spec(width, cb, rows):
        if rows == 1:
            return pl.BlockSpec((1, width), lambda i: (0, cb))
        return pl.BlockSpec((tr, width), lambda i: (i, cb))

    in_specs = [spec(w, cb, a.shape[0]) for a, w, cb in ins]
    out_specs = [spec(w, cb, sd.shape[0]) for sd, w, cb in outs]
    return pl.pallas_call(
        body, name=name, grid=(s // tr,), in_specs=in_specs, out_specs=out_specs,
        out_shape=[sd for sd, _, _ in outs],
        compiler_params=_cp("arbitrary" if acc else "parallel"),
    )(*[a for a, _, _ in ins])


def _sds(shape, dtype):
    return jax.ShapeDtypeStruct(shape, dtype)


def _s5_gelu_fwd(yraw, proj, dvec, name):
    s, w = yraw.shape

    def body(y_ref, u_ref, d_ref, yo_ref, g_ref):
        y = y_ref[...] + d_ref[...] * u_ref[...]
        yo_ref[...] = y
        g_ref[...] = _gelu(y).astype(BF16)

    return _row_call(body, name, s, [(yraw, w, 0), (proj, w, 4), (dvec, w, 0)],
                     [(_sds((s, w), F32), w, 0), (_sds((s, w), BF16), w, 0)])


def _s5_glu_fwd(y, z, b, name):
    s, w = y.shape

    def body(y_ref, z_ref, b_ref, o_ref):
        o_ref[...] = (_gelu(y_ref[...]) * _sigmoid(z_ref[...] + b_ref[...])).astype(BF16)

    return _row_call(body, name, s, [(y, w, 0), (z, w, 0), (b, w, 0)], [(_sds((s, w), BF16), w, 0)])[0]


def _s5_glu_bwd(dmerged, y, z, b, name):
    s, w = y.shape

    def body(do_ref, y_ref, z_ref, b_ref, dz_ref, dg_ref, db_ref):
        @pl.when(pl.program_id(0) == 0)
        def _():
            db_ref[...] = jnp.zeros_like(db_ref)

        g = _gelu(y_ref[...])
        sg = _sigmoid(z_ref[...] + b_ref[...])
        dout = do_ref[...]
        dz = dout * g * sg * (1.0 - sg)
        dz_ref[...] = dz.astype(BF16)
        dg_ref[...] = dout * sg
        db_ref[...] += jnp.sum(dz, axis=0, keepdims=True)

    return _row_call(body, name, s, [(dmerged, w, 1), (y, w, 0), (z, w, 0), (b, w, 0)],
                     [(_sds((s, w), BF16), w, 0), (_sds((s, w), F32), w, 0), (_sds((1, w), F32), w, 0)], acc=True)


def _s5_gelu_bwd(dg1, dg2, y, proj, dvec, name):
    s, w = y.shape

    def body(a_ref, b_ref, y_ref, u_ref, d_ref, dy_ref, du_ref, dd_ref):
        @pl.when(pl.program_id(0) == 0)
        def _():
            dd_ref[...] = jnp.zeros_like(dd_ref)

        dy = (a_ref[...] + b_ref[...]) * _gelu_grad(y_ref[...])
        dy_ref[...] = dy.astype(BF16)
        du_ref[...] = dy * d_ref[...]
        dd_ref[...] += jnp.sum(dy * u_ref[...], axis=0, keepdims=True)

    return _row_call(body, name, s, [(dg1, w, 0), (dg2, w, 0), (y, w, 0), (proj, w, 4), (dvec, w, 0)],
                     [(_sds((s, w), BF16), w, 0), (_sds((s, w), F32), w, 0), (_sds((1, w), F32), w, 0)], acc=True)


def _gdn_conv_fwd(projx, cw, name):
    s = projx.shape[0]
    nh = GDN_HEADS

    def body(x_ref, w_ref, o_ref):
        j = pl.program_id(0)
        cv = _conv_fwd(_conv_taps(x_ref[...], GDN_CONV), w_ref)
        y = cv * _sigmoid(cv)
        nrm = y * lax.rsqrt(jnp.sum(y * y, axis=-1, keepdims=True) + EPS)
        o_ref[...] = jnp.where(j < nh, nrm * (GDN_DH ** -0.5), jnp.where(j < 2 * nh, nrm, y))

    return pl.pallas_call(
        body, name=name, grid=(3 * nh,),
        in_specs=[pl.BlockSpec((s, GDN_DH), lambda j: (0, j)), pl.BlockSpec((GDN_CONV, GDN_DH), lambda j: (0, j))],
        out_specs=pl.BlockSpec((s, GDN_DH), lambda j: (0, j)),
        out_shape=jax.ShapeDtypeStruct((s, 3 * nh * GDN_DH), F32), compiler_params=_cp("parallel"),
    )(projx, cw)


def _gdn_conv_bwd(projx, cw, dqkv, dprojx, name):
    s = projx.shape[0]
    nh = GDN_HEADS

    def body(x_ref, w_ref, d_ref, _, dx_ref, dw_ref):
        j = pl.program_id(0)
        x = x_ref[...]
        taps = _conv_taps(x, GDN_CONV)
        cv = _conv_fwd(taps, w_ref)
        sg = _sigmoid(cv)
        y = cv * sg
        rinv = lax.rsqrt(jnp.sum(y * y, axis=-1, keepdims=True) + EPS)
        nrm = y * rinv
        dn = d_ref[...]
        dns = jnp.where(j < nh, dn * (GDN_DH ** -0.5), dn)
        dyn = rinv * (dns - nrm * jnp.sum(dns * nrm, axis=-1, keepdims=True))
        dy = jnp.where(j < 2 * nh, dyn, dn)
        dc = dy * (sg * (1.0 + cv * (1.0 - sg)))
        dx_ref[...] = _conv_bwd(taps, dc, w_ref, dw_ref).astype(BF16)

    col = pl.BlockSpec((s, GDN_DH), lambda j: (0, j))
    wcol = pl.BlockSpec((GDN_CONV, GDN_DH), lambda j: (0, j))
    return pl.pallas_call(
        body, name=name, grid=(3 * nh,), in_specs=[col, wcol, col, pl.BlockSpec(memory_space=pl.ANY)],
        out_specs=[col, wcol],
        out_shape=[jax.ShapeDtypeStruct(dprojx.shape, dprojx.dtype), jax.ShapeDtypeStruct((GDN_CONV, 3 * nh * GDN_DH), F32)],
        input_output_aliases={3: 0}, compiler_params=_cp("parallel"),
    )(projx, cw, dqkv, dprojx)


def _softplus(x):
    return jnp.maximum(x, 0.0) + jnp.log1p(jnp.exp(-jnp.abs(x)))


def _gdn_gates_fwd(projx, alog, dtb, name):
    s = projx.shape[0]
    w = GDN_HEADS * GDN_DH

    def body(b_ref, a_ref, al_ref, dt_ref, bo_ref, go_ref):
        bo_ref[...] = _sigmoid(b_ref[...])
        go_ref[...] = -jnp.exp(al_ref[...]) * _softplus(a_ref[...] + dt_ref[...])

    return _row_call(body, name, s, [(projx, w, 4), (projx, w, 5), (alog, w, 0), (dtb, w, 0)],
                     [(_sds((s, w), F32), w, 0), (_sds((s, w), F32), w, 0)])


def _gdn_gates_bwd(projx, alog, dtb, dbeta, dg, dprojx, name):
    s = projx.shape[0]
    w = GDN_HEADS * GDN_DH
    tr = min(512, s)

    def body(b_ref, a_ref, al_ref, dt_ref, dbe_ref, dg_ref, _, o_ref, dal_ref, ddt_ref):
        @pl.when(pl.program_id(0) == 0)
        def _():
            dal_ref[...] = jnp.zeros_like(dal_ref)
            ddt_ref[...] = jnp.zeros_like(ddt_ref)

        for h in range(GDN_HEADS):
            lo, hi = h * GDN_DH, (h + 1) * GDN_DH
            beta = _sigmoid(b_ref[:, lo:hi])
            pb = jnp.sum(dbe_ref[:, lo:hi], axis=-1, keepdims=True) * (1.0 / GDN_DH)
            o_ref[:, lo:hi] = (pb * beta * (1.0 - beta)).astype(BF16)
            xa = a_ref[:, lo:hi] + dt_ref[:, lo:hi]
            ea = -jnp.exp(al_ref[:, lo:hi])
            pg = jnp.sum(dg_ref[:, lo:hi], axis=-1, keepdims=True) * (1.0 / GDN_DH)
            da = pg * ea * _sigmoid(xa)
            o_ref[:, w + lo:w + hi] = da.astype(BF16)
            dal_ref[:, lo:hi] += jnp.sum(pg * ea * _softplus(xa), axis=0, keepdims=True)
            ddt_ref[:, lo:hi] += jnp.sum(da, axis=0, keepdims=True)

    def row(cb):
        return pl.BlockSpec((tr, w), lambda i: (i, cb))

    vec = pl.BlockSpec((1, w), lambda i: (0, 0))
    return pl.pallas_call(
        body, name=name, grid=(s // tr,),
        in_specs=[row(4), row(5), vec, vec, row(0), row(0), pl.BlockSpec(memory_space=pl.ANY)],
        out_specs=[pl.BlockSpec((tr, 2 * w), lambda i: (i, 2)), vec, vec],
        out_shape=[jax.ShapeDtypeStruct(dprojx.shape, dprojx.dtype), jax.ShapeDtypeStruct((1, w), F32),
                   jax.ShapeDtypeStruct((1, w), F32)],
        input_output_aliases={6: 0}, compiler_params=_cp("arbitrary"),
    )(projx, projx, alog, dtb, dbeta, dg, dprojx)


def _gdn_tri():
    c = GDN_CHUNK
    i = lax.broadcasted_iota(jnp.int32, (c, c), 0)
    j = lax.broadcasted_iota(jnp.int32, (c, c), 1)
    return ((i >= j).astype(F32), (i <= j).astype(F32), i >= j, i > j, (i == j).astype(F32))


def _bdot(a, b, ca=2, cb=1, precision=None):
    return lax.dot_general(a, b, (((ca,), (cb,)), ((0,), (0,))), precision=precision, preferred_element_type=F32)


def _bmxu(a, b, ca=2, cb=1):
    return _bdot(a.astype(BF16), b.astype(BF16), ca, cb)


def _split(x):
    hi = x.astype(BF16)
    return hi, (x - hi.astype(F32)).astype(BF16)


def _bdot3(a, b, ca=2, cb=1):
    ah, al = _split(a)
    bh, bl = _split(b)
    return _bdot(ah, bh, ca, cb) + (_bdot(ah, bl, ca, cb) + _bdot(al, bh, ca, cb))


def _tri_dot(tri, x):
    t = tri.astype(BF16)
    hi = x.astype(BF16)
    r1 = x - hi.astype(F32)
    mid = r1.astype(BF16)
    lo = (r1 - mid.astype(F32)).astype(BF16)
    return _dot(t, hi) + (_dot(t, mid) + _dot(t, lo))


def _heads(x):
    return jnp.stack([x[:, h * GDN_DH:(h + 1) * GDN_DH] for h in range(GDN_HEADS)], axis=0)


def _unheads(x):
    return jnp.concatenate([x[h] for h in range(GDN_HEADS)], axis=1)


def _gdn_chunk(q, k, v, bb, g2d, tri):
    low, up, incl, strict, eye = tri
    c = GDN_CHUNK
    gc = _heads(_tri_dot(low, g2d))
    gci = gc[:, :, :c]
    gdiff = gci - jnp.swapaxes(gci, 1, 2)
    decay = jnp.where(incl, jnp.exp(jnp.where(incl, gdiff, 0.0)), 0.0)
    kb, vb = k * bb, v * bb
    kbk = _bmxu(kb, k, 2, 2)
    x = -jnp.where(strict, kbk * decay, 0.0)
    t = eye + x
    p = x
    for _ in range(c.bit_length() - 2):
        p = _bdot3(p, p)
        t = t + _bdot3(t, p)
    eg = jnp.exp(gc)
    kbg = kb * eg
    gcl = gc[:, c - 1:c, :]
    ek = jnp.exp(gcl - gc)
    qkraw = _bmxu(q, k, 2, 2)
    return dict(decay=decay, kb=kb, vb=vb, kbk=kbk, t=t, eg=eg, kbg=kbg, ek=ek, gl=jnp.exp(gcl),
                w=_bmxu(t, kbg), u=_bmxu(t, vb), qkraw=qkraw, qk=jnp.where(incl, qkraw * decay, 0.0),
                qd=q * eg, kd=k * ek)


def _gdn_specs(n_of):
    c, w = GDN_CHUNK, GDN_HEADS * GDN_DH

    def blk(cb, width=w):
        return pl.BlockSpec((c, width), lambda n: (n_of(n), cb))

    st = pl.BlockSpec((None, GDN_HEADS, GDN_DH, GDN_DH), lambda n: (n_of(n), 0, 0, 0))
    vec = pl.BlockSpec((1, GDN_DH), lambda n: (0, 0))
    return blk, st, vec


def _gdn_load(qkv_ref, b_ref, g_ref, tri):
    w = GDN_HEADS * GDN_DH
    q, k, v = _heads(qkv_ref[:, :w]), _heads(qkv_ref[:, w:2 * w]), _heads(qkv_ref[:, 2 * w:])
    bb = _heads(b_ref[...])
    return q, k, v, bb, _gdn_chunk(q, k, v, bb, g_ref[...], tri)


def _gdn_fwd(qkv, beta, g, projx, onorm, name):
    s = qkv.shape[0]
    nc = s // GDN_CHUNK
    w = GDN_HEADS * GDN_DH
    blk, st, vec = _gdn_specs(lambda n: n)

    def body(qkv_ref, b_ref, g_ref, z_ref, on_ref, o_ref, st_ref, state):
        @pl.when(pl.program_id(0) == 0)
        def _():
            state[...] = jnp.zeros_like(state)

        _, _, _, _, ch = _gdn_load(qkv_ref, b_ref, g_ref, _gdn_tri())
        sp = state[...]
        st_ref[...] = sp
        vn = ch["u"] - _bmxu(ch["w"], sp)
        o = _bmxu(ch["qd"], sp) + _bmxu(ch["qk"], vn)
        state[...] = sp * ch["gl"] + _bmxu(ch["kd"], vn, 1, 1)
        r = lax.rsqrt(jnp.mean(o * o, axis=-1, keepdims=True) + EPS)
        z = _heads(z_ref[...])
        o_ref[...] = _unheads(o * r * on_ref[...] * (z * _sigmoid(z))).astype(BF16)

    return pl.pallas_call(
        body, name=name, grid=(nc,),
        in_specs=[blk(0, 3 * w), blk(0), blk(0), blk(3), vec], out_specs=[blk(0), st],
        out_shape=[jax.ShapeDtypeStruct((s, w), BF16), jax.ShapeDtypeStruct((nc, GDN_HEADS, GDN_DH, GDN_DH), F32)],
        scratch_shapes=[pltpu.VMEM((GDN_HEADS, GDN_DH, GDN_DH), F32)],
        compiler_params=_cp("arbitrary"),
    )(qkv, beta, g, projx, onorm.reshape(1, -1))


def _gdn_bwd(qkv, beta, g, projx, onorm, states, dout, name):
    s = qkv.shape[0]
    c = GDN_CHUNK
    nc = s // c
    w = GDN_HEADS * GDN_DH
    blk, st, vec = _gdn_specs(lambda n: nc - 1 - n)

    def body(qkv_ref, b_ref, g_ref, z_ref, on_ref, st_ref, do_ref,
             dqkv_ref, db_ref, dg_ref, dz_ref, don_ref, carry):
        @pl.when(pl.program_id(0) == 0)
        def _():
            carry[...] = jnp.zeros_like(carry)
            don_ref[...] = jnp.zeros_like(don_ref)

        tri = _gdn_tri()
        low, up, incl, strict, eye = tri
        q, k, v, bb, ch = _gdn_load(qkv_ref, b_ref, g_ref, tri)
        sp = st_ref[...]
        vn = ch["u"] - _bmxu(ch["w"], sp)
        o = _bmxu(ch["qd"], sp) + _bmxu(ch["qk"], vn)
        r = lax.rsqrt(jnp.mean(o * o, axis=-1, keepdims=True) + EPS)
        orn = o * r
        z = _heads(z_ref[...])
        sg = _sigmoid(z)
        dout = _heads(do_ref[...])
        onw = on_ref[...]
        dz_ref[...] = _unheads(dout * orn * onw * (sg * (1.0 + z * (1.0 - sg)))).astype(BF16)
        don = dout * (z * sg)
        don_ref[...] += jnp.sum(jnp.sum(don * orn, axis=0), axis=0, keepdims=True)
        dor = don * onw
        do = r * (dor - orn * jnp.mean(dor * orn, axis=-1, keepdims=True))
        dsn = carry[...]
        dqd = _bmxu(do, sp, 2, 2)
        dqk = jnp.where(incl, _bmxu(do, vn, 2, 2), 0.0)
        dvn = _bmxu(ch["qk"], do, 1, 1) + _bmxu(ch["kd"], dsn)
        dkd = _bmxu(vn, dsn, 2, 2)
        dgl = jnp.sum(dsn * sp, axis=1, keepdims=True)
        dw = -_bmxu(dvn, sp, 2, 2)
        carry[...] = _bmxu(ch["qd"], do, 1, 1) + dsn * ch["gl"] - _bmxu(ch["w"], dvn, 1, 1)
        t = ch["t"]
        dvb = _bmxu(t, dvn, 1, 1)
        dkbg = _bmxu(t, dw, 1, 1)
        dt = _bmxu(dvn, ch["vb"], 2, 2) + _bmxu(dw, ch["kbg"], 2, 2)
        da = -_bdot3(_bdot3(t, dt, 1, 1), t, 2, 2)
        da = jnp.where(strict, da, 0.0)
        decay = ch["decay"]
        dkbk = da * decay
        dqkr = dqk * decay
        mdec = (da * ch["kbk"] + dqk * ch["qkraw"]) * decay
        dkb = _bmxu(dkbk, k) + dkbg * ch["eg"]
        dk = _bmxu(dkbk, ch["kb"], 1, 1) + _bmxu(dqkr, q, 1, 1) + dkd * ch["ek"] + dkb * bb
        dq = _bmxu(dqkr, k) + dqd * ch["eg"]
        tk = dkd * ch["kd"]
        dgcl = jnp.sum(tk, axis=1, keepdims=True) + dgl * ch["gl"]
        row = lax.broadcasted_iota(jnp.int32, (GDN_HEADS, c, GDN_DH), 1)
        zpad = jnp.zeros((GDN_HEADS, c, GDN_DH - c), F32)
        dgc = (jnp.concatenate([mdec, zpad], axis=2) - jnp.concatenate([jnp.swapaxes(mdec, 1, 2), zpad], axis=2)
               + dqd * ch["qd"] - tk + dkbg * ch["kbg"] + jnp.where(row == c - 1, dgcl, 0.0))
        dqkv_ref[:, :w] = _unheads(dq)
        dqkv_ref[:, w:2 * w] = _unheads(dk)
        dqkv_ref[:, 2 * w:] = _unheads(dvb * bb)
        db_ref[...] = _unheads(dkb * k + dvb * v)
        dg_ref[...] = _tri_dot(up, _unheads(dgc))

    return pl.pallas_call(
        body, name=name, grid=(nc,),
        in_specs=[blk(0, 3 * w), blk(0), blk(0), blk(3), vec, st, blk(0)],
        out_specs=[blk(0, 3 * w), blk(0), blk(0), blk(3), vec],
        out_shape=[jax.ShapeDtypeStruct((s, 3 * w), F32), jax.ShapeDtypeStruct((s, w), F32),
                   jax.ShapeDtypeStruct((s, w), F32), jax.ShapeDtypeStruct(projx.shape, BF16),
                   jax.ShapeDtypeStruct((1, GDN_DH), F32)],
        scratch_shapes=[pltpu.VMEM((GDN_HEADS, GDN_DH, GDN_DH), F32)],
        compiler_params=_cp("arbitrary"),
    )(qkv, beta, g, projx, onorm.reshape(1, -1), states, dout)


_WEIGHTS = (
    "l0_mix_norm", "l0_w_in", "l0_ret_norm", "l0_s5_lambda_re", "l0_s5_lambda_im", "l0_s5_b_re", "l0_s5_b_im",
    "l0_s5_c_re", "l0_s5_c_im", "l0_s5_d", "l0_s5_log_dt", "l0_s5_w_glu", "l0_s5_b_glu", "l0_w_out",
    "l0_xa_norm", "l0_mem_norm", "l0_xa_wq", "l0_xa_wkv", "l0_xa_wo", "l0_ffn_norm", "l0_ffn_w_up",
    "l0_ffn_conv", "l0_ffn_w_down", "l1_mix_norm", "l1_w_in", "l1_conv", "l1_a_log", "l1_dt_bias", "l1_o_norm",
    "l1_w_out", "l1_xa_norm", "l1_mem_norm", "l1_xa_wq", "l1_xa_wkv", "l1_xa_wo", "l1_ffn_norm", "l1_ffn_w_up",
    "l1_ffn_conv", "l1_ffn_w_down", "final_norm")
_INPUTS = ("x", "mem") + _WEIGHTS + ("loss_target",) + tuple("m_" + n for n in _WEIGHTS) + tuple("v_" + n for n in _WEIGHTS)

_COL = ("l0_w_in", "l0_xa_wkv", "l0_ffn_w_up", "l0_ffn_conv", "l1_w_in", "l1_conv", "l1_xa_wkv", "l1_ffn_w_up",
        "l1_ffn_conv")
_ROW = ("l0_s5_w_glu", "l0_w_out", "l0_xa_wq", "l0_xa_wo", "l0_ffn_w_down", "l1_w_out", "l1_xa_wq", "l1_xa_wo",
        "l1_ffn_w_down")
_F32_WIRE = ("l0_ffn_conv", "l1_conv", "l1_ffn_conv")
_REP = tuple(n for n in _WEIGHTS if n not in _COL + _ROW)
_GATHER_GROUPS = (("l0_w_in", "l0_s5_w_glu", "l0_w_out"),
                  ("l0_xa_wq", "l0_xa_wkv", "l0_xa_wo", "l0_ffn_w_up", "l0_ffn_conv", "l0_ffn_w_down"),
                  ("l1_w_in", "l1_conv", "l1_w_out", "l1_xa_wq", "l1_xa_wkv", "l1_xa_wo"),
                  ("l1_ffn_w_up", "l1_ffn_conv", "l1_ffn_w_down"))


def _round_up(n, m):
    return (n + m - 1) // m * m


_REP_BIG = ("l0_s5_lambda_re", "l0_s5_lambda_im", "l0_s5_b_re", "l0_s5_b_im", "l0_s5_c_re", "l0_s5_c_im", "l0_s5_d")
_REP_LAST = "l0_mix_norm"
_REP_SMALL = tuple(n for n in _REP if n not in _REP_BIG + (_REP_LAST,))
PACK_WIDTH = 1024


def _pack_rows(ts):
    rows = [jnp.pad(t, ((0, 0), (0, PACK_WIDTH - t.shape[1]))) for t in ts]
    rows.append(jnp.zeros((_round_up(len(ts), 8) - len(ts), PACK_WIDTH), F32))
    return jnp.concatenate(rows, axis=0)


def _s5_interleave(re, im):
    lead = re.shape[:-1]
    nt = re.shape[-1] // S5_TILE
    both = jnp.stack([re.reshape(lead + (nt, S5_TILE)), im.reshape(lead + (nt, S5_TILE))], axis=-2)
    return both.reshape(lead + (2 * re.shape[-1],))


def _s5_split(x):
    lead = x.shape[:-1]
    y = x.reshape(lead + (x.shape[-1] // (2 * S5_TILE), 2, S5_TILE))
    return y[..., 0, :].reshape(lead + (-1,)), y[..., 1, :].reshape(lead + (-1,))


def _s5_discretise(lr, li, log_dt, b_re, b_im):
    dt = jnp.exp(log_dt)[:, None]
    mag = jnp.exp(lr * dt)
    a_re = mag * jnp.cos(li * dt)
    a_im = mag * jnp.sin(li * dt)
    den = lr * lr + li * li
    z_re = ((a_re - 1.0) * lr + a_im * li) / den
    z_im = (a_im * lr - (a_re - 1.0) * li) / den
    bb_re = z_re[:, None, :] * b_re - z_im[:, None, :] * b_im
    bb_im = z_re[:, None, :] * b_im + z_im[:, None, :] * b_re
    return a_re, a_im, bb_re, bb_im


def _block_diag(b):
    g, r, c = b.shape
    return jnp.einsum("grc,gk->grkc", b, jnp.eye(g, dtype=b.dtype)).reshape(g * r, g * c)


def _block_diag_of(d, g):
    r, c = d.shape[0] // g, d.shape[1] // g
    return jnp.einsum("grkc,gk->grc", d.reshape(g, r, g, c), jnp.eye(g, dtype=d.dtype))


def kernel(*args):
    p = dict(zip(_INPUTS, args, strict=True))
    x0, mem0, tgt = p["x"][0], p["mem"][0], p["loss_target"][0]
    s, d = x0.shape
    me = _slot(*_mesh_pos())
    grads = {}
    wire = {n: (F32 if n in _F32_WIRE else BF16) for n in _COL + _ROW}

    zones = {n: _into_slot(p[n], wire[n], me, "place_" + n) for names in _GATHER_GROUPS for n in names}
    gather, pin = [], jnp.zeros((), F32)
    for i, names in enumerate(_GATHER_GROUPS):
        handle, token = _push_start([], [zones[n] for n in names], f"gather{i}_start")
        gather.append(handle)
        pin = pin + token[0, 0]
    w = {}

    def gathered(i, after):
        for n, full in zip(_GATHER_GROUPS[i], _push_wait(gather[i], after, f"gather{i}_wait")):
            if n in _COL:
                full = full.transpose(1, 0, 2)
            w[n] = full.reshape(-1, full.shape[-1]) if n in _ROW else full.reshape(full.shape[0], -1)

    pending = []

    def exchange(names, gain, tag):
        slots = []
        for n in names:
            g = grads[n]
            if n in _COL:
                pieces = g if isinstance(g, tuple) else (g,)
                g = jnp.concatenate([t.reshape(t.shape[0], -1, p[n].shape[1]).transpose(1, 0, 2) for t in pieces], axis=0)
            else:
                g = g.reshape((N_DEV, -1) + g.shape[1:])
            slots.append(g.astype(wire[n]))
        handle, token = _push_start(slots, [], tag + "_start")
        pending.append((names, slots, handle, tag))
        return gain + token[0, 0]

    def xattn(pre, x_in):
        hx = _norm_fwd(x_in, p[pre + "xa_norm"], pre + "xa_norm_fwd")
        q = _mm(hx, w[pre + "xa_wq"], out_dtype=BF16, name=pre + "xa_q")
        memn = _norm_fwd(mem0, p[pre + "mem_norm"], pre + "mem_norm_fwd")
        kv = _mm(memn, w[pre + "xa_wkv"], out_dtype=BF16, name=pre + "xa_kv")
        ao = _xattn_fwd(q, kv, pre + "xattn_fwd")
        x_out = _mm(ao, w[pre + "xa_wo"], res=x_in, name=pre + "xa_o")
        return x_out, (x_in, hx, q, memn, kv, ao)

    def xattn_bwd(pre, saved, dxo):
        x_in, hx, q, memn, kv, ao = saved
        dao = _mm(dxo, w[pre + "xa_wo"], tb=True, name=pre + "xa_o_dx")
        grads[pre + "xa_wo"] = _mm(ao, dxo, ta=True, out_dtype=BF16, name=pre + "xa_o_dw")
        dq, dkv = _xattn_bwd(q, kv, dao, pre + "xattn_bwd")
        grads[pre + "xa_wq"] = _mm(hx, dq, ta=True, out_dtype=BF16, name=pre + "xa_q_dw")
        grads[pre + "xa_wkv"] = _mm(memn, dkv, ta=True, out_dtype=BF16, name=pre + "xa_kv_dw")
        dmemn = _mm(dkv, w[pre + "xa_wkv"], tb=True, name=pre + "xa_kv_dx")
        gain = exchange((pre + "xa_wo", pre + "xa_wq", pre + "xa_wkv"), p[pre + "xa_norm"], pre + "xa_grads")
        dx_in, grads[pre + "xa_norm"] = _mm_norm_bwd(dq, w[pre + "xa_wq"], x_in, gain, dxo, pre + "xa_q_dx")
        _, grads[pre + "mem_norm"] = _norm_bwd(mem0, p[pre + "mem_norm"], dmemn, jnp.zeros_like(mem0), pre + "mem_norm_bwd")
        return dx_in

    def ffn(pre, x_in):
        hf = _norm_fwd(x_in, p[pre + "ffn_norm"], pre + "ffn_norm_fwd")
        up = _mm(hf, w[pre + "ffn_w_up"], out_dtype=BF16, name=pre + "ffn_up")
        act = _ffn_act_fwd(up, w[pre + "ffn_conv"], pre + "ffn_act_fwd")
        x_out = _mm(act, w[pre + "ffn_w_down"], res=x_in, name=pre + "ffn_down")
        return x_out, (x_in, hf, up, act)

    def ffn_bwd(pre, saved, dxo):
        x_in, hf, up, act = saved
        dact = _mm(dxo, w[pre + "ffn_w_down"], tb=True, out_dtype=BF16, name=pre + "ffn_down_dx")
        grads[pre + "ffn_w_down"] = _mm(act, dxo, ta=True, out_dtype=BF16, name=pre + "ffn_down_dw")
        dpu, dpg, dcu, dcg = _ffn_act_bwd(up, w[pre + "ffn_conv"], dact, pre + "ffn_act_bwd")
        grads[pre + "ffn_conv"] = jnp.concatenate([dcu, dcg], axis=1)
        grads[pre + "ffn_w_up"] = (_mm(hf, dpu, ta=True, out_dtype=BF16, name=pre + "ffn_up_dw_u"),
                                   _mm(hf, dpg, ta=True, out_dtype=BF16, name=pre + "ffn_up_dw_g"))
        gain = exchange((pre + "ffn_w_down", pre + "ffn_w_up", pre + "ffn_conv"), p[pre + "ffn_norm"], pre + "ffn_grads")
        dx_in, grads[pre + "ffn_norm"] = _mm_norm_bwd([dpu, dpg], w[pre + "ffn_w_up"], x_in, gain, dxo, pre + "ffn_up_dx")
        return dx_in

    cos, sin = _rope_tables(s)
    (a_re, a_im, bb_re, bb_im), disc_vjp = jax.vjp(
        _s5_discretise, p["l0_s5_lambda_re"], p["l0_s5_lambda_im"], p["l0_s5_log_dt"], p["l0_s5_b_re"], p["l0_s5_b_im"])
    apow, apow_rev = _s5_pow_tables(_s5_interleave(a_re.reshape(1, -1), a_im.reshape(1, -1)), "l0_s5_pow_tables")
    bbt = _s5_tile_b(bb_re, bb_im).astype(BF16)
    cct = _s5_tile_c(p["l0_s5_c_re"], p["l0_s5_c_im"]).astype(BF16)
    s5_d = p["l0_s5_d"].reshape(1, -1)
    b_glu = p["l0_s5_b_glu"].reshape(1, -1)

    h0 = _norm_fwd(x0, p["l0_mix_norm"] + pin, "l0_mix_norm_fwd")
    gathered(0, h0)
    proj = _mm(h0, w["l0_w_in"], name="l0_in")
    o_ret, ret_states = _ret_fwd(proj, cos, sin, p["l0_ret_norm"], "l0_ret_fwd")
    st, y, gy = _s5_fwd(proj, bbt, cct, apow, s5_d, "l0_s5_fwd")
    z = _mm(gy, w["l0_s5_w_glu"], name="l0_s5_glu_mm")
    y2 = _s5_glu_fwd(y, z, b_glu, "l0_s5_glu_fwd")
    merged = jnp.concatenate([o_ret, y2], axis=1)
    x1 = _mm(merged, w["l0_w_out"], res=x0, name="l0_out")
    gathered(1, x1)
    x2, xa0 = xattn("l0_", x1)
    x3, ff0 = ffn("l0_", x2)

    gathered(2, x3)
    nqkv = 4 * GDN_HEADS * GDN_DH
    w1 = w["l1_w_in"]
    wx = jnp.concatenate([w1[:, :nqkv], jnp.repeat(w1[:, nqkv:nqkv + GDN_HEADS], GDN_DH, axis=1),
                          jnp.repeat(w1[:, nqkv + GDN_HEADS:], GDN_DH, axis=1)], axis=1)
    alog_x = jnp.repeat(p["l1_a_log"], GDN_DH).reshape(1, -1)
    dtb_x = jnp.repeat(p["l1_dt_bias"], GDN_DH).reshape(1, -1)
    h1 = _norm_fwd(x3, p["l1_mix_norm"], "l1_mix_norm_fwd")
    projx = _mm(h1, wx, name="l1_in")
    qkv = _gdn_conv_fwd(projx, w["l1_conv"], "l1_conv_fwd")
    beta, glog = _gdn_gates_fwd(projx, alog_x, dtb_x, "l1_gates_fwd")
    o_gdn, gdn_states = _gdn_fwd(qkv, beta, glog, projx, p["l1_o_norm"], "l1_gdn_fwd")
    x4 = _mm(o_gdn, w["l1_w_out"], res=x3, name="l1_out")
    x5, xa1 = xattn("l1_", x4)
    gathered(3, x5)
    x6, ff1 = ffn("l1_", x5)

    loss_part, dx6, grads["final_norm"] = _loss_head(x6, p["final_norm"], tgt, "loss_head")
    loss = lax.psum(loss_part[0, 0], ("x", "y", "c"))
    dx5 = ffn_bwd("l1_", ff1, dx6)
    dx4 = xattn_bwd("l1_", xa1, dx5)

    do_gdn = _mm(dx4, w["l1_w_out"], tb=True, name="l1_out_dx")
    grads["l1_w_out"] = _mm(o_gdn, dx4, ta=True, out_dtype=BF16, name="l1_out_dw")
    dqkv, dbeta, dglog, dprojx, grads["l1_o_norm"] = _gdn_bwd(
        qkv, beta, glog, projx, p["l1_o_norm"], gdn_states, do_gdn, "l1_gdn_bwd")
    dprojx, grads["l1_conv"] = _gdn_conv_bwd(projx, w["l1_conv"], dqkv, dprojx, "l1_conv_bwd")
    dprojx, dalog_x, ddtb_x = _gdn_gates_bwd(projx, alog_x, dtb_x, dbeta, dglog, dprojx, "l1_gates_bwd")
    dwx = _mm(h1, dprojx, ta=True, name="l1_in_dw")
    grads["l1_w_in"] = jnp.concatenate(
        [dwx[:, :nqkv], dwx[:, nqkv:nqkv + GDN_HEADS * GDN_DH].reshape(d, GDN_HEADS, GDN_DH).sum(-1),
         dwx[:, nqkv + GDN_HEADS * GDN_DH:].reshape(d, GDN_HEADS, GDN_DH).sum(-1)], axis=1)
    grads["l1_a_log"] = dalog_x.reshape(GDN_HEADS, GDN_DH).sum(-1)
    grads["l1_dt_bias"] = ddtb_x.reshape(GDN_HEADS, GDN_DH).sum(-1)
    gain = exchange(("l1_w_out", "l1_w_in", "l1_conv"), p["l1_mix_norm"], "l1_mix_grads")
    dx3, grads["l1_mix_norm"] = _mm_norm_bwd(dprojx, wx, x3, gain, dx4, "l1_in_dx")

    dx2 = ffn_bwd("l0_", ff0, dx3)
    dx1 = xattn_bwd("l0_", xa0, dx2)

    dmerged = _mm(dx1, w["l0_w_out"], tb=True, name="l0_out_dx")
    grads["l0_w_out"] = _mm(merged, dx1, ta=True, out_dtype=BF16, name="l0_out_dw")
    dproj, grads["l0_ret_norm"] = _ret_bwd(proj, cos, sin, p["l0_ret_norm"], ret_states, dmerged, "l0_ret_bwd")
    dzg, dg1, grads["l0_s5_b_glu"] = _s5_glu_bwd(dmerged, y, z, b_glu, "l0_s5_glu_bwd")
    grads["l0_s5_w_glu"] = _mm(gy, dzg, ta=True, out_dtype=BF16, name="l0_s5_glu_dw")
    s5_d_after = exchange(("l0_w_out", "l0_s5_w_glu"), s5_d, "l0_out_grads")
    dg2 = _mm(dzg, w["l0_s5_w_glu"], tb=True, name="l0_s5_glu_dx")
    dproj, da_s5, dbbt, dcct, grads["l0_s5_d"] = _s5_bwd(dg1, dg2, y, proj, st, bbt, cct, apow_rev, s5_d_after, dproj, "l0_s5_bwd")
    dbb_re, dbb_im = _s5_untile_b(dbbt)
    grads["l0_s5_c_re"], grads["l0_s5_c_im"] = _s5_untile_c(dcct)
    da_re, da_im = (t.reshape(S5_GROUPS, S5_STATE) for t in _s5_split(da_s5[0]))
    (grads["l0_s5_lambda_re"], grads["l0_s5_lambda_im"], grads["l0_s5_log_dt"], grads["l0_s5_b_re"],
     grads["l0_s5_b_im"]) = disc_vjp((da_re, da_im, dbb_re, dbb_im))

    def as_2d(t):
        return t.reshape(-1, t.shape[-1])

    def as_row(t):
        return t.reshape(1, -1)

    small_own = _pack_rows([as_row(grads[n]) for n in _REP_SMALL])
    big_own = [as_2d(grads[n].reshape(p[n].shape)) for n in _REP_BIG]
    rep_zones = [_into_slot(small_own, F32, me, "place_rep0")]
    rep_zones += [_into_slot(t.reshape(-1, LANES), BF16, me, f"place_rep{i + 1}") for i, t in enumerate(big_own)]
    rep_handle, rep_token = _push_start([], rep_zones, "rep_grads_start")

    grads["l0_w_in"] = _mm(h0, dproj, ta=True, out_dtype=BF16, pin=rep_token, name="l0_in_dw")
    gain = exchange(("l0_w_in",), p["l0_mix_norm"], "l0_mix_grads")
    dx0, grads["l0_mix_norm"] = _mm_norm_bwd(dproj, w["l0_w_in"], x0, gain, dx1, "l0_in_dx")

    last_own = _pack_rows([as_row(grads[_REP_LAST])])
    last_handle, _ = _push_start([], [_into_slot(last_own, F32, me, "place_rep_last")], "rep_last_start")
    last_land, = _push_wait(last_handle, dx0, "rep_last_wait")
    rep_lands = _push_wait(rep_handle, last_land, "rep_grads_wait")
    rep_land = rep_lands[0]

    outs = {}
    kinds = ("grad_", "delta_", "new_m_", "new_v_")
    for names, slots, handle, tag in pending:
        for n, own_slots, land in zip(names, slots, _push_wait(handle, rep_land, tag + "_wait")):
            shape = p[n].shape
            own = lax.dynamic_index_in_dim(own_slots, me, 0, keepdims=False)
            res = _adamw(land, own, *(p[pre + n].reshape(own.shape) for pre in ("", "m_", "v_")), "adamw_" + n)
            for kind, t in zip(kinds, res):
                outs[kind + n] = t.reshape(shape)
    for n, own, land in zip(_REP_BIG, big_own, rep_lands[1:]):
        res = _adamw(land.reshape((N_DEV,) + own.shape), None, *(as_2d(p[pre + n]) for pre in ("", "m_", "v_")), "adamw_" + n)
        for kind, t in zip(kinds, res):
            outs[kind + n] = t.reshape(p[n].shape)
    for names, land, own, nm in ((_REP_SMALL, rep_land, small_own, "adamw_small"), ((_REP_LAST,), last_land, last_own, "adamw_last")):
        res = _adamw_rows(land, own, *([as_row(p[pre + n]) for n in names] for pre in ("", "m_", "v_")), nm)
        for j, kind in enumerate(kinds):
            for i, n in enumerate(names):
                outs[kind + n] = res[j * len(names) + i].reshape(p[n].shape)

    return (loss, dx0[None]) + tuple(outs[kind + n] for kind in kinds for n in _WEIGHTS)
```

```python
import functools
import math

import numpy as np
import jax
import jax.numpy as jnp
from jax import lax
from jax.experimental import pallas as pl
from jax.experimental.pallas import tpu as pltpu

F32 = jnp.float32
BF16 = jnp.bfloat16
EPS = 1e-6
N_DEV = 8
LANES = 128
VMEM_LIMIT = 48 * 1024 * 1024
HI = lax.Precision.HIGHEST

RET_HEADS, RET_DH, RET_CHUNK = 4, 128, 128
S5_GROUPS, S5_GROUP, S5_STATE = 32, 16, 64
GDN_HEADS, GDN_DH, GDN_CHUNK, GDN_CONV = 8, 128, 64, 4
XA_HEADS, XA_DH = 4, 256
FFN_CONV = 3
SCAN_ROWS = 256

ADAM_LR, ADAM_B1, ADAM_B2, ADAM_EPS, ADAM_WD, ADAM_STEP = 0.001, 0.9, 0.999, 1e-08, 0.01, 10


def _cp(*sem):
    return pltpu.CompilerParams(dimension_semantics=sem if sem else None, vmem_limit_bytes=VMEM_LIMIT)


def _tile(n, cap):
    if n <= cap:
        return n
    best = None
    for t in range(LANES, cap + 1, LANES):
        if n % t == 0:
            best = t
    assert best is not None, n
    return best


def _dot(a, b, ca=1, cb=0, precision=None):
    return lax.dot_general(a, b, (((ca,), (cb,)), ((), ())), precision=precision, preferred_element_type=F32)


def _mxu(a, b, ca=1, cb=0):
    return _dot(a.astype(BF16), b.astype(BF16), ca, cb)


def _sigmoid(x):
    return 0.5 * jnp.tanh(0.5 * x) + 0.5


def _shift_down(x, k):
    r = pltpu.roll(x, k, 0)
    row = lax.broadcasted_iota(jnp.int32, (8,) + x.shape[1:], 0)
    return jnp.concatenate([jnp.where(row >= k, r[:8], 0.0), r[8:]], axis=0)


def _shift_up(x, k):
    n = x.shape[0]
    r = pltpu.roll(x, n - k, 0)
    row = lax.broadcasted_iota(jnp.int32, (8,) + x.shape[1:], 0)
    return jnp.concatenate([r[:n - 8], jnp.where(row < 8 - k, r[n - 8:], 0.0)], axis=0)


def _mesh_pos():
    return lax.axis_index("x"), lax.axis_index("y"), lax.axis_index("c")


def _slot(px, py, pc):
    return 4 * px + 2 * py + pc


def _all_peers(x, y, c):
    flips = [(fx, fy, fc) for fx in (0, 1) for fy in (0, 1) for fc in (0, 1)][1:]
    return [(1 - x if fx else x, 1 - y if fy else y, 1 - c if fc else c) for fx, fy, fc in flips]


_HBM = pl.BlockSpec(memory_space=pltpu.HBM)
_SEM = pl.BlockSpec(memory_space=pltpu.SEMAPHORE)
N_PEERS = N_DEV - 1


def _push_copies(srcs, lands, send_sems, recv_sems, start):
    x, y, c = _mesh_pos()
    me = _slot(x, y, c)
    out = []
    for k, to in enumerate(_all_peers(x, y, c)):
        for a in range(len(lands)):
            src = srcs[a].at[_slot(*to)] if a < len(srcs) else lands[a].at[me]
            dst = lands[a].at[me if start else _slot(*to)]
            out.append(pltpu.make_async_remote_copy(
                src_ref=src, dst_ref=dst, send_sem=send_sems.at[a * N_PEERS + k], recv_sem=recv_sems.at[a * N_PEERS + k],
                device_id=to, device_id_type=pl.DeviceIdType.MESH))
    return out


def _into_slot(x, dtype, me, name):
    r, c = x.shape
    cap = max(16, 512 * 1024 // c)
    tr = max(t for t in range(16, min(r, cap) + 1, 16) if r % t == 0) if r % 16 == 0 else r

    def body(me_ref, x_ref, o_ref):
        o_ref[...] = x_ref[...].astype(dtype)

    return pl.pallas_call(
        body, name=name, out_shape=jax.ShapeDtypeStruct((N_DEV, r, c), dtype),
        grid_spec=pltpu.PrefetchScalarGridSpec(
            num_scalar_prefetch=1, grid=(r // tr,),
            in_specs=[pl.BlockSpec((tr, c), lambda i, me_ref: (i, 0))],
            out_specs=pl.BlockSpec((None, tr, c), lambda i, me_ref: (me_ref[0], i, 0))),
        compiler_params=_cp("parallel"),
    )(me.reshape(1).astype(jnp.int32), x)


def _push_start(scatter, gather_lands, name):
    ns, n = len(scatter), len(scatter) + len(gather_lands)
    lands = [lax.empty(a.shape, a.dtype) for a in scatter] + list(gather_lands)

    def body(*refs):
        srcs, zones = refs[:ns], refs[ns:ns + n]
        for cp in _push_copies(srcs, zones, refs[ns + n], refs[ns + n + 1], True):
            cp.start()
        refs[-1][...] = jnp.zeros((8, LANES), F32)

    hbm_in = [pltpu.with_memory_space_constraint(a, pltpu.HBM) for a in list(scatter) + lands]
    res = pl.pallas_call(
        body, name=name,
        out_shape=(pltpu.SemaphoreType.DMA((n * N_PEERS,)), pltpu.SemaphoreType.DMA((n * N_PEERS,)))
        + tuple(pltpu.HBM(a.shape, a.dtype) for a in list(scatter) + lands)
        + (jax.ShapeDtypeStruct((8, LANES), F32),),
        in_specs=[_HBM] * (ns + n),
        out_specs=(_SEM, _SEM) + (_HBM,) * (ns + n) + (pl.BlockSpec(memory_space=pltpu.VMEM),),
        input_output_aliases={i: 2 + i for i in range(ns + n)},
        compiler_params=pltpu.CompilerParams(has_side_effects=pltpu.SideEffectType.DATAFLOW_SIDE_EFFECTING),
    )(*hbm_in)
    return (res[0], res[1], res[2:2 + ns], res[2 + ns:2 + ns + n]), res[-1]


def _push_wait(handle, after, name):
    send_sems, recv_sems, srcs, lands = handle
    ns, n = len(srcs), len(lands)

    def body(*refs):
        for cp in _push_copies(refs[:ns], refs[ns:ns + n], refs[ns + n], refs[ns + n + 1], False):
            cp.wait_send()
            cp.wait_recv()

    res = pl.pallas_call(
        body, name=name,
        out_shape=tuple(pltpu.HBM(a.shape, a.dtype) for a in list(srcs) + list(lands)),
        in_specs=[_HBM] * (ns + n) + [_SEM, _SEM, pl.BlockSpec(memory_space=pl.ANY)],
        out_specs=(_HBM,) * (ns + n),
        input_output_aliases={i: i for i in range(ns + n)},
        compiler_params=pltpu.CompilerParams(has_side_effects=pltpu.SideEffectType.DATAFLOW_SIDE_EFFECTING),
    )(*srcs, *lands, send_sems, recv_sems, after)
    return res[ns:]


def _mm(a, b, *, ta=False, tb=False, out_dtype=F32, res=None, pin=None, name="mm"):
    m, k = (a.shape[1], a.shape[0]) if ta else a.shape
    n = b.shape[0] if tb else b.shape[1]
    assert k == (b.shape[1] if tb else b.shape[0]), (a.shape, b.shape, ta, tb)
    tm, tn, tk = _tile(m, 1408), _tile(n, 1536), _tile(k, 1408)
    nk = k // tk
    has_res = res is not None
    n_in = 2 + has_res + (pin is not None)

    def body(*refs):
        a_ref, b_ref = refs[:2]
        r_ref = refs[2] if has_res else None
        o_ref = refs[n_in]
        part = _mxu(a_ref[...], b_ref[...], 0 if ta else 1, 1 if tb else 0)

        def finish(r):
            if has_res:
                r = r + r_ref[...].astype(F32)
            o_ref[...] = r.astype(out_dtype)

        if nk == 1:
            finish(part)
            return
        acc = refs[-1]
        kk = pl.program_id(2)

        @pl.when(kk == 0)
        def _():
            acc[...] = part

        @pl.when(kk > 0)
        def _():
            acc[...] += part

        @pl.when(kk == nk - 1)
        def _():
            finish(acc[...])

    a_spec = pl.BlockSpec((tk, tm), lambda i, j, kk: (kk, i)) if ta else pl.BlockSpec((tm, tk), lambda i, j, kk: (i, kk))
    b_spec = pl.BlockSpec((tn, tk), lambda i, j, kk: (j, kk)) if tb else pl.BlockSpec((tk, tn), lambda i, j, kk: (kk, j))
    o_spec = pl.BlockSpec((tm, tn), lambda i, j, kk: (i, j))
    in_specs = [a_spec, b_spec] + ([o_spec] if has_res else [])
    args = (a, b) + ((res,) if has_res else ())
    if pin is not None:
        in_specs.append(pl.BlockSpec(pin.shape, lambda i, j, kk: (0, 0)))
        args += (pin,)
    return pl.pallas_call(
        body, name=name, grid=(m // tm, n // tn, nk), in_specs=in_specs, out_specs=o_spec,
        out_shape=jax.ShapeDtypeStruct((m, n), out_dtype),
        scratch_shapes=[pltpu.VMEM((tm, tn), F32)] if nk > 1 else [],
        compiler_params=_cp("parallel", "parallel", "arbitrary"),
    )(*args)


def _mm_norm_bwd(dy, w, x, g, dres, name, pin=None):
    dys = list(dy) if isinstance(dy, (list, tuple)) else [dy]
    nq = len(dys)
    s, kq = dys[0].shape
    d = w.shape[0]
    tm, tk = min(1024 if nq == 1 else 512, s), _tile(kq, 1408)
    per = kq // tk
    nk = nq * per
    n_in = nq + 4 + (pin is not None)

    def body(*refs):
        w_ref, x_ref, g_ref, dres_ref = refs[nq:nq + 4]
        dx_ref, dg_ref = refs[n_in], refs[n_in + 1]
        i, kk = pl.program_id(0), pl.program_id(1)

        @pl.when((i == 0) & (kk == 0))
        def _():
            dg_ref[...] = jnp.zeros_like(dg_ref)

        def finish(dh):
            xv = x_ref[...]
            r = lax.rsqrt(jnp.mean(xv * xv, axis=-1, keepdims=True) + EPS)
            xn = xv * r
            dg_ref[...] += jnp.sum(dh * xn, axis=0, keepdims=True)
            dhg = dh * g_ref[...]
            dx_ref[...] = dres_ref[...] + r * (dhg - xn * jnp.mean(dhg * xn, axis=-1, keepdims=True))

        if nk == 1:
            finish(_mxu(refs[0][...], w_ref[...], 1, 1))
            return
        acc = refs[-1]
        for q in range(nq):
            @pl.when((kk >= q * per) & (kk < (q + 1) * per))
            def _(q=q):
                part = _mxu(refs[q][...], w_ref[...], 1, 1)

                @pl.when(kk == 0)
                def _():
                    acc[...] = part

                @pl.when(kk > 0)
                def _():
                    acc[...] += part

        @pl.when(kk == nk - 1)
        def _():
            finish(acc[...])

    row = pl.BlockSpec((tm, d), lambda i, kk: (i, 0))
    vec = pl.BlockSpec((1, d), lambda i, kk: (0, 0))
    in_specs = [pl.BlockSpec((tm, tk), lambda i, kk, q=q: (i, jnp.clip(kk - q * per, 0, per - 1))) for q in range(nq)]
    in_specs += [pl.BlockSpec((d, tk), lambda i, kk: (0, kk)), row, vec, row]
    args = (*dys, w, x, g.reshape(1, d), dres)
    if pin is not None:
        in_specs.append(pl.BlockSpec(pin.shape, lambda i, kk: (0, 0)))
        args += (pin,)
    return pl.pallas_call(
        body, name=name, grid=(s // tm, nk), in_specs=in_specs, out_specs=[row, vec],
        out_shape=[jax.ShapeDtypeStruct((s, d), F32), jax.ShapeDtypeStruct((1, d), F32)],
        scratch_shapes=[pltpu.VMEM((tm, d), F32)] if nk > 1 else [],
        compiler_params=_cp("arbitrary", "arbitrary"),
    )(*args)


def _norm_fwd(x, g, name):
    s, d = x.shape
    tr = min(512, s)

    def body(x_ref, g_ref, o_ref):
        xv = x_ref[...]
        r = lax.rsqrt(jnp.mean(xv * xv, axis=-1, keepdims=True) + EPS)
        o_ref[...] = (xv * r * g_ref[...]).astype(BF16)

    row = pl.BlockSpec((tr, d), lambda i: (i, 0))
    return pl.pallas_call(
        body, name=name, grid=(s // tr,), in_specs=[row, pl.BlockSpec((1, d), lambda i: (0, 0))],
        out_specs=row, out_shape=jax.ShapeDtypeStruct((s, d), BF16), compiler_params=_cp("parallel"),
    )(x, g.reshape(1, d))


def _norm_bwd(x, g, dh, dres, name):
    s, d = x.shape
    tr = min(512, s)

    def body(x_ref, g_ref, dh_ref, dres_ref, dx_ref, dg_ref):
        @pl.when(pl.program_id(0) == 0)
        def _():
            dg_ref[...] = jnp.zeros_like(dg_ref)

        xv = x_ref[...]
        r = lax.rsqrt(jnp.mean(xv * xv, axis=-1, keepdims=True) + EPS)
        xn = xv * r
        dhv = dh_ref[...].astype(F32)
        dg_ref[...] += jnp.sum(dhv * xn, axis=0, keepdims=True)
        dhg = dhv * g_ref[...]
        dx_ref[...] = dres_ref[...] + r * (dhg - xn * jnp.mean(dhg * xn, axis=-1, keepdims=True))

    row = pl.BlockSpec((tr, d), lambda i: (i, 0))
    vec = pl.BlockSpec((1, d), lambda i: (0, 0))
    return pl.pallas_call(
        body, name=name, grid=(s // tr,), in_specs=[row, vec, row, row], out_specs=[row, vec],
        out_shape=[jax.ShapeDtypeStruct((s, d), F32), jax.ShapeDtypeStruct((1, d), F32)],
        compiler_params=_cp("arbitrary"),
    )(x, g.reshape(1, d), dh, dres)


def _loss_head(x, g, tgt, name):
    s, d = x.shape
    tr = min(512, s)

    def body(x_ref, g_ref, t_ref, l_ref, dx_ref, dg_ref):
        @pl.when(pl.program_id(0) == 0)
        def _():
            dg_ref[...] = jnp.zeros_like(dg_ref)
            l_ref[...] = jnp.zeros_like(l_ref)

        xv = x_ref[...]
        r = lax.rsqrt(jnp.mean(xv * xv, axis=-1, keepdims=True) + EPS)
        xn = xv * r
        err = xn * g_ref[...] - t_ref[...]
        part = 0.5 * jnp.sum(jnp.mean(err * err, axis=-1, keepdims=True), axis=0, keepdims=True)
        l_ref[...] += jnp.broadcast_to(part, l_ref.shape)
        dy = err * (1.0 / d)
        dg_ref[...] += jnp.sum(dy * xn, axis=0, keepdims=True)
        dyg = dy * g_ref[...]
        dx_ref[...] = r * (dyg - xn * jnp.mean(dyg * xn, axis=-1, keepdims=True))

    row = pl.BlockSpec((tr, d), lambda i: (i, 0))
    vec = pl.BlockSpec((1, d), lambda i: (0, 0))
    return pl.pallas_call(
        body, name=name, grid=(s // tr,), in_specs=[row, vec, row],
        out_specs=[pl.BlockSpec((1, LANES), lambda i: (0, 0)), row, vec],
        out_shape=[jax.ShapeDtypeStruct((1, LANES), F32), jax.ShapeDtypeStruct((s, d), F32),
                   jax.ShapeDtypeStruct((1, d), F32)],
        compiler_params=_cp("arbitrary"),
    )(x, g.reshape(1, d), tgt)


def _sum_slots(landed_slot, own):
    me = _slot(*_mesh_pos())
    mine = own.astype(F32)
    g = jnp.where(me == 0, mine, landed_slot(0).astype(F32))
    for i in range(1, N_DEV):
        g = g + jnp.where(me == i, mine, landed_slot(i).astype(F32))
    return g


def _adam_update(g, w, m, v):
    mm = ADAM_B1 * m + (1.0 - ADAM_B1) * g
    vv = ADAM_B2 * v + (1.0 - ADAM_B2) * (g * g)
    m_hat = mm / (1.0 - ADAM_B1 ** ADAM_STEP)
    v_hat = vv / (1.0 - ADAM_B2 ** ADAM_STEP)
    return g, -ADAM_LR * (m_hat / (jnp.sqrt(v_hat) + ADAM_EPS) + ADAM_WD * w), mm, vv


def _adamw_rows(landed, own, ws, ms, vs, name):
    k = len(ws)
    sizes = [w.shape[1] for w in ws]

    def body(*refs):
        p_ref, o_ref = refs[:2]
        w_refs, m_refs, v_refs = refs[2:2 + k], refs[2 + k:2 + 2 * k], refs[2 + 2 * k:2 + 3 * k]
        outs = refs[2 + 3 * k:]
        for i, n in enumerate(sizes):
            g = _sum_slots(lambda s: p_ref[s, i:i + 1, :n], o_ref[i:i + 1, :n])
            res = _adam_update(g, w_refs[i][...], m_refs[i][...], v_refs[i][...])
            for j in range(4):
                outs[j * k + i][...] = res[j]

    return pl.pallas_call(
        body, name=name, out_shape=[jax.ShapeDtypeStruct((1, n), F32) for _ in range(4) for n in sizes],
    )(landed, own, *ws, *ms, *vs)


def _adamw(landed, own, w, m, v, name):
    r, c = w.shape
    cap = max(8, 256 * 1024 // c)
    tr = max(t for t in range(8, min(r, cap) + 1, 8) if r % t == 0) if r % 8 == 0 else r
    gathered = own is None

    def body(*refs):
        p_ref = refs[0]
        w_ref, m_ref, v_ref, g_ref, d_ref, nm_ref, nv_ref = refs[1 if gathered else 2:]
        if gathered:
            g = p_ref[0].astype(F32)
            for i in range(1, N_DEV):
                g = g + p_ref[i].astype(F32)
        else:
            g = _sum_slots(lambda i: p_ref[i], refs[1][...])
        g_ref[...], d_ref[...], nm_ref[...], nv_ref[...] = _adam_update(g, w_ref[...], m_ref[...], v_ref[...])

    blk = pl.BlockSpec((tr, c), lambda i: (i, 0))
    n_blk = 3 if gathered else 4
    return pl.pallas_call(
        body, name=name, grid=(r // tr,),
        in_specs=[pl.BlockSpec((N_DEV, tr, c), lambda i: (0, i, 0))] + [blk] * n_blk,
        out_specs=[blk] * 4, out_shape=[jax.ShapeDtypeStruct((r, c), F32)] * 4,
        compiler_params=_cp("parallel"),
    )(*((landed,) if gathered else (landed, own)), w, m, v)


def _conv_taps(x, kw):
    return [_shift_down(x, kw - 1 - j) for j in range(kw - 1)] + [x]


def _conv_fwd(taps, w_ref):
    acc = w_ref[0:1, :] * taps[0]
    for j in range(1, len(taps)):
        acc = acc + w_ref[j:j + 1, :] * taps[j]
    return acc


def _conv_bwd(taps, dy, w_ref, dw_ref):
    kw = len(taps)
    dx = w_ref[kw - 1:kw, :] * dy
    for j in range(kw):
        dw_ref[j:j + 1, :] = jnp.sum(dy * taps[j], axis=0, keepdims=True)
        if j < kw - 1:
            dx = dx + w_ref[j:j + 1, :] * _shift_up(dy, kw - 1 - j)
    return dx


def _ffn_act_fwd(pre, cw, name):
    s, f2 = pre.shape
    nt = f2 // 2 // LANES

    def body(pu_ref, pg_ref, wu_ref, wg_ref, o_ref):
        up = _conv_fwd(_conv_taps(pu_ref[...].astype(F32), FFN_CONV), wu_ref)
        gate = _conv_fwd(_conv_taps(pg_ref[...].astype(F32), FFN_CONV), wg_ref)
        o_ref[...] = (gate * _sigmoid(gate) * up).astype(BF16)

    def col(rows, off):
        return pl.BlockSpec((rows, LANES), lambda j: (0, j + off))

    return pl.pallas_call(
        body, name=name, grid=(nt,),
        in_specs=[col(s, 0), col(s, nt), col(FFN_CONV, 0), col(FFN_CONV, nt)], out_specs=col(s, 0),
        out_shape=jax.ShapeDtypeStruct((s, f2 // 2), BF16), compiler_params=_cp("parallel"),
    )(pre, pre, cw, cw)


def _ffn_act_bwd(pre, cw, dact, name):
    s, f2 = pre.shape
    f = f2 // 2
    nt = f // LANES

    def body(pu_ref, pg_ref, wu_ref, wg_ref, da_ref, dpu_ref, dpg_ref, dwu_ref, dwg_ref):
        pu, pg = pu_ref[...].astype(F32), pg_ref[...].astype(F32)
        tu, tg = _conv_taps(pu, FFN_CONV), _conv_taps(pg, FFN_CONV)
        up = _conv_fwd(tu, wu_ref)
        gate = _conv_fwd(tg, wg_ref)
        sg = _sigmoid(gate)
        da = da_ref[...].astype(F32)
        dup = da * gate * sg
        dgate = da * up * (sg * (1.0 + gate * (1.0 - sg)))
        dpu_ref[...] = _conv_bwd(tu, dup, wu_ref, dwu_ref).astype(BF16)
        dpg_ref[...] = _conv_bwd(tg, dgate, wg_ref, dwg_ref).astype(BF16)

    def col(rows, off):
        return pl.BlockSpec((rows, LANES), lambda j: (0, j + off))

    return pl.pallas_call(
        body, name=name, grid=(nt,),
        in_specs=[col(s, 0), col(s, nt), col(FFN_CONV, 0), col(FFN_CONV, nt), col(s, 0)],
        out_specs=[col(s, 0), col(s, 0), col(FFN_CONV, 0), col(FFN_CONV, 0)],
        out_shape=[jax.ShapeDtypeStruct((s, f), BF16), jax.ShapeDtypeStruct((s, f), BF16),
                   jax.ShapeDtypeStruct((FFN_CONV, f), F32), jax.ShapeDtypeStruct((FFN_CONV, f), F32)],
        compiler_params=_cp("parallel"),
    )(pre, pre, cw, cw, dact)


def _xa_probs(qh, kh):
    sc = _mxu(qh, kh, 1, 1) * (XA_DH ** -0.5)
    e = jnp.exp(sc - jnp.max(sc, axis=-1, keepdims=True))
    return e / jnp.sum(e, axis=-1, keepdims=True)


def _xattn_fwd(q, kv, name):
    s, d = q.shape
    m = kv.shape[0]
    tr = min(512, s)

    def body(q_ref, kv_ref, o_ref):
        for h in range(XA_HEADS):
            lo, hi = h * XA_DH, (h + 1) * XA_DH
            p = _xa_probs(q_ref[:, lo:hi], kv_ref[:, lo:hi])
            o_ref[:, lo:hi] = _mxu(p, kv_ref[:, d + lo:d + hi]).astype(BF16)

    row = pl.BlockSpec((tr, d), lambda i: (i, 0))
    return pl.pallas_call(
        body, name=name, grid=(s // tr,), in_specs=[row, pl.BlockSpec((m, 2 * d), lambda i: (0, 0))],
        out_specs=row, out_shape=jax.ShapeDtypeStruct((s, d), BF16), compiler_params=_cp("parallel"),
    )(q, kv)


def _xattn_bwd(q, kv, do, name):
    s, d = q.shape
    m = kv.shape[0]
    tr = min(512, s)

    def body(q_ref, kv_ref, do_ref, dq_ref, dkv_ref):
        @pl.when(pl.program_id(0) == 0)
        def _():
            dkv_ref[...] = jnp.zeros_like(dkv_ref)

        for h in range(XA_HEADS):
            lo, hi = h * XA_DH, (h + 1) * XA_DH
            qh, kh, vh = q_ref[:, lo:hi], kv_ref[:, lo:hi], kv_ref[:, d + lo:d + hi]
            doh = do_ref[:, lo:hi]
            p = _xa_probs(qh, kh)
            dp = _mxu(doh, vh, 1, 1)
            ds = p * (dp - jnp.sum(p * dp, axis=-1, keepdims=True)) * (XA_DH ** -0.5)
            dq_ref[:, lo:hi] = _mxu(ds, kh).astype(BF16)
            dkv_ref[:, lo:hi] += _mxu(ds, qh, 0, 0)
            dkv_ref[:, d + lo:d + hi] += _mxu(p, doh, 0, 0)

    row = pl.BlockSpec((tr, d), lambda i: (i, 0))
    full = pl.BlockSpec((m, 2 * d), lambda i: (0, 0))
    return pl.pallas_call(
        body, name=name, grid=(s // tr,), in_specs=[row, full, row], out_specs=[row, full],
        out_shape=[jax.ShapeDtypeStruct((s, d), BF16), jax.ShapeDtypeStruct((m, 2 * d), F32)],
        compiler_params=_cp("arbitrary"),
    )(q, kv, do)


def _ret_tables():
    c = RET_CHUNK
    lg = np.log1p(-np.exp2(-5.0 - np.arange(RET_HEADS, dtype=np.float32))).astype(np.float32)
    idx = np.arange(c, dtype=np.float32)
    diff = idx[:, None] - idx[None, :]
    intra = np.where(diff >= 0, np.exp(lg[:, None, None] * np.where(diff >= 0, diff, 0.0)), 0.0)
    rk = np.broadcast_to(np.exp(lg[:, None] * (c - 1 - idx))[:, :, None], (RET_HEADS, c, LANES))
    rq = np.broadcast_to(np.exp(lg[:, None] * (idx + 1))[:, :, None], (RET_HEADS, c, LANES))
    return jnp.asarray(np.stack([intra, rk, rq], axis=1).astype(np.float32))


def _rope_tables(s):
    half = RET_DH // 2
    inv = jnp.exp(-math.log(10000.0) * jnp.arange(half, dtype=F32) / half)
    ang = jnp.arange(s, dtype=F32)[:, None] * inv[None, :]
    cos, sin = jnp.cos(ang), jnp.sin(ang)
    return jnp.concatenate([cos, cos], axis=1), jnp.concatenate([-sin, sin], axis=1)


def _ret_specs(n_of):
    c, w = RET_CHUNK, RET_HEADS * RET_DH

    def part(off):
        return pl.BlockSpec((c, w), lambda n: (n_of(n), off))

    pos = pl.BlockSpec((c, RET_DH), lambda n: (n_of(n), 0))
    gain = pl.BlockSpec((1, w), lambda n: (0, 0))
    tab = pl.BlockSpec((RET_HEADS, 3, c, LANES), lambda n: (0, 0, 0, 0))
    st = pl.BlockSpec((RET_HEADS, None, RET_DH, RET_DH), lambda n: (0, n_of(n), 0, 0))
    return part, pos, gain, tab, st


def _rheads(x):
    return jnp.stack([x[:, h * RET_DH:(h + 1) * RET_DH] for h in range(RET_HEADS)], axis=0)


def _runheads(x):
    return jnp.concatenate([x[h] for h in range(RET_HEADS)], axis=1)


def _rope(x, cos, sin):
    return x * cos + pltpu.roll(x, RET_DH // 2, 2) * sin


def _ret_chunk(q_ref, k_ref, v_ref, cos_ref, sin_ref, tab_ref, prev):
    cos, sin = cos_ref[...], sin_ref[...]
    q = _rope(_rheads(q_ref[...]), cos, sin)
    k = _rope(_rheads(k_ref[...]), cos, sin) * (RET_DH ** -0.5)
    v = _rheads(v_ref[...])
    scores = _bmxu(q, k, 2, 2) * tab_ref[:, 0]
    qdec = q * tab_ref[:, 2]
    kdec = k * tab_ref[:, 1]
    o = _bmxu(scores, v) + _bmxu(qdec, prev)
    return q, k, v, scores, qdec, kdec, o


def _ret_fwd(proj, cos, sin, gain, name):
    s = proj.shape[0]
    c = RET_CHUNK
    nc = s // c
    part, pos, gvec, tab, st = _ret_specs(lambda n: n)

    def body(q_ref, k_ref, v_ref, g_ref, cos_ref, sin_ref, rn_ref, tab_ref, o_ref, st_ref, state):
        @pl.when(pl.program_id(0) == 0)
        def _():
            state[...] = jnp.zeros_like(state)

        prev = state[...]
        st_ref[...] = prev
        _, _, v, _, _, kdec, o = _ret_chunk(q_ref, k_ref, v_ref, cos_ref, sin_ref, tab_ref, prev)
        state[...] = prev * tab_ref[:, 2, c - 1:c, :] + _bmxu(kdec, v, 1, 1)
        r = lax.rsqrt(jnp.mean(o * o, axis=-1, keepdims=True) + EPS)
        gate = g_ref[...]
        o_ref[...] = (_runheads(o * r) * rn_ref[...] * (gate * _sigmoid(gate))).astype(BF16)

    return pl.pallas_call(
        body, name=name, grid=(nc,),
        in_specs=[part(0), part(1), part(2), part(3), pos, pos, gvec, tab],
        out_specs=[part(0), st],
        out_shape=[jax.ShapeDtypeStruct((s, RET_HEADS * RET_DH), BF16),
                   jax.ShapeDtypeStruct((RET_HEADS, nc, RET_DH, RET_DH), F32)],
        scratch_shapes=[pltpu.VMEM((RET_HEADS, RET_DH, RET_DH), F32)],
        compiler_params=_cp("arbitrary"),
    )(proj, proj, proj, proj, cos, sin, gain.reshape(1, -1), _ret_tables())


def _ret_bwd(proj, cos, sin, gain, states, dmerged, name):
    s = proj.shape[0]
    c = RET_CHUNK
    nc = s // c
    width = RET_HEADS * RET_DH
    part, pos, gvec, tab, st = _ret_specs(lambda n: nc - 1 - n)

    def body(q_ref, k_ref, v_ref, g_ref, cos_ref, sin_ref, rn_ref, tab_ref, st_ref, do_ref,
             dp_ref, drn_ref, carry):
        @pl.when(pl.program_id(0) == 0)
        def _():
            carry[...] = jnp.zeros_like(carry)
            drn_ref[...] = jnp.zeros_like(drn_ref)

        prev = st_ref[...]
        q, k, v, scores, qdec, kdec, o = _ret_chunk(q_ref, k_ref, v_ref, cos_ref, sin_ref, tab_ref, prev)
        r = lax.rsqrt(jnp.mean(o * o, axis=-1, keepdims=True) + EPS)
        on = o * r
        on2 = _runheads(on)
        gate = g_ref[...]
        sg = _sigmoid(gate)
        sil = gate * sg
        dout = do_ref[...]
        rn = rn_ref[...]
        dp_ref[:, 3 * width:] = (dout * on2 * rn * (sg * (1.0 + gate * (1.0 - sg)))).astype(BF16)
        drn_ref[...] += jnp.sum(dout * on2 * sil, axis=0, keepdims=True)
        don = _rheads(dout * rn * sil)
        do = r * (don - on * jnp.mean(don * on, axis=-1, keepdims=True))
        dc = carry[...]
        dsc = _bmxu(do, v, 2, 2) * tab_ref[:, 0]
        dq = _bmxu(dsc, k) + _bmxu(do, prev, 2, 2) * tab_ref[:, 2]
        dk = _bmxu(dsc, q, 1, 1) + _bmxu(v, dc, 2, 2) * tab_ref[:, 1]
        dv = _bmxu(scores, do, 1, 1) + _bmxu(kdec, dc)
        carry[...] = _bmxu(qdec, do, 1, 1) + dc * tab_ref[:, 2, c - 1:c, :]
        cos, sin = cos_ref[...], sin_ref[...]
        dk = dk * (RET_DH ** -0.5)
        dp_ref[:, :width] = _runheads(dq * cos + pltpu.roll(dq * sin, RET_DH // 2, 2)).astype(BF16)
        dp_ref[:, width:2 * width] = _runheads(dk * cos + pltpu.roll(dk * sin, RET_DH // 2, 2)).astype(BF16)
        dp_ref[:, 2 * width:3 * width] = _runheads(dv).astype(BF16)

    return pl.pallas_call(
        body, name=name, grid=(nc,),
        in_specs=[part(0), part(1), part(2), part(3), pos, pos, gvec, tab, st, part(0)],
        out_specs=[pl.BlockSpec((c, 4 * width), lambda n: (nc - 1 - n, 0)), gvec],
        out_shape=[jax.ShapeDtypeStruct(proj.shape, BF16), jax.ShapeDtypeStruct((1, width), F32)],
        scratch_shapes=[pltpu.VMEM((RET_HEADS, RET_DH, RET_DH), F32)],
        compiler_params=_cp("arbitrary"),
    )(proj, proj, proj, proj, cos, sin, gain.reshape(1, -1), _ret_tables(), states, dmerged)


S5_TILE = 512


def _cmul_add(xr, xi, ar, ai, yr, yi):
    return xr + ar * yr - ai * yi, xi + ar * yi + ai * yr


def _s5_pow_tables(a_il, name):
    r = SCAN_ROWS
    t = S5_TILE
    w2 = a_il.shape[1]

    def body(a_ref, up_ref, dn_ref):
        for j in range(w2 // (2 * t)):
            re, im = pl.ds(2 * t * j, t), pl.ds(2 * t * j + t, t)
            up_ref[0:1, re] = a_ref[:, re]
            up_ref[0:1, im] = a_ref[:, im]
            dn_ref[r - 1:r, re] = a_ref[:, re]
            dn_ref[r - 1:r, im] = -a_ref[:, im]
            n = 1
            while n < r:
                lr, li = up_ref[n - 1:n, re], up_ref[n - 1:n, im]
                xr, xi = up_ref[0:n, re], up_ref[0:n, im]
                up_ref[n:2 * n, re] = xr * lr - xi * li
                up_ref[n:2 * n, im] = xr * li + xi * lr
                yr, yi = dn_ref[r - n:r, re], dn_ref[r - n:r, im]
                dn_ref[r - 2 * n:r - n, re] = yr * lr + yi * li
                dn_ref[r - 2 * n:r - n, im] = yi * lr - yr * li
                n *= 2

    return pl.pallas_call(
        body, name=name, out_shape=[jax.ShapeDtypeStruct((r, w2), F32)] * 2, compiler_params=_cp(),
    )(a_il)


def _s5_scan_fwd(bu, apow, name):
    s, w2 = bu.shape
    r = SCAN_ROWS
    t = S5_TILE
    steps = r.bit_length() - 1

    def body(b_ref, p_ref, o_ref, cr, ci):
        @pl.when(pl.program_id(1) == 0)
        def _():
            cr[...] = jnp.zeros_like(cr)
            ci[...] = jnp.zeros_like(ci)

        xr, xi = b_ref[:, :t], b_ref[:, t:]
        for k in range(steps):
            sh = 1 << k
            xr, xi = _cmul_add(xr, xi, p_ref[sh - 1:sh, :t], p_ref[sh - 1:sh, t:],
                               _shift_down(xr, sh), _shift_down(xi, sh))
        xr, xi = _cmul_add(xr, xi, p_ref[:, :t], p_ref[:, t:], cr[...], ci[...])
        o_ref[:, :t] = xr
        o_ref[:, t:] = xi
        cr[...] = xr[r - 1:r, :]
        ci[...] = xi[r - 1:r, :]

    blk = pl.BlockSpec((r, 2 * t), lambda j, i: (i, j))
    return pl.pallas_call(
        body, name=name, grid=(w2 // (2 * t), s // r),
        in_specs=[blk, pl.BlockSpec((r, 2 * t), lambda j, i: (0, j))], out_specs=blk,
        out_shape=jax.ShapeDtypeStruct((s, w2), F32),
        scratch_shapes=[pltpu.VMEM((1, t), F32), pltpu.VMEM((1, t), F32)],
        compiler_params=_cp("parallel", "arbitrary"),
    )(bu, apow)


def _s5_scan_bwd(dst, apow_rev, st, name):
    s, w2 = dst.shape
    r = SCAN_ROWS
    t = S5_TILE
    nb = s // r
    steps = r.bit_length() - 1

    def body(d_ref, p_ref, s_ref, sp_ref, g_ref, da_ref, cr, ci):
        i = pl.program_id(1)

        @pl.when(i == 0)
        def _():
            cr[...] = jnp.zeros_like(cr)
            ci[...] = jnp.zeros_like(ci)
            da_ref[...] = jnp.zeros_like(da_ref)

        xr, xi = d_ref[:, :t], d_ref[:, t:]
        for k in range(steps):
            sh = 1 << k
            xr, xi = _cmul_add(xr, xi, p_ref[r - sh:r - sh + 1, :t], p_ref[r - sh:r - sh + 1, t:],
                               _shift_up(xr, sh), _shift_up(xi, sh))
        xr, xi = _cmul_add(xr, xi, p_ref[:, :t], p_ref[:, t:], cr[...], ci[...])
        g_ref[:, :t] = xr.astype(BF16)
        g_ref[:, t:] = xi.astype(BF16)
        cr[...] = xr[0:1, :]
        ci[...] = xi[0:1, :]
        first = i == nb - 1
        row = lax.broadcasted_iota(jnp.int32, (r, t), 0)
        last_r = jnp.where(first, 0.0, sp_ref[7:8, :t])
        last_i = jnp.where(first, 0.0, sp_ref[7:8, t:])
        pr = jnp.where(row == 0, last_r, pltpu.roll(s_ref[:, :t], 1, 0))
        pi = jnp.where(row == 0, last_i, pltpu.roll(s_ref[:, t:], 1, 0))
        da_ref[:, :t] += jnp.sum(xr * pr + xi * pi, axis=0, keepdims=True)
        da_ref[:, t:] += jnp.sum(xi * pr - xr * pi, axis=0, keepdims=True)

    blk = pl.BlockSpec((r, 2 * t), lambda j, i: (nb - 1 - i, j))
    halo = pl.BlockSpec((8, 2 * t), lambda j, i: (jnp.maximum((nb - 1 - i) * (r // 8) - 1, 0), j))
    vec = pl.BlockSpec((1, 2 * t), lambda j, i: (0, j))
    return pl.pallas_call(
        body, name=name, grid=(w2 // (2 * t), nb),
        in_specs=[blk, pl.BlockSpec((r, 2 * t), lambda j, i: (0, j)), blk, halo], out_specs=[blk, vec],
        out_shape=[jax.ShapeDtypeStruct((s, w2), BF16), jax.ShapeDtypeStruct((1, w2), F32)],
        scratch_shapes=[pltpu.VMEM((1, t), F32), pltpu.VMEM((1, t), F32)],
        compiler_params=_cp("parallel", "arbitrary"),
    )(dst, apow_rev, st, st)


_GELU_C = math.sqrt(2.0 / math.pi)
_GELU_A = 0.044715


def _gelu(y):
    return 0.5 * y * (1.0 + jnp.tanh(_GELU_C * (y + _GELU_A * y * y * y)))


def _gelu_grad(y):
    th = jnp.tanh(_GELU_C * (y + _GELU_A * y * y * y))
    return 0.5 * (1.0 + th) + 0.5 * y * (1.0 - th * th) * _GELU_C * (1.0 + 3.0 * _GELU_A * y * y)


def _rows_shift(x, k, axis, up):
    n = x.shape[axis]
    idx = lax.broadcasted_iota(jnp.int32, x.shape, axis)
    if up:
        return jnp.where(idx < n - k, pltpu.roll(x, n - k, axis), 0.0)
    return jnp.where(idx >= k, pltpu.roll(x, k, axis), 0.0)


def _scan_block(xr, xi, pr, pi, cr, ci, rev):
    r, w = xr.shape
    nt = r // 8
    x3r, x3i = xr.reshape(nt, 8, w), xi.reshape(nt, 8, w)
    p3r, p3i = pr.reshape(nt, 8, w), pi.reshape(nt, 8, w)

    def power(rows):
        t = r - rows if rev else rows - 1
        return pr[t:t + 1, :], pi[t:t + 1, :]

    tile_row = lax.broadcasted_iota(jnp.int32, (8, w), 0)
    for sh in (1, 2, 4):
        ar, ai = power(sh)
        keep = tile_row < 8 - sh if rev else tile_row >= sh
        mr, mi = jnp.where(keep, ar, 0.0)[None], jnp.where(keep, ai, 0.0)[None]
        turn = 8 - sh if rev else sh
        x3r, x3i = _cmul_add(x3r, x3i, mr, mi, pltpu.roll(x3r, turn, 1), pltpu.roll(x3i, turn, 1))
    edge = 0 if rev else 7
    lr, li = x3r[:, edge, :], x3i[:, edge, :]
    sh = 1
    while sh < nt:
        ar, ai = power(8 * sh)
        lr, li = _cmul_add(lr, li, ar, ai, _rows_shift(lr, sh, 0, rev), _rows_shift(li, sh, 0, rev))
        sh *= 2
    tr_, ti_ = p3r[:, edge, :], p3i[:, edge, :]
    first = lax.broadcasted_iota(jnp.int32, (nt, w), 0) == (nt - 1 if rev else 0)
    wr = jnp.where(first, 1.0, _rows_shift(tr_, 1, 0, rev))
    wi = jnp.where(first, 0.0, _rows_shift(ti_, 1, 0, rev))
    er, ei = _cmul_add(_rows_shift(lr, 1, 0, rev), _rows_shift(li, 1, 0, rev), wr, wi, cr, ci)
    a8r, a8i = (p3r[nt - 1], p3i[nt - 1]) if rev else (p3r[0], p3i[0])
    x3r, x3i = _cmul_add(x3r, x3i, a8r[None], a8i[None], er[:, None, :], ei[:, None, :])
    outr, outi = x3r.reshape(r, w), x3i.reshape(r, w)
    last = 0 if rev else r - 1
    return outr, outi, outr[last:last + 1, :], outi[last:last + 1, :]


def _s5_tile_specs(n_of, r):
    t = S5_TILE
    ucol = 4 * RET_HEADS * RET_DH // LANES
    u = pl.BlockSpec((r, LANES), lambda j, i: (n_of(i), ucol + j))
    col = pl.BlockSpec((r, LANES), lambda j, i: (n_of(i), j))
    state = pl.BlockSpec((r, 2 * t), lambda j, i: (n_of(i), j))
    table = pl.BlockSpec((r, 2 * t), lambda j, i: (0, j))
    bbt = pl.BlockSpec((None, LANES, 2 * t), lambda j, i: (j, 0, 0))
    cct = pl.BlockSpec((None, 2 * t, LANES), lambda j, i: (j, 0, 0))
    vec = pl.BlockSpec((1, LANES), lambda j, i: (0, j))
    return u, col, state, table, bbt, cct, vec


def _s5_fwd(proj, bbt, cct, apow, dvec, name):
    s = proj.shape[0]
    r, t = SCAN_ROWS, S5_TILE
    w = S5_GROUPS * S5_GROUP
    u_s, col, state, table, bb_s, cc_s, vec = _s5_tile_specs(lambda i: i, r)

    def body(u_ref, bb_ref, cc_ref, p_ref, d_ref, st_ref, y_ref, g_ref, cr, ci):
        @pl.when(pl.program_id(1) == 0)
        def _():
            cr[...] = jnp.zeros_like(cr)
            ci[...] = jnp.zeros_like(ci)

        u = u_ref[...]
        bu = _mxu(u, bb_ref[...])
        xr, xi, cr[...], ci[...] = _scan_block(bu[:, :t], bu[:, t:], p_ref[:, :t], p_ref[:, t:], cr[...], ci[...], False)
        st_ref[:, :t] = xr
        st_ref[:, t:] = xi
        y = _mxu(xr, cc_ref[:t, :]) + _mxu(xi, cc_ref[t:, :]) + d_ref[...] * u
        y_ref[...] = y
        g_ref[...] = _gelu(y).astype(BF16)

    return pl.pallas_call(
        body, name=name, grid=(2 * S5_GROUPS * S5_STATE // (2 * t), s // r),
        in_specs=[u_s, bb_s, cc_s, table, vec], out_specs=[state, col, col],
        out_shape=[jax.ShapeDtypeStruct((s, 2 * S5_GROUPS * S5_STATE), F32), jax.ShapeDtypeStruct((s, w), F32),
                   jax.ShapeDtypeStruct((s, w), BF16)],
        scratch_shapes=[pltpu.VMEM((1, t), F32), pltpu.VMEM((1, t), F32)],
        compiler_params=_cp("parallel", "arbitrary"),
    )(proj, bbt, cct, apow, dvec)


def _s5_bwd(dg1, dg2, y, proj, st, bbt, cct, apow_rev, dvec, dproj, name):
    s = proj.shape[0]
    r, t = SCAN_ROWS, S5_TILE
    nb = s // r
    w = S5_GROUPS * S5_GROUP
    u_s, col, state, table, bb_s, cc_s, vec = _s5_tile_specs(lambda i: nb - 1 - i, r)
    halo = pl.BlockSpec((8, 2 * t), lambda j, i: (jnp.maximum((nb - 1 - i) * (r // 8) - 1, 0), j))
    acc = pl.BlockSpec((1, 2 * t), lambda j, i: (0, j))

    def body(a_ref, b_ref, y_ref, u_ref, s_ref, sp_ref, bb_ref, cc_ref, p_ref, d_ref, _,
             du_ref, da_ref, dbb_ref, dcc_ref, dd_ref, cr, ci):
        i = pl.program_id(1)

        @pl.when(i == 0)
        def _():
            cr[...] = jnp.zeros_like(cr)
            ci[...] = jnp.zeros_like(ci)
            da_ref[...] = jnp.zeros_like(da_ref)
            dbb_ref[...] = jnp.zeros_like(dbb_ref)
            dcc_ref[...] = jnp.zeros_like(dcc_ref)
            dd_ref[...] = jnp.zeros_like(dd_ref)

        u = u_ref[...]
        dy = (a_ref[...] + b_ref[...]) * _gelu_grad(y_ref[...])
        dd_ref[...] += jnp.sum(dy * u, axis=0, keepdims=True)
        sr, si = s_ref[:, :t], s_ref[:, t:]
        dcc_ref[:t, :] += _mxu(sr, dy, 0, 0)
        dcc_ref[t:, :] += _mxu(si, dy, 0, 0)
        xr, xi, cr[...], ci[...] = _scan_block(_mxu(dy, cc_ref[:t, :], 1, 1), _mxu(dy, cc_ref[t:, :], 1, 1),
                                               p_ref[:, :t], p_ref[:, t:], cr[...], ci[...], True)
        du_ref[...] = (dy * d_ref[...] + _mxu(xr, bb_ref[:, :t], 1, 1) + _mxu(xi, bb_ref[:, t:], 1, 1)).astype(BF16)
        dbb_ref[:, :t] += _mxu(u, xr, 0, 0)
        dbb_ref[:, t:] += _mxu(u, xi, 0, 0)
        first = i == nb - 1
        row = lax.broadcasted_iota(jnp.int32, (r, t), 0)
        pr = jnp.where(row == 0, jnp.where(first, 0.0, sp_ref[7:8, :t]), pltpu.roll(sr, 1, 0))
        pi = jnp.where(row == 0, jnp.where(first, 0.0, sp_ref[7:8, t:]), pltpu.roll(si, 1, 0))
        da_ref[:, :t] += jnp.sum(xr * pr + xi * pi, axis=0, keepdims=True)
        da_ref[:, t:] += jnp.sum(xi * pr - xr * pi, axis=0, keepdims=True)

    return pl.pallas_call(
        body, name=name, grid=(2 * S5_GROUPS * S5_STATE // (2 * t), nb),
        in_specs=[col, col, col, u_s, state, halo, bb_s, cc_s, table, vec, pl.BlockSpec(memory_space=pl.ANY)],
        out_specs=[u_s, acc, bb_s, cc_s, vec],
        out_shape=[jax.ShapeDtypeStruct(dproj.shape, dproj.dtype), jax.ShapeDtypeStruct((1, 2 * S5_GROUPS * S5_STATE), F32),
                   jax.ShapeDtypeStruct(bbt.shape, F32), jax.ShapeDtypeStruct(cct.shape, F32),
                   jax.ShapeDtypeStruct((1, w), F32)],
        scratch_shapes=[pltpu.VMEM((1, t), F32), pltpu.VMEM((1, t), F32)],
        input_output_aliases={10: 0}, compiler_params=_cp("parallel", "arbitrary"),
    )(dg1, dg2, y, proj, st, st, bbt, cct, apow_rev, dvec, dproj)


def _s5_tile_b(b_re, b_im):
    nt = S5_GROUPS * S5_STATE // S5_TILE
    gpt = S5_GROUPS // nt
    eye = jnp.eye(gpt, dtype=F32)

    def tile(b):
        t5 = jnp.einsum("jghp,gk->jghkp", b.reshape(nt, gpt, S5_GROUP, S5_STATE), eye)
        return t5.reshape(nt, gpt * S5_GROUP, S5_TILE)

    return jnp.concatenate([tile(b_re), tile(b_im)], axis=2)


def _s5_untile_b(d):
    nt = S5_GROUPS * S5_STATE // S5_TILE
    gpt = S5_GROUPS // nt
    eye = jnp.eye(gpt, dtype=F32)

    def untile(x):
        x5 = x.reshape(nt, gpt, S5_GROUP, gpt, S5_STATE)
        return jnp.einsum("jghkp,gk->jghp", x5, eye).reshape(S5_GROUPS, S5_GROUP, S5_STATE)

    return untile(d[:, :, :S5_TILE]), untile(d[:, :, S5_TILE:])


def _s5_tile_c(c_re, c_im):
    nt = S5_GROUPS * S5_STATE // S5_TILE
    gpt = S5_GROUPS // nt
    eye = jnp.eye(gpt, dtype=F32)

    def tile(c):
        t5 = jnp.einsum("jgph,gk->jkpgh", c.reshape(nt, gpt, S5_STATE, S5_GROUP), eye)
        return t5.reshape(nt, S5_TILE, gpt * S5_GROUP)

    return jnp.concatenate([tile(c_re), -tile(c_im)], axis=1)


def _s5_untile_c(d):
    nt = S5_GROUPS * S5_STATE // S5_TILE
    gpt = S5_GROUPS // nt
    eye = jnp.eye(gpt, dtype=F32)

    def untile(x):
        x5 = x.reshape(nt, gpt, S5_STATE, gpt, S5_GROUP)
        return jnp.einsum("jkpgh,gk->jgph", x5, eye).reshape(S5_GROUPS, S5_STATE, S5_GROUP)

    return untile(d[:, :S5_TILE, :]), -untile(d[:, S5_TILE:, :])


def _row_call(body, name, s, ins, outs, acc=False):
    tr = min(512, s)

    def spec(width, cb, rows):
        if rows == 1:
            return pl.BlockSpec((1, width), lambda i: (0, cb))
        return pl.BlockSpec((tr, width), lambda i: (i, cb))

    in_specs = [spec(w, cb, a.shape[0]) for a, w, cb in ins]
    out_specs = [spec(w, cb, sd.shape[0]) for sd, w, cb in outs]
    return pl.pallas_call(
        body, name=name, grid=(s // tr,), in_specs=in_specs, out_specs=out_specs,
        out_shape=[sd for sd, _, _ in outs],
        compiler_params=_cp("arbitrary" if acc else "parallel"),
    )(*[a for a, _, _ in ins])


def _sds(shape, dtype):
    return jax.ShapeDtypeStruct(shape, dtype)


def _s5_gelu_fwd(yraw, proj, dvec, name):
    s, w = yraw.shape

    def body(y_ref, u_ref, d_ref, yo_ref, g_ref):
        y = y_ref[...] + d_ref[...] * u_ref[...]
        yo_ref[...] = y
        g_ref[...] = _gelu(y).astype(BF16)

    return _row_call(body, name, s, [(yraw, w, 0), (proj, w, 4), (dvec, w, 0)],
                     [(_sds((s, w), F32), w, 0), (_sds((s, w), BF16), w, 0)])


def _s5_glu_fwd(y, z, b, name):
    s, w = y.shape

    def body(y_ref, z_ref, b_ref, o_ref):
        o_ref[...] = (_gelu(y_ref[...]) * _sigmoid(z_ref[...] + b_ref[...])).astype(BF16)

    return _row_call(body, name, s, [(y, w, 0), (z, w, 0), (b, w, 0)], [(_sds((s, w), BF16), w, 0)])[0]


def _s5_glu_bwd(dmerged, y, z, b, name):
    s, w = y.shape

    def body(do_ref, y_ref, z_ref, b_ref, dz_ref, dg_ref, db_ref):
        @pl.when(pl.program_id(0) == 0)
        def _():
            db_ref[...] = jnp.zeros_like(db_ref)

        g = _gelu(y_ref[...])
        sg = _sigmoid(z_ref[...] + b_ref[...])
        dout = do_ref[...]
        dz = dout * g * sg * (1.0 - sg)
        dz_ref[...] = dz.astype(BF16)
        dg_ref[...] = dout * sg
        db_ref[...] += jnp.sum(dz, axis=0, keepdims=True)

    return _row_call(body, name, s, [(dmerged, w, 1), (y, w, 0), (z, w, 0), (b, w, 0)],
                     [(_sds((s, w), BF16), w, 0), (_sds((s, w), F32), w, 0), (_sds((1, w), F32), w, 0)], acc=True)


def _s5_gelu_bwd(dg1, dg2, y, proj, dvec, name):
    s, w = y.shape

    def body(a_ref, b_ref, y_ref, u_ref, d_ref, dy_ref, du_ref, dd_ref):
        @pl.when(pl.program_id(0) == 0)
        def _():
            dd_ref[...] = jnp.zeros_like(dd_ref)

        dy = (a_ref[...] + b_ref[...]) * _gelu_grad(y_ref[...])
        dy_ref[...] = dy.astype(BF16)
        du_ref[...] = dy * d_ref[...]
        dd_ref[...] += jnp.sum(dy * u_ref[...], axis=0, keepdims=True)

    return _row_call(body, name, s, [(dg1, w, 0), (dg2, w, 0), (y, w, 0), (proj, w, 4), (dvec, w, 0)],
                     [(_sds((s, w), BF16), w, 0), (_sds((s, w), F32), w, 0), (_sds((1, w), F32), w, 0)], acc=True)


def _gdn_conv_fwd(projx, cw, name):
    s = projx.shape[0]
    nh = GDN_HEADS

    def body(x_ref, w_ref, o_ref):
        j = pl.program_id(0)
        cv = _conv_fwd(_conv_taps(x_ref[...], GDN_CONV), w_ref)
        y = cv * _sigmoid(cv)
        nrm = y * lax.rsqrt(jnp.sum(y * y, axis=-1, keepdims=True) + EPS)
        o_ref[...] = jnp.where(j < nh, nrm * (GDN_DH ** -0.5), jnp.where(j < 2 * nh, nrm, y))

    return pl.pallas_call(
        body, name=name, grid=(3 * nh,),
        in_specs=[pl.BlockSpec((s, GDN_DH), lambda j: (0, j)), pl.BlockSpec((GDN_CONV, GDN_DH), lambda j: (0, j))],
        out_specs=pl.BlockSpec((s, GDN_DH), lambda j: (0, j)),
        out_shape=jax.ShapeDtypeStruct((s, 3 * nh * GDN_DH), F32), compiler_params=_cp("parallel"),
    )(projx, cw)


def _gdn_conv_bwd(projx, cw, dqkv, dprojx, name):
    s = projx.shape[0]
    nh = GDN_HEADS

    def body(x_ref, w_ref, d_ref, _, dx_ref, dw_ref):
        j = pl.program_id(0)
        x = x_ref[...]
        taps = _conv_taps(x, GDN_CONV)
        cv = _conv_fwd(taps, w_ref)
        sg = _sigmoid(cv)
        y = cv * sg
        rinv = lax.rsqrt(jnp.sum(y * y, axis=-1, keepdims=True) + EPS)
        nrm = y * rinv
        dn = d_ref[...]
        dns = jnp.where(j < nh, dn * (GDN_DH ** -0.5), dn)
        dyn = rinv * (dns - nrm * jnp.sum(dns * nrm, axis=-1, keepdims=True))
        dy = jnp.where(j < 2 * nh, dyn, dn)
        dc = dy * (sg * (1.0 + cv * (1.0 - sg)))
        dx_ref[...] = _conv_bwd(taps, dc, w_ref, dw_ref).astype(BF16)

    col = pl.BlockSpec((s, GDN_DH), lambda j: (0, j))
    wcol = pl.BlockSpec((GDN_CONV, GDN_DH), lambda j: (0, j))
    return pl.pallas_call(
        body, name=name, grid=(3 * nh,), in_specs=[col, wcol, col, pl.BlockSpec(memory_space=pl.ANY)],
        out_specs=[col, wcol],
        out_shape=[jax.ShapeDtypeStruct(dprojx.shape, dprojx.dtype), jax.ShapeDtypeStruct((GDN_CONV, 3 * nh * GDN_DH), F32)],
        input_output_aliases={3: 0}, compiler_params=_cp("parallel"),
    )(projx, cw, dqkv, dprojx)


def _softplus(x):
    return jnp.maximum(x, 0.0) + jnp.log1p(jnp.exp(-jnp.abs(x)))


def _gdn_gates_fwd(projx, alog, dtb, name):
    s = projx.shape[0]
    w = GDN_HEADS * GDN_DH
    tr = min(512, s)

    def body(t_ref, al_ref, dt_ref, bo_ref, go_ref):
        t = t_ref[...]
        for h in range(GDN_HEADS):
            lo, hi = h * GDN_DH, (h + 1) * GDN_DH
            b = jnp.broadcast_to(t[:, h:h + 1], (tr, GDN_DH))
            a = jnp.broadcast_to(t[:, GDN_HEADS + h:GDN_HEADS + h + 1], (tr, GDN_DH))
            bo_ref[:, lo:hi] = _sigmoid(b)
            go_ref[:, lo:hi] = -jnp.exp(al_ref[:, lo:hi]) * _softplus(a + dt_ref[:, lo:hi])

    row = pl.BlockSpec((tr, w), lambda i: (i, 0))
    vec = pl.BlockSpec((1, w), lambda i: (0, 0))
    return pl.pallas_call(
        body, name=name, grid=(s // tr,),
        in_specs=[pl.BlockSpec((tr, LANES), lambda i: (i, 4 * w // LANES)), vec, vec], out_specs=[row, row],
        out_shape=[jax.ShapeDtypeStruct((s, w), F32)] * 2, compiler_params=_cp("parallel"),
    )(projx, alog, dtb)


def _gdn_gates_bwd(projx, alog, dtb, dbeta, dg, dprojx, name):
    s = projx.shape[0]
    w = GDN_HEADS * GDN_DH
    tr = min(512, s)
    gate_blk = 4 * w // LANES

    def body(t_ref, al_ref, dt_ref, dbe_ref, dg_ref, _, o_ref, dal_ref, ddt_ref):
        @pl.when(pl.program_id(0) == 0)
        def _():
            dal_ref[...] = jnp.zeros_like(dal_ref)
            ddt_ref[...] = jnp.zeros_like(ddt_ref)

        t = t_ref[...]
        lane = lax.broadcasted_iota(jnp.int32, (tr, LANES), 1)
        lane1 = lax.broadcasted_iota(jnp.int32, (1, LANES), 1)
        out = jnp.zeros((tr, LANES), F32)
        dal = jnp.zeros((1, LANES), F32)
        ddt = jnp.zeros((1, LANES), F32)
        for h in range(GDN_HEADS):
            lo, hi = h * GDN_DH, (h + 1) * GDN_DH
            beta = _sigmoid(t[:, h:h + 1])
            pb = jnp.sum(dbe_ref[:, lo:hi], axis=-1, keepdims=True)
            db = pb * beta * (1.0 - beta)
            xa = t[:, GDN_HEADS + h:GDN_HEADS + h + 1] + dt_ref[:, lo:lo + 1]
            ea = -jnp.exp(al_ref[:, lo:lo + 1])
            pg = jnp.sum(dg_ref[:, lo:hi], axis=-1, keepdims=True)
            da = pg * ea * _sigmoid(xa)
            out = jnp.where(lane == h, db, jnp.where(lane == GDN_HEADS + h, da, out))
            dal = jnp.where(lane1 == h, jnp.sum(pg * ea * _softplus(xa), axis=0, keepdims=True), dal)
            ddt = jnp.where(lane1 == h, jnp.sum(da, axis=0, keepdims=True), ddt)
        o_ref[...] = out.astype(BF16)
        dal_ref[...] += dal
        ddt_ref[...] += ddt

    row = pl.BlockSpec((tr, w), lambda i: (i, 0))
    vec = pl.BlockSpec((1, w), lambda i: (0, 0))
    small = pl.BlockSpec((1, LANES), lambda i: (0, 0))
    gates = pl.BlockSpec((tr, LANES), lambda i: (i, gate_blk))
    return pl.pallas_call(
        body, name=name, grid=(s // tr,),
        in_specs=[gates, vec, vec, row, row, pl.BlockSpec(memory_space=pl.ANY)],
        out_specs=[gates, small, small],
        out_shape=[jax.ShapeDtypeStruct(dprojx.shape, dprojx.dtype), jax.ShapeDtypeStruct((1, LANES), F32),
                   jax.ShapeDtypeStruct((1, LANES), F32)],
        input_output_aliases={5: 0}, compiler_params=_cp("arbitrary"),
    )(projx, alog, dtb, dbeta, dg, dprojx)


def _gdn_tri():
    c = GDN_CHUNK
    i = lax.broadcasted_iota(jnp.int32, (c, c), 0)
    j = lax.broadcasted_iota(jnp.int32, (c, c), 1)
    return ((i >= j).astype(F32), (i <= j).astype(F32), i >= j, i > j, (i == j).astype(F32))


def _bdot(a, b, ca=2, cb=1, precision=None):
    return lax.dot_general(a, b, (((ca,), (cb,)), ((0,), (0,))), precision=precision, preferred_element_type=F32)


def _bmxu(a, b, ca=2, cb=1):
    return _bdot(a.astype(BF16), b.astype(BF16), ca, cb)


def _split(x):
    hi = x.astype(BF16)
    return hi, (x - hi.astype(F32)).astype(BF16)


def _bdot3(a, b, ca=2, cb=1):
    ah, al = _split(a)
    bh, bl = _split(b)
    return _bdot(ah, bh, ca, cb) + (_bdot(ah, bl, ca, cb) + _bdot(al, bh, ca, cb))


def _tri_dot(tri, x):
    t = tri.astype(BF16)
    hi = x.astype(BF16)
    r1 = x - hi.astype(F32)
    mid = r1.astype(BF16)
    lo = (r1 - mid.astype(F32)).astype(BF16)
    return _dot(t, hi) + (_dot(t, mid) + _dot(t, lo))


def _heads(x):
    return jnp.stack([x[:, h * GDN_DH:(h + 1) * GDN_DH] for h in range(GDN_HEADS)], axis=0)


def _unheads(x):
    return jnp.concatenate([x[h] for h in range(GDN_HEADS)], axis=1)


def _gdn_chunk(q, k, v, bb, g2d, tri):
    low, up, incl, strict, eye = tri
    c = GDN_CHUNK
    gc = _heads(_tri_dot(low, g2d))
    gci = gc[:, :, :c]
    gdiff = gci - jnp.swapaxes(gci, 1, 2)
    decay = jnp.where(incl, jnp.exp(jnp.where(incl, gdiff, 0.0)), 0.0)
    kb, vb = k * bb, v * bb
    kbk = _bmxu(kb, k, 2, 2)
    x = -jnp.where(strict, kbk * decay, 0.0)
    t = eye + x
    p = x
    for _ in range(c.bit_length() - 2):
        p = _bdot3(p, p)
        t = t + _bdot3(t, p)
    eg = jnp.exp(gc)
    kbg = kb * eg
    gcl = gc[:, c - 1:c, :]
    ek = jnp.exp(gcl - gc)
    qkraw = _bmxu(q, k, 2, 2)
    return dict(decay=decay, kb=kb, vb=vb, kbk=kbk, t=t, eg=eg, kbg=kbg, ek=ek, gl=jnp.exp(gcl),
                w=_bmxu(t, kbg), u=_bmxu(t, vb), qkraw=qkraw, qk=jnp.where(incl, qkraw * decay, 0.0),
                qd=q * eg, kd=k * ek)


def _gdn_specs(n_of):
    c, w = GDN_CHUNK, GDN_HEADS * GDN_DH

    def blk(cb, width=w):
        return pl.BlockSpec((c, width), lambda n: (n_of(n), cb))

    st = pl.BlockSpec((None, GDN_HEADS, GDN_DH, GDN_DH), lambda n: (n_of(n), 0, 0, 0))
    vec = pl.BlockSpec((1, GDN_DH), lambda n: (0, 0))
    return blk, st, vec


def _gdn_load(qkv_ref, b_ref, g_ref, tri):
    w = GDN_HEADS * GDN_DH
    q, k, v = _heads(qkv_ref[:, :w]), _heads(qkv_ref[:, w:2 * w]), _heads(qkv_ref[:, 2 * w:])
    bb = _heads(b_ref[...])
    return q, k, v, bb, _gdn_chunk(q, k, v, bb, g_ref[...], tri)


def _gdn_fwd(qkv, beta, g, projx, onorm, name):
    s = qkv.shape[0]
    nc = s // GDN_CHUNK
    w = GDN_HEADS * GDN_DH
    blk, st, vec = _gdn_specs(lambda n: n)

    def body(qkv_ref, b_ref, g_ref, z_ref, on_ref, o_ref, st_ref, state):
        @pl.when(pl.program_id(0) == 0)
        def _():
            state[...] = jnp.zeros_like(state)

        _, _, _, _, ch = _gdn_load(qkv_ref, b_ref, g_ref, _gdn_tri())
        sp = state[...]
        st_ref[...] = sp
        vn = ch["u"] - _bmxu(ch["w"], sp)
        o = _bmxu(ch["qd"], sp) + _bmxu(ch["qk"], vn)
        state[...] = sp * ch["gl"] + _bmxu(ch["kd"], vn, 1, 1)
        r = lax.rsqrt(jnp.mean(o * o, axis=-1, keepdims=True) + EPS)
        z = _heads(z_ref[...])
        o_ref[...] = _unheads(o * r * on_ref[...] * (z * _sigmoid(z))).astype(BF16)

    return pl.pallas_call(
        body, name=name, grid=(nc,),
        in_specs=[blk(0, 3 * w), blk(0), blk(0), blk(3), vec], out_specs=[blk(0), st],
        out_shape=[jax.ShapeDtypeStruct((s, w), BF16), jax.ShapeDtypeStruct((nc, GDN_HEADS, GDN_DH, GDN_DH), F32)],
        scratch_shapes=[pltpu.VMEM((GDN_HEADS, GDN_DH, GDN_DH), F32)],
        compiler_params=_cp("arbitrary"),
    )(qkv, beta, g, projx, onorm.reshape(1, -1))


def _gdn_bwd(qkv, beta, g, projx, onorm, states, dout, name):
    s = qkv.shape[0]
    c = GDN_CHUNK
    nc = s // c
    w = GDN_HEADS * GDN_DH
    blk, st, vec = _gdn_specs(lambda n: nc - 1 - n)

    def body(qkv_ref, b_ref, g_ref, z_ref, on_ref, st_ref, do_ref,
             dqkv_ref, db_ref, dg_ref, dz_ref, don_ref, carry):
        @pl.when(pl.program_id(0) == 0)
        def _():
            carry[...] = jnp.zeros_like(carry)
            don_ref[...] = jnp.zeros_like(don_ref)

        tri = _gdn_tri()
        low, up, incl, strict, eye = tri
        q, k, v, bb, ch = _gdn_load(qkv_ref, b_ref, g_ref, tri)
        sp = st_ref[...]
        vn = ch["u"] - _bmxu(ch["w"], sp)
        o = _bmxu(ch["qd"], sp) + _bmxu(ch["qk"], vn)
        r = lax.rsqrt(jnp.mean(o * o, axis=-1, keepdims=True) + EPS)
        orn = o * r
        z = _heads(z_ref[...])
        sg = _sigmoid(z)
        dout = _heads(do_ref[...])
        onw = on_ref[...]
        dz_ref[...] = _unheads(dout * orn * onw * (sg * (1.0 + z * (1.0 - sg)))).astype(BF16)
        don = dout * (z * sg)
        don_ref[...] += jnp.sum(jnp.sum(don * orn, axis=0), axis=0, keepdims=True)
        dor = don * onw
        do = r * (dor - orn * jnp.mean(dor * orn, axis=-1, keepdims=True))
        dsn = carry[...]
        dqd = _bmxu(do, sp, 2, 2)
        dqk = jnp.where(incl, _bmxu(do, vn, 2, 2), 0.0)
        dvn = _bmxu(ch["qk"], do, 1, 1) + _bmxu(ch["kd"], dsn)
        dkd = _bmxu(vn, dsn, 2, 2)
        dgl = jnp.sum(dsn * sp, axis=1, keepdims=True)
        dw = -_bmxu(dvn, sp, 2, 2)
        carry[...] = _bmxu(ch["qd"], do, 1, 1) + dsn * ch["gl"] - _bmxu(ch["w"], dvn, 1, 1)
        t = ch["t"]
        dvb = _bmxu(t, dvn, 1, 1)
        dkbg = _bmxu(t, dw, 1, 1)
        dt = _bmxu(dvn, ch["vb"], 2, 2) + _bmxu(dw, ch["kbg"], 2, 2)
        da = -_bdot3(_bdot3(t, dt, 1, 1), t, 2, 2)
        da = jnp.where(strict, da, 0.0)
        decay = ch["decay"]
        dkbk = da * decay
        dqkr = dqk * decay
        mdec = (da * ch["kbk"] + dqk * ch["qkraw"]) * decay
        dkb = _bmxu(dkbk, k) + dkbg * ch["eg"]
        dk = _bmxu(dkbk, ch["kb"], 1, 1) + _bmxu(dqkr, q, 1, 1) + dkd * ch["ek"] + dkb * bb
        dq = _bmxu(dqkr, k) + dqd * ch["eg"]
        tk = dkd * ch["kd"]
        dgcl = jnp.sum(tk, axis=1, keepdims=True) + dgl * ch["gl"]
        row = lax.broadcasted_iota(jnp.int32, (GDN_HEADS, c, GDN_DH), 1)
        zpad = jnp.zeros((GDN_HEADS, c, GDN_DH - c), F32)
        dgc = (jnp.concatenate([mdec, zpad], axis=2) - jnp.concatenate([jnp.swapaxes(mdec, 1, 2), zpad], axis=2)
               + dqd * ch["qd"] - tk + dkbg * ch["kbg"] + jnp.where(row == c - 1, dgcl, 0.0))
        dqkv_ref[:, :w] = _unheads(dq)
        dqkv_ref[:, w:2 * w] = _unheads(dk)
        dqkv_ref[:, 2 * w:] = _unheads(dvb * bb)
        db_ref[...] = _unheads(dkb * k + dvb * v)
        dg_ref[...] = _tri_dot(up, _unheads(dgc))

    return pl.pallas_call(
        body, name=name, grid=(nc,),
        in_specs=[blk(0, 3 * w), blk(0), blk(0), blk(3), vec, st, blk(0)],
        out_specs=[blk(0, 3 * w), blk(0), blk(0), blk(3), vec],
        out_shape=[jax.ShapeDtypeStruct((s, 3 * w), F32), jax.ShapeDtypeStruct((s, w), F32),
                   jax.ShapeDtypeStruct((s, w), F32), jax.ShapeDtypeStruct(projx.shape, BF16),
                   jax.ShapeDtypeStruct((1, GDN_DH), F32)],
        scratch_shapes=[pltpu.VMEM((GDN_HEADS, GDN_DH, GDN_DH), F32)],
        compiler_params=_cp("arbitrary"),
    )(qkv, beta, g, projx, onorm.reshape(1, -1), states, dout)


_WEIGHTS = (
    "l0_mix_norm", "l0_w_in", "l0_ret_norm", "l0_s5_lambda_re", "l0_s5_lambda_im", "l0_s5_b_re", "l0_s5_b_im",
    "l0_s5_c_re", "l0_s5_c_im", "l0_s5_d", "l0_s5_log_dt", "l0_s5_w_glu", "l0_s5_b_glu", "l0_w_out",
    "l0_xa_norm", "l0_mem_norm", "l0_xa_wq", "l0_xa_wkv", "l0_xa_wo", "l0_ffn_norm", "l0_ffn_w_up",
    "l0_ffn_conv", "l0_ffn_w_down", "l1_mix_norm", "l1_w_in", "l1_conv", "l1_a_log", "l1_dt_bias", "l1_o_norm",
    "l1_w_out", "l1_xa_norm", "l1_mem_norm", "l1_xa_wq", "l1_xa_wkv", "l1_xa_wo", "l1_ffn_norm", "l1_ffn_w_up",
    "l1_ffn_conv", "l1_ffn_w_down", "final_norm")
_INPUTS = ("x", "mem") + _WEIGHTS + ("loss_target",) + tuple("m_" + n for n in _WEIGHTS) + tuple("v_" + n for n in _WEIGHTS)

_COL = ("l0_w_in", "l0_xa_wkv", "l0_ffn_w_up", "l0_ffn_conv", "l1_w_in", "l1_conv", "l1_xa_wkv", "l1_ffn_w_up",
        "l1_ffn_conv")
_ROW = ("l0_s5_w_glu", "l0_w_out", "l0_xa_wq", "l0_xa_wo", "l0_ffn_w_down", "l1_w_out", "l1_xa_wq", "l1_xa_wo",
        "l1_ffn_w_down")
_F32_WIRE = ("l0_ffn_conv", "l1_conv", "l1_ffn_conv")
_REP = tuple(n for n in _WEIGHTS if n not in _COL + _ROW)
_GATHER_GROUPS = (("l0_w_in", "l0_s5_w_glu", "l0_w_out"),
                  ("l0_xa_wq", "l0_xa_wkv", "l0_xa_wo", "l0_ffn_w_up", "l0_ffn_conv", "l0_ffn_w_down"),
                  ("l1_w_in", "l1_conv", "l1_w_out", "l1_xa_wq", "l1_xa_wkv", "l1_xa_wo"),
                  ("l1_ffn_w_up", "l1_ffn_conv", "l1_ffn_w_down"))


def _round_up(n, m):
    return (n + m - 1) // m * m


_REP_BIG = ("l0_s5_lambda_re", "l0_s5_lambda_im", "l0_s5_b_re", "l0_s5_b_im", "l0_s5_c_re", "l0_s5_c_im", "l0_s5_d")
_REP_LAST = "l0_mix_norm"
_REP_SMALL = tuple(n for n in _REP if n not in _REP_BIG + (_REP_LAST,))
PACK_WIDTH = 1024


def _pack_rows(ts):
    rows = [jnp.pad(t, ((0, 0), (0, PACK_WIDTH - t.shape[1]))) for t in ts]
    rows.append(jnp.zeros((_round_up(len(ts), 8) - len(ts), PACK_WIDTH), F32))
    return jnp.concatenate(rows, axis=0)


def _s5_interleave(re, im):
    lead = re.shape[:-1]
    nt = re.shape[-1] // S5_TILE
    both = jnp.stack([re.reshape(lead + (nt, S5_TILE)), im.reshape(lead + (nt, S5_TILE))], axis=-2)
    return both.reshape(lead + (2 * re.shape[-1],))


def _s5_split(x):
    lead = x.shape[:-1]
    y = x.reshape(lead + (x.shape[-1] // (2 * S5_TILE), 2, S5_TILE))
    return y[..., 0, :].reshape(lead + (-1,)), y[..., 1, :].reshape(lead + (-1,))


def _s5_discretise(lr, li, log_dt, b_re, b_im):
    dt = jnp.exp(log_dt)[:, None]
    mag = jnp.exp(lr * dt)
    a_re = mag * jnp.cos(li * dt)
    a_im = mag * jnp.sin(li * dt)
    den = lr * lr + li * li
    z_re = ((a_re - 1.0) * lr + a_im * li) / den
    z_im = (a_im * lr - (a_re - 1.0) * li) / den
    bb_re = z_re[:, None, :] * b_re - z_im[:, None, :] * b_im
    bb_im = z_re[:, None, :] * b_im + z_im[:, None, :] * b_re
    return a_re, a_im, bb_re, bb_im


def _block_diag(b):
    g, r, c = b.shape
    return jnp.einsum("grc,gk->grkc", b, jnp.eye(g, dtype=b.dtype)).reshape(g * r, g * c)


def _block_diag_of(d, g):
    r, c = d.shape[0] // g, d.shape[1] // g
    return jnp.einsum("grkc,gk->grc", d.reshape(g, r, g, c), jnp.eye(g, dtype=d.dtype))


def kernel(*args):
    p = dict(zip(_INPUTS, args, strict=True))
    x0, mem0, tgt = p["x"][0], p["mem"][0], p["loss_target"][0]
    s, d = x0.shape
    me = _slot(*_mesh_pos())
    grads = {}
    wire = {n: (F32 if n in _F32_WIRE else BF16) for n in _COL + _ROW}

    zones = {n: _into_slot(p[n], wire[n], me, "place_" + n) for names in _GATHER_GROUPS for n in names}
    gather, pin = [], jnp.zeros((), F32)
    for i, names in enumerate(_GATHER_GROUPS):
        handle, token = _push_start([], [zones[n] for n in names], f"gather{i}_start")
        gather.append(handle)
        pin = pin + token[0, 0]
    w = {}

    def gathered(i, after):
        for n, full in zip(_GATHER_GROUPS[i], _push_wait(gather[i], after, f"gather{i}_wait")):
            if n in _COL:
                full = full.transpose(1, 0, 2)
            w[n] = full.reshape(-1, full.shape[-1]) if n in _ROW else full.reshape(full.shape[0], -1)

    pending = []

    def exchange(names, gain, tag):
        slots = []
        for n in names:
            g = grads[n]
            if n in _COL:
                pieces = g if isinstance(g, tuple) else (g,)
                g = jnp.concatenate([t.reshape(t.shape[0], -1, p[n].shape[1]).transpose(1, 0, 2) for t in pieces], axis=0)
            else:
                g = g.reshape((N_DEV, -1) + g.shape[1:])
            slots.append(g.astype(wire[n]))
        handle, token = _push_start(slots, [], tag + "_start")
        pending.append((names, slots, handle, tag))
        return gain + token[0, 0]

    def xattn(pre, x_in):
        hx = _norm_fwd(x_in, p[pre + "xa_norm"], pre + "xa_norm_fwd")
        q = _mm(hx, w[pre + "xa_wq"], out_dtype=BF16, name=pre + "xa_q")
        memn = _norm_fwd(mem0, p[pre + "mem_norm"], pre + "mem_norm_fwd")
        kv = _mm(memn, w[pre + "xa_wkv"], out_dtype=BF16, name=pre + "xa_kv")
        ao = _xattn_fwd(q, kv, pre + "xattn_fwd")
        x_out = _mm(ao, w[pre + "xa_wo"], res=x_in, name=pre + "xa_o")
        return x_out, (x_in, hx, q, memn, kv, ao)

    def xattn_bwd(pre, saved, dxo):
        x_in, hx, q, memn, kv, ao = saved
        dao = _mm(dxo, w[pre + "xa_wo"], tb=True, name=pre + "xa_o_dx")
        grads[pre + "xa_wo"] = _mm(ao, dxo, ta=True, out_dtype=BF16, name=pre + "xa_o_dw")
        dq, dkv = _xattn_bwd(q, kv, dao, pre + "xattn_bwd")
        grads[pre + "xa_wq"] = _mm(hx, dq, ta=True, out_dtype=BF16, name=pre + "xa_q_dw")
        grads[pre + "xa_wkv"] = _mm(memn, dkv, ta=True, out_dtype=BF16, name=pre + "xa_kv_dw")
        dmemn = _mm(dkv, w[pre + "xa_wkv"], tb=True, name=pre + "xa_kv_dx")
        gain = exchange((pre + "xa_wo", pre + "xa_wq", pre + "xa_wkv"), p[pre + "xa_norm"], pre + "xa_grads")
        dx_in, grads[pre + "xa_norm"] = _mm_norm_bwd(dq, w[pre + "xa_wq"], x_in, gain, dxo, pre + "xa_q_dx")
        _, grads[pre + "mem_norm"] = _norm_bwd(mem0, p[pre + "mem_norm"], dmemn, jnp.zeros_like(mem0), pre + "mem_norm_bwd")
        return dx_in

    def ffn(pre, x_in):
        hf = _norm_fwd(x_in, p[pre + "ffn_norm"], pre + "ffn_norm_fwd")
        up = _mm(hf, w[pre + "ffn_w_up"], out_dtype=BF16, name=pre + "ffn_up")
        act = _ffn_act_fwd(up, w[pre + "ffn_conv"], pre + "ffn_act_fwd")
        x_out = _mm(act, w[pre + "ffn_w_down"], res=x_in, name=pre + "ffn_down")
        return x_out, (x_in, hf, up, act)

    def ffn_bwd(pre, saved, dxo):
        x_in, hf, up, act = saved
        dact = _mm(dxo, w[pre + "ffn_w_down"], tb=True, out_dtype=BF16, name=pre + "ffn_down_dx")
        grads[pre + "ffn_w_down"] = _mm(act, dxo, ta=True, out_dtype=BF16, name=pre + "ffn_down_dw")
        dpu, dpg, dcu, dcg = _ffn_act_bwd(up, w[pre + "ffn_conv"], dact, pre + "ffn_act_bwd")
        grads[pre + "ffn_conv"] = jnp.concatenate([dcu, dcg], axis=1)
        grads[pre + "ffn_w_up"] = (_mm(hf, dpu, ta=True, out_dtype=BF16, name=pre + "ffn_up_dw_u"),
                                   _mm(hf, dpg, ta=True, out_dtype=BF16, name=pre + "ffn_up_dw_g"))
        gain = exchange((pre + "ffn_w_down", pre + "ffn_w_up", pre + "ffn_conv"), p[pre + "ffn_norm"], pre + "ffn_grads")
        dx_in, grads[pre + "ffn_norm"] = _mm_norm_bwd([dpu, dpg], w[pre + "ffn_w_up"], x_in, gain, dxo, pre + "ffn_up_dx")
        return dx_in

    cos, sin = _rope_tables(s)
    (a_re, a_im, bb_re, bb_im), disc_vjp = jax.vjp(
        _s5_discretise, p["l0_s5_lambda_re"], p["l0_s5_lambda_im"], p["l0_s5_log_dt"], p["l0_s5_b_re"], p["l0_s5_b_im"])
    apow, apow_rev = _s5_pow_tables(_s5_interleave(a_re.reshape(1, -1), a_im.reshape(1, -1)), "l0_s5_pow_tables")
    bbt = _s5_tile_b(bb_re, bb_im).astype(BF16)
    cct = _s5_tile_c(p["l0_s5_c_re"], p["l0_s5_c_im"]).astype(BF16)
    s5_d = p["l0_s5_d"].reshape(1, -1)
    b_glu = p["l0_s5_b_glu"].reshape(1, -1)

    h0 = _norm_fwd(x0, p["l0_mix_norm"] + pin, "l0_mix_norm_fwd")
    gathered(0, h0)
    proj = _mm(h0, w["l0_w_in"], name="l0_in")
    o_ret, ret_states = _ret_fwd(proj, cos, sin, p["l0_ret_norm"], "l0_ret_fwd")
    st, y, gy = _s5_fwd(proj, bbt, cct, apow, s5_d, "l0_s5_fwd")
    z = _mm(gy, w["l0_s5_w_glu"], name="l0_s5_glu_mm")
    y2 = _s5_glu_fwd(y, z, b_glu, "l0_s5_glu_fwd")
    merged = jnp.concatenate([o_ret, y2], axis=1)
    x1 = _mm(merged, w["l0_w_out"], res=x0, name="l0_out")
    gathered(1, x1)
    x2, xa0 = xattn("l0_", x1)
    x3, ff0 = ffn("l0_", x2)

    gathered(2, x3)
    w1 = w["l1_w_in"]
    wx = jnp.pad(w1, ((0, 0), (0, _round_up(w1.shape[1], LANES) - w1.shape[1])))
    alog_x = jnp.repeat(p["l1_a_log"], GDN_DH).reshape(1, -1)
    dtb_x = jnp.repeat(p["l1_dt_bias"], GDN_DH).reshape(1, -1)
    h1 = _norm_fwd(x3, p["l1_mix_norm"], "l1_mix_norm_fwd")
    projx = _mm(h1, wx, name="l1_in")
    qkv = _gdn_conv_fwd(projx, w["l1_conv"], "l1_conv_fwd")
    beta, glog = _gdn_gates_fwd(projx, alog_x, dtb_x, "l1_gates_fwd")
    o_gdn, gdn_states = _gdn_fwd(qkv, beta, glog, projx, p["l1_o_norm"], "l1_gdn_fwd")
    x4 = _mm(o_gdn, w["l1_w_out"], res=x3, name="l1_out")
    x5, xa1 = xattn("l1_", x4)
    gathered(3, x5)
    x6, ff1 = ffn("l1_", x5)

    loss_part, dx6, grads["final_norm"] = _loss_head(x6, p["final_norm"], tgt, "loss_head")
    loss = lax.psum(loss_part[0, 0], ("x", "y", "c"))
    dx5 = ffn_bwd("l1_", ff1, dx6)
    dx4 = xattn_bwd("l1_", xa1, dx5)

    do_gdn = _mm(dx4, w["l1_w_out"], tb=True, name="l1_out_dx")
    grads["l1_w_out"] = _mm(o_gdn, dx4, ta=True, out_dtype=BF16, name="l1_out_dw")
    dqkv, dbeta, dglog, dprojx, grads["l1_o_norm"] = _gdn_bwd(
        qkv, beta, glog, projx, p["l1_o_norm"], gdn_states, do_gdn, "l1_gdn_bwd")
    dprojx, grads["l1_conv"] = _gdn_conv_bwd(projx, w["l1_conv"], dqkv, dprojx, "l1_conv_bwd")
    dprojx, dalog_x, ddtb_x = _gdn_gates_bwd(projx, alog_x, dtb_x, dbeta, dglog, dprojx, "l1_gates_bwd")
    grads["l1_w_in"] = _mm(h1, dprojx, ta=True, out_dtype=BF16, name="l1_in_dw")[:, :w1.shape[1]]
    grads["l1_a_log"] = dalog_x[0, :GDN_HEADS]
    grads["l1_dt_bias"] = ddtb_x[0, :GDN_HEADS]
    gain = exchange(("l1_w_out", "l1_w_in", "l1_conv"), p["l1_mix_norm"], "l1_mix_grads")
    dx3, grads["l1_mix_norm"] = _mm_norm_bwd(dprojx, wx, x3, gain, dx4, "l1_in_dx")

    dx2 = ffn_bwd("l0_", ff0, dx3)
    dx1 = xattn_bwd("l0_", xa0, dx2)

    dmerged = _mm(dx1, w["l0_w_out"], tb=True, name="l0_out_dx")
    grads["l0_w_out"] = _mm(merged, dx1, ta=True, out_dtype=BF16, name="l0_out_dw")
    dproj, grads["l0_ret_norm"] = _ret_bwd(proj, cos, sin, p["l0_ret_norm"], ret_states, dmerged, "l0_ret_bwd")
    dzg, dg1, grads["l0_s5_b_glu"] = _s5_glu_bwd(dmerged, y, z, b_glu, "l0_s5_glu_bwd")
    grads["l0_s5_w_glu"] = _mm(gy, dzg, ta=True, out_dtype=BF16, name="l0_s5_glu_dw")
    s5_d_after = exchange(("l0_w_out", "l0_s5_w_glu"), s5_d, "l0_out_grads")
    dg2 = _mm(dzg, w["l0_s5_w_glu"], tb=True, name="l0_s5_glu_dx")
    dproj, da_s5, dbbt, dcct, grads["l0_s5_d"] = _s5_bwd(dg1, dg2, y, proj, st, bbt, cct, apow_rev, s5_d_after, dproj, "l0_s5_bwd")
    dbb_re, dbb_im = _s5_untile_b(dbbt)
    grads["l0_s5_c_re"], grads["l0_s5_c_im"] = _s5_untile_c(dcct)
    da_re, da_im = (t.reshape(S5_GROUPS, S5_STATE) for t in _s5_split(da_s5[0]))
    (grads["l0_s5_lambda_re"], grads["l0_s5_lambda_im"], grads["l0_s5_log_dt"], grads["l0_s5_b_re"],
     grads["l0_s5_b_im"]) = disc_vjp((da_re, da_im, dbb_re, dbb_im))

    def as_2d(t):
        return t.reshape(-1, t.shape[-1])

    def as_row(t):
        return t.reshape(1, -1)

    small_own = _pack_rows([as_row(grads[n]) for n in _REP_SMALL])
    big_own = [as_2d(grads[n].reshape(p[n].shape)) for n in _REP_BIG]
    rep_zones = [_into_slot(small_own, F32, me, "place_rep0")]
    rep_zones += [_into_slot(t.reshape(-1, LANES), BF16, me, f"place_rep{i + 1}") for i, t in enumerate(big_own)]
    rep_handle, rep_token = _push_start([], rep_zones, "rep_grads_start")

    grads["l0_w_in"] = _mm(h0, dproj, ta=True, out_dtype=BF16, pin=rep_token, name="l0_in_dw")
    gain = exchange(("l0_w_in",), p["l0_mix_norm"], "l0_mix_grads")
    dx0, grads["l0_mix_norm"] = _mm_norm_bwd(dproj, w["l0_w_in"], x0, gain, dx1, "l0_in_dx")

    last_own = _pack_rows([as_row(grads[_REP_LAST])])
    last_handle, _ = _push_start([], [_into_slot(last_own, F32, me, "place_rep_last")], "rep_last_start")
    last_land, = _push_wait(last_handle, dx0, "rep_last_wait")
    rep_lands = _push_wait(rep_handle, last_land, "rep_grads_wait")
    rep_land = rep_lands[0]

    outs = {}
    kinds = ("grad_", "delta_", "new_m_", "new_v_")
    for names, slots, handle, tag in pending:
        for n, own_slots, land in zip(names, slots, _push_wait(handle, rep_land, tag + "_wait")):
            shape = p[n].shape
            own = lax.dynamic_index_in_dim(own_slots, me, 0, keepdims=False)
            res = _adamw(land, own, *(p[pre + n].reshape(own.shape) for pre in ("", "m_", "v_")), "adamw_" + n)
            for kind, t in zip(kinds, res):
                outs[kind + n] = t.reshape(shape)
    for n, own, land in zip(_REP_BIG, big_own, rep_lands[1:]):
        res = _adamw(land.reshape((N_DEV,) + own.shape), None, *(as_2d(p[pre + n]) for pre in ("", "m_", "v_")), "adamw_" + n)
        for kind, t in zip(kinds, res):
            outs[kind + n] = t.reshape(p[n].shape)
    for names, land, own, nm in ((_REP_SMALL, rep_land, small_own, "adamw_small"), ((_REP_LAST,), last_land, last_own, "adamw_last")):
        res = _adamw_rows(land, own, *([as_row(p[pre + n]) for n in names] for pre in ("", "m_", "v_")), nm)
        for j, kind in enumerate(kinds):
            for i, n in enumerate(names):
                outs[kind + n] = res[j * len(names) + i].reshape(p[n].shape)

    return (loss, dx0[None]) + tuple(outs[kind + n] for kind in kinds for n in _WEIGHTS)
```

```python
import functools
import math

import numpy as np
import jax
import jax.numpy as jnp
from jax import lax
from jax.experimental import pallas as pl
from jax.experimental.pallas import tpu as pltpu

F32 = jnp.float32
BF16 = jnp.bfloat16
EPS = 1e-6
N_DEV = 8
LANES = 128
VMEM_LIMIT = 48 * 1024 * 1024
HI = lax.Precision.HIGHEST

RET_HEADS, RET_DH, RET_CHUNK = 4, 128, 128
S5_GROUPS, S5_GROUP, S5_STATE = 32, 16, 64
GDN_HEADS, GDN_DH, GDN_CHUNK, GDN_CONV = 8, 128, 64, 4
XA_HEADS, XA_DH = 4, 256
FFN_CONV = 3
SCAN_ROWS = 256

ADAM_LR, ADAM_B1, ADAM_B2, ADAM_EPS, ADAM_WD, ADAM_STEP = 0.001, 0.9, 0.999, 1e-08, 0.01, 10


def _cp(*sem):
    return pltpu.CompilerParams(dimension_semantics=sem if sem else None, vmem_limit_bytes=VMEM_LIMIT)


def _tile(n, cap):
    if n <= cap:
        return n
    best = None
    for t in range(LANES, cap + 1, LANES):
        if n % t == 0:
            best = t
    assert best is not None, n
    return best


def _dot(a, b, ca=1, cb=0, precision=None):
    return lax.dot_general(a, b, (((ca,), (cb,)), ((), ())), precision=precision, preferred_element_type=F32)


def _mxu(a, b, ca=1, cb=0):
    return _dot(a.astype(BF16), b.astype(BF16), ca, cb)


def _sigmoid(x):
    return 0.5 * jnp.tanh(0.5 * x) + 0.5


def _shift_down(x, k):
    r = pltpu.roll(x, k, 0)
    row = lax.broadcasted_iota(jnp.int32, (8,) + x.shape[1:], 0)
    return jnp.concatenate([jnp.where(row >= k, r[:8], 0.0), r[8:]], axis=0)


def _shift_up(x, k):
    n = x.shape[0]
    r = pltpu.roll(x, n - k, 0)
    row = lax.broadcasted_iota(jnp.int32, (8,) + x.shape[1:], 0)
    return jnp.concatenate([r[:n - 8], jnp.where(row < 8 - k, r[n - 8:], 0.0)], axis=0)


def _mesh_pos():
    return lax.axis_index("x"), lax.axis_index("y"), lax.axis_index("c")


def _slot(px, py, pc):
    return 4 * px + 2 * py + pc


def _all_peers(x, y, c):
    flips = [(fx, fy, fc) for fx in (0, 1) for fy in (0, 1) for fc in (0, 1)][1:]
    return [(1 - x if fx else x, 1 - y if fy else y, 1 - c if fc else c) for fx, fy, fc in flips]


_HBM = pl.BlockSpec(memory_space=pltpu.HBM)
_SEM = pl.BlockSpec(memory_space=pltpu.SEMAPHORE)
N_PEERS = N_DEV - 1


def _push_copies(srcs, lands, send_sems, recv_sems, start):
    x, y, c = _mesh_pos()
    me = _slot(x, y, c)
    out = []
    for k, to in enumerate(_all_peers(x, y, c)):
        for a in range(len(lands)):
            src = srcs[a].at[_slot(*to)] if a < len(srcs) else lands[a].at[me]
            dst = lands[a].at[me if start else _slot(*to)]
            out.append(pltpu.make_async_remote_copy(
                src_ref=src, dst_ref=dst, send_sem=send_sems.at[a * N_PEERS + k], recv_sem=recv_sems.at[a * N_PEERS + k],
                device_id=to, device_id_type=pl.DeviceIdType.MESH))
    return out


def _into_slot(x, dtype, me, name):
    r, c = x.shape
    cap = max(16, 512 * 1024 // c)
    tr = max(t for t in range(16, min(r, cap) + 1, 16) if r % t == 0) if r % 16 == 0 else r

    def body(me_ref, x_ref, o_ref):
        o_ref[...] = x_ref[...].astype(dtype)

    return pl.pallas_call(
        body, name=name, out_shape=jax.ShapeDtypeStruct((N_DEV, r, c), dtype),
        grid_spec=pltpu.PrefetchScalarGridSpec(
            num_scalar_prefetch=1, grid=(r // tr,),
            in_specs=[pl.BlockSpec((tr, c), lambda i, me_ref: (i, 0))],
            out_specs=pl.BlockSpec((None, tr, c), lambda i, me_ref: (me_ref[0], i, 0))),
        compiler_params=_cp("parallel"),
    )(me.reshape(1).astype(jnp.int32), x)


def _push_start(scatter, gather_lands, name):
    ns, n = len(scatter), len(scatter) + len(gather_lands)
    lands = [lax.empty(a.shape, a.dtype) for a in scatter] + list(gather_lands)

    def body(*refs):
        srcs, zones = refs[:ns], refs[ns:ns + n]
        for cp in _push_copies(srcs, zones, refs[ns + n], refs[ns + n + 1], True):
            cp.start()
        refs[-1][...] = jnp.zeros((8, LANES), F32)

    hbm_in = [pltpu.with_memory_space_constraint(a, pltpu.HBM) for a in list(scatter) + lands]
    res = pl.pallas_call(
        body, name=name,
        out_shape=(pltpu.SemaphoreType.DMA((n * N_PEERS,)), pltpu.SemaphoreType.DMA((n * N_PEERS,)))
        + tuple(pltpu.HBM(a.shape, a.dtype) for a in list(scatter) + lands)
        + (jax.ShapeDtypeStruct((8, LANES), F32),),
        in_specs=[_HBM] * (ns + n),
        out_specs=(_SEM, _SEM) + (_HBM,) * (ns + n) + (pl.BlockSpec(memory_space=pltpu.VMEM),),
        input_output_aliases={i: 2 + i for i in range(ns + n)},
        compiler_params=pltpu.CompilerParams(has_side_effects=pltpu.SideEffectType.DATAFLOW_SIDE_EFFECTING),
    )(*hbm_in)
    return (res[0], res[1], res[2:2 + ns], res[2 + ns:2 + ns + n]), res[-1]


def _push_wait(handle, after, name):
    send_sems, recv_sems, srcs, lands = handle
    ns, n = len(srcs), len(lands)

    def body(*refs):
        for cp in _push_copies(refs[:ns], refs[ns:ns + n], refs[ns + n], refs[ns + n + 1], False):
            cp.wait_send()
            cp.wait_recv()

    res = pl.pallas_call(
        body, name=name,
        out_shape=tuple(pltpu.HBM(a.shape, a.dtype) for a in list(srcs) + list(lands)),
        in_specs=[_HBM] * (ns + n) + [_SEM, _SEM, pl.BlockSpec(memory_space=pl.ANY)],
        out_specs=(_HBM,) * (ns + n),
        input_output_aliases={i: i for i in range(ns + n)},
        compiler_params=pltpu.CompilerParams(has_side_effects=pltpu.SideEffectType.DATAFLOW_SIDE_EFFECTING),
    )(*srcs, *lands, send_sems, recv_sems, after)
    return res[ns:]


def _mm(a, b, *, ta=False, tb=False, out_dtype=F32, res=None, pin=None, norm_gain=None, name="mm"):
    m, k = (a.shape[1], a.shape[0]) if ta else a.shape
    n = b.shape[0] if tb else b.shape[1]
    assert k == (b.shape[1] if tb else b.shape[0]), (a.shape, b.shape, ta, tb)
    tm, tn, tk = _tile(m, 1408), _tile(n, 1536), _tile(k, 1408)
    nk = k // tk
    has_res = res is not None
    has_norm = norm_gain is not None
    assert not has_norm or tn == n
    n_in = 2 + has_res + (pin is not None) + has_norm

    def body(*refs):
        a_ref, b_ref = refs[:2]
        r_ref = refs[2] if has_res else None
        o_ref = refs[n_in]
        part = _mxu(a_ref[...], b_ref[...], 0 if ta else 1, 1 if tb else 0)

        def finish(r):
            if has_res:
                r = r + r_ref[...].astype(F32)
            o_ref[...] = r.astype(out_dtype)
            if has_norm:
                scale = lax.rsqrt(jnp.mean(r * r, axis=-1, keepdims=True) + EPS)
                refs[n_in + 1][...] = (r * scale * refs[n_in - 1][...]).astype(BF16)

        if nk == 1:
            finish(part)
            return
        acc = refs[-1]
        kk = pl.program_id(2)

        @pl.when(kk == 0)
        def _():
            acc[...] = part

        @pl.when(kk > 0)
        def _():
            acc[...] += part

        @pl.when(kk == nk - 1)
        def _():
            finish(acc[...])

    a_spec = pl.BlockSpec((tk, tm), lambda i, j, kk: (kk, i)) if ta else pl.BlockSpec((tm, tk), lambda i, j, kk: (i, kk))
    b_spec = pl.BlockSpec((tn, tk), lambda i, j, kk: (j, kk)) if tb else pl.BlockSpec((tk, tn), lambda i, j, kk: (kk, j))
    o_spec = pl.BlockSpec((tm, tn), lambda i, j, kk: (i, j))
    in_specs = [a_spec, b_spec] + ([o_spec] if has_res else [])
    args = (a, b) + ((res,) if has_res else ())
    if pin is not None:
        in_specs.append(pl.BlockSpec(pin.shape, lambda i, j, kk: (0, 0)))
        args += (pin,)
    if has_norm:
        in_specs.append(pl.BlockSpec((1, n), lambda i, j, kk: (0, 0)))
        args += (norm_gain.reshape(1, n),)
    out = jax.ShapeDtypeStruct((m, n), out_dtype)
    return pl.pallas_call(
        body, name=name, grid=(m // tm, n // tn, nk), in_specs=in_specs,
        out_specs=[o_spec, o_spec] if has_norm else o_spec,
        out_shape=[out, jax.ShapeDtypeStruct((m, n), BF16)] if has_norm else out,
        scratch_shapes=[pltpu.VMEM((tm, tn), F32)] if nk > 1 else [],
        compiler_params=_cp("parallel", "parallel", "arbitrary"),
    )(*args)


def _mm_norm_bwd(dy, w, x, g, dres, name, pin=None):
    dys = list(dy) if isinstance(dy, (list, tuple)) else [dy]
    nq = len(dys)
    s, kq = dys[0].shape
    d = w.shape[0]
    tm, tk = min(1024 if nq == 1 else 512, s), _tile(kq, 1408)
    per = kq // tk
    nk = nq * per
    n_in = nq + 4 + (pin is not None)

    def body(*refs):
        w_ref, x_ref, g_ref, dres_ref = refs[nq:nq + 4]
        dx_ref, dg_ref = refs[n_in], refs[n_in + 1]
        i, kk = pl.program_id(0), pl.program_id(1)

        @pl.when((i == 0) & (kk == 0))
        def _():
            dg_ref[...] = jnp.zeros_like(dg_ref)

        def finish(dh):
            xv = x_ref[...]
            r = lax.rsqrt(jnp.mean(xv * xv, axis=-1, keepdims=True) + EPS)
            xn = xv * r
            dg_ref[...] += jnp.sum(dh * xn, axis=0, keepdims=True)
            dhg = dh * g_ref[...]
            dx_ref[...] = dres_ref[...] + r * (dhg - xn * jnp.mean(dhg * xn, axis=-1, keepdims=True))

        if nk == 1:
            finish(_mxu(refs[0][...], w_ref[...], 1, 1))
            return
        acc = refs[-1]
        for q in range(nq):
            @pl.when((kk >= q * per) & (kk < (q + 1) * per))
            def _(q=q):
                part = _mxu(refs[q][...], w_ref[...], 1, 1)

                @pl.when(kk == 0)
                def _():
                    acc[...] = part

                @pl.when(kk > 0)
                def _():
                    acc[...] += part

        @pl.when(kk == nk - 1)
        def _():
            finish(acc[...])

    row = pl.BlockSpec((tm, d), lambda i, kk: (i, 0))
    vec = pl.BlockSpec((1, d), lambda i, kk: (0, 0))
    in_specs = [pl.BlockSpec((tm, tk), lambda i, kk, q=q: (i, jnp.clip(kk - q * per, 0, per - 1))) for q in range(nq)]
    in_specs += [pl.BlockSpec((d, tk), lambda i, kk: (0, kk)), row, vec, row]
    args = (*dys, w, x, g.reshape(1, d), dres)
    if pin is not None:
        in_specs.append(pl.BlockSpec(pin.shape, lambda i, kk: (0, 0)))
        args += (pin,)
    return pl.pallas_call(
        body, name=name, grid=(s // tm, nk), in_specs=in_specs, out_specs=[row, vec],
        out_shape=[jax.ShapeDtypeStruct((s, d), F32), jax.ShapeDtypeStruct((1, d), F32)],
        scratch_shapes=[pltpu.VMEM((tm, d), F32)] if nk > 1 else [],
        compiler_params=_cp("arbitrary", "arbitrary"),
    )(*args)


def _norm_fwd(x, g, name):
    s, d = x.shape
    tr = min(512, s)

    def body(x_ref, g_ref, o_ref):
        xv = x_ref[...]
        r = lax.rsqrt(jnp.mean(xv * xv, axis=-1, keepdims=True) + EPS)
        o_ref[...] = (xv * r * g_ref[...]).astype(BF16)

    row = pl.BlockSpec((tr, d), lambda i: (i, 0))
    return pl.pallas_call(
        body, name=name, grid=(s // tr,), in_specs=[row, pl.BlockSpec((1, d), lambda i: (0, 0))],
        out_specs=row, out_shape=jax.ShapeDtypeStruct((s, d), BF16), compiler_params=_cp("parallel"),
    )(x, g.reshape(1, d))


def _norm_bwd(x, g, dh, dres, name):
    s, d = x.shape
    tr = min(512, s)

    def body(x_ref, g_ref, dh_ref, dres_ref, dx_ref, dg_ref):
        @pl.when(pl.program_id(0) == 0)
        def _():
            dg_ref[...] = jnp.zeros_like(dg_ref)

        xv = x_ref[...]
        r = lax.rsqrt(jnp.mean(xv * xv, axis=-1, keepdims=True) + EPS)
        xn = xv * r
        dhv = dh_ref[...].astype(F32)
        dg_ref[...] += jnp.sum(dhv * xn, axis=0, keepdims=True)
        dhg = dhv * g_ref[...]
        dx_ref[...] = dres_ref[...] + r * (dhg - xn * jnp.mean(dhg * xn, axis=-1, keepdims=True))

    row = pl.BlockSpec((tr, d), lambda i: (i, 0))
    vec = pl.BlockSpec((1, d), lambda i: (0, 0))
    return pl.pallas_call(
        body, name=name, grid=(s // tr,), in_specs=[row, vec, row, row], out_specs=[row, vec],
        out_shape=[jax.ShapeDtypeStruct((s, d), F32), jax.ShapeDtypeStruct((1, d), F32)],
        compiler_params=_cp("arbitrary"),
    )(x, g.reshape(1, d), dh, dres)


def _loss_head(x, g, tgt, name):
    s, d = x.shape
    tr = min(512, s)

    def body(x_ref, g_ref, t_ref, l_ref, dx_ref, dg_ref):
        @pl.when(pl.program_id(0) == 0)
        def _():
            dg_ref[...] = jnp.zeros_like(dg_ref)
            l_ref[...] = jnp.zeros_like(l_ref)

        xv = x_ref[...]
        r = lax.rsqrt(jnp.mean(xv * xv, axis=-1, keepdims=True) + EPS)
        xn = xv * r
        err = xn * g_ref[...] - t_ref[...]
        part = 0.5 * jnp.sum(jnp.mean(err * err, axis=-1, keepdims=True), axis=0, keepdims=True)
        l_ref[...] += jnp.broadcast_to(part, l_ref.shape)
        dy = err * (1.0 / d)
        dg_ref[...] += jnp.sum(dy * xn, axis=0, keepdims=True)
        dyg = dy * g_ref[...]
        dx_ref[...] = r * (dyg - xn * jnp.mean(dyg * xn, axis=-1, keepdims=True))

    row = pl.BlockSpec((tr, d), lambda i: (i, 0))
    vec = pl.BlockSpec((1, d), lambda i: (0, 0))
    return pl.pallas_call(
        body, name=name, grid=(s // tr,), in_specs=[row, vec, row],
        out_specs=[pl.BlockSpec((1, LANES), lambda i: (0, 0)), row, vec],
        out_shape=[jax.ShapeDtypeStruct((1, LANES), F32), jax.ShapeDtypeStruct((s, d), F32),
                   jax.ShapeDtypeStruct((1, d), F32)],
        compiler_params=_cp("arbitrary"),
    )(x, g.reshape(1, d), tgt)


def _sum_slots(landed_slot, own):
    me = _slot(*_mesh_pos())
    mine = own.astype(F32)
    g = jnp.where(me == 0, mine, landed_slot(0).astype(F32))
    for i in range(1, N_DEV):
        g = g + jnp.where(me == i, mine, landed_slot(i).astype(F32))
    return g


def _adam_update(g, w, m, v):
    mm = ADAM_B1 * m + (1.0 - ADAM_B1) * g
    vv = ADAM_B2 * v + (1.0 - ADAM_B2) * (g * g)
    m_hat = mm / (1.0 - ADAM_B1 ** ADAM_STEP)
    v_hat = vv / (1.0 - ADAM_B2 ** ADAM_STEP)
    return g, -ADAM_LR * (m_hat / (jnp.sqrt(v_hat) + ADAM_EPS) + ADAM_WD * w), mm, vv


def _adamw_rows(landed, own, ws, ms, vs, name):
    k = len(ws)
    sizes = [w.shape[1] for w in ws]

    def body(*refs):
        p_ref, o_ref = refs[:2]
        w_refs, m_refs, v_refs = refs[2:2 + k], refs[2 + k:2 + 2 * k], refs[2 + 2 * k:2 + 3 * k]
        outs = refs[2 + 3 * k:]
        for i, n in enumerate(sizes):
            g = _sum_slots(lambda s: p_ref[s, i:i + 1, :n], o_ref[i:i + 1, :n])
            res = _adam_update(g, w_refs[i][...], m_refs[i][...], v_refs[i][...])
            for j in range(4):
                outs[j * k + i][...] = res[j]

    return pl.pallas_call(
        body, name=name, out_shape=[jax.ShapeDtypeStruct((1, n), F32) for _ in range(4) for n in sizes],
    )(landed, own, *ws, *ms, *vs)


def _adamw(landed, own, w, m, v, name):
    r, c = w.shape
    cap = max(8, 256 * 1024 // c)
    tr = max(t for t in range(8, min(r, cap) + 1, 8) if r % t == 0) if r % 8 == 0 else r
    gathered = own is None

    def body(*refs):
        p_ref = refs[0]
        w_ref, m_ref, v_ref, g_ref, d_ref, nm_ref, nv_ref = refs[1 if gathered else 2:]
        if gathered:
            g = p_ref[0].astype(F32)
            for i in range(1, N_DEV):
                g = g + p_ref[i].astype(F32)
        else:
            g = _sum_slots(lambda i: p_ref[i], refs[1][...])
        g_ref[...], d_ref[...], nm_ref[...], nv_ref[...] = _adam_update(g, w_ref[...], m_ref[...], v_ref[...])

    blk = pl.BlockSpec((tr, c), lambda i: (i, 0))
    n_blk = 3 if gathered else 4
    return pl.pallas_call(
        body, name=name, grid=(r // tr,),
        in_specs=[pl.BlockSpec((N_DEV, tr, c), lambda i: (0, i, 0))] + [blk] * n_blk,
        out_specs=[blk] * 4, out_shape=[jax.ShapeDtypeStruct((r, c), F32)] * 4,
        compiler_params=_cp("parallel"),
    )(*((landed,) if gathered else (landed, own)), w, m, v)


def _conv_taps(x, kw):
    return [_shift_down(x, kw - 1 - j) for j in range(kw - 1)] + [x]


def _conv_fwd(taps, w_ref):
    acc = w_ref[0:1, :] * taps[0]
    for j in range(1, len(taps)):
        acc = acc + w_ref[j:j + 1, :] * taps[j]
    return acc


def _conv_bwd(taps, dy, w_ref, dw_ref):
    kw = len(taps)
    dx = w_ref[kw - 1:kw, :] * dy
    for j in range(kw):
        dw_ref[j:j + 1, :] = jnp.sum(dy * taps[j], axis=0, keepdims=True)
        if j < kw - 1:
            dx = dx + w_ref[j:j + 1, :] * _shift_up(dy, kw - 1 - j)
    return dx


def _ffn_act_fwd(pre, cw, name):
    s, f2 = pre.shape
    nt = f2 // 2 // LANES

    def body(pu_ref, pg_ref, wu_ref, wg_ref, o_ref):
        up = _conv_fwd(_conv_taps(pu_ref[...].astype(F32), FFN_CONV), wu_ref)
        gate = _conv_fwd(_conv_taps(pg_ref[...].astype(F32), FFN_CONV), wg_ref)
        o_ref[...] = (gate * _sigmoid(gate) * up).astype(BF16)

    def col(rows, off):
        return pl.BlockSpec((rows, LANES), lambda j: (0, j + off))

    return pl.pallas_call(
        body, name=name, grid=(nt,),
        in_specs=[col(s, 0), col(s, nt), col(FFN_CONV, 0), col(FFN_CONV, nt)], out_specs=col(s, 0),
        out_shape=jax.ShapeDtypeStruct((s, f2 // 2), BF16), compiler_params=_cp("parallel"),
    )(pre, pre, cw, cw)


def _ffn_act_bwd(pre, cw, dact, name):
    s, f2 = pre.shape
    f = f2 // 2
    nt = f // LANES

    def body(pu_ref, pg_ref, wu_ref, wg_ref, da_ref, dpu_ref, dpg_ref, dwu_ref, dwg_ref):
        pu, pg = pu_ref[...].astype(F32), pg_ref[...].astype(F32)
        tu, tg = _conv_taps(pu, FFN_CONV), _conv_taps(pg, FFN_CONV)
        up = _conv_fwd(tu, wu_ref)
        gate = _conv_fwd(tg, wg_ref)
        sg = _sigmoid(gate)
        da = da_ref[...].astype(F32)
        dup = da * gate * sg
        dgate = da * up * (sg * (1.0 + gate * (1.0 - sg)))
        dpu_ref[...] = _conv_bwd(tu, dup, wu_ref, dwu_ref).astype(BF16)
        dpg_ref[...] = _conv_bwd(tg, dgate, wg_ref, dwg_ref).astype(BF16)

    def col(rows, off):
        return pl.BlockSpec((rows, LANES), lambda j: (0, j + off))

    return pl.pallas_call(
        body, name=name, grid=(nt,),
        in_specs=[col(s, 0), col(s, nt), col(FFN_CONV, 0), col(FFN_CONV, nt), col(s, 0)],
        out_specs=[col(s, 0), col(s, 0), col(FFN_CONV, 0), col(FFN_CONV, 0)],
        out_shape=[jax.ShapeDtypeStruct((s, f), BF16), jax.ShapeDtypeStruct((s, f), BF16),
                   jax.ShapeDtypeStruct((FFN_CONV, f), F32), jax.ShapeDtypeStruct((FFN_CONV, f), F32)],
        compiler_params=_cp("parallel"),
    )(pre, pre, cw, cw, dact)


def _xa_probs(qh, kh):
    sc = _mxu(qh, kh, 1, 1) * (XA_DH ** -0.5)
    e = jnp.exp(sc - jnp.max(sc, axis=-1, keepdims=True))
    return e / jnp.sum(e, axis=-1, keepdims=True)


def _xattn_fwd(q, kv, name):
    s, d = q.shape
    m = kv.shape[0]
    tr = min(512, s)

    def body(q_ref, kv_ref, o_ref):
        for h in range(XA_HEADS):
            lo, hi = h * XA_DH, (h + 1) * XA_DH
            p = _xa_probs(q_ref[:, lo:hi], kv_ref[:, lo:hi])
            o_ref[:, lo:hi] = _mxu(p, kv_ref[:, d + lo:d + hi]).astype(BF16)

    row = pl.BlockSpec((tr, d), lambda i: (i, 0))
    return pl.pallas_call(
        body, name=name, grid=(s // tr,), in_specs=[row, pl.BlockSpec((m, 2 * d), lambda i: (0, 0))],
        out_specs=row, out_shape=jax.ShapeDtypeStruct((s, d), BF16), compiler_params=_cp("parallel"),
    )(q, kv)


def _xattn_bwd(q, kv, do, name):
    s, d = q.shape
    m = kv.shape[0]
    tr = min(512, s)

    def body(q_ref, kv_ref, do_ref, dq_ref, dkv_ref):
        @pl.when(pl.program_id(0) == 0)
        def _():
            dkv_ref[...] = jnp.zeros_like(dkv_ref)

        for h in range(XA_HEADS):
            lo, hi = h * XA_DH, (h + 1) * XA_DH
            qh, kh, vh = q_ref[:, lo:hi], kv_ref[:, lo:hi], kv_ref[:, d + lo:d + hi]
            doh = do_ref[:, lo:hi]
            p = _xa_probs(qh, kh)
            dp = _mxu(doh, vh, 1, 1)
            ds = p * (dp - jnp.sum(p * dp, axis=-1, keepdims=True)) * (XA_DH ** -0.5)
            dq_ref[:, lo:hi] = _mxu(ds, kh).astype(BF16)
            dkv_ref[:, lo:hi] += _mxu(ds, qh, 0, 0)
            dkv_ref[:, d + lo:d + hi] += _mxu(p, doh, 0, 0)

    row = pl.BlockSpec((tr, d), lambda i: (i, 0))
    full = pl.BlockSpec((m, 2 * d), lambda i: (0, 0))
    return pl.pallas_call(
        body, name=name, grid=(s // tr,), in_specs=[row, full, row], out_specs=[row, full],
        out_shape=[jax.ShapeDtypeStruct((s, d), BF16), jax.ShapeDtypeStruct((m, 2 * d), F32)],
        compiler_params=_cp("arbitrary"),
    )(q, kv, do)


def _ret_tables():
    c = RET_CHUNK
    lg = np.log1p(-np.exp2(-5.0 - np.arange(RET_HEADS, dtype=np.float32))).astype(np.float32)
    idx = np.arange(c, dtype=np.float32)
    diff = idx[:, None] - idx[None, :]
    intra = np.where(diff >= 0, np.exp(lg[:, None, None] * np.where(diff >= 0, diff, 0.0)), 0.0)
    rk = np.broadcast_to(np.exp(lg[:, None] * (c - 1 - idx))[:, :, None], (RET_HEADS, c, LANES))
    rq = np.broadcast_to(np.exp(lg[:, None] * (idx + 1))[:, :, None], (RET_HEADS, c, LANES))
    return jnp.asarray(np.stack([intra, rk, rq], axis=1).astype(np.float32))


def _rope_tables(s):
    half = RET_DH // 2
    inv = jnp.exp(-math.log(10000.0) * jnp.arange(half, dtype=F32) / half)
    ang = jnp.arange(s, dtype=F32)[:, None] * inv[None, :]
    cos, sin = jnp.cos(ang), jnp.sin(ang)
    return jnp.concatenate([cos, cos], axis=1), jnp.concatenate([-sin, sin], axis=1)


def _ret_specs(n_of):
    c, w = RET_CHUNK, RET_HEADS * RET_DH

    def part(off):
        return pl.BlockSpec((c, w), lambda n: (n_of(n), off))

    pos = pl.BlockSpec((c, RET_DH), lambda n: (n_of(n), 0))
    gain = pl.BlockSpec((1, w), lambda n: (0, 0))
    tab = pl.BlockSpec((RET_HEADS, 3, c, LANES), lambda n: (0, 0, 0, 0))
    st = pl.BlockSpec((RET_HEADS, None, RET_DH, RET_DH), lambda n: (0, n_of(n), 0, 0))
    return part, pos, gain, tab, st


def _rheads(x):
    return jnp.stack([x[:, h * RET_DH:(h + 1) * RET_DH] for h in range(RET_HEADS)], axis=0)


def _runheads(x):
    return jnp.concatenate([x[h] for h in range(RET_HEADS)], axis=1)


def _rope(x, cos, sin):
    return x * cos + pltpu.roll(x, RET_DH // 2, 2) * sin


def _ret_chunk(q_ref, k_ref, v_ref, cos_ref, sin_ref, tab_ref, prev):
    cos, sin = cos_ref[...], sin_ref[...]
    q = _rope(_rheads(q_ref[...]), cos, sin)
    k = _rope(_rheads(k_ref[...]), cos, sin) * (RET_DH ** -0.5)
    v = _rheads(v_ref[...])
    scores = _bmxu(q, k, 2, 2) * tab_ref[:, 0]
    qdec = q * tab_ref[:, 2]
    kdec = k * tab_ref[:, 1]
    o = _bmxu(scores, v) + _bmxu(qdec, prev)
    return q, k, v, scores, qdec, kdec, o


def _ret_fwd(proj, cos, sin, gain, name):
    s = proj.shape[0]
    c = RET_CHUNK
    nc = s // c
    part, pos, gvec, tab, st = _ret_specs(lambda n: n)

    def body(q_ref, k_ref, v_ref, g_ref, cos_ref, sin_ref, rn_ref, tab_ref, o_ref, st_ref, state):
        @pl.when(pl.program_id(0) == 0)
        def _():
            state[...] = jnp.zeros_like(state)

        prev = state[...]
        st_ref[...] = prev
        _, _, v, _, _, kdec, o = _ret_chunk(q_ref, k_ref, v_ref, cos_ref, sin_ref, tab_ref, prev)
        state[...] = prev * tab_ref[:, 2, c - 1:c, :] + _bmxu(kdec, v, 1, 1)
        r = lax.rsqrt(jnp.mean(o * o, axis=-1, keepdims=True) + EPS)
        gate = g_ref[...]
        o_ref[...] = (_runheads(o * r) * rn_ref[...] * (gate * _sigmoid(gate))).astype(BF16)

    return pl.pallas_call(
        body, name=name, grid=(nc,),
        in_specs=[part(0), part(1), part(2), part(3), pos, pos, gvec, tab],
        out_specs=[part(0), st],
        out_shape=[jax.ShapeDtypeStruct((s, 2 * RET_HEADS * RET_DH), BF16),
                   jax.ShapeDtypeStruct((RET_HEADS, nc, RET_DH, RET_DH), F32)],
        scratch_shapes=[pltpu.VMEM((RET_HEADS, RET_DH, RET_DH), F32)],
        compiler_params=_cp("arbitrary"),
    )(proj, proj, proj, proj, cos, sin, gain.reshape(1, -1), _ret_tables())


def _ret_bwd(proj, cos, sin, gain, states, dmerged, name):
    s = proj.shape[0]
    c = RET_CHUNK
    nc = s // c
    width = RET_HEADS * RET_DH
    part, pos, gvec, tab, st = _ret_specs(lambda n: nc - 1 - n)

    def body(q_ref, k_ref, v_ref, g_ref, cos_ref, sin_ref, rn_ref, tab_ref, st_ref, do_ref,
             dp_ref, drn_ref, carry):
        @pl.when(pl.program_id(0) == 0)
        def _():
            carry[...] = jnp.zeros_like(carry)
            drn_ref[...] = jnp.zeros_like(drn_ref)

        prev = st_ref[...]
        q, k, v, scores, qdec, kdec, o = _ret_chunk(q_ref, k_ref, v_ref, cos_ref, sin_ref, tab_ref, prev)
        r = lax.rsqrt(jnp.mean(o * o, axis=-1, keepdims=True) + EPS)
        on = o * r
        on2 = _runheads(on)
        gate = g_ref[...]
        sg = _sigmoid(gate)
        sil = gate * sg
        dout = do_ref[...]
        rn = rn_ref[...]
        dp_ref[:, 3 * width:] = (dout * on2 * rn * (sg * (1.0 + gate * (1.0 - sg)))).astype(BF16)
        drn_ref[...] += jnp.sum(dout * on2 * sil, axis=0, keepdims=True)
        don = _rheads(dout * rn * sil)
        do = r * (don - on * jnp.mean(don * on, axis=-1, keepdims=True))
        dc = carry[...]
        dsc = _bmxu(do, v, 2, 2) * tab_ref[:, 0]
        dq = _bmxu(dsc, k) + _bmxu(do, prev, 2, 2) * tab_ref[:, 2]
        dk = _bmxu(dsc, q, 1, 1) + _bmxu(v, dc, 2, 2) * tab_ref[:, 1]
        dv = _bmxu(scores, do, 1, 1) + _bmxu(kdec, dc)
        carry[...] = _bmxu(qdec, do, 1, 1) + dc * tab_ref[:, 2, c - 1:c, :]
        cos, sin = cos_ref[...], sin_ref[...]
        dk = dk * (RET_DH ** -0.5)
        dp_ref[:, :width] = _runheads(dq * cos + pltpu.roll(dq * sin, RET_DH // 2, 2)).astype(BF16)
        dp_ref[:, width:2 * width] = _runheads(dk * cos + pltpu.roll(dk * sin, RET_DH // 2, 2)).astype(BF16)
        dp_ref[:, 2 * width:3 * width] = _runheads(dv).astype(BF16)

    return pl.pallas_call(
        body, name=name, grid=(nc,),
        in_specs=[part(0), part(1), part(2), part(3), pos, pos, gvec, tab, st, part(0)],
        out_specs=[pl.BlockSpec((c, 4 * width), lambda n: (nc - 1 - n, 0)), gvec],
        out_shape=[jax.ShapeDtypeStruct(proj.shape, BF16), jax.ShapeDtypeStruct((1, width), F32)],
        scratch_shapes=[pltpu.VMEM((RET_HEADS, RET_DH, RET_DH), F32)],
        compiler_params=_cp("arbitrary"),
    )(proj, proj, proj, proj, cos, sin, gain.reshape(1, -1), _ret_tables(), states, dmerged)


S5_TILE = 512


def _cmul_add(xr, xi, ar, ai, yr, yi):
    return xr + ar * yr - ai * yi, xi + ar * yi + ai * yr


def _s5_pow_tables(a_il, name):
    r = SCAN_ROWS
    t = S5_TILE
    w2 = a_il.shape[1]

    def body(a_ref, up_ref, dn_ref):
        for j in range(w2 // (2 * t)):
            re, im = pl.ds(2 * t * j, t), pl.ds(2 * t * j + t, t)
            up_ref[0:1, re] = a_ref[:, re]
            up_ref[0:1, im] = a_ref[:, im]
            dn_ref[r - 1:r, re] = a_ref[:, re]
            dn_ref[r - 1:r, im] = -a_ref[:, im]
            n = 1
            while n < r:
                lr, li = up_ref[n - 1:n, re], up_ref[n - 1:n, im]
                xr, xi = up_ref[0:n, re], up_ref[0:n, im]
                up_ref[n:2 * n, re] = xr * lr - xi * li
                up_ref[n:2 * n, im] = xr * li + xi * lr
                yr, yi = dn_ref[r - n:r, re], dn_ref[r - n:r, im]
                dn_ref[r - 2 * n:r - n, re] = yr * lr + yi * li
                dn_ref[r - 2 * n:r - n, im] = yi * lr - yr * li
                n *= 2

    return pl.pallas_call(
        body, name=name, out_shape=[jax.ShapeDtypeStruct((r, w2), F32)] * 2, compiler_params=_cp(),
    )(a_il)


def _s5_scan_fwd(bu, apow, name):
    s, w2 = bu.shape
    r = SCAN_ROWS
    t = S5_TILE
    steps = r.bit_length() - 1

    def body(b_ref, p_ref, o_ref, cr, ci):
        @pl.when(pl.program_id(1) == 0)
        def _():
            cr[...] = jnp.zeros_like(cr)
            ci[...] = jnp.zeros_like(ci)

        xr, xi = b_ref[:, :t], b_ref[:, t:]
        for k in range(steps):
            sh = 1 << k
            xr, xi = _cmul_add(xr, xi, p_ref[sh - 1:sh, :t], p_ref[sh - 1:sh, t:],
                               _shift_down(xr, sh), _shift_down(xi, sh))
        xr, xi = _cmul_add(xr, xi, p_ref[:, :t], p_ref[:, t:], cr[...], ci[...])
        o_ref[:, :t] = xr
        o_ref[:, t:] = xi
        cr[...] = xr[r - 1:r, :]
        ci[...] = xi[r - 1:r, :]

    blk = pl.BlockSpec((r, 2 * t), lambda j, i: (i, j))
    return pl.pallas_call(
        body, name=name, grid=(w2 // (2 * t), s // r),
        in_specs=[blk, pl.BlockSpec((r, 2 * t), lambda j, i: (0, j))], out_specs=blk,
        out_shape=jax.ShapeDtypeStruct((s, w2), F32),
        scratch_shapes=[pltpu.VMEM((1, t), F32), pltpu.VMEM((1, t), F32)],
        compiler_params=_cp("parallel", "arbitrary"),
    )(bu, apow)


def _s5_scan_bwd(dst, apow_rev, st, name):
    s, w2 = dst.shape
    r = SCAN_ROWS
    t = S5_TILE
    nb = s // r
    steps = r.bit_length() - 1

    def body(d_ref, p_ref, s_ref, sp_ref, g_ref, da_ref, cr, ci):
        i = pl.program_id(1)

        @pl.when(i == 0)
        def _():
            cr[...] = jnp.zeros_like(cr)
            ci[...] = jnp.zeros_like(ci)
            da_ref[...] = jnp.zeros_like(da_ref)

        xr, xi = d_ref[:, :t], d_ref[:, t:]
        for k in range(steps):
            sh = 1 << k
            xr, xi = _cmul_add(xr, xi, p_ref[r - sh:r - sh + 1, :t], p_ref[r - sh:r - sh + 1, t:],
                               _shift_up(xr, sh), _shift_up(xi, sh))
        xr, xi = _cmul_add(xr, xi, p_ref[:, :t], p_ref[:, t:], cr[...], ci[...])
        g_ref[:, :t] = xr.astype(BF16)
        g_ref[:, t:] = xi.astype(BF16)
        cr[...] = xr[0:1, :]
        ci[...] = xi[0:1, :]
        first = i == nb - 1
        row = lax.broadcasted_iota(jnp.int32, (r, t), 0)
        last_r = jnp.where(first, 0.0, sp_ref[7:8, :t])
        last_i = jnp.where(first, 0.0, sp_ref[7:8, t:])
        pr = jnp.where(row == 0, last_r, pltpu.roll(s_ref[:, :t], 1, 0))
        pi = jnp.where(row == 0, last_i, pltpu.roll(s_ref[:, t:], 1, 0))
        da_ref[:, :t] += jnp.sum(xr * pr + xi * pi, axis=0, keepdims=True)
        da_ref[:, t:] += jnp.sum(xi * pr - xr * pi, axis=0, keepdims=True)

    blk = pl.BlockSpec((r, 2 * t), lambda j, i: (nb - 1 - i, j))
    halo = pl.BlockSpec((8, 2 * t), lambda j, i: (jnp.maximum((nb - 1 - i) * (r // 8) - 1, 0), j))
    vec = pl.BlockSpec((1, 2 * t), lambda j, i: (0, j))
    return pl.pallas_call(
        body, name=name, grid=(w2 // (2 * t), nb),
        in_specs=[blk, pl.BlockSpec((r, 2 * t), lambda j, i: (0, j)), blk, halo], out_specs=[blk, vec],
        out_shape=[jax.ShapeDtypeStruct((s, w2), BF16), jax.ShapeDtypeStruct((1, w2), F32)],
        scratch_shapes=[pltpu.VMEM((1, t), F32), pltpu.VMEM((1, t), F32)],
        compiler_params=_cp("parallel", "arbitrary"),
    )(dst, apow_rev, st, st)


_GELU_C = math.sqrt(2.0 / math.pi)
_GELU_A = 0.044715


def _gelu(y):
    return 0.5 * y * (1.0 + jnp.tanh(_GELU_C * (y + _GELU_A * y * y * y)))


def _gelu_grad(y):
    th = jnp.tanh(_GELU_C * (y + _GELU_A * y * y * y))
    return 0.5 * (1.0 + th) + 0.5 * y * (1.0 - th * th) * _GELU_C * (1.0 + 3.0 * _GELU_A * y * y)


def _rows_shift(x, k, axis, up):
    n = x.shape[axis]
    idx = lax.broadcasted_iota(jnp.int32, x.shape, axis)
    if up:
        return jnp.where(idx < n - k, pltpu.roll(x, n - k, axis), 0.0)
    return jnp.where(idx >= k, pltpu.roll(x, k, axis), 0.0)


def _scan_block(xr, xi, pr, pi, cr, ci, rev):
    r, w = xr.shape
    nt = r // 8
    x3r, x3i = xr.reshape(nt, 8, w), xi.reshape(nt, 8, w)
    p3r, p3i = pr.reshape(nt, 8, w), pi.reshape(nt, 8, w)

    def power(rows):
        t = r - rows if rev else rows - 1
        return pr[t:t + 1, :], pi[t:t + 1, :]

    tile_row = lax.broadcasted_iota(jnp.int32, (8, w), 0)
    for sh in (1, 2, 4):
        ar, ai = power(sh)
        keep = tile_row < 8 - sh if rev else tile_row >= sh
        mr, mi = jnp.where(keep, ar, 0.0)[None], jnp.where(keep, ai, 0.0)[None]
        turn = 8 - sh if rev else sh
        x3r, x3i = _cmul_add(x3r, x3i, mr, mi, pltpu.roll(x3r, turn, 1), pltpu.roll(x3i, turn, 1))
    edge = 0 if rev else 7
    lr, li = x3r[:, edge, :], x3i[:, edge, :]
    sh = 1
    while sh < nt:
        ar, ai = power(8 * sh)
        lr, li = _cmul_add(lr, li, ar, ai, _rows_shift(lr, sh, 0, rev), _rows_shift(li, sh, 0, rev))
        sh *= 2
    tr_, ti_ = p3r[:, edge, :], p3i[:, edge, :]
    first = lax.broadcasted_iota(jnp.int32, (nt, w), 0) == (nt - 1 if rev else 0)
    wr = jnp.where(first, 1.0, _rows_shift(tr_, 1, 0, rev))
    wi = jnp.where(first, 0.0, _rows_shift(ti_, 1, 0, rev))
    er, ei = _cmul_add(_rows_shift(lr, 1, 0, rev), _rows_shift(li, 1, 0, rev), wr, wi, cr, ci)
    a8r, a8i = (p3r[nt - 1], p3i[nt - 1]) if rev else (p3r[0], p3i[0])
    x3r, x3i = _cmul_add(x3r, x3i, a8r[None], a8i[None], er[:, None, :], ei[:, None, :])
    outr, outi = x3r.reshape(r, w), x3i.reshape(r, w)
    last = 0 if rev else r - 1
    return outr, outi, outr[last:last + 1, :], outi[last:last + 1, :]


def _s5_tile_specs(n_of, r):
    t = S5_TILE
    ucol = 4 * RET_HEADS * RET_DH // LANES
    u = pl.BlockSpec((r, LANES), lambda j, i: (n_of(i), ucol + j))
    col = pl.BlockSpec((r, LANES), lambda j, i: (n_of(i), j))
    state = pl.BlockSpec((r, 2 * t), lambda j, i: (n_of(i), j))
    table = pl.BlockSpec((r, 2 * t), lambda j, i: (0, j))
    bbt = pl.BlockSpec((None, LANES, 2 * t), lambda j, i: (j, 0, 0))
    cct = pl.BlockSpec((None, 2 * t, LANES), lambda j, i: (j, 0, 0))
    vec = pl.BlockSpec((1, LANES), lambda j, i: (0, j))
    return u, col, state, table, bbt, cct, vec


def _s5_fwd(proj, bbt, cct, apow, dvec, name):
    s = proj.shape[0]
    r, t = SCAN_ROWS, S5_TILE
    w = S5_GROUPS * S5_GROUP
    u_s, col, state, table, bb_s, cc_s, vec = _s5_tile_specs(lambda i: i, r)

    def body(u_ref, bb_ref, cc_ref, p_ref, d_ref, st_ref, y_ref, g_ref, cr, ci):
        @pl.when(pl.program_id(1) == 0)
        def _():
            cr[...] = jnp.zeros_like(cr)
            ci[...] = jnp.zeros_like(ci)

        u = u_ref[...]
        bu = _mxu(u, bb_ref[...])
        xr, xi, cr[...], ci[...] = _scan_block(bu[:, :t], bu[:, t:], p_ref[:, :t], p_ref[:, t:], cr[...], ci[...], False)
        st_ref[:, :t] = xr
        st_ref[:, t:] = xi
        y = _mxu(xr, cc_ref[:t, :]) + _mxu(xi, cc_ref[t:, :]) + d_ref[...] * u
        y_ref[...] = y
        g_ref[...] = _gelu(y).astype(BF16)

    return pl.pallas_call(
        body, name=name, grid=(2 * S5_GROUPS * S5_STATE // (2 * t), s // r),
        in_specs=[u_s, bb_s, cc_s, table, vec], out_specs=[state, col, col],
        out_shape=[jax.ShapeDtypeStruct((s, 2 * S5_GROUPS * S5_STATE), F32), jax.ShapeDtypeStruct((s, w), F32),
                   jax.ShapeDtypeStruct((s, w), BF16)],
        scratch_shapes=[pltpu.VMEM((1, t), F32), pltpu.VMEM((1, t), F32)],
        compiler_params=_cp("parallel", "arbitrary"),
    )(proj, bbt, cct, apow, dvec)


def _s5_bwd(dg1, dg2, y, proj, st, bbt, cct, apow_rev, dvec, dproj, name):
    s = proj.shape[0]
    r, t = SCAN_ROWS, S5_TILE
    nb = s // r
    w = S5_GROUPS * S5_GROUP
    u_s, col, state, table, bb_s, cc_s, vec = _s5_tile_specs(lambda i: nb - 1 - i, r)
    halo = pl.BlockSpec((8, 2 * t), lambda j, i: (jnp.maximum((nb - 1 - i) * (r // 8) - 1, 0), j))
    acc = pl.BlockSpec((1, 2 * t), lambda j, i: (0, j))

    def body(a_ref, b_ref, y_ref, u_ref, s_ref, sp_ref, bb_ref, cc_ref, p_ref, d_ref, _,
             du_ref, da_ref, dbb_ref, dcc_ref, dd_ref, cr, ci):
        i = pl.program_id(1)

        @pl.when(i == 0)
        def _():
            cr[...] = jnp.zeros_like(cr)
            ci[...] = jnp.zeros_like(ci)
            da_ref[...] = jnp.zeros_like(da_ref)
            dbb_ref[...] = jnp.zeros_like(dbb_ref)
            dcc_ref[...] = jnp.zeros_like(dcc_ref)
            dd_ref[...] = jnp.zeros_like(dd_ref)

        u = u_ref[...]
        dy = (a_ref[...] + b_ref[...]) * _gelu_grad(y_ref[...])
        dd_ref[...] += jnp.sum(dy * u, axis=0, keepdims=True)
        sr, si = s_ref[:, :t], s_ref[:, t:]
        dcc_ref[:t, :] += _mxu(sr, dy, 0, 0)
        dcc_ref[t:, :] += _mxu(si, dy, 0, 0)
        xr, xi, cr[...], ci[...] = _scan_block(_mxu(dy, cc_ref[:t, :], 1, 1), _mxu(dy, cc_ref[t:, :], 1, 1),
                                               p_ref[:, :t], p_ref[:, t:], cr[...], ci[...], True)
        du_ref[...] = (dy * d_ref[...] + _mxu(xr, bb_ref[:, :t], 1, 1) + _mxu(xi, bb_ref[:, t:], 1, 1)).astype(BF16)
        dbb_ref[:, :t] += _mxu(u, xr, 0, 0)
        dbb_ref[:, t:] += _mxu(u, xi, 0, 0)
        first = i == nb - 1
        row = lax.broadcasted_iota(jnp.int32, (r, t), 0)
        pr = jnp.where(row == 0, jnp.where(first, 0.0, sp_ref[7:8, :t]), pltpu.roll(sr, 1, 0))
        pi = jnp.where(row == 0, jnp.where(first, 0.0, sp_ref[7:8, t:]), pltpu.roll(si, 1, 0))
        da_ref[:, :t] += jnp.sum(xr * pr + xi * pi, axis=0, keepdims=True)
        da_ref[:, t:] += jnp.sum(xi * pr - xr * pi, axis=0, keepdims=True)

    return pl.pallas_call(
        body, name=name, grid=(2 * S5_GROUPS * S5_STATE // (2 * t), nb),
        in_specs=[col, col, col, u_s, state, halo, bb_s, cc_s, table, vec, pl.BlockSpec(memory_space=pl.ANY)],
        out_specs=[u_s, acc, bb_s, cc_s, vec],
        out_shape=[jax.ShapeDtypeStruct(dproj.shape, dproj.dtype), jax.ShapeDtypeStruct((1, 2 * S5_GROUPS * S5_STATE), F32),
                   jax.ShapeDtypeStruct(bbt.shape, F32), jax.ShapeDtypeStruct(cct.shape, F32),
                   jax.ShapeDtypeStruct((1, w), F32)],
        scratch_shapes=[pltpu.VMEM((1, t), F32), pltpu.VMEM((1, t), F32)],
        input_output_aliases={10: 0}, compiler_params=_cp("parallel", "arbitrary"),
    )(dg1, dg2, y, proj, st, st, bbt, cct, apow_rev, dvec, dproj)


def _s5_tile_b(b_re, b_im):
    nt = S5_GROUPS * S5_STATE // S5_TILE
    gpt = S5_GROUPS // nt
    eye = jnp.eye(gpt, dtype=F32)

    def tile(b):
        t5 = jnp.einsum("jghp,gk->jghkp", b.reshape(nt, gpt, S5_GROUP, S5_STATE), eye)
        return t5.reshape(nt, gpt * S5_GROUP, S5_TILE)

    return jnp.concatenate([tile(b_re), tile(b_im)], axis=2)


def _s5_untile_b(d):
    nt = S5_GROUPS * S5_STATE // S5_TILE
    gpt = S5_GROUPS // nt
    eye = jnp.eye(gpt, dtype=F32)

    def untile(x):
        x5 = x.reshape(nt, gpt, S5_GROUP, gpt, S5_STATE)
        return jnp.einsum("jghkp,gk->jghp", x5, eye).reshape(S5_GROUPS, S5_GROUP, S5_STATE)

    return untile(d[:, :, :S5_TILE]), untile(d[:, :, S5_TILE:])


def _s5_tile_c(c_re, c_im):
    nt = S5_GROUPS * S5_STATE // S5_TILE
    gpt = S5_GROUPS // nt
    eye = jnp.eye(gpt, dtype=F32)

    def tile(c):
        t5 = jnp.einsum("jgph,gk->jkpgh", c.reshape(nt, gpt, S5_STATE, S5_GROUP), eye)
        return t5.reshape(nt, S5_TILE, gpt * S5_GROUP)

    return jnp.concatenate([tile(c_re), -tile(c_im)], axis=1)


def _s5_untile_c(d):
    nt = S5_GROUPS * S5_STATE // S5_TILE
    gpt = S5_GROUPS // nt
    eye = jnp.eye(gpt, dtype=F32)

    def untile(x):
        x5 = x.reshape(nt, gpt, S5_STATE, gpt, S5_GROUP)
        return jnp.einsum("jkpgh,gk->jgph", x5, eye).reshape(S5_GROUPS, S5_STATE, S5_GROUP)

    return untile(d[:, :S5_TILE, :]), -untile(d[:, S5_TILE:, :])


def _row_call(body, name, s, ins, outs, acc=False):
    tr = min(512, s)

    def spec(width, cb, rows):
        if rows == 1:
            return pl.BlockSpec((1, width), lambda i: (0, cb))
        return pl.BlockSpec((tr, width), lambda i: (i, cb))

    in_specs = [spec(w, cb, a.shape[0]) for a, w, cb in ins]
    out_specs = [spec(w, cb, sd.shape[0]) for sd, w, cb in outs]
    return pl.pallas_call(
        body, name=name, grid=(s // tr,), in_specs=in_specs, out_specs=out_specs,
        out_shape=[sd for sd, _, _ in outs],
        compiler_params=_cp("arbitrary" if acc else "parallel"),
    )(*[a for a, _, _ in ins])


def _sds(shape, dtype):
    return jax.ShapeDtypeStruct(shape, dtype)


def _s5_gelu_fwd(yraw, proj, dvec, name):
    s, w = yraw.shape

    def body(y_ref, u_ref, d_ref, yo_ref, g_ref):
        y = y_ref[...] + d_ref[...] * u_ref[...]
        yo_ref[...] = y
        g_ref[...] = _gelu(y).astype(BF16)

    return _row_call(body, name, s, [(yraw, w, 0), (proj, w, 4), (dvec, w, 0)],
                     [(_sds((s, w), F32), w, 0), (_sds((s, w), BF16), w, 0)])


def _s5_glu_fwd(y, z, b, merged, name):
    s, w = y.shape
    tr = min(512, s)

    def body(y_ref, z_ref, b_ref, _, o_ref):
        o_ref[...] = (_gelu(y_ref[...]) * _sigmoid(z_ref[...] + b_ref[...])).astype(BF16)

    row = pl.BlockSpec((tr, w), lambda i: (i, 0))
    return pl.pallas_call(
        body, name=name, grid=(s // tr,),
        in_specs=[row, row, pl.BlockSpec((1, w), lambda i: (0, 0)), pl.BlockSpec(memory_space=pl.ANY)],
        out_specs=pl.BlockSpec((tr, w), lambda i: (i, 1)),
        out_shape=jax.ShapeDtypeStruct(merged.shape, merged.dtype),
        input_output_aliases={3: 0}, compiler_params=_cp("parallel"),
    )(y, z, b, merged)


def _s5_glu_bwd(dmerged, y, z, b, name):
    s, w = y.shape

    def body(do_ref, y_ref, z_ref, b_ref, dz_ref, dg_ref, db_ref):
        @pl.when(pl.program_id(0) == 0)
        def _():
            db_ref[...] = jnp.zeros_like(db_ref)

        g = _gelu(y_ref[...])
        sg = _sigmoid(z_ref[...] + b_ref[...])
        dout = do_ref[...]
        dz = dout * g * sg * (1.0 - sg)
        dz_ref[...] = dz.astype(BF16)
        dg_ref[...] = dout * sg
        db_ref[...] += jnp.sum(dz, axis=0, keepdims=True)

    return _row_call(body, name, s, [(dmerged, w, 1), (y, w, 0), (z, w, 0), (b, w, 0)],
                     [(_sds((s, w), BF16), w, 0), (_sds((s, w), F32), w, 0), (_sds((1, w), F32), w, 0)], acc=True)


def _s5_gelu_bwd(dg1, dg2, y, proj, dvec, name):
    s, w = y.shape

    def body(a_ref, b_ref, y_ref, u_ref, d_ref, dy_ref, du_ref, dd_ref):
        @pl.when(pl.program_id(0) == 0)
        def _():
            dd_ref[...] = jnp.zeros_like(dd_ref)

        dy = (a_ref[...] + b_ref[...]) * _gelu_grad(y_ref[...])
        dy_ref[...] = dy.astype(BF16)
        du_ref[...] = dy * d_ref[...]
        dd_ref[...] += jnp.sum(dy * u_ref[...], axis=0, keepdims=True)

    return _row_call(body, name, s, [(dg1, w, 0), (dg2, w, 0), (y, w, 0), (proj, w, 4), (dvec, w, 0)],
                     [(_sds((s, w), BF16), w, 0), (_sds((s, w), F32), w, 0), (_sds((1, w), F32), w, 0)], acc=True)


def _gdn_conv_fwd(projx, cw, name):
    s = projx.shape[0]
    nh = GDN_HEADS

    def body(x_ref, w_ref, o_ref):
        j = pl.program_id(0)
        cv = _conv_fwd(_conv_taps(x_ref[...], GDN_CONV), w_ref)
        y = cv * _sigmoid(cv)
        nrm = y * lax.rsqrt(jnp.sum(y * y, axis=-1, keepdims=True) + EPS)
        o_ref[...] = jnp.where(j < nh, nrm * (GDN_DH ** -0.5), jnp.where(j < 2 * nh, nrm, y))

    return pl.pallas_call(
        body, name=name, grid=(3 * nh,),
        in_specs=[pl.BlockSpec((s, GDN_DH), lambda j: (0, j)), pl.BlockSpec((GDN_CONV, GDN_DH), lambda j: (0, j))],
        out_specs=pl.BlockSpec((s, GDN_DH), lambda j: (0, j)),
        out_shape=jax.ShapeDtypeStruct((s, 3 * nh * GDN_DH), F32), compiler_params=_cp("parallel"),
    )(projx, cw)


def _gdn_conv_bwd(projx, cw, dqkv, dprojx, name):
    s = projx.shape[0]
    nh = GDN_HEADS

    def body(x_ref, w_ref, d_ref, _, dx_ref, dw_ref):
        j = pl.program_id(0)
        x = x_ref[...]
        taps = _conv_taps(x, GDN_CONV)
        cv = _conv_fwd(taps, w_ref)
        sg = _sigmoid(cv)
        y = cv * sg
        rinv = lax.rsqrt(jnp.sum(y * y, axis=-1, keepdims=True) + EPS)
        nrm = y * rinv
        dn = d_ref[...]
        dns = jnp.where(j < nh, dn * (GDN_DH ** -0.5), dn)
        dyn = rinv * (dns - nrm * jnp.sum(dns * nrm, axis=-1, keepdims=True))
        dy = jnp.where(j < 2 * nh, dyn, dn)
        dc = dy * (sg * (1.0 + cv * (1.0 - sg)))
        dx_ref[...] = _conv_bwd(taps, dc, w_ref, dw_ref).astype(BF16)

    col = pl.BlockSpec((s, GDN_DH), lambda j: (0, j))
    wcol = pl.BlockSpec((GDN_CONV, GDN_DH), lambda j: (0, j))
    return pl.pallas_call(
        body, name=name, grid=(3 * nh,), in_specs=[col, wcol, col, pl.BlockSpec(memory_space=pl.ANY)],
        out_specs=[col, wcol],
        out_shape=[jax.ShapeDtypeStruct(dprojx.shape, dprojx.dtype), jax.ShapeDtypeStruct((GDN_CONV, 3 * nh * GDN_DH), F32)],
        input_output_aliases={3: 0}, compiler_params=_cp("parallel"),
    )(projx, cw, dqkv, dprojx)


def _softplus(x):
    return jnp.maximum(x, 0.0) + jnp.log1p(jnp.exp(-jnp.abs(x)))


def _gdn_gates_fwd(projx, alog, dtb, name):
    s = projx.shape[0]
    w = GDN_HEADS * GDN_DH
    tr = min(512, s)

    def body(t_ref, al_ref, dt_ref, bo_ref, go_ref):
        t = t_ref[...]
        for h in range(GDN_HEADS):
            lo, hi = h * GDN_DH, (h + 1) * GDN_DH
            b = jnp.broadcast_to(t[:, h:h + 1], (tr, GDN_DH))
            a = jnp.broadcast_to(t[:, GDN_HEADS + h:GDN_HEADS + h + 1], (tr, GDN_DH))
            bo_ref[:, lo:hi] = _sigmoid(b)
            go_ref[:, lo:hi] = -jnp.exp(al_ref[:, lo:hi]) * _softplus(a + dt_ref[:, lo:hi])

    row = pl.BlockSpec((tr, w), lambda i: (i, 0))
    vec = pl.BlockSpec((1, w), lambda i: (0, 0))
    return pl.pallas_call(
        body, name=name, grid=(s // tr,),
        in_specs=[pl.BlockSpec((tr, LANES), lambda i: (i, 4 * w // LANES)), vec, vec], out_specs=[row, row],
        out_shape=[jax.ShapeDtypeStruct((s, w), F32)] * 2, compiler_params=_cp("parallel"),
    )(projx, alog, dtb)


def _gdn_gates_bwd(projx, alog, dtb, dbeta, dg, dprojx, name):
    s = projx.shape[0]
    w = GDN_HEADS * GDN_DH
    tr = min(512, s)
    gate_blk = 4 * w // LANES

    def body(t_ref, al_ref, dt_ref, dbe_ref, dg_ref, _, o_ref, dal_ref, ddt_ref):
        @pl.when(pl.program_id(0) == 0)
        def _():
            dal_ref[...] = jnp.zeros_like(dal_ref)
            ddt_ref[...] = jnp.zeros_like(ddt_ref)

        t = t_ref[...]
        lane = lax.broadcasted_iota(jnp.int32, (tr, LANES), 1)
        lane1 = lax.broadcasted_iota(jnp.int32, (1, LANES), 1)
        out = jnp.zeros((tr, LANES), F32)
        dal = jnp.zeros((1, LANES), F32)
        ddt = jnp.zeros((1, LANES), F32)
        for h in range(GDN_HEADS):
            lo, hi = h * GDN_DH, (h + 1) * GDN_DH
            beta = _sigmoid(t[:, h:h + 1])
            pb = jnp.sum(dbe_ref[:, lo:hi], axis=-1, keepdims=True)
            db = pb * beta * (1.0 - beta)
            xa = t[:, GDN_HEADS + h:GDN_HEADS + h + 1] + dt_ref[:, lo:lo + 1]
            ea = -jnp.exp(al_ref[:, lo:lo + 1])
            pg = jnp.sum(dg_ref[:, lo:hi], axis=-1, keepdims=True)
            da = pg * ea * _sigmoid(xa)
            out = jnp.where(lane == h, db, jnp.where(lane == GDN_HEADS + h, da, out))
            dal = jnp.where(lane1 == h, jnp.sum(pg * ea * _softplus(xa), axis=0, keepdims=True), dal)
            ddt = jnp.where(lane1 == h, jnp.sum(da, axis=0, keepdims=True), ddt)
        o_ref[...] = out.astype(BF16)
        dal_ref[...] += dal
        ddt_ref[...] += ddt

    row = pl.BlockSpec((tr, w), lambda i: (i, 0))
    vec = pl.BlockSpec((1, w), lambda i: (0, 0))
    small = pl.BlockSpec((1, LANES), lambda i: (0, 0))
    gates = pl.BlockSpec((tr, LANES), lambda i: (i, gate_blk))
    return pl.pallas_call(
        body, name=name, grid=(s // tr,),
        in_specs=[gates, vec, vec, row, row, pl.BlockSpec(memory_space=pl.ANY)],
        out_specs=[gates, small, small],
        out_shape=[jax.ShapeDtypeStruct(dprojx.shape, dprojx.dtype), jax.ShapeDtypeStruct((1, LANES), F32),
                   jax.ShapeDtypeStruct((1, LANES), F32)],
        input_output_aliases={5: 0}, compiler_params=_cp("arbitrary"),
    )(projx, alog, dtb, dbeta, dg, dprojx)


def _gdn_tri():
    c = GDN_CHUNK
    i = lax.broadcasted_iota(jnp.int32, (c, c), 0)
    j = lax.broadcasted_iota(jnp.int32, (c, c), 1)
    return ((i >= j).astype(F32), (i <= j).astype(F32), i >= j, i > j, (i == j).astype(F32))


def _bdot(a, b, ca=2, cb=1, precision=None):
    return lax.dot_general(a, b, (((ca,), (cb,)), ((0,), (0,))), precision=precision, preferred_element_type=F32)


def _bmxu(a, b, ca=2, cb=1):
    return _bdot(a.astype(BF16), b.astype(BF16), ca, cb)


def _split(x):
    hi = x.astype(BF16)
    return hi, (x - hi.astype(F32)).astype(BF16)


def _bdot3(a, b, ca=2, cb=1):
    ah, al = _split(a)
    bh, bl = _split(b)
    return _bdot(ah, bh, ca, cb) + (_bdot(ah, bl, ca, cb) + _bdot(al, bh, ca, cb))


def _tri_dot(tri, x):
    t = tri.astype(BF16)
    hi = x.astype(BF16)
    r1 = x - hi.astype(F32)
    mid = r1.astype(BF16)
    lo = (r1 - mid.astype(F32)).astype(BF16)
    return _dot(t, hi) + (_dot(t, mid) + _dot(t, lo))


def _heads(x):
    return jnp.stack([x[:, h * GDN_DH:(h + 1) * GDN_DH] for h in range(GDN_HEADS)], axis=0)


def _unheads(x):
    return jnp.concatenate([x[h] for h in range(GDN_HEADS)], axis=1)


def _gdn_chunk(q, k, v, bb, g2d, tri):
    low, up, incl, strict, eye = tri
    c = GDN_CHUNK
    gc = _heads(_tri_dot(low, g2d))
    gci = gc[:, :, :c]
    gdiff = gci - jnp.swapaxes(gci, 1, 2)
    decay = jnp.where(incl, jnp.exp(jnp.where(incl, gdiff, 0.0)), 0.0)
    kb, vb = k * bb, v * bb
    kbk = _bmxu(kb, k, 2, 2)
    x = -jnp.where(strict, kbk * decay, 0.0)
    t = eye + x
    p = x
    for _ in range(c.bit_length() - 2):
        p = _bdot3(p, p)
        t = t + _bdot3(t, p)
    eg = jnp.exp(gc)
    kbg = kb * eg
    gcl = gc[:, c - 1:c, :]
    ek = jnp.exp(gcl - gc)
    qkraw = _bmxu(q, k, 2, 2)
    return dict(decay=decay, kb=kb, vb=vb, kbk=kbk, t=t, eg=eg, kbg=kbg, ek=ek, gl=jnp.exp(gcl),
                w=_bmxu(t, kbg), u=_bmxu(t, vb), qkraw=qkraw, qk=jnp.where(incl, qkraw * decay, 0.0),
                qd=q * eg, kd=k * ek)


def _gdn_specs(n_of):
    c, w = GDN_CHUNK, GDN_HEADS * GDN_DH

    def blk(cb, width=w):
        return pl.BlockSpec((c, width), lambda n: (n_of(n), cb))

    st = pl.BlockSpec((None, GDN_HEADS, GDN_DH, GDN_DH), lambda n: (n_of(n), 0, 0, 0))
    vec = pl.BlockSpec((1, GDN_DH), lambda n: (0, 0))
    return blk, st, vec


def _gdn_load(qkv_ref, b_ref, g_ref, tri):
    w = GDN_HEADS * GDN_DH
    q, k, v = _heads(qkv_ref[:, :w]), _heads(qkv_ref[:, w:2 * w]), _heads(qkv_ref[:, 2 * w:])
    bb = _heads(b_ref[...])
    return q, k, v, bb, _gdn_chunk(q, k, v, bb, g_ref[...], tri)


def _gdn_fwd(qkv, beta, g, projx, onorm, name):
    s = qkv.shape[0]
    nc = s // GDN_CHUNK
    w = GDN_HEADS * GDN_DH
    blk, st, vec = _gdn_specs(lambda n: n)

    def body(qkv_ref, b_ref, g_ref, z_ref, on_ref, o_ref, st_ref, state):
        @pl.when(pl.program_id(0) == 0)
        def _():
            state[...] = jnp.zeros_like(state)

        _, _, _, _, ch = _gdn_load(qkv_ref, b_ref, g_ref, _gdn_tri())
        sp = state[...]
        st_ref[...] = sp
        vn = ch["u"] - _bmxu(ch["w"], sp)
        o = _bmxu(ch["qd"], sp) + _bmxu(ch["qk"], vn)
        state[...] = sp * ch["gl"] + _bmxu(ch["kd"], vn, 1, 1)
        r = lax.rsqrt(jnp.mean(o * o, axis=-1, keepdims=True) + EPS)
        z = _heads(z_ref[...])
        o_ref[...] = _unheads(o * r * on_ref[...] * (z * _sigmoid(z))).astype(BF16)

    return pl.pallas_call(
        body, name=name, grid=(nc,),
        in_specs=[blk(0, 3 * w), blk(0), blk(0), blk(3), vec], out_specs=[blk(0), st],
        out_shape=[jax.ShapeDtypeStruct((s, w), BF16), jax.ShapeDtypeStruct((nc, GDN_HEADS, GDN_DH, GDN_DH), F32)],
        scratch_shapes=[pltpu.VMEM((GDN_HEADS, GDN_DH, GDN_DH), F32)],
        compiler_params=_cp("arbitrary"),
    )(qkv, beta, g, projx, onorm.reshape(1, -1))


def _gdn_bwd(qkv, beta, g, projx, onorm, states, dout, name):
    s = qkv.shape[0]
    c = GDN_CHUNK
    nc = s // c
    w = GDN_HEADS * GDN_DH
    blk, st, vec = _gdn_specs(lambda n: nc - 1 - n)

    def body(qkv_ref, b_ref, g_ref, z_ref, on_ref, st_ref, do_ref,
             dqkv_ref, db_ref, dg_ref, dz_ref, don_ref, carry):
        @pl.when(pl.program_id(0) == 0)
        def _():
            carry[...] = jnp.zeros_like(carry)
            don_ref[...] = jnp.zeros_like(don_ref)

        tri = _gdn_tri()
        low, up, incl, strict, eye = tri
        q, k, v, bb, ch = _gdn_load(qkv_ref, b_ref, g_ref, tri)
        sp = st_ref[...]
        vn = ch["u"] - _bmxu(ch["w"], sp)
        o = _bmxu(ch["qd"], sp) + _bmxu(ch["qk"], vn)
        r = lax.rsqrt(jnp.mean(o * o, axis=-1, keepdims=True) + EPS)
        orn = o * r
        z = _heads(z_ref[...])
        sg = _sigmoid(z)
        dout = _heads(do_ref[...])
        onw = on_ref[...]
        dz_ref[...] = _unheads(dout * orn * onw * (sg * (1.0 + z * (1.0 - sg)))).astype(BF16)
        don = dout * (z * sg)
        don_ref[...] += jnp.sum(jnp.sum(don * orn, axis=0), axis=0, keepdims=True)
        dor = don * onw
        do = r * (dor - orn * jnp.mean(dor * orn, axis=-1, keepdims=True))
        dsn = carry[...]
        dqd = _bmxu(do, sp, 2, 2)
        dqk = jnp.where(incl, _bmxu(do, vn, 2, 2), 0.0)
        dvn = _bmxu(ch["qk"], do, 1, 1) + _bmxu(ch["kd"], dsn)
        dkd = _bmxu(vn, dsn, 2, 2)
        dgl = jnp.sum(dsn * sp, axis=1, keepdims=True)
        dw = -_bmxu(dvn, sp, 2, 2)
        carry[...] = _bmxu(ch["qd"], do, 1, 1) + dsn * ch["gl"] - _bmxu(ch["w"], dvn, 1, 1)
        t = ch["t"]
        dvb = _bmxu(t, dvn, 1, 1)
        dkbg = _bmxu(t, dw, 1, 1)
        dt = _bmxu(dvn, ch["vb"], 2, 2) + _bmxu(dw, ch["kbg"], 2, 2)
        da = -_bdot3(_bdot3(t, dt, 1, 1), t, 2, 2)
        da = jnp.where(strict, da, 0.0)
        decay = ch["decay"]
        dkbk = da * decay
        dqkr = dqk * decay
        mdec = (da * ch["kbk"] + dqk * ch["qkraw"]) * decay
        dkb = _bmxu(dkbk, k) + dkbg * ch["eg"]
        dk = _bmxu(dkbk, ch["kb"], 1, 1) + _bmxu(dqkr, q, 1, 1) + dkd * ch["ek"] + dkb * bb
        dq = _bmxu(dqkr, k) + dqd * ch["eg"]
        tk = dkd * ch["kd"]
        dgcl = jnp.sum(tk, axis=1, keepdims=True) + dgl * ch["gl"]
        row = lax.broadcasted_iota(jnp.int32, (GDN_HEADS, c, GDN_DH), 1)
        zpad = jnp.zeros((GDN_HEADS, c, GDN_DH - c), F32)
        dgc = (jnp.concatenate([mdec, zpad], axis=2) - jnp.concatenate([jnp.swapaxes(mdec, 1, 2), zpad], axis=2)
               + dqd * ch["qd"] - tk + dkbg * ch["kbg"] + jnp.where(row == c - 1, dgcl, 0.0))
        dqkv_ref[:, :w] = _unheads(dq)
        dqkv_ref[:, w:2 * w] = _unheads(dk)
        dqkv_ref[:, 2 * w:] = _unheads(dvb * bb)
        db_ref[...] = _unheads(dkb * k + dvb * v)
        dg_ref[...] = _tri_dot(up, _unheads(dgc))

    return pl.pallas_call(
        body, name=name, grid=(nc,),
        in_specs=[blk(0, 3 * w), blk(0), blk(0), blk(3), vec, st, blk(0)],
        out_specs=[blk(0, 3 * w), blk(0), blk(0), blk(3), vec],
        out_shape=[jax.ShapeDtypeStruct((s, 3 * w), F32), jax.ShapeDtypeStruct((s, w), F32),
                   jax.ShapeDtypeStruct((s, w), F32), jax.ShapeDtypeStruct(projx.shape, BF16),
                   jax.ShapeDtypeStruct((1, GDN_DH), F32)],
        scratch_shapes=[pltpu.VMEM((GDN_HEADS, GDN_DH, GDN_DH), F32)],
        compiler_params=_cp("arbitrary"),
    )(qkv, beta, g, projx, onorm.reshape(1, -1), states, dout)


_WEIGHTS = (
    "l0_mix_norm", "l0_w_in", "l0_ret_norm", "l0_s5_lambda_re", "l0_s5_lambda_im", "l0_s5_b_re", "l0_s5_b_im",
    "l0_s5_c_re", "l0_s5_c_im", "l0_s5_d", "l0_s5_log_dt", "l0_s5_w_glu", "l0_s5_b_glu", "l0_w_out",
    "l0_xa_norm", "l0_mem_norm", "l0_xa_wq", "l0_xa_wkv", "l0_xa_wo", "l0_ffn_norm", "l0_ffn_w_up",
    "l0_ffn_conv", "l0_ffn_w_down", "l1_mix_norm", "l1_w_in", "l1_conv", "l1_a_log", "l1_dt_bias", "l1_o_norm",
    "l1_w_out", "l1_xa_norm", "l1_mem_norm", "l1_xa_wq", "l1_xa_wkv", "l1_xa_wo", "l1_ffn_norm", "l1_ffn_w_up",
    "l1_ffn_conv", "l1_ffn_w_down", "final_norm")
_INPUTS = ("x", "mem") + _WEIGHTS + ("loss_target",) + tuple("m_" + n for n in _WEIGHTS) + tuple("v_" + n for n in _WEIGHTS)

_COL = ("l0_w_in", "l0_xa_wkv", "l0_ffn_w_up", "l0_ffn_conv", "l1_w_in", "l1_conv", "l1_xa_wkv", "l1_ffn_w_up",
        "l1_ffn_conv")
_ROW = ("l0_s5_w_glu", "l0_w_out", "l0_xa_wq", "l0_xa_wo", "l0_ffn_w_down", "l1_w_out", "l1_xa_wq", "l1_xa_wo",
        "l1_ffn_w_down")
_F32_WIRE = ("l0_ffn_conv", "l1_conv", "l1_ffn_conv")
_REP = tuple(n for n in _WEIGHTS if n not in _COL + _ROW)
_GATHER_GROUPS = (("l0_w_in", "l0_s5_w_glu", "l0_w_out"),
                  ("l0_xa_wq", "l0_xa_wkv", "l0_xa_wo", "l0_ffn_w_up", "l0_ffn_conv", "l0_ffn_w_down"),
                  ("l1_w_in", "l1_conv", "l1_w_out", "l1_xa_wq", "l1_xa_wkv", "l1_xa_wo"),
                  ("l1_ffn_w_up", "l1_ffn_conv", "l1_ffn_w_down"))


def _round_up(n, m):
    return (n + m - 1) // m * m


_REP_BIG = ("l0_s5_lambda_re", "l0_s5_lambda_im", "l0_s5_b_re", "l0_s5_b_im", "l0_s5_c_re", "l0_s5_c_im", "l0_s5_d")
_REP_LAST = "l0_mix_norm"
_REP_SMALL = tuple(n for n in _REP if n not in _REP_BIG + (_REP_LAST,))
PACK_WIDTH = 1024


def _pack_rows(ts):
    rows = [jnp.pad(t, ((0, 0), (0, PACK_WIDTH - t.shape[1]))) for t in ts]
    rows.append(jnp.zeros((_round_up(len(ts), 8) - len(ts), PACK_WIDTH), F32))
    return jnp.concatenate(rows, axis=0)


def _s5_interleave(re, im):
    lead = re.shape[:-1]
    nt = re.shape[-1] // S5_TILE
    both = jnp.stack([re.reshape(lead + (nt, S5_TILE)), im.reshape(lead + (nt, S5_TILE))], axis=-2)
    return both.reshape(lead + (2 * re.shape[-1],))


def _s5_split(x):
    lead = x.shape[:-1]
    y = x.reshape(lead + (x.shape[-1] // (2 * S5_TILE), 2, S5_TILE))
    return y[..., 0, :].reshape(lead + (-1,)), y[..., 1, :].reshape(lead + (-1,))


def _s5_discretise(lr, li, log_dt, b_re, b_im):
    dt = jnp.exp(log_dt)[:, None]
    mag = jnp.exp(lr * dt)
    a_re = mag * jnp.cos(li * dt)
    a_im = mag * jnp.sin(li * dt)
    den = lr * lr + li * li
    z_re = ((a_re - 1.0) * lr + a_im * li) / den
    z_im = (a_im * lr - (a_re - 1.0) * li) / den
    bb_re = z_re[:, None, :] * b_re - z_im[:, None, :] * b_im
    bb_im = z_re[:, None, :] * b_im + z_im[:, None, :] * b_re
    return a_re, a_im, bb_re, bb_im


def _block_diag(b):
    g, r, c = b.shape
    return jnp.einsum("grc,gk->grkc", b, jnp.eye(g, dtype=b.dtype)).reshape(g * r, g * c)


def _block_diag_of(d, g):
    r, c = d.shape[0] // g, d.shape[1] // g
    return jnp.einsum("grkc,gk->grc", d.reshape(g, r, g, c), jnp.eye(g, dtype=d.dtype))


def kernel(*args):
    p = dict(zip(_INPUTS, args, strict=True))
    x0, mem0, tgt = p["x"][0], p["mem"][0], p["loss_target"][0]
    s, d = x0.shape
    me = _slot(*_mesh_pos())
    grads = {}
    wire = {n: (F32 if n in _F32_WIRE else BF16) for n in _COL + _ROW}

    zones = {n: _into_slot(p[n], wire[n], me, "place_" + n) for names in _GATHER_GROUPS for n in names}
    gather, pin = [], jnp.zeros((), F32)
    for i, names in enumerate(_GATHER_GROUPS):
        handle, token = _push_start([], [zones[n] for n in names], f"gather{i}_start")
        gather.append(handle)
        pin = pin + token[0, 0]
    w = {}

    def gathered(i, after):
        for n, full in zip(_GATHER_GROUPS[i], _push_wait(gather[i], after, f"gather{i}_wait")):
            if n in _COL:
                full = full.transpose(1, 0, 2)
            w[n] = full.reshape(-1, full.shape[-1]) if n in _ROW else full.reshape(full.shape[0], -1)

    pending = []

    def exchange(names, gain, tag):
        slots = []
        for n in names:
            g = grads[n]
            if n in _COL:
                pieces = g if isinstance(g, tuple) else (g,)
                g = jnp.concatenate([t.reshape(t.shape[0], -1, p[n].shape[1]).transpose(1, 0, 2) for t in pieces], axis=0)
            else:
                g = g.reshape((N_DEV, -1) + g.shape[1:])
            slots.append(g.astype(wire[n]))
        handle, token = _push_start(slots, [], tag + "_start")
        pending.append((names, slots, handle, tag))
        return gain + token[0, 0]

    def xattn(pre, x_in, hx):
        q = _mm(hx, w[pre + "xa_wq"], out_dtype=BF16, name=pre + "xa_q")
        memn = _norm_fwd(mem0, p[pre + "mem_norm"], pre + "mem_norm_fwd")
        kv = _mm(memn, w[pre + "xa_wkv"], out_dtype=BF16, name=pre + "xa_kv")
        ao = _xattn_fwd(q, kv, pre + "xattn_fwd")
        x_out, hf = _mm(ao, w[pre + "xa_wo"], res=x_in, norm_gain=p[pre + "ffn_norm"], name=pre + "xa_o")
        return x_out, hf, (x_in, hx, q, memn, kv, ao)

    def xattn_bwd(pre, saved, dxo):
        x_in, hx, q, memn, kv, ao = saved
        dao = _mm(dxo, w[pre + "xa_wo"], tb=True, name=pre + "xa_o_dx")
        grads[pre + "xa_wo"] = _mm(ao, dxo, ta=True, out_dtype=BF16, name=pre + "xa_o_dw")
        dq, dkv = _xattn_bwd(q, kv, dao, pre + "xattn_bwd")
        grads[pre + "xa_wq"] = _mm(hx, dq, ta=True, out_dtype=BF16, name=pre + "xa_q_dw")
        grads[pre + "xa_wkv"] = _mm(memn, dkv, ta=True, out_dtype=BF16, name=pre + "xa_kv_dw")
        dmemn = _mm(dkv, w[pre + "xa_wkv"], tb=True, name=pre + "xa_kv_dx")
        gain = exchange((pre + "xa_wo", pre + "xa_wq", pre + "xa_wkv"), p[pre + "xa_norm"], pre + "xa_grads")
        dx_in, grads[pre + "xa_norm"] = _mm_norm_bwd(dq, w[pre + "xa_wq"], x_in, gain, dxo, pre + "xa_q_dx")
        _, grads[pre + "mem_norm"] = _norm_bwd(mem0, p[pre + "mem_norm"], dmemn, jnp.zeros_like(mem0), pre + "mem_norm_bwd")
        return dx_in

    def ffn(pre, x_in, hf, next_gain):
        up = _mm(hf, w[pre + "ffn_w_up"], out_dtype=BF16, name=pre + "ffn_up")
        act = _ffn_act_fwd(up, w[pre + "ffn_conv"], pre + "ffn_act_fwd")
        res = _mm(act, w[pre + "ffn_w_down"], res=x_in, norm_gain=next_gain, name=pre + "ffn_down")
        x_out, h_next = res if next_gain is not None else (res, None)
        return x_out, h_next, (x_in, hf, up, act)

    def ffn_bwd(pre, saved, dxo):
        x_in, hf, up, act = saved
        dact = _mm(dxo, w[pre + "ffn_w_down"], tb=True, out_dtype=BF16, name=pre + "ffn_down_dx")
        grads[pre + "ffn_w_down"] = _mm(act, dxo, ta=True, out_dtype=BF16, name=pre + "ffn_down_dw")
        dpu, dpg, dcu, dcg = _ffn_act_bwd(up, w[pre + "ffn_conv"], dact, pre + "ffn_act_bwd")
        grads[pre + "ffn_conv"] = jnp.concatenate([dcu, dcg], axis=1)
        grads[pre + "ffn_w_up"] = (_mm(hf, dpu, ta=True, out_dtype=BF16, name=pre + "ffn_up_dw_u"),
                                   _mm(hf, dpg, ta=True, out_dtype=BF16, name=pre + "ffn_up_dw_g"))
        gain = exchange((pre + "ffn_w_down", pre + "ffn_w_up", pre + "ffn_conv"), p[pre + "ffn_norm"], pre + "ffn_grads")
        dx_in, grads[pre + "ffn_norm"] = _mm_norm_bwd([dpu, dpg], w[pre + "ffn_w_up"], x_in, gain, dxo, pre + "ffn_up_dx")
        return dx_in

    cos, sin = _rope_tables(s)
    (a_re, a_im, bb_re, bb_im), disc_vjp = jax.vjp(
        _s5_discretise, p["l0_s5_lambda_re"], p["l0_s5_lambda_im"], p["l0_s5_log_dt"], p["l0_s5_b_re"], p["l0_s5_b_im"])
    apow, apow_rev = _s5_pow_tables(_s5_interleave(a_re.reshape(1, -1), a_im.reshape(1, -1)), "l0_s5_pow_tables")
    bbt = _s5_tile_b(bb_re, bb_im).astype(BF16)
    cct = _s5_tile_c(p["l0_s5_c_re"], p["l0_s5_c_im"]).astype(BF16)
    s5_d = p["l0_s5_d"].reshape(1, -1)
    b_glu = p["l0_s5_b_glu"].reshape(1, -1)

    h0 = _norm_fwd(x0, p["l0_mix_norm"] + pin, "l0_mix_norm_fwd")
    gathered(0, h0)
    proj = _mm(h0, w["l0_w_in"], name="l0_in")
    merged, ret_states = _ret_fwd(proj, cos, sin, p["l0_ret_norm"], "l0_ret_fwd")
    st, y, gy = _s5_fwd(proj, bbt, cct, apow, s5_d, "l0_s5_fwd")
    z = _mm(gy, w["l0_s5_w_glu"], name="l0_s5_glu_mm")
    merged = _s5_glu_fwd(y, z, b_glu, merged, "l0_s5_glu_fwd")
    x1, hx0 = _mm(merged, w["l0_w_out"], res=x0, norm_gain=p["l0_xa_norm"], name="l0_out")
    gathered(1, x1)
    x2, hf0, xa0 = xattn("l0_", x1, hx0)
    x3, h1, ff0 = ffn("l0_", x2, hf0, p["l1_mix_norm"])

    gathered(2, x3)
    w1 = w["l1_w_in"]
    wx = jnp.pad(w1, ((0, 0), (0, _round_up(w1.shape[1], LANES) - w1.shape[1])))
    alog_x = jnp.repeat(p["l1_a_log"], GDN_DH).reshape(1, -1)
    dtb_x = jnp.repeat(p["l1_dt_bias"], GDN_DH).reshape(1, -1)
    projx = _mm(h1, wx, name="l1_in")
    qkv = _gdn_conv_fwd(projx, w["l1_conv"], "l1_conv_fwd")
    beta, glog = _gdn_gates_fwd(projx, alog_x, dtb_x, "l1_gates_fwd")
    o_gdn, gdn_states = _gdn_fwd(qkv, beta, glog, projx, p["l1_o_norm"], "l1_gdn_fwd")
    x4, hx1 = _mm(o_gdn, w["l1_w_out"], res=x3, norm_gain=p["l1_xa_norm"], name="l1_out")
    x5, hf1, xa1 = xattn("l1_", x4, hx1)
    gathered(3, x5)
    x6, _, ff1 = ffn("l1_", x5, hf1, None)

    loss_part, dx6, grads["final_norm"] = _loss_head(x6, p["final_norm"], tgt, "loss_head")
    loss = lax.psum(loss_part[0, 0], ("x", "y", "c"))
    dx5 = ffn_bwd("l1_", ff1, dx6)
    dx4 = xattn_bwd("l1_", xa1, dx5)

    do_gdn = _mm(dx4, w["l1_w_out"], tb=True, name="l1_out_dx")
    grads["l1_w_out"] = _mm(o_gdn, dx4, ta=True, out_dtype=BF16, name="l1_out_dw")
    dqkv, dbeta, dglog, dprojx, grads["l1_o_norm"] = _gdn_bwd(
        qkv, beta, glog, projx, p["l1_o_norm"], gdn_states, do_gdn, "l1_gdn_bwd")
    dprojx, grads["l1_conv"] = _gdn_conv_bwd(projx, w["l1_conv"], dqkv, dprojx, "l1_conv_bwd")
    dprojx, dalog_x, ddtb_x = _gdn_gates_bwd(projx, alog_x, dtb_x, dbeta, dglog, dprojx, "l1_gates_bwd")
    grads["l1_w_in"] = _mm(h1, dprojx, ta=True, out_dtype=BF16, name="l1_in_dw")[:, :w1.shape[1]]
    grads["l1_a_log"] = dalog_x[0, :GDN_HEADS]
    grads["l1_dt_bias"] = ddtb_x[0, :GDN_HEADS]
    gain = exchange(("l1_w_out", "l1_w_in", "l1_conv"), p["l1_mix_norm"], "l1_mix_grads")
    dx3, grads["l1_mix_norm"] = _mm_norm_bwd(dprojx, wx, x3, gain, dx4, "l1_in_dx")

    dx2 = ffn_bwd("l0_", ff0, dx3)
    dx1 = xattn_bwd("l0_", xa0, dx2)

    dmerged = _mm(dx1, w["l0_w_out"], tb=True, name="l0_out_dx")
    grads["l0_w_out"] = _mm(merged, dx1, ta=True, out_dtype=BF16, name="l0_out_dw")
    dproj, grads["l0_ret_norm"] = _ret_bwd(proj, cos, sin, p["l0_ret_norm"], ret_states, dmerged, "l0_ret_bwd")
    dzg, dg1, grads["l0_s5_b_glu"] = _s5_glu_bwd(dmerged, y, z, b_glu, "l0_s5_glu_bwd")
    grads["l0_s5_w_glu"] = _mm(gy, dzg, ta=True, out_dtype=BF16, name="l0_s5_glu_dw")
    s5_d_after = exchange(("l0_w_out", "l0_s5_w_glu"), s5_d, "l0_out_grads")
    dg2 = _mm(dzg, w["l0_s5_w_glu"], tb=True, name="l0_s5_glu_dx")
    dproj, da_s5, dbbt, dcct, grads["l0_s5_d"] = _s5_bwd(dg1, dg2, y, proj, st, bbt, cct, apow_rev, s5_d_after, dproj, "l0_s5_bwd")
    dbb_re, dbb_im = _s5_untile_b(dbbt)
    grads["l0_s5_c_re"], grads["l0_s5_c_im"] = _s5_untile_c(dcct)
    da_re, da_im = (t.reshape(S5_GROUPS, S5_STATE) for t in _s5_split(da_s5[0]))
    (grads["l0_s5_lambda_re"], grads["l0_s5_lambda_im"], grads["l0_s5_log_dt"], grads["l0_s5_b_re"],
     grads["l0_s5_b_im"]) = disc_vjp((da_re, da_im, dbb_re, dbb_im))

    def as_2d(t):
        return t.reshape(-1, t.shape[-1])

    def as_row(t):
        return t.reshape(1, -1)

    small_own = _pack_rows([as_row(grads[n]) for n in _REP_SMALL])
    big_own = [as_2d(grads[n].reshape(p[n].shape)) for n in _REP_BIG]
    rep_zones = [_into_slot(small_own, F32, me, "place_rep0")]
    rep_zones += [_into_slot(t.reshape(-1, LANES), BF16, me, f"place_rep{i + 1}") for i, t in enumerate(big_own)]
    rep_handle, rep_token = _push_start([], rep_zones, "rep_grads_start")

    grads["l0_w_in"] = _mm(h0, dproj, ta=True, out_dtype=BF16, pin=rep_token, name="l0_in_dw")
    gain = exchange(("l0_w_in",), p["l0_mix_norm"], "l0_mix_grads")
    dx0, grads["l0_mix_norm"] = _mm_norm_bwd(dproj, w["l0_w_in"], x0, gain, dx1, "l0_in_dx")

    last_own = _pack_rows([as_row(grads[_REP_LAST])])
    last_handle, _ = _push_start([], [_into_slot(last_own, F32, me, "place_rep_last")], "rep_last_start")
    last_land, = _push_wait(last_handle, dx0, "rep_last_wait")
    rep_lands = _push_wait(rep_handle, last_land, "rep_grads_wait")
    rep_land = rep_lands[0]

    outs = {}
    kinds = ("grad_", "delta_", "new_m_", "new_v_")
    for names, slots, handle, tag in pending:
        for n, own_slots, land in zip(names, slots, _push_wait(handle, rep_land, tag + "_wait")):
            shape = p[n].shape
            own = lax.dynamic_index_in_dim(own_slots, me, 0, keepdims=False)
            res = _adamw(land, own, *(p[pre + n].reshape(own.shape) for pre in ("", "m_", "v_")), "adamw_" + n)
            for kind, t in zip(kinds, res):
                outs[kind + n] = t.reshape(shape)
    for n, own, land in zip(_REP_BIG, big_own, rep_lands[1:]):
        res = _adamw(land.reshape((N_DEV,) + own.shape), None, *(as_2d(p[pre + n]) for pre in ("", "m_", "v_")), "adamw_" + n)
        for kind, t in zip(kinds, res):
            outs[kind + n] = t.reshape(p[n].shape)
    for names, land, own, nm in ((_REP_SMALL, rep_land, small_own, "adamw_small"), ((_REP_LAST,), last_land, last_own, "adamw_last")):
        res = _adamw_rows(land, own, *([as_row(p[pre + n]) for n in names] for pre in ("", "m_", "v_")), nm)
        for j, kind in enumerate(kinds):
            for i, n in enumerate(names):
                outs[kind + n] = res[j * len(names) + i].reshape(p[n].shape)

    return (loss, dx0[None]) + tuple(outs[kind + n] for kind in kinds for n in _WEIGHTS)
```

```python
import math

import numpy as np
import jax
import jax.numpy as jnp
from jax import lax
from jax.experimental import pallas as pl
from jax.experimental.pallas import tpu as pltpu

F32 = jnp.float32
BF16 = jnp.bfloat16
EPS = 1e-6
N_DEV = 8
LANES = 128
VMEM_LIMIT = 48 * 1024 * 1024

RET_HEADS, RET_DH, RET_CHUNK = 4, 128, 128
S5_GROUPS, S5_GROUP, S5_STATE = 32, 16, 64
GDN_HEADS, GDN_DH, GDN_CHUNK, GDN_CONV = 8, 128, 64, 4
XA_HEADS, XA_DH = 4, 256
FFN_CONV = 3
SCAN_ROWS = 512

ADAM_LR, ADAM_B1, ADAM_B2, ADAM_EPS, ADAM_WD, ADAM_STEP = 0.001, 0.9, 0.999, 1e-08, 0.01, 10


def _cp(*sem):
    return pltpu.CompilerParams(dimension_semantics=sem if sem else None, vmem_limit_bytes=VMEM_LIMIT)


def _tile(n, cap):
    if n <= cap:
        return n
    best = None
    for t in range(LANES, cap + 1, LANES):
        if n % t == 0:
            best = t
    assert best is not None, n
    return best


def _dot(a, b, ca=1, cb=0, precision=None):
    return lax.dot_general(a, b, (((ca,), (cb,)), ((), ())), precision=precision, preferred_element_type=F32)


def _mxu(a, b, ca=1, cb=0):
    return _dot(a.astype(BF16), b.astype(BF16), ca, cb)


def _sigmoid(x):
    return 0.5 * jnp.tanh(0.5 * x) + 0.5


def _shift_down(x, k):
    r = pltpu.roll(x, k, 0)
    row = lax.broadcasted_iota(jnp.int32, (8,) + x.shape[1:], 0)
    return jnp.concatenate([jnp.where(row >= k, r[:8], 0.0), r[8:]], axis=0)


def _shift_up(x, k):
    n = x.shape[0]
    r = pltpu.roll(x, n - k, 0)
    row = lax.broadcasted_iota(jnp.int32, (8,) + x.shape[1:], 0)
    return jnp.concatenate([r[:n - 8], jnp.where(row < 8 - k, r[n - 8:], 0.0)], axis=0)


def _mesh_pos():
    return lax.axis_index("x"), lax.axis_index("y"), lax.axis_index("c")


def _slot(px, py, pc):
    return 4 * px + 2 * py + pc


def _all_peers(x, y, c):
    flips = [(fx, fy, fc) for fx in (0, 1) for fy in (0, 1) for fc in (0, 1)][1:]
    return [(1 - x if fx else x, 1 - y if fy else y, 1 - c if fc else c) for fx, fy, fc in flips]


_HBM = pl.BlockSpec(memory_space=pltpu.HBM)
_SEM = pl.BlockSpec(memory_space=pltpu.SEMAPHORE)
N_PEERS = N_DEV - 1


def _push_copies(srcs, lands, send_sems, recv_sems, start):
    x, y, c = _mesh_pos()
    me = _slot(x, y, c)
    out = []
    for k, to in enumerate(_all_peers(x, y, c)):
        for a in range(len(lands)):
            src = srcs[a].at[_slot(*to)] if a < len(srcs) else lands[a].at[me]
            dst = lands[a].at[me if start else _slot(*to)]
            out.append(pltpu.make_async_remote_copy(
                src_ref=src, dst_ref=dst, send_sem=send_sems.at[a * N_PEERS + k], recv_sem=recv_sems.at[a * N_PEERS + k],
                device_id=to, device_id_type=pl.DeviceIdType.MESH))
    return out


def _into_slot(x, dtype, me, name):
    r, c = x.shape
    cap = max(16, 512 * 1024 // c)
    tr = max(t for t in range(16, min(r, cap) + 1, 16) if r % t == 0) if r % 16 == 0 else r

    def body(me_ref, x_ref, o_ref):
        o_ref[...] = x_ref[...].astype(dtype)

    return pl.pallas_call(
        body, name=name, out_shape=jax.ShapeDtypeStruct((N_DEV, r, c), dtype),
        grid_spec=pltpu.PrefetchScalarGridSpec(
            num_scalar_prefetch=1, grid=(r // tr,),
            in_specs=[pl.BlockSpec((tr, c), lambda i, me_ref: (i, 0))],
            out_specs=pl.BlockSpec((None, tr, c), lambda i, me_ref: (me_ref[0], i, 0))),
        compiler_params=_cp("parallel"),
    )(me.reshape(1).astype(jnp.int32), x)


def _push_start(scatter, gather_lands, name):
    ns, n = len(scatter), len(scatter) + len(gather_lands)
    lands = [lax.empty(a.shape, a.dtype) for a in scatter] + list(gather_lands)

    def body(*refs):
        srcs, zones = refs[:ns], refs[ns:ns + n]
        for cp in _push_copies(srcs, zones, refs[ns + n], refs[ns + n + 1], True):
            cp.start()
        refs[-1][...] = jnp.zeros((8, LANES), F32)

    hbm_in = [pltpu.with_memory_space_constraint(a, pltpu.HBM) for a in list(scatter) + lands]
    res = pl.pallas_call(
        body, name=name,
        out_shape=(pltpu.SemaphoreType.DMA((n * N_PEERS,)), pltpu.SemaphoreType.DMA((n * N_PEERS,)))
        + tuple(pltpu.HBM(a.shape, a.dtype) for a in list(scatter) + lands)
        + (jax.ShapeDtypeStruct((8, LANES), F32),),
        in_specs=[_HBM] * (ns + n),
        out_specs=(_SEM, _SEM) + (_HBM,) * (ns + n) + (pl.BlockSpec(memory_space=pltpu.VMEM),),
        input_output_aliases={i: 2 + i for i in range(ns + n)},
        compiler_params=pltpu.CompilerParams(has_side_effects=pltpu.SideEffectType.DATAFLOW_SIDE_EFFECTING),
    )(*hbm_in)
    return (res[0], res[1], res[2:2 + ns], res[2 + ns:2 + ns + n]), res[-1]


def _push_wait(handle, after, name):
    send_sems, recv_sems, srcs, lands = handle
    ns, n = len(srcs), len(lands)

    def body(*refs):
        for cp in _push_copies(refs[:ns], refs[ns:ns + n], refs[ns + n], refs[ns + n + 1], False):
            cp.wait_send()
            cp.wait_recv()

    res = pl.pallas_call(
        body, name=name,
        out_shape=tuple(pltpu.HBM(a.shape, a.dtype) for a in list(srcs) + list(lands)),
        in_specs=[_HBM] * (ns + n) + [_SEM, _SEM, pl.BlockSpec(memory_space=pl.ANY)],
        out_specs=(_HBM,) * (ns + n),
        input_output_aliases={i: i for i in range(ns + n)},
        compiler_params=pltpu.CompilerParams(has_side_effects=pltpu.SideEffectType.DATAFLOW_SIDE_EFFECTING),
    )(*srcs, *lands, send_sems, recv_sems, after)
    return res[ns:]


def _mm(a, b, *, ta=False, tb=False, out_dtype=F32, res=None, pin=None, norm_gain=None, name="mm"):
    m, k = (a.shape[1], a.shape[0]) if ta else a.shape
    n = b.shape[0] if tb else b.shape[1]
    assert k == (b.shape[1] if tb else b.shape[0]), (a.shape, b.shape, ta, tb)
    tm, tn, tk = _tile(m, 1408), _tile(n, 1536), _tile(k, 1408)
    nk = k // tk
    has_res = res is not None
    has_norm = norm_gain is not None
    assert not has_norm or tn == n
    n_in = 2 + has_res + (pin is not None) + has_norm

    def body(*refs):
        a_ref, b_ref = refs[:2]
        r_ref = refs[2] if has_res else None
        o_ref = refs[n_in]
        part = _mxu(a_ref[...], b_ref[...], 0 if ta else 1, 1 if tb else 0)

        def finish(r):
            if has_res:
                r = r + r_ref[...].astype(F32)
            o_ref[...] = r.astype(out_dtype)
            if has_norm:
                scale = lax.rsqrt(jnp.mean(r * r, axis=-1, keepdims=True) + EPS)
                refs[n_in + 1][...] = (r * scale * refs[n_in - 1][...]).astype(BF16)

        if nk == 1:
            finish(part)
            return
        acc = refs[-1]
        kk = pl.program_id(2)

        @pl.when(kk == 0)
        def _():
            acc[...] = part

        @pl.when(kk > 0)
        def _():
            acc[...] += part

        @pl.when(kk == nk - 1)
        def _():
            finish(acc[...])

    a_spec = pl.BlockSpec((tk, tm), lambda i, j, kk: (kk, i)) if ta else pl.BlockSpec((tm, tk), lambda i, j, kk: (i, kk))
    b_spec = pl.BlockSpec((tn, tk), lambda i, j, kk: (j, kk)) if tb else pl.BlockSpec((tk, tn), lambda i, j, kk: (kk, j))
    o_spec = pl.BlockSpec((tm, tn), lambda i, j, kk: (i, j))
    in_specs = [a_spec, b_spec] + ([o_spec] if has_res else [])
    args = (a, b) + ((res,) if has_res else ())
    if pin is not None:
        in_specs.append(pl.BlockSpec(pin.shape, lambda i, j, kk: (0, 0)))
        args += (pin,)
    if has_norm:
        in_specs.append(pl.BlockSpec((1, n), lambda i, j, kk: (0, 0)))
        args += (norm_gain.reshape(1, n),)
    out = jax.ShapeDtypeStruct((m, n), out_dtype)
    return pl.pallas_call(
        body, name=name, grid=(m // tm, n // tn, nk), in_specs=in_specs,
        out_specs=[o_spec, o_spec] if has_norm else o_spec,
        out_shape=[out, jax.ShapeDtypeStruct((m, n), BF16)] if has_norm else out,
        scratch_shapes=[pltpu.VMEM((tm, tn), F32)] if nk > 1 else [],
        compiler_params=_cp("parallel", "parallel", "arbitrary"),
    )(*args)


def _mm_norm_bwd(dy, w, x, g, dres, name, pin=None):
    dys = list(dy) if isinstance(dy, (list, tuple)) else [dy]
    nq = len(dys)
    s, kq = dys[0].shape
    d = w.shape[0]
    tm, tk = min(1024 if nq == 1 else 512, s), _tile(kq, 1408)
    per = kq // tk
    nk = nq * per
    n_in = nq + 4 + (pin is not None)

    def body(*refs):
        w_ref, x_ref, g_ref, dres_ref = refs[nq:nq + 4]
        dx_ref, dg_ref = refs[n_in], refs[n_in + 1]
        i, kk = pl.program_id(0), pl.program_id(1)

        @pl.when((i == 0) & (kk == 0))
        def _():
            dg_ref[...] = jnp.zeros_like(dg_ref)

        def finish(dh):
            xv = x_ref[...]
            r = lax.rsqrt(jnp.mean(xv * xv, axis=-1, keepdims=True) + EPS)
            xn = xv * r
            dg_ref[...] += jnp.sum(dh * xn, axis=0, keepdims=True)
            dhg = dh * g_ref[...]
            dx_ref[...] = dres_ref[...] + r * (dhg - xn * jnp.mean(dhg * xn, axis=-1, keepdims=True))

        if nk == 1:
            finish(_mxu(refs[0][...], w_ref[...], 1, 1))
            return
        acc = refs[-1]
        for q in range(nq):
            @pl.when((kk >= q * per) & (kk < (q + 1) * per))
            def _(q=q):
                part = _mxu(refs[q][...], w_ref[...], 1, 1)

                @pl.when(kk == 0)
                def _():
                    acc[...] = part

                @pl.when(kk > 0)
                def _():
                    acc[...] += part

        @pl.when(kk == nk - 1)
        def _():
            finish(acc[...])

    row = pl.BlockSpec((tm, d), lambda i, kk: (i, 0))
    vec = pl.BlockSpec((1, d), lambda i, kk: (0, 0))
    in_specs = [pl.BlockSpec((tm, tk), lambda i, kk, q=q: (i, jnp.clip(kk - q * per, 0, per - 1))) for q in range(nq)]
    in_specs += [pl.BlockSpec((d, tk), lambda i, kk: (0, kk)), row, vec, row]
    args = (*dys, w, x, g.reshape(1, d), dres)
    if pin is not None:
        in_specs.append(pl.BlockSpec(pin.shape, lambda i, kk: (0, 0)))
        args += (pin,)
    return pl.pallas_call(
        body, name=name, grid=(s // tm, nk), in_specs=in_specs, out_specs=[row, vec],
        out_shape=[jax.ShapeDtypeStruct((s, d), F32), jax.ShapeDtypeStruct((1, d), F32)],
        scratch_shapes=[pltpu.VMEM((tm, d), F32)] if nk > 1 else [],
        compiler_params=_cp("arbitrary", "arbitrary"),
    )(*args)


def _norm_fwd(x, g, name):
    s, d = x.shape
    tr = min(512, s)

    def body(x_ref, g_ref, o_ref):
        xv = x_ref[...]
        r = lax.rsqrt(jnp.mean(xv * xv, axis=-1, keepdims=True) + EPS)
        o_ref[...] = (xv * r * g_ref[...]).astype(BF16)

    row = pl.BlockSpec((tr, d), lambda i: (i, 0))
    return pl.pallas_call(
        body, name=name, grid=(s // tr,), in_specs=[row, pl.BlockSpec((1, d), lambda i: (0, 0))],
        out_specs=row, out_shape=jax.ShapeDtypeStruct((s, d), BF16), compiler_params=_cp("parallel"),
    )(x, g.reshape(1, d))


def _norm_bwd(x, g, dh, dres, name):
    s, d = x.shape
    tr = min(512, s)

    def body(x_ref, g_ref, dh_ref, dres_ref, dx_ref, dg_ref):
        @pl.when(pl.program_id(0) == 0)
        def _():
            dg_ref[...] = jnp.zeros_like(dg_ref)

        xv = x_ref[...]
        r = lax.rsqrt(jnp.mean(xv * xv, axis=-1, keepdims=True) + EPS)
        xn = xv * r
        dhv = dh_ref[...].astype(F32)
        dg_ref[...] += jnp.sum(dhv * xn, axis=0, keepdims=True)
        dhg = dhv * g_ref[...]
        dx_ref[...] = dres_ref[...] + r * (dhg - xn * jnp.mean(dhg * xn, axis=-1, keepdims=True))

    row = pl.BlockSpec((tr, d), lambda i: (i, 0))
    vec = pl.BlockSpec((1, d), lambda i: (0, 0))
    return pl.pallas_call(
        body, name=name, grid=(s // tr,), in_specs=[row, vec, row, row], out_specs=[row, vec],
        out_shape=[jax.ShapeDtypeStruct((s, d), F32), jax.ShapeDtypeStruct((1, d), F32)],
        compiler_params=_cp("arbitrary"),
    )(x, g.reshape(1, d), dh, dres)


def _loss_head(x, g, tgt, name):
    s, d = x.shape
    tr = min(512, s)

    def body(x_ref, g_ref, t_ref, l_ref, dx_ref, dg_ref):
        @pl.when(pl.program_id(0) == 0)
        def _():
            dg_ref[...] = jnp.zeros_like(dg_ref)
            l_ref[...] = jnp.zeros_like(l_ref)

        xv = x_ref[...]
        r = lax.rsqrt(jnp.mean(xv * xv, axis=-1, keepdims=True) + EPS)
        xn = xv * r
        err = xn * g_ref[...] - t_ref[...]
        part = 0.5 * jnp.sum(jnp.mean(err * err, axis=-1, keepdims=True), axis=0, keepdims=True)
        l_ref[...] += jnp.broadcast_to(part, l_ref.shape)
        dy = err * (1.0 / d)
        dg_ref[...] += jnp.sum(dy * xn, axis=0, keepdims=True)
        dyg = dy * g_ref[...]
        dx_ref[...] = r * (dyg - xn * jnp.mean(dyg * xn, axis=-1, keepdims=True))

    row = pl.BlockSpec((tr, d), lambda i: (i, 0))
    vec = pl.BlockSpec((1, d), lambda i: (0, 0))
    return pl.pallas_call(
        body, name=name, grid=(s // tr,), in_specs=[row, vec, row],
        out_specs=[pl.BlockSpec((1, LANES), lambda i: (0, 0)), row, vec],
        out_shape=[jax.ShapeDtypeStruct((1, LANES), F32), jax.ShapeDtypeStruct((s, d), F32),
                   jax.ShapeDtypeStruct((1, d), F32)],
        compiler_params=_cp("arbitrary"),
    )(x, g.reshape(1, d), tgt)


def _sum_slots(landed_slot, own):
    me = _slot(*_mesh_pos())
    mine = own.astype(F32)
    g = jnp.where(me == 0, mine, landed_slot(0).astype(F32))
    for i in range(1, N_DEV):
        g = g + jnp.where(me == i, mine, landed_slot(i).astype(F32))
    return g


def _adam_update(g, w, m, v):
    mm = ADAM_B1 * m + (1.0 - ADAM_B1) * g
    vv = ADAM_B2 * v + (1.0 - ADAM_B2) * (g * g)
    m_hat = mm / (1.0 - ADAM_B1 ** ADAM_STEP)
    v_hat = vv / (1.0 - ADAM_B2 ** ADAM_STEP)
    return g, -ADAM_LR * (m_hat / (jnp.sqrt(v_hat) + ADAM_EPS) + ADAM_WD * w), mm, vv


def _adamw_rows(landed, own, ws, ms, vs, name):
    k = len(ws)
    sizes = [w.shape[1] for w in ws]

    def body(*refs):
        p_ref, o_ref = refs[:2]
        w_refs, m_refs, v_refs = refs[2:2 + k], refs[2 + k:2 + 2 * k], refs[2 + 2 * k:2 + 3 * k]
        outs = refs[2 + 3 * k:]
        for i, n in enumerate(sizes):
            g = _sum_slots(lambda s: p_ref[s, i:i + 1, :n], o_ref[i:i + 1, :n])
            res = _adam_update(g, w_refs[i][...], m_refs[i][...], v_refs[i][...])
            for j in range(4):
                outs[j * k + i][...] = res[j]

    return pl.pallas_call(
        body, name=name, out_shape=[jax.ShapeDtypeStruct((1, n), F32) for _ in range(4) for n in sizes],
    )(landed, own, *ws, *ms, *vs)


def _adamw(landed, own, w, m, v, name):
    r, c = w.shape
    cap = max(8, 256 * 1024 // c)
    tr = max(t for t in range(8, min(r, cap) + 1, 8) if r % t == 0) if r % 8 == 0 else r
    gathered = own is None

    def body(*refs):
        p_ref = refs[0]
        w_ref, m_ref, v_ref, g_ref, d_ref, nm_ref, nv_ref = refs[1 if gathered else 2:]
        if gathered:
            g = p_ref[0].astype(F32)
            for i in range(1, N_DEV):
                g = g + p_ref[i].astype(F32)
        else:
            g = _sum_slots(lambda i: p_ref[i], refs[1][...])
        g_ref[...], d_ref[...], nm_ref[...], nv_ref[...] = _adam_update(g, w_ref[...], m_ref[...], v_ref[...])

    blk = pl.BlockSpec((tr, c), lambda i: (i, 0))
    n_blk = 3 if gathered else 4
    return pl.pallas_call(
        body, name=name, grid=(r // tr,),
        in_specs=[pl.BlockSpec((N_DEV, tr, c), lambda i: (0, i, 0))] + [blk] * n_blk,
        out_specs=[blk] * 4, out_shape=[jax.ShapeDtypeStruct((r, c), F32)] * 4,
        compiler_params=_cp("parallel"),
    )(*((landed,) if gathered else (landed, own)), w, m, v)


def _conv_taps(x, kw):
    return [_shift_down(x, kw - 1 - j) for j in range(kw - 1)] + [x]


def _conv_fwd(taps, w_ref):
    acc = w_ref[0:1, :] * taps[0]
    for j in range(1, len(taps)):
        acc = acc + w_ref[j:j + 1, :] * taps[j]
    return acc


def _conv_bwd(taps, dy, w_ref, dw_ref):
    kw = len(taps)
    dx = w_ref[kw - 1:kw, :] * dy
    for j in range(kw):
        dw_ref[j:j + 1, :] = jnp.sum(dy * taps[j], axis=0, keepdims=True)
        if j < kw - 1:
            dx = dx + w_ref[j:j + 1, :] * _shift_up(dy, kw - 1 - j)
    return dx


def _ffn_act_fwd(pre, cw, name):
    s, f2 = pre.shape
    nt = f2 // 2 // LANES

    def body(pu_ref, pg_ref, wu_ref, wg_ref, o_ref):
        up = _conv_fwd(_conv_taps(pu_ref[...].astype(F32), FFN_CONV), wu_ref)
        gate = _conv_fwd(_conv_taps(pg_ref[...].astype(F32), FFN_CONV), wg_ref)
        o_ref[...] = (gate * _sigmoid(gate) * up).astype(BF16)

    def col(rows, off):
        return pl.BlockSpec((rows, LANES), lambda j: (0, j + off))

    return pl.pallas_call(
        body, name=name, grid=(nt,),
        in_specs=[col(s, 0), col(s, nt), col(FFN_CONV, 0), col(FFN_CONV, nt)], out_specs=col(s, 0),
        out_shape=jax.ShapeDtypeStruct((s, f2 // 2), BF16), compiler_params=_cp("parallel"),
    )(pre, pre, cw, cw)


def _ffn_act_bwd(pre, cw, dact, name):
    s, f2 = pre.shape
    f = f2 // 2
    nt = f // LANES

    def body(pu_ref, pg_ref, wu_ref, wg_ref, da_ref, dpu_ref, dpg_ref, dwu_ref, dwg_ref):
        pu, pg = pu_ref[...].astype(F32), pg_ref[...].astype(F32)
        tu, tg = _conv_taps(pu, FFN_CONV), _conv_taps(pg, FFN_CONV)
        up = _conv_fwd(tu, wu_ref)
        gate = _conv_fwd(tg, wg_ref)
        sg = _sigmoid(gate)
        da = da_ref[...].astype(F32)
        dup = da * gate * sg
        dgate = da * up * (sg * (1.0 + gate * (1.0 - sg)))
        dpu_ref[...] = _conv_bwd(tu, dup, wu_ref, dwu_ref).astype(BF16)
        dpg_ref[...] = _conv_bwd(tg, dgate, wg_ref, dwg_ref).astype(BF16)

    def col(rows, off):
        return pl.BlockSpec((rows, LANES), lambda j: (0, j + off))

    return pl.pallas_call(
        body, name=name, grid=(nt,),
        in_specs=[col(s, 0), col(s, nt), col(FFN_CONV, 0), col(FFN_CONV, nt), col(s, 0)],
        out_specs=[col(s, 0), col(s, 0), col(FFN_CONV, 0), col(FFN_CONV, 0)],
        out_shape=[jax.ShapeDtypeStruct((s, f), BF16), jax.ShapeDtypeStruct((s, f), BF16),
                   jax.ShapeDtypeStruct((FFN_CONV, f), F32), jax.ShapeDtypeStruct((FFN_CONV, f), F32)],
        compiler_params=_cp("parallel"),
    )(pre, pre, cw, cw, dact)


def _xa_probs(qh, kh):
    sc = _mxu(qh, kh, 1, 1) * (XA_DH ** -0.5)
    e = jnp.exp(sc - jnp.max(sc, axis=-1, keepdims=True))
    return e / jnp.sum(e, axis=-1, keepdims=True)


def _xattn_fwd(q, kv, name):
    s, d = q.shape
    m = kv.shape[0]
    tr = min(512, s)

    def body(q_ref, kv_ref, o_ref):
        for h in range(XA_HEADS):
            lo, hi = h * XA_DH, (h + 1) * XA_DH
            p = _xa_probs(q_ref[:, lo:hi], kv_ref[:, lo:hi])
            o_ref[:, lo:hi] = _mxu(p, kv_ref[:, d + lo:d + hi]).astype(BF16)

    row = pl.BlockSpec((tr, d), lambda i: (i, 0))
    return pl.pallas_call(
        body, name=name, grid=(s // tr,), in_specs=[row, pl.BlockSpec((m, 2 * d), lambda i: (0, 0))],
        out_specs=row, out_shape=jax.ShapeDtypeStruct((s, d), BF16), compiler_params=_cp("parallel"),
    )(q, kv)


def _xattn_bwd(q, kv, do, name):
    s, d = q.shape
    m = kv.shape[0]
    tr = min(512, s)

    def body(q_ref, kv_ref, do_ref, dq_ref, dkv_ref):
        @pl.when(pl.program_id(0) == 0)
        def _():
            dkv_ref[...] = jnp.zeros_like(dkv_ref)

        for h in range(XA_HEADS):
            lo, hi = h * XA_DH, (h + 1) * XA_DH
            qh, kh, vh = q_ref[:, lo:hi], kv_ref[:, lo:hi], kv_ref[:, d + lo:d + hi]
            doh = do_ref[:, lo:hi]
            p = _xa_probs(qh, kh)
            dp = _mxu(doh, vh, 1, 1)
            ds = p * (dp - jnp.sum(p * dp, axis=-1, keepdims=True)) * (XA_DH ** -0.5)
            dq_ref[:, lo:hi] = _mxu(ds, kh).astype(BF16)
            dkv_ref[:, lo:hi] += _mxu(ds, qh, 0, 0)
            dkv_ref[:, d + lo:d + hi] += _mxu(p, doh, 0, 0)

    row = pl.BlockSpec((tr, d), lambda i: (i, 0))
    full = pl.BlockSpec((m, 2 * d), lambda i: (0, 0))
    return pl.pallas_call(
        body, name=name, grid=(s // tr,), in_specs=[row, full, row], out_specs=[row, full],
        out_shape=[jax.ShapeDtypeStruct((s, d), BF16), jax.ShapeDtypeStruct((m, 2 * d), F32)],
        compiler_params=_cp("arbitrary"),
    )(q, kv, do)


def _ret_tables():
    c = RET_CHUNK
    lg = np.log1p(-np.exp2(-5.0 - np.arange(RET_HEADS, dtype=np.float32))).astype(np.float32)
    idx = np.arange(c, dtype=np.float32)
    diff = idx[:, None] - idx[None, :]
    intra = np.where(diff >= 0, np.exp(lg[:, None, None] * np.where(diff >= 0, diff, 0.0)), 0.0)
    rk = np.broadcast_to(np.exp(lg[:, None] * (c - 1 - idx))[:, :, None], (RET_HEADS, c, LANES))
    rq = np.broadcast_to(np.exp(lg[:, None] * (idx + 1))[:, :, None], (RET_HEADS, c, LANES))
    return jnp.asarray(np.stack([intra, rk, rq], axis=1).astype(np.float32))


def _rope_tables(s):
    half = RET_DH // 2
    inv = jnp.exp(-math.log(10000.0) * jnp.arange(half, dtype=F32) / half)
    ang = jnp.arange(s, dtype=F32)[:, None] * inv[None, :]
    cos, sin = jnp.cos(ang), jnp.sin(ang)
    return jnp.concatenate([cos, cos], axis=1), jnp.concatenate([-sin, sin], axis=1)


def _ret_specs(n_of):
    c, w = RET_CHUNK, RET_HEADS * RET_DH

    def part(off):
        return pl.BlockSpec((c, w), lambda n: (n_of(n), off))

    pos = pl.BlockSpec((c, RET_DH), lambda n: (n_of(n), 0))
    gain = pl.BlockSpec((1, w), lambda n: (0, 0))
    tab = pl.BlockSpec((RET_HEADS, 3, c, LANES), lambda n: (0, 0, 0, 0))
    st = pl.BlockSpec((RET_HEADS, None, RET_DH, RET_DH), lambda n: (0, n_of(n), 0, 0))
    return part, pos, gain, tab, st


def _rheads(x):
    return jnp.stack([x[:, h * RET_DH:(h + 1) * RET_DH] for h in range(RET_HEADS)], axis=0)


def _runheads(x):
    return jnp.concatenate([x[h] for h in range(RET_HEADS)], axis=1)


def _rope(x, cos, sin):
    return x * cos + pltpu.roll(x, RET_DH // 2, 2) * sin


def _ret_chunk(q_ref, k_ref, v_ref, cos_ref, sin_ref, tab_ref, prev):
    cos, sin = cos_ref[...], sin_ref[...]
    q = _rope(_rheads(q_ref[...]), cos, sin)
    k = _rope(_rheads(k_ref[...]), cos, sin) * (RET_DH ** -0.5)
    v = _rheads(v_ref[...])
    scores = _bmxu(q, k, 2, 2) * tab_ref[:, 0]
    qdec = q * tab_ref[:, 2]
    kdec = k * tab_ref[:, 1]
    o = _bmxu(scores, v) + _bmxu(qdec, prev)
    return q, k, v, scores, qdec, kdec, o


def _ret_fwd(proj, cos, sin, gain, name):
    s = proj.shape[0]
    c = RET_CHUNK
    nc = s // c
    part, pos, gvec, tab, st = _ret_specs(lambda n: n)

    def body(q_ref, k_ref, v_ref, g_ref, cos_ref, sin_ref, rn_ref, tab_ref, o_ref, st_ref, state):
        @pl.when(pl.program_id(0) == 0)
        def _():
            state[...] = jnp.zeros_like(state)

        prev = state[...]
        st_ref[...] = prev
        _, _, v, _, _, kdec, o = _ret_chunk(q_ref, k_ref, v_ref, cos_ref, sin_ref, tab_ref, prev)
        state[...] = prev * tab_ref[:, 2, c - 1:c, :] + _bmxu(kdec, v, 1, 1)
        r = lax.rsqrt(jnp.mean(o * o, axis=-1, keepdims=True) + EPS)
        gate = g_ref[...]
        o_ref[...] = (_runheads(o * r) * rn_ref[...] * (gate * _sigmoid(gate))).astype(BF16)

    return pl.pallas_call(
        body, name=name, grid=(nc,),
        in_specs=[part(0), part(1), part(2), part(3), pos, pos, gvec, tab],
        out_specs=[part(0), st],
        out_shape=[jax.ShapeDtypeStruct((s, 2 * RET_HEADS * RET_DH), BF16),
                   jax.ShapeDtypeStruct((RET_HEADS, nc, RET_DH, RET_DH), F32)],
        scratch_shapes=[pltpu.VMEM((RET_HEADS, RET_DH, RET_DH), F32)],
        compiler_params=_cp("arbitrary"),
    )(proj, proj, proj, proj, cos, sin, gain.reshape(1, -1), _ret_tables())


def _ret_bwd(proj, cos, sin, gain, states, dmerged, name):
    s = proj.shape[0]
    c = RET_CHUNK
    nc = s // c
    width = RET_HEADS * RET_DH
    part, pos, gvec, tab, st = _ret_specs(lambda n: nc - 1 - n)

    def body(q_ref, k_ref, v_ref, g_ref, cos_ref, sin_ref, rn_ref, tab_ref, st_ref, do_ref,
             dp_ref, drn_ref, carry):
        @pl.when(pl.program_id(0) == 0)
        def _():
            carry[...] = jnp.zeros_like(carry)
            drn_ref[...] = jnp.zeros_like(drn_ref)

        prev = st_ref[...]
        q, k, v, scores, qdec, kdec, o = _ret_chunk(q_ref, k_ref, v_ref, cos_ref, sin_ref, tab_ref, prev)
        r = lax.rsqrt(jnp.mean(o * o, axis=-1, keepdims=True) + EPS)
        on = o * r
        on2 = _runheads(on)
        gate = g_ref[...]
        sg = _sigmoid(gate)
        sil = gate * sg
        dout = do_ref[...]
        rn = rn_ref[...]
        dp_ref[:, 3 * width:] = (dout * on2 * rn * (sg * (1.0 + gate * (1.0 - sg)))).astype(BF16)
        drn_ref[...] += jnp.sum(dout * on2 * sil, axis=0, keepdims=True)
        don = _rheads(dout * rn * sil)
        do = r * (don - on * jnp.mean(don * on, axis=-1, keepdims=True))
        dc = carry[...]
        dsc = _bmxu(do, v, 2, 2) * tab_ref[:, 0]
        dq = _bmxu(dsc, k) + _bmxu(do, prev, 2, 2) * tab_ref[:, 2]
        dk = _bmxu(dsc, q, 1, 1) + _bmxu(v, dc, 2, 2) * tab_ref[:, 1]
        dv = _bmxu(scores, do, 1, 1) + _bmxu(kdec, dc)
        carry[...] = _bmxu(qdec, do, 1, 1) + dc * tab_ref[:, 2, c - 1:c, :]
        cos, sin = cos_ref[...], sin_ref[...]
        dk = dk * (RET_DH ** -0.5)
        dp_ref[:, :width] = _runheads(dq * cos + pltpu.roll(dq * sin, RET_DH // 2, 2)).astype(BF16)
        dp_ref[:, width:2 * width] = _runheads(dk * cos + pltpu.roll(dk * sin, RET_DH // 2, 2)).astype(BF16)
        dp_ref[:, 2 * width:3 * width] = _runheads(dv).astype(BF16)

    return pl.pallas_call(
        body, name=name, grid=(nc,),
        in_specs=[part(0), part(1), part(2), part(3), pos, pos, gvec, tab, st, part(0)],
        out_specs=[pl.BlockSpec((c, 4 * width), lambda n: (nc - 1 - n, 0)), gvec],
        out_shape=[jax.ShapeDtypeStruct(proj.shape, BF16), jax.ShapeDtypeStruct((1, width), F32)],
        scratch_shapes=[pltpu.VMEM((RET_HEADS, RET_DH, RET_DH), F32)],
        compiler_params=_cp("arbitrary"),
    )(proj, proj, proj, proj, cos, sin, gain.reshape(1, -1), _ret_tables(), states, dmerged)


S5_TILE = 512


def _cmul_add(xr, xi, ar, ai, yr, yi):
    return xr + ar * yr - ai * yi, xi + ar * yi + ai * yr


def _s5_pow_tables(a_il, name):
    r = SCAN_ROWS
    t = S5_TILE
    w2 = a_il.shape[1]

    def body(a_ref, up_ref, dn_ref):
        for j in range(w2 // (2 * t)):
            re, im = pl.ds(2 * t * j, t), pl.ds(2 * t * j + t, t)
            up_ref[0:1, re] = a_ref[:, re]
            up_ref[0:1, im] = a_ref[:, im]
            dn_ref[r - 1:r, re] = a_ref[:, re]
            dn_ref[r - 1:r, im] = -a_ref[:, im]
            n = 1
            while n < r:
                lr, li = up_ref[n - 1:n, re], up_ref[n - 1:n, im]
                xr, xi = up_ref[0:n, re], up_ref[0:n, im]
                up_ref[n:2 * n, re] = xr * lr - xi * li
                up_ref[n:2 * n, im] = xr * li + xi * lr
                yr, yi = dn_ref[r - n:r, re], dn_ref[r - n:r, im]
                dn_ref[r - 2 * n:r - n, re] = yr * lr + yi * li
                dn_ref[r - 2 * n:r - n, im] = yi * lr - yr * li
                n *= 2

    return pl.pallas_call(
        body, name=name, out_shape=[jax.ShapeDtypeStruct((r, w2), F32)] * 2, compiler_params=_cp(),
    )(a_il)


_GELU_C = math.sqrt(2.0 / math.pi)
_GELU_A = 0.044715


def _gelu(y):
    return 0.5 * y * (1.0 + jnp.tanh(_GELU_C * (y + _GELU_A * y * y * y)))


def _gelu_grad(y):
    th = jnp.tanh(_GELU_C * (y + _GELU_A * y * y * y))
    return 0.5 * (1.0 + th) + 0.5 * y * (1.0 - th * th) * _GELU_C * (1.0 + 3.0 * _GELU_A * y * y)


def _rows_shift(x, k, axis, up):
    n = x.shape[axis]
    idx = lax.broadcasted_iota(jnp.int32, x.shape, axis)
    if up:
        return jnp.where(idx < n - k, pltpu.roll(x, n - k, axis), 0.0)
    return jnp.where(idx >= k, pltpu.roll(x, k, axis), 0.0)


def _scan_block(xr, xi, pr, pi, cr, ci, rev):
    r, w = xr.shape
    nt = r // 8
    x3r, x3i = xr.reshape(nt, 8, w), xi.reshape(nt, 8, w)
    p3r, p3i = pr.reshape(nt, 8, w), pi.reshape(nt, 8, w)

    def power(rows):
        t = r - rows if rev else rows - 1
        return pr[t:t + 1, :], pi[t:t + 1, :]

    tile_row = lax.broadcasted_iota(jnp.int32, (8, w), 0)
    for sh in (1, 2, 4):
        ar, ai = power(sh)
        keep = tile_row < 8 - sh if rev else tile_row >= sh
        mr, mi = jnp.where(keep, ar, 0.0)[None], jnp.where(keep, ai, 0.0)[None]
        turn = 8 - sh if rev else sh
        x3r, x3i = _cmul_add(x3r, x3i, mr, mi, pltpu.roll(x3r, turn, 1), pltpu.roll(x3i, turn, 1))
    edge = 0 if rev else 7
    lr, li = x3r[:, edge, :], x3i[:, edge, :]
    sh = 1
    while sh < nt:
        ar, ai = power(8 * sh)
        lr, li = _cmul_add(lr, li, ar, ai, _rows_shift(lr, sh, 0, rev), _rows_shift(li, sh, 0, rev))
        sh *= 2
    tr_, ti_ = p3r[:, edge, :], p3i[:, edge, :]
    first = lax.broadcasted_iota(jnp.int32, (nt, w), 0) == (nt - 1 if rev else 0)
    wr = jnp.where(first, 1.0, _rows_shift(tr_, 1, 0, rev))
    wi = jnp.where(first, 0.0, _rows_shift(ti_, 1, 0, rev))
    er, ei = _cmul_add(_rows_shift(lr, 1, 0, rev), _rows_shift(li, 1, 0, rev), wr, wi, cr, ci)
    a8r, a8i = (p3r[nt - 1], p3i[nt - 1]) if rev else (p3r[0], p3i[0])
    x3r, x3i = _cmul_add(x3r, x3i, a8r[None], a8i[None], er[:, None, :], ei[:, None, :])
    outr, outi = x3r.reshape(r, w), x3i.reshape(r, w)
    last = 0 if rev else r - 1
    return outr, outi, outr[last:last + 1, :], outi[last:last + 1, :]


def _s5_tile_specs(n_of, r):
    t = S5_TILE
    ucol = 4 * RET_HEADS * RET_DH // LANES
    u = pl.BlockSpec((r, LANES), lambda j, i: (n_of(i), ucol + j))
    col = pl.BlockSpec((r, LANES), lambda j, i: (n_of(i), j))
    state = pl.BlockSpec((r, 2 * t), lambda j, i: (n_of(i), j))
    table = pl.BlockSpec((r, 2 * t), lambda j, i: (0, j))
    bbt = pl.BlockSpec((None, LANES, 2 * t), lambda j, i: (j, 0, 0))
    cct = pl.BlockSpec((None, 2 * t, LANES), lambda j, i: (j, 0, 0))
    vec = pl.BlockSpec((1, LANES), lambda j, i: (0, j))
    return u, col, state, table, bbt, cct, vec


def _s5_fwd(proj, bbt, cct, apow, dvec, name):
    s = proj.shape[0]
    r, t = SCAN_ROWS, S5_TILE
    w = S5_GROUPS * S5_GROUP
    u_s, col, state, table, bb_s, cc_s, vec = _s5_tile_specs(lambda i: i, r)

    def body(u_ref, bb_ref, cc_ref, p_ref, d_ref, st_ref, y_ref, g_ref, cr, ci):
        @pl.when(pl.program_id(1) == 0)
        def _():
            cr[...] = jnp.zeros_like(cr)
            ci[...] = jnp.zeros_like(ci)

        u = u_ref[...]
        bu = _mxu(u, bb_ref[...])
        xr, xi, cr[...], ci[...] = _scan_block(bu[:, :t], bu[:, t:], p_ref[:, :t], p_ref[:, t:], cr[...], ci[...], False)
        st_ref[:, :t] = xr
        st_ref[:, t:] = xi
        y = _mxu(xr, cc_ref[:t, :]) + _mxu(xi, cc_ref[t:, :]) + d_ref[...] * u
        y_ref[...] = y
        g_ref[...] = _gelu(y).astype(BF16)

    return pl.pallas_call(
        body, name=name, grid=(2 * S5_GROUPS * S5_STATE // (2 * t), s // r),
        in_specs=[u_s, bb_s, cc_s, table, vec], out_specs=[state, col, col],
        out_shape=[jax.ShapeDtypeStruct((s, 2 * S5_GROUPS * S5_STATE), F32), jax.ShapeDtypeStruct((s, w), F32),
                   jax.ShapeDtypeStruct((s, w), BF16)],
        scratch_shapes=[pltpu.VMEM((1, t), F32), pltpu.VMEM((1, t), F32)],
        compiler_params=_cp("parallel", "arbitrary"),
    )(proj, bbt, cct, apow, dvec)


def _s5_bwd(dg1, dg2, y, proj, st, bbt, cct, apow_rev, dvec, dproj, name):
    s = proj.shape[0]
    r, t = SCAN_ROWS, S5_TILE
    nb = s // r
    w = S5_GROUPS * S5_GROUP
    u_s, col, state, table, bb_s, cc_s, vec = _s5_tile_specs(lambda i: nb - 1 - i, r)
    halo = pl.BlockSpec((8, 2 * t), lambda j, i: (jnp.maximum((nb - 1 - i) * (r // 8) - 1, 0), j))
    acc = pl.BlockSpec((1, 2 * t), lambda j, i: (0, j))

    def body(a_ref, b_ref, y_ref, u_ref, s_ref, sp_ref, bb_ref, cc_ref, p_ref, d_ref, _,
             du_ref, da_ref, dbb_ref, dcc_ref, dd_ref, cr, ci):
        i = pl.program_id(1)

        @pl.when(i == 0)
        def _():
            cr[...] = jnp.zeros_like(cr)
            ci[...] = jnp.zeros_like(ci)
            da_ref[...] = jnp.zeros_like(da_ref)
            dbb_ref[...] = jnp.zeros_like(dbb_ref)
            dcc_ref[...] = jnp.zeros_like(dcc_ref)
            dd_ref[...] = jnp.zeros_like(dd_ref)

        u = u_ref[...]
        dy = (a_ref[...] + b_ref[...]) * _gelu_grad(y_ref[...])
        dd_ref[...] += jnp.sum(dy * u, axis=0, keepdims=True)
        sr, si = s_ref[:, :t], s_ref[:, t:]
        dcc_ref[:t, :] += _mxu(sr, dy, 0, 0)
        dcc_ref[t:, :] += _mxu(si, dy, 0, 0)
        xr, xi, cr[...], ci[...] = _scan_block(_mxu(dy, cc_ref[:t, :], 1, 1), _mxu(dy, cc_ref[t:, :], 1, 1),
                                               p_ref[:, :t], p_ref[:, t:], cr[...], ci[...], True)
        du_ref[...] = (dy * d_ref[...] + _mxu(xr, bb_ref[:, :t], 1, 1) + _mxu(xi, bb_ref[:, t:], 1, 1)).astype(BF16)
        dbb_ref[:, :t] += _mxu(u, xr, 0, 0)
        dbb_ref[:, t:] += _mxu(u, xi, 0, 0)
        first = i == nb - 1
        row = lax.broadcasted_iota(jnp.int32, (r, t), 0)
        pr = jnp.where(row == 0, jnp.where(first, 0.0, sp_ref[7:8, :t]), pltpu.roll(sr, 1, 0))
        pi = jnp.where(row == 0, jnp.where(first, 0.0, sp_ref[7:8, t:]), pltpu.roll(si, 1, 0))
        da_ref[:, :t] += jnp.sum(xr * pr + xi * pi, axis=0, keepdims=True)
        da_ref[:, t:] += jnp.sum(xi * pr - xr * pi, axis=0, keepdims=True)

    return pl.pallas_call(
        body, name=name, grid=(2 * S5_GROUPS * S5_STATE // (2 * t), nb),
        in_specs=[col, col, col, u_s, state, halo, bb_s, cc_s, table, vec, pl.BlockSpec(memory_space=pl.ANY)],
        out_specs=[u_s, acc, bb_s, cc_s, vec],
        out_shape=[jax.ShapeDtypeStruct(dproj.shape, dproj.dtype), jax.ShapeDtypeStruct((1, 2 * S5_GROUPS * S5_STATE), F32),
                   jax.ShapeDtypeStruct(bbt.shape, F32), jax.ShapeDtypeStruct(cct.shape, F32),
                   jax.ShapeDtypeStruct((1, w), F32)],
        scratch_shapes=[pltpu.VMEM((1, t), F32), pltpu.VMEM((1, t), F32)],
        input_output_aliases={10: 0}, compiler_params=_cp("parallel", "arbitrary"),
    )(dg1, dg2, y, proj, st, st, bbt, cct, apow_rev, dvec, dproj)


def _s5_tile_b(b_re, b_im):
    nt = S5_GROUPS * S5_STATE // S5_TILE
    gpt = S5_GROUPS // nt
    eye = jnp.eye(gpt, dtype=F32)

    def tile(b):
        t5 = jnp.einsum("jghp,gk->jghkp", b.reshape(nt, gpt, S5_GROUP, S5_STATE), eye)
        return t5.reshape(nt, gpt * S5_GROUP, S5_TILE)

    return jnp.concatenate([tile(b_re), tile(b_im)], axis=2)


def _s5_untile_b(d):
    nt = S5_GROUPS * S5_STATE // S5_TILE
    gpt = S5_GROUPS // nt
    eye = jnp.eye(gpt, dtype=F32)

    def untile(x):
        x5 = x.reshape(nt, gpt, S5_GROUP, gpt, S5_STATE)
        return jnp.einsum("jghkp,gk->jghp", x5, eye).reshape(S5_GROUPS, S5_GROUP, S5_STATE)

    return untile(d[:, :, :S5_TILE]), untile(d[:, :, S5_TILE:])


def _s5_tile_c(c_re, c_im):
    nt = S5_GROUPS * S5_STATE // S5_TILE
    gpt = S5_GROUPS // nt
    eye = jnp.eye(gpt, dtype=F32)

    def tile(c):
        t5 = jnp.einsum("jgph,gk->jkpgh", c.reshape(nt, gpt, S5_STATE, S5_GROUP), eye)
        return t5.reshape(nt, S5_TILE, gpt * S5_GROUP)

    return jnp.concatenate([tile(c_re), -tile(c_im)], axis=1)


def _s5_untile_c(d):
    nt = S5_GROUPS * S5_STATE // S5_TILE
    gpt = S5_GROUPS // nt
    eye = jnp.eye(gpt, dtype=F32)

    def untile(x):
        x5 = x.reshape(nt, gpt, S5_STATE, gpt, S5_GROUP)
        return jnp.einsum("jkpgh,gk->jgph", x5, eye).reshape(S5_GROUPS, S5_STATE, S5_GROUP)

    return untile(d[:, :S5_TILE, :]), -untile(d[:, S5_TILE:, :])


def _row_call(body, name, s, ins, outs, acc=False):
    tr = min(512, s)

    def spec(width, cb, rows):
        if rows == 1:
            return pl.BlockSpec((1, width), lambda i: (0, cb))
        return pl.BlockSpec((tr, width), lambda i: (i, cb))

    in_specs = [spec(w, cb, a.shape[0]) for a, w, cb in ins]
    out_specs = [spec(w, cb, sd.shape[0]) for sd, w, cb in outs]
    return pl.pallas_call(
        body, name=name, grid=(s // tr,), in_specs=in_specs, out_specs=out_specs,
        out_shape=[sd for sd, _, _ in outs],
        compiler_params=_cp("arbitrary" if acc else "parallel"),
    )(*[a for a, _, _ in ins])


def _sds(shape, dtype):
    return jax.ShapeDtypeStruct(shape, dtype)


def _s5_glu_fwd(y, z, b, merged, name):
    s, w = y.shape
    tr = min(512, s)

    def body(y_ref, z_ref, b_ref, _, o_ref):
        o_ref[...] = (_gelu(y_ref[...]) * _sigmoid(z_ref[...] + b_ref[...])).astype(BF16)

    row = pl.BlockSpec((tr, w), lambda i: (i, 0))
    return pl.pallas_call(
        body, name=name, grid=(s // tr,),
        in_specs=[row, row, pl.BlockSpec((1, w), lambda i: (0, 0)), pl.BlockSpec(memory_space=pl.ANY)],
        out_specs=pl.BlockSpec((tr, w), lambda i: (i, 1)),
        out_shape=jax.ShapeDtypeStruct(merged.shape, merged.dtype),
        input_output_aliases={3: 0}, compiler_params=_cp("parallel"),
    )(y, z, b, merged)


def _s5_glu_bwd(dmerged, y, z, b, name):
    s, w = y.shape

    def body(do_ref, y_ref, z_ref, b_ref, dz_ref, dg_ref, db_ref):
        @pl.when(pl.program_id(0) == 0)
        def _():
            db_ref[...] = jnp.zeros_like(db_ref)

        g = _gelu(y_ref[...])
        sg = _sigmoid(z_ref[...] + b_ref[...])
        dout = do_ref[...]
        dz = dout * g * sg * (1.0 - sg)
        dz_ref[...] = dz.astype(BF16)
        dg_ref[...] = dout * sg
        db_ref[...] += jnp.sum(dz, axis=0, keepdims=True)

    return _row_call(body, name, s, [(dmerged, w, 1), (y, w, 0), (z, w, 0), (b, w, 0)],
                     [(_sds((s, w), BF16), w, 0), (_sds((s, w), F32), w, 0), (_sds((1, w), F32), w, 0)], acc=True)


def _gdn_conv_fwd(projx, cw, name):
    s = projx.shape[0]
    nh = GDN_HEADS

    def body(x_ref, w_ref, o_ref):
        j = pl.program_id(0)
        cv = _conv_fwd(_conv_taps(x_ref[...], GDN_CONV), w_ref)
        y = cv * _sigmoid(cv)
        nrm = y * lax.rsqrt(jnp.sum(y * y, axis=-1, keepdims=True) + EPS)
        o_ref[...] = jnp.where(j < nh, nrm * (GDN_DH ** -0.5), jnp.where(j < 2 * nh, nrm, y))

    return pl.pallas_call(
        body, name=name, grid=(3 * nh,),
        in_specs=[pl.BlockSpec((s, GDN_DH), lambda j: (0, j)), pl.BlockSpec((GDN_CONV, GDN_DH), lambda j: (0, j))],
        out_specs=pl.BlockSpec((s, GDN_DH), lambda j: (0, j)),
        out_shape=jax.ShapeDtypeStruct((s, 3 * nh * GDN_DH), F32), compiler_params=_cp("parallel"),
    )(projx, cw)


def _gdn_conv_bwd(projx, cw, dqkv, dprojx, name):
    s = projx.shape[0]
    nh = GDN_HEADS

    def body(x_ref, w_ref, d_ref, _, dx_ref, dw_ref):
        j = pl.program_id(0)
        x = x_ref[...]
        taps = _conv_taps(x, GDN_CONV)
        cv = _conv_fwd(taps, w_ref)
        sg = _sigmoid(cv)
        y = cv * sg
        rinv = lax.rsqrt(jnp.sum(y * y, axis=-1, keepdims=True) + EPS)
        nrm = y * rinv
        dn = d_ref[...]
        dns = jnp.where(j < nh, dn * (GDN_DH ** -0.5), dn)
        dyn = rinv * (dns - nrm * jnp.sum(dns * nrm, axis=-1, keepdims=True))
        dy = jnp.where(j < 2 * nh, dyn, dn)
        dc = dy * (sg * (1.0 + cv * (1.0 - sg)))
        dx_ref[...] = _conv_bwd(taps, dc, w_ref, dw_ref).astype(BF16)

    col = pl.BlockSpec((s, GDN_DH), lambda j: (0, j))
    wcol = pl.BlockSpec((GDN_CONV, GDN_DH), lambda j: (0, j))
    return pl.pallas_call(
        body, name=name, grid=(3 * nh,), in_specs=[col, wcol, col, pl.BlockSpec(memory_space=pl.ANY)],
        out_specs=[col, wcol],
        out_shape=[jax.ShapeDtypeStruct(dprojx.shape, dprojx.dtype), jax.ShapeDtypeStruct((GDN_CONV, 3 * nh * GDN_DH), F32)],
        input_output_aliases={3: 0}, compiler_params=_cp("parallel"),
    )(projx, cw, dqkv, dprojx)


def _softplus(x):
    return jnp.maximum(x, 0.0) + jnp.log1p(jnp.exp(-jnp.abs(x)))


def _gdn_gates_fwd(projx, alog, dtb, name):
    s = projx.shape[0]
    w = GDN_HEADS * GDN_DH
    tr = min(512, s)

    def body(t_ref, al_ref, dt_ref, bo_ref, go_ref):
        t = t_ref[...]
        for h in range(GDN_HEADS):
            lo, hi = h * GDN_DH, (h + 1) * GDN_DH
            b = jnp.broadcast_to(t[:, h:h + 1], (tr, GDN_DH))
            a = jnp.broadcast_to(t[:, GDN_HEADS + h:GDN_HEADS + h + 1], (tr, GDN_DH))
            bo_ref[:, lo:hi] = _sigmoid(b)
            go_ref[:, lo:hi] = -jnp.exp(al_ref[:, lo:hi]) * _softplus(a + dt_ref[:, lo:hi])

    row = pl.BlockSpec((tr, w), lambda i: (i, 0))
    vec = pl.BlockSpec((1, w), lambda i: (0, 0))
    return pl.pallas_call(
        body, name=name, grid=(s // tr,),
        in_specs=[pl.BlockSpec((tr, LANES), lambda i: (i, 4 * w // LANES)), vec, vec], out_specs=[row, row],
        out_shape=[jax.ShapeDtypeStruct((s, w), F32)] * 2, compiler_params=_cp("parallel"),
    )(projx, alog, dtb)


def _gdn_gates_bwd(projx, alog, dtb, dbeta, dg, dprojx, name):
    s = projx.shape[0]
    w = GDN_HEADS * GDN_DH
    tr = min(512, s)
    gate_blk = 4 * w // LANES

    def body(t_ref, al_ref, dt_ref, dbe_ref, dg_ref, _, o_ref, dal_ref, ddt_ref):
        @pl.when(pl.program_id(0) == 0)
        def _():
            dal_ref[...] = jnp.zeros_like(dal_ref)
            ddt_ref[...] = jnp.zeros_like(ddt_ref)

        t = t_ref[...]
        lane = lax.broadcasted_iota(jnp.int32, (tr, LANES), 1)
        lane1 = lax.broadcasted_iota(jnp.int32, (1, LANES), 1)
        out = jnp.zeros((tr, LANES), F32)
        dal = jnp.zeros((1, LANES), F32)
        ddt = jnp.zeros((1, LANES), F32)
        for h in range(GDN_HEADS):
            lo, hi = h * GDN_DH, (h + 1) * GDN_DH
            beta = _sigmoid(t[:, h:h + 1])
            pb = jnp.sum(dbe_ref[:, lo:hi], axis=-1, keepdims=True)
            db = pb * beta * (1.0 - beta)
            xa = t[:, GDN_HEADS + h:GDN_HEADS + h + 1] + dt_ref[:, lo:lo + 1]
            ea = -jnp.exp(al_ref[:, lo:lo + 1])
            pg = jnp.sum(dg_ref[:, lo:hi], axis=-1, keepdims=True)
            da = pg * ea * _sigmoid(xa)
            out = jnp.where(lane == h, db, jnp.where(lane == GDN_HEADS + h, da, out))
            dal = jnp.where(lane1 == h, jnp.sum(pg * ea * _softplus(xa), axis=0, keepdims=True), dal)
            ddt = jnp.where(lane1 == h, jnp.sum(da, axis=0, keepdims=True), ddt)
        o_ref[...] = out.astype(BF16)
        dal_ref[...] += dal
        ddt_ref[...] += ddt

    row = pl.BlockSpec((tr, w), lambda i: (i, 0))
    vec = pl.BlockSpec((1, w), lambda i: (0, 0))
    small = pl.BlockSpec((1, LANES), lambda i: (0, 0))
    gates = pl.BlockSpec((tr, LANES), lambda i: (i, gate_blk))
    return pl.pallas_call(
        body, name=name, grid=(s // tr,),
        in_specs=[gates, vec, vec, row, row, pl.BlockSpec(memory_space=pl.ANY)],
        out_specs=[gates, small, small],
        out_shape=[jax.ShapeDtypeStruct(dprojx.shape, dprojx.dtype), jax.ShapeDtypeStruct((1, LANES), F32),
                   jax.ShapeDtypeStruct((1, LANES), F32)],
        input_output_aliases={5: 0}, compiler_params=_cp("arbitrary"),
    )(projx, alog, dtb, dbeta, dg, dprojx)


def _gdn_tri():
    c = GDN_CHUNK
    i = lax.broadcasted_iota(jnp.int32, (c, c), 0)
    j = lax.broadcasted_iota(jnp.int32, (c, c), 1)
    return ((i >= j).astype(F32), (i <= j).astype(F32), i >= j, i > j, (i == j).astype(F32))


def _bdot(a, b, ca=2, cb=1, precision=None):
    return lax.dot_general(a, b, (((ca,), (cb,)), ((0,), (0,))), precision=precision, preferred_element_type=F32)


def _bmxu(a, b, ca=2, cb=1):
    return _bdot(a.astype(BF16), b.astype(BF16), ca, cb)


def _split(x):
    hi = x.astype(BF16)
    return hi, (x - hi.astype(F32)).astype(BF16)


def _bdot3(a, b, ca=2, cb=1):
    ah, al = _split(a)
    bh, bl = _split(b)
    return _bdot(ah, bh, ca, cb) + (_bdot(ah, bl, ca, cb) + _bdot(al, bh, ca, cb))


def _tri_dot(tri, x):
    t = tri.astype(BF16)
    hi = x.astype(BF16)
    r1 = x - hi.astype(F32)
    mid = r1.astype(BF16)
    lo = (r1 - mid.astype(F32)).astype(BF16)
    return _dot(t, hi) + (_dot(t, mid) + _dot(t, lo))


def _heads(x):
    return jnp.stack([x[:, h * GDN_DH:(h + 1) * GDN_DH] for h in range(GDN_HEADS)], axis=0)


def _unheads(x):
    return jnp.concatenate([x[h] for h in range(GDN_HEADS)], axis=1)


def _gdn_chunk(q, k, v, bb, g2d, tri):
    low, up, incl, strict, eye = tri
    c = GDN_CHUNK
    gc = _heads(_tri_dot(low, g2d))
    gci = gc[:, :, :c]
    gdiff = gci - jnp.swapaxes(gci, 1, 2)
    decay = jnp.where(incl, jnp.exp(jnp.where(incl, gdiff, 0.0)), 0.0)
    kb, vb = k * bb, v * bb
    kbk = _bmxu(kb, k, 2, 2)
    x = -jnp.where(strict, kbk * decay, 0.0)
    t = eye + x
    p = x
    for _ in range(c.bit_length() - 2):
        p = _bdot3(p, p)
        t = t + _bdot3(t, p)
    eg = jnp.exp(gc)
    kbg = kb * eg
    gcl = gc[:, c - 1:c, :]
    ek = jnp.exp(gcl - gc)
    qkraw = _bmxu(q, k, 2, 2)
    return dict(decay=decay, kb=kb, vb=vb, kbk=kbk, t=t, eg=eg, kbg=kbg, ek=ek, gl=jnp.exp(gcl),
                w=_bmxu(t, kbg), u=_bmxu(t, vb), qkraw=qkraw, qk=jnp.where(incl, qkraw * decay, 0.0),
                qd=q * eg, kd=k * ek)


def _gdn_specs(n_of):
    c, w = GDN_CHUNK, GDN_HEADS * GDN_DH

    def blk(cb, width=w):
        return pl.BlockSpec((c, width), lambda n: (n_of(n), cb))

    st = pl.BlockSpec((None, GDN_HEADS, GDN_DH, GDN_DH), lambda n: (n_of(n), 0, 0, 0))
    vec = pl.BlockSpec((1, GDN_DH), lambda n: (0, 0))
    return blk, st, vec


def _gdn_load(qkv_ref, b_ref, g_ref, tri):
    w = GDN_HEADS * GDN_DH
    q, k, v = _heads(qkv_ref[:, :w]), _heads(qkv_ref[:, w:2 * w]), _heads(qkv_ref[:, 2 * w:])
    bb = _heads(b_ref[...])
    return q, k, v, bb, _gdn_chunk(q, k, v, bb, g_ref[...], tri)


def _gdn_fwd(qkv, beta, g, projx, onorm, name):
    s = qkv.shape[0]
    nc = s // GDN_CHUNK
    w = GDN_HEADS * GDN_DH
    blk, st, vec = _gdn_specs(lambda n: n)

    def body(qkv_ref, b_ref, g_ref, z_ref, on_ref, o_ref, st_ref, state):
        @pl.when(pl.program_id(0) == 0)
        def _():
            state[...] = jnp.zeros_like(state)

        _, _, _, _, ch = _gdn_load(qkv_ref, b_ref, g_ref, _gdn_tri())
        sp = state[...]
        st_ref[...] = sp
        vn = ch["u"] - _bmxu(ch["w"], sp)
        o = _bmxu(ch["qd"], sp) + _bmxu(ch["qk"], vn)
        state[...] = sp * ch["gl"] + _bmxu(ch["kd"], vn, 1, 1)
        r = lax.rsqrt(jnp.mean(o * o, axis=-1, keepdims=True) + EPS)
        z = _heads(z_ref[...])
        o_ref[...] = _unheads(o * r * on_ref[...] * (z * _sigmoid(z))).astype(BF16)

    return pl.pallas_call(
        body, name=name, grid=(nc,),
        in_specs=[blk(0, 3 * w), blk(0), blk(0), blk(3), vec], out_specs=[blk(0), st],
        out_shape=[jax.ShapeDtypeStruct((s, w), BF16), jax.ShapeDtypeStruct((nc, GDN_HEADS, GDN_DH, GDN_DH), F32)],
        scratch_shapes=[pltpu.VMEM((GDN_HEADS, GDN_DH, GDN_DH), F32)],
        compiler_params=_cp("arbitrary"),
    )(qkv, beta, g, projx, onorm.reshape(1, -1))


def _gdn_bwd(qkv, beta, g, projx, onorm, states, dout, name):
    s = qkv.shape[0]
    c = GDN_CHUNK
    nc = s // c
    w = GDN_HEADS * GDN_DH
    blk, st, vec = _gdn_specs(lambda n: nc - 1 - n)

    def body(qkv_ref, b_ref, g_ref, z_ref, on_ref, st_ref, do_ref,
             dqkv_ref, db_ref, dg_ref, dz_ref, don_ref, carry):
        @pl.when(pl.program_id(0) == 0)
        def _():
            carry[...] = jnp.zeros_like(carry)
            don_ref[...] = jnp.zeros_like(don_ref)

        tri = _gdn_tri()
        low, up, incl, strict, eye = tri
        q, k, v, bb, ch = _gdn_load(qkv_ref, b_ref, g_ref, tri)
        sp = st_ref[...]
        vn = ch["u"] - _bmxu(ch["w"], sp)
        o = _bmxu(ch["qd"], sp) + _bmxu(ch["qk"], vn)
        r = lax.rsqrt(jnp.mean(o * o, axis=-1, keepdims=True) + EPS)
        orn = o * r
        z = _heads(z_ref[...])
        sg = _sigmoid(z)
        dout = _heads(do_ref[...])
        onw = on_ref[...]
        dz_ref[...] = _unheads(dout * orn * onw * (sg * (1.0 + z * (1.0 - sg)))).astype(BF16)
        don = dout * (z * sg)
        don_ref[...] += jnp.sum(jnp.sum(don * orn, axis=0), axis=0, keepdims=True)
        dor = don * onw
        do = r * (dor - orn * jnp.mean(dor * orn, axis=-1, keepdims=True))
        dsn = carry[...]
        dqd = _bmxu(do, sp, 2, 2)
        dqk = jnp.where(incl, _bmxu(do, vn, 2, 2), 0.0)
        dvn = _bmxu(ch["qk"], do, 1, 1) + _bmxu(ch["kd"], dsn)
        dkd = _bmxu(vn, dsn, 2, 2)
        dgl = jnp.sum(dsn * sp, axis=1, keepdims=True)
        dw = -_bmxu(dvn, sp, 2, 2)
        carry[...] = _bmxu(ch["qd"], do, 1, 1) + dsn * ch["gl"] - _bmxu(ch["w"], dvn, 1, 1)
        t = ch["t"]
        dvb = _bmxu(t, dvn, 1, 1)
        dkbg = _bmxu(t, dw, 1, 1)
        dt = _bmxu(dvn, ch["vb"], 2, 2) + _bmxu(dw, ch["kbg"], 2, 2)
        da = -_bdot3(_bdot3(t, dt, 1, 1), t, 2, 2)
        da = jnp.where(strict, da, 0.0)
        decay = ch["decay"]
        dkbk = da * decay
        dqkr = dqk * decay
        mdec = (da * ch["kbk"] + dqk * ch["qkraw"]) * decay
        dkb = _bmxu(dkbk, k) + dkbg * ch["eg"]
        dk = _bmxu(dkbk, ch["kb"], 1, 1) + _bmxu(dqkr, q, 1, 1) + dkd * ch["ek"] + dkb * bb
        dq = _bmxu(dqkr, k) + dqd * ch["eg"]
        tk = dkd * ch["kd"]
        dgcl = jnp.sum(tk, axis=1, keepdims=True) + dgl * ch["gl"]
        row = lax.broadcasted_iota(jnp.int32, (GDN_HEADS, c, GDN_DH), 1)
        zpad = jnp.zeros((GDN_HEADS, c, GDN_DH - c), F32)
        dgc = (jnp.concatenate([mdec, zpad], axis=2) - jnp.concatenate([jnp.swapaxes(mdec, 1, 2), zpad], axis=2)
               + dqd * ch["qd"] - tk + dkbg * ch["kbg"] + jnp.where(row == c - 1, dgcl, 0.0))
        dqkv_ref[:, :w] = _unheads(dq)
        dqkv_ref[:, w:2 * w] = _unheads(dk)
        dqkv_ref[:, 2 * w:] = _unheads(dvb * bb)
        db_ref[...] = _unheads(dkb * k + dvb * v)
        dg_ref[...] = _tri_dot(up, _unheads(dgc))

    return pl.pallas_call(
        body, name=name, grid=(nc,),
        in_specs=[blk(0, 3 * w), blk(0), blk(0), blk(3), vec, st, blk(0)],
        out_specs=[blk(0, 3 * w), blk(0), blk(0), blk(3), vec],
        out_shape=[jax.ShapeDtypeStruct((s, 3 * w), F32), jax.ShapeDtypeStruct((s, w), F32),
                   jax.ShapeDtypeStruct((s, w), F32), jax.ShapeDtypeStruct(projx.shape, BF16),
                   jax.ShapeDtypeStruct((1, GDN_DH), F32)],
        scratch_shapes=[pltpu.VMEM((GDN_HEADS, GDN_DH, GDN_DH), F32)],
        compiler_params=_cp("arbitrary"),
    )(qkv, beta, g, projx, onorm.reshape(1, -1), states, dout)


_WEIGHTS = (
    "l0_mix_norm", "l0_w_in", "l0_ret_norm", "l0_s5_lambda_re", "l0_s5_lambda_im", "l0_s5_b_re", "l0_s5_b_im",
    "l0_s5_c_re", "l0_s5_c_im", "l0_s5_d", "l0_s5_log_dt", "l0_s5_w_glu", "l0_s5_b_glu", "l0_w_out",
    "l0_xa_norm", "l0_mem_norm", "l0_xa_wq", "l0_xa_wkv", "l0_xa_wo", "l0_ffn_norm", "l0_ffn_w_up",
    "l0_ffn_conv", "l0_ffn_w_down", "l1_mix_norm", "l1_w_in", "l1_conv", "l1_a_log", "l1_dt_bias", "l1_o_norm",
    "l1_w_out", "l1_xa_norm", "l1_mem_norm", "l1_xa_wq", "l1_xa_wkv", "l1_xa_wo", "l1_ffn_norm", "l1_ffn_w_up",
    "l1_ffn_conv", "l1_ffn_w_down", "final_norm")
_INPUTS = ("x", "mem") + _WEIGHTS + ("loss_target",) + tuple("m_" + n for n in _WEIGHTS) + tuple("v_" + n for n in _WEIGHTS)

_COL = ("l0_w_in", "l0_xa_wkv", "l0_ffn_w_up", "l0_ffn_conv", "l1_w_in", "l1_conv", "l1_xa_wkv", "l1_ffn_w_up",
        "l1_ffn_conv")
_ROW = ("l0_s5_w_glu", "l0_w_out", "l0_xa_wq", "l0_xa_wo", "l0_ffn_w_down", "l1_w_out", "l1_xa_wq", "l1_xa_wo",
        "l1_ffn_w_down")
_F32_WIRE = ("l0_ffn_conv", "l1_conv", "l1_ffn_conv")
_REP = tuple(n for n in _WEIGHTS if n not in _COL + _ROW)
_GATHER_GROUPS = (("l0_w_in", "l0_s5_w_glu", "l0_w_out"),
                  ("l0_xa_wq", "l0_xa_wkv", "l0_xa_wo", "l0_ffn_w_up", "l0_ffn_conv", "l0_ffn_w_down"),
                  ("l1_w_in", "l1_conv", "l1_w_out", "l1_xa_wq", "l1_xa_wkv", "l1_xa_wo"),
                  ("l1_ffn_w_up", "l1_ffn_conv", "l1_ffn_w_down"))


def _round_up(n, m):
    return (n + m - 1) // m * m


_REP_BIG = ("l0_s5_lambda_re", "l0_s5_lambda_im", "l0_s5_b_re", "l0_s5_b_im", "l0_s5_c_re", "l0_s5_c_im", "l0_s5_d")
_REP_LAST = "l0_mix_norm"
_REP_SMALL = tuple(n for n in _REP if n not in _REP_BIG + (_REP_LAST,))
PACK_WIDTH = 1024


def _pack_rows(ts):
    rows = [jnp.pad(t, ((0, 0), (0, PACK_WIDTH - t.shape[1]))) for t in ts]
    rows.append(jnp.zeros((_round_up(len(ts), 8) - len(ts), PACK_WIDTH), F32))
    return jnp.concatenate(rows, axis=0)


def _s5_interleave(re, im):
    lead = re.shape[:-1]
    nt = re.shape[-1] // S5_TILE
    both = jnp.stack([re.reshape(lead + (nt, S5_TILE)), im.reshape(lead + (nt, S5_TILE))], axis=-2)
    return both.reshape(lead + (2 * re.shape[-1],))


def _s5_split(x):
    lead = x.shape[:-1]
    y = x.reshape(lead + (x.shape[-1] // (2 * S5_TILE), 2, S5_TILE))
    return y[..., 0, :].reshape(lead + (-1,)), y[..., 1, :].reshape(lead + (-1,))


def _s5_discretise(lr, li, log_dt, b_re, b_im):
    dt = jnp.exp(log_dt)[:, None]
    mag = jnp.exp(lr * dt)
    a_re = mag * jnp.cos(li * dt)
    a_im = mag * jnp.sin(li * dt)
    den = lr * lr + li * li
    z_re = ((a_re - 1.0) * lr + a_im * li) / den
    z_im = (a_im * lr - (a_re - 1.0) * li) / den
    bb_re = z_re[:, None, :] * b_re - z_im[:, None, :] * b_im
    bb_im = z_re[:, None, :] * b_im + z_im[:, None, :] * b_re
    return a_re, a_im, bb_re, bb_im


def kernel(*args):
    p = dict(zip(_INPUTS, args, strict=True))
    x0, mem0, tgt = p["x"][0], p["mem"][0], p["loss_target"][0]
    s, d = x0.shape
    me = _slot(*_mesh_pos())
    grads = {}
    wire = {n: (F32 if n in _F32_WIRE else BF16) for n in _COL + _ROW}

    zones = {n: _into_slot(p[n], wire[n], me, "place_" + n) for names in _GATHER_GROUPS for n in names}
    gather, pin = [], jnp.zeros((), F32)
    for i, names in enumerate(_GATHER_GROUPS):
        handle, token = _push_start([], [zones[n] for n in names], f"gather{i}_start")
        gather.append(handle)
        pin = pin + token[0, 0]
    w = {}

    def gathered(i, after):
        for n, full in zip(_GATHER_GROUPS[i], _push_wait(gather[i], after, f"gather{i}_wait")):
            if n in _COL:
                full = full.transpose(1, 0, 2)
            w[n] = full.reshape(-1, full.shape[-1]) if n in _ROW else full.reshape(full.shape[0], -1)

    pending = []

    def exchange(names, gain, tag):
        slots = []
        for n in names:
            g = grads[n]
            if n in _COL:
                pieces = g if isinstance(g, tuple) else (g,)
                g = jnp.concatenate([t.reshape(t.shape[0], -1, p[n].shape[1]).transpose(1, 0, 2) for t in pieces], axis=0)
            else:
                g = g.reshape((N_DEV, -1) + g.shape[1:])
            slots.append(g.astype(wire[n]))
        handle, token = _push_start(slots, [], tag + "_start")
        pending.append((names, slots, handle, tag))
        return gain + token[0, 0]

    def xattn(pre, x_in, hx):
        q = _mm(hx, w[pre + "xa_wq"], out_dtype=BF16, name=pre + "xa_q")
        memn = _norm_fwd(mem0, p[pre + "mem_norm"], pre + "mem_norm_fwd")
        kv = _mm(memn, w[pre + "xa_wkv"], out_dtype=BF16, name=pre + "xa_kv")
        ao = _xattn_fwd(q, kv, pre + "xattn_fwd")
        x_out, hf = _mm(ao, w[pre + "xa_wo"], res=x_in, norm_gain=p[pre + "ffn_norm"], name=pre + "xa_o")
        return x_out, hf, (x_in, hx, q, memn, kv, ao)

    def xattn_bwd(pre, saved, dxo):
        x_in, hx, q, memn, kv, ao = saved
        dao = _mm(dxo, w[pre + "xa_wo"], tb=True, name=pre + "xa_o_dx")
        grads[pre + "xa_wo"] = _mm(ao, dxo, ta=True, out_dtype=BF16, name=pre + "xa_o_dw")
        dq, dkv = _xattn_bwd(q, kv, dao, pre + "xattn_bwd")
        grads[pre + "xa_wq"] = _mm(hx, dq, ta=True, out_dtype=BF16, name=pre + "xa_q_dw")
        grads[pre + "xa_wkv"] = _mm(memn, dkv, ta=True, out_dtype=BF16, name=pre + "xa_kv_dw")
        dmemn = _mm(dkv, w[pre + "xa_wkv"], tb=True, name=pre + "xa_kv_dx")
        gain = exchange((pre + "xa_wo", pre + "xa_wq", pre + "xa_wkv"), p[pre + "xa_norm"], pre + "xa_grads")
        dx_in, grads[pre + "xa_norm"] = _mm_norm_bwd(dq, w[pre + "xa_wq"], x_in, gain, dxo, pre + "xa_q_dx")
        _, grads[pre + "mem_norm"] = _norm_bwd(mem0, p[pre + "mem_norm"], dmemn, jnp.zeros_like(mem0), pre + "mem_norm_bwd")
        return dx_in

    def ffn(pre, x_in, hf, next_gain):
        up = _mm(hf, w[pre + "ffn_w_up"], out_dtype=BF16, name=pre + "ffn_up")
        act = _ffn_act_fwd(up, w[pre + "ffn_conv"], pre + "ffn_act_fwd")
        res = _mm(act, w[pre + "ffn_w_down"], res=x_in, norm_gain=next_gain, name=pre + "ffn_down")
        x_out, h_next = res if next_gain is not None else (res, None)
        return x_out, h_next, (x_in, hf, up, act)

    def ffn_bwd(pre, saved, dxo):
        x_in, hf, up, act = saved
        dact = _mm(dxo, w[pre + "ffn_w_down"], tb=True, out_dtype=BF16, name=pre + "ffn_down_dx")
        grads[pre + "ffn_w_down"] = _mm(act, dxo, ta=True, out_dtype=BF16, name=pre + "ffn_down_dw")
        dpu, dpg, dcu, dcg = _ffn_act_bwd(up, w[pre + "ffn_conv"], dact, pre + "ffn_act_bwd")
        grads[pre + "ffn_conv"] = jnp.concatenate([dcu, dcg], axis=1)
        grads[pre + "ffn_w_up"] = (_mm(hf, dpu, ta=True, out_dtype=BF16, name=pre + "ffn_up_dw_u"),
                                   _mm(hf, dpg, ta=True, out_dtype=BF16, name=pre + "ffn_up_dw_g"))
        gain = exchange((pre + "ffn_w_down", pre + "ffn_w_up", pre + "ffn_conv"), p[pre + "ffn_norm"], pre + "ffn_grads")
        dx_in, grads[pre + "ffn_norm"] = _mm_norm_bwd([dpu, dpg], w[pre + "ffn_w_up"], x_in, gain, dxo, pre + "ffn_up_dx")
        return dx_in

    cos, sin = _rope_tables(s)
    (a_re, a_im, bb_re, bb_im), disc_vjp = jax.vjp(
        _s5_discretise, p["l0_s5_lambda_re"], p["l0_s5_lambda_im"], p["l0_s5_log_dt"], p["l0_s5_b_re"], p["l0_s5_b_im"])
    apow, apow_rev = _s5_pow_tables(_s5_interleave(a_re.reshape(1, -1), a_im.reshape(1, -1)), "l0_s5_pow_tables")
    bbt = _s5_tile_b(bb_re, bb_im).astype(BF16)
    cct = _s5_tile_c(p["l0_s5_c_re"], p["l0_s5_c_im"]).astype(BF16)
    s5_d = p["l0_s5_d"].reshape(1, -1)
    b_glu = p["l0_s5_b_glu"].reshape(1, -1)

    h0 = _norm_fwd(x0, p["l0_mix_norm"] + pin, "l0_mix_norm_fwd")
    gathered(0, h0)
    proj = _mm(h0, w["l0_w_in"], name="l0_in")
    merged, ret_states = _ret_fwd(proj, cos, sin, p["l0_ret_norm"], "l0_ret_fwd")
    st, y, gy = _s5_fwd(proj, bbt, cct, apow, s5_d, "l0_s5_fwd")
    z = _mm(gy, w["l0_s5_w_glu"], name="l0_s5_glu_mm")
    merged = _s5_glu_fwd(y, z, b_glu, merged, "l0_s5_glu_fwd")
    x1, hx0 = _mm(merged, w["l0_w_out"], res=x0, norm_gain=p["l0_xa_norm"], name="l0_out")
    gathered(1, x1)
    x2, hf0, xa0 = xattn("l0_", x1, hx0)
    x3, h1, ff0 = ffn("l0_", x2, hf0, p["l1_mix_norm"])

    gathered(2, x3)
    w1 = w["l1_w_in"]
    wx = jnp.pad(w1, ((0, 0), (0, _round_up(w1.shape[1], LANES) - w1.shape[1])))
    alog_x = jnp.repeat(p["l1_a_log"], GDN_DH).reshape(1, -1)
    dtb_x = jnp.repeat(p["l1_dt_bias"], GDN_DH).reshape(1, -1)
    projx = _mm(h1, wx, name="l1_in")
    qkv = _gdn_conv_fwd(projx, w["l1_conv"], "l1_conv_fwd")
    beta, glog = _gdn_gates_fwd(projx, alog_x, dtb_x, "l1_gates_fwd")
    o_gdn, gdn_states = _gdn_fwd(qkv, beta, glog, projx, p["l1_o_norm"], "l1_gdn_fwd")
    x4, hx1 = _mm(o_gdn, w["l1_w_out"], res=x3, norm_gain=p["l1_xa_norm"], name="l1_out")
    x5, hf1, xa1 = xattn("l1_", x4, hx1)
    gathered(3, x5)
    x6, _, ff1 = ffn("l1_", x5, hf1, None)

    loss_part, dx6, grads["final_norm"] = _loss_head(x6, p["final_norm"], tgt, "loss_head")
    loss = lax.psum(loss_part[0, 0], ("x", "y", "c"))
    dx5 = ffn_bwd("l1_", ff1, dx6)
    dx4 = xattn_bwd("l1_", xa1, dx5)

    do_gdn = _mm(dx4, w["l1_w_out"], tb=True, name="l1_out_dx")
    grads["l1_w_out"] = _mm(o_gdn, dx4, ta=True, out_dtype=BF16, name="l1_out_dw")
    dqkv, dbeta, dglog, dprojx, grads["l1_o_norm"] = _gdn_bwd(
        qkv, beta, glog, projx, p["l1_o_norm"], gdn_states, do_gdn, "l1_gdn_bwd")
    dprojx, grads["l1_conv"] = _gdn_conv_bwd(projx, w["l1_conv"], dqkv, dprojx, "l1_conv_bwd")
    dprojx, dalog_x, ddtb_x = _gdn_gates_bwd(projx, alog_x, dtb_x, dbeta, dglog, dprojx, "l1_gates_bwd")
    grads["l1_w_in"] = _mm(h1, dprojx, ta=True, out_dtype=BF16, name="l1_in_dw")[:, :w1.shape[1]]
    grads["l1_a_log"] = dalog_x[0, :GDN_HEADS]
    grads["l1_dt_bias"] = ddtb_x[0, :GDN_HEADS]
    gain = exchange(("l1_w_out", "l1_w_in", "l1_conv"), p["l1_mix_norm"], "l1_mix_grads")
    dx3, grads["l1_mix_norm"] = _mm_norm_bwd(dprojx, wx, x3, gain, dx4, "l1_in_dx")

    dx2 = ffn_bwd("l0_", ff0, dx3)
    dx1 = xattn_bwd("l0_", xa0, dx2)

    dmerged = _mm(dx1, w["l0_w_out"], tb=True, name="l0_out_dx")
    grads["l0_w_out"] = _mm(merged, dx1, ta=True, out_dtype=BF16, name="l0_out_dw")
    dproj, grads["l0_ret_norm"] = _ret_bwd(proj, cos, sin, p["l0_ret_norm"], ret_states, dmerged, "l0_ret_bwd")
    dzg, dg1, grads["l0_s5_b_glu"] = _s5_glu_bwd(dmerged, y, z, b_glu, "l0_s5_glu_bwd")
    grads["l0_s5_w_glu"] = _mm(gy, dzg, ta=True, out_dtype=BF16, name="l0_s5_glu_dw")
    s5_d_after = exchange(("l0_w_out", "l0_s5_w_glu"), s5_d, "l0_out_grads")
    dg2 = _mm(dzg, w["l0_s5_w_glu"], tb=True, name="l0_s5_glu_dx")
    dproj, da_s5, dbbt, dcct, grads["l0_s5_d"] = _s5_bwd(dg1, dg2, y, proj, st, bbt, cct, apow_rev, s5_d_after, dproj, "l0_s5_bwd")
    dbb_re, dbb_im = _s5_untile_b(dbbt)
    grads["l0_s5_c_re"], grads["l0_s5_c_im"] = _s5_untile_c(dcct)
    da_re, da_im = (t.reshape(S5_GROUPS, S5_STATE) for t in _s5_split(da_s5[0]))
    (grads["l0_s5_lambda_re"], grads["l0_s5_lambda_im"], grads["l0_s5_log_dt"], grads["l0_s5_b_re"],
     grads["l0_s5_b_im"]) = disc_vjp((da_re, da_im, dbb_re, dbb_im))

    def as_2d(t):
        return t.reshape(-1, t.shape[-1])

    def as_row(t):
        return t.reshape(1, -1)

    small_own = _pack_rows([as_row(grads[n]) for n in _REP_SMALL])
    big_own = [as_2d(grads[n].reshape(p[n].shape)) for n in _REP_BIG]
    rep_zones = [_into_slot(small_own, F32, me, "place_rep0")]
    rep_zones += [_into_slot(t.reshape(-1, LANES), BF16, me, f"place_rep{i + 1}") for i, t in enumerate(big_own)]
    rep_handle, rep_token = _push_start([], rep_zones, "rep_grads_start")

    grads["l0_w_in"] = _mm(h0, dproj, ta=True, out_dtype=BF16, pin=rep_token, name="l0_in_dw")
    gain = exchange(("l0_w_in",), p["l0_mix_norm"], "l0_mix_grads")
    dx0, grads["l0_mix_norm"] = _mm_norm_bwd(dproj, w["l0_w_in"], x0, gain, dx1, "l0_in_dx")

    last_own = _pack_rows([as_row(grads[_REP_LAST])])
    last_handle, _ = _push_start([], [_into_slot(last_own, F32, me, "place_rep_last")], "rep_last_start")
    last_land, = _push_wait(last_handle, dx0, "rep_last_wait")
    rep_lands = _push_wait(rep_handle, last_land, "rep_grads_wait")
    rep_land = rep_lands[0]

    outs = {}
    kinds = ("grad_", "delta_", "new_m_", "new_v_")
    for names, slots, handle, tag in pending:
        for n, own_slots, land in zip(names, slots, _push_wait(handle, rep_land, tag + "_wait")):
            shape = p[n].shape
            own = lax.dynamic_index_in_dim(own_slots, me, 0, keepdims=False)
            res = _adamw(land, own, *(p[pre + n].reshape(own.shape) for pre in ("", "m_", "v_")), "adamw_" + n)
            for kind, t in zip(kinds, res):
                outs[kind + n] = t.reshape(shape)
    for n, own, land in zip(_REP_BIG, big_own, rep_lands[1:]):
        res = _adamw(land.reshape((N_DEV,) + own.shape), None, *(as_2d(p[pre + n]) for pre in ("", "m_", "v_")), "adamw_" + n)
        for kind, t in zip(kinds, res):
            outs[kind + n] = t.reshape(p[n].shape)
    for names, land, own, nm in ((_REP_SMALL, rep_land, small_own, "adamw_small"), ((_REP_LAST,), last_land, last_own, "adamw_last")):
        res = _adamw_rows(land, own, *([as_row(p[pre + n]) for n in names] for pre in ("", "m_", "v_")), nm)
        for j, kind in enumerate(kinds):
            for i, n in enumerate(names):
                outs[kind + n] = res[j * len(names) + i].reshape(p[n].shape)

    return (loss, dx0[None]) + tuple(outs[kind + n] for kind in kinds for n in _WEIGHTS)
```

```python
import math

import numpy as np
import jax
import jax.numpy as jnp
from jax import lax
from jax.experimental import pallas as pl
from jax.experimental.pallas import tpu as pltpu

F32 = jnp.float32
BF16 = jnp.bfloat16
EPS = 1e-6
N_DEV = 8
LANES = 128
VMEM_LIMIT = 48 * 1024 * 1024

RET_HEADS, RET_DH, RET_CHUNK = 4, 128, 128
S5_GROUPS, S5_GROUP, S5_STATE = 32, 16, 64
GDN_HEADS, GDN_DH, GDN_CHUNK, GDN_CONV = 8, 128, 64, 4
XA_HEADS, XA_DH = 4, 256
FFN_CONV = 3
SCAN_ROWS = 256

ADAM_LR, ADAM_B1, ADAM_B2, ADAM_EPS, ADAM_WD, ADAM_STEP = 0.001, 0.9, 0.999, 1e-08, 0.01, 10


def _cp(*sem):
    return pltpu.CompilerParams(dimension_semantics=sem if sem else None, vmem_limit_bytes=VMEM_LIMIT)


def _tile(n, cap):
    if n <= cap:
        return n
    best = None
    for t in range(LANES, cap + 1, LANES):
        if n % t == 0:
            best = t
    assert best is not None, n
    return best


def _dot(a, b, ca=1, cb=0, precision=None):
    return lax.dot_general(a, b, (((ca,), (cb,)), ((), ())), precision=precision, preferred_element_type=F32)


def _mxu(a, b, ca=1, cb=0):
    return _dot(a.astype(BF16), b.astype(BF16), ca, cb)


def _sigmoid(x):
    return 0.5 * jnp.tanh(0.5 * x) + 0.5


def _shift_down(x, k):
    r = pltpu.roll(x, k, 0)
    row = lax.broadcasted_iota(jnp.int32, (8,) + x.shape[1:], 0)
    return jnp.concatenate([jnp.where(row >= k, r[:8], 0.0), r[8:]], axis=0)


def _shift_up(x, k):
    n = x.shape[0]
    r = pltpu.roll(x, n - k, 0)
    row = lax.broadcasted_iota(jnp.int32, (8,) + x.shape[1:], 0)
    return jnp.concatenate([r[:n - 8], jnp.where(row < 8 - k, r[n - 8:], 0.0)], axis=0)


def _mesh_pos():
    return lax.axis_index("x"), lax.axis_index("y"), lax.axis_index("c")


def _slot(px, py, pc):
    return 4 * px + 2 * py + pc


def _all_peers(x, y, c):
    flips = [(fx, fy, fc) for fx in (0, 1) for fy in (0, 1) for fc in (0, 1)][1:]
    return [(1 - x if fx else x, 1 - y if fy else y, 1 - c if fc else c) for fx, fy, fc in flips]


_HBM = pl.BlockSpec(memory_space=pltpu.HBM)
_SEM = pl.BlockSpec(memory_space=pltpu.SEMAPHORE)
N_PEERS = N_DEV - 1


def _push_copies(srcs, lands, send_sems, recv_sems, start):
    x, y, c = _mesh_pos()
    me = _slot(x, y, c)
    out = []
    for k, to in enumerate(_all_peers(x, y, c)):
        for a in range(len(lands)):
            src = srcs[a].at[_slot(*to)] if a < len(srcs) else lands[a].at[me]
            dst = lands[a].at[me if start else _slot(*to)]
            out.append(pltpu.make_async_remote_copy(
                src_ref=src, dst_ref=dst, send_sem=send_sems.at[a * N_PEERS + k], recv_sem=recv_sems.at[a * N_PEERS + k],
                device_id=to, device_id_type=pl.DeviceIdType.MESH))
    return out


def _into_slot(x, dtype, me, name):
    r, c = x.shape
    cap = max(16, 512 * 1024 // c)
    tr = max(t for t in range(16, min(r, cap) + 1, 16) if r % t == 0) if r % 16 == 0 else r

    def body(me_ref, x_ref, o_ref):
        o_ref[...] = x_ref[...].astype(dtype)

    return pl.pallas_call(
        body, name=name, out_shape=jax.ShapeDtypeStruct((N_DEV, r, c), dtype),
        grid_spec=pltpu.PrefetchScalarGridSpec(
            num_scalar_prefetch=1, grid=(r // tr,),
            in_specs=[pl.BlockSpec((tr, c), lambda i, me_ref: (i, 0))],
            out_specs=pl.BlockSpec((None, tr, c), lambda i, me_ref: (me_ref[0], i, 0))),
        compiler_params=_cp("parallel"),
    )(me.reshape(1).astype(jnp.int32), x)


def _push_start(scatter, gather_lands, name):
    ns, n = len(scatter), len(scatter) + len(gather_lands)
    lands = [lax.empty(a.shape, a.dtype) for a in scatter] + list(gather_lands)

    def body(*refs):
        srcs, zones = refs[:ns], refs[ns:ns + n]
        for cp in _push_copies(srcs, zones, refs[ns + n], refs[ns + n + 1], True):
            cp.start()
        refs[-1][...] = jnp.zeros((8, LANES), F32)

    hbm_in = [pltpu.with_memory_space_constraint(a, pltpu.HBM) for a in list(scatter) + lands]
    res = pl.pallas_call(
        body, name=name,
        out_shape=(pltpu.SemaphoreType.DMA((n * N_PEERS,)), pltpu.SemaphoreType.DMA((n * N_PEERS,)))
        + tuple(pltpu.HBM(a.shape, a.dtype) for a in list(scatter) + lands)
        + (jax.ShapeDtypeStruct((8, LANES), F32),),
        in_specs=[_HBM] * (ns + n),
        out_specs=(_SEM, _SEM) + (_HBM,) * (ns + n) + (pl.BlockSpec(memory_space=pltpu.VMEM),),
        input_output_aliases={i: 2 + i for i in range(ns + n)},
        compiler_params=pltpu.CompilerParams(has_side_effects=pltpu.SideEffectType.DATAFLOW_SIDE_EFFECTING),
    )(*hbm_in)
    return (res[0], res[1], res[2:2 + ns], res[2 + ns:2 + ns + n]), res[-1]


def _push_wait(handle, after, name):
    send_sems, recv_sems, srcs, lands = handle
    ns, n = len(srcs), len(lands)

    def body(*refs):
        for cp in _push_copies(refs[:ns], refs[ns:ns + n], refs[ns + n], refs[ns + n + 1], False):
            cp.wait_send()
            cp.wait_recv()

    res = pl.pallas_call(
        body, name=name,
        out_shape=tuple(pltpu.HBM(a.shape, a.dtype) for a in list(srcs) + list(lands)),
        in_specs=[_HBM] * (ns + n) + [_SEM, _SEM, pl.BlockSpec(memory_space=pl.ANY)],
        out_specs=(_HBM,) * (ns + n),
        input_output_aliases={i: i for i in range(ns + n)},
        compiler_params=pltpu.CompilerParams(has_side_effects=pltpu.SideEffectType.DATAFLOW_SIDE_EFFECTING),
    )(*srcs, *lands, send_sems, recv_sems, after)
    return res[ns:]


def _mm(a, b, *, ta=False, tb=False, out_dtype=F32, res=None, pin=None, norm_gain=None, name="mm"):
    m, k = (a.shape[1], a.shape[0]) if ta else a.shape
    n = b.shape[0] if tb else b.shape[1]
    assert k == (b.shape[1] if tb else b.shape[0]), (a.shape, b.shape, ta, tb)
    tm, tn, tk = _tile(m, 1408), _tile(n, 1536), _tile(k, 1408)
    nk = k // tk
    has_res = res is not None
    has_norm = norm_gain is not None
    assert not has_norm or tn == n
    n_in = 2 + has_res + (pin is not None) + has_norm

    def body(*refs):
        a_ref, b_ref = refs[:2]
        r_ref = refs[2] if has_res else None
        o_ref = refs[n_in]
        part = _mxu(a_ref[...], b_ref[...], 0 if ta else 1, 1 if tb else 0)

        def finish(r):
            if has_res:
                r = r + r_ref[...].astype(F32)
            o_ref[...] = r.astype(out_dtype)
            if has_norm:
                scale = lax.rsqrt(jnp.mean(r * r, axis=-1, keepdims=True) + EPS)
                refs[n_in + 1][...] = (r * scale * refs[n_in - 1][...]).astype(BF16)

        if nk == 1:
            finish(part)
            return
        acc = refs[-1]
        kk = pl.program_id(2)

        @pl.when(kk == 0)
        def _():
            acc[...] = part

        @pl.when(kk > 0)
        def _():
            acc[...] += part

        @pl.when(kk == nk - 1)
        def _():
            finish(acc[...])

    a_spec = pl.BlockSpec((tk, tm), lambda i, j, kk: (kk, i)) if ta else pl.BlockSpec((tm, tk), lambda i, j, kk: (i, kk))
    b_spec = pl.BlockSpec((tn, tk), lambda i, j, kk: (j, kk)) if tb else pl.BlockSpec((tk, tn), lambda i, j, kk: (kk, j))
    o_spec = pl.BlockSpec((tm, tn), lambda i, j, kk: (i, j))
    in_specs = [a_spec, b_spec] + ([o_spec] if has_res else [])
    args = (a, b) + ((res,) if has_res else ())
    if pin is not None:
        in_specs.append(pl.BlockSpec(pin.shape, lambda i, j, kk: (0, 0)))
        args += (pin,)
    if has_norm:
        in_specs.append(pl.BlockSpec((1, n), lambda i, j, kk: (0, 0)))
        args += (norm_gain.reshape(1, n),)
    out = jax.ShapeDtypeStruct((m, n), out_dtype)
    return pl.pallas_call(
        body, name=name, grid=(m // tm, n // tn, nk), in_specs=in_specs,
        out_specs=[o_spec, o_spec] if has_norm else o_spec,
        out_shape=[out, jax.ShapeDtypeStruct((m, n), BF16)] if has_norm else out,
        scratch_shapes=[pltpu.VMEM((tm, tn), F32)] if nk > 1 else [],
        compiler_params=_cp("parallel", "parallel", "arbitrary"),
    )(*args)


def _mm_norm_bwd(dy, w, x, g, dres, name, pin=None):
    dys = list(dy) if isinstance(dy, (list, tuple)) else [dy]
    nq = len(dys)
    s, kq = dys[0].shape
    d = w.shape[0]
    tm, tk = min(1024 if nq == 1 else 512, s), _tile(kq, 1408)
    per = kq // tk
    nk = nq * per
    n_in = nq + 4 + (pin is not None)

    def body(*refs):
        w_ref, x_ref, g_ref, dres_ref = refs[nq:nq + 4]
        dx_ref, dg_ref = refs[n_in], refs[n_in + 1]
        i, kk = pl.program_id(0), pl.program_id(1)

        @pl.when((i == 0) & (kk == 0))
        def _():
            dg_ref[...] = jnp.zeros_like(dg_ref)

        def finish(dh):
            xv = x_ref[...]
            r = lax.rsqrt(jnp.mean(xv * xv, axis=-1, keepdims=True) + EPS)
            xn = xv * r
            dg_ref[...] += jnp.sum(dh * xn, axis=0, keepdims=True)
            dhg = dh * g_ref[...]
            dx_ref[...] = dres_ref[...] + r * (dhg - xn * jnp.mean(dhg * xn, axis=-1, keepdims=True))

        if nk == 1:
            finish(_mxu(refs[0][...], w_ref[...], 1, 1))
            return
        acc = refs[-1]
        for q in range(nq):
            @pl.when((kk >= q * per) & (kk < (q + 1) * per))
            def _(q=q):
                part = _mxu(refs[q][...], w_ref[...], 1, 1)

                @pl.when(kk == 0)
                def _():
                    acc[...] = part

                @pl.when(kk > 0)
                def _():
                    acc[...] += part

        @pl.when(kk == nk - 1)
        def _():
            finish(acc[...])

    row = pl.BlockSpec((tm, d), lambda i, kk: (i, 0))
    vec = pl.BlockSpec((1, d), lambda i, kk: (0, 0))
    in_specs = [pl.BlockSpec((tm, tk), lambda i, kk, q=q: (i, jnp.clip(kk - q * per, 0, per - 1))) for q in range(nq)]
    in_specs += [pl.BlockSpec((d, tk), lambda i, kk: (0, kk)), row, vec, row]
    args = (*dys, w, x, g.reshape(1, d), dres)
    if pin is not None:
        in_specs.append(pl.BlockSpec(pin.shape, lambda i, kk: (0, 0)))
        args += (pin,)
    return pl.pallas_call(
        body, name=name, grid=(s // tm, nk), in_specs=in_specs, out_specs=[row, vec],
        out_shape=[jax.ShapeDtypeStruct((s, d), F32), jax.ShapeDtypeStruct((1, d), F32)],
        scratch_shapes=[pltpu.VMEM((tm, d), F32)] if nk > 1 else [],
        compiler_params=_cp("arbitrary", "arbitrary"),
    )(*args)


def _norm_fwd(x, g, name):
    s, d = x.shape
    tr = min(512, s)

    def body(x_ref, g_ref, o_ref):
        xv = x_ref[...]
        r = lax.rsqrt(jnp.mean(xv * xv, axis=-1, keepdims=True) + EPS)
        o_ref[...] = (xv * r * g_ref[...]).astype(BF16)

    row = pl.BlockSpec((tr, d), lambda i: (i, 0))
    return pl.pallas_call(
        body, name=name, grid=(s // tr,), in_specs=[row, pl.BlockSpec((1, d), lambda i: (0, 0))],
        out_specs=row, out_shape=jax.ShapeDtypeStruct((s, d), BF16), compiler_params=_cp("parallel"),
    )(x, g.reshape(1, d))


def _norm_bwd(x, g, dh, dres, name):
    s, d = x.shape
    tr = min(512, s)

    def body(x_ref, g_ref, dh_ref, dres_ref, dx_ref, dg_ref):
        @pl.when(pl.program_id(0) == 0)
        def _():
            dg_ref[...] = jnp.zeros_like(dg_ref)

        xv = x_ref[...]
        r = lax.rsqrt(jnp.mean(xv * xv, axis=-1, keepdims=True) + EPS)
        xn = xv * r
        dhv = dh_ref[...].astype(F32)
        dg_ref[...] += jnp.sum(dhv * xn, axis=0, keepdims=True)
        dhg = dhv * g_ref[...]
        dx_ref[...] = dres_ref[...] + r * (dhg - xn * jnp.mean(dhg * xn, axis=-1, keepdims=True))

    row = pl.BlockSpec((tr, d), lambda i: (i, 0))
    vec = pl.BlockSpec((1, d), lambda i: (0, 0))
    return pl.pallas_call(
        body, name=name, grid=(s // tr,), in_specs=[row, vec, row, row], out_specs=[row, vec],
        out_shape=[jax.ShapeDtypeStruct((s, d), F32), jax.ShapeDtypeStruct((1, d), F32)],
        compiler_params=_cp("arbitrary"),
    )(x, g.reshape(1, d), dh, dres)


def _loss_head(x, g, tgt, name):
    s, d = x.shape
    tr = min(512, s)

    def body(x_ref, g_ref, t_ref, l_ref, dx_ref, dg_ref):
        @pl.when(pl.program_id(0) == 0)
        def _():
            dg_ref[...] = jnp.zeros_like(dg_ref)
            l_ref[...] = jnp.zeros_like(l_ref)

        xv = x_ref[...]
        r = lax.rsqrt(jnp.mean(xv * xv, axis=-1, keepdims=True) + EPS)
        xn = xv * r
        err = xn * g_ref[...] - t_ref[...]
        part = 0.5 * jnp.sum(jnp.mean(err * err, axis=-1, keepdims=True), axis=0, keepdims=True)
        l_ref[...] += jnp.broadcast_to(part, l_ref.shape)
        dy = err * (1.0 / d)
        dg_ref[...] += jnp.sum(dy * xn, axis=0, keepdims=True)
        dyg = dy * g_ref[...]
        dx_ref[...] = r * (dyg - xn * jnp.mean(dyg * xn, axis=-1, keepdims=True))

    row = pl.BlockSpec((tr, d), lambda i: (i, 0))
    vec = pl.BlockSpec((1, d), lambda i: (0, 0))
    return pl.pallas_call(
        body, name=name, grid=(s // tr,), in_specs=[row, vec, row],
        out_specs=[pl.BlockSpec((1, LANES), lambda i: (0, 0)), row, vec],
        out_shape=[jax.ShapeDtypeStruct((1, LANES), F32), jax.ShapeDtypeStruct((s, d), F32),
                   jax.ShapeDtypeStruct((1, d), F32)],
        compiler_params=_cp("arbitrary"),
    )(x, g.reshape(1, d), tgt)


def _sum_slots(landed_slot, own):
    me = _slot(*_mesh_pos())
    mine = own.astype(F32)
    g = jnp.where(me == 0, mine, landed_slot(0).astype(F32))
    for i in range(1, N_DEV):
        g = g + jnp.where(me == i, mine, landed_slot(i).astype(F32))
    return g


def _adam_update(g, w, m, v):
    mm = ADAM_B1 * m + (1.0 - ADAM_B1) * g
    vv = ADAM_B2 * v + (1.0 - ADAM_B2) * (g * g)
    m_hat = mm / (1.0 - ADAM_B1 ** ADAM_STEP)
    v_hat = vv / (1.0 - ADAM_B2 ** ADAM_STEP)
    return g, -ADAM_LR * (m_hat / (jnp.sqrt(v_hat) + ADAM_EPS) + ADAM_WD * w), mm, vv


def _adamw_rows(landed, own, ws, ms, vs, name):
    k = len(ws)
    sizes = [w.shape[1] for w in ws]

    def body(*refs):
        p_ref, o_ref = refs[:2]
        w_refs, m_refs, v_refs = refs[2:2 + k], refs[2 + k:2 + 2 * k], refs[2 + 2 * k:2 + 3 * k]
        outs = refs[2 + 3 * k:]
        for i, n in enumerate(sizes):
            g = _sum_slots(lambda s: p_ref[s, i:i + 1, :n], o_ref[i:i + 1, :n])
            res = _adam_update(g, w_refs[i][...], m_refs[i][...], v_refs[i][...])
            for j in range(4):
                outs[j * k + i][...] = res[j]

    return pl.pallas_call(
        body, name=name, out_shape=[jax.ShapeDtypeStruct((1, n), F32) for _ in range(4) for n in sizes],
    )(landed, own, *ws, *ms, *vs)


def _adamw(landed, own, w, m, v, name):
    r, c = w.shape
    cap = max(8, 256 * 1024 // c)
    tr = max(t for t in range(8, min(r, cap) + 1, 8) if r % t == 0) if r % 8 == 0 else r
    gathered = own is None

    def body(*refs):
        p_ref = refs[0]
        w_ref, m_ref, v_ref, g_ref, d_ref, nm_ref, nv_ref = refs[1 if gathered else 2:]
        if gathered:
            g = p_ref[0].astype(F32)
            for i in range(1, N_DEV):
                g = g + p_ref[i].astype(F32)
        else:
            g = _sum_slots(lambda i: p_ref[i], refs[1][...])
        g_ref[...], d_ref[...], nm_ref[...], nv_ref[...] = _adam_update(g, w_ref[...], m_ref[...], v_ref[...])

    blk = pl.BlockSpec((tr, c), lambda i: (i, 0))
    n_blk = 3 if gathered else 4
    return pl.pallas_call(
        body, name=name, grid=(r // tr,),
        in_specs=[pl.BlockSpec((N_DEV, tr, c), lambda i: (0, i, 0))] + [blk] * n_blk,
        out_specs=[blk] * 4, out_shape=[jax.ShapeDtypeStruct((r, c), F32)] * 4,
        compiler_params=_cp("parallel"),
    )(*((landed,) if gathered else (landed, own)), w, m, v)


def _conv_taps(x, kw):
    return [_shift_down(x, kw - 1 - j) for j in range(kw - 1)] + [x]


def _conv_fwd(taps, w_ref):
    acc = w_ref[0:1, :] * taps[0]
    for j in range(1, len(taps)):
        acc = acc + w_ref[j:j + 1, :] * taps[j]
    return acc


def _conv_bwd(taps, dy, w_ref, dw_ref):
    kw = len(taps)
    dx = w_ref[kw - 1:kw, :] * dy
    for j in range(kw):
        dw_ref[j:j + 1, :] = jnp.sum(dy * taps[j], axis=0, keepdims=True)
        if j < kw - 1:
            dx = dx + w_ref[j:j + 1, :] * _shift_up(dy, kw - 1 - j)
    return dx


def _ffn_act_fwd(pre, cw, name):
    s, f2 = pre.shape
    nt = f2 // 2 // LANES

    def body(pu_ref, pg_ref, wu_ref, wg_ref, o_ref):
        up = _conv_fwd(_conv_taps(pu_ref[...].astype(F32), FFN_CONV), wu_ref)
        gate = _conv_fwd(_conv_taps(pg_ref[...].astype(F32), FFN_CONV), wg_ref)
        o_ref[...] = (gate * _sigmoid(gate) * up).astype(BF16)

    def col(rows, off):
        return pl.BlockSpec((rows, LANES), lambda j: (0, j + off))

    return pl.pallas_call(
        body, name=name, grid=(nt,),
        in_specs=[col(s, 0), col(s, nt), col(FFN_CONV, 0), col(FFN_CONV, nt)], out_specs=col(s, 0),
        out_shape=jax.ShapeDtypeStruct((s, f2 // 2), BF16), compiler_params=_cp("parallel"),
    )(pre, pre, cw, cw)


def _ffn_act_bwd(pre, cw, dact, name):
    s, f2 = pre.shape
    f = f2 // 2
    nt = f // LANES

    def body(pu_ref, pg_ref, wu_ref, wg_ref, da_ref, dpu_ref, dpg_ref, dwu_ref, dwg_ref):
        pu, pg = pu_ref[...].astype(F32), pg_ref[...].astype(F32)
        tu, tg = _conv_taps(pu, FFN_CONV), _conv_taps(pg, FFN_CONV)
        up = _conv_fwd(tu, wu_ref)
        gate = _conv_fwd(tg, wg_ref)
        sg = _sigmoid(gate)
        da = da_ref[...].astype(F32)
        dup = da * gate * sg
        dgate = da * up * (sg * (1.0 + gate * (1.0 - sg)))
        dpu_ref[...] = _conv_bwd(tu, dup, wu_ref, dwu_ref).astype(BF16)
        dpg_ref[...] = _conv_bwd(tg, dgate, wg_ref, dwg_ref).astype(BF16)

    def col(rows, off):
        return pl.BlockSpec((rows, LANES), lambda j: (0, j + off))

    return pl.pallas_call(
        body, name=name, grid=(nt,),
        in_specs=[col(s, 0), col(s, nt), col(FFN_CONV, 0), col(FFN_CONV, nt), col(s, 0)],
        out_specs=[col(s, 0), col(s, 0), col(FFN_CONV, 0), col(FFN_CONV, 0)],
        out_shape=[jax.ShapeDtypeStruct((s, f), BF16), jax.ShapeDtypeStruct((s, f), BF16),
                   jax.ShapeDtypeStruct((FFN_CONV, f), F32), jax.ShapeDtypeStruct((FFN_CONV, f), F32)],
        compiler_params=_cp("parallel"),
    )(pre, pre, cw, cw, dact)


def _xa_probs(qh, kh):
    sc = _mxu(qh, kh, 1, 1) * (XA_DH ** -0.5)
    e = jnp.exp(sc - jnp.max(sc, axis=-1, keepdims=True))
    return e / jnp.sum(e, axis=-1, keepdims=True)


def _xattn_fwd(q, kv, name):
    s, d = q.shape
    m = kv.shape[0]
    tr = min(512, s)

    def body(q_ref, kv_ref, o_ref):
        for h in range(XA_HEADS):
            lo, hi = h * XA_DH, (h + 1) * XA_DH
            p = _xa_probs(q_ref[:, lo:hi], kv_ref[:, lo:hi])
            o_ref[:, lo:hi] = _mxu(p, kv_ref[:, d + lo:d + hi]).astype(BF16)

    row = pl.BlockSpec((tr, d), lambda i: (i, 0))
    return pl.pallas_call(
        body, name=name, grid=(s // tr,), in_specs=[row, pl.BlockSpec((m, 2 * d), lambda i: (0, 0))],
        out_specs=row, out_shape=jax.ShapeDtypeStruct((s, d), BF16), compiler_params=_cp("parallel"),
    )(q, kv)


def _xattn_bwd(q, kv, do, name):
    s, d = q.shape
    m = kv.shape[0]
    tr = min(512, s)

    def body(q_ref, kv_ref, do_ref, dq_ref, dkv_ref):
        @pl.when(pl.program_id(0) == 0)
        def _():
            dkv_ref[...] = jnp.zeros_like(dkv_ref)

        for h in range(XA_HEADS):
            lo, hi = h * XA_DH, (h + 1) * XA_DH
            qh, kh, vh = q_ref[:, lo:hi], kv_ref[:, lo:hi], kv_ref[:, d + lo:d + hi]
            doh = do_ref[:, lo:hi]
            p = _xa_probs(qh, kh)
            dp = _mxu(doh, vh, 1, 1)
            ds = p * (dp - jnp.sum(p * dp, axis=-1, keepdims=True)) * (XA_DH ** -0.5)
            dq_ref[:, lo:hi] = _mxu(ds, kh).astype(BF16)
            dkv_ref[:, lo:hi] += _mxu(ds, qh, 0, 0)
            dkv_ref[:, d + lo:d + hi] += _mxu(p, doh, 0, 0)

    row = pl.BlockSpec((tr, d), lambda i: (i, 0))
    full = pl.BlockSpec((m, 2 * d), lambda i: (0, 0))
    return pl.pallas_call(
        body, name=name, grid=(s // tr,), in_specs=[row, full, row], out_specs=[row, full],
        out_shape=[jax.ShapeDtypeStruct((s, d), BF16), jax.ShapeDtypeStruct((m, 2 * d), F32)],
        compiler_params=_cp("arbitrary"),
    )(q, kv, do)


def _ret_tables():
    c = RET_CHUNK
    lg = np.log1p(-np.exp2(-5.0 - np.arange(RET_HEADS, dtype=np.float32))).astype(np.float32)
    idx = np.arange(c, dtype=np.float32)
    diff = idx[:, None] - idx[None, :]
    intra = np.where(diff >= 0, np.exp(lg[:, None, None] * np.where(diff >= 0, diff, 0.0)), 0.0)
    rk = np.broadcast_to(np.exp(lg[:, None] * (c - 1 - idx))[:, :, None], (RET_HEADS, c, LANES))
    rq = np.broadcast_to(np.exp(lg[:, None] * (idx + 1))[:, :, None], (RET_HEADS, c, LANES))
    return jnp.asarray(np.stack([intra, rk, rq], axis=1).astype(np.float32))


def _rope_tables(s):
    half = RET_DH // 2
    inv = jnp.exp(-math.log(10000.0) * jnp.arange(half, dtype=F32) / half)
    ang = jnp.arange(s, dtype=F32)[:, None] * inv[None, :]
    cos, sin = jnp.cos(ang), jnp.sin(ang)
    return jnp.concatenate([cos, cos], axis=1), jnp.concatenate([-sin, sin], axis=1)


def _ret_specs(n_of):
    c, w = RET_CHUNK, RET_HEADS * RET_DH

    def part(off):
        return pl.BlockSpec((c, w), lambda n: (n_of(n), off))

    pos = pl.BlockSpec((c, RET_DH), lambda n: (n_of(n), 0))
    gain = pl.BlockSpec((1, w), lambda n: (0, 0))
    tab = pl.BlockSpec((RET_HEADS, 3, c, LANES), lambda n: (0, 0, 0, 0))
    st = pl.BlockSpec((RET_HEADS, None, RET_DH, RET_DH), lambda n: (0, n_of(n), 0, 0))
    return part, pos, gain, tab, st


def _rheads(x):
    return jnp.stack([x[:, h * RET_DH:(h + 1) * RET_DH] for h in range(RET_HEADS)], axis=0)


def _runheads(x):
    return jnp.concatenate([x[h] for h in range(RET_HEADS)], axis=1)


def _rope(x, cos, sin):
    return x * cos + pltpu.roll(x, RET_DH // 2, 2) * sin


def _ret_chunk(q_ref, k_ref, v_ref, cos_ref, sin_ref, tab_ref, prev):
    cos, sin = cos_ref[...], sin_ref[...]
    q = _rope(_rheads(q_ref[...]), cos, sin)
    k = _rope(_rheads(k_ref[...]), cos, sin) * (RET_DH ** -0.5)
    v = _rheads(v_ref[...])
    scores = _bmxu(q, k, 2, 2) * tab_ref[:, 0]
    qdec = q * tab_ref[:, 2]
    kdec = k * tab_ref[:, 1]
    o = _bmxu(scores, v) + _bmxu(qdec, prev)
    return q, k, v, scores, qdec, kdec, o


def _ret_fwd(proj, cos, sin, gain, name):
    s = proj.shape[0]
    c = RET_CHUNK
    nc = s // c
    part, pos, gvec, tab, st = _ret_specs(lambda n: n)

    def body(q_ref, k_ref, v_ref, g_ref, cos_ref, sin_ref, rn_ref, tab_ref, o_ref, st_ref, state):
        @pl.when(pl.program_id(0) == 0)
        def _():
            state[...] = jnp.zeros_like(state)

        prev = state[...]
        st_ref[...] = prev
        _, _, v, _, _, kdec, o = _ret_chunk(q_ref, k_ref, v_ref, cos_ref, sin_ref, tab_ref, prev)
        state[...] = prev * tab_ref[:, 2, c - 1:c, :] + _bmxu(kdec, v, 1, 1)
        r = lax.rsqrt(jnp.mean(o * o, axis=-1, keepdims=True) + EPS)
        gate = g_ref[...]
        o_ref[...] = (_runheads(o * r) * rn_ref[...] * (gate * _sigmoid(gate))).astype(BF16)

    return pl.pallas_call(
        body, name=name, grid=(nc,),
        in_specs=[part(0), part(1), part(2), part(3), pos, pos, gvec, tab],
        out_specs=[part(0), st],
        out_shape=[jax.ShapeDtypeStruct((s, 2 * RET_HEADS * RET_DH), BF16),
                   jax.ShapeDtypeStruct((RET_HEADS, nc, RET_DH, RET_DH), F32)],
        scratch_shapes=[pltpu.VMEM((RET_HEADS, RET_DH, RET_DH), F32)],
        compiler_params=_cp("arbitrary"),
    )(proj, proj, proj, proj, cos, sin, gain.reshape(1, -1), _ret_tables())


def _ret_bwd(proj, cos, sin, gain, states, dmerged, name):
    s = proj.shape[0]
    c = RET_CHUNK
    nc = s // c
    width = RET_HEADS * RET_DH
    part, pos, gvec, tab, st = _ret_specs(lambda n: nc - 1 - n)

    def body(q_ref, k_ref, v_ref, g_ref, cos_ref, sin_ref, rn_ref, tab_ref, st_ref, do_ref,
             dp_ref, drn_ref, carry):
        @pl.when(pl.program_id(0) == 0)
        def _():
            carry[...] = jnp.zeros_like(carry)
            drn_ref[...] = jnp.zeros_like(drn_ref)

        prev = st_ref[...]
        q, k, v, scores, qdec, kdec, o = _ret_chunk(q_ref, k_ref, v_ref, cos_ref, sin_ref, tab_ref, prev)
        r = lax.rsqrt(jnp.mean(o * o, axis=-1, keepdims=True) + EPS)
        on = o * r
        on2 = _runheads(on)
        gate = g_ref[...]
        sg = _sigmoid(gate)
        sil = gate * sg
        dout = do_ref[...]
        rn = rn_ref[...]
        dp_ref[:, 3 * width:] = (dout * on2 * rn * (sg * (1.0 + gate * (1.0 - sg)))).astype(BF16)
        drn_ref[...] += jnp.sum(dout * on2 * sil, axis=0, keepdims=True)
        don = _rheads(dout * rn * sil)
        do = r * (don - on * jnp.mean(don * on, axis=-1, keepdims=True))
        dc = carry[...]
        dsc = _bmxu(do, v, 2, 2) * tab_ref[:, 0]
        dq = _bmxu(dsc, k) + _bmxu(do, prev, 2, 2) * tab_ref[:, 2]
        dk = _bmxu(dsc, q, 1, 1) + _bmxu(v, dc, 2, 2) * tab_ref[:, 1]
        dv = _bmxu(scores, do, 1, 1) + _bmxu(kdec, dc)
        carry[...] = _bmxu(qdec, do, 1, 1) + dc * tab_ref[:, 2, c - 1:c, :]
        cos, sin = cos_ref[...], sin_ref[...]
        dk = dk * (RET_DH ** -0.5)
        dp_ref[:, :width] = _runheads(dq * cos + pltpu.roll(dq * sin, RET_DH // 2, 2)).astype(BF16)
        dp_ref[:, width:2 * width] = _runheads(dk * cos + pltpu.roll(dk * sin, RET_DH // 2, 2)).astype(BF16)
        dp_ref[:, 2 * width:3 * width] = _runheads(dv).astype(BF16)

    return pl.pallas_call(
        body, name=name, grid=(nc,),
        in_specs=[part(0), part(1), part(2), part(3), pos, pos, gvec, tab, st, part(0)],
        out_specs=[pl.BlockSpec((c, 4 * width), lambda n: (nc - 1 - n, 0)), gvec],
        out_shape=[jax.ShapeDtypeStruct(proj.shape, BF16), jax.ShapeDtypeStruct((1, width), F32)],
        scratch_shapes=[pltpu.VMEM((RET_HEADS, RET_DH, RET_DH), F32)],
        compiler_params=_cp("arbitrary"),
    )(proj, proj, proj, proj, cos, sin, gain.reshape(1, -1), _ret_tables(), states, dmerged)


S5_TILE = 512


def _cmul_add(xr, xi, ar, ai, yr, yi):
    return xr + ar * yr - ai * yi, xi + ar * yi + ai * yr


def _s5_pow_tables(a_il, name):
    r = SCAN_ROWS
    t = S5_TILE
    w2 = a_il.shape[1]

    def body(a_ref, up_ref, dn_ref):
        for j in range(w2 // (2 * t)):
            re, im = pl.ds(2 * t * j, t), pl.ds(2 * t * j + t, t)
            up_ref[0:1, re] = a_ref[:, re]
            up_ref[0:1, im] = a_ref[:, im]
            dn_ref[r - 1:r, re] = a_ref[:, re]
            dn_ref[r - 1:r, im] = -a_ref[:, im]
            n = 1
            while n < r:
                lr, li = up_ref[n - 1:n, re], up_ref[n - 1:n, im]
                xr, xi = up_ref[0:n, re], up_ref[0:n, im]
                up_ref[n:2 * n, re] = xr * lr - xi * li
                up_ref[n:2 * n, im] = xr * li + xi * lr
                yr, yi = dn_ref[r - n:r, re], dn_ref[r - n:r, im]
                dn_ref[r - 2 * n:r - n, re] = yr * lr + yi * li
                dn_ref[r - 2 * n:r - n, im] = yi * lr - yr * li
                n *= 2

    return pl.pallas_call(
        body, name=name, out_shape=[jax.ShapeDtypeStruct((r, w2), F32)] * 2, compiler_params=_cp(),
    )(a_il)


_GELU_C = math.sqrt(2.0 / math.pi)
_GELU_A = 0.044715


def _gelu(y):
    return 0.5 * y * (1.0 + jnp.tanh(_GELU_C * (y + _GELU_A * y * y * y)))


def _gelu_grad(y):
    th = jnp.tanh(_GELU_C * (y + _GELU_A * y * y * y))
    return 0.5 * (1.0 + th) + 0.5 * y * (1.0 - th * th) * _GELU_C * (1.0 + 3.0 * _GELU_A * y * y)


def _rows_shift(x, k, axis, up):
    n = x.shape[axis]
    idx = lax.broadcasted_iota(jnp.int32, x.shape, axis)
    if up:
        return jnp.where(idx < n - k, pltpu.roll(x, n - k, axis), 0.0)
    return jnp.where(idx >= k, pltpu.roll(x, k, axis), 0.0)


def _scan_block(xr, xi, pr, pi, cr, ci, rev):
    r, w = xr.shape
    nt = r // 8
    x3r, x3i = xr.reshape(nt, 8, w), xi.reshape(nt, 8, w)
    p3r, p3i = pr.reshape(nt, 8, w), pi.reshape(nt, 8, w)

    def power(rows):
        t = r - rows if rev else rows - 1
        return pr[t:t + 1, :], pi[t:t + 1, :]

    tile_row = lax.broadcasted_iota(jnp.int32, (8, w), 0)
    for sh in (1, 2, 4):
        ar, ai = power(sh)
        keep = tile_row < 8 - sh if rev else tile_row >= sh
        mr, mi = jnp.where(keep, ar, 0.0)[None], jnp.where(keep, ai, 0.0)[None]
        turn = 8 - sh if rev else sh
        x3r, x3i = _cmul_add(x3r, x3i, mr, mi, pltpu.roll(x3r, turn, 1), pltpu.roll(x3i, turn, 1))
    edge = 0 if rev else 7
    lr, li = x3r[:, edge, :], x3i[:, edge, :]
    sh = 1
    while sh < nt:
        ar, ai = power(8 * sh)
        lr, li = _cmul_add(lr, li, ar, ai, _rows_shift(lr, sh, 0, rev), _rows_shift(li, sh, 0, rev))
        sh *= 2
    tr_, ti_ = p3r[:, edge, :], p3i[:, edge, :]
    first = lax.broadcasted_iota(jnp.int32, (nt, w), 0) == (nt - 1 if rev else 0)
    wr = jnp.where(first, 1.0, _rows_shift(tr_, 1, 0, rev))
    wi = jnp.where(first, 0.0, _rows_shift(ti_, 1, 0, rev))
    er, ei = _cmul_add(_rows_shift(lr, 1, 0, rev), _rows_shift(li, 1, 0, rev), wr, wi, cr, ci)
    a8r, a8i = (p3r[nt - 1], p3i[nt - 1]) if rev else (p3r[0], p3i[0])
    x3r, x3i = _cmul_add(x3r, x3i, a8r[None], a8i[None], er[:, None, :], ei[:, None, :])
    outr, outi = x3r.reshape(r, w), x3i.reshape(r, w)
    last = 0 if rev else r - 1
    return outr, outi, outr[last:last + 1, :], outi[last:last + 1, :]


def _s5_tile_specs(n_of, r):
    t = S5_TILE
    ucol = 4 * RET_HEADS * RET_DH // LANES
    u = pl.BlockSpec((r, LANES), lambda j, i: (n_of(i), ucol + j))
    col = pl.BlockSpec((r, LANES), lambda j, i: (n_of(i), j))
    state = pl.BlockSpec((r, 2 * t), lambda j, i: (n_of(i), j))
    table = pl.BlockSpec((r, 2 * t), lambda j, i: (0, j))
    bbt = pl.BlockSpec((None, LANES, 2 * t), lambda j, i: (j, 0, 0))
    cct = pl.BlockSpec((None, 2 * t, LANES), lambda j, i: (j, 0, 0))
    vec = pl.BlockSpec((1, LANES), lambda j, i: (0, j))
    return u, col, state, table, bbt, cct, vec


def _s5_fwd(proj, bbt, cct, apow, dvec, name):
    s = proj.shape[0]
    r, t = SCAN_ROWS, S5_TILE
    w = S5_GROUPS * S5_GROUP
    u_s, col, state, table, bb_s, cc_s, vec = _s5_tile_specs(lambda i: i, r)

    def body(u_ref, bb_ref, cc_ref, p_ref, d_ref, st_ref, y_ref, g_ref, cr, ci):
        @pl.when(pl.program_id(1) == 0)
        def _():
            cr[...] = jnp.zeros_like(cr)
            ci[...] = jnp.zeros_like(ci)

        u = u_ref[...]
        bu = _mxu(u, bb_ref[...])
        xr, xi, cr[...], ci[...] = _scan_block(bu[:, :t], bu[:, t:], p_ref[:, :t], p_ref[:, t:], cr[...], ci[...], False)
        st_ref[:, :t] = xr
        st_ref[:, t:] = xi
        y = _mxu(xr, cc_ref[:t, :]) + _mxu(xi, cc_ref[t:, :]) + d_ref[...] * u
        y_ref[...] = y
        g_ref[...] = _gelu(y).astype(BF16)

    return pl.pallas_call(
        body, name=name, grid=(2 * S5_GROUPS * S5_STATE // (2 * t), s // r),
        in_specs=[u_s, bb_s, cc_s, table, vec], out_specs=[state, col, col],
        out_shape=[jax.ShapeDtypeStruct((s, 2 * S5_GROUPS * S5_STATE), F32), jax.ShapeDtypeStruct((s, w), F32),
                   jax.ShapeDtypeStruct((s, w), BF16)],
        scratch_shapes=[pltpu.VMEM((1, t), F32), pltpu.VMEM((1, t), F32)],
        compiler_params=_cp("parallel", "arbitrary"),
    )(proj, bbt, cct, apow, dvec)


def _s5_bwd(dg1, dg2, y, proj, st, bbt, cct, apow_rev, dvec, dproj, name):
    s = proj.shape[0]
    r, t = SCAN_ROWS, S5_TILE
    nb = s // r
    w = S5_GROUPS * S5_GROUP
    u_s, col, state, table, bb_s, cc_s, vec = _s5_tile_specs(lambda i: nb - 1 - i, r)
    halo = pl.BlockSpec((8, 2 * t), lambda j, i: (jnp.maximum((nb - 1 - i) * (r // 8) - 1, 0), j))
    acc = pl.BlockSpec((1, 2 * t), lambda j, i: (0, j))

    def body(a_ref, b_ref, y_ref, u_ref, s_ref, sp_ref, bb_ref, cc_ref, p_ref, d_ref, _,
             du_ref, da_ref, dbb_ref, dcc_ref, dd_ref, cr, ci):
        i = pl.program_id(1)

        @pl.when(i == 0)
        def _():
            cr[...] = jnp.zeros_like(cr)
            ci[...] = jnp.zeros_like(ci)
            da_ref[...] = jnp.zeros_like(da_ref)
            dbb_ref[...] = jnp.zeros_like(dbb_ref)
            dcc_ref[...] = jnp.zeros_like(dcc_ref)
            dd_ref[...] = jnp.zeros_like(dd_ref)

        u = u_ref[...]
        dy = (a_ref[...] + b_ref[...]) * _gelu_grad(y_ref[...])
        dd_ref[...] += jnp.sum(dy * u, axis=0, keepdims=True)
        sr, si = s_ref[:, :t], s_ref[:, t:]
        dcc_ref[:t, :] += _mxu(sr, dy, 0, 0)
        dcc_ref[t:, :] += _mxu(si, dy, 0, 0)
        xr, xi, cr[...], ci[...] = _scan_block(_mxu(dy, cc_ref[:t, :], 1, 1), _mxu(dy, cc_ref[t:, :], 1, 1),
                                               p_ref[:, :t], p_ref[:, t:], cr[...], ci[...], True)
        du_ref[...] = (dy * d_ref[...] + _mxu(xr, bb_ref[:, :t], 1, 1) + _mxu(xi, bb_ref[:, t:], 1, 1)).astype(BF16)
        dbb_ref[:, :t] += _mxu(u, xr, 0, 0)
        dbb_ref[:, t:] += _mxu(u, xi, 0, 0)
        first = i == nb - 1
        row = lax.broadcasted_iota(jnp.int32, (r, t), 0)
        pr = jnp.where(row == 0, jnp.where(first, 0.0, sp_ref[7:8, :t]), pltpu.roll(sr, 1, 0))
        pi = jnp.where(row == 0, jnp.where(first, 0.0, sp_ref[7:8, t:]), pltpu.roll(si, 1, 0))
        da_ref[:, :t] += jnp.sum(xr * pr + xi * pi, axis=0, keepdims=True)
        da_ref[:, t:] += jnp.sum(xi * pr - xr * pi, axis=0, keepdims=True)

    return pl.pallas_call(
        body, name=name, grid=(2 * S5_GROUPS * S5_STATE // (2 * t), nb),
        in_specs=[col, col, col, u_s, state, halo, bb_s, cc_s, table, vec, pl.BlockSpec(memory_space=pl.ANY)],
        out_specs=[u_s, acc, bb_s, cc_s, vec],
        out_shape=[jax.ShapeDtypeStruct(dproj.shape, dproj.dtype), jax.ShapeDtypeStruct((1, 2 * S5_GROUPS * S5_STATE), F32),
                   jax.ShapeDtypeStruct(bbt.shape, F32), jax.ShapeDtypeStruct(cct.shape, F32),
                   jax.ShapeDtypeStruct((1, w), F32)],
        scratch_shapes=[pltpu.VMEM((1, t), F32), pltpu.VMEM((1, t), F32)],
        input_output_aliases={10: 0}, compiler_params=_cp("parallel", "arbitrary"),
    )(dg1, dg2, y, proj, st, st, bbt, cct, apow_rev, dvec, dproj)


def _s5_tile_b(b_re, b_im):
    nt = S5_GROUPS * S5_STATE // S5_TILE
    gpt = S5_GROUPS // nt
    eye = jnp.eye(gpt, dtype=F32)

    def tile(b):
        t5 = jnp.einsum("jghp,gk->jghkp", b.reshape(nt, gpt, S5_GROUP, S5_STATE), eye)
        return t5.reshape(nt, gpt * S5_GROUP, S5_TILE)

    return jnp.concatenate([tile(b_re), tile(b_im)], axis=2)


def _s5_untile_b(d):
    nt = S5_GROUPS * S5_STATE // S5_TILE
    gpt = S5_GROUPS // nt
    eye = jnp.eye(gpt, dtype=F32)

    def untile(x):
        x5 = x.reshape(nt, gpt, S5_GROUP, gpt, S5_STATE)
        return jnp.einsum("jghkp,gk->jghp", x5, eye).reshape(S5_GROUPS, S5_GROUP, S5_STATE)

    return untile(d[:, :, :S5_TILE]), untile(d[:, :, S5_TILE:])


def _s5_tile_c(c_re, c_im):
    nt = S5_GROUPS * S5_STATE // S5_TILE
    gpt = S5_GROUPS // nt
    eye = jnp.eye(gpt, dtype=F32)

    def tile(c):
        t5 = jnp.einsum("jgph,gk->jkpgh", c.reshape(nt, gpt, S5_STATE, S5_GROUP), eye)
        return t5.reshape(nt, S5_TILE, gpt * S5_GROUP)

    return jnp.concatenate([tile(c_re), -tile(c_im)], axis=1)


def _s5_untile_c(d):
    nt = S5_GROUPS * S5_STATE // S5_TILE
    gpt = S5_GROUPS // nt
    eye = jnp.eye(gpt, dtype=F32)

    def untile(x):
        x5 = x.reshape(nt, gpt, S5_STATE, gpt, S5_GROUP)
        return jnp.einsum("jkpgh,gk->jgph", x5, eye).reshape(S5_GROUPS, S5_STATE, S5_GROUP)

    return untile(d[:, :S5_TILE, :]), -untile(d[:, S5_TILE:, :])


def _row_call(body, name, s, ins, outs, acc=False):
    tr = min(512, s)

    def spec(width, cb, rows):
        if rows == 1:
            return pl.BlockSpec((1, width), lambda i: (0, cb))
        return pl.BlockSpec((tr, width), lambda i: (i, cb))

    in_specs = [spec(w, cb, a.shape[0]) for a, w, cb in ins]
    out_specs = [spec(w, cb, sd.shape[0]) for sd, w, cb in outs]
    return pl.pallas_call(
        body, name=name, grid=(s // tr,), in_specs=in_specs, out_specs=out_specs,
        out_shape=[sd for sd, _, _ in outs],
        compiler_params=_cp("arbitrary" if acc else "parallel"),
    )(*[a for a, _, _ in ins])


def _sds(shape, dtype):
    return jax.ShapeDtypeStruct(shape, dtype)


def _s5_glu_fwd(y, z, b, merged, name):
    s, w = y.shape
    tr = min(512, s)

    def body(y_ref, z_ref, b_ref, _, o_ref):
        o_ref[...] = (_gelu(y_ref[...]) * _sigmoid(z_ref[...] + b_ref[...])).astype(BF16)

    row = pl.BlockSpec((tr, w), lambda i: (i, 0))
    return pl.pallas_call(
        body, name=name, grid=(s // tr,),
        in_specs=[row, row, pl.BlockSpec((1, w), lambda i: (0, 0)), pl.BlockSpec(memory_space=pl.ANY)],
        out_specs=pl.BlockSpec((tr, w), lambda i: (i, 1)),
        out_shape=jax.ShapeDtypeStruct(merged.shape, merged.dtype),
        input_output_aliases={3: 0}, compiler_params=_cp("parallel"),
    )(y, z, b, merged)


def _s5_glu_bwd(dmerged, y, z, b, name):
    s, w = y.shape

    def body(do_ref, y_ref, z_ref, b_ref, dz_ref, dg_ref, db_ref):
        @pl.when(pl.program_id(0) == 0)
        def _():
            db_ref[...] = jnp.zeros_like(db_ref)

        g = _gelu(y_ref[...])
        sg = _sigmoid(z_ref[...] + b_ref[...])
        dout = do_ref[...]
        dz = dout * g * sg * (1.0 - sg)
        dz_ref[...] = dz.astype(BF16)
        dg_ref[...] = dout * sg
        db_ref[...] += jnp.sum(dz, axis=0, keepdims=True)

    return _row_call(body, name, s, [(dmerged, w, 1), (y, w, 0), (z, w, 0), (b, w, 0)],
                     [(_sds((s, w), BF16), w, 0), (_sds((s, w), F32), w, 0), (_sds((1, w), F32), w, 0)], acc=True)


def _gdn_conv_fwd(projx, cw, name):
    s = projx.shape[0]
    nh = GDN_HEADS

    def body(x_ref, w_ref, o_ref):
        j = pl.program_id(0)
        cv = _conv_fwd(_conv_taps(x_ref[...], GDN_CONV), w_ref)
        y = cv * _sigmoid(cv)
        nrm = y * lax.rsqrt(jnp.sum(y * y, axis=-1, keepdims=True) + EPS)
        o_ref[...] = jnp.where(j < nh, nrm * (GDN_DH ** -0.5), jnp.where(j < 2 * nh, nrm, y))

    return pl.pallas_call(
        body, name=name, grid=(3 * nh,),
        in_specs=[pl.BlockSpec((s, GDN_DH), lambda j: (0, j)), pl.BlockSpec((GDN_CONV, GDN_DH), lambda j: (0, j))],
        out_specs=pl.BlockSpec((s, GDN_DH), lambda j: (0, j)),
        out_shape=jax.ShapeDtypeStruct((s, 3 * nh * GDN_DH), F32), compiler_params=_cp("parallel"),
    )(projx, cw)


def _gdn_conv_bwd(projx, cw, dqkv, dprojx, name):
    s = projx.shape[0]
    nh = GDN_HEADS

    def body(x_ref, w_ref, d_ref, _, dx_ref, dw_ref):
        j = pl.program_id(0)
        x = x_ref[...]
        taps = _conv_taps(x, GDN_CONV)
        cv = _conv_fwd(taps, w_ref)
        sg = _sigmoid(cv)
        y = cv * sg
        rinv = lax.rsqrt(jnp.sum(y * y, axis=-1, keepdims=True) + EPS)
        nrm = y * rinv
        dn = d_ref[...]
        dns = jnp.where(j < nh, dn * (GDN_DH ** -0.5), dn)
        dyn = rinv * (dns - nrm * jnp.sum(dns * nrm, axis=-1, keepdims=True))
        dy = jnp.where(j < 2 * nh, dyn, dn)
        dc = dy * (sg * (1.0 + cv * (1.0 - sg)))
        dx_ref[...] = _conv_bwd(taps, dc, w_ref, dw_ref).astype(BF16)

    col = pl.BlockSpec((s, GDN_DH), lambda j: (0, j))
    wcol = pl.BlockSpec((GDN_CONV, GDN_DH), lambda j: (0, j))
    return pl.pallas_call(
        body, name=name, grid=(3 * nh,), in_specs=[col, wcol, col, pl.BlockSpec(memory_space=pl.ANY)],
        out_specs=[col, wcol],
        out_shape=[jax.ShapeDtypeStruct(dprojx.shape, dprojx.dtype), jax.ShapeDtypeStruct((GDN_CONV, 3 * nh * GDN_DH), F32)],
        input_output_aliases={3: 0}, compiler_params=_cp("parallel"),
    )(projx, cw, dqkv, dprojx)


def _softplus(x):
    return jnp.maximum(x, 0.0) + jnp.log1p(jnp.exp(-jnp.abs(x)))


def _gdn_gates_fwd(projx, alog, dtb, name):
    s = projx.shape[0]
    w = GDN_HEADS * GDN_DH
    tr = min(512, s)

    def body(t_ref, al_ref, dt_ref, bo_ref, go_ref):
        t = t_ref[...]
        for h in range(GDN_HEADS):
            lo, hi = h * GDN_DH, (h + 1) * GDN_DH
            b = jnp.broadcast_to(t[:, h:h + 1], (tr, GDN_DH))
            a = jnp.broadcast_to(t[:, GDN_HEADS + h:GDN_HEADS + h + 1], (tr, GDN_DH))
            bo_ref[:, lo:hi] = _sigmoid(b)
            go_ref[:, lo:hi] = -jnp.exp(al_ref[:, lo:hi]) * _softplus(a + dt_ref[:, lo:hi])

    row = pl.BlockSpec((tr, w), lambda i: (i, 0))
    vec = pl.BlockSpec((1, w), lambda i: (0, 0))
    return pl.pallas_call(
        body, name=name, grid=(s // tr,),
        in_specs=[pl.BlockSpec((tr, LANES), lambda i: (i, 4 * w // LANES)), vec, vec], out_specs=[row, row],
        out_shape=[jax.ShapeDtypeStruct((s, w), F32)] * 2, compiler_params=_cp("parallel"),
    )(projx, alog, dtb)


def _gdn_gates_bwd(projx, alog, dtb, dbeta, dg, dprojx, name):
    s = projx.shape[0]
    w = GDN_HEADS * GDN_DH
    tr = min(512, s)
    gate_blk = 4 * w // LANES

    def body(t_ref, al_ref, dt_ref, dbe_ref, dg_ref, _, o_ref, dal_ref, ddt_ref):
        @pl.when(pl.program_id(0) == 0)
        def _():
            dal_ref[...] = jnp.zeros_like(dal_ref)
            ddt_ref[...] = jnp.zeros_like(ddt_ref)

        t = t_ref[...]
        lane = lax.broadcasted_iota(jnp.int32, (tr, LANES), 1)
        lane1 = lax.broadcasted_iota(jnp.int32, (1, LANES), 1)
        out = jnp.zeros((tr, LANES), F32)
        dal = jnp.zeros((1, LANES), F32)
        ddt = jnp.zeros((1, LANES), F32)
        for h in range(GDN_HEADS):
            lo, hi = h * GDN_DH, (h + 1) * GDN_DH
            beta = _sigmoid(t[:, h:h + 1])
            pb = jnp.sum(dbe_ref[:, lo:hi], axis=-1, keepdims=True)
            db = pb * beta * (1.0 - beta)
            xa = t[:, GDN_HEADS + h:GDN_HEADS + h + 1] + dt_ref[:, lo:lo + 1]
            ea = -jnp.exp(al_ref[:, lo:lo + 1])
            pg = jnp.sum(dg_ref[:, lo:hi], axis=-1, keepdims=True)
            da = pg * ea * _sigmoid(xa)
            out = jnp.where(lane == h, db, jnp.where(lane == GDN_HEADS + h, da, out))
            dal = jnp.where(lane1 == h, jnp.sum(pg * ea * _softplus(xa), axis=0, keepdims=True), dal)
            ddt = jnp.where(lane1 == h, jnp.sum(da, axis=0, keepdims=True), ddt)
        o_ref[...] = out.astype(BF16)
        dal_ref[...] += dal
        ddt_ref[...] += ddt

    row = pl.BlockSpec((tr, w), lambda i: (i, 0))
    vec = pl.BlockSpec((1, w), lambda i: (0, 0))
    small = pl.BlockSpec((1, LANES), lambda i: (0, 0))
    gates = pl.BlockSpec((tr, LANES), lambda i: (i, gate_blk))
    return pl.pallas_call(
        body, name=name, grid=(s // tr,),
        in_specs=[gates, vec, vec, row, row, pl.BlockSpec(memory_space=pl.ANY)],
        out_specs=[gates, small, small],
        out_shape=[jax.ShapeDtypeStruct(dprojx.shape, dprojx.dtype), jax.ShapeDtypeStruct((1, LANES), F32),
                   jax.ShapeDtypeStruct((1, LANES), F32)],
        input_output_aliases={5: 0}, compiler_params=_cp("arbitrary"),
    )(projx, alog, dtb, dbeta, dg, dprojx)


def _gdn_tri():
    c = GDN_CHUNK
    i = lax.broadcasted_iota(jnp.int32, (c, c), 0)
    j = lax.broadcasted_iota(jnp.int32, (c, c), 1)
    return ((i >= j).astype(F32), (i <= j).astype(F32), i >= j, i > j, (i == j).astype(F32))


def _bdot(a, b, ca=2, cb=1, precision=None):
    return lax.dot_general(a, b, (((ca,), (cb,)), ((0,), (0,))), precision=precision, preferred_element_type=F32)


def _bmxu(a, b, ca=2, cb=1):
    return _bdot(a.astype(BF16), b.astype(BF16), ca, cb)


def _split(x):
    hi = x.astype(BF16)
    return hi, (x - hi.astype(F32)).astype(BF16)


def _bdot3(a, b, ca=2, cb=1):
    ah, al = _split(a)
    bh, bl = _split(b)
    return _bdot(ah, bh, ca, cb) + (_bdot(ah, bl, ca, cb) + _bdot(al, bh, ca, cb))


def _tri_dot(tri, x):
    t = tri.astype(BF16)
    hi = x.astype(BF16)
    r1 = x - hi.astype(F32)
    mid = r1.astype(BF16)
    lo = (r1 - mid.astype(F32)).astype(BF16)
    return _dot(t, hi) + (_dot(t, mid) + _dot(t, lo))


def _heads(x):
    return jnp.stack([x[:, h * GDN_DH:(h + 1) * GDN_DH] for h in range(GDN_HEADS)], axis=0)


def _unheads(x):
    return jnp.concatenate([x[h] for h in range(GDN_HEADS)], axis=1)


def _gdn_chunk(q, k, v, bb, g2d, tri, t=None):
    low, up, incl, strict, eye = tri
    c = GDN_CHUNK
    gc = _heads(_tri_dot(low, g2d))
    gci = gc[:, :, :c]
    gdiff = gci - jnp.swapaxes(gci, 1, 2)
    decay = jnp.where(incl, jnp.exp(jnp.where(incl, gdiff, 0.0)), 0.0)
    kb, vb = k * bb, v * bb
    kbk = _bmxu(kb, k, 2, 2)
    if t is None:
        x = -jnp.where(strict, kbk * decay, 0.0)
        t = eye + x
        p = x
        for _ in range(c.bit_length() - 2):
            p = _bdot3(p, p)
            t = t + _bdot3(t, p)
    eg = jnp.exp(gc)
    kbg = kb * eg
    gcl = gc[:, c - 1:c, :]
    ek = jnp.exp(gcl - gc)
    qkraw = _bmxu(q, k, 2, 2)
    return dict(decay=decay, kb=kb, vb=vb, kbk=kbk, t=t, eg=eg, kbg=kbg, ek=ek, gl=jnp.exp(gcl),
                w=_bmxu(t, kbg), u=_bmxu(t, vb), qkraw=qkraw, qk=jnp.where(incl, qkraw * decay, 0.0),
                qd=q * eg, kd=k * ek)


def _gdn_specs(n_of):
    c, w = GDN_CHUNK, GDN_HEADS * GDN_DH

    def blk(cb, width=w):
        return pl.BlockSpec((c, width), lambda n: (n_of(n), cb))

    st = pl.BlockSpec((None, GDN_HEADS, GDN_DH, GDN_DH), lambda n: (n_of(n), 0, 0, 0))
    vec = pl.BlockSpec((1, GDN_DH), lambda n: (0, 0))
    return blk, st, vec


def _gdn_load(qkv_ref, b_ref, g_ref, tri, t=None):
    w = GDN_HEADS * GDN_DH
    q, k, v = _heads(qkv_ref[:, :w]), _heads(qkv_ref[:, w:2 * w]), _heads(qkv_ref[:, 2 * w:])
    bb = _heads(b_ref[...])
    return q, k, v, bb, _gdn_chunk(q, k, v, bb, g_ref[...], tri, t)


def _gdn_fwd(qkv, beta, g, projx, onorm, name):
    s = qkv.shape[0]
    c = GDN_CHUNK
    nc = s // c
    w = GDN_HEADS * GDN_DH
    blk, st, vec = _gdn_specs(lambda n: n)
    inv = pl.BlockSpec((None, GDN_HEADS, c, c), lambda n: (n, 0, 0, 0))

    def body(qkv_ref, b_ref, g_ref, z_ref, on_ref, o_ref, st_ref, t_ref, state):
        @pl.when(pl.program_id(0) == 0)
        def _():
            state[...] = jnp.zeros_like(state)

        _, _, _, _, ch = _gdn_load(qkv_ref, b_ref, g_ref, _gdn_tri())
        t_ref[...] = ch["t"]
        sp = state[...]
        st_ref[...] = sp
        vn = ch["u"] - _bmxu(ch["w"], sp)
        o = _bmxu(ch["qd"], sp) + _bmxu(ch["qk"], vn)
        state[...] = sp * ch["gl"] + _bmxu(ch["kd"], vn, 1, 1)
        r = lax.rsqrt(jnp.mean(o * o, axis=-1, keepdims=True) + EPS)
        z = _heads(z_ref[...])
        o_ref[...] = _unheads(o * r * on_ref[...] * (z * _sigmoid(z))).astype(BF16)

    return pl.pallas_call(
        body, name=name, grid=(nc,),
        in_specs=[blk(0, 3 * w), blk(0), blk(0), blk(3), vec], out_specs=[blk(0), st, inv],
        out_shape=[jax.ShapeDtypeStruct((s, w), BF16), jax.ShapeDtypeStruct((nc, GDN_HEADS, GDN_DH, GDN_DH), F32),
                   jax.ShapeDtypeStruct((nc, GDN_HEADS, c, c), F32)],
        scratch_shapes=[pltpu.VMEM((GDN_HEADS, GDN_DH, GDN_DH), F32)],
        compiler_params=_cp("arbitrary"),
    )(qkv, beta, g, projx, onorm.reshape(1, -1))


def _gdn_bwd(qkv, beta, g, projx, onorm, states, tinv, dout, name):
    s = qkv.shape[0]
    c = GDN_CHUNK
    nc = s // c
    w = GDN_HEADS * GDN_DH
    blk, st, vec = _gdn_specs(lambda n: nc - 1 - n)
    inv = pl.BlockSpec((None, GDN_HEADS, c, c), lambda n: (nc - 1 - n, 0, 0, 0))

    def body(qkv_ref, b_ref, g_ref, z_ref, on_ref, st_ref, t_ref, do_ref,
             dqkv_ref, db_ref, dg_ref, dz_ref, don_ref, carry):
        @pl.when(pl.program_id(0) == 0)
        def _():
            carry[...] = jnp.zeros_like(carry)
            don_ref[...] = jnp.zeros_like(don_ref)

        tri = _gdn_tri()
        low, up, incl, strict, eye = tri
        q, k, v, bb, ch = _gdn_load(qkv_ref, b_ref, g_ref, tri, t_ref[...])
        sp = st_ref[...]
        vn = ch["u"] - _bmxu(ch["w"], sp)
        o = _bmxu(ch["qd"], sp) + _bmxu(ch["qk"], vn)
        r = lax.rsqrt(jnp.mean(o * o, axis=-1, keepdims=True) + EPS)
        orn = o * r
        z = _heads(z_ref[...])
        sg = _sigmoid(z)
        dout = _heads(do_ref[...])
        onw = on_ref[...]
        dz_ref[...] = _unheads(dout * orn * onw * (sg * (1.0 + z * (1.0 - sg)))).astype(BF16)
        don = dout * (z * sg)
        don_ref[...] += jnp.sum(jnp.sum(don * orn, axis=0), axis=0, keepdims=True)
        dor = don * onw
        do = r * (dor - orn * jnp.mean(dor * orn, axis=-1, keepdims=True))
        dsn = carry[...]
        dqd = _bmxu(do, sp, 2, 2)
        dqk = jnp.where(incl, _bmxu(do, vn, 2, 2), 0.0)
        dvn = _bmxu(ch["qk"], do, 1, 1) + _bmxu(ch["kd"], dsn)
        dkd = _bmxu(vn, dsn, 2, 2)
        dgl = jnp.sum(dsn * sp, axis=1, keepdims=True)
        dw = -_bmxu(dvn, sp, 2, 2)
        carry[...] = _bmxu(ch["qd"], do, 1, 1) + dsn * ch["gl"] - _bmxu(ch["w"], dvn, 1, 1)
        t = ch["t"]
        dvb = _bmxu(t, dvn, 1, 1)
        dkbg = _bmxu(t, dw, 1, 1)
        dt = _bmxu(dvn, ch["vb"], 2, 2) + _bmxu(dw, ch["kbg"], 2, 2)
        da = -_bdot3(_bdot3(t, dt, 1, 1), t, 2, 2)
        da = jnp.where(strict, da, 0.0)
        decay = ch["decay"]
        dkbk = da * decay
        dqkr = dqk * decay
        mdec = (da * ch["kbk"] + dqk * ch["qkraw"]) * decay
        dkb = _bmxu(dkbk, k) + dkbg * ch["eg"]
        dk = _bmxu(dkbk, ch["kb"], 1, 1) + _bmxu(dqkr, q, 1, 1) + dkd * ch["ek"] + dkb * bb
        dq = _bmxu(dqkr, k) + dqd * ch["eg"]
        tk = dkd * ch["kd"]
        dgcl = jnp.sum(tk, axis=1, keepdims=True) + dgl * ch["gl"]
        row = lax.broadcasted_iota(jnp.int32, (GDN_HEADS, c, GDN_DH), 1)
        zpad = jnp.zeros((GDN_HEADS, c, GDN_DH - c), F32)
        dgc = (jnp.concatenate([mdec, zpad], axis=2) - jnp.concatenate([jnp.swapaxes(mdec, 1, 2), zpad], axis=2)
               + dqd * ch["qd"] - tk + dkbg * ch["kbg"] + jnp.where(row == c - 1, dgcl, 0.0))
        dqkv_ref[:, :w] = _unheads(dq)
        dqkv_ref[:, w:2 * w] = _unheads(dk)
        dqkv_ref[:, 2 * w:] = _unheads(dvb * bb)
        db_ref[...] = _unheads(dkb * k + dvb * v)
        dg_ref[...] = _tri_dot(up, _unheads(dgc))

    return pl.pallas_call(
        body, name=name, grid=(nc,),
        in_specs=[blk(0, 3 * w), blk(0), blk(0), blk(3), vec, st, inv, blk(0)],
        out_specs=[blk(0, 3 * w), blk(0), blk(0), blk(3), vec],
        out_shape=[jax.ShapeDtypeStruct((s, 3 * w), F32), jax.ShapeDtypeStruct((s, w), F32),
                   jax.ShapeDtypeStruct((s, w), F32), jax.ShapeDtypeStruct(projx.shape, BF16),
                   jax.ShapeDtypeStruct((1, GDN_DH), F32)],
        scratch_shapes=[pltpu.VMEM((GDN_HEADS, GDN_DH, GDN_DH), F32)],
        compiler_params=_cp("arbitrary"),
    )(qkv, beta, g, projx, onorm.reshape(1, -1), states, tinv, dout)


_WEIGHTS = (
    "l0_mix_norm", "l0_w_in", "l0_ret_norm", "l0_s5_lambda_re", "l0_s5_lambda_im", "l0_s5_b_re", "l0_s5_b_im",
    "l0_s5_c_re", "l0_s5_c_im", "l0_s5_d", "l0_s5_log_dt", "l0_s5_w_glu", "l0_s5_b_glu", "l0_w_out",
    "l0_xa_norm", "l0_mem_norm", "l0_xa_wq", "l0_xa_wkv", "l0_xa_wo", "l0_ffn_norm", "l0_ffn_w_up",
    "l0_ffn_conv", "l0_ffn_w_down", "l1_mix_norm", "l1_w_in", "l1_conv", "l1_a_log", "l1_dt_bias", "l1_o_norm",
    "l1_w_out", "l1_xa_norm", "l1_mem_norm", "l1_xa_wq", "l1_xa_wkv", "l1_xa_wo", "l1_ffn_norm", "l1_ffn_w_up",
    "l1_ffn_conv", "l1_ffn_w_down", "final_norm")
_INPUTS = ("x", "mem") + _WEIGHTS + ("loss_target",) + tuple("m_" + n for n in _WEIGHTS) + tuple("v_" + n for n in _WEIGHTS)

_COL = ("l0_w_in", "l0_xa_wkv", "l0_ffn_w_up", "l0_ffn_conv", "l1_w_in", "l1_conv", "l1_xa_wkv", "l1_ffn_w_up",
        "l1_ffn_conv")
_ROW = ("l0_s5_w_glu", "l0_w_out", "l0_xa_wq", "l0_xa_wo", "l0_ffn_w_down", "l1_w_out", "l1_xa_wq", "l1_xa_wo",
        "l1_ffn_w_down")
_F32_WIRE = ("l0_ffn_conv", "l1_conv", "l1_ffn_conv")
_REP = tuple(n for n in _WEIGHTS if n not in _COL + _ROW)
_GATHER_GROUPS = (("l0_w_in", "l0_s5_w_glu", "l0_w_out"),
                  ("l0_xa_wq", "l0_xa_wkv", "l0_xa_wo", "l0_ffn_w_up", "l0_ffn_conv", "l0_ffn_w_down"),
                  ("l1_w_in", "l1_conv", "l1_w_out", "l1_xa_wq", "l1_xa_wkv", "l1_xa_wo"),
                  ("l1_ffn_w_up", "l1_ffn_conv", "l1_ffn_w_down"))


def _round_up(n, m):
    return (n + m - 1) // m * m


_REP_BIG = ("l0_s5_lambda_re", "l0_s5_lambda_im", "l0_s5_b_re", "l0_s5_b_im", "l0_s5_c_re", "l0_s5_c_im", "l0_s5_d")
_REP_LAST = "l0_mix_norm"
_REP_SMALL = tuple(n for n in _REP if n not in _REP_BIG + (_REP_LAST,))
PACK_WIDTH = 1024


def _pack_rows(ts):
    rows = [jnp.pad(t, ((0, 0), (0, PACK_WIDTH - t.shape[1]))) for t in ts]
    rows.append(jnp.zeros((_round_up(len(ts), 8) - len(ts), PACK_WIDTH), F32))
    return jnp.concatenate(rows, axis=0)


def _s5_interleave(re, im):
    lead = re.shape[:-1]
    nt = re.shape[-1] // S5_TILE
    both = jnp.stack([re.reshape(lead + (nt, S5_TILE)), im.reshape(lead + (nt, S5_TILE))], axis=-2)
    return both.reshape(lead + (2 * re.shape[-1],))


def _s5_split(x):
    lead = x.shape[:-1]
    y = x.reshape(lead + (x.shape[-1] // (2 * S5_TILE), 2, S5_TILE))
    return y[..., 0, :].reshape(lead + (-1,)), y[..., 1, :].reshape(lead + (-1,))


def _s5_discretise(lr, li, log_dt, b_re, b_im):
    dt = jnp.exp(log_dt)[:, None]
    mag = jnp.exp(lr * dt)
    a_re = mag * jnp.cos(li * dt)
    a_im = mag * jnp.sin(li * dt)
    den = lr * lr + li * li
    z_re = ((a_re - 1.0) * lr + a_im * li) / den
    z_im = (a_im * lr - (a_re - 1.0) * li) / den
    bb_re = z_re[:, None, :] * b_re - z_im[:, None, :] * b_im
    bb_im = z_re[:, None, :] * b_im + z_im[:, None, :] * b_re
    return a_re, a_im, bb_re, bb_im


def kernel(*args):
    p = dict(zip(_INPUTS, args, strict=True))
    x0, mem0, tgt = p["x"][0], p["mem"][0], p["loss_target"][0]
    s, d = x0.shape
    me = _slot(*_mesh_pos())
    grads = {}
    wire = {n: (F32 if n in _F32_WIRE else BF16) for n in _COL + _ROW}

    zones = {n: _into_slot(p[n], wire[n], me, "place_" + n) for names in _GATHER_GROUPS for n in names}
    gather, pin = [], jnp.zeros((), F32)
    for i, names in enumerate(_GATHER_GROUPS):
        handle, token = _push_start([], [zones[n] for n in names], f"gather{i}_start")
        gather.append(handle)
        pin = pin + token[0, 0]
    w = {}

    def gathered(i, after):
        for n, full in zip(_GATHER_GROUPS[i], _push_wait(gather[i], after, f"gather{i}_wait")):
            if n in _COL:
                full = full.transpose(1, 0, 2)
            w[n] = full.reshape(-1, full.shape[-1]) if n in _ROW else full.reshape(full.shape[0], -1)

    pending = []

    def exchange(names, gain, tag):
        slots = []
        for n in names:
            g = grads[n]
            if n in _COL:
                pieces = g if isinstance(g, tuple) else (g,)
                g = jnp.concatenate([t.reshape(t.shape[0], -1, p[n].shape[1]).transpose(1, 0, 2) for t in pieces], axis=0)
            else:
                g = g.reshape((N_DEV, -1) + g.shape[1:])
            slots.append(g.astype(wire[n]))
        handle, token = _push_start(slots, [], tag + "_start")
        pending.append((names, slots, handle, tag))
        return gain + token[0, 0]

    def xattn(pre, x_in, hx):
        q = _mm(hx, w[pre + "xa_wq"], out_dtype=BF16, name=pre + "xa_q")
        memn = _norm_fwd(mem0, p[pre + "mem_norm"], pre + "mem_norm_fwd")
        kv = _mm(memn, w[pre + "xa_wkv"], out_dtype=BF16, name=pre + "xa_kv")
        ao = _xattn_fwd(q, kv, pre + "xattn_fwd")
        x_out, hf = _mm(ao, w[pre + "xa_wo"], res=x_in, norm_gain=p[pre + "ffn_norm"], name=pre + "xa_o")
        return x_out, hf, (x_in, hx, q, memn, kv, ao)

    def xattn_bwd(pre, saved, dxo):
        x_in, hx, q, memn, kv, ao = saved
        dao = _mm(dxo, w[pre + "xa_wo"], tb=True, name=pre + "xa_o_dx")
        grads[pre + "xa_wo"] = _mm(ao, dxo, ta=True, out_dtype=BF16, name=pre + "xa_o_dw")
        dq, dkv = _xattn_bwd(q, kv, dao, pre + "xattn_bwd")
        grads[pre + "xa_wq"] = _mm(hx, dq, ta=True, out_dtype=BF16, name=pre + "xa_q_dw")
        grads[pre + "xa_wkv"] = _mm(memn, dkv, ta=True, out_dtype=BF16, name=pre + "xa_kv_dw")
        dmemn = _mm(dkv, w[pre + "xa_wkv"], tb=True, name=pre + "xa_kv_dx")
        gain = exchange((pre + "xa_wo", pre + "xa_wq", pre + "xa_wkv"), p[pre + "xa_norm"], pre + "xa_grads")
        dx_in, grads[pre + "xa_norm"] = _mm_norm_bwd(dq, w[pre + "xa_wq"], x_in, gain, dxo, pre + "xa_q_dx")
        _, grads[pre + "mem_norm"] = _norm_bwd(mem0, p[pre + "mem_norm"], dmemn, jnp.zeros_like(mem0), pre + "mem_norm_bwd")
        return dx_in

    def ffn(pre, x_in, hf, next_gain):
        up = _mm(hf, w[pre + "ffn_w_up"], out_dtype=BF16, name=pre + "ffn_up")
        act = _ffn_act_fwd(up, w[pre + "ffn_conv"], pre + "ffn_act_fwd")
        res = _mm(act, w[pre + "ffn_w_down"], res=x_in, norm_gain=next_gain, name=pre + "ffn_down")
        x_out, h_next = res if next_gain is not None else (res, None)
        return x_out, h_next, (x_in, hf, up, act)

    def ffn_bwd(pre, saved, dxo):
        x_in, hf, up, act = saved
        dact = _mm(dxo, w[pre + "ffn_w_down"], tb=True, out_dtype=BF16, name=pre + "ffn_down_dx")
        grads[pre + "ffn_w_down"] = _mm(act, dxo, ta=True, out_dtype=BF16, name=pre + "ffn_down_dw")
        dpu, dpg, dcu, dcg = _ffn_act_bwd(up, w[pre + "ffn_conv"], dact, pre + "ffn_act_bwd")
        grads[pre + "ffn_conv"] = jnp.concatenate([dcu, dcg], axis=1)
        grads[pre + "ffn_w_up"] = (_mm(hf, dpu, ta=True, out_dtype=BF16, name=pre + "ffn_up_dw_u"),
                                   _mm(hf, dpg, ta=True, out_dtype=BF16, name=pre + "ffn_up_dw_g"))
        gain = exchange((pre + "ffn_w_down", pre + "ffn_w_up", pre + "ffn_conv"), p[pre + "ffn_norm"], pre + "ffn_grads")
        dx_in, grads[pre + "ffn_norm"] = _mm_norm_bwd([dpu, dpg], w[pre + "ffn_w_up"], x_in, gain, dxo, pre + "ffn_up_dx")
        return dx_in

    cos, sin = _rope_tables(s)
    (a_re, a_im, bb_re, bb_im), disc_vjp = jax.vjp(
        _s5_discretise, p["l0_s5_lambda_re"], p["l0_s5_lambda_im"], p["l0_s5_log_dt"], p["l0_s5_b_re"], p["l0_s5_b_im"])
    apow, apow_rev = _s5_pow_tables(_s5_interleave(a_re.reshape(1, -1), a_im.reshape(1, -1)), "l0_s5_pow_tables")
    bbt = _s5_tile_b(bb_re, bb_im).astype(BF16)
    cct = _s5_tile_c(p["l0_s5_c_re"], p["l0_s5_c_im"]).astype(BF16)
    s5_d = p["l0_s5_d"].reshape(1, -1)
    b_glu = p["l0_s5_b_glu"].reshape(1, -1)

    h0 = _norm_fwd(x0, p["l0_mix_norm"] + pin, "l0_mix_norm_fwd")
    gathered(0, h0)
    proj = _mm(h0, w["l0_w_in"], name="l0_in")
    merged, ret_states = _ret_fwd(proj, cos, sin, p["l0_ret_norm"], "l0_ret_fwd")
    st, y, gy = _s5_fwd(proj, bbt, cct, apow, s5_d, "l0_s5_fwd")
    z = _mm(gy, w["l0_s5_w_glu"], name="l0_s5_glu_mm")
    merged = _s5_glu_fwd(y, z, b_glu, merged, "l0_s5_glu_fwd")
    x1, hx0 = _mm(merged, w["l0_w_out"], res=x0, norm_gain=p["l0_xa_norm"], name="l0_out")
    gathered(1, x1)
    x2, hf0, xa0 = xattn("l0_", x1, hx0)
    x3, h1, ff0 = ffn("l0_", x2, hf0, p["l1_mix_norm"])

    gathered(2, x3)
    w1 = w["l1_w_in"]
    wx = jnp.pad(w1, ((0, 0), (0, _round_up(w1.shape[1], LANES) - w1.shape[1])))
    alog_x = jnp.repeat(p["l1_a_log"], GDN_DH).reshape(1, -1)
    dtb_x = jnp.repeat(p["l1_dt_bias"], GDN_DH).reshape(1, -1)
    projx = _mm(h1, wx, name="l1_in")
    qkv = _gdn_conv_fwd(projx, w["l1_conv"], "l1_conv_fwd")
    beta, glog = _gdn_gates_fwd(projx, alog_x, dtb_x, "l1_gates_fwd")
    o_gdn, gdn_states, gdn_tinv = _gdn_fwd(qkv, beta, glog, projx, p["l1_o_norm"], "l1_gdn_fwd")
    x4, hx1 = _mm(o_gdn, w["l1_w_out"], res=x3, norm_gain=p["l1_xa_norm"], name="l1_out")
    x5, hf1, xa1 = xattn("l1_", x4, hx1)
    gathered(3, x5)
    x6, _, ff1 = ffn("l1_", x5, hf1, None)

    loss_part, dx6, grads["final_norm"] = _loss_head(x6, p["final_norm"], tgt, "loss_head")
    loss = lax.psum(loss_part[0, 0], ("x", "y", "c"))
    dx5 = ffn_bwd("l1_", ff1, dx6)
    dx4 = xattn_bwd("l1_", xa1, dx5)

    do_gdn = _mm(dx4, w["l1_w_out"], tb=True, name="l1_out_dx")
    grads["l1_w_out"] = _mm(o_gdn, dx4, ta=True, out_dtype=BF16, name="l1_out_dw")
    dqkv, dbeta, dglog, dprojx, grads["l1_o_norm"] = _gdn_bwd(
        qkv, beta, glog, projx, p["l1_o_norm"], gdn_states, gdn_tinv, do_gdn, "l1_gdn_bwd")
    dprojx, grads["l1_conv"] = _gdn_conv_bwd(projx, w["l1_conv"], dqkv, dprojx, "l1_conv_bwd")
    dprojx, dalog_x, ddtb_x = _gdn_gates_bwd(projx, alog_x, dtb_x, dbeta, dglog, dprojx, "l1_gates_bwd")
    grads["l1_w_in"] = _mm(h1, dprojx, ta=True, out_dtype=BF16, name="l1_in_dw")[:, :w1.shape[1]]
    grads["l1_a_log"] = dalog_x[0, :GDN_HEADS]
    grads["l1_dt_bias"] = ddtb_x[0, :GDN_HEADS]
    gain = exchange(("l1_w_out", "l1_w_in", "l1_conv"), p["l1_mix_norm"], "l1_mix_grads")
    dx3, grads["l1_mix_norm"] = _mm_norm_bwd(dprojx, wx, x3, gain, dx4, "l1_in_dx")

    dx2 = ffn_bwd("l0_", ff0, dx3)
    dx1 = xattn_bwd("l0_", xa0, dx2)

    dmerged = _mm(dx1, w["l0_w_out"], tb=True, name="l0_out_dx")
    grads["l0_w_out"] = _mm(merged, dx1, ta=True, out_dtype=BF16, name="l0_out_dw")
    dproj, grads["l0_ret_norm"] = _ret_bwd(proj, cos, sin, p["l0_ret_norm"], ret_states, dmerged, "l0_ret_bwd")
    dzg, dg1, grads["l0_s5_b_glu"] = _s5_glu_bwd(dmerged, y, z, b_glu, "l0_s5_glu_bwd")
    grads["l0_s5_w_glu"] = _mm(gy, dzg, ta=True, out_dtype=BF16, name="l0_s5_glu_dw")
    s5_d_after = exchange(("l0_w_out", "l0_s5_w_glu"), s5_d, "l0_out_grads")
    dg2 = _mm(dzg, w["l0_s5_w_glu"], tb=True, name="l0_s5_glu_dx")
    dproj, da_s5, dbbt, dcct, grads["l0_s5_d"] = _s5_bwd(dg1, dg2, y, proj, st, bbt, cct, apow_rev, s5_d_after, dproj, "l0_s5_bwd")
    dbb_re, dbb_im = _s5_untile_b(dbbt)
    grads["l0_s5_c_re"], grads["l0_s5_c_im"] = _s5_untile_c(dcct)
    da_re, da_im = (t.reshape(S5_GROUPS, S5_STATE) for t in _s5_split(da_s5[0]))
    (grads["l0_s5_lambda_re"], grads["l0_s5_lambda_im"], grads["l0_s5_log_dt"], grads["l0_s5_b_re"],
     grads["l0_s5_b_im"]) = disc_vjp((da_re, da_im, dbb_re, dbb_im))

    def as_2d(t):
        return t.reshape(-1, t.shape[-1])

    def as_row(t):
        return t.reshape(1, -1)

    small_own = _pack_rows([as_row(grads[n]) for n in _REP_SMALL])
    big_own = [as_2d(grads[n].reshape(p[n].shape)) for n in _REP_BIG]
    rep_zones = [_into_slot(small_own, F32, me, "place_rep0")]
    rep_zones += [_into_slot(t.reshape(-1, LANES), BF16, me, f"place_rep{i + 1}") for i, t in enumerate(big_own)]
    rep_handle, rep_token = _push_start([], rep_zones, "rep_grads_start")

    grads["l0_w_in"] = _mm(h0, dproj, ta=True, out_dtype=BF16, pin=rep_token, name="l0_in_dw")
    gain = exchange(("l0_w_in",), p["l0_mix_norm"], "l0_mix_grads")
    dx0, grads["l0_mix_norm"] = _mm_norm_bwd(dproj, w["l0_w_in"], x0, gain, dx1, "l0_in_dx")

    last_own = _pack_rows([as_row(grads[_REP_LAST])])
    last_handle, _ = _push_start([], [_into_slot(last_own, F32, me, "place_rep_last")], "rep_last_start")
    last_land, = _push_wait(last_handle, dx0, "rep_last_wait")
    rep_lands = _push_wait(rep_handle, last_land, "rep_grads_wait")
    rep_land = rep_lands[0]

    outs = {}
    kinds = ("grad_", "delta_", "new_m_", "new_v_")
    for names, slots, handle, tag in pending:
        for n, own_slots, land in zip(names, slots, _push_wait(handle, rep_land, tag + "_wait")):
            shape = p[n].shape
            own = lax.dynamic_index_in_dim(own_slots, me, 0, keepdims=False)
            res = _adamw(land, own, *(p[pre + n].reshape(own.shape) for pre in ("", "m_", "v_")), "adamw_" + n)
            for kind, t in zip(kinds, res):
                outs[kind + n] = t.reshape(shape)
    for n, own, land in zip(_REP_BIG, big_own, rep_lands[1:]):
        res = _adamw(land.reshape((N_DEV,) + own.shape), None, *(as_2d(p[pre + n]) for pre in ("", "m_", "v_")), "adamw_" + n)
        for kind, t in zip(kinds, res):
            outs[kind + n] = t.reshape(p[n].shape)
    for names, land, own, nm in ((_REP_SMALL, rep_land, small_own, "adamw_small"), ((_REP_LAST,), last_land, last_own, "adamw_last")):
        res = _adamw_rows(land, own, *([as_row(p[pre + n]) for n in names] for pre in ("", "m_", "v_")), nm)
        for j, kind in enumerate(kinds):
            for i, n in enumerate(names):
                outs[kind + n] = res[j * len(names) + i].reshape(p[n].shape)

    return (loss, dx0[None]) + tuple(outs[kind + n] for kind in kinds for n in _WEIGHTS)
```

```python
import math

import numpy as np
import jax
import jax.numpy as jnp
from jax import lax
from jax.experimental import pallas as pl
from jax.experimental.pallas import tpu as pltpu

F32 = jnp.float32
BF16 = jnp.bfloat16
EPS = 1e-6
N_DEV = 8
LANES = 128
VMEM_LIMIT = 48 * 1024 * 1024

RET_HEADS, RET_DH, RET_CHUNK = 4, 128, 128
S5_GROUPS, S5_GROUP, S5_STATE = 32, 16, 64
GDN_HEADS, GDN_DH, GDN_CHUNK, GDN_CONV = 8, 128, 64, 4
XA_HEADS, XA_DH = 4, 256
FFN_CONV = 3
SCAN_ROWS = 256

ADAM_LR, ADAM_B1, ADAM_B2, ADAM_EPS, ADAM_WD, ADAM_STEP = 0.001, 0.9, 0.999, 1e-08, 0.01, 10


def _cp(*sem):
    return pltpu.CompilerParams(dimension_semantics=sem if sem else None, vmem_limit_bytes=VMEM_LIMIT)


def _tile(n, cap):
    if n <= cap:
        return n
    best = None
    for t in range(LANES, cap + 1, LANES):
        if n % t == 0:
            best = t
    assert best is not None, n
    return best


def _dot(a, b, ca=1, cb=0, precision=None):
    return lax.dot_general(a, b, (((ca,), (cb,)), ((), ())), precision=precision, preferred_element_type=F32)


def _mxu(a, b, ca=1, cb=0):
    return _dot(a.astype(BF16), b.astype(BF16), ca, cb)


def _sigmoid(x):
    return 0.5 * jnp.tanh(0.5 * x) + 0.5


def _shift_down(x, k):
    r = pltpu.roll(x, k, 0)
    row = lax.broadcasted_iota(jnp.int32, (8,) + x.shape[1:], 0)
    return jnp.concatenate([jnp.where(row >= k, r[:8], 0.0), r[8:]], axis=0)


def _shift_up(x, k):
    n = x.shape[0]
    r = pltpu.roll(x, n - k, 0)
    row = lax.broadcasted_iota(jnp.int32, (8,) + x.shape[1:], 0)
    return jnp.concatenate([r[:n - 8], jnp.where(row < 8 - k, r[n - 8:], 0.0)], axis=0)


def _mesh_pos():
    return lax.axis_index("x"), lax.axis_index("y"), lax.axis_index("c")


def _slot(px, py, pc):
    return 4 * px + 2 * py + pc


def _all_peers(x, y, c):
    flips = [(fx, fy, fc) for fx in (0, 1) for fy in (0, 1) for fc in (0, 1)][1:]
    return [(1 - x if fx else x, 1 - y if fy else y, 1 - c if fc else c) for fx, fy, fc in flips]


_HBM = pl.BlockSpec(memory_space=pltpu.HBM)
_SEM = pl.BlockSpec(memory_space=pltpu.SEMAPHORE)
N_PEERS = N_DEV - 1


def _push_copies(srcs, lands, send_sems, recv_sems, start):
    x, y, c = _mesh_pos()
    me = _slot(x, y, c)
    out = []
    for k, to in enumerate(_all_peers(x, y, c)):
        for a in range(len(lands)):
            src = srcs[a].at[_slot(*to)] if a < len(srcs) else lands[a].at[me]
            dst = lands[a].at[me if start else _slot(*to)]
            out.append(pltpu.make_async_remote_copy(
                src_ref=src, dst_ref=dst, send_sem=send_sems.at[a * N_PEERS + k], recv_sem=recv_sems.at[a * N_PEERS + k],
                device_id=to, device_id_type=pl.DeviceIdType.MESH))
    return out


def _into_slot(x, dtype, me, name, pin=None):
    r, c = x.shape
    cap = max(16, 512 * 1024 // c)
    tr = max(t for t in range(16, min(r, cap) + 1, 16) if r % t == 0) if r % 16 == 0 else r

    def body(me_ref, x_ref, *rest):
        rest[-1][...] = x_ref[...].astype(dtype)

    in_specs = [pl.BlockSpec((tr, c), lambda i, me_ref: (i, 0))]
    args = (x,)
    if pin is not None:
        in_specs.append(pl.BlockSpec(pin.shape, lambda i, me_ref: (0, 0)))
        args += (pin,)
    return pl.pallas_call(
        body, name=name, out_shape=jax.ShapeDtypeStruct((N_DEV, r, c), dtype),
        grid_spec=pltpu.PrefetchScalarGridSpec(
            num_scalar_prefetch=1, grid=(r // tr,), in_specs=in_specs,
            out_specs=pl.BlockSpec((None, tr, c), lambda i, me_ref: (me_ref[0], i, 0))),
        compiler_params=_cp("parallel"),
    )(me.reshape(1).astype(jnp.int32), *args)


def _push_start(scatter, gather_lands, name):
    ns, n = len(scatter), len(scatter) + len(gather_lands)
    lands = [lax.empty(a.shape, a.dtype) for a in scatter] + list(gather_lands)

    def body(*refs):
        srcs, zones = refs[:ns], refs[ns:ns + n]
        for cp in _push_copies(srcs, zones, refs[ns + n], refs[ns + n + 1], True):
            cp.start()
        refs[-1][...] = jnp.zeros((8, LANES), F32)

    hbm_in = [pltpu.with_memory_space_constraint(a, pltpu.HBM) for a in list(scatter) + lands]
    res = pl.pallas_call(
        body, name=name,
        out_shape=(pltpu.SemaphoreType.DMA((n * N_PEERS,)), pltpu.SemaphoreType.DMA((n * N_PEERS,)))
        + tuple(pltpu.HBM(a.shape, a.dtype) for a in list(scatter) + lands)
        + (jax.ShapeDtypeStruct((8, LANES), F32),),
        in_specs=[_HBM] * (ns + n),
        out_specs=(_SEM, _SEM) + (_HBM,) * (ns + n) + (pl.BlockSpec(memory_space=pltpu.VMEM),),
        input_output_aliases={i: 2 + i for i in range(ns + n)},
        compiler_params=pltpu.CompilerParams(has_side_effects=pltpu.SideEffectType.DATAFLOW_SIDE_EFFECTING),
    )(*hbm_in)
    return (res[0], res[1], res[2:2 + ns], res[2 + ns:2 + ns + n]), res[-1]


def _push_wait(handle, after, name):
    send_sems, recv_sems, srcs, lands = handle
    ns, n = len(srcs), len(lands)

    def body(*refs):
        for cp in _push_copies(refs[:ns], refs[ns:ns + n], refs[ns + n], refs[ns + n + 1], False):
            cp.wait_send()
            cp.wait_recv()

    res = pl.pallas_call(
        body, name=name,
        out_shape=tuple(pltpu.HBM(a.shape, a.dtype) for a in list(srcs) + list(lands)),
        in_specs=[_HBM] * (ns + n) + [_SEM, _SEM, pl.BlockSpec(memory_space=pl.ANY)],
        out_specs=(_HBM,) * (ns + n),
        input_output_aliases={i: i for i in range(ns + n)},
        compiler_params=pltpu.CompilerParams(has_side_effects=pltpu.SideEffectType.DATAFLOW_SIDE_EFFECTING),
    )(*srcs, *lands, send_sems, recv_sems, after)
    return res[ns:]


def _mm(a, b, *, ta=False, tb=False, out_dtype=F32, res=None, pin=None, norm_gain=None, name="mm"):
    m, k = (a.shape[1], a.shape[0]) if ta else a.shape
    n = b.shape[0] if tb else b.shape[1]
    assert k == (b.shape[1] if tb else b.shape[0]), (a.shape, b.shape, ta, tb)
    tm, tn, tk = _tile(m, 1408), _tile(n, 1536), _tile(k, 1408)
    nk = k // tk
    has_res = res is not None
    has_norm = norm_gain is not None
    assert not has_norm or tn == n
    n_in = 2 + has_res + (pin is not None) + has_norm

    def body(*refs):
        a_ref, b_ref = refs[:2]
        r_ref = refs[2] if has_res else None
        o_ref = refs[n_in]
        part = _mxu(a_ref[...], b_ref[...], 0 if ta else 1, 1 if tb else 0)

        def finish(r):
            if has_res:
                r = r + r_ref[...].astype(F32)
            o_ref[...] = r.astype(out_dtype)
            if has_norm:
                scale = lax.rsqrt(jnp.mean(r * r, axis=-1, keepdims=True) + EPS)
                refs[n_in + 1][...] = (r * scale * refs[n_in - 1][...]).astype(BF16)

        if nk == 1:
            finish(part)
            return
        acc = refs[-1]
        kk = pl.program_id(2)

        @pl.when(kk == 0)
        def _():
            acc[...] = part

        @pl.when(kk > 0)
        def _():
            acc[...] += part

        @pl.when(kk == nk - 1)
        def _():
            finish(acc[...])

    a_spec = pl.BlockSpec((tk, tm), lambda i, j, kk: (kk, i)) if ta else pl.BlockSpec((tm, tk), lambda i, j, kk: (i, kk))
    b_spec = pl.BlockSpec((tn, tk), lambda i, j, kk: (j, kk)) if tb else pl.BlockSpec((tk, tn), lambda i, j, kk: (kk, j))
    o_spec = pl.BlockSpec((tm, tn), lambda i, j, kk: (i, j))
    in_specs = [a_spec, b_spec] + ([o_spec] if has_res else [])
    args = (a, b) + ((res,) if has_res else ())
    if pin is not None:
        in_specs.append(pl.BlockSpec(pin.shape, lambda i, j, kk: (0, 0)))
        args += (pin,)
    if has_norm:
        in_specs.append(pl.BlockSpec((1, n), lambda i, j, kk: (0, 0)))
        args += (norm_gain.reshape(1, n),)
    out = jax.ShapeDtypeStruct((m, n), out_dtype)
    return pl.pallas_call(
        body, name=name, grid=(m // tm, n // tn, nk), in_specs=in_specs,
        out_specs=[o_spec, o_spec] if has_norm else o_spec,
        out_shape=[out, jax.ShapeDtypeStruct((m, n), BF16)] if has_norm else out,
        scratch_shapes=[pltpu.VMEM((tm, tn), F32)] if nk > 1 else [],
        compiler_params=_cp("parallel", "parallel", "arbitrary"),
    )(*args)


def _mm_norm_bwd(dy, w, x, g, dres, name, pin=None):
    dys = list(dy) if isinstance(dy, (list, tuple)) else [dy]
    nq = len(dys)
    s, kq = dys[0].shape
    d = w.shape[0]
    tm, tk = min(1024 if nq == 1 else 512, s), _tile(kq, 1408)
    per = kq // tk
    nk = nq * per
    n_in = nq + 4 + (pin is not None)

    def body(*refs):
        w_ref, x_ref, g_ref, dres_ref = refs[nq:nq + 4]
        dx_ref, dg_ref = refs[n_in], refs[n_in + 1]
        i, kk = pl.program_id(0), pl.program_id(1)

        @pl.when((i == 0) & (kk == 0))
        def _():
            dg_ref[...] = jnp.zeros_like(dg_ref)

        def finish(dh):
            xv = x_ref[...]
            r = lax.rsqrt(jnp.mean(xv * xv, axis=-1, keepdims=True) + EPS)
            xn = xv * r
            dg_ref[...] += jnp.sum(dh * xn, axis=0, keepdims=True)
            dhg = dh * g_ref[...]
            dx_ref[...] = dres_ref[...] + r * (dhg - xn * jnp.mean(dhg * xn, axis=-1, keepdims=True))

        if nk == 1:
            finish(_mxu(refs[0][...], w_ref[...], 1, 1))
            return
        acc = refs[-1]
        for q in range(nq):
            @pl.when((kk >= q * per) & (kk < (q + 1) * per))
            def _(q=q):
                part = _mxu(refs[q][...], w_ref[...], 1, 1)

                @pl.when(kk == 0)
                def _():
                    acc[...] = part

                @pl.when(kk > 0)
                def _():
                    acc[...] += part

        @pl.when(kk == nk - 1)
        def _():
            finish(acc[...])

    row = pl.BlockSpec((tm, d), lambda i, kk: (i, 0))
    vec = pl.BlockSpec((1, d), lambda i, kk: (0, 0))
    in_specs = [pl.BlockSpec((tm, tk), lambda i, kk, q=q: (i, jnp.clip(kk - q * per, 0, per - 1))) for q in range(nq)]
    in_specs += [pl.BlockSpec((d, tk), lambda i, kk: (0, kk)), row, vec, row]
    args = (*dys, w, x, g.reshape(1, d), dres)
    if pin is not None:
        in_specs.append(pl.BlockSpec(pin.shape, lambda i, kk: (0, 0)))
        args += (pin,)
    return pl.pallas_call(
        body, name=name, grid=(s // tm, nk), in_specs=in_specs, out_specs=[row, vec],
        out_shape=[jax.ShapeDtypeStruct((s, d), F32), jax.ShapeDtypeStruct((1, d), F32)],
        scratch_shapes=[pltpu.VMEM((tm, d), F32)] if nk > 1 else [],
        compiler_params=_cp("arbitrary", "arbitrary"),
    )(*args)


def _norm_fwd(x, g, name):
    s, d = x.shape
    tr = min(512, s)

    def body(x_ref, g_ref, o_ref):
        xv = x_ref[...]
        r = lax.rsqrt(jnp.mean(xv * xv, axis=-1, keepdims=True) + EPS)
        o_ref[...] = (xv * r * g_ref[...]).astype(BF16)

    row = pl.BlockSpec((tr, d), lambda i: (i, 0))
    return pl.pallas_call(
        body, name=name, grid=(s // tr,), in_specs=[row, pl.BlockSpec((1, d), lambda i: (0, 0))],
        out_specs=row, out_shape=jax.ShapeDtypeStruct((s, d), BF16), compiler_params=_cp("parallel"),
    )(x, g.reshape(1, d))


def _norm_bwd(x, g, dh, dres, name):
    s, d = x.shape
    tr = min(512, s)

    def body(x_ref, g_ref, dh_ref, dres_ref, dx_ref, dg_ref):
        @pl.when(pl.program_id(0) == 0)
        def _():
            dg_ref[...] = jnp.zeros_like(dg_ref)

        xv = x_ref[...]
        r = lax.rsqrt(jnp.mean(xv * xv, axis=-1, keepdims=True) + EPS)
        xn = xv * r
        dhv = dh_ref[...].astype(F32)
        dg_ref[...] += jnp.sum(dhv * xn, axis=0, keepdims=True)
        dhg = dhv * g_ref[...]
        dx_ref[...] = dres_ref[...] + r * (dhg - xn * jnp.mean(dhg * xn, axis=-1, keepdims=True))

    row = pl.BlockSpec((tr, d), lambda i: (i, 0))
    vec = pl.BlockSpec((1, d), lambda i: (0, 0))
    return pl.pallas_call(
        body, name=name, grid=(s // tr,), in_specs=[row, vec, row, row], out_specs=[row, vec],
        out_shape=[jax.ShapeDtypeStruct((s, d), F32), jax.ShapeDtypeStruct((1, d), F32)],
        compiler_params=_cp("arbitrary"),
    )(x, g.reshape(1, d), dh, dres)


def _loss_head(x, g, tgt, name):
    s, d = x.shape
    tr = min(512, s)

    def body(x_ref, g_ref, t_ref, l_ref, dx_ref, dg_ref):
        @pl.when(pl.program_id(0) == 0)
        def _():
            dg_ref[...] = jnp.zeros_like(dg_ref)
            l_ref[...] = jnp.zeros_like(l_ref)

        xv = x_ref[...]
        r = lax.rsqrt(jnp.mean(xv * xv, axis=-1, keepdims=True) + EPS)
        xn = xv * r
        err = xn * g_ref[...] - t_ref[...]
        part = 0.5 * jnp.sum(jnp.mean(err * err, axis=-1, keepdims=True), axis=0, keepdims=True)
        l_ref[...] += jnp.broadcast_to(part, l_ref.shape)
        dy = err * (1.0 / d)
        dg_ref[...] += jnp.sum(dy * xn, axis=0, keepdims=True)
        dyg = dy * g_ref[...]
        dx_ref[...] = r * (dyg - xn * jnp.mean(dyg * xn, axis=-1, keepdims=True))

    row = pl.BlockSpec((tr, d), lambda i: (i, 0))
    vec = pl.BlockSpec((1, d), lambda i: (0, 0))
    return pl.pallas_call(
        body, name=name, grid=(s // tr,), in_specs=[row, vec, row],
        out_specs=[pl.BlockSpec((1, LANES), lambda i: (0, 0)), row, vec],
        out_shape=[jax.ShapeDtypeStruct((1, LANES), F32), jax.ShapeDtypeStruct((s, d), F32),
                   jax.ShapeDtypeStruct((1, d), F32)],
        compiler_params=_cp("arbitrary"),
    )(x, g.reshape(1, d), tgt)


def _sum_slots(landed_slot, own):
    me = _slot(*_mesh_pos())
    mine = own.astype(F32)
    g = jnp.where(me == 0, mine, landed_slot(0).astype(F32))
    for i in range(1, N_DEV):
        g = g + jnp.where(me == i, mine, landed_slot(i).astype(F32))
    return g


def _adam_update(g, w, m, v):
    mm = ADAM_B1 * m + (1.0 - ADAM_B1) * g
    vv = ADAM_B2 * v + (1.0 - ADAM_B2) * (g * g)
    m_hat = mm / (1.0 - ADAM_B1 ** ADAM_STEP)
    v_hat = vv / (1.0 - ADAM_B2 ** ADAM_STEP)
    return g, -ADAM_LR * (m_hat / (jnp.sqrt(v_hat) + ADAM_EPS) + ADAM_WD * w), mm, vv


def _adamw_rows(landed, own, ws, ms, vs, name):
    k = len(ws)
    sizes = [w.shape[1] for w in ws]

    def body(*refs):
        p_ref, o_ref = refs[:2]
        w_refs, m_refs, v_refs = refs[2:2 + k], refs[2 + k:2 + 2 * k], refs[2 + 2 * k:2 + 3 * k]
        outs = refs[2 + 3 * k:]
        for i, n in enumerate(sizes):
            g = _sum_slots(lambda s: p_ref[s, i:i + 1, :n], o_ref[i:i + 1, :n])
            res = _adam_update(g, w_refs[i][...], m_refs[i][...], v_refs[i][...])
            for j in range(4):
                outs[j * k + i][...] = res[j]

    return pl.pallas_call(
        body, name=name, out_shape=[jax.ShapeDtypeStruct((1, n), F32) for _ in range(4) for n in sizes],
    )(landed, own, *ws, *ms, *vs)


def _adamw(landed, own, w, m, v, name):
    r, c = w.shape
    cap = max(8, 256 * 1024 // c)
    tr = max(t for t in range(8, min(r, cap) + 1, 8) if r % t == 0) if r % 8 == 0 else r
    gathered = own is None

    def body(*refs):
        p_ref = refs[0]
        w_ref, m_ref, v_ref, g_ref, d_ref, nm_ref, nv_ref = refs[1 if gathered else 2:]
        if gathered:
            g = p_ref[0].astype(F32)
            for i in range(1, N_DEV):
                g = g + p_ref[i].astype(F32)
        else:
            g = _sum_slots(lambda i: p_ref[i], refs[1][...])
        g_ref[...], d_ref[...], nm_ref[...], nv_ref[...] = _adam_update(g, w_ref[...], m_ref[...], v_ref[...])

    blk = pl.BlockSpec((tr, c), lambda i: (i, 0))
    n_blk = 3 if gathered else 4
    return pl.pallas_call(
        body, name=name, grid=(r // tr,),
        in_specs=[pl.BlockSpec((N_DEV, tr, c), lambda i: (0, i, 0))] + [blk] * n_blk,
        out_specs=[blk] * 4, out_shape=[jax.ShapeDtypeStruct((r, c), F32)] * 4,
        compiler_params=_cp("parallel"),
    )(*((landed,) if gathered else (landed, own)), w, m, v)


def _conv_taps(x, kw):
    return [_shift_down(x, kw - 1 - j) for j in range(kw - 1)] + [x]


def _conv_fwd(taps, w_ref):
    acc = w_ref[0:1, :] * taps[0]
    for j in range(1, len(taps)):
        acc = acc + w_ref[j:j + 1, :] * taps[j]
    return acc


def _conv_bwd(taps, dy, w_ref, dw_ref):
    kw = len(taps)
    dx = w_ref[kw - 1:kw, :] * dy
    for j in range(kw):
        dw_ref[j:j + 1, :] = jnp.sum(dy * taps[j], axis=0, keepdims=True)
        if j < kw - 1:
            dx = dx + w_ref[j:j + 1, :] * _shift_up(dy, kw - 1 - j)
    return dx


def _ffn_act_fwd(pre, cw, name):
    s, f2 = pre.shape
    nt = f2 // 2 // LANES

    def body(pu_ref, pg_ref, wu_ref, wg_ref, o_ref):
        up = _conv_fwd(_conv_taps(pu_ref[...].astype(F32), FFN_CONV), wu_ref)
        gate = _conv_fwd(_conv_taps(pg_ref[...].astype(F32), FFN_CONV), wg_ref)
        o_ref[...] = (gate * _sigmoid(gate) * up).astype(BF16)

    def col(rows, off):
        return pl.BlockSpec((rows, LANES), lambda j: (0, j + off))

    return pl.pallas_call(
        body, name=name, grid=(nt,),
        in_specs=[col(s, 0), col(s, nt), col(FFN_CONV, 0), col(FFN_CONV, nt)], out_specs=col(s, 0),
        out_shape=jax.ShapeDtypeStruct((s, f2 // 2), BF16), compiler_params=_cp("parallel"),
    )(pre, pre, cw, cw)


def _ffn_act_bwd(pre, cw, dact, name):
    s, f2 = pre.shape
    f = f2 // 2
    nt = f // LANES

    def body(pu_ref, pg_ref, wu_ref, wg_ref, da_ref, dpu_ref, dpg_ref, dwu_ref, dwg_ref):
        pu, pg = pu_ref[...].astype(F32), pg_ref[...].astype(F32)
        tu, tg = _conv_taps(pu, FFN_CONV), _conv_taps(pg, FFN_CONV)
        up = _conv_fwd(tu, wu_ref)
        gate = _conv_fwd(tg, wg_ref)
        sg = _sigmoid(gate)
        da = da_ref[...].astype(F32)
        dup = da * gate * sg
        dgate = da * up * (sg * (1.0 + gate * (1.0 - sg)))
        dpu_ref[...] = _conv_bwd(tu, dup, wu_ref, dwu_ref).astype(BF16)
        dpg_ref[...] = _conv_bwd(tg, dgate, wg_ref, dwg_ref).astype(BF16)

    def col(rows, off):
        return pl.BlockSpec((rows, LANES), lambda j: (0, j + off))

    return pl.pallas_call(
        body, name=name, grid=(nt,),
        in_specs=[col(s, 0), col(s, nt), col(FFN_CONV, 0), col(FFN_CONV, nt), col(s, 0)],
        out_specs=[col(s, 0), col(s, 0), col(FFN_CONV, 0), col(FFN_CONV, 0)],
        out_shape=[jax.ShapeDtypeStruct((s, f), BF16), jax.ShapeDtypeStruct((s, f), BF16),
                   jax.ShapeDtypeStruct((FFN_CONV, f), F32), jax.ShapeDtypeStruct((FFN_CONV, f), F32)],
        compiler_params=_cp("parallel"),
    )(pre, pre, cw, cw, dact)


def _xa_probs(qh, kh):
    sc = _mxu(qh, kh, 1, 1) * (XA_DH ** -0.5)
    e = jnp.exp(sc - jnp.max(sc, axis=-1, keepdims=True))
    return e / jnp.sum(e, axis=-1, keepdims=True)


def _xattn_fwd(q, kv, name):
    s, d = q.shape
    m = kv.shape[0]
    tr = min(512, s)

    def body(q_ref, kv_ref, o_ref):
        for h in range(XA_HEADS):
            lo, hi = h * XA_DH, (h + 1) * XA_DH
            p = _xa_probs(q_ref[:, lo:hi], kv_ref[:, lo:hi])
            o_ref[:, lo:hi] = _mxu(p, kv_ref[:, d + lo:d + hi]).astype(BF16)

    row = pl.BlockSpec((tr, d), lambda i: (i, 0))
    return pl.pallas_call(
        body, name=name, grid=(s // tr,), in_specs=[row, pl.BlockSpec((m, 2 * d), lambda i: (0, 0))],
        out_specs=row, out_shape=jax.ShapeDtypeStruct((s, d), BF16), compiler_params=_cp("parallel"),
    )(q, kv)


def _xattn_bwd(q, kv, do, name):
    s, d = q.shape
    m = kv.shape[0]
    tr = min(512, s)

    def body(q_ref, kv_ref, do_ref, dq_ref, dkv_ref):
        @pl.when(pl.program_id(0) == 0)
        def _():
            dkv_ref[...] = jnp.zeros_like(dkv_ref)

        for h in range(XA_HEADS):
            lo, hi = h * XA_DH, (h + 1) * XA_DH
            qh, kh, vh = q_ref[:, lo:hi], kv_ref[:, lo:hi], kv_ref[:, d + lo:d + hi]
            doh = do_ref[:, lo:hi]
            p = _xa_probs(qh, kh)
            dp = _mxu(doh, vh, 1, 1)
            ds = p * (dp - jnp.sum(p * dp, axis=-1, keepdims=True)) * (XA_DH ** -0.5)
            dq_ref[:, lo:hi] = _mxu(ds, kh).astype(BF16)
            dkv_ref[:, lo:hi] += _mxu(ds, qh, 0, 0)
            dkv_ref[:, d + lo:d + hi] += _mxu(p, doh, 0, 0)

    row = pl.BlockSpec((tr, d), lambda i: (i, 0))
    full = pl.BlockSpec((m, 2 * d), lambda i: (0, 0))
    return pl.pallas_call(
        body, name=name, grid=(s // tr,), in_specs=[row, full, row], out_specs=[row, full],
        out_shape=[jax.ShapeDtypeStruct((s, d), BF16), jax.ShapeDtypeStruct((m, 2 * d), F32)],
        compiler_params=_cp("arbitrary"),
    )(q, kv, do)


def _ret_tables():
    c = RET_CHUNK
    lg = np.log1p(-np.exp2(-5.0 - np.arange(RET_HEADS, dtype=np.float32))).astype(np.float32)
    idx = np.arange(c, dtype=np.float32)
    diff = idx[:, None] - idx[None, :]
    intra = np.where(diff >= 0, np.exp(lg[:, None, None] * np.where(diff >= 0, diff, 0.0)), 0.0)
    rk = np.broadcast_to(np.exp(lg[:, None] * (c - 1 - idx))[:, :, None], (RET_HEADS, c, LANES))
    rq = np.broadcast_to(np.exp(lg[:, None] * (idx + 1))[:, :, None], (RET_HEADS, c, LANES))
    return jnp.asarray(np.stack([intra, rk, rq], axis=1).astype(np.float32))


def _rope_tables(s):
    half = RET_DH // 2
    inv = jnp.exp(-math.log(10000.0) * jnp.arange(half, dtype=F32) / half)
    ang = jnp.arange(s, dtype=F32)[:, None] * inv[None, :]
    cos, sin = jnp.cos(ang), jnp.sin(ang)
    return jnp.concatenate([cos, cos], axis=1), jnp.concatenate([-sin, sin], axis=1)


def _ret_specs(n_of):
    c, w = RET_CHUNK, RET_HEADS * RET_DH

    def part(off):
        return pl.BlockSpec((c, w), lambda n: (n_of(n), off))

    pos = pl.BlockSpec((c, RET_DH), lambda n: (n_of(n), 0))
    gain = pl.BlockSpec((1, w), lambda n: (0, 0))
    tab = pl.BlockSpec((RET_HEADS, 3, c, LANES), lambda n: (0, 0, 0, 0))
    st = pl.BlockSpec((RET_HEADS, None, RET_DH, RET_DH), lambda n: (0, n_of(n), 0, 0))
    return part, pos, gain, tab, st


def _rheads(x):
    return jnp.stack([x[:, h * RET_DH:(h + 1) * RET_DH] for h in range(RET_HEADS)], axis=0)


def _runheads(x):
    return jnp.concatenate([x[h] for h in range(RET_HEADS)], axis=1)


def _rope(x, cos, sin):
    return x * cos + pltpu.roll(x, RET_DH // 2, 2) * sin


def _ret_chunk(q_ref, k_ref, v_ref, cos_ref, sin_ref, tab_ref, prev):
    cos, sin = cos_ref[...], sin_ref[...]
    q = _rope(_rheads(q_ref[...]), cos, sin)
    k = _rope(_rheads(k_ref[...]), cos, sin) * (RET_DH ** -0.5)
    v = _rheads(v_ref[...])
    scores = _bmxu(q, k, 2, 2) * tab_ref[:, 0]
    qdec = q * tab_ref[:, 2]
    kdec = k * tab_ref[:, 1]
    o = _bmxu(scores, v) + _bmxu(qdec, prev)
    return q, k, v, scores, qdec, kdec, o


def _ret_fwd(proj, cos, sin, gain, name):
    s = proj.shape[0]
    c = RET_CHUNK
    nc = s // c
    part, pos, gvec, tab, st = _ret_specs(lambda n: n)

    def body(q_ref, k_ref, v_ref, g_ref, cos_ref, sin_ref, rn_ref, tab_ref, o_ref, st_ref, state):
        @pl.when(pl.program_id(0) == 0)
        def _():
            state[...] = jnp.zeros_like(state)

        prev = state[...]
        st_ref[...] = prev
        _, _, v, _, _, kdec, o = _ret_chunk(q_ref, k_ref, v_ref, cos_ref, sin_ref, tab_ref, prev)
        state[...] = prev * tab_ref[:, 2, c - 1:c, :] + _bmxu(kdec, v, 1, 1)
        r = lax.rsqrt(jnp.mean(o * o, axis=-1, keepdims=True) + EPS)
        gate = g_ref[...]
        o_ref[...] = (_runheads(o * r) * rn_ref[...] * (gate * _sigmoid(gate))).astype(BF16)

    return pl.pallas_call(
        body, name=name, grid=(nc,),
        in_specs=[part(0), part(1), part(2), part(3), pos, pos, gvec, tab],
        out_specs=[part(0), st],
        out_shape=[jax.ShapeDtypeStruct((s, 2 * RET_HEADS * RET_DH), BF16),
                   jax.ShapeDtypeStruct((RET_HEADS, nc, RET_DH, RET_DH), F32)],
        scratch_shapes=[pltpu.VMEM((RET_HEADS, RET_DH, RET_DH), F32)],
        compiler_params=_cp("arbitrary"),
    )(proj, proj, proj, proj, cos, sin, gain.reshape(1, -1), _ret_tables())


def _ret_bwd(proj, cos, sin, gain, states, dmerged, name):
    s = proj.shape[0]
    c = RET_CHUNK
    nc = s // c
    width = RET_HEADS * RET_DH
    part, pos, gvec, tab, st = _ret_specs(lambda n: nc - 1 - n)

    def body(q_ref, k_ref, v_ref, g_ref, cos_ref, sin_ref, rn_ref, tab_ref, st_ref, do_ref,
             dp_ref, drn_ref, carry):
        @pl.when(pl.program_id(0) == 0)
        def _():
            carry[...] = jnp.zeros_like(carry)
            drn_ref[...] = jnp.zeros_like(drn_ref)

        prev = st_ref[...]
        q, k, v, scores, qdec, kdec, o = _ret_chunk(q_ref, k_ref, v_ref, cos_ref, sin_ref, tab_ref, prev)
        r = lax.rsqrt(jnp.mean(o * o, axis=-1, keepdims=True) + EPS)
        on = o * r
        on2 = _runheads(on)
        gate = g_ref[...]
        sg = _sigmoid(gate)
        sil = gate * sg
        dout = do_ref[...]
        rn = rn_ref[...]
        dp_ref[:, 3 * width:] = (dout * on2 * rn * (sg * (1.0 + gate * (1.0 - sg)))).astype(BF16)
        drn_ref[...] += jnp.sum(dout * on2 * sil, axis=0, keepdims=True)
        don = _rheads(dout * rn * sil)
        do = r * (don - on * jnp.mean(don * on, axis=-1, keepdims=True))
        dc = carry[...]
        dsc = _bmxu(do, v, 2, 2) * tab_ref[:, 0]
        dq = _bmxu(dsc, k) + _bmxu(do, prev, 2, 2) * tab_ref[:, 2]
        dk = _bmxu(dsc, q, 1, 1) + _bmxu(v, dc, 2, 2) * tab_ref[:, 1]
        dv = _bmxu(scores, do, 1, 1) + _bmxu(kdec, dc)
        carry[...] = _bmxu(qdec, do, 1, 1) + dc * tab_ref[:, 2, c - 1:c, :]
        cos, sin = cos_ref[...], sin_ref[...]
        dk = dk * (RET_DH ** -0.5)
        dp_ref[:, :width] = _runheads(dq * cos + pltpu.roll(dq * sin, RET_DH // 2, 2)).astype(BF16)
        dp_ref[:, width:2 * width] = _runheads(dk * cos + pltpu.roll(dk * sin, RET_DH // 2, 2)).astype(BF16)
        dp_ref[:, 2 * width:3 * width] = _runheads(dv).astype(BF16)

    return pl.pallas_call(
        body, name=name, grid=(nc,),
        in_specs=[part(0), part(1), part(2), part(3), pos, pos, gvec, tab, st, part(0)],
        out_specs=[pl.BlockSpec((c, 4 * width), lambda n: (nc - 1 - n, 0)), gvec],
        out_shape=[jax.ShapeDtypeStruct(proj.shape, BF16), jax.ShapeDtypeStruct((1, width), F32)],
        scratch_shapes=[pltpu.VMEM((RET_HEADS, RET_DH, RET_DH), F32)],
        compiler_params=_cp("arbitrary"),
    )(proj, proj, proj, proj, cos, sin, gain.reshape(1, -1), _ret_tables(), states, dmerged)


S5_TILE = 512


def _cmul_add(xr, xi, ar, ai, yr, yi):
    return xr + ar * yr - ai * yi, xi + ar * yi + ai * yr


def _s5_pow_tables(a_il, name):
    r = SCAN_ROWS
    t = S5_TILE
    w2 = a_il.shape[1]

    def body(a_ref, up_ref, dn_ref):
        for j in range(w2 // (2 * t)):
            re, im = pl.ds(2 * t * j, t), pl.ds(2 * t * j + t, t)
            up_ref[0:1, re] = a_ref[:, re]
            up_ref[0:1, im] = a_ref[:, im]
            dn_ref[r - 1:r, re] = a_ref[:, re]
            dn_ref[r - 1:r, im] = -a_ref[:, im]
            n = 1
            while n < r:
                lr, li = up_ref[n - 1:n, re], up_ref[n - 1:n, im]
                xr, xi = up_ref[0:n, re], up_ref[0:n, im]
                up_ref[n:2 * n, re] = xr * lr - xi * li
                up_ref[n:2 * n, im] = xr * li + xi * lr
                yr, yi = dn_ref[r - n:r, re], dn_ref[r - n:r, im]
                dn_ref[r - 2 * n:r - n, re] = yr * lr + yi * li
                dn_ref[r - 2 * n:r - n, im] = yi * lr - yr * li
                n *= 2

    return pl.pallas_call(
        body, name=name, out_shape=[jax.ShapeDtypeStruct((r, w2), F32)] * 2, compiler_params=_cp(),
    )(a_il)


_GELU_C = math.sqrt(2.0 / math.pi)
_GELU_A = 0.044715


def _gelu(y):
    return 0.5 * y * (1.0 + jnp.tanh(_GELU_C * (y + _GELU_A * y * y * y)))


def _gelu_grad(y):
    th = jnp.tanh(_GELU_C * (y + _GELU_A * y * y * y))
    return 0.5 * (1.0 + th) + 0.5 * y * (1.0 - th * th) * _GELU_C * (1.0 + 3.0 * _GELU_A * y * y)


def _rows_shift(x, k, axis, up):
    n = x.shape[axis]
    idx = lax.broadcasted_iota(jnp.int32, x.shape, axis)
    if up:
        return jnp.where(idx < n - k, pltpu.roll(x, n - k, axis), 0.0)
    return jnp.where(idx >= k, pltpu.roll(x, k, axis), 0.0)


def _scan_block(xr, xi, pr, pi, cr, ci, rev):
    r, w = xr.shape
    nt = r // 8
    x3r, x3i = xr.reshape(nt, 8, w), xi.reshape(nt, 8, w)
    p3r, p3i = pr.reshape(nt, 8, w), pi.reshape(nt, 8, w)

    def power(rows):
        t = r - rows if rev else rows - 1
        return pr[t:t + 1, :], pi[t:t + 1, :]

    tile_row = lax.broadcasted_iota(jnp.int32, (8, w), 0)
    for sh in (1, 2, 4):
        ar, ai = power(sh)
        keep = tile_row < 8 - sh if rev else tile_row >= sh
        mr, mi = jnp.where(keep, ar, 0.0)[None], jnp.where(keep, ai, 0.0)[None]
        turn = 8 - sh if rev else sh
        x3r, x3i = _cmul_add(x3r, x3i, mr, mi, pltpu.roll(x3r, turn, 1), pltpu.roll(x3i, turn, 1))
    edge = 0 if rev else 7
    lr, li = x3r[:, edge, :], x3i[:, edge, :]
    sh = 1
    while sh < nt:
        ar, ai = power(8 * sh)
        lr, li = _cmul_add(lr, li, ar, ai, _rows_shift(lr, sh, 0, rev), _rows_shift(li, sh, 0, rev))
        sh *= 2
    tr_, ti_ = p3r[:, edge, :], p3i[:, edge, :]
    first = lax.broadcasted_iota(jnp.int32, (nt, w), 0) == (nt - 1 if rev else 0)
    wr = jnp.where(first, 1.0, _rows_shift(tr_, 1, 0, rev))
    wi = jnp.where(first, 0.0, _rows_shift(ti_, 1, 0, rev))
    er, ei = _cmul_add(_rows_shift(lr, 1, 0, rev), _rows_shift(li, 1, 0, rev), wr, wi, cr, ci)
    a8r, a8i = (p3r[nt - 1], p3i[nt - 1]) if rev else (p3r[0], p3i[0])
    x3r, x3i = _cmul_add(x3r, x3i, a8r[None], a8i[None], er[:, None, :], ei[:, None, :])
    outr, outi = x3r.reshape(r, w), x3i.reshape(r, w)
    last = 0 if rev else r - 1
    return outr, outi, outr[last:last + 1, :], outi[last:last + 1, :]


def _s5_tile_specs(n_of, r):
    t = S5_TILE
    ucol = 4 * RET_HEADS * RET_DH // LANES
    u = pl.BlockSpec((r, LANES), lambda j, i: (n_of(i), ucol + j))
    col = pl.BlockSpec((r, LANES), lambda j, i: (n_of(i), j))
    state = pl.BlockSpec((r, 2 * t), lambda j, i: (n_of(i), j))
    table = pl.BlockSpec((r, 2 * t), lambda j, i: (0, j))
    bbt = pl.BlockSpec((None, LANES, 2 * t), lambda j, i: (j, 0, 0))
    cct = pl.BlockSpec((None, 2 * t, LANES), lambda j, i: (j, 0, 0))
    vec = pl.BlockSpec((1, LANES), lambda j, i: (0, j))
    return u, col, state, table, bbt, cct, vec


def _s5_fwd(proj, bbt, cct, apow, dvec, name):
    s = proj.shape[0]
    r, t = SCAN_ROWS, S5_TILE
    w = S5_GROUPS * S5_GROUP
    u_s, col, state, table, bb_s, cc_s, vec = _s5_tile_specs(lambda i: i, r)

    def body(u_ref, bb_ref, cc_ref, p_ref, d_ref, st_ref, y_ref, g_ref, cr, ci):
        @pl.when(pl.program_id(1) == 0)
        def _():
            cr[...] = jnp.zeros_like(cr)
            ci[...] = jnp.zeros_like(ci)

        u = u_ref[...]
        bu = _mxu(u, bb_ref[...])
        xr, xi, cr[...], ci[...] = _scan_block(bu[:, :t], bu[:, t:], p_ref[:, :t], p_ref[:, t:], cr[...], ci[...], False)
        st_ref[:, :t] = xr
        st_ref[:, t:] = xi
        y = _mxu(xr, cc_ref[:t, :]) + _mxu(xi, cc_ref[t:, :]) + d_ref[...] * u
        y_ref[...] = y
        g_ref[...] = _gelu(y).astype(BF16)

    return pl.pallas_call(
        body, name=name, grid=(2 * S5_GROUPS * S5_STATE // (2 * t), s // r),
        in_specs=[u_s, bb_s, cc_s, table, vec], out_specs=[state, col, col],
        out_shape=[jax.ShapeDtypeStruct((s, 2 * S5_GROUPS * S5_STATE), F32), jax.ShapeDtypeStruct((s, w), F32),
                   jax.ShapeDtypeStruct((s, w), BF16)],
        scratch_shapes=[pltpu.VMEM((1, t), F32), pltpu.VMEM((1, t), F32)],
        compiler_params=_cp("parallel", "arbitrary"),
    )(proj, bbt, cct, apow, dvec)


def _s5_bwd(dg1, dg2, y, proj, st, bbt, cct, apow_rev, dvec, dproj, name):
    s = proj.shape[0]
    r, t = SCAN_ROWS, S5_TILE
    nb = s // r
    w = S5_GROUPS * S5_GROUP
    u_s, col, state, table, bb_s, cc_s, vec = _s5_tile_specs(lambda i: nb - 1 - i, r)
    halo = pl.BlockSpec((8, 2 * t), lambda j, i: (jnp.maximum((nb - 1 - i) * (r // 8) - 1, 0), j))
    acc = pl.BlockSpec((1, 2 * t), lambda j, i: (0, j))

    def body(a_ref, b_ref, y_ref, u_ref, s_ref, sp_ref, bb_ref, cc_ref, p_ref, d_ref, _,
             du_ref, da_ref, dbb_ref, dcc_ref, dd_ref, cr, ci):
        i = pl.program_id(1)

        @pl.when(i == 0)
        def _():
            cr[...] = jnp.zeros_like(cr)
            ci[...] = jnp.zeros_like(ci)
            da_ref[...] = jnp.zeros_like(da_ref)
            dbb_ref[...] = jnp.zeros_like(dbb_ref)
            dcc_ref[...] = jnp.zeros_like(dcc_ref)
            dd_ref[...] = jnp.zeros_like(dd_ref)

        u = u_ref[...]
        dy = (a_ref[...] + b_ref[...]) * _gelu_grad(y_ref[...])
        dd_ref[...] += jnp.sum(dy * u, axis=0, keepdims=True)
        sr, si = s_ref[:, :t], s_ref[:, t:]
        dcc_ref[:t, :] += _mxu(sr, dy, 0, 0)
        dcc_ref[t:, :] += _mxu(si, dy, 0, 0)
        xr, xi, cr[...], ci[...] = _scan_block(_mxu(dy, cc_ref[:t, :], 1, 1), _mxu(dy, cc_ref[t:, :], 1, 1),
                                               p_ref[:, :t], p_ref[:, t:], cr[...], ci[...], True)
        du_ref[...] = (dy * d_ref[...] + _mxu(xr, bb_ref[:, :t], 1, 1) + _mxu(xi, bb_ref[:, t:], 1, 1)).astype(BF16)
        dbb_ref[:, :t] += _mxu(u, xr, 0, 0)
        dbb_ref[:, t:] += _mxu(u, xi, 0, 0)
        first = i == nb - 1
        row = lax.broadcasted_iota(jnp.int32, (r, t), 0)
        pr = jnp.where(row == 0, jnp.where(first, 0.0, sp_ref[7:8, :t]), pltpu.roll(sr, 1, 0))
        pi = jnp.where(row == 0, jnp.where(first, 0.0, sp_ref[7:8, t:]), pltpu.roll(si, 1, 0))
        da_ref[:, :t] += jnp.sum(xr * pr + xi * pi, axis=0, keepdims=True)
        da_ref[:, t:] += jnp.sum(xi * pr - xr * pi, axis=0, keepdims=True)

    return pl.pallas_call(
        body, name=name, grid=(2 * S5_GROUPS * S5_STATE // (2 * t), nb),
        in_specs=[col, col, col, u_s, state, halo, bb_s, cc_s, table, vec, pl.BlockSpec(memory_space=pl.ANY)],
        out_specs=[u_s, acc, bb_s, cc_s, vec],
        out_shape=[jax.ShapeDtypeStruct(dproj.shape, dproj.dtype), jax.ShapeDtypeStruct((1, 2 * S5_GROUPS * S5_STATE), F32),
                   jax.ShapeDtypeStruct(bbt.shape, F32), jax.ShapeDtypeStruct(cct.shape, F32),
                   jax.ShapeDtypeStruct((1, w), F32)],
        scratch_shapes=[pltpu.VMEM((1, t), F32), pltpu.VMEM((1, t), F32)],
        input_output_aliases={10: 0}, compiler_params=_cp("parallel", "arbitrary"),
    )(dg1, dg2, y, proj, st, st, bbt, cct, apow_rev, dvec, dproj)


def _s5_tile_b(b_re, b_im):
    nt = S5_GROUPS * S5_STATE // S5_TILE
    gpt = S5_GROUPS // nt
    eye = jnp.eye(gpt, dtype=F32)

    def tile(b):
        t5 = jnp.einsum("jghp,gk->jghkp", b.reshape(nt, gpt, S5_GROUP, S5_STATE), eye)
        return t5.reshape(nt, gpt * S5_GROUP, S5_TILE)

    return jnp.concatenate([tile(b_re), tile(b_im)], axis=2)


def _s5_untile_b(d):
    nt = S5_GROUPS * S5_STATE // S5_TILE
    gpt = S5_GROUPS // nt
    eye = jnp.eye(gpt, dtype=F32)

    def untile(x):
        x5 = x.reshape(nt, gpt, S5_GROUP, gpt, S5_STATE)
        return jnp.einsum("jghkp,gk->jghp", x5, eye).reshape(S5_GROUPS, S5_GROUP, S5_STATE)

    return untile(d[:, :, :S5_TILE]), untile(d[:, :, S5_TILE:])


def _s5_tile_c(c_re, c_im):
    nt = S5_GROUPS * S5_STATE // S5_TILE
    gpt = S5_GROUPS // nt
    eye = jnp.eye(gpt, dtype=F32)

    def tile(c):
        t5 = jnp.einsum("jgph,gk->jkpgh", c.reshape(nt, gpt, S5_STATE, S5_GROUP), eye)
        return t5.reshape(nt, S5_TILE, gpt * S5_GROUP)

    return jnp.concatenate([tile(c_re), -tile(c_im)], axis=1)


def _s5_untile_c(d):
    nt = S5_GROUPS * S5_STATE // S5_TILE
    gpt = S5_GROUPS // nt
    eye = jnp.eye(gpt, dtype=F32)

    def untile(x):
        x5 = x.reshape(nt, gpt, S5_STATE, gpt, S5_GROUP)
        return jnp.einsum("jkpgh,gk->jgph", x5, eye).reshape(S5_GROUPS, S5_STATE, S5_GROUP)

    return untile(d[:, :S5_TILE, :]), -untile(d[:, S5_TILE:, :])


def _row_call(body, name, s, ins, outs, acc=False):
    tr = min(512, s)

    def spec(width, cb, rows):
        if rows == 1:
            return pl.BlockSpec((1, width), lambda i: (0, cb))
        return pl.BlockSpec((tr, width), lambda i: (i, cb))

    in_specs = [spec(w, cb, a.shape[0]) for a, w, cb in ins]
    out_specs = [spec(w, cb, sd.shape[0]) for sd, w, cb in outs]
    return pl.pallas_call(
        body, name=name, grid=(s // tr,), in_specs=in_specs, out_specs=out_specs,
        out_shape=[sd for sd, _, _ in outs],
        compiler_params=_cp("arbitrary" if acc else "parallel"),
    )(*[a for a, _, _ in ins])


def _sds(shape, dtype):
    return jax.ShapeDtypeStruct(shape, dtype)


def _s5_glu_fwd(y, z, b, merged, name):
    s, w = y.shape
    tr = min(512, s)

    def body(y_ref, z_ref, b_ref, _, o_ref):
        o_ref[...] = (_gelu(y_ref[...]) * _sigmoid(z_ref[...] + b_ref[...])).astype(BF16)

    row = pl.BlockSpec((tr, w), lambda i: (i, 0))
    return pl.pallas_call(
        body, name=name, grid=(s // tr,),
        in_specs=[row, row, pl.BlockSpec((1, w), lambda i: (0, 0)), pl.BlockSpec(memory_space=pl.ANY)],
        out_specs=pl.BlockSpec((tr, w), lambda i: (i, 1)),
        out_shape=jax.ShapeDtypeStruct(merged.shape, merged.dtype),
        input_output_aliases={3: 0}, compiler_params=_cp("parallel"),
    )(y, z, b, merged)


def _s5_glu_bwd(dmerged, y, z, b, name):
    s, w = y.shape

    def body(do_ref, y_ref, z_ref, b_ref, dz_ref, dg_ref, db_ref):
        @pl.when(pl.program_id(0) == 0)
        def _():
            db_ref[...] = jnp.zeros_like(db_ref)

        g = _gelu(y_ref[...])
        sg = _sigmoid(z_ref[...] + b_ref[...])
        dout = do_ref[...]
        dz = dout * g * sg * (1.0 - sg)
        dz_ref[...] = dz.astype(BF16)
        dg_ref[...] = dout * sg
        db_ref[...] += jnp.sum(dz, axis=0, keepdims=True)

    return _row_call(body, name, s, [(dmerged, w, 1), (y, w, 0), (z, w, 0), (b, w, 0)],
                     [(_sds((s, w), BF16), w, 0), (_sds((s, w), F32), w, 0), (_sds((1, w), F32), w, 0)], acc=True)


def _gdn_conv_fwd(projx, cw, name):
    s = projx.shape[0]
    nh = GDN_HEADS

    def body(x_ref, w_ref, o_ref):
        j = pl.program_id(0)
        cv = _conv_fwd(_conv_taps(x_ref[...], GDN_CONV), w_ref)
        y = cv * _sigmoid(cv)
        nrm = y * lax.rsqrt(jnp.sum(y * y, axis=-1, keepdims=True) + EPS)
        o_ref[...] = jnp.where(j < nh, nrm * (GDN_DH ** -0.5), jnp.where(j < 2 * nh, nrm, y))

    return pl.pallas_call(
        body, name=name, grid=(3 * nh,),
        in_specs=[pl.BlockSpec((s, GDN_DH), lambda j: (0, j)), pl.BlockSpec((GDN_CONV, GDN_DH), lambda j: (0, j))],
        out_specs=pl.BlockSpec((s, GDN_DH), lambda j: (0, j)),
        out_shape=jax.ShapeDtypeStruct((s, 3 * nh * GDN_DH), F32), compiler_params=_cp("parallel"),
    )(projx, cw)


def _gdn_conv_bwd(projx, cw, dqkv, dprojx, name):
    s = projx.shape[0]
    nh = GDN_HEADS

    def body(x_ref, w_ref, d_ref, _, dx_ref, dw_ref):
        j = pl.program_id(0)
        x = x_ref[...]
        taps = _conv_taps(x, GDN_CONV)
        cv = _conv_fwd(taps, w_ref)
        sg = _sigmoid(cv)
        y = cv * sg
        rinv = lax.rsqrt(jnp.sum(y * y, axis=-1, keepdims=True) + EPS)
        nrm = y * rinv
        dn = d_ref[...]
        dns = jnp.where(j < nh, dn * (GDN_DH ** -0.5), dn)
        dyn = rinv * (dns - nrm * jnp.sum(dns * nrm, axis=-1, keepdims=True))
        dy = jnp.where(j < 2 * nh, dyn, dn)
        dc = dy * (sg * (1.0 + cv * (1.0 - sg)))
        dx_ref[...] = _conv_bwd(taps, dc, w_ref, dw_ref).astype(BF16)

    col = pl.BlockSpec((s, GDN_DH), lambda j: (0, j))
    wcol = pl.BlockSpec((GDN_CONV, GDN_DH), lambda j: (0, j))
    return pl.pallas_call(
        body, name=name, grid=(3 * nh,), in_specs=[col, wcol, col, pl.BlockSpec(memory_space=pl.ANY)],
        out_specs=[col, wcol],
        out_shape=[jax.ShapeDtypeStruct(dprojx.shape, dprojx.dtype), jax.ShapeDtypeStruct((GDN_CONV, 3 * nh * GDN_DH), F32)],
        input_output_aliases={3: 0}, compiler_params=_cp("parallel"),
    )(projx, cw, dqkv, dprojx)


def _softplus(x):
    return jnp.maximum(x, 0.0) + jnp.log1p(jnp.exp(-jnp.abs(x)))


def _gdn_gates_fwd(projx, alog, dtb, name):
    s = projx.shape[0]
    w = GDN_HEADS * GDN_DH
    tr = min(512, s)

    def body(t_ref, al_ref, dt_ref, bo_ref, go_ref):
        t = t_ref[...]
        for h in range(GDN_HEADS):
            lo, hi = h * GDN_DH, (h + 1) * GDN_DH
            b = jnp.broadcast_to(t[:, h:h + 1], (tr, GDN_DH))
            a = jnp.broadcast_to(t[:, GDN_HEADS + h:GDN_HEADS + h + 1], (tr, GDN_DH))
            bo_ref[:, lo:hi] = _sigmoid(b)
            go_ref[:, lo:hi] = -jnp.exp(al_ref[:, lo:hi]) * _softplus(a + dt_ref[:, lo:hi])

    row = pl.BlockSpec((tr, w), lambda i: (i, 0))
    vec = pl.BlockSpec((1, w), lambda i: (0, 0))
    return pl.pallas_call(
        body, name=name, grid=(s // tr,),
        in_specs=[pl.BlockSpec((tr, LANES), lambda i: (i, 4 * w // LANES)), vec, vec], out_specs=[row, row],
        out_shape=[jax.ShapeDtypeStruct((s, w), F32)] * 2, compiler_params=_cp("parallel"),
    )(projx, alog, dtb)


def _gdn_gates_bwd(projx, alog, dtb, dbeta, dg, dprojx, name):
    s = projx.shape[0]
    w = GDN_HEADS * GDN_DH
    tr = min(512, s)
    gate_blk = 4 * w // LANES

    def body(t_ref, al_ref, dt_ref, dbe_ref, dg_ref, _, o_ref, dal_ref, ddt_ref):
        @pl.when(pl.program_id(0) == 0)
        def _():
            dal_ref[...] = jnp.zeros_like(dal_ref)
            ddt_ref[...] = jnp.zeros_like(ddt_ref)

        t = t_ref[...]
        lane = lax.broadcasted_iota(jnp.int32, (tr, LANES), 1)
        lane1 = lax.broadcasted_iota(jnp.int32, (1, LANES), 1)
        out = jnp.zeros((tr, LANES), F32)
        dal = jnp.zeros((1, LANES), F32)
        ddt = jnp.zeros((1, LANES), F32)
        for h in range(GDN_HEADS):
            lo, hi = h * GDN_DH, (h + 1) * GDN_DH
            beta = _sigmoid(t[:, h:h + 1])
            pb = jnp.sum(dbe_ref[:, lo:hi], axis=-1, keepdims=True)
            db = pb * beta * (1.0 - beta)
            xa = t[:, GDN_HEADS + h:GDN_HEADS + h + 1] + dt_ref[:, lo:lo + 1]
            ea = -jnp.exp(al_ref[:, lo:lo + 1])
            pg = jnp.sum(dg_ref[:, lo:hi], axis=-1, keepdims=True)
            da = pg * ea * _sigmoid(xa)
            out = jnp.where(lane == h, db, jnp.where(lane == GDN_HEADS + h, da, out))
            dal = jnp.where(lane1 == h, jnp.sum(pg * ea * _softplus(xa), axis=0, keepdims=True), dal)
            ddt = jnp.where(lane1 == h, jnp.sum(da, axis=0, keepdims=True), ddt)
        o_ref[...] = out.astype(BF16)
        dal_ref[...] += dal
        ddt_ref[...] += ddt

    row = pl.BlockSpec((tr, w), lambda i: (i, 0))
    vec = pl.BlockSpec((1, w), lambda i: (0, 0))
    small = pl.BlockSpec((1, LANES), lambda i: (0, 0))
    gates = pl.BlockSpec((tr, LANES), lambda i: (i, gate_blk))
    return pl.pallas_call(
        body, name=name, grid=(s // tr,),
        in_specs=[gates, vec, vec, row, row, pl.BlockSpec(memory_space=pl.ANY)],
        out_specs=[gates, small, small],
        out_shape=[jax.ShapeDtypeStruct(dprojx.shape, dprojx.dtype), jax.ShapeDtypeStruct((1, LANES), F32),
                   jax.ShapeDtypeStruct((1, LANES), F32)],
        input_output_aliases={5: 0}, compiler_params=_cp("arbitrary"),
    )(projx, alog, dtb, dbeta, dg, dprojx)


def _gdn_tri():
    c = GDN_CHUNK
    i = lax.broadcasted_iota(jnp.int32, (c, c), 0)
    j = lax.broadcasted_iota(jnp.int32, (c, c), 1)
    return ((i >= j).astype(F32), (i <= j).astype(F32), i >= j, i > j, (i == j).astype(F32))


def _bdot(a, b, ca=2, cb=1, precision=None):
    return lax.dot_general(a, b, (((ca,), (cb,)), ((0,), (0,))), precision=precision, preferred_element_type=F32)


def _bmxu(a, b, ca=2, cb=1):
    return _bdot(a.astype(BF16), b.astype(BF16), ca, cb)


def _split(x):
    hi = x.astype(BF16)
    return hi, (x - hi.astype(F32)).astype(BF16)


def _bdot3(a, b, ca=2, cb=1):
    ah, al = _split(a)
    bh, bl = _split(b)
    return _bdot(ah, bh, ca, cb) + (_bdot(ah, bl, ca, cb) + _bdot(al, bh, ca, cb))


def _tri_dot(tri, x):
    t = tri.astype(BF16)
    hi = x.astype(BF16)
    r1 = x - hi.astype(F32)
    mid = r1.astype(BF16)
    lo = (r1 - mid.astype(F32)).astype(BF16)
    return _dot(t, hi) + (_dot(t, mid) + _dot(t, lo))


def _heads(x):
    return jnp.stack([x[:, h * GDN_DH:(h + 1) * GDN_DH] for h in range(GDN_HEADS)], axis=0)


def _unheads(x):
    return jnp.concatenate([x[h] for h in range(GDN_HEADS)], axis=1)


def _gdn_chunk(q, k, v, bb, g2d, tri, t=None):
    low, up, incl, strict, eye = tri
    c = GDN_CHUNK
    gc = _heads(_tri_dot(low, g2d))
    gci = gc[:, :, :c]
    gdiff = gci - jnp.swapaxes(gci, 1, 2)
    decay = jnp.where(incl, jnp.exp(jnp.where(incl, gdiff, 0.0)), 0.0)
    kb, vb = k * bb, v * bb
    kbk = _bmxu(kb, k, 2, 2)
    if t is None:
        x = -jnp.where(strict, kbk * decay, 0.0)
        t = eye + x
        p = x
        for _ in range(c.bit_length() - 2):
            p = _bdot3(p, p)
            t = t + _bdot3(t, p)
    eg = jnp.exp(gc)
    kbg = kb * eg
    gcl = gc[:, c - 1:c, :]
    ek = jnp.exp(gcl - gc)
    qkraw = _bmxu(q, k, 2, 2)
    return dict(decay=decay, kb=kb, vb=vb, kbk=kbk, t=t, eg=eg, kbg=kbg, ek=ek, gl=jnp.exp(gcl),
                w=_bmxu(t, kbg), u=_bmxu(t, vb), qkraw=qkraw, qk=jnp.where(incl, qkraw * decay, 0.0),
                qd=q * eg, kd=k * ek)


def _gdn_specs(n_of):
    c, w = GDN_CHUNK, GDN_HEADS * GDN_DH

    def blk(cb, width=w):
        return pl.BlockSpec((c, width), lambda n: (n_of(n), cb))

    st = pl.BlockSpec((None, GDN_HEADS, GDN_DH, GDN_DH), lambda n: (n_of(n), 0, 0, 0))
    vec = pl.BlockSpec((1, GDN_DH), lambda n: (0, 0))
    return blk, st, vec


def _gdn_load(qkv_ref, b_ref, g_ref, tri, t=None):
    w = GDN_HEADS * GDN_DH
    q, k, v = _heads(qkv_ref[:, :w]), _heads(qkv_ref[:, w:2 * w]), _heads(qkv_ref[:, 2 * w:])
    bb = _heads(b_ref[...])
    return q, k, v, bb, _gdn_chunk(q, k, v, bb, g_ref[...], tri, t)


def _gdn_fwd(qkv, beta, g, projx, onorm, name):
    s = qkv.shape[0]
    c = GDN_CHUNK
    nc = s // c
    w = GDN_HEADS * GDN_DH
    blk, st, vec = _gdn_specs(lambda n: n)
    inv = pl.BlockSpec((None, GDN_HEADS, c, c), lambda n: (n, 0, 0, 0))

    def body(qkv_ref, b_ref, g_ref, z_ref, on_ref, o_ref, st_ref, t_ref, state):
        @pl.when(pl.program_id(0) == 0)
        def _():
            state[...] = jnp.zeros_like(state)

        _, _, _, _, ch = _gdn_load(qkv_ref, b_ref, g_ref, _gdn_tri())
        t_ref[...] = ch["t"]
        sp = state[...]
        st_ref[...] = sp
        vn = ch["u"] - _bmxu(ch["w"], sp)
        o = _bmxu(ch["qd"], sp) + _bmxu(ch["qk"], vn)
        state[...] = sp * ch["gl"] + _bmxu(ch["kd"], vn, 1, 1)
        r = lax.rsqrt(jnp.mean(o * o, axis=-1, keepdims=True) + EPS)
        z = _heads(z_ref[...])
        o_ref[...] = _unheads(o * r * on_ref[...] * (z * _sigmoid(z))).astype(BF16)

    return pl.pallas_call(
        body, name=name, grid=(nc,),
        in_specs=[blk(0, 3 * w), blk(0), blk(0), blk(3), vec], out_specs=[blk(0), st, inv],
        out_shape=[jax.ShapeDtypeStruct((s, w), BF16), jax.ShapeDtypeStruct((nc, GDN_HEADS, GDN_DH, GDN_DH), F32),
                   jax.ShapeDtypeStruct((nc, GDN_HEADS, c, c), F32)],
        scratch_shapes=[pltpu.VMEM((GDN_HEADS, GDN_DH, GDN_DH), F32)],
        compiler_params=_cp("arbitrary"),
    )(qkv, beta, g, projx, onorm.reshape(1, -1))


def _gdn_bwd(qkv, beta, g, projx, onorm, states, tinv, dout, name):
    s = qkv.shape[0]
    c = GDN_CHUNK
    nc = s // c
    w = GDN_HEADS * GDN_DH
    blk, st, vec = _gdn_specs(lambda n: nc - 1 - n)
    inv = pl.BlockSpec((None, GDN_HEADS, c, c), lambda n: (nc - 1 - n, 0, 0, 0))

    def body(qkv_ref, b_ref, g_ref, z_ref, on_ref, st_ref, t_ref, do_ref,
             dqkv_ref, db_ref, dg_ref, dz_ref, don_ref, carry):
        @pl.when(pl.program_id(0) == 0)
        def _():
            carry[...] = jnp.zeros_like(carry)
            don_ref[...] = jnp.zeros_like(don_ref)

        tri = _gdn_tri()
        low, up, incl, strict, eye = tri
        q, k, v, bb, ch = _gdn_load(qkv_ref, b_ref, g_ref, tri, t_ref[...])
        sp = st_ref[...]
        vn = ch["u"] - _bmxu(ch["w"], sp)
        o = _bmxu(ch["qd"], sp) + _bmxu(ch["qk"], vn)
        r = lax.rsqrt(jnp.mean(o * o, axis=-1, keepdims=True) + EPS)
        orn = o * r
        z = _heads(z_ref[...])
        sg = _sigmoid(z)
        dout = _heads(do_ref[...])
        onw = on_ref[...]
        dz_ref[...] = _unheads(dout * orn * onw * (sg * (1.0 + z * (1.0 - sg)))).astype(BF16)
        don = dout * (z * sg)
        don_ref[...] += jnp.sum(jnp.sum(don * orn, axis=0), axis=0, keepdims=True)
        dor = don * onw
        do = r * (dor - orn * jnp.mean(dor * orn, axis=-1, keepdims=True))
        dsn = carry[...]
        dqd = _bmxu(do, sp, 2, 2)
        dqk = jnp.where(incl, _bmxu(do, vn, 2, 2), 0.0)
        dvn = _bmxu(ch["qk"], do, 1, 1) + _bmxu(ch["kd"], dsn)
        dkd = _bmxu(vn, dsn, 2, 2)
        dgl = jnp.sum(dsn * sp, axis=1, keepdims=True)
        dw = -_bmxu(dvn, sp, 2, 2)
        carry[...] = _bmxu(ch["qd"], do, 1, 1) + dsn * ch["gl"] - _bmxu(ch["w"], dvn, 1, 1)
        t = ch["t"]
        dvb = _bmxu(t, dvn, 1, 1)
        dkbg = _bmxu(t, dw, 1, 1)
        dt = _bmxu(dvn, ch["vb"], 2, 2) + _bmxu(dw, ch["kbg"], 2, 2)
        da = -_bdot3(_bdot3(t, dt, 1, 1), t, 2, 2)
        da = jnp.where(strict, da, 0.0)
        decay = ch["decay"]
        dkbk = da * decay
        dqkr = dqk * decay
        mdec = (da * ch["kbk"] + dqk * ch["qkraw"]) * decay
        dkb = _bmxu(dkbk, k) + dkbg * ch["eg"]
        dk = _bmxu(dkbk, ch["kb"], 1, 1) + _bmxu(dqkr, q, 1, 1) + dkd * ch["ek"] + dkb * bb
        dq = _bmxu(dqkr, k) + dqd * ch["eg"]
        tk = dkd * ch["kd"]
        dgcl = jnp.sum(tk, axis=1, keepdims=True) + dgl * ch["gl"]
        row = lax.broadcasted_iota(jnp.int32, (GDN_HEADS, c, GDN_DH), 1)
        zpad = jnp.zeros((GDN_HEADS, c, GDN_DH - c), F32)
        dgc = (jnp.concatenate([mdec, zpad], axis=2) - jnp.concatenate([jnp.swapaxes(mdec, 1, 2), zpad], axis=2)
               + dqd * ch["qd"] - tk + dkbg * ch["kbg"] + jnp.where(row == c - 1, dgcl, 0.0))
        dqkv_ref[:, :w] = _unheads(dq)
        dqkv_ref[:, w:2 * w] = _unheads(dk)
        dqkv_ref[:, 2 * w:] = _unheads(dvb * bb)
        db_ref[...] = _unheads(dkb * k + dvb * v)
        dg_ref[...] = _tri_dot(up, _unheads(dgc))

    return pl.pallas_call(
        body, name=name, grid=(nc,),
        in_specs=[blk(0, 3 * w), blk(0), blk(0), blk(3), vec, st, inv, blk(0)],
        out_specs=[blk(0, 3 * w), blk(0), blk(0), blk(3), vec],
        out_shape=[jax.ShapeDtypeStruct((s, 3 * w), F32), jax.ShapeDtypeStruct((s, w), F32),
                   jax.ShapeDtypeStruct((s, w), F32), jax.ShapeDtypeStruct(projx.shape, BF16),
                   jax.ShapeDtypeStruct((1, GDN_DH), F32)],
        scratch_shapes=[pltpu.VMEM((GDN_HEADS, GDN_DH, GDN_DH), F32)],
        compiler_params=_cp("arbitrary"),
    )(qkv, beta, g, projx, onorm.reshape(1, -1), states, tinv, dout)


_WEIGHTS = (
    "l0_mix_norm", "l0_w_in", "l0_ret_norm", "l0_s5_lambda_re", "l0_s5_lambda_im", "l0_s5_b_re", "l0_s5_b_im",
    "l0_s5_c_re", "l0_s5_c_im", "l0_s5_d", "l0_s5_log_dt", "l0_s5_w_glu", "l0_s5_b_glu", "l0_w_out",
    "l0_xa_norm", "l0_mem_norm", "l0_xa_wq", "l0_xa_wkv", "l0_xa_wo", "l0_ffn_norm", "l0_ffn_w_up",
    "l0_ffn_conv", "l0_ffn_w_down", "l1_mix_norm", "l1_w_in", "l1_conv", "l1_a_log", "l1_dt_bias", "l1_o_norm",
    "l1_w_out", "l1_xa_norm", "l1_mem_norm", "l1_xa_wq", "l1_xa_wkv", "l1_xa_wo", "l1_ffn_norm", "l1_ffn_w_up",
    "l1_ffn_conv", "l1_ffn_w_down", "final_norm")
_INPUTS = ("x", "mem") + _WEIGHTS + ("loss_target",) + tuple("m_" + n for n in _WEIGHTS) + tuple("v_" + n for n in _WEIGHTS)

_COL = ("l0_w_in", "l0_xa_wkv", "l0_ffn_w_up", "l0_ffn_conv", "l1_w_in", "l1_conv", "l1_xa_wkv", "l1_ffn_w_up",
        "l1_ffn_conv")
_ROW = ("l0_s5_w_glu", "l0_w_out", "l0_xa_wq", "l0_xa_wo", "l0_ffn_w_down", "l1_w_out", "l1_xa_wq", "l1_xa_wo",
        "l1_ffn_w_down")
_F32_WIRE = ("l0_ffn_conv", "l1_conv", "l1_ffn_conv")
_REP = tuple(n for n in _WEIGHTS if n not in _COL + _ROW)
_GATHER_GROUPS = (("l0_w_in", "l0_s5_w_glu", "l0_w_out"),
                  ("l0_xa_wq", "l0_xa_wkv", "l0_xa_wo", "l0_ffn_w_up", "l0_ffn_conv", "l0_ffn_w_down"),
                  ("l1_w_in", "l1_conv", "l1_w_out", "l1_xa_wq", "l1_xa_wkv", "l1_xa_wo"),
                  ("l1_ffn_w_up", "l1_ffn_conv", "l1_ffn_w_down"))


def _round_up(n, m):
    return (n + m - 1) // m * m


_REP_BIG = ("l0_s5_lambda_re", "l0_s5_lambda_im", "l0_s5_b_re", "l0_s5_b_im", "l0_s5_c_re", "l0_s5_c_im", "l0_s5_d")
_REP_LAST = "l0_mix_norm"
_REP_SMALL = tuple(n for n in _REP if n not in _REP_BIG + (_REP_LAST,))
PACK_WIDTH = 1024


def _pack_rows(ts):
    rows = [jnp.pad(t, ((0, 0), (0, PACK_WIDTH - t.shape[1]))) for t in ts]
    rows.append(jnp.zeros((_round_up(len(ts), 8) - len(ts), PACK_WIDTH), F32))
    return jnp.concatenate(rows, axis=0)


def _s5_interleave(re, im):
    lead = re.shape[:-1]
    nt = re.shape[-1] // S5_TILE
    both = jnp.stack([re.reshape(lead + (nt, S5_TILE)), im.reshape(lead + (nt, S5_TILE))], axis=-2)
    return both.reshape(lead + (2 * re.shape[-1],))


def _s5_split(x):
    lead = x.shape[:-1]
    y = x.reshape(lead + (x.shape[-1] // (2 * S5_TILE), 2, S5_TILE))
    return y[..., 0, :].reshape(lead + (-1,)), y[..., 1, :].reshape(lead + (-1,))


def _s5_discretise(lr, li, log_dt, b_re, b_im):
    dt = jnp.exp(log_dt)[:, None]
    mag = jnp.exp(lr * dt)
    a_re = mag * jnp.cos(li * dt)
    a_im = mag * jnp.sin(li * dt)
    den = lr * lr + li * li
    z_re = ((a_re - 1.0) * lr + a_im * li) / den
    z_im = (a_im * lr - (a_re - 1.0) * li) / den
    bb_re = z_re[:, None, :] * b_re - z_im[:, None, :] * b_im
    bb_im = z_re[:, None, :] * b_im + z_im[:, None, :] * b_re
    return a_re, a_im, bb_re, bb_im


def kernel(*args):
    p = dict(zip(_INPUTS, args, strict=True))
    x0, mem0, tgt = p["x"][0], p["mem"][0], p["loss_target"][0]
    s, d = x0.shape
    me = _slot(*_mesh_pos())
    grads = {}
    wire = {n: (F32 if n in _F32_WIRE else BF16) for n in _COL + _ROW}

    gather, pin, token = [], jnp.zeros((), F32), None
    for i, names in enumerate(_GATHER_GROUPS):
        zones = [_into_slot(p[n], wire[n], me, "place_" + n, pin=token) for n in names]
        handle, token = _push_start([], zones, f"gather{i}_start")
        gather.append(handle)
        pin = pin + token[0, 0]
    w = {}

    def gathered(i, after):
        for n, full in zip(_GATHER_GROUPS[i], _push_wait(gather[i], after, f"gather{i}_wait")):
            if n in _COL:
                full = full.transpose(1, 0, 2)
            w[n] = full.reshape(-1, full.shape[-1]) if n in _ROW else full.reshape(full.shape[0], -1)

    pending = []

    def exchange(names, gain, tag):
        slots = []
        for n in names:
            g = grads[n]
            if n in _COL:
                pieces = g if isinstance(g, tuple) else (g,)
                g = jnp.concatenate([t.reshape(t.shape[0], -1, p[n].shape[1]).transpose(1, 0, 2) for t in pieces], axis=0)
            else:
                g = g.reshape((N_DEV, -1) + g.shape[1:])
            slots.append(g.astype(wire[n]))
        handle, token = _push_start(slots, [], tag + "_start")
        pending.append((names, slots, handle, tag))
        return gain + token[0, 0]

    def xattn(pre, x_in, hx):
        q = _mm(hx, w[pre + "xa_wq"], out_dtype=BF16, name=pre + "xa_q")
        memn = _norm_fwd(mem0, p[pre + "mem_norm"], pre + "mem_norm_fwd")
        kv = _mm(memn, w[pre + "xa_wkv"], out_dtype=BF16, name=pre + "xa_kv")
        ao = _xattn_fwd(q, kv, pre + "xattn_fwd")
        x_out, hf = _mm(ao, w[pre + "xa_wo"], res=x_in, norm_gain=p[pre + "ffn_norm"], name=pre + "xa_o")
        return x_out, hf, (x_in, hx, q, memn, kv, ao)

    def xattn_bwd(pre, saved, dxo):
        x_in, hx, q, memn, kv, ao = saved
        dao = _mm(dxo, w[pre + "xa_wo"], tb=True, name=pre + "xa_o_dx")
        grads[pre + "xa_wo"] = _mm(ao, dxo, ta=True, out_dtype=BF16, name=pre + "xa_o_dw")
        dq, dkv = _xattn_bwd(q, kv, dao, pre + "xattn_bwd")
        grads[pre + "xa_wq"] = _mm(hx, dq, ta=True, out_dtype=BF16, name=pre + "xa_q_dw")
        grads[pre + "xa_wkv"] = _mm(memn, dkv, ta=True, out_dtype=BF16, name=pre + "xa_kv_dw")
        dmemn = _mm(dkv, w[pre + "xa_wkv"], tb=True, name=pre + "xa_kv_dx")
        gain = exchange((pre + "xa_wo", pre + "xa_wq", pre + "xa_wkv"), p[pre + "xa_norm"], pre + "xa_grads")
        dx_in, grads[pre + "xa_norm"] = _mm_norm_bwd(dq, w[pre + "xa_wq"], x_in, gain, dxo, pre + "xa_q_dx")
        _, grads[pre + "mem_norm"] = _norm_bwd(mem0, p[pre + "mem_norm"], dmemn, jnp.zeros_like(mem0), pre + "mem_norm_bwd")
        return dx_in

    def ffn(pre, x_in, hf, next_gain):
        up = _mm(hf, w[pre + "ffn_w_up"], out_dtype=BF16, name=pre + "ffn_up")
        act = _ffn_act_fwd(up, w[pre + "ffn_conv"], pre + "ffn_act_fwd")
        res = _mm(act, w[pre + "ffn_w_down"], res=x_in, norm_gain=next_gain, name=pre + "ffn_down")
        x_out, h_next = res if next_gain is not None else (res, None)
        return x_out, h_next, (x_in, hf, up, act)

    def ffn_bwd(pre, saved, dxo):
        x_in, hf, up, act = saved
        dact = _mm(dxo, w[pre + "ffn_w_down"], tb=True, out_dtype=BF16, name=pre + "ffn_down_dx")
        grads[pre + "ffn_w_down"] = _mm(act, dxo, ta=True, out_dtype=BF16, name=pre + "ffn_down_dw")
        dpu, dpg, dcu, dcg = _ffn_act_bwd(up, w[pre + "ffn_conv"], dact, pre + "ffn_act_bwd")
        grads[pre + "ffn_conv"] = jnp.concatenate([dcu, dcg], axis=1)
        grads[pre + "ffn_w_up"] = (_mm(hf, dpu, ta=True, out_dtype=BF16, name=pre + "ffn_up_dw_u"),
                                   _mm(hf, dpg, ta=True, out_dtype=BF16, name=pre + "ffn_up_dw_g"))
        gain = exchange((pre + "ffn_w_down", pre + "ffn_w_up", pre + "ffn_conv"), p[pre + "ffn_norm"], pre + "ffn_grads")
        dx_in, grads[pre + "ffn_norm"] = _mm_norm_bwd([dpu, dpg], w[pre + "ffn_w_up"], x_in, gain, dxo, pre + "ffn_up_dx")
        return dx_in

    cos, sin = _rope_tables(s)
    (a_re, a_im, bb_re, bb_im), disc_vjp = jax.vjp(
        _s5_discretise, p["l0_s5_lambda_re"], p["l0_s5_lambda_im"], p["l0_s5_log_dt"], p["l0_s5_b_re"], p["l0_s5_b_im"])
    apow, apow_rev = _s5_pow_tables(_s5_interleave(a_re.reshape(1, -1), a_im.reshape(1, -1)), "l0_s5_pow_tables")
    bbt = _s5_tile_b(bb_re, bb_im).astype(BF16)
    cct = _s5_tile_c(p["l0_s5_c_re"], p["l0_s5_c_im"]).astype(BF16)
    s5_d = p["l0_s5_d"].reshape(1, -1)
    b_glu = p["l0_s5_b_glu"].reshape(1, -1)

    h0 = _norm_fwd(x0, p["l0_mix_norm"] + pin, "l0_mix_norm_fwd")
    gathered(0, h0)
    proj = _mm(h0, w["l0_w_in"], name="l0_in")
    merged, ret_states = _ret_fwd(proj, cos, sin, p["l0_ret_norm"], "l0_ret_fwd")
    st, y, gy = _s5_fwd(proj, bbt, cct, apow, s5_d, "l0_s5_fwd")
    z = _mm(gy, w["l0_s5_w_glu"], name="l0_s5_glu_mm")
    merged = _s5_glu_fwd(y, z, b_glu, merged, "l0_s5_glu_fwd")
    x1, hx0 = _mm(merged, w["l0_w_out"], res=x0, norm_gain=p["l0_xa_norm"], name="l0_out")
    gathered(1, x1)
    x2, hf0, xa0 = xattn("l0_", x1, hx0)
    x3, h1, ff0 = ffn("l0_", x2, hf0, p["l1_mix_norm"])

    gathered(2, x3)
    w1 = w["l1_w_in"]
    wx = jnp.pad(w1, ((0, 0), (0, _round_up(w1.shape[1], LANES) - w1.shape[1])))
    alog_x = jnp.repeat(p["l1_a_log"], GDN_DH).reshape(1, -1)
    dtb_x = jnp.repeat(p["l1_dt_bias"], GDN_DH).reshape(1, -1)
    projx = _mm(h1, wx, name="l1_in")
    qkv = _gdn_conv_fwd(projx, w["l1_conv"], "l1_conv_fwd")
    beta, glog = _gdn_gates_fwd(projx, alog_x, dtb_x, "l1_gates_fwd")
    o_gdn, gdn_states, gdn_tinv = _gdn_fwd(qkv, beta, glog, projx, p["l1_o_norm"], "l1_gdn_fwd")
    x4, hx1 = _mm(o_gdn, w["l1_w_out"], res=x3, norm_gain=p["l1_xa_norm"], name="l1_out")
    x5, hf1, xa1 = xattn("l1_", x4, hx1)
    gathered(3, x5)
    x6, _, ff1 = ffn("l1_", x5, hf1, None)

    loss_part, dx6, grads["final_norm"] = _loss_head(x6, p["final_norm"], tgt, "loss_head")
    loss = lax.psum(loss_part[0, 0], ("x", "y", "c"))
    dx5 = ffn_bwd("l1_", ff1, dx6)
    dx4 = xattn_bwd("l1_", xa1, dx5)

    do_gdn = _mm(dx4, w["l1_w_out"], tb=True, name="l1_out_dx")
    grads["l1_w_out"] = _mm(o_gdn, dx4, ta=True, out_dtype=BF16, name="l1_out_dw")
    dqkv, dbeta, dglog, dprojx, grads["l1_o_norm"] = _gdn_bwd(
        qkv, beta, glog, projx, p["l1_o_norm"], gdn_states, gdn_tinv, do_gdn, "l1_gdn_bwd")
    dprojx, grads["l1_conv"] = _gdn_conv_bwd(projx, w["l1_conv"], dqkv, dprojx, "l1_conv_bwd")
    dprojx, dalog_x, ddtb_x = _gdn_gates_bwd(projx, alog_x, dtb_x, dbeta, dglog, dprojx, "l1_gates_bwd")
    grads["l1_w_in"] = _mm(h1, dprojx, ta=True, out_dtype=BF16, name="l1_in_dw")[:, :w1.shape[1]]
    grads["l1_a_log"] = dalog_x[0, :GDN_HEADS]
    grads["l1_dt_bias"] = ddtb_x[0, :GDN_HEADS]
    gain = exchange(("l1_w_out", "l1_w_in", "l1_conv"), p["l1_mix_norm"], "l1_mix_grads")
    dx3, grads["l1_mix_norm"] = _mm_norm_bwd(dprojx, wx, x3, gain, dx4, "l1_in_dx")

    dx2 = ffn_bwd("l0_", ff0, dx3)
    dx1 = xattn_bwd("l0_", xa0, dx2)

    dmerged = _mm(dx1, w["l0_w_out"], tb=True, name="l0_out_dx")
    grads["l0_w_out"] = _mm(merged, dx1, ta=True, out_dtype=BF16, name="l0_out_dw")
    dproj, grads["l0_ret_norm"] = _ret_bwd(proj, cos, sin, p["l0_ret_norm"], ret_states, dmerged, "l0_ret_bwd")
    dzg, dg1, grads["l0_s5_b_glu"] = _s5_glu_bwd(dmerged, y, z, b_glu, "l0_s5_glu_bwd")
    grads["l0_s5_w_glu"] = _mm(gy, dzg, ta=True, out_dtype=BF16, name="l0_s5_glu_dw")
    s5_d_after = exchange(("l0_w_out", "l0_s5_w_glu"), s5_d, "l0_out_grads")
    dg2 = _mm(dzg, w["l0_s5_w_glu"], tb=True, name="l0_s5_glu_dx")
    dproj, da_s5, dbbt, dcct, grads["l0_s5_d"] = _s5_bwd(dg1, dg2, y, proj, st, bbt, cct, apow_rev, s5_d_after, dproj, "l0_s5_bwd")
    dbb_re, dbb_im = _s5_untile_b(dbbt)
    grads["l0_s5_c_re"], grads["l0_s5_c_im"] = _s5_untile_c(dcct)
    da_re, da_im = (t.reshape(S5_GROUPS, S5_STATE) for t in _s5_split(da_s5[0]))
    (grads["l0_s5_lambda_re"], grads["l0_s5_lambda_im"], grads["l0_s5_log_dt"], grads["l0_s5_b_re"],
     grads["l0_s5_b_im"]) = disc_vjp((da_re, da_im, dbb_re, dbb_im))

    def as_2d(t):
        return t.reshape(-1, t.shape[-1])

    def as_row(t):
        return t.reshape(1, -1)

    small_own = _pack_rows([as_row(grads[n]) for n in _REP_SMALL])
    big_own = [as_2d(grads[n].reshape(p[n].shape)) for n in _REP_BIG]
    rep_zones = [_into_slot(small_own, F32, me, "place_rep0")]
    rep_zones += [_into_slot(t.reshape(-1, LANES), BF16, me, f"place_rep{i + 1}") for i, t in enumerate(big_own)]
    rep_handle, rep_token = _push_start([], rep_zones, "rep_grads_start")

    grads["l0_w_in"] = _mm(h0, dproj, ta=True, out_dtype=BF16, pin=rep_token, name="l0_in_dw")
    gain = exchange(("l0_w_in",), p["l0_mix_norm"], "l0_mix_grads")
    dx0, grads["l0_mix_norm"] = _mm_norm_bwd(dproj, w["l0_w_in"], x0, gain, dx1, "l0_in_dx")

    last_own = _pack_rows([as_row(grads[_REP_LAST])])
    last_handle, _ = _push_start([], [_into_slot(last_own, F32, me, "place_rep_last")], "rep_last_start")
    last_land, = _push_wait(last_handle, dx0, "rep_last_wait")
    rep_lands = _push_wait(rep_handle, last_land, "rep_grads_wait")
    rep_land = rep_lands[0]

    outs = {}
    kinds = ("grad_", "delta_", "new_m_", "new_v_")
    for names, slots, handle, tag in pending:
        for n, own_slots, land in zip(names, slots, _push_wait(handle, rep_land, tag + "_wait")):
            shape = p[n].shape
            own = lax.dynamic_index_in_dim(own_slots, me, 0, keepdims=False)
            res = _adamw(land, own, *(p[pre + n].reshape(own.shape) for pre in ("", "m_", "v_")), "adamw_" + n)
            for kind, t in zip(kinds, res):
                outs[kind + n] = t.reshape(shape)
    for n, own, land in zip(_REP_BIG, big_own, rep_lands[1:]):
        res = _adamw(land.reshape((N_DEV,) + own.shape), None, *(as_2d(p[pre + n]) for pre in ("", "m_", "v_")), "adamw_" + n)
        for kind, t in zip(kinds, res):
            outs[kind + n] = t.reshape(p[n].shape)
    for names, land, own, nm in ((_REP_SMALL, rep_land, small_own, "adamw_small"), ((_REP_LAST,), last_land, last_own, "adamw_last")):
        res = _adamw_rows(land, own, *([as_row(p[pre + n]) for n in names] for pre in ("", "m_", "v_")), nm)
        for j, kind in enumerate(kinds):
            for i, n in enumerate(names):
                outs[kind + n] = res[j * len(names) + i].reshape(p[n].shape)

    return (loss, dx0[None]) + tuple(outs[kind + n] for kind in kinds for n in _WEIGHTS)
```

```python
import math

import numpy as np
import jax
import jax.numpy as jnp
from jax import lax
from jax.experimental import pallas as pl
from jax.experimental.pallas import tpu as pltpu

F32 = jnp.float32
BF16 = jnp.bfloat16
EPS = 1e-6
N_DEV = 8
LANES = 128
VMEM_LIMIT = 56 * 1024 * 1024

RET_HEADS, RET_DH, RET_CHUNK = 4, 128, 128
S5_GROUPS, S5_GROUP, S5_STATE = 32, 16, 64
GDN_HEADS, GDN_DH, GDN_CHUNK, GDN_CONV = 8, 128, 64, 4
XA_HEADS, XA_DH = 4, 256
FFN_CONV = 3
SCAN_ROWS = 256

ADAM_LR, ADAM_B1, ADAM_B2, ADAM_EPS, ADAM_WD, ADAM_STEP = 0.001, 0.9, 0.999, 1e-08, 0.01, 10


def _cp(*sem):
    return pltpu.CompilerParams(dimension_semantics=sem if sem else None, vmem_limit_bytes=VMEM_LIMIT)


def _tile(n, cap):
    if n <= cap:
        return n
    best = None
    for t in range(LANES, cap + 1, LANES):
        if n % t == 0:
            best = t
    assert best is not None, n
    return best


def _dot(a, b, ca=1, cb=0, precision=None):
    return lax.dot_general(a, b, (((ca,), (cb,)), ((), ())), precision=precision, preferred_element_type=F32)


def _mxu(a, b, ca=1, cb=0):
    return _dot(a.astype(BF16), b.astype(BF16), ca, cb)


def _sigmoid(x):
    return 0.5 * jnp.tanh(0.5 * x) + 0.5


def _shift_down(x, k):
    r = pltpu.roll(x, k, 0)
    row = lax.broadcasted_iota(jnp.int32, (8,) + x.shape[1:], 0)
    return jnp.concatenate([jnp.where(row >= k, r[:8], 0.0), r[8:]], axis=0)


def _shift_up(x, k):
    n = x.shape[0]
    r = pltpu.roll(x, n - k, 0)
    row = lax.broadcasted_iota(jnp.int32, (8,) + x.shape[1:], 0)
    return jnp.concatenate([r[:n - 8], jnp.where(row < 8 - k, r[n - 8:], 0.0)], axis=0)


def _mesh_pos():
    return lax.axis_index("x"), lax.axis_index("y"), lax.axis_index("c")


def _slot(px, py, pc):
    return 4 * px + 2 * py + pc


def _all_peers(x, y, c):
    flips = [(fx, fy, fc) for fx in (0, 1) for fy in (0, 1) for fc in (0, 1)][1:]
    return [(1 - x if fx else x, 1 - y if fy else y, 1 - c if fc else c) for fx, fy, fc in flips]


_HBM = pl.BlockSpec(memory_space=pltpu.HBM)
_SEM = pl.BlockSpec(memory_space=pltpu.SEMAPHORE)
N_PEERS = N_DEV - 1


def _push_copies(srcs, lands, send_sems, recv_sems, start):
    x, y, c = _mesh_pos()
    me = _slot(x, y, c)
    out = []
    for k, to in enumerate(_all_peers(x, y, c)):
        for a in range(len(lands)):
            src = srcs[a].at[_slot(*to)] if a < len(srcs) else lands[a].at[me]
            dst = lands[a].at[me if start else _slot(*to)]
            out.append(pltpu.make_async_remote_copy(
                src_ref=src, dst_ref=dst, send_sem=send_sems.at[a * N_PEERS + k], recv_sem=recv_sems.at[a * N_PEERS + k],
                device_id=to, device_id_type=pl.DeviceIdType.MESH))
    return out


def _into_slot(x, dtype, me, name, pin=None):
    r, c = x.shape
    cap = max(16, 512 * 1024 // c)
    tr = max(t for t in range(16, min(r, cap) + 1, 16) if r % t == 0) if r % 16 == 0 else r

    def body(me_ref, x_ref, *rest):
        rest[-1][...] = x_ref[...].astype(dtype)

    in_specs = [pl.BlockSpec((tr, c), lambda i, me_ref: (i, 0))]
    args = (x,)
    if pin is not None:
        in_specs.append(pl.BlockSpec(pin.shape, lambda i, me_ref: (0, 0)))
        args += (pin,)
    return pl.pallas_call(
        body, name=name, out_shape=jax.ShapeDtypeStruct((N_DEV, r, c), dtype),
        grid_spec=pltpu.PrefetchScalarGridSpec(
            num_scalar_prefetch=1, grid=(r // tr,), in_specs=in_specs,
            out_specs=pl.BlockSpec((None, tr, c), lambda i, me_ref: (me_ref[0], i, 0))),
        compiler_params=_cp("parallel"),
    )(me.reshape(1).astype(jnp.int32), *args)


def _push_start(scatter, gather_lands, name):
    ns, n = len(scatter), len(scatter) + len(gather_lands)
    lands = [lax.empty(a.shape, a.dtype) for a in scatter] + list(gather_lands)

    def body(*refs):
        srcs, zones = refs[:ns], refs[ns:ns + n]
        for cp in _push_copies(srcs, zones, refs[ns + n], refs[ns + n + 1], True):
            cp.start()
        refs[-1][...] = jnp.zeros((8, LANES), F32)

    hbm_in = [pltpu.with_memory_space_constraint(a, pltpu.HBM) for a in list(scatter) + lands]
    res = pl.pallas_call(
        body, name=name,
        out_shape=(pltpu.SemaphoreType.DMA((n * N_PEERS,)), pltpu.SemaphoreType.DMA((n * N_PEERS,)))
        + tuple(pltpu.HBM(a.shape, a.dtype) for a in list(scatter) + lands)
        + (jax.ShapeDtypeStruct((8, LANES), F32),),
        in_specs=[_HBM] * (ns + n),
        out_specs=(_SEM, _SEM) + (_HBM,) * (ns + n) + (pl.BlockSpec(memory_space=pltpu.VMEM),),
        input_output_aliases={i: 2 + i for i in range(ns + n)},
        compiler_params=pltpu.CompilerParams(has_side_effects=pltpu.SideEffectType.DATAFLOW_SIDE_EFFECTING),
    )(*hbm_in)
    return (res[0], res[1], res[2:2 + ns], res[2 + ns:2 + ns + n]), res[-1]


def _push_wait(handle, after, name):
    send_sems, recv_sems, srcs, lands = handle
    ns, n = len(srcs), len(lands)

    def body(*refs):
        for cp in _push_copies(refs[:ns], refs[ns:ns + n], refs[ns + n], refs[ns + n + 1], False):
            cp.wait_send()
            cp.wait_recv()

    res = pl.pallas_call(
        body, name=name,
        out_shape=tuple(pltpu.HBM(a.shape, a.dtype) for a in list(srcs) + list(lands)),
        in_specs=[_HBM] * (ns + n) + [_SEM, _SEM, pl.BlockSpec(memory_space=pl.ANY)],
        out_specs=(_HBM,) * (ns + n),
        input_output_aliases={i: i for i in range(ns + n)},
        compiler_params=pltpu.CompilerParams(has_side_effects=pltpu.SideEffectType.DATAFLOW_SIDE_EFFECTING),
    )(*srcs, *lands, send_sems, recv_sems, after)
    return res[ns:]


def _mm(a, b, *, ta=False, tb=False, out_dtype=F32, res=None, pin=None, norm_gain=None, name="mm"):
    m, k = (a.shape[1], a.shape[0]) if ta else a.shape
    n = b.shape[0] if tb else b.shape[1]
    assert k == (b.shape[1] if tb else b.shape[0]), (a.shape, b.shape, ta, tb)
    tm, tn, tk = _tile(m, 1408), _tile(n, 1536), _tile(k, 1408)
    nk = k // tk
    has_res = res is not None
    has_norm = norm_gain is not None
    assert not has_norm or tn == n
    n_in = 2 + has_res + (pin is not None) + has_norm

    def body(*refs):
        a_ref, b_ref = refs[:2]
        r_ref = refs[2] if has_res else None
        o_ref = refs[n_in]
        part = _mxu(a_ref[...], b_ref[...], 0 if ta else 1, 1 if tb else 0)

        def finish(r):
            if has_res:
                r = r + r_ref[...].astype(F32)
            o_ref[...] = r.astype(out_dtype)
            if has_norm:
                scale = lax.rsqrt(jnp.mean(r * r, axis=-1, keepdims=True) + EPS)
                refs[n_in + 1][...] = (r * scale * refs[n_in - 1][...]).astype(BF16)

        if nk == 1:
            finish(part)
            return
        acc = refs[-1]
        kk = pl.program_id(2)

        @pl.when(kk == 0)
        def _():
            acc[...] = part

        @pl.when(kk > 0)
        def _():
            acc[...] += part

        @pl.when(kk == nk - 1)
        def _():
            finish(acc[...])

    a_spec = pl.BlockSpec((tk, tm), lambda i, j, kk: (kk, i)) if ta else pl.BlockSpec((tm, tk), lambda i, j, kk: (i, kk))
    b_spec = pl.BlockSpec((tn, tk), lambda i, j, kk: (j, kk)) if tb else pl.BlockSpec((tk, tn), lambda i, j, kk: (kk, j))
    o_spec = pl.BlockSpec((tm, tn), lambda i, j, kk: (i, j))
    in_specs = [a_spec, b_spec] + ([o_spec] if has_res else [])
    args = (a, b) + ((res,) if has_res else ())
    if pin is not None:
        in_specs.append(pl.BlockSpec(pin.shape, lambda i, j, kk: (0, 0)))
        args += (pin,)
    if has_norm:
        in_specs.append(pl.BlockSpec((1, n), lambda i, j, kk: (0, 0)))
        args += (norm_gain.reshape(1, n),)
    out = jax.ShapeDtypeStruct((m, n), out_dtype)
    return pl.pallas_call(
        body, name=name, grid=(m // tm, n // tn, nk), in_specs=in_specs,
        out_specs=[o_spec, o_spec] if has_norm else o_spec,
        out_shape=[out, jax.ShapeDtypeStruct((m, n), BF16)] if has_norm else out,
        scratch_shapes=[pltpu.VMEM((tm, tn), F32)] if nk > 1 else [],
        compiler_params=_cp("parallel", "parallel", "arbitrary"),
    )(*args)


def _mm_norm_bwd(dy, w, x, g, dres, name, pin=None):
    dys = list(dy) if isinstance(dy, (list, tuple)) else [dy]
    nq = len(dys)
    s, kq = dys[0].shape
    d = w.shape[0]
    tm, tk = min(1024 if nq == 1 else 512, s), _tile(kq, 1408)
    per = kq // tk
    nk = nq * per
    n_in = nq + 4 + (pin is not None)

    def body(*refs):
        w_ref, x_ref, g_ref, dres_ref = refs[nq:nq + 4]
        dx_ref, dg_ref = refs[n_in], refs[n_in + 1]
        i, kk = pl.program_id(0), pl.program_id(1)

        @pl.when((i == 0) & (kk == 0))
        def _():
            dg_ref[...] = jnp.zeros_like(dg_ref)

        def finish(dh):
            xv = x_ref[...]
            r = lax.rsqrt(jnp.mean(xv * xv, axis=-1, keepdims=True) + EPS)
            xn = xv * r
            dg_ref[...] += jnp.sum(dh * xn, axis=0, keepdims=True)
            dhg = dh * g_ref[...]
            dx = dres_ref[...] + r * (dhg - xn * jnp.mean(dhg * xn, axis=-1, keepdims=True))
            dx_ref[...] = dx
            refs[n_in + 2][...] = dx.astype(BF16)

        if nk == 1:
            finish(_mxu(refs[0][...], w_ref[...], 1, 1))
            return
        acc = refs[-1]
        for q in range(nq):
            @pl.when((kk >= q * per) & (kk < (q + 1) * per))
            def _(q=q):
                part = _mxu(refs[q][...], w_ref[...], 1, 1)

                @pl.when(kk == 0)
                def _():
                    acc[...] = part

                @pl.when(kk > 0)
                def _():
                    acc[...] += part

        @pl.when(kk == nk - 1)
        def _():
            finish(acc[...])

    row = pl.BlockSpec((tm, d), lambda i, kk: (i, 0))
    vec = pl.BlockSpec((1, d), lambda i, kk: (0, 0))
    in_specs = [pl.BlockSpec((tm, tk), lambda i, kk, q=q: (i, jnp.clip(kk - q * per, 0, per - 1))) for q in range(nq)]
    in_specs += [pl.BlockSpec((d, tk), lambda i, kk: (0, kk)), row, vec, row]
    args = (*dys, w, x, g.reshape(1, d), dres)
    if pin is not None:
        in_specs.append(pl.BlockSpec(pin.shape, lambda i, kk: (0, 0)))
        args += (pin,)
    return pl.pallas_call(
        body, name=name, grid=(s // tm, nk), in_specs=in_specs, out_specs=[row, vec, row],
        out_shape=[jax.ShapeDtypeStruct((s, d), F32), jax.ShapeDtypeStruct((1, d), F32), jax.ShapeDtypeStruct((s, d), BF16)],
        scratch_shapes=[pltpu.VMEM((tm, d), F32)] if nk > 1 else [],
        compiler_params=_cp("arbitrary", "arbitrary"),
    )(*args)


def _norm_fwd(x, g, name):
    s, d = x.shape
    tr = min(512, s)

    def body(x_ref, g_ref, o_ref):
        xv = x_ref[...]
        r = lax.rsqrt(jnp.mean(xv * xv, axis=-1, keepdims=True) + EPS)
        o_ref[...] = (xv * r * g_ref[...]).astype(BF16)

    row = pl.BlockSpec((tr, d), lambda i: (i, 0))
    return pl.pallas_call(
        body, name=name, grid=(s // tr,), in_specs=[row, pl.BlockSpec((1, d), lambda i: (0, 0))],
        out_specs=row, out_shape=jax.ShapeDtypeStruct((s, d), BF16), compiler_params=_cp("parallel"),
    )(x, g.reshape(1, d))


def _norm_bwd(x, g, dh, dres, name):
    s, d = x.shape
    tr = min(512, s)

    def body(x_ref, g_ref, dh_ref, dres_ref, dx_ref, dg_ref):
        @pl.when(pl.program_id(0) == 0)
        def _():
            dg_ref[...] = jnp.zeros_like(dg_ref)

        xv = x_ref[...]
        r = lax.rsqrt(jnp.mean(xv * xv, axis=-1, keepdims=True) + EPS)
        xn = xv * r
        dhv = dh_ref[...].astype(F32)
        dg_ref[...] += jnp.sum(dhv * xn, axis=0, keepdims=True)
        dhg = dhv * g_ref[...]
        dx_ref[...] = dres_ref[...] + r * (dhg - xn * jnp.mean(dhg * xn, axis=-1, keepdims=True))

    row = pl.BlockSpec((tr, d), lambda i: (i, 0))
    vec = pl.BlockSpec((1, d), lambda i: (0, 0))
    return pl.pallas_call(
        body, name=name, grid=(s // tr,), in_specs=[row, vec, row, row], out_specs=[row, vec],
        out_shape=[jax.ShapeDtypeStruct((s, d), F32), jax.ShapeDtypeStruct((1, d), F32)],
        compiler_params=_cp("arbitrary"),
    )(x, g.reshape(1, d), dh, dres)


def _loss_head(x, g, tgt, name):
    s, d = x.shape
    tr = min(512, s)

    def body(x_ref, g_ref, t_ref, l_ref, dx_ref, dg_ref, dxb_ref):
        @pl.when(pl.program_id(0) == 0)
        def _():
            dg_ref[...] = jnp.zeros_like(dg_ref)
            l_ref[...] = jnp.zeros_like(l_ref)

        xv = x_ref[...]
        r = lax.rsqrt(jnp.mean(xv * xv, axis=-1, keepdims=True) + EPS)
        xn = xv * r
        err = xn * g_ref[...] - t_ref[...]
        part = 0.5 * jnp.sum(jnp.mean(err * err, axis=-1, keepdims=True), axis=0, keepdims=True)
        l_ref[...] += jnp.broadcast_to(part, l_ref.shape)
        dy = err * (1.0 / d)
        dg_ref[...] += jnp.sum(dy * xn, axis=0, keepdims=True)
        dyg = dy * g_ref[...]
        dx = r * (dyg - xn * jnp.mean(dyg * xn, axis=-1, keepdims=True))
        dx_ref[...] = dx
        dxb_ref[...] = dx.astype(BF16)

    row = pl.BlockSpec((tr, d), lambda i: (i, 0))
    vec = pl.BlockSpec((1, d), lambda i: (0, 0))
    return pl.pallas_call(
        body, name=name, grid=(s // tr,), in_specs=[row, vec, row],
        out_specs=[pl.BlockSpec((1, LANES), lambda i: (0, 0)), row, vec, row],
        out_shape=[jax.ShapeDtypeStruct((1, LANES), F32), jax.ShapeDtypeStruct((s, d), F32),
                   jax.ShapeDtypeStruct((1, d), F32), jax.ShapeDtypeStruct((s, d), BF16)],
        compiler_params=_cp("arbitrary"),
    )(x, g.reshape(1, d), tgt)


def _sum_slots(landed_slot, own):
    me = _slot(*_mesh_pos())
    mine = own.astype(F32)
    g = jnp.where(me == 0, mine, landed_slot(0).astype(F32))
    for i in range(1, N_DEV):
        g = g + jnp.where(me == i, mine, landed_slot(i).astype(F32))
    return g


def _adam_update(g, w, m, v):
    mm = ADAM_B1 * m + (1.0 - ADAM_B1) * g
    vv = ADAM_B2 * v + (1.0 - ADAM_B2) * (g * g)
    m_hat = mm / (1.0 - ADAM_B1 ** ADAM_STEP)
    v_hat = vv / (1.0 - ADAM_B2 ** ADAM_STEP)
    return g, -ADAM_LR * (m_hat / (jnp.sqrt(v_hat) + ADAM_EPS) + ADAM_WD * w), mm, vv


def _adamw_rows(landed, own, ws, ms, vs, name):
    k = len(ws)
    sizes = [w.shape[1] for w in ws]

    def body(*refs):
        p_ref, o_ref = refs[:2]
        w_refs, m_refs, v_refs = refs[2:2 + k], refs[2 + k:2 + 2 * k], refs[2 + 2 * k:2 + 3 * k]
        outs = refs[2 + 3 * k:]
        for i, n in enumerate(sizes):
            g = _sum_slots(lambda s: p_ref[s, i:i + 1, :n], o_ref[i:i + 1, :n])
            res = _adam_update(g, w_refs[i][...], m_refs[i][...], v_refs[i][...])
            for j in range(4):
                outs[j * k + i][...] = res[j]

    return pl.pallas_call(
        body, name=name, out_shape=[jax.ShapeDtypeStruct((1, n), F32) for _ in range(4) for n in sizes],
    )(landed, own, *ws, *ms, *vs)


def _adamw(landed, own, w, m, v, name):
    r, c = w.shape
    cap = max(8, 256 * 1024 // c)
    tr = max(t for t in range(8, min(r, cap) + 1, 8) if r % t == 0) if r % 8 == 0 else r
    gathered = own is None

    def body(*refs):
        p_ref = refs[0]
        w_ref, m_ref, v_ref, g_ref, d_ref, nm_ref, nv_ref = refs[1 if gathered else 2:]
        if gathered:
            g = p_ref[0].astype(F32)
            for i in range(1, N_DEV):
                g = g + p_ref[i].astype(F32)
        else:
            g = _sum_slots(lambda i: p_ref[i], refs[1][...])
        g_ref[...], d_ref[...], nm_ref[...], nv_ref[...] = _adam_update(g, w_ref[...], m_ref[...], v_ref[...])

    blk = pl.BlockSpec((tr, c), lambda i: (i, 0))
    n_blk = 3 if gathered else 4
    return pl.pallas_call(
        body, name=name, grid=(r // tr,),
        in_specs=[pl.BlockSpec((N_DEV, tr, c), lambda i: (0, i, 0))] + [blk] * n_blk,
        out_specs=[blk] * 4, out_shape=[jax.ShapeDtypeStruct((r, c), F32)] * 4,
        compiler_params=_cp("parallel"),
    )(*((landed,) if gathered else (landed, own)), w, m, v)


def _conv_taps(x, kw):
    return [_shift_down(x, kw - 1 - j) for j in range(kw - 1)] + [x]


def _conv_fwd(taps, w_ref):
    acc = w_ref[0:1, :] * taps[0]
    for j in range(1, len(taps)):
        acc = acc + w_ref[j:j + 1, :] * taps[j]
    return acc


def _conv_bwd(taps, dy, w_ref, dw_ref):
    kw = len(taps)
    dx = w_ref[kw - 1:kw, :] * dy
    for j in range(kw):
        dw_ref[j:j + 1, :] = jnp.sum(dy * taps[j], axis=0, keepdims=True)
        if j < kw - 1:
            dx = dx + w_ref[j:j + 1, :] * _shift_up(dy, kw - 1 - j)
    return dx


def _ffn_act_fwd(pre, cw, name):
    s, f2 = pre.shape
    nt = f2 // 2 // LANES

    def body(pu_ref, pg_ref, wu_ref, wg_ref, o_ref):
        up = _conv_fwd(_conv_taps(pu_ref[...].astype(F32), FFN_CONV), wu_ref)
        gate = _conv_fwd(_conv_taps(pg_ref[...].astype(F32), FFN_CONV), wg_ref)
        o_ref[...] = (gate * _sigmoid(gate) * up).astype(BF16)

    def col(rows, off):
        return pl.BlockSpec((rows, LANES), lambda j: (0, j + off))

    return pl.pallas_call(
        body, name=name, grid=(nt,),
        in_specs=[col(s, 0), col(s, nt), col(FFN_CONV, 0), col(FFN_CONV, nt)], out_specs=col(s, 0),
        out_shape=jax.ShapeDtypeStruct((s, f2 // 2), BF16), compiler_params=_cp("parallel"),
    )(pre, pre, cw, cw)


def _ffn_act_bwd(pre, cw, dact, name):
    s, f2 = pre.shape
    f = f2 // 2
    nt = f // LANES

    def body(pu_ref, pg_ref, wu_ref, wg_ref, da_ref, dpu_ref, dpg_ref, dwu_ref, dwg_ref):
        pu, pg = pu_ref[...].astype(F32), pg_ref[...].astype(F32)
        tu, tg = _conv_taps(pu, FFN_CONV), _conv_taps(pg, FFN_CONV)
        up = _conv_fwd(tu, wu_ref)
        gate = _conv_fwd(tg, wg_ref)
        sg = _sigmoid(gate)
        da = da_ref[...].astype(F32)
        dup = da * gate * sg
        dgate = da * up * (sg * (1.0 + gate * (1.0 - sg)))
        dpu_ref[...] = _conv_bwd(tu, dup, wu_ref, dwu_ref).astype(BF16)
        dpg_ref[...] = _conv_bwd(tg, dgate, wg_ref, dwg_ref).astype(BF16)

    def col(rows, off):
        return pl.BlockSpec((rows, LANES), lambda j: (0, j + off))

    return pl.pallas_call(
        body, name=name, grid=(nt,),
        in_specs=[col(s, 0), col(s, nt), col(FFN_CONV, 0), col(FFN_CONV, nt), col(s, 0)],
        out_specs=[col(s, 0), col(s, 0), col(FFN_CONV, 0), col(FFN_CONV, 0)],
        out_shape=[jax.ShapeDtypeStruct((s, f), BF16), jax.ShapeDtypeStruct((s, f), BF16),
                   jax.ShapeDtypeStruct((FFN_CONV, f), F32), jax.ShapeDtypeStruct((FFN_CONV, f), F32)],
        compiler_params=_cp("parallel"),
    )(pre, pre, cw, cw, dact)


def _xa_probs(qh, kh):
    sc = _mxu(qh, kh, 1, 1) * (XA_DH ** -0.5)
    e = jnp.exp(sc - jnp.max(sc, axis=-1, keepdims=True))
    return e / jnp.sum(e, axis=-1, keepdims=True)


def _xattn_fwd(q, kv, name):
    s, d = q.shape
    m = kv.shape[0]
    tr = min(512, s)

    def body(q_ref, kv_ref, o_ref):
        for h in range(XA_HEADS):
            lo, hi = h * XA_DH, (h + 1) * XA_DH
            p = _xa_probs(q_ref[:, lo:hi], kv_ref[:, lo:hi])
            o_ref[:, lo:hi] = _mxu(p, kv_ref[:, d + lo:d + hi]).astype(BF16)

    row = pl.BlockSpec((tr, d), lambda i: (i, 0))
    return pl.pallas_call(
        body, name=name, grid=(s // tr,), in_specs=[row, pl.BlockSpec((m, 2 * d), lambda i: (0, 0))],
        out_specs=row, out_shape=jax.ShapeDtypeStruct((s, d), BF16), compiler_params=_cp("parallel"),
    )(q, kv)


def _xattn_bwd(q, kv, do, name):
    s, d = q.shape
    m = kv.shape[0]
    tr = min(512, s)

    def body(q_ref, kv_ref, do_ref, dq_ref, dkv_ref):
        @pl.when(pl.program_id(0) == 0)
        def _():
            dkv_ref[...] = jnp.zeros_like(dkv_ref)

        for h in range(XA_HEADS):
            lo, hi = h * XA_DH, (h + 1) * XA_DH
            qh, kh, vh = q_ref[:, lo:hi], kv_ref[:, lo:hi], kv_ref[:, d + lo:d + hi]
            doh = do_ref[:, lo:hi]
            p = _xa_probs(qh, kh)
            dp = _mxu(doh, vh, 1, 1)
            ds = p * (dp - jnp.sum(p * dp, axis=-1, keepdims=True)) * (XA_DH ** -0.5)
            dq_ref[:, lo:hi] = _mxu(ds, kh).astype(BF16)
            dkv_ref[:, lo:hi] += _mxu(ds, qh, 0, 0)
            dkv_ref[:, d + lo:d + hi] += _mxu(p, doh, 0, 0)

    row = pl.BlockSpec((tr, d), lambda i: (i, 0))
    full = pl.BlockSpec((m, 2 * d), lambda i: (0, 0))
    return pl.pallas_call(
        body, name=name, grid=(s // tr,), in_specs=[row, full, row], out_specs=[row, full],
        out_shape=[jax.ShapeDtypeStruct((s, d), BF16), jax.ShapeDtypeStruct((m, 2 * d), F32)],
        compiler_params=_cp("arbitrary"),
    )(q, kv, do)


def _ret_tables():
    c = RET_CHUNK
    lg = np.log1p(-np.exp2(-5.0 - np.arange(RET_HEADS, dtype=np.float32))).astype(np.float32)
    idx = np.arange(c, dtype=np.float32)
    diff = idx[:, None] - idx[None, :]
    intra = np.where(diff >= 0, np.exp(lg[:, None, None] * np.where(diff >= 0, diff, 0.0)), 0.0)
    rk = np.broadcast_to(np.exp(lg[:, None] * (c - 1 - idx))[:, :, None], (RET_HEADS, c, LANES))
    rq = np.broadcast_to(np.exp(lg[:, None] * (idx + 1))[:, :, None], (RET_HEADS, c, LANES))
    return jnp.asarray(np.stack([intra, rk, rq], axis=1).astype(np.float32))


def _rope_tables(s):
    half = RET_DH // 2
    inv = jnp.exp(-math.log(10000.0) * jnp.arange(half, dtype=F32) / half)
    ang = jnp.arange(s, dtype=F32)[:, None] * inv[None, :]
    cos, sin = jnp.cos(ang), jnp.sin(ang)
    return jnp.concatenate([cos, cos], axis=1), jnp.concatenate([-sin, sin], axis=1)


def _ret_specs(n_of):
    c, w = RET_CHUNK, RET_HEADS * RET_DH

    def part(off):
        return pl.BlockSpec((c, w), lambda n: (n_of(n), off))

    pos = pl.BlockSpec((c, RET_DH), lambda n: (n_of(n), 0))
    gain = pl.BlockSpec((1, w), lambda n: (0, 0))
    tab = pl.BlockSpec((RET_HEADS, 3, c, LANES), lambda n: (0, 0, 0, 0))
    st = pl.BlockSpec((RET_HEADS, None, RET_DH, RET_DH), lambda n: (0, n_of(n), 0, 0))
    return part, pos, gain, tab, st


def _rheads(x):
    return jnp.stack([x[:, h * RET_DH:(h + 1) * RET_DH] for h in range(RET_HEADS)], axis=0)


def _runheads(x):
    return jnp.concatenate([x[h] for h in range(RET_HEADS)], axis=1)


def _rope(x, cos, sin):
    return x * cos + pltpu.roll(x, RET_DH // 2, 2) * sin


def _ret_chunk(q_ref, k_ref, v_ref, cos_ref, sin_ref, tab_ref, prev):
    cos, sin = cos_ref[...], sin_ref[...]
    q = _rope(_rheads(q_ref[...]), cos, sin)
    k = _rope(_rheads(k_ref[...]), cos, sin) * (RET_DH ** -0.5)
    v = _rheads(v_ref[...])
    scores = _bmxu(q, k, 2, 2) * tab_ref[:, 0]
    qdec = q * tab_ref[:, 2]
    kdec = k * tab_ref[:, 1]
    o = _bmxu(scores, v) + _bmxu(qdec, prev)
    return q, k, v, scores, qdec, kdec, o


def _ret_fwd(proj, cos, sin, gain, name):
    s = proj.shape[0]
    c = RET_CHUNK
    nc = s // c
    part, pos, gvec, tab, st = _ret_specs(lambda n: n)

    def body(q_ref, k_ref, v_ref, g_ref, cos_ref, sin_ref, rn_ref, tab_ref, o_ref, st_ref, state):
        @pl.when(pl.program_id(0) == 0)
        def _():
            state[...] = jnp.zeros_like(state)

        prev = state[...]
        st_ref[...] = prev
        _, _, v, _, _, kdec, o = _ret_chunk(q_ref, k_ref, v_ref, cos_ref, sin_ref, tab_ref, prev)
        state[...] = prev * tab_ref[:, 2, c - 1:c, :] + _bmxu(kdec, v, 1, 1)
        r = lax.rsqrt(jnp.mean(o * o, axis=-1, keepdims=True) + EPS)
        gate = g_ref[...]
        o_ref[...] = (_runheads(o * r) * rn_ref[...] * (gate * _sigmoid(gate))).astype(BF16)

    return pl.pallas_call(
        body, name=name, grid=(nc,),
        in_specs=[part(0), part(1), part(2), part(3), pos, pos, gvec, tab],
        out_specs=[part(0), st],
        out_shape=[jax.ShapeDtypeStruct((s, 2 * RET_HEADS * RET_DH), BF16),
                   jax.ShapeDtypeStruct((RET_HEADS, nc, RET_DH, RET_DH), F32)],
        scratch_shapes=[pltpu.VMEM((RET_HEADS, RET_DH, RET_DH), F32)],
        compiler_params=_cp("arbitrary"),
    )(proj, proj, proj, proj, cos, sin, gain.reshape(1, -1), _ret_tables())


def _ret_bwd(proj, cos, sin, gain, states, dmerged, name):
    s = proj.shape[0]
    c = RET_CHUNK
    nc = s // c
    width = RET_HEADS * RET_DH
    part, pos, gvec, tab, st = _ret_specs(lambda n: nc - 1 - n)

    def body(q_ref, k_ref, v_ref, g_ref, cos_ref, sin_ref, rn_ref, tab_ref, st_ref, do_ref,
             dp_ref, drn_ref, carry):
        @pl.when(pl.program_id(0) == 0)
        def _():
            carry[...] = jnp.zeros_like(carry)
            drn_ref[...] = jnp.zeros_like(drn_ref)

        prev = st_ref[...]
        q, k, v, scores, qdec, kdec, o = _ret_chunk(q_ref, k_ref, v_ref, cos_ref, sin_ref, tab_ref, prev)
        r = lax.rsqrt(jnp.mean(o * o, axis=-1, keepdims=True) + EPS)
        on = o * r
        on2 = _runheads(on)
        gate = g_ref[...]
        sg = _sigmoid(gate)
        sil = gate * sg
        dout = do_ref[...]
        rn = rn_ref[...]
        dp_ref[:, 3 * width:] = (dout * on2 * rn * (sg * (1.0 + gate * (1.0 - sg)))).astype(BF16)
        drn_ref[...] += jnp.sum(dout * on2 * sil, axis=0, keepdims=True)
        don = _rheads(dout * rn * sil)
        do = r * (don - on * jnp.mean(don * on, axis=-1, keepdims=True))
        dc = carry[...]
        dsc = _bmxu(do, v, 2, 2) * tab_ref[:, 0]
        dq = _bmxu(dsc, k) + _bmxu(do, prev, 2, 2) * tab_ref[:, 2]
        dk = _bmxu(dsc, q, 1, 1) + _bmxu(v, dc, 2, 2) * tab_ref[:, 1]
        dv = _bmxu(scores, do, 1, 1) + _bmxu(kdec, dc)
        carry[...] = _bmxu(qdec, do, 1, 1) + dc * tab_ref[:, 2, c - 1:c, :]
        cos, sin = cos_ref[...], sin_ref[...]
        dk = dk * (RET_DH ** -0.5)
        dp_ref[:, :width] = _runheads(dq * cos + pltpu.roll(dq * sin, RET_DH // 2, 2)).astype(BF16)
        dp_ref[:, width:2 * width] = _runheads(dk * cos + pltpu.roll(dk * sin, RET_DH // 2, 2)).astype(BF16)
        dp_ref[:, 2 * width:3 * width] = _runheads(dv).astype(BF16)

    return pl.pallas_call(
        body, name=name, grid=(nc,),
        in_specs=[part(0), part(1), part(2), part(3), pos, pos, gvec, tab, st, part(0)],
        out_specs=[pl.BlockSpec((c, 4 * width), lambda n: (nc - 1 - n, 0)), gvec],
        out_shape=[jax.ShapeDtypeStruct(proj.shape, BF16), jax.ShapeDtypeStruct((1, width), F32)],
        scratch_shapes=[pltpu.VMEM((RET_HEADS, RET_DH, RET_DH), F32)],
        compiler_params=_cp("arbitrary"),
    )(proj, proj, proj, proj, cos, sin, gain.reshape(1, -1), _ret_tables(), states, dmerged)


S5_TILE = 512


def _cmul_add(xr, xi, ar, ai, yr, yi):
    return xr + ar * yr - ai * yi, xi + ar * yi + ai * yr


def _s5_pow_tables(a_il, name):
    r = SCAN_ROWS
    t = S5_TILE
    w2 = a_il.shape[1]

    def body(a_ref, up_ref, dn_ref):
        for j in range(w2 // (2 * t)):
            re, im = pl.ds(2 * t * j, t), pl.ds(2 * t * j + t, t)
            up_ref[0:1, re] = a_ref[:, re]
            up_ref[0:1, im] = a_ref[:, im]
            dn_ref[r - 1:r, re] = a_ref[:, re]
            dn_ref[r - 1:r, im] = -a_ref[:, im]
            n = 1
            while n < r:
                lr, li = up_ref[n - 1:n, re], up_ref[n - 1:n, im]
                xr, xi = up_ref[0:n, re], up_ref[0:n, im]
                up_ref[n:2 * n, re] = xr * lr - xi * li
                up_ref[n:2 * n, im] = xr * li + xi * lr
                yr, yi = dn_ref[r - n:r, re], dn_ref[r - n:r, im]
                dn_ref[r - 2 * n:r - n, re] = yr * lr + yi * li
                dn_ref[r - 2 * n:r - n, im] = yi * lr - yr * li
                n *= 2

    return pl.pallas_call(
        body, name=name, out_shape=[jax.ShapeDtypeStruct((r, w2), F32)] * 2, compiler_params=_cp(),
    )(a_il)


_GELU_C = math.sqrt(2.0 / math.pi)
_GELU_A = 0.044715


def _gelu(y):
    return 0.5 * y * (1.0 + jnp.tanh(_GELU_C * (y + _GELU_A * y * y * y)))


def _gelu_grad(y):
    th = jnp.tanh(_GELU_C * (y + _GELU_A * y * y * y))
    return 0.5 * (1.0 + th) + 0.5 * y * (1.0 - th * th) * _GELU_C * (1.0 + 3.0 * _GELU_A * y * y)


def _rows_shift(x, k, axis, up):
    n = x.shape[axis]
    idx = lax.broadcasted_iota(jnp.int32, x.shape, axis)
    if up:
        return jnp.where(idx < n - k, pltpu.roll(x, n - k, axis), 0.0)
    return jnp.where(idx >= k, pltpu.roll(x, k, axis), 0.0)


def _scan_block(xr, xi, pr, pi, cr, ci, rev):
    r, w = xr.shape
    nt = r // 8
    x3r, x3i = xr.reshape(nt, 8, w), xi.reshape(nt, 8, w)
    p3r, p3i = pr.reshape(nt, 8, w), pi.reshape(nt, 8, w)

    def power(rows):
        t = r - rows if rev else rows - 1
        return pr[t:t + 1, :], pi[t:t + 1, :]

    tile_row = lax.broadcasted_iota(jnp.int32, (8, w), 0)
    for sh in (1, 2, 4):
        ar, ai = power(sh)
        keep = tile_row < 8 - sh if rev else tile_row >= sh
        mr, mi = jnp.where(keep, ar, 0.0)[None], jnp.where(keep, ai, 0.0)[None]
        turn = 8 - sh if rev else sh
        x3r, x3i = _cmul_add(x3r, x3i, mr, mi, pltpu.roll(x3r, turn, 1), pltpu.roll(x3i, turn, 1))
    edge = 0 if rev else 7
    lr, li = x3r[:, edge, :], x3i[:, edge, :]
    sh = 1
    while sh < nt:
        ar, ai = power(8 * sh)
        lr, li = _cmul_add(lr, li, ar, ai, _rows_shift(lr, sh, 0, rev), _rows_shift(li, sh, 0, rev))
        sh *= 2
    tr_, ti_ = p3r[:, edge, :], p3i[:, edge, :]
    first = lax.broadcasted_iota(jnp.int32, (nt, w), 0) == (nt - 1 if rev else 0)
    wr = jnp.where(first, 1.0, _rows_shift(tr_, 1, 0, rev))
    wi = jnp.where(first, 0.0, _rows_shift(ti_, 1, 0, rev))
    er, ei = _cmul_add(_rows_shift(lr, 1, 0, rev), _rows_shift(li, 1, 0, rev), wr, wi, cr, ci)
    a8r, a8i = (p3r[nt - 1], p3i[nt - 1]) if rev else (p3r[0], p3i[0])
    x3r, x3i = _cmul_add(x3r, x3i, a8r[None], a8i[None], er[:, None, :], ei[:, None, :])
    outr, outi = x3r.reshape(r, w), x3i.reshape(r, w)
    last = 0 if rev else r - 1
    return outr, outi, outr[last:last + 1, :], outi[last:last + 1, :]


def _s5_tile_specs(n_of, r):
    t = S5_TILE
    ucol = 4 * RET_HEADS * RET_DH // LANES
    u = pl.BlockSpec((r, LANES), lambda j, i: (n_of(i), ucol + j))
    col = pl.BlockSpec((r, LANES), lambda j, i: (n_of(i), j))
    state = pl.BlockSpec((r, 2 * t), lambda j, i: (n_of(i), j))
    table = pl.BlockSpec((r, 2 * t), lambda j, i: (0, j))
    bbt = pl.BlockSpec((None, LANES, 2 * t), lambda j, i: (j, 0, 0))
    cct = pl.BlockSpec((None, 2 * t, LANES), lambda j, i: (j, 0, 0))
    vec = pl.BlockSpec((1, LANES), lambda j, i: (0, j))
    return u, col, state, table, bbt, cct, vec


def _s5_fwd(proj, bbt, cct, apow, dvec, name):
    s = proj.shape[0]
    r, t = SCAN_ROWS, S5_TILE
    w = S5_GROUPS * S5_GROUP
    u_s, col, state, table, bb_s, cc_s, vec = _s5_tile_specs(lambda i: i, r)

    def body(u_ref, bb_ref, cc_ref, p_ref, d_ref, st_ref, y_ref, g_ref, cr, ci):
        @pl.when(pl.program_id(1) == 0)
        def _():
            cr[...] = jnp.zeros_like(cr)
            ci[...] = jnp.zeros_like(ci)

        u = u_ref[...]
        bu = _mxu(u, bb_ref[...])
        xr, xi, cr[...], ci[...] = _scan_block(bu[:, :t], bu[:, t:], p_ref[:, :t], p_ref[:, t:], cr[...], ci[...], False)
        st_ref[:, :t] = xr
        st_ref[:, t:] = xi
        y = _mxu(xr, cc_ref[:t, :]) + _mxu(xi, cc_ref[t:, :]) + d_ref[...] * u
        y_ref[...] = y
        g_ref[...] = _gelu(y).astype(BF16)

    return pl.pallas_call(
        body, name=name, grid=(2 * S5_GROUPS * S5_STATE // (2 * t), s // r),
        in_specs=[u_s, bb_s, cc_s, table, vec], out_specs=[state, col, col],
        out_shape=[jax.ShapeDtypeStruct((s, 2 * S5_GROUPS * S5_STATE), F32), jax.ShapeDtypeStruct((s, w), F32),
                   jax.ShapeDtypeStruct((s, w), BF16)],
        scratch_shapes=[pltpu.VMEM((1, t), F32), pltpu.VMEM((1, t), F32)],
        compiler_params=_cp("parallel", "arbitrary"),
    )(proj, bbt, cct, apow, dvec)


def _s5_bwd(dg1, dg2, y, proj, st, bbt, cct, apow_rev, dvec, dproj, name):
    s = proj.shape[0]
    r, t = SCAN_ROWS, S5_TILE
    nb = s // r
    w = S5_GROUPS * S5_GROUP
    u_s, col, state, table, bb_s, cc_s, vec = _s5_tile_specs(lambda i: nb - 1 - i, r)
    halo = pl.BlockSpec((8, 2 * t), lambda j, i: (jnp.maximum((nb - 1 - i) * (r // 8) - 1, 0), j))
    acc = pl.BlockSpec((1, 2 * t), lambda j, i: (0, j))

    def body(a_ref, b_ref, y_ref, u_ref, s_ref, sp_ref, bb_ref, cc_ref, p_ref, d_ref, _,
             du_ref, da_ref, dbb_ref, dcc_ref, dd_ref, cr, ci):
        i = pl.program_id(1)

        @pl.when(i == 0)
        def _():
            cr[...] = jnp.zeros_like(cr)
            ci[...] = jnp.zeros_like(ci)
            da_ref[...] = jnp.zeros_like(da_ref)
            dbb_ref[...] = jnp.zeros_like(dbb_ref)
            dcc_ref[...] = jnp.zeros_like(dcc_ref)
            dd_ref[...] = jnp.zeros_like(dd_ref)

        u = u_ref[...]
        dy = (a_ref[...] + b_ref[...]) * _gelu_grad(y_ref[...])
        dd_ref[...] += jnp.sum(dy * u, axis=0, keepdims=True)
        sr, si = s_ref[:, :t], s_ref[:, t:]
        dcc_ref[:t, :] += _mxu(sr, dy, 0, 0)
        dcc_ref[t:, :] += _mxu(si, dy, 0, 0)
        xr, xi, cr[...], ci[...] = _scan_block(_mxu(dy, cc_ref[:t, :], 1, 1), _mxu(dy, cc_ref[t:, :], 1, 1),
                                               p_ref[:, :t], p_ref[:, t:], cr[...], ci[...], True)
        du_ref[...] = (dy * d_ref[...] + _mxu(xr, bb_ref[:, :t], 1, 1) + _mxu(xi, bb_ref[:, t:], 1, 1)).astype(BF16)
        dbb_ref[:, :t] += _mxu(u, xr, 0, 0)
        dbb_ref[:, t:] += _mxu(u, xi, 0, 0)
        first = i == nb - 1
        row = lax.broadcasted_iota(jnp.int32, (r, t), 0)
        pr = jnp.where(row == 0, jnp.where(first, 0.0, sp_ref[7:8, :t]), pltpu.roll(sr, 1, 0))
        pi = jnp.where(row == 0, jnp.where(first, 0.0, sp_ref[7:8, t:]), pltpu.roll(si, 1, 0))
        da_ref[:, :t] += jnp.sum(xr * pr + xi * pi, axis=0, keepdims=True)
        da_ref[:, t:] += jnp.sum(xi * pr - xr * pi, axis=0, keepdims=True)

    return pl.pallas_call(
        body, name=name, grid=(2 * S5_GROUPS * S5_STATE // (2 * t), nb),
        in_specs=[col, col, col, u_s, state, halo, bb_s, cc_s, table, vec, pl.BlockSpec(memory_space=pl.ANY)],
        out_specs=[u_s, acc, bb_s, cc_s, vec],
        out_shape=[jax.ShapeDtypeStruct(dproj.shape, dproj.dtype), jax.ShapeDtypeStruct((1, 2 * S5_GROUPS * S5_STATE), F32),
                   jax.ShapeDtypeStruct(bbt.shape, F32), jax.ShapeDtypeStruct(cct.shape, F32),
                   jax.ShapeDtypeStruct((1, w), F32)],
        scratch_shapes=[pltpu.VMEM((1, t), F32), pltpu.VMEM((1, t), F32)],
        input_output_aliases={10: 0}, compiler_params=_cp("parallel", "arbitrary"),
    )(dg1, dg2, y, proj, st, st, bbt, cct, apow_rev, dvec, dproj)


def _s5_tile_b(b_re, b_im):
    nt = S5_GROUPS * S5_STATE // S5_TILE
    gpt = S5_GROUPS // nt
    eye = jnp.eye(gpt, dtype=F32)

    def tile(b):
        t5 = jnp.einsum("jghp,gk->jghkp", b.reshape(nt, gpt, S5_GROUP, S5_STATE), eye)
        return t5.reshape(nt, gpt * S5_GROUP, S5_TILE)

    return jnp.concatenate([tile(b_re), tile(b_im)], axis=2)


def _s5_untile_b(d):
    nt = S5_GROUPS * S5_STATE // S5_TILE
    gpt = S5_GROUPS // nt
    eye = jnp.eye(gpt, dtype=F32)

    def untile(x):
        x5 = x.reshape(nt, gpt, S5_GROUP, gpt, S5_STATE)
        return jnp.einsum("jghkp,gk->jghp", x5, eye).reshape(S5_GROUPS, S5_GROUP, S5_STATE)

    return untile(d[:, :, :S5_TILE]), untile(d[:, :, S5_TILE:])


def _s5_tile_c(c_re, c_im):
    nt = S5_GROUPS * S5_STATE // S5_TILE
    gpt = S5_GROUPS // nt
    eye = jnp.eye(gpt, dtype=F32)

    def tile(c):
        t5 = jnp.einsum("jgph,gk->jkpgh", c.reshape(nt, gpt, S5_STATE, S5_GROUP), eye)
        return t5.reshape(nt, S5_TILE, gpt * S5_GROUP)

    return jnp.concatenate([tile(c_re), -tile(c_im)], axis=1)


def _s5_untile_c(d):
    nt = S5_GROUPS * S5_STATE // S5_TILE
    gpt = S5_GROUPS // nt
    eye = jnp.eye(gpt, dtype=F32)

    def untile(x):
        x5 = x.reshape(nt, gpt, S5_STATE, gpt, S5_GROUP)
        return jnp.einsum("jkpgh,gk->jgph", x5, eye).reshape(S5_GROUPS, S5_STATE, S5_GROUP)

    return untile(d[:, :S5_TILE, :]), -untile(d[:, S5_TILE:, :])


def _row_call(body, name, s, ins, outs, acc=False):
    tr = min(512, s)

    def spec(width, cb, rows):
        if rows == 1:
            return pl.BlockSpec((1, width), lambda i: (0, cb))
        return pl.BlockSpec((tr, width), lambda i: (i, cb))

    in_specs = [spec(w, cb, a.shape[0]) for a, w, cb in ins]
    out_specs = [spec(w, cb, sd.shape[0]) for sd, w, cb in outs]
    return pl.pallas_call(
        body, name=name, grid=(s // tr,), in_specs=in_specs, out_specs=out_specs,
        out_shape=[sd for sd, _, _ in outs],
        compiler_params=_cp("arbitrary" if acc else "parallel"),
    )(*[a for a, _, _ in ins])


def _sds(shape, dtype):
    return jax.ShapeDtypeStruct(shape, dtype)


def _s5_glu_fwd(y, z, b, merged, name):
    s, w = y.shape
    tr = min(512, s)

    def body(y_ref, z_ref, b_ref, _, o_ref):
        o_ref[...] = (_gelu(y_ref[...]) * _sigmoid(z_ref[...] + b_ref[...])).astype(BF16)

    row = pl.BlockSpec((tr, w), lambda i: (i, 0))
    return pl.pallas_call(
        body, name=name, grid=(s // tr,),
        in_specs=[row, row, pl.BlockSpec((1, w), lambda i: (0, 0)), pl.BlockSpec(memory_space=pl.ANY)],
        out_specs=pl.BlockSpec((tr, w), lambda i: (i, 1)),
        out_shape=jax.ShapeDtypeStruct(merged.shape, merged.dtype),
        input_output_aliases={3: 0}, compiler_params=_cp("parallel"),
    )(y, z, b, merged)


def _s5_glu_bwd(dmerged, y, z, b, name):
    s, w = y.shape

    def body(do_ref, y_ref, z_ref, b_ref, dz_ref, dg_ref, db_ref):
        @pl.when(pl.program_id(0) == 0)
        def _():
            db_ref[...] = jnp.zeros_like(db_ref)

        g = _gelu(y_ref[...])
        sg = _sigmoid(z_ref[...] + b_ref[...])
        dout = do_ref[...]
        dz = dout * g * sg * (1.0 - sg)
        dz_ref[...] = dz.astype(BF16)
        dg_ref[...] = dout * sg
        db_ref[...] += jnp.sum(dz, axis=0, keepdims=True)

    return _row_call(body, name, s, [(dmerged, w, 1), (y, w, 0), (z, w, 0), (b, w, 0)],
                     [(_sds((s, w), BF16), w, 0), (_sds((s, w), F32), w, 0), (_sds((1, w), F32), w, 0)], acc=True)


def _gdn_conv_fwd(projx, cw, name):
    s = projx.shape[0]
    nh = GDN_HEADS

    def body(x_ref, w_ref, o_ref):
        j = pl.program_id(0)
        cv = _conv_fwd(_conv_taps(x_ref[...], GDN_CONV), w_ref)
        y = cv * _sigmoid(cv)
        nrm = y * lax.rsqrt(jnp.sum(y * y, axis=-1, keepdims=True) + EPS)
        o_ref[...] = jnp.where(j < nh, nrm * (GDN_DH ** -0.5), jnp.where(j < 2 * nh, nrm, y))

    return pl.pallas_call(
        body, name=name, grid=(3 * nh,),
        in_specs=[pl.BlockSpec((s, GDN_DH), lambda j: (0, j)), pl.BlockSpec((GDN_CONV, GDN_DH), lambda j: (0, j))],
        out_specs=pl.BlockSpec((s, GDN_DH), lambda j: (0, j)),
        out_shape=jax.ShapeDtypeStruct((s, 3 * nh * GDN_DH), F32), compiler_params=_cp("parallel"),
    )(projx, cw)


def _gdn_conv_bwd(projx, cw, dqkv, dprojx, name):
    s = projx.shape[0]
    nh = GDN_HEADS

    def body(x_ref, w_ref, d_ref, _, dx_ref, dw_ref):
        j = pl.program_id(0)
        x = x_ref[...]
        taps = _conv_taps(x, GDN_CONV)
        cv = _conv_fwd(taps, w_ref)
        sg = _sigmoid(cv)
        y = cv * sg
        rinv = lax.rsqrt(jnp.sum(y * y, axis=-1, keepdims=True) + EPS)
        nrm = y * rinv
        dn = d_ref[...]
        dns = jnp.where(j < nh, dn * (GDN_DH ** -0.5), dn)
        dyn = rinv * (dns - nrm * jnp.sum(dns * nrm, axis=-1, keepdims=True))
        dy = jnp.where(j < 2 * nh, dyn, dn)
        dc = dy * (sg * (1.0 + cv * (1.0 - sg)))
        dx_ref[...] = _conv_bwd(taps, dc, w_ref, dw_ref).astype(BF16)

    col = pl.BlockSpec((s, GDN_DH), lambda j: (0, j))
    wcol = pl.BlockSpec((GDN_CONV, GDN_DH), lambda j: (0, j))
    return pl.pallas_call(
        body, name=name, grid=(3 * nh,), in_specs=[col, wcol, col, pl.BlockSpec(memory_space=pl.ANY)],
        out_specs=[col, wcol],
        out_shape=[jax.ShapeDtypeStruct(dprojx.shape, dprojx.dtype), jax.ShapeDtypeStruct((GDN_CONV, 3 * nh * GDN_DH), F32)],
        input_output_aliases={3: 0}, compiler_params=_cp("parallel"),
    )(projx, cw, dqkv, dprojx)


def _softplus(x):
    return jnp.maximum(x, 0.0) + jnp.log1p(jnp.exp(-jnp.abs(x)))


def _gdn_gates_fwd(projx, alog, dtb, name):
    s = projx.shape[0]
    w = GDN_HEADS * GDN_DH
    tr = min(512, s)

    def body(t_ref, al_ref, dt_ref, bo_ref, go_ref):
        t = t_ref[...]
        for h in range(GDN_HEADS):
            lo, hi = h * GDN_DH, (h + 1) * GDN_DH
            b = jnp.broadcast_to(t[:, h:h + 1], (tr, GDN_DH))
            a = jnp.broadcast_to(t[:, GDN_HEADS + h:GDN_HEADS + h + 1], (tr, GDN_DH))
            bo_ref[:, lo:hi] = _sigmoid(b)
            go_ref[:, lo:hi] = -jnp.exp(al_ref[:, lo:hi]) * _softplus(a + dt_ref[:, lo:hi])

    row = pl.BlockSpec((tr, w), lambda i: (i, 0))
    vec = pl.BlockSpec((1, w), lambda i: (0, 0))
    return pl.pallas_call(
        body, name=name, grid=(s // tr,),
        in_specs=[pl.BlockSpec((tr, LANES), lambda i: (i, 4 * w // LANES)), vec, vec], out_specs=[row, row],
        out_shape=[jax.ShapeDtypeStruct((s, w), F32)] * 2, compiler_params=_cp("parallel"),
    )(projx, alog, dtb)


def _gdn_gates_bwd(projx, alog, dtb, dbeta, dg, dprojx, name):
    s = projx.shape[0]
    w = GDN_HEADS * GDN_DH
    tr = min(512, s)
    gate_blk = 4 * w // LANES

    def body(t_ref, al_ref, dt_ref, dbe_ref, dg_ref, _, o_ref, dal_ref, ddt_ref):
        @pl.when(pl.program_id(0) == 0)
        def _():
            dal_ref[...] = jnp.zeros_like(dal_ref)
            ddt_ref[...] = jnp.zeros_like(ddt_ref)

        t = t_ref[...]
        lane = lax.broadcasted_iota(jnp.int32, (tr, LANES), 1)
        lane1 = lax.broadcasted_iota(jnp.int32, (1, LANES), 1)
        out = jnp.zeros((tr, LANES), F32)
        dal = jnp.zeros((1, LANES), F32)
        ddt = jnp.zeros((1, LANES), F32)
        for h in range(GDN_HEADS):
            lo, hi = h * GDN_DH, (h + 1) * GDN_DH
            beta = _sigmoid(t[:, h:h + 1])
            pb = jnp.sum(dbe_ref[:, lo:hi], axis=-1, keepdims=True)
            db = pb * beta * (1.0 - beta)
            xa = t[:, GDN_HEADS + h:GDN_HEADS + h + 1] + dt_ref[:, lo:lo + 1]
            ea = -jnp.exp(al_ref[:, lo:lo + 1])
            pg = jnp.sum(dg_ref[:, lo:hi], axis=-1, keepdims=True)
            da = pg * ea * _sigmoid(xa)
            out = jnp.where(lane == h, db, jnp.where(lane == GDN_HEADS + h, da, out))
            dal = jnp.where(lane1 == h, jnp.sum(pg * ea * _softplus(xa), axis=0, keepdims=True), dal)
            ddt = jnp.where(lane1 == h, jnp.sum(da, axis=0, keepdims=True), ddt)
        o_ref[...] = out.astype(BF16)
        dal_ref[...] += dal
        ddt_ref[...] += ddt

    row = pl.BlockSpec((tr, w), lambda i: (i, 0))
    vec = pl.BlockSpec((1, w), lambda i: (0, 0))
    small = pl.BlockSpec((1, LANES), lambda i: (0, 0))
    gates = pl.BlockSpec((tr, LANES), lambda i: (i, gate_blk))
    return pl.pallas_call(
        body, name=name, grid=(s // tr,),
        in_specs=[gates, vec, vec, row, row, pl.BlockSpec(memory_space=pl.ANY)],
        out_specs=[gates, small, small],
        out_shape=[jax.ShapeDtypeStruct(dprojx.shape, dprojx.dtype), jax.ShapeDtypeStruct((1, LANES), F32),
                   jax.ShapeDtypeStruct((1, LANES), F32)],
        input_output_aliases={5: 0}, compiler_params=_cp("arbitrary"),
    )(projx, alog, dtb, dbeta, dg, dprojx)


def _gdn_tri():
    c = GDN_CHUNK
    i = lax.broadcasted_iota(jnp.int32, (c, c), 0)
    j = lax.broadcasted_iota(jnp.int32, (c, c), 1)
    return ((i >= j).astype(F32), (i <= j).astype(F32), i >= j, i > j, (i == j).astype(F32))


def _bdot(a, b, ca=2, cb=1, precision=None):
    return lax.dot_general(a, b, (((ca,), (cb,)), ((0,), (0,))), precision=precision, preferred_element_type=F32)


def _bmxu(a, b, ca=2, cb=1):
    return _bdot(a.astype(BF16), b.astype(BF16), ca, cb)


def _split(x):
    hi = x.astype(BF16)
    return hi, (x - hi.astype(F32)).astype(BF16)


def _bdot3(a, b, ca=2, cb=1):
    ah, al = _split(a)
    bh, bl = _split(b)
    return _bdot(ah, bh, ca, cb) + (_bdot(ah, bl, ca, cb) + _bdot(al, bh, ca, cb))


def _tri_dot(tri, x):
    t = tri.astype(BF16)
    hi = x.astype(BF16)
    r1 = x - hi.astype(F32)
    mid = r1.astype(BF16)
    lo = (r1 - mid.astype(F32)).astype(BF16)
    return _dot(t, hi) + (_dot(t, mid) + _dot(t, lo))


def _heads(x):
    return jnp.stack([x[:, h * GDN_DH:(h + 1) * GDN_DH] for h in range(GDN_HEADS)], axis=0)


def _unheads(x):
    return jnp.concatenate([x[h] for h in range(GDN_HEADS)], axis=1)


def _gdn_chunk(q, k, v, bb, g2d, tri, t=None):
    low, up, incl, strict, eye = tri
    c = GDN_CHUNK
    gc = _heads(_tri_dot(low, g2d))
    gci = gc[:, :, :c]
    gdiff = gci - jnp.swapaxes(gci, 1, 2)
    decay = jnp.where(incl, jnp.exp(jnp.where(incl, gdiff, 0.0)), 0.0)
    kb, vb = k * bb, v * bb
    kbk = _bmxu(kb, k, 2, 2)
    if t is None:
        x = -jnp.where(strict, kbk * decay, 0.0)
        t = eye + x
        p = x
        for _ in range(c.bit_length() - 2):
            p = _bdot3(p, p)
            t = t + _bdot3(t, p)
    eg = jnp.exp(gc)
    kbg = kb * eg
    gcl = gc[:, c - 1:c, :]
    ek = jnp.exp(gcl - gc)
    qkraw = _bmxu(q, k, 2, 2)
    return dict(decay=decay, kb=kb, vb=vb, kbk=kbk, t=t, eg=eg, kbg=kbg, ek=ek, gl=jnp.exp(gcl),
                w=_bmxu(t, kbg), u=_bmxu(t, vb), qkraw=qkraw, qk=jnp.where(incl, qkraw * decay, 0.0),
                qd=q * eg, kd=k * ek)


def _gdn_specs(n_of):
    c, w = GDN_CHUNK, GDN_HEADS * GDN_DH

    def blk(cb, width=w):
        return pl.BlockSpec((c, width), lambda n: (n_of(n), cb))

    st = pl.BlockSpec((None, GDN_HEADS, GDN_DH, GDN_DH), lambda n: (n_of(n), 0, 0, 0))
    vec = pl.BlockSpec((1, GDN_DH), lambda n: (0, 0))
    return blk, st, vec


def _gdn_load(qkv_ref, b_ref, g_ref, tri, t=None):
    w = GDN_HEADS * GDN_DH
    q, k, v = _heads(qkv_ref[:, :w]), _heads(qkv_ref[:, w:2 * w]), _heads(qkv_ref[:, 2 * w:])
    bb = _heads(b_ref[...])
    return q, k, v, bb, _gdn_chunk(q, k, v, bb, g_ref[...], tri, t)


def _gdn_fwd(qkv, beta, g, projx, onorm, name):
    s = qkv.shape[0]
    c = GDN_CHUNK
    nc = s // c
    w = GDN_HEADS * GDN_DH
    blk, st, vec = _gdn_specs(lambda n: n)
    inv = pl.BlockSpec((None, GDN_HEADS, c, c), lambda n: (n, 0, 0, 0))

    def body(qkv_ref, b_ref, g_ref, z_ref, on_ref, o_ref, st_ref, t_ref, state):
        @pl.when(pl.program_id(0) == 0)
        def _():
            state[...] = jnp.zeros_like(state)

        _, _, _, _, ch = _gdn_load(qkv_ref, b_ref, g_ref, _gdn_tri())
        t_ref[...] = ch["t"]
        sp = state[...]
        st_ref[...] = sp
        vn = ch["u"] - _bmxu(ch["w"], sp)
        o = _bmxu(ch["qd"], sp) + _bmxu(ch["qk"], vn)
        state[...] = sp * ch["gl"] + _bmxu(ch["kd"], vn, 1, 1)
        r = lax.rsqrt(jnp.mean(o * o, axis=-1, keepdims=True) + EPS)
        z = _heads(z_ref[...])
        o_ref[...] = _unheads(o * r * on_ref[...] * (z * _sigmoid(z))).astype(BF16)

    return pl.pallas_call(
        body, name=name, grid=(nc,),
        in_specs=[blk(0, 3 * w), blk(0), blk(0), blk(3), vec], out_specs=[blk(0), st, inv],
        out_shape=[jax.ShapeDtypeStruct((s, w), BF16), jax.ShapeDtypeStruct((nc, GDN_HEADS, GDN_DH, GDN_DH), F32),
                   jax.ShapeDtypeStruct((nc, GDN_HEADS, c, c), F32)],
        scratch_shapes=[pltpu.VMEM((GDN_HEADS, GDN_DH, GDN_DH), F32)],
        compiler_params=_cp("arbitrary"),
    )(qkv, beta, g, projx, onorm.reshape(1, -1))


def _gdn_bwd(qkv, beta, g, projx, onorm, states, tinv, dout, name):
    s = qkv.shape[0]
    c = GDN_CHUNK
    nc = s // c
    w = GDN_HEADS * GDN_DH
    blk, st, vec = _gdn_specs(lambda n: nc - 1 - n)
    inv = pl.BlockSpec((None, GDN_HEADS, c, c), lambda n: (nc - 1 - n, 0, 0, 0))

    def body(qkv_ref, b_ref, g_ref, z_ref, on_ref, st_ref, t_ref, do_ref,
             dqkv_ref, db_ref, dg_ref, dz_ref, don_ref, carry):
        @pl.when(pl.program_id(0) == 0)
        def _():
            carry[...] = jnp.zeros_like(carry)
            don_ref[...] = jnp.zeros_like(don_ref)

        tri = _gdn_tri()
        low, up, incl, strict, eye = tri
        q, k, v, bb, ch = _gdn_load(qkv_ref, b_ref, g_ref, tri, t_ref[...])
        sp = st_ref[...]
        vn = ch["u"] - _bmxu(ch["w"], sp)
        o = _bmxu(ch["qd"], sp) + _bmxu(ch["qk"], vn)
        r = lax.rsqrt(jnp.mean(o * o, axis=-1, keepdims=True) + EPS)
        orn = o * r
        z = _heads(z_ref[...])
        sg = _sigmoid(z)
        dout = _heads(do_ref[...])
        onw = on_ref[...]
        dz_ref[...] = _unheads(dout * orn * onw * (sg * (1.0 + z * (1.0 - sg)))).astype(BF16)
        don = dout * (z * sg)
        don_ref[...] += jnp.sum(jnp.sum(don * orn, axis=0), axis=0, keepdims=True)
        dor = don * onw
        do = r * (dor - orn * jnp.mean(dor * orn, axis=-1, keepdims=True))
        dsn = carry[...]
        dqd = _bmxu(do, sp, 2, 2)
        dqk = jnp.where(incl, _bmxu(do, vn, 2, 2), 0.0)
        dvn = _bmxu(ch["qk"], do, 1, 1) + _bmxu(ch["kd"], dsn)
        dkd = _bmxu(vn, dsn, 2, 2)
        dgl = jnp.sum(dsn * sp, axis=1, keepdims=True)
        dw = -_bmxu(dvn, sp, 2, 2)
        carry[...] = _bmxu(ch["qd"], do, 1, 1) + dsn * ch["gl"] - _bmxu(ch["w"], dvn, 1, 1)
        t = ch["t"]
        dvb = _bmxu(t, dvn, 1, 1)
        dkbg = _bmxu(t, dw, 1, 1)
        dt = _bmxu(dvn, ch["vb"], 2, 2) + _bmxu(dw, ch["kbg"], 2, 2)
        da = -_bdot3(_bdot3(t, dt, 1, 1), t, 2, 2)
        da = jnp.where(strict, da, 0.0)
        decay = ch["decay"]
        dkbk = da * decay
        dqkr = dqk * decay
        mdec = (da * ch["kbk"] + dqk * ch["qkraw"]) * decay
        dkb = _bmxu(dkbk, k) + dkbg * ch["eg"]
        dk = _bmxu(dkbk, ch["kb"], 1, 1) + _bmxu(dqkr, q, 1, 1) + dkd * ch["ek"] + dkb * bb
        dq = _bmxu(dqkr, k) + dqd * ch["eg"]
        tk = dkd * ch["kd"]
        dgcl = jnp.sum(tk, axis=1, keepdims=True) + dgl * ch["gl"]
        row = lax.broadcasted_iota(jnp.int32, (GDN_HEADS, c, GDN_DH), 1)
        zpad = jnp.zeros((GDN_HEADS, c, GDN_DH - c), F32)
        dgc = (jnp.concatenate([mdec, zpad], axis=2) - jnp.concatenate([jnp.swapaxes(mdec, 1, 2), zpad], axis=2)
               + dqd * ch["qd"] - tk + dkbg * ch["kbg"] + jnp.where(row == c - 1, dgcl, 0.0))
        dqkv_ref[:, :w] = _unheads(dq)
        dqkv_ref[:, w:2 * w] = _unheads(dk)
        dqkv_ref[:, 2 * w:] = _unheads(dvb * bb)
        db_ref[...] = _unheads(dkb * k + dvb * v)
        dg_ref[...] = _tri_dot(up, _unheads(dgc))

    return pl.pallas_call(
        body, name=name, grid=(nc,),
        in_specs=[blk(0, 3 * w), blk(0), blk(0), blk(3), vec, st, inv, blk(0)],
        out_specs=[blk(0, 3 * w), blk(0), blk(0), blk(3), vec],
        out_shape=[jax.ShapeDtypeStruct((s, 3 * w), F32), jax.ShapeDtypeStruct((s, w), F32),
                   jax.ShapeDtypeStruct((s, w), F32), jax.ShapeDtypeStruct(projx.shape, BF16),
                   jax.ShapeDtypeStruct((1, GDN_DH), F32)],
        scratch_shapes=[pltpu.VMEM((GDN_HEADS, GDN_DH, GDN_DH), F32)],
        compiler_params=_cp("arbitrary"),
    )(qkv, beta, g, projx, onorm.reshape(1, -1), states, tinv, dout)


_WEIGHTS = (
    "l0_mix_norm", "l0_w_in", "l0_ret_norm", "l0_s5_lambda_re", "l0_s5_lambda_im", "l0_s5_b_re", "l0_s5_b_im",
    "l0_s5_c_re", "l0_s5_c_im", "l0_s5_d", "l0_s5_log_dt", "l0_s5_w_glu", "l0_s5_b_glu", "l0_w_out",
    "l0_xa_norm", "l0_mem_norm", "l0_xa_wq", "l0_xa_wkv", "l0_xa_wo", "l0_ffn_norm", "l0_ffn_w_up",
    "l0_ffn_conv", "l0_ffn_w_down", "l1_mix_norm", "l1_w_in", "l1_conv", "l1_a_log", "l1_dt_bias", "l1_o_norm",
    "l1_w_out", "l1_xa_norm", "l1_mem_norm", "l1_xa_wq", "l1_xa_wkv", "l1_xa_wo", "l1_ffn_norm", "l1_ffn_w_up",
    "l1_ffn_conv", "l1_ffn_w_down", "final_norm")
_INPUTS = ("x", "mem") + _WEIGHTS + ("loss_target",) + tuple("m_" + n for n in _WEIGHTS) + tuple("v_" + n for n in _WEIGHTS)

_COL = ("l0_w_in", "l0_xa_wkv", "l0_ffn_w_up", "l0_ffn_conv", "l1_w_in", "l1_conv", "l1_xa_wkv", "l1_ffn_w_up",
        "l1_ffn_conv")
_ROW = ("l0_s5_w_glu", "l0_w_out", "l0_xa_wq", "l0_xa_wo", "l0_ffn_w_down", "l1_w_out", "l1_xa_wq", "l1_xa_wo",
        "l1_ffn_w_down")
_F32_WIRE = ("l0_ffn_conv", "l1_conv", "l1_ffn_conv")
_REP = tuple(n for n in _WEIGHTS if n not in _COL + _ROW)
_GATHER_GROUPS = (("l0_w_in", "l0_s5_w_glu", "l0_w_out"),
                  ("l0_xa_wq", "l0_xa_wkv", "l0_xa_wo", "l0_ffn_w_up", "l0_ffn_conv", "l0_ffn_w_down"),
                  ("l1_w_in", "l1_conv", "l1_w_out", "l1_xa_wq", "l1_xa_wkv", "l1_xa_wo"),
                  ("l1_ffn_w_up", "l1_ffn_conv", "l1_ffn_w_down"))


def _round_up(n, m):
    return (n + m - 1) // m * m


_REP_BIG = ("l0_s5_lambda_re", "l0_s5_lambda_im", "l0_s5_b_re", "l0_s5_b_im", "l0_s5_c_re", "l0_s5_c_im", "l0_s5_d")
_REP_LAST = "l0_mix_norm"
_REP_SMALL = tuple(n for n in _REP if n not in _REP_BIG + (_REP_LAST,))
PACK_WIDTH = 1024


def _pack_rows(ts):
    rows = [jnp.pad(t, ((0, 0), (0, PACK_WIDTH - t.shape[1]))) for t in ts]
    rows.append(jnp.zeros((_round_up(len(ts), 8) - len(ts), PACK_WIDTH), F32))
    return jnp.concatenate(rows, axis=0)


def _s5_interleave(re, im):
    lead = re.shape[:-1]
    nt = re.shape[-1] // S5_TILE
    both = jnp.stack([re.reshape(lead + (nt, S5_TILE)), im.reshape(lead + (nt, S5_TILE))], axis=-2)
    return both.reshape(lead + (2 * re.shape[-1],))


def _s5_split(x):
    lead = x.shape[:-1]
    y = x.reshape(lead + (x.shape[-1] // (2 * S5_TILE), 2, S5_TILE))
    return y[..., 0, :].reshape(lead + (-1,)), y[..., 1, :].reshape(lead + (-1,))


def _s5_discretise(lr, li, log_dt, b_re, b_im):
    dt = jnp.exp(log_dt)[:, None]
    mag = jnp.exp(lr * dt)
    a_re = mag * jnp.cos(li * dt)
    a_im = mag * jnp.sin(li * dt)
    den = lr * lr + li * li
    z_re = ((a_re - 1.0) * lr + a_im * li) / den
    z_im = (a_im * lr - (a_re - 1.0) * li) / den
    bb_re = z_re[:, None, :] * b_re - z_im[:, None, :] * b_im
    bb_im = z_re[:, None, :] * b_im + z_im[:, None, :] * b_re
    return a_re, a_im, bb_re, bb_im


def kernel(*args):
    p = dict(zip(_INPUTS, args, strict=True))
    x0, mem0, tgt = p["x"][0], p["mem"][0], p["loss_target"][0]
    s, d = x0.shape
    me = _slot(*_mesh_pos())
    grads = {}
    wire = {n: (F32 if n in _F32_WIRE else BF16) for n in _COL + _ROW}

    gather, pin, token = [], jnp.zeros((), F32), None
    for i, names in enumerate(_GATHER_GROUPS):
        zones = [_into_slot(p[n], wire[n], me, "place_" + n, pin=token) for n in names]
        handle, token = _push_start([], zones, f"gather{i}_start")
        gather.append(handle)
        pin = pin + token[0, 0]
    w = {}

    def gathered(i, after):
        for n, full in zip(_GATHER_GROUPS[i], _push_wait(gather[i], after, f"gather{i}_wait")):
            if n in _COL:
                full = full.transpose(1, 0, 2)
            w[n] = full.reshape(-1, full.shape[-1]) if n in _ROW else full.reshape(full.shape[0], -1)

    pending = []

    def exchange(names, gain, tag):
        slots = []
        for n in names:
            g = grads[n]
            if n in _COL:
                pieces = g if isinstance(g, tuple) else (g,)
                g = jnp.concatenate([t.reshape(t.shape[0], -1, p[n].shape[1]).transpose(1, 0, 2) for t in pieces], axis=0)
            else:
                g = g.reshape((N_DEV, -1) + g.shape[1:])
            slots.append(g.astype(wire[n]))
        handle, token = _push_start(slots, [], tag + "_start")
        pending.append((names, slots, handle, tag))
        return gain + token[0, 0]

    def xattn(pre, x_in, hx):
        q = _mm(hx, w[pre + "xa_wq"], out_dtype=BF16, name=pre + "xa_q")
        memn = _norm_fwd(mem0, p[pre + "mem_norm"], pre + "mem_norm_fwd")
        kv = _mm(memn, w[pre + "xa_wkv"], out_dtype=BF16, name=pre + "xa_kv")
        ao = _xattn_fwd(q, kv, pre + "xattn_fwd")
        x_out, hf = _mm(ao, w[pre + "xa_wo"], res=x_in, norm_gain=p[pre + "ffn_norm"], name=pre + "xa_o")
        return x_out, hf, (x_in, hx, q, memn, kv, ao)

    def xattn_bwd(pre, saved, dx_pair):
        x_in, hx, q, memn, kv, ao = saved
        dxo, dxb = dx_pair
        dao = _mm(dxb, w[pre + "xa_wo"], tb=True, name=pre + "xa_o_dx")
        grads[pre + "xa_wo"] = _mm(ao, dxb, ta=True, out_dtype=BF16, name=pre + "xa_o_dw")
        dq, dkv = _xattn_bwd(q, kv, dao, pre + "xattn_bwd")
        grads[pre + "xa_wq"] = _mm(hx, dq, ta=True, out_dtype=BF16, name=pre + "xa_q_dw")
        grads[pre + "xa_wkv"] = _mm(memn, dkv, ta=True, out_dtype=BF16, name=pre + "xa_kv_dw")
        dmemn = _mm(dkv, w[pre + "xa_wkv"], tb=True, name=pre + "xa_kv_dx")
        gain = exchange((pre + "xa_wo", pre + "xa_wq", pre + "xa_wkv"), p[pre + "xa_norm"], pre + "xa_grads")
        dx_in, grads[pre + "xa_norm"], dx_in_b = _mm_norm_bwd(dq, w[pre + "xa_wq"], x_in, gain, dxo, pre + "xa_q_dx")
        _, grads[pre + "mem_norm"] = _norm_bwd(mem0, p[pre + "mem_norm"], dmemn, jnp.zeros_like(mem0), pre + "mem_norm_bwd")
        return dx_in, dx_in_b

    def ffn(pre, x_in, hf, next_gain):
        up = _mm(hf, w[pre + "ffn_w_up"], out_dtype=BF16, name=pre + "ffn_up")
        act = _ffn_act_fwd(up, w[pre + "ffn_conv"], pre + "ffn_act_fwd")
        res = _mm(act, w[pre + "ffn_w_down"], res=x_in, norm_gain=next_gain, name=pre + "ffn_down")
        x_out, h_next = res if next_gain is not None else (res, None)
        return x_out, h_next, (x_in, hf, up, act)

    def ffn_bwd(pre, saved, dx_pair):
        x_in, hf, up, act = saved
        dxo, dxb = dx_pair
        dact = _mm(dxb, w[pre + "ffn_w_down"], tb=True, out_dtype=BF16, name=pre + "ffn_down_dx")
        grads[pre + "ffn_w_down"] = _mm(act, dxb, ta=True, out_dtype=BF16, name=pre + "ffn_down_dw")
        dpu, dpg, dcu, dcg = _ffn_act_bwd(up, w[pre + "ffn_conv"], dact, pre + "ffn_act_bwd")
        grads[pre + "ffn_conv"] = jnp.concatenate([dcu, dcg], axis=1)
        grads[pre + "ffn_w_up"] = (_mm(hf, dpu, ta=True, out_dtype=BF16, name=pre + "ffn_up_dw_u"),
                                   _mm(hf, dpg, ta=True, out_dtype=BF16, name=pre + "ffn_up_dw_g"))
        gain = exchange((pre + "ffn_w_down", pre + "ffn_w_up", pre + "ffn_conv"), p[pre + "ffn_norm"], pre + "ffn_grads")
        dx_in, grads[pre + "ffn_norm"], dx_in_b = _mm_norm_bwd([dpu, dpg], w[pre + "ffn_w_up"], x_in, gain, dxo, pre + "ffn_up_dx")
        return dx_in, dx_in_b

    cos, sin = _rope_tables(s)
    (a_re, a_im, bb_re, bb_im), disc_vjp = jax.vjp(
        _s5_discretise, p["l0_s5_lambda_re"], p["l0_s5_lambda_im"], p["l0_s5_log_dt"], p["l0_s5_b_re"], p["l0_s5_b_im"])
    apow, apow_rev = _s5_pow_tables(_s5_interleave(a_re.reshape(1, -1), a_im.reshape(1, -1)), "l0_s5_pow_tables")
    bbt = _s5_tile_b(bb_re, bb_im).astype(BF16)
    cct = _s5_tile_c(p["l0_s5_c_re"], p["l0_s5_c_im"]).astype(BF16)
    s5_d = p["l0_s5_d"].reshape(1, -1)
    b_glu = p["l0_s5_b_glu"].reshape(1, -1)

    h0 = _norm_fwd(x0, p["l0_mix_norm"] + pin, "l0_mix_norm_fwd")
    gathered(0, h0)
    proj = _mm(h0, w["l0_w_in"], name="l0_in")
    merged, ret_states = _ret_fwd(proj, cos, sin, p["l0_ret_norm"], "l0_ret_fwd")
    st, y, gy = _s5_fwd(proj, bbt, cct, apow, s5_d, "l0_s5_fwd")
    z = _mm(gy, w["l0_s5_w_glu"], name="l0_s5_glu_mm")
    merged = _s5_glu_fwd(y, z, b_glu, merged, "l0_s5_glu_fwd")
    x1, hx0 = _mm(merged, w["l0_w_out"], res=x0, norm_gain=p["l0_xa_norm"], name="l0_out")
    gathered(1, x1)
    x2, hf0, xa0 = xattn("l0_", x1, hx0)
    x3, h1, ff0 = ffn("l0_", x2, hf0, p["l1_mix_norm"])

    gathered(2, x3)
    w1 = w["l1_w_in"]
    wx = jnp.pad(w1, ((0, 0), (0, _round_up(w1.shape[1], LANES) - w1.shape[1])))
    alog_x = jnp.repeat(p["l1_a_log"], GDN_DH).reshape(1, -1)
    dtb_x = jnp.repeat(p["l1_dt_bias"], GDN_DH).reshape(1, -1)
    projx = _mm(h1, wx, name="l1_in")
    qkv = _gdn_conv_fwd(projx, w["l1_conv"], "l1_conv_fwd")
    beta, glog = _gdn_gates_fwd(projx, alog_x, dtb_x, "l1_gates_fwd")
    o_gdn, gdn_states, gdn_tinv = _gdn_fwd(qkv, beta, glog, projx, p["l1_o_norm"], "l1_gdn_fwd")
    x4, hx1 = _mm(o_gdn, w["l1_w_out"], res=x3, norm_gain=p["l1_xa_norm"], name="l1_out")
    x5, hf1, xa1 = xattn("l1_", x4, hx1)
    gathered(3, x5)
    x6, _, ff1 = ffn("l1_", x5, hf1, None)

    loss_part, dx6, grads["final_norm"], dx6b = _loss_head(x6, p["final_norm"], tgt, "loss_head")
    loss = lax.psum(loss_part[0, 0], ("x", "y", "c"))
    dx5 = ffn_bwd("l1_", ff1, (dx6, dx6b))
    dx4, dx4b = xattn_bwd("l1_", xa1, dx5)

    do_gdn = _mm(dx4b, w["l1_w_out"], tb=True, name="l1_out_dx")
    grads["l1_w_out"] = _mm(o_gdn, dx4b, ta=True, out_dtype=BF16, name="l1_out_dw")
    dqkv, dbeta, dglog, dprojx, grads["l1_o_norm"] = _gdn_bwd(
        qkv, beta, glog, projx, p["l1_o_norm"], gdn_states, gdn_tinv, do_gdn, "l1_gdn_bwd")
    dprojx, grads["l1_conv"] = _gdn_conv_bwd(projx, w["l1_conv"], dqkv, dprojx, "l1_conv_bwd")
    dprojx, dalog_x, ddtb_x = _gdn_gates_bwd(projx, alog_x, dtb_x, dbeta, dglog, dprojx, "l1_gates_bwd")
    grads["l1_w_in"] = _mm(h1, dprojx, ta=True, out_dtype=BF16, name="l1_in_dw")[:, :w1.shape[1]]
    grads["l1_a_log"] = dalog_x[0, :GDN_HEADS]
    grads["l1_dt_bias"] = ddtb_x[0, :GDN_HEADS]
    gain = exchange(("l1_w_out", "l1_w_in", "l1_conv"), p["l1_mix_norm"], "l1_mix_grads")
    dx3, grads["l1_mix_norm"], dx3b = _mm_norm_bwd(dprojx, wx, x3, gain, dx4, "l1_in_dx")

    dx2 = ffn_bwd("l0_", ff0, (dx3, dx3b))
    dx1, dx1b = xattn_bwd("l0_", xa0, dx2)

    dmerged = _mm(dx1b, w["l0_w_out"], tb=True, name="l0_out_dx")
    grads["l0_w_out"] = _mm(merged, dx1b, ta=True, out_dtype=BF16, name="l0_out_dw")
    dproj, grads["l0_ret_norm"] = _ret_bwd(proj, cos, sin, p["l0_ret_norm"], ret_states, dmerged, "l0_ret_bwd")
    dzg, dg1, grads["l0_s5_b_glu"] = _s5_glu_bwd(dmerged, y, z, b_glu, "l0_s5_glu_bwd")
    grads["l0_s5_w_glu"] = _mm(gy, dzg, ta=True, out_dtype=BF16, name="l0_s5_glu_dw")
    s5_d_after = exchange(("l0_w_out", "l0_s5_w_glu"), s5_d, "l0_out_grads")
    dg2 = _mm(dzg, w["l0_s5_w_glu"], tb=True, name="l0_s5_glu_dx")
    dproj, da_s5, dbbt, dcct, grads["l0_s5_d"] = _s5_bwd(dg1, dg2, y, proj, st, bbt, cct, apow_rev, s5_d_after, dproj, "l0_s5_bwd")
    dbb_re, dbb_im = _s5_untile_b(dbbt)
    grads["l0_s5_c_re"], grads["l0_s5_c_im"] = _s5_untile_c(dcct)
    da_re, da_im = (t.reshape(S5_GROUPS, S5_STATE) for t in _s5_split(da_s5[0]))
    (grads["l0_s5_lambda_re"], grads["l0_s5_lambda_im"], grads["l0_s5_log_dt"], grads["l0_s5_b_re"],
     grads["l0_s5_b_im"]) = disc_vjp((da_re, da_im, dbb_re, dbb_im))

    def as_2d(t):
        return t.reshape(-1, t.shape[-1])

    def as_row(t):
        return t.reshape(1, -1)

    small_own = _pack_rows([as_row(grads[n]) for n in _REP_SMALL])
    big_own = [as_2d(grads[n].reshape(p[n].shape)) for n in _REP_BIG]
    rep_zones = [_into_slot(small_own, F32, me, "place_rep0")]
    rep_zones += [_into_slot(t.reshape(-1, LANES), BF16, me, f"place_rep{i + 1}") for i, t in enumerate(big_own)]
    rep_handle, rep_token = _push_start([], rep_zones, "rep_grads_start")

    grads["l0_w_in"] = _mm(h0, dproj, ta=True, out_dtype=BF16, pin=rep_token, name="l0_in_dw")
    gain = exchange(("l0_w_in",), p["l0_mix_norm"], "l0_mix_grads")
    dx0, grads["l0_mix_norm"], _ = _mm_norm_bwd(dproj, w["l0_w_in"], x0, gain, dx1, "l0_in_dx")

    last_own = _pack_rows([as_row(grads[_REP_LAST])])
    last_handle, _ = _push_start([], [_into_slot(last_own, F32, me, "place_rep_last")], "rep_last_start")
    last_land, = _push_wait(last_handle, dx0, "rep_last_wait")
    rep_lands = _push_wait(rep_handle, last_land, "rep_grads_wait")
    rep_land = rep_lands[0]

    outs = {}
    kinds = ("grad_", "delta_", "new_m_", "new_v_")
    for names, slots, handle, tag in pending:
        for n, own_slots, land in zip(names, slots, _push_wait(handle, rep_land, tag + "_wait")):
            shape = p[n].shape
            own = lax.dynamic_index_in_dim(own_slots, me, 0, keepdims=False)
            res = _adamw(land, own, *(p[pre + n].reshape(own.shape) for pre in ("", "m_", "v_")), "adamw_" + n)
            for kind, t in zip(kinds, res):
                outs[kind + n] = t.reshape(shape)
    for n, own, land in zip(_REP_BIG, big_own, rep_lands[1:]):
        res = _adamw(land.reshape((N_DEV,) + own.shape), None, *(as_2d(p[pre + n]) for pre in ("", "m_", "v_")), "adamw_" + n)
        for kind, t in zip(kinds, res):
            outs[kind + n] = t.reshape(p[n].shape)
    for names, land, own, nm in ((_REP_SMALL, rep_land, small_own, "adamw_small"), ((_REP_LAST,), last_land, last_own, "adamw_last")):
        res = _adamw_rows(land, own, *([as_row(p[pre + n]) for n in names] for pre in ("", "m_", "v_")), nm)
        for j, kind in enumerate(kinds):
            for i, n in enumerate(names):
                outs[kind + n] = res[j * len(names) + i].reshape(p[n].shape)

    return (loss, dx0[None]) + tuple(outs[kind + n] for kind in kinds for n in _WEIGHTS)
```

```python
import math

import numpy as np
import jax
import jax.numpy as jnp
from jax import lax
from jax.experimental import pallas as pl
from jax.experimental.pallas import tpu as pltpu

F32 = jnp.float32
BF16 = jnp.bfloat16
EPS = 1e-6
N_DEV = 8
LANES = 128
VMEM_LIMIT = 48 * 1024 * 1024

RET_HEADS, RET_DH, RET_CHUNK = 4, 128, 128
S5_GROUPS, S5_GROUP, S5_STATE = 32, 16, 64
GDN_HEADS, GDN_DH, GDN_CHUNK, GDN_CONV = 8, 128, 64, 4
XA_HEADS, XA_DH = 4, 256
FFN_CONV = 3
SCAN_ROWS = 256

ADAM_LR, ADAM_B1, ADAM_B2, ADAM_EPS, ADAM_WD, ADAM_STEP = 0.001, 0.9, 0.999, 1e-08, 0.01, 10


def _cp(*sem):
    return pltpu.CompilerParams(dimension_semantics=sem if sem else None, vmem_limit_bytes=VMEM_LIMIT)


def _tile(n, cap):
    if n <= cap:
        return n
    best = None
    for t in range(LANES, cap + 1, LANES):
        if n % t == 0:
            best = t
    assert best is not None, n
    return best


def _dot(a, b, ca=1, cb=0, precision=None):
    return lax.dot_general(a, b, (((ca,), (cb,)), ((), ())), precision=precision, preferred_element_type=F32)


def _mxu(a, b, ca=1, cb=0):
    return _dot(a.astype(BF16), b.astype(BF16), ca, cb)


def _sigmoid(x):
    return 0.5 * jnp.tanh(0.5 * x) + 0.5


def _shift_down(x, k):
    r = pltpu.roll(x, k, 0)
    row = lax.broadcasted_iota(jnp.int32, (8,) + x.shape[1:], 0)
    return jnp.concatenate([jnp.where(row >= k, r[:8], 0.0), r[8:]], axis=0)


def _shift_up(x, k):
    n = x.shape[0]
    r = pltpu.roll(x, n - k, 0)
    row = lax.broadcasted_iota(jnp.int32, (8,) + x.shape[1:], 0)
    return jnp.concatenate([r[:n - 8], jnp.where(row < 8 - k, r[n - 8:], 0.0)], axis=0)


def _mesh_pos():
    return lax.axis_index("x"), lax.axis_index("y"), lax.axis_index("c")


def _slot(px, py, pc):
    return 4 * px + 2 * py + pc


def _all_peers(x, y, c):
    flips = [(fx, fy, fc) for fx in (0, 1) for fy in (0, 1) for fc in (0, 1)][1:]
    return [(1 - x if fx else x, 1 - y if fy else y, 1 - c if fc else c) for fx, fy, fc in flips]


_HBM = pl.BlockSpec(memory_space=pltpu.HBM)
_SEM = pl.BlockSpec(memory_space=pltpu.SEMAPHORE)
N_PEERS = N_DEV - 1


def _push_copies(srcs, lands, send_sems, recv_sems, start):
    x, y, c = _mesh_pos()
    me = _slot(x, y, c)
    out = []
    for k, to in enumerate(_all_peers(x, y, c)):
        for a in range(len(lands)):
            src = srcs[a].at[_slot(*to)] if a < len(srcs) else lands[a].at[me]
            dst = lands[a].at[me if start else _slot(*to)]
            out.append(pltpu.make_async_remote_copy(
                src_ref=src, dst_ref=dst, send_sem=send_sems.at[a * N_PEERS + k], recv_sem=recv_sems.at[a * N_PEERS + k],
                device_id=to, device_id_type=pl.DeviceIdType.MESH))
    return out


def _into_slot(x, dtype, me, name):
    r, c = x.shape
    cap = max(16, 512 * 1024 // c)
    tr = max(t for t in range(16, min(r, cap) + 1, 16) if r % t == 0) if r % 16 == 0 else r

    def body(me_ref, x_ref, o_ref):
        o_ref[...] = x_ref[...].astype(dtype)

    return pl.pallas_call(
        body, name=name, out_shape=jax.ShapeDtypeStruct((N_DEV, r, c), dtype),
        grid_spec=pltpu.PrefetchScalarGridSpec(
            num_scalar_prefetch=1, grid=(r // tr,),
            in_specs=[pl.BlockSpec((tr, c), lambda i, me_ref: (i, 0))],
            out_specs=pl.BlockSpec((None, tr, c), lambda i, me_ref: (me_ref[0], i, 0))),
        compiler_params=_cp("parallel"),
    )(me.reshape(1).astype(jnp.int32), x)


def _push_start(scatter, gather_lands, name):
    ns, n = len(scatter), len(scatter) + len(gather_lands)
    lands = [lax.empty(a.shape, a.dtype) for a in scatter] + list(gather_lands)

    def body(*refs):
        srcs, zones = refs[:ns], refs[ns:ns + n]
        for cp in _push_copies(srcs, zones, refs[ns + n], refs[ns + n + 1], True):
            cp.start()
        refs[-1][...] = jnp.zeros((8, LANES), F32)

    hbm_in = [pltpu.with_memory_space_constraint(a, pltpu.HBM) for a in list(scatter) + lands]
    res = pl.pallas_call(
        body, name=name,
        out_shape=(pltpu.SemaphoreType.DMA((n * N_PEERS,)), pltpu.SemaphoreType.DMA((n * N_PEERS,)))
        + tuple(pltpu.HBM(a.shape, a.dtype) for a in list(scatter) + lands)
        + (jax.ShapeDtypeStruct((8, LANES), F32),),
        in_specs=[_HBM] * (ns + n),
        out_specs=(_SEM, _SEM) + (_HBM,) * (ns + n) + (pl.BlockSpec(memory_space=pltpu.VMEM),),
        input_output_aliases={i: 2 + i for i in range(ns + n)},
        compiler_params=pltpu.CompilerParams(has_side_effects=pltpu.SideEffectType.DATAFLOW_SIDE_EFFECTING),
    )(*hbm_in)
    return (res[0], res[1], res[2:2 + ns], res[2 + ns:2 + ns + n]), res[-1]


def _push_wait(handle, after, name):
    send_sems, recv_sems, srcs, lands = handle
    ns, n = len(srcs), len(lands)

    def body(*refs):
        for cp in _push_copies(refs[:ns], refs[ns:ns + n], refs[ns + n], refs[ns + n + 1], False):
            cp.wait_send()
            cp.wait_recv()

    res = pl.pallas_call(
        body, name=name,
        out_shape=tuple(pltpu.HBM(a.shape, a.dtype) for a in list(srcs) + list(lands)),
        in_specs=[_HBM] * (ns + n) + [_SEM, _SEM, pl.BlockSpec(memory_space=pl.ANY)],
        out_specs=(_HBM,) * (ns + n),
        input_output_aliases={i: i for i in range(ns + n)},
        compiler_params=pltpu.CompilerParams(has_side_effects=pltpu.SideEffectType.DATAFLOW_SIDE_EFFECTING),
    )(*srcs, *lands, send_sems, recv_sems, after)
    return res[ns:]


def _mm(a, b, *, ta=False, tb=False, out_dtype=F32, res=None, pin=None, norm_gain=None, name="mm"):
    m, k = (a.shape[1], a.shape[0]) if ta else a.shape
    n = b.shape[0] if tb else b.shape[1]
    assert k == (b.shape[1] if tb else b.shape[0]), (a.shape, b.shape, ta, tb)
    has_res = res is not None
    has_norm = norm_gain is not None
    plain = not (ta or has_res or has_norm)
    tm, tn, tk = _tile(m, 2048 if plain else 1408), _tile(n, 1536), _tile(k, 1408)
    nk = k // tk
    assert not has_norm or tn == n
    n_in = 2 + has_res + (pin is not None) + has_norm

    def body(*refs):
        a_ref, b_ref = refs[:2]
        r_ref = refs[2] if has_res else None
        o_ref = refs[n_in]
        part = _mxu(a_ref[...], b_ref[...], 0 if ta else 1, 1 if tb else 0)

        def finish(r):
            if has_res:
                r = r + r_ref[...].astype(F32)
            o_ref[...] = r.astype(out_dtype)
            if has_norm:
                scale = lax.rsqrt(jnp.mean(r * r, axis=-1, keepdims=True) + EPS)
                refs[n_in + 1][...] = (r * scale * refs[n_in - 1][...]).astype(BF16)

        if nk == 1:
            finish(part)
            return
        acc = refs[-1]
        kk = pl.program_id(2)

        @pl.when(kk == 0)
        def _():
            acc[...] = part

        @pl.when(kk > 0)
        def _():
            acc[...] += part

        @pl.when(kk == nk - 1)
        def _():
            finish(acc[...])

    a_spec = pl.BlockSpec((tk, tm), lambda i, j, kk: (kk, i)) if ta else pl.BlockSpec((tm, tk), lambda i, j, kk: (i, kk))
    b_spec = pl.BlockSpec((tn, tk), lambda i, j, kk: (j, kk)) if tb else pl.BlockSpec((tk, tn), lambda i, j, kk: (kk, j))
    o_spec = pl.BlockSpec((tm, tn), lambda i, j, kk: (i, j))
    in_specs = [a_spec, b_spec] + ([o_spec] if has_res else [])
    args = (a, b) + ((res,) if has_res else ())
    if pin is not None:
        in_specs.append(pl.BlockSpec(pin.shape, lambda i, j, kk: (0, 0)))
        args += (pin,)
    if has_norm:
        in_specs.append(pl.BlockSpec((1, n), lambda i, j, kk: (0, 0)))
        args += (norm_gain.reshape(1, n),)
    out = jax.ShapeDtypeStruct((m, n), out_dtype)
    return pl.pallas_call(
        body, name=name, grid=(m // tm, n // tn, nk), in_specs=in_specs,
        out_specs=[o_spec, o_spec] if has_norm else o_spec,
        out_shape=[out, jax.ShapeDtypeStruct((m, n), BF16)] if has_norm else out,
        scratch_shapes=[pltpu.VMEM((tm, tn), F32)] if nk > 1 else [],
        compiler_params=_cp("parallel", "parallel", "arbitrary"),
    )(*args)


def _mm_norm_bwd(dy, w, x, g, dres, name, pin=None):
    dys = list(dy) if isinstance(dy, (list, tuple)) else [dy]
    nq = len(dys)
    s, kq = dys[0].shape
    d = w.shape[0]
    tm, tk = min(1024 if nq == 1 else 512, s), _tile(kq, 1408)
    per = kq // tk
    nk = nq * per
    n_in = nq + 4 + (pin is not None)

    def body(*refs):
        w_ref, x_ref, g_ref, dres_ref = refs[nq:nq + 4]
        dx_ref, dg_ref = refs[n_in], refs[n_in + 1]
        i, kk = pl.program_id(0), pl.program_id(1)

        @pl.when((i == 0) & (kk == 0))
        def _():
            dg_ref[...] = jnp.zeros_like(dg_ref)

        def finish(dh):
            xv = x_ref[...]
            r = lax.rsqrt(jnp.mean(xv * xv, axis=-1, keepdims=True) + EPS)
            xn = xv * r
            dg_ref[...] += jnp.sum(dh * xn, axis=0, keepdims=True)
            dhg = dh * g_ref[...]
            dx_ref[...] = dres_ref[...] + r * (dhg - xn * jnp.mean(dhg * xn, axis=-1, keepdims=True))

        if nk == 1:
            finish(_mxu(refs[0][...], w_ref[...], 1, 1))
            return
        acc = refs[-1]
        for q in range(nq):
            @pl.when((kk >= q * per) & (kk < (q + 1) * per))
            def _(q=q):
                part = _mxu(refs[q][...], w_ref[...], 1, 1)

                @pl.when(kk == 0)
                def _():
                    acc[...] = part

                @pl.when(kk > 0)
                def _():
                    acc[...] += part

        @pl.when(kk == nk - 1)
        def _():
            finish(acc[...])

    row = pl.BlockSpec((tm, d), lambda i, kk: (i, 0))
    vec = pl.BlockSpec((1, d), lambda i, kk: (0, 0))
    in_specs = [pl.BlockSpec((tm, tk), lambda i, kk, q=q: (i, jnp.clip(kk - q * per, 0, per - 1))) for q in range(nq)]
    in_specs += [pl.BlockSpec((d, tk), lambda i, kk: (0, kk)), row, vec, row]
    args = (*dys, w, x, g.reshape(1, d), dres)
    if pin is not None:
        in_specs.append(pl.BlockSpec(pin.shape, lambda i, kk: (0, 0)))
        args += (pin,)
    return pl.pallas_call(
        body, name=name, grid=(s // tm, nk), in_specs=in_specs, out_specs=[row, vec],
        out_shape=[jax.ShapeDtypeStruct((s, d), F32), jax.ShapeDtypeStruct((1, d), F32)],
        scratch_shapes=[pltpu.VMEM((tm, d), F32)] if nk > 1 else [],
        compiler_params=_cp("arbitrary", "arbitrary"),
    )(*args)


def _norm_fwd(x, g, name):
    s, d = x.shape
    tr = min(512, s)

    def body(x_ref, g_ref, o_ref):
        xv = x_ref[...]
        r = lax.rsqrt(jnp.mean(xv * xv, axis=-1, keepdims=True) + EPS)
        o_ref[...] = (xv * r * g_ref[...]).astype(BF16)

    row = pl.BlockSpec((tr, d), lambda i: (i, 0))
    return pl.pallas_call(
        body, name=name, grid=(s // tr,), in_specs=[row, pl.BlockSpec((1, d), lambda i: (0, 0))],
        out_specs=row, out_shape=jax.ShapeDtypeStruct((s, d), BF16), compiler_params=_cp("parallel"),
    )(x, g.reshape(1, d))


def _norm_bwd(x, g, dh, dres, name):
    s, d = x.shape
    tr = min(512, s)

    def body(x_ref, g_ref, dh_ref, dres_ref, dx_ref, dg_ref):
        @pl.when(pl.program_id(0) == 0)
        def _():
            dg_ref[...] = jnp.zeros_like(dg_ref)

        xv = x_ref[...]
        r = lax.rsqrt(jnp.mean(xv * xv, axis=-1, keepdims=True) + EPS)
        xn = xv * r
        dhv = dh_ref[...].astype(F32)
        dg_ref[...] += jnp.sum(dhv * xn, axis=0, keepdims=True)
        dhg = dhv * g_ref[...]
        dx_ref[...] = dres_ref[...] + r * (dhg - xn * jnp.mean(dhg * xn, axis=-1, keepdims=True))

    row = pl.BlockSpec((tr, d), lambda i: (i, 0))
    vec = pl.BlockSpec((1, d), lambda i: (0, 0))
    return pl.pallas_call(
        body, name=name, grid=(s // tr,), in_specs=[row, vec, row, row], out_specs=[row, vec],
        out_shape=[jax.ShapeDtypeStruct((s, d), F32), jax.ShapeDtypeStruct((1, d), F32)],
        compiler_params=_cp("arbitrary"),
    )(x, g.reshape(1, d), dh, dres)


def _loss_head(x, g, tgt, name):
    s, d = x.shape
    tr = min(512, s)

    def body(x_ref, g_ref, t_ref, l_ref, dx_ref, dg_ref):
        @pl.when(pl.program_id(0) == 0)
        def _():
            dg_ref[...] = jnp.zeros_like(dg_ref)
            l_ref[...] = jnp.zeros_like(l_ref)

        xv = x_ref[...]
        r = lax.rsqrt(jnp.mean(xv * xv, axis=-1, keepdims=True) + EPS)
        xn = xv * r
        err = xn * g_ref[...] - t_ref[...]
        part = 0.5 * jnp.sum(jnp.mean(err * err, axis=-1, keepdims=True), axis=0, keepdims=True)
        l_ref[...] += jnp.broadcast_to(part, l_ref.shape)
        dy = err * (1.0 / d)
        dg_ref[...] += jnp.sum(dy * xn, axis=0, keepdims=True)
        dyg = dy * g_ref[...]
        dx_ref[...] = r * (dyg - xn * jnp.mean(dyg * xn, axis=-1, keepdims=True))

    row = pl.BlockSpec((tr, d), lambda i: (i, 0))
    vec = pl.BlockSpec((1, d), lambda i: (0, 0))
    return pl.pallas_call(
        body, name=name, grid=(s // tr,), in_specs=[row, vec, row],
        out_specs=[pl.BlockSpec((1, LANES), lambda i: (0, 0)), row, vec],
        out_shape=[jax.ShapeDtypeStruct((1, LANES), F32), jax.ShapeDtypeStruct((s, d), F32),
                   jax.ShapeDtypeStruct((1, d), F32)],
        compiler_params=_cp("arbitrary"),
    )(x, g.reshape(1, d), tgt)


def _sum_slots(landed_slot, own):
    me = _slot(*_mesh_pos())
    mine = own.astype(F32)
    g = jnp.where(me == 0, mine, landed_slot(0).astype(F32))
    for i in range(1, N_DEV):
        g = g + jnp.where(me == i, mine, landed_slot(i).astype(F32))
    return g


def _adam_update(g, w, m, v):
    mm = ADAM_B1 * m + (1.0 - ADAM_B1) * g
    vv = ADAM_B2 * v + (1.0 - ADAM_B2) * (g * g)
    m_hat = mm / (1.0 - ADAM_B1 ** ADAM_STEP)
    v_hat = vv / (1.0 - ADAM_B2 ** ADAM_STEP)
    return g, -ADAM_LR * (m_hat / (jnp.sqrt(v_hat) + ADAM_EPS) + ADAM_WD * w), mm, vv


def _adamw_rows(landed, own, ws, ms, vs, name):
    k = len(ws)
    sizes = [w.shape[1] for w in ws]

    def body(*refs):
        p_ref, o_ref = refs[:2]
        w_refs, m_refs, v_refs = refs[2:2 + k], refs[2 + k:2 + 2 * k], refs[2 + 2 * k:2 + 3 * k]
        outs = refs[2 + 3 * k:]
        for i, n in enumerate(sizes):
            g = _sum_slots(lambda s: p_ref[s, i:i + 1, :n], o_ref[i:i + 1, :n])
            res = _adam_update(g, w_refs[i][...], m_refs[i][...], v_refs[i][...])
            for j in range(4):
                outs[j * k + i][...] = res[j]

    return pl.pallas_call(
        body, name=name, out_shape=[jax.ShapeDtypeStruct((1, n), F32) for _ in range(4) for n in sizes],
    )(landed, own, *ws, *ms, *vs)


def _adamw(landed, own, w, m, v, name):
    r, c = w.shape
    cap = max(8, 256 * 1024 // c)
    tr = max(t for t in range(8, min(r, cap) + 1, 8) if r % t == 0) if r % 8 == 0 else r
    gathered = own is None

    def body(*refs):
        p_ref = refs[0]
        w_ref, m_ref, v_ref, g_ref, d_ref, nm_ref, nv_ref = refs[1 if gathered else 2:]
        if gathered:
            g = p_ref[0].astype(F32)
            for i in range(1, N_DEV):
                g = g + p_ref[i].astype(F32)
        else:
            g = _sum_slots(lambda i: p_ref[i], refs[1][...])
        g_ref[...], d_ref[...], nm_ref[...], nv_ref[...] = _adam_update(g, w_ref[...], m_ref[...], v_ref[...])

    blk = pl.BlockSpec((tr, c), lambda i: (i, 0))
    n_blk = 3 if gathered else 4
    return pl.pallas_call(
        body, name=name, grid=(r // tr,),
        in_specs=[pl.BlockSpec((N_DEV, tr, c), lambda i: (0, i, 0))] + [blk] * n_blk,
        out_specs=[blk] * 4, out_shape=[jax.ShapeDtypeStruct((r, c), F32)] * 4,
        compiler_params=_cp("parallel"),
    )(*((landed,) if gathered else (landed, own)), w, m, v)


def _conv_taps(x, kw):
    return [_shift_down(x, kw - 1 - j) for j in range(kw - 1)] + [x]


def _conv_fwd(taps, w_ref):
    acc = w_ref[0:1, :] * taps[0]
    for j in range(1, len(taps)):
        acc = acc + w_ref[j:j + 1, :] * taps[j]
    return acc


def _conv_bwd(taps, dy, w_ref, dw_ref):
    kw = len(taps)
    dx = w_ref[kw - 1:kw, :] * dy
    for j in range(kw):
        dw_ref[j:j + 1, :] = jnp.sum(dy * taps[j], axis=0, keepdims=True)
        if j < kw - 1:
            dx = dx + w_ref[j:j + 1, :] * _shift_up(dy, kw - 1 - j)
    return dx


def _ffn_act_fwd(pre, cw, name):
    s, f2 = pre.shape
    nt = f2 // 2 // LANES

    def body(pu_ref, pg_ref, wu_ref, wg_ref, o_ref):
        up = _conv_fwd(_conv_taps(pu_ref[...].astype(F32), FFN_CONV), wu_ref)
        gate = _conv_fwd(_conv_taps(pg_ref[...].astype(F32), FFN_CONV), wg_ref)
        o_ref[...] = (gate * _sigmoid(gate) * up).astype(BF16)

    def col(rows, off):
        return pl.BlockSpec((rows, LANES), lambda j: (0, j + off))

    return pl.pallas_call(
        body, name=name, grid=(nt,),
        in_specs=[col(s, 0), col(s, nt), col(FFN_CONV, 0), col(FFN_CONV, nt)], out_specs=col(s, 0),
        out_shape=jax.ShapeDtypeStruct((s, f2 // 2), BF16), compiler_params=_cp("parallel"),
    )(pre, pre, cw, cw)


def _ffn_act_bwd(pre, cw, dact, name):
    s, f2 = pre.shape
    f = f2 // 2
    nt = f // LANES

    def body(pu_ref, pg_ref, wu_ref, wg_ref, da_ref, dpu_ref, dpg_ref, dwu_ref, dwg_ref):
        pu, pg = pu_ref[...].astype(F32), pg_ref[...].astype(F32)
        tu, tg = _conv_taps(pu, FFN_CONV), _conv_taps(pg, FFN_CONV)
        up = _conv_fwd(tu, wu_ref)
        gate = _conv_fwd(tg, wg_ref)
        sg = _sigmoid(gate)
        da = da_ref[...].astype(F32)
        dup = da * gate * sg
        dgate = da * up * (sg * (1.0 + gate * (1.0 - sg)))
        dpu_ref[...] = _conv_bwd(tu, dup, wu_ref, dwu_ref).astype(BF16)
        dpg_ref[...] = _conv_bwd(tg, dgate, wg_ref, dwg_ref).astype(BF16)

    def col(rows, off):
        return pl.BlockSpec((rows, LANES), lambda j: (0, j + off))

    return pl.pallas_call(
        body, name=name, grid=(nt,),
        in_specs=[col(s, 0), col(s, nt), col(FFN_CONV, 0), col(FFN_CONV, nt), col(s, 0)],
        out_specs=[col(s, 0), col(s, 0), col(FFN_CONV, 0), col(FFN_CONV, 0)],
        out_shape=[jax.ShapeDtypeStruct((s, f), BF16), jax.ShapeDtypeStruct((s, f), BF16),
                   jax.ShapeDtypeStruct((FFN_CONV, f), F32), jax.ShapeDtypeStruct((FFN_CONV, f), F32)],
        compiler_params=_cp("parallel"),
    )(pre, pre, cw, cw, dact)


def _xa_probs(qh, kh):
    sc = _mxu(qh, kh, 1, 1) * (XA_DH ** -0.5)
    e = jnp.exp(sc - jnp.max(sc, axis=-1, keepdims=True))
    return e / jnp.sum(e, axis=-1, keepdims=True)


def _xattn_fwd(q, kv, name):
    s, d = q.shape
    m = kv.shape[0]
    tr = min(512, s)

    def body(q_ref, kv_ref, o_ref):
        for h in range(XA_HEADS):
            lo, hi = h * XA_DH, (h + 1) * XA_DH
            p = _xa_probs(q_ref[:, lo:hi], kv_ref[:, lo:hi])
            o_ref[:, lo:hi] = _mxu(p, kv_ref[:, d + lo:d + hi]).astype(BF16)

    row = pl.BlockSpec((tr, d), lambda i: (i, 0))
    return pl.pallas_call(
        body, name=name, grid=(s // tr,), in_specs=[row, pl.BlockSpec((m, 2 * d), lambda i: (0, 0))],
        out_specs=row, out_shape=jax.ShapeDtypeStruct((s, d), BF16), compiler_params=_cp("parallel"),
    )(q, kv)


def _xattn_bwd(q, kv, do, name):
    s, d = q.shape
    m = kv.shape[0]
    tr = min(512, s)

    def body(q_ref, kv_ref, do_ref, dq_ref, dkv_ref):
        @pl.when(pl.program_id(0) == 0)
        def _():
            dkv_ref[...] = jnp.zeros_like(dkv_ref)

        for h in range(XA_HEADS):
            lo, hi = h * XA_DH, (h + 1) * XA_DH
            qh, kh, vh = q_ref[:, lo:hi], kv_ref[:, lo:hi], kv_ref[:, d + lo:d + hi]
            doh = do_ref[:, lo:hi]
            p = _xa_probs(qh, kh)
            dp = _mxu(doh, vh, 1, 1)
            ds = p * (dp - jnp.sum(p * dp, axis=-1, keepdims=True)) * (XA_DH ** -0.5)
            dq_ref[:, lo:hi] = _mxu(ds, kh).astype(BF16)
            dkv_ref[:, lo:hi] += _mxu(ds, qh, 0, 0)
            dkv_ref[:, d + lo:d + hi] += _mxu(p, doh, 0, 0)

    row = pl.BlockSpec((tr, d), lambda i: (i, 0))
    full = pl.BlockSpec((m, 2 * d), lambda i: (0, 0))
    return pl.pallas_call(
        body, name=name, grid=(s // tr,), in_specs=[row, full, row], out_specs=[row, full],
        out_shape=[jax.ShapeDtypeStruct((s, d), BF16), jax.ShapeDtypeStruct((m, 2 * d), F32)],
        compiler_params=_cp("arbitrary"),
    )(q, kv, do)


def _ret_tables():
    c = RET_CHUNK
    lg = np.log1p(-np.exp2(-5.0 - np.arange(RET_HEADS, dtype=np.float32))).astype(np.float32)
    idx = np.arange(c, dtype=np.float32)
    diff = idx[:, None] - idx[None, :]
    intra = np.where(diff >= 0, np.exp(lg[:, None, None] * np.where(diff >= 0, diff, 0.0)), 0.0)
    rk = np.broadcast_to(np.exp(lg[:, None] * (c - 1 - idx))[:, :, None], (RET_HEADS, c, LANES))
    rq = np.broadcast_to(np.exp(lg[:, None] * (idx + 1))[:, :, None], (RET_HEADS, c, LANES))
    return jnp.asarray(np.stack([intra, rk, rq], axis=1).astype(np.float32))


def _rope_tables(s):
    half = RET_DH // 2
    inv = jnp.exp(-math.log(10000.0) * jnp.arange(half, dtype=F32) / half)
    ang = jnp.arange(s, dtype=F32)[:, None] * inv[None, :]
    cos, sin = jnp.cos(ang), jnp.sin(ang)
    return jnp.concatenate([cos, cos], axis=1), jnp.concatenate([-sin, sin], axis=1)


def _ret_specs(n_of):
    c, w = RET_CHUNK, RET_HEADS * RET_DH

    def part(off):
        return pl.BlockSpec((c, w), lambda n: (n_of(n), off))

    pos = pl.BlockSpec((c, RET_DH), lambda n: (n_of(n), 0))
    gain = pl.BlockSpec((1, w), lambda n: (0, 0))
    tab = pl.BlockSpec((RET_HEADS, 3, c, LANES), lambda n: (0, 0, 0, 0))
    st = pl.BlockSpec((RET_HEADS, None, RET_DH, RET_DH), lambda n: (0, n_of(n), 0, 0))
    return part, pos, gain, tab, st


def _rheads(x):
    return jnp.stack([x[:, h * RET_DH:(h + 1) * RET_DH] for h in range(RET_HEADS)], axis=0)


def _runheads(x):
    return jnp.concatenate([x[h] for h in range(RET_HEADS)], axis=1)


def _rope(x, cos, sin):
    return x * cos + pltpu.roll(x, RET_DH // 2, 2) * sin


def _ret_chunk(q_ref, k_ref, v_ref, cos_ref, sin_ref, tab_ref, prev):
    cos, sin = cos_ref[...], sin_ref[...]
    q = _rope(_rheads(q_ref[...]), cos, sin)
    k = _rope(_rheads(k_ref[...]), cos, sin) * (RET_DH ** -0.5)
    v = _rheads(v_ref[...])
    scores = _bmxu(q, k, 2, 2) * tab_ref[:, 0]
    qdec = q * tab_ref[:, 2]
    kdec = k * tab_ref[:, 1]
    o = _bmxu(scores, v) + _bmxu(qdec, prev)
    return q, k, v, scores, qdec, kdec, o


def _ret_fwd(proj, cos, sin, gain, name):
    s = proj.shape[0]
    c = RET_CHUNK
    nc = s // c
    part, pos, gvec, tab, st = _ret_specs(lambda n: n)

    def body(q_ref, k_ref, v_ref, g_ref, cos_ref, sin_ref, rn_ref, tab_ref, o_ref, st_ref, state):
        @pl.when(pl.program_id(0) == 0)
        def _():
            state[...] = jnp.zeros_like(state)

        prev = state[...]
        st_ref[...] = prev
        _, _, v, _, _, kdec, o = _ret_chunk(q_ref, k_ref, v_ref, cos_ref, sin_ref, tab_ref, prev)
        state[...] = prev * tab_ref[:, 2, c - 1:c, :] + _bmxu(kdec, v, 1, 1)
        r = lax.rsqrt(jnp.mean(o * o, axis=-1, keepdims=True) + EPS)
        gate = g_ref[...]
        o_ref[...] = (_runheads(o * r) * rn_ref[...] * (gate * _sigmoid(gate))).astype(BF16)

    return pl.pallas_call(
        body, name=name, grid=(nc,),
        in_specs=[part(0), part(1), part(2), part(3), pos, pos, gvec, tab],
        out_specs=[part(0), st],
        out_shape=[jax.ShapeDtypeStruct((s, 2 * RET_HEADS * RET_DH), BF16),
                   jax.ShapeDtypeStruct((RET_HEADS, nc, RET_DH, RET_DH), F32)],
        scratch_shapes=[pltpu.VMEM((RET_HEADS, RET_DH, RET_DH), F32)],
        compiler_params=_cp("arbitrary"),
    )(proj, proj, proj, proj, cos, sin, gain.reshape(1, -1), _ret_tables())


def _ret_bwd(proj, cos, sin, gain, states, dmerged, name):
    s = proj.shape[0]
    c = RET_CHUNK
    nc = s // c
    width = RET_HEADS * RET_DH
    part, pos, gvec, tab, st = _ret_specs(lambda n: nc - 1 - n)

    def body(q_ref, k_ref, v_ref, g_ref, cos_ref, sin_ref, rn_ref, tab_ref, st_ref, do_ref,
             dp_ref, drn_ref, carry):
        @pl.when(pl.program_id(0) == 0)
        def _():
            carry[...] = jnp.zeros_like(carry)
            drn_ref[...] = jnp.zeros_like(drn_ref)

        prev = st_ref[...]
        q, k, v, scores, qdec, kdec, o = _ret_chunk(q_ref, k_ref, v_ref, cos_ref, sin_ref, tab_ref, prev)
        r = lax.rsqrt(jnp.mean(o * o, axis=-1, keepdims=True) + EPS)
        on = o * r
        on2 = _runheads(on)
        gate = g_ref[...]
        sg = _sigmoid(gate)
        sil = gate * sg
        dout = do_ref[...]
        rn = rn_ref[...]
        dp_ref[:, 3 * width:] = (dout * on2 * rn * (sg * (1.0 + gate * (1.0 - sg)))).astype(BF16)
        drn_ref[...] += jnp.sum(dout * on2 * sil, axis=0, keepdims=True)
        don = _rheads(dout * rn * sil)
        do = r * (don - on * jnp.mean(don * on, axis=-1, keepdims=True))
        dc = carry[...]
        dsc = _bmxu(do, v, 2, 2) * tab_ref[:, 0]
        dq = _bmxu(dsc, k) + _bmxu(do, prev, 2, 2) * tab_ref[:, 2]
        dk = _bmxu(dsc, q, 1, 1) + _bmxu(v, dc, 2, 2) * tab_ref[:, 1]
        dv = _bmxu(scores, do, 1, 1) + _bmxu(kdec, dc)
        carry[...] = _bmxu(qdec, do, 1, 1) + dc * tab_ref[:, 2, c - 1:c, :]
        cos, sin = cos_ref[...], sin_ref[...]
        dk = dk * (RET_DH ** -0.5)
        dp_ref[:, :width] = _runheads(dq * cos + pltpu.roll(dq * sin, RET_DH // 2, 2)).astype(BF16)
        dp_ref[:, width:2 * width] = _runheads(dk * cos + pltpu.roll(dk * sin, RET_DH // 2, 2)).astype(BF16)
        dp_ref[:, 2 * width:3 * width] = _runheads(dv).astype(BF16)

    return pl.pallas_call(
        body, name=name, grid=(nc,),
        in_specs=[part(0), part(1), part(2), part(3), pos, pos, gvec, tab, st, part(0)],
        out_specs=[pl.BlockSpec((c, 4 * width), lambda n: (nc - 1 - n, 0)), gvec],
        out_shape=[jax.ShapeDtypeStruct(proj.shape, BF16), jax.ShapeDtypeStruct((1, width), F32)],
        scratch_shapes=[pltpu.VMEM((RET_HEADS, RET_DH, RET_DH), F32)],
        compiler_params=_cp("arbitrary"),
    )(proj, proj, proj, proj, cos, sin, gain.reshape(1, -1), _ret_tables(), states, dmerged)


S5_TILE = 512


def _cmul_add(xr, xi, ar, ai, yr, yi):
    return xr + ar * yr - ai * yi, xi + ar * yi + ai * yr


def _s5_pow_tables(a_il, name):
    r = SCAN_ROWS
    t = S5_TILE
    w2 = a_il.shape[1]

    def body(a_ref, up_ref, dn_ref):
        for j in range(w2 // (2 * t)):
            re, im = pl.ds(2 * t * j, t), pl.ds(2 * t * j + t, t)
            up_ref[0:1, re] = a_ref[:, re]
            up_ref[0:1, im] = a_ref[:, im]
            dn_ref[r - 1:r, re] = a_ref[:, re]
            dn_ref[r - 1:r, im] = -a_ref[:, im]
            n = 1
            while n < r:
                lr, li = up_ref[n - 1:n, re], up_ref[n - 1:n, im]
                xr, xi = up_ref[0:n, re], up_ref[0:n, im]
                up_ref[n:2 * n, re] = xr * lr - xi * li
                up_ref[n:2 * n, im] = xr * li + xi * lr
                yr, yi = dn_ref[r - n:r, re], dn_ref[r - n:r, im]
                dn_ref[r - 2 * n:r - n, re] = yr * lr + yi * li
                dn_ref[r - 2 * n:r - n, im] = yi * lr - yr * li
                n *= 2

    return pl.pallas_call(
        body, name=name, out_shape=[jax.ShapeDtypeStruct((r, w2), F32)] * 2, compiler_params=_cp(),
    )(a_il)


_GELU_C = math.sqrt(2.0 / math.pi)
_GELU_A = 0.044715


def _gelu(y):
    return 0.5 * y * (1.0 + jnp.tanh(_GELU_C * (y + _GELU_A * y * y * y)))


def _gelu_grad(y):
    th = jnp.tanh(_GELU_C * (y + _GELU_A * y * y * y))
    return 0.5 * (1.0 + th) + 0.5 * y * (1.0 - th * th) * _GELU_C * (1.0 + 3.0 * _GELU_A * y * y)


def _rows_shift(x, k, axis, up):
    n = x.shape[axis]
    idx = lax.broadcasted_iota(jnp.int32, x.shape, axis)
    if up:
        return jnp.where(idx < n - k, pltpu.roll(x, n - k, axis), 0.0)
    return jnp.where(idx >= k, pltpu.roll(x, k, axis), 0.0)


def _scan_block(xr, xi, pr, pi, cr, ci, rev):
    r, w = xr.shape
    nt = r // 8
    x3r, x3i = xr.reshape(nt, 8, w), xi.reshape(nt, 8, w)
    p3r, p3i = pr.reshape(nt, 8, w), pi.reshape(nt, 8, w)

    def power(rows):
        t = r - rows if rev else rows - 1
        return pr[t:t + 1, :], pi[t:t + 1, :]

    tile_row = lax.broadcasted_iota(jnp.int32, (8, w), 0)
    for sh in (1, 2, 4):
        ar, ai = power(sh)
        keep = tile_row < 8 - sh if rev else tile_row >= sh
        mr, mi = jnp.where(keep, ar, 0.0)[None], jnp.where(keep, ai, 0.0)[None]
        turn = 8 - sh if rev else sh
        x3r, x3i = _cmul_add(x3r, x3i, mr, mi, pltpu.roll(x3r, turn, 1), pltpu.roll(x3i, turn, 1))
    edge = 0 if rev else 7
    lr, li = x3r[:, edge, :], x3i[:, edge, :]
    sh = 1
    while sh < nt:
        ar, ai = power(8 * sh)
        lr, li = _cmul_add(lr, li, ar, ai, _rows_shift(lr, sh, 0, rev), _rows_shift(li, sh, 0, rev))
        sh *= 2
    tr_, ti_ = p3r[:, edge, :], p3i[:, edge, :]
    first = lax.broadcasted_iota(jnp.int32, (nt, w), 0) == (nt - 1 if rev else 0)
    wr = jnp.where(first, 1.0, _rows_shift(tr_, 1, 0, rev))
    wi = jnp.where(first, 0.0, _rows_shift(ti_, 1, 0, rev))
    er, ei = _cmul_add(_rows_shift(lr, 1, 0, rev), _rows_shift(li, 1, 0, rev), wr, wi, cr, ci)
    a8r, a8i = (p3r[nt - 1], p3i[nt - 1]) if rev else (p3r[0], p3i[0])
    x3r, x3i = _cmul_add(x3r, x3i, a8r[None], a8i[None], er[:, None, :], ei[:, None, :])
    outr, outi = x3r.reshape(r, w), x3i.reshape(r, w)
    last = 0 if rev else r - 1
    return outr, outi, outr[last:last + 1, :], outi[last:last + 1, :]


def _s5_tile_specs(n_of, r):
    t = S5_TILE
    ucol = 4 * RET_HEADS * RET_DH // LANES
    u = pl.BlockSpec((r, LANES), lambda j, i: (n_of(i), ucol + j))
    col = pl.BlockSpec((r, LANES), lambda j, i: (n_of(i), j))
    state = pl.BlockSpec((r, 2 * t), lambda j, i: (n_of(i), j))
    table = pl.BlockSpec((r, 2 * t), lambda j, i: (0, j))
    bbt = pl.BlockSpec((None, LANES, 2 * t), lambda j, i: (j, 0, 0))
    cct = pl.BlockSpec((None, 2 * t, LANES), lambda j, i: (j, 0, 0))
    vec = pl.BlockSpec((1, LANES), lambda j, i: (0, j))
    return u, col, state, table, bbt, cct, vec


def _s5_fwd(proj, bbt, cct, apow, dvec, name):
    s = proj.shape[0]
    r, t = SCAN_ROWS, S5_TILE
    w = S5_GROUPS * S5_GROUP
    u_s, col, state, table, bb_s, cc_s, vec = _s5_tile_specs(lambda i: i, r)

    def body(u_ref, bb_ref, cc_ref, p_ref, d_ref, st_ref, y_ref, g_ref, cr, ci):
        @pl.when(pl.program_id(1) == 0)
        def _():
            cr[...] = jnp.zeros_like(cr)
            ci[...] = jnp.zeros_like(ci)

        u = u_ref[...]
        bu = _mxu(u, bb_ref[...])
        xr, xi, cr[...], ci[...] = _scan_block(bu[:, :t], bu[:, t:], p_ref[:, :t], p_ref[:, t:], cr[...], ci[...], False)
        st_ref[:, :t] = xr
        st_ref[:, t:] = xi
        y = _mxu(xr, cc_ref[:t, :]) + _mxu(xi, cc_ref[t:, :]) + d_ref[...] * u
        y_ref[...] = y
        g_ref[...] = _gelu(y).astype(BF16)

    return pl.pallas_call(
        body, name=name, grid=(2 * S5_GROUPS * S5_STATE // (2 * t), s // r),
        in_specs=[u_s, bb_s, cc_s, table, vec], out_specs=[state, col, col],
        out_shape=[jax.ShapeDtypeStruct((s, 2 * S5_GROUPS * S5_STATE), F32), jax.ShapeDtypeStruct((s, w), F32),
                   jax.ShapeDtypeStruct((s, w), BF16)],
        scratch_shapes=[pltpu.VMEM((1, t), F32), pltpu.VMEM((1, t), F32)],
        compiler_params=_cp("parallel", "arbitrary"),
    )(proj, bbt, cct, apow, dvec)


def _s5_bwd(dg1, dg2, y, proj, st, bbt, cct, apow_rev, dvec, dproj, name):
    s = proj.shape[0]
    r, t = SCAN_ROWS, S5_TILE
    nb = s // r
    w = S5_GROUPS * S5_GROUP
    u_s, col, state, table, bb_s, cc_s, vec = _s5_tile_specs(lambda i: nb - 1 - i, r)
    halo = pl.BlockSpec((8, 2 * t), lambda j, i: (jnp.maximum((nb - 1 - i) * (r // 8) - 1, 0), j))
    acc = pl.BlockSpec((1, 2 * t), lambda j, i: (0, j))

    def body(a_ref, b_ref, y_ref, u_ref, s_ref, sp_ref, bb_ref, cc_ref, p_ref, d_ref, _,
             du_ref, da_ref, dbb_ref, dcc_ref, dd_ref, cr, ci):
        i = pl.program_id(1)

        @pl.when(i == 0)
        def _():
            cr[...] = jnp.zeros_like(cr)
            ci[...] = jnp.zeros_like(ci)
            da_ref[...] = jnp.zeros_like(da_ref)
            dbb_ref[...] = jnp.zeros_like(dbb_ref)
            dcc_ref[...] = jnp.zeros_like(dcc_ref)
            dd_ref[...] = jnp.zeros_like(dd_ref)

        u = u_ref[...]
        dy = (a_ref[...] + b_ref[...]) * _gelu_grad(y_ref[...])
        dd_ref[...] += jnp.sum(dy * u, axis=0, keepdims=True)
        sr, si = s_ref[:, :t], s_ref[:, t:]
        dcc_ref[:t, :] += _mxu(sr, dy, 0, 0)
        dcc_ref[t:, :] += _mxu(si, dy, 0, 0)
        xr, xi, cr[...], ci[...] = _scan_block(_mxu(dy, cc_ref[:t, :], 1, 1), _mxu(dy, cc_ref[t:, :], 1, 1),
                                               p_ref[:, :t], p_ref[:, t:], cr[...], ci[...], True)
        du_ref[...] = (dy * d_ref[...] + _mxu(xr, bb_ref[:, :t], 1, 1) + _mxu(xi, bb_ref[:, t:], 1, 1)).astype(BF16)
        dbb_ref[:, :t] += _mxu(u, xr, 0, 0)
        dbb_ref[:, t:] += _mxu(u, xi, 0, 0)
        first = i == nb - 1
        row = lax.broadcasted_iota(jnp.int32, (r, t), 0)
        pr = jnp.where(row == 0, jnp.where(first, 0.0, sp_ref[7:8, :t]), pltpu.roll(sr, 1, 0))
        pi = jnp.where(row == 0, jnp.where(first, 0.0, sp_ref[7:8, t:]), pltpu.roll(si, 1, 0))
        da_ref[:, :t] += jnp.sum(xr * pr + xi * pi, axis=0, keepdims=True)
        da_ref[:, t:] += jnp.sum(xi * pr - xr * pi, axis=0, keepdims=True)

    return pl.pallas_call(
        body, name=name, grid=(2 * S5_GROUPS * S5_STATE // (2 * t), nb),
        in_specs=[col, col, col, u_s, state, halo, bb_s, cc_s, table, vec, pl.BlockSpec(memory_space=pl.ANY)],
        out_specs=[u_s, acc, bb_s, cc_s, vec],
        out_shape=[jax.ShapeDtypeStruct(dproj.shape, dproj.dtype), jax.ShapeDtypeStruct((1, 2 * S5_GROUPS * S5_STATE), F32),
                   jax.ShapeDtypeStruct(bbt.shape, F32), jax.ShapeDtypeStruct(cct.shape, F32),
                   jax.ShapeDtypeStruct((1, w), F32)],
        scratch_shapes=[pltpu.VMEM((1, t), F32), pltpu.VMEM((1, t), F32)],
        input_output_aliases={10: 0}, compiler_params=_cp("parallel", "arbitrary"),
    )(dg1, dg2, y, proj, st, st, bbt, cct, apow_rev, dvec, dproj)


def _s5_tile_b(b_re, b_im):
    nt = S5_GROUPS * S5_STATE // S5_TILE
    gpt = S5_GROUPS // nt
    eye = jnp.eye(gpt, dtype=F32)

    def tile(b):
        t5 = jnp.einsum("jghp,gk->jghkp", b.reshape(nt, gpt, S5_GROUP, S5_STATE), eye)
        return t5.reshape(nt, gpt * S5_GROUP, S5_TILE)

    return jnp.concatenate([tile(b_re), tile(b_im)], axis=2)


def _s5_untile_b(d):
    nt = S5_GROUPS * S5_STATE // S5_TILE
    gpt = S5_GROUPS // nt
    eye = jnp.eye(gpt, dtype=F32)

    def untile(x):
        x5 = x.reshape(nt, gpt, S5_GROUP, gpt, S5_STATE)
        return jnp.einsum("jghkp,gk->jghp", x5, eye).reshape(S5_GROUPS, S5_GROUP, S5_STATE)

    return untile(d[:, :, :S5_TILE]), untile(d[:, :, S5_TILE:])


def _s5_tile_c(c_re, c_im):
    nt = S5_GROUPS * S5_STATE // S5_TILE
    gpt = S5_GROUPS // nt
    eye = jnp.eye(gpt, dtype=F32)

    def tile(c):
        t5 = jnp.einsum("jgph,gk->jkpgh", c.reshape(nt, gpt, S5_STATE, S5_GROUP), eye)
        return t5.reshape(nt, S5_TILE, gpt * S5_GROUP)

    return jnp.concatenate([tile(c_re), -tile(c_im)], axis=1)


def _s5_untile_c(d):
    nt = S5_GROUPS * S5_STATE // S5_TILE
    gpt = S5_GROUPS // nt
    eye = jnp.eye(gpt, dtype=F32)

    def untile(x):
        x5 = x.reshape(nt, gpt, S5_STATE, gpt, S5_GROUP)
        return jnp.einsum("jkpgh,gk->jgph", x5, eye).reshape(S5_GROUPS, S5_STATE, S5_GROUP)

    return untile(d[:, :S5_TILE, :]), -untile(d[:, S5_TILE:, :])


def _row_call(body, name, s, ins, outs, acc=False):
    tr = min(512, s)

    def spec(width, cb, rows):
        if rows == 1:
            return pl.BlockSpec((1, width), lambda i: (0, cb))
        return pl.BlockSpec((tr, width), lambda i: (i, cb))

    in_specs = [spec(w, cb, a.shape[0]) for a, w, cb in ins]
    out_specs = [spec(w, cb, sd.shape[0]) for sd, w, cb in outs]
    return pl.pallas_call(
        body, name=name, grid=(s // tr,), in_specs=in_specs, out_specs=out_specs,
        out_shape=[sd for sd, _, _ in outs],
        compiler_params=_cp("arbitrary" if acc else "parallel"),
    )(*[a for a, _, _ in ins])


def _sds(shape, dtype):
    return jax.ShapeDtypeStruct(shape, dtype)


def _s5_glu_fwd(y, z, b, merged, name):
    s, w = y.shape
    tr = min(512, s)

    def body(y_ref, z_ref, b_ref, _, o_ref):
        o_ref[...] = (_gelu(y_ref[...]) * _sigmoid(z_ref[...] + b_ref[...])).astype(BF16)

    row = pl.BlockSpec((tr, w), lambda i: (i, 0))
    return pl.pallas_call(
        body, name=name, grid=(s // tr,),
        in_specs=[row, row, pl.BlockSpec((1, w), lambda i: (0, 0)), pl.BlockSpec(memory_space=pl.ANY)],
        out_specs=pl.BlockSpec((tr, w), lambda i: (i, 1)),
        out_shape=jax.ShapeDtypeStruct(merged.shape, merged.dtype),
        input_output_aliases={3: 0}, compiler_params=_cp("parallel"),
    )(y, z, b, merged)


def _s5_glu_bwd(dmerged, y, z, b, name):
    s, w = y.shape

    def body(do_ref, y_ref, z_ref, b_ref, dz_ref, dg_ref, db_ref):
        @pl.when(pl.program_id(0) == 0)
        def _():
            db_ref[...] = jnp.zeros_like(db_ref)

        g = _gelu(y_ref[...])
        sg = _sigmoid(z_ref[...] + b_ref[...])
        dout = do_ref[...]
        dz = dout * g * sg * (1.0 - sg)
        dz_ref[...] = dz.astype(BF16)
        dg_ref[...] = dout * sg
        db_ref[...] += jnp.sum(dz, axis=0, keepdims=True)

    return _row_call(body, name, s, [(dmerged, w, 1), (y, w, 0), (z, w, 0), (b, w, 0)],
                     [(_sds((s, w), BF16), w, 0), (_sds((s, w), F32), w, 0), (_sds((1, w), F32), w, 0)], acc=True)


def _gdn_conv_fwd(projx, cw, name):
    s = projx.shape[0]
    nh = GDN_HEADS

    def body(x_ref, w_ref, o_ref):
        j = pl.program_id(0)
        cv = _conv_fwd(_conv_taps(x_ref[...], GDN_CONV), w_ref)
        y = cv * _sigmoid(cv)
        nrm = y * lax.rsqrt(jnp.sum(y * y, axis=-1, keepdims=True) + EPS)
        o_ref[...] = jnp.where(j < nh, nrm * (GDN_DH ** -0.5), jnp.where(j < 2 * nh, nrm, y))

    return pl.pallas_call(
        body, name=name, grid=(3 * nh,),
        in_specs=[pl.BlockSpec((s, GDN_DH), lambda j: (0, j)), pl.BlockSpec((GDN_CONV, GDN_DH), lambda j: (0, j))],
        out_specs=pl.BlockSpec((s, GDN_DH), lambda j: (0, j)),
        out_shape=jax.ShapeDtypeStruct((s, 3 * nh * GDN_DH), F32), compiler_params=_cp("parallel"),
    )(projx, cw)


def _gdn_conv_bwd(projx, cw, dqkv, dprojx, name):
    s = projx.shape[0]
    nh = GDN_HEADS

    def body(x_ref, w_ref, d_ref, _, dx_ref, dw_ref):
        j = pl.program_id(0)
        x = x_ref[...]
        taps = _conv_taps(x, GDN_CONV)
        cv = _conv_fwd(taps, w_ref)
        sg = _sigmoid(cv)
        y = cv * sg
        rinv = lax.rsqrt(jnp.sum(y * y, axis=-1, keepdims=True) + EPS)
        nrm = y * rinv
        dn = d_ref[...]
        dns = jnp.where(j < nh, dn * (GDN_DH ** -0.5), dn)
        dyn = rinv * (dns - nrm * jnp.sum(dns * nrm, axis=-1, keepdims=True))
        dy = jnp.where(j < 2 * nh, dyn, dn)
        dc = dy * (sg * (1.0 + cv * (1.0 - sg)))
        dx_ref[...] = _conv_bwd(taps, dc, w_ref, dw_ref).astype(BF16)

    col = pl.BlockSpec((s, GDN_DH), lambda j: (0, j))
    wcol = pl.BlockSpec((GDN_CONV, GDN_DH), lambda j: (0, j))
    return pl.pallas_call(
        body, name=name, grid=(3 * nh,), in_specs=[col, wcol, col, pl.BlockSpec(memory_space=pl.ANY)],
        out_specs=[col, wcol],
        out_shape=[jax.ShapeDtypeStruct(dprojx.shape, dprojx.dtype), jax.ShapeDtypeStruct((GDN_CONV, 3 * nh * GDN_DH), F32)],
        input_output_aliases={3: 0}, compiler_params=_cp("parallel"),
    )(projx, cw, dqkv, dprojx)


def _softplus(x):
    return jnp.maximum(x, 0.0) + jnp.log1p(jnp.exp(-jnp.abs(x)))


def _gdn_gates_fwd(projx, alog, dtb, name):
    s = projx.shape[0]
    w = GDN_HEADS * GDN_DH
    tr = min(512, s)

    def body(t_ref, al_ref, dt_ref, bo_ref, go_ref):
        t = t_ref[...]
        for h in range(GDN_HEADS):
            lo, hi = h * GDN_DH, (h + 1) * GDN_DH
            b = jnp.broadcast_to(t[:, h:h + 1], (tr, GDN_DH))
            a = jnp.broadcast_to(t[:, GDN_HEADS + h:GDN_HEADS + h + 1], (tr, GDN_DH))
            bo_ref[:, lo:hi] = _sigmoid(b)
            go_ref[:, lo:hi] = -jnp.exp(al_ref[:, lo:hi]) * _softplus(a + dt_ref[:, lo:hi])

    row = pl.BlockSpec((tr, w), lambda i: (i, 0))
    vec = pl.BlockSpec((1, w), lambda i: (0, 0))
    return pl.pallas_call(
        body, name=name, grid=(s // tr,),
        in_specs=[pl.BlockSpec((tr, LANES), lambda i: (i, 4 * w // LANES)), vec, vec], out_specs=[row, row],
        out_shape=[jax.ShapeDtypeStruct((s, w), F32)] * 2, compiler_params=_cp("parallel"),
    )(projx, alog, dtb)


def _gdn_gates_bwd(projx, alog, dtb, dbeta, dg, dprojx, name):
    s = projx.shape[0]
    w = GDN_HEADS * GDN_DH
    tr = min(512, s)
    gate_blk = 4 * w // LANES

    def body(t_ref, al_ref, dt_ref, dbe_ref, dg_ref, _, o_ref, dal_ref, ddt_ref):
        @pl.when(pl.program_id(0) == 0)
        def _():
            dal_ref[...] = jnp.zeros_like(dal_ref)
            ddt_ref[...] = jnp.zeros_like(ddt_ref)

        t = t_ref[...]
        lane = lax.broadcasted_iota(jnp.int32, (tr, LANES), 1)
        lane1 = lax.broadcasted_iota(jnp.int32, (1, LANES), 1)
        out = jnp.zeros((tr, LANES), F32)
        dal = jnp.zeros((1, LANES), F32)
        ddt = jnp.zeros((1, LANES), F32)
        for h in range(GDN_HEADS):
            lo, hi = h * GDN_DH, (h + 1) * GDN_DH
            beta = _sigmoid(t[:, h:h + 1])
            pb = jnp.sum(dbe_ref[:, lo:hi], axis=-1, keepdims=True)
            db = pb * beta * (1.0 - beta)
            xa = t[:, GDN_HEADS + h:GDN_HEADS + h + 1] + dt_ref[:, lo:lo + 1]
            ea = -jnp.exp(al_ref[:, lo:lo + 1])
            pg = jnp.sum(dg_ref[:, lo:hi], axis=-1, keepdims=True)
            da = pg * ea * _sigmoid(xa)
            out = jnp.where(lane == h, db, jnp.where(lane == GDN_HEADS + h, da, out))
            dal = jnp.where(lane1 == h, jnp.sum(pg * ea * _softplus(xa), axis=0, keepdims=True), dal)
            ddt = jnp.where(lane1 == h, jnp.sum(da, axis=0, keepdims=True), ddt)
        o_ref[...] = out.astype(BF16)
        dal_ref[...] += dal
        ddt_ref[...] += ddt

    row = pl.BlockSpec((tr, w), lambda i: (i, 0))
    vec = pl.BlockSpec((1, w), lambda i: (0, 0))
    small = pl.BlockSpec((1, LANES), lambda i: (0, 0))
    gates = pl.BlockSpec((tr, LANES), lambda i: (i, gate_blk))
    return pl.pallas_call(
        body, name=name, grid=(s // tr,),
        in_specs=[gates, vec, vec, row, row, pl.BlockSpec(memory_space=pl.ANY)],
        out_specs=[gates, small, small],
        out_shape=[jax.ShapeDtypeStruct(dprojx.shape, dprojx.dtype), jax.ShapeDtypeStruct((1, LANES), F32),
                   jax.ShapeDtypeStruct((1, LANES), F32)],
        input_output_aliases={5: 0}, compiler_params=_cp("arbitrary"),
    )(projx, alog, dtb, dbeta, dg, dprojx)


def _gdn_tri():
    c = GDN_CHUNK
    i = lax.broadcasted_iota(jnp.int32, (c, c), 0)
    j = lax.broadcasted_iota(jnp.int32, (c, c), 1)
    return ((i >= j).astype(F32), (i <= j).astype(F32), i >= j, i > j, (i == j).astype(F32))


def _bdot(a, b, ca=2, cb=1, precision=None):
    return lax.dot_general(a, b, (((ca,), (cb,)), ((0,), (0,))), precision=precision, preferred_element_type=F32)


def _bmxu(a, b, ca=2, cb=1):
    return _bdot(a.astype(BF16), b.astype(BF16), ca, cb)


def _split(x):
    hi = x.astype(BF16)
    return hi, (x - hi.astype(F32)).astype(BF16)


def _bdot3(a, b, ca=2, cb=1):
    ah, al = _split(a)
    bh, bl = _split(b)
    return _bdot(ah, bh, ca, cb) + (_bdot(ah, bl, ca, cb) + _bdot(al, bh, ca, cb))


def _tri_dot(tri, x):
    t = tri.astype(BF16)
    hi = x.astype(BF16)
    r1 = x - hi.astype(F32)
    mid = r1.astype(BF16)
    lo = (r1 - mid.astype(F32)).astype(BF16)
    return _dot(t, hi) + (_dot(t, mid) + _dot(t, lo))


def _heads(x):
    return jnp.stack([x[:, h * GDN_DH:(h + 1) * GDN_DH] for h in range(GDN_HEADS)], axis=0)


def _unheads(x):
    return jnp.concatenate([x[h] for h in range(GDN_HEADS)], axis=1)


def _gdn_chunk(q, k, v, bb, g2d, tri, t=None):
    low, up, incl, strict, eye = tri
    c = GDN_CHUNK
    gc = _heads(_tri_dot(low, g2d))
    gci = gc[:, :, :c]
    gdiff = gci - jnp.swapaxes(gci, 1, 2)
    decay = jnp.where(incl, jnp.exp(jnp.where(incl, gdiff, 0.0)), 0.0)
    kb, vb = k * bb, v * bb
    kbk = _bmxu(kb, k, 2, 2)
    if t is None:
        x = -jnp.where(strict, kbk * decay, 0.0)
        t = eye + x
        p = x
        for _ in range(c.bit_length() - 2):
            p = _bdot3(p, p)
            t = t + _bdot3(t, p)
    eg = jnp.exp(gc)
    kbg = kb * eg
    gcl = gc[:, c - 1:c, :]
    ek = jnp.exp(gcl - gc)
    qkraw = _bmxu(q, k, 2, 2)
    return dict(decay=decay, kb=kb, vb=vb, kbk=kbk, t=t, eg=eg, kbg=kbg, ek=ek, gl=jnp.exp(gcl),
                w=_bmxu(t, kbg), u=_bmxu(t, vb), qkraw=qkraw, qk=jnp.where(incl, qkraw * decay, 0.0),
                qd=q * eg, kd=k * ek)


def _gdn_specs(n_of):
    c, w = GDN_CHUNK, GDN_HEADS * GDN_DH

    def blk(cb, width=w):
        return pl.BlockSpec((c, width), lambda n: (n_of(n), cb))

    st = pl.BlockSpec((None, GDN_HEADS, GDN_DH, GDN_DH), lambda n: (n_of(n), 0, 0, 0))
    vec = pl.BlockSpec((1, GDN_DH), lambda n: (0, 0))
    return blk, st, vec


def _gdn_load(qkv_ref, b_ref, g_ref, tri, t=None):
    w = GDN_HEADS * GDN_DH
    q, k, v = _heads(qkv_ref[:, :w]), _heads(qkv_ref[:, w:2 * w]), _heads(qkv_ref[:, 2 * w:])
    bb = _heads(b_ref[...])
    return q, k, v, bb, _gdn_chunk(q, k, v, bb, g_ref[...], tri, t)


def _gdn_fwd(qkv, beta, g, projx, onorm, name):
    s = qkv.shape[0]
    c = GDN_CHUNK
    nc = s // c
    w = GDN_HEADS * GDN_DH
    blk, st, vec = _gdn_specs(lambda n: n)
    inv = pl.BlockSpec((None, GDN_HEADS, c, c), lambda n: (n, 0, 0, 0))

    def body(qkv_ref, b_ref, g_ref, z_ref, on_ref, o_ref, st_ref, t_ref, state):
        @pl.when(pl.program_id(0) == 0)
        def _():
            state[...] = jnp.zeros_like(state)

        _, _, _, _, ch = _gdn_load(qkv_ref, b_ref, g_ref, _gdn_tri())
        t_ref[...] = ch["t"]
        sp = state[...]
        st_ref[...] = sp
        vn = ch["u"] - _bmxu(ch["w"], sp)
        o = _bmxu(ch["qd"], sp) + _bmxu(ch["qk"], vn)
        state[...] = sp * ch["gl"] + _bmxu(ch["kd"], vn, 1, 1)
        r = lax.rsqrt(jnp.mean(o * o, axis=-1, keepdims=True) + EPS)
        z = _heads(z_ref[...])
        o_ref[...] = _unheads(o * r * on_ref[...] * (z * _sigmoid(z))).astype(BF16)

    return pl.pallas_call(
        body, name=name, grid=(nc,),
        in_specs=[blk(0, 3 * w), blk(0), blk(0), blk(3), vec], out_specs=[blk(0), st, inv],
        out_shape=[jax.ShapeDtypeStruct((s, w), BF16), jax.ShapeDtypeStruct((nc, GDN_HEADS, GDN_DH, GDN_DH), F32),
                   jax.ShapeDtypeStruct((nc, GDN_HEADS, c, c), F32)],
        scratch_shapes=[pltpu.VMEM((GDN_HEADS, GDN_DH, GDN_DH), F32)],
        compiler_params=_cp("arbitrary"),
    )(qkv, beta, g, projx, onorm.reshape(1, -1))


def _gdn_bwd(qkv, beta, g, projx, onorm, states, tinv, dout, name):
    s = qkv.shape[0]
    c = GDN_CHUNK
    nc = s // c
    w = GDN_HEADS * GDN_DH
    blk, st, vec = _gdn_specs(lambda n: nc - 1 - n)
    inv = pl.BlockSpec((None, GDN_HEADS, c, c), lambda n: (nc - 1 - n, 0, 0, 0))

    def body(qkv_ref, b_ref, g_ref, z_ref, on_ref, st_ref, t_ref, do_ref,
             dqkv_ref, db_ref, dg_ref, dz_ref, don_ref, carry):
        @pl.when(pl.program_id(0) == 0)
        def _():
            carry[...] = jnp.zeros_like(carry)
            don_ref[...] = jnp.zeros_like(don_ref)

        tri = _gdn_tri()
        low, up, incl, strict, eye = tri
        q, k, v, bb, ch = _gdn_load(qkv_ref, b_ref, g_ref, tri, t_ref[...])
        sp = st_ref[...]
        vn = ch["u"] - _bmxu(ch["w"], sp)
        o = _bmxu(ch["qd"], sp) + _bmxu(ch["qk"], vn)
        r = lax.rsqrt(jnp.mean(o * o, axis=-1, keepdims=True) + EPS)
        orn = o * r
        z = _heads(z_ref[...])
        sg = _sigmoid(z)
        dout = _heads(do_ref[...])
        onw = on_ref[...]
        dz_ref[...] = _unheads(dout * orn * onw * (sg * (1.0 + z * (1.0 - sg)))).astype(BF16)
        don = dout * (z * sg)
        don_ref[...] += jnp.sum(jnp.sum(don * orn, axis=0), axis=0, keepdims=True)
        dor = don * onw
        do = r * (dor - orn * jnp.mean(dor * orn, axis=-1, keepdims=True))
        dsn = carry[...]
        dqd = _bmxu(do, sp, 2, 2)
        dqk = jnp.where(incl, _bmxu(do, vn, 2, 2), 0.0)
        dvn = _bmxu(ch["qk"], do, 1, 1) + _bmxu(ch["kd"], dsn)
        dkd = _bmxu(vn, dsn, 2, 2)
        dgl = jnp.sum(dsn * sp, axis=1, keepdims=True)
        dw = -_bmxu(dvn, sp, 2, 2)
        carry[...] = _bmxu(ch["qd"], do, 1, 1) + dsn * ch["gl"] - _bmxu(ch["w"], dvn, 1, 1)
        t = ch["t"]
        dvb = _bmxu(t, dvn, 1, 1)
        dkbg = _bmxu(t, dw, 1, 1)
        dt = _bmxu(dvn, ch["vb"], 2, 2) + _bmxu(dw, ch["kbg"], 2, 2)
        da = -_bdot3(_bdot3(t, dt, 1, 1), t, 2, 2)
        da = jnp.where(strict, da, 0.0)
        decay = ch["decay"]
        dkbk = da * decay
        dqkr = dqk * decay
        mdec = (da * ch["kbk"] + dqk * ch["qkraw"]) * decay
        dkb = _bmxu(dkbk, k) + dkbg * ch["eg"]
        dk = _bmxu(dkbk, ch["kb"], 1, 1) + _bmxu(dqkr, q, 1, 1) + dkd * ch["ek"] + dkb * bb
        dq = _bmxu(dqkr, k) + dqd * ch["eg"]
        tk = dkd * ch["kd"]
        dgcl = jnp.sum(tk, axis=1, keepdims=True) + dgl * ch["gl"]
        row = lax.broadcasted_iota(jnp.int32, (GDN_HEADS, c, GDN_DH), 1)
        zpad = jnp.zeros((GDN_HEADS, c, GDN_DH - c), F32)
        dgc = (jnp.concatenate([mdec, zpad], axis=2) - jnp.concatenate([jnp.swapaxes(mdec, 1, 2), zpad], axis=2)
               + dqd * ch["qd"] - tk + dkbg * ch["kbg"] + jnp.where(row == c - 1, dgcl, 0.0))
        dqkv_ref[:, :w] = _unheads(dq)
        dqkv_ref[:, w:2 * w] = _unheads(dk)
        dqkv_ref[:, 2 * w:] = _unheads(dvb * bb)
        db_ref[...] = _unheads(dkb * k + dvb * v)
        dg_ref[...] = _tri_dot(up, _unheads(dgc))

    return pl.pallas_call(
        body, name=name, grid=(nc,),
        in_specs=[blk(0, 3 * w), blk(0), blk(0), blk(3), vec, st, inv, blk(0)],
        out_specs=[blk(0, 3 * w), blk(0), blk(0), blk(3), vec],
        out_shape=[jax.ShapeDtypeStruct((s, 3 * w), F32), jax.ShapeDtypeStruct((s, w), F32),
                   jax.ShapeDtypeStruct((s, w), F32), jax.ShapeDtypeStruct(projx.shape, BF16),
                   jax.ShapeDtypeStruct((1, GDN_DH), F32)],
        scratch_shapes=[pltpu.VMEM((GDN_HEADS, GDN_DH, GDN_DH), F32)],
        compiler_params=_cp("arbitrary"),
    )(qkv, beta, g, projx, onorm.reshape(1, -1), states, tinv, dout)


_WEIGHTS = (
    "l0_mix_norm", "l0_w_in", "l0_ret_norm", "l0_s5_lambda_re", "l0_s5_lambda_im", "l0_s5_b_re", "l0_s5_b_im",
    "l0_s5_c_re", "l0_s5_c_im", "l0_s5_d", "l0_s5_log_dt", "l0_s5_w_glu", "l0_s5_b_glu", "l0_w_out",
    "l0_xa_norm", "l0_mem_norm", "l0_xa_wq", "l0_xa_wkv", "l0_xa_wo", "l0_ffn_norm", "l0_ffn_w_up",
    "l0_ffn_conv", "l0_ffn_w_down", "l1_mix_norm", "l1_w_in", "l1_conv", "l1_a_log", "l1_dt_bias", "l1_o_norm",
    "l1_w_out", "l1_xa_norm", "l1_mem_norm", "l1_xa_wq", "l1_xa_wkv", "l1_xa_wo", "l1_ffn_norm", "l1_ffn_w_up",
    "l1_ffn_conv", "l1_ffn_w_down", "final_norm")
_INPUTS = ("x", "mem") + _WEIGHTS + ("loss_target",) + tuple("m_" + n for n in _WEIGHTS) + tuple("v_" + n for n in _WEIGHTS)

_COL = ("l0_w_in", "l0_xa_wkv", "l0_ffn_w_up", "l0_ffn_conv", "l1_w_in", "l1_conv", "l1_xa_wkv", "l1_ffn_w_up",
        "l1_ffn_conv")
_ROW = ("l0_s5_w_glu", "l0_w_out", "l0_xa_wq", "l0_xa_wo", "l0_ffn_w_down", "l1_w_out", "l1_xa_wq", "l1_xa_wo",
        "l1_ffn_w_down")
_F32_WIRE = ("l0_ffn_conv", "l1_conv", "l1_ffn_conv")
_REP = tuple(n for n in _WEIGHTS if n not in _COL + _ROW)
_GATHER_GROUPS = (("l0_w_in", "l0_s5_w_glu", "l0_w_out"),
                  ("l0_xa_wq", "l0_xa_wkv", "l0_xa_wo", "l0_ffn_w_up", "l0_ffn_conv", "l0_ffn_w_down"),
                  ("l1_w_in", "l1_conv", "l1_w_out", "l1_xa_wq", "l1_xa_wkv", "l1_xa_wo"),
                  ("l1_ffn_w_up", "l1_ffn_conv", "l1_ffn_w_down"))


def _round_up(n, m):
    return (n + m - 1) // m * m


_REP_BIG = ("l0_s5_lambda_re", "l0_s5_lambda_im", "l0_s5_b_re", "l0_s5_b_im", "l0_s5_c_re", "l0_s5_c_im", "l0_s5_d")
_REP_LAST = "l0_mix_norm"
_REP_SMALL = tuple(n for n in _REP if n not in _REP_BIG + (_REP_LAST,))
PACK_WIDTH = 1024


def _pack_rows(ts):
    rows = [jnp.pad(t, ((0, 0), (0, PACK_WIDTH - t.shape[1]))) for t in ts]
    rows.append(jnp.zeros((_round_up(len(ts), 8) - len(ts), PACK_WIDTH), F32))
    return jnp.concatenate(rows, axis=0)


def _s5_interleave(re, im):
    lead = re.shape[:-1]
    nt = re.shape[-1] // S5_TILE
    both = jnp.stack([re.reshape(lead + (nt, S5_TILE)), im.reshape(lead + (nt, S5_TILE))], axis=-2)
    return both.reshape(lead + (2 * re.shape[-1],))


def _s5_split(x):
    lead = x.shape[:-1]
    y = x.reshape(lead + (x.shape[-1] // (2 * S5_TILE), 2, S5_TILE))
    return y[..., 0, :].reshape(lead + (-1,)), y[..., 1, :].reshape(lead + (-1,))


def _s5_discretise(lr, li, log_dt, b_re, b_im):
    dt = jnp.exp(log_dt)[:, None]
    mag = jnp.exp(lr * dt)
    a_re = mag * jnp.cos(li * dt)
    a_im = mag * jnp.sin(li * dt)
    den = lr * lr + li * li
    z_re = ((a_re - 1.0) * lr + a_im * li) / den
    z_im = (a_im * lr - (a_re - 1.0) * li) / den
    bb_re = z_re[:, None, :] * b_re - z_im[:, None, :] * b_im
    bb_im = z_re[:, None, :] * b_im + z_im[:, None, :] * b_re
    return a_re, a_im, bb_re, bb_im


def kernel(*args):
    p = dict(zip(_INPUTS, args, strict=True))
    x0, mem0, tgt = p["x"][0], p["mem"][0], p["loss_target"][0]
    s, d = x0.shape
    me = _slot(*_mesh_pos())
    grads = {}
    wire = {n: (F32 if n in _F32_WIRE else BF16) for n in _COL + _ROW}

    zones = {n: _into_slot(p[n], wire[n], me, "place_" + n) for names in _GATHER_GROUPS for n in names}
    gather, pin = [], jnp.zeros((), F32)
    for i, names in enumerate(_GATHER_GROUPS):
        handle, token = _push_start([], [zones[n] for n in names], f"gather{i}_start")
        gather.append(handle)
        pin = pin + token[0, 0]
    w = {}

    def gathered(i, after):
        for n, full in zip(_GATHER_GROUPS[i], _push_wait(gather[i], after, f"gather{i}_wait")):
            if n in _COL:
                full = full.transpose(1, 0, 2)
            w[n] = full.reshape(-1, full.shape[-1]) if n in _ROW else full.reshape(full.shape[0], -1)

    pending = []

    def exchange(names, gain, tag):
        slots = []
        for n in names:
            g = grads[n]
            if n in _COL:
                pieces = g if isinstance(g, tuple) else (g,)
                g = jnp.concatenate([t.reshape(t.shape[0], -1, p[n].shape[1]).transpose(1, 0, 2) for t in pieces], axis=0)
            else:
                g = g.reshape((N_DEV, -1) + g.shape[1:])
            slots.append(g.astype(wire[n]))
        handle, token = _push_start(slots, [], tag + "_start")
        pending.append((names, slots, handle, tag))
        return gain + token[0, 0]

    def xattn(pre, x_in, hx):
        q = _mm(hx, w[pre + "xa_wq"], out_dtype=BF16, name=pre + "xa_q")
        memn = _norm_fwd(mem0, p[pre + "mem_norm"], pre + "mem_norm_fwd")
        kv = _mm(memn, w[pre + "xa_wkv"], out_dtype=BF16, name=pre + "xa_kv")
        ao = _xattn_fwd(q, kv, pre + "xattn_fwd")
        x_out, hf = _mm(ao, w[pre + "xa_wo"], res=x_in, norm_gain=p[pre + "ffn_norm"], name=pre + "xa_o")
        return x_out, hf, (x_in, hx, q, memn, kv, ao)

    def xattn_bwd(pre, saved, dxo):
        x_in, hx, q, memn, kv, ao = saved
        dao = _mm(dxo, w[pre + "xa_wo"], tb=True, name=pre + "xa_o_dx")
        grads[pre + "xa_wo"] = _mm(ao, dxo, ta=True, out_dtype=BF16, name=pre + "xa_o_dw")
        dq, dkv = _xattn_bwd(q, kv, dao, pre + "xattn_bwd")
        grads[pre + "xa_wq"] = _mm(hx, dq, ta=True, out_dtype=BF16, name=pre + "xa_q_dw")
        grads[pre + "xa_wkv"] = _mm(memn, dkv, ta=True, out_dtype=BF16, name=pre + "xa_kv_dw")
        dmemn = _mm(dkv, w[pre + "xa_wkv"], tb=True, name=pre + "xa_kv_dx")
        gain = exchange((pre + "xa_wo", pre + "xa_wq", pre + "xa_wkv"), p[pre + "xa_norm"], pre + "xa_grads")
        dx_in, grads[pre + "xa_norm"] = _mm_norm_bwd(dq, w[pre + "xa_wq"], x_in, gain, dxo, pre + "xa_q_dx")
        _, grads[pre + "mem_norm"] = _norm_bwd(mem0, p[pre + "mem_norm"], dmemn, jnp.zeros_like(mem0), pre + "mem_norm_bwd")
        return dx_in

    def ffn(pre, x_in, hf, next_gain):
        up = _mm(hf, w[pre + "ffn_w_up"], out_dtype=BF16, name=pre + "ffn_up")
        act = _ffn_act_fwd(up, w[pre + "ffn_conv"], pre + "ffn_act_fwd")
        res = _mm(act, w[pre + "ffn_w_down"], res=x_in, norm_gain=next_gain, name=pre + "ffn_down")
        x_out, h_next = res if next_gain is not None else (res, None)
        return x_out, h_next, (x_in, hf, up, act)

    def ffn_bwd(pre, saved, dxo):
        x_in, hf, up, act = saved
        dact = _mm(dxo, w[pre + "ffn_w_down"], tb=True, out_dtype=BF16, name=pre + "ffn_down_dx")
        grads[pre + "ffn_w_down"] = _mm(act, dxo, ta=True, out_dtype=BF16, name=pre + "ffn_down_dw")
        dpu, dpg, dcu, dcg = _ffn_act_bwd(up, w[pre + "ffn_conv"], dact, pre + "ffn_act_bwd")
        grads[pre + "ffn_conv"] = jnp.concatenate([dcu, dcg], axis=1)
        grads[pre + "ffn_w_up"] = (_mm(hf, dpu, ta=True, out_dtype=BF16, name=pre + "ffn_up_dw_u"),
                                   _mm(hf, dpg, ta=True, out_dtype=BF16, name=pre + "ffn_up_dw_g"))
        gain = exchange((pre + "ffn_w_down", pre + "ffn_w_up", pre + "ffn_conv"), p[pre + "ffn_norm"], pre + "ffn_grads")
        dx_in, grads[pre + "ffn_norm"] = _mm_norm_bwd([dpu, dpg], w[pre + "ffn_w_up"], x_in, gain, dxo, pre + "ffn_up_dx")
        return dx_in

    cos, sin = _rope_tables(s)
    (a_re, a_im, bb_re, bb_im), disc_vjp = jax.vjp(
        _s5_discretise, p["l0_s5_lambda_re"], p["l0_s5_lambda_im"], p["l0_s5_log_dt"], p["l0_s5_b_re"], p["l0_s5_b_im"])
    apow, apow_rev = _s5_pow_tables(_s5_interleave(a_re.reshape(1, -1), a_im.reshape(1, -1)), "l0_s5_pow_tables")
    bbt = _s5_tile_b(bb_re, bb_im).astype(BF16)
    cct = _s5_tile_c(p["l0_s5_c_re"], p["l0_s5_c_im"]).astype(BF16)
    s5_d = p["l0_s5_d"].reshape(1, -1)
    b_glu = p["l0_s5_b_glu"].reshape(1, -1)

    h0 = _norm_fwd(x0, p["l0_mix_norm"] + pin, "l0_mix_norm_fwd")
    gathered(0, h0)
    proj = _mm(h0, w["l0_w_in"], name="l0_in")
    merged, ret_states = _ret_fwd(proj, cos, sin, p["l0_ret_norm"], "l0_ret_fwd")
    st, y, gy = _s5_fwd(proj, bbt, cct, apow, s5_d, "l0_s5_fwd")
    z = _mm(gy, w["l0_s5_w_glu"], name="l0_s5_glu_mm")
    merged = _s5_glu_fwd(y, z, b_glu, merged, "l0_s5_glu_fwd")
    x1, hx0 = _mm(merged, w["l0_w_out"], res=x0, norm_gain=p["l0_xa_norm"], name="l0_out")
    gathered(1, x1)
    x2, hf0, xa0 = xattn("l0_", x1, hx0)
    x3, h1, ff0 = ffn("l0_", x2, hf0, p["l1_mix_norm"])

    gathered(2, x3)
    w1 = w["l1_w_in"]
    wx = jnp.pad(w1, ((0, 0), (0, _round_up(w1.shape[1], LANES) - w1.shape[1])))
    alog_x = jnp.repeat(p["l1_a_log"], GDN_DH).reshape(1, -1)
    dtb_x = jnp.repeat(p["l1_dt_bias"], GDN_DH).reshape(1, -1)
    projx = _mm(h1, wx, name="l1_in")
    qkv = _gdn_conv_fwd(projx, w["l1_conv"], "l1_conv_fwd")
    beta, glog = _gdn_gates_fwd(projx, alog_x, dtb_x, "l1_gates_fwd")
    o_gdn, gdn_states, gdn_tinv = _gdn_fwd(qkv, beta, glog, projx, p["l1_o_norm"], "l1_gdn_fwd")
    x4, hx1 = _mm(o_gdn, w["l1_w_out"], res=x3, norm_gain=p["l1_xa_norm"], name="l1_out")
    x5, hf1, xa1 = xattn("l1_", x4, hx1)
    gathered(3, x5)
    x6, _, ff1 = ffn("l1_", x5, hf1, None)

    loss_part, dx6, grads["final_norm"] = _loss_head(x6, p["final_norm"], tgt, "loss_head")
    loss = lax.psum(loss_part[0, 0], ("x", "y", "c"))
    dx5 = ffn_bwd("l1_", ff1, dx6)
    dx4 = xattn_bwd("l1_", xa1, dx5)

    do_gdn = _mm(dx4, w["l1_w_out"], tb=True, name="l1_out_dx")
    grads["l1_w_out"] = _mm(o_gdn, dx4, ta=True, out_dtype=BF16, name="l1_out_dw")
    dqkv, dbeta, dglog, dprojx, grads["l1_o_norm"] = _gdn_bwd(
        qkv, beta, glog, projx, p["l1_o_norm"], gdn_states, gdn_tinv, do_gdn, "l1_gdn_bwd")
    dprojx, grads["l1_conv"] = _gdn_conv_bwd(projx, w["l1_conv"], dqkv, dprojx, "l1_conv_bwd")
    dprojx, dalog_x, ddtb_x = _gdn_gates_bwd(projx, alog_x, dtb_x, dbeta, dglog, dprojx, "l1_gates_bwd")
    grads["l1_w_in"] = _mm(h1, dprojx, ta=True, out_dtype=BF16, name="l1_in_dw")[:, :w1.shape[1]]
    grads["l1_a_log"] = dalog_x[0, :GDN_HEADS]
    grads["l1_dt_bias"] = ddtb_x[0, :GDN_HEADS]
    gain = exchange(("l1_w_out", "l1_w_in", "l1_conv"), p["l1_mix_norm"], "l1_mix_grads")
    dx3, grads["l1_mix_norm"] = _mm_norm_bwd(dprojx, wx, x3, gain, dx4, "l1_in_dx")

    dx2 = ffn_bwd("l0_", ff0, dx3)
    dx1 = xattn_bwd("l0_", xa0, dx2)

    dmerged = _mm(dx1, w["l0_w_out"], tb=True, name="l0_out_dx")
    grads["l0_w_out"] = _mm(merged, dx1, ta=True, out_dtype=BF16, name="l0_out_dw")
    dproj, grads["l0_ret_norm"] = _ret_bwd(proj, cos, sin, p["l0_ret_norm"], ret_states, dmerged, "l0_ret_bwd")
    dzg, dg1, grads["l0_s5_b_glu"] = _s5_glu_bwd(dmerged, y, z, b_glu, "l0_s5_glu_bwd")
    grads["l0_s5_w_glu"] = _mm(gy, dzg, ta=True, out_dtype=BF16, name="l0_s5_glu_dw")
    s5_d_after = exchange(("l0_w_out", "l0_s5_w_glu"), s5_d, "l0_out_grads")
    dg2 = _mm(dzg, w["l0_s5_w_glu"], tb=True, name="l0_s5_glu_dx")
    dproj, da_s5, dbbt, dcct, grads["l0_s5_d"] = _s5_bwd(dg1, dg2, y, proj, st, bbt, cct, apow_rev, s5_d_after, dproj, "l0_s5_bwd")
    dbb_re, dbb_im = _s5_untile_b(dbbt)
    grads["l0_s5_c_re"], grads["l0_s5_c_im"] = _s5_untile_c(dcct)
    da_re, da_im = (t.reshape(S5_GROUPS, S5_STATE) for t in _s5_split(da_s5[0]))
    (grads["l0_s5_lambda_re"], grads["l0_s5_lambda_im"], grads["l0_s5_log_dt"], grads["l0_s5_b_re"],
     grads["l0_s5_b_im"]) = disc_vjp((da_re, da_im, dbb_re, dbb_im))

    def as_2d(t):
        return t.reshape(-1, t.shape[-1])

    def as_row(t):
        return t.reshape(1, -1)

    small_own = _pack_rows([as_row(grads[n]) for n in _REP_SMALL])
    big_own = [as_2d(grads[n].reshape(p[n].shape)) for n in _REP_BIG]
    rep_zones = [_into_slot(small_own, F32, me, "place_rep0")]
    rep_zones += [_into_slot(t.reshape(-1, LANES), BF16, me, f"place_rep{i + 1}") for i, t in enumerate(big_own)]
    rep_handle, rep_token = _push_start([], rep_zones, "rep_grads_start")

    grads["l0_w_in"] = _mm(h0, dproj, ta=True, out_dtype=BF16, pin=rep_token, name="l0_in_dw")
    gain = exchange(("l0_w_in",), p["l0_mix_norm"], "l0_mix_grads")
    dx0, grads["l0_mix_norm"] = _mm_norm_bwd(dproj, w["l0_w_in"], x0, gain, dx1, "l0_in_dx")

    last_own = _pack_rows([as_row(grads[_REP_LAST])])
    last_handle, _ = _push_start([], [_into_slot(last_own, F32, me, "place_rep_last")], "rep_last_start")
    last_land, = _push_wait(last_handle, dx0, "rep_last_wait")
    rep_lands = _push_wait(rep_handle, last_land, "rep_grads_wait")
    rep_land = rep_lands[0]

    outs = {}
    kinds = ("grad_", "delta_", "new_m_", "new_v_")
    for names, slots, handle, tag in pending:
        for n, own_slots, land in zip(names, slots, _push_wait(handle, rep_land, tag + "_wait")):
            shape = p[n].shape
            own = lax.dynamic_index_in_dim(own_slots, me, 0, keepdims=False)
            res = _adamw(land, own, *(p[pre + n].reshape(own.shape) for pre in ("", "m_", "v_")), "adamw_" + n)
            for kind, t in zip(kinds, res):
                outs[kind + n] = t.reshape(shape)
    for n, own, land in zip(_REP_BIG, big_own, rep_lands[1:]):
        res = _adamw(land.reshape((N_DEV,) + own.shape), None, *(as_2d(p[pre + n]) for pre in ("", "m_", "v_")), "adamw_" + n)
        for kind, t in zip(kinds, res):
            outs[kind + n] = t.reshape(p[n].shape)
    for names, land, own, nm in ((_REP_SMALL, rep_land, small_own, "adamw_small"), ((_REP_LAST,), last_land, last_own, "adamw_last")):
        res = _adamw_rows(land, own, *([as_row(p[pre + n]) for n in names] for pre in ("", "m_", "v_")), nm)
        for j, kind in enumerate(kinds):
            for i, n in enumerate(names):
                outs[kind + n] = res[j * len(names) + i].reshape(p[n].shape)

    return (loss, dx0[None]) + tuple(outs[kind + n] for kind in kinds for n in _WEIGHTS)
```

```python
import math

import numpy as np
import jax
import jax.numpy as jnp
from jax import lax
from jax.experimental import pallas as pl
from jax.experimental.pallas import tpu as pltpu

F32 = jnp.float32
BF16 = jnp.bfloat16
EPS = 1e-6
N_DEV = 8
LANES = 128
VMEM_LIMIT = 48 * 1024 * 1024

RET_HEADS, RET_DH, RET_CHUNK = 4, 128, 128
S5_GROUPS, S5_GROUP, S5_STATE = 32, 16, 64
GDN_HEADS, GDN_DH, GDN_CHUNK, GDN_CONV = 8, 128, 64, 4
XA_HEADS, XA_DH = 4, 256
FFN_CONV = 3
SCAN_ROWS = 256

ADAM_LR, ADAM_B1, ADAM_B2, ADAM_EPS, ADAM_WD, ADAM_STEP = 0.001, 0.9, 0.999, 1e-08, 0.01, 10


def _cp(*sem):
    return pltpu.CompilerParams(dimension_semantics=sem if sem else None, vmem_limit_bytes=VMEM_LIMIT)


def _tile(n, cap):
    if n <= cap:
        return n
    best = None
    for t in range(LANES, cap + 1, LANES):
        if n % t == 0:
            best = t
    assert best is not None, n
    return best


def _dot(a, b, ca=1, cb=0, precision=None):
    return lax.dot_general(a, b, (((ca,), (cb,)), ((), ())), precision=precision, preferred_element_type=F32)


def _mxu(a, b, ca=1, cb=0):
    return _dot(a.astype(BF16), b.astype(BF16), ca, cb)


def _sigmoid(x):
    return 0.5 * jnp.tanh(0.5 * x) + 0.5


def _shift_down(x, k):
    r = pltpu.roll(x, k, 0)
    row = lax.broadcasted_iota(jnp.int32, (8,) + x.shape[1:], 0)
    return jnp.concatenate([jnp.where(row >= k, r[:8], 0.0), r[8:]], axis=0)


def _shift_up(x, k):
    n = x.shape[0]
    r = pltpu.roll(x, n - k, 0)
    row = lax.broadcasted_iota(jnp.int32, (8,) + x.shape[1:], 0)
    return jnp.concatenate([r[:n - 8], jnp.where(row < 8 - k, r[n - 8:], 0.0)], axis=0)


def _mesh_pos():
    return lax.axis_index("x"), lax.axis_index("y"), lax.axis_index("c")


def _slot(px, py, pc):
    return 4 * px + 2 * py + pc


def _all_peers(x, y, c):
    flips = [(fx, fy, fc) for fx in (0, 1) for fy in (0, 1) for fc in (0, 1)][1:]
    return [(1 - x if fx else x, 1 - y if fy else y, 1 - c if fc else c) for fx, fy, fc in flips]


_HBM = pl.BlockSpec(memory_space=pltpu.HBM)
_SEM = pl.BlockSpec(memory_space=pltpu.SEMAPHORE)
N_PEERS = N_DEV - 1


def _push_copies(srcs, lands, send_sems, recv_sems, start):
    x, y, c = _mesh_pos()
    me = _slot(x, y, c)
    out = []
    for k, to in enumerate(_all_peers(x, y, c)):
        for a in range(len(lands)):
            src = srcs[a].at[_slot(*to)] if a < len(srcs) else lands[a].at[me]
            dst = lands[a].at[me if start else _slot(*to)]
            out.append(pltpu.make_async_remote_copy(
                src_ref=src, dst_ref=dst, send_sem=send_sems.at[a * N_PEERS + k], recv_sem=recv_sems.at[a * N_PEERS + k],
                device_id=to, device_id_type=pl.DeviceIdType.MESH))
    return out


def _into_slot(x, dtype, me, name):
    r, c = x.shape
    cap = max(16, 512 * 1024 // c)
    tr = max(t for t in range(16, min(r, cap) + 1, 16) if r % t == 0) if r % 16 == 0 else r

    def body(me_ref, x_ref, o_ref):
        o_ref[...] = x_ref[...].astype(dtype)

    return pl.pallas_call(
        body, name=name, out_shape=jax.ShapeDtypeStruct((N_DEV, r, c), dtype),
        grid_spec=pltpu.PrefetchScalarGridSpec(
            num_scalar_prefetch=1, grid=(r // tr,),
            in_specs=[pl.BlockSpec((tr, c), lambda i, me_ref: (i, 0))],
            out_specs=pl.BlockSpec((None, tr, c), lambda i, me_ref: (me_ref[0], i, 0))),
        compiler_params=_cp("parallel"),
    )(me.reshape(1).astype(jnp.int32), x)


def _push_start(scatter, gather_lands, name):
    ns, n = len(scatter), len(scatter) + len(gather_lands)
    lands = [lax.empty(a.shape, a.dtype) for a in scatter] + list(gather_lands)

    def body(*refs):
        srcs, zones = refs[:ns], refs[ns:ns + n]
        for cp in _push_copies(srcs, zones, refs[ns + n], refs[ns + n + 1], True):
            cp.start()
        refs[-1][...] = jnp.zeros((8, LANES), F32)

    hbm_in = [pltpu.with_memory_space_constraint(a, pltpu.HBM) for a in list(scatter) + lands]
    res = pl.pallas_call(
        body, name=name,
        out_shape=(pltpu.SemaphoreType.DMA((n * N_PEERS,)), pltpu.SemaphoreType.DMA((n * N_PEERS,)))
        + tuple(pltpu.HBM(a.shape, a.dtype) for a in list(scatter) + lands)
        + (jax.ShapeDtypeStruct((8, LANES), F32),),
        in_specs=[_HBM] * (ns + n),
        out_specs=(_SEM, _SEM) + (_HBM,) * (ns + n) + (pl.BlockSpec(memory_space=pltpu.VMEM),),
        input_output_aliases={i: 2 + i for i in range(ns + n)},
        compiler_params=pltpu.CompilerParams(has_side_effects=pltpu.SideEffectType.DATAFLOW_SIDE_EFFECTING),
    )(*hbm_in)
    return (res[0], res[1], res[2:2 + ns], res[2 + ns:2 + ns + n]), res[-1]


def _push_wait(handle, after, name):
    send_sems, recv_sems, srcs, lands = handle
    ns, n = len(srcs), len(lands)

    def body(*refs):
        for cp in _push_copies(refs[:ns], refs[ns:ns + n], refs[ns + n], refs[ns + n + 1], False):
            cp.wait_send()
            cp.wait_recv()

    res = pl.pallas_call(
        body, name=name,
        out_shape=tuple(pltpu.HBM(a.shape, a.dtype) for a in list(srcs) + list(lands)),
        in_specs=[_HBM] * (ns + n) + [_SEM, _SEM, pl.BlockSpec(memory_space=pl.ANY)],
        out_specs=(_HBM,) * (ns + n),
        input_output_aliases={i: i for i in range(ns + n)},
        compiler_params=pltpu.CompilerParams(has_side_effects=pltpu.SideEffectType.DATAFLOW_SIDE_EFFECTING),
    )(*srcs, *lands, send_sems, recv_sems, after)
    return res[ns:]


def _mm(a, b, *, ta=False, tb=False, out_dtype=F32, res=None, pin=None, norm_gain=None, name="mm"):
    m, k = (a.shape[1], a.shape[0]) if ta else a.shape
    n = b.shape[0] if tb else b.shape[1]
    assert k == (b.shape[1] if tb else b.shape[0]), (a.shape, b.shape, ta, tb)
    tm, tn, tk = _tile(m, 1408), _tile(n, 1536), _tile(k, 1408)
    nk = k // tk
    has_res = res is not None
    has_norm = norm_gain is not None
    assert not has_norm or tn == n
    n_in = 2 + has_res + (pin is not None) + has_norm

    def body(*refs):
        a_ref, b_ref = refs[:2]
        r_ref = refs[2] if has_res else None
        o_ref = refs[n_in]
        part = _mxu(a_ref[...], b_ref[...], 0 if ta else 1, 1 if tb else 0)

        def finish(r):
            if has_res:
                r = r + r_ref[...].astype(F32)
            o_ref[...] = r.astype(out_dtype)
            if has_norm:
                scale = lax.rsqrt(jnp.mean(r * r, axis=-1, keepdims=True) + EPS)
                refs[n_in + 1][...] = (r * scale * refs[n_in - 1][...]).astype(BF16)

        if nk == 1:
            finish(part)
            return
        acc = refs[-1]
        kk = pl.program_id(2)

        @pl.when(kk == 0)
        def _():
            acc[...] = part

        @pl.when(kk > 0)
        def _():
            acc[...] += part

        @pl.when(kk == nk - 1)
        def _():
            finish(acc[...])

    a_spec = pl.BlockSpec((tk, tm), lambda i, j, kk: (kk, i)) if ta else pl.BlockSpec((tm, tk), lambda i, j, kk: (i, kk))
    b_spec = pl.BlockSpec((tn, tk), lambda i, j, kk: (j, kk)) if tb else pl.BlockSpec((tk, tn), lambda i, j, kk: (kk, j))
    o_spec = pl.BlockSpec((tm, tn), lambda i, j, kk: (i, j))
    in_specs = [a_spec, b_spec] + ([o_spec] if has_res else [])
    args = (a, b) + ((res,) if has_res else ())
    if pin is not None:
        in_specs.append(pl.BlockSpec(pin.shape, lambda i, j, kk: (0, 0)))
        args += (pin,)
    if has_norm:
        in_specs.append(pl.BlockSpec((1, n), lambda i, j, kk: (0, 0)))
        args += (norm_gain.reshape(1, n),)
    out = jax.ShapeDtypeStruct((m, n), out_dtype)
    return pl.pallas_call(
        body, name=name, grid=(m // tm, n // tn, nk), in_specs=in_specs,
        out_specs=[o_spec, o_spec] if has_norm else o_spec,
        out_shape=[out, jax.ShapeDtypeStruct((m, n), BF16)] if has_norm else out,
        scratch_shapes=[pltpu.VMEM((tm, tn), F32)] if nk > 1 else [],
        compiler_params=_cp("parallel", "parallel", "arbitrary"),
    )(*args)


def _mm_norm_bwd(dy, w, x, g, dres, name, pin=None):
    dys = list(dy) if isinstance(dy, (list, tuple)) else [dy]
    nq = len(dys)
    s, kq = dys[0].shape
    d = w.shape[0]
    tm, tk = min(1024 if nq == 1 else 512, s), _tile(kq, 1408)
    per = kq // tk
    nk = nq * per
    n_in = nq + 4 + (pin is not None)

    def body(*refs):
        w_ref, x_ref, g_ref, dres_ref = refs[nq:nq + 4]
        dx_ref, dg_ref = refs[n_in], refs[n_in + 1]
        i, kk = pl.program_id(0), pl.program_id(1)

        @pl.when((i == 0) & (kk == 0))
        def _():
            dg_ref[...] = jnp.zeros_like(dg_ref)

        def finish(dh):
            xv = x_ref[...]
            r = lax.rsqrt(jnp.mean(xv * xv, axis=-1, keepdims=True) + EPS)
            xn = xv * r
            dg_ref[...] += jnp.sum(dh * xn, axis=0, keepdims=True)
            dhg = dh * g_ref[...]
            dx_ref[...] = dres_ref[...] + r * (dhg - xn * jnp.mean(dhg * xn, axis=-1, keepdims=True))

        if nk == 1:
            finish(_mxu(refs[0][...], w_ref[...], 1, 1))
            return
        acc = refs[-1]
        for q in range(nq):
            @pl.when((kk >= q * per) & (kk < (q + 1) * per))
            def _(q=q):
                part = _mxu(refs[q][...], w_ref[...], 1, 1)

                @pl.when(kk == 0)
                def _():
                    acc[...] = part

                @pl.when(kk > 0)
                def _():
                    acc[...] += part

        @pl.when(kk == nk - 1)
        def _():
            finish(acc[...])

    row = pl.BlockSpec((tm, d), lambda i, kk: (i, 0))
    vec = pl.BlockSpec((1, d), lambda i, kk: (0, 0))
    in_specs = [pl.BlockSpec((tm, tk), lambda i, kk, q=q: (i, jnp.clip(kk - q * per, 0, per - 1))) for q in range(nq)]
    in_specs += [pl.BlockSpec((d, tk), lambda i, kk: (0, kk)), row, vec, row]
    args = (*dys, w, x, g.reshape(1, d), dres)
    if pin is not None:
        in_specs.append(pl.BlockSpec(pin.shape, lambda i, kk: (0, 0)))
        args += (pin,)
    return pl.pallas_call(
        body, name=name, grid=(s // tm, nk), in_specs=in_specs, out_specs=[row, vec],
        out_shape=[jax.ShapeDtypeStruct((s, d), F32), jax.ShapeDtypeStruct((1, d), F32)],
        scratch_shapes=[pltpu.VMEM((tm, d), F32)] if nk > 1 else [],
        compiler_params=_cp("arbitrary", "arbitrary"),
    )(*args)


def _norm_fwd(x, g, name):
    s, d = x.shape
    tr = min(512, s)

    def body(x_ref, g_ref, o_ref):
        xv = x_ref[...]
        r = lax.rsqrt(jnp.mean(xv * xv, axis=-1, keepdims=True) + EPS)
        o_ref[...] = (xv * r * g_ref[...]).astype(BF16)

    row = pl.BlockSpec((tr, d), lambda i: (i, 0))
    return pl.pallas_call(
        body, name=name, grid=(s // tr,), in_specs=[row, pl.BlockSpec((1, d), lambda i: (0, 0))],
        out_specs=row, out_shape=jax.ShapeDtypeStruct((s, d), BF16), compiler_params=_cp("parallel"),
    )(x, g.reshape(1, d))


def _norm_bwd(x, g, dh, dres, name):
    s, d = x.shape
    tr = min(512, s)

    def body(x_ref, g_ref, dh_ref, dres_ref, dx_ref, dg_ref):
        @pl.when(pl.program_id(0) == 0)
        def _():
            dg_ref[...] = jnp.zeros_like(dg_ref)

        xv = x_ref[...]
        r = lax.rsqrt(jnp.mean(xv * xv, axis=-1, keepdims=True) + EPS)
        xn = xv * r
        dhv = dh_ref[...].astype(F32)
        dg_ref[...] += jnp.sum(dhv * xn, axis=0, keepdims=True)
        dhg = dhv * g_ref[...]
        dx_ref[...] = dres_ref[...] + r * (dhg - xn * jnp.mean(dhg * xn, axis=-1, keepdims=True))

    row = pl.BlockSpec((tr, d), lambda i: (i, 0))
    vec = pl.BlockSpec((1, d), lambda i: (0, 0))
    return pl.pallas_call(
        body, name=name, grid=(s // tr,), in_specs=[row, vec, row, row], out_specs=[row, vec],
        out_shape=[jax.ShapeDtypeStruct((s, d), F32), jax.ShapeDtypeStruct((1, d), F32)],
        compiler_params=_cp("arbitrary"),
    )(x, g.reshape(1, d), dh, dres)


def _loss_head(x, g, tgt, name):
    s, d = x.shape
    tr = min(512, s)

    def body(x_ref, g_ref, t_ref, l_ref, dx_ref, dg_ref):
        @pl.when(pl.program_id(0) == 0)
        def _():
            dg_ref[...] = jnp.zeros_like(dg_ref)
            l_ref[...] = jnp.zeros_like(l_ref)

        xv = x_ref[...]
        r = lax.rsqrt(jnp.mean(xv * xv, axis=-1, keepdims=True) + EPS)
        xn = xv * r
        err = xn * g_ref[...] - t_ref[...]
        part = 0.5 * jnp.sum(jnp.mean(err * err, axis=-1, keepdims=True), axis=0, keepdims=True)
        l_ref[...] += jnp.broadcast_to(part, l_ref.shape)
        dy = err * (1.0 / d)
        dg_ref[...] += jnp.sum(dy * xn, axis=0, keepdims=True)
        dyg = dy * g_ref[...]
        dx_ref[...] = r * (dyg - xn * jnp.mean(dyg * xn, axis=-1, keepdims=True))

    row = pl.BlockSpec((tr, d), lambda i: (i, 0))
    vec = pl.BlockSpec((1, d), lambda i: (0, 0))
    return pl.pallas_call(
        body, name=name, grid=(s // tr,), in_specs=[row, vec, row],
        out_specs=[pl.BlockSpec((1, LANES), lambda i: (0, 0)), row, vec],
        out_shape=[jax.ShapeDtypeStruct((1, LANES), F32), jax.ShapeDtypeStruct((s, d), F32),
                   jax.ShapeDtypeStruct((1, d), F32)],
        compiler_params=_cp("arbitrary"),
    )(x, g.reshape(1, d), tgt)


def _sum_slots(landed_slot, own):
    me = _slot(*_mesh_pos())
    mine = own.astype(F32)
    g = jnp.where(me == 0, mine, landed_slot(0).astype(F32))
    for i in range(1, N_DEV):
        g = g + jnp.where(me == i, mine, landed_slot(i).astype(F32))
    return g


def _adam_update(g, w, m, v):
    mm = ADAM_B1 * m + (1.0 - ADAM_B1) * g
    vv = ADAM_B2 * v + (1.0 - ADAM_B2) * (g * g)
    m_hat = mm / (1.0 - ADAM_B1 ** ADAM_STEP)
    v_hat = vv / (1.0 - ADAM_B2 ** ADAM_STEP)
    return g, -ADAM_LR * (m_hat / (jnp.sqrt(v_hat) + ADAM_EPS) + ADAM_WD * w), mm, vv


def _adamw_rows(landed, own, ws, ms, vs, name):
    k = len(ws)
    sizes = [w.shape[1] for w in ws]

    def body(*refs):
        p_ref, o_ref = refs[:2]
        w_refs, m_refs, v_refs = refs[2:2 + k], refs[2 + k:2 + 2 * k], refs[2 + 2 * k:2 + 3 * k]
        outs = refs[2 + 3 * k:]
        for i, n in enumerate(sizes):
            g = _sum_slots(lambda s: p_ref[s, i:i + 1, :n], o_ref[i:i + 1, :n])
            res = _adam_update(g, w_refs[i][...], m_refs[i][...], v_refs[i][...])
            for j in range(4):
                outs[j * k + i][...] = res[j]

    return pl.pallas_call(
        body, name=name, out_shape=[jax.ShapeDtypeStruct((1, n), F32) for _ in range(4) for n in sizes],
    )(landed, own, *ws, *ms, *vs)


def _adamw(landed, own, w, m, v, name):
    r, c = w.shape
    cap = max(8, 256 * 1024 // c)
    tr = max(t for t in range(8, min(r, cap) + 1, 8) if r % t == 0) if r % 8 == 0 else r
    gathered = own is None

    def body(*refs):
        p_ref = refs[0]
        w_ref, m_ref, v_ref, g_ref, d_ref, nm_ref, nv_ref = refs[1 if gathered else 2:]
        if gathered:
            g = p_ref[0].astype(F32)
            for i in range(1, N_DEV):
                g = g + p_ref[i].astype(F32)
        else:
            g = _sum_slots(lambda i: p_ref[i], refs[1][...])
        g_ref[...], d_ref[...], nm_ref[...], nv_ref[...] = _adam_update(g, w_ref[...], m_ref[...], v_ref[...])

    blk = pl.BlockSpec((tr, c), lambda i: (i, 0))
    n_blk = 3 if gathered else 4
    return pl.pallas_call(
        body, name=name, grid=(r // tr,),
        in_specs=[pl.BlockSpec((N_DEV, tr, c), lambda i: (0, i, 0))] + [blk] * n_blk,
        out_specs=[blk] * 4, out_shape=[jax.ShapeDtypeStruct((r, c), F32)] * 4,
        compiler_params=_cp("parallel"),
    )(*((landed,) if gathered else (landed, own)), w, m, v)


def _conv_taps(x, kw):
    return [_shift_down(x, kw - 1 - j) for j in range(kw - 1)] + [x]


def _conv_fwd(taps, w_ref):
    acc = w_ref[0:1, :] * taps[0]
    for j in range(1, len(taps)):
        acc = acc + w_ref[j:j + 1, :] * taps[j]
    return acc


def _conv_bwd(taps, dy, w_ref, dw_ref):
    kw = len(taps)
    dx = w_ref[kw - 1:kw, :] * dy
    for j in range(kw):
        dw_ref[j:j + 1, :] = jnp.sum(dy * taps[j], axis=0, keepdims=True)
        if j < kw - 1:
            dx = dx + w_ref[j:j + 1, :] * _shift_up(dy, kw - 1 - j)
    return dx


def _ffn_act_fwd(pre, cw, name):
    s, f2 = pre.shape
    nt = f2 // 2 // LANES

    rows = min(512, s)

    def body(pu_ref, pg_ref, wu_ref, wg_ref, o_ref):
        def chunk(i, carry):
            r0 = pl.multiple_of(i * rows, rows)
            p0 = pl.multiple_of(jnp.maximum(r0 - 16, 0), 16)

            def conv(ref, w_ref):
                prev = jnp.where(i == 0, 0.0, ref[pl.ds(p0, 16), :].astype(F32))
                ext = jnp.concatenate([prev, ref[pl.ds(r0, rows), :].astype(F32)], axis=0)
                acc = w_ref[FFN_CONV - 1:FFN_CONV, :] * ext
                for j in range(FFN_CONV - 1):
                    acc = acc + w_ref[j:j + 1, :] * pltpu.roll(ext, FFN_CONV - 1 - j, 0)
                return acc[16:]

            up, gate = conv(pu_ref, wu_ref), conv(pg_ref, wg_ref)
            o_ref[pl.ds(r0, rows), :] = (gate * _sigmoid(gate) * up).astype(BF16)
            return carry

        lax.fori_loop(0, s // rows, chunk, 0)

    def col(rows, off):
        return pl.BlockSpec((rows, LANES), lambda j: (0, j + off))

    return pl.pallas_call(
        body, name=name, grid=(nt,),
        in_specs=[col(s, 0), col(s, nt), col(FFN_CONV, 0), col(FFN_CONV, nt)], out_specs=col(s, 0),
        out_shape=jax.ShapeDtypeStruct((s, f2 // 2), BF16), compiler_params=_cp("parallel"),
    )(pre, pre, cw, cw)


def _ffn_act_bwd(pre, cw, dact, name):
    s, f2 = pre.shape
    f = f2 // 2
    nt = f // LANES

    def body(pu_ref, pg_ref, wu_ref, wg_ref, da_ref, dpu_ref, dpg_ref, dwu_ref, dwg_ref):
        pu, pg = pu_ref[...].astype(F32), pg_ref[...].astype(F32)
        tu, tg = _conv_taps(pu, FFN_CONV), _conv_taps(pg, FFN_CONV)
        up = _conv_fwd(tu, wu_ref)
        gate = _conv_fwd(tg, wg_ref)
        sg = _sigmoid(gate)
        da = da_ref[...].astype(F32)
        dup = da * gate * sg
        dgate = da * up * (sg * (1.0 + gate * (1.0 - sg)))
        dpu_ref[...] = _conv_bwd(tu, dup, wu_ref, dwu_ref).astype(BF16)
        dpg_ref[...] = _conv_bwd(tg, dgate, wg_ref, dwg_ref).astype(BF16)

    def col(rows, off):
        return pl.BlockSpec((rows, LANES), lambda j: (0, j + off))

    return pl.pallas_call(
        body, name=name, grid=(nt,),
        in_specs=[col(s, 0), col(s, nt), col(FFN_CONV, 0), col(FFN_CONV, nt), col(s, 0)],
        out_specs=[col(s, 0), col(s, 0), col(FFN_CONV, 0), col(FFN_CONV, 0)],
        out_shape=[jax.ShapeDtypeStruct((s, f), BF16), jax.ShapeDtypeStruct((s, f), BF16),
                   jax.ShapeDtypeStruct((FFN_CONV, f), F32), jax.ShapeDtypeStruct((FFN_CONV, f), F32)],
        compiler_params=_cp("parallel"),
    )(pre, pre, cw, cw, dact)


def _xa_probs(qh, kh):
    sc = _mxu(qh, kh, 1, 1) * (XA_DH ** -0.5)
    e = jnp.exp(sc - jnp.max(sc, axis=-1, keepdims=True))
    return e / jnp.sum(e, axis=-1, keepdims=True)


def _xattn_fwd(q, kv, name):
    s, d = q.shape
    m = kv.shape[0]
    tr = min(512, s)

    def body(q_ref, kv_ref, o_ref):
        for h in range(XA_HEADS):
            lo, hi = h * XA_DH, (h + 1) * XA_DH
            p = _xa_probs(q_ref[:, lo:hi], kv_ref[:, lo:hi])
            o_ref[:, lo:hi] = _mxu(p, kv_ref[:, d + lo:d + hi]).astype(BF16)

    row = pl.BlockSpec((tr, d), lambda i: (i, 0))
    return pl.pallas_call(
        body, name=name, grid=(s // tr,), in_specs=[row, pl.BlockSpec((m, 2 * d), lambda i: (0, 0))],
        out_specs=row, out_shape=jax.ShapeDtypeStruct((s, d), BF16), compiler_params=_cp("parallel"),
    )(q, kv)


def _xattn_bwd(q, kv, do, name):
    s, d = q.shape
    m = kv.shape[0]
    tr = min(512, s)

    def body(q_ref, kv_ref, do_ref, dq_ref, dkv_ref):
        @pl.when(pl.program_id(0) == 0)
        def _():
            dkv_ref[...] = jnp.zeros_like(dkv_ref)

        for h in range(XA_HEADS):
            lo, hi = h * XA_DH, (h + 1) * XA_DH
            qh, kh, vh = q_ref[:, lo:hi], kv_ref[:, lo:hi], kv_ref[:, d + lo:d + hi]
            doh = do_ref[:, lo:hi]
            p = _xa_probs(qh, kh)
            dp = _mxu(doh, vh, 1, 1)
            ds = p * (dp - jnp.sum(p * dp, axis=-1, keepdims=True)) * (XA_DH ** -0.5)
            dq_ref[:, lo:hi] = _mxu(ds, kh).astype(BF16)
            dkv_ref[:, lo:hi] += _mxu(ds, qh, 0, 0)
            dkv_ref[:, d + lo:d + hi] += _mxu(p, doh, 0, 0)

    row = pl.BlockSpec((tr, d), lambda i: (i, 0))
    full = pl.BlockSpec((m, 2 * d), lambda i: (0, 0))
    return pl.pallas_call(
        body, name=name, grid=(s // tr,), in_specs=[row, full, row], out_specs=[row, full],
        out_shape=[jax.ShapeDtypeStruct((s, d), BF16), jax.ShapeDtypeStruct((m, 2 * d), F32)],
        compiler_params=_cp("arbitrary"),
    )(q, kv, do)


def _ret_tables():
    c = RET_CHUNK
    lg = np.log1p(-np.exp2(-5.0 - np.arange(RET_HEADS, dtype=np.float32))).astype(np.float32)
    idx = np.arange(c, dtype=np.float32)
    diff = idx[:, None] - idx[None, :]
    intra = np.where(diff >= 0, np.exp(lg[:, None, None] * np.where(diff >= 0, diff, 0.0)), 0.0)
    rk = np.broadcast_to(np.exp(lg[:, None] * (c - 1 - idx))[:, :, None], (RET_HEADS, c, LANES))
    rq = np.broadcast_to(np.exp(lg[:, None] * (idx + 1))[:, :, None], (RET_HEADS, c, LANES))
    return jnp.asarray(np.stack([intra, rk, rq], axis=1).astype(np.float32))


def _rope_tables(s):
    half = RET_DH // 2
    inv = jnp.exp(-math.log(10000.0) * jnp.arange(half, dtype=F32) / half)
    ang = jnp.arange(s, dtype=F32)[:, None] * inv[None, :]
    cos, sin = jnp.cos(ang), jnp.sin(ang)
    return jnp.concatenate([cos, cos], axis=1), jnp.concatenate([-sin, sin], axis=1)


def _ret_specs(n_of):
    c, w = RET_CHUNK, RET_HEADS * RET_DH

    def part(off):
        return pl.BlockSpec((c, w), lambda n: (n_of(n), off))

    pos = pl.BlockSpec((c, RET_DH), lambda n: (n_of(n), 0))
    gain = pl.BlockSpec((1, w), lambda n: (0, 0))
    tab = pl.BlockSpec((RET_HEADS, 3, c, LANES), lambda n: (0, 0, 0, 0))
    st = pl.BlockSpec((RET_HEADS, None, RET_DH, RET_DH), lambda n: (0, n_of(n), 0, 0))
    return part, pos, gain, tab, st


def _rheads(x):
    return jnp.stack([x[:, h * RET_DH:(h + 1) * RET_DH] for h in range(RET_HEADS)], axis=0)


def _runheads(x):
    return jnp.concatenate([x[h] for h in range(RET_HEADS)], axis=1)


def _rope(x, cos, sin):
    return x * cos + pltpu.roll(x, RET_DH // 2, 2) * sin


def _ret_chunk(q_ref, k_ref, v_ref, cos_ref, sin_ref, tab_ref, prev):
    cos, sin = cos_ref[...], sin_ref[...]
    q = _rope(_rheads(q_ref[...]), cos, sin)
    k = _rope(_rheads(k_ref[...]), cos, sin) * (RET_DH ** -0.5)
    v = _rheads(v_ref[...])
    scores = _bmxu(q, k, 2, 2) * tab_ref[:, 0]
    qdec = q * tab_ref[:, 2]
    kdec = k * tab_ref[:, 1]
    o = _bmxu(scores, v) + _bmxu(qdec, prev)
    return q, k, v, scores, qdec, kdec, o


def _ret_fwd(proj, cos, sin, gain, name):
    s = proj.shape[0]
    c = RET_CHUNK
    nc = s // c
    part, pos, gvec, tab, st = _ret_specs(lambda n: n)

    def body(q_ref, k_ref, v_ref, g_ref, cos_ref, sin_ref, rn_ref, tab_ref, o_ref, st_ref, state):
        @pl.when(pl.program_id(0) == 0)
        def _():
            state[...] = jnp.zeros_like(state)

        prev = state[...]
        st_ref[...] = prev
        _, _, v, _, _, kdec, o = _ret_chunk(q_ref, k_ref, v_ref, cos_ref, sin_ref, tab_ref, prev)
        state[...] = prev * tab_ref[:, 2, c - 1:c, :] + _bmxu(kdec, v, 1, 1)
        r = lax.rsqrt(jnp.mean(o * o, axis=-1, keepdims=True) + EPS)
        gate = g_ref[...]
        o_ref[...] = (_runheads(o * r) * rn_ref[...] * (gate * _sigmoid(gate))).astype(BF16)

    return pl.pallas_call(
        body, name=name, grid=(nc,),
        in_specs=[part(0), part(1), part(2), part(3), pos, pos, gvec, tab],
        out_specs=[part(0), st],
        out_shape=[jax.ShapeDtypeStruct((s, 2 * RET_HEADS * RET_DH), BF16),
                   jax.ShapeDtypeStruct((RET_HEADS, nc, RET_DH, RET_DH), F32)],
        scratch_shapes=[pltpu.VMEM((RET_HEADS, RET_DH, RET_DH), F32)],
        compiler_params=_cp("arbitrary"),
    )(proj, proj, proj, proj, cos, sin, gain.reshape(1, -1), _ret_tables())


def _ret_bwd(proj, cos, sin, gain, states, dmerged, name):
    s = proj.shape[0]
    c = RET_CHUNK
    nc = s // c
    width = RET_HEADS * RET_DH
    part, pos, gvec, tab, st = _ret_specs(lambda n: nc - 1 - n)

    def body(q_ref, k_ref, v_ref, g_ref, cos_ref, sin_ref, rn_ref, tab_ref, st_ref, do_ref,
             dp_ref, drn_ref, carry):
        @pl.when(pl.program_id(0) == 0)
        def _():
            carry[...] = jnp.zeros_like(carry)
            drn_ref[...] = jnp.zeros_like(drn_ref)

        prev = st_ref[...]
        q, k, v, scores, qdec, kdec, o = _ret_chunk(q_ref, k_ref, v_ref, cos_ref, sin_ref, tab_ref, prev)
        r = lax.rsqrt(jnp.mean(o * o, axis=-1, keepdims=True) + EPS)
        on = o * r
        on2 = _runheads(on)
        gate = g_ref[...]
        sg = _sigmoid(gate)
        sil = gate * sg
        dout = do_ref[...]
        rn = rn_ref[...]
        dp_ref[:, 3 * width:] = (dout * on2 * rn * (sg * (1.0 + gate * (1.0 - sg)))).astype(BF16)
        drn_ref[...] += jnp.sum(dout * on2 * sil, axis=0, keepdims=True)
        don = _rheads(dout * rn * sil)
        do = r * (don - on * jnp.mean(don * on, axis=-1, keepdims=True))
        dc = carry[...]
        dsc = _bmxu(do, v, 2, 2) * tab_ref[:, 0]
        dq = _bmxu(dsc, k) + _bmxu(do, prev, 2, 2) * tab_ref[:, 2]
        dk = _bmxu(dsc, q, 1, 1) + _bmxu(v, dc, 2, 2) * tab_ref[:, 1]
        dv = _bmxu(scores, do, 1, 1) + _bmxu(kdec, dc)
        carry[...] = _bmxu(qdec, do, 1, 1) + dc * tab_ref[:, 2, c - 1:c, :]
        cos, sin = cos_ref[...], sin_ref[...]
        dk = dk * (RET_DH ** -0.5)
        dp_ref[:, :width] = _runheads(dq * cos + pltpu.roll(dq * sin, RET_DH // 2, 2)).astype(BF16)
        dp_ref[:, width:2 * width] = _runheads(dk * cos + pltpu.roll(dk * sin, RET_DH // 2, 2)).astype(BF16)
        dp_ref[:, 2 * width:3 * width] = _runheads(dv).astype(BF16)

    return pl.pallas_call(
        body, name=name, grid=(nc,),
        in_specs=[part(0), part(1), part(2), part(3), pos, pos, gvec, tab, st, part(0)],
        out_specs=[pl.BlockSpec((c, 4 * width), lambda n: (nc - 1 - n, 0)), gvec],
        out_shape=[jax.ShapeDtypeStruct(proj.shape, BF16), jax.ShapeDtypeStruct((1, width), F32)],
        scratch_shapes=[pltpu.VMEM((RET_HEADS, RET_DH, RET_DH), F32)],
        compiler_params=_cp("arbitrary"),
    )(proj, proj, proj, proj, cos, sin, gain.reshape(1, -1), _ret_tables(), states, dmerged)


S5_TILE = 512


def _cmul_add(xr, xi, ar, ai, yr, yi):
    return xr + ar * yr - ai * yi, xi + ar * yi + ai * yr


def _s5_pow_tables(a_il, name):
    r = SCAN_ROWS
    t = S5_TILE
    w2 = a_il.shape[1]

    def body(a_ref, up_ref, dn_ref):
        for j in range(w2 // (2 * t)):
            re, im = pl.ds(2 * t * j, t), pl.ds(2 * t * j + t, t)
            up_ref[0:1, re] = a_ref[:, re]
            up_ref[0:1, im] = a_ref[:, im]
            dn_ref[r - 1:r, re] = a_ref[:, re]
            dn_ref[r - 1:r, im] = -a_ref[:, im]
            n = 1
            while n < r:
                lr, li = up_ref[n - 1:n, re], up_ref[n - 1:n, im]
                xr, xi = up_ref[0:n, re], up_ref[0:n, im]
                up_ref[n:2 * n, re] = xr * lr - xi * li
                up_ref[n:2 * n, im] = xr * li + xi * lr
                yr, yi = dn_ref[r - n:r, re], dn_ref[r - n:r, im]
                dn_ref[r - 2 * n:r - n, re] = yr * lr + yi * li
                dn_ref[r - 2 * n:r - n, im] = yi * lr - yr * li
                n *= 2

    return pl.pallas_call(
        body, name=name, out_shape=[jax.ShapeDtypeStruct((r, w2), F32)] * 2, compiler_params=_cp(),
    )(a_il)


_GELU_C = math.sqrt(2.0 / math.pi)
_GELU_A = 0.044715


def _gelu(y):
    return 0.5 * y * (1.0 + jnp.tanh(_GELU_C * (y + _GELU_A * y * y * y)))


def _gelu_grad(y):
    th = jnp.tanh(_GELU_C * (y + _GELU_A * y * y * y))
    return 0.5 * (1.0 + th) + 0.5 * y * (1.0 - th * th) * _GELU_C * (1.0 + 3.0 * _GELU_A * y * y)


def _rows_shift(x, k, axis, up):
    n = x.shape[axis]
    idx = lax.broadcasted_iota(jnp.int32, x.shape, axis)
    if up:
        return jnp.where(idx < n - k, pltpu.roll(x, n - k, axis), 0.0)
    return jnp.where(idx >= k, pltpu.roll(x, k, axis), 0.0)


def _scan_block(xr, xi, pr, pi, cr, ci, rev):
    r, w = xr.shape
    nt = r // 8
    x3r, x3i = xr.reshape(nt, 8, w), xi.reshape(nt, 8, w)
    p3r, p3i = pr.reshape(nt, 8, w), pi.reshape(nt, 8, w)

    def power(rows):
        t = r - rows if rev else rows - 1
        return pr[t:t + 1, :], pi[t:t + 1, :]

    tile_row = lax.broadcasted_iota(jnp.int32, (8, w), 0)
    for sh in (1, 2, 4):
        ar, ai = power(sh)
        keep = tile_row < 8 - sh if rev else tile_row >= sh
        mr, mi = jnp.where(keep, ar, 0.0)[None], jnp.where(keep, ai, 0.0)[None]
        turn = 8 - sh if rev else sh
        x3r, x3i = _cmul_add(x3r, x3i, mr, mi, pltpu.roll(x3r, turn, 1), pltpu.roll(x3i, turn, 1))
    edge = 0 if rev else 7
    lr, li = x3r[:, edge, :], x3i[:, edge, :]
    sh = 1
    while sh < nt:
        ar, ai = power(8 * sh)
        lr, li = _cmul_add(lr, li, ar, ai, _rows_shift(lr, sh, 0, rev), _rows_shift(li, sh, 0, rev))
        sh *= 2
    tr_, ti_ = p3r[:, edge, :], p3i[:, edge, :]
    first = lax.broadcasted_iota(jnp.int32, (nt, w), 0) == (nt - 1 if rev else 0)
    wr = jnp.where(first, 1.0, _rows_shift(tr_, 1, 0, rev))
    wi = jnp.where(first, 0.0, _rows_shift(ti_, 1, 0, rev))
    er, ei = _cmul_add(_rows_shift(lr, 1, 0, rev), _rows_shift(li, 1, 0, rev), wr, wi, cr, ci)
    a8r, a8i = (p3r[nt - 1], p3i[nt - 1]) if rev else (p3r[0], p3i[0])
    x3r, x3i = _cmul_add(x3r, x3i, a8r[None], a8i[None], er[:, None, :], ei[:, None, :])
    outr, outi = x3r.reshape(r, w), x3i.reshape(r, w)
    last = 0 if rev else r - 1
    return outr, outi, outr[last:last + 1, :], outi[last:last + 1, :]


def _s5_tile_specs(n_of, r):
    t = S5_TILE
    ucol = 4 * RET_HEADS * RET_DH // LANES
    u = pl.BlockSpec((r, LANES), lambda j, i: (n_of(i), ucol + j))
    col = pl.BlockSpec((r, LANES), lambda j, i: (n_of(i), j))
    state = pl.BlockSpec((r, 2 * t), lambda j, i: (n_of(i), j))
    table = pl.BlockSpec((r, 2 * t), lambda j, i: (0, j))
    bbt = pl.BlockSpec((None, LANES, 2 * t), lambda j, i: (j, 0, 0))
    cct = pl.BlockSpec((None, 2 * t, LANES), lambda j, i: (j, 0, 0))
    vec = pl.BlockSpec((1, LANES), lambda j, i: (0, j))
    return u, col, state, table, bbt, cct, vec


def _s5_fwd(proj, bbt, cct, apow, dvec, name):
    s = proj.shape[0]
    r, t = SCAN_ROWS, S5_TILE
    w = S5_GROUPS * S5_GROUP
    u_s, col, state, table, bb_s, cc_s, vec = _s5_tile_specs(lambda i: i, r)

    def body(u_ref, bb_ref, cc_ref, p_ref, d_ref, st_ref, y_ref, g_ref, cr, ci):
        @pl.when(pl.program_id(1) == 0)
        def _():
            cr[...] = jnp.zeros_like(cr)
            ci[...] = jnp.zeros_like(ci)

        u = u_ref[...]
        bu = _mxu(u, bb_ref[...])
        xr, xi, cr[...], ci[...] = _scan_block(bu[:, :t], bu[:, t:], p_ref[:, :t], p_ref[:, t:], cr[...], ci[...], False)
        st_ref[:, :t] = xr
        st_ref[:, t:] = xi
        y = _mxu(xr, cc_ref[:t, :]) + _mxu(xi, cc_ref[t:, :]) + d_ref[...] * u
        y_ref[...] = y
        g_ref[...] = _gelu(y).astype(BF16)

    return pl.pallas_call(
        body, name=name, grid=(2 * S5_GROUPS * S5_STATE // (2 * t), s // r),
        in_specs=[u_s, bb_s, cc_s, table, vec], out_specs=[state, col, col],
        out_shape=[jax.ShapeDtypeStruct((s, 2 * S5_GROUPS * S5_STATE), F32), jax.ShapeDtypeStruct((s, w), F32),
                   jax.ShapeDtypeStruct((s, w), BF16)],
        scratch_shapes=[pltpu.VMEM((1, t), F32), pltpu.VMEM((1, t), F32)],
        compiler_params=_cp("parallel", "arbitrary"),
    )(proj, bbt, cct, apow, dvec)


def _s5_bwd(dg1, dg2, y, proj, st, bbt, cct, apow_rev, dvec, dproj, name):
    s = proj.shape[0]
    r, t = SCAN_ROWS, S5_TILE
    nb = s // r
    w = S5_GROUPS * S5_GROUP
    u_s, col, state, table, bb_s, cc_s, vec = _s5_tile_specs(lambda i: nb - 1 - i, r)
    halo = pl.BlockSpec((8, 2 * t), lambda j, i: (jnp.maximum((nb - 1 - i) * (r // 8) - 1, 0), j))
    acc = pl.BlockSpec((1, 2 * t), lambda j, i: (0, j))

    def body(a_ref, b_ref, y_ref, u_ref, s_ref, sp_ref, bb_ref, cc_ref, p_ref, d_ref, _,
             du_ref, da_ref, dbb_ref, dcc_ref, dd_ref, cr, ci):
        i = pl.program_id(1)

        @pl.when(i == 0)
        def _():
            cr[...] = jnp.zeros_like(cr)
            ci[...] = jnp.zeros_like(ci)
            da_ref[...] = jnp.zeros_like(da_ref)
            dbb_ref[...] = jnp.zeros_like(dbb_ref)
            dcc_ref[...] = jnp.zeros_like(dcc_ref)
            dd_ref[...] = jnp.zeros_like(dd_ref)

        u = u_ref[...]
        dy = (a_ref[...] + b_ref[...]) * _gelu_grad(y_ref[...])
        dd_ref[...] += jnp.sum(dy * u, axis=0, keepdims=True)
        sr, si = s_ref[:, :t], s_ref[:, t:]
        dcc_ref[:t, :] += _mxu(sr, dy, 0, 0)
        dcc_ref[t:, :] += _mxu(si, dy, 0, 0)
        xr, xi, cr[...], ci[...] = _scan_block(_mxu(dy, cc_ref[:t, :], 1, 1), _mxu(dy, cc_ref[t:, :], 1, 1),
                                               p_ref[:, :t], p_ref[:, t:], cr[...], ci[...], True)
        du_ref[...] = (dy * d_ref[...] + _mxu(xr, bb_ref[:, :t], 1, 1) + _mxu(xi, bb_ref[:, t:], 1, 1)).astype(BF16)
        dbb_ref[:, :t] += _mxu(u, xr, 0, 0)
        dbb_ref[:, t:] += _mxu(u, xi, 0, 0)
        first = i == nb - 1
        row = lax.broadcasted_iota(jnp.int32, (r, t), 0)
        pr = jnp.where(row == 0, jnp.where(first, 0.0, sp_ref[7:8, :t]), pltpu.roll(sr, 1, 0))
        pi = jnp.where(row == 0, jnp.where(first, 0.0, sp_ref[7:8, t:]), pltpu.roll(si, 1, 0))
        da_ref[:, :t] += jnp.sum(xr * pr + xi * pi, axis=0, keepdims=True)
        da_ref[:, t:] += jnp.sum(xi * pr - xr * pi, axis=0, keepdims=True)

    return pl.pallas_call(
        body, name=name, grid=(2 * S5_GROUPS * S5_STATE // (2 * t), nb),
        in_specs=[col, col, col, u_s, state, halo, bb_s, cc_s, table, vec, pl.BlockSpec(memory_space=pl.ANY)],
        out_specs=[u_s, acc, bb_s, cc_s, vec],
        out_shape=[jax.ShapeDtypeStruct(dproj.shape, dproj.dtype), jax.ShapeDtypeStruct((1, 2 * S5_GROUPS * S5_STATE), F32),
                   jax.ShapeDtypeStruct(bbt.shape, F32), jax.ShapeDtypeStruct(cct.shape, F32),
                   jax.ShapeDtypeStruct((1, w), F32)],
        scratch_shapes=[pltpu.VMEM((1, t), F32), pltpu.VMEM((1, t), F32)],
        input_output_aliases={10: 0}, compiler_params=_cp("parallel", "arbitrary"),
    )(dg1, dg2, y, proj, st, st, bbt, cct, apow_rev, dvec, dproj)


def _s5_tile_b(b_re, b_im):
    nt = S5_GROUPS * S5_STATE // S5_TILE
    gpt = S5_GROUPS // nt
    eye = jnp.eye(gpt, dtype=F32)

    def tile(b):
        t5 = jnp.einsum("jghp,gk->jghkp", b.reshape(nt, gpt, S5_GROUP, S5_STATE), eye)
        return t5.reshape(nt, gpt * S5_GROUP, S5_TILE)

    return jnp.concatenate([tile(b_re), tile(b_im)], axis=2)


def _s5_untile_b(d):
    nt = S5_GROUPS * S5_STATE // S5_TILE
    gpt = S5_GROUPS // nt
    eye = jnp.eye(gpt, dtype=F32)

    def untile(x):
        x5 = x.reshape(nt, gpt, S5_GROUP, gpt, S5_STATE)
        return jnp.einsum("jghkp,gk->jghp", x5, eye).reshape(S5_GROUPS, S5_GROUP, S5_STATE)

    return untile(d[:, :, :S5_TILE]), untile(d[:, :, S5_TILE:])


def _s5_tile_c(c_re, c_im):
    nt = S5_GROUPS * S5_STATE // S5_TILE
    gpt = S5_GROUPS // nt
    eye = jnp.eye(gpt, dtype=F32)

    def tile(c):
        t5 = jnp.einsum("jgph,gk->jkpgh", c.reshape(nt, gpt, S5_STATE, S5_GROUP), eye)
        return t5.reshape(nt, S5_TILE, gpt * S5_GROUP)

    return jnp.concatenate([tile(c_re), -tile(c_im)], axis=1)


def _s5_untile_c(d):
    nt = S5_GROUPS * S5_STATE // S5_TILE
    gpt = S5_GROUPS // nt
    eye = jnp.eye(gpt, dtype=F32)

    def untile(x):
        x5 = x.reshape(nt, gpt, S5_STATE, gpt, S5_GROUP)
        return jnp.einsum("jkpgh,gk->jgph", x5, eye).reshape(S5_GROUPS, S5_STATE, S5_GROUP)

    return untile(d[:, :S5_TILE, :]), -untile(d[:, S5_TILE:, :])


def _row_call(body, name, s, ins, outs, acc=False):
    tr = min(512, s)

    def spec(width, cb, rows):
        if rows == 1:
            return pl.BlockSpec((1, width), lambda i: (0, cb))
        return pl.BlockSpec((tr, width), lambda i: (i, cb))

    in_specs = [spec(w, cb, a.shape[0]) for a, w, cb in ins]
    out_specs = [spec(w, cb, sd.shape[0]) for sd, w, cb in outs]
    return pl.pallas_call(
        body, name=name, grid=(s // tr,), in_specs=in_specs, out_specs=out_specs,
        out_shape=[sd for sd, _, _ in outs],
        compiler_params=_cp("arbitrary" if acc else "parallel"),
    )(*[a for a, _, _ in ins])


def _sds(shape, dtype):
    return jax.ShapeDtypeStruct(shape, dtype)


def _s5_glu_fwd(y, z, b, merged, name):
    s, w = y.shape
    tr = min(512, s)

    def body(y_ref, z_ref, b_ref, _, o_ref):
        o_ref[...] = (_gelu(y_ref[...]) * _sigmoid(z_ref[...] + b_ref[...])).astype(BF16)

    row = pl.BlockSpec((tr, w), lambda i: (i, 0))
    return pl.pallas_call(
        body, name=name, grid=(s // tr,),
        in_specs=[row, row, pl.BlockSpec((1, w), lambda i: (0, 0)), pl.BlockSpec(memory_space=pl.ANY)],
        out_specs=pl.BlockSpec((tr, w), lambda i: (i, 1)),
        out_shape=jax.ShapeDtypeStruct(merged.shape, merged.dtype),
        input_output_aliases={3: 0}, compiler_params=_cp("parallel"),
    )(y, z, b, merged)


def _s5_glu_bwd(dmerged, y, z, b, name):
    s, w = y.shape

    def body(do_ref, y_ref, z_ref, b_ref, dz_ref, dg_ref, db_ref):
        @pl.when(pl.program_id(0) == 0)
        def _():
            db_ref[...] = jnp.zeros_like(db_ref)

        g = _gelu(y_ref[...])
        sg = _sigmoid(z_ref[...] + b_ref[...])
        dout = do_ref[...]
        dz = dout * g * sg * (1.0 - sg)
        dz_ref[...] = dz.astype(BF16)
        dg_ref[...] = dout * sg
        db_ref[...] += jnp.sum(dz, axis=0, keepdims=True)

    return _row_call(body, name, s, [(dmerged, w, 1), (y, w, 0), (z, w, 0), (b, w, 0)],
                     [(_sds((s, w), BF16), w, 0), (_sds((s, w), F32), w, 0), (_sds((1, w), F32), w, 0)], acc=True)


def _gdn_conv_fwd(projx, cw, name):
    s = projx.shape[0]
    nh = GDN_HEADS

    def body(x_ref, w_ref, o_ref):
        j = pl.program_id(0)
        cv = _conv_fwd(_conv_taps(x_ref[...], GDN_CONV), w_ref)
        y = cv * _sigmoid(cv)
        nrm = y * lax.rsqrt(jnp.sum(y * y, axis=-1, keepdims=True) + EPS)
        o_ref[...] = jnp.where(j < nh, nrm * (GDN_DH ** -0.5), jnp.where(j < 2 * nh, nrm, y))

    return pl.pallas_call(
        body, name=name, grid=(3 * nh,),
        in_specs=[pl.BlockSpec((s, GDN_DH), lambda j: (0, j)), pl.BlockSpec((GDN_CONV, GDN_DH), lambda j: (0, j))],
        out_specs=pl.BlockSpec((s, GDN_DH), lambda j: (0, j)),
        out_shape=jax.ShapeDtypeStruct((s, 3 * nh * GDN_DH), F32), compiler_params=_cp("parallel"),
    )(projx, cw)


def _gdn_conv_bwd(projx, cw, dqkv, dprojx, name):
    s = projx.shape[0]
    nh = GDN_HEADS

    def body(x_ref, w_ref, d_ref, _, dx_ref, dw_ref):
        j = pl.program_id(0)
        x = x_ref[...]
        taps = _conv_taps(x, GDN_CONV)
        cv = _conv_fwd(taps, w_ref)
        sg = _sigmoid(cv)
        y = cv * sg
        rinv = lax.rsqrt(jnp.sum(y * y, axis=-1, keepdims=True) + EPS)
        nrm = y * rinv
        dn = d_ref[...]
        dns = jnp.where(j < nh, dn * (GDN_DH ** -0.5), dn)
        dyn = rinv * (dns - nrm * jnp.sum(dns * nrm, axis=-1, keepdims=True))
        dy = jnp.where(j < 2 * nh, dyn, dn)
        dc = dy * (sg * (1.0 + cv * (1.0 - sg)))
        dx_ref[...] = _conv_bwd(taps, dc, w_ref, dw_ref).astype(BF16)

    col = pl.BlockSpec((s, GDN_DH), lambda j: (0, j))
    wcol = pl.BlockSpec((GDN_CONV, GDN_DH), lambda j: (0, j))
    return pl.pallas_call(
        body, name=name, grid=(3 * nh,), in_specs=[col, wcol, col, pl.BlockSpec(memory_space=pl.ANY)],
        out_specs=[col, wcol],
        out_shape=[jax.ShapeDtypeStruct(dprojx.shape, dprojx.dtype), jax.ShapeDtypeStruct((GDN_CONV, 3 * nh * GDN_DH), F32)],
        input_output_aliases={3: 0}, compiler_params=_cp("parallel"),
    )(projx, cw, dqkv, dprojx)


def _softplus(x):
    return jnp.maximum(x, 0.0) + jnp.log1p(jnp.exp(-jnp.abs(x)))


def _gdn_gates_fwd(projx, alog, dtb, name):
    s = projx.shape[0]
    w = GDN_HEADS * GDN_DH
    tr = min(512, s)

    def body(t_ref, al_ref, dt_ref, bo_ref, go_ref):
        t = t_ref[...]
        for h in range(GDN_HEADS):
            lo, hi = h * GDN_DH, (h + 1) * GDN_DH
            b = jnp.broadcast_to(t[:, h:h + 1], (tr, GDN_DH))
            a = jnp.broadcast_to(t[:, GDN_HEADS + h:GDN_HEADS + h + 1], (tr, GDN_DH))
            bo_ref[:, lo:hi] = _sigmoid(b)
            go_ref[:, lo:hi] = -jnp.exp(al_ref[:, lo:hi]) * _softplus(a + dt_ref[:, lo:hi])

    row = pl.BlockSpec((tr, w), lambda i: (i, 0))
    vec = pl.BlockSpec((1, w), lambda i: (0, 0))
    return pl.pallas_call(
        body, name=name, grid=(s // tr,),
        in_specs=[pl.BlockSpec((tr, LANES), lambda i: (i, 4 * w // LANES)), vec, vec], out_specs=[row, row],
        out_shape=[jax.ShapeDtypeStruct((s, w), F32)] * 2, compiler_params=_cp("parallel"),
    )(projx, alog, dtb)


def _gdn_gates_bwd(projx, alog, dtb, dbeta, dg, dprojx, name):
    s = projx.shape[0]
    w = GDN_HEADS * GDN_DH
    tr = min(512, s)
    gate_blk = 4 * w // LANES

    def body(t_ref, al_ref, dt_ref, dbe_ref, dg_ref, _, o_ref, dal_ref, ddt_ref):
        @pl.when(pl.program_id(0) == 0)
        def _():
            dal_ref[...] = jnp.zeros_like(dal_ref)
            ddt_ref[...] = jnp.zeros_like(ddt_ref)

        t = t_ref[...]
        lane = lax.broadcasted_iota(jnp.int32, (tr, LANES), 1)
        lane1 = lax.broadcasted_iota(jnp.int32, (1, LANES), 1)
        out = jnp.zeros((tr, LANES), F32)
        dal = jnp.zeros((1, LANES), F32)
        ddt = jnp.zeros((1, LANES), F32)
        for h in range(GDN_HEADS):
            lo, hi = h * GDN_DH, (h + 1) * GDN_DH
            beta = _sigmoid(t[:, h:h + 1])
            pb = jnp.sum(dbe_ref[:, lo:hi], axis=-1, keepdims=True)
            db = pb * beta * (1.0 - beta)
            xa = t[:, GDN_HEADS + h:GDN_HEADS + h + 1] + dt_ref[:, lo:lo + 1]
            ea = -jnp.exp(al_ref[:, lo:lo + 1])
            pg = jnp.sum(dg_ref[:, lo:hi], axis=-1, keepdims=True)
            da = pg * ea * _sigmoid(xa)
            out = jnp.where(lane == h, db, jnp.where(lane == GDN_HEADS + h, da, out))
            dal = jnp.where(lane1 == h, jnp.sum(pg * ea * _softplus(xa), axis=0, keepdims=True), dal)
            ddt = jnp.where(lane1 == h, jnp.sum(da, axis=0, keepdims=True), ddt)
        o_ref[...] = out.astype(BF16)
        dal_ref[...] += dal
        ddt_ref[...] += ddt

    row = pl.BlockSpec((tr, w), lambda i: (i, 0))
    vec = pl.BlockSpec((1, w), lambda i: (0, 0))
    small = pl.BlockSpec((1, LANES), lambda i: (0, 0))
    gates = pl.BlockSpec((tr, LANES), lambda i: (i, gate_blk))
    return pl.pallas_call(
        body, name=name, grid=(s // tr,),
        in_specs=[gates, vec, vec, row, row, pl.BlockSpec(memory_space=pl.ANY)],
        out_specs=[gates, small, small],
        out_shape=[jax.ShapeDtypeStruct(dprojx.shape, dprojx.dtype), jax.ShapeDtypeStruct((1, LANES), F32),
                   jax.ShapeDtypeStruct((1, LANES), F32)],
        input_output_aliases={5: 0}, compiler_params=_cp("arbitrary"),
    )(projx, alog, dtb, dbeta, dg, dprojx)


def _gdn_tri():
    c = GDN_CHUNK
    i = lax.broadcasted_iota(jnp.int32, (c, c), 0)
    j = lax.broadcasted_iota(jnp.int32, (c, c), 1)
    return ((i >= j).astype(F32), (i <= j).astype(F32), i >= j, i > j, (i == j).astype(F32))


def _bdot(a, b, ca=2, cb=1, precision=None):
    return lax.dot_general(a, b, (((ca,), (cb,)), ((0,), (0,))), precision=precision, preferred_element_type=F32)


def _bmxu(a, b, ca=2, cb=1):
    return _bdot(a.astype(BF16), b.astype(BF16), ca, cb)


def _split(x):
    hi = x.astype(BF16)
    return hi, (x - hi.astype(F32)).astype(BF16)


def _bdot3(a, b, ca=2, cb=1):
    ah, al = _split(a)
    bh, bl = _split(b)
    return _bdot(ah, bh, ca, cb) + (_bdot(ah, bl, ca, cb) + _bdot(al, bh, ca, cb))


def _tri_dot(tri, x):
    t = tri.astype(BF16)
    hi = x.astype(BF16)
    r1 = x - hi.astype(F32)
    mid = r1.astype(BF16)
    lo = (r1 - mid.astype(F32)).astype(BF16)
    return _dot(t, hi) + (_dot(t, mid) + _dot(t, lo))


def _heads(x):
    return jnp.stack([x[:, h * GDN_DH:(h + 1) * GDN_DH] for h in range(GDN_HEADS)], axis=0)


def _unheads(x):
    return jnp.concatenate([x[h] for h in range(GDN_HEADS)], axis=1)


def _gdn_chunk(q, k, v, bb, g2d, tri, t=None):
    low, up, incl, strict, eye = tri
    c = GDN_CHUNK
    gc = _heads(_tri_dot(low, g2d))
    gci = gc[:, :, :c]
    gdiff = gci - jnp.swapaxes(gci, 1, 2)
    decay = jnp.where(incl, jnp.exp(jnp.where(incl, gdiff, 0.0)), 0.0)
    kb, vb = k * bb, v * bb
    kbk = _bmxu(kb, k, 2, 2)
    if t is None:
        x = -jnp.where(strict, kbk * decay, 0.0)
        t = eye + x
        p = x
        for _ in range(c.bit_length() - 2):
            p = _bdot3(p, p)
            t = t + _bdot3(t, p)
    eg = jnp.exp(gc)
    kbg = kb * eg
    gcl = gc[:, c - 1:c, :]
    ek = jnp.exp(gcl - gc)
    qkraw = _bmxu(q, k, 2, 2)
    return dict(decay=decay, kb=kb, vb=vb, kbk=kbk, t=t, eg=eg, kbg=kbg, ek=ek, gl=jnp.exp(gcl),
                w=_bmxu(t, kbg), u=_bmxu(t, vb), qkraw=qkraw, qk=jnp.where(incl, qkraw * decay, 0.0),
                qd=q * eg, kd=k * ek)


def _gdn_specs(n_of):
    c, w = GDN_CHUNK, GDN_HEADS * GDN_DH

    def blk(cb, width=w):
        return pl.BlockSpec((c, width), lambda n: (n_of(n), cb))

    st = pl.BlockSpec((None, GDN_HEADS, GDN_DH, GDN_DH), lambda n: (n_of(n), 0, 0, 0))
    vec = pl.BlockSpec((1, GDN_DH), lambda n: (0, 0))
    return blk, st, vec


def _gdn_load(qkv_ref, b_ref, g_ref, tri, t=None):
    w = GDN_HEADS * GDN_DH
    q, k, v = _heads(qkv_ref[:, :w]), _heads(qkv_ref[:, w:2 * w]), _heads(qkv_ref[:, 2 * w:])
    bb = _heads(b_ref[...])
    return q, k, v, bb, _gdn_chunk(q, k, v, bb, g_ref[...], tri, t)


def _gdn_fwd(qkv, beta, g, projx, onorm, name):
    s = qkv.shape[0]
    c = GDN_CHUNK
    nc = s // c
    w = GDN_HEADS * GDN_DH
    blk, st, vec = _gdn_specs(lambda n: n)
    inv = pl.BlockSpec((None, GDN_HEADS, c, c), lambda n: (n, 0, 0, 0))

    def body(qkv_ref, b_ref, g_ref, z_ref, on_ref, o_ref, st_ref, t_ref, state):
        @pl.when(pl.program_id(0) == 0)
        def _():
            state[...] = jnp.zeros_like(state)

        _, _, _, _, ch = _gdn_load(qkv_ref, b_ref, g_ref, _gdn_tri())
        t_ref[...] = ch["t"]
        sp = state[...]
        st_ref[...] = sp
        vn = ch["u"] - _bmxu(ch["w"], sp)
        o = _bmxu(ch["qd"], sp) + _bmxu(ch["qk"], vn)
        state[...] = sp * ch["gl"] + _bmxu(ch["kd"], vn, 1, 1)
        r = lax.rsqrt(jnp.mean(o * o, axis=-1, keepdims=True) + EPS)
        z = _heads(z_ref[...])
        o_ref[...] = _unheads(o * r * on_ref[...] * (z * _sigmoid(z))).astype(BF16)

    return pl.pallas_call(
        body, name=name, grid=(nc,),
        in_specs=[blk(0, 3 * w), blk(0), blk(0), blk(3), vec], out_specs=[blk(0), st, inv],
        out_shape=[jax.ShapeDtypeStruct((s, w), BF16), jax.ShapeDtypeStruct((nc, GDN_HEADS, GDN_DH, GDN_DH), F32),
                   jax.ShapeDtypeStruct((nc, GDN_HEADS, c, c), F32)],
        scratch_shapes=[pltpu.VMEM((GDN_HEADS, GDN_DH, GDN_DH), F32)],
        compiler_params=_cp("arbitrary"),
    )(qkv, beta, g, projx, onorm.reshape(1, -1))


def _gdn_bwd(qkv, beta, g, projx, onorm, states, tinv, dout, name):
    s = qkv.shape[0]
    c = GDN_CHUNK
    nc = s // c
    w = GDN_HEADS * GDN_DH
    blk, st, vec = _gdn_specs(lambda n: nc - 1 - n)
    inv = pl.BlockSpec((None, GDN_HEADS, c, c), lambda n: (nc - 1 - n, 0, 0, 0))

    def body(qkv_ref, b_ref, g_ref, z_ref, on_ref, st_ref, t_ref, do_ref,
             dqkv_ref, db_ref, dg_ref, dz_ref, don_ref, carry):
        @pl.when(pl.program_id(0) == 0)
        def _():
            carry[...] = jnp.zeros_like(carry)
            don_ref[...] = jnp.zeros_like(don_ref)

        tri = _gdn_tri()
        low, up, incl, strict, eye = tri
        q, k, v, bb, ch = _gdn_load(qkv_ref, b_ref, g_ref, tri, t_ref[...])
        sp = st_ref[...]
        vn = ch["u"] - _bmxu(ch["w"], sp)
        o = _bmxu(ch["qd"], sp) + _bmxu(ch["qk"], vn)
        r = lax.rsqrt(jnp.mean(o * o, axis=-1, keepdims=True) + EPS)
        orn = o * r
        z = _heads(z_ref[...])
        sg = _sigmoid(z)
        dout = _heads(do_ref[...])
        onw = on_ref[...]
        dz_ref[...] = _unheads(dout * orn * onw * (sg * (1.0 + z * (1.0 - sg)))).astype(BF16)
        don = dout * (z * sg)
        don_ref[...] += jnp.sum(jnp.sum(don * orn, axis=0), axis=0, keepdims=True)
        dor = don * onw
        do = r * (dor - orn * jnp.mean(dor * orn, axis=-1, keepdims=True))
        dsn = carry[...]
        dqd = _bmxu(do, sp, 2, 2)
        dqk = jnp.where(incl, _bmxu(do, vn, 2, 2), 0.0)
        dvn = _bmxu(ch["qk"], do, 1, 1) + _bmxu(ch["kd"], dsn)
        dkd = _bmxu(vn, dsn, 2, 2)
        dgl = jnp.sum(dsn * sp, axis=1, keepdims=True)
        dw = -_bmxu(dvn, sp, 2, 2)
        carry[...] = _bmxu(ch["qd"], do, 1, 1) + dsn * ch["gl"] - _bmxu(ch["w"], dvn, 1, 1)
        t = ch["t"]
        dvb = _bmxu(t, dvn, 1, 1)
        dkbg = _bmxu(t, dw, 1, 1)
        dt = _bmxu(dvn, ch["vb"], 2, 2) + _bmxu(dw, ch["kbg"], 2, 2)
        da = -_bdot3(_bdot3(t, dt, 1, 1), t, 2, 2)
        da = jnp.where(strict, da, 0.0)
        decay = ch["decay"]
        dkbk = da * decay
        dqkr = dqk * decay
        mdec = (da * ch["kbk"] + dqk * ch["qkraw"]) * decay
        dkb = _bmxu(dkbk, k) + dkbg * ch["eg"]
        dk = _bmxu(dkbk, ch["kb"], 1, 1) + _bmxu(dqkr, q, 1, 1) + dkd * ch["ek"] + dkb * bb
        dq = _bmxu(dqkr, k) + dqd * ch["eg"]
        tk = dkd * ch["kd"]
        dgcl = jnp.sum(tk, axis=1, keepdims=True) + dgl * ch["gl"]
        row = lax.broadcasted_iota(jnp.int32, (GDN_HEADS, c, GDN_DH), 1)
        zpad = jnp.zeros((GDN_HEADS, c, GDN_DH - c), F32)
        dgc = (jnp.concatenate([mdec, zpad], axis=2) - jnp.concatenate([jnp.swapaxes(mdec, 1, 2), zpad], axis=2)
               + dqd * ch["qd"] - tk + dkbg * ch["kbg"] + jnp.where(row == c - 1, dgcl, 0.0))
        dqkv_ref[:, :w] = _unheads(dq)
        dqkv_ref[:, w:2 * w] = _unheads(dk)
        dqkv_ref[:, 2 * w:] = _unheads(dvb * bb)
        db_ref[...] = _unheads(dkb * k + dvb * v)
        dg_ref[...] = _tri_dot(up, _unheads(dgc))

    return pl.pallas_call(
        body, name=name, grid=(nc,),
        in_specs=[blk(0, 3 * w), blk(0), blk(0), blk(3), vec, st, inv, blk(0)],
        out_specs=[blk(0, 3 * w), blk(0), blk(0), blk(3), vec],
        out_shape=[jax.ShapeDtypeStruct((s, 3 * w), F32), jax.ShapeDtypeStruct((s, w), F32),
                   jax.ShapeDtypeStruct((s, w), F32), jax.ShapeDtypeStruct(projx.shape, BF16),
                   jax.ShapeDtypeStruct((1, GDN_DH), F32)],
        scratch_shapes=[pltpu.VMEM((GDN_HEADS, GDN_DH, GDN_DH), F32)],
        compiler_params=_cp("arbitrary"),
    )(qkv, beta, g, projx, onorm.reshape(1, -1), states, tinv, dout)


_WEIGHTS = (
    "l0_mix_norm", "l0_w_in", "l0_ret_norm", "l0_s5_lambda_re", "l0_s5_lambda_im", "l0_s5_b_re", "l0_s5_b_im",
    "l0_s5_c_re", "l0_s5_c_im", "l0_s5_d", "l0_s5_log_dt", "l0_s5_w_glu", "l0_s5_b_glu", "l0_w_out",
    "l0_xa_norm", "l0_mem_norm", "l0_xa_wq", "l0_xa_wkv", "l0_xa_wo", "l0_ffn_norm", "l0_ffn_w_up",
    "l0_ffn_conv", "l0_ffn_w_down", "l1_mix_norm", "l1_w_in", "l1_conv", "l1_a_log", "l1_dt_bias", "l1_o_norm",
    "l1_w_out", "l1_xa_norm", "l1_mem_norm", "l1_xa_wq", "l1_xa_wkv", "l1_xa_wo", "l1_ffn_norm", "l1_ffn_w_up",
    "l1_ffn_conv", "l1_ffn_w_down", "final_norm")
_INPUTS = ("x", "mem") + _WEIGHTS + ("loss_target",) + tuple("m_" + n for n in _WEIGHTS) + tuple("v_" + n for n in _WEIGHTS)

_COL = ("l0_w_in", "l0_xa_wkv", "l0_ffn_w_up", "l0_ffn_conv", "l1_w_in", "l1_conv", "l1_xa_wkv", "l1_ffn_w_up",
        "l1_ffn_conv")
_ROW = ("l0_s5_w_glu", "l0_w_out", "l0_xa_wq", "l0_xa_wo", "l0_ffn_w_down", "l1_w_out", "l1_xa_wq", "l1_xa_wo",
        "l1_ffn_w_down")
_F32_WIRE = ("l0_ffn_conv", "l1_conv", "l1_ffn_conv")
_REP = tuple(n for n in _WEIGHTS if n not in _COL + _ROW)
_GATHER_GROUPS = (("l0_w_in", "l0_s5_w_glu", "l0_w_out"),
                  ("l0_xa_wq", "l0_xa_wkv", "l0_xa_wo", "l0_ffn_w_up", "l0_ffn_conv", "l0_ffn_w_down"),
                  ("l1_w_in", "l1_conv", "l1_w_out", "l1_xa_wq", "l1_xa_wkv", "l1_xa_wo"),
                  ("l1_ffn_w_up", "l1_ffn_conv", "l1_ffn_w_down"))


def _round_up(n, m):
    return (n + m - 1) // m * m


_REP_BIG = ("l0_s5_lambda_re", "l0_s5_lambda_im", "l0_s5_b_re", "l0_s5_b_im", "l0_s5_c_re", "l0_s5_c_im", "l0_s5_d")
_REP_LAST = "l0_mix_norm"
_REP_SMALL = tuple(n for n in _REP if n not in _REP_BIG + (_REP_LAST,))
PACK_WIDTH = 1024


def _pack_rows(ts):
    rows = [jnp.pad(t, ((0, 0), (0, PACK_WIDTH - t.shape[1]))) for t in ts]
    rows.append(jnp.zeros((_round_up(len(ts), 8) - len(ts), PACK_WIDTH), F32))
    return jnp.concatenate(rows, axis=0)


def _s5_interleave(re, im):
    lead = re.shape[:-1]
    nt = re.shape[-1] // S5_TILE
    both = jnp.stack([re.reshape(lead + (nt, S5_TILE)), im.reshape(lead + (nt, S5_TILE))], axis=-2)
    return both.reshape(lead + (2 * re.shape[-1],))


def _s5_split(x):
    lead = x.shape[:-1]
    y = x.reshape(lead + (x.shape[-1] // (2 * S5_TILE), 2, S5_TILE))
    return y[..., 0, :].reshape(lead + (-1,)), y[..., 1, :].reshape(lead + (-1,))


def _s5_discretise(lr, li, log_dt, b_re, b_im):
    dt = jnp.exp(log_dt)[:, None]
    mag = jnp.exp(lr * dt)
    a_re = mag * jnp.cos(li * dt)
    a_im = mag * jnp.sin(li * dt)
    den = lr * lr + li * li
    z_re = ((a_re - 1.0) * lr + a_im * li) / den
    z_im = (a_im * lr - (a_re - 1.0) * li) / den
    bb_re = z_re[:, None, :] * b_re - z_im[:, None, :] * b_im
    bb_im = z_re[:, None, :] * b_im + z_im[:, None, :] * b_re
    return a_re, a_im, bb_re, bb_im


def kernel(*args):
    p = dict(zip(_INPUTS, args, strict=True))
    x0, mem0, tgt = p["x"][0], p["mem"][0], p["loss_target"][0]
    s, d = x0.shape
    me = _slot(*_mesh_pos())
    grads = {}
    wire = {n: (F32 if n in _F32_WIRE else BF16) for n in _COL + _ROW}

    zones = {n: _into_slot(p[n], wire[n], me, "place_" + n) for names in _GATHER_GROUPS for n in names}
    gather, pin = [], jnp.zeros((), F32)
    for i, names in enumerate(_GATHER_GROUPS):
        handle, token = _push_start([], [zones[n] for n in names], f"gather{i}_start")
        gather.append(handle)
        pin = pin + token[0, 0]
    w = {}

    def gathered(i, after):
        for n, full in zip(_GATHER_GROUPS[i], _push_wait(gather[i], after, f"gather{i}_wait")):
            if n in _COL:
                full = full.transpose(1, 0, 2)
            w[n] = full.reshape(-1, full.shape[-1]) if n in _ROW else full.reshape(full.shape[0], -1)

    pending = []

    def exchange(names, gain, tag):
        slots = []
        for n in names:
            g = grads[n]
            if n in _COL:
                pieces = g if isinstance(g, tuple) else (g,)
                g = jnp.concatenate([t.reshape(t.shape[0], -1, p[n].shape[1]).transpose(1, 0, 2) for t in pieces], axis=0)
            else:
                g = g.reshape((N_DEV, -1) + g.shape[1:])
            slots.append(g.astype(wire[n]))
        handle, token = _push_start(slots, [], tag + "_start")
        pending.append((names, slots, handle, tag))
        return gain + token[0, 0]

    def xattn(pre, x_in, hx):
        q = _mm(hx, w[pre + "xa_wq"], out_dtype=BF16, name=pre + "xa_q")
        memn = _norm_fwd(mem0, p[pre + "mem_norm"], pre + "mem_norm_fwd")
        kv = _mm(memn, w[pre + "xa_wkv"], out_dtype=BF16, name=pre + "xa_kv")
        ao = _xattn_fwd(q, kv, pre + "xattn_fwd")
        x_out, hf = _mm(ao, w[pre + "xa_wo"], res=x_in, norm_gain=p[pre + "ffn_norm"], name=pre + "xa_o")
        return x_out, hf, (x_in, hx, q, memn, kv, ao)

    def xattn_bwd(pre, saved, dxo):
        x_in, hx, q, memn, kv, ao = saved
        dao = _mm(dxo, w[pre + "xa_wo"], tb=True, name=pre + "xa_o_dx")
        grads[pre + "xa_wo"] = _mm(ao, dxo, ta=True, out_dtype=BF16, name=pre + "xa_o_dw")
        dq, dkv = _xattn_bwd(q, kv, dao, pre + "xattn_bwd")
        grads[pre + "xa_wq"] = _mm(hx, dq, ta=True, out_dtype=BF16, name=pre + "xa_q_dw")
        grads[pre + "xa_wkv"] = _mm(memn, dkv, ta=True, out_dtype=BF16, name=pre + "xa_kv_dw")
        dmemn = _mm(dkv, w[pre + "xa_wkv"], tb=True, name=pre + "xa_kv_dx")
        gain = exchange((pre + "xa_wo", pre + "xa_wq", pre + "xa_wkv"), p[pre + "xa_norm"], pre + "xa_grads")
        dx_in, grads[pre + "xa_norm"] = _mm_norm_bwd(dq, w[pre + "xa_wq"], x_in, gain, dxo, pre + "xa_q_dx")
        _, grads[pre + "mem_norm"] = _norm_bwd(mem0, p[pre + "mem_norm"], dmemn, jnp.zeros_like(mem0), pre + "mem_norm_bwd")
        return dx_in

    def ffn(pre, x_in, hf, next_gain):
        up = _mm(hf, w[pre + "ffn_w_up"], out_dtype=BF16, name=pre + "ffn_up")
        act = _ffn_act_fwd(up, w[pre + "ffn_conv"], pre + "ffn_act_fwd")
        res = _mm(act, w[pre + "ffn_w_down"], res=x_in, norm_gain=next_gain, name=pre + "ffn_down")
        x_out, h_next = res if next_gain is not None else (res, None)
        return x_out, h_next, (x_in, hf, up, act)

    def ffn_bwd(pre, saved, dxo):
        x_in, hf, up, act = saved
        dact = _mm(dxo, w[pre + "ffn_w_down"], tb=True, out_dtype=BF16, name=pre + "ffn_down_dx")
        grads[pre + "ffn_w_down"] = _mm(act, dxo, ta=True, out_dtype=BF16, name=pre + "ffn_down_dw")
        dpu, dpg, dcu, dcg = _ffn_act_bwd(up, w[pre + "ffn_conv"], dact, pre + "ffn_act_bwd")
        grads[pre + "ffn_conv"] = jnp.concatenate([dcu, dcg], axis=1)
        grads[pre + "ffn_w_up"] = (_mm(hf, dpu, ta=True, out_dtype=BF16, name=pre + "ffn_up_dw_u"),
                                   _mm(hf, dpg, ta=True, out_dtype=BF16, name=pre + "ffn_up_dw_g"))
        gain = exchange((pre + "ffn_w_down", pre + "ffn_w_up", pre + "ffn_conv"), p[pre + "ffn_norm"], pre + "ffn_grads")
        dx_in, grads[pre + "ffn_norm"] = _mm_norm_bwd([dpu, dpg], w[pre + "ffn_w_up"], x_in, gain, dxo, pre + "ffn_up_dx")
        return dx_in

    cos, sin = _rope_tables(s)
    (a_re, a_im, bb_re, bb_im), disc_vjp = jax.vjp(
        _s5_discretise, p["l0_s5_lambda_re"], p["l0_s5_lambda_im"], p["l0_s5_log_dt"], p["l0_s5_b_re"], p["l0_s5_b_im"])
    apow, apow_rev = _s5_pow_tables(_s5_interleave(a_re.reshape(1, -1), a_im.reshape(1, -1)), "l0_s5_pow_tables")
    bbt = _s5_tile_b(bb_re, bb_im).astype(BF16)
    cct = _s5_tile_c(p["l0_s5_c_re"], p["l0_s5_c_im"]).astype(BF16)
    s5_d = p["l0_s5_d"].reshape(1, -1)
    b_glu = p["l0_s5_b_glu"].reshape(1, -1)

    h0 = _norm_fwd(x0, p["l0_mix_norm"] + pin, "l0_mix_norm_fwd")
    gathered(0, h0)
    proj = _mm(h0, w["l0_w_in"], name="l0_in")
    merged, ret_states = _ret_fwd(proj, cos, sin, p["l0_ret_norm"], "l0_ret_fwd")
    st, y, gy = _s5_fwd(proj, bbt, cct, apow, s5_d, "l0_s5_fwd")
    z = _mm(gy, w["l0_s5_w_glu"], name="l0_s5_glu_mm")
    merged = _s5_glu_fwd(y, z, b_glu, merged, "l0_s5_glu_fwd")
    x1, hx0 = _mm(merged, w["l0_w_out"], res=x0, norm_gain=p["l0_xa_norm"], name="l0_out")
    gathered(1, x1)
    x2, hf0, xa0 = xattn("l0_", x1, hx0)
    x3, h1, ff0 = ffn("l0_", x2, hf0, p["l1_mix_norm"])

    gathered(2, x3)
    w1 = w["l1_w_in"]
    wx = jnp.pad(w1, ((0, 0), (0, _round_up(w1.shape[1], LANES) - w1.shape[1])))
    alog_x = jnp.repeat(p["l1_a_log"], GDN_DH).reshape(1, -1)
    dtb_x = jnp.repeat(p["l1_dt_bias"], GDN_DH).reshape(1, -1)
    projx = _mm(h1, wx, name="l1_in")
    qkv = _gdn_conv_fwd(projx, w["l1_conv"], "l1_conv_fwd")
    beta, glog = _gdn_gates_fwd(projx, alog_x, dtb_x, "l1_gates_fwd")
    o_gdn, gdn_states, gdn_tinv = _gdn_fwd(qkv, beta, glog, projx, p["l1_o_norm"], "l1_gdn_fwd")
    x4, hx1 = _mm(o_gdn, w["l1_w_out"], res=x3, norm_gain=p["l1_xa_norm"], name="l1_out")
    x5, hf1, xa1 = xattn("l1_", x4, hx1)
    gathered(3, x5)
    x6, _, ff1 = ffn("l1_", x5, hf1, None)

    loss_part, dx6, grads["final_norm"] = _loss_head(x6, p["final_norm"], tgt, "loss_head")
    loss = lax.psum(loss_part[0, 0], ("x", "y", "c"))
    dx5 = ffn_bwd("l1_", ff1, dx6)
    dx4 = xattn_bwd("l1_", xa1, dx5)

    do_gdn = _mm(dx4, w["l1_w_out"], tb=True, name="l1_out_dx")
    grads["l1_w_out"] = _mm(o_gdn, dx4, ta=True, out_dtype=BF16, name="l1_out_dw")
    dqkv, dbeta, dglog, dprojx, grads["l1_o_norm"] = _gdn_bwd(
        qkv, beta, glog, projx, p["l1_o_norm"], gdn_states, gdn_tinv, do_gdn, "l1_gdn_bwd")
    dprojx, grads["l1_conv"] = _gdn_conv_bwd(projx, w["l1_conv"], dqkv, dprojx, "l1_conv_bwd")
    dprojx, dalog_x, ddtb_x = _gdn_gates_bwd(projx, alog_x, dtb_x, dbeta, dglog, dprojx, "l1_gates_bwd")
    grads["l1_w_in"] = _mm(h1, dprojx, ta=True, out_dtype=BF16, name="l1_in_dw")[:, :w1.shape[1]]
    grads["l1_a_log"] = dalog_x[0, :GDN_HEADS]
    grads["l1_dt_bias"] = ddtb_x[0, :GDN_HEADS]
    gain = exchange(("l1_w_out", "l1_w_in", "l1_conv"), p["l1_mix_norm"], "l1_mix_grads")
    dx3, grads["l1_mix_norm"] = _mm_norm_bwd(dprojx, wx, x3, gain, dx4, "l1_in_dx")

    dx2 = ffn_bwd("l0_", ff0, dx3)
    dx1 = xattn_bwd("l0_", xa0, dx2)

    dmerged = _mm(dx1, w["l0_w_out"], tb=True, name="l0_out_dx")
    grads["l0_w_out"] = _mm(merged, dx1, ta=True, out_dtype=BF16, name="l0_out_dw")
    dproj, grads["l0_ret_norm"] = _ret_bwd(proj, cos, sin, p["l0_ret_norm"], ret_states, dmerged, "l0_ret_bwd")
    dzg, dg1, grads["l0_s5_b_glu"] = _s5_glu_bwd(dmerged, y, z, b_glu, "l0_s5_glu_bwd")
    grads["l0_s5_w_glu"] = _mm(gy, dzg, ta=True, out_dtype=BF16, name="l0_s5_glu_dw")
    s5_d_after = exchange(("l0_w_out", "l0_s5_w_glu"), s5_d, "l0_out_grads")
    dg2 = _mm(dzg, w["l0_s5_w_glu"], tb=True, name="l0_s5_glu_dx")
    dproj, da_s5, dbbt, dcct, grads["l0_s5_d"] = _s5_bwd(dg1, dg2, y, proj, st, bbt, cct, apow_rev, s5_d_after, dproj, "l0_s5_bwd")
    dbb_re, dbb_im = _s5_untile_b(dbbt)
    grads["l0_s5_c_re"], grads["l0_s5_c_im"] = _s5_untile_c(dcct)
    da_re, da_im = (t.reshape(S5_GROUPS, S5_STATE) for t in _s5_split(da_s5[0]))
    (grads["l0_s5_lambda_re"], grads["l0_s5_lambda_im"], grads["l0_s5_log_dt"], grads["l0_s5_b_re"],
     grads["l0_s5_b_im"]) = disc_vjp((da_re, da_im, dbb_re, dbb_im))

    def as_2d(t):
        return t.reshape(-1, t.shape[-1])

    def as_row(t):
        return t.reshape(1, -1)

    small_own = _pack_rows([as_row(grads[n]) for n in _REP_SMALL])
    big_own = [as_2d(grads[n].reshape(p[n].shape)) for n in _REP_BIG]
    rep_zones = [_into_slot(small_own, F32, me, "place_rep0")]
    rep_zones += [_into_slot(t.reshape(-1, LANES), BF16, me, f"place_rep{i + 1}") for i, t in enumerate(big_own)]
    rep_handle, rep_token = _push_start([], rep_zones, "rep_grads_start")

    grads["l0_w_in"] = _mm(h0, dproj, ta=True, out_dtype=BF16, pin=rep_token, name="l0_in_dw")
    gain = exchange(("l0_w_in",), p["l0_mix_norm"], "l0_mix_grads")
    dx0, grads["l0_mix_norm"] = _mm_norm_bwd(dproj, w["l0_w_in"], x0, gain, dx1, "l0_in_dx")

    last_own = _pack_rows([as_row(grads[_REP_LAST])])
    last_handle, _ = _push_start([], [_into_slot(last_own, F32, me, "place_rep_last")], "rep_last_start")
    last_land, = _push_wait(last_handle, dx0, "rep_last_wait")
    rep_lands = _push_wait(rep_handle, last_land, "rep_grads_wait")
    rep_land = rep_lands[0]

    outs = {}
    kinds = ("grad_", "delta_", "new_m_", "new_v_")
    for names, slots, handle, tag in pending:
        for n, own_slots, land in zip(names, slots, _push_wait(handle, rep_land, tag + "_wait")):
            shape = p[n].shape
            own = lax.dynamic_index_in_dim(own_slots, me, 0, keepdims=False)
            res = _adamw(land, own, *(p[pre + n].reshape(own.shape) for pre in ("", "m_", "v_")), "adamw_" + n)
            for kind, t in zip(kinds, res):
                outs[kind + n] = t.reshape(shape)
    for n, own, land in zip(_REP_BIG, big_own, rep_lands[1:]):
        res = _adamw(land.reshape((N_DEV,) + own.shape), None, *(as_2d(p[pre + n]) for pre in ("", "m_", "v_")), "adamw_" + n)
        for kind, t in zip(kinds, res):
            outs[kind + n] = t.reshape(p[n].shape)
    for names, land, own, nm in ((_REP_SMALL, rep_land, small_own, "adamw_small"), ((_REP_LAST,), last_land, last_own, "adamw_last")):
        res = _adamw_rows(land, own, *([as_row(p[pre + n]) for n in names] for pre in ("", "m_", "v_")), nm)
        for j, kind in enumerate(kinds):
            for i, n in enumerate(names):
                outs[kind + n] = res[j * len(names) + i].reshape(p[n].shape)

    return (loss, dx0[None]) + tuple(outs[kind + n] for kind in kinds for n in _WEIGHTS)
```
